```python
import jax, jax.numpy as jnp
from jax import lax
import numpy as np

D_MODEL = 1024
BATCH = 8
SEQ = 4096
DEPTH = 1

HEAD_DIM = 64
RWKV_HEADS = 8
RWKV_WIDTH = RWKV_HEADS * HEAD_DIM
DECAY_LORA = 64
ICLR_LORA = 64
GATE_LORA = 128
ATTN_Q_HEADS = 8
ATTN_KV_HEADS = 2
ATTN_GROUPS = ATTN_Q_HEADS // ATTN_KV_HEADS
ATTN_Q_WIDTH = ATTN_Q_HEADS * HEAD_DIM
ATTN_KV_WIDTH = ATTN_KV_HEADS * HEAD_DIM
WINDOW = 128
BLOCK = 128
ROPE_THETA = 500000.0
ROPE_DIM = HEAD_DIM // 4
D_FF = -(-8 * D_MODEL // (3 * 256)) * 256
N_BRANCH = 2
RMS_EPS = 1e-6
GN_EPS = 64e-5
NEG_INF = -1e30
RWKV_SHIFT_WIDTH = 3 * RWKV_WIDTH + DECAY_LORA + ICLR_LORA + GATE_LORA
IN_WIDTH = RWKV_SHIFT_WIDTH + ATTN_Q_WIDTH + 2 * ATTN_KV_WIDTH + N_BRANCH * D_MODEL

kernel_name = "hybrid_rwkv7_swa_sink_adaln_block"


def rms_norm(x, gain, eps=RMS_EPS):
    x32 = x.astype(jnp.float32)
    inv = lax.rsqrt(jnp.mean(x32 * x32, axis=-1, keepdims=True) + eps)
    return (x32 * inv).astype(x.dtype) * gain


def token_shift(p):
    return jnp.pad(p, ((0, 0), (1, 0), (0, 0)))[:, :-1]


def partial_rope(x, positions):
    half = ROPE_DIM // 2
    inv_freq = ROPE_THETA ** (-jnp.arange(half, dtype=jnp.float32) / half)
    ang = positions.astype(jnp.float32)[..., None] * inv_freq
    cos = jnp.cos(ang)[:, :, None, :]
    sin = jnp.sin(ang)[:, :, None, :]
    xr = x[..., :ROPE_DIM].astype(jnp.float32)
    x1, x2 = xr[..., :half], xr[..., half:]
    rot = jnp.concatenate([x1 * cos - x2 * sin, x2 * cos + x1 * sin], axis=-1).astype(x.dtype)
    return jnp.concatenate([rot, x[..., ROPE_DIM:]], axis=-1)


def wkv7_scan(r, w, k, v, a, b):
    B_, S_, H, N = r.shape

    def step(state, inp):
        r_t, w_t, k_t, v_t, a_t, b_t = inp
        sa = jnp.einsum('bhvk,bhk->bhv', state, a_t)
        state = state * w_t[:, :, None, :] + sa[..., None] * b_t[:, :, None, :] + v_t[..., None] * k_t[:, :, None, :]
        return state, jnp.einsum('bhvk,bhk->bhv', state, r_t)

    xs = tuple(jnp.moveaxis(t.astype(jnp.float32), 1, 0) for t in (r, w, k, v, a, b))
    init = jnp.zeros((B_, H, N, N), jnp.float32)
    _, ys = lax.scan(step, init, xs)
    return jnp.moveaxis(ys, 0, 1)


def rwkv7_time_mix(cols, decay_w0, decay_up, iclr_a0, iclr_up, gate_up, k_k, k_a, r_k, lnx_gain, lnx_bias):
    B_, S_, _ = cols.shape
    W = RWKV_WIDTH
    f32 = jnp.float32
    r = cols[..., :W]
    k = cols[..., W:2 * W]
    v = cols[..., 2 * W:3 * W]
    o = 3 * W
    xw = cols[..., o:o + DECAY_LORA]
    xa = cols[..., o + DECAY_LORA:o + DECAY_LORA + ICLR_LORA]
    xg = cols[..., o + DECAY_LORA + ICLR_LORA:]
    w_log = -jax.nn.softplus(-(decay_w0 + jnp.tanh(xw) @ decay_up).astype(f32)) - 0.5
    decay = jnp.exp(-jnp.exp(w_log))
    a = jax.nn.sigmoid(iclr_a0 + xa @ iclr_up)
    g = jax.nn.sigmoid(xg) @ gate_up
    heads = lambda t: t.reshape(B_, S_, RWKV_HEADS, HEAD_DIM)
    kk = heads(k * k_k).astype(f32)
    kk = kk / jnp.maximum(jnp.sqrt(jnp.sum(kk * kk, axis=-1, keepdims=True)), 1e-12)
    k = k * (1 + (a - 1) * k_a)
    rh, kh, vh = heads(r), heads(k), heads(v)
    ah = heads(a).astype(f32)
    y = wkv7_scan(rh, heads(decay), kh, vh, -kk, kk * ah)
    mu = jnp.mean(y, axis=-1, keepdims=True)
    var = jnp.mean(jnp.square(y - mu), axis=-1, keepdims=True)
    yn = (y - mu) * lax.rsqrt(var + GN_EPS)
    yn = yn * lnx_gain.reshape(RWKV_HEADS, HEAD_DIM).astype(f32) + lnx_bias.reshape(RWKV_HEADS, HEAD_DIM).astype(f32)
    bonus = jnp.sum((rh * kh * r_k).astype(f32), axis=-1, keepdims=True) * vh.astype(f32)
    return (yn + bonus).reshape(B_, S_, W).astype(cols.dtype) * g


def sliding_window_sink_attention(q, k, v, positions, q_norm_gain, k_norm_gain, sinks):
    B_, S_, _ = q.shape
    nblk = S_ // BLOCK
    q = partial_rope(rms_norm(q.reshape(B_, S_, ATTN_Q_HEADS, HEAD_DIM), q_norm_gain), positions)
    k = partial_rope(rms_norm(k.reshape(B_, S_, ATTN_KV_HEADS, HEAD_DIM), k_norm_gain), positions)
    v = v.reshape(B_, S_, ATTN_KV_HEADS, HEAD_DIM)
    qb = q.reshape(B_, nblk, BLOCK, ATTN_KV_HEADS, ATTN_GROUPS, HEAD_DIM)

    def band(t):
        tb = t.reshape(B_, nblk, BLOCK, ATTN_KV_HEADS, HEAD_DIM)
        prev = jnp.pad(tb, ((0, 0), (1, 0), (0, 0), (0, 0), (0, 0)))[:, :-1]
        return jnp.concatenate([prev, tb], axis=2)

    kband, vband = band(k), band(v)
    scores = jnp.einsum('bnqhgd,bnkhd->bnhgqk', qb, kband).astype(jnp.float32) * (HEAD_DIM ** -0.5)
    q_idx = jnp.arange(BLOCK)[:, None]
    k_idx = jnp.arange(2 * BLOCK)[None, :]
    dist = q_idx + BLOCK - k_idx
    in_band = (dist >= 0) & (dist < WINDOW)
    blk = jnp.arange(nblk)[:, None, None]
    valid = in_band[None] & ((blk > 0) | (k_idx >= BLOCK)[None])
    scores = jnp.where(valid[None, :, None, None], scores, NEG_INF)
    sink = sinks.astype(jnp.float32).reshape(ATTN_KV_HEADS, ATTN_GROUPS)[None, None, :, :, None, None]
    m = jnp.maximum(jnp.max(scores, axis=-1, keepdims=True), sink)
    e = jnp.exp(scores - m)
    probs = e / (jnp.sum(e, axis=-1, keepdims=True) + jnp.exp(sink - m))
    out = jnp.einsum('bnhgqk,bnkhd->bnqhgd', probs.astype(v.dtype), vband)
    return out.reshape(B_, S_, ATTN_Q_WIDTH)


def _fwd_setup_inputs(seed: int = 0) -> dict:
    key = jax.random.key(seed)
    ks = jax.random.split(key, 32)
    f32 = jnp.float32
    nrm = lambda i, shape, s: jax.random.normal(ks[i], shape, f32) * s
    L = DEPTH
    offsets = jax.random.randint(ks[2], (BATCH, 1), 0, 2048, dtype=jnp.int32)
    positions = offsets + jnp.arange(SEQ, dtype=jnp.int32)[None, :]
    return {
        "x": nrm(0, (BATCH, SEQ, D_MODEL), 1.0),
        "c": nrm(1, (BATCH, D_MODEL), 1.0),
        "positions": positions,
        "ada_w": nrm(3, (L, D_MODEL, 6 * D_MODEL), 0.2 * D_MODEL ** -0.5),
        "ada_b": nrm(4, (L, 6 * D_MODEL), 0.01),
        "norm1_gain": 1.0 + nrm(5, (L, D_MODEL), 0.02),
        "norm2_gain": 1.0 + nrm(6, (L, D_MODEL), 0.02),
        "w_in": nrm(7, (L, D_MODEL, IN_WIDTH), D_MODEL ** -0.5),
        "tshift_mu": jax.random.uniform(ks[8], (L, RWKV_SHIFT_WIDTH), f32),
        "decay_w0": jax.random.uniform(ks[9], (L, RWKV_WIDTH), f32, -6.0, 1.0),
        "decay_up": nrm(10, (L, DECAY_LORA, RWKV_WIDTH), 0.5 * DECAY_LORA ** -0.5),
        "iclr_a0": nrm(11, (L, RWKV_WIDTH), 0.5),
        "iclr_up": nrm(12, (L, ICLR_LORA, RWKV_WIDTH), 0.5 * ICLR_LORA ** -0.5),
        "gate_up": nrm(13, (L, GATE_LORA, RWKV_WIDTH), GATE_LORA ** -0.5),
        "k_k": 0.85 + nrm(14, (L, RWKV_WIDTH), 0.05),
        "k_a": 1.0 + nrm(15, (L, RWKV_WIDTH), 0.05),
        "r_k": nrm(16, (L, RWKV_HEADS, HEAD_DIM), 0.1),
        "lnx_gain": 1.0 + nrm(17, (L, RWKV_WIDTH), 0.02),
        "lnx_bias": nrm(18, (L, RWKV_WIDTH), 0.01),
        "q_norm_gain": 1.0 + nrm(19, (L, HEAD_DIM), 0.02),
        "k_norm_gain": 1.0 + nrm(20, (L, HEAD_DIM), 0.02),
        "attn_sinks": nrm(21, (L, ATTN_Q_HEADS), 1.0),
        "branch_gate_b": nrm(22, (L, N_BRANCH * D_MODEL), 0.1),
        "w_branch_a": nrm(23, (L, RWKV_WIDTH, D_MODEL), RWKV_WIDTH ** -0.5),
        "w_branch_b": nrm(24, (L, ATTN_Q_WIDTH, D_MODEL), ATTN_Q_WIDTH ** -0.5),
        "w_out": nrm(25, (L, D_MODEL, D_MODEL), D_MODEL ** -0.5),
        "ffn_w1": nrm(26, (L, D_MODEL, D_FF), D_MODEL ** -0.5),
        "ffn_w3": nrm(27, (L, D_MODEL, D_FF), D_MODEL ** -0.5),
        "ffn_w2": nrm(28, (L, D_FF, D_MODEL), D_FF ** -0.5),
    }


def _fwd_reference(x, c, positions, ada_w, ada_b, norm1_gain, norm2_gain, w_in, tshift_mu,
              decay_w0, decay_up, iclr_a0, iclr_up, gate_up, k_k, k_a, r_k, lnx_gain, lnx_bias,
              q_norm_gain, k_norm_gain, attn_sinks, branch_gate_b, w_branch_a, w_branch_b, w_out,
              ffn_w1, ffn_w3, ffn_w2):
    q_lo = RWKV_SHIFT_WIDTH
    k_lo = q_lo + ATTN_Q_WIDTH
    v_lo = k_lo + ATTN_KV_WIDTH
    g_lo = v_lo + ATTN_KV_WIDTH
    for l in range(DEPTH):
        ada = (c @ ada_w[l] + ada_b[l])[:, None, :]
        shift1, scale1, gate1, shift2, scale2, gate2 = jnp.split(ada, 6, axis=-1)

        h = rms_norm(x, norm1_gain[l]) * (1 + scale1) + shift1
        proj = jnp.einsum('bsd,de->bse', h, w_in[l])
        rwkv_cols = proj[..., :q_lo]
        rwkv_cols = rwkv_cols + (token_shift(rwkv_cols) - rwkv_cols) * tshift_mu[l]
        y_a = rwkv7_time_mix(rwkv_cols, decay_w0[l], decay_up[l], iclr_a0[l], iclr_up[l], gate_up[l],
                             k_k[l], k_a[l], r_k[l], lnx_gain[l], lnx_bias[l])
        y_b = sliding_window_sink_attention(proj[..., q_lo:k_lo], proj[..., k_lo:v_lo], proj[..., v_lo:g_lo],
                                            positions, q_norm_gain[l], k_norm_gain[l], attn_sinks[l])
        gates = jax.nn.sigmoid(proj[..., g_lo:] + branch_gate_b[l])
        gate_a, gate_b = gates[..., :D_MODEL], gates[..., D_MODEL:]
        merged = gate_a * (y_a @ w_branch_a[l]) + gate_b * (y_b @ w_branch_b[l])
        x = x + gate1 * (merged @ w_out[l])

        h2 = rms_norm(x, norm2_gain[l]) * (1 + scale2) + shift2
        ffn = (jax.nn.silu(h2 @ ffn_w1[l]) * (h2 @ ffn_w3[l])) @ ffn_w2[l]
        x = x + gate2 * ffn
    return x


import jax as _jax
import jax.numpy as _jnp

TWIN_FORMAT = 'train_step'
FWD_PARAMS = ['x', 'c', 'positions', 'ada_w', 'ada_b', 'norm1_gain', 'norm2_gain', 'w_in', 'tshift_mu', 'decay_w0', 'decay_up', 'iclr_a0', 'iclr_up', 'gate_up', 'k_k', 'k_a', 'r_k', 'lnx_gain', 'lnx_bias', 'q_norm_gain', 'k_norm_gain', 'attn_sinks', 'branch_gate_b', 'w_branch_a', 'w_branch_b', 'w_out', 'ffn_w1', 'ffn_w3', 'ffn_w2']
TWIN_WEIGHTS = ['ada_w', 'ada_b', 'norm1_gain', 'norm2_gain', 'w_in', 'tshift_mu', 'decay_w0', 'decay_up', 'iclr_a0', 'iclr_up', 'gate_up', 'k_k', 'k_a', 'r_k', 'lnx_gain', 'lnx_bias', 'q_norm_gain', 'k_norm_gain', 'attn_sinks', 'branch_gate_b', 'w_branch_a', 'w_branch_b', 'w_out', 'ffn_w1', 'ffn_w3', 'ffn_w2']
TWIN_DIFF_INPUT = 'x'
TWIN_INPUTS = ['x', 'c', 'positions', 'ada_w', 'ada_b', 'norm1_gain', 'norm2_gain', 'w_in', 'tshift_mu', 'decay_w0', 'decay_up', 'iclr_a0', 'iclr_up', 'gate_up', 'k_k', 'k_a', 'r_k', 'lnx_gain', 'lnx_bias', 'q_norm_gain', 'k_norm_gain', 'attn_sinks', 'branch_gate_b', 'w_branch_a', 'w_branch_b', 'w_out', 'ffn_w1', 'ffn_w3', 'ffn_w2', 'loss_target', 'm_ada_w', 'm_ada_b', 'm_norm1_gain', 'm_norm2_gain', 'm_w_in', 'm_tshift_mu', 'm_decay_w0', 'm_decay_up', 'm_iclr_a0', 'm_iclr_up', 'm_gate_up', 'm_k_k', 'm_k_a', 'm_r_k', 'm_lnx_gain', 'm_lnx_bias', 'm_q_norm_gain', 'm_k_norm_gain', 'm_attn_sinks', 'm_branch_gate_b', 'm_w_branch_a', 'm_w_branch_b', 'm_w_out', 'm_ffn_w1', 'm_ffn_w3', 'm_ffn_w2', 'v_ada_w', 'v_ada_b', 'v_norm1_gain', 'v_norm2_gain', 'v_w_in', 'v_tshift_mu', 'v_decay_w0', 'v_decay_up', 'v_iclr_a0', 'v_iclr_up', 'v_gate_up', 'v_k_k', 'v_k_a', 'v_r_k', 'v_lnx_gain', 'v_lnx_bias', 'v_q_norm_gain', 'v_k_norm_gain', 'v_attn_sinks', 'v_branch_gate_b', 'v_w_branch_a', 'v_w_branch_b', 'v_w_out', 'v_ffn_w1', 'v_ffn_w3', 'v_ffn_w2']
TWIN_OUTPUTS = ['loss', 'grad_x', 'grad_ada_w', 'grad_ada_b', 'grad_norm1_gain', 'grad_norm2_gain', 'grad_w_in', 'grad_tshift_mu', 'grad_decay_w0', 'grad_decay_up', 'grad_iclr_a0', 'grad_iclr_up', 'grad_gate_up', 'grad_k_k', 'grad_k_a', 'grad_r_k', 'grad_lnx_gain', 'grad_lnx_bias', 'grad_q_norm_gain', 'grad_k_norm_gain', 'grad_attn_sinks', 'grad_branch_gate_b', 'grad_w_branch_a', 'grad_w_branch_b', 'grad_w_out', 'grad_ffn_w1', 'grad_ffn_w3', 'grad_ffn_w2', 'delta_ada_w', 'delta_ada_b', 'delta_norm1_gain', 'delta_norm2_gain', 'delta_w_in', 'delta_tshift_mu', 'delta_decay_w0', 'delta_decay_up', 'delta_iclr_a0', 'delta_iclr_up', 'delta_gate_up', 'delta_k_k', 'delta_k_a', 'delta_r_k', 'delta_lnx_gain', 'delta_lnx_bias', 'delta_q_norm_gain', 'delta_k_norm_gain', 'delta_attn_sinks', 'delta_branch_gate_b', 'delta_w_branch_a', 'delta_w_branch_b', 'delta_w_out', 'delta_ffn_w1', 'delta_ffn_w3', 'delta_ffn_w2', 'new_m_ada_w', 'new_m_ada_b', 'new_m_norm1_gain', 'new_m_norm2_gain', 'new_m_w_in', 'new_m_tshift_mu', 'new_m_decay_w0', 'new_m_decay_up', 'new_m_iclr_a0', 'new_m_iclr_up', 'new_m_gate_up', 'new_m_k_k', 'new_m_k_a', 'new_m_r_k', 'new_m_lnx_gain', 'new_m_lnx_bias', 'new_m_q_norm_gain', 'new_m_k_norm_gain', 'new_m_attn_sinks', 'new_m_branch_gate_b', 'new_m_w_branch_a', 'new_m_w_branch_b', 'new_m_w_out', 'new_m_ffn_w1', 'new_m_ffn_w3', 'new_m_ffn_w2', 'new_v_ada_w', 'new_v_ada_b', 'new_v_norm1_gain', 'new_v_norm2_gain', 'new_v_w_in', 'new_v_tshift_mu', 'new_v_decay_w0', 'new_v_decay_up', 'new_v_iclr_a0', 'new_v_iclr_up', 'new_v_gate_up', 'new_v_k_k', 'new_v_k_a', 'new_v_r_k', 'new_v_lnx_gain', 'new_v_lnx_bias', 'new_v_q_norm_gain', 'new_v_k_norm_gain', 'new_v_attn_sinks', 'new_v_branch_gate_b', 'new_v_w_branch_a', 'new_v_w_branch_b', 'new_v_w_out', 'new_v_ffn_w1', 'new_v_ffn_w3', 'new_v_ffn_w2']
TWIN_LEAF_KINDS = {'loss': 'loss', 'grad_x': 'grad_x', 'grad_ada_w': 'grad_w', 'grad_ada_b': 'grad_w', 'grad_norm1_gain': 'grad_w', 'grad_norm2_gain': 'grad_w', 'grad_w_in': 'grad_w', 'grad_tshift_mu': 'grad_w', 'grad_decay_w0': 'grad_w', 'grad_decay_up': 'grad_w', 'grad_iclr_a0': 'grad_w', 'grad_iclr_up': 'grad_w', 'grad_gate_up': 'grad_w', 'grad_k_k': 'grad_w', 'grad_k_a': 'grad_w', 'grad_r_k': 'grad_w', 'grad_lnx_gain': 'grad_w', 'grad_lnx_bias': 'grad_w', 'grad_q_norm_gain': 'grad_w', 'grad_k_norm_gain': 'grad_w', 'grad_attn_sinks': 'grad_w', 'grad_branch_gate_b': 'grad_w', 'grad_w_branch_a': 'grad_w', 'grad_w_branch_b': 'grad_w', 'grad_w_out': 'grad_w', 'grad_ffn_w1': 'grad_w', 'grad_ffn_w3': 'grad_w', 'grad_ffn_w2': 'grad_w', 'delta_ada_w': 'delta_w', 'delta_ada_b': 'delta_w', 'delta_norm1_gain': 'delta_w', 'delta_norm2_gain': 'delta_w', 'delta_w_in': 'delta_w', 'delta_tshift_mu': 'delta_w', 'delta_decay_w0': 'delta_w', 'delta_decay_up': 'delta_w', 'delta_iclr_a0': 'delta_w', 'delta_iclr_up': 'delta_w', 'delta_gate_up': 'delta_w', 'delta_k_k': 'delta_w', 'delta_k_a': 'delta_w', 'delta_r_k': 'delta_w', 'delta_lnx_gain': 'delta_w', 'delta_lnx_bias': 'delta_w', 'delta_q_norm_gain': 'delta_w', 'delta_k_norm_gain': 'delta_w', 'delta_attn_sinks': 'delta_w', 'delta_branch_gate_b': 'delta_w', 'delta_w_branch_a': 'delta_w', 'delta_w_branch_b': 'delta_w', 'delta_w_out': 'delta_w', 'delta_ffn_w1': 'delta_w', 'delta_ffn_w3': 'delta_w', 'delta_ffn_w2': 'delta_w', 'new_m_ada_w': 'new_m', 'new_m_ada_b': 'new_m', 'new_m_norm1_gain': 'new_m', 'new_m_norm2_gain': 'new_m', 'new_m_w_in': 'new_m', 'new_m_tshift_mu': 'new_m', 'new_m_decay_w0': 'new_m', 'new_m_decay_up': 'new_m', 'new_m_iclr_a0': 'new_m', 'new_m_iclr_up': 'new_m', 'new_m_gate_up': 'new_m', 'new_m_k_k': 'new_m', 'new_m_k_a': 'new_m', 'new_m_r_k': 'new_m', 'new_m_lnx_gain': 'new_m', 'new_m_lnx_bias': 'new_m', 'new_m_q_norm_gain': 'new_m', 'new_m_k_norm_gain': 'new_m', 'new_m_attn_sinks': 'new_m', 'new_m_branch_gate_b': 'new_m', 'new_m_w_branch_a': 'new_m', 'new_m_w_branch_b': 'new_m', 'new_m_w_out': 'new_m', 'new_m_ffn_w1': 'new_m', 'new_m_ffn_w3': 'new_m', 'new_m_ffn_w2': 'new_m', 'new_v_ada_w': 'new_v', 'new_v_ada_b': 'new_v', 'new_v_norm1_gain': 'new_v', 'new_v_norm2_gain': 'new_v', 'new_v_w_in': 'new_v', 'new_v_tshift_mu': 'new_v', 'new_v_decay_w0': 'new_v', 'new_v_decay_up': 'new_v', 'new_v_iclr_a0': 'new_v', 'new_v_iclr_up': 'new_v', 'new_v_gate_up': 'new_v', 'new_v_k_k': 'new_v', 'new_v_k_a': 'new_v', 'new_v_r_k': 'new_v', 'new_v_lnx_gain': 'new_v', 'new_v_lnx_bias': 'new_v', 'new_v_q_norm_gain': 'new_v', 'new_v_k_norm_gain': 'new_v', 'new_v_attn_sinks': 'new_v', 'new_v_branch_gate_b': 'new_v', 'new_v_w_branch_a': 'new_v', 'new_v_w_branch_b': 'new_v', 'new_v_w_out': 'new_v', 'new_v_ffn_w1': 'new_v', 'new_v_ffn_w3': 'new_v', 'new_v_ffn_w2': 'new_v'}


def _forward(args):
    return _fwd_reference(*[args[k] for k in FWD_PARAMS])


def _output_shape():
    out = _jax.eval_shape(lambda: _forward(_fwd_setup_inputs(0)))
    return out.shape, out.dtype

N_MICROBATCH = 1
ADAM_LR = 0.001
ADAM_B1 = 0.9
ADAM_B2 = 0.999
ADAM_EPS = 1e-08
ADAM_WD = 0.01
ADAM_STEP = 10
PER_EXAMPLE_BATCH_AXIS = {'x': 0, 'c': 0, 'positions': 0, 'loss_target': 0}
SHARED_INPUTS = []
_WEIGHT_DTYPES = {'ada_w': _jnp.float32, 'ada_b': _jnp.float32, 'norm1_gain': _jnp.float32, 'norm2_gain': _jnp.float32, 'w_in': _jnp.float32, 'tshift_mu': _jnp.float32, 'decay_w0': _jnp.float32, 'decay_up': _jnp.float32, 'iclr_a0': _jnp.float32, 'iclr_up': _jnp.float32, 'gate_up': _jnp.float32, 'k_k': _jnp.float32, 'k_a': _jnp.float32, 'r_k': _jnp.float32, 'lnx_gain': _jnp.float32, 'lnx_bias': _jnp.float32, 'q_norm_gain': _jnp.float32, 'k_norm_gain': _jnp.float32, 'attn_sinks': _jnp.float32, 'branch_gate_b': _jnp.float32, 'w_branch_a': _jnp.float32, 'w_branch_b': _jnp.float32, 'w_out': _jnp.float32, 'ffn_w1': _jnp.float32, 'ffn_w3': _jnp.float32, 'ffn_w2': _jnp.float32}
MOMENT_SCALE = {'ada_w': 5.706435e-01, 'ada_b': 6.960798e-01, 'norm1_gain': 4.974589e-02, 'norm2_gain': 1.212929e+00, 'w_in': 1.845255e-02, 'tshift_mu': 8.029695e-02, 'decay_w0': 1.096504e-02, 'decay_up': 1.842275e-03, 'iclr_a0': 1.150240e-02, 'iclr_up': 9.068292e-03, 'gate_up': 1.962982e-01, 'k_k': 4.018039e-02, 'k_a': 3.017519e-02, 'r_k': 1.831963e-01, 'lnx_gain': 3.883847e-01, 'lnx_bias': 2.856358e-02, 'q_norm_gain': 7.182331e-02, 'k_norm_gain': 7.286864e-02, 'attn_sinks': 1.746742e-02, 'branch_gate_b': 2.560174e-02, 'w_branch_a': 1.822845e-02, 'w_branch_b': 1.100145e-02, 'w_out': 2.036811e-02, 'ffn_w1': 2.455152e-02, 'ffn_w3': 2.339870e-02, 'ffn_w2': 3.815899e-02}


def _to_microbatches(a, axis):
    t = _jnp.moveaxis(a, axis, 0)
    t = t.reshape((N_MICROBATCH, t.shape[0] // N_MICROBATCH) + t.shape[1:])
    return _jnp.moveaxis(t, 1, axis + 1)


def setup_inputs(seed: int = 0) -> dict:
    inp = _fwd_setup_inputs(seed)
    key = _jax.random.fold_in(_jax.random.key(seed), 7919)
    shape, _ = _output_shape()
    out = dict(inp)
    out["loss_target"] = _jax.random.normal(_jax.random.fold_in(key, 0), shape, _jnp.float32)
    for i, name in enumerate(TWIN_WEIGHTS):
        w = inp[name].astype(_jnp.float32)
        if MOMENT_SCALE is None:
            s = _jnp.sqrt(_jnp.mean(_jnp.square(w)) + 1e-30)
        else:
            s = MOMENT_SCALE[name]
        km, kv = _jax.random.split(_jax.random.fold_in(key, i + 1))
        out[name] = w
        out["m_" + name] = s * _jax.random.normal(km, w.shape, _jnp.float32)
        out["v_" + name] = (s * s) * _jax.random.uniform(kv, w.shape, _jnp.float32, 0.5, 1.5)
    if N_MICROBATCH > 1:
        for name, axis in PER_EXAMPLE_BATCH_AXIS.items():
            out[name] = _to_microbatches(out[name], axis)
    return {'x': out['x'], 'c': out['c'], 'positions': out['positions'], 'ada_w': out['ada_w'], 'ada_b': out['ada_b'], 'norm1_gain': out['norm1_gain'], 'norm2_gain': out['norm2_gain'], 'w_in': out['w_in'], 'tshift_mu': out['tshift_mu'], 'decay_w0': out['decay_w0'], 'decay_up': out['decay_up'], 'iclr_a0': out['iclr_a0'], 'iclr_up': out['iclr_up'], 'gate_up': out['gate_up'], 'k_k': out['k_k'], 'k_a': out['k_a'], 'r_k': out['r_k'], 'lnx_gain': out['lnx_gain'], 'lnx_bias': out['lnx_bias'], 'q_norm_gain': out['q_norm_gain'], 'k_norm_gain': out['k_norm_gain'], 'attn_sinks': out['attn_sinks'], 'branch_gate_b': out['branch_gate_b'], 'w_branch_a': out['w_branch_a'], 'w_branch_b': out['w_branch_b'], 'w_out': out['w_out'], 'ffn_w1': out['ffn_w1'], 'ffn_w3': out['ffn_w3'], 'ffn_w2': out['ffn_w2'], 'loss_target': out['loss_target'], 'm_ada_w': out['m_ada_w'], 'm_ada_b': out['m_ada_b'], 'm_norm1_gain': out['m_norm1_gain'], 'm_norm2_gain': out['m_norm2_gain'], 'm_w_in': out['m_w_in'], 'm_tshift_mu': out['m_tshift_mu'], 'm_decay_w0': out['m_decay_w0'], 'm_decay_up': out['m_decay_up'], 'm_iclr_a0': out['m_iclr_a0'], 'm_iclr_up': out['m_iclr_up'], 'm_gate_up': out['m_gate_up'], 'm_k_k': out['m_k_k'], 'm_k_a': out['m_k_a'], 'm_r_k': out['m_r_k'], 'm_lnx_gain': out['m_lnx_gain'], 'm_lnx_bias': out['m_lnx_bias'], 'm_q_norm_gain': out['m_q_norm_gain'], 'm_k_norm_gain': out['m_k_norm_gain'], 'm_attn_sinks': out['m_attn_sinks'], 'm_branch_gate_b': out['m_branch_gate_b'], 'm_w_branch_a': out['m_w_branch_a'], 'm_w_branch_b': out['m_w_branch_b'], 'm_w_out': out['m_w_out'], 'm_ffn_w1': out['m_ffn_w1'], 'm_ffn_w3': out['m_ffn_w3'], 'm_ffn_w2': out['m_ffn_w2'], 'v_ada_w': out['v_ada_w'], 'v_ada_b': out['v_ada_b'], 'v_norm1_gain': out['v_norm1_gain'], 'v_norm2_gain': out['v_norm2_gain'], 'v_w_in': out['v_w_in'], 'v_tshift_mu': out['v_tshift_mu'], 'v_decay_w0': out['v_decay_w0'], 'v_decay_up': out['v_decay_up'], 'v_iclr_a0': out['v_iclr_a0'], 'v_iclr_up': out['v_iclr_up'], 'v_gate_up': out['v_gate_up'], 'v_k_k': out['v_k_k'], 'v_k_a': out['v_k_a'], 'v_r_k': out['v_r_k'], 'v_lnx_gain': out['v_lnx_gain'], 'v_lnx_bias': out['v_lnx_bias'], 'v_q_norm_gain': out['v_q_norm_gain'], 'v_k_norm_gain': out['v_k_norm_gain'], 'v_attn_sinks': out['v_attn_sinks'], 'v_branch_gate_b': out['v_branch_gate_b'], 'v_w_branch_a': out['v_w_branch_a'], 'v_w_branch_b': out['v_w_branch_b'], 'v_w_out': out['v_w_out'], 'v_ffn_w1': out['v_ffn_w1'], 'v_ffn_w3': out['v_ffn_w3'], 'v_ffn_w2': out['v_ffn_w2']}


def _loss(weights, diff, rest, loss_target):
    with _jax.named_scope("forward"):
        args = {**rest, TWIN_DIFF_INPUT: diff, **{k: w.astype(_WEIGHT_DTYPES[k]) for k, w in weights.items()}}
        y = _forward(args)
    with _jax.named_scope("loss_head"):
        err = _jnp.square(y.astype(_jnp.float32) - loss_target)
        return 0.5 * _jnp.sum(_jnp.mean(err, axis=-1)) if err.ndim else 0.5 * err


def _adamw(w, g, m, v):
    m = ADAM_B1 * m + (1.0 - ADAM_B1) * g
    v = ADAM_B2 * v + (1.0 - ADAM_B2) * _jnp.square(g)
    m_hat = m / (1.0 - ADAM_B1 ** ADAM_STEP)
    v_hat = v / (1.0 - ADAM_B2 ** ADAM_STEP)
    delta = -ADAM_LR * (m_hat / (_jnp.sqrt(v_hat) + ADAM_EPS) + ADAM_WD * w)
    return delta, m, v


def reference(x, c, positions, ada_w, ada_b, norm1_gain, norm2_gain, w_in, tshift_mu, decay_w0, decay_up, iclr_a0, iclr_up, gate_up, k_k, k_a, r_k, lnx_gain, lnx_bias, q_norm_gain, k_norm_gain, attn_sinks, branch_gate_b, w_branch_a, w_branch_b, w_out, ffn_w1, ffn_w3, ffn_w2, loss_target, m_ada_w, m_ada_b, m_norm1_gain, m_norm2_gain, m_w_in, m_tshift_mu, m_decay_w0, m_decay_up, m_iclr_a0, m_iclr_up, m_gate_up, m_k_k, m_k_a, m_r_k, m_lnx_gain, m_lnx_bias, m_q_norm_gain, m_k_norm_gain, m_attn_sinks, m_branch_gate_b, m_w_branch_a, m_w_branch_b, m_w_out, m_ffn_w1, m_ffn_w3, m_ffn_w2, v_ada_w, v_ada_b, v_norm1_gain, v_norm2_gain, v_w_in, v_tshift_mu, v_decay_w0, v_decay_up, v_iclr_a0, v_iclr_up, v_gate_up, v_k_k, v_k_a, v_r_k, v_lnx_gain, v_lnx_bias, v_q_norm_gain, v_k_norm_gain, v_attn_sinks, v_branch_gate_b, v_w_branch_a, v_w_branch_b, v_w_out, v_ffn_w1, v_ffn_w3, v_ffn_w2):
    given = dict(x=x, c=c, positions=positions, ada_w=ada_w, ada_b=ada_b, norm1_gain=norm1_gain, norm2_gain=norm2_gain, w_in=w_in, tshift_mu=tshift_mu, decay_w0=decay_w0, decay_up=decay_up, iclr_a0=iclr_a0, iclr_up=iclr_up, gate_up=gate_up, k_k=k_k, k_a=k_a, r_k=r_k, lnx_gain=lnx_gain, lnx_bias=lnx_bias, q_norm_gain=q_norm_gain, k_norm_gain=k_norm_gain, attn_sinks=attn_sinks, branch_gate_b=branch_gate_b, w_branch_a=w_branch_a, w_branch_b=w_branch_b, w_out=w_out, ffn_w1=ffn_w1, ffn_w3=ffn_w3, ffn_w2=ffn_w2, loss_target=loss_target, m_ada_w=m_ada_w, m_ada_b=m_ada_b, m_norm1_gain=m_norm1_gain, m_norm2_gain=m_norm2_gain, m_w_in=m_w_in, m_tshift_mu=m_tshift_mu, m_decay_w0=m_decay_w0, m_decay_up=m_decay_up, m_iclr_a0=m_iclr_a0, m_iclr_up=m_iclr_up, m_gate_up=m_gate_up, m_k_k=m_k_k, m_k_a=m_k_a, m_r_k=m_r_k, m_lnx_gain=m_lnx_gain, m_lnx_bias=m_lnx_bias, m_q_norm_gain=m_q_norm_gain, m_k_norm_gain=m_k_norm_gain, m_attn_sinks=m_attn_sinks, m_branch_gate_b=m_branch_gate_b, m_w_branch_a=m_w_branch_a, m_w_branch_b=m_w_branch_b, m_w_out=m_w_out, m_ffn_w1=m_ffn_w1, m_ffn_w3=m_ffn_w3, m_ffn_w2=m_ffn_w2, v_ada_w=v_ada_w, v_ada_b=v_ada_b, v_norm1_gain=v_norm1_gain, v_norm2_gain=v_norm2_gain, v_w_in=v_w_in, v_tshift_mu=v_tshift_mu, v_decay_w0=v_decay_w0, v_decay_up=v_decay_up, v_iclr_a0=v_iclr_a0, v_iclr_up=v_iclr_up, v_gate_up=v_gate_up, v_k_k=v_k_k, v_k_a=v_k_a, v_r_k=v_r_k, v_lnx_gain=v_lnx_gain, v_lnx_bias=v_lnx_bias, v_q_norm_gain=v_q_norm_gain, v_k_norm_gain=v_k_norm_gain, v_attn_sinks=v_attn_sinks, v_branch_gate_b=v_branch_gate_b, v_w_branch_a=v_w_branch_a, v_w_branch_b=v_w_branch_b, v_w_out=v_w_out, v_ffn_w1=v_ffn_w1, v_ffn_w3=v_ffn_w3, v_ffn_w2=v_ffn_w2)
    weights = {n: given[n] for n in TWIN_WEIGHTS}
    shared = {n: given[n] for n in SHARED_INPUTS}
    per_example = {n: given[n] for n in ['x', 'c', 'positions']}
    grad_fn = _jax.value_and_grad(_loss, argnums=(0, 1))

    def one_microbatch(ex, loss_target):
        ex = dict(ex)
        diff = ex.pop(TWIN_DIFF_INPUT)
        return grad_fn(weights, diff, {**shared, **ex}, loss_target)

    if N_MICROBATCH == 1:
        loss, (grad_w, grad_x) = one_microbatch(per_example, given["loss_target"])
    else:
        def body(carry, xs):
            loss_sum, grad_sum = carry
            l_k, (gw_k, gx_k) = one_microbatch(xs[0], xs[1])
            with _jax.named_scope("update"):
                return (loss_sum + l_k, _jax.tree.map(_jnp.add, grad_sum, gw_k)), gx_k

        init = (_jnp.zeros((), _jnp.float32), _jax.tree.map(_jnp.zeros_like, weights))
        (loss, grad_w), grad_x = _jax.lax.scan(body, init, (per_example, given["loss_target"]))
    with _jax.named_scope("update"):
        delta_w, new_m, new_v = {}, {}, {}
        for n in TWIN_WEIGHTS:
            delta_w[n], new_m[n], new_v[n] = _adamw(weights[n], grad_w[n], given["m_" + n], given["v_" + n])
    return (loss, grad_x, *[grad_w[n] for n in TWIN_WEIGHTS], *[delta_w[n] for n in TWIN_WEIGHTS],
            *[new_m[n] for n in TWIN_WEIGHTS], *[new_v[n] for n in TWIN_WEIGHTS])
```

```python
import functools
import math

import jax
import jax.numpy as jnp
from jax import lax
from jax.experimental import pallas as pl
from jax.experimental.pallas import tpu as pltpu

F32 = jnp.float32
BF16 = jnp.bfloat16
MXU = BF16
HI = lax.Precision.HIGHEST

D = 1024
HD = 64
NH = 8
RW = NH * HD
SHIFT_W = 3 * RW + 64 + 64 + 128
QKV_W = RW + 2 * 128
GATE_W = 2 * D
IN_W = SHIFT_W + QKV_W + GATE_W
DFF = 2816
BLK = 128
CHUNK = 64
RMS_EPS = 1e-6
GN_EPS = 64e-5
NEG_INF = -1e30
ADAM_LR, ADAM_B1, ADAM_B2, ADAM_EPS, ADAM_WD, ADAM_STEP = 0.001, 0.9, 0.999, 1e-08, 0.01, 10
VMEM_LIMIT = 56 * 1024 * 1024
MESH = pl.DeviceIdType.MESH


def _cparams(sem=None):
    return pltpu.CompilerParams(dimension_semantics=sem, vmem_limit_bytes=VMEM_LIMIT)


def _full_spec(a):
    nd = a.ndim
    return pl.BlockSpec(a.shape, lambda *_: (0,) * nd)


def _rowwise(fn, rows, consts, outs, *, tm, name, halo=()):
    rows = [a if isinstance(a, tuple) else (a, a.shape[1]) for a in rows]
    T = rows[0][0].shape[0]
    assert T % tm == 0 and tm % 8 == 0
    n_tiles = T // tm
    n_in = len(rows) + len(halo) + len(consts)
    in_specs = [pl.BlockSpec((tm, nc), lambda i: (i, 0)) for _, nc in rows]
    args = [a for a, _ in rows]
    for a, nc, kind in halo:
        if kind == 'prev':
            in_specs.append(pl.BlockSpec((8, nc), lambda i: (jnp.maximum(i * (tm // 8) - 1, 0), 0)))
        else:
            in_specs.append(pl.BlockSpec((8, nc), lambda i: (jnp.minimum((i + 1) * (tm // 8), T // 8 - 1), 0)))
        args.append(a)
    in_specs += [_full_spec(a) for a in consts]
    args += list(consts)
    out_shape, out_specs = [], []
    for ncols, dtype, kind in outs:
        if kind == 'row':
            out_shape.append(jax.ShapeDtypeStruct((T, ncols), dtype))
            out_specs.append(pl.BlockSpec((tm, ncols), lambda i: (i, 0)))
        else:
            out_shape.append(jax.ShapeDtypeStruct((kind, ncols), dtype))
            out_specs.append(pl.BlockSpec((kind, ncols), lambda i: (0, 0)))

    def body(*refs):
        i = pl.program_id(0)
        vals = [r[...] for r in refs[:n_in]]
        res = fn(i, n_tiles, *vals)
        for (ncols, dtype, kind), o_ref, val in zip(outs, refs[n_in:], res, strict=True):
            if kind == 'row':
                o_ref[...] = val.astype(dtype)
            else:
                @pl.when(i == 0)
                def _():
                    o_ref[...] = jnp.zeros_like(o_ref)
                o_ref[...] += val.astype(dtype)

    res = pl.pallas_call(
        body, name=name, grid=(n_tiles,), in_specs=in_specs, out_specs=out_specs, out_shape=out_shape,
        compiler_params=_cparams(("arbitrary",)),
    )(*args)
    return res


def _pick(n, cands):
    for c in cands:
        if n % c == 0:
            return c
    return n


def _mm_nn(a, w, *, name, out_dtype=F32):
    T, K = a.shape
    N = w.shape[1]
    tm = _pick(T, (512, 256, 128))
    tn = _pick(N, (1024, 1408, 896, 768, 512, 256, 128))

    def body(a_ref, w_ref, o_ref):
        o_ref[...] = jnp.dot(a_ref[...], w_ref[...], preferred_element_type=F32).astype(out_dtype)

    return pl.pallas_call(
        body, name=name, grid=(N // tn, T // tm),
        in_specs=[pl.BlockSpec((tm, K), lambda j, i: (i, 0)), pl.BlockSpec((K, tn), lambda j, i: (0, j))],
        out_specs=pl.BlockSpec((tm, tn), lambda j, i: (i, j)),
        out_shape=jax.ShapeDtypeStruct((T, N), out_dtype),
        compiler_params=_cparams(("arbitrary", "arbitrary")),
    )(a, w)


def _mm_nt(dy, w, *, name, out_dtype=F32):
    T, N = dy.shape
    K = w.shape[0]
    tm = _pick(T, (512, 256, 128))
    tk = _pick(K, (1024, 1408, 896, 768, 512, 256, 128))

    def body(dy_ref, w_ref, o_ref):
        o_ref[...] = lax.dot_general(dy_ref[...], w_ref[...], (((1,), (1,)), ((), ())),
                                     preferred_element_type=F32).astype(out_dtype)

    return pl.pallas_call(
        body, name=name, grid=(K // tk, T // tm),
        in_specs=[pl.BlockSpec((tm, N), lambda j, i: (i, 0)), pl.BlockSpec((tk, N), lambda j, i: (j, 0))],
        out_specs=pl.BlockSpec((tm, tk), lambda j, i: (i, j)),
        out_shape=jax.ShapeDtypeStruct((T, K), out_dtype),
        compiler_params=_cparams(("arbitrary", "arbitrary")),
    )(dy, w)


def _mm_tn(a, dy, *, name):
    T, K = a.shape
    N = dy.shape[1]
    tm = _pick(T, (512, 256, 128))
    tn = _pick(N, (1024, 1408, 896, 768, 512, 256, 128))
    n_t = T // tm

    def body(a_ref, dy_ref, o_ref):
        i = pl.program_id(1)

        @pl.when(i == 0)
        def _():
            o_ref[...] = jnp.zeros_like(o_ref)

        o_ref[...] += lax.dot_general(a_ref[...], dy_ref[...], (((0,), (0,)), ((), ())), preferred_element_type=F32)

    return pl.pallas_call(
        body, name=name, grid=(N // tn, n_t),
        in_specs=[pl.BlockSpec((tm, K), lambda j, i: (i, 0)), pl.BlockSpec((tm, tn), lambda j, i: (i, j))],
        out_specs=pl.BlockSpec((K, tn), lambda j, i: (0, j)),
        out_shape=jax.ShapeDtypeStruct((K, N), F32),
        compiler_params=_cparams(("arbitrary", "arbitrary")),
    )(a, dy)


def _seg_ones(n):
    r = lax.broadcasted_iota(jnp.int32, (n, n), 0) // HD
    c = lax.broadcasted_iota(jnp.int32, (n, n), 1) // HD
    return (r == c).astype(F32)


def _segsum(x):
    return jnp.dot(x, _seg_ones(x.shape[1]), precision=HI, preferred_element_type=F32)


def _mxu(x):
    return x.astype(MXU)


@jax.custom_vjp
def _bdot(a, b):
    return jnp.dot(_mxu(a), _mxu(b), preferred_element_type=F32)


def _bdot_fwd(a, b):
    return _bdot(a, b), (a, b)


def _bdot_bwd(res, g):
    a, b = res
    da = lax.dot_general(_mxu(g), _mxu(b), (((1,), (1,)), ((), ())), preferred_element_type=F32)
    db = lax.dot_general(_mxu(a), _mxu(g), (((0,), (0,)), ((), ())), preferred_element_type=F32)
    return da.astype(a.dtype), db.astype(b.dtype)


_bdot.defvjp(_bdot_fwd, _bdot_bwd)


@jax.custom_vjp
def _bdot_nt(a, b):
    return lax.dot_general(_mxu(a), _mxu(b), (((1,), (1,)), ((), ())), preferred_element_type=F32)


def _bdot_nt_fwd(a, b):
    return _bdot_nt(a, b), (a, b)


def _bdot_nt_bwd(res, g):
    a, b = res
    da = jnp.dot(_mxu(g), _mxu(b), preferred_element_type=F32)
    db = lax.dot_general(_mxu(g), _mxu(a), (((0,), (0,)), ((), ())), preferred_element_type=F32)
    return da.astype(a.dtype), db.astype(b.dtype)


_bdot_nt.defvjp(_bdot_nt_fwd, _bdot_nt_bwd)


def _sigmoid(x):
    return 1.0 / (1.0 + jnp.exp(-x))


def _softplus(x):
    return jnp.maximum(x, 0.0) + jnp.log(1.0 + jnp.exp(jnp.minimum(x, -x)))


def _norm_mod(x, gain, scale, shift):
    inv = lax.rsqrt(jnp.mean(x * x, axis=-1, keepdims=True) + RMS_EPS)
    return (x * inv) * gain * (1.0 + scale) + shift


def _prep(mixed, decay_w0, lora_up, iclr_a0, gate_up, k_k, k_a):
    r = mixed[:, 0:RW]
    k = mixed[:, RW:2 * RW]
    v = mixed[:, 2 * RW:3 * RW]
    z = mixed[:, 3 * RW:3 * RW + 128]
    xg = mixed[:, 3 * RW + 128:]
    lane = lax.broadcasted_iota(jnp.int32, z.shape, 1)
    tz = jnp.where(lane < 64, jnp.tanh(z), z)
    lo = _bdot(tz, lora_up)
    w_log = -_softplus(-(decay_w0 + lo[:, :RW])) - 0.5
    lw = -jnp.exp(w_log)
    a_ic = _sigmoid(iclr_a0 + lo[:, RW:])
    g = _bdot(_sigmoid(xg), gate_up)
    kk = k * k_k
    kk = kk / jnp.maximum(jnp.sqrt(_segsum(kk * kk)), 1e-12)
    k_mod = k * (1.0 + (a_ic - 1.0) * k_a)
    return jnp.concatenate([r, lw, k_mod, v, -kk, kk * a_ic, g], axis=1)


def _post(y, r, k, v, g, lnx_gain, lnx_bias, r_k):
    mu = _segsum(y) * (1.0 / HD)
    yc = y - mu
    var = _segsum(yc * yc) * (1.0 / HD)
    yn = yc * lax.rsqrt(var + GN_EPS) * lnx_gain + lnx_bias
    bonus = _segsum(r * k * r_k) * v
    return (yn + bonus) * g


def _merge(pg, ma, mb, bias):
    gates = _sigmoid(pg + bias)
    return gates[:, :D] * ma + gates[:, D:] * mb


def _swiglu(u, v):
    return u * _sigmoid(u) * v


@functools.partial(jax.custom_vjp, nondiff_argnums=(1,))
def _lane_roll(x, s):
    return pltpu.roll(x, s, 1)


def _lane_roll_fwd(x, s):
    return pltpu.roll(x, s, 1), None


def _lane_roll_bwd(s, _, g):
    n = g.shape[1]
    return (pltpu.roll(g, (n - s) % n, 1),)


_lane_roll.defvjp(_lane_roll_fwd, _lane_roll_bwd)


def _rope(x, cos, sin_lo, sin_hi):
    n = x.shape[1]
    return x * cos + _lane_roll(x, n - 8) * sin_lo + _lane_roll(x, 8) * sin_hi


def _head_rms(x, gain):
    return x * lax.rsqrt(_segsum(x * x) * (1.0 / HD) + RMS_EPS) * gain


def _attn_block(qkv_c, qkv_p, tab_c, tab_p, qg, kg, sinks, first):
    def tabs(tab, n):
        return tab[:, 0:n], tab[:, RW:RW + n], tab[:, 2 * RW:2 * RW + n]

    qg = jnp.concatenate([qg] * NH, axis=1)
    kg = jnp.concatenate([kg] * 2, axis=1)
    q = _rope(_head_rms(qkv_c[:, :RW], qg), *tabs(tab_c, RW))
    k_c = _rope(_head_rms(qkv_c[:, RW:RW + 128], kg), *tabs(tab_c, 128))
    k_p = _rope(_head_rms(qkv_p[:, RW:RW + 128], kg), *tabs(tab_p, 128))
    kband = jnp.concatenate([k_p, k_c], axis=0)
    vband = jnp.concatenate([qkv_p[:, RW + 128:], qkv_c[:, RW + 128:]], axis=0)
    G = 4
    qi = lax.broadcasted_iota(jnp.int32, (G * BLK, 2 * BLK), 0) % BLK
    kj = lax.broadcasted_iota(jnp.int32, (G * BLK, 2 * BLK), 1)
    dist = qi + BLK - kj
    valid = (dist >= 0) & (dist < BLK) & (jnp.logical_not(first) | (kj >= BLK))
    row_g = lax.broadcasted_iota(jnp.int32, (G * BLK, 1), 0) // BLK
    outs = []
    for kvh in range(2):
        kb = kband[:, kvh * HD:(kvh + 1) * HD]
        vb = vband[:, kvh * HD:(kvh + 1) * HD]
        qs = jnp.concatenate([q[:, (G * kvh + g) * HD:(G * kvh + g + 1) * HD] for g in range(G)], axis=0)
        s = _bdot_nt(qs, kb) * (HD ** -0.5)
        s = jnp.where(valid, s, NEG_INF)
        sink = jnp.zeros((G * BLK, 1), F32)
        for g in range(G):
            sink = jnp.where(row_g == g, sinks[:, G * kvh + g:G * kvh + g + 1], sink)
        m = lax.stop_gradient(jnp.maximum(jnp.max(s, axis=-1, keepdims=True), sink))
        e = jnp.exp(s - m)
        p = e / (jnp.sum(e, axis=-1, keepdims=True) + jnp.exp(sink - m))
        o = _bdot(p, vb)
        outs += [o[g * BLK:(g + 1) * BLK] for g in range(G)]
    return jnp.concatenate(outs, axis=1)


def _heads(x):
    return jnp.stack([x[:, h * HD:(h + 1) * HD] for h in range(NH)], axis=0)


def _unheads(x):
    return jnp.concatenate([x[h] for h in range(NH)], axis=1)


def _bmm(a, b, ca, cb):
    return lax.dot_general(a, b, (((ca,), (cb,)), ((0,), (0,))), precision=HI, preferred_element_type=F32)


def _chunk(S0, r, lw, k, v, a, b):
    C = r.shape[1]
    ri = lax.broadcasted_iota(jnp.int32, (C, C), 0)
    ci = lax.broadcasted_iota(jnp.int32, (C, C), 1)
    incl = (ri >= ci)
    strict = (ri > ci)
    eye = (ri == ci).astype(F32)
    tri = jnp.broadcast_to(incl.astype(F32), (NH, C, C))
    cum = _bmm(tri, lw, 2, 1)
    p_in = jnp.exp(cum)
    p_ex = jnp.exp(cum - lw)
    p_inv = jnp.exp(-cum)
    at, rt, bt, kt = a * p_ex, r * p_in, b * p_inv, k * p_inv
    a_ab = jnp.where(strict, _bmm(at, bt, 2, 2), 0.0)
    a_ak = jnp.where(strict, _bmm(at, kt, 2, 2), 0.0)
    a_rb = jnp.where(incl, _bmm(rt, bt, 2, 2), 0.0)
    a_rk = jnp.where(incl, _bmm(rt, kt, 2, 2), 0.0)
    rhs = _bmm(at, S0, 2, 2) + _bmm(a_ak, v, 2, 1)
    x = eye + a_ab
    lp = a_ab
    for _ in range(int(math.log2(C)) - 1):
        lp = _bmm(lp, lp, 2, 1)
        x = x + _bmm(x, lp, 2, 1)
    u = _bmm(x, rhs, 2, 1)
    y = _bmm(rt, S0, 2, 2) + _bmm(a_rb, u, 2, 1) + _bmm(a_rk, v, 2, 1)
    p_last = jnp.exp(cum[:, C - 1:C, :])
    S1 = (S0 + _bmm(u, bt, 1, 1) + _bmm(v, kt, 1, 1)) * p_last
    return y, S1


def _scan_fwd(rw, *, name):
    T = rw.shape[0]
    n = T // CHUNK

    def body(r_ref, lw_ref, k_ref, v_ref, a_ref, b_ref, y_ref, ck_ref, s_ref):
        @pl.when(pl.program_id(0) == 0)
        def _():
            s_ref[...] = jnp.zeros_like(s_ref)

        S0 = s_ref[...]
        ck_ref[0] = S0
        y, S1 = _chunk(S0, *[_heads(ref[...]) for ref in (r_ref, lw_ref, k_ref, v_ref, a_ref, b_ref)])
        y_ref[...] = _unheads(y)
        s_ref[...] = S1

    col = lambda j: pl.BlockSpec((CHUNK, RW), lambda i: (i, j))
    return pl.pallas_call(
        body, name=name, grid=(n,),
        in_specs=[col(j) for j in range(6)],
        out_specs=[pl.BlockSpec((CHUNK, RW), lambda i: (i, 0)), pl.BlockSpec((1, NH, HD, HD), lambda i: (i, 0, 0, 0))],
        out_shape=[jax.ShapeDtypeStruct((T, RW), F32), jax.ShapeDtypeStruct((n, NH, HD, HD), F32)],
        scratch_shapes=[pltpu.VMEM((NH, HD, HD), F32)],
        compiler_params=_cparams(("arbitrary",)),
    )(rw, rw, rw, rw, rw, rw)


def _scan_bwd(rw, ck, dy, *, name):
    T = rw.shape[0]
    n = T // CHUNK

    def body(r_ref, lw_ref, k_ref, v_ref, a_ref, b_ref, ck_ref, dy_ref, o_ref, ds_ref):
        @pl.when(pl.program_id(0) == 0)
        def _():
            ds_ref[...] = jnp.zeros_like(ds_ref)

        prim = [_heads(ref[...]) for ref in (r_ref, lw_ref, k_ref, v_ref, a_ref, b_ref)]
        _, vjp = jax.vjp(_chunk, ck_ref[0], *prim)
        grads = vjp((_heads(dy_ref[...]), ds_ref[...]))
        ds_ref[...] = grads[0]
        o_ref[...] = jnp.concatenate([_unheads(g) for g in grads[1:]], axis=1)

    col = lambda j: pl.BlockSpec((CHUNK, RW), lambda i: (n - 1 - i, j))
    return pl.pallas_call(
        body, name=name, grid=(n,),
        in_specs=[col(j) for j in range(6)] + [pl.BlockSpec((1, NH, HD, HD), lambda i: (n - 1 - i, 0, 0, 0)),
                                               pl.BlockSpec((CHUNK, RW), lambda i: (n - 1 - i, 0))],
        out_specs=pl.BlockSpec((CHUNK, 6 * RW), lambda i: (n - 1 - i, 0)),
        out_shape=jax.ShapeDtypeStruct((T, 6 * RW), F32),
        scratch_shapes=[pltpu.VMEM((NH, HD, HD), F32)],
        compiler_params=_cparams(("arbitrary",)),
    )(rw, rw, rw, rw, rw, rw, ck, dy)


def _attn_fwd(qkv, tab, qg, kg, sinks, *, name):
    T = qkv.shape[0]
    n = T // BLK

    def body(c_ref, p_ref, tc_ref, tp_ref, qg_ref, kg_ref, s_ref, o_ref):
        first = pl.program_id(0) == 0
        o_ref[...] = _attn_block(c_ref[...], p_ref[...], tc_ref[...], tp_ref[...], qg_ref[...], kg_ref[...],
                                 s_ref[...], first).astype(o_ref.dtype)

    cur = lambda w: pl.BlockSpec((BLK, w), lambda i: (i, 0))
    prev = lambda w: pl.BlockSpec((BLK, w), lambda i: (jnp.maximum(i - 1, 0), 0))
    return pl.pallas_call(
        body, name=name, grid=(n,),
        in_specs=[cur(QKV_W), prev(QKV_W), cur(3 * RW), prev(3 * RW), _full_spec(qg), _full_spec(kg), _full_spec(sinks)],
        out_specs=cur(RW), out_shape=jax.ShapeDtypeStruct((T, RW), MXU),
        compiler_params=_cparams(("arbitrary",)),
    )(qkv, qkv, tab, tab, qg, kg, sinks)


def _attn_bwd(qkv, tab, qg, kg, sinks, dy, *, name):
    T = qkv.shape[0]
    n = T // BLK

    def body(c_ref, p_ref, tc_ref, tp_ref, qg_ref, kg_ref, s_ref, dy_ref, dqkv_ref, dqg_ref, dkg_ref, ds_ref, carry_ref):
        i = pl.program_id(0)
        first = i == n - 1

        @pl.when(i == 0)
        def _():
            carry_ref[...] = jnp.zeros_like(carry_ref)
            dqg_ref[...] = jnp.zeros_like(dqg_ref)
            dkg_ref[...] = jnp.zeros_like(dkg_ref)
            ds_ref[...] = jnp.zeros_like(ds_ref)

        tc, tp = tc_ref[...], tp_ref[...]
        f = lambda c, p, qg_, kg_, sk: _attn_block(c, p, tc, tp, qg_, kg_, sk, first)
        _, vjp = jax.vjp(f, c_ref[...], p_ref[...], qg_ref[...], kg_ref[...], s_ref[...])
        dc, dp, dqg, dkg, dsk = vjp(dy_ref[...].astype(F32))
        dqkv_ref[...] = (dc + carry_ref[...]).astype(dqkv_ref.dtype)
        carry_ref[...] = dp
        dqg_ref[...] += dqg
        dkg_ref[...] += dkg
        ds_ref[...] += dsk

    cur = lambda w: pl.BlockSpec((BLK, w), lambda i: (n - 1 - i, 0))
    prev = lambda w: pl.BlockSpec((BLK, w), lambda i: (jnp.maximum(n - 2 - i, 0), 0))
    return pl.pallas_call(
        body, name=name, grid=(n,),
        in_specs=[cur(QKV_W), prev(QKV_W), cur(3 * RW), prev(3 * RW), _full_spec(qg), _full_spec(kg), _full_spec(sinks),
                  cur(RW)],
        out_specs=[cur(QKV_W), _full_spec(qg), _full_spec(kg), _full_spec(sinks)],
        out_shape=[jax.ShapeDtypeStruct((T, QKV_W), MXU), jax.ShapeDtypeStruct(qg.shape, F32),
                   jax.ShapeDtypeStruct(kg.shape, F32), jax.ShapeDtypeStruct(sinks.shape, F32)],
        scratch_shapes=[pltpu.VMEM((BLK, QKV_W), F32)],
        compiler_params=_cparams(("arbitrary",)),
    )(qkv, qkv, tab, tab, qg, kg, sinks, dy)


def _shift_down(cur, prev8, i):
    rolled = pltpu.roll(cur, 1, 0)
    first_row = jnp.where(i > 0, prev8[7:8, :], 0.0)
    row = lax.broadcasted_iota(jnp.int32, cur.shape, 0)
    return jnp.where(row == 0, first_row, rolled)


def _shift_up(cur, next8, i, n):
    tm = cur.shape[0]
    rolled = pltpu.roll(cur, tm - 1, 0)
    last_row = jnp.where(i < n - 1, next8[0:1, :], 0.0)
    row = lax.broadcasted_iota(jnp.int32, cur.shape, 0)
    return jnp.where(row == tm - 1, last_row, rolled)


def _ada_parts(ada):
    return [ada[:, j * D:(j + 1) * D] for j in range(6)]


def _rope_table(positions):
    half = HD // 8
    inv_freq = 500000.0 ** (-jnp.arange(half, dtype=F32) / half)
    ang = positions.astype(F32)[:, None] * inv_freq
    cos, sin = jnp.cos(ang), jnp.sin(ang)
    T = positions.shape[0]
    pad = HD - 2 * half
    c64 = jnp.concatenate([cos, cos, jnp.ones((T, pad), F32)], axis=1)
    lo64 = jnp.concatenate([-sin, jnp.zeros((T, HD - half), F32)], axis=1)
    hi64 = jnp.concatenate([jnp.zeros((T, half), F32), sin, jnp.zeros((T, pad), F32)], axis=1)
    return jnp.concatenate([jnp.tile(t, (1, NH)) for t in (c64, lo64, hi64)], axis=1)


def _local_step(x, target, ada, tab, w, s):
    T = x.shape[0]
    tm = _pick(T, (512, 256, 128))
    tm_wide = _pick(T, (256, 128))
    tm_vjp = _pick(T, (128,))
    row = lambda n, dt=F32: (n, dt, 'row')
    acc = lambda n, r=1: (n, F32, r)

    def f_norm1(i, n, x_, g, ada_):
        sh, sc = ada_[:, 0:D], ada_[:, D:2 * D]
        return (_norm_mod(x_, g, sc, sh),)
    (h1,) = _rowwise(f_norm1, [x], [s['norm1_gain'], ada], [row(D, MXU)], tm=tm, name="f_norm1")
    proj = _mm_nn(h1, w['w_in'], name="f_proj")
    proj_qkv = proj[:, SHIFT_W:SHIFT_W + QKV_W]
    proj_g = proj[:, SHIFT_W + QKV_W:]
    prep_consts = [s['decay_w0'], s['lora_up'], s['iclr_a0'], s['gate_up'], s['k_k'], s['k_a']]

    def f_prep(i, n, cur, prev8, mu, *params):
        mixed = cur + (_shift_down(cur, prev8, i) - cur) * mu
        return (_prep(mixed, *params),)
    (rw,) = _rowwise(f_prep, [(proj, SHIFT_W)], [s['tshift_mu']] + prep_consts, [row(7 * RW)], tm=tm_wide,
                     name="f_prep", halo=[(proj, SHIFT_W, 'prev')])
    y, ck = _scan_fwd(rw, name="f_scan")
    post_consts = [s['lnx_gain'], s['lnx_bias'], s['r_k']]

    def post_of(y_, rw_, *params):
        return _post(y_, rw_[:, 0:RW], rw_[:, 2 * RW:3 * RW], rw_[:, 3 * RW:4 * RW], rw_[:, 6 * RW:7 * RW], *params)

    def f_post(i, n, y_, rw_, *params):
        return (post_of(y_, rw_, *params),)
    (ya,) = _rowwise(f_post, [y, rw], post_consts, [row(RW, MXU)], tm=tm_wide, name="f_post")
    yb = _attn_fwd(proj_qkv, tab, s['q_norm_gain'], s['k_norm_gain'], s['attn_sinks'], name="f_attn")
    ma = _mm_nn(ya, w['w_branch_a'], name="f_branch_a")
    mb = _mm_nn(yb, w['w_branch_b'], name="f_branch_b")

    def f_merge(i, n, pg, ma_, mb_, bias):
        return (_merge(pg, ma_, mb_, bias),)
    (merged,) = _rowwise(f_merge, [proj_g, ma, mb], [s['branch_gate_b']], [row(D, MXU)], tm=tm, name="f_merge")
    mo = _mm_nn(merged, w['w_out'], name="f_out")

    def f_res1(i, n, x_, mo_, g, ada_):
        x1_ = x_ + ada_[:, 2 * D:3 * D] * mo_
        return x1_, _norm_mod(x1_, g, ada_[:, 4 * D:5 * D], ada_[:, 3 * D:4 * D])
    x1, h2 = _rowwise(f_res1, [x, mo], [s['norm2_gain'], ada], [row(D), row(D, MXU)], tm=tm, name="f_res1")
    uv = _mm_nn(h2, w['ffn_w13'], name="f_ffn_in")

    def f_act(i, n, uv_):
        return (_swiglu(uv_[:, :DFF], uv_[:, DFF:]),)
    (act,) = _rowwise(f_act, [uv], [], [row(DFF, MXU)], tm=tm_wide, name="f_act")
    ff = _mm_nn(act, w['ffn_w2'], name="f_ffn_out")

    def f_loss(i, n, x1_, ff_, tgt, ada_):
        g2 = ada_[:, 5 * D:6 * D]
        err = x1_ + g2 * ff_ - tgt
        dx2 = err * (1.0 / D)
        loss = 0.5 * jnp.sum(jnp.sum(err * err, axis=1, keepdims=True) * (1.0 / D), axis=0, keepdims=True)
        return dx2, (dx2 * g2), jnp.broadcast_to(loss, (1, 128)), jnp.sum(dx2 * ff_, axis=0, keepdims=True)
    dx2, dff, loss, dgate2 = _rowwise(f_loss, [x1, ff, target], [ada], [row(D), row(D, MXU), acc(128), acc(D)],
                                      tm=tm, name="f_loss")

    dact = _mm_nt(dff, w['ffn_w2'], name="b_ffn_out_dx")
    g_w2 = _mm_tn(act, dff, name="b_ffn_out_dw")

    def b_act(i, n, uv_, dact_):
        _, vjp = jax.vjp(_swiglu, uv_[:, :DFF], uv_[:, DFF:])
        du, dv = vjp(dact_)
        return (jnp.concatenate([du, dv], axis=1),)
    (duv,) = _rowwise(b_act, [uv, dact], [], [row(2 * DFF, MXU)], tm=tm_wide, name="b_act")
    dh2 = _mm_nt(duv, w['ffn_w13'], name="b_ffn_in_dx")
    g_w13 = _mm_tn(h2, duv, name="b_ffn_in_dw")

    def b_res1(i, n, x1_, dh2_, dx2_, mo_, g, ada_):
        _, vjp = jax.vjp(_norm_mod, x1_, g, ada_[:, 4 * D:5 * D], ada_[:, 3 * D:4 * D])
        dxn, dg, dsc, dsh = vjp(dh2_)
        dx1_ = dxn + dx2_
        g1 = ada_[:, 2 * D:3 * D]
        return dx1_, dx1_ * g1, dg, dsc, dsh, jnp.sum(dx1_ * mo_, axis=0, keepdims=True)
    dx1, dmo, d_gain2, d_scale2, d_shift2, dgate1 = _rowwise(
        b_res1, [x1, dh2, dx2, mo], [s['norm2_gain'], ada], [row(D), row(D, MXU), acc(D), acc(D), acc(D), acc(D)],
        tm=tm, name="b_res1")
    dmerged = _mm_nt(dmo, w['w_out'], name="b_out_dx")
    g_wout = _mm_tn(merged, dmo, name="b_out_dw")

    def b_merge(i, n, pg, ma_, mb_, dm, bias):
        _, vjp = jax.vjp(_merge, pg, ma_, mb_, bias)
        dpg, dma_, dmb_, dbias = vjp(dm)
        return dpg, dma_, dmb_, dbias
    dpg, dma, dmb, d_bias = _rowwise(b_merge, [proj_g, ma, mb, dmerged], [s['branch_gate_b']],
                                     [row(GATE_W, MXU), row(D, MXU), row(D, MXU), acc(GATE_W)], tm=tm_wide, name="b_merge")
    dya = _mm_nt(dma, w['w_branch_a'], name="b_branch_a_dx")
    g_wa = _mm_tn(ya, dma, name="b_branch_a_dw")
    dyb = _mm_nt(dmb, w['w_branch_b'], name="b_branch_b_dx", out_dtype=F32)
    g_wb = _mm_tn(yb, dmb, name="b_branch_b_dw")
    dqkv, d_qg, d_kg, d_sinks = _attn_bwd(proj_qkv, tab, s['q_norm_gain'], s['k_norm_gain'], s['attn_sinks'], dyb,
                                          name="b_attn")

    def b_post(i, n, y_, rw_, dya_, *params):
        _, vjp = jax.vjp(post_of, y_, rw_, *params)
        return vjp(dya_)
    dy, drw_post, d_lnx_gain, d_lnx_bias, d_r_k = _rowwise(
        b_post, [y, rw, dya], post_consts, [row(RW), row(7 * RW), acc(RW), acc(RW), acc(RW)], tm=tm_vjp, name="b_post")
    dscan = _scan_bwd(rw, ck, dy, name="b_scan")

    def b_prep(i, n, cur, drw_, dscan_, prev8, mu, *params):
        shifted = _shift_down(cur, prev8, i)
        mixed = cur + (shifted - cur) * mu
        _, vjp = jax.vjp(_prep, mixed, *params)
        ct = drw_ + jnp.concatenate([dscan_, jnp.zeros((dscan_.shape[0], RW), F32)], axis=1)
        grads = vjp(ct)
        dmixed = grads[0]
        return (dmixed, jnp.sum(dmixed * (shifted - cur), axis=0, keepdims=True)) + tuple(grads[1:])
    dmixed, d_mu, d_w0, d_lora, d_a0, d_gate_up, d_kk, d_ka = _rowwise(
        b_prep, [(proj, SHIFT_W), drw_post, dscan], [s['tshift_mu']] + prep_consts,
        [row(SHIFT_W), acc(SHIFT_W), acc(RW), acc(2 * RW, 128), acc(RW), acc(RW, 128), acc(RW), acc(RW)],
        tm=tm_vjp, name="b_prep", halo=[(proj, SHIFT_W, 'prev')])

    def b_gather(i, n, dm, dqkv_, dpg_, next8, mu):
        dcur = dm * (1.0 - mu) + _shift_up(dm, next8, i, n) * mu
        return (jnp.concatenate([dcur.astype(MXU), dqkv_, dpg_], axis=1),)
    (dproj,) = _rowwise(b_gather, [dmixed, dqkv, dpg], [s['tshift_mu']], [row(IN_W, MXU)], tm=tm_wide, name="b_gather",
                        halo=[(dmixed, SHIFT_W, 'next')])
    dh1 = _mm_nt(dproj, w['w_in'], name="b_proj_dx")
    g_win = _mm_tn(h1, dproj, name="b_proj_dw")

    def b_norm1(i, n, x_, dh1_, dx1_, g, ada_):
        _, vjp = jax.vjp(_norm_mod, x_, g, ada_[:, D:2 * D], ada_[:, 0:D])
        dxn, dg, dsc, dsh = vjp(dh1_)
        return dxn + dx1_, dg, dsc, dsh
    dx, d_gain1, d_scale1, d_shift1 = _rowwise(b_norm1, [x, dh1, dx1], [s['norm1_gain'], ada],
                                               [row(D), acc(D), acc(D), acc(D)], tm=tm, name="b_norm1")

    d_ada = jnp.concatenate([d_shift1, d_scale1, dgate1, d_shift2, d_scale2, dgate2], axis=1)
    gw = dict(w_in=g_win, decay_up=d_lora[:64, :RW], iclr_up=d_lora[64:, RW:], gate_up=d_gate_up,
              w_branch_a=g_wa, w_branch_b=g_wb, w_out=g_wout, ffn_w1=g_w13[:, :DFF], ffn_w3=g_w13[:, DFF:], ffn_w2=g_w2)
    gs = dict(norm1_gain=d_gain1, norm2_gain=d_gain2, tshift_mu=d_mu, decay_w0=d_w0, iclr_a0=d_a0, k_k=d_kk, k_a=d_ka,
              r_k=d_r_k, lnx_gain=d_lnx_gain, lnx_bias=d_lnx_bias, q_norm_gain=d_qg, k_norm_gain=d_kg,
              attn_sinks=d_sinks, branch_gate_b=d_bias)
    return loss, dx, d_ada, gw, gs


ANY = pl.BlockSpec(memory_space=pl.ANY)


def _place():
    x, y, c = lax.axis_index("x"), lax.axis_index("y"), lax.axis_index("c")
    return x, y, c, [(1 - x, y), (x, 1 - y), (1 - x, 1 - y)]


def _all_gather8(x_shard, *, name):
    m_per, n = x_shard.shape

    def body(x_ref, out_ref, send_sems, recv_sems, local_sem):
        x, y, c, chips = _place()
        me, sibling = (x, y, c), (x, y, 1 - c)

        def rows(px, py, pc):
            return out_ref.at[pl.ds((4 * px + 2 * py + pc) * m_per, m_per), :]

        def copy(k, block, to, src=None):
            return pltpu.make_async_remote_copy(
                src_ref=rows(*block) if src is None else src, dst_ref=rows(*block),
                send_sem=send_sems.at[k], recv_sem=recv_sems.at[k], device_id=to, device_id_type=MESH)

        mine = pltpu.make_async_copy(x_ref, rows(*me), local_sem)
        mine.start()
        first = [copy(0, me, sibling, src=x_ref)]
        first += [copy(1 + j, me, (*chip, c), src=x_ref) for j, chip in enumerate(chips)]
        for cp in first:
            cp.start()
        passed = [copy(4 + j, (*chip, c), sibling) for j, chip in enumerate(chips)]
        for j, chip in enumerate(chips):
            copy(1 + j, (*chip, c), me).wait_recv()
            passed[j].start()
        copy(0, sibling, me).wait_recv()
        for j, chip in enumerate(chips):
            copy(4 + j, (*chip, 1 - c), me).wait_recv()
        for cp in first + passed:
            cp.wait_send()
        mine.wait()

    return pl.pallas_call(
        body, name=name, out_shape=jax.ShapeDtypeStruct((8 * m_per, n), x_shard.dtype),
        in_specs=[pl.BlockSpec(memory_space=pltpu.VMEM)], out_specs=pl.BlockSpec(memory_space=pltpu.VMEM),
        scratch_shapes=[pltpu.SemaphoreType.DMA((7,)), pltpu.SemaphoreType.DMA((7,)), pltpu.SemaphoreType.DMA],
    )(x_shard)


def _gather_chips(shard, *, name):
    def body(x_ref, out_ref, send_sems, recv_sems, local_sem):
        x, y, c, chips = _place()
        s_me = 2 * x + y
        mine = pltpu.make_async_copy(x_ref, out_ref.at[s_me], local_sem)
        mine.start()

        def copy(k, s):
            return pltpu.make_async_remote_copy(
                src_ref=x_ref, dst_ref=out_ref.at[s], send_sem=send_sems.at[k], recv_sem=recv_sems.at[k],
                device_id=(*chips[k], c), device_id_type=MESH)

        sends = [copy(k, s_me) for k in range(3)]
        for cp in sends:
            cp.start()
        for k, (px, py) in enumerate(chips):
            copy(k, 2 * px + py).wait_recv()
        for cp in sends:
            cp.wait_send()
        mine.wait()

    return pl.pallas_call(
        body, name=name, out_shape=jax.ShapeDtypeStruct((4,) + shard.shape, shard.dtype),
        in_specs=[ANY], out_specs=ANY,
        scratch_shapes=[pltpu.SemaphoreType.DMA((3,)), pltpu.SemaphoreType.DMA((3,)), pltpu.SemaphoreType.DMA],
    )(shard)


def _scatter_chips(parts, *, name):
    def body(g_ref, out_ref, send_sems, recv_sems):
        x, y, c, chips = _place()

        def copy(k, s):
            return pltpu.make_async_remote_copy(
                src_ref=g_ref.at[s], dst_ref=out_ref.at[k], send_sem=send_sems.at[k], recv_sem=recv_sems.at[k],
                device_id=(*chips[k], c), device_id_type=MESH)

        sends = [copy(k, 2 * px + py) for k, (px, py) in enumerate(chips)]
        for cp in sends:
            cp.start()
        for cp in sends:
            cp.wait_recv()
        for cp in sends:
            cp.wait_send()

    return pl.pallas_call(
        body, name=name, out_shape=jax.ShapeDtypeStruct((3,) + parts.shape[1:], parts.dtype),
        in_specs=[ANY], out_specs=ANY,
        scratch_shapes=[pltpu.SemaphoreType.DMA((3,)), pltpu.SemaphoreType.DMA((3,))],
    )(parts)


def _swap_sibling(v, *, name):
    def body(v_ref, out_ref, send_sem, recv_sem):
        x, y, c, _ = _place()
        cp = pltpu.make_async_remote_copy(src_ref=v_ref, dst_ref=out_ref, send_sem=send_sem, recv_sem=recv_sem,
                                          device_id=(x, y, 1 - c), device_id_type=MESH)
        cp.start()
        cp.wait()

    return pl.pallas_call(
        body, name=name, out_shape=jax.ShapeDtypeStruct(v.shape, v.dtype), in_specs=[ANY], out_specs=ANY,
        scratch_shapes=[pltpu.SemaphoreType.DMA, pltpu.SemaphoreType.DMA],
    )(v)


def _sum_parts(parts, *, tm, name):
    def fn(i, n, *ps):
        tot = ps[0].astype(F32)
        for p in ps[1:]:
            tot = tot + p.astype(F32)
        return (tot,)
    return _rowwise(fn, list(parts), [], [(parts[0].shape[1], F32, 'row')], tm=tm, name=name)[0]


def _adamw(w, m, v, gparts, *, tm, name):
    def fn(i, n, w_, m_, v_, *gs):
        g = gs[0]
        for p in gs[1:]:
            g = g + p
        m2 = ADAM_B1 * m_ + (1.0 - ADAM_B1) * g
        v2 = ADAM_B2 * v_ + (1.0 - ADAM_B2) * jnp.square(g)
        m_hat = m2 / (1.0 - ADAM_B1 ** ADAM_STEP)
        v_hat = v2 / (1.0 - ADAM_B2 ** ADAM_STEP)
        delta = -ADAM_LR * (m_hat / (jnp.sqrt(v_hat) + ADAM_EPS) + ADAM_WD * w_)
        return g, delta, m2, v2
    nc = w.shape[1]
    return _rowwise(fn, [w, m, v] + list(gparts), [], [(nc, F32, 'row')] * 4, tm=tm, name=name)


WEIGHTS = ['ada_w', 'ada_b', 'norm1_gain', 'norm2_gain', 'w_in', 'tshift_mu', 'decay_w0', 'decay_up', 'iclr_a0',
           'iclr_up', 'gate_up', 'k_k', 'k_a', 'r_k', 'lnx_gain', 'lnx_bias', 'q_norm_gain', 'k_norm_gain', 'attn_sinks',
           'branch_gate_b', 'w_branch_a', 'w_branch_b', 'w_out', 'ffn_w1', 'ffn_w3', 'ffn_w2']
PACKED = [('w_in', 1), ('decay_up', 1), ('iclr_up', 1), ('gate_up', 1), ('w_branch_a', 1), ('w_branch_b', 1),
          ('w_out', 0), ('ffn_w1', 1), ('ffn_w3', 1), ('ffn_w2', 0)]
SMALL = ['ada_b', 'norm1_gain', 'norm2_gain', 'tshift_mu', 'decay_w0', 'iclr_a0', 'k_k', 'k_a', 'r_k', 'lnx_gain',
         'lnx_bias', 'q_norm_gain', 'k_norm_gain', 'attn_sinks', 'branch_gate_b']
PACK_COLS = 1024
SMALL_ROWS = 128


def _pack_rows(arrays, cols, rows=None):
    flat = jnp.concatenate([a.reshape(-1) for a in arrays])
    n = flat.shape[0]
    rows = -(-n // cols) if rows is None else rows
    if rows * cols != n:
        flat = jnp.concatenate([flat, jnp.zeros((rows * cols - n,), flat.dtype)])
    return flat.reshape(rows, cols)


def _unpack(flat, shapes):
    out, o = [], 0
    flat = flat.reshape(-1)
    for shp in shapes:
        n = math.prod(shp)
        out.append(flat[o:o + n].reshape(shp))
        o += n
    return out


def kernel(x, c, positions, ada_w, ada_b, norm1_gain, norm2_gain, w_in, tshift_mu, decay_w0, decay_up, iclr_a0, iclr_up, gate_up, k_k, k_a, r_k, lnx_gain, lnx_bias, q_norm_gain, k_norm_gain, attn_sinks, branch_gate_b, w_branch_a, w_branch_b, w_out, ffn_w1, ffn_w3, ffn_w2, loss_target, m_ada_w, m_ada_b, m_norm1_gain, m_norm2_gain, m_w_in, m_tshift_mu, m_decay_w0, m_decay_up, m_iclr_a0, m_iclr_up, m_gate_up, m_k_k, m_k_a, m_r_k, m_lnx_gain, m_lnx_bias, m_q_norm_gain, m_k_norm_gain, m_attn_sinks, m_branch_gate_b, m_w_branch_a, m_w_branch_b, m_w_out, m_ffn_w1, m_ffn_w3, m_ffn_w2, v_ada_w, v_ada_b, v_norm1_gain, v_norm2_gain, v_w_in, v_tshift_mu, v_decay_w0, v_decay_up, v_iclr_a0, v_iclr_up, v_gate_up, v_k_k, v_k_a, v_r_k, v_lnx_gain, v_lnx_bias, v_q_norm_gain, v_k_norm_gain, v_attn_sinks, v_branch_gate_b, v_w_branch_a, v_w_branch_b, v_w_out, v_ffn_w1, v_ffn_w3, v_ffn_w2):
    a = dict(locals())
    W = {n: a[n] for n in WEIGHTS}
    M = {n: a['m_' + n] for n in WEIGHTS}
    V = {n: a['v_' + n] for n in WEIGHTS}
    xi, yi, ci = lax.axis_index("x"), lax.axis_index("y"), lax.axis_index("c")
    me = 4 * xi + 2 * yi + ci
    shard = 2 * xi + yi
    mat = lambda t: t.reshape(t.shape[-2], t.shape[-1])
    packed_names = [n for n, _ in PACKED]
    local_shapes = [mat(W[n]).shape for n in packed_names]

    wpack = _pack_rows([mat(W[n]).astype(MXU) for n in packed_names], PACK_COLS)
    gathered = _gather_chips(wpack, name="gather_weights")
    per_shard = [_unpack(gathered[j], local_shapes) for j in range(4)]
    full = {n: jnp.concatenate([per_shard[j][i] for j in range(4)], axis=ax) for i, (n, ax) in enumerate(PACKED)}

    c_all = _all_gather8(jnp.broadcast_to(c, (8, D)), name="gather_c")[0::8]
    pad_rows = lambda t: jnp.concatenate([t, jnp.zeros((BLK - 8, t.shape[1]), t.dtype)])
    c_all = pad_rows(c_all.astype(MXU))
    ada_cols = _mm_nn(c_all, mat(ada_w).astype(MXU), name="f_ada")[:8]
    ada_all = _all_gather8(ada_cols, name="gather_ada").reshape(2, 2, 2, 8, 6 * D // 4)
    ada_mine = lax.dynamic_index_in_dim(ada_all[:, :, 0], me, axis=2, keepdims=False)
    ada = ada_mine.reshape(1, 6 * D) + mat(ada_b)

    lora = jnp.zeros((128, 2 * RW), MXU).at[:64, :RW].set(full['decay_up']).at[64:, RW:].set(full['iclr_up'])
    w = dict(w_in=full['w_in'], w_branch_a=full['w_branch_a'], w_branch_b=full['w_branch_b'], w_out=full['w_out'],
             ffn_w13=jnp.concatenate([full['ffn_w1'], full['ffn_w3']], axis=1), ffn_w2=full['ffn_w2'])
    s = {n: W[n].reshape(1, -1) for n in SMALL if n != 'ada_b'}
    s['lora_up'] = lora.astype(F32)
    s['gate_up'] = full['gate_up'].astype(F32)
    tab = _rope_table(positions.reshape(-1))
    loss, dx, d_ada, gw, gs = _local_step(x[0], loss_target[0], ada, tab, w, s)
    loss = lax.psum(loss[0, 0], ("x", "y", "c"))

    gs['ada_b'] = d_ada
    small_shapes = [W[n].shape for n in SMALL]
    gsmall = _pack_rows([gs[n] for n in SMALL], 128, SMALL_ROWS)
    gsmall_all = _all_gather8(gsmall, name="gather_small_grads")
    small_parts = [gsmall_all[d * SMALL_ROWS:(d + 1) * SMALL_ROWS] for d in range(8)]
    pk = lambda src: _pack_rows([src[n] for n in SMALL], 128, SMALL_ROWS)
    sm_out = _adamw(pk(W), pk(M), pk(V), small_parts, tm=SMALL_ROWS, name="adamw_small")
    sm_out = [dict(zip(SMALL, _unpack(o, small_shapes))) for o in sm_out]

    d_ada_all = jnp.stack([p[:6 * D // 128].reshape(6 * D) for p in small_parts])
    d_ada_cols = lax.dynamic_slice_in_dim(d_ada_all, shard * (6 * D // 4), 6 * D // 4, axis=1)
    g_ada_w = _mm_tn(c_all, pad_rows(d_ada_cols.astype(MXU)), name="b_ada")
    ada_out = _adamw(mat(ada_w), mat(m_ada_w), mat(v_ada_w), [g_ada_w], tm=256, name="adamw_ada")

    def for_chip(j):
        parts = []
        for (n, ax), shp in zip(PACKED, local_shapes, strict=True):
            parts.append(lax.slice_in_dim(gw[n], j * shp[ax], (j + 1) * shp[ax], axis=ax))
        return _pack_rows([p.astype(MXU) for p in parts], PACK_COLS)
    gparts = jnp.stack([for_chip(j) for j in range(4)])
    from_chips = _scatter_chips(gparts, name="scatter_grads")
    own = lax.dynamic_index_in_dim(gparts, shard, axis=0, keepdims=False)
    part = _sum_parts([own, from_chips[0], from_chips[1], from_chips[2]], tm=224, name="sum_grads")
    other = _swap_sibling(part, name="swap_grads")
    pkw = lambda src: _pack_rows([mat(src[n]) for n in packed_names], PACK_COLS)
    pw_out = _adamw(pkw(W), pkw(M), pkw(V), [part, other], tm=224, name="adamw_packed")
    pw_out = [dict(zip(packed_names, _unpack(o, local_shapes))) for o in pw_out]

    def leaf(k, n):
        if n == 'ada_w':
            return ada_out[k].reshape(W[n].shape)
        if n in packed_names:
            return pw_out[k][n].reshape(W[n].shape)
        return sm_out[k][n]
    outs = [leaf(k, n) for k in range(4) for n in WEIGHTS]
    return (loss, dx[None], *outs)
```

```python
import functools
import math

import jax
import jax.numpy as jnp
from jax import lax
from jax.experimental import pallas as pl
from jax.experimental.pallas import tpu as pltpu

F32 = jnp.float32
BF16 = jnp.bfloat16
MXU = BF16
HI = lax.Precision.HIGHEST

D = 1024
HD = 64
NH = 8
RW = NH * HD
SHIFT_W = 3 * RW + 64 + 64 + 128
QKV_W = RW + 2 * 128
GATE_W = 2 * D
IN_W = SHIFT_W + QKV_W + GATE_W
DFF = 2816
BLK = 128
CHUNK = 64
RMS_EPS = 1e-6
GN_EPS = 64e-5
NEG_INF = -1e30
ADAM_LR, ADAM_B1, ADAM_B2, ADAM_EPS, ADAM_WD, ADAM_STEP = 0.001, 0.9, 0.999, 1e-08, 0.01, 10
VMEM_LIMIT = 56 * 1024 * 1024
MESH = pl.DeviceIdType.MESH


def _cparams(sem=None):
    return pltpu.CompilerParams(dimension_semantics=sem, vmem_limit_bytes=VMEM_LIMIT)


def _full_spec(a):
    nd = a.ndim
    return pl.BlockSpec(a.shape, lambda *_: (0,) * nd)


def _rowwise(fn, rows, consts, outs, *, tm, name, halo=()):
    rows = [a if isinstance(a, tuple) else (a, a.shape[1]) for a in rows]
    T = rows[0][0].shape[0]
    assert T % tm == 0 and tm % 8 == 0
    n_tiles = T // tm
    n_in = len(rows) + len(halo) + len(consts)
    in_specs = [pl.BlockSpec((tm, nc), lambda i: (i, 0)) for _, nc in rows]
    args = [a for a, _ in rows]
    for a, nc, kind in halo:
        if kind == 'prev':
            in_specs.append(pl.BlockSpec((8, nc), lambda i: (jnp.maximum(i * (tm // 8) - 1, 0), 0)))
        else:
            in_specs.append(pl.BlockSpec((8, nc), lambda i: (jnp.minimum((i + 1) * (tm // 8), T // 8 - 1), 0)))
        args.append(a)
    in_specs += [_full_spec(a) for a in consts]
    args += list(consts)
    out_shape, out_specs = [], []
    for ncols, dtype, kind in outs:
        if kind == 'row':
            out_shape.append(jax.ShapeDtypeStruct((T, ncols), dtype))
            out_specs.append(pl.BlockSpec((tm, ncols), lambda i: (i, 0)))
        else:
            out_shape.append(jax.ShapeDtypeStruct((kind, ncols), dtype))
            out_specs.append(pl.BlockSpec((kind, ncols), lambda i: (0, 0)))

    def body(*refs):
        i = pl.program_id(0)
        vals = [r[...] for r in refs[:n_in]]
        res = fn(i, n_tiles, *vals)
        for (ncols, dtype, kind), o_ref, val in zip(outs, refs[n_in:], res, strict=True):
            if kind == 'row':
                o_ref[...] = val.astype(dtype)
            else:
                @pl.when(i == 0)
                def _():
                    o_ref[...] = jnp.zeros_like(o_ref)
                o_ref[...] += val.astype(dtype)

    res = pl.pallas_call(
        body, name=name, grid=(n_tiles,), in_specs=in_specs, out_specs=out_specs, out_shape=out_shape,
        compiler_params=_cparams(("arbitrary",)),
    )(*args)
    return res


def _pick(n, cands):
    for c in cands:
        if n % c == 0:
            return c
    return n


def _mm_nn(a, w, *, name, out_dtype=F32):
    T, K = a.shape
    N = w.shape[1]
    tm = _pick(T, (512, 256, 128))
    tn = _pick(N, (1024, 1408, 896, 768, 512, 256, 128))

    def body(a_ref, w_ref, o_ref):
        o_ref[...] = jnp.dot(a_ref[...], w_ref[...], preferred_element_type=F32).astype(out_dtype)

    return pl.pallas_call(
        body, name=name, grid=(N // tn, T // tm),
        in_specs=[pl.BlockSpec((tm, K), lambda j, i: (i, 0)), pl.BlockSpec((K, tn), lambda j, i: (0, j))],
        out_specs=pl.BlockSpec((tm, tn), lambda j, i: (i, j)),
        out_shape=jax.ShapeDtypeStruct((T, N), out_dtype),
        compiler_params=_cparams(("arbitrary", "arbitrary")),
    )(a, w)


def _mm_nt(dy, w, *, name, out_dtype=F32):
    T, N = dy.shape
    K = w.shape[0]
    tm = _pick(T, (512, 256, 128))
    tk = _pick(K, (1024, 1408, 896, 768, 512, 256, 128))

    def body(dy_ref, w_ref, o_ref):
        o_ref[...] = lax.dot_general(dy_ref[...], w_ref[...], (((1,), (1,)), ((), ())),
                                     preferred_element_type=F32).astype(out_dtype)

    return pl.pallas_call(
        body, name=name, grid=(K // tk, T // tm),
        in_specs=[pl.BlockSpec((tm, N), lambda j, i: (i, 0)), pl.BlockSpec((tk, N), lambda j, i: (j, 0))],
        out_specs=pl.BlockSpec((tm, tk), lambda j, i: (i, j)),
        out_shape=jax.ShapeDtypeStruct((T, K), out_dtype),
        compiler_params=_cparams(("arbitrary", "arbitrary")),
    )(dy, w)


def _mm_tn(a, dy, *, name):
    T, K = a.shape
    N = dy.shape[1]
    tm = _pick(T, (512, 256, 128))
    tn = _pick(N, (1024, 1408, 896, 768, 512, 256, 128))
    n_t = T // tm

    def body(a_ref, dy_ref, o_ref):
        i = pl.program_id(1)

        @pl.when(i == 0)
        def _():
            o_ref[...] = jnp.zeros_like(o_ref)

        o_ref[...] += lax.dot_general(a_ref[...], dy_ref[...], (((0,), (0,)), ((), ())), preferred_element_type=F32)

    return pl.pallas_call(
        body, name=name, grid=(N // tn, n_t),
        in_specs=[pl.BlockSpec((tm, K), lambda j, i: (i, 0)), pl.BlockSpec((tm, tn), lambda j, i: (i, j))],
        out_specs=pl.BlockSpec((K, tn), lambda j, i: (0, j)),
        out_shape=jax.ShapeDtypeStruct((K, N), F32),
        compiler_params=_cparams(("arbitrary", "arbitrary")),
    )(a, dy)


def _seg_ones(n):
    r = lax.broadcasted_iota(jnp.int32, (n, n), 0) // HD
    c = lax.broadcasted_iota(jnp.int32, (n, n), 1) // HD
    return (r == c).astype(F32)


def _segsum(x):
    return jnp.dot(x, _seg_ones(x.shape[1]), precision=HI, preferred_element_type=F32)


def _mxu(x):
    return x.astype(MXU)


@jax.custom_vjp
def _bdot(a, b):
    return jnp.dot(_mxu(a), _mxu(b), preferred_element_type=F32)


def _bdot_fwd(a, b):
    return _bdot(a, b), (a, b)


def _bdot_bwd(res, g):
    a, b = res
    da = lax.dot_general(_mxu(g), _mxu(b), (((1,), (1,)), ((), ())), preferred_element_type=F32)
    db = lax.dot_general(_mxu(a), _mxu(g), (((0,), (0,)), ((), ())), preferred_element_type=F32)
    return da.astype(a.dtype), db.astype(b.dtype)


_bdot.defvjp(_bdot_fwd, _bdot_bwd)


@jax.custom_vjp
def _bdot_nt(a, b):
    return lax.dot_general(_mxu(a), _mxu(b), (((1,), (1,)), ((), ())), preferred_element_type=F32)


def _bdot_nt_fwd(a, b):
    return _bdot_nt(a, b), (a, b)


def _bdot_nt_bwd(res, g):
    a, b = res
    da = jnp.dot(_mxu(g), _mxu(b), preferred_element_type=F32)
    db = lax.dot_general(_mxu(g), _mxu(a), (((0,), (0,)), ((), ())), preferred_element_type=F32)
    return da.astype(a.dtype), db.astype(b.dtype)


_bdot_nt.defvjp(_bdot_nt_fwd, _bdot_nt_bwd)


def _sigmoid(x):
    return 1.0 / (1.0 + jnp.exp(-x))


def _softplus(x):
    return jnp.maximum(x, 0.0) + jnp.log(1.0 + jnp.exp(jnp.minimum(x, -x)))


def _norm_mod(x, gain, scale, shift):
    inv = lax.rsqrt(jnp.mean(x * x, axis=-1, keepdims=True) + RMS_EPS)
    return (x * inv) * gain * (1.0 + scale) + shift


def _prep(mixed, decay_w0, lora_up, iclr_a0, gate_up, k_k, k_a):
    r = mixed[:, 0:RW]
    k = mixed[:, RW:2 * RW]
    v = mixed[:, 2 * RW:3 * RW]
    z = mixed[:, 3 * RW:3 * RW + 128]
    xg = mixed[:, 3 * RW + 128:]
    lane = lax.broadcasted_iota(jnp.int32, z.shape, 1)
    tz = jnp.where(lane < 64, jnp.tanh(z), z)
    lo = _bdot(tz, lora_up)
    w_log = -_softplus(-(decay_w0 + lo[:, :RW])) - 0.5
    lw = -jnp.exp(w_log)
    a_ic = _sigmoid(iclr_a0 + lo[:, RW:])
    g = _bdot(_sigmoid(xg), gate_up)
    kk = k * k_k
    kk = kk / jnp.maximum(jnp.sqrt(_segsum(kk * kk)), 1e-12)
    k_mod = k * (1.0 + (a_ic - 1.0) * k_a)
    return jnp.concatenate([r, lw, k_mod, v, -kk, kk * a_ic, g], axis=1)


def _post(y, r, k, v, g, lnx_gain, lnx_bias, r_k):
    mu = _segsum(y) * (1.0 / HD)
    yc = y - mu
    var = _segsum(yc * yc) * (1.0 / HD)
    yn = yc * lax.rsqrt(var + GN_EPS) * lnx_gain + lnx_bias
    bonus = _segsum(r * k * r_k) * v
    return (yn + bonus) * g


def _merge(pg, ma, mb, bias):
    gates = _sigmoid(pg + bias)
    return gates[:, :D] * ma + gates[:, D:] * mb


def _swiglu(u, v):
    return u * _sigmoid(u) * v


@functools.partial(jax.custom_vjp, nondiff_argnums=(1,))
def _lane_roll(x, s):
    return pltpu.roll(x, s, 1)


def _lane_roll_fwd(x, s):
    return pltpu.roll(x, s, 1), None


def _lane_roll_bwd(s, _, g):
    n = g.shape[1]
    return (pltpu.roll(g, (n - s) % n, 1),)


_lane_roll.defvjp(_lane_roll_fwd, _lane_roll_bwd)


def _rope(x, cos, sin_lo, sin_hi):
    n = x.shape[1]
    return x * cos + _lane_roll(x, n - 8) * sin_lo + _lane_roll(x, 8) * sin_hi


def _head_rms(x, gain):
    return x * lax.rsqrt(_segsum(x * x) * (1.0 / HD) + RMS_EPS) * gain


def _attn_block(qkv_c, qkv_p, tab_c, tab_p, qg, kg, sinks, first):
    def tabs(tab, n):
        return tab[:, 0:n], tab[:, RW:RW + n], tab[:, 2 * RW:2 * RW + n]

    qg = jnp.concatenate([qg] * NH, axis=1)
    kg = jnp.concatenate([kg] * 2, axis=1)
    q = _rope(_head_rms(qkv_c[:, :RW], qg), *tabs(tab_c, RW))
    k_c = _rope(_head_rms(qkv_c[:, RW:RW + 128], kg), *tabs(tab_c, 128))
    k_p = _rope(_head_rms(qkv_p[:, RW:RW + 128], kg), *tabs(tab_p, 128))
    kband = jnp.concatenate([k_p, k_c], axis=0)
    vband = jnp.concatenate([qkv_p[:, RW + 128:], qkv_c[:, RW + 128:]], axis=0)
    G = 4
    qi = lax.broadcasted_iota(jnp.int32, (G * BLK, 2 * BLK), 0) % BLK
    kj = lax.broadcasted_iota(jnp.int32, (G * BLK, 2 * BLK), 1)
    dist = qi + BLK - kj
    valid = (dist >= 0) & (dist < BLK) & (jnp.logical_not(first) | (kj >= BLK))
    row_g = lax.broadcasted_iota(jnp.int32, (G * BLK, 1), 0) // BLK
    outs = []
    for kvh in range(2):
        kb = kband[:, kvh * HD:(kvh + 1) * HD]
        vb = vband[:, kvh * HD:(kvh + 1) * HD]
        qs = jnp.concatenate([q[:, (G * kvh + g) * HD:(G * kvh + g + 1) * HD] for g in range(G)], axis=0)
        s = _bdot_nt(qs, kb) * (HD ** -0.5)
        s = jnp.where(valid, s, NEG_INF)
        sink = jnp.zeros((G * BLK, 1), F32)
        for g in range(G):
            sink = jnp.where(row_g == g, sinks[:, G * kvh + g:G * kvh + g + 1], sink)
        m = lax.stop_gradient(jnp.maximum(jnp.max(s, axis=-1, keepdims=True), sink))
        e = jnp.exp(s - m)
        p = e / (jnp.sum(e, axis=-1, keepdims=True) + jnp.exp(sink - m))
        o = _bdot(p, vb)
        outs += [o[g * BLK:(g + 1) * BLK] for g in range(G)]
    return jnp.concatenate(outs, axis=1)


def _heads(x):
    return jnp.stack([x[:, h * HD:(h + 1) * HD] for h in range(NH)], axis=0)


def _unheads(x):
    return jnp.concatenate([x[h] for h in range(NH)], axis=1)


def _split(x, n):
    parts, rest = [], x
    for _ in range(n):
        p = rest.astype(MXU)
        parts.append(p)
        rest = rest - p.astype(F32)
    return parts


def _bdot_batched(a, b, ca, cb):
    return lax.dot_general(a, b, (((ca,), (cb,)), ((0,), (0,))), preferred_element_type=F32)


def _bmm_passes(a, b, ca, cb, passes):
    if MXU == F32:
        return lax.dot_general(a, b, (((ca,), (cb,)), ((0,), (0,))), precision=HI, preferred_element_type=F32)
    if passes == 1:
        return _bdot_batched(a.astype(MXU), b.astype(MXU), ca, cb)
    (a0, a1), (b0, b1) = _split(a, 2), _split(b, 2)
    return _bdot_batched(a0, b0, ca, cb) + (_bdot_batched(a0, b1, ca, cb) + _bdot_batched(a1, b0, ca, cb))


@functools.partial(jax.custom_vjp, nondiff_argnums=(2, 3, 4))
def _bmm(a, b, ca, cb, passes=1):
    return _bmm_passes(a, b, ca, cb, passes)


def _bmm_fwd(a, b, ca, cb, passes):
    return _bmm_passes(a, b, ca, cb, passes), (a, b)


def _bmm_bwd(ca, cb, passes, res, g):
    a, b = res
    if (ca, cb) == (2, 1):
        return _bmm_passes(g, b, 2, 2, passes), _bmm_passes(a, g, 1, 1, passes)
    if (ca, cb) == (2, 2):
        return _bmm_passes(g, b, 2, 1, passes), _bmm_passes(g, a, 1, 1, passes)
    return _bmm_passes(b, g, 2, 2, passes), _bmm_passes(a, g, 2, 1, passes)


_bmm.defvjp(_bmm_fwd, _bmm_bwd)


def _tri_dot(x, transpose):
    C = x.shape[1]
    ri = lax.broadcasted_iota(jnp.int32, (C, C), 0)
    ci = lax.broadcasted_iota(jnp.int32, (C, C), 1)
    tri = jnp.broadcast_to(((ri <= ci) if transpose else (ri >= ci)).astype(MXU), (x.shape[0], C, C))
    if MXU == F32:
        return lax.dot_general(tri, x, (((2,), (1,)), ((0,), (0,))), precision=HI, preferred_element_type=F32)
    p0, p1, p2 = _split(x, 3)
    return _bdot_batched(tri, p0, 2, 1) + (_bdot_batched(tri, p1, 2, 1) + _bdot_batched(tri, p2, 2, 1))


@jax.custom_vjp
def _cumsum_rows(x):
    return _tri_dot(x, False)


def _cumsum_rows_fwd(x):
    return _tri_dot(x, False), None


def _cumsum_rows_bwd(_, g):
    return (_tri_dot(g, True),)


_cumsum_rows.defvjp(_cumsum_rows_fwd, _cumsum_rows_bwd)

P_SCORE = 1
P_SOLVE = 1
P_STATE = 1


def _chunk(S0, r, lw, k, v, a, b):
    C = r.shape[1]
    ri = lax.broadcasted_iota(jnp.int32, (C, C), 0)
    ci = lax.broadcasted_iota(jnp.int32, (C, C), 1)
    incl = (ri >= ci)
    strict = (ri > ci)
    eye = (ri == ci).astype(F32)
    cum = _cumsum_rows(lw)
    p_in = jnp.exp(cum)
    p_ex = jnp.exp(cum - lw)
    p_inv = jnp.exp(-cum)
    at, rt, bt, kt = a * p_ex, r * p_in, b * p_inv, k * p_inv
    sc = _bmm(jnp.concatenate([at, rt], axis=1), jnp.concatenate([bt, kt], axis=1), 2, 2, P_SCORE)
    a_ab = jnp.where(strict, sc[:, :C, :C], 0.0)
    a_ak = jnp.where(strict, sc[:, :C, C:], 0.0)
    a_rb = jnp.where(incl, sc[:, C:, :C], 0.0)
    a_rk = jnp.where(incl, sc[:, C:, C:], 0.0)
    s0 = _bmm(jnp.concatenate([at, rt], axis=1), S0, 2, 2, P_STATE)
    rhs = s0[:, :C] + _bmm(a_ak, v, 2, 1, P_SCORE)
    x = eye + a_ab
    lp = a_ab
    for _ in range(int(math.log2(C)) - 1):
        lp = _bmm(lp, lp, 2, 1, P_SOLVE)
        x = x + _bmm(x, lp, 2, 1, P_SOLVE)
    u = _bmm(x, rhs, 2, 1, P_SOLVE)
    y = s0[:, C:] + _bmm(jnp.concatenate([a_rb, a_rk], axis=2), jnp.concatenate([u, v], axis=1), 2, 1, P_SCORE)
    p_last = jnp.exp(cum[:, C - 1:C, :])
    S1 = (S0 + _bmm(jnp.concatenate([u, v], axis=1), jnp.concatenate([bt, kt], axis=1), 1, 1, P_STATE)) * p_last
    return y, S1


def _scan_fwd(rw, *, name):
    T = rw.shape[0]
    n = T // CHUNK

    def body(r_ref, lw_ref, k_ref, v_ref, a_ref, b_ref, y_ref, ck_ref, s_ref):
        @pl.when(pl.program_id(0) == 0)
        def _():
            s_ref[...] = jnp.zeros_like(s_ref)

        S0 = s_ref[...]
        ck_ref[0] = S0
        y, S1 = _chunk(S0, *[_heads(ref[...]) for ref in (r_ref, lw_ref, k_ref, v_ref, a_ref, b_ref)])
        y_ref[...] = _unheads(y)
        s_ref[...] = S1

    col = lambda j: pl.BlockSpec((CHUNK, RW), lambda i: (i, j))
    return pl.pallas_call(
        body, name=name, grid=(n,),
        in_specs=[col(j) for j in range(6)],
        out_specs=[pl.BlockSpec((CHUNK, RW), lambda i: (i, 0)), pl.BlockSpec((1, NH, HD, HD), lambda i: (i, 0, 0, 0))],
        out_shape=[jax.ShapeDtypeStruct((T, RW), F32), jax.ShapeDtypeStruct((n, NH, HD, HD), F32)],
        scratch_shapes=[pltpu.VMEM((NH, HD, HD), F32)],
        compiler_params=_cparams(("arbitrary",)),
    )(rw, rw, rw, rw, rw, rw)


def _scan_bwd(rw, ck, dy, *, name):
    T = rw.shape[0]
    n = T // CHUNK

    def body(r_ref, lw_ref, k_ref, v_ref, a_ref, b_ref, ck_ref, dy_ref, o_ref, ds_ref):
        @pl.when(pl.program_id(0) == 0)
        def _():
            ds_ref[...] = jnp.zeros_like(ds_ref)

        prim = [_heads(ref[...]) for ref in (r_ref, lw_ref, k_ref, v_ref, a_ref, b_ref)]
        _, vjp = jax.vjp(_chunk, ck_ref[0], *prim)
        grads = vjp((_heads(dy_ref[...]), ds_ref[...]))
        ds_ref[...] = grads[0]
        o_ref[...] = jnp.concatenate([_unheads(g) for g in grads[1:]], axis=1)

    col = lambda j: pl.BlockSpec((CHUNK, RW), lambda i: (n - 1 - i, j))
    return pl.pallas_call(
        body, name=name, grid=(n,),
        in_specs=[col(j) for j in range(6)] + [pl.BlockSpec((1, NH, HD, HD), lambda i: (n - 1 - i, 0, 0, 0)),
                                               pl.BlockSpec((CHUNK, RW), lambda i: (n - 1 - i, 0))],
        out_specs=pl.BlockSpec((CHUNK, 6 * RW), lambda i: (n - 1 - i, 0)),
        out_shape=jax.ShapeDtypeStruct((T, 6 * RW), F32),
        scratch_shapes=[pltpu.VMEM((NH, HD, HD), F32)],
        compiler_params=_cparams(("arbitrary",)),
    )(rw, rw, rw, rw, rw, rw, ck, dy)


def _attn_fwd(qkv, tab, qg, kg, sinks, *, name):
    T = qkv.shape[0]
    n = T // BLK

    def body(c_ref, p_ref, tc_ref, tp_ref, qg_ref, kg_ref, s_ref, o_ref):
        first = pl.program_id(0) == 0
        o_ref[...] = _attn_block(c_ref[...], p_ref[...], tc_ref[...], tp_ref[...], qg_ref[...], kg_ref[...],
                                 s_ref[...], first).astype(o_ref.dtype)

    cur = lambda w: pl.BlockSpec((BLK, w), lambda i: (i, 0))
    prev = lambda w: pl.BlockSpec((BLK, w), lambda i: (jnp.maximum(i - 1, 0), 0))
    return pl.pallas_call(
        body, name=name, grid=(n,),
        in_specs=[cur(QKV_W), prev(QKV_W), cur(3 * RW), prev(3 * RW), _full_spec(qg), _full_spec(kg), _full_spec(sinks)],
        out_specs=cur(RW), out_shape=jax.ShapeDtypeStruct((T, RW), MXU),
        compiler_params=_cparams(("arbitrary",)),
    )(qkv, qkv, tab, tab, qg, kg, sinks)


def _attn_bwd(qkv, tab, qg, kg, sinks, dy, *, name):
    T = qkv.shape[0]
    n = T // BLK

    def body(c_ref, p_ref, tc_ref, tp_ref, qg_ref, kg_ref, s_ref, dy_ref, dqkv_ref, dqg_ref, dkg_ref, ds_ref, carry_ref):
        i = pl.program_id(0)
        first = i == n - 1

        @pl.when(i == 0)
        def _():
            carry_ref[...] = jnp.zeros_like(carry_ref)
            dqg_ref[...] = jnp.zeros_like(dqg_ref)
            dkg_ref[...] = jnp.zeros_like(dkg_ref)
            ds_ref[...] = jnp.zeros_like(ds_ref)

        tc, tp = tc_ref[...], tp_ref[...]
        f = lambda c, p, qg_, kg_, sk: _attn_block(c, p, tc, tp, qg_, kg_, sk, first)
        _, vjp = jax.vjp(f, c_ref[...], p_ref[...], qg_ref[...], kg_ref[...], s_ref[...])
        dc, dp, dqg, dkg, dsk = vjp(dy_ref[...].astype(F32))
        dqkv_ref[...] = (dc + carry_ref[...]).astype(dqkv_ref.dtype)
        carry_ref[...] = dp
        dqg_ref[...] += dqg
        dkg_ref[...] += dkg
        ds_ref[...] += dsk

    cur = lambda w: pl.BlockSpec((BLK, w), lambda i: (n - 1 - i, 0))
    prev = lambda w: pl.BlockSpec((BLK, w), lambda i: (jnp.maximum(n - 2 - i, 0), 0))
    return pl.pallas_call(
        body, name=name, grid=(n,),
        in_specs=[cur(QKV_W), prev(QKV_W), cur(3 * RW), prev(3 * RW), _full_spec(qg), _full_spec(kg), _full_spec(sinks),
                  cur(RW)],
        out_specs=[cur(QKV_W), _full_spec(qg), _full_spec(kg), _full_spec(sinks)],
        out_shape=[jax.ShapeDtypeStruct((T, QKV_W), MXU), jax.ShapeDtypeStruct(qg.shape, F32),
                   jax.ShapeDtypeStruct(kg.shape, F32), jax.ShapeDtypeStruct(sinks.shape, F32)],
        scratch_shapes=[pltpu.VMEM((BLK, QKV_W), F32)],
        compiler_params=_cparams(("arbitrary",)),
    )(qkv, qkv, tab, tab, qg, kg, sinks, dy)


def _shift_down(cur, prev8, i):
    rolled = pltpu.roll(cur, 1, 0)
    first_row = jnp.where(i > 0, prev8[7:8, :], 0.0)
    row = lax.broadcasted_iota(jnp.int32, cur.shape, 0)
    return jnp.where(row == 0, first_row, rolled)


def _shift_up(cur, next8, i, n):
    tm = cur.shape[0]
    rolled = pltpu.roll(cur, tm - 1, 0)
    last_row = jnp.where(i < n - 1, next8[0:1, :], 0.0)
    row = lax.broadcasted_iota(jnp.int32, cur.shape, 0)
    return jnp.where(row == tm - 1, last_row, rolled)


def _ada_parts(ada):
    return [ada[:, j * D:(j + 1) * D] for j in range(6)]


def _rope_table(positions):
    half = HD // 8
    inv_freq = 500000.0 ** (-jnp.arange(half, dtype=F32) / half)
    ang = positions.astype(F32)[:, None] * inv_freq
    cos, sin = jnp.cos(ang), jnp.sin(ang)
    T = positions.shape[0]
    pad = HD - 2 * half
    c64 = jnp.concatenate([cos, cos, jnp.ones((T, pad), F32)], axis=1)
    lo64 = jnp.concatenate([-sin, jnp.zeros((T, HD - half), F32)], axis=1)
    hi64 = jnp.concatenate([jnp.zeros((T, half), F32), sin, jnp.zeros((T, pad), F32)], axis=1)
    return jnp.concatenate([jnp.tile(t, (1, NH)) for t in (c64, lo64, hi64)], axis=1)


def _local_step(x, target, ada, tab, w, s):
    T = x.shape[0]
    tm = _pick(T, (512, 256, 128))
    tm_wide = _pick(T, (256, 128))
    tm_vjp = _pick(T, (128,))
    row = lambda n, dt=F32: (n, dt, 'row')
    acc = lambda n, r=1: (n, F32, r)

    def f_norm1(i, n, x_, g, ada_):
        sh, sc = ada_[:, 0:D], ada_[:, D:2 * D]
        return (_norm_mod(x_, g, sc, sh),)
    (h1,) = _rowwise(f_norm1, [x], [s['norm1_gain'], ada], [row(D, MXU)], tm=tm, name="f_norm1")
    proj = _mm_nn(h1, w['w_in'], name="f_proj")
    proj_qkv = proj[:, SHIFT_W:SHIFT_W + QKV_W]
    proj_g = proj[:, SHIFT_W + QKV_W:]
    prep_consts = [s['decay_w0'], s['lora_up'], s['iclr_a0'], s['gate_up'], s['k_k'], s['k_a']]

    def f_prep(i, n, cur, prev8, mu, *params):
        mixed = cur + (_shift_down(cur, prev8, i) - cur) * mu
        return (_prep(mixed, *params),)
    (rw,) = _rowwise(f_prep, [(proj, SHIFT_W)], [s['tshift_mu']] + prep_consts, [row(7 * RW)], tm=tm_wide,
                     name="f_prep", halo=[(proj, SHIFT_W, 'prev')])
    y, ck = _scan_fwd(rw, name="f_scan")
    post_consts = [s['lnx_gain'], s['lnx_bias'], s['r_k']]

    def post_of(y_, rw_, *params):
        return _post(y_, rw_[:, 0:RW], rw_[:, 2 * RW:3 * RW], rw_[:, 3 * RW:4 * RW], rw_[:, 6 * RW:7 * RW], *params)

    def f_post(i, n, y_, rw_, *params):
        return (post_of(y_, rw_, *params),)
    (ya,) = _rowwise(f_post, [y, rw], post_consts, [row(RW, MXU)], tm=tm_wide, name="f_post")
    yb = _attn_fwd(proj_qkv, tab, s['q_norm_gain'], s['k_norm_gain'], s['attn_sinks'], name="f_attn")
    ma = _mm_nn(ya, w['w_branch_a'], name="f_branch_a")
    mb = _mm_nn(yb, w['w_branch_b'], name="f_branch_b")

    def f_merge(i, n, pg, ma_, mb_, bias):
        return (_merge(pg, ma_, mb_, bias),)
    (merged,) = _rowwise(f_merge, [proj_g, ma, mb], [s['branch_gate_b']], [row(D, MXU)], tm=tm, name="f_merge")
    mo = _mm_nn(merged, w['w_out'], name="f_out")

    def f_res1(i, n, x_, mo_, g, ada_):
        x1_ = x_ + ada_[:, 2 * D:3 * D] * mo_
        return x1_, _norm_mod(x1_, g, ada_[:, 4 * D:5 * D], ada_[:, 3 * D:4 * D])
    x1, h2 = _rowwise(f_res1, [x, mo], [s['norm2_gain'], ada], [row(D), row(D, MXU)], tm=tm, name="f_res1")
    uv = _mm_nn(h2, w['ffn_w13'], name="f_ffn_in")

    def f_act(i, n, uv_):
        return (_swiglu(uv_[:, :DFF], uv_[:, DFF:]),)
    (act,) = _rowwise(f_act, [uv], [], [row(DFF, MXU)], tm=tm_wide, name="f_act")
    ff = _mm_nn(act, w['ffn_w2'], name="f_ffn_out")

    def f_loss(i, n, x1_, ff_, tgt, ada_):
        g2 = ada_[:, 5 * D:6 * D]
        err = x1_ + g2 * ff_ - tgt
        dx2 = err * (1.0 / D)
        loss = 0.5 * jnp.sum(jnp.sum(err * err, axis=1, keepdims=True) * (1.0 / D), axis=0, keepdims=True)
        return dx2, (dx2 * g2), jnp.broadcast_to(loss, (1, 128)), jnp.sum(dx2 * ff_, axis=0, keepdims=True)
    dx2, dff, loss, dgate2 = _rowwise(f_loss, [x1, ff, target], [ada], [row(D), row(D, MXU), acc(128), acc(D)],
                                      tm=tm, name="f_loss")

    dact = _mm_nt(dff, w['ffn_w2'], name="b_ffn_out_dx")
    g_w2 = _mm_tn(act, dff, name="b_ffn_out_dw")

    def b_act(i, n, uv_, dact_):
        _, vjp = jax.vjp(_swiglu, uv_[:, :DFF], uv_[:, DFF:])
        du, dv = vjp(dact_)
        return (jnp.concatenate([du, dv], axis=1),)
    (duv,) = _rowwise(b_act, [uv, dact], [], [row(2 * DFF, MXU)], tm=tm_wide, name="b_act")
    dh2 = _mm_nt(duv, w['ffn_w13'], name="b_ffn_in_dx")
    g_w13 = _mm_tn(h2, duv, name="b_ffn_in_dw")

    def b_res1(i, n, x1_, dh2_, dx2_, mo_, g, ada_):
        _, vjp = jax.vjp(_norm_mod, x1_, g, ada_[:, 4 * D:5 * D], ada_[:, 3 * D:4 * D])
        dxn, dg, dsc, dsh = vjp(dh2_)
        dx1_ = dxn + dx2_
        g1 = ada_[:, 2 * D:3 * D]
        return dx1_, dx1_ * g1, dg, dsc, dsh, jnp.sum(dx1_ * mo_, axis=0, keepdims=True)
    dx1, dmo, d_gain2, d_scale2, d_shift2, dgate1 = _rowwise(
        b_res1, [x1, dh2, dx2, mo], [s['norm2_gain'], ada], [row(D), row(D, MXU), acc(D), acc(D), acc(D), acc(D)],
        tm=tm, name="b_res1")
    dmerged = _mm_nt(dmo, w['w_out'], name="b_out_dx")
    g_wout = _mm_tn(merged, dmo, name="b_out_dw")

    def b_merge(i, n, pg, ma_, mb_, dm, bias):
        _, vjp = jax.vjp(_merge, pg, ma_, mb_, bias)
        dpg, dma_, dmb_, dbias = vjp(dm)
        return dpg, dma_, dmb_, dbias
    dpg, dma, dmb, d_bias = _rowwise(b_merge, [proj_g, ma, mb, dmerged], [s['branch_gate_b']],
                                     [row(GATE_W, MXU), row(D, MXU), row(D, MXU), acc(GATE_W)], tm=tm_wide, name="b_merge")
    dya = _mm_nt(dma, w['w_branch_a'], name="b_branch_a_dx")
    g_wa = _mm_tn(ya, dma, name="b_branch_a_dw")
    dyb = _mm_nt(dmb, w['w_branch_b'], name="b_branch_b_dx", out_dtype=F32)
    g_wb = _mm_tn(yb, dmb, name="b_branch_b_dw")
    dqkv, d_qg, d_kg, d_sinks = _attn_bwd(proj_qkv, tab, s['q_norm_gain'], s['k_norm_gain'], s['attn_sinks'], dyb,
                                          name="b_attn")

    def b_post(i, n, y_, rw_, dya_, *params):
        _, vjp = jax.vjp(post_of, y_, rw_, *params)
        return vjp(dya_)
    dy, drw_post, d_lnx_gain, d_lnx_bias, d_r_k = _rowwise(
        b_post, [y, rw, dya], post_consts, [row(RW), row(7 * RW), acc(RW), acc(RW), acc(RW)], tm=tm_vjp, name="b_post")
    dscan = _scan_bwd(rw, ck, dy, name="b_scan")

    def b_prep(i, n, cur, drw_, dscan_, prev8, mu, *params):
        shifted = _shift_down(cur, prev8, i)
        mixed = cur + (shifted - cur) * mu
        _, vjp = jax.vjp(_prep, mixed, *params)
        ct = drw_ + jnp.concatenate([dscan_, jnp.zeros((dscan_.shape[0], RW), F32)], axis=1)
        grads = vjp(ct)
        dmixed = grads[0]
        return (dmixed, jnp.sum(dmixed * (shifted - cur), axis=0, keepdims=True)) + tuple(grads[1:])
    dmixed, d_mu, d_w0, d_lora, d_a0, d_gate_up, d_kk, d_ka = _rowwise(
        b_prep, [(proj, SHIFT_W), drw_post, dscan], [s['tshift_mu']] + prep_consts,
        [row(SHIFT_W), acc(SHIFT_W), acc(RW), acc(2 * RW, 128), acc(RW), acc(RW, 128), acc(RW), acc(RW)],
        tm=tm_vjp, name="b_prep", halo=[(proj, SHIFT_W, 'prev')])

    def b_gather(i, n, dm, dqkv_, dpg_, next8, mu):
        dcur = dm * (1.0 - mu) + _shift_up(dm, next8, i, n) * mu
        return (jnp.concatenate([dcur.astype(MXU), dqkv_, dpg_], axis=1),)
    (dproj,) = _rowwise(b_gather, [dmixed, dqkv, dpg], [s['tshift_mu']], [row(IN_W, MXU)], tm=tm_wide, name="b_gather",
                        halo=[(dmixed, SHIFT_W, 'next')])
    dh1 = _mm_nt(dproj, w['w_in'], name="b_proj_dx")
    g_win = _mm_tn(h1, dproj, name="b_proj_dw")

    def b_norm1(i, n, x_, dh1_, dx1_, g, ada_):
        _, vjp = jax.vjp(_norm_mod, x_, g, ada_[:, D:2 * D], ada_[:, 0:D])
        dxn, dg, dsc, dsh = vjp(dh1_)
        return dxn + dx1_, dg, dsc, dsh
    dx, d_gain1, d_scale1, d_shift1 = _rowwise(b_norm1, [x, dh1, dx1], [s['norm1_gain'], ada],
                                               [row(D), acc(D), acc(D), acc(D)], tm=tm, name="b_norm1")

    d_ada = jnp.concatenate([d_shift1, d_scale1, dgate1, d_shift2, d_scale2, dgate2], axis=1)
    gw = dict(w_in=g_win, decay_up=d_lora[:64, :RW], iclr_up=d_lora[64:, RW:], gate_up=d_gate_up,
              w_branch_a=g_wa, w_branch_b=g_wb, w_out=g_wout, ffn_w1=g_w13[:, :DFF], ffn_w3=g_w13[:, DFF:], ffn_w2=g_w2)
    gs = dict(norm1_gain=d_gain1, norm2_gain=d_gain2, tshift_mu=d_mu, decay_w0=d_w0, iclr_a0=d_a0, k_k=d_kk, k_a=d_ka,
              r_k=d_r_k, lnx_gain=d_lnx_gain, lnx_bias=d_lnx_bias, q_norm_gain=d_qg, k_norm_gain=d_kg,
              attn_sinks=d_sinks, branch_gate_b=d_bias)
    return loss, dx, d_ada, gw, gs


ANY = pl.BlockSpec(memory_space=pl.ANY)


def _place():
    x, y, c = lax.axis_index("x"), lax.axis_index("y"), lax.axis_index("c")
    return x, y, c, [(1 - x, y), (x, 1 - y), (1 - x, 1 - y)]


def _all_gather8(x_shard, *, name):
    m_per, n = x_shard.shape

    def body(x_ref, out_ref, send_sems, recv_sems, local_sem):
        x, y, c, chips = _place()
        me, sibling = (x, y, c), (x, y, 1 - c)

        def rows(px, py, pc):
            return out_ref.at[pl.ds((4 * px + 2 * py + pc) * m_per, m_per), :]

        def copy(k, block, to, src=None):
            return pltpu.make_async_remote_copy(
                src_ref=rows(*block) if src is None else src, dst_ref=rows(*block),
                send_sem=send_sems.at[k], recv_sem=recv_sems.at[k], device_id=to, device_id_type=MESH)

        mine = pltpu.make_async_copy(x_ref, rows(*me), local_sem)
        mine.start()
        first = [copy(0, me, sibling, src=x_ref)]
        first += [copy(1 + j, me, (*chip, c), src=x_ref) for j, chip in enumerate(chips)]
        for cp in first:
            cp.start()
        passed = [copy(4 + j, (*chip, c), sibling) for j, chip in enumerate(chips)]
        for j, chip in enumerate(chips):
            copy(1 + j, (*chip, c), me).wait_recv()
            passed[j].start()
        copy(0, sibling, me).wait_recv()
        for j, chip in enumerate(chips):
            copy(4 + j, (*chip, 1 - c), me).wait_recv()
        for cp in first + passed:
            cp.wait_send()
        mine.wait()

    return pl.pallas_call(
        body, name=name, out_shape=jax.ShapeDtypeStruct((8 * m_per, n), x_shard.dtype),
        in_specs=[pl.BlockSpec(memory_space=pltpu.VMEM)], out_specs=pl.BlockSpec(memory_space=pltpu.VMEM),
        scratch_shapes=[pltpu.SemaphoreType.DMA((7,)), pltpu.SemaphoreType.DMA((7,)), pltpu.SemaphoreType.DMA],
    )(x_shard)


def _gather_chips(shard, *, name):
    def body(x_ref, out_ref, send_sems, recv_sems, local_sem):
        x, y, c, chips = _place()
        s_me = 2 * x + y
        mine = pltpu.make_async_copy(x_ref, out_ref.at[s_me], local_sem)
        mine.start()

        def copy(k, s):
            return pltpu.make_async_remote_copy(
                src_ref=x_ref, dst_ref=out_ref.at[s], send_sem=send_sems.at[k], recv_sem=recv_sems.at[k],
                device_id=(*chips[k], c), device_id_type=MESH)

        sends = [copy(k, s_me) for k in range(3)]
        for cp in sends:
            cp.start()
        for k, (px, py) in enumerate(chips):
            copy(k, 2 * px + py).wait_recv()
        for cp in sends:
            cp.wait_send()
        mine.wait()

    return pl.pallas_call(
        body, name=name, out_shape=jax.ShapeDtypeStruct((4,) + shard.shape, shard.dtype),
        in_specs=[ANY], out_specs=ANY,
        scratch_shapes=[pltpu.SemaphoreType.DMA((3,)), pltpu.SemaphoreType.DMA((3,)), pltpu.SemaphoreType.DMA],
    )(shard)


def _scatter_chips(parts, *, name):
    def body(g_ref, out_ref, send_sems, recv_sems):
        x, y, c, chips = _place()

        def copy(k, s):
            return pltpu.make_async_remote_copy(
                src_ref=g_ref.at[s], dst_ref=out_ref.at[k], send_sem=send_sems.at[k], recv_sem=recv_sems.at[k],
                device_id=(*chips[k], c), device_id_type=MESH)

        sends = [copy(k, 2 * px + py) for k, (px, py) in enumerate(chips)]
        for cp in sends:
            cp.start()
        for cp in sends:
            cp.wait_recv()
        for cp in sends:
            cp.wait_send()

    return pl.pallas_call(
        body, name=name, out_shape=jax.ShapeDtypeStruct((3,) + parts.shape[1:], parts.dtype),
        in_specs=[ANY], out_specs=ANY,
        scratch_shapes=[pltpu.SemaphoreType.DMA((3,)), pltpu.SemaphoreType.DMA((3,))],
    )(parts)


def _swap_sibling(v, *, name):
    def body(v_ref, out_ref, send_sem, recv_sem):
        x, y, c, _ = _place()
        cp = pltpu.make_async_remote_copy(src_ref=v_ref, dst_ref=out_ref, send_sem=send_sem, recv_sem=recv_sem,
                                          device_id=(x, y, 1 - c), device_id_type=MESH)
        cp.start()
        cp.wait()

    return pl.pallas_call(
        body, name=name, out_shape=jax.ShapeDtypeStruct(v.shape, v.dtype), in_specs=[ANY], out_specs=ANY,
        scratch_shapes=[pltpu.SemaphoreType.DMA, pltpu.SemaphoreType.DMA],
    )(v)


def _sum_parts(parts, *, tm, name):
    def fn(i, n, *ps):
        tot = ps[0].astype(F32)
        for p in ps[1:]:
            tot = tot + p.astype(F32)
        return (tot,)
    return _rowwise(fn, list(parts), [], [(parts[0].shape[1], F32, 'row')], tm=tm, name=name)[0]


def _adamw(w, m, v, gparts, *, tm, name):
    def fn(i, n, w_, m_, v_, *gs):
        g = gs[0]
        for p in gs[1:]:
            g = g + p
        m2 = ADAM_B1 * m_ + (1.0 - ADAM_B1) * g
        v2 = ADAM_B2 * v_ + (1.0 - ADAM_B2) * jnp.square(g)
        m_hat = m2 / (1.0 - ADAM_B1 ** ADAM_STEP)
        v_hat = v2 / (1.0 - ADAM_B2 ** ADAM_STEP)
        delta = -ADAM_LR * (m_hat / (jnp.sqrt(v_hat) + ADAM_EPS) + ADAM_WD * w_)
        return g, delta, m2, v2
    nc = w.shape[1]
    return _rowwise(fn, [w, m, v] + list(gparts), [], [(nc, F32, 'row')] * 4, tm=tm, name=name)


WEIGHTS = ['ada_w', 'ada_b', 'norm1_gain', 'norm2_gain', 'w_in', 'tshift_mu', 'decay_w0', 'decay_up', 'iclr_a0',
           'iclr_up', 'gate_up', 'k_k', 'k_a', 'r_k', 'lnx_gain', 'lnx_bias', 'q_norm_gain', 'k_norm_gain', 'attn_sinks',
           'branch_gate_b', 'w_branch_a', 'w_branch_b', 'w_out', 'ffn_w1', 'ffn_w3', 'ffn_w2']
PACKED = [('w_in', 1), ('decay_up', 1), ('iclr_up', 1), ('gate_up', 1), ('w_branch_a', 1), ('w_branch_b', 1),
          ('w_out', 0), ('ffn_w1', 1), ('ffn_w3', 1), ('ffn_w2', 0)]
SMALL = ['ada_b', 'norm1_gain', 'norm2_gain', 'tshift_mu', 'decay_w0', 'iclr_a0', 'k_k', 'k_a', 'r_k', 'lnx_gain',
         'lnx_bias', 'q_norm_gain', 'k_norm_gain', 'attn_sinks', 'branch_gate_b']
PACK_COLS = 1024
SMALL_ROWS = 128


def _pack_rows(arrays, cols, rows=None):
    flat = jnp.concatenate([a.reshape(-1) for a in arrays])
    n = flat.shape[0]
    rows = -(-n // cols) if rows is None else rows
    if rows * cols != n:
        flat = jnp.concatenate([flat, jnp.zeros((rows * cols - n,), flat.dtype)])
    return flat.reshape(rows, cols)


def _unpack(flat, shapes):
    out, o = [], 0
    flat = flat.reshape(-1)
    for shp in shapes:
        n = math.prod(shp)
        out.append(flat[o:o + n].reshape(shp))
        o += n
    return out


def kernel(x, c, positions, ada_w, ada_b, norm1_gain, norm2_gain, w_in, tshift_mu, decay_w0, decay_up, iclr_a0, iclr_up, gate_up, k_k, k_a, r_k, lnx_gain, lnx_bias, q_norm_gain, k_norm_gain, attn_sinks, branch_gate_b, w_branch_a, w_branch_b, w_out, ffn_w1, ffn_w3, ffn_w2, loss_target, m_ada_w, m_ada_b, m_norm1_gain, m_norm2_gain, m_w_in, m_tshift_mu, m_decay_w0, m_decay_up, m_iclr_a0, m_iclr_up, m_gate_up, m_k_k, m_k_a, m_r_k, m_lnx_gain, m_lnx_bias, m_q_norm_gain, m_k_norm_gain, m_attn_sinks, m_branch_gate_b, m_w_branch_a, m_w_branch_b, m_w_out, m_ffn_w1, m_ffn_w3, m_ffn_w2, v_ada_w, v_ada_b, v_norm1_gain, v_norm2_gain, v_w_in, v_tshift_mu, v_decay_w0, v_decay_up, v_iclr_a0, v_iclr_up, v_gate_up, v_k_k, v_k_a, v_r_k, v_lnx_gain, v_lnx_bias, v_q_norm_gain, v_k_norm_gain, v_attn_sinks, v_branch_gate_b, v_w_branch_a, v_w_branch_b, v_w_out, v_ffn_w1, v_ffn_w3, v_ffn_w2):
    a = dict(locals())
    W = {n: a[n] for n in WEIGHTS}
    M = {n: a['m_' + n] for n in WEIGHTS}
    V = {n: a['v_' + n] for n in WEIGHTS}
    xi, yi, ci = lax.axis_index("x"), lax.axis_index("y"), lax.axis_index("c")
    me = 4 * xi + 2 * yi + ci
    shard = 2 * xi + yi
    mat = lambda t: t.reshape(t.shape[-2], t.shape[-1])
    packed_names = [n for n, _ in PACKED]
    local_shapes = [mat(W[n]).shape for n in packed_names]

    wpack = _pack_rows([mat(W[n]).astype(MXU) for n in packed_names], PACK_COLS)
    gathered = _gather_chips(wpack, name="gather_weights")
    per_shard = [_unpack(gathered[j], local_shapes) for j in range(4)]
    full = {n: jnp.concatenate([per_shard[j][i] for j in range(4)], axis=ax) for i, (n, ax) in enumerate(PACKED)}

    c_all = _all_gather8(jnp.broadcast_to(c, (8, D)), name="gather_c")[0::8]
    pad_rows = lambda t: jnp.concatenate([t, jnp.zeros((BLK - 8, t.shape[1]), t.dtype)])
    c_all = pad_rows(c_all.astype(MXU))
    ada_cols = _mm_nn(c_all, mat(ada_w).astype(MXU), name="f_ada")[:8]
    ada_all = _all_gather8(ada_cols, name="gather_ada").reshape(2, 2, 2, 8, 6 * D // 4)
    ada_mine = lax.dynamic_index_in_dim(ada_all[:, :, 0], me, axis=2, keepdims=False)
    ada = ada_mine.reshape(1, 6 * D) + mat(ada_b)

    lora = jnp.zeros((128, 2 * RW), MXU).at[:64, :RW].set(full['decay_up']).at[64:, RW:].set(full['iclr_up'])
    w = dict(w_in=full['w_in'], w_branch_a=full['w_branch_a'], w_branch_b=full['w_branch_b'], w_out=full['w_out'],
             ffn_w13=jnp.concatenate([full['ffn_w1'], full['ffn_w3']], axis=1), ffn_w2=full['ffn_w2'])
    s = {n: W[n].reshape(1, -1) for n in SMALL if n != 'ada_b'}
    s['lora_up'] = lora.astype(F32)
    s['gate_up'] = full['gate_up'].astype(F32)
    tab = _rope_table(positions.reshape(-1))
    loss, dx, d_ada, gw, gs = _local_step(x[0], loss_target[0], ada, tab, w, s)
    loss = lax.psum(loss[0, 0], ("x", "y", "c"))

    gs['ada_b'] = d_ada
    small_shapes = [W[n].shape for n in SMALL]
    gsmall = _pack_rows([gs[n] for n in SMALL], 128, SMALL_ROWS)
    gsmall_all = _all_gather8(gsmall, name="gather_small_grads")
    small_parts = [gsmall_all[d * SMALL_ROWS:(d + 1) * SMALL_ROWS] for d in range(8)]
    pk = lambda src: _pack_rows([src[n] for n in SMALL], 128, SMALL_ROWS)
    sm_out = _adamw(pk(W), pk(M), pk(V), small_parts, tm=SMALL_ROWS, name="adamw_small")
    sm_out = [dict(zip(SMALL, _unpack(o, small_shapes))) for o in sm_out]

    d_ada_all = jnp.stack([p[:6 * D // 128].reshape(6 * D) for p in small_parts])
    d_ada_cols = lax.dynamic_slice_in_dim(d_ada_all, shard * (6 * D // 4), 6 * D // 4, axis=1)
    g_ada_w = _mm_tn(c_all, pad_rows(d_ada_cols.astype(MXU)), name="b_ada")
    ada_out = _adamw(mat(ada_w), mat(m_ada_w), mat(v_ada_w), [g_ada_w], tm=256, name="adamw_ada")

    def for_chip(j):
        parts = []
        for (n, ax), shp in zip(PACKED, local_shapes, strict=True):
            parts.append(lax.slice_in_dim(gw[n], j * shp[ax], (j + 1) * shp[ax], axis=ax))
        return _pack_rows([p.astype(MXU) for p in parts], PACK_COLS)
    gparts = jnp.stack([for_chip(j) for j in range(4)])
    from_chips = _scatter_chips(gparts, name="scatter_grads")
    own = lax.dynamic_index_in_dim(gparts, shard, axis=0, keepdims=False)
    part = _sum_parts([own, from_chips[0], from_chips[1], from_chips[2]], tm=224, name="sum_grads")
    other = _swap_sibling(part, name="swap_grads")
    pkw = lambda src: _pack_rows([mat(src[n]) for n in packed_names], PACK_COLS)
    pw_out = _adamw(pkw(W), pkw(M), pkw(V), [part, other], tm=224, name="adamw_packed")
    pw_out = [dict(zip(packed_names, _unpack(o, local_shapes))) for o in pw_out]

    def leaf(k, n):
        if n == 'ada_w':
            return ada_out[k].reshape(W[n].shape)
        if n in packed_names:
            return pw_out[k][n].reshape(W[n].shape)
        return sm_out[k][n]
    outs = [leaf(k, n) for k in range(4) for n in WEIGHTS]
    return (loss, dx[None], *outs)
```

```python
import functools
import math

import jax
import jax.numpy as jnp
from jax import lax
from jax.experimental import pallas as pl
from jax.experimental.pallas import tpu as pltpu

F32 = jnp.float32
BF16 = jnp.bfloat16
MXU = BF16
HI = lax.Precision.HIGHEST

D = 1024
HD = 64
NH = 8
RW = NH * HD
SHIFT_W = 3 * RW + 64 + 64 + 128
QKV_W = RW + 2 * 128
GATE_W = 2 * D
IN_W = SHIFT_W + QKV_W + GATE_W
DFF = 2816
BLK = 128
CHUNK = 64
RMS_EPS = 1e-6
GN_EPS = 64e-5
NEG_INF = -1e30
ADAM_LR, ADAM_B1, ADAM_B2, ADAM_EPS, ADAM_WD, ADAM_STEP = 0.001, 0.9, 0.999, 1e-08, 0.01, 10
VMEM_LIMIT = 56 * 1024 * 1024
MESH = pl.DeviceIdType.MESH


def _cparams(sem=None):
    return pltpu.CompilerParams(dimension_semantics=sem, vmem_limit_bytes=VMEM_LIMIT)


def _full_spec(a):
    nd = a.ndim
    return pl.BlockSpec(a.shape, lambda *_: (0,) * nd)


def _rowwise(fn, rows, consts, outs, *, tm, name, halo=()):
    rows = [a if isinstance(a, tuple) else (a, a.shape[1]) for a in rows]
    T = rows[0][0].shape[0]
    assert T % tm == 0 and tm % 8 == 0
    n_tiles = T // tm
    n_in = len(rows) + len(halo) + len(consts)
    in_specs = [pl.BlockSpec((tm, nc), lambda i: (i, 0)) for _, nc in rows]
    args = [a for a, _ in rows]
    for a, nc, kind in halo:
        if kind == 'prev':
            in_specs.append(pl.BlockSpec((8, nc), lambda i: (jnp.maximum(i * (tm // 8) - 1, 0), 0)))
        else:
            in_specs.append(pl.BlockSpec((8, nc), lambda i: (jnp.minimum((i + 1) * (tm // 8), T // 8 - 1), 0)))
        args.append(a)
    in_specs += [_full_spec(a) for a in consts]
    args += list(consts)
    out_shape, out_specs = [], []
    for ncols, dtype, kind in outs:
        if kind == 'row':
            out_shape.append(jax.ShapeDtypeStruct((T, ncols), dtype))
            out_specs.append(pl.BlockSpec((tm, ncols), lambda i: (i, 0)))
        else:
            out_shape.append(jax.ShapeDtypeStruct((kind, ncols), dtype))
            out_specs.append(pl.BlockSpec((kind, ncols), lambda i: (0, 0)))

    def body(*refs):
        i = pl.program_id(0)
        vals = [r[...] for r in refs[:n_in]]
        res = fn(i, n_tiles, *vals)
        for (ncols, dtype, kind), o_ref, val in zip(outs, refs[n_in:], res, strict=True):
            if kind == 'row':
                o_ref[...] = val.astype(dtype)
            else:
                @pl.when(i == 0)
                def _():
                    o_ref[...] = jnp.zeros_like(o_ref)
                o_ref[...] += val.astype(dtype)

    res = pl.pallas_call(
        body, name=name, grid=(n_tiles,), in_specs=in_specs, out_specs=out_specs, out_shape=out_shape,
        compiler_params=_cparams(("arbitrary",)),
    )(*args)
    return res


def _pick(n, cands):
    for c in cands:
        if n % c == 0:
            return c
    return n


def _mm_nn(a, w, *, name, out_dtype=F32):
    T, K = a.shape
    N = w.shape[1]
    tm = _pick(T, (512, 256, 128))
    tn = _pick(N, (1024, 1408, 896, 768, 512, 256, 128))

    def body(a_ref, w_ref, o_ref):
        o_ref[...] = jnp.dot(a_ref[...], w_ref[...], preferred_element_type=F32).astype(out_dtype)

    return pl.pallas_call(
        body, name=name, grid=(N // tn, T // tm),
        in_specs=[pl.BlockSpec((tm, K), lambda j, i: (i, 0)), pl.BlockSpec((K, tn), lambda j, i: (0, j))],
        out_specs=pl.BlockSpec((tm, tn), lambda j, i: (i, j)),
        out_shape=jax.ShapeDtypeStruct((T, N), out_dtype),
        compiler_params=_cparams(("arbitrary", "arbitrary")),
    )(a, w)


def _mm_nt(dy, w, *, name, out_dtype=F32):
    T, N = dy.shape
    K = w.shape[0]
    tm = _pick(T, (512, 256, 128))
    tk = _pick(K, (1024, 1408, 896, 768, 512, 256, 128))

    def body(dy_ref, w_ref, o_ref):
        o_ref[...] = lax.dot_general(dy_ref[...], w_ref[...], (((1,), (1,)), ((), ())),
                                     preferred_element_type=F32).astype(out_dtype)

    return pl.pallas_call(
        body, name=name, grid=(K // tk, T // tm),
        in_specs=[pl.BlockSpec((tm, N), lambda j, i: (i, 0)), pl.BlockSpec((tk, N), lambda j, i: (j, 0))],
        out_specs=pl.BlockSpec((tm, tk), lambda j, i: (i, j)),
        out_shape=jax.ShapeDtypeStruct((T, K), out_dtype),
        compiler_params=_cparams(("arbitrary", "arbitrary")),
    )(dy, w)


def _mm_tn(a, dy, *, name, out_dtype=F32, col_shards=None):
    T, K = a.shape
    N = dy.shape[1]
    tm = _pick(T, (512, 256, 128))
    tn = N // col_shards if col_shards else _pick(N, (1024, 1408, 896, 768, 512, 256, 128))
    n_t = T // tm

    def body(a_ref, dy_ref, o_ref, acc_ref):
        i = pl.program_id(1)

        @pl.when(i == 0)
        def _():
            acc_ref[...] = jnp.zeros_like(acc_ref)

        acc_ref[...] += lax.dot_general(a_ref[...], dy_ref[...], (((0,), (0,)), ((), ())), preferred_element_type=F32)

        @pl.when(i == n_t - 1)
        def _():
            o_ref[...] = acc_ref[...].astype(out_dtype)

    if col_shards:
        out_specs = pl.BlockSpec((None, K, tn), lambda j, i: (j, 0, 0))
        out_shape = jax.ShapeDtypeStruct((col_shards, K, tn), out_dtype)
    else:
        out_specs = pl.BlockSpec((K, tn), lambda j, i: (0, j))
        out_shape = jax.ShapeDtypeStruct((K, N), out_dtype)
    return pl.pallas_call(
        body, name=name, grid=(N // tn, n_t),
        in_specs=[pl.BlockSpec((tm, K), lambda j, i: (i, 0)), pl.BlockSpec((tm, tn), lambda j, i: (i, j))],
        out_specs=out_specs, out_shape=out_shape, scratch_shapes=[pltpu.VMEM((K, tn), F32)],
        compiler_params=_cparams(("arbitrary", "arbitrary")),
    )(a, dy)


def _seg_ones(n):
    r = lax.broadcasted_iota(jnp.int32, (n, n), 0) // HD
    c = lax.broadcasted_iota(jnp.int32, (n, n), 1) // HD
    return (r == c).astype(F32)


def _segsum(x):
    return jnp.dot(x, _seg_ones(x.shape[1]), precision=HI, preferred_element_type=F32)


def _mxu(x):
    return x.astype(MXU)


@jax.custom_vjp
def _bdot(a, b):
    return jnp.dot(_mxu(a), _mxu(b), preferred_element_type=F32)


def _bdot_fwd(a, b):
    return _bdot(a, b), (a, b)


def _bdot_bwd(res, g):
    a, b = res
    da = lax.dot_general(_mxu(g), _mxu(b), (((1,), (1,)), ((), ())), preferred_element_type=F32)
    db = lax.dot_general(_mxu(a), _mxu(g), (((0,), (0,)), ((), ())), preferred_element_type=F32)
    return da.astype(a.dtype), db.astype(b.dtype)


_bdot.defvjp(_bdot_fwd, _bdot_bwd)


@jax.custom_vjp
def _bdot_nt(a, b):
    return lax.dot_general(_mxu(a), _mxu(b), (((1,), (1,)), ((), ())), preferred_element_type=F32)


def _bdot_nt_fwd(a, b):
    return _bdot_nt(a, b), (a, b)


def _bdot_nt_bwd(res, g):
    a, b = res
    da = jnp.dot(_mxu(g), _mxu(b), preferred_element_type=F32)
    db = lax.dot_general(_mxu(g), _mxu(a), (((0,), (0,)), ((), ())), preferred_element_type=F32)
    return da.astype(a.dtype), db.astype(b.dtype)


_bdot_nt.defvjp(_bdot_nt_fwd, _bdot_nt_bwd)


def _sigmoid(x):
    return 1.0 / (1.0 + jnp.exp(-x))


def _softplus(x):
    return jnp.maximum(x, 0.0) + jnp.log(1.0 + jnp.exp(jnp.minimum(x, -x)))


def _norm_mod(x, gain, scale, shift):
    inv = lax.rsqrt(jnp.mean(x * x, axis=-1, keepdims=True) + RMS_EPS)
    return (x * inv) * gain * (1.0 + scale) + shift


def _prep(mixed, decay_w0, lora_up, iclr_a0, gate_up, k_k, k_a):
    r = mixed[:, 0:RW]
    k = mixed[:, RW:2 * RW]
    v = mixed[:, 2 * RW:3 * RW]
    z = mixed[:, 3 * RW:3 * RW + 128]
    xg = mixed[:, 3 * RW + 128:]
    lane = lax.broadcasted_iota(jnp.int32, z.shape, 1)
    tz = jnp.where(lane < 64, jnp.tanh(z), z)
    lo = _bdot(tz, lora_up)
    w_log = -_softplus(-(decay_w0 + lo[:, :RW])) - 0.5
    lw = -jnp.exp(w_log)
    a_ic = _sigmoid(iclr_a0 + lo[:, RW:])
    g = _bdot(_sigmoid(xg), gate_up)
    kk = k * k_k
    kk = kk / jnp.maximum(jnp.sqrt(_segsum(kk * kk)), 1e-12)
    k_mod = k * (1.0 + (a_ic - 1.0) * k_a)
    return jnp.concatenate([r, lw, k_mod, v, -kk, kk * a_ic, g], axis=1)


def _post(y, r, k, v, g, lnx_gain, lnx_bias, r_k):
    mu = _segsum(y) * (1.0 / HD)
    yc = y - mu
    var = _segsum(yc * yc) * (1.0 / HD)
    yn = yc * lax.rsqrt(var + GN_EPS) * lnx_gain + lnx_bias
    bonus = _segsum(r * k * r_k) * v
    return (yn + bonus) * g


def _merge(pg, ma, mb, bias):
    gates = _sigmoid(pg + bias)
    return gates[:, :D] * ma + gates[:, D:] * mb


def _swiglu(u, v):
    return u * _sigmoid(u) * v


@functools.partial(jax.custom_vjp, nondiff_argnums=(1,))
def _lane_roll(x, s):
    return pltpu.roll(x, s, 1)


def _lane_roll_fwd(x, s):
    return pltpu.roll(x, s, 1), None


def _lane_roll_bwd(s, _, g):
    n = g.shape[1]
    return (pltpu.roll(g, (n - s) % n, 1),)


_lane_roll.defvjp(_lane_roll_fwd, _lane_roll_bwd)


def _rope(x, cos, sin_lo, sin_hi):
    n = x.shape[1]
    return x * cos + _lane_roll(x, n - 8) * sin_lo + _lane_roll(x, 8) * sin_hi


def _head_rms(x, gain):
    return x * lax.rsqrt(_segsum(x * x) * (1.0 / HD) + RMS_EPS) * gain


def _attn_block(qkv_c, qkv_p, tab_c, tab_p, qg, kg, sinks, first):
    def tabs(tab, n):
        return tab[:, 0:n], tab[:, RW:RW + n], tab[:, 2 * RW:2 * RW + n]

    qg = jnp.concatenate([qg] * NH, axis=1)
    kg = jnp.concatenate([kg] * 2, axis=1)
    q = _rope(_head_rms(qkv_c[:, :RW], qg), *tabs(tab_c, RW))
    k_c = _rope(_head_rms(qkv_c[:, RW:RW + 128], kg), *tabs(tab_c, 128))
    k_p = _rope(_head_rms(qkv_p[:, RW:RW + 128], kg), *tabs(tab_p, 128))
    kband = jnp.concatenate([k_p, k_c], axis=0)
    vband = jnp.concatenate([qkv_p[:, RW + 128:], qkv_c[:, RW + 128:]], axis=0)
    G = 4
    qi = lax.broadcasted_iota(jnp.int32, (G * BLK, 2 * BLK), 0) % BLK
    kj = lax.broadcasted_iota(jnp.int32, (G * BLK, 2 * BLK), 1)
    dist = qi + BLK - kj
    valid = (dist >= 0) & (dist < BLK) & (jnp.logical_not(first) | (kj >= BLK))
    row_g = lax.broadcasted_iota(jnp.int32, (G * BLK, 1), 0) // BLK
    outs = []
    for kvh in range(2):
        kb = kband[:, kvh * HD:(kvh + 1) * HD]
        vb = vband[:, kvh * HD:(kvh + 1) * HD]
        qs = jnp.concatenate([q[:, (G * kvh + g) * HD:(G * kvh + g + 1) * HD] for g in range(G)], axis=0)
        s = _bdot_nt(qs, kb) * (HD ** -0.5)
        s = jnp.where(valid, s, NEG_INF)
        sink = jnp.zeros((G * BLK, 1), F32)
        for g in range(G):
            sink = jnp.where(row_g == g, sinks[:, G * kvh + g:G * kvh + g + 1], sink)
        m = lax.stop_gradient(jnp.maximum(jnp.max(s, axis=-1, keepdims=True), sink))
        e = jnp.exp(s - m)
        p = e / (jnp.sum(e, axis=-1, keepdims=True) + jnp.exp(sink - m))
        o = _bdot(p, vb)
        outs += [o[g * BLK:(g + 1) * BLK] for g in range(G)]
    return jnp.concatenate(outs, axis=1)


def _heads(x):
    return jnp.stack([x[:, h * HD:(h + 1) * HD] for h in range(NH)], axis=0)


def _unheads(x):
    return jnp.concatenate([x[h] for h in range(NH)], axis=1)


def _split(x, n):
    parts, rest = [], x
    for _ in range(n):
        p = rest.astype(MXU)
        parts.append(p)
        rest = rest - p.astype(F32)
    return parts


def _bdot_batched(a, b, ca, cb):
    return lax.dot_general(a, b, (((ca,), (cb,)), ((0,), (0,))), preferred_element_type=F32)


def _bmm_passes(a, b, ca, cb, passes):
    if MXU == F32:
        return lax.dot_general(a, b, (((ca,), (cb,)), ((0,), (0,))), precision=HI, preferred_element_type=F32)
    if passes == 1:
        return _bdot_batched(a.astype(MXU), b.astype(MXU), ca, cb)
    (a0, a1), (b0, b1) = _split(a, 2), _split(b, 2)
    return _bdot_batched(a0, b0, ca, cb) + (_bdot_batched(a0, b1, ca, cb) + _bdot_batched(a1, b0, ca, cb))


@functools.partial(jax.custom_vjp, nondiff_argnums=(2, 3, 4))
def _bmm(a, b, ca, cb, passes=1):
    return _bmm_passes(a, b, ca, cb, passes)


def _bmm_fwd(a, b, ca, cb, passes):
    return _bmm_passes(a, b, ca, cb, passes), (a, b)


def _bmm_bwd(ca, cb, passes, res, g):
    a, b = res
    if (ca, cb) == (2, 1):
        return _bmm_passes(g, b, 2, 2, passes), _bmm_passes(a, g, 1, 1, passes)
    if (ca, cb) == (2, 2):
        return _bmm_passes(g, b, 2, 1, passes), _bmm_passes(g, a, 1, 1, passes)
    return _bmm_passes(b, g, 2, 2, passes), _bmm_passes(a, g, 2, 1, passes)


_bmm.defvjp(_bmm_fwd, _bmm_bwd)


def _tri_dot(x, transpose):
    C = x.shape[1]
    ri = lax.broadcasted_iota(jnp.int32, (C, C), 0)
    ci = lax.broadcasted_iota(jnp.int32, (C, C), 1)
    tri = jnp.broadcast_to(((ri <= ci) if transpose else (ri >= ci)).astype(MXU), (x.shape[0], C, C))
    if MXU == F32:
        return lax.dot_general(tri, x, (((2,), (1,)), ((0,), (0,))), precision=HI, preferred_element_type=F32)
    p0, p1, p2 = _split(x, 3)
    return _bdot_batched(tri, p0, 2, 1) + (_bdot_batched(tri, p1, 2, 1) + _bdot_batched(tri, p2, 2, 1))


@jax.custom_vjp
def _cumsum_rows(x):
    return _tri_dot(x, False)


def _cumsum_rows_fwd(x):
    return _tri_dot(x, False), None


def _cumsum_rows_bwd(_, g):
    return (_tri_dot(g, True),)


_cumsum_rows.defvjp(_cumsum_rows_fwd, _cumsum_rows_bwd)

P_SCORE = 1
P_SOLVE = 1
P_STATE = 1


def _chunk(S0, r, lw, k, v, a, b):
    C = r.shape[1]
    ri = lax.broadcasted_iota(jnp.int32, (C, C), 0)
    ci = lax.broadcasted_iota(jnp.int32, (C, C), 1)
    incl = (ri >= ci)
    strict = (ri > ci)
    eye = (ri == ci).astype(F32)
    cum = _cumsum_rows(lw)
    p_in = jnp.exp(cum)
    p_ex = jnp.exp(cum - lw)
    p_inv = jnp.exp(-cum)
    at, rt, bt, kt = a * p_ex, r * p_in, b * p_inv, k * p_inv
    sc = _bmm(jnp.concatenate([at, rt], axis=1), jnp.concatenate([bt, kt], axis=1), 2, 2, P_SCORE)
    a_ab = jnp.where(strict, sc[:, :C, :C], 0.0)
    a_ak = jnp.where(strict, sc[:, :C, C:], 0.0)
    a_rb = jnp.where(incl, sc[:, C:, :C], 0.0)
    a_rk = jnp.where(incl, sc[:, C:, C:], 0.0)
    s0 = _bmm(jnp.concatenate([at, rt], axis=1), S0, 2, 2, P_STATE)
    rhs = s0[:, :C] + _bmm(a_ak, v, 2, 1, P_SCORE)
    x = eye + a_ab
    lp = a_ab
    for _ in range(int(math.log2(C)) - 1):
        lp = _bmm(lp, lp, 2, 1, P_SOLVE)
        x = x + _bmm(x, lp, 2, 1, P_SOLVE)
    u = _bmm(x, rhs, 2, 1, P_SOLVE)
    y = s0[:, C:] + _bmm(jnp.concatenate([a_rb, a_rk], axis=2), jnp.concatenate([u, v], axis=1), 2, 1, P_SCORE)
    p_last = jnp.exp(cum[:, C - 1:C, :])
    S1 = (S0 + _bmm(jnp.concatenate([u, v], axis=1), jnp.concatenate([bt, kt], axis=1), 1, 1, P_STATE)) * p_last
    return y, S1


def _scan_fwd(rw, *, name):
    T = rw.shape[0]
    n = T // CHUNK

    def body(r_ref, lw_ref, k_ref, v_ref, a_ref, b_ref, y_ref, ck_ref, s_ref):
        @pl.when(pl.program_id(0) == 0)
        def _():
            s_ref[...] = jnp.zeros_like(s_ref)

        S0 = s_ref[...]
        ck_ref[0] = S0
        y, S1 = _chunk(S0, *[_heads(ref[...]) for ref in (r_ref, lw_ref, k_ref, v_ref, a_ref, b_ref)])
        y_ref[...] = _unheads(y)
        s_ref[...] = S1

    col = lambda j: pl.BlockSpec((CHUNK, RW), lambda i: (i, j))
    return pl.pallas_call(
        body, name=name, grid=(n,),
        in_specs=[col(j) for j in range(6)],
        out_specs=[pl.BlockSpec((CHUNK, RW), lambda i: (i, 0)), pl.BlockSpec((1, NH, HD, HD), lambda i: (i, 0, 0, 0))],
        out_shape=[jax.ShapeDtypeStruct((T, RW), F32), jax.ShapeDtypeStruct((n, NH, HD, HD), F32)],
        scratch_shapes=[pltpu.VMEM((NH, HD, HD), F32)],
        compiler_params=_cparams(("arbitrary",)),
    )(rw, rw, rw, rw, rw, rw)


def _scan_bwd(rw, ck, dy, *, name):
    T = rw.shape[0]
    n = T // CHUNK

    def body(r_ref, lw_ref, k_ref, v_ref, a_ref, b_ref, ck_ref, dy_ref, o_ref, ds_ref):
        @pl.when(pl.program_id(0) == 0)
        def _():
            ds_ref[...] = jnp.zeros_like(ds_ref)

        prim = [_heads(ref[...]) for ref in (r_ref, lw_ref, k_ref, v_ref, a_ref, b_ref)]
        _, vjp = jax.vjp(_chunk, ck_ref[0], *prim)
        grads = vjp((_heads(dy_ref[...]), ds_ref[...]))
        ds_ref[...] = grads[0]
        o_ref[...] = jnp.concatenate([_unheads(g) for g in grads[1:]], axis=1)

    col = lambda j: pl.BlockSpec((CHUNK, RW), lambda i: (n - 1 - i, j))
    return pl.pallas_call(
        body, name=name, grid=(n,),
        in_specs=[col(j) for j in range(6)] + [pl.BlockSpec((1, NH, HD, HD), lambda i: (n - 1 - i, 0, 0, 0)),
                                               pl.BlockSpec((CHUNK, RW), lambda i: (n - 1 - i, 0))],
        out_specs=pl.BlockSpec((CHUNK, 6 * RW), lambda i: (n - 1 - i, 0)),
        out_shape=jax.ShapeDtypeStruct((T, 6 * RW), F32),
        scratch_shapes=[pltpu.VMEM((NH, HD, HD), F32)],
        compiler_params=_cparams(("arbitrary",)),
    )(rw, rw, rw, rw, rw, rw, ck, dy)


def _attn_fwd(qkv, tab, qg, kg, sinks, *, name):
    T = qkv.shape[0]
    n = T // BLK

    def body(c_ref, p_ref, tc_ref, tp_ref, qg_ref, kg_ref, s_ref, o_ref):
        first = pl.program_id(0) == 0
        o_ref[...] = _attn_block(c_ref[...], p_ref[...], tc_ref[...], tp_ref[...], qg_ref[...], kg_ref[...],
                                 s_ref[...], first).astype(o_ref.dtype)

    cur = lambda w: pl.BlockSpec((BLK, w), lambda i: (i, 0))
    prev = lambda w: pl.BlockSpec((BLK, w), lambda i: (jnp.maximum(i - 1, 0), 0))
    return pl.pallas_call(
        body, name=name, grid=(n,),
        in_specs=[cur(QKV_W), prev(QKV_W), cur(3 * RW), prev(3 * RW), _full_spec(qg), _full_spec(kg), _full_spec(sinks)],
        out_specs=cur(RW), out_shape=jax.ShapeDtypeStruct((T, RW), MXU),
        compiler_params=_cparams(("arbitrary",)),
    )(qkv, qkv, tab, tab, qg, kg, sinks)


def _attn_bwd(qkv, tab, qg, kg, sinks, dy, *, name):
    T = qkv.shape[0]
    n = T // BLK

    def body(c_ref, p_ref, tc_ref, tp_ref, qg_ref, kg_ref, s_ref, dy_ref, dqkv_ref, dqg_ref, dkg_ref, ds_ref, carry_ref):
        i = pl.program_id(0)
        first = i == n - 1

        @pl.when(i == 0)
        def _():
            carry_ref[...] = jnp.zeros_like(carry_ref)
            dqg_ref[...] = jnp.zeros_like(dqg_ref)
            dkg_ref[...] = jnp.zeros_like(dkg_ref)
            ds_ref[...] = jnp.zeros_like(ds_ref)

        tc, tp = tc_ref[...], tp_ref[...]
        f = lambda c, p, qg_, kg_, sk: _attn_block(c, p, tc, tp, qg_, kg_, sk, first)
        _, vjp = jax.vjp(f, c_ref[...], p_ref[...], qg_ref[...], kg_ref[...], s_ref[...])
        dc, dp, dqg, dkg, dsk = vjp(dy_ref[...].astype(F32))
        dqkv_ref[...] = (dc + carry_ref[...]).astype(dqkv_ref.dtype)
        carry_ref[...] = dp
        dqg_ref[...] += dqg
        dkg_ref[...] += dkg
        ds_ref[...] += dsk

    cur = lambda w: pl.BlockSpec((BLK, w), lambda i: (n - 1 - i, 0))
    prev = lambda w: pl.BlockSpec((BLK, w), lambda i: (jnp.maximum(n - 2 - i, 0), 0))
    return pl.pallas_call(
        body, name=name, grid=(n,),
        in_specs=[cur(QKV_W), prev(QKV_W), cur(3 * RW), prev(3 * RW), _full_spec(qg), _full_spec(kg), _full_spec(sinks),
                  cur(RW)],
        out_specs=[cur(QKV_W), _full_spec(qg), _full_spec(kg), _full_spec(sinks)],
        out_shape=[jax.ShapeDtypeStruct((T, QKV_W), MXU), jax.ShapeDtypeStruct(qg.shape, F32),
                   jax.ShapeDtypeStruct(kg.shape, F32), jax.ShapeDtypeStruct(sinks.shape, F32)],
        scratch_shapes=[pltpu.VMEM((BLK, QKV_W), F32)],
        compiler_params=_cparams(("arbitrary",)),
    )(qkv, qkv, tab, tab, qg, kg, sinks, dy)


def _shift_down(cur, prev8, i):
    rolled = pltpu.roll(cur, 1, 0)
    first_row = jnp.where(i > 0, prev8[7:8, :], 0.0)
    row = lax.broadcasted_iota(jnp.int32, cur.shape, 0)
    return jnp.where(row == 0, first_row, rolled)


def _shift_up(cur, next8, i, n):
    tm = cur.shape[0]
    rolled = pltpu.roll(cur, tm - 1, 0)
    last_row = jnp.where(i < n - 1, next8[0:1, :], 0.0)
    row = lax.broadcasted_iota(jnp.int32, cur.shape, 0)
    return jnp.where(row == tm - 1, last_row, rolled)


def _ada_parts(ada):
    return [ada[:, j * D:(j + 1) * D] for j in range(6)]


def _rope_table(positions):
    half = HD // 8
    inv_freq = 500000.0 ** (-jnp.arange(half, dtype=F32) / half)
    ang = positions.astype(F32)[:, None] * inv_freq
    cos, sin = jnp.cos(ang), jnp.sin(ang)
    T = positions.shape[0]
    pad = HD - 2 * half
    c64 = jnp.concatenate([cos, cos, jnp.ones((T, pad), F32)], axis=1)
    lo64 = jnp.concatenate([-sin, jnp.zeros((T, HD - half), F32)], axis=1)
    hi64 = jnp.concatenate([jnp.zeros((T, half), F32), sin, jnp.zeros((T, pad), F32)], axis=1)
    return jnp.concatenate([jnp.tile(t, (1, NH)) for t in (c64, lo64, hi64)], axis=1)


def _local_step(x, target, ada, tab, w, s):
    T = x.shape[0]
    tm = _pick(T, (512, 256, 128))
    tm_wide = _pick(T, (256, 128))
    tm_vjp = _pick(T, (128,))
    row = lambda n, dt=F32: (n, dt, 'row')
    acc = lambda n, r=1: (n, F32, r)

    def f_norm1(i, n, x_, g, ada_):
        sh, sc = ada_[:, 0:D], ada_[:, D:2 * D]
        return (_norm_mod(x_, g, sc, sh),)
    (h1,) = _rowwise(f_norm1, [x], [s['norm1_gain'], ada], [row(D, MXU)], tm=tm, name="f_norm1")
    proj = _mm_nn(h1, w['w_in'], name="f_proj")
    proj_qkv = proj[:, SHIFT_W:SHIFT_W + QKV_W]
    proj_g = proj[:, SHIFT_W + QKV_W:]
    prep_consts = [s['decay_w0'], s['lora_up'], s['iclr_a0'], s['gate_up'], s['k_k'], s['k_a']]

    def f_prep(i, n, cur, prev8, mu, *params):
        mixed = cur + (_shift_down(cur, prev8, i) - cur) * mu
        return (_prep(mixed, *params),)
    (rw,) = _rowwise(f_prep, [(proj, SHIFT_W)], [s['tshift_mu']] + prep_consts, [row(7 * RW)], tm=tm_wide,
                     name="f_prep", halo=[(proj, SHIFT_W, 'prev')])
    y, ck = _scan_fwd(rw, name="f_scan")
    post_consts = [s['lnx_gain'], s['lnx_bias'], s['r_k']]

    def post_of(y_, rw_, *params):
        return _post(y_, rw_[:, 0:RW], rw_[:, 2 * RW:3 * RW], rw_[:, 3 * RW:4 * RW], rw_[:, 6 * RW:7 * RW], *params)

    def f_post(i, n, y_, rw_, *params):
        return (post_of(y_, rw_, *params),)
    (ya,) = _rowwise(f_post, [y, rw], post_consts, [row(RW, MXU)], tm=tm_wide, name="f_post")
    yb = _attn_fwd(proj_qkv, tab, s['q_norm_gain'], s['k_norm_gain'], s['attn_sinks'], name="f_attn")
    ma = _mm_nn(ya, w['w_branch_a'], name="f_branch_a")
    mb = _mm_nn(yb, w['w_branch_b'], name="f_branch_b")

    def f_merge(i, n, pg, ma_, mb_, bias):
        return (_merge(pg, ma_, mb_, bias),)
    (merged,) = _rowwise(f_merge, [proj_g, ma, mb], [s['branch_gate_b']], [row(D, MXU)], tm=tm, name="f_merge")
    mo = _mm_nn(merged, w['w_out'], name="f_out")

    def f_res1(i, n, x_, mo_, g, ada_):
        x1_ = x_ + ada_[:, 2 * D:3 * D] * mo_
        return x1_, _norm_mod(x1_, g, ada_[:, 4 * D:5 * D], ada_[:, 3 * D:4 * D])
    x1, h2 = _rowwise(f_res1, [x, mo], [s['norm2_gain'], ada], [row(D), row(D, MXU)], tm=tm, name="f_res1")
    uv = _mm_nn(h2, w['ffn_w13'], name="f_ffn_in")

    def f_act(i, n, uv_):
        return (_swiglu(uv_[:, :DFF], uv_[:, DFF:]),)
    (act,) = _rowwise(f_act, [uv], [], [row(DFF, MXU)], tm=tm_wide, name="f_act")
    ff = _mm_nn(act, w['ffn_w2'], name="f_ffn_out")

    def f_loss(i, n, x1_, ff_, tgt, ada_):
        g2 = ada_[:, 5 * D:6 * D]
        err = x1_ + g2 * ff_ - tgt
        dx2 = err * (1.0 / D)
        loss = 0.5 * jnp.sum(jnp.sum(err * err, axis=1, keepdims=True) * (1.0 / D), axis=0, keepdims=True)
        return dx2, (dx2 * g2), jnp.broadcast_to(loss, (1, 128)), jnp.sum(dx2 * ff_, axis=0, keepdims=True)
    dx2, dff, loss, dgate2 = _rowwise(f_loss, [x1, ff, target], [ada], [row(D), row(D, MXU), acc(128), acc(D)],
                                      tm=tm, name="f_loss")

    dact = _mm_nt(dff, w['ffn_w2'], name="b_ffn_out_dx")
    g_w2 = _mm_tn(act, dff, name="b_ffn_out_dw", out_dtype=MXU)

    def b_act(i, n, uv_, dact_):
        _, vjp = jax.vjp(_swiglu, uv_[:, :DFF], uv_[:, DFF:])
        du, dv = vjp(dact_)
        return (jnp.concatenate([du, dv], axis=1),)
    (duv,) = _rowwise(b_act, [uv, dact], [], [row(2 * DFF, MXU)], tm=tm_wide, name="b_act")
    dh2 = _mm_nt(duv, w['ffn_w13'], name="b_ffn_in_dx")
    g_w13 = _mm_tn(h2, duv, name="b_ffn_in_dw", out_dtype=MXU)

    def b_res1(i, n, x1_, dh2_, dx2_, mo_, g, ada_):
        _, vjp = jax.vjp(_norm_mod, x1_, g, ada_[:, 4 * D:5 * D], ada_[:, 3 * D:4 * D])
        dxn, dg, dsc, dsh = vjp(dh2_)
        dx1_ = dxn + dx2_
        g1 = ada_[:, 2 * D:3 * D]
        return dx1_, dx1_ * g1, dg, dsc, dsh, jnp.sum(dx1_ * mo_, axis=0, keepdims=True)
    dx1, dmo, d_gain2, d_scale2, d_shift2, dgate1 = _rowwise(
        b_res1, [x1, dh2, dx2, mo], [s['norm2_gain'], ada], [row(D), row(D, MXU), acc(D), acc(D), acc(D), acc(D)],
        tm=tm, name="b_res1")
    dmerged = _mm_nt(dmo, w['w_out'], name="b_out_dx")
    g_wout = _mm_tn(merged, dmo, name="b_out_dw", out_dtype=MXU)

    def b_merge(i, n, pg, ma_, mb_, dm, bias):
        _, vjp = jax.vjp(_merge, pg, ma_, mb_, bias)
        dpg, dma_, dmb_, dbias = vjp(dm)
        return dpg, dma_, dmb_, dbias
    dpg, dma, dmb, d_bias = _rowwise(b_merge, [proj_g, ma, mb, dmerged], [s['branch_gate_b']],
                                     [row(GATE_W, MXU), row(D, MXU), row(D, MXU), acc(GATE_W)], tm=tm_wide, name="b_merge")
    dya = _mm_nt(dma, w['w_branch_a'], name="b_branch_a_dx")
    g_wa = _mm_tn(ya, dma, name="b_branch_a_dw", out_dtype=MXU, col_shards=4)
    dyb = _mm_nt(dmb, w['w_branch_b'], name="b_branch_b_dx", out_dtype=F32)
    g_wb = _mm_tn(yb, dmb, name="b_branch_b_dw", out_dtype=MXU, col_shards=4)
    dqkv, d_qg, d_kg, d_sinks = _attn_bwd(proj_qkv, tab, s['q_norm_gain'], s['k_norm_gain'], s['attn_sinks'], dyb,
                                          name="b_attn")

    def b_post(i, n, y_, rw_, dya_, *params):
        _, vjp = jax.vjp(post_of, y_, rw_, *params)
        return vjp(dya_)
    dy, drw_post, d_lnx_gain, d_lnx_bias, d_r_k = _rowwise(
        b_post, [y, rw, dya], post_consts, [row(RW), row(7 * RW), acc(RW), acc(RW), acc(RW)], tm=tm_vjp, name="b_post")
    dscan = _scan_bwd(rw, ck, dy, name="b_scan")

    def b_prep(i, n, cur, drw_, dscan_, prev8, mu, *params):
        shifted = _shift_down(cur, prev8, i)
        mixed = cur + (shifted - cur) * mu
        _, vjp = jax.vjp(_prep, mixed, *params)
        ct = drw_ + jnp.concatenate([dscan_, jnp.zeros((dscan_.shape[0], RW), F32)], axis=1)
        grads = vjp(ct)
        dmixed = grads[0]
        return (dmixed, jnp.sum(dmixed * (shifted - cur), axis=0, keepdims=True)) + tuple(grads[1:])
    dmixed, d_mu, d_w0, d_lora, d_a0, d_gate_up, d_kk, d_ka = _rowwise(
        b_prep, [(proj, SHIFT_W), drw_post, dscan], [s['tshift_mu']] + prep_consts,
        [row(SHIFT_W), acc(SHIFT_W), acc(RW), acc(2 * RW, 128), acc(RW), acc(RW, 128), acc(RW), acc(RW)],
        tm=tm_vjp, name="b_prep", halo=[(proj, SHIFT_W, 'prev')])

    def b_gather(i, n, dm, dqkv_, dpg_, next8, mu):
        dcur = dm * (1.0 - mu) + _shift_up(dm, next8, i, n) * mu
        return (jnp.concatenate([dcur.astype(MXU), dqkv_, dpg_], axis=1),)
    (dproj,) = _rowwise(b_gather, [dmixed, dqkv, dpg], [s['tshift_mu']], [row(IN_W, MXU)], tm=tm_wide, name="b_gather",
                        halo=[(dmixed, SHIFT_W, 'next')])
    dh1 = _mm_nt(dproj, w['w_in'], name="b_proj_dx")
    g_win = _mm_tn(h1, dproj, name="b_proj_dw", out_dtype=MXU, col_shards=4)

    def b_norm1(i, n, x_, dh1_, dx1_, g, ada_):
        _, vjp = jax.vjp(_norm_mod, x_, g, ada_[:, D:2 * D], ada_[:, 0:D])
        dxn, dg, dsc, dsh = vjp(dh1_)
        return dxn + dx1_, dg, dsc, dsh
    dx, d_gain1, d_scale1, d_shift1 = _rowwise(b_norm1, [x, dh1, dx1], [s['norm1_gain'], ada],
                                               [row(D), acc(D), acc(D), acc(D)], tm=tm, name="b_norm1")

    d_ada = jnp.concatenate([d_shift1, d_scale1, dgate1, d_shift2, d_scale2, dgate2], axis=1)
    def col_blocks(g):
        k, n = g.shape
        return g.reshape(k, 4, n // 4).transpose(1, 0, 2).astype(MXU)
    fs = DFF // 4
    gw = dict(w_in=g_win, decay_up=col_blocks(d_lora[:64, :RW]), iclr_up=col_blocks(d_lora[64:, RW:]),
              gate_up=col_blocks(d_gate_up), w_branch_a=g_wa, w_branch_b=g_wb, w_out=g_wout.reshape(4, D // 4, D),
              ffn_w1=jnp.stack([g_w13[:, j * fs:(j + 1) * fs] for j in range(4)]),
              ffn_w3=jnp.stack([g_w13[:, DFF + j * fs:DFF + (j + 1) * fs] for j in range(4)]),
              ffn_w2=g_w2.reshape(4, fs, D))
    gs = dict(norm1_gain=d_gain1, norm2_gain=d_gain2, tshift_mu=d_mu, decay_w0=d_w0, iclr_a0=d_a0, k_k=d_kk, k_a=d_ka,
              r_k=d_r_k, lnx_gain=d_lnx_gain, lnx_bias=d_lnx_bias, q_norm_gain=d_qg, k_norm_gain=d_kg,
              attn_sinks=d_sinks, branch_gate_b=d_bias)
    return loss, dx, d_ada, gw, gs


ANY = pl.BlockSpec(memory_space=pl.ANY)


def _place():
    x, y, c = lax.axis_index("x"), lax.axis_index("y"), lax.axis_index("c")
    return x, y, c, [(1 - x, y), (x, 1 - y), (1 - x, 1 - y)]


def _all_gather8(x_shard, *, name):
    m_per, n = x_shard.shape

    def body(x_ref, out_ref, send_sems, recv_sems, local_sem):
        x, y, c, chips = _place()
        me, sibling = (x, y, c), (x, y, 1 - c)

        def rows(px, py, pc):
            return out_ref.at[pl.ds((4 * px + 2 * py + pc) * m_per, m_per), :]

        def copy(k, block, to, src=None):
            return pltpu.make_async_remote_copy(
                src_ref=rows(*block) if src is None else src, dst_ref=rows(*block),
                send_sem=send_sems.at[k], recv_sem=recv_sems.at[k], device_id=to, device_id_type=MESH)

        mine = pltpu.make_async_copy(x_ref, rows(*me), local_sem)
        mine.start()
        first = [copy(0, me, sibling, src=x_ref)]
        first += [copy(1 + j, me, (*chip, c), src=x_ref) for j, chip in enumerate(chips)]
        for cp in first:
            cp.start()
        passed = [copy(4 + j, (*chip, c), sibling) for j, chip in enumerate(chips)]
        for j, chip in enumerate(chips):
            copy(1 + j, (*chip, c), me).wait_recv()
            passed[j].start()
        copy(0, sibling, me).wait_recv()
        for j, chip in enumerate(chips):
            copy(4 + j, (*chip, 1 - c), me).wait_recv()
        for cp in first + passed:
            cp.wait_send()
        mine.wait()

    return pl.pallas_call(
        body, name=name, out_shape=jax.ShapeDtypeStruct((8 * m_per, n), x_shard.dtype),
        in_specs=[pl.BlockSpec(memory_space=pltpu.VMEM)], out_specs=pl.BlockSpec(memory_space=pltpu.VMEM),
        scratch_shapes=[pltpu.SemaphoreType.DMA((7,)), pltpu.SemaphoreType.DMA((7,)), pltpu.SemaphoreType.DMA],
    )(x_shard)


def _gather_chips(shards, *, name):
    n = len(shards)

    def body(*refs):
        x_refs, out_refs = refs[:n], refs[n:2 * n]
        send_sems, recv_sems, local_sems = refs[2 * n:]
        x, y, c, chips = _place()
        s_me = 2 * x + y

        def copy(a, k, s):
            return pltpu.make_async_remote_copy(
                src_ref=x_refs[a], dst_ref=out_refs[a].at[s], send_sem=send_sems.at[3 * a + k],
                recv_sem=recv_sems.at[3 * a + k], device_id=(*chips[k], c), device_id_type=MESH)

        mine = [pltpu.make_async_copy(x_refs[a], out_refs[a].at[s_me], local_sems.at[a]) for a in range(n)]
        sends = [copy(a, k, s_me) for a in range(n) for k in range(3)]
        for cp in mine + sends:
            cp.start()
        for a in range(n):
            for k, (px, py) in enumerate(chips):
                copy(a, k, 2 * px + py).wait_recv()
        for cp in sends:
            cp.wait_send()
        for cp in mine:
            cp.wait()

    return pl.pallas_call(
        body, name=name, out_shape=[jax.ShapeDtypeStruct((4,) + s.shape, s.dtype) for s in shards],
        in_specs=[ANY] * n, out_specs=[ANY] * n,
        scratch_shapes=[pltpu.SemaphoreType.DMA((3 * n,)), pltpu.SemaphoreType.DMA((3 * n,)),
                        pltpu.SemaphoreType.DMA((n,))],
    )(*shards)


def _scatter_chips(parts, *, name):
    n = len(parts)

    def body(*refs):
        g_refs, out_refs = refs[:n], refs[n:2 * n]
        send_sems, recv_sems = refs[2 * n:]
        x, y, c, chips = _place()

        def copy(a, k, s):
            return pltpu.make_async_remote_copy(
                src_ref=g_refs[a].at[s], dst_ref=out_refs[a].at[k], send_sem=send_sems.at[3 * a + k],
                recv_sem=recv_sems.at[3 * a + k], device_id=(*chips[k], c), device_id_type=MESH)

        sends = [copy(a, k, 2 * px + py) for a in range(n) for k, (px, py) in enumerate(chips)]
        for cp in sends:
            cp.start()
        for cp in sends:
            cp.wait_recv()
        for cp in sends:
            cp.wait_send()

    return pl.pallas_call(
        body, name=name, out_shape=[jax.ShapeDtypeStruct((3,) + p.shape[1:], p.dtype) for p in parts],
        in_specs=[ANY] * n, out_specs=[ANY] * n,
        scratch_shapes=[pltpu.SemaphoreType.DMA((3 * n,)), pltpu.SemaphoreType.DMA((3 * n,))],
    )(*parts)


def _swap_sibling(vs, *, name):
    n = len(vs)

    def body(*refs):
        v_refs, out_refs = refs[:n], refs[n:2 * n]
        send_sems, recv_sems = refs[2 * n:]
        x, y, c, _ = _place()
        cps = [pltpu.make_async_remote_copy(src_ref=v_refs[a], dst_ref=out_refs[a], send_sem=send_sems.at[a],
                                            recv_sem=recv_sems.at[a], device_id=(x, y, 1 - c), device_id_type=MESH)
               for a in range(n)]
        for cp in cps:
            cp.start()
        for cp in cps:
            cp.wait()

    return pl.pallas_call(
        body, name=name, out_shape=[jax.ShapeDtypeStruct(v.shape, v.dtype) for v in vs],
        in_specs=[ANY] * n, out_specs=[ANY] * n,
        scratch_shapes=[pltpu.SemaphoreType.DMA((n,)), pltpu.SemaphoreType.DMA((n,))],
    )(*vs)


def _sum_parts(own, others, *, name):
    R, C = own.shape
    tm = _pick(R, (256, 128, 64))

    def body(own_ref, o0_ref, o1_ref, o2_ref, out_ref):
        tot = own_ref[...].astype(F32)
        for ref in (o0_ref, o1_ref, o2_ref):
            tot = tot + ref[...].astype(F32)
        out_ref[...] = tot

    part = lambda k: pl.BlockSpec((None, tm, C), lambda i: (k, i, 0))
    return pl.pallas_call(
        body, name=name, grid=(R // tm,),
        in_specs=[pl.BlockSpec((tm, C), lambda i: (i, 0)), part(0), part(1), part(2)],
        out_specs=pl.BlockSpec((tm, C), lambda i: (i, 0)), out_shape=jax.ShapeDtypeStruct((R, C), F32),
        compiler_params=_cparams(("arbitrary",)),
    )(own, others, others, others)


def _adam_math(w_, m_, v_, g):
    m2 = ADAM_B1 * m_ + (1.0 - ADAM_B1) * g
    v2 = ADAM_B2 * v_ + (1.0 - ADAM_B2) * jnp.square(g)
    m_hat = m2 / (1.0 - ADAM_B1 ** ADAM_STEP)
    v_hat = v2 / (1.0 - ADAM_B2 ** ADAM_STEP)
    delta = -ADAM_LR * (m_hat / (jnp.sqrt(v_hat) + ADAM_EPS) + ADAM_WD * w_)
    return delta, m2, v2


SMALL_SLOTS = 16
SMALL_COLS = 6 * D


def _pack_small(grads, *, name):
    n = len(grads)

    def body(*refs):
        out_ref = refs[n]
        out_ref[...] = jnp.zeros_like(out_ref)
        for i, ref in enumerate(refs[:n]):
            out_ref[i:i + 1, 0:ref.shape[1]] = ref[...]

    return pl.pallas_call(body, name=name, out_shape=jax.ShapeDtypeStruct((SMALL_SLOTS, SMALL_COLS), F32))(*grads)


def _adamw_small(ws, ms, vs, gathered, *, name):
    n = len(ws)

    def body(*refs):
        w_refs, m_refs, v_refs, g_ref = refs[:n], refs[n:2 * n], refs[2 * n:3 * n], refs[3 * n]
        outs = refs[3 * n + 1:]
        for i in range(n):
            nc = w_refs[i].shape[1]
            g = g_ref[i:i + 1, 0:nc]
            for d in range(1, 8):
                g = g + g_ref[d * SMALL_SLOTS + i:d * SMALL_SLOTS + i + 1, 0:nc]
            delta, m2, v2 = _adam_math(w_refs[i][...], m_refs[i][...], v_refs[i][...], g)
            for k, val in enumerate((g, delta, m2, v2)):
                outs[k * n + i][...] = val

    shapes = [jax.ShapeDtypeStruct(w.shape, F32) for w in ws]
    res = pl.pallas_call(body, name=name, out_shape=shapes * 4,
                         compiler_params=pltpu.CompilerParams(vmem_limit_bytes=VMEM_LIMIT))(*ws, *ms, *vs, gathered)
    return [res[k * n:(k + 1) * n] for k in range(4)]


def _adamw(w, m, v, gparts, *, tm, name):
    def fn(i, n, w_, m_, v_, *gs):
        g = gs[0]
        for p in gs[1:]:
            g = g + p
        return (g,) + _adam_math(w_, m_, v_, g)
    nc = w.shape[1]
    return _rowwise(fn, [w, m, v] + list(gparts), [], [(nc, F32, 'row')] * 4, tm=tm, name=name)


WEIGHTS = ['ada_w', 'ada_b', 'norm1_gain', 'norm2_gain', 'w_in', 'tshift_mu', 'decay_w0', 'decay_up', 'iclr_a0',
           'iclr_up', 'gate_up', 'k_k', 'k_a', 'r_k', 'lnx_gain', 'lnx_bias', 'q_norm_gain', 'k_norm_gain', 'attn_sinks',
           'branch_gate_b', 'w_branch_a', 'w_branch_b', 'w_out', 'ffn_w1', 'ffn_w3', 'ffn_w2']
SHARDED = [('w_in', 1), ('decay_up', 1), ('iclr_up', 1), ('gate_up', 1), ('w_branch_a', 1), ('w_branch_b', 1),
           ('w_out', 0), ('ffn_w1', 1), ('ffn_w3', 1), ('ffn_w2', 0)]
SMALL = ['ada_b', 'norm1_gain', 'norm2_gain', 'tshift_mu', 'decay_w0', 'iclr_a0', 'k_k', 'k_a', 'r_k', 'lnx_gain',
         'lnx_bias', 'q_norm_gain', 'k_norm_gain', 'attn_sinks', 'branch_gate_b']


def kernel(x, c, positions, ada_w, ada_b, norm1_gain, norm2_gain, w_in, tshift_mu, decay_w0, decay_up, iclr_a0, iclr_up, gate_up, k_k, k_a, r_k, lnx_gain, lnx_bias, q_norm_gain, k_norm_gain, attn_sinks, branch_gate_b, w_branch_a, w_branch_b, w_out, ffn_w1, ffn_w3, ffn_w2, loss_target, m_ada_w, m_ada_b, m_norm1_gain, m_norm2_gain, m_w_in, m_tshift_mu, m_decay_w0, m_decay_up, m_iclr_a0, m_iclr_up, m_gate_up, m_k_k, m_k_a, m_r_k, m_lnx_gain, m_lnx_bias, m_q_norm_gain, m_k_norm_gain, m_attn_sinks, m_branch_gate_b, m_w_branch_a, m_w_branch_b, m_w_out, m_ffn_w1, m_ffn_w3, m_ffn_w2, v_ada_w, v_ada_b, v_norm1_gain, v_norm2_gain, v_w_in, v_tshift_mu, v_decay_w0, v_decay_up, v_iclr_a0, v_iclr_up, v_gate_up, v_k_k, v_k_a, v_r_k, v_lnx_gain, v_lnx_bias, v_q_norm_gain, v_k_norm_gain, v_attn_sinks, v_branch_gate_b, v_w_branch_a, v_w_branch_b, v_w_out, v_ffn_w1, v_ffn_w3, v_ffn_w2):
    a = dict(locals())
    W = {n: a[n] for n in WEIGHTS}
    M = {n: a['m_' + n] for n in WEIGHTS}
    V = {n: a['v_' + n] for n in WEIGHTS}
    xi, yi, ci = lax.axis_index("x"), lax.axis_index("y"), lax.axis_index("c")
    me = 4 * xi + 2 * yi + ci
    shard = 2 * xi + yi
    mat = lambda t: t.reshape(t.shape[-2], t.shape[-1])
    sharded = [n for n, _ in SHARDED]

    gathered = _gather_chips([mat(W[n]).astype(MXU) for n in sharded], name="gather_weights")
    full = {}
    for (n, ax), g in zip(SHARDED, gathered, strict=True):
        full[n] = g.reshape(-1, g.shape[2]) if ax == 0 else jnp.concatenate([g[j] for j in range(4)], axis=1)

    c_all = _all_gather8(jnp.broadcast_to(c, (8, D)), name="gather_c")[0::8]
    pad_rows = lambda t: jnp.concatenate([t, jnp.zeros((BLK - 8, t.shape[1]), t.dtype)])
    c_all = pad_rows(c_all.astype(MXU))
    ada_cols = _mm_nn(c_all, mat(ada_w).astype(MXU), name="f_ada")[:8]
    ada_all = _all_gather8(ada_cols, name="gather_ada").reshape(2, 2, 2, 8, 6 * D // 4)
    ada_mine = lax.dynamic_index_in_dim(ada_all[:, :, 0], me, axis=2, keepdims=False)
    ada = ada_mine.reshape(1, 6 * D) + mat(ada_b)

    zero = jnp.zeros((64, RW), MXU)
    lora = jnp.concatenate([jnp.concatenate([full['decay_up'], zero], axis=1),
                            jnp.concatenate([zero, full['iclr_up']], axis=1)], axis=0)
    w = dict(w_in=full['w_in'], w_branch_a=full['w_branch_a'], w_branch_b=full['w_branch_b'], w_out=full['w_out'],
             ffn_w13=jnp.concatenate([full['ffn_w1'], full['ffn_w3']], axis=1), ffn_w2=full['ffn_w2'])
    s = {n: W[n].reshape(1, -1) for n in SMALL if n != 'ada_b'}
    s['lora_up'] = lora.astype(F32)
    s['gate_up'] = full['gate_up'].astype(F32)
    tab = _rope_table(positions.reshape(-1))
    loss, dx, d_ada, gw, gs = _local_step(x[0], loss_target[0], ada, tab, w, s)
    loss = lax.psum(loss[0, 0], ("x", "y", "c"))

    gs['ada_b'] = d_ada
    gsmall = _pack_small([gs[n] for n in SMALL], name="pack_small_grads")
    gsmall_all = _all_gather8(gsmall, name="gather_small_grads")
    row = lambda src: [src[n].reshape(1, -1) for n in SMALL]
    sm_out = _adamw_small(row(W), row(M), row(V), gsmall_all, name="adamw_small")
    sm_out = [{n: o.reshape(W[n].shape) for n, o in zip(SMALL, outs_k, strict=True)} for outs_k in sm_out]

    d_ada_all = gsmall_all[0::SMALL_SLOTS]
    d_ada_cols = lax.dynamic_slice_in_dim(d_ada_all, shard * (6 * D // 4), 6 * D // 4, axis=1)
    g_ada_w = _mm_tn(c_all, pad_rows(d_ada_cols.astype(MXU)), name="b_ada")
    ada_out = _adamw(mat(ada_w), mat(m_ada_w), mat(v_ada_w), [g_ada_w], tm=256, name="adamw_ada")

    from_chips = _scatter_chips([gw[n] for n in sharded], name="scatter_grads")
    parts = [_sum_parts(lax.dynamic_index_in_dim(gw[n], shard, axis=0, keepdims=False), r, name="sum_" + n)
             for n, r in zip(sharded, from_chips, strict=True)]
    others = _swap_sibling(parts, name="swap_grads")
    sh_out = {}
    for n, part, other in zip(sharded, parts, others, strict=True):
        sh_out[n] = _adamw(mat(W[n]), mat(M[n]), mat(V[n]), [part, other], tm=_pick(part.shape[0], (256, 128, 64)),
                           name="adamw_" + n)

    def leaf(k, n):
        if n == 'ada_w':
            return ada_out[k].reshape(W[n].shape)
        if n in sharded:
            return sh_out[n][k].reshape(W[n].shape)
        return sm_out[k][n]
    outs = [leaf(k, n) for k in range(4) for n in WEIGHTS]
    return (loss, dx[None], *outs)
```

```python
import functools
import math

import jax
import jax.numpy as jnp
from jax import lax
from jax.experimental import pallas as pl
from jax.experimental.pallas import tpu as pltpu

F32 = jnp.float32
BF16 = jnp.bfloat16
MXU = BF16
HI = lax.Precision.HIGHEST

D = 1024
HD = 64
NH = 8
RW = NH * HD
SHIFT_W = 3 * RW + 64 + 64 + 128
QKV_W = RW + 2 * 128
GATE_W = 2 * D
IN_W = SHIFT_W + QKV_W + GATE_W
DFF = 2816
BLK = 128
CHUNK = 64
RMS_EPS = 1e-6
GN_EPS = 64e-5
NEG_INF = -1e30
ADAM_LR, ADAM_B1, ADAM_B2, ADAM_EPS, ADAM_WD, ADAM_STEP = 0.001, 0.9, 0.999, 1e-08, 0.01, 10
VMEM_LIMIT = 56 * 1024 * 1024
MESH = pl.DeviceIdType.MESH


def _cparams(sem=None):
    return pltpu.CompilerParams(dimension_semantics=sem, vmem_limit_bytes=VMEM_LIMIT)


def _full_spec(a):
    nd = a.ndim
    return pl.BlockSpec(a.shape, lambda *_: (0,) * nd)


def _rowwise(fn, rows, consts, outs, *, tm, name, halo=()):
    rows = [a if isinstance(a, tuple) else (a, a.shape[1]) for a in rows]
    T = rows[0][0].shape[0]
    assert T % tm == 0 and tm % 8 == 0
    n_tiles = T // tm
    n_in = len(rows) + len(halo) + len(consts)
    in_specs = [pl.BlockSpec((tm, nc), lambda i: (i, 0)) for _, nc in rows]
    args = [a for a, _ in rows]
    for a, nc, kind in halo:
        if kind == 'prev':
            in_specs.append(pl.BlockSpec((8, nc), lambda i: (jnp.maximum(i * (tm // 8) - 1, 0), 0)))
        else:
            in_specs.append(pl.BlockSpec((8, nc), lambda i: (jnp.minimum((i + 1) * (tm // 8), T // 8 - 1), 0)))
        args.append(a)
    in_specs += [_full_spec(a) for a in consts]
    args += list(consts)
    out_shape, out_specs = [], []
    for ncols, dtype, kind in outs:
        if kind == 'row':
            out_shape.append(jax.ShapeDtypeStruct((T, ncols), dtype))
            out_specs.append(pl.BlockSpec((tm, ncols), lambda i: (i, 0)))
        else:
            out_shape.append(jax.ShapeDtypeStruct((kind, ncols), dtype))
            out_specs.append(pl.BlockSpec((kind, ncols), lambda i: (0, 0)))

    def body(*refs):
        i = pl.program_id(0)
        vals = [r[...] for r in refs[:n_in]]
        res = fn(i, n_tiles, *vals)
        for (ncols, dtype, kind), o_ref, val in zip(outs, refs[n_in:], res, strict=True):
            if kind == 'row':
                o_ref[...] = val.astype(dtype)
            else:
                @pl.when(i == 0)
                def _():
                    o_ref[...] = jnp.zeros_like(o_ref)
                o_ref[...] += val.astype(dtype)

    res = pl.pallas_call(
        body, name=name, grid=(n_tiles,), in_specs=in_specs, out_specs=out_specs, out_shape=out_shape,
        compiler_params=_cparams(("arbitrary",)),
    )(*args)
    return res


def _pick(n, cands):
    for c in cands:
        if n % c == 0:
            return c
    return n


def _mm_nn(a, w, *, name, out_dtype=F32):
    T, K = a.shape
    N = w.shape[1]
    tm = _pick(T, (512, 256, 128))
    tn = _pick(N, (1024, 1408, 896, 768, 512, 256, 128))

    def body(a_ref, w_ref, o_ref):
        o_ref[...] = jnp.dot(a_ref[...], w_ref[...], preferred_element_type=F32).astype(out_dtype)

    return pl.pallas_call(
        body, name=name, grid=(N // tn, T // tm),
        in_specs=[pl.BlockSpec((tm, K), lambda j, i: (i, 0)), pl.BlockSpec((K, tn), lambda j, i: (0, j))],
        out_specs=pl.BlockSpec((tm, tn), lambda j, i: (i, j)),
        out_shape=jax.ShapeDtypeStruct((T, N), out_dtype),
        compiler_params=_cparams(("arbitrary", "arbitrary")),
    )(a, w)


def _mm_nt(dy, w, *, name, out_dtype=F32):
    T, N = dy.shape
    K = w.shape[0]
    tm = _pick(T, (512, 256, 128))
    tk = _pick(K, (1024, 1408, 896, 768, 512, 256, 128))

    def body(dy_ref, w_ref, o_ref):
        o_ref[...] = lax.dot_general(dy_ref[...], w_ref[...], (((1,), (1,)), ((), ())),
                                     preferred_element_type=F32).astype(out_dtype)

    return pl.pallas_call(
        body, name=name, grid=(K // tk, T // tm),
        in_specs=[pl.BlockSpec((tm, N), lambda j, i: (i, 0)), pl.BlockSpec((tk, N), lambda j, i: (j, 0))],
        out_specs=pl.BlockSpec((tm, tk), lambda j, i: (i, j)),
        out_shape=jax.ShapeDtypeStruct((T, K), out_dtype),
        compiler_params=_cparams(("arbitrary", "arbitrary")),
    )(dy, w)


def _mm_tn(a, dy, *, name, out_dtype=F32, col_shards=None):
    T, K = a.shape
    N = dy.shape[1]
    tm = _pick(T, (512, 256, 128))
    tn = N // col_shards if col_shards else _pick(N, (1024, 1408, 896, 768, 512, 256, 128))
    n_t = T // tm

    def body(a_ref, dy_ref, o_ref, acc_ref):
        i = pl.program_id(1)

        @pl.when(i == 0)
        def _():
            acc_ref[...] = jnp.zeros_like(acc_ref)

        acc_ref[...] += lax.dot_general(a_ref[...], dy_ref[...], (((0,), (0,)), ((), ())), preferred_element_type=F32)

        @pl.when(i == n_t - 1)
        def _():
            o_ref[...] = acc_ref[...].astype(out_dtype)

    if col_shards:
        out_specs = pl.BlockSpec((None, K, tn), lambda j, i: (j, 0, 0))
        out_shape = jax.ShapeDtypeStruct((col_shards, K, tn), out_dtype)
    else:
        out_specs = pl.BlockSpec((K, tn), lambda j, i: (0, j))
        out_shape = jax.ShapeDtypeStruct((K, N), out_dtype)
    return pl.pallas_call(
        body, name=name, grid=(N // tn, n_t),
        in_specs=[pl.BlockSpec((tm, K), lambda j, i: (i, 0)), pl.BlockSpec((tm, tn), lambda j, i: (i, j))],
        out_specs=out_specs, out_shape=out_shape, scratch_shapes=[pltpu.VMEM((K, tn), F32)],
        compiler_params=_cparams(("arbitrary", "arbitrary")),
    )(a, dy)


def _seg_ones(n):
    r = lax.broadcasted_iota(jnp.int32, (n, n), 0) // HD
    c = lax.broadcasted_iota(jnp.int32, (n, n), 1) // HD
    return (r == c).astype(F32)


def _segsum_raw(x):
    ones = _seg_ones(x.shape[1])
    if MXU == F32:
        return jnp.dot(x, ones, precision=HI, preferred_element_type=F32)
    hi = x.astype(MXU)
    lo = (x - hi.astype(F32)).astype(MXU)
    ones = ones.astype(MXU)
    return jnp.dot(hi, ones, preferred_element_type=F32) + jnp.dot(lo, ones, preferred_element_type=F32)


@jax.custom_vjp
def _segsum(x):
    return _segsum_raw(x)


def _segsum_fwd(x):
    return _segsum_raw(x), None


def _segsum_bwd(_, g):
    return (_segsum_raw(g),)


_segsum.defvjp(_segsum_fwd, _segsum_bwd)


def _mxu(x):
    return x.astype(MXU)


@jax.custom_vjp
def _bdot(a, b):
    return jnp.dot(_mxu(a), _mxu(b), preferred_element_type=F32)


def _bdot_fwd(a, b):
    return _bdot(a, b), (a, b)


def _bdot_bwd(res, g):
    a, b = res
    da = lax.dot_general(_mxu(g), _mxu(b), (((1,), (1,)), ((), ())), preferred_element_type=F32)
    db = lax.dot_general(_mxu(a), _mxu(g), (((0,), (0,)), ((), ())), preferred_element_type=F32)
    return da.astype(a.dtype), db.astype(b.dtype)


_bdot.defvjp(_bdot_fwd, _bdot_bwd)


@jax.custom_vjp
def _bdot_nt(a, b):
    return lax.dot_general(_mxu(a), _mxu(b), (((1,), (1,)), ((), ())), preferred_element_type=F32)


def _bdot_nt_fwd(a, b):
    return _bdot_nt(a, b), (a, b)


def _bdot_nt_bwd(res, g):
    a, b = res
    da = jnp.dot(_mxu(g), _mxu(b), preferred_element_type=F32)
    db = lax.dot_general(_mxu(g), _mxu(a), (((0,), (0,)), ((), ())), preferred_element_type=F32)
    return da.astype(a.dtype), db.astype(b.dtype)


_bdot_nt.defvjp(_bdot_nt_fwd, _bdot_nt_bwd)


def _sigmoid(x):
    return 1.0 / (1.0 + jnp.exp(-x))


def _softplus(x):
    return jnp.maximum(x, 0.0) + jnp.log(1.0 + jnp.exp(jnp.minimum(x, -x)))


def _norm_mod(x, gain, scale, shift):
    inv = lax.rsqrt(jnp.mean(x * x, axis=-1, keepdims=True) + RMS_EPS)
    return (x * inv) * gain * (1.0 + scale) + shift


def _prep(mixed, decay_w0, lora_up, iclr_a0, gate_up, k_k, k_a):
    r = mixed[:, 0:RW]
    k = mixed[:, RW:2 * RW]
    v = mixed[:, 2 * RW:3 * RW]
    z = mixed[:, 3 * RW:3 * RW + 128]
    xg = mixed[:, 3 * RW + 128:]
    lane = lax.broadcasted_iota(jnp.int32, z.shape, 1)
    tz = jnp.where(lane < 64, jnp.tanh(z), z)
    lo = _bdot(tz, lora_up)
    w_log = -_softplus(-(decay_w0 + lo[:, :RW])) - 0.5
    lw = -jnp.exp(w_log)
    a_ic = _sigmoid(iclr_a0 + lo[:, RW:])
    g = _bdot(_sigmoid(xg), gate_up)
    kk = k * k_k
    kk = kk / jnp.maximum(jnp.sqrt(_segsum(kk * kk)), 1e-12)
    k_mod = k * (1.0 + (a_ic - 1.0) * k_a)
    return jnp.concatenate([r, lw, k_mod, v, -kk, kk * a_ic, g], axis=1)


def _post(y, r, k, v, g, lnx_gain, lnx_bias, r_k):
    mu = _segsum(y) * (1.0 / HD)
    yc = y - mu
    var = _segsum(yc * yc) * (1.0 / HD)
    yn = yc * lax.rsqrt(var + GN_EPS) * lnx_gain + lnx_bias
    bonus = _segsum(r * k * r_k) * v
    return (yn + bonus) * g


def _merge(pg, ma, mb, bias):
    gates = _sigmoid(pg + bias)
    return gates[:, :D] * ma + gates[:, D:] * mb


def _swiglu(u, v):
    return u * _sigmoid(u) * v


@functools.partial(jax.custom_vjp, nondiff_argnums=(1,))
def _lane_roll(x, s):
    return pltpu.roll(x, s, 1)


def _lane_roll_fwd(x, s):
    return pltpu.roll(x, s, 1), None


def _lane_roll_bwd(s, _, g):
    n = g.shape[1]
    return (pltpu.roll(g, (n - s) % n, 1),)


_lane_roll.defvjp(_lane_roll_fwd, _lane_roll_bwd)


def _rope(x, cos, sin_lo, sin_hi):
    n = x.shape[1]
    return x * cos + _lane_roll(x, n - 8) * sin_lo + _lane_roll(x, 8) * sin_hi


def _head_rms(x, gain):
    return x * lax.rsqrt(_segsum(x * x) * (1.0 / HD) + RMS_EPS) * gain


def _attn_block(qkv_c, qkv_p, tab_c, tab_p, qg, kg, sinks, first):
    def tabs(tab, n):
        return tab[:, 0:n], tab[:, RW:RW + n], tab[:, 2 * RW:2 * RW + n]

    qg = jnp.concatenate([qg] * NH, axis=1)
    kg = jnp.concatenate([kg] * 2, axis=1)
    q = _rope(_head_rms(qkv_c[:, :RW], qg), *tabs(tab_c, RW))
    k_c = _rope(_head_rms(qkv_c[:, RW:RW + 128], kg), *tabs(tab_c, 128))
    k_p = _rope(_head_rms(qkv_p[:, RW:RW + 128], kg), *tabs(tab_p, 128))
    kband = jnp.concatenate([k_p, k_c], axis=0)
    vband = jnp.concatenate([qkv_p[:, RW + 128:], qkv_c[:, RW + 128:]], axis=0)
    G = 4
    qi = lax.broadcasted_iota(jnp.int32, (G * BLK, 2 * BLK), 0) % BLK
    kj = lax.broadcasted_iota(jnp.int32, (G * BLK, 2 * BLK), 1)
    dist = qi + BLK - kj
    valid = (dist >= 0) & (dist < BLK) & (jnp.logical_not(first) | (kj >= BLK))
    row_g = lax.broadcasted_iota(jnp.int32, (G * BLK, 1), 0) // BLK
    outs = []
    for kvh in range(2):
        kb = kband[:, kvh * HD:(kvh + 1) * HD]
        vb = vband[:, kvh * HD:(kvh + 1) * HD]
        qs = jnp.concatenate([q[:, (G * kvh + g) * HD:(G * kvh + g + 1) * HD] for g in range(G)], axis=0)
        s = _bdot_nt(qs, kb) * (HD ** -0.5)
        s = jnp.where(valid, s, NEG_INF)
        sink = jnp.zeros((G * BLK, 1), F32)
        for g in range(G):
            sink = jnp.where(row_g == g, sinks[:, G * kvh + g:G * kvh + g + 1], sink)
        m = lax.stop_gradient(jnp.maximum(jnp.max(s, axis=-1, keepdims=True), sink))
        e = jnp.exp(s - m)
        p = e / (jnp.sum(e, axis=-1, keepdims=True) + jnp.exp(sink - m))
        o = _bdot(p, vb)
        outs += [o[g * BLK:(g + 1) * BLK] for g in range(G)]
    return jnp.concatenate(outs, axis=1)


def _heads(x):
    return jnp.stack([x[:, h * HD:(h + 1) * HD] for h in range(NH)], axis=0)


def _unheads(x):
    return jnp.concatenate([x[h] for h in range(NH)], axis=1)


def _split(x, n):
    parts, rest = [], x
    for _ in range(n):
        p = rest.astype(MXU)
        parts.append(p)
        rest = rest - p.astype(F32)
    return parts


def _bdot_batched(a, b, ca, cb):
    return lax.dot_general(a, b, (((ca,), (cb,)), ((0,), (0,))), preferred_element_type=F32)


def _bmm_passes(a, b, ca, cb, passes):
    if MXU == F32:
        return lax.dot_general(a, b, (((ca,), (cb,)), ((0,), (0,))), precision=HI, preferred_element_type=F32)
    if passes == 1:
        return _bdot_batched(a.astype(MXU), b.astype(MXU), ca, cb)
    (a0, a1), (b0, b1) = _split(a, 2), _split(b, 2)
    return _bdot_batched(a0, b0, ca, cb) + (_bdot_batched(a0, b1, ca, cb) + _bdot_batched(a1, b0, ca, cb))


@functools.partial(jax.custom_vjp, nondiff_argnums=(2, 3, 4))
def _bmm(a, b, ca, cb, passes=1):
    return _bmm_passes(a, b, ca, cb, passes)


def _bmm_fwd(a, b, ca, cb, passes):
    return _bmm_passes(a, b, ca, cb, passes), (a, b)


def _bmm_bwd(ca, cb, passes, res, g):
    a, b = res
    if (ca, cb) == (2, 1):
        return _bmm_passes(g, b, 2, 2, passes), _bmm_passes(a, g, 1, 1, passes)
    if (ca, cb) == (2, 2):
        return _bmm_passes(g, b, 2, 1, passes), _bmm_passes(g, a, 1, 1, passes)
    return _bmm_passes(b, g, 2, 2, passes), _bmm_passes(a, g, 2, 1, passes)


_bmm.defvjp(_bmm_fwd, _bmm_bwd)


def _tri_dot(x, transpose):
    C = x.shape[1]
    ri = lax.broadcasted_iota(jnp.int32, (C, C), 0)
    ci = lax.broadcasted_iota(jnp.int32, (C, C), 1)
    tri = jnp.broadcast_to(((ri <= ci) if transpose else (ri >= ci)).astype(MXU), (x.shape[0], C, C))
    if MXU == F32:
        return lax.dot_general(tri, x, (((2,), (1,)), ((0,), (0,))), precision=HI, preferred_element_type=F32)
    p0, p1, p2 = _split(x, 3)
    return _bdot_batched(tri, p0, 2, 1) + (_bdot_batched(tri, p1, 2, 1) + _bdot_batched(tri, p2, 2, 1))


@jax.custom_vjp
def _cumsum_rows(x):
    return _tri_dot(x, False)


def _cumsum_rows_fwd(x):
    return _tri_dot(x, False), None


def _cumsum_rows_bwd(_, g):
    return (_tri_dot(g, True),)


_cumsum_rows.defvjp(_cumsum_rows_fwd, _cumsum_rows_bwd)

P_SCORE = 1
P_SOLVE = 1
P_STATE = 1


def _chunk(S0, r, lw, k, v, a, b):
    C = r.shape[1]
    ri = lax.broadcasted_iota(jnp.int32, (C, C), 0)
    ci = lax.broadcasted_iota(jnp.int32, (C, C), 1)
    incl = (ri >= ci)
    strict = (ri > ci)
    eye = (ri == ci).astype(F32)
    cum = _cumsum_rows(lw)
    p_in = jnp.exp(cum)
    p_ex = jnp.exp(cum - lw)
    p_inv = jnp.exp(-cum)
    at, rt, bt, kt = a * p_ex, r * p_in, b * p_inv, k * p_inv
    sc = _bmm(jnp.concatenate([at, rt], axis=1), jnp.concatenate([bt, kt], axis=1), 2, 2, P_SCORE)
    a_ab = jnp.where(strict, sc[:, :C, :C], 0.0)
    a_ak = jnp.where(strict, sc[:, :C, C:], 0.0)
    a_rb = jnp.where(incl, sc[:, C:, :C], 0.0)
    a_rk = jnp.where(incl, sc[:, C:, C:], 0.0)
    s0 = _bmm(jnp.concatenate([at, rt], axis=1), S0, 2, 2, P_STATE)
    rhs = s0[:, :C] + _bmm(a_ak, v, 2, 1, P_SCORE)
    x = eye + a_ab
    lp = a_ab
    for _ in range(int(math.log2(C)) - 1):
        lp = _bmm(lp, lp, 2, 1, P_SOLVE)
        x = x + _bmm(x, lp, 2, 1, P_SOLVE)
    u = _bmm(x, rhs, 2, 1, P_SOLVE)
    y = s0[:, C:] + _bmm(jnp.concatenate([a_rb, a_rk], axis=2), jnp.concatenate([u, v], axis=1), 2, 1, P_SCORE)
    p_last = jnp.exp(cum[:, C - 1:C, :])
    S1 = (S0 + _bmm(jnp.concatenate([u, v], axis=1), jnp.concatenate([bt, kt], axis=1), 1, 1, P_STATE)) * p_last
    return y, S1


def _hosting(body, ex, n_in, n_out, n_scratch, n_steps):
    if ex is None:
        return body

    def wrapped(*refs):
        a = n_in
        b = a + ex.n_in
        c = b + n_out
        d = c + ex.n_out
        e = d + n_scratch
        ex_refs = (refs[a:b], refs[c:d], refs[e:])
        i = pl.program_id(0)

        @pl.when(i == 0)
        def _():
            ex.start(*ex_refs)

        body(*refs[:a], *refs[b:c], *refs[d:e])

        @pl.when(i == n_steps - 1)
        def _():
            ex.wait(*ex_refs)

    return wrapped


def _hosted_args(ex):
    if ex is None:
        return [], [], [], [], []
    any_spec = pl.BlockSpec(memory_space=pl.ANY)
    return list(ex.arrays), [any_spec] * ex.n_in, [any_spec] * ex.n_out, list(ex.out_shape), list(ex.scratch)


def _scan_fwd(rw, *, name, hosted=None):
    T = rw.shape[0]
    n = T // CHUNK
    h_in, h_in_specs, h_out_specs, h_out_shape, h_scratch = _hosted_args(hosted)

    def body(r_ref, lw_ref, k_ref, v_ref, a_ref, b_ref, y_ref, ck_ref, s_ref):
        @pl.when(pl.program_id(0) == 0)
        def _():
            s_ref[...] = jnp.zeros_like(s_ref)

        S0 = s_ref[...]
        ck_ref[0] = S0
        y, S1 = _chunk(S0, *[_heads(ref[...]) for ref in (r_ref, lw_ref, k_ref, v_ref, a_ref, b_ref)])
        y_ref[...] = _unheads(y)
        s_ref[...] = S1

    col = lambda j: pl.BlockSpec((CHUNK, RW), lambda i: (i, j))
    return pl.pallas_call(
        _hosting(body, hosted, 6, 2, 1, n), name=name, grid=(n,),
        in_specs=[col(j) for j in range(6)] + h_in_specs,
        out_specs=[pl.BlockSpec((CHUNK, RW), lambda i: (i, 0)),
                   pl.BlockSpec((1, NH, HD, HD), lambda i: (i, 0, 0, 0))] + h_out_specs,
        out_shape=[jax.ShapeDtypeStruct((T, RW), F32), jax.ShapeDtypeStruct((n, NH, HD, HD), F32)] + h_out_shape,
        scratch_shapes=[pltpu.VMEM((NH, HD, HD), F32)] + h_scratch,
        compiler_params=_cparams(("arbitrary",)),
    )(rw, rw, rw, rw, rw, rw, *h_in)


def _scan_bwd(rw, ck, dy, *, name, hosted=None):
    T = rw.shape[0]
    n = T // CHUNK

    def body(r_ref, lw_ref, k_ref, v_ref, a_ref, b_ref, ck_ref, dy_ref, o_ref, ds_ref):
        @pl.when(pl.program_id(0) == 0)
        def _():
            ds_ref[...] = jnp.zeros_like(ds_ref)

        prim = [_heads(ref[...]) for ref in (r_ref, lw_ref, k_ref, v_ref, a_ref, b_ref)]
        _, vjp = jax.vjp(_chunk, ck_ref[0], *prim)
        grads = vjp((_heads(dy_ref[...]), ds_ref[...]))
        ds_ref[...] = grads[0]
        o_ref[...] = jnp.concatenate([_unheads(g) for g in grads[1:]], axis=1)

    h_in, h_in_specs, h_out_specs, h_out_shape, h_scratch = _hosted_args(hosted)
    col = lambda j: pl.BlockSpec((CHUNK, RW), lambda i: (n - 1 - i, j))
    return pl.pallas_call(
        _hosting(body, hosted, 8, 1, 1, n), name=name, grid=(n,),
        in_specs=[col(j) for j in range(6)] + [pl.BlockSpec((1, NH, HD, HD), lambda i: (n - 1 - i, 0, 0, 0)),
                                               pl.BlockSpec((CHUNK, RW), lambda i: (n - 1 - i, 0))] + h_in_specs,
        out_specs=[pl.BlockSpec((CHUNK, 6 * RW), lambda i: (n - 1 - i, 0))] + h_out_specs,
        out_shape=[jax.ShapeDtypeStruct((T, 6 * RW), F32)] + h_out_shape,
        scratch_shapes=[pltpu.VMEM((NH, HD, HD), F32)] + h_scratch,
        compiler_params=_cparams(("arbitrary",)),
    )(rw, rw, rw, rw, rw, rw, ck, dy, *h_in)


def _attn_fwd(qkv, tab, qg, kg, sinks, *, name, hosted=None):
    T = qkv.shape[0]
    n = T // BLK
    h_in, h_in_specs, h_out_specs, h_out_shape, h_scratch = _hosted_args(hosted)

    def body(c_ref, p_ref, tc_ref, tp_ref, qg_ref, kg_ref, s_ref, o_ref):
        first = pl.program_id(0) == 0
        o_ref[...] = _attn_block(c_ref[...], p_ref[...], tc_ref[...], tp_ref[...], qg_ref[...], kg_ref[...],
                                 s_ref[...], first).astype(o_ref.dtype)

    cur = lambda w: pl.BlockSpec((BLK, w), lambda i: (i, 0))
    prev = lambda w: pl.BlockSpec((BLK, w), lambda i: (jnp.maximum(i - 1, 0), 0))
    return pl.pallas_call(
        _hosting(body, hosted, 7, 1, 0, n), name=name, grid=(n,),
        in_specs=[cur(QKV_W), prev(QKV_W), cur(3 * RW), prev(3 * RW), _full_spec(qg), _full_spec(kg),
                  _full_spec(sinks)] + h_in_specs,
        out_specs=[cur(RW)] + h_out_specs, out_shape=[jax.ShapeDtypeStruct((T, RW), MXU)] + h_out_shape,
        scratch_shapes=h_scratch,
        compiler_params=_cparams(("arbitrary",)),
    )(qkv, qkv, tab, tab, qg, kg, sinks, *h_in)


def _attn_bwd(qkv, tab, qg, kg, sinks, dy, *, name, hosted=None):
    T = qkv.shape[0]
    n = T // BLK
    h_in, h_in_specs, h_out_specs, h_out_shape, h_scratch = _hosted_args(hosted)

    def body(c_ref, p_ref, tc_ref, tp_ref, qg_ref, kg_ref, s_ref, dy_ref, dqkv_ref, dqg_ref, dkg_ref, ds_ref, carry_ref):
        i = pl.program_id(0)
        first = i == n - 1

        @pl.when(i == 0)
        def _():
            carry_ref[...] = jnp.zeros_like(carry_ref)
            dqg_ref[...] = jnp.zeros_like(dqg_ref)
            dkg_ref[...] = jnp.zeros_like(dkg_ref)
            ds_ref[...] = jnp.zeros_like(ds_ref)

        tc, tp = tc_ref[...], tp_ref[...]
        f = lambda c, p, qg_, kg_, sk: _attn_block(c, p, tc, tp, qg_, kg_, sk, first)
        _, vjp = jax.vjp(f, c_ref[...], p_ref[...], qg_ref[...], kg_ref[...], s_ref[...])
        dc, dp, dqg, dkg, dsk = vjp(dy_ref[...].astype(F32))
        dqkv_ref[...] = (dc + carry_ref[...]).astype(dqkv_ref.dtype)
        carry_ref[...] = dp
        dqg_ref[...] += dqg
        dkg_ref[...] += dkg
        ds_ref[...] += dsk

    cur = lambda w: pl.BlockSpec((BLK, w), lambda i: (n - 1 - i, 0))
    prev = lambda w: pl.BlockSpec((BLK, w), lambda i: (jnp.maximum(n - 2 - i, 0), 0))
    return pl.pallas_call(
        _hosting(body, hosted, 8, 4, 1, n), name=name, grid=(n,),
        in_specs=[cur(QKV_W), prev(QKV_W), cur(3 * RW), prev(3 * RW), _full_spec(qg), _full_spec(kg), _full_spec(sinks),
                  cur(RW)] + h_in_specs,
        out_specs=[cur(QKV_W), _full_spec(qg), _full_spec(kg), _full_spec(sinks)] + h_out_specs,
        out_shape=[jax.ShapeDtypeStruct((T, QKV_W), MXU), jax.ShapeDtypeStruct(qg.shape, F32),
                   jax.ShapeDtypeStruct(kg.shape, F32), jax.ShapeDtypeStruct(sinks.shape, F32)] + h_out_shape,
        scratch_shapes=[pltpu.VMEM((BLK, QKV_W), F32)] + h_scratch,
        compiler_params=_cparams(("arbitrary",)),
    )(qkv, qkv, tab, tab, qg, kg, sinks, dy, *h_in)


def _shift_down(cur, prev8, i):
    rolled = pltpu.roll(cur, 1, 0)
    first_row = jnp.where(i > 0, prev8[7:8, :], 0.0)
    row = lax.broadcasted_iota(jnp.int32, cur.shape, 0)
    return jnp.where(row == 0, first_row, rolled)


def _shift_up(cur, next8, i, n):
    tm = cur.shape[0]
    rolled = pltpu.roll(cur, tm - 1, 0)
    last_row = jnp.where(i < n - 1, next8[0:1, :], 0.0)
    row = lax.broadcasted_iota(jnp.int32, cur.shape, 0)
    return jnp.where(row == tm - 1, last_row, rolled)


def _ada_parts(ada):
    return [ada[:, j * D:(j + 1) * D] for j in range(6)]


def _rope_table(positions):
    half = HD // 8
    inv_freq = 500000.0 ** (-jnp.arange(half, dtype=F32) / half)
    ang = positions.astype(F32)[:, None] * inv_freq
    cos, sin = jnp.cos(ang), jnp.sin(ang)
    T = positions.shape[0]
    pad = HD - 2 * half
    c64 = jnp.concatenate([cos, cos, jnp.ones((T, pad), F32)], axis=1)
    lo64 = jnp.concatenate([-sin, jnp.zeros((T, HD - half), F32)], axis=1)
    hi64 = jnp.concatenate([jnp.zeros((T, half), F32), sin, jnp.zeros((T, pad), F32)], axis=1)
    return jnp.concatenate([jnp.tile(t, (1, NH)) for t in (c64, lo64, hi64)], axis=1)


LATE_SCAN = ['ffn_w1', 'ffn_w3']
LATE_ATTN = ['ffn_w2', 'w_branch_a', 'w_branch_b', 'w_out']
BACK_ATTN = ['ffn_w1', 'ffn_w3', 'ffn_w2']
BACK_SCAN = ['w_out', 'w_branch_a', 'w_branch_b']


def _full_weight(g, ax):
    return g.reshape(-1, g.shape[2]) if ax == 0 else jnp.concatenate([g[j] for j in range(4)], axis=1)


def _local_step(x, target, ada, tab, w, s, shards=None):
    T = x.shape[0]
    tm = _pick(T, (512, 256, 128))
    tm_wide = _pick(T, (256, 128))
    tm_vjp = _pick(T, (128,))
    row = lambda n, dt=F32: (n, dt, 'row')
    acc = lambda n, r=1: (n, F32, r)

    def f_norm1(i, n, x_, g, ada_):
        sh, sc = ada_[:, 0:D], ada_[:, D:2 * D]
        return (_norm_mod(x_, g, sc, sh),)
    (h1,) = _rowwise(f_norm1, [x], [s['norm1_gain'], ada], [row(D, MXU)], tm=tm, name="f_norm1")
    proj = _mm_nn(h1, w['w_in'], name="f_proj")
    proj_qkv = proj[:, SHIFT_W:SHIFT_W + QKV_W]
    proj_g = proj[:, SHIFT_W + QKV_W:]
    prep_consts = [s['decay_w0'], s['lora_up'], s['iclr_a0'], s['gate_up'], s['k_k'], s['k_a']]

    def f_prep(i, n, cur, prev8, mu, *params):
        mixed = cur + (_shift_down(cur, prev8, i) - cur) * mu
        return (_prep(mixed, *params),)
    (rw,) = _rowwise(f_prep, [(proj, SHIFT_W)], [s['tshift_mu']] + prep_consts, [row(7 * RW)], tm=tm_wide,
                     name="f_prep", halo=[(proj, SHIFT_W, 'prev')])
    ax = dict(SHARDED)
    w = dict(w)
    y, ck, *got = _scan_fwd(rw, name="f_scan", hosted=shards and _GatherChips([shards[n] for n in LATE_SCAN]))
    if shards:
        w['ffn_w13'] = jnp.concatenate([_full_weight(g, ax[n]) for n, g in zip(LATE_SCAN, got, strict=True)], axis=1)
    post_consts = [s['lnx_gain'], s['lnx_bias'], s['r_k']]

    def post_of(y_, rw_, *params):
        return _post(y_, rw_[:, 0:RW], rw_[:, 2 * RW:3 * RW], rw_[:, 3 * RW:4 * RW], rw_[:, 6 * RW:7 * RW], *params)

    def f_post(i, n, y_, rw_, *params):
        return (post_of(y_, rw_, *params),)
    (ya,) = _rowwise(f_post, [y, rw], post_consts, [row(RW, MXU)], tm=tm_wide, name="f_post")
    yb, *got = _attn_fwd(proj_qkv, tab, s['q_norm_gain'], s['k_norm_gain'], s['attn_sinks'], name="f_attn",
                         hosted=shards and _GatherChips([shards[n] for n in LATE_ATTN]))
    if shards:
        w.update({n: _full_weight(g, ax[n]) for n, g in zip(LATE_ATTN, got, strict=True)})
    ma = _mm_nn(ya, w['w_branch_a'], name="f_branch_a")
    mb = _mm_nn(yb, w['w_branch_b'], name="f_branch_b")

    def f_merge(i, n, pg, ma_, mb_, bias):
        return (_merge(pg, ma_, mb_, bias),)
    (merged,) = _rowwise(f_merge, [proj_g, ma, mb], [s['branch_gate_b']], [row(D, MXU)], tm=tm, name="f_merge")
    mo = _mm_nn(merged, w['w_out'], name="f_out")

    def f_res1(i, n, x_, mo_, g, ada_):
        x1_ = x_ + ada_[:, 2 * D:3 * D] * mo_
        return x1_, _norm_mod(x1_, g, ada_[:, 4 * D:5 * D], ada_[:, 3 * D:4 * D])
    x1, h2 = _rowwise(f_res1, [x, mo], [s['norm2_gain'], ada], [row(D), row(D, MXU)], tm=tm, name="f_res1")
    uv = _mm_nn(h2, w['ffn_w13'], name="f_ffn_in")

    def f_act(i, n, uv_):
        return (_swiglu(uv_[:, :DFF], uv_[:, DFF:]),)
    (act,) = _rowwise(f_act, [uv], [], [row(DFF, MXU)], tm=tm_wide, name="f_act")
    ff = _mm_nn(act, w['ffn_w2'], name="f_ffn_out")

    def f_loss(i, n, x1_, ff_, tgt, ada_):
        g2 = ada_[:, 5 * D:6 * D]
        err = x1_ + g2 * ff_ - tgt
        dx2 = err * (1.0 / D)
        loss = 0.5 * jnp.sum(jnp.sum(err * err, axis=1, keepdims=True) * (1.0 / D), axis=0, keepdims=True)
        return dx2, (dx2 * g2), jnp.broadcast_to(loss, (1, 128)), jnp.sum(dx2 * ff_, axis=0, keepdims=True)
    dx2, dff, loss, dgate2 = _rowwise(f_loss, [x1, ff, target], [ada], [row(D), row(D, MXU), acc(128), acc(D)],
                                      tm=tm, name="f_loss")

    dact = _mm_nt(dff, w['ffn_w2'], name="b_ffn_out_dx")
    g_w2 = _mm_tn(act, dff, name="b_ffn_out_dw", out_dtype=MXU)

    def b_act(i, n, uv_, dact_):
        _, vjp = jax.vjp(_swiglu, uv_[:, :DFF], uv_[:, DFF:])
        du, dv = vjp(dact_)
        return (jnp.concatenate([du, dv], axis=1),)
    (duv,) = _rowwise(b_act, [uv, dact], [], [row(2 * DFF, MXU)], tm=tm_wide, name="b_act")
    dh2 = _mm_nt(duv, w['ffn_w13'], name="b_ffn_in_dx")
    g_w13 = _mm_tn(h2, duv, name="b_ffn_in_dw", out_dtype=MXU)

    def b_res1(i, n, x1_, dh2_, dx2_, mo_, g, ada_):
        _, vjp = jax.vjp(_norm_mod, x1_, g, ada_[:, 4 * D:5 * D], ada_[:, 3 * D:4 * D])
        dxn, dg, dsc, dsh = vjp(dh2_)
        dx1_ = dxn + dx2_
        g1 = ada_[:, 2 * D:3 * D]
        return dx1_, dx1_ * g1, dg, dsc, dsh, jnp.sum(dx1_ * mo_, axis=0, keepdims=True)
    dx1, dmo, d_gain2, d_scale2, d_shift2, dgate1 = _rowwise(
        b_res1, [x1, dh2, dx2, mo], [s['norm2_gain'], ada], [row(D), row(D, MXU), acc(D), acc(D), acc(D), acc(D)],
        tm=tm, name="b_res1")
    dmerged = _mm_nt(dmo, w['w_out'], name="b_out_dx")
    g_wout = _mm_tn(merged, dmo, name="b_out_dw", out_dtype=MXU)

    def b_merge(i, n, pg, ma_, mb_, dm, bias):
        _, vjp = jax.vjp(_merge, pg, ma_, mb_, bias)
        dpg, dma_, dmb_, dbias = vjp(dm)
        return dpg, dma_, dmb_, dbias
    dpg, dma, dmb, d_bias = _rowwise(b_merge, [proj_g, ma, mb, dmerged], [s['branch_gate_b']],
                                     [row(GATE_W, MXU), row(D, MXU), row(D, MXU), acc(GATE_W)], tm=tm_wide, name="b_merge")
    dya = _mm_nt(dma, w['w_branch_a'], name="b_branch_a_dx")
    g_wa = _mm_tn(ya, dma, name="b_branch_a_dw", out_dtype=MXU, col_shards=4)
    dyb = _mm_nt(dmb, w['w_branch_b'], name="b_branch_b_dx", out_dtype=F32)
    g_wb = _mm_tn(yb, dmb, name="b_branch_b_dw", out_dtype=MXU, col_shards=4)
    fs = DFF // 4
    gw = dict(w_branch_a=g_wa, w_branch_b=g_wb, w_out=g_wout.reshape(4, D // 4, D),
              ffn_w1=jnp.stack([g_w13[:, j * fs:(j + 1) * fs] for j in range(4)]),
              ffn_w3=jnp.stack([g_w13[:, DFF + j * fs:DFF + (j + 1) * fs] for j in range(4)]),
              ffn_w2=g_w2.reshape(4, fs, D))
    recv = {}
    dqkv, d_qg, d_kg, d_sinks, *got = _attn_bwd(
        proj_qkv, tab, s['q_norm_gain'], s['k_norm_gain'], s['attn_sinks'], dyb, name="b_attn",
        hosted=shards and _ScatterChips([gw[n] for n in BACK_ATTN]))
    recv.update(zip(BACK_ATTN, got))

    def b_post(i, n, y_, rw_, dya_, *params):
        _, vjp = jax.vjp(post_of, y_, rw_, *params)
        return vjp(dya_)
    dy, drw_post, d_lnx_gain, d_lnx_bias, d_r_k = _rowwise(
        b_post, [y, rw, dya], post_consts, [row(RW), row(7 * RW), acc(RW), acc(RW), acc(RW)], tm=tm_vjp, name="b_post")
    dscan, *got = _scan_bwd(rw, ck, dy, name="b_scan",
                            hosted=shards and _ScatterChips([gw[n] for n in BACK_SCAN]))
    recv.update(zip(BACK_SCAN, got))

    def b_prep(i, n, cur, drw_, dscan_, prev8, mu, *params):
        shifted = _shift_down(cur, prev8, i)
        mixed = cur + (shifted - cur) * mu
        _, vjp = jax.vjp(_prep, mixed, *params)
        ct = drw_ + jnp.concatenate([dscan_, jnp.zeros((dscan_.shape[0], RW), F32)], axis=1)
        grads = vjp(ct)
        dmixed = grads[0]
        return (dmixed, jnp.sum(dmixed * (shifted - cur), axis=0, keepdims=True)) + tuple(grads[1:])
    dmixed, d_mu, d_w0, d_lora, d_a0, d_gate_up, d_kk, d_ka = _rowwise(
        b_prep, [(proj, SHIFT_W), drw_post, dscan], [s['tshift_mu']] + prep_consts,
        [row(SHIFT_W), acc(SHIFT_W), acc(RW), acc(2 * RW, 128), acc(RW), acc(RW, 128), acc(RW), acc(RW)],
        tm=tm_vjp, name="b_prep", halo=[(proj, SHIFT_W, 'prev')])

    def b_gather(i, n, dm, dqkv_, dpg_, next8, mu):
        dcur = dm * (1.0 - mu) + _shift_up(dm, next8, i, n) * mu
        return (jnp.concatenate([dcur.astype(MXU), dqkv_, dpg_], axis=1),)
    (dproj,) = _rowwise(b_gather, [dmixed, dqkv, dpg], [s['tshift_mu']], [row(IN_W, MXU)], tm=tm_wide, name="b_gather",
                        halo=[(dmixed, SHIFT_W, 'next')])
    dh1 = _mm_nt(dproj, w['w_in'], name="b_proj_dx")
    g_win = _mm_tn(h1, dproj, name="b_proj_dw", out_dtype=MXU, col_shards=4)

    def b_norm1(i, n, x_, dh1_, dx1_, g, ada_):
        _, vjp = jax.vjp(_norm_mod, x_, g, ada_[:, D:2 * D], ada_[:, 0:D])
        dxn, dg, dsc, dsh = vjp(dh1_)
        return dxn + dx1_, dg, dsc, dsh
    dx, d_gain1, d_scale1, d_shift1 = _rowwise(b_norm1, [x, dh1, dx1], [s['norm1_gain'], ada],
                                               [row(D), acc(D), acc(D), acc(D)], tm=tm, name="b_norm1")

    d_ada = jnp.concatenate([d_shift1, d_scale1, dgate1, d_shift2, d_scale2, dgate2], axis=1)
    def col_blocks(g):
        k, n = g.shape
        return g.reshape(k, 4, n // 4).transpose(1, 0, 2).astype(MXU)
    gw.update(w_in=g_win, decay_up=col_blocks(d_lora[:64, :RW]), iclr_up=col_blocks(d_lora[64:, RW:]),
              gate_up=col_blocks(d_gate_up))
    gs = dict(norm1_gain=d_gain1, norm2_gain=d_gain2, tshift_mu=d_mu, decay_w0=d_w0, iclr_a0=d_a0, k_k=d_kk, k_a=d_ka,
              r_k=d_r_k, lnx_gain=d_lnx_gain, lnx_bias=d_lnx_bias, q_norm_gain=d_qg, k_norm_gain=d_kg,
              attn_sinks=d_sinks, branch_gate_b=d_bias)
    return loss, dx, d_ada, gw, gs, recv


ANY = pl.BlockSpec(memory_space=pl.ANY)


def _place():
    x, y, c = lax.axis_index("x"), lax.axis_index("y"), lax.axis_index("c")
    return x, y, c, [(1 - x, y), (x, 1 - y), (1 - x, 1 - y)]


def _all_gather8(x_shard, *, name):
    m_per, n = x_shard.shape

    def body(x_ref, out_ref, send_sems, recv_sems, local_sem):
        x, y, c, chips = _place()
        me, sibling = (x, y, c), (x, y, 1 - c)

        def rows(px, py, pc):
            return out_ref.at[pl.ds((4 * px + 2 * py + pc) * m_per, m_per), :]

        def copy(k, block, to, src=None):
            return pltpu.make_async_remote_copy(
                src_ref=rows(*block) if src is None else src, dst_ref=rows(*block),
                send_sem=send_sems.at[k], recv_sem=recv_sems.at[k], device_id=to, device_id_type=MESH)

        mine = pltpu.make_async_copy(x_ref, rows(*me), local_sem)
        mine.start()
        first = [copy(0, me, sibling, src=x_ref)]
        first += [copy(1 + j, me, (*chip, c), src=x_ref) for j, chip in enumerate(chips)]
        for cp in first:
            cp.start()
        passed = [copy(4 + j, (*chip, c), sibling) for j, chip in enumerate(chips)]
        for j, chip in enumerate(chips):
            copy(1 + j, (*chip, c), me).wait_recv()
            passed[j].start()
        copy(0, sibling, me).wait_recv()
        for j, chip in enumerate(chips):
            copy(4 + j, (*chip, 1 - c), me).wait_recv()
        for cp in first + passed:
            cp.wait_send()
        mine.wait()

    return pl.pallas_call(
        body, name=name, out_shape=jax.ShapeDtypeStruct((8 * m_per, n), x_shard.dtype),
        in_specs=[pl.BlockSpec(memory_space=pltpu.VMEM)], out_specs=pl.BlockSpec(memory_space=pltpu.VMEM),
        scratch_shapes=[pltpu.SemaphoreType.DMA((7,)), pltpu.SemaphoreType.DMA((7,)), pltpu.SemaphoreType.DMA],
    )(x_shard)


class _GatherChips:
    def __init__(self, shards):
        n = len(shards)
        self.arrays, self.n_in, self.n_out = list(shards), n, n
        self.out_shape = [jax.ShapeDtypeStruct((4,) + s.shape, s.dtype) for s in shards]
        self.scratch = [pltpu.SemaphoreType.DMA((3 * n,)), pltpu.SemaphoreType.DMA((3 * n,)),
                        pltpu.SemaphoreType.DMA((n,))]

    def _copies(self, x_refs, out_refs, sems, receiving):
        send_sems, recv_sems, local_sems = sems
        x, y, c, chips = _place()
        s_me = 2 * x + y
        n = self.n_in

        def copy(a, k, s):
            return pltpu.make_async_remote_copy(
                src_ref=x_refs[a], dst_ref=out_refs[a].at[s], send_sem=send_sems.at[3 * a + k],
                recv_sem=recv_sems.at[3 * a + k], device_id=(*chips[k], c), device_id_type=MESH)

        mine = [pltpu.make_async_copy(x_refs[a], out_refs[a].at[s_me], local_sems.at[a]) for a in range(n)]
        sends = [copy(a, k, s_me) for a in range(n) for k in range(3)]
        if not receiving:
            return mine, sends
        return mine, sends, [copy(a, k, 2 * px + py) for a in range(n) for k, (px, py) in enumerate(chips)]

    def start(self, x_refs, out_refs, sems):
        mine, sends = self._copies(x_refs, out_refs, sems, False)
        for cp in mine + sends:
            cp.start()

    def wait(self, x_refs, out_refs, sems):
        mine, sends, recvs = self._copies(x_refs, out_refs, sems, True)
        for cp in recvs:
            cp.wait_recv()
        for cp in sends:
            cp.wait_send()
        for cp in mine:
            cp.wait()


class _ScatterChips:
    def __init__(self, parts):
        n = len(parts)
        self.arrays, self.n_in, self.n_out = list(parts), n, n
        self.out_shape = [jax.ShapeDtypeStruct((3,) + p.shape[1:], p.dtype) for p in parts]
        self.scratch = [pltpu.SemaphoreType.DMA((3 * n,)), pltpu.SemaphoreType.DMA((3 * n,))]

    def _copies(self, g_refs, out_refs, sems):
        send_sems, recv_sems = sems
        x, y, c, chips = _place()
        return [pltpu.make_async_remote_copy(
            src_ref=g_refs[a].at[2 * px + py], dst_ref=out_refs[a].at[k], send_sem=send_sems.at[3 * a + k],
            recv_sem=recv_sems.at[3 * a + k], device_id=(px, py, c), device_id_type=MESH)
            for a in range(self.n_in) for k, (px, py) in enumerate(chips)]

    def start(self, g_refs, out_refs, sems):
        for cp in self._copies(g_refs, out_refs, sems):
            cp.start()

    def wait(self, g_refs, out_refs, sems):
        sends = self._copies(g_refs, out_refs, sems)
        for cp in sends:
            cp.wait_recv()
        for cp in sends:
            cp.wait_send()


def _exchange_call(ex, *, name):
    def body(*refs):
        parts = (refs[:ex.n_in], refs[ex.n_in:ex.n_in + ex.n_out], refs[ex.n_in + ex.n_out:])
        ex.start(*parts)
        ex.wait(*parts)

    return pl.pallas_call(body, name=name, out_shape=ex.out_shape, in_specs=[ANY] * ex.n_in,
                          out_specs=[ANY] * ex.n_out, scratch_shapes=ex.scratch)(*ex.arrays)


def _swap_sibling(vs, *, name):
    n = len(vs)

    def body(*refs):
        v_refs, out_refs = refs[:n], refs[n:2 * n]
        send_sems, recv_sems = refs[2 * n:]
        x, y, c, _ = _place()
        cps = [pltpu.make_async_remote_copy(src_ref=v_refs[a], dst_ref=out_refs[a], send_sem=send_sems.at[a],
                                            recv_sem=recv_sems.at[a], device_id=(x, y, 1 - c), device_id_type=MESH)
               for a in range(n)]
        for cp in cps:
            cp.start()
        for cp in cps:
            cp.wait()

    return pl.pallas_call(
        body, name=name, out_shape=[jax.ShapeDtypeStruct(v.shape, v.dtype) for v in vs],
        in_specs=[ANY] * n, out_specs=[ANY] * n,
        scratch_shapes=[pltpu.SemaphoreType.DMA((n,)), pltpu.SemaphoreType.DMA((n,))],
    )(*vs)


def _sum_parts(own, others, *, name):
    R, C = own.shape
    tm = _pick(R, (256, 128, 64))

    def body(own_ref, o0_ref, o1_ref, o2_ref, out_ref):
        tot = own_ref[...].astype(F32)
        for ref in (o0_ref, o1_ref, o2_ref):
            tot = tot + ref[...].astype(F32)
        out_ref[...] = tot

    part = lambda k: pl.BlockSpec((None, tm, C), lambda i: (k, i, 0))
    return pl.pallas_call(
        body, name=name, grid=(R // tm,),
        in_specs=[pl.BlockSpec((tm, C), lambda i: (i, 0)), part(0), part(1), part(2)],
        out_specs=pl.BlockSpec((tm, C), lambda i: (i, 0)), out_shape=jax.ShapeDtypeStruct((R, C), F32),
        compiler_params=_cparams(("arbitrary",)),
    )(own, others, others, others)


def _adam_math(w_, m_, v_, g):
    m2 = ADAM_B1 * m_ + (1.0 - ADAM_B1) * g
    v2 = ADAM_B2 * v_ + (1.0 - ADAM_B2) * jnp.square(g)
    m_hat = m2 / (1.0 - ADAM_B1 ** ADAM_STEP)
    v_hat = v2 / (1.0 - ADAM_B2 ** ADAM_STEP)
    delta = -ADAM_LR * (m_hat / (jnp.sqrt(v_hat) + ADAM_EPS) + ADAM_WD * w_)
    return delta, m2, v2


SMALL_SLOTS = 16
SMALL_COLS = 6 * D


def _pack_small(grads, *, name):
    n = len(grads)

    def body(*refs):
        out_ref = refs[n]
        out_ref[...] = jnp.zeros_like(out_ref)
        for i, ref in enumerate(refs[:n]):
            out_ref[i:i + 1, 0:ref.shape[1]] = ref[...]

    return pl.pallas_call(body, name=name, out_shape=jax.ShapeDtypeStruct((SMALL_SLOTS, SMALL_COLS), F32))(*grads)


def _adamw_small(ws, ms, vs, gathered, *, name):
    n = len(ws)

    def body(*refs):
        w_refs, m_refs, v_refs, g_ref = refs[:n], refs[n:2 * n], refs[2 * n:3 * n], refs[3 * n]
        outs = refs[3 * n + 1:]
        for i in range(n):
            nc = w_refs[i].shape[1]
            g = g_ref[i:i + 1, 0:nc]
            for d in range(1, 8):
                g = g + g_ref[d * SMALL_SLOTS + i:d * SMALL_SLOTS + i + 1, 0:nc]
            delta, m2, v2 = _adam_math(w_refs[i][...], m_refs[i][...], v_refs[i][...], g)
            for k, val in enumerate((g, delta, m2, v2)):
                outs[k * n + i][...] = val

    shapes = [jax.ShapeDtypeStruct(w.shape, F32) for w in ws]
    res = pl.pallas_call(body, name=name, out_shape=shapes * 4,
                         compiler_params=pltpu.CompilerParams(vmem_limit_bytes=VMEM_LIMIT))(*ws, *ms, *vs, gathered)
    return [res[k * n:(k + 1) * n] for k in range(4)]


def _adamw(w, m, v, gparts, *, tm, name):
    def fn(i, n, w_, m_, v_, *gs):
        g = gs[0]
        for p in gs[1:]:
            g = g + p
        return (g,) + _adam_math(w_, m_, v_, g)
    nc = w.shape[1]
    return _rowwise(fn, [w, m, v] + list(gparts), [], [(nc, F32, 'row')] * 4, tm=tm, name=name)


WEIGHTS = ['ada_w', 'ada_b', 'norm1_gain', 'norm2_gain', 'w_in', 'tshift_mu', 'decay_w0', 'decay_up', 'iclr_a0',
           'iclr_up', 'gate_up', 'k_k', 'k_a', 'r_k', 'lnx_gain', 'lnx_bias', 'q_norm_gain', 'k_norm_gain', 'attn_sinks',
           'branch_gate_b', 'w_branch_a', 'w_branch_b', 'w_out', 'ffn_w1', 'ffn_w3', 'ffn_w2']
SHARDED = [('w_in', 1), ('decay_up', 1), ('iclr_up', 1), ('gate_up', 1), ('w_branch_a', 1), ('w_branch_b', 1),
           ('w_out', 0), ('ffn_w1', 1), ('ffn_w3', 1), ('ffn_w2', 0)]
SMALL = ['ada_b', 'norm1_gain', 'norm2_gain', 'tshift_mu', 'decay_w0', 'iclr_a0', 'k_k', 'k_a', 'r_k', 'lnx_gain',
         'lnx_bias', 'q_norm_gain', 'k_norm_gain', 'attn_sinks', 'branch_gate_b']


def kernel(x, c, positions, ada_w, ada_b, norm1_gain, norm2_gain, w_in, tshift_mu, decay_w0, decay_up, iclr_a0, iclr_up, gate_up, k_k, k_a, r_k, lnx_gain, lnx_bias, q_norm_gain, k_norm_gain, attn_sinks, branch_gate_b, w_branch_a, w_branch_b, w_out, ffn_w1, ffn_w3, ffn_w2, loss_target, m_ada_w, m_ada_b, m_norm1_gain, m_norm2_gain, m_w_in, m_tshift_mu, m_decay_w0, m_decay_up, m_iclr_a0, m_iclr_up, m_gate_up, m_k_k, m_k_a, m_r_k, m_lnx_gain, m_lnx_bias, m_q_norm_gain, m_k_norm_gain, m_attn_sinks, m_branch_gate_b, m_w_branch_a, m_w_branch_b, m_w_out, m_ffn_w1, m_ffn_w3, m_ffn_w2, v_ada_w, v_ada_b, v_norm1_gain, v_norm2_gain, v_w_in, v_tshift_mu, v_decay_w0, v_decay_up, v_iclr_a0, v_iclr_up, v_gate_up, v_k_k, v_k_a, v_r_k, v_lnx_gain, v_lnx_bias, v_q_norm_gain, v_k_norm_gain, v_attn_sinks, v_branch_gate_b, v_w_branch_a, v_w_branch_b, v_w_out, v_ffn_w1, v_ffn_w3, v_ffn_w2):
    a = dict(locals())
    W = {n: a[n] for n in WEIGHTS}
    M = {n: a['m_' + n] for n in WEIGHTS}
    V = {n: a['v_' + n] for n in WEIGHTS}
    xi, yi, ci = lax.axis_index("x"), lax.axis_index("y"), lax.axis_index("c")
    me = 4 * xi + 2 * yi + ci
    shard = 2 * xi + yi
    mat = lambda t: t.reshape(t.shape[-2], t.shape[-1])
    sharded = [n for n, _ in SHARDED]

    ax = dict(SHARDED)
    late = LATE_SCAN + LATE_ATTN
    early = [n for n in sharded if n not in late]
    shards = {n: mat(W[n]).astype(MXU) for n in sharded}
    gathered = _exchange_call(_GatherChips([shards[n] for n in early]), name="gather_weights")
    full = {n: _full_weight(g, ax[n]) for n, g in zip(early, gathered, strict=True)}

    c_all = _all_gather8(jnp.broadcast_to(c, (8, D)), name="gather_c")[0::8]
    pad_rows = lambda t: jnp.concatenate([t, jnp.zeros((BLK - 8, t.shape[1]), t.dtype)])
    c_all = pad_rows(c_all.astype(MXU))
    ada_cols = _mm_nn(c_all, mat(ada_w).astype(MXU), name="f_ada")[:8]
    ada_all = _all_gather8(ada_cols, name="gather_ada").reshape(2, 2, 2, 8, 6 * D // 4)
    ada_mine = lax.dynamic_index_in_dim(ada_all[:, :, 0], me, axis=2, keepdims=False)
    ada = ada_mine.reshape(1, 6 * D) + mat(ada_b)

    zero = jnp.zeros((64, RW), MXU)
    lora = jnp.concatenate([jnp.concatenate([full['decay_up'], zero], axis=1),
                            jnp.concatenate([zero, full['iclr_up']], axis=1)], axis=0)
    s = {n: W[n].reshape(1, -1) for n in SMALL if n != 'ada_b'}
    s['lora_up'] = lora.astype(F32)
    s['gate_up'] = full['gate_up'].astype(F32)
    tab = _rope_table(positions.reshape(-1))
    loss, dx, d_ada, gw, gs, from_chips = _local_step(x[0], loss_target[0], ada, tab, dict(w_in=full['w_in']), s,
                                                      shards={n: shards[n] for n in late})
    loss = lax.psum(loss[0, 0], ("x", "y", "c"))

    gs['ada_b'] = d_ada
    gsmall = _pack_small([gs[n] for n in SMALL], name="pack_small_grads")
    gsmall_all = _all_gather8(gsmall, name="gather_small_grads")
    row = lambda src: [src[n].reshape(1, -1) for n in SMALL]
    sm_out = _adamw_small(row(W), row(M), row(V), gsmall_all, name="adamw_small")
    sm_out = [{n: o.reshape(W[n].shape) for n, o in zip(SMALL, outs_k, strict=True)} for outs_k in sm_out]

    d_ada_all = gsmall_all[0::SMALL_SLOTS]
    d_ada_cols = lax.dynamic_slice_in_dim(d_ada_all, shard * (6 * D // 4), 6 * D // 4, axis=1)
    g_ada_w = _mm_tn(c_all, pad_rows(d_ada_cols.astype(MXU)), name="b_ada")
    ada_out = _adamw(mat(ada_w), mat(m_ada_w), mat(v_ada_w), [g_ada_w], tm=256, name="adamw_ada")

    last = [n for n in sharded if n not in from_chips]
    from_chips.update(zip(last, _exchange_call(_ScatterChips([gw[n] for n in last]), name="scatter_grads")))
    parts = [_sum_parts(lax.dynamic_index_in_dim(gw[n], shard, axis=0, keepdims=False), from_chips[n], name="sum_" + n)
             for n in sharded]
    others = _swap_sibling(parts, name="swap_grads")
    sh_out = {}
    for n, part, other in zip(sharded, parts, others, strict=True):
        sh_out[n] = _adamw(mat(W[n]), mat(M[n]), mat(V[n]), [part, other], tm=_pick(part.shape[0], (256, 128, 64)),
                           name="adamw_" + n)

    def leaf(k, n):
        if n == 'ada_w':
            return ada_out[k].reshape(W[n].shape)
        if n in sharded:
            return sh_out[n][k].reshape(W[n].shape)
        return sm_out[k][n]
    outs = [leaf(k, n) for k in range(4) for n in WEIGHTS]
    return (loss, dx[None], *outs)
```

```python
import functools
import math

import jax
import jax.numpy as jnp
from jax import lax
from jax.experimental import pallas as pl
from jax.experimental.pallas import tpu as pltpu

F32 = jnp.float32
BF16 = jnp.bfloat16
MXU = BF16
HI = lax.Precision.HIGHEST

D = 1024
HD = 64
NH = 8
RW = NH * HD
SHIFT_W = 3 * RW + 64 + 64 + 128
QKV_W = RW + 2 * 128
GATE_W = 2 * D
IN_W = SHIFT_W + QKV_W + GATE_W
DFF = 2816
BLK = 128
CHUNK = 64
RMS_EPS = 1e-6
GN_EPS = 64e-5
NEG_INF = -1e30
ADAM_LR, ADAM_B1, ADAM_B2, ADAM_EPS, ADAM_WD, ADAM_STEP = 0.001, 0.9, 0.999, 1e-08, 0.01, 10
VMEM_LIMIT = 56 * 1024 * 1024
MESH = pl.DeviceIdType.MESH


def _cparams(sem=None):
    return pltpu.CompilerParams(dimension_semantics=sem, vmem_limit_bytes=VMEM_LIMIT)


def _full_spec(a):
    nd = a.ndim
    return pl.BlockSpec(a.shape, lambda *_: (0,) * nd)


def _rowwise(fn, rows, consts, outs, *, tm, name, halo=()):
    rows = [a if isinstance(a, tuple) else (a, a.shape[1]) for a in rows]
    T = rows[0][0].shape[0]
    assert T % tm == 0 and tm % 8 == 0
    n_tiles = T // tm
    n_in = len(rows) + len(halo) + len(consts)
    in_specs = [pl.BlockSpec((tm, nc), lambda i: (i, 0)) for _, nc in rows]
    args = [a for a, _ in rows]
    for a, nc, kind in halo:
        if kind == 'prev':
            in_specs.append(pl.BlockSpec((8, nc), lambda i: (jnp.maximum(i * (tm // 8) - 1, 0), 0)))
        else:
            in_specs.append(pl.BlockSpec((8, nc), lambda i: (jnp.minimum((i + 1) * (tm // 8), T // 8 - 1), 0)))
        args.append(a)
    in_specs += [_full_spec(a) for a in consts]
    args += list(consts)
    out_shape, out_specs = [], []
    for ncols, dtype, kind in outs:
        if kind == 'row':
            out_shape.append(jax.ShapeDtypeStruct((T, ncols), dtype))
            out_specs.append(pl.BlockSpec((tm, ncols), lambda i: (i, 0)))
        else:
            out_shape.append(jax.ShapeDtypeStruct((kind, ncols), dtype))
            out_specs.append(pl.BlockSpec((kind, ncols), lambda i: (0, 0)))

    def body(*refs):
        i = pl.program_id(0)
        vals = [r[...] for r in refs[:n_in]]
        res = fn(i, n_tiles, *vals)
        for (ncols, dtype, kind), o_ref, val in zip(outs, refs[n_in:], res, strict=True):
            if kind == 'row':
                o_ref[...] = val.astype(dtype)
            else:
                @pl.when(i == 0)
                def _():
                    o_ref[...] = jnp.zeros_like(o_ref)
                o_ref[...] += val.astype(dtype)

    res = pl.pallas_call(
        body, name=name, grid=(n_tiles,), in_specs=in_specs, out_specs=out_specs, out_shape=out_shape,
        compiler_params=_cparams(("arbitrary",)),
    )(*args)
    return res


def _pick(n, cands):
    for c in cands:
        if n % c == 0:
            return c
    return n


MM_ROWS = (1024, 512, 256, 128)
MM_COLS = (1536, 1408, 1024, 896, 768, 512, 256, 128)
MM_WIDE = 3000


def _mm_nn(a, w, *, name, out_dtype=F32):
    T, K = a.shape
    N = w.shape[1]
    tm = _pick(T, MM_ROWS)
    tn = _pick(N, MM_COLS)

    def body(a_ref, w_ref, o_ref):
        o_ref[...] = jnp.dot(a_ref[...], w_ref[...], preferred_element_type=F32).astype(out_dtype)

    return pl.pallas_call(
        body, name=name, grid=(N // tn, T // tm),
        in_specs=[pl.BlockSpec((tm, K), lambda j, i: (i, 0)), pl.BlockSpec((K, tn), lambda j, i: (0, j))],
        out_specs=pl.BlockSpec((tm, tn), lambda j, i: (i, j)),
        out_shape=jax.ShapeDtypeStruct((T, N), out_dtype),
        compiler_params=_cparams(("arbitrary", "arbitrary")),
    )(a, w)


def _mm_nt(dy, w, *, name, out_dtype=F32, hosted=None):
    T, N = dy.shape
    K = w.shape[0]
    tm = _pick(T, MM_ROWS if N <= MM_WIDE else MM_ROWS[1:])
    tk = _pick(K, MM_COLS[1:])
    grid = (K // tk, T // tm)
    h_in, h_in_specs, h_out_specs, h_out_shape, h_scratch = _hosted_args(hosted)

    def body(dy_ref, w_ref, o_ref):
        o_ref[...] = lax.dot_general(dy_ref[...], w_ref[...], (((1,), (1,)), ((), ())),
                                     preferred_element_type=F32).astype(out_dtype)

    res = pl.pallas_call(
        _hosting(body, hosted, 2, 1, 0, grid), name=name, grid=grid,
        in_specs=[pl.BlockSpec((tm, N), lambda j, i: (i, 0)), pl.BlockSpec((tk, N), lambda j, i: (j, 0))] + h_in_specs,
        out_specs=[pl.BlockSpec((tm, tk), lambda j, i: (i, j))] + h_out_specs,
        out_shape=[jax.ShapeDtypeStruct((T, K), out_dtype)] + h_out_shape, scratch_shapes=h_scratch,
        compiler_params=_cparams(("arbitrary", "arbitrary")),
    )(dy, w, *h_in)
    return res if hosted else res[0]


def _mm_tn(a, dy, *, name, out_dtype=F32, col_shards=None):
    T, K = a.shape
    N = dy.shape[1]
    tm = _pick(T, MM_ROWS)
    tn = N // col_shards if col_shards else _pick(N, MM_COLS[1:])
    n_t = T // tm

    def body(a_ref, dy_ref, o_ref, acc_ref):
        i = pl.program_id(1)

        @pl.when(i == 0)
        def _():
            acc_ref[...] = jnp.zeros_like(acc_ref)

        acc_ref[...] += lax.dot_general(a_ref[...], dy_ref[...], (((0,), (0,)), ((), ())), preferred_element_type=F32)

        @pl.when(i == n_t - 1)
        def _():
            o_ref[...] = acc_ref[...].astype(out_dtype)

    if col_shards:
        out_specs = pl.BlockSpec((None, K, tn), lambda j, i: (j, 0, 0))
        out_shape = jax.ShapeDtypeStruct((col_shards, K, tn), out_dtype)
    else:
        out_specs = pl.BlockSpec((K, tn), lambda j, i: (0, j))
        out_shape = jax.ShapeDtypeStruct((K, N), out_dtype)
    return pl.pallas_call(
        body, name=name, grid=(N // tn, n_t),
        in_specs=[pl.BlockSpec((tm, K), lambda j, i: (i, 0)), pl.BlockSpec((tm, tn), lambda j, i: (i, j))],
        out_specs=out_specs, out_shape=out_shape, scratch_shapes=[pltpu.VMEM((K, tn), F32)],
        compiler_params=_cparams(("arbitrary", "arbitrary")),
    )(a, dy)


def _seg_ones(n):
    r = lax.broadcasted_iota(jnp.int32, (n, n), 0) // HD
    c = lax.broadcasted_iota(jnp.int32, (n, n), 1) // HD
    return (r == c).astype(F32)


def _segsum_raw(x):
    ones = _seg_ones(x.shape[1])
    if MXU == F32:
        return jnp.dot(x, ones, precision=HI, preferred_element_type=F32)
    hi = x.astype(MXU)
    lo = (x - hi.astype(F32)).astype(MXU)
    ones = ones.astype(MXU)
    return jnp.dot(hi, ones, preferred_element_type=F32) + jnp.dot(lo, ones, preferred_element_type=F32)


@jax.custom_vjp
def _segsum(x):
    return _segsum_raw(x)


def _segsum_fwd(x):
    return _segsum_raw(x), None


def _segsum_bwd(_, g):
    return (_segsum_raw(g),)


_segsum.defvjp(_segsum_fwd, _segsum_bwd)


def _mxu(x):
    return x.astype(MXU)


@jax.custom_vjp
def _bdot(a, b):
    return jnp.dot(_mxu(a), _mxu(b), preferred_element_type=F32)


def _bdot_fwd(a, b):
    return _bdot(a, b), (a, b)


def _bdot_bwd(res, g):
    a, b = res
    da = lax.dot_general(_mxu(g), _mxu(b), (((1,), (1,)), ((), ())), preferred_element_type=F32)
    db = lax.dot_general(_mxu(a), _mxu(g), (((0,), (0,)), ((), ())), preferred_element_type=F32)
    return da.astype(a.dtype), db.astype(b.dtype)


_bdot.defvjp(_bdot_fwd, _bdot_bwd)


@jax.custom_vjp
def _bdot_nt(a, b):
    return lax.dot_general(_mxu(a), _mxu(b), (((1,), (1,)), ((), ())), preferred_element_type=F32)


def _bdot_nt_fwd(a, b):
    return _bdot_nt(a, b), (a, b)


def _bdot_nt_bwd(res, g):
    a, b = res
    da = jnp.dot(_mxu(g), _mxu(b), preferred_element_type=F32)
    db = lax.dot_general(_mxu(g), _mxu(a), (((0,), (0,)), ((), ())), preferred_element_type=F32)
    return da.astype(a.dtype), db.astype(b.dtype)


_bdot_nt.defvjp(_bdot_nt_fwd, _bdot_nt_bwd)


def _sigmoid(x):
    return 1.0 / (1.0 + jnp.exp(-x))


def _softplus(x):
    return jnp.maximum(x, 0.0) + jnp.log(1.0 + jnp.exp(jnp.minimum(x, -x)))


def _norm_mod(x, gain, scale, shift):
    inv = lax.rsqrt(jnp.mean(x * x, axis=-1, keepdims=True) + RMS_EPS)
    return (x * inv) * gain * (1.0 + scale) + shift


def _prep(mixed, decay_w0, lora_up, iclr_a0, gate_up, k_k, k_a):
    r = mixed[:, 0:RW]
    k = mixed[:, RW:2 * RW]
    v = mixed[:, 2 * RW:3 * RW]
    z = mixed[:, 3 * RW:3 * RW + 128]
    xg = mixed[:, 3 * RW + 128:]
    lane = lax.broadcasted_iota(jnp.int32, z.shape, 1)
    tz = jnp.where(lane < 64, jnp.tanh(z), z)
    lo = _bdot(tz, lora_up)
    w_log = -_softplus(-(decay_w0 + lo[:, :RW])) - 0.5
    lw = -jnp.exp(w_log)
    a_ic = _sigmoid(iclr_a0 + lo[:, RW:])
    g = _bdot(_sigmoid(xg), gate_up)
    kk = k * k_k
    kk = kk / jnp.maximum(jnp.sqrt(_segsum(kk * kk)), 1e-12)
    k_mod = k * (1.0 + (a_ic - 1.0) * k_a)
    return jnp.concatenate([r, lw, k_mod, v, -kk, kk * a_ic, g], axis=1)


def _post(y, r, k, v, g, lnx_gain, lnx_bias, r_k):
    mu = _segsum(y) * (1.0 / HD)
    yc = y - mu
    var = _segsum(yc * yc) * (1.0 / HD)
    yn = yc * lax.rsqrt(var + GN_EPS) * lnx_gain + lnx_bias
    bonus = _segsum(r * k * r_k) * v
    return (yn + bonus) * g


def _merge(pg, ma, mb, bias):
    gates = _sigmoid(pg + bias)
    return gates[:, :D] * ma + gates[:, D:] * mb


def _swiglu(u, v):
    return u * _sigmoid(u) * v


@functools.partial(jax.custom_vjp, nondiff_argnums=(1,))
def _lane_roll(x, s):
    return pltpu.roll(x, s, 1)


def _lane_roll_fwd(x, s):
    return pltpu.roll(x, s, 1), None


def _lane_roll_bwd(s, _, g):
    n = g.shape[1]
    return (pltpu.roll(g, (n - s) % n, 1),)


_lane_roll.defvjp(_lane_roll_fwd, _lane_roll_bwd)


def _rope(x, cos, sin_lo, sin_hi):
    n = x.shape[1]
    return x * cos + _lane_roll(x, n - 8) * sin_lo + _lane_roll(x, 8) * sin_hi


def _head_rms(x, gain):
    return x * lax.rsqrt(_segsum(x * x) * (1.0 / HD) + RMS_EPS) * gain


def _attn_block(qkv_c, qkv_p, tab_c, tab_p, qg, kg, sinks, first):
    def tabs(tab, n):
        return tab[:, 0:n], tab[:, RW:RW + n], tab[:, 2 * RW:2 * RW + n]

    qg = jnp.concatenate([qg] * NH, axis=1)
    kg = jnp.concatenate([kg] * 2, axis=1)
    q = _rope(_head_rms(qkv_c[:, :RW], qg), *tabs(tab_c, RW))
    k_c = _rope(_head_rms(qkv_c[:, RW:RW + 128], kg), *tabs(tab_c, 128))
    k_p = _rope(_head_rms(qkv_p[:, RW:RW + 128], kg), *tabs(tab_p, 128))
    kband = jnp.concatenate([k_p, k_c], axis=0)
    vband = jnp.concatenate([qkv_p[:, RW + 128:], qkv_c[:, RW + 128:]], axis=0)
    G = 4
    qi = lax.broadcasted_iota(jnp.int32, (G * BLK, 2 * BLK), 0) % BLK
    kj = lax.broadcasted_iota(jnp.int32, (G * BLK, 2 * BLK), 1)
    dist = qi + BLK - kj
    valid = (dist >= 0) & (dist < BLK) & (jnp.logical_not(first) | (kj >= BLK))
    row_g = lax.broadcasted_iota(jnp.int32, (G * BLK, 1), 0) // BLK
    outs = []
    for kvh in range(2):
        kb = kband[:, kvh * HD:(kvh + 1) * HD]
        vb = vband[:, kvh * HD:(kvh + 1) * HD]
        qs = jnp.concatenate([q[:, (G * kvh + g) * HD:(G * kvh + g + 1) * HD] for g in range(G)], axis=0)
        s = _bdot_nt(qs, kb) * (HD ** -0.5)
        s = jnp.where(valid, s, NEG_INF)
        sink = jnp.zeros((G * BLK, 1), F32)
        for g in range(G):
            sink = jnp.where(row_g == g, sinks[:, G * kvh + g:G * kvh + g + 1], sink)
        m = lax.stop_gradient(jnp.maximum(jnp.max(s, axis=-1, keepdims=True), sink))
        e = jnp.exp(s - m)
        p = e / (jnp.sum(e, axis=-1, keepdims=True) + jnp.exp(sink - m))
        o = _bdot(p, vb)
        outs += [o[g * BLK:(g + 1) * BLK] for g in range(G)]
    return jnp.concatenate(outs, axis=1)


def _heads(x):
    return jnp.stack([x[:, h * HD:(h + 1) * HD] for h in range(NH)], axis=0)


def _unheads(x):
    return jnp.concatenate([x[h] for h in range(NH)], axis=1)


def _split(x, n):
    parts, rest = [], x
    for _ in range(n):
        p = rest.astype(MXU)
        parts.append(p)
        rest = rest - p.astype(F32)
    return parts


def _bdot_batched(a, b, ca, cb):
    return lax.dot_general(a, b, (((ca,), (cb,)), ((0,), (0,))), preferred_element_type=F32)


def _bmm_passes(a, b, ca, cb, passes):
    if MXU == F32:
        return lax.dot_general(a, b, (((ca,), (cb,)), ((0,), (0,))), precision=HI, preferred_element_type=F32)
    if passes == 1:
        return _bdot_batched(a.astype(MXU), b.astype(MXU), ca, cb)
    (a0, a1), (b0, b1) = _split(a, 2), _split(b, 2)
    return _bdot_batched(a0, b0, ca, cb) + (_bdot_batched(a0, b1, ca, cb) + _bdot_batched(a1, b0, ca, cb))


@functools.partial(jax.custom_vjp, nondiff_argnums=(2, 3, 4))
def _bmm(a, b, ca, cb, passes=1):
    return _bmm_passes(a, b, ca, cb, passes)


def _bmm_fwd(a, b, ca, cb, passes):
    return _bmm_passes(a, b, ca, cb, passes), (a, b)


def _bmm_bwd(ca, cb, passes, res, g):
    a, b = res
    if (ca, cb) == (2, 1):
        return _bmm_passes(g, b, 2, 2, passes), _bmm_passes(a, g, 1, 1, passes)
    if (ca, cb) == (2, 2):
        return _bmm_passes(g, b, 2, 1, passes), _bmm_passes(g, a, 1, 1, passes)
    return _bmm_passes(b, g, 2, 2, passes), _bmm_passes(a, g, 2, 1, passes)


_bmm.defvjp(_bmm_fwd, _bmm_bwd)


def _tri_dot(x, transpose):
    C = x.shape[1]
    ri = lax.broadcasted_iota(jnp.int32, (C, C), 0)
    ci = lax.broadcasted_iota(jnp.int32, (C, C), 1)
    tri = jnp.broadcast_to(((ri <= ci) if transpose else (ri >= ci)).astype(MXU), (x.shape[0], C, C))
    if MXU == F32:
        return lax.dot_general(tri, x, (((2,), (1,)), ((0,), (0,))), precision=HI, preferred_element_type=F32)
    p0, p1, p2 = _split(x, 3)
    return _bdot_batched(tri, p0, 2, 1) + (_bdot_batched(tri, p1, 2, 1) + _bdot_batched(tri, p2, 2, 1))


@jax.custom_vjp
def _cumsum_rows(x):
    return _tri_dot(x, False)


def _cumsum_rows_fwd(x):
    return _tri_dot(x, False), None


def _cumsum_rows_bwd(_, g):
    return (_tri_dot(g, True),)


_cumsum_rows.defvjp(_cumsum_rows_fwd, _cumsum_rows_bwd)

P_SCORE = 1
P_SOLVE = 1
P_STATE = 1


def _chunk(S0, r, lw, k, v, a, b):
    C = r.shape[1]
    ri = lax.broadcasted_iota(jnp.int32, (C, C), 0)
    ci = lax.broadcasted_iota(jnp.int32, (C, C), 1)
    incl = (ri >= ci)
    strict = (ri > ci)
    eye = (ri == ci).astype(F32)
    cum = _cumsum_rows(lw)
    p_in = jnp.exp(cum)
    p_ex = jnp.exp(cum - lw)
    p_inv = jnp.exp(-cum)
    at, rt, bt, kt = a * p_ex, r * p_in, b * p_inv, k * p_inv
    sc = _bmm(jnp.concatenate([at, rt], axis=1), jnp.concatenate([bt, kt], axis=1), 2, 2, P_SCORE)
    a_ab = jnp.where(strict, sc[:, :C, :C], 0.0)
    a_ak = jnp.where(strict, sc[:, :C, C:], 0.0)
    a_rb = jnp.where(incl, sc[:, C:, :C], 0.0)
    a_rk = jnp.where(incl, sc[:, C:, C:], 0.0)
    s0 = _bmm(jnp.concatenate([at, rt], axis=1), S0, 2, 2, P_STATE)
    rhs = s0[:, :C] + _bmm(a_ak, v, 2, 1, P_SCORE)
    x = eye + a_ab
    lp = a_ab
    for _ in range(int(math.log2(C)) - 1):
        lp = _bmm(lp, lp, 2, 1, P_SOLVE)
        x = x + _bmm(x, lp, 2, 1, P_SOLVE)
    u = _bmm(x, rhs, 2, 1, P_SOLVE)
    y = s0[:, C:] + _bmm(jnp.concatenate([a_rb, a_rk], axis=2), jnp.concatenate([u, v], axis=1), 2, 1, P_SCORE)
    p_last = jnp.exp(cum[:, C - 1:C, :])
    S1 = (S0 + _bmm(jnp.concatenate([u, v], axis=1), jnp.concatenate([bt, kt], axis=1), 1, 1, P_STATE)) * p_last
    return y, S1


def _hosting(body, ex, n_in, n_out, n_scratch, n_steps):
    if ex is None:
        return body

    def wrapped(*refs):
        a = n_in
        b = a + ex.n_in
        c = b + n_out
        d = c + ex.n_out
        e = d + n_scratch
        ex_refs = (refs[a:b], refs[c:d], refs[e:])
        grid = n_steps if isinstance(n_steps, tuple) else (n_steps,)
        first = last = True
        for ax_, size in enumerate(grid):
            first = first & (pl.program_id(ax_) == 0)
            last = last & (pl.program_id(ax_) == size - 1)

        @pl.when(first)
        def _():
            ex.start(*ex_refs)

        body(*refs[:a], *refs[b:c], *refs[d:e])

        @pl.when(last)
        def _():
            ex.wait(*ex_refs)

    return wrapped


def _hosted_args(ex):
    if ex is None:
        return [], [], [], [], []
    any_spec = pl.BlockSpec(memory_space=pl.ANY)
    return list(ex.arrays), [any_spec] * ex.n_in, [any_spec] * ex.n_out, list(ex.out_shape), list(ex.scratch)


def _scan_fwd(rw, *, name, hosted=None):
    T = rw.shape[0]
    n = T // CHUNK
    h_in, h_in_specs, h_out_specs, h_out_shape, h_scratch = _hosted_args(hosted)

    def body(r_ref, lw_ref, k_ref, v_ref, a_ref, b_ref, y_ref, ck_ref, s_ref):
        @pl.when(pl.program_id(0) == 0)
        def _():
            s_ref[...] = jnp.zeros_like(s_ref)

        S0 = s_ref[...]
        ck_ref[0] = S0
        y, S1 = _chunk(S0, *[_heads(ref[...]) for ref in (r_ref, lw_ref, k_ref, v_ref, a_ref, b_ref)])
        y_ref[...] = _unheads(y)
        s_ref[...] = S1

    col = lambda j: pl.BlockSpec((CHUNK, RW), lambda i: (i, j))
    return pl.pallas_call(
        _hosting(body, hosted, 6, 2, 1, n), name=name, grid=(n,),
        in_specs=[col(j) for j in range(6)] + h_in_specs,
        out_specs=[pl.BlockSpec((CHUNK, RW), lambda i: (i, 0)),
                   pl.BlockSpec((1, NH, HD, HD), lambda i: (i, 0, 0, 0))] + h_out_specs,
        out_shape=[jax.ShapeDtypeStruct((T, RW), F32), jax.ShapeDtypeStruct((n, NH, HD, HD), F32)] + h_out_shape,
        scratch_shapes=[pltpu.VMEM((NH, HD, HD), F32)] + h_scratch,
        compiler_params=_cparams(("arbitrary",)),
    )(rw, rw, rw, rw, rw, rw, *h_in)


def _scan_bwd(rw, ck, dy, *, name, hosted=None):
    T = rw.shape[0]
    n = T // CHUNK

    def body(r_ref, lw_ref, k_ref, v_ref, a_ref, b_ref, ck_ref, dy_ref, o_ref, ds_ref):
        @pl.when(pl.program_id(0) == 0)
        def _():
            ds_ref[...] = jnp.zeros_like(ds_ref)

        prim = [_heads(ref[...]) for ref in (r_ref, lw_ref, k_ref, v_ref, a_ref, b_ref)]
        _, vjp = jax.vjp(_chunk, ck_ref[0], *prim)
        grads = vjp((_heads(dy_ref[...]), ds_ref[...]))
        ds_ref[...] = grads[0]
        o_ref[...] = jnp.concatenate([_unheads(g) for g in grads[1:]], axis=1)

    h_in, h_in_specs, h_out_specs, h_out_shape, h_scratch = _hosted_args(hosted)
    col = lambda j: pl.BlockSpec((CHUNK, RW), lambda i: (n - 1 - i, j))
    return pl.pallas_call(
        _hosting(body, hosted, 8, 1, 1, n), name=name, grid=(n,),
        in_specs=[col(j) for j in range(6)] + [pl.BlockSpec((1, NH, HD, HD), lambda i: (n - 1 - i, 0, 0, 0)),
                                               pl.BlockSpec((CHUNK, RW), lambda i: (n - 1 - i, 0))] + h_in_specs,
        out_specs=[pl.BlockSpec((CHUNK, 6 * RW), lambda i: (n - 1 - i, 0))] + h_out_specs,
        out_shape=[jax.ShapeDtypeStruct((T, 6 * RW), F32)] + h_out_shape,
        scratch_shapes=[pltpu.VMEM((NH, HD, HD), F32)] + h_scratch,
        compiler_params=_cparams(("arbitrary",)),
    )(rw, rw, rw, rw, rw, rw, ck, dy, *h_in)


def _attn_fwd(qkv, tab, qg, kg, sinks, *, name, hosted=None):
    T = qkv.shape[0]
    n = T // BLK
    h_in, h_in_specs, h_out_specs, h_out_shape, h_scratch = _hosted_args(hosted)

    def body(c_ref, p_ref, tc_ref, tp_ref, qg_ref, kg_ref, s_ref, o_ref):
        first = pl.program_id(0) == 0
        o_ref[...] = _attn_block(c_ref[...], p_ref[...], tc_ref[...], tp_ref[...], qg_ref[...], kg_ref[...],
                                 s_ref[...], first).astype(o_ref.dtype)

    cur = lambda w: pl.BlockSpec((BLK, w), lambda i: (i, 0))
    prev = lambda w: pl.BlockSpec((BLK, w), lambda i: (jnp.maximum(i - 1, 0), 0))
    return pl.pallas_call(
        _hosting(body, hosted, 7, 1, 0, n), name=name, grid=(n,),
        in_specs=[cur(QKV_W), prev(QKV_W), cur(3 * RW), prev(3 * RW), _full_spec(qg), _full_spec(kg),
                  _full_spec(sinks)] + h_in_specs,
        out_specs=[cur(RW)] + h_out_specs, out_shape=[jax.ShapeDtypeStruct((T, RW), MXU)] + h_out_shape,
        scratch_shapes=h_scratch,
        compiler_params=_cparams(("arbitrary",)),
    )(qkv, qkv, tab, tab, qg, kg, sinks, *h_in)


def _attn_bwd(qkv, tab, qg, kg, sinks, dy, *, name, hosted=None):
    T = qkv.shape[0]
    n = T // BLK
    h_in, h_in_specs, h_out_specs, h_out_shape, h_scratch = _hosted_args(hosted)

    def body(c_ref, p_ref, tc_ref, tp_ref, qg_ref, kg_ref, s_ref, dy_ref, dqkv_ref, dqg_ref, dkg_ref, ds_ref, carry_ref):
        i = pl.program_id(0)
        first = i == n - 1

        @pl.when(i == 0)
        def _():
            carry_ref[...] = jnp.zeros_like(carry_ref)
            dqg_ref[...] = jnp.zeros_like(dqg_ref)
            dkg_ref[...] = jnp.zeros_like(dkg_ref)
            ds_ref[...] = jnp.zeros_like(ds_ref)

        tc, tp = tc_ref[...], tp_ref[...]
        f = lambda c, p, qg_, kg_, sk: _attn_block(c, p, tc, tp, qg_, kg_, sk, first)
        _, vjp = jax.vjp(f, c_ref[...], p_ref[...], qg_ref[...], kg_ref[...], s_ref[...])
        dc, dp, dqg, dkg, dsk = vjp(dy_ref[...].astype(F32))
        dqkv_ref[...] = (dc + carry_ref[...]).astype(dqkv_ref.dtype)
        carry_ref[...] = dp
        dqg_ref[...] += dqg
        dkg_ref[...] += dkg
        ds_ref[...] += dsk

    cur = lambda w: pl.BlockSpec((BLK, w), lambda i: (n - 1 - i, 0))
    prev = lambda w: pl.BlockSpec((BLK, w), lambda i: (jnp.maximum(n - 2 - i, 0), 0))
    return pl.pallas_call(
        _hosting(body, hosted, 8, 4, 1, n), name=name, grid=(n,),
        in_specs=[cur(QKV_W), prev(QKV_W), cur(3 * RW), prev(3 * RW), _full_spec(qg), _full_spec(kg), _full_spec(sinks),
                  cur(RW)] + h_in_specs,
        out_specs=[cur(QKV_W), _full_spec(qg), _full_spec(kg), _full_spec(sinks)] + h_out_specs,
        out_shape=[jax.ShapeDtypeStruct((T, QKV_W), MXU), jax.ShapeDtypeStruct(qg.shape, F32),
                   jax.ShapeDtypeStruct(kg.shape, F32), jax.ShapeDtypeStruct(sinks.shape, F32)] + h_out_shape,
        scratch_shapes=[pltpu.VMEM((BLK, QKV_W), F32)] + h_scratch,
        compiler_params=_cparams(("arbitrary",)),
    )(qkv, qkv, tab, tab, qg, kg, sinks, dy, *h_in)


def _shift_down(cur, prev8, i):
    rolled = pltpu.roll(cur, 1, 0)
    first_row = jnp.where(i > 0, prev8[7:8, :], 0.0)
    row = lax.broadcasted_iota(jnp.int32, cur.shape, 0)
    return jnp.where(row == 0, first_row, rolled)


def _shift_up(cur, next8, i, n):
    tm = cur.shape[0]
    rolled = pltpu.roll(cur, tm - 1, 0)
    last_row = jnp.where(i < n - 1, next8[0:1, :], 0.0)
    row = lax.broadcasted_iota(jnp.int32, cur.shape, 0)
    return jnp.where(row == tm - 1, last_row, rolled)


def _ada_parts(ada):
    return [ada[:, j * D:(j + 1) * D] for j in range(6)]


def _rope_table(positions):
    half = HD // 8
    inv_freq = 500000.0 ** (-jnp.arange(half, dtype=F32) / half)
    ang = positions.astype(F32)[:, None] * inv_freq
    cos, sin = jnp.cos(ang), jnp.sin(ang)
    T = positions.shape[0]
    pad = HD - 2 * half
    c64 = jnp.concatenate([cos, cos, jnp.ones((T, pad), F32)], axis=1)
    lo64 = jnp.concatenate([-sin, jnp.zeros((T, HD - half), F32)], axis=1)
    hi64 = jnp.concatenate([jnp.zeros((T, half), F32), sin, jnp.zeros((T, pad), F32)], axis=1)
    return jnp.concatenate([jnp.tile(t, (1, NH)) for t in (c64, lo64, hi64)], axis=1)


LATE_SCAN = ['ffn_w1', 'ffn_w3']
LATE_ATTN = ['ffn_w2', 'w_branch_a', 'w_branch_b', 'w_out']
BACK_ATTN = ['ffn_w1', 'ffn_w3', 'ffn_w2']
BACK_SCAN = ['w_out', 'w_branch_a', 'w_branch_b']
BACK_LAST = ['w_in', 'decay_up', 'iclr_up', 'gate_up']


def _full_weight(g, ax):
    return g.reshape(-1, g.shape[2]) if ax == 0 else jnp.concatenate([g[j] for j in range(4)], axis=1)


def _local_step(x, target, ada, tab, w, s, shards=None):
    T = x.shape[0]
    tm = _pick(T, (512, 256, 128))
    tm_wide = _pick(T, (256, 128))
    tm_vjp = _pick(T, (128,))
    row = lambda n, dt=F32: (n, dt, 'row')
    acc = lambda n, r=1: (n, F32, r)

    def f_norm1(i, n, x_, g, ada_):
        sh, sc = ada_[:, 0:D], ada_[:, D:2 * D]
        return (_norm_mod(x_, g, sc, sh),)
    (h1,) = _rowwise(f_norm1, [x], [s['norm1_gain'], ada], [row(D, MXU)], tm=tm, name="f_norm1")
    proj = _mm_nn(h1, w['w_in'][:, :SHIFT_W], name="f_proj_shift")
    proj_qkv = _mm_nn(h1, w['w_in'][:, SHIFT_W:SHIFT_W + QKV_W], name="f_proj_qkv")
    proj_g = _mm_nn(h1, w['w_in'][:, SHIFT_W + QKV_W:], name="f_proj_gates")
    prep_consts = [s['decay_w0'], s['lora_up'], s['iclr_a0'], s['gate_up'], s['k_k'], s['k_a']]

    def f_prep(i, n, cur, prev8, mu, *params):
        mixed = cur + (_shift_down(cur, prev8, i) - cur) * mu
        return (_prep(mixed, *params),)
    (rw,) = _rowwise(f_prep, [(proj, SHIFT_W)], [s['tshift_mu']] + prep_consts, [row(7 * RW)], tm=tm_wide,
                     name="f_prep", halo=[(proj, SHIFT_W, 'prev')])
    ax = dict(SHARDED)
    w = dict(w)
    y, ck, *got = _scan_fwd(rw, name="f_scan", hosted=shards and _GatherChips([shards[n] for n in LATE_SCAN]))
    if shards:
        w['ffn_w13'] = jnp.concatenate([_full_weight(g, ax[n]) for n, g in zip(LATE_SCAN, got, strict=True)], axis=1)
    post_consts = [s['lnx_gain'], s['lnx_bias'], s['r_k']]

    def post_of(y_, rw_, *params):
        return _post(y_, rw_[:, 0:RW], rw_[:, 2 * RW:3 * RW], rw_[:, 3 * RW:4 * RW], rw_[:, 6 * RW:7 * RW], *params)

    def f_post(i, n, y_, rw_, *params):
        return (post_of(y_, rw_, *params),)
    (ya,) = _rowwise(f_post, [y, rw], post_consts, [row(RW, MXU)], tm=tm_wide, name="f_post")
    yb, *got = _attn_fwd(proj_qkv, tab, s['q_norm_gain'], s['k_norm_gain'], s['attn_sinks'], name="f_attn",
                         hosted=shards and _GatherChips([shards[n] for n in LATE_ATTN]))
    if shards:
        w.update({n: _full_weight(g, ax[n]) for n, g in zip(LATE_ATTN, got, strict=True)})
    ma = _mm_nn(ya, w['w_branch_a'], name="f_branch_a")
    mb = _mm_nn(yb, w['w_branch_b'], name="f_branch_b")

    def f_merge(i, n, pg, ma_, mb_, bias):
        return (_merge(pg, ma_, mb_, bias),)
    (merged,) = _rowwise(f_merge, [proj_g, ma, mb], [s['branch_gate_b']], [row(D, MXU)], tm=tm, name="f_merge")
    mo = _mm_nn(merged, w['w_out'], name="f_out")

    def f_res1(i, n, x_, mo_, g, ada_):
        x1_ = x_ + ada_[:, 2 * D:3 * D] * mo_
        return x1_, _norm_mod(x1_, g, ada_[:, 4 * D:5 * D], ada_[:, 3 * D:4 * D])
    x1, h2 = _rowwise(f_res1, [x, mo], [s['norm2_gain'], ada], [row(D), row(D, MXU)], tm=tm, name="f_res1")
    uv = _mm_nn(h2, w['ffn_w13'], name="f_ffn_in")

    def f_act(i, n, uv_):
        return (_swiglu(uv_[:, :DFF], uv_[:, DFF:]),)
    (act,) = _rowwise(f_act, [uv], [], [row(DFF, MXU)], tm=tm_wide, name="f_act")
    ff = _mm_nn(act, w['ffn_w2'], name="f_ffn_out")

    def f_loss(i, n, x1_, ff_, tgt, ada_):
        g2 = ada_[:, 5 * D:6 * D]
        err = x1_ + g2 * ff_ - tgt
        dx2 = err * (1.0 / D)
        loss = 0.5 * jnp.sum(jnp.sum(err * err, axis=1, keepdims=True) * (1.0 / D), axis=0, keepdims=True)
        return dx2, (dx2 * g2), jnp.broadcast_to(loss, (1, 128)), jnp.sum(dx2 * ff_, axis=0, keepdims=True)
    dx2, dff, loss, dgate2 = _rowwise(f_loss, [x1, ff, target], [ada], [row(D), row(D, MXU), acc(128), acc(D)],
                                      tm=tm, name="f_loss")

    dact = _mm_nt(dff, w['ffn_w2'], name="b_ffn_out_dx")
    g_w2 = _mm_tn(act, dff, name="b_ffn_out_dw", out_dtype=MXU)

    def b_act(i, n, uv_, dact_):
        _, vjp = jax.vjp(_swiglu, uv_[:, :DFF], uv_[:, DFF:])
        du, dv = vjp(dact_)
        return (jnp.concatenate([du, dv], axis=1),)
    (duv,) = _rowwise(b_act, [uv, dact], [], [row(2 * DFF, MXU)], tm=tm_wide, name="b_act")
    dh2 = _mm_nt(duv, w['ffn_w13'], name="b_ffn_in_dx")
    g_w13 = _mm_tn(h2, duv, name="b_ffn_in_dw", out_dtype=MXU)

    def b_res1(i, n, x1_, dh2_, dx2_, mo_, g, ada_):
        _, vjp = jax.vjp(_norm_mod, x1_, g, ada_[:, 4 * D:5 * D], ada_[:, 3 * D:4 * D])
        dxn, dg, dsc, dsh = vjp(dh2_)
        dx1_ = dxn + dx2_
        g1 = ada_[:, 2 * D:3 * D]
        return dx1_, dx1_ * g1, dg, dsc, dsh, jnp.sum(dx1_ * mo_, axis=0, keepdims=True)
    dx1, dmo, d_gain2, d_scale2, d_shift2, dgate1 = _rowwise(
        b_res1, [x1, dh2, dx2, mo], [s['norm2_gain'], ada], [row(D), row(D, MXU), acc(D), acc(D), acc(D), acc(D)],
        tm=tm, name="b_res1")
    dmerged = _mm_nt(dmo, w['w_out'], name="b_out_dx")
    g_wout = _mm_tn(merged, dmo, name="b_out_dw", out_dtype=MXU)

    def b_merge(i, n, pg, ma_, mb_, dm, bias):
        _, vjp = jax.vjp(_merge, pg, ma_, mb_, bias)
        dpg, dma_, dmb_, dbias = vjp(dm)
        return dpg, dma_, dmb_, dbias
    dpg, dma, dmb, d_bias = _rowwise(b_merge, [proj_g, ma, mb, dmerged], [s['branch_gate_b']],
                                     [row(GATE_W, MXU), row(D, MXU), row(D, MXU), acc(GATE_W)], tm=tm_wide, name="b_merge")
    dya = _mm_nt(dma, w['w_branch_a'], name="b_branch_a_dx")
    g_wa = _mm_tn(ya, dma, name="b_branch_a_dw", out_dtype=MXU, col_shards=4)
    dyb = _mm_nt(dmb, w['w_branch_b'], name="b_branch_b_dx", out_dtype=F32)
    g_wb = _mm_tn(yb, dmb, name="b_branch_b_dw", out_dtype=MXU, col_shards=4)
    fs = DFF // 4
    gw = dict(w_branch_a=g_wa, w_branch_b=g_wb, w_out=g_wout.reshape(4, D // 4, D),
              ffn_w1=jnp.stack([g_w13[:, j * fs:(j + 1) * fs] for j in range(4)]),
              ffn_w3=jnp.stack([g_w13[:, DFF + j * fs:DFF + (j + 1) * fs] for j in range(4)]),
              ffn_w2=g_w2.reshape(4, fs, D))
    recv = {}
    dqkv, d_qg, d_kg, d_sinks, *got = _attn_bwd(
        proj_qkv, tab, s['q_norm_gain'], s['k_norm_gain'], s['attn_sinks'], dyb, name="b_attn",
        hosted=shards and _ScatterChips([gw[n] for n in BACK_ATTN]))
    recv.update(zip(BACK_ATTN, got))

    def b_post(i, n, y_, rw_, dya_, *params):
        _, vjp = jax.vjp(post_of, y_, rw_, *params)
        return vjp(dya_)
    dy, drw_post, d_lnx_gain, d_lnx_bias, d_r_k = _rowwise(
        b_post, [y, rw, dya], post_consts, [row(RW), row(7 * RW), acc(RW), acc(RW), acc(RW)], tm=tm_vjp, name="b_post")
    dscan, *got = _scan_bwd(rw, ck, dy, name="b_scan",
                            hosted=shards and _ScatterChips([gw[n] for n in BACK_SCAN]))
    recv.update(zip(BACK_SCAN, got))

    def b_prep(i, n, cur, drw_, dscan_, prev8, mu, *params):
        shifted = _shift_down(cur, prev8, i)
        mixed = cur + (shifted - cur) * mu
        _, vjp = jax.vjp(_prep, mixed, *params)
        ct = drw_ + jnp.concatenate([dscan_, jnp.zeros((dscan_.shape[0], RW), F32)], axis=1)
        grads = vjp(ct)
        dmixed = grads[0]
        return (dmixed, jnp.sum(dmixed * (shifted - cur), axis=0, keepdims=True)) + tuple(grads[1:])
    dmixed, d_mu, d_w0, d_lora, d_a0, d_gate_up, d_kk, d_ka = _rowwise(
        b_prep, [(proj, SHIFT_W), drw_post, dscan], [s['tshift_mu']] + prep_consts,
        [row(SHIFT_W), acc(SHIFT_W), acc(RW), acc(2 * RW, 128), acc(RW), acc(RW, 128), acc(RW), acc(RW)],
        tm=tm_vjp, name="b_prep", halo=[(proj, SHIFT_W, 'prev')])

    def b_gather(i, n, dm, dqkv_, dpg_, next8, mu):
        dcur = dm * (1.0 - mu) + _shift_up(dm, next8, i, n) * mu
        return (jnp.concatenate([dcur.astype(MXU), dqkv_, dpg_], axis=1),)
    (dproj,) = _rowwise(b_gather, [dmixed, dqkv, dpg], [s['tshift_mu']], [row(IN_W, MXU)], tm=tm_wide, name="b_gather",
                        halo=[(dmixed, SHIFT_W, 'next')])
    g_win = _mm_tn(h1, dproj, name="b_proj_dw", out_dtype=MXU, col_shards=4)

    def col_blocks(g):
        k, n = g.shape
        return g.reshape(k, 4, n // 4).transpose(1, 0, 2).astype(MXU)
    gw.update(w_in=g_win, decay_up=col_blocks(d_lora[:64, :RW]), iclr_up=col_blocks(d_lora[64:, RW:]),
              gate_up=col_blocks(d_gate_up))
    if shards:
        dh1, *got = _mm_nt(dproj, w['w_in'], name="b_proj_dx", hosted=_ScatterChips([gw[n] for n in BACK_LAST]))
        recv.update(zip(BACK_LAST, got))
    else:
        dh1 = _mm_nt(dproj, w['w_in'], name="b_proj_dx")

    def b_norm1(i, n, x_, dh1_, dx1_, g, ada_):
        _, vjp = jax.vjp(_norm_mod, x_, g, ada_[:, D:2 * D], ada_[:, 0:D])
        dxn, dg, dsc, dsh = vjp(dh1_)
        return dxn + dx1_, dg, dsc, dsh
    dx, d_gain1, d_scale1, d_shift1 = _rowwise(b_norm1, [x, dh1, dx1], [s['norm1_gain'], ada],
                                               [row(D), acc(D), acc(D), acc(D)], tm=tm, name="b_norm1")

    d_ada = jnp.concatenate([d_shift1, d_scale1, dgate1, d_shift2, d_scale2, dgate2], axis=1)
    gs = dict(norm1_gain=d_gain1, norm2_gain=d_gain2, tshift_mu=d_mu, decay_w0=d_w0, iclr_a0=d_a0, k_k=d_kk, k_a=d_ka,
              r_k=d_r_k, lnx_gain=d_lnx_gain, lnx_bias=d_lnx_bias, q_norm_gain=d_qg, k_norm_gain=d_kg,
              attn_sinks=d_sinks, branch_gate_b=d_bias)
    return loss, dx, d_ada, gw, gs, recv


ANY = pl.BlockSpec(memory_space=pl.ANY)


def _place():
    x, y, c = lax.axis_index("x"), lax.axis_index("y"), lax.axis_index("c")
    return x, y, c, [(1 - x, y), (x, 1 - y), (1 - x, 1 - y)]


def _all_gather8(x_shard, *, name):
    m_per, n = x_shard.shape

    def body(x_ref, out_ref, send_sems, recv_sems, local_sem):
        x, y, c, chips = _place()
        me, sibling = (x, y, c), (x, y, 1 - c)

        def rows(px, py, pc):
            return out_ref.at[pl.ds((4 * px + 2 * py + pc) * m_per, m_per), :]

        def copy(k, block, to, src=None):
            return pltpu.make_async_remote_copy(
                src_ref=rows(*block) if src is None else src, dst_ref=rows(*block),
                send_sem=send_sems.at[k], recv_sem=recv_sems.at[k], device_id=to, device_id_type=MESH)

        mine = pltpu.make_async_copy(x_ref, rows(*me), local_sem)
        mine.start()
        first = [copy(0, me, sibling, src=x_ref)]
        first += [copy(1 + j, me, (*chip, c), src=x_ref) for j, chip in enumerate(chips)]
        for cp in first:
            cp.start()
        passed = [copy(4 + j, (*chip, c), sibling) for j, chip in enumerate(chips)]
        for j, chip in enumerate(chips):
            copy(1 + j, (*chip, c), me).wait_recv()
            passed[j].start()
        copy(0, sibling, me).wait_recv()
        for j, chip in enumerate(chips):
            copy(4 + j, (*chip, 1 - c), me).wait_recv()
        for cp in first + passed:
            cp.wait_send()
        mine.wait()

    return pl.pallas_call(
        body, name=name, out_shape=jax.ShapeDtypeStruct((8 * m_per, n), x_shard.dtype),
        in_specs=[pl.BlockSpec(memory_space=pltpu.VMEM)], out_specs=pl.BlockSpec(memory_space=pltpu.VMEM),
        scratch_shapes=[pltpu.SemaphoreType.DMA((7,)), pltpu.SemaphoreType.DMA((7,)), pltpu.SemaphoreType.DMA],
    )(x_shard)


class _GatherChips:
    def __init__(self, shards):
        n = len(shards)
        self.arrays, self.n_in, self.n_out = list(shards), n, n
        self.out_shape = [jax.ShapeDtypeStruct((4,) + s.shape, s.dtype) for s in shards]
        self.scratch = [pltpu.SemaphoreType.DMA((3 * n,)), pltpu.SemaphoreType.DMA((3 * n,)),
                        pltpu.SemaphoreType.DMA((n,))]

    def _copies(self, x_refs, out_refs, sems, receiving):
        send_sems, recv_sems, local_sems = sems
        x, y, c, chips = _place()
        s_me = 2 * x + y
        n = self.n_in

        def copy(a, k, s):
            return pltpu.make_async_remote_copy(
                src_ref=x_refs[a], dst_ref=out_refs[a].at[s], send_sem=send_sems.at[3 * a + k],
                recv_sem=recv_sems.at[3 * a + k], device_id=(*chips[k], c), device_id_type=MESH)

        mine = [pltpu.make_async_copy(x_refs[a], out_refs[a].at[s_me], local_sems.at[a]) for a in range(n)]
        sends = [copy(a, k, s_me) for a in range(n) for k in range(3)]
        if not receiving:
            return mine, sends
        return mine, sends, [copy(a, k, 2 * px + py) for a in range(n) for k, (px, py) in enumerate(chips)]

    def start(self, x_refs, out_refs, sems):
        mine, sends = self._copies(x_refs, out_refs, sems, False)
        for cp in mine + sends:
            cp.start()

    def wait(self, x_refs, out_refs, sems):
        mine, sends, recvs = self._copies(x_refs, out_refs, sems, True)
        for cp in recvs:
            cp.wait_recv()
        for cp in sends:
            cp.wait_send()
        for cp in mine:
            cp.wait()


class _ScatterChips:
    def __init__(self, parts):
        n = len(parts)
        self.arrays, self.n_in, self.n_out = list(parts), n, n
        self.out_shape = [jax.ShapeDtypeStruct((3,) + p.shape[1:], p.dtype) for p in parts]
        self.scratch = [pltpu.SemaphoreType.DMA((3 * n,)), pltpu.SemaphoreType.DMA((3 * n,))]

    def _copies(self, g_refs, out_refs, sems):
        send_sems, recv_sems = sems
        x, y, c, chips = _place()
        return [pltpu.make_async_remote_copy(
            src_ref=g_refs[a].at[2 * px + py], dst_ref=out_refs[a].at[k], send_sem=send_sems.at[3 * a + k],
            recv_sem=recv_sems.at[3 * a + k], device_id=(px, py, c), device_id_type=MESH)
            for a in range(self.n_in) for k, (px, py) in enumerate(chips)]

    def start(self, g_refs, out_refs, sems):
        for cp in self._copies(g_refs, out_refs, sems):
            cp.start()

    def wait(self, g_refs, out_refs, sems):
        sends = self._copies(g_refs, out_refs, sems)
        for cp in sends:
            cp.wait_recv()
        for cp in sends:
            cp.wait_send()


def _exchange_call(ex, *, name):
    def body(*refs):
        parts = (refs[:ex.n_in], refs[ex.n_in:ex.n_in + ex.n_out], refs[ex.n_in + ex.n_out:])
        ex.start(*parts)
        ex.wait(*parts)

    return pl.pallas_call(body, name=name, out_shape=ex.out_shape, in_specs=[ANY] * ex.n_in,
                          out_specs=[ANY] * ex.n_out, scratch_shapes=ex.scratch)(*ex.arrays)


def _swap_sibling(vs, *, name):
    n = len(vs)

    def body(*refs):
        v_refs, out_refs = refs[:n], refs[n:2 * n]
        send_sems, recv_sems = refs[2 * n:]
        x, y, c, _ = _place()
        cps = [pltpu.make_async_remote_copy(src_ref=v_refs[a], dst_ref=out_refs[a], send_sem=send_sems.at[a],
                                            recv_sem=recv_sems.at[a], device_id=(x, y, 1 - c), device_id_type=MESH)
               for a in range(n)]
        for cp in cps:
            cp.start()
        for cp in cps:
            cp.wait()

    return pl.pallas_call(
        body, name=name, out_shape=[jax.ShapeDtypeStruct(v.shape, v.dtype) for v in vs],
        in_specs=[ANY] * n, out_specs=[ANY] * n,
        scratch_shapes=[pltpu.SemaphoreType.DMA((n,)), pltpu.SemaphoreType.DMA((n,))],
    )(*vs)


def _sum_parts(own, others, *, name):
    R, C = own.shape
    tm = _pick(R, (256, 128, 64))

    def body(own_ref, o0_ref, o1_ref, o2_ref, out_ref):
        tot = own_ref[...].astype(F32)
        for ref in (o0_ref, o1_ref, o2_ref):
            tot = tot + ref[...].astype(F32)
        out_ref[...] = tot

    part = lambda k: pl.BlockSpec((None, tm, C), lambda i: (k, i, 0))
    return pl.pallas_call(
        body, name=name, grid=(R // tm,),
        in_specs=[pl.BlockSpec((tm, C), lambda i: (i, 0)), part(0), part(1), part(2)],
        out_specs=pl.BlockSpec((tm, C), lambda i: (i, 0)), out_shape=jax.ShapeDtypeStruct((R, C), F32),
        compiler_params=_cparams(("arbitrary",)),
    )(own, others, others, others)


def _adam_math(w_, m_, v_, g):
    m2 = ADAM_B1 * m_ + (1.0 - ADAM_B1) * g
    v2 = ADAM_B2 * v_ + (1.0 - ADAM_B2) * jnp.square(g)
    m_hat = m2 / (1.0 - ADAM_B1 ** ADAM_STEP)
    v_hat = v2 / (1.0 - ADAM_B2 ** ADAM_STEP)
    delta = -ADAM_LR * (m_hat / (jnp.sqrt(v_hat) + ADAM_EPS) + ADAM_WD * w_)
    return delta, m2, v2


SMALL_SLOTS = 16
SMALL_COLS = 6 * D


def _pack_small(grads, *, name):
    n = len(grads)

    def body(*refs):
        out_ref = refs[n]
        out_ref[...] = jnp.zeros_like(out_ref)
        for i, ref in enumerate(refs[:n]):
            out_ref[i:i + 1, 0:ref.shape[1]] = ref[...]

    return pl.pallas_call(body, name=name, out_shape=jax.ShapeDtypeStruct((SMALL_SLOTS, SMALL_COLS), F32))(*grads)


def _adamw_small(ws, ms, vs, gathered, *, name):
    n = len(ws)

    def body(*refs):
        w_refs, m_refs, v_refs, g_ref = refs[:n], refs[n:2 * n], refs[2 * n:3 * n], refs[3 * n]
        outs = refs[3 * n + 1:]
        for i in range(n):
            nc = w_refs[i].shape[1]
            g = g_ref[i:i + 1, 0:nc]
            for d in range(1, 8):
                g = g + g_ref[d * SMALL_SLOTS + i:d * SMALL_SLOTS + i + 1, 0:nc]
            delta, m2, v2 = _adam_math(w_refs[i][...], m_refs[i][...], v_refs[i][...], g)
            for k, val in enumerate((g, delta, m2, v2)):
                outs[k * n + i][...] = val

    shapes = [jax.ShapeDtypeStruct(w.shape, F32) for w in ws]
    res = pl.pallas_call(body, name=name, out_shape=shapes * 4,
                         compiler_params=pltpu.CompilerParams(vmem_limit_bytes=VMEM_LIMIT))(*ws, *ms, *vs, gathered)
    return [res[k * n:(k + 1) * n] for k in range(4)]


def _adamw(w, m, v, gparts, *, tm, name):
    def fn(i, n, w_, m_, v_, *gs):
        g = gs[0]
        for p in gs[1:]:
            g = g + p
        return (g,) + _adam_math(w_, m_, v_, g)
    nc = w.shape[1]
    return _rowwise(fn, [w, m, v] + list(gparts), [], [(nc, F32, 'row')] * 4, tm=tm, name=name)


WEIGHTS = ['ada_w', 'ada_b', 'norm1_gain', 'norm2_gain', 'w_in', 'tshift_mu', 'decay_w0', 'decay_up', 'iclr_a0',
           'iclr_up', 'gate_up', 'k_k', 'k_a', 'r_k', 'lnx_gain', 'lnx_bias', 'q_norm_gain', 'k_norm_gain', 'attn_sinks',
           'branch_gate_b', 'w_branch_a', 'w_branch_b', 'w_out', 'ffn_w1', 'ffn_w3', 'ffn_w2']
SHARDED = [('w_in', 1), ('decay_up', 1), ('iclr_up', 1), ('gate_up', 1), ('w_branch_a', 1), ('w_branch_b', 1),
           ('w_out', 0), ('ffn_w1', 1), ('ffn_w3', 1), ('ffn_w2', 0)]
SMALL = ['ada_b', 'norm1_gain', 'norm2_gain', 'tshift_mu', 'decay_w0', 'iclr_a0', 'k_k', 'k_a', 'r_k', 'lnx_gain',
         'lnx_bias', 'q_norm_gain', 'k_norm_gain', 'attn_sinks', 'branch_gate_b']


def kernel(x, c, positions, ada_w, ada_b, norm1_gain, norm2_gain, w_in, tshift_mu, decay_w0, decay_up, iclr_a0, iclr_up, gate_up, k_k, k_a, r_k, lnx_gain, lnx_bias, q_norm_gain, k_norm_gain, attn_sinks, branch_gate_b, w_branch_a, w_branch_b, w_out, ffn_w1, ffn_w3, ffn_w2, loss_target, m_ada_w, m_ada_b, m_norm1_gain, m_norm2_gain, m_w_in, m_tshift_mu, m_decay_w0, m_decay_up, m_iclr_a0, m_iclr_up, m_gate_up, m_k_k, m_k_a, m_r_k, m_lnx_gain, m_lnx_bias, m_q_norm_gain, m_k_norm_gain, m_attn_sinks, m_branch_gate_b, m_w_branch_a, m_w_branch_b, m_w_out, m_ffn_w1, m_ffn_w3, m_ffn_w2, v_ada_w, v_ada_b, v_norm1_gain, v_norm2_gain, v_w_in, v_tshift_mu, v_decay_w0, v_decay_up, v_iclr_a0, v_iclr_up, v_gate_up, v_k_k, v_k_a, v_r_k, v_lnx_gain, v_lnx_bias, v_q_norm_gain, v_k_norm_gain, v_attn_sinks, v_branch_gate_b, v_w_branch_a, v_w_branch_b, v_w_out, v_ffn_w1, v_ffn_w3, v_ffn_w2):
    a = dict(locals())
    W = {n: a[n] for n in WEIGHTS}
    M = {n: a['m_' + n] for n in WEIGHTS}
    V = {n: a['v_' + n] for n in WEIGHTS}
    xi, yi, ci = lax.axis_index("x"), lax.axis_index("y"), lax.axis_index("c")
    me = 4 * xi + 2 * yi + ci
    shard = 2 * xi + yi
    mat = lambda t: t.reshape(t.shape[-2], t.shape[-1])
    sharded = [n for n, _ in SHARDED]

    ax = dict(SHARDED)
    late = LATE_SCAN + LATE_ATTN
    early = [n for n in sharded if n not in late]
    shards = {n: mat(W[n]).astype(MXU) for n in sharded}
    gathered = _exchange_call(_GatherChips([shards[n] for n in early]), name="gather_weights")
    full = {n: _full_weight(g, ax[n]) for n, g in zip(early, gathered, strict=True)}

    c_all = _all_gather8(jnp.broadcast_to(c, (8, D)), name="gather_c")[0::8]
    pad_rows = lambda t: jnp.concatenate([t, jnp.zeros((BLK - 8, t.shape[1]), t.dtype)])
    c_all = pad_rows(c_all.astype(MXU))
    ada_cols = _mm_nn(c_all, mat(ada_w).astype(MXU), name="f_ada")[:8]
    ada_all = _all_gather8(ada_cols, name="gather_ada").reshape(2, 2, 2, 8, 6 * D // 4)
    ada_mine = lax.dynamic_index_in_dim(ada_all[:, :, 0], me, axis=2, keepdims=False)
    ada = ada_mine.reshape(1, 6 * D) + mat(ada_b)

    zero = jnp.zeros((64, RW), MXU)
    lora = jnp.concatenate([jnp.concatenate([full['decay_up'], zero], axis=1),
                            jnp.concatenate([zero, full['iclr_up']], axis=1)], axis=0)
    s = {n: W[n].reshape(1, -1) for n in SMALL if n != 'ada_b'}
    s['lora_up'] = lora.astype(F32)
    s['gate_up'] = full['gate_up'].astype(F32)
    tab = _rope_table(positions.reshape(-1))
    loss, dx, d_ada, gw, gs, from_chips = _local_step(x[0], loss_target[0], ada, tab, dict(w_in=full['w_in']), s,
                                                      shards={n: shards[n] for n in late})
    loss = lax.psum(loss[0, 0], ("x", "y", "c"))

    gs['ada_b'] = d_ada
    gsmall = _pack_small([gs[n] for n in SMALL], name="pack_small_grads")
    gsmall_all = _all_gather8(gsmall, name="gather_small_grads")
    row = lambda src: [src[n].reshape(1, -1) for n in SMALL]
    sm_out = _adamw_small(row(W), row(M), row(V), gsmall_all, name="adamw_small")
    sm_out = [{n: o.reshape(W[n].shape) for n, o in zip(SMALL, outs_k, strict=True)} for outs_k in sm_out]

    d_ada_all = gsmall_all[0::SMALL_SLOTS]
    d_ada_cols = lax.dynamic_slice_in_dim(d_ada_all, shard * (6 * D // 4), 6 * D // 4, axis=1)
    g_ada_w = _mm_tn(c_all, pad_rows(d_ada_cols.astype(MXU)), name="b_ada")
    ada_out = _adamw(mat(ada_w), mat(m_ada_w), mat(v_ada_w), [g_ada_w], tm=256, name="adamw_ada")

    parts =[_sum_parts(lax.dynamic_index_in_dim(gw[n], shard, axis=0, keepdims=False), from_chips[n], name="sum_" + n)
             for n in sharded]
    others = _swap_sibling(parts, name="swap_grads")
    sh_out = {}
    for n, part, other in zip(sharded, parts, others, strict=True):
        sh_out[n] = _adamw(mat(W[n]), mat(M[n]), mat(V[n]), [part, other], tm=_pick(part.shape[0], (256, 128, 64)),
                           name="adamw_" + n)

    def leaf(k, n):
        if n == 'ada_w':
            return ada_out[k].reshape(W[n].shape)
        if n in sharded:
            return sh_out[n][k].reshape(W[n].shape)
        return sm_out[k][n]
    outs = [leaf(k, n) for k in range(4) for n in WEIGHTS]
    return (loss, dx[None], *outs)
```

```python
import functools
import math

import jax
import jax.numpy as jnp
from jax import lax
from jax.experimental import pallas as pl
from jax.experimental.pallas import tpu as pltpu

F32 = jnp.float32
BF16 = jnp.bfloat16
MXU = BF16
HI = lax.Precision.HIGHEST

D = 1024
HD = 64
NH = 8
RW = NH * HD
SHIFT_W = 3 * RW + 64 + 64 + 128
QKV_W = RW + 2 * 128
GATE_W = 2 * D
IN_W = SHIFT_W + QKV_W + GATE_W
DFF = 2816
BLK = 128
CHUNK = 64
RMS_EPS = 1e-6
GN_EPS = 64e-5
NEG_INF = -1e30
ADAM_LR, ADAM_B1, ADAM_B2, ADAM_EPS, ADAM_WD, ADAM_STEP = 0.001, 0.9, 0.999, 1e-08, 0.01, 10
VMEM_LIMIT = 56 * 1024 * 1024
MESH = pl.DeviceIdType.MESH


def _cparams(sem=None):
    return pltpu.CompilerParams(dimension_semantics=sem, vmem_limit_bytes=VMEM_LIMIT)


def _full_spec(a):
    nd = a.ndim
    return pl.BlockSpec(a.shape, lambda *_: (0,) * nd)


def _rowwise(fn, rows, consts, outs, *, tm, name, halo=()):
    rows = [a if isinstance(a, tuple) else (a, a.shape[1]) for a in rows]
    T = rows[0][0].shape[0]
    assert T % tm == 0 and tm % 8 == 0
    n_tiles = T // tm
    n_in = len(rows) + len(halo) + len(consts)
    in_specs = [pl.BlockSpec((tm, nc), lambda i: (i, 0)) for _, nc in rows]
    args = [a for a, _ in rows]
    for a, nc, kind in halo:
        if kind == 'prev':
            in_specs.append(pl.BlockSpec((8, nc), lambda i: (jnp.maximum(i * (tm // 8) - 1, 0), 0)))
        else:
            in_specs.append(pl.BlockSpec((8, nc), lambda i: (jnp.minimum((i + 1) * (tm // 8), T // 8 - 1), 0)))
        args.append(a)
    in_specs += [_full_spec(a) for a in consts]
    args += list(consts)
    out_shape, out_specs = [], []
    for ncols, dtype, kind in outs:
        if kind == 'row':
            out_shape.append(jax.ShapeDtypeStruct((T, ncols), dtype))
            out_specs.append(pl.BlockSpec((tm, ncols), lambda i: (i, 0)))
        else:
            out_shape.append(jax.ShapeDtypeStruct((kind, ncols), dtype))
            out_specs.append(pl.BlockSpec((kind, ncols), lambda i: (0, 0)))

    def body(*refs):
        i = pl.program_id(0)
        vals = [r[...] for r in refs[:n_in]]
        res = fn(i, n_tiles, *vals)
        for (ncols, dtype, kind), o_ref, val in zip(outs, refs[n_in:], res, strict=True):
            if kind == 'row':
                o_ref[...] = val.astype(dtype)
            else:
                @pl.when(i == 0)
                def _():
                    o_ref[...] = jnp.zeros_like(o_ref)
                o_ref[...] += val.astype(dtype)

    res = pl.pallas_call(
        body, name=name, grid=(n_tiles,), in_specs=in_specs, out_specs=out_specs, out_shape=out_shape,
        compiler_params=_cparams(("arbitrary",)),
    )(*args)
    return res


def _pick(n, cands):
    for c in cands:
        if n % c == 0:
            return c
    return n


MM_ROWS = (1024, 512, 256, 128)
MM_COLS = (1536, 1408, 1024, 896, 768, 512, 256, 128)
MM_WIDE = 3000


def _mm_nn(a, w, *, name, out_dtype=F32):
    T, K = a.shape
    N = w.shape[1]
    tm = _pick(T, MM_ROWS)
    tn = _pick(N, MM_COLS)

    def body(a_ref, w_ref, o_ref):
        o_ref[...] = jnp.dot(a_ref[...], w_ref[...], preferred_element_type=F32).astype(out_dtype)

    return pl.pallas_call(
        body, name=name, grid=(N // tn, T // tm),
        in_specs=[pl.BlockSpec((tm, K), lambda j, i: (i, 0)), pl.BlockSpec((K, tn), lambda j, i: (0, j))],
        out_specs=pl.BlockSpec((tm, tn), lambda j, i: (i, j)),
        out_shape=jax.ShapeDtypeStruct((T, N), out_dtype),
        compiler_params=_cparams(("arbitrary", "arbitrary")),
    )(a, w)


def _mm_nt(dy, w, *, name, out_dtype=F32, hosted=None):
    T, N = dy.shape
    K = w.shape[0]
    tm = _pick(T, MM_ROWS if N <= MM_WIDE else MM_ROWS[1:])
    tk = _pick(K, MM_COLS[1:])
    grid = (K // tk, T // tm)
    h_in, h_in_specs, h_out_specs, h_out_shape, h_scratch = _hosted_args(hosted)

    def body(dy_ref, w_ref, o_ref):
        o_ref[...] = lax.dot_general(dy_ref[...], w_ref[...], (((1,), (1,)), ((), ())),
                                     preferred_element_type=F32).astype(out_dtype)

    res = pl.pallas_call(
        _hosting(body, hosted, 2, 1, 0, grid), name=name, grid=grid,
        in_specs=[pl.BlockSpec((tm, N), lambda j, i: (i, 0)), pl.BlockSpec((tk, N), lambda j, i: (j, 0))] + h_in_specs,
        out_specs=[pl.BlockSpec((tm, tk), lambda j, i: (i, j))] + h_out_specs,
        out_shape=[jax.ShapeDtypeStruct((T, K), out_dtype)] + h_out_shape, scratch_shapes=h_scratch,
        compiler_params=_cparams(("arbitrary", "arbitrary")),
    )(dy, w, *h_in)
    return res if hosted else res[0]


def _mm_tn(a, dy, *, name, out_dtype=F32, col_shards=None):
    T, K = a.shape
    N = dy.shape[1]
    tm = _pick(T, MM_ROWS)
    tn = N // col_shards if col_shards else _pick(N, MM_COLS[1:])
    n_t = T // tm

    def body(a_ref, dy_ref, o_ref, acc_ref):
        i = pl.program_id(1)

        @pl.when(i == 0)
        def _():
            acc_ref[...] = jnp.zeros_like(acc_ref)

        acc_ref[...] += lax.dot_general(a_ref[...], dy_ref[...], (((0,), (0,)), ((), ())), preferred_element_type=F32)

        @pl.when(i == n_t - 1)
        def _():
            o_ref[...] = acc_ref[...].astype(out_dtype)

    if col_shards:
        out_specs = pl.BlockSpec((None, K, tn), lambda j, i: (j, 0, 0))
        out_shape = jax.ShapeDtypeStruct((col_shards, K, tn), out_dtype)
    else:
        out_specs = pl.BlockSpec((K, tn), lambda j, i: (0, j))
        out_shape = jax.ShapeDtypeStruct((K, N), out_dtype)
    return pl.pallas_call(
        body, name=name, grid=(N // tn, n_t),
        in_specs=[pl.BlockSpec((tm, K), lambda j, i: (i, 0)), pl.BlockSpec((tm, tn), lambda j, i: (i, j))],
        out_specs=out_specs, out_shape=out_shape, scratch_shapes=[pltpu.VMEM((K, tn), F32)],
        compiler_params=_cparams(("arbitrary", "arbitrary")),
    )(a, dy)


def _seg_ones(n):
    r = lax.broadcasted_iota(jnp.int32, (n, n), 0) // HD
    c = lax.broadcasted_iota(jnp.int32, (n, n), 1) // HD
    return (r == c).astype(F32)


def _segsum_raw(x):
    ones = _seg_ones(x.shape[1])
    if MXU == F32:
        return jnp.dot(x, ones, precision=HI, preferred_element_type=F32)
    hi = x.astype(MXU)
    lo = (x - hi.astype(F32)).astype(MXU)
    ones = ones.astype(MXU)
    return jnp.dot(hi, ones, preferred_element_type=F32) + jnp.dot(lo, ones, preferred_element_type=F32)


@jax.custom_vjp
def _segsum(x):
    return _segsum_raw(x)


def _segsum_fwd(x):
    return _segsum_raw(x), None


def _segsum_bwd(_, g):
    return (_segsum_raw(g),)


_segsum.defvjp(_segsum_fwd, _segsum_bwd)


def _mxu(x):
    return x.astype(MXU)


@jax.custom_vjp
def _bdot(a, b):
    return jnp.dot(_mxu(a), _mxu(b), preferred_element_type=F32)


def _bdot_fwd(a, b):
    return _bdot(a, b), (a, b)


def _bdot_bwd(res, g):
    a, b = res
    da = lax.dot_general(_mxu(g), _mxu(b), (((1,), (1,)), ((), ())), preferred_element_type=F32)
    db = lax.dot_general(_mxu(a), _mxu(g), (((0,), (0,)), ((), ())), preferred_element_type=F32)
    return da.astype(a.dtype), db.astype(b.dtype)


_bdot.defvjp(_bdot_fwd, _bdot_bwd)


@jax.custom_vjp
def _bdot_nt(a, b):
    return lax.dot_general(_mxu(a), _mxu(b), (((1,), (1,)), ((), ())), preferred_element_type=F32)


def _bdot_nt_fwd(a, b):
    return _bdot_nt(a, b), (a, b)


def _bdot_nt_bwd(res, g):
    a, b = res
    da = jnp.dot(_mxu(g), _mxu(b), preferred_element_type=F32)
    db = lax.dot_general(_mxu(g), _mxu(a), (((0,), (0,)), ((), ())), preferred_element_type=F32)
    return da.astype(a.dtype), db.astype(b.dtype)


_bdot_nt.defvjp(_bdot_nt_fwd, _bdot_nt_bwd)


def _sigmoid(x):
    return 1.0 / (1.0 + jnp.exp(-x))


def _softplus(x):
    return jnp.maximum(x, 0.0) + jnp.log(1.0 + jnp.exp(jnp.minimum(x, -x)))


def _norm_mod(x, gain, scale, shift):
    inv = lax.rsqrt(jnp.mean(x * x, axis=-1, keepdims=True) + RMS_EPS)
    return (x * inv) * gain * (1.0 + scale) + shift


def _prep(mixed, decay_w0, lora_up, iclr_a0, gate_up, k_k, k_a):
    r = mixed[:, 0:RW]
    k = mixed[:, RW:2 * RW]
    v = mixed[:, 2 * RW:3 * RW]
    z = mixed[:, 3 * RW:3 * RW + 128]
    xg = mixed[:, 3 * RW + 128:]
    lane = lax.broadcasted_iota(jnp.int32, z.shape, 1)
    tz = jnp.where(lane < 64, jnp.tanh(z), z)
    lo = _bdot(tz, lora_up)
    w_log = -_softplus(-(decay_w0 + lo[:, :RW])) - 0.5
    lw = -jnp.exp(w_log)
    a_ic = _sigmoid(iclr_a0 + lo[:, RW:])
    g = _bdot(_sigmoid(xg), gate_up)
    kk = k * k_k
    kk = kk / jnp.maximum(jnp.sqrt(_segsum(kk * kk)), 1e-12)
    k_mod = k * (1.0 + (a_ic - 1.0) * k_a)
    return jnp.concatenate([r, lw, k_mod, v, -kk, kk * a_ic, g], axis=1)


def _post(y, r, k, v, g, lnx_gain, lnx_bias, r_k):
    mu = _segsum(y) * (1.0 / HD)
    yc = y - mu
    var = _segsum(yc * yc) * (1.0 / HD)
    yn = yc * lax.rsqrt(var + GN_EPS) * lnx_gain + lnx_bias
    bonus = _segsum(r * k * r_k) * v
    return (yn + bonus) * g


def _merge(pg, ma, mb, bias):
    gates = _sigmoid(pg + bias)
    return gates[:, :D] * ma + gates[:, D:] * mb


def _swiglu(u, v):
    return u * _sigmoid(u) * v


@functools.partial(jax.custom_vjp, nondiff_argnums=(1,))
def _lane_roll(x, s):
    return pltpu.roll(x, s, 1)


def _lane_roll_fwd(x, s):
    return pltpu.roll(x, s, 1), None


def _lane_roll_bwd(s, _, g):
    n = g.shape[1]
    return (pltpu.roll(g, (n - s) % n, 1),)


_lane_roll.defvjp(_lane_roll_fwd, _lane_roll_bwd)


def _rope(x, cos, sin_lo, sin_hi):
    n = x.shape[1]
    return x * cos + _lane_roll(x, n - 8) * sin_lo + _lane_roll(x, 8) * sin_hi


def _head_rms(x, gain):
    return x * lax.rsqrt(_segsum(x * x) * (1.0 / HD) + RMS_EPS) * gain


def _attn_block(qkv_c, qkv_p, tab_c, tab_p, qg, kg, sinks, first):
    def tabs(tab, n):
        return tab[:, 0:n], tab[:, RW:RW + n], tab[:, 2 * RW:2 * RW + n]

    qg = jnp.concatenate([qg] * NH, axis=1)
    kg = jnp.concatenate([kg] * 2, axis=1)
    q = _rope(_head_rms(qkv_c[:, :RW], qg), *tabs(tab_c, RW))
    k_c = _rope(_head_rms(qkv_c[:, RW:RW + 128], kg), *tabs(tab_c, 128))
    k_p = _rope(_head_rms(qkv_p[:, RW:RW + 128], kg), *tabs(tab_p, 128))
    kband = jnp.concatenate([k_p, k_c], axis=0)
    vband = jnp.concatenate([qkv_p[:, RW + 128:], qkv_c[:, RW + 128:]], axis=0)
    G = 4
    qi = lax.broadcasted_iota(jnp.int32, (G * BLK, 2 * BLK), 0) % BLK
    kj = lax.broadcasted_iota(jnp.int32, (G * BLK, 2 * BLK), 1)
    dist = qi + BLK - kj
    valid = (dist >= 0) & (dist < BLK) & (jnp.logical_not(first) | (kj >= BLK))
    row_g = lax.broadcasted_iota(jnp.int32, (G * BLK, 1), 0) // BLK
    outs = []
    for kvh in range(2):
        kb = kband[:, kvh * HD:(kvh + 1) * HD]
        vb = vband[:, kvh * HD:(kvh + 1) * HD]
        qs = jnp.concatenate([q[:, (G * kvh + g) * HD:(G * kvh + g + 1) * HD] for g in range(G)], axis=0)
        s = _bdot_nt(qs, kb) * (HD ** -0.5)
        s = jnp.where(valid, s, NEG_INF)
        sink = jnp.zeros((G * BLK, 1), F32)
        for g in range(G):
            sink = jnp.where(row_g == g, sinks[:, G * kvh + g:G * kvh + g + 1], sink)
        m = lax.stop_gradient(jnp.maximum(jnp.max(s, axis=-1, keepdims=True), sink))
        e = jnp.exp(s - m)
        p = e / (jnp.sum(e, axis=-1, keepdims=True) + jnp.exp(sink - m))
        o = _bdot(p, vb)
        outs += [o[g * BLK:(g + 1) * BLK] for g in range(G)]
    return jnp.concatenate(outs, axis=1)


def _heads(x):
    return jnp.stack([x[:, h * HD:(h + 1) * HD] for h in range(NH)], axis=0)


def _unheads(x):
    return jnp.concatenate([x[h] for h in range(NH)], axis=1)


def _split(x, n):
    parts, rest = [], x
    for _ in range(n):
        p = rest.astype(MXU)
        parts.append(p)
        rest = rest - p.astype(F32)
    return parts


def _bdot_batched(a, b, ca, cb):
    return lax.dot_general(a, b, (((ca,), (cb,)), ((0,), (0,))), preferred_element_type=F32)


def _bmm_passes(a, b, ca, cb, passes):
    if MXU == F32:
        return lax.dot_general(a, b, (((ca,), (cb,)), ((0,), (0,))), precision=HI, preferred_element_type=F32)
    if passes == 1:
        return _bdot_batched(a.astype(MXU), b.astype(MXU), ca, cb)
    (a0, a1), (b0, b1) = _split(a, 2), _split(b, 2)
    return _bdot_batched(a0, b0, ca, cb) + (_bdot_batched(a0, b1, ca, cb) + _bdot_batched(a1, b0, ca, cb))


@functools.partial(jax.custom_vjp, nondiff_argnums=(2, 3, 4))
def _bmm(a, b, ca, cb, passes=1):
    return _bmm_passes(a, b, ca, cb, passes)


def _bmm_fwd(a, b, ca, cb, passes):
    return _bmm_passes(a, b, ca, cb, passes), (a, b)


def _bmm_bwd(ca, cb, passes, res, g):
    a, b = res
    if (ca, cb) == (2, 1):
        return _bmm_passes(g, b, 2, 2, passes), _bmm_passes(a, g, 1, 1, passes)
    if (ca, cb) == (2, 2):
        return _bmm_passes(g, b, 2, 1, passes), _bmm_passes(g, a, 1, 1, passes)
    return _bmm_passes(b, g, 2, 2, passes), _bmm_passes(a, g, 2, 1, passes)


_bmm.defvjp(_bmm_fwd, _bmm_bwd)


def _tri_dot(x, transpose):
    C = x.shape[1]
    ri = lax.broadcasted_iota(jnp.int32, (C, C), 0)
    ci = lax.broadcasted_iota(jnp.int32, (C, C), 1)
    tri = jnp.broadcast_to(((ri <= ci) if transpose else (ri >= ci)).astype(MXU), (x.shape[0], C, C))
    if MXU == F32:
        return lax.dot_general(tri, x, (((2,), (1,)), ((0,), (0,))), precision=HI, preferred_element_type=F32)
    p0, p1, p2 = _split(x, 3)
    return _bdot_batched(tri, p0, 2, 1) + (_bdot_batched(tri, p1, 2, 1) + _bdot_batched(tri, p2, 2, 1))


@jax.custom_vjp
def _cumsum_rows(x):
    return _tri_dot(x, False)


def _cumsum_rows_fwd(x):
    return _tri_dot(x, False), None


def _cumsum_rows_bwd(_, g):
    return (_tri_dot(g, True),)


_cumsum_rows.defvjp(_cumsum_rows_fwd, _cumsum_rows_bwd)

P_SCORE = 1
P_SOLVE = 1
P_STATE = 1


def _chunk(S0, r, lw, k, v, a, b):
    C = r.shape[1]
    ri = lax.broadcasted_iota(jnp.int32, (C, C), 0)
    ci = lax.broadcasted_iota(jnp.int32, (C, C), 1)
    incl = (ri >= ci)
    strict = (ri > ci)
    eye = (ri == ci).astype(F32)
    cum = _cumsum_rows(lw)
    p_in = jnp.exp(cum)
    p_ex = jnp.exp(cum - lw)
    p_inv = jnp.exp(-cum)
    at, rt, bt, kt = a * p_ex, r * p_in, b * p_inv, k * p_inv
    sc = _bmm(jnp.concatenate([at, rt], axis=1), jnp.concatenate([bt, kt], axis=1), 2, 2, P_SCORE)
    a_ab = jnp.where(strict, sc[:, :C, :C], 0.0)
    a_ak = jnp.where(strict, sc[:, :C, C:], 0.0)
    a_rb = jnp.where(incl, sc[:, C:, :C], 0.0)
    a_rk = jnp.where(incl, sc[:, C:, C:], 0.0)
    s0 = _bmm(jnp.concatenate([at, rt], axis=1), S0, 2, 2, P_STATE)
    rhs = s0[:, :C] + _bmm(a_ak, v, 2, 1, P_SCORE)
    x = eye + a_ab
    lp = a_ab
    for _ in range(int(math.log2(C)) - 1):
        lp = _bmm(lp, lp, 2, 1, P_SOLVE)
        x = x + _bmm(x, lp, 2, 1, P_SOLVE)
    u = _bmm(x, rhs, 2, 1, P_SOLVE)
    y = s0[:, C:] + _bmm(jnp.concatenate([a_rb, a_rk], axis=2), jnp.concatenate([u, v], axis=1), 2, 1, P_SCORE)
    p_last = jnp.exp(cum[:, C - 1:C, :])
    S1 = (S0 + _bmm(jnp.concatenate([u, v], axis=1), jnp.concatenate([bt, kt], axis=1), 1, 1, P_STATE)) * p_last
    return y, S1


def _hosting(body, ex, n_in, n_out, n_scratch, n_steps):
    if ex is None:
        return body

    def wrapped(*refs):
        a = n_in
        b = a + ex.n_in
        c = b + n_out
        d = c + ex.n_out
        e = d + n_scratch
        ex_refs = (refs[a:b], refs[c:d], refs[e:])
        grid = n_steps if isinstance(n_steps, tuple) else (n_steps,)
        first = last = True
        for ax_, size in enumerate(grid):
            first = first & (pl.program_id(ax_) == 0)
            last = last & (pl.program_id(ax_) == size - 1)

        @pl.when(first)
        def _():
            ex.start(*ex_refs)

        body(*refs[:a], *refs[b:c], *refs[d:e])

        @pl.when(last)
        def _():
            ex.wait(*ex_refs)

    return wrapped


def _hosted_args(ex):
    if ex is None:
        return [], [], [], [], []
    any_spec = pl.BlockSpec(memory_space=pl.ANY)
    return list(ex.arrays), [any_spec] * ex.n_in, [any_spec] * ex.n_out, list(ex.out_shape), list(ex.scratch)


def _scan_fwd(rw, *, name, hosted=None):
    T = rw.shape[0]
    n = T // CHUNK
    h_in, h_in_specs, h_out_specs, h_out_shape, h_scratch = _hosted_args(hosted)

    def body(r_ref, lw_ref, k_ref, v_ref, a_ref, b_ref, y_ref, ck_ref, s_ref):
        @pl.when(pl.program_id(0) == 0)
        def _():
            s_ref[...] = jnp.zeros_like(s_ref)

        S0 = s_ref[...]
        ck_ref[0] = S0
        y, S1 = _chunk(S0, *[_heads(ref[...]) for ref in (r_ref, lw_ref, k_ref, v_ref, a_ref, b_ref)])
        y_ref[...] = _unheads(y)
        s_ref[...] = S1

    col = lambda j: pl.BlockSpec((CHUNK, RW), lambda i: (i, j))
    return pl.pallas_call(
        _hosting(body, hosted, 6, 2, 1, n), name=name, grid=(n,),
        in_specs=[col(j) for j in range(6)] + h_in_specs,
        out_specs=[pl.BlockSpec((CHUNK, RW), lambda i: (i, 0)),
                   pl.BlockSpec((1, NH, HD, HD), lambda i: (i, 0, 0, 0))] + h_out_specs,
        out_shape=[jax.ShapeDtypeStruct((T, RW), F32), jax.ShapeDtypeStruct((n, NH, HD, HD), F32)] + h_out_shape,
        scratch_shapes=[pltpu.VMEM((NH, HD, HD), F32)] + h_scratch,
        compiler_params=_cparams(("arbitrary",)),
    )(rw, rw, rw, rw, rw, rw, *h_in)


def _scan_bwd(rw, ck, dy, *, name, hosted=None):
    T = rw.shape[0]
    n = T // CHUNK

    def body(r_ref, lw_ref, k_ref, v_ref, a_ref, b_ref, ck_ref, dy_ref, o_ref, ds_ref):
        @pl.when(pl.program_id(0) == 0)
        def _():
            ds_ref[...] = jnp.zeros_like(ds_ref)

        prim = [_heads(ref[...]) for ref in (r_ref, lw_ref, k_ref, v_ref, a_ref, b_ref)]
        _, vjp = jax.vjp(_chunk, ck_ref[0], *prim)
        grads = vjp((_heads(dy_ref[...]), ds_ref[...]))
        ds_ref[...] = grads[0]
        o_ref[...] = jnp.concatenate([_unheads(g) for g in grads[1:]], axis=1)

    h_in, h_in_specs, h_out_specs, h_out_shape, h_scratch = _hosted_args(hosted)
    col = lambda j: pl.BlockSpec((CHUNK, RW), lambda i: (n - 1 - i, j))
    return pl.pallas_call(
        _hosting(body, hosted, 8, 1, 1, n), name=name, grid=(n,),
        in_specs=[col(j) for j in range(6)] + [pl.BlockSpec((1, NH, HD, HD), lambda i: (n - 1 - i, 0, 0, 0)),
                                               pl.BlockSpec((CHUNK, RW), lambda i: (n - 1 - i, 0))] + h_in_specs,
        out_specs=[pl.BlockSpec((CHUNK, 6 * RW), lambda i: (n - 1 - i, 0))] + h_out_specs,
        out_shape=[jax.ShapeDtypeStruct((T, 6 * RW), F32)] + h_out_shape,
        scratch_shapes=[pltpu.VMEM((NH, HD, HD), F32)] + h_scratch,
        compiler_params=_cparams(("arbitrary",)),
    )(rw, rw, rw, rw, rw, rw, ck, dy, *h_in)


ATTN_BLOCKS = (4, 2, 1)

def _attn_fwd(qkv, tab, qg, kg, sinks, *, name, hosted=None):
    T = qkv.shape[0]
    nb = _pick(T // BLK, ATTN_BLOCKS)
    n = T // (BLK * nb)
    h_in, h_in_specs, h_out_specs, h_out_shape, h_scratch = _hosted_args(hosted)

    def body(c_ref, p_ref, tc_ref, tp_ref, qg_ref, kg_ref, s_ref, o_ref):
        for b in range(nb):
            rows = slice(b * BLK, (b + 1) * BLK)
            before = slice((b - 1) * BLK, b * BLK)
            p, tp = (p_ref[...], tp_ref[...]) if b == 0 else (c_ref[before, :], tc_ref[before, :])
            first = (pl.program_id(0) == 0) if b == 0 else False
            o_ref[rows, :] = _attn_block(c_ref[rows, :], p, tc_ref[rows, :], tp, qg_ref[...], kg_ref[...],
                                         s_ref[...], first).astype(o_ref.dtype)

    cur = lambda w: pl.BlockSpec((nb * BLK, w), lambda i: (i, 0))
    prev = lambda w: pl.BlockSpec((BLK, w), lambda i: (jnp.maximum(i * nb - 1, 0), 0))
    return pl.pallas_call(
        _hosting(body, hosted, 7, 1, 0, n), name=name, grid=(n,),
        in_specs=[cur(QKV_W), prev(QKV_W), cur(3 * RW), prev(3 * RW), _full_spec(qg), _full_spec(kg),
                  _full_spec(sinks)] + h_in_specs,
        out_specs=[cur(RW)] + h_out_specs, out_shape=[jax.ShapeDtypeStruct((T, RW), MXU)] + h_out_shape,
        scratch_shapes=h_scratch,
        compiler_params=_cparams(("arbitrary",)),
    )(qkv, qkv, tab, tab, qg, kg, sinks, *h_in)


def _attn_bwd(qkv, tab, qg, kg, sinks, dy, *, name, hosted=None):
    T = qkv.shape[0]
    nb = _pick(T // BLK, ATTN_BLOCKS)
    n = T // (BLK * nb)
    h_in, h_in_specs, h_out_specs, h_out_shape, h_scratch = _hosted_args(hosted)

    def body(c_ref, p_ref, tc_ref, tp_ref, qg_ref, kg_ref, s_ref, dy_ref, dqkv_ref, dqg_ref, dkg_ref, ds_ref, carry_ref):
        i = pl.program_id(0)

        @pl.when(i == 0)
        def _():
            carry_ref[...] = jnp.zeros_like(carry_ref)
            dqg_ref[...] = jnp.zeros_like(dqg_ref)
            dkg_ref[...] = jnp.zeros_like(dkg_ref)
            ds_ref[...] = jnp.zeros_like(ds_ref)

        carry = carry_ref[...]
        dqg_t, dkg_t, ds_t = jnp.zeros_like(dqg_ref), jnp.zeros_like(dkg_ref), jnp.zeros_like(ds_ref)
        for b in reversed(range(nb)):
            rows = slice(b * BLK, (b + 1) * BLK)
            before = slice((b - 1) * BLK, b * BLK)
            tc = tc_ref[rows, :]
            p, tp = (p_ref[...], tp_ref[...]) if b == 0 else (c_ref[before, :], tc_ref[before, :])
            first = (i == n - 1) if b == 0 else False
            f = lambda c, p_, qg_, kg_, sk, tc=tc, tp=tp, first=first: _attn_block(c, p_, tc, tp, qg_, kg_, sk, first)
            _, vjp = jax.vjp(f, c_ref[rows, :], p, qg_ref[...], kg_ref[...], s_ref[...])
            dc, dp, dqg, dkg, dsk = vjp(dy_ref[rows, :].astype(F32))
            dqkv_ref[rows, :] = (dc + carry).astype(dqkv_ref.dtype)
            carry = dp
            dqg_t, dkg_t, ds_t = dqg_t + dqg, dkg_t + dkg, ds_t + dsk
        carry_ref[...] = carry
        dqg_ref[...] += dqg_t
        dkg_ref[...] += dkg_t
        ds_ref[...] += ds_t

    cur = lambda w: pl.BlockSpec((nb * BLK, w), lambda i: (n - 1 - i, 0))
    prev = lambda w: pl.BlockSpec((BLK, w), lambda i: (jnp.maximum((n - 1 - i) * nb - 1, 0), 0))
    return pl.pallas_call(
        _hosting(body, hosted, 8, 4, 1, n), name=name, grid=(n,),
        in_specs=[cur(QKV_W), prev(QKV_W), cur(3 * RW), prev(3 * RW), _full_spec(qg), _full_spec(kg), _full_spec(sinks),
                  cur(RW)] + h_in_specs,
        out_specs=[cur(QKV_W), _full_spec(qg), _full_spec(kg), _full_spec(sinks)] + h_out_specs,
        out_shape=[jax.ShapeDtypeStruct((T, QKV_W), MXU), jax.ShapeDtypeStruct(qg.shape, F32),
                   jax.ShapeDtypeStruct(kg.shape, F32), jax.ShapeDtypeStruct(sinks.shape, F32)] + h_out_shape,
        scratch_shapes=[pltpu.VMEM((BLK, QKV_W), F32)] + h_scratch,
        compiler_params=_cparams(("arbitrary",)),
    )(qkv, qkv, tab, tab, qg, kg, sinks, dy, *h_in)


def _shift_down(cur, prev8, i):
    rolled = pltpu.roll(cur, 1, 0)
    first_row = jnp.where(i > 0, prev8[7:8, :], 0.0)
    row = lax.broadcasted_iota(jnp.int32, cur.shape, 0)
    return jnp.where(row == 0, first_row, rolled)


def _shift_up(cur, next8, i, n):
    tm = cur.shape[0]
    rolled = pltpu.roll(cur, tm - 1, 0)
    last_row = jnp.where(i < n - 1, next8[0:1, :], 0.0)
    row = lax.broadcasted_iota(jnp.int32, cur.shape, 0)
    return jnp.where(row == tm - 1, last_row, rolled)


def _ada_parts(ada):
    return [ada[:, j * D:(j + 1) * D] for j in range(6)]


def _rope_table(positions):
    half = HD // 8
    inv_freq = 500000.0 ** (-jnp.arange(half, dtype=F32) / half)
    ang = positions.astype(F32)[:, None] * inv_freq
    cos, sin = jnp.cos(ang), jnp.sin(ang)
    T = positions.shape[0]
    pad = HD - 2 * half
    c64 = jnp.concatenate([cos, cos, jnp.ones((T, pad), F32)], axis=1)
    lo64 = jnp.concatenate([-sin, jnp.zeros((T, HD - half), F32)], axis=1)
    hi64 = jnp.concatenate([jnp.zeros((T, half), F32), sin, jnp.zeros((T, pad), F32)], axis=1)
    return jnp.concatenate([jnp.tile(t, (1, NH)) for t in (c64, lo64, hi64)], axis=1)


LATE_SCAN = ['ffn_w1', 'ffn_w3']
LATE_ATTN = ['ffn_w2', 'w_branch_a', 'w_branch_b', 'w_out']
BACK_ATTN = ['ffn_w1', 'ffn_w3', 'ffn_w2']
BACK_SCAN = ['w_out', 'w_branch_a', 'w_branch_b']
BACK_LAST = ['w_in', 'decay_up', 'iclr_up', 'gate_up']


def _full_weight(g, ax):
    return g.reshape(-1, g.shape[2]) if ax == 0 else jnp.concatenate([g[j] for j in range(4)], axis=1)


def _local_step(x, target, ada, tab, w, s, shards=None):
    T = x.shape[0]
    tm = _pick(T, (512, 256, 128))
    tm_wide = _pick(T, (256, 128))
    tm_vjp = _pick(T, (128,))
    row = lambda n, dt=F32: (n, dt, 'row')
    acc = lambda n, r=1: (n, F32, r)

    def f_norm1(i, n, x_, g, ada_):
        sh, sc = ada_[:, 0:D], ada_[:, D:2 * D]
        return (_norm_mod(x_, g, sc, sh),)
    (h1,) = _rowwise(f_norm1, [x], [s['norm1_gain'], ada], [row(D, MXU)], tm=tm, name="f_norm1")
    proj = _mm_nn(h1, w['w_in'][:, :SHIFT_W], name="f_proj_shift")
    proj_qkv = _mm_nn(h1, w['w_in'][:, SHIFT_W:SHIFT_W + QKV_W], name="f_proj_qkv")
    proj_g = _mm_nn(h1, w['w_in'][:, SHIFT_W + QKV_W:], name="f_proj_gates", out_dtype=MXU)
    prep_consts = [s['decay_w0'], s['lora_up'], s['iclr_a0'], s['gate_up'], s['k_k'], s['k_a']]

    def f_prep(i, n, cur, prev8, mu, *params):
        mixed = cur + (_shift_down(cur, prev8, i) - cur) * mu
        return (_prep(mixed, *params),)
    (rw,) = _rowwise(f_prep, [(proj, SHIFT_W)], [s['tshift_mu']] + prep_consts, [row(7 * RW)], tm=tm_wide,
                     name="f_prep", halo=[(proj, SHIFT_W, 'prev')])
    ax = dict(SHARDED)
    w = dict(w)
    y, ck, *got = _scan_fwd(rw, name="f_scan", hosted=shards and _GatherChips([shards[n] for n in LATE_SCAN]))
    if shards:
        w['ffn_w13'] = jnp.concatenate([_full_weight(g, ax[n]) for n, g in zip(LATE_SCAN, got, strict=True)], axis=1)
    post_consts = [s['lnx_gain'], s['lnx_bias'], s['r_k']]

    def post_of(y_, rw_, *params):
        return _post(y_, rw_[:, 0:RW], rw_[:, 2 * RW:3 * RW], rw_[:, 3 * RW:4 * RW], rw_[:, 6 * RW:7 * RW], *params)

    def f_post(i, n, y_, rw_, *params):
        return (post_of(y_, rw_, *params),)
    (ya,) = _rowwise(f_post, [y, rw], post_consts, [row(RW, MXU)], tm=tm_wide, name="f_post")
    yb, *got = _attn_fwd(proj_qkv, tab, s['q_norm_gain'], s['k_norm_gain'], s['attn_sinks'], name="f_attn",
                         hosted=shards and _GatherChips([shards[n] for n in LATE_ATTN]))
    if shards:
        w.update({n: _full_weight(g, ax[n]) for n, g in zip(LATE_ATTN, got, strict=True)})
    ma = _mm_nn(ya, w['w_branch_a'], name="f_branch_a", out_dtype=MXU)
    mb = _mm_nn(yb, w['w_branch_b'], name="f_branch_b", out_dtype=MXU)

    def f_merge(i, n, pg, ma_, mb_, bias):
        return (_merge(pg.astype(F32), ma_.astype(F32), mb_.astype(F32), bias),)
    (merged,) = _rowwise(f_merge, [proj_g, ma, mb], [s['branch_gate_b']], [row(D, MXU)], tm=tm, name="f_merge")
    mo = _mm_nn(merged, w['w_out'], name="f_out")

    def f_res1(i, n, x_, mo_, g, ada_):
        x1_ = x_ + ada_[:, 2 * D:3 * D] * mo_
        return x1_, _norm_mod(x1_, g, ada_[:, 4 * D:5 * D], ada_[:, 3 * D:4 * D])
    x1, h2 = _rowwise(f_res1, [x, mo], [s['norm2_gain'], ada], [row(D), row(D, MXU)], tm=tm, name="f_res1")
    uv = _mm_nn(h2, w['ffn_w13'], name="f_ffn_in", out_dtype=MXU)

    def f_act(i, n, uv_):
        return (_swiglu(uv_[:, :DFF].astype(F32), uv_[:, DFF:].astype(F32)),)
    (act,) = _rowwise(f_act, [uv], [], [row(DFF, MXU)], tm=tm_wide, name="f_act")
    ff = _mm_nn(act, w['ffn_w2'], name="f_ffn_out")

    def f_loss(i, n, x1_, ff_, tgt, ada_):
        g2 = ada_[:, 5 * D:6 * D]
        err = x1_ + g2 * ff_ - tgt
        dx2 = err * (1.0 / D)
        loss = 0.5 * jnp.sum(jnp.sum(err * err, axis=1, keepdims=True) * (1.0 / D), axis=0, keepdims=True)
        return dx2, (dx2 * g2), jnp.broadcast_to(loss, (1, 128)), jnp.sum(dx2 * ff_, axis=0, keepdims=True)
    dx2, dff, loss, dgate2 = _rowwise(f_loss, [x1, ff, target], [ada], [row(D), row(D, MXU), acc(128), acc(D)],
                                      tm=tm, name="f_loss")

    dact = _mm_nt(dff, w['ffn_w2'], name="b_ffn_out_dx", out_dtype=MXU)
    g_w2 = _mm_tn(act, dff, name="b_ffn_out_dw", out_dtype=MXU)

    def b_act(i, n, uv_, dact_):
        _, vjp = jax.vjp(_swiglu, uv_[:, :DFF].astype(F32), uv_[:, DFF:].astype(F32))
        du, dv = vjp(dact_.astype(F32))
        return (jnp.concatenate([du, dv], axis=1),)
    (duv,) = _rowwise(b_act, [uv, dact], [], [row(2 * DFF, MXU)], tm=tm_wide, name="b_act")
    dh2 = _mm_nt(duv, w['ffn_w13'], name="b_ffn_in_dx")
    g_w13 = _mm_tn(h2, duv, name="b_ffn_in_dw", out_dtype=MXU)

    def b_res1(i, n, x1_, dh2_, dx2_, mo_, g, ada_):
        _, vjp = jax.vjp(_norm_mod, x1_, g, ada_[:, 4 * D:5 * D], ada_[:, 3 * D:4 * D])
        dxn, dg, dsc, dsh = vjp(dh2_)
        dx1_ = dxn + dx2_
        g1 = ada_[:, 2 * D:3 * D]
        return dx1_, dx1_ * g1, dg, dsc, dsh, jnp.sum(dx1_ * mo_, axis=0, keepdims=True)
    dx1, dmo, d_gain2, d_scale2, d_shift2, dgate1 = _rowwise(
        b_res1, [x1, dh2, dx2, mo], [s['norm2_gain'], ada], [row(D), row(D, MXU), acc(D), acc(D), acc(D), acc(D)],
        tm=tm, name="b_res1")
    dmerged = _mm_nt(dmo, w['w_out'], name="b_out_dx", out_dtype=MXU)
    g_wout = _mm_tn(merged, dmo, name="b_out_dw", out_dtype=MXU)

    def b_merge(i, n, pg, ma_, mb_, dm, bias):
        _, vjp = jax.vjp(_merge, pg.astype(F32), ma_.astype(F32), mb_.astype(F32), bias)
        dpg, dma_, dmb_, dbias = vjp(dm.astype(F32))
        return dpg, dma_, dmb_, dbias
    dpg, dma, dmb, d_bias = _rowwise(b_merge, [proj_g, ma, mb, dmerged], [s['branch_gate_b']],
                                     [row(GATE_W, MXU), row(D, MXU), row(D, MXU), acc(GATE_W)], tm=tm_wide, name="b_merge")
    dya = _mm_nt(dma, w['w_branch_a'], name="b_branch_a_dx")
    g_wa = _mm_tn(ya, dma, name="b_branch_a_dw", out_dtype=MXU, col_shards=4)
    dyb = _mm_nt(dmb, w['w_branch_b'], name="b_branch_b_dx", out_dtype=F32)
    g_wb = _mm_tn(yb, dmb, name="b_branch_b_dw", out_dtype=MXU, col_shards=4)
    fs = DFF // 4
    gw = dict(w_branch_a=g_wa, w_branch_b=g_wb, w_out=g_wout.reshape(4, D // 4, D),
              ffn_w1=jnp.stack([g_w13[:, j * fs:(j + 1) * fs] for j in range(4)]),
              ffn_w3=jnp.stack([g_w13[:, DFF + j * fs:DFF + (j + 1) * fs] for j in range(4)]),
              ffn_w2=g_w2.reshape(4, fs, D))
    recv = {}
    dqkv, d_qg, d_kg, d_sinks, *got = _attn_bwd(
        proj_qkv, tab, s['q_norm_gain'], s['k_norm_gain'], s['attn_sinks'], dyb, name="b_attn",
        hosted=shards and _ScatterChips([gw[n] for n in BACK_ATTN]))
    recv.update(zip(BACK_ATTN, got))

    def b_post(i, n, y_, rw_, dya_, *params):
        _, vjp = jax.vjp(post_of, y_, rw_, *params)
        return vjp(dya_)
    dy, drw_post, d_lnx_gain, d_lnx_bias, d_r_k = _rowwise(
        b_post, [y, rw, dya], post_consts, [row(RW), row(7 * RW), acc(RW), acc(RW), acc(RW)], tm=tm_vjp, name="b_post")
    dscan, *got = _scan_bwd(rw, ck, dy, name="b_scan",
                            hosted=shards and _ScatterChips([gw[n] for n in BACK_SCAN]))
    recv.update(zip(BACK_SCAN, got))

    def b_prep(i, n, cur, drw_, dscan_, prev8, mu, *params):
        shifted = _shift_down(cur, prev8, i)
        mixed = cur + (shifted - cur) * mu
        _, vjp = jax.vjp(_prep, mixed, *params)
        ct = drw_ + jnp.concatenate([dscan_, jnp.zeros((dscan_.shape[0], RW), F32)], axis=1)
        grads = vjp(ct)
        dmixed = grads[0]
        return (dmixed, jnp.sum(dmixed * (shifted - cur), axis=0, keepdims=True)) + tuple(grads[1:])
    dmixed, d_mu, d_w0, d_lora, d_a0, d_gate_up, d_kk, d_ka = _rowwise(
        b_prep, [(proj, SHIFT_W), drw_post, dscan], [s['tshift_mu']] + prep_consts,
        [row(SHIFT_W), acc(SHIFT_W), acc(RW), acc(2 * RW, 128), acc(RW), acc(RW, 128), acc(RW), acc(RW)],
        tm=tm_vjp, name="b_prep", halo=[(proj, SHIFT_W, 'prev')])

    def b_gather(i, n, dm, dqkv_, dpg_, next8, mu):
        dcur = dm * (1.0 - mu) + _shift_up(dm, next8, i, n) * mu
        return (jnp.concatenate([dcur.astype(MXU), dqkv_, dpg_], axis=1),)
    (dproj,) = _rowwise(b_gather, [dmixed, dqkv, dpg], [s['tshift_mu']], [row(IN_W, MXU)], tm=tm_wide, name="b_gather",
                        halo=[(dmixed, SHIFT_W, 'next')])
    g_win = _mm_tn(h1, dproj, name="b_proj_dw", out_dtype=MXU, col_shards=4)

    def col_blocks(g):
        k, n = g.shape
        return g.reshape(k, 4, n // 4).transpose(1, 0, 2).astype(MXU)
    gw.update(w_in=g_win, decay_up=col_blocks(d_lora[:64, :RW]), iclr_up=col_blocks(d_lora[64:, RW:]),
              gate_up=col_blocks(d_gate_up))
    if shards:
        dh1, *got = _mm_nt(dproj, w['w_in'], name="b_proj_dx", hosted=_ScatterChips([gw[n] for n in BACK_LAST]))
        recv.update(zip(BACK_LAST, got))
    else:
        dh1 = _mm_nt(dproj, w['w_in'], name="b_proj_dx")

    def b_norm1(i, n, x_, dh1_, dx1_, g, ada_):
        _, vjp = jax.vjp(_norm_mod, x_, g, ada_[:, D:2 * D], ada_[:, 0:D])
        dxn, dg, dsc, dsh = vjp(dh1_)
        return dxn + dx1_, dg, dsc, dsh
    dx, d_gain1, d_scale1, d_shift1 = _rowwise(b_norm1, [x, dh1, dx1], [s['norm1_gain'], ada],
                                               [row(D), acc(D), acc(D), acc(D)], tm=tm, name="b_norm1")

    d_ada = jnp.concatenate([d_shift1, d_scale1, dgate1, d_shift2, d_scale2, dgate2], axis=1)
    gs = dict(norm1_gain=d_gain1, norm2_gain=d_gain2, tshift_mu=d_mu, decay_w0=d_w0, iclr_a0=d_a0, k_k=d_kk, k_a=d_ka,
              r_k=d_r_k, lnx_gain=d_lnx_gain, lnx_bias=d_lnx_bias, q_norm_gain=d_qg, k_norm_gain=d_kg,
              attn_sinks=d_sinks, branch_gate_b=d_bias)
    return loss, dx, d_ada, gw, gs, recv


ANY = pl.BlockSpec(memory_space=pl.ANY)


def _place():
    x, y, c = lax.axis_index("x"), lax.axis_index("y"), lax.axis_index("c")
    return x, y, c, [(1 - x, y), (x, 1 - y), (1 - x, 1 - y)]


def _all_gather8(x_shard, *, name):
    m_per, n = x_shard.shape

    def body(x_ref, out_ref, send_sems, recv_sems, local_sem):
        x, y, c, chips = _place()
        me, sibling = (x, y, c), (x, y, 1 - c)

        def rows(px, py, pc):
            return out_ref.at[pl.ds((4 * px + 2 * py + pc) * m_per, m_per), :]

        def copy(k, block, to, src=None):
            return pltpu.make_async_remote_copy(
                src_ref=rows(*block) if src is None else src, dst_ref=rows(*block),
                send_sem=send_sems.at[k], recv_sem=recv_sems.at[k], device_id=to, device_id_type=MESH)

        mine = pltpu.make_async_copy(x_ref, rows(*me), local_sem)
        mine.start()
        first = [copy(0, me, sibling, src=x_ref)]
        first += [copy(1 + j, me, (*chip, c), src=x_ref) for j, chip in enumerate(chips)]
        for cp in first:
            cp.start()
        passed = [copy(4 + j, (*chip, c), sibling) for j, chip in enumerate(chips)]
        for j, chip in enumerate(chips):
            copy(1 + j, (*chip, c), me).wait_recv()
            passed[j].start()
        copy(0, sibling, me).wait_recv()
        for j, chip in enumerate(chips):
            copy(4 + j, (*chip, 1 - c), me).wait_recv()
        for cp in first + passed:
            cp.wait_send()
        mine.wait()

    return pl.pallas_call(
        body, name=name, out_shape=jax.ShapeDtypeStruct((8 * m_per, n), x_shard.dtype),
        in_specs=[pl.BlockSpec(memory_space=pltpu.VMEM)], out_specs=pl.BlockSpec(memory_space=pltpu.VMEM),
        scratch_shapes=[pltpu.SemaphoreType.DMA((7,)), pltpu.SemaphoreType.DMA((7,)), pltpu.SemaphoreType.DMA],
    )(x_shard)


class _GatherChips:
    def __init__(self, shards):
        n = len(shards)
        self.arrays, self.n_in, self.n_out = list(shards), n, n
        self.out_shape = [jax.ShapeDtypeStruct((4,) + s.shape, s.dtype) for s in shards]
        self.scratch = [pltpu.SemaphoreType.DMA((3 * n,)), pltpu.SemaphoreType.DMA((3 * n,)),
                        pltpu.SemaphoreType.DMA((n,))]

    def _copies(self, x_refs, out_refs, sems, receiving):
        send_sems, recv_sems, local_sems = sems
        x, y, c, chips = _place()
        s_me = 2 * x + y
        n = self.n_in

        def copy(a, k, s):
            return pltpu.make_async_remote_copy(
                src_ref=x_refs[a], dst_ref=out_refs[a].at[s], send_sem=send_sems.at[3 * a + k],
                recv_sem=recv_sems.at[3 * a + k], device_id=(*chips[k], c), device_id_type=MESH)

        mine = [pltpu.make_async_copy(x_refs[a], out_refs[a].at[s_me], local_sems.at[a]) for a in range(n)]
        sends = [copy(a, k, s_me) for a in range(n) for k in range(3)]
        if not receiving:
            return mine, sends
        return mine, sends, [copy(a, k, 2 * px + py) for a in range(n) for k, (px, py) in enumerate(chips)]

    def start(self, x_refs, out_refs, sems):
        mine, sends = self._copies(x_refs, out_refs, sems, False)
        for cp in mine + sends:
            cp.start()

    def wait(self, x_refs, out_refs, sems):
        mine, sends, recvs = self._copies(x_refs, out_refs, sems, True)
        for cp in recvs:
            cp.wait_recv()
        for cp in sends:
            cp.wait_send()
        for cp in mine:
            cp.wait()


class _ScatterChips:
    def __init__(self, parts):
        n = len(parts)
        self.arrays, self.n_in, self.n_out = list(parts), n, n
        self.out_shape = [jax.ShapeDtypeStruct((3,) + p.shape[1:], p.dtype) for p in parts]
        self.scratch = [pltpu.SemaphoreType.DMA((3 * n,)), pltpu.SemaphoreType.DMA((3 * n,))]

    def _copies(self, g_refs, out_refs, sems):
        send_sems, recv_sems = sems
        x, y, c, chips = _place()
        return [pltpu.make_async_remote_copy(
            src_ref=g_refs[a].at[2 * px + py], dst_ref=out_refs[a].at[k], send_sem=send_sems.at[3 * a + k],
            recv_sem=recv_sems.at[3 * a + k], device_id=(px, py, c), device_id_type=MESH)
            for a in range(self.n_in) for k, (px, py) in enumerate(chips)]

    def start(self, g_refs, out_refs, sems):
        for cp in self._copies(g_refs, out_refs, sems):
            cp.start()

    def wait(self, g_refs, out_refs, sems):
        sends = self._copies(g_refs, out_refs, sems)
        for cp in sends:
            cp.wait_recv()
        for cp in sends:
            cp.wait_send()


def _exchange_call(ex, *, name):
    def body(*refs):
        parts = (refs[:ex.n_in], refs[ex.n_in:ex.n_in + ex.n_out], refs[ex.n_in + ex.n_out:])
        ex.start(*parts)
        ex.wait(*parts)

    return pl.pallas_call(body, name=name, out_shape=ex.out_shape, in_specs=[ANY] * ex.n_in,
                          out_specs=[ANY] * ex.n_out, scratch_shapes=ex.scratch)(*ex.arrays)


def _swap_sibling(vs, *, name):
    n = len(vs)

    def body(*refs):
        v_refs, out_refs = refs[:n], refs[n:2 * n]
        send_sems, recv_sems = refs[2 * n:]
        x, y, c, _ = _place()
        cps = [pltpu.make_async_remote_copy(src_ref=v_refs[a], dst_ref=out_refs[a], send_sem=send_sems.at[a],
                                            recv_sem=recv_sems.at[a], device_id=(x, y, 1 - c), device_id_type=MESH)
               for a in range(n)]
        for cp in cps:
            cp.start()
        for cp in cps:
            cp.wait()

    return pl.pallas_call(
        body, name=name, out_shape=[jax.ShapeDtypeStruct(v.shape, v.dtype) for v in vs],
        in_specs=[ANY] * n, out_specs=[ANY] * n,
        scratch_shapes=[pltpu.SemaphoreType.DMA((n,)), pltpu.SemaphoreType.DMA((n,))],
    )(*vs)


def _sum_parts(own, others, *, name):
    R, C = own.shape
    tm = _pick(R, (256, 128, 64))

    def body(own_ref, o0_ref, o1_ref, o2_ref, out_ref):
        tot = own_ref[...].astype(F32)
        for ref in (o0_ref, o1_ref, o2_ref):
            tot = tot + ref[...].astype(F32)
        out_ref[...] = tot

    part = lambda k: pl.BlockSpec((None, tm, C), lambda i: (k, i, 0))
    return pl.pallas_call(
        body, name=name, grid=(R // tm,),
        in_specs=[pl.BlockSpec((tm, C), lambda i: (i, 0)), part(0), part(1), part(2)],
        out_specs=pl.BlockSpec((tm, C), lambda i: (i, 0)), out_shape=jax.ShapeDtypeStruct((R, C), F32),
        compiler_params=_cparams(("arbitrary",)),
    )(own, others, others, others)


def _adam_math(w_, m_, v_, g):
    m2 = ADAM_B1 * m_ + (1.0 - ADAM_B1) * g
    v2 = ADAM_B2 * v_ + (1.0 - ADAM_B2) * jnp.square(g)
    m_hat = m2 / (1.0 - ADAM_B1 ** ADAM_STEP)
    v_hat = v2 / (1.0 - ADAM_B2 ** ADAM_STEP)
    delta = -ADAM_LR * (m_hat / (jnp.sqrt(v_hat) + ADAM_EPS) + ADAM_WD * w_)
    return delta, m2, v2


SMALL_SLOTS = 16
SMALL_COLS = 6 * D


def _pack_small(grads, *, name):
    n = len(grads)

    def body(*refs):
        out_ref = refs[n]
        out_ref[...] = jnp.zeros_like(out_ref)
        for i, ref in enumerate(refs[:n]):
            out_ref[i:i + 1, 0:ref.shape[1]] = ref[...]

    return pl.pallas_call(body, name=name, out_shape=jax.ShapeDtypeStruct((SMALL_SLOTS, SMALL_COLS), F32))(*grads)


def _adamw_small(ws, ms, vs, gathered, *, name):
    n = len(ws)

    def body(*refs):
        w_refs, m_refs, v_refs, g_ref = refs[:n], refs[n:2 * n], refs[2 * n:3 * n], refs[3 * n]
        outs = refs[3 * n + 1:]
        for i in range(n):
            nc = w_refs[i].shape[1]
            g = g_ref[i:i + 1, 0:nc]
            for d in range(1, 8):
                g = g + g_ref[d * SMALL_SLOTS + i:d * SMALL_SLOTS + i + 1, 0:nc]
            delta, m2, v2 = _adam_math(w_refs[i][...], m_refs[i][...], v_refs[i][...], g)
            for k, val in enumerate((g, delta, m2, v2)):
                outs[k * n + i][...] = val

    shapes = [jax.ShapeDtypeStruct(w.shape, F32) for w in ws]
    res = pl.pallas_call(body, name=name, out_shape=shapes * 4,
                         compiler_params=pltpu.CompilerParams(vmem_limit_bytes=VMEM_LIMIT))(*ws, *ms, *vs, gathered)
    return [res[k * n:(k + 1) * n] for k in range(4)]


def _adamw(w, m, v, gparts, *, tm, name):
    def fn(i, n, w_, m_, v_, *gs):
        g = gs[0]
        for p in gs[1:]:
            g = g + p
        return (g,) + _adam_math(w_, m_, v_, g)
    nc = w.shape[1]
    return _rowwise(fn, [w, m, v] + list(gparts), [], [(nc, F32, 'row')] * 4, tm=tm, name=name)


WEIGHTS = ['ada_w', 'ada_b', 'norm1_gain', 'norm2_gain', 'w_in', 'tshift_mu', 'decay_w0', 'decay_up', 'iclr_a0',
           'iclr_up', 'gate_up', 'k_k', 'k_a', 'r_k', 'lnx_gain', 'lnx_bias', 'q_norm_gain', 'k_norm_gain', 'attn_sinks',
           'branch_gate_b', 'w_branch_a', 'w_branch_b', 'w_out', 'ffn_w1', 'ffn_w3', 'ffn_w2']
SHARDED = [('w_in', 1), ('decay_up', 1), ('iclr_up', 1), ('gate_up', 1), ('w_branch_a', 1), ('w_branch_b', 1),
           ('w_out', 0), ('ffn_w1', 1), ('ffn_w3', 1), ('ffn_w2', 0)]
SMALL = ['ada_b', 'norm1_gain', 'norm2_gain', 'tshift_mu', 'decay_w0', 'iclr_a0', 'k_k', 'k_a', 'r_k', 'lnx_gain',
         'lnx_bias', 'q_norm_gain', 'k_norm_gain', 'attn_sinks', 'branch_gate_b']


def kernel(x, c, positions, ada_w, ada_b, norm1_gain, norm2_gain, w_in, tshift_mu, decay_w0, decay_up, iclr_a0, iclr_up, gate_up, k_k, k_a, r_k, lnx_gain, lnx_bias, q_norm_gain, k_norm_gain, attn_sinks, branch_gate_b, w_branch_a, w_branch_b, w_out, ffn_w1, ffn_w3, ffn_w2, loss_target, m_ada_w, m_ada_b, m_norm1_gain, m_norm2_gain, m_w_in, m_tshift_mu, m_decay_w0, m_decay_up, m_iclr_a0, m_iclr_up, m_gate_up, m_k_k, m_k_a, m_r_k, m_lnx_gain, m_lnx_bias, m_q_norm_gain, m_k_norm_gain, m_attn_sinks, m_branch_gate_b, m_w_branch_a, m_w_branch_b, m_w_out, m_ffn_w1, m_ffn_w3, m_ffn_w2, v_ada_w, v_ada_b, v_norm1_gain, v_norm2_gain, v_w_in, v_tshift_mu, v_decay_w0, v_decay_up, v_iclr_a0, v_iclr_up, v_gate_up, v_k_k, v_k_a, v_r_k, v_lnx_gain, v_lnx_bias, v_q_norm_gain, v_k_norm_gain, v_attn_sinks, v_branch_gate_b, v_w_branch_a, v_w_branch_b, v_w_out, v_ffn_w1, v_ffn_w3, v_ffn_w2):
    a = dict(locals())
    W = {n: a[n] for n in WEIGHTS}
    M = {n: a['m_' + n] for n in WEIGHTS}
    V = {n: a['v_' + n] for n in WEIGHTS}
    xi, yi, ci = lax.axis_index("x"), lax.axis_index("y"), lax.axis_index("c")
    me = 4 * xi + 2 * yi + ci
    shard = 2 * xi + yi
    mat = lambda t: t.reshape(t.shape[-2], t.shape[-1])
    sharded = [n for n, _ in SHARDED]

    ax = dict(SHARDED)
    late = LATE_SCAN + LATE_ATTN
    early = [n for n in sharded if n not in late]
    shards = {n: mat(W[n]).astype(MXU) for n in sharded}
    gathered = _exchange_call(_GatherChips([shards[n] for n in early]), name="gather_weights")
    full = {n: _full_weight(g, ax[n]) for n, g in zip(early, gathered, strict=True)}

    c_all = _all_gather8(jnp.broadcast_to(c, (8, D)), name="gather_c")[0::8]
    pad_rows = lambda t: jnp.concatenate([t, jnp.zeros((BLK - 8, t.shape[1]), t.dtype)])
    c_all = pad_rows(c_all.astype(MXU))
    ada_cols = _mm_nn(c_all, mat(ada_w).astype(MXU), name="f_ada")[:8]
    ada_all = _all_gather8(ada_cols, name="gather_ada").reshape(2, 2, 2, 8, 6 * D // 4)
    ada_mine = lax.dynamic_index_in_dim(ada_all[:, :, 0], me, axis=2, keepdims=False)
    ada = ada_mine.reshape(1, 6 * D) + mat(ada_b)

    zero = jnp.zeros((64, RW), MXU)
    lora = jnp.concatenate([jnp.concatenate([full['decay_up'], zero], axis=1),
                            jnp.concatenate([zero, full['iclr_up']], axis=1)], axis=0)
    s = {n: W[n].reshape(1, -1) for n in SMALL if n != 'ada_b'}
    s['lora_up'] = lora.astype(F32)
    s['gate_up'] = full['gate_up'].astype(F32)
    tab = _rope_table(positions.reshape(-1))
    loss, dx, d_ada, gw, gs, from_chips = _local_step(x[0], loss_target[0], ada, tab, dict(w_in=full['w_in']), s,
                                                      shards={n: shards[n] for n in late})
    loss = lax.psum(loss[0, 0], ("x", "y", "c"))

    gs['ada_b'] = d_ada
    gsmall = _pack_small([gs[n] for n in SMALL], name="pack_small_grads")
    gsmall_all = _all_gather8(gsmall, name="gather_small_grads")
    row = lambda src: [src[n].reshape(1, -1) for n in SMALL]
    sm_out = _adamw_small(row(W), row(M), row(V), gsmall_all, name="adamw_small")
    sm_out = [{n: o.reshape(W[n].shape) for n, o in zip(SMALL, outs_k, strict=True)} for outs_k in sm_out]

    d_ada_all = gsmall_all[0::SMALL_SLOTS]
    d_ada_cols = lax.dynamic_slice_in_dim(d_ada_all, shard * (6 * D // 4), 6 * D // 4, axis=1)
    g_ada_w = _mm_tn(c_all, pad_rows(d_ada_cols.astype(MXU)), name="b_ada")
    ada_out = _adamw(mat(ada_w), mat(m_ada_w), mat(v_ada_w), [g_ada_w], tm=256, name="adamw_ada")

    parts =[_sum_parts(lax.dynamic_index_in_dim(gw[n], shard, axis=0, keepdims=False), from_chips[n], name="sum_" + n)
             for n in sharded]
    others = _swap_sibling(parts, name="swap_grads")
    sh_out = {}
    for n, part, other in zip(sharded, parts, others, strict=True):
        sh_out[n] = _adamw(mat(W[n]), mat(M[n]), mat(V[n]), [part, other], tm=_pick(part.shape[0], (256, 128, 64)),
                           name="adamw_" + n)

    def leaf(k, n):
        if n == 'ada_w':
            return ada_out[k].reshape(W[n].shape)
        if n in sharded:
            return sh_out[n][k].reshape(W[n].shape)
        return sm_out[k][n]
    outs = [leaf(k, n) for k in range(4) for n in WEIGHTS]
    return (loss, dx[None], *outs)
```

```python
import functools
import math

import jax
import jax.numpy as jnp
from jax import lax
from jax.experimental import pallas as pl
from jax.experimental.pallas import tpu as pltpu

F32 = jnp.float32
BF16 = jnp.bfloat16
MXU = BF16
HI = lax.Precision.HIGHEST

D = 1024
HD = 64
NH = 8
RW = NH * HD
SHIFT_W = 3 * RW + 64 + 64 + 128
QKV_W = RW + 2 * 128
GATE_W = 2 * D
IN_W = SHIFT_W + QKV_W + GATE_W
DFF = 2816
BLK = 128
CHUNK = 64
RMS_EPS = 1e-6
GN_EPS = 64e-5
NEG_INF = -1e30
ADAM_LR, ADAM_B1, ADAM_B2, ADAM_EPS, ADAM_WD, ADAM_STEP = 0.001, 0.9, 0.999, 1e-08, 0.01, 10
VMEM_LIMIT = 56 * 1024 * 1024
MESH = pl.DeviceIdType.MESH


def _cparams(sem=None):
    return pltpu.CompilerParams(dimension_semantics=sem, vmem_limit_bytes=VMEM_LIMIT)


def _full_spec(a):
    nd = a.ndim
    return pl.BlockSpec(a.shape, lambda *_: (0,) * nd)


def _rowwise(fn, rows, consts, outs, *, tm, name, halo=()):
    rows = [a if isinstance(a, tuple) else (a, a.shape[1]) for a in rows]
    T = rows[0][0].shape[0]
    assert T % tm == 0 and tm % 8 == 0
    n_tiles = T // tm
    n_in = len(rows) + len(halo) + len(consts)
    in_specs = [pl.BlockSpec((tm, nc), lambda i: (i, 0)) for _, nc in rows]
    args = [a for a, _ in rows]
    for a, nc, kind in halo:
        if kind == 'prev':
            in_specs.append(pl.BlockSpec((8, nc), lambda i: (jnp.maximum(i * (tm // 8) - 1, 0), 0)))
        else:
            in_specs.append(pl.BlockSpec((8, nc), lambda i: (jnp.minimum((i + 1) * (tm // 8), T // 8 - 1), 0)))
        args.append(a)
    in_specs += [_full_spec(a) for a in consts]
    args += list(consts)
    out_shape, out_specs = [], []
    for ncols, dtype, kind in outs:
        if kind == 'row':
            out_shape.append(jax.ShapeDtypeStruct((T, ncols), dtype))
            out_specs.append(pl.BlockSpec((tm, ncols), lambda i: (i, 0)))
        else:
            out_shape.append(jax.ShapeDtypeStruct((kind, ncols), dtype))
            out_specs.append(pl.BlockSpec((kind, ncols), lambda i: (0, 0)))

    def body(*refs):
        i = pl.program_id(0)
        vals = [r[...] for r in refs[:n_in]]
        res = fn(i, n_tiles, *vals)
        for (ncols, dtype, kind), o_ref, val in zip(outs, refs[n_in:], res, strict=True):
            if kind == 'row':
                o_ref[...] = val.astype(dtype)
            else:
                @pl.when(i == 0)
                def _():
                    o_ref[...] = jnp.zeros_like(o_ref)
                o_ref[...] += val.astype(dtype)

    res = pl.pallas_call(
        body, name=name, grid=(n_tiles,), in_specs=in_specs, out_specs=out_specs, out_shape=out_shape,
        compiler_params=_cparams(("arbitrary",)),
    )(*args)
    return res


def _pick(n, cands):
    for c in cands:
        if n % c == 0:
            return c
    return n


MM_ROWS = (1024, 512, 256, 128)
MM_COLS = (1536, 1408, 1024, 896, 768, 512, 256, 128)
MM_WIDE = 3000


def _mm_nn(a, w, *, name, out_dtype=F32):
    T, K = a.shape
    N = w.shape[1]
    tm = _pick(T, MM_ROWS)
    tn = _pick(N, MM_COLS)

    def body(a_ref, w_ref, o_ref):
        o_ref[...] = jnp.dot(a_ref[...], w_ref[...], preferred_element_type=F32).astype(out_dtype)

    return pl.pallas_call(
        body, name=name, grid=(N // tn, T // tm),
        in_specs=[pl.BlockSpec((tm, K), lambda j, i: (i, 0)), pl.BlockSpec((K, tn), lambda j, i: (0, j))],
        out_specs=pl.BlockSpec((tm, tn), lambda j, i: (i, j)),
        out_shape=jax.ShapeDtypeStruct((T, N), out_dtype),
        compiler_params=_cparams(("arbitrary", "arbitrary")),
    )(a, w)


def _mm_nt(dy, w, *, name, out_dtype=F32, hosted=None):
    T, N = dy.shape
    K = w.shape[0]
    tm = _pick(T, MM_ROWS if N <= MM_WIDE else MM_ROWS[1:])
    tk = _pick(K, MM_COLS[1:])
    grid = (K // tk, T // tm)
    h_in, h_in_specs, h_out_specs, h_out_shape, h_scratch = _hosted_args(hosted)

    def body(dy_ref, w_ref, o_ref):
        o_ref[...] = lax.dot_general(dy_ref[...], w_ref[...], (((1,), (1,)), ((), ())),
                                     preferred_element_type=F32).astype(out_dtype)

    res = pl.pallas_call(
        _hosting(body, hosted, 2, 1, 0, grid), name=name, grid=grid,
        in_specs=[pl.BlockSpec((tm, N), lambda j, i: (i, 0)), pl.BlockSpec((tk, N), lambda j, i: (j, 0))] + h_in_specs,
        out_specs=[pl.BlockSpec((tm, tk), lambda j, i: (i, j))] + h_out_specs,
        out_shape=[jax.ShapeDtypeStruct((T, K), out_dtype)] + h_out_shape, scratch_shapes=h_scratch,
        compiler_params=_cparams(("arbitrary", "arbitrary")),
    )(dy, w, *h_in)
    return res if hosted else res[0]


def _mm_tn(a, dy, *, name, out_dtype=F32, col_shards=None):
    T, K = a.shape
    N = dy.shape[1]
    tm = _pick(T, MM_ROWS)
    tn = N // col_shards if col_shards else _pick(N, MM_COLS[1:])
    n_t = T // tm

    def body(a_ref, dy_ref, o_ref, acc_ref):
        i = pl.program_id(1)

        @pl.when(i == 0)
        def _():
            acc_ref[...] = jnp.zeros_like(acc_ref)

        acc_ref[...] += lax.dot_general(a_ref[...], dy_ref[...], (((0,), (0,)), ((), ())), preferred_element_type=F32)

        @pl.when(i == n_t - 1)
        def _():
            o_ref[...] = acc_ref[...].astype(out_dtype)

    if col_shards:
        out_specs = pl.BlockSpec((None, K, tn), lambda j, i: (j, 0, 0))
        out_shape = jax.ShapeDtypeStruct((col_shards, K, tn), out_dtype)
    else:
        out_specs = pl.BlockSpec((K, tn), lambda j, i: (0, j))
        out_shape = jax.ShapeDtypeStruct((K, N), out_dtype)
    return pl.pallas_call(
        body, name=name, grid=(N // tn, n_t),
        in_specs=[pl.BlockSpec((tm, K), lambda j, i: (i, 0)), pl.BlockSpec((tm, tn), lambda j, i: (i, j))],
        out_specs=out_specs, out_shape=out_shape, scratch_shapes=[pltpu.VMEM((K, tn), F32)],
        compiler_params=_cparams(("arbitrary", "arbitrary")),
    )(a, dy)


def _seg_ones(n):
    r = lax.broadcasted_iota(jnp.int32, (n, n), 0) // HD
    c = lax.broadcasted_iota(jnp.int32, (n, n), 1) // HD
    return (r == c).astype(F32)


def _segsum_raw(x):
    ones = _seg_ones(x.shape[1])
    if MXU == F32:
        return jnp.dot(x, ones, precision=HI, preferred_element_type=F32)
    hi = x.astype(MXU)
    lo = (x - hi.astype(F32)).astype(MXU)
    ones = ones.astype(MXU)
    return jnp.dot(hi, ones, preferred_element_type=F32) + jnp.dot(lo, ones, preferred_element_type=F32)


@jax.custom_vjp
def _segsum(x):
    return _segsum_raw(x)


def _segsum_fwd(x):
    return _segsum_raw(x), None


def _segsum_bwd(_, g):
    return (_segsum_raw(g),)


_segsum.defvjp(_segsum_fwd, _segsum_bwd)


def _mxu(x):
    return x.astype(MXU)


@jax.custom_vjp
def _bdot(a, b):
    return jnp.dot(_mxu(a), _mxu(b), preferred_element_type=F32)


def _bdot_fwd(a, b):
    return _bdot(a, b), (a, b)


def _bdot_bwd(res, g):
    a, b = res
    da = lax.dot_general(_mxu(g), _mxu(b), (((1,), (1,)), ((), ())), preferred_element_type=F32)
    db = lax.dot_general(_mxu(a), _mxu(g), (((0,), (0,)), ((), ())), preferred_element_type=F32)
    return da.astype(a.dtype), db.astype(b.dtype)


_bdot.defvjp(_bdot_fwd, _bdot_bwd)


@jax.custom_vjp
def _bdot_nt(a, b):
    return lax.dot_general(_mxu(a), _mxu(b), (((1,), (1,)), ((), ())), preferred_element_type=F32)


def _bdot_nt_fwd(a, b):
    return _bdot_nt(a, b), (a, b)


def _bdot_nt_bwd(res, g):
    a, b = res
    da = jnp.dot(_mxu(g), _mxu(b), preferred_element_type=F32)
    db = lax.dot_general(_mxu(g), _mxu(a), (((0,), (0,)), ((), ())), preferred_element_type=F32)
    return da.astype(a.dtype), db.astype(b.dtype)


_bdot_nt.defvjp(_bdot_nt_fwd, _bdot_nt_bwd)


def _sigmoid(x):
    return 1.0 / (1.0 + jnp.exp(-x))


def _softplus(x):
    return jnp.maximum(x, 0.0) + jnp.log(1.0 + jnp.exp(jnp.minimum(x, -x)))


def _norm_mod(x, gain, scale, shift):
    inv = lax.rsqrt(jnp.mean(x * x, axis=-1, keepdims=True) + RMS_EPS)
    return (x * inv) * gain * (1.0 + scale) + shift


def _prep(mixed, decay_w0, lora_up, iclr_a0, gate_up, k_k, k_a):
    r = mixed[:, 0:RW]
    k = mixed[:, RW:2 * RW]
    v = mixed[:, 2 * RW:3 * RW]
    z = mixed[:, 3 * RW:3 * RW + 128]
    xg = mixed[:, 3 * RW + 128:]
    lane = lax.broadcasted_iota(jnp.int32, z.shape, 1)
    tz = jnp.where(lane < 64, jnp.tanh(z), z)
    lo = _bdot(tz, lora_up)
    w_log = -_softplus(-(decay_w0 + lo[:, :RW])) - 0.5
    lw = -jnp.exp(w_log)
    a_ic = _sigmoid(iclr_a0 + lo[:, RW:])
    g = _bdot(_sigmoid(xg), gate_up)
    kk = k * k_k
    kk = kk / jnp.maximum(jnp.sqrt(_segsum(kk * kk)), 1e-12)
    k_mod = k * (1.0 + (a_ic - 1.0) * k_a)
    return jnp.concatenate([r, lw, k_mod, v, -kk, kk * a_ic, g], axis=1)


def _post(y, r, k, v, g, lnx_gain, lnx_bias, r_k):
    mu = _segsum(y) * (1.0 / HD)
    yc = y - mu
    var = _segsum(yc * yc) * (1.0 / HD)
    yn = yc * lax.rsqrt(var + GN_EPS) * lnx_gain + lnx_bias
    bonus = _segsum(r * k * r_k) * v
    return (yn + bonus) * g


def _merge(pg, ma, mb, bias):
    gates = _sigmoid(pg + bias)
    return gates[:, :D] * ma + gates[:, D:] * mb


def _swiglu(u, v):
    return u * _sigmoid(u) * v


@functools.partial(jax.custom_vjp, nondiff_argnums=(1,))
def _lane_roll(x, s):
    return pltpu.roll(x, s, 1)


def _lane_roll_fwd(x, s):
    return pltpu.roll(x, s, 1), None


def _lane_roll_bwd(s, _, g):
    n = g.shape[1]
    return (pltpu.roll(g, (n - s) % n, 1),)


_lane_roll.defvjp(_lane_roll_fwd, _lane_roll_bwd)


def _rope(x, cos, sin_lo, sin_hi):
    n = x.shape[1]
    return x * cos + _lane_roll(x, n - 8) * sin_lo + _lane_roll(x, 8) * sin_hi


def _head_rms(x, gain):
    return x * lax.rsqrt(_segsum(x * x) * (1.0 / HD) + RMS_EPS) * gain


def _attn_block(qkv_c, qkv_p, tab_c, tab_p, qg, kg, sinks, first):
    def tabs(tab, n):
        return tab[:, 0:n], tab[:, RW:RW + n], tab[:, 2 * RW:2 * RW + n]

    qg = jnp.concatenate([qg] * NH, axis=1)
    kg = jnp.concatenate([kg] * 2, axis=1)
    q = _rope(_head_rms(qkv_c[:, :RW], qg), *tabs(tab_c, RW))
    k_c = _rope(_head_rms(qkv_c[:, RW:RW + 128], kg), *tabs(tab_c, 128))
    k_p = _rope(_head_rms(qkv_p[:, RW:RW + 128], kg), *tabs(tab_p, 128))
    kband = jnp.concatenate([k_p, k_c], axis=0)
    vband = jnp.concatenate([qkv_p[:, RW + 128:], qkv_c[:, RW + 128:]], axis=0)
    G = 4
    qi = lax.broadcasted_iota(jnp.int32, (G * BLK, 2 * BLK), 0) % BLK
    kj = lax.broadcasted_iota(jnp.int32, (G * BLK, 2 * BLK), 1)
    dist = qi + BLK - kj
    valid = (dist >= 0) & (dist < BLK) & (jnp.logical_not(first) | (kj >= BLK))
    row_g = lax.broadcasted_iota(jnp.int32, (G * BLK, 1), 0) // BLK
    outs = []
    for kvh in range(2):
        kb = kband[:, kvh * HD:(kvh + 1) * HD]
        vb = vband[:, kvh * HD:(kvh + 1) * HD]
        qs = jnp.concatenate([q[:, (G * kvh + g) * HD:(G * kvh + g + 1) * HD] for g in range(G)], axis=0)
        s = _bdot_nt(qs, kb) * (HD ** -0.5)
        s = jnp.where(valid, s, NEG_INF)
        sink = jnp.zeros((G * BLK, 1), F32)
        for g in range(G):
            sink = jnp.where(row_g == g, sinks[:, G * kvh + g:G * kvh + g + 1], sink)
        m = lax.stop_gradient(jnp.maximum(jnp.max(s, axis=-1, keepdims=True), sink))
        e = jnp.exp(s - m)
        p = e / (jnp.sum(e, axis=-1, keepdims=True) + jnp.exp(sink - m))
        o = _bdot(p, vb)
        outs += [o[g * BLK:(g + 1) * BLK] for g in range(G)]
    return jnp.concatenate(outs, axis=1)


def _heads(x):
    return jnp.stack([x[:, h * HD:(h + 1) * HD] for h in range(NH)], axis=0)


def _unheads(x):
    return jnp.concatenate([x[h] for h in range(NH)], axis=1)


def _split(x, n):
    parts, rest = [], x
    for _ in range(n):
        p = rest.astype(MXU)
        parts.append(p)
        rest = rest - p.astype(F32)
    return parts


def _bdot_batched(a, b, ca, cb):
    return lax.dot_general(a, b, (((ca,), (cb,)), ((0,), (0,))), preferred_element_type=F32)


def _bmm_passes(a, b, ca, cb, passes):
    if MXU == F32:
        return lax.dot_general(a, b, (((ca,), (cb,)), ((0,), (0,))), precision=HI, preferred_element_type=F32)
    if passes == 1:
        return _bdot_batched(a.astype(MXU), b.astype(MXU), ca, cb)
    (a0, a1), (b0, b1) = _split(a, 2), _split(b, 2)
    return _bdot_batched(a0, b0, ca, cb) + (_bdot_batched(a0, b1, ca, cb) + _bdot_batched(a1, b0, ca, cb))


@functools.partial(jax.custom_vjp, nondiff_argnums=(2, 3, 4))
def _bmm(a, b, ca, cb, passes=1):
    return _bmm_passes(a, b, ca, cb, passes)


def _bmm_fwd(a, b, ca, cb, passes):
    return _bmm_passes(a, b, ca, cb, passes), (a, b)


def _bmm_bwd(ca, cb, passes, res, g):
    a, b = res
    if (ca, cb) == (2, 1):
        return _bmm_passes(g, b, 2, 2, passes), _bmm_passes(a, g, 1, 1, passes)
    if (ca, cb) == (2, 2):
        return _bmm_passes(g, b, 2, 1, passes), _bmm_passes(g, a, 1, 1, passes)
    return _bmm_passes(b, g, 2, 2, passes), _bmm_passes(a, g, 2, 1, passes)


_bmm.defvjp(_bmm_fwd, _bmm_bwd)


def _tri_dot(x, transpose):
    C = x.shape[1]
    ri = lax.broadcasted_iota(jnp.int32, (C, C), 0)
    ci = lax.broadcasted_iota(jnp.int32, (C, C), 1)
    tri = jnp.broadcast_to(((ri <= ci) if transpose else (ri >= ci)).astype(MXU), (x.shape[0], C, C))
    if MXU == F32:
        return lax.dot_general(tri, x, (((2,), (1,)), ((0,), (0,))), precision=HI, preferred_element_type=F32)
    p0, p1, p2 = _split(x, 3)
    return _bdot_batched(tri, p0, 2, 1) + (_bdot_batched(tri, p1, 2, 1) + _bdot_batched(tri, p2, 2, 1))


@jax.custom_vjp
def _cumsum_rows(x):
    return _tri_dot(x, False)


def _cumsum_rows_fwd(x):
    return _tri_dot(x, False), None


def _cumsum_rows_bwd(_, g):
    return (_tri_dot(g, True),)


_cumsum_rows.defvjp(_cumsum_rows_fwd, _cumsum_rows_bwd)

P_SCORE = 1
P_SOLVE = 1
P_STATE = 1
SCAN_CHUNKS = (4, 2, 1)


def _chunk(S0, r, lw, k, v, a, b):
    C = CHUNK
    n = r.shape[1] // C
    fold = lambda t: t.reshape(NH * n, C, HD)
    r, lw, k, v, a, b = (fold(t) for t in (r, lw, k, v, a, b))
    ri = lax.broadcasted_iota(jnp.int32, (C, C), 0)
    ci = lax.broadcasted_iota(jnp.int32, (C, C), 1)
    incl = (ri >= ci)
    strict = (ri > ci)
    eye = (ri == ci).astype(F32)
    cum = _cumsum_rows(lw)
    p_in = jnp.exp(cum)
    p_ex = jnp.exp(cum - lw)
    p_inv = jnp.exp(-cum)
    at, rt, bt, kt = a * p_ex, r * p_in, b * p_inv, k * p_inv
    lhs = jnp.concatenate([at, rt], axis=1)
    rhs_ = jnp.concatenate([bt, kt], axis=1)
    sc = _bmm(lhs, rhs_, 2, 2, P_SCORE)
    a_ab = jnp.where(strict, sc[:, :C, :C], 0.0)
    a_ak = jnp.where(strict, sc[:, :C, C:], 0.0)
    incl2 = (lax.broadcasted_iota(jnp.int32, (C, 2 * C), 0) >= lax.broadcasted_iota(jnp.int32, (C, 2 * C), 1) % C)
    a_r = jnp.where(incl2, sc[:, C:, :], 0.0)
    av = _bmm(a_ak, v, 2, 1, P_SCORE)
    x = eye + a_ab
    lp = a_ab
    for _ in range(int(math.log2(C)) - 1):
        lp = _bmm(lp, lp, 2, 1, P_SOLVE)
        x = x + _bmm(x, lp, 2, 1, P_SOLVE)
    p_last = jnp.exp(cum[:, C - 1:C, :])
    per_chunk = lambda t: t.reshape((NH, n) + t.shape[1:])
    lhs, rhs_, a_r, av, x, v, p_last = (per_chunk(t) for t in (lhs, rhs_, a_r, av, x, v, p_last))
    S, ys = S0, []
    for c in range(n):
        s0 = _bmm(lhs[:, c], S, 2, 2, P_STATE)
        u = _bmm(x[:, c], s0[:, :C] + av[:, c], 2, 1, P_SOLVE)
        uv = jnp.concatenate([u, v[:, c]], axis=1)
        ys.append(s0[:, C:] + _bmm(a_r[:, c], uv, 2, 1, P_SCORE))
        S = (S + _bmm(uv, rhs_[:, c], 1, 1, P_STATE)) * p_last[:, c]
    return jnp.concatenate(ys, axis=1), S


def _hosting(body, ex, n_in, n_out, n_scratch, n_steps):
    if ex is None:
        return body

    def wrapped(*refs):
        a = n_in
        b = a + ex.n_in
        c = b + n_out
        d = c + ex.n_out
        e = d + n_scratch
        ex_refs = (refs[a:b], refs[c:d], refs[e:])
        grid = n_steps if isinstance(n_steps, tuple) else (n_steps,)
        first = last = True
        for ax_, size in enumerate(grid):
            first = first & (pl.program_id(ax_) == 0)
            last = last & (pl.program_id(ax_) == size - 1)

        @pl.when(first)
        def _():
            ex.start(*ex_refs)

        body(*refs[:a], *refs[b:c], *refs[d:e])

        @pl.when(last)
        def _():
            ex.wait(*ex_refs)

    return wrapped


def _hosted_args(ex):
    if ex is None:
        return [], [], [], [], []
    any_spec = pl.BlockSpec(memory_space=pl.ANY)
    return list(ex.arrays), [any_spec] * ex.n_in, [any_spec] * ex.n_out, list(ex.out_shape), list(ex.scratch)


def _scan_fwd(rw, *, name, hosted=None):
    T = rw.shape[0]
    rows = CHUNK * _pick(T // CHUNK, SCAN_CHUNKS)
    n = T // rows
    h_in, h_in_specs, h_out_specs, h_out_shape, h_scratch = _hosted_args(hosted)

    def body(r_ref, lw_ref, k_ref, v_ref, a_ref, b_ref, y_ref, ck_ref, s_ref):
        @pl.when(pl.program_id(0) == 0)
        def _():
            s_ref[...] = jnp.zeros_like(s_ref)

        S0 = s_ref[...]
        ck_ref[0] = S0
        y, S1 = _chunk(S0, *[_heads(ref[...]) for ref in (r_ref, lw_ref, k_ref, v_ref, a_ref, b_ref)])
        y_ref[...] = _unheads(y)
        s_ref[...] = S1

    col = lambda j: pl.BlockSpec((rows, RW), lambda i: (i, j))
    return pl.pallas_call(
        _hosting(body, hosted, 6, 2, 1, n), name=name, grid=(n,),
        in_specs=[col(j) for j in range(6)] + h_in_specs,
        out_specs=[pl.BlockSpec((rows, RW), lambda i: (i, 0)),
                   pl.BlockSpec((1, NH, HD, HD), lambda i: (i, 0, 0, 0))] + h_out_specs,
        out_shape=[jax.ShapeDtypeStruct((T, RW), F32), jax.ShapeDtypeStruct((n, NH, HD, HD), F32)] + h_out_shape,
        scratch_shapes=[pltpu.VMEM((NH, HD, HD), F32)] + h_scratch,
        compiler_params=_cparams(("arbitrary",)),
    )(rw, rw, rw, rw, rw, rw, *h_in)


def _scan_bwd(rw, ck, dy, *, name, hosted=None):
    T = rw.shape[0]
    rows = CHUNK * _pick(T // CHUNK, SCAN_CHUNKS)
    n = T // rows

    def body(r_ref, lw_ref, k_ref, v_ref, a_ref, b_ref, ck_ref, dy_ref, o_ref, ds_ref):
        @pl.when(pl.program_id(0) == 0)
        def _():
            ds_ref[...] = jnp.zeros_like(ds_ref)

        prim = [_heads(ref[...]) for ref in (r_ref, lw_ref, k_ref, v_ref, a_ref, b_ref)]
        _, vjp = jax.vjp(_chunk, ck_ref[0], *prim)
        grads = vjp((_heads(dy_ref[...]), ds_ref[...]))
        ds_ref[...] = grads[0]
        o_ref[...] = jnp.concatenate([_unheads(g) for g in grads[1:]], axis=1)

    h_in, h_in_specs, h_out_specs, h_out_shape, h_scratch = _hosted_args(hosted)
    col = lambda j: pl.BlockSpec((rows, RW), lambda i: (n - 1 - i, j))
    return pl.pallas_call(
        _hosting(body, hosted, 8, 1, 1, n), name=name, grid=(n,),
        in_specs=[col(j) for j in range(6)] + [pl.BlockSpec((1, NH, HD, HD), lambda i: (n - 1 - i, 0, 0, 0)),
                                               pl.BlockSpec((rows, RW), lambda i: (n - 1 - i, 0))] + h_in_specs,
        out_specs=[pl.BlockSpec((rows, 6 * RW), lambda i: (n - 1 - i, 0))] + h_out_specs,
        out_shape=[jax.ShapeDtypeStruct((T, 6 * RW), F32)] + h_out_shape,
        scratch_shapes=[pltpu.VMEM((NH, HD, HD), F32)] + h_scratch,
        compiler_params=_cparams(("arbitrary",)),
    )(rw, rw, rw, rw, rw, rw, ck, dy, *h_in)


ATTN_BLOCKS = (4, 2, 1)

def _attn_fwd(qkv, tab, qg, kg, sinks, *, name, hosted=None):
    T = qkv.shape[0]
    nb = _pick(T // BLK, ATTN_BLOCKS)
    n = T // (BLK * nb)
    h_in, h_in_specs, h_out_specs, h_out_shape, h_scratch = _hosted_args(hosted)

    def body(c_ref, p_ref, tc_ref, tp_ref, qg_ref, kg_ref, s_ref, o_ref):
        for b in range(nb):
            rows = slice(b * BLK, (b + 1) * BLK)
            before = slice((b - 1) * BLK, b * BLK)
            p, tp = (p_ref[...], tp_ref[...]) if b == 0 else (c_ref[before, :], tc_ref[before, :])
            first = (pl.program_id(0) == 0) if b == 0 else False
            o_ref[rows, :] = _attn_block(c_ref[rows, :], p, tc_ref[rows, :], tp, qg_ref[...], kg_ref[...],
                                         s_ref[...], first).astype(o_ref.dtype)

    cur = lambda w: pl.BlockSpec((nb * BLK, w), lambda i: (i, 0))
    prev = lambda w: pl.BlockSpec((BLK, w), lambda i: (jnp.maximum(i * nb - 1, 0), 0))
    return pl.pallas_call(
        _hosting(body, hosted, 7, 1, 0, n), name=name, grid=(n,),
        in_specs=[cur(QKV_W), prev(QKV_W), cur(3 * RW), prev(3 * RW), _full_spec(qg), _full_spec(kg),
                  _full_spec(sinks)] + h_in_specs,
        out_specs=[cur(RW)] + h_out_specs, out_shape=[jax.ShapeDtypeStruct((T, RW), MXU)] + h_out_shape,
        scratch_shapes=h_scratch,
        compiler_params=_cparams(("arbitrary",)),
    )(qkv, qkv, tab, tab, qg, kg, sinks, *h_in)


def _attn_bwd(qkv, tab, qg, kg, sinks, dy, *, name, hosted=None):
    T = qkv.shape[0]
    nb = _pick(T // BLK, ATTN_BLOCKS)
    n = T // (BLK * nb)
    h_in, h_in_specs, h_out_specs, h_out_shape, h_scratch = _hosted_args(hosted)

    def body(c_ref, p_ref, tc_ref, tp_ref, qg_ref, kg_ref, s_ref, dy_ref, dqkv_ref, dqg_ref, dkg_ref, ds_ref, carry_ref):
        i = pl.program_id(0)

        @pl.when(i == 0)
        def _():
            carry_ref[...] = jnp.zeros_like(carry_ref)
            dqg_ref[...] = jnp.zeros_like(dqg_ref)
            dkg_ref[...] = jnp.zeros_like(dkg_ref)
            ds_ref[...] = jnp.zeros_like(ds_ref)

        carry = carry_ref[...]
        dqg_t, dkg_t, ds_t = jnp.zeros_like(dqg_ref), jnp.zeros_like(dkg_ref), jnp.zeros_like(ds_ref)
        for b in reversed(range(nb)):
            rows = slice(b * BLK, (b + 1) * BLK)
            before = slice((b - 1) * BLK, b * BLK)
            tc = tc_ref[rows, :]
            p, tp = (p_ref[...], tp_ref[...]) if b == 0 else (c_ref[before, :], tc_ref[before, :])
            first = (i == n - 1) if b == 0 else False
            f = lambda c, p_, qg_, kg_, sk, tc=tc, tp=tp, first=first: _attn_block(c, p_, tc, tp, qg_, kg_, sk, first)
            _, vjp = jax.vjp(f, c_ref[rows, :], p, qg_ref[...], kg_ref[...], s_ref[...])
            dc, dp, dqg, dkg, dsk = vjp(dy_ref[rows, :].astype(F32))
            dqkv_ref[rows, :] = (dc + carry).astype(dqkv_ref.dtype)
            carry = dp
            dqg_t, dkg_t, ds_t = dqg_t + dqg, dkg_t + dkg, ds_t + dsk
        carry_ref[...] = carry
        dqg_ref[...] += dqg_t
        dkg_ref[...] += dkg_t
        ds_ref[...] += ds_t

    cur = lambda w: pl.BlockSpec((nb * BLK, w), lambda i: (n - 1 - i, 0))
    prev = lambda w: pl.BlockSpec((BLK, w), lambda i: (jnp.maximum((n - 1 - i) * nb - 1, 0), 0))
    return pl.pallas_call(
        _hosting(body, hosted, 8, 4, 1, n), name=name, grid=(n,),
        in_specs=[cur(QKV_W), prev(QKV_W), cur(3 * RW), prev(3 * RW), _full_spec(qg), _full_spec(kg), _full_spec(sinks),
                  cur(RW)] + h_in_specs,
        out_specs=[cur(QKV_W), _full_spec(qg), _full_spec(kg), _full_spec(sinks)] + h_out_specs,
        out_shape=[jax.ShapeDtypeStruct((T, QKV_W), MXU), jax.ShapeDtypeStruct(qg.shape, F32),
                   jax.ShapeDtypeStruct(kg.shape, F32), jax.ShapeDtypeStruct(sinks.shape, F32)] + h_out_shape,
        scratch_shapes=[pltpu.VMEM((BLK, QKV_W), F32)] + h_scratch,
        compiler_params=_cparams(("arbitrary",)),
    )(qkv, qkv, tab, tab, qg, kg, sinks, dy, *h_in)


def _shift_down(cur, prev8, i):
    rolled = pltpu.roll(cur, 1, 0)
    first_row = jnp.where(i > 0, prev8[7:8, :], 0.0)
    row = lax.broadcasted_iota(jnp.int32, cur.shape, 0)
    return jnp.where(row == 0, first_row, rolled)


def _shift_up(cur, next8, i, n):
    tm = cur.shape[0]
    rolled = pltpu.roll(cur, tm - 1, 0)
    last_row = jnp.where(i < n - 1, next8[0:1, :], 0.0)
    row = lax.broadcasted_iota(jnp.int32, cur.shape, 0)
    return jnp.where(row == tm - 1, last_row, rolled)


def _ada_parts(ada):
    return [ada[:, j * D:(j + 1) * D] for j in range(6)]


def _rope_table(positions):
    half = HD // 8
    inv_freq = 500000.0 ** (-jnp.arange(half, dtype=F32) / half)
    ang = positions.astype(F32)[:, None] * inv_freq
    cos, sin = jnp.cos(ang), jnp.sin(ang)
    T = positions.shape[0]
    pad = HD - 2 * half
    c64 = jnp.concatenate([cos, cos, jnp.ones((T, pad), F32)], axis=1)
    lo64 = jnp.concatenate([-sin, jnp.zeros((T, HD - half), F32)], axis=1)
    hi64 = jnp.concatenate([jnp.zeros((T, half), F32), sin, jnp.zeros((T, pad), F32)], axis=1)
    return jnp.concatenate([jnp.tile(t, (1, NH)) for t in (c64, lo64, hi64)], axis=1)


LATE_SCAN = ['ffn_w1', 'ffn_w3']
LATE_ATTN = ['ffn_w2', 'w_branch_a', 'w_branch_b', 'w_out']
BACK_ATTN = ['ffn_w1', 'ffn_w3', 'ffn_w2']
BACK_SCAN = ['w_out', 'w_branch_a', 'w_branch_b']
BACK_LAST = ['w_in', 'decay_up', 'iclr_up', 'gate_up']


def _full_weight(g, ax):
    return g.reshape(-1, g.shape[2]) if ax == 0 else jnp.concatenate([g[j] for j in range(4)], axis=1)


def _local_step(x, target, ada, tab, w, s, shards=None):
    T = x.shape[0]
    tm = _pick(T, (512, 256, 128))
    tm_wide = _pick(T, (256, 128))
    tm_vjp = _pick(T, (128,))
    row = lambda n, dt=F32: (n, dt, 'row')
    acc = lambda n, r=1: (n, F32, r)

    def f_norm1(i, n, x_, g, ada_):
        sh, sc = ada_[:, 0:D], ada_[:, D:2 * D]
        return (_norm_mod(x_, g, sc, sh),)
    (h1,) = _rowwise(f_norm1, [x], [s['norm1_gain'], ada], [row(D, MXU)], tm=tm, name="f_norm1")
    proj = _mm_nn(h1, w['w_in'][:, :SHIFT_W], name="f_proj_shift")
    proj_qkv = _mm_nn(h1, w['w_in'][:, SHIFT_W:SHIFT_W + QKV_W], name="f_proj_qkv")
    proj_g = _mm_nn(h1, w['w_in'][:, SHIFT_W + QKV_W:], name="f_proj_gates", out_dtype=MXU)
    prep_consts = [s['decay_w0'], s['lora_up'], s['iclr_a0'], s['gate_up'], s['k_k'], s['k_a']]

    def f_prep(i, n, cur, prev8, mu, *params):
        mixed = cur + (_shift_down(cur, prev8, i) - cur) * mu
        return (_prep(mixed, *params),)
    (rw,) = _rowwise(f_prep, [(proj, SHIFT_W)], [s['tshift_mu']] + prep_consts, [row(7 * RW)], tm=tm_wide,
                     name="f_prep", halo=[(proj, SHIFT_W, 'prev')])
    ax = dict(SHARDED)
    w = dict(w)
    y, ck, *got = _scan_fwd(rw, name="f_scan", hosted=shards and _GatherChips([shards[n] for n in LATE_SCAN]))
    if shards:
        w['ffn_w13'] = jnp.concatenate([_full_weight(g, ax[n]) for n, g in zip(LATE_SCAN, got, strict=True)], axis=1)
    post_consts = [s['lnx_gain'], s['lnx_bias'], s['r_k']]

    def post_of(y_, rw_, *params):
        return _post(y_, rw_[:, 0:RW], rw_[:, 2 * RW:3 * RW], rw_[:, 3 * RW:4 * RW], rw_[:, 6 * RW:7 * RW], *params)

    def f_post(i, n, y_, rw_, *params):
        return (post_of(y_, rw_, *params),)
    (ya,) = _rowwise(f_post, [y, rw], post_consts, [row(RW, MXU)], tm=tm_wide, name="f_post")
    yb, *got = _attn_fwd(proj_qkv, tab, s['q_norm_gain'], s['k_norm_gain'], s['attn_sinks'], name="f_attn",
                         hosted=shards and _GatherChips([shards[n] for n in LATE_ATTN]))
    if shards:
        w.update({n: _full_weight(g, ax[n]) for n, g in zip(LATE_ATTN, got, strict=True)})
    ma = _mm_nn(ya, w['w_branch_a'], name="f_branch_a", out_dtype=MXU)
    mb = _mm_nn(yb, w['w_branch_b'], name="f_branch_b", out_dtype=MXU)

    def f_merge(i, n, pg, ma_, mb_, bias):
        return (_merge(pg.astype(F32), ma_.astype(F32), mb_.astype(F32), bias),)
    (merged,) = _rowwise(f_merge, [proj_g, ma, mb], [s['branch_gate_b']], [row(D, MXU)], tm=tm, name="f_merge")
    mo = _mm_nn(merged, w['w_out'], name="f_out")

    def f_res1(i, n, x_, mo_, g, ada_):
        x1_ = x_ + ada_[:, 2 * D:3 * D] * mo_
        return x1_, _norm_mod(x1_, g, ada_[:, 4 * D:5 * D], ada_[:, 3 * D:4 * D])
    x1, h2 = _rowwise(f_res1, [x, mo], [s['norm2_gain'], ada], [row(D), row(D, MXU)], tm=tm, name="f_res1")
    uv = _mm_nn(h2, w['ffn_w13'], name="f_ffn_in", out_dtype=MXU)

    def f_act(i, n, uv_):
        return (_swiglu(uv_[:, :DFF].astype(F32), uv_[:, DFF:].astype(F32)),)
    (act,) = _rowwise(f_act, [uv], [], [row(DFF, MXU)], tm=tm_wide, name="f_act")
    ff = _mm_nn(act, w['ffn_w2'], name="f_ffn_out")

    def f_loss(i, n, x1_, ff_, tgt, ada_):
        g2 = ada_[:, 5 * D:6 * D]
        err = x1_ + g2 * ff_ - tgt
        dx2 = err * (1.0 / D)
        loss = 0.5 * jnp.sum(jnp.sum(err * err, axis=1, keepdims=True) * (1.0 / D), axis=0, keepdims=True)
        return dx2, (dx2 * g2), jnp.broadcast_to(loss, (1, 128)), jnp.sum(dx2 * ff_, axis=0, keepdims=True)
    dx2, dff, loss, dgate2 = _rowwise(f_loss, [x1, ff, target], [ada], [row(D), row(D, MXU), acc(128), acc(D)],
                                      tm=tm, name="f_loss")

    dact = _mm_nt(dff, w['ffn_w2'], name="b_ffn_out_dx", out_dtype=MXU)
    g_w2 = _mm_tn(act, dff, name="b_ffn_out_dw", out_dtype=MXU)

    def b_act(i, n, uv_, dact_):
        _, vjp = jax.vjp(_swiglu, uv_[:, :DFF].astype(F32), uv_[:, DFF:].astype(F32))
        du, dv = vjp(dact_.astype(F32))
        return (jnp.concatenate([du, dv], axis=1),)
    (duv,) = _rowwise(b_act, [uv, dact], [], [row(2 * DFF, MXU)], tm=tm_wide, name="b_act")
    dh2 = _mm_nt(duv, w['ffn_w13'], name="b_ffn_in_dx")
    g_w13 = _mm_tn(h2, duv, name="b_ffn_in_dw", out_dtype=MXU)

    def b_res1(i, n, x1_, dh2_, dx2_, mo_, g, ada_):
        _, vjp = jax.vjp(_norm_mod, x1_, g, ada_[:, 4 * D:5 * D], ada_[:, 3 * D:4 * D])
        dxn, dg, dsc, dsh = vjp(dh2_)
        dx1_ = dxn + dx2_
        g1 = ada_[:, 2 * D:3 * D]
        return dx1_, dx1_ * g1, dg, dsc, dsh, jnp.sum(dx1_ * mo_, axis=0, keepdims=True)
    dx1, dmo, d_gain2, d_scale2, d_shift2, dgate1 = _rowwise(
        b_res1, [x1, dh2, dx2, mo], [s['norm2_gain'], ada], [row(D), row(D, MXU), acc(D), acc(D), acc(D), acc(D)],
        tm=tm, name="b_res1")
    dmerged = _mm_nt(dmo, w['w_out'], name="b_out_dx", out_dtype=MXU)
    g_wout = _mm_tn(merged, dmo, name="b_out_dw", out_dtype=MXU)

    def b_merge(i, n, pg, ma_, mb_, dm, bias):
        _, vjp = jax.vjp(_merge, pg.astype(F32), ma_.astype(F32), mb_.astype(F32), bias)
        dpg, dma_, dmb_, dbias = vjp(dm.astype(F32))
        return dpg, dma_, dmb_, dbias
    dpg, dma, dmb, d_bias = _rowwise(b_merge, [proj_g, ma, mb, dmerged], [s['branch_gate_b']],
                                     [row(GATE_W, MXU), row(D, MXU), row(D, MXU), acc(GATE_W)], tm=tm_wide, name="b_merge")
    dya = _mm_nt(dma, w['w_branch_a'], name="b_branch_a_dx")
    g_wa = _mm_tn(ya, dma, name="b_branch_a_dw", out_dtype=MXU, col_shards=4)
    dyb = _mm_nt(dmb, w['w_branch_b'], name="b_branch_b_dx", out_dtype=F32)
    g_wb = _mm_tn(yb, dmb, name="b_branch_b_dw", out_dtype=MXU, col_shards=4)
    fs = DFF // 4
    gw = dict(w_branch_a=g_wa, w_branch_b=g_wb, w_out=g_wout.reshape(4, D // 4, D),
              ffn_w1=jnp.stack([g_w13[:, j * fs:(j + 1) * fs] for j in range(4)]),
              ffn_w3=jnp.stack([g_w13[:, DFF + j * fs:DFF + (j + 1) * fs] for j in range(4)]),
              ffn_w2=g_w2.reshape(4, fs, D))
    recv = {}
    dqkv, d_qg, d_kg, d_sinks, *got = _attn_bwd(
        proj_qkv, tab, s['q_norm_gain'], s['k_norm_gain'], s['attn_sinks'], dyb, name="b_attn",
        hosted=shards and _ScatterChips([gw[n] for n in BACK_ATTN]))
    recv.update(zip(BACK_ATTN, got))

    def b_post(i, n, y_, rw_, dya_, *params):
        _, vjp = jax.vjp(post_of, y_, rw_, *params)
        return vjp(dya_)
    dy, drw_post, d_lnx_gain, d_lnx_bias, d_r_k = _rowwise(
        b_post, [y, rw, dya], post_consts, [row(RW), row(7 * RW), acc(RW), acc(RW), acc(RW)], tm=tm_vjp, name="b_post")
    dscan, *got = _scan_bwd(rw, ck, dy, name="b_scan",
                            hosted=shards and _ScatterChips([gw[n] for n in BACK_SCAN]))
    recv.update(zip(BACK_SCAN, got))

    def b_prep(i, n, cur, drw_, dscan_, prev8, mu, *params):
        shifted = _shift_down(cur, prev8, i)
        mixed = cur + (shifted - cur) * mu
        _, vjp = jax.vjp(_prep, mixed, *params)
        ct = drw_ + jnp.concatenate([dscan_, jnp.zeros((dscan_.shape[0], RW), F32)], axis=1)
        grads = vjp(ct)
        dmixed = grads[0]
        return (dmixed, jnp.sum(dmixed * (shifted - cur), axis=0, keepdims=True)) + tuple(grads[1:])
    dmixed, d_mu, d_w0, d_lora, d_a0, d_gate_up, d_kk, d_ka = _rowwise(
        b_prep, [(proj, SHIFT_W), drw_post, dscan], [s['tshift_mu']] + prep_consts,
        [row(SHIFT_W), acc(SHIFT_W), acc(RW), acc(2 * RW, 128), acc(RW), acc(RW, 128), acc(RW), acc(RW)],
        tm=tm_vjp, name="b_prep", halo=[(proj, SHIFT_W, 'prev')])

    def b_gather(i, n, dm, dqkv_, dpg_, next8, mu):
        dcur = dm * (1.0 - mu) + _shift_up(dm, next8, i, n) * mu
        return (jnp.concatenate([dcur.astype(MXU), dqkv_, dpg_], axis=1),)
    (dproj,) = _rowwise(b_gather, [dmixed, dqkv, dpg], [s['tshift_mu']], [row(IN_W, MXU)], tm=tm_wide, name="b_gather",
                        halo=[(dmixed, SHIFT_W, 'next')])
    g_win = _mm_tn(h1, dproj, name="b_proj_dw", out_dtype=MXU, col_shards=4)

    def col_blocks(g):
        k, n = g.shape
        return g.reshape(k, 4, n // 4).transpose(1, 0, 2).astype(MXU)
    gw.update(w_in=g_win, decay_up=col_blocks(d_lora[:64, :RW]), iclr_up=col_blocks(d_lora[64:, RW:]),
              gate_up=col_blocks(d_gate_up))
    if shards:
        dh1, *got = _mm_nt(dproj, w['w_in'], name="b_proj_dx", hosted=_ScatterChips([gw[n] for n in BACK_LAST]))
        recv.update(zip(BACK_LAST, got))
    else:
        dh1 = _mm_nt(dproj, w['w_in'], name="b_proj_dx")

    def b_norm1(i, n, x_, dh1_, dx1_, g, ada_):
        _, vjp = jax.vjp(_norm_mod, x_, g, ada_[:, D:2 * D], ada_[:, 0:D])
        dxn, dg, dsc, dsh = vjp(dh1_)
        return dxn + dx1_, dg, dsc, dsh
    dx, d_gain1, d_scale1, d_shift1 = _rowwise(b_norm1, [x, dh1, dx1], [s['norm1_gain'], ada],
                                               [row(D), acc(D), acc(D), acc(D)], tm=tm, name="b_norm1")

    d_ada = jnp.concatenate([d_shift1, d_scale1, dgate1, d_shift2, d_scale2, dgate2], axis=1)
    gs = dict(norm1_gain=d_gain1, norm2_gain=d_gain2, tshift_mu=d_mu, decay_w0=d_w0, iclr_a0=d_a0, k_k=d_kk, k_a=d_ka,
              r_k=d_r_k, lnx_gain=d_lnx_gain, lnx_bias=d_lnx_bias, q_norm_gain=d_qg, k_norm_gain=d_kg,
              attn_sinks=d_sinks, branch_gate_b=d_bias)
    return loss, dx, d_ada, gw, gs, recv


ANY = pl.BlockSpec(memory_space=pl.ANY)


def _place():
    x, y, c = lax.axis_index("x"), lax.axis_index("y"), lax.axis_index("c")
    return x, y, c, [(1 - x, y), (x, 1 - y), (1 - x, 1 - y)]


def _all_gather8(x_shard, *, name):
    m_per, n = x_shard.shape

    def body(x_ref, out_ref, send_sems, recv_sems, local_sem):
        x, y, c, chips = _place()
        me, sibling = (x, y, c), (x, y, 1 - c)

        def rows(px, py, pc):
            return out_ref.at[pl.ds((4 * px + 2 * py + pc) * m_per, m_per), :]

        def copy(k, block, to, src=None):
            return pltpu.make_async_remote_copy(
                src_ref=rows(*block) if src is None else src, dst_ref=rows(*block),
                send_sem=send_sems.at[k], recv_sem=recv_sems.at[k], device_id=to, device_id_type=MESH)

        mine = pltpu.make_async_copy(x_ref, rows(*me), local_sem)
        mine.start()
        first = [copy(0, me, sibling, src=x_ref)]
        first += [copy(1 + j, me, (*chip, c), src=x_ref) for j, chip in enumerate(chips)]
        for cp in first:
            cp.start()
        passed = [copy(4 + j, (*chip, c), sibling) for j, chip in enumerate(chips)]
        for j, chip in enumerate(chips):
            copy(1 + j, (*chip, c), me).wait_recv()
            passed[j].start()
        copy(0, sibling, me).wait_recv()
        for j, chip in enumerate(chips):
            copy(4 + j, (*chip, 1 - c), me).wait_recv()
        for cp in first + passed:
            cp.wait_send()
        mine.wait()

    return pl.pallas_call(
        body, name=name, out_shape=jax.ShapeDtypeStruct((8 * m_per, n), x_shard.dtype),
        in_specs=[pl.BlockSpec(memory_space=pltpu.VMEM)], out_specs=pl.BlockSpec(memory_space=pltpu.VMEM),
        scratch_shapes=[pltpu.SemaphoreType.DMA((7,)), pltpu.SemaphoreType.DMA((7,)), pltpu.SemaphoreType.DMA],
    )(x_shard)


class _GatherChips:
    def __init__(self, shards):
        n = len(shards)
        self.arrays, self.n_in, self.n_out = list(shards), n, n
        self.out_shape = [jax.ShapeDtypeStruct((4,) + s.shape, s.dtype) for s in shards]
        self.scratch = [pltpu.SemaphoreType.DMA((3 * n,)), pltpu.SemaphoreType.DMA((3 * n,)),
                        pltpu.SemaphoreType.DMA((n,))]

    def _copies(self, x_refs, out_refs, sems, receiving):
        send_sems, recv_sems, local_sems = sems
        x, y, c, chips = _place()
        s_me = 2 * x + y
        n = self.n_in

        def copy(a, k, s):
            return pltpu.make_async_remote_copy(
                src_ref=x_refs[a], dst_ref=out_refs[a].at[s], send_sem=send_sems.at[3 * a + k],
                recv_sem=recv_sems.at[3 * a + k], device_id=(*chips[k], c), device_id_type=MESH)

        mine = [pltpu.make_async_copy(x_refs[a], out_refs[a].at[s_me], local_sems.at[a]) for a in range(n)]
        sends = [copy(a, k, s_me) for a in range(n) for k in range(3)]
        if not receiving:
            return mine, sends
        return mine, sends, [copy(a, k, 2 * px + py) for a in range(n) for k, (px, py) in enumerate(chips)]

    def start(self, x_refs, out_refs, sems):
        mine, sends = self._copies(x_refs, out_refs, sems, False)
        for cp in mine + sends:
            cp.start()

    def wait(self, x_refs, out_refs, sems):
        mine, sends, recvs = self._copies(x_refs, out_refs, sems, True)
        for cp in recvs:
            cp.wait_recv()
        for cp in sends:
            cp.wait_send()
        for cp in mine:
            cp.wait()


class _ScatterChips:
    def __init__(self, parts):
        n = len(parts)
        self.arrays, self.n_in, self.n_out = list(parts), n, n
        self.out_shape = [jax.ShapeDtypeStruct((3,) + p.shape[1:], p.dtype) for p in parts]
        self.scratch = [pltpu.SemaphoreType.DMA((3 * n,)), pltpu.SemaphoreType.DMA((3 * n,))]

    def _copies(self, g_refs, out_refs, sems):
        send_sems, recv_sems = sems
        x, y, c, chips = _place()
        return [pltpu.make_async_remote_copy(
            src_ref=g_refs[a].at[2 * px + py], dst_ref=out_refs[a].at[k], send_sem=send_sems.at[3 * a + k],
            recv_sem=recv_sems.at[3 * a + k], device_id=(px, py, c), device_id_type=MESH)
            for a in range(self.n_in) for k, (px, py) in enumerate(chips)]

    def start(self, g_refs, out_refs, sems):
        for cp in self._copies(g_refs, out_refs, sems):
            cp.start()

    def wait(self, g_refs, out_refs, sems):
        sends = self._copies(g_refs, out_refs, sems)
        for cp in sends:
            cp.wait_recv()
        for cp in sends:
            cp.wait_send()


def _exchange_call(ex, *, name):
    def body(*refs):
        parts = (refs[:ex.n_in], refs[ex.n_in:ex.n_in + ex.n_out], refs[ex.n_in + ex.n_out:])
        ex.start(*parts)
        ex.wait(*parts)

    return pl.pallas_call(body, name=name, out_shape=ex.out_shape, in_specs=[ANY] * ex.n_in,
                          out_specs=[ANY] * ex.n_out, scratch_shapes=ex.scratch)(*ex.arrays)


def _swap_sibling(vs, *, name):
    n = len(vs)

    def body(*refs):
        v_refs, out_refs = refs[:n], refs[n:2 * n]
        send_sems, recv_sems = refs[2 * n:]
        x, y, c, _ = _place()
        cps = [pltpu.make_async_remote_copy(src_ref=v_refs[a], dst_ref=out_refs[a], send_sem=send_sems.at[a],
                                            recv_sem=recv_sems.at[a], device_id=(x, y, 1 - c), device_id_type=MESH)
               for a in range(n)]
        for cp in cps:
            cp.start()
        for cp in cps:
            cp.wait()

    return pl.pallas_call(
        body, name=name, out_shape=[jax.ShapeDtypeStruct(v.shape, v.dtype) for v in vs],
        in_specs=[ANY] * n, out_specs=[ANY] * n,
        scratch_shapes=[pltpu.SemaphoreType.DMA((n,)), pltpu.SemaphoreType.DMA((n,))],
    )(*vs)


def _sum_parts(own, others, *, name):
    R, C = own.shape
    tm = _pick(R, (256, 128, 64))

    def body(own_ref, o0_ref, o1_ref, o2_ref, out_ref):
        tot = own_ref[...].astype(F32)
        for ref in (o0_ref, o1_ref, o2_ref):
            tot = tot + ref[...].astype(F32)
        out_ref[...] = tot

    part = lambda k: pl.BlockSpec((None, tm, C), lambda i: (k, i, 0))
    return pl.pallas_call(
        body, name=name, grid=(R // tm,),
        in_specs=[pl.BlockSpec((tm, C), lambda i: (i, 0)), part(0), part(1), part(2)],
        out_specs=pl.BlockSpec((tm, C), lambda i: (i, 0)), out_shape=jax.ShapeDtypeStruct((R, C), F32),
        compiler_params=_cparams(("arbitrary",)),
    )(own, others, others, others)


def _adam_math(w_, m_, v_, g):
    m2 = ADAM_B1 * m_ + (1.0 - ADAM_B1) * g
    v2 = ADAM_B2 * v_ + (1.0 - ADAM_B2) * jnp.square(g)
    m_hat = m2 / (1.0 - ADAM_B1 ** ADAM_STEP)
    v_hat = v2 / (1.0 - ADAM_B2 ** ADAM_STEP)
    delta = -ADAM_LR * (m_hat / (jnp.sqrt(v_hat) + ADAM_EPS) + ADAM_WD * w_)
    return delta, m2, v2


SMALL_SLOTS = 16
SMALL_COLS = 6 * D


def _pack_small(grads, *, name):
    n = len(grads)

    def body(*refs):
        out_ref = refs[n]
        out_ref[...] = jnp.zeros_like(out_ref)
        for i, ref in enumerate(refs[:n]):
            out_ref[i:i + 1, 0:ref.shape[1]] = ref[...]

    return pl.pallas_call(body, name=name, out_shape=jax.ShapeDtypeStruct((SMALL_SLOTS, SMALL_COLS), F32))(*grads)


def _adamw_small(ws, ms, vs, gathered, *, name):
    n = len(ws)

    def body(*refs):
        w_refs, m_refs, v_refs, g_ref = refs[:n], refs[n:2 * n], refs[2 * n:3 * n], refs[3 * n]
        outs = refs[3 * n + 1:]
        for i in range(n):
            nc = w_refs[i].shape[1]
            g = g_ref[i:i + 1, 0:nc]
            for d in range(1, 8):
                g = g + g_ref[d * SMALL_SLOTS + i:d * SMALL_SLOTS + i + 1, 0:nc]
            delta, m2, v2 = _adam_math(w_refs[i][...], m_refs[i][...], v_refs[i][...], g)
            for k, val in enumerate((g, delta, m2, v2)):
                outs[k * n + i][...] = val

    shapes = [jax.ShapeDtypeStruct(w.shape, F32) for w in ws]
    res = pl.pallas_call(body, name=name, out_shape=shapes * 4,
                         compiler_params=pltpu.CompilerParams(vmem_limit_bytes=VMEM_LIMIT))(*ws, *ms, *vs, gathered)
    return [res[k * n:(k + 1) * n] for k in range(4)]


def _adamw(w, m, v, gparts, *, tm, name):
    def fn(i, n, w_, m_, v_, *gs):
        g = gs[0]
        for p in gs[1:]:
            g = g + p
        return (g,) + _adam_math(w_, m_, v_, g)
    nc = w.shape[1]
    return _rowwise(fn, [w, m, v] + list(gparts), [], [(nc, F32, 'row')] * 4, tm=tm, name=name)


WEIGHTS = ['ada_w', 'ada_b', 'norm1_gain', 'norm2_gain', 'w_in', 'tshift_mu', 'decay_w0', 'decay_up', 'iclr_a0',
           'iclr_up', 'gate_up', 'k_k', 'k_a', 'r_k', 'lnx_gain', 'lnx_bias', 'q_norm_gain', 'k_norm_gain', 'attn_sinks',
           'branch_gate_b', 'w_branch_a', 'w_branch_b', 'w_out', 'ffn_w1', 'ffn_w3', 'ffn_w2']
SHARDED = [('w_in', 1), ('decay_up', 1), ('iclr_up', 1), ('gate_up', 1), ('w_branch_a', 1), ('w_branch_b', 1),
           ('w_out', 0), ('ffn_w1', 1), ('ffn_w3', 1), ('ffn_w2', 0)]
SMALL = ['ada_b', 'norm1_gain', 'norm2_gain', 'tshift_mu', 'decay_w0', 'iclr_a0', 'k_k', 'k_a', 'r_k', 'lnx_gain',
         'lnx_bias', 'q_norm_gain', 'k_norm_gain', 'attn_sinks', 'branch_gate_b']


def kernel(x, c, positions, ada_w, ada_b, norm1_gain, norm2_gain, w_in, tshift_mu, decay_w0, decay_up, iclr_a0, iclr_up, gate_up, k_k, k_a, r_k, lnx_gain, lnx_bias, q_norm_gain, k_norm_gain, attn_sinks, branch_gate_b, w_branch_a, w_branch_b, w_out, ffn_w1, ffn_w3, ffn_w2, loss_target, m_ada_w, m_ada_b, m_norm1_gain, m_norm2_gain, m_w_in, m_tshift_mu, m_decay_w0, m_decay_up, m_iclr_a0, m_iclr_up, m_gate_up, m_k_k, m_k_a, m_r_k, m_lnx_gain, m_lnx_bias, m_q_norm_gain, m_k_norm_gain, m_attn_sinks, m_branch_gate_b, m_w_branch_a, m_w_branch_b, m_w_out, m_ffn_w1, m_ffn_w3, m_ffn_w2, v_ada_w, v_ada_b, v_norm1_gain, v_norm2_gain, v_w_in, v_tshift_mu, v_decay_w0, v_decay_up, v_iclr_a0, v_iclr_up, v_gate_up, v_k_k, v_k_a, v_r_k, v_lnx_gain, v_lnx_bias, v_q_norm_gain, v_k_norm_gain, v_attn_sinks, v_branch_gate_b, v_w_branch_a, v_w_branch_b, v_w_out, v_ffn_w1, v_ffn_w3, v_ffn_w2):
    a = dict(locals())
    W = {n: a[n] for n in WEIGHTS}
    M = {n: a['m_' + n] for n in WEIGHTS}
    V = {n: a['v_' + n] for n in WEIGHTS}
    xi, yi, ci = lax.axis_index("x"), lax.axis_index("y"), lax.axis_index("c")
    me = 4 * xi + 2 * yi + ci
    shard = 2 * xi + yi
    mat = lambda t: t.reshape(t.shape[-2], t.shape[-1])
    sharded = [n for n, _ in SHARDED]

    ax = dict(SHARDED)
    late = LATE_SCAN + LATE_ATTN
    early = [n for n in sharded if n not in late]
    shards = {n: mat(W[n]).astype(MXU) for n in sharded}
    gathered = _exchange_call(_GatherChips([shards[n] for n in early]), name="gather_weights")
    full = {n: _full_weight(g, ax[n]) for n, g in zip(early, gathered, strict=True)}

    c_all = _all_gather8(jnp.broadcast_to(c, (8, D)), name="gather_c")[0::8]
    pad_rows = lambda t: jnp.concatenate([t, jnp.zeros((BLK - 8, t.shape[1]), t.dtype)])
    c_all = pad_rows(c_all.astype(MXU))
    ada_cols = _mm_nn(c_all, mat(ada_w).astype(MXU), name="f_ada")[:8]
    ada_all = _all_gather8(ada_cols, name="gather_ada").reshape(2, 2, 2, 8, 6 * D // 4)
    ada_mine = lax.dynamic_index_in_dim(ada_all[:, :, 0], me, axis=2, keepdims=False)
    ada = ada_mine.reshape(1, 6 * D) + mat(ada_b)

    zero = jnp.zeros((64, RW), MXU)
    lora = jnp.concatenate([jnp.concatenate([full['decay_up'], zero], axis=1),
                            jnp.concatenate([zero, full['iclr_up']], axis=1)], axis=0)
    s = {n: W[n].reshape(1, -1) for n in SMALL if n != 'ada_b'}
    s['lora_up'] = lora.astype(F32)
    s['gate_up'] = full['gate_up'].astype(F32)
    tab = _rope_table(positions.reshape(-1))
    loss, dx, d_ada, gw, gs, from_chips = _local_step(x[0], loss_target[0], ada, tab, dict(w_in=full['w_in']), s,
                                                      shards={n: shards[n] for n in late})
    loss = lax.psum(loss[0, 0], ("x", "y", "c"))

    gs['ada_b'] = d_ada
    gsmall = _pack_small([gs[n] for n in SMALL], name="pack_small_grads")
    gsmall_all = _all_gather8(gsmall, name="gather_small_grads")
    row = lambda src: [src[n].reshape(1, -1) for n in SMALL]
    sm_out = _adamw_small(row(W), row(M), row(V), gsmall_all, name="adamw_small")
    sm_out = [{n: o.reshape(W[n].shape) for n, o in zip(SMALL, outs_k, strict=True)} for outs_k in sm_out]

    d_ada_all = gsmall_all[0::SMALL_SLOTS]
    d_ada_cols = lax.dynamic_slice_in_dim(d_ada_all, shard * (6 * D // 4), 6 * D // 4, axis=1)
    g_ada_w = _mm_tn(c_all, pad_rows(d_ada_cols.astype(MXU)), name="b_ada")
    ada_out = _adamw(mat(ada_w), mat(m_ada_w), mat(v_ada_w), [g_ada_w], tm=256, name="adamw_ada")

    parts =[_sum_parts(lax.dynamic_index_in_dim(gw[n], shard, axis=0, keepdims=False), from_chips[n], name="sum_" + n)
             for n in sharded]
    others = _swap_sibling(parts, name="swap_grads")
    sh_out = {}
    for n, part, other in zip(sharded, parts, others, strict=True):
        sh_out[n] = _adamw(mat(W[n]), mat(M[n]), mat(V[n]), [part, other], tm=_pick(part.shape[0], (256, 128, 64)),
                           name="adamw_" + n)

    def leaf(k, n):
        if n == 'ada_w':
            return ada_out[k].reshape(W[n].shape)
        if n in sharded:
            return sh_out[n][k].reshape(W[n].shape)
        return sm_out[k][n]
    outs = [leaf(k, n) for k in range(4) for n in WEIGHTS]
    return (loss, dx[None], *outs)
```

```python
import functools
import math

import jax
import jax.numpy as jnp
from jax import lax
from jax.experimental import pallas as pl
from jax.experimental.pallas import tpu as pltpu

F32 = jnp.float32
BF16 = jnp.bfloat16
MXU = BF16
HI = lax.Precision.HIGHEST

D = 1024
HD = 64
NH = 8
RW = NH * HD
SHIFT_W = 3 * RW + 64 + 64 + 128
QKV_W = RW + 2 * 128
GATE_W = 2 * D
IN_W = SHIFT_W + QKV_W + GATE_W
DFF = 2816
BLK = 128
CHUNK = 64
RMS_EPS = 1e-6
GN_EPS = 64e-5
NEG_INF = -1e30
ADAM_LR, ADAM_B1, ADAM_B2, ADAM_EPS, ADAM_WD, ADAM_STEP = 0.001, 0.9, 0.999, 1e-08, 0.01, 10
VMEM_LIMIT = 56 * 1024 * 1024
MESH = pl.DeviceIdType.MESH


def _cparams(sem=None):
    return pltpu.CompilerParams(dimension_semantics=sem, vmem_limit_bytes=VMEM_LIMIT)


def _full_spec(a):
    nd = a.ndim
    return pl.BlockSpec(a.shape, lambda *_: (0,) * nd)


def _rowwise(fn, rows, consts, outs, *, tm, name, halo=(), hosted=None):
    rows = [a if isinstance(a, tuple) else (a, a.shape[1]) for a in rows]
    T = rows[0][0].shape[0]
    assert T % tm == 0 and tm % 8 == 0
    n_tiles = T // tm
    n_in = len(rows) + len(halo) + len(consts)
    in_specs = [pl.BlockSpec((tm, nc), lambda i: (i, 0)) for _, nc in rows]
    args = [a for a, _ in rows]
    for a, nc, kind in halo:
        if kind == 'prev':
            in_specs.append(pl.BlockSpec((8, nc), lambda i: (jnp.maximum(i * (tm // 8) - 1, 0), 0)))
        else:
            in_specs.append(pl.BlockSpec((8, nc), lambda i: (jnp.minimum((i + 1) * (tm // 8), T // 8 - 1), 0)))
        args.append(a)
    in_specs += [_full_spec(a) for a in consts]
    args += list(consts)
    out_shape, out_specs = [], []
    for ncols, dtype, kind in outs:
        if kind == 'row':
            out_shape.append(jax.ShapeDtypeStruct((T, ncols), dtype))
            out_specs.append(pl.BlockSpec((tm, ncols), lambda i: (i, 0)))
        else:
            out_shape.append(jax.ShapeDtypeStruct((kind, ncols), dtype))
            out_specs.append(pl.BlockSpec((kind, ncols), lambda i: (0, 0)))

    def body(*refs):
        i = pl.program_id(0)
        vals = [r[...] for r in refs[:n_in]]
        res = fn(i, n_tiles, *vals)
        for (ncols, dtype, kind), o_ref, val in zip(outs, refs[n_in:], res, strict=True):
            if kind == 'row':
                o_ref[...] = val.astype(dtype)
            else:
                @pl.when(i == 0)
                def _():
                    o_ref[...] = jnp.zeros_like(o_ref)
                o_ref[...] += val.astype(dtype)

    h_in, h_in_specs, h_out_specs, h_out_shape, h_scratch = _hosted_args(hosted)
    res = pl.pallas_call(
        _hosting(body, hosted, n_in, len(outs), 0, n_tiles), name=name, grid=(n_tiles,),
        in_specs=in_specs + h_in_specs, out_specs=out_specs + h_out_specs, out_shape=out_shape + h_out_shape,
        scratch_shapes=h_scratch, compiler_params=_cparams(("arbitrary",)),
    )(*args, *h_in)
    return res


def _pick(n, cands):
    for c in cands:
        if n % c == 0:
            return c
    return n


MM_ROWS = (1024, 512, 256, 128)
MM_COLS = (1536, 1408, 1024, 896, 768, 512, 256, 128)
MM_WIDE = 3000


def _mm_nn(a, w, *, name, out_dtype=F32, hosted=None):
    T, K = a.shape
    N = w.shape[1]
    tm = _pick(T, MM_ROWS)
    tn = _pick(N, MM_COLS)
    grid = (N // tn, T // tm)
    h_in, h_in_specs, h_out_specs, h_out_shape, h_scratch = _hosted_args(hosted)

    def body(a_ref, w_ref, o_ref):
        o_ref[...] = jnp.dot(a_ref[...], w_ref[...], preferred_element_type=F32).astype(out_dtype)

    res = pl.pallas_call(
        _hosting(body, hosted, 2, 1, 0, grid), name=name, grid=grid,
        in_specs=[pl.BlockSpec((tm, K), lambda j, i: (i, 0)), pl.BlockSpec((K, tn), lambda j, i: (0, j))] + h_in_specs,
        out_specs=[pl.BlockSpec((tm, tn), lambda j, i: (i, j))] + h_out_specs,
        out_shape=[jax.ShapeDtypeStruct((T, N), out_dtype)] + h_out_shape, scratch_shapes=h_scratch,
        compiler_params=_cparams(("arbitrary", "arbitrary")),
    )(a, w, *h_in)
    return res if hosted else res[0]


def _mm_nt(dy, w, *, name, out_dtype=F32, hosted=None):
    T, N = dy.shape
    K = w.shape[0]
    tm = _pick(T, MM_ROWS if N <= MM_WIDE else MM_ROWS[1:])
    tk = _pick(K, MM_COLS[1:])
    grid = (K // tk, T // tm)
    h_in, h_in_specs, h_out_specs, h_out_shape, h_scratch = _hosted_args(hosted)

    def body(dy_ref, w_ref, o_ref):
        o_ref[...] = lax.dot_general(dy_ref[...], w_ref[...], (((1,), (1,)), ((), ())),
                                     preferred_element_type=F32).astype(out_dtype)

    res = pl.pallas_call(
        _hosting(body, hosted, 2, 1, 0, grid), name=name, grid=grid,
        in_specs=[pl.BlockSpec((tm, N), lambda j, i: (i, 0)), pl.BlockSpec((tk, N), lambda j, i: (j, 0))] + h_in_specs,
        out_specs=[pl.BlockSpec((tm, tk), lambda j, i: (i, j))] + h_out_specs,
        out_shape=[jax.ShapeDtypeStruct((T, K), out_dtype)] + h_out_shape, scratch_shapes=h_scratch,
        compiler_params=_cparams(("arbitrary", "arbitrary")),
    )(dy, w, *h_in)
    return res if hosted else res[0]


def _mm_tn(a, dy, *, name, out_dtype=F32, col_shards=None):
    T, K = a.shape
    N = dy.shape[1]
    tm = _pick(T, MM_ROWS)
    tn = N // col_shards if col_shards else _pick(N, MM_COLS[1:])
    n_t = T // tm

    def body(a_ref, dy_ref, o_ref, acc_ref):
        i = pl.program_id(1)

        @pl.when(i == 0)
        def _():
            acc_ref[...] = jnp.zeros_like(acc_ref)

        acc_ref[...] += lax.dot_general(a_ref[...], dy_ref[...], (((0,), (0,)), ((), ())), preferred_element_type=F32)

        @pl.when(i == n_t - 1)
        def _():
            o_ref[...] = acc_ref[...].astype(out_dtype)

    if col_shards:
        out_specs = pl.BlockSpec((None, K, tn), lambda j, i: (j, 0, 0))
        out_shape = jax.ShapeDtypeStruct((col_shards, K, tn), out_dtype)
    else:
        out_specs = pl.BlockSpec((K, tn), lambda j, i: (0, j))
        out_shape = jax.ShapeDtypeStruct((K, N), out_dtype)
    return pl.pallas_call(
        body, name=name, grid=(N // tn, n_t),
        in_specs=[pl.BlockSpec((tm, K), lambda j, i: (i, 0)), pl.BlockSpec((tm, tn), lambda j, i: (i, j))],
        out_specs=out_specs, out_shape=out_shape, scratch_shapes=[pltpu.VMEM((K, tn), F32)],
        compiler_params=_cparams(("arbitrary", "arbitrary")),
    )(a, dy)


def _seg_ones(n):
    r = lax.broadcasted_iota(jnp.int32, (n, n), 0) // HD
    c = lax.broadcasted_iota(jnp.int32, (n, n), 1) // HD
    return (r == c).astype(F32)


def _segsum_raw(x):
    ones = _seg_ones(x.shape[1])
    if MXU == F32:
        return jnp.dot(x, ones, precision=HI, preferred_element_type=F32)
    hi = x.astype(MXU)
    lo = (x - hi.astype(F32)).astype(MXU)
    ones = ones.astype(MXU)
    return jnp.dot(hi, ones, preferred_element_type=F32) + jnp.dot(lo, ones, preferred_element_type=F32)


@jax.custom_vjp
def _segsum(x):
    return _segsum_raw(x)


def _segsum_fwd(x):
    return _segsum_raw(x), None


def _segsum_bwd(_, g):
    return (_segsum_raw(g),)


_segsum.defvjp(_segsum_fwd, _segsum_bwd)


def _mxu(x):
    return x.astype(MXU)


@jax.custom_vjp
def _bdot(a, b):
    return jnp.dot(_mxu(a), _mxu(b), preferred_element_type=F32)


def _bdot_fwd(a, b):
    return _bdot(a, b), (a, b)


def _bdot_bwd(res, g):
    a, b = res
    da = lax.dot_general(_mxu(g), _mxu(b), (((1,), (1,)), ((), ())), preferred_element_type=F32)
    db = lax.dot_general(_mxu(a), _mxu(g), (((0,), (0,)), ((), ())), preferred_element_type=F32)
    return da.astype(a.dtype), db.astype(b.dtype)


_bdot.defvjp(_bdot_fwd, _bdot_bwd)


@jax.custom_vjp
def _bdot_nt(a, b):
    return lax.dot_general(_mxu(a), _mxu(b), (((1,), (1,)), ((), ())), preferred_element_type=F32)


def _bdot_nt_fwd(a, b):
    return _bdot_nt(a, b), (a, b)


def _bdot_nt_bwd(res, g):
    a, b = res
    da = jnp.dot(_mxu(g), _mxu(b), preferred_element_type=F32)
    db = lax.dot_general(_mxu(g), _mxu(a), (((0,), (0,)), ((), ())), preferred_element_type=F32)
    return da.astype(a.dtype), db.astype(b.dtype)


_bdot_nt.defvjp(_bdot_nt_fwd, _bdot_nt_bwd)


def _sigmoid(x):
    return 1.0 / (1.0 + jnp.exp(-x))


def _softplus(x):
    return jnp.maximum(x, 0.0) + jnp.log(1.0 + jnp.exp(jnp.minimum(x, -x)))


def _norm_mod(x, gain, scale, shift):
    inv = lax.rsqrt(jnp.mean(x * x, axis=-1, keepdims=True) + RMS_EPS)
    return (x * inv) * gain * (1.0 + scale) + shift


def _prep(mixed, decay_w0, lora_up, iclr_a0, gate_up, k_k, k_a):
    r = mixed[:, 0:RW]
    k = mixed[:, RW:2 * RW]
    v = mixed[:, 2 * RW:3 * RW]
    z = mixed[:, 3 * RW:3 * RW + 128]
    xg = mixed[:, 3 * RW + 128:]
    lane = lax.broadcasted_iota(jnp.int32, z.shape, 1)
    tz = jnp.where(lane < 64, jnp.tanh(z), z)
    lo = _bdot(tz, lora_up)
    w_log = -_softplus(-(decay_w0 + lo[:, :RW])) - 0.5
    lw = -jnp.exp(w_log)
    a_ic = _sigmoid(iclr_a0 + lo[:, RW:])
    g = _bdot(_sigmoid(xg), gate_up)
    kk = k * k_k
    kk = kk / jnp.maximum(jnp.sqrt(_segsum(kk * kk)), 1e-12)
    k_mod = k * (1.0 + (a_ic - 1.0) * k_a)
    return jnp.concatenate([r, lw, k_mod, v, -kk, kk * a_ic, g], axis=1)


def _post(y, r, k, v, g, lnx_gain, lnx_bias, r_k):
    mu = _segsum(y) * (1.0 / HD)
    yc = y - mu
    var = _segsum(yc * yc) * (1.0 / HD)
    yn = yc * lax.rsqrt(var + GN_EPS) * lnx_gain + lnx_bias
    bonus = _segsum(r * k * r_k) * v
    return (yn + bonus) * g


def _merge(pg, ma, mb, bias):
    gates = _sigmoid(pg + bias)
    return gates[:, :D] * ma + gates[:, D:] * mb


def _swiglu(u, v):
    return u * _sigmoid(u) * v


@functools.partial(jax.custom_vjp, nondiff_argnums=(1,))
def _lane_roll(x, s):
    return pltpu.roll(x, s, 1)


def _lane_roll_fwd(x, s):
    return pltpu.roll(x, s, 1), None


def _lane_roll_bwd(s, _, g):
    n = g.shape[1]
    return (pltpu.roll(g, (n - s) % n, 1),)


_lane_roll.defvjp(_lane_roll_fwd, _lane_roll_bwd)


def _rope(x, cos, sin_lo, sin_hi):
    n = x.shape[1]
    return x * cos + _lane_roll(x, n - 8) * sin_lo + _lane_roll(x, 8) * sin_hi


def _head_rms(x, gain):
    return x * lax.rsqrt(_segsum(x * x) * (1.0 / HD) + RMS_EPS) * gain


def _attn_block(qkv_c, qkv_p, tab_c, tab_p, qg, kg, sinks, first):
    def tabs(tab, n):
        return tab[:, 0:n], tab[:, RW:RW + n], tab[:, 2 * RW:2 * RW + n]

    qg = jnp.concatenate([qg] * NH, axis=1)
    kg = jnp.concatenate([kg] * 2, axis=1)
    q = _rope(_head_rms(qkv_c[:, :RW], qg), *tabs(tab_c, RW))
    k_c = _rope(_head_rms(qkv_c[:, RW:RW + 128], kg), *tabs(tab_c, 128))
    k_p = _rope(_head_rms(qkv_p[:, RW:RW + 128], kg), *tabs(tab_p, 128))
    kband = jnp.concatenate([k_p, k_c], axis=0)
    vband = jnp.concatenate([qkv_p[:, RW + 128:], qkv_c[:, RW + 128:]], axis=0)
    G = 4
    qi = lax.broadcasted_iota(jnp.int32, (G * BLK, 2 * BLK), 0) % BLK
    kj = lax.broadcasted_iota(jnp.int32, (G * BLK, 2 * BLK), 1)
    dist = qi + BLK - kj
    valid = (dist >= 0) & (dist < BLK) & (jnp.logical_not(first) | (kj >= BLK))
    row_g = lax.broadcasted_iota(jnp.int32, (G * BLK, 1), 0) // BLK
    outs = []
    for kvh in range(2):
        kb = kband[:, kvh * HD:(kvh + 1) * HD]
        vb = vband[:, kvh * HD:(kvh + 1) * HD]
        qs = jnp.concatenate([q[:, (G * kvh + g) * HD:(G * kvh + g + 1) * HD] for g in range(G)], axis=0)
        s = _bdot_nt(qs, kb) * (HD ** -0.5)
        s = jnp.where(valid, s, NEG_INF)
        sink = jnp.zeros((G * BLK, 1), F32)
        for g in range(G):
            sink = jnp.where(row_g == g, sinks[:, G * kvh + g:G * kvh + g + 1], sink)
        m = lax.stop_gradient(jnp.maximum(jnp.max(s, axis=-1, keepdims=True), sink))
        e = jnp.exp(s - m)
        p = e / (jnp.sum(e, axis=-1, keepdims=True) + jnp.exp(sink - m))
        o = _bdot(p, vb)
        outs += [o[g * BLK:(g + 1) * BLK] for g in range(G)]
    return jnp.concatenate(outs, axis=1)


def _heads(x):
    return jnp.stack([x[:, h * HD:(h + 1) * HD] for h in range(NH)], axis=0)


def _unheads(x):
    return jnp.concatenate([x[h] for h in range(NH)], axis=1)


def _split(x, n):
    parts, rest = [], x
    for _ in range(n):
        p = rest.astype(MXU)
        parts.append(p)
        rest = rest - p.astype(F32)
    return parts


def _bdot_batched(a, b, ca, cb):
    return lax.dot_general(a, b, (((ca,), (cb,)), ((0,), (0,))), preferred_element_type=F32)


def _bmm_passes(a, b, ca, cb, passes):
    if MXU == F32:
        return lax.dot_general(a, b, (((ca,), (cb,)), ((0,), (0,))), precision=HI, preferred_element_type=F32)
    if passes == 1:
        return _bdot_batched(a.astype(MXU), b.astype(MXU), ca, cb)
    (a0, a1), (b0, b1) = _split(a, 2), _split(b, 2)
    return _bdot_batched(a0, b0, ca, cb) + (_bdot_batched(a0, b1, ca, cb) + _bdot_batched(a1, b0, ca, cb))


@functools.partial(jax.custom_vjp, nondiff_argnums=(2, 3, 4))
def _bmm(a, b, ca, cb, passes=1):
    return _bmm_passes(a, b, ca, cb, passes)


def _bmm_fwd(a, b, ca, cb, passes):
    return _bmm_passes(a, b, ca, cb, passes), (a, b)


def _bmm_bwd(ca, cb, passes, res, g):
    a, b = res
    if (ca, cb) == (2, 1):
        return _bmm_passes(g, b, 2, 2, passes), _bmm_passes(a, g, 1, 1, passes)
    if (ca, cb) == (2, 2):
        return _bmm_passes(g, b, 2, 1, passes), _bmm_passes(g, a, 1, 1, passes)
    return _bmm_passes(b, g, 2, 2, passes), _bmm_passes(a, g, 2, 1, passes)


_bmm.defvjp(_bmm_fwd, _bmm_bwd)


def _tri_dot(x, transpose):
    C = x.shape[1]
    ri = lax.broadcasted_iota(jnp.int32, (C, C), 0)
    ci = lax.broadcasted_iota(jnp.int32, (C, C), 1)
    tri = jnp.broadcast_to(((ri <= ci) if transpose else (ri >= ci)).astype(MXU), (x.shape[0], C, C))
    if MXU == F32:
        return lax.dot_general(tri, x, (((2,), (1,)), ((0,), (0,))), precision=HI, preferred_element_type=F32)
    p0, p1, p2 = _split(x, 3)
    return _bdot_batched(tri, p0, 2, 1) + (_bdot_batched(tri, p1, 2, 1) + _bdot_batched(tri, p2, 2, 1))


@jax.custom_vjp
def _cumsum_rows(x):
    return _tri_dot(x, False)


def _cumsum_rows_fwd(x):
    return _tri_dot(x, False), None


def _cumsum_rows_bwd(_, g):
    return (_tri_dot(g, True),)


_cumsum_rows.defvjp(_cumsum_rows_fwd, _cumsum_rows_bwd)

P_SCORE = 1
P_SOLVE = 1
P_STATE = 1
SCAN_CHUNKS = (4, 2, 1)


def _chunk(S0, r, lw, k, v, a, b):
    C = CHUNK
    n = r.shape[1] // C
    fold = lambda t: t.reshape(NH * n, C, HD)
    r, lw, k, v, a, b = (fold(t) for t in (r, lw, k, v, a, b))
    ri = lax.broadcasted_iota(jnp.int32, (C, C), 0)
    ci = lax.broadcasted_iota(jnp.int32, (C, C), 1)
    incl = (ri >= ci)
    strict = (ri > ci)
    eye = (ri == ci).astype(F32)
    cum = _cumsum_rows(lw)
    p_in = jnp.exp(cum)
    p_ex = jnp.exp(cum - lw)
    p_inv = jnp.exp(-cum)
    at, rt, bt, kt = a * p_ex, r * p_in, b * p_inv, k * p_inv
    lhs = jnp.concatenate([at, rt], axis=1)
    rhs_ = jnp.concatenate([bt, kt], axis=1)
    sc = _bmm(lhs, rhs_, 2, 2, P_SCORE)
    a_ab = jnp.where(strict, sc[:, :C, :C], 0.0)
    a_ak = jnp.where(strict, sc[:, :C, C:], 0.0)
    incl2 = (lax.broadcasted_iota(jnp.int32, (C, 2 * C), 0) >= lax.broadcasted_iota(jnp.int32, (C, 2 * C), 1) % C)
    a_r = jnp.where(incl2, sc[:, C:, :], 0.0)
    av = _bmm(a_ak, v, 2, 1, P_SCORE)
    x = eye + a_ab
    lp = a_ab
    for _ in range(int(math.log2(C)) - 1):
        lp = _bmm(lp, lp, 2, 1, P_SOLVE)
        x = x + _bmm(x, lp, 2, 1, P_SOLVE)
    p_last = jnp.exp(cum[:, C - 1:C, :])
    per_chunk = lambda t: t.reshape((NH, n) + t.shape[1:])
    lhs, rhs_, a_r, av, x, v, p_last = (per_chunk(t) for t in (lhs, rhs_, a_r, av, x, v, p_last))
    S, ys = S0, []
    for c in range(n):
        s0 = _bmm(lhs[:, c], S, 2, 2, P_STATE)
        u = _bmm(x[:, c], s0[:, :C] + av[:, c], 2, 1, P_SOLVE)
        uv = jnp.concatenate([u, v[:, c]], axis=1)
        ys.append(s0[:, C:] + _bmm(a_r[:, c], uv, 2, 1, P_SCORE))
        S = (S + _bmm(uv, rhs_[:, c], 1, 1, P_STATE)) * p_last[:, c]
    return jnp.concatenate(ys, axis=1), S


def _hosting(body, ex, n_in, n_out, n_scratch, n_steps):
    if ex is None:
        return body

    def wrapped(*refs):
        a = n_in
        b = a + ex.n_in
        c = b + n_out
        d = c + ex.n_out
        e = d + n_scratch
        ex_refs = (refs[a:b], refs[c:d], refs[e:])
        grid = n_steps if isinstance(n_steps, tuple) else (n_steps,)
        first = last = True
        for ax_, size in enumerate(grid):
            first = first & (pl.program_id(ax_) == 0)
            last = last & (pl.program_id(ax_) == size - 1)

        @pl.when(first)
        def _():
            ex.start(*ex_refs)

        body(*refs[:a], *refs[b:c], *refs[d:e])

        @pl.when(last)
        def _():
            ex.wait(*ex_refs)

    return wrapped


def _hosted_args(ex):
    if ex is None:
        return [], [], [], [], []
    any_spec = pl.BlockSpec(memory_space=pl.ANY)
    return list(ex.arrays), [any_spec] * ex.n_in, [any_spec] * ex.n_out, list(ex.out_shape), list(ex.scratch)


def _scan_fwd(rw, *, name, hosted=None):
    T = rw.shape[0]
    rows = CHUNK * _pick(T // CHUNK, SCAN_CHUNKS)
    n = T // rows
    h_in, h_in_specs, h_out_specs, h_out_shape, h_scratch = _hosted_args(hosted)

    def body(r_ref, lw_ref, k_ref, v_ref, a_ref, b_ref, y_ref, ck_ref, s_ref):
        @pl.when(pl.program_id(0) == 0)
        def _():
            s_ref[...] = jnp.zeros_like(s_ref)

        S0 = s_ref[...]
        ck_ref[0] = S0
        y, S1 = _chunk(S0, *[_heads(ref[...]) for ref in (r_ref, lw_ref, k_ref, v_ref, a_ref, b_ref)])
        y_ref[...] = _unheads(y)
        s_ref[...] = S1

    col = lambda j: pl.BlockSpec((rows, RW), lambda i: (i, j))
    return pl.pallas_call(
        _hosting(body, hosted, 6, 2, 1, n), name=name, grid=(n,),
        in_specs=[col(j) for j in range(6)] + h_in_specs,
        out_specs=[pl.BlockSpec((rows, RW), lambda i: (i, 0)),
                   pl.BlockSpec((1, NH, HD, HD), lambda i: (i, 0, 0, 0))] + h_out_specs,
        out_shape=[jax.ShapeDtypeStruct((T, RW), F32), jax.ShapeDtypeStruct((n, NH, HD, HD), F32)] + h_out_shape,
        scratch_shapes=[pltpu.VMEM((NH, HD, HD), F32)] + h_scratch,
        compiler_params=_cparams(("arbitrary",)),
    )(rw, rw, rw, rw, rw, rw, *h_in)


def _scan_bwd(rw, ck, dy, *, name, hosted=None):
    T = rw.shape[0]
    rows = CHUNK * _pick(T // CHUNK, SCAN_CHUNKS)
    n = T // rows

    def body(r_ref, lw_ref, k_ref, v_ref, a_ref, b_ref, ck_ref, dy_ref, o_ref, ds_ref):
        @pl.when(pl.program_id(0) == 0)
        def _():
            ds_ref[...] = jnp.zeros_like(ds_ref)

        prim = [_heads(ref[...]) for ref in (r_ref, lw_ref, k_ref, v_ref, a_ref, b_ref)]
        _, vjp = jax.vjp(_chunk, ck_ref[0], *prim)
        grads = vjp((_heads(dy_ref[...]), ds_ref[...]))
        ds_ref[...] = grads[0]
        o_ref[...] = jnp.concatenate([_unheads(g) for g in grads[1:]], axis=1)

    h_in, h_in_specs, h_out_specs, h_out_shape, h_scratch = _hosted_args(hosted)
    col = lambda j: pl.BlockSpec((rows, RW), lambda i: (n - 1 - i, j))
    return pl.pallas_call(
        _hosting(body, hosted, 8, 1, 1, n), name=name, grid=(n,),
        in_specs=[col(j) for j in range(6)] + [pl.BlockSpec((1, NH, HD, HD), lambda i: (n - 1 - i, 0, 0, 0)),
                                               pl.BlockSpec((rows, RW), lambda i: (n - 1 - i, 0))] + h_in_specs,
        out_specs=[pl.BlockSpec((rows, 6 * RW), lambda i: (n - 1 - i, 0))] + h_out_specs,
        out_shape=[jax.ShapeDtypeStruct((T, 6 * RW), F32)] + h_out_shape,
        scratch_shapes=[pltpu.VMEM((NH, HD, HD), F32)] + h_scratch,
        compiler_params=_cparams(("arbitrary",)),
    )(rw, rw, rw, rw, rw, rw, ck, dy, *h_in)


ATTN_BLOCKS = (4, 2, 1)

def _attn_fwd(qkv, tab, qg, kg, sinks, *, name, hosted=None):
    T = qkv.shape[0]
    nb = _pick(T // BLK, ATTN_BLOCKS)
    n = T // (BLK * nb)
    h_in, h_in_specs, h_out_specs, h_out_shape, h_scratch = _hosted_args(hosted)

    def body(c_ref, p_ref, tc_ref, tp_ref, qg_ref, kg_ref, s_ref, o_ref):
        for b in range(nb):
            rows = slice(b * BLK, (b + 1) * BLK)
            before = slice((b - 1) * BLK, b * BLK)
            p, tp = (p_ref[...], tp_ref[...]) if b == 0 else (c_ref[before, :], tc_ref[before, :])
            first = (pl.program_id(0) == 0) if b == 0 else False
            o_ref[rows, :] = _attn_block(c_ref[rows, :], p, tc_ref[rows, :], tp, qg_ref[...], kg_ref[...],
                                         s_ref[...], first).astype(o_ref.dtype)

    cur = lambda w: pl.BlockSpec((nb * BLK, w), lambda i: (i, 0))
    prev = lambda w: pl.BlockSpec((BLK, w), lambda i: (jnp.maximum(i * nb - 1, 0), 0))
    return pl.pallas_call(
        _hosting(body, hosted, 7, 1, 0, n), name=name, grid=(n,),
        in_specs=[cur(QKV_W), prev(QKV_W), cur(3 * RW), prev(3 * RW), _full_spec(qg), _full_spec(kg),
                  _full_spec(sinks)] + h_in_specs,
        out_specs=[cur(RW)] + h_out_specs, out_shape=[jax.ShapeDtypeStruct((T, RW), MXU)] + h_out_shape,
        scratch_shapes=h_scratch,
        compiler_params=_cparams(("arbitrary",)),
    )(qkv, qkv, tab, tab, qg, kg, sinks, *h_in)


def _attn_bwd(qkv, tab, qg, kg, sinks, dy, *, name, hosted=None):
    T = qkv.shape[0]
    nb = _pick(T // BLK, ATTN_BLOCKS)
    n = T // (BLK * nb)
    h_in, h_in_specs, h_out_specs, h_out_shape, h_scratch = _hosted_args(hosted)

    def body(c_ref, p_ref, tc_ref, tp_ref, qg_ref, kg_ref, s_ref, dy_ref, dqkv_ref, dqg_ref, dkg_ref, ds_ref, carry_ref):
        i = pl.program_id(0)

        @pl.when(i == 0)
        def _():
            carry_ref[...] = jnp.zeros_like(carry_ref)
            dqg_ref[...] = jnp.zeros_like(dqg_ref)
            dkg_ref[...] = jnp.zeros_like(dkg_ref)
            ds_ref[...] = jnp.zeros_like(ds_ref)

        carry = carry_ref[...]
        dqg_t, dkg_t, ds_t = jnp.zeros_like(dqg_ref), jnp.zeros_like(dkg_ref), jnp.zeros_like(ds_ref)
        for b in reversed(range(nb)):
            rows = slice(b * BLK, (b + 1) * BLK)
            before = slice((b - 1) * BLK, b * BLK)
            tc = tc_ref[rows, :]
            p, tp = (p_ref[...], tp_ref[...]) if b == 0 else (c_ref[before, :], tc_ref[before, :])
            first = (i == n - 1) if b == 0 else False
            f = lambda c, p_, qg_, kg_, sk, tc=tc, tp=tp, first=first: _attn_block(c, p_, tc, tp, qg_, kg_, sk, first)
            _, vjp = jax.vjp(f, c_ref[rows, :], p, qg_ref[...], kg_ref[...], s_ref[...])
            dc, dp, dqg, dkg, dsk = vjp(dy_ref[rows, :].astype(F32))
            dqkv_ref[rows, :] = (dc + carry).astype(dqkv_ref.dtype)
            carry = dp
            dqg_t, dkg_t, ds_t = dqg_t + dqg, dkg_t + dkg, ds_t + dsk
        carry_ref[...] = carry
        dqg_ref[...] += dqg_t
        dkg_ref[...] += dkg_t
        ds_ref[...] += ds_t

    cur = lambda w: pl.BlockSpec((nb * BLK, w), lambda i: (n - 1 - i, 0))
    prev = lambda w: pl.BlockSpec((BLK, w), lambda i: (jnp.maximum((n - 1 - i) * nb - 1, 0), 0))
    return pl.pallas_call(
        _hosting(body, hosted, 8, 4, 1, n), name=name, grid=(n,),
        in_specs=[cur(QKV_W), prev(QKV_W), cur(3 * RW), prev(3 * RW), _full_spec(qg), _full_spec(kg), _full_spec(sinks),
                  cur(RW)] + h_in_specs,
        out_specs=[cur(QKV_W), _full_spec(qg), _full_spec(kg), _full_spec(sinks)] + h_out_specs,
        out_shape=[jax.ShapeDtypeStruct((T, QKV_W), MXU), jax.ShapeDtypeStruct(qg.shape, F32),
                   jax.ShapeDtypeStruct(kg.shape, F32), jax.ShapeDtypeStruct(sinks.shape, F32)] + h_out_shape,
        scratch_shapes=[pltpu.VMEM((BLK, QKV_W), F32)] + h_scratch,
        compiler_params=_cparams(("arbitrary",)),
    )(qkv, qkv, tab, tab, qg, kg, sinks, dy, *h_in)


def _shift_down(cur, prev8, i):
    rolled = pltpu.roll(cur, 1, 0)
    first_row = jnp.where(i > 0, prev8[7:8, :], 0.0)
    row = lax.broadcasted_iota(jnp.int32, cur.shape, 0)
    return jnp.where(row == 0, first_row, rolled)


def _shift_up(cur, next8, i, n):
    tm = cur.shape[0]
    rolled = pltpu.roll(cur, tm - 1, 0)
    last_row = jnp.where(i < n - 1, next8[0:1, :], 0.0)
    row = lax.broadcasted_iota(jnp.int32, cur.shape, 0)
    return jnp.where(row == tm - 1, last_row, rolled)


def _ada_parts(ada):
    return [ada[:, j * D:(j + 1) * D] for j in range(6)]


def _rope_table(positions):
    half = HD // 8
    inv_freq = 500000.0 ** (-jnp.arange(half, dtype=F32) / half)
    ang = positions.astype(F32)[:, None] * inv_freq
    cos, sin = jnp.cos(ang), jnp.sin(ang)
    T = positions.shape[0]
    pad = HD - 2 * half
    c64 = jnp.concatenate([cos, cos, jnp.ones((T, pad), F32)], axis=1)
    lo64 = jnp.concatenate([-sin, jnp.zeros((T, HD - half), F32)], axis=1)
    hi64 = jnp.concatenate([jnp.zeros((T, half), F32), sin, jnp.zeros((T, pad), F32)], axis=1)
    return jnp.concatenate([jnp.tile(t, (1, NH)) for t in (c64, lo64, hi64)], axis=1)


GATHER_BEHIND = {"f_proj_shift": ['w_out'], "f_proj_gates": ['w_branch_a', 'w_branch_b'], "f_prep": ['ffn_w2_lo'],
                 "f_scan": ['ffn_w1'], "f_post": ['ffn_w2_hi'], "f_attn": ['ffn_w3']}
LATE = ['w_out', 'w_branch_a', 'w_branch_b', 'ffn_w1', 'ffn_w3', 'ffn_w2']
BACK_ATTN = ['w_out', 'w_branch_a', 'w_branch_b']
BACK_SCAN = ['ffn_w1', 'ffn_w3', 'ffn_w2']
BACK_LAST = ['w_in', 'decay_up', 'iclr_up', 'gate_up']


def _full_weight(g, ax):
    return g.reshape(-1, g.shape[2]) if ax == 0 else jnp.concatenate([g[j] for j in range(4)], axis=1)


def _local_step(x, target, ada, tab, w, s, shards=None):
    T = x.shape[0]
    tm = _pick(T, (512, 256, 128))
    tm_wide = _pick(T, (256, 128))
    tm_vjp = _pick(T, (128,))
    row = lambda n, dt=F32: (n, dt, 'row')
    acc = lambda n, r=1: (n, F32, r)

    def f_norm1(i, n, x_, g, ada_):
        sh, sc = ada_[:, 0:D], ada_[:, D:2 * D]
        return (_norm_mod(x_, g, sc, sh),)
    (h1,) = _rowwise(f_norm1, [x], [s['norm1_gain'], ada], [row(D, MXU)], tm=tm, name="f_norm1")

    pieces, late = {}, {}
    if shards:
        half = shards['ffn_w2'].shape[0] // 2
        pieces = dict(shards, ffn_w2_lo=shards['ffn_w2'][:half], ffn_w2_hi=shards['ffn_w2'][half:])

    def behind(kernel_name):
        return _GatherChips([pieces[n] for n in GATHER_BEHIND[kernel_name]]) if shards else None

    def took(kernel_name, got):
        late.update(zip(GATHER_BEHIND[kernel_name], got))

    def mm_behind(a_, w_, kernel_name, **kw):
        ex = behind(kernel_name)
        res = _mm_nn(a_, w_, name=kernel_name, hosted=ex, **kw)
        if ex:
            took(kernel_name, res[1:])
            return res[0]
        return res

    proj = mm_behind(h1, w['w_in'][:, :SHIFT_W], "f_proj_shift")
    proj_qkv = _mm_nn(h1, w['w_in'][:, SHIFT_W:SHIFT_W + QKV_W], name="f_proj_qkv")
    proj_g = mm_behind(h1, w['w_in'][:, SHIFT_W + QKV_W:], "f_proj_gates", out_dtype=MXU)
    prep_consts = [s['decay_w0'], s['lora_up'], s['iclr_a0'], s['gate_up'], s['k_k'], s['k_a']]

    def f_prep(i, n, cur, prev8, mu, *params):
        mixed = cur + (_shift_down(cur, prev8, i) - cur) * mu
        return (_prep(mixed, *params),)
    rw, *got = _rowwise(f_prep, [(proj, SHIFT_W)], [s['tshift_mu']] + prep_consts, [row(7 * RW)], tm=tm_wide,
                        name="f_prep", halo=[(proj, SHIFT_W, 'prev')], hosted=behind("f_prep"))
    took("f_prep", got)
    y, ck, *got = _scan_fwd(rw, name="f_scan", hosted=behind("f_scan"))
    took("f_scan", got)
    post_consts = [s['lnx_gain'], s['lnx_bias'], s['r_k']]

    def post_of(y_, rw_, *params):
        return _post(y_, rw_[:, 0:RW], rw_[:, 2 * RW:3 * RW], rw_[:, 3 * RW:4 * RW], rw_[:, 6 * RW:7 * RW], *params)

    def f_post(i, n, y_, rw_, *params):
        return (post_of(y_, rw_, *params),)
    ya, *got = _rowwise(f_post, [y, rw], post_consts, [row(RW, MXU)], tm=tm_wide, name="f_post", hosted=behind("f_post"))
    took("f_post", got)
    yb, *got = _attn_fwd(proj_qkv, tab, s['q_norm_gain'], s['k_norm_gain'], s['attn_sinks'], name="f_attn",
                         hosted=behind("f_attn"))
    took("f_attn", got)
    w = dict(w)
    if shards:
        ax = dict(SHARDED)
        w.update({n: _full_weight(late[n], ax[n]) for n in ('w_branch_a', 'w_branch_b', 'w_out')})
        w['ffn_w13'] = jnp.concatenate([_full_weight(late[n], 1) for n in ('ffn_w1', 'ffn_w3')], axis=1)
        w['ffn_w2'] = jnp.concatenate([late['ffn_w2_lo'], late['ffn_w2_hi']], axis=1).reshape(-1, D)
    ma = _mm_nn(ya, w['w_branch_a'], name="f_branch_a", out_dtype=MXU)
    mb = _mm_nn(yb, w['w_branch_b'], name="f_branch_b", out_dtype=MXU)

    def f_merge(i, n, pg, ma_, mb_, bias):
        return (_merge(pg.astype(F32), ma_.astype(F32), mb_.astype(F32), bias),)
    (merged,) = _rowwise(f_merge, [proj_g, ma, mb], [s['branch_gate_b']], [row(D, MXU)], tm=tm, name="f_merge")
    mo = _mm_nn(merged, w['w_out'], name="f_out")

    def f_res1(i, n, x_, mo_, g, ada_):
        x1_ = x_ + ada_[:, 2 * D:3 * D] * mo_
        return x1_, _norm_mod(x1_, g, ada_[:, 4 * D:5 * D], ada_[:, 3 * D:4 * D])
    x1, h2 = _rowwise(f_res1, [x, mo], [s['norm2_gain'], ada], [row(D), row(D, MXU)], tm=tm, name="f_res1")
    uv = _mm_nn(h2, w['ffn_w13'], name="f_ffn_in", out_dtype=MXU)

    def f_act(i, n, uv_):
        return (_swiglu(uv_[:, :DFF].astype(F32), uv_[:, DFF:].astype(F32)),)
    (act,) = _rowwise(f_act, [uv], [], [row(DFF, MXU)], tm=tm_wide, name="f_act")
    ff = _mm_nn(act, w['ffn_w2'], name="f_ffn_out")

    def f_loss(i, n, x1_, ff_, tgt, ada_):
        g2 = ada_[:, 5 * D:6 * D]
        err = x1_ + g2 * ff_ - tgt
        dx2 = err * (1.0 / D)
        loss = 0.5 * jnp.sum(jnp.sum(err * err, axis=1, keepdims=True) * (1.0 / D), axis=0, keepdims=True)
        return dx2, (dx2 * g2), jnp.broadcast_to(loss, (1, 128)), jnp.sum(dx2 * ff_, axis=0, keepdims=True)
    dx2, dff, loss, dgate2 = _rowwise(f_loss, [x1, ff, target], [ada], [row(D), row(D, MXU), acc(128), acc(D)],
                                      tm=tm, name="f_loss")

    dact = _mm_nt(dff, w['ffn_w2'], name="b_ffn_out_dx", out_dtype=MXU)
    g_w2 = _mm_tn(act, dff, name="b_ffn_out_dw", out_dtype=MXU)

    def b_act(i, n, uv_, dact_):
        _, vjp = jax.vjp(_swiglu, uv_[:, :DFF].astype(F32), uv_[:, DFF:].astype(F32))
        du, dv = vjp(dact_.astype(F32))
        return (jnp.concatenate([du, dv], axis=1),)
    (duv,) = _rowwise(b_act, [uv, dact], [], [row(2 * DFF, MXU)], tm=tm_wide, name="b_act")
    dh2 = _mm_nt(duv, w['ffn_w13'], name="b_ffn_in_dx")
    g_w13 = _mm_tn(h2, duv, name="b_ffn_in_dw", out_dtype=MXU)

    def b_res1(i, n, x1_, dh2_, dx2_, mo_, g, ada_):
        _, vjp = jax.vjp(_norm_mod, x1_, g, ada_[:, 4 * D:5 * D], ada_[:, 3 * D:4 * D])
        dxn, dg, dsc, dsh = vjp(dh2_)
        dx1_ = dxn + dx2_
        g1 = ada_[:, 2 * D:3 * D]
        return dx1_, dx1_ * g1, dg, dsc, dsh, jnp.sum(dx1_ * mo_, axis=0, keepdims=True)
    dx1, dmo, d_gain2, d_scale2, d_shift2, dgate1 = _rowwise(
        b_res1, [x1, dh2, dx2, mo], [s['norm2_gain'], ada], [row(D), row(D, MXU), acc(D), acc(D), acc(D), acc(D)],
        tm=tm, name="b_res1")
    dmerged = _mm_nt(dmo, w['w_out'], name="b_out_dx", out_dtype=MXU)
    g_wout = _mm_tn(merged, dmo, name="b_out_dw", out_dtype=MXU)

    def b_merge(i, n, pg, ma_, mb_, dm, bias):
        _, vjp = jax.vjp(_merge, pg.astype(F32), ma_.astype(F32), mb_.astype(F32), bias)
        dpg, dma_, dmb_, dbias = vjp(dm.astype(F32))
        return dpg, dma_, dmb_, dbias
    dpg, dma, dmb, d_bias = _rowwise(b_merge, [proj_g, ma, mb, dmerged], [s['branch_gate_b']],
                                     [row(GATE_W, MXU), row(D, MXU), row(D, MXU), acc(GATE_W)], tm=tm_wide, name="b_merge")
    dya = _mm_nt(dma, w['w_branch_a'], name="b_branch_a_dx")
    g_wa = _mm_tn(ya, dma, name="b_branch_a_dw", out_dtype=MXU, col_shards=4)
    dyb = _mm_nt(dmb, w['w_branch_b'], name="b_branch_b_dx", out_dtype=F32)
    g_wb = _mm_tn(yb, dmb, name="b_branch_b_dw", out_dtype=MXU, col_shards=4)
    fs = DFF // 4
    gw = dict(w_branch_a=g_wa, w_branch_b=g_wb, w_out=g_wout.reshape(4, D // 4, D),
              ffn_w1=jnp.stack([g_w13[:, j * fs:(j + 1) * fs] for j in range(4)]),
              ffn_w3=jnp.stack([g_w13[:, DFF + j * fs:DFF + (j + 1) * fs] for j in range(4)]),
              ffn_w2=g_w2.reshape(4, fs, D))
    recv = {}
    dqkv, d_qg, d_kg, d_sinks, *got = _attn_bwd(
        proj_qkv, tab, s['q_norm_gain'], s['k_norm_gain'], s['attn_sinks'], dyb, name="b_attn",
        hosted=shards and _ScatterChips([gw[n] for n in BACK_ATTN]))
    recv.update(zip(BACK_ATTN, got))

    def b_post(i, n, y_, rw_, dya_, *params):
        _, vjp = jax.vjp(post_of, y_, rw_, *params)
        return vjp(dya_)
    dy, drw_post, d_lnx_gain, d_lnx_bias, d_r_k = _rowwise(
        b_post, [y, rw, dya], post_consts, [row(RW), row(7 * RW), acc(RW), acc(RW), acc(RW)], tm=tm_vjp, name="b_post")
    dscan, *got = _scan_bwd(rw, ck, dy, name="b_scan",
                            hosted=shards and _ScatterChips([gw[n] for n in BACK_SCAN]))
    recv.update(zip(BACK_SCAN, got))

    def b_prep(i, n, cur, drw_, dscan_, prev8, mu, *params):
        shifted = _shift_down(cur, prev8, i)
        mixed = cur + (shifted - cur) * mu
        _, vjp = jax.vjp(_prep, mixed, *params)
        ct = drw_ + jnp.concatenate([dscan_, jnp.zeros((dscan_.shape[0], RW), F32)], axis=1)
        grads = vjp(ct)
        dmixed = grads[0]
        return (dmixed, jnp.sum(dmixed * (shifted - cur), axis=0, keepdims=True)) + tuple(grads[1:])
    dmixed, d_mu, d_w0, d_lora, d_a0, d_gate_up, d_kk, d_ka = _rowwise(
        b_prep, [(proj, SHIFT_W), drw_post, dscan], [s['tshift_mu']] + prep_consts,
        [row(SHIFT_W), acc(SHIFT_W), acc(RW), acc(2 * RW, 128), acc(RW), acc(RW, 128), acc(RW), acc(RW)],
        tm=tm_vjp, name="b_prep", halo=[(proj, SHIFT_W, 'prev')])

    def b_gather(i, n, dm, dqkv_, dpg_, next8, mu):
        dcur = dm * (1.0 - mu) + _shift_up(dm, next8, i, n) * mu
        return (jnp.concatenate([dcur.astype(MXU), dqkv_, dpg_], axis=1),)
    (dproj,) = _rowwise(b_gather, [dmixed, dqkv, dpg], [s['tshift_mu']], [row(IN_W, MXU)], tm=tm_wide, name="b_gather",
                        halo=[(dmixed, SHIFT_W, 'next')])
    g_win = _mm_tn(h1, dproj, name="b_proj_dw", out_dtype=MXU, col_shards=4)

    def col_blocks(g):
        k, n = g.shape
        return g.reshape(k, 4, n // 4).transpose(1, 0, 2).astype(MXU)
    gw.update(w_in=g_win, decay_up=col_blocks(d_lora[:64, :RW]), iclr_up=col_blocks(d_lora[64:, RW:]),
              gate_up=col_blocks(d_gate_up))
    if shards:
        dh1, *got = _mm_nt(dproj, w['w_in'], name="b_proj_dx", hosted=_ScatterChips([gw[n] for n in BACK_LAST]))
        recv.update(zip(BACK_LAST, got))
    else:
        dh1 = _mm_nt(dproj, w['w_in'], name="b_proj_dx")

    def b_norm1(i, n, x_, dh1_, dx1_, g, ada_):
        _, vjp = jax.vjp(_norm_mod, x_, g, ada_[:, D:2 * D], ada_[:, 0:D])
        dxn, dg, dsc, dsh = vjp(dh1_)
        return dxn + dx1_, dg, dsc, dsh
    dx, d_gain1, d_scale1, d_shift1 = _rowwise(b_norm1, [x, dh1, dx1], [s['norm1_gain'], ada],
                                               [row(D), acc(D), acc(D), acc(D)], tm=tm, name="b_norm1")

    d_ada = jnp.concatenate([d_shift1, d_scale1, dgate1, d_shift2, d_scale2, dgate2], axis=1)
    gs = dict(norm1_gain=d_gain1, norm2_gain=d_gain2, tshift_mu=d_mu, decay_w0=d_w0, iclr_a0=d_a0, k_k=d_kk, k_a=d_ka,
              r_k=d_r_k, lnx_gain=d_lnx_gain, lnx_bias=d_lnx_bias, q_norm_gain=d_qg, k_norm_gain=d_kg,
              attn_sinks=d_sinks, branch_gate_b=d_bias)
    return loss, dx, d_ada, gw, gs, recv


ANY = pl.BlockSpec(memory_space=pl.ANY)


def _place():
    x, y, c = lax.axis_index("x"), lax.axis_index("y"), lax.axis_index("c")
    return x, y, c, [(1 - x, y), (x, 1 - y), (1 - x, 1 - y)]


def _all_gather8(x_shard, *, name):
    m_per, n = x_shard.shape

    def body(x_ref, out_ref, send_sems, recv_sems, local_sem):
        x, y, c, chips = _place()
        me, sibling = (x, y, c), (x, y, 1 - c)

        def rows(px, py, pc):
            return out_ref.at[pl.ds((4 * px + 2 * py + pc) * m_per, m_per), :]

        def copy(k, block, to, src=None):
            return pltpu.make_async_remote_copy(
                src_ref=rows(*block) if src is None else src, dst_ref=rows(*block),
                send_sem=send_sems.at[k], recv_sem=recv_sems.at[k], device_id=to, device_id_type=MESH)

        mine = pltpu.make_async_copy(x_ref, rows(*me), local_sem)
        mine.start()
        first = [copy(0, me, sibling, src=x_ref)]
        first += [copy(1 + j, me, (*chip, c), src=x_ref) for j, chip in enumerate(chips)]
        for cp in first:
            cp.start()
        passed = [copy(4 + j, (*chip, c), sibling) for j, chip in enumerate(chips)]
        for j, chip in enumerate(chips):
            copy(1 + j, (*chip, c), me).wait_recv()
            passed[j].start()
        copy(0, sibling, me).wait_recv()
        for j, chip in enumerate(chips):
            copy(4 + j, (*chip, 1 - c), me).wait_recv()
        for cp in first + passed:
            cp.wait_send()
        mine.wait()

    return pl.pallas_call(
        body, name=name, out_shape=jax.ShapeDtypeStruct((8 * m_per, n), x_shard.dtype),
        in_specs=[pl.BlockSpec(memory_space=pltpu.VMEM)], out_specs=pl.BlockSpec(memory_space=pltpu.VMEM),
        scratch_shapes=[pltpu.SemaphoreType.DMA((7,)), pltpu.SemaphoreType.DMA((7,)), pltpu.SemaphoreType.DMA],
    )(x_shard)


class _GatherChips:
    def __init__(self, shards):
        n = len(shards)
        self.arrays, self.n_in, self.n_out = list(shards), n, n
        self.out_shape = [jax.ShapeDtypeStruct((4,) + s.shape, s.dtype) for s in shards]
        self.scratch = [pltpu.SemaphoreType.DMA((3 * n,)), pltpu.SemaphoreType.DMA((3 * n,)),
                        pltpu.SemaphoreType.DMA((n,))]

    def _copies(self, x_refs, out_refs, sems, receiving):
        send_sems, recv_sems, local_sems = sems
        x, y, c, chips = _place()
        s_me = 2 * x + y
        n = self.n_in

        def copy(a, k, s):
            return pltpu.make_async_remote_copy(
                src_ref=x_refs[a], dst_ref=out_refs[a].at[s], send_sem=send_sems.at[3 * a + k],
                recv_sem=recv_sems.at[3 * a + k], device_id=(*chips[k], c), device_id_type=MESH)

        mine = [pltpu.make_async_copy(x_refs[a], out_refs[a].at[s_me], local_sems.at[a]) for a in range(n)]
        sends = [copy(a, k, s_me) for a in range(n) for k in range(3)]
        if not receiving:
            return mine, sends
        return mine, sends, [copy(a, k, 2 * px + py) for a in range(n) for k, (px, py) in enumerate(chips)]

    def start(self, x_refs, out_refs, sems):
        mine, sends = self._copies(x_refs, out_refs, sems, False)
        for cp in mine + sends:
            cp.start()

    def wait(self, x_refs, out_refs, sems):
        mine, sends, recvs = self._copies(x_refs, out_refs, sems, True)
        for cp in recvs:
            cp.wait_recv()
        for cp in sends:
            cp.wait_send()
        for cp in mine:
            cp.wait()


class _ScatterChips:
    def __init__(self, parts):
        n = len(parts)
        self.arrays, self.n_in, self.n_out = list(parts), n, n
        self.out_shape = [jax.ShapeDtypeStruct((3,) + p.shape[1:], p.dtype) for p in parts]
        self.scratch = [pltpu.SemaphoreType.DMA((3 * n,)), pltpu.SemaphoreType.DMA((3 * n,))]

    def _copies(self, g_refs, out_refs, sems):
        send_sems, recv_sems = sems
        x, y, c, chips = _place()
        return [pltpu.make_async_remote_copy(
            src_ref=g_refs[a].at[2 * px + py], dst_ref=out_refs[a].at[k], send_sem=send_sems.at[3 * a + k],
            recv_sem=recv_sems.at[3 * a + k], device_id=(px, py, c), device_id_type=MESH)
            for a in range(self.n_in) for k, (px, py) in enumerate(chips)]

    def start(self, g_refs, out_refs, sems):
        for cp in self._copies(g_refs, out_refs, sems):
            cp.start()

    def wait(self, g_refs, out_refs, sems):
        sends = self._copies(g_refs, out_refs, sems)
        for cp in sends:
            cp.wait_recv()
        for cp in sends:
            cp.wait_send()


def _exchange_call(ex, *, name):
    def body(*refs):
        parts = (refs[:ex.n_in], refs[ex.n_in:ex.n_in + ex.n_out], refs[ex.n_in + ex.n_out:])
        ex.start(*parts)
        ex.wait(*parts)

    return pl.pallas_call(body, name=name, out_shape=ex.out_shape, in_specs=[ANY] * ex.n_in,
                          out_specs=[ANY] * ex.n_out, scratch_shapes=ex.scratch)(*ex.arrays)


def _swap_sibling(vs, *, name):
    n = len(vs)

    def body(*refs):
        v_refs, out_refs = refs[:n], refs[n:2 * n]
        send_sems, recv_sems = refs[2 * n:]
        x, y, c, _ = _place()
        cps = [pltpu.make_async_remote_copy(src_ref=v_refs[a], dst_ref=out_refs[a], send_sem=send_sems.at[a],
                                            recv_sem=recv_sems.at[a], device_id=(x, y, 1 - c), device_id_type=MESH)
               for a in range(n)]
        for cp in cps:
            cp.start()
        for cp in cps:
            cp.wait()

    return pl.pallas_call(
        body, name=name, out_shape=[jax.ShapeDtypeStruct(v.shape, v.dtype) for v in vs],
        in_specs=[ANY] * n, out_specs=[ANY] * n,
        scratch_shapes=[pltpu.SemaphoreType.DMA((n,)), pltpu.SemaphoreType.DMA((n,))],
    )(*vs)


def _sum_parts(own, others, *, name):
    R, C = own.shape
    tm = _pick(R, (256, 128, 64))

    def body(own_ref, o0_ref, o1_ref, o2_ref, out_ref):
        tot = own_ref[...].astype(F32)
        for ref in (o0_ref, o1_ref, o2_ref):
            tot = tot + ref[...].astype(F32)
        out_ref[...] = tot

    part = lambda k: pl.BlockSpec((None, tm, C), lambda i: (k, i, 0))
    return pl.pallas_call(
        body, name=name, grid=(R // tm,),
        in_specs=[pl.BlockSpec((tm, C), lambda i: (i, 0)), part(0), part(1), part(2)],
        out_specs=pl.BlockSpec((tm, C), lambda i: (i, 0)), out_shape=jax.ShapeDtypeStruct((R, C), F32),
        compiler_params=_cparams(("arbitrary",)),
    )(own, others, others, others)


def _adam_math(w_, m_, v_, g):
    m2 = ADAM_B1 * m_ + (1.0 - ADAM_B1) * g
    v2 = ADAM_B2 * v_ + (1.0 - ADAM_B2) * jnp.square(g)
    m_hat = m2 / (1.0 - ADAM_B1 ** ADAM_STEP)
    v_hat = v2 / (1.0 - ADAM_B2 ** ADAM_STEP)
    delta = -ADAM_LR * (m_hat / (jnp.sqrt(v_hat) + ADAM_EPS) + ADAM_WD * w_)
    return delta, m2, v2


SMALL_SLOTS = 16
SMALL_COLS = 6 * D


def _pack_small(grads, *, name):
    n = len(grads)

    def body(*refs):
        out_ref = refs[n]
        out_ref[...] = jnp.zeros_like(out_ref)
        for i, ref in enumerate(refs[:n]):
            out_ref[i:i + 1, 0:ref.shape[1]] = ref[...]

    return pl.pallas_call(body, name=name, out_shape=jax.ShapeDtypeStruct((SMALL_SLOTS, SMALL_COLS), F32))(*grads)


def _adamw_small(ws, ms, vs, gathered, *, name):
    n = len(ws)

    def body(*refs):
        w_refs, m_refs, v_refs, g_ref = refs[:n], refs[n:2 * n], refs[2 * n:3 * n], refs[3 * n]
        outs = refs[3 * n + 1:]
        for i in range(n):
            nc = w_refs[i].shape[1]
            g = g_ref[i:i + 1, 0:nc]
            for d in range(1, 8):
                g = g + g_ref[d * SMALL_SLOTS + i:d * SMALL_SLOTS + i + 1, 0:nc]
            delta, m2, v2 = _adam_math(w_refs[i][...], m_refs[i][...], v_refs[i][...], g)
            for k, val in enumerate((g, delta, m2, v2)):
                outs[k * n + i][...] = val

    shapes = [jax.ShapeDtypeStruct(w.shape, F32) for w in ws]
    res = pl.pallas_call(body, name=name, out_shape=shapes * 4,
                         compiler_params=pltpu.CompilerParams(vmem_limit_bytes=VMEM_LIMIT))(*ws, *ms, *vs, gathered)
    return [res[k * n:(k + 1) * n] for k in range(4)]


def _adamw(w, m, v, gparts, *, tm, name):
    def fn(i, n, w_, m_, v_, *gs):
        g = gs[0]
        for p in gs[1:]:
            g = g + p
        return (g,) + _adam_math(w_, m_, v_, g)
    nc = w.shape[1]
    return _rowwise(fn, [w, m, v] + list(gparts), [], [(nc, F32, 'row')] * 4, tm=tm, name=name)


WEIGHTS = ['ada_w', 'ada_b', 'norm1_gain', 'norm2_gain', 'w_in', 'tshift_mu', 'decay_w0', 'decay_up', 'iclr_a0',
           'iclr_up', 'gate_up', 'k_k', 'k_a', 'r_k', 'lnx_gain', 'lnx_bias', 'q_norm_gain', 'k_norm_gain', 'attn_sinks',
           'branch_gate_b', 'w_branch_a', 'w_branch_b', 'w_out', 'ffn_w1', 'ffn_w3', 'ffn_w2']
SHARDED = [('w_in', 1), ('decay_up', 1), ('iclr_up', 1), ('gate_up', 1), ('w_branch_a', 1), ('w_branch_b', 1),
           ('w_out', 0), ('ffn_w1', 1), ('ffn_w3', 1), ('ffn_w2', 0)]
SMALL = ['ada_b', 'norm1_gain', 'norm2_gain', 'tshift_mu', 'decay_w0', 'iclr_a0', 'k_k', 'k_a', 'r_k', 'lnx_gain',
         'lnx_bias', 'q_norm_gain', 'k_norm_gain', 'attn_sinks', 'branch_gate_b']


def kernel(x, c, positions, ada_w, ada_b, norm1_gain, norm2_gain, w_in, tshift_mu, decay_w0, decay_up, iclr_a0, iclr_up, gate_up, k_k, k_a, r_k, lnx_gain, lnx_bias, q_norm_gain, k_norm_gain, attn_sinks, branch_gate_b, w_branch_a, w_branch_b, w_out, ffn_w1, ffn_w3, ffn_w2, loss_target, m_ada_w, m_ada_b, m_norm1_gain, m_norm2_gain, m_w_in, m_tshift_mu, m_decay_w0, m_decay_up, m_iclr_a0, m_iclr_up, m_gate_up, m_k_k, m_k_a, m_r_k, m_lnx_gain, m_lnx_bias, m_q_norm_gain, m_k_norm_gain, m_attn_sinks, m_branch_gate_b, m_w_branch_a, m_w_branch_b, m_w_out, m_ffn_w1, m_ffn_w3, m_ffn_w2, v_ada_w, v_ada_b, v_norm1_gain, v_norm2_gain, v_w_in, v_tshift_mu, v_decay_w0, v_decay_up, v_iclr_a0, v_iclr_up, v_gate_up, v_k_k, v_k_a, v_r_k, v_lnx_gain, v_lnx_bias, v_q_norm_gain, v_k_norm_gain, v_attn_sinks, v_branch_gate_b, v_w_branch_a, v_w_branch_b, v_w_out, v_ffn_w1, v_ffn_w3, v_ffn_w2):
    a = dict(locals())
    W = {n: a[n] for n in WEIGHTS}
    M = {n: a['m_' + n] for n in WEIGHTS}
    V = {n: a['v_' + n] for n in WEIGHTS}
    xi, yi, ci = lax.axis_index("x"), lax.axis_index("y"), lax.axis_index("c")
    me = 4 * xi + 2 * yi + ci
    shard = 2 * xi + yi
    mat = lambda t: t.reshape(t.shape[-2], t.shape[-1])
    sharded = [n for n, _ in SHARDED]

    ax = dict(SHARDED)
    late = LATE
    early = [n for n in sharded if n not in late]
    shards = {n: mat(W[n]).astype(MXU) for n in sharded}
    gathered = _exchange_call(_GatherChips([shards[n] for n in early]), name="gather_weights")
    full = {n: _full_weight(g, ax[n]) for n, g in zip(early, gathered, strict=True)}

    c_all = _all_gather8(jnp.broadcast_to(c, (8, D)), name="gather_c")[0::8]
    pad_rows = lambda t: jnp.concatenate([t, jnp.zeros((BLK - 8, t.shape[1]), t.dtype)])
    c_all = pad_rows(c_all.astype(MXU))
    ada_cols = _mm_nn(c_all, mat(ada_w).astype(MXU), name="f_ada")[:8]
    ada_all = _all_gather8(ada_cols, name="gather_ada").reshape(2, 2, 2, 8, 6 * D // 4)
    ada_mine = lax.dynamic_index_in_dim(ada_all[:, :, 0], me, axis=2, keepdims=False)
    ada = ada_mine.reshape(1, 6 * D) + mat(ada_b)

    zero = jnp.zeros((64, RW), MXU)
    lora = jnp.concatenate([jnp.concatenate([full['decay_up'], zero], axis=1),
                            jnp.concatenate([zero, full['iclr_up']], axis=1)], axis=0)
    s = {n: W[n].reshape(1, -1) for n in SMALL if n != 'ada_b'}
    s['lora_up'] = lora.astype(F32)
    s['gate_up'] = full['gate_up'].astype(F32)
    tab = _rope_table(positions.reshape(-1))
    loss, dx, d_ada, gw, gs, from_chips = _local_step(x[0], loss_target[0], ada, tab, dict(w_in=full['w_in']), s,
                                                      shards={n: shards[n] for n in late})
    loss = lax.psum(loss[0, 0], ("x", "y", "c"))

    gs['ada_b'] = d_ada
    gsmall = _pack_small([gs[n] for n in SMALL], name="pack_small_grads")
    gsmall_all = _all_gather8(gsmall, name="gather_small_grads")
    row = lambda src: [src[n].reshape(1, -1) for n in SMALL]
    sm_out = _adamw_small(row(W), row(M), row(V), gsmall_all, name="adamw_small")
    sm_out = [{n: o.reshape(W[n].shape) for n, o in zip(SMALL, outs_k, strict=True)} for outs_k in sm_out]

    d_ada_all = gsmall_all[0::SMALL_SLOTS]
    d_ada_cols = lax.dynamic_slice_in_dim(d_ada_all, shard * (6 * D // 4), 6 * D // 4, axis=1)
    g_ada_w = _mm_tn(c_all, pad_rows(d_ada_cols.astype(MXU)), name="b_ada")
    ada_out = _adamw(mat(ada_w), mat(m_ada_w), mat(v_ada_w), [g_ada_w], tm=256, name="adamw_ada")

    parts =[_sum_parts(lax.dynamic_index_in_dim(gw[n], shard, axis=0, keepdims=False), from_chips[n], name="sum_" + n)
             for n in sharded]
    others = _swap_sibling(parts, name="swap_grads")
    sh_out = {}
    for n, part, other in zip(sharded, parts, others, strict=True):
        sh_out[n] = _adamw(mat(W[n]), mat(M[n]), mat(V[n]), [part, other], tm=_pick(part.shape[0], (256, 128, 64)),
                           name="adamw_" + n)

    def leaf(k, n):
        if n == 'ada_w':
            return ada_out[k].reshape(W[n].shape)
        if n in sharded:
            return sh_out[n][k].reshape(W[n].shape)
        return sm_out[k][n]
    outs = [leaf(k, n) for k in range(4) for n in WEIGHTS]
    return (loss, dx[None], *outs)
```

```python
import functools
import math

import jax
import jax.numpy as jnp
from jax import lax
from jax.experimental import pallas as pl
from jax.experimental.pallas import tpu as pltpu

F32 = jnp.float32
BF16 = jnp.bfloat16
MXU = BF16
HI = lax.Precision.HIGHEST

D = 1024
HD = 64
NH = 8
RW = NH * HD
SHIFT_W = 3 * RW + 64 + 64 + 128
QKV_W = RW + 2 * 128
GATE_W = 2 * D
IN_W = SHIFT_W + QKV_W + GATE_W
DFF = 2816
BLK = 128
CHUNK = 64
RMS_EPS = 1e-6
GN_EPS = 64e-5
NEG_INF = -1e30
ADAM_LR, ADAM_B1, ADAM_B2, ADAM_EPS, ADAM_WD, ADAM_STEP = 0.001, 0.9, 0.999, 1e-08, 0.01, 10
VMEM_LIMIT = 56 * 1024 * 1024
MESH = pl.DeviceIdType.MESH


def _cparams(sem=None):
    return pltpu.CompilerParams(dimension_semantics=sem, vmem_limit_bytes=VMEM_LIMIT)


def _full_spec(a):
    nd = a.ndim
    return pl.BlockSpec(a.shape, lambda *_: (0,) * nd)


def _rowwise(fn, rows, consts, outs, *, tm, name, halo=(), hosted=None):
    rows = [a if isinstance(a, tuple) else (a, a.shape[1]) for a in rows]
    T = rows[0][0].shape[0]
    assert T % tm == 0 and tm % 8 == 0
    n_tiles = T // tm
    n_in = len(rows) + len(halo) + len(consts)
    in_specs = [pl.BlockSpec((tm, nc), lambda i: (i, 0)) for _, nc in rows]
    args = [a for a, _ in rows]
    for a, nc, kind in halo:
        if kind == 'prev':
            in_specs.append(pl.BlockSpec((8, nc), lambda i: (jnp.maximum(i * (tm // 8) - 1, 0), 0)))
        else:
            in_specs.append(pl.BlockSpec((8, nc), lambda i: (jnp.minimum((i + 1) * (tm // 8), T // 8 - 1), 0)))
        args.append(a)
    in_specs += [_full_spec(a) for a in consts]
    args += list(consts)
    out_shape, out_specs = [], []
    for ncols, dtype, kind in outs:
        if kind == 'row':
            out_shape.append(jax.ShapeDtypeStruct((T, ncols), dtype))
            out_specs.append(pl.BlockSpec((tm, ncols), lambda i: (i, 0)))
        else:
            out_shape.append(jax.ShapeDtypeStruct((kind, ncols), dtype))
            out_specs.append(pl.BlockSpec((kind, ncols), lambda i: (0, 0)))

    def body(*refs):
        i = pl.program_id(0)
        vals = [r[...] for r in refs[:n_in]]
        res = fn(i, n_tiles, *vals)
        for (ncols, dtype, kind), o_ref, val in zip(outs, refs[n_in:], res, strict=True):
            if kind == 'row':
                o_ref[...] = val.astype(dtype)
            else:
                @pl.when(i == 0)
                def _():
                    o_ref[...] = jnp.zeros_like(o_ref)
                o_ref[...] += val.astype(dtype)

    h_in, h_in_specs, h_out_specs, h_out_shape, h_scratch = _hosted_args(hosted)
    res = pl.pallas_call(
        _hosting(body, hosted, n_in, len(outs), 0, n_tiles), name=name, grid=(n_tiles,),
        in_specs=in_specs + h_in_specs, out_specs=out_specs + h_out_specs, out_shape=out_shape + h_out_shape,
        scratch_shapes=h_scratch, compiler_params=_cparams(("arbitrary",)),
    )(*args, *h_in)
    return res


def _pick(n, cands):
    for c in cands:
        if n % c == 0:
            return c
    return n


MM_ROWS = (1024, 512, 256, 128)
MM_COLS = (1536, 1408, 1024, 896, 768, 512, 256, 128)
MM_WIDE = 3000


def _mm_nn(a, w, *, name, out_dtype=F32, hosted=None):
    T, K = a.shape
    N = w.shape[1]
    tm = _pick(T, MM_ROWS)
    tn = _pick(N, MM_COLS)
    grid = (N // tn, T // tm)
    h_in, h_in_specs, h_out_specs, h_out_shape, h_scratch = _hosted_args(hosted)

    def body(a_ref, w_ref, o_ref):
        o_ref[...] = jnp.dot(a_ref[...], w_ref[...], preferred_element_type=F32).astype(out_dtype)

    res = pl.pallas_call(
        _hosting(body, hosted, 2, 1, 0, grid), name=name, grid=grid,
        in_specs=[pl.BlockSpec((tm, K), lambda j, i: (i, 0)), pl.BlockSpec((K, tn), lambda j, i: (0, j))] + h_in_specs,
        out_specs=[pl.BlockSpec((tm, tn), lambda j, i: (i, j))] + h_out_specs,
        out_shape=[jax.ShapeDtypeStruct((T, N), out_dtype)] + h_out_shape, scratch_shapes=h_scratch,
        compiler_params=_cparams(("arbitrary", "arbitrary")),
    )(a, w, *h_in)
    return res if hosted else res[0]


def _mm_nt(dy, w, *, name, out_dtype=F32, hosted=None):
    T, N = dy.shape
    K = w.shape[0]
    tm = _pick(T, MM_ROWS if N <= MM_WIDE else MM_ROWS[1:])
    tk = _pick(K, MM_COLS[1:])
    grid = (K // tk, T // tm)
    h_in, h_in_specs, h_out_specs, h_out_shape, h_scratch = _hosted_args(hosted)

    def body(dy_ref, w_ref, o_ref):
        o_ref[...] = lax.dot_general(dy_ref[...], w_ref[...], (((1,), (1,)), ((), ())),
                                     preferred_element_type=F32).astype(out_dtype)

    res = pl.pallas_call(
        _hosting(body, hosted, 2, 1, 0, grid), name=name, grid=grid,
        in_specs=[pl.BlockSpec((tm, N), lambda j, i: (i, 0)), pl.BlockSpec((tk, N), lambda j, i: (j, 0))] + h_in_specs,
        out_specs=[pl.BlockSpec((tm, tk), lambda j, i: (i, j))] + h_out_specs,
        out_shape=[jax.ShapeDtypeStruct((T, K), out_dtype)] + h_out_shape, scratch_shapes=h_scratch,
        compiler_params=_cparams(("arbitrary", "arbitrary")),
    )(dy, w, *h_in)
    return res if hosted else res[0]


def _mm_tn(a, dy, *, name, out_dtype=F32, col_shards=None):
    T, K = a.shape
    N = dy.shape[1]
    tm = _pick(T, MM_ROWS)
    tn = N // col_shards if col_shards else _pick(N, MM_COLS[1:])
    n_t = T // tm

    def body(a_ref, dy_ref, o_ref, acc_ref):
        i = pl.program_id(1)

        @pl.when(i == 0)
        def _():
            acc_ref[...] = jnp.zeros_like(acc_ref)

        acc_ref[...] += lax.dot_general(a_ref[...], dy_ref[...], (((0,), (0,)), ((), ())), preferred_element_type=F32)

        @pl.when(i == n_t - 1)
        def _():
            o_ref[...] = acc_ref[...].astype(out_dtype)

    if col_shards:
        out_specs = pl.BlockSpec((None, K, tn), lambda j, i: (j, 0, 0))
        out_shape = jax.ShapeDtypeStruct((col_shards, K, tn), out_dtype)
    else:
        out_specs = pl.BlockSpec((K, tn), lambda j, i: (0, j))
        out_shape = jax.ShapeDtypeStruct((K, N), out_dtype)
    return pl.pallas_call(
        body, name=name, grid=(N // tn, n_t),
        in_specs=[pl.BlockSpec((tm, K), lambda j, i: (i, 0)), pl.BlockSpec((tm, tn), lambda j, i: (i, j))],
        out_specs=out_specs, out_shape=out_shape, scratch_shapes=[pltpu.VMEM((K, tn), F32)],
        compiler_params=_cparams(("arbitrary", "arbitrary")),
    )(a, dy)


def _seg_ones(n):
    r = lax.broadcasted_iota(jnp.int32, (n, n), 0) // HD
    c = lax.broadcasted_iota(jnp.int32, (n, n), 1) // HD
    return (r == c).astype(F32)


def _segsum_raw(x):
    ones = _seg_ones(x.shape[1])
    if MXU == F32:
        return jnp.dot(x, ones, precision=HI, preferred_element_type=F32)
    hi = x.astype(MXU)
    lo = (x - hi.astype(F32)).astype(MXU)
    ones = ones.astype(MXU)
    return jnp.dot(hi, ones, preferred_element_type=F32) + jnp.dot(lo, ones, preferred_element_type=F32)


@jax.custom_vjp
def _segsum(x):
    return _segsum_raw(x)


def _segsum_fwd(x):
    return _segsum_raw(x), None


def _segsum_bwd(_, g):
    return (_segsum_raw(g),)


_segsum.defvjp(_segsum_fwd, _segsum_bwd)


def _mxu(x):
    return x.astype(MXU)


@jax.custom_vjp
def _bdot(a, b):
    return jnp.dot(_mxu(a), _mxu(b), preferred_element_type=F32)


def _bdot_fwd(a, b):
    return _bdot(a, b), (a, b)


def _bdot_bwd(res, g):
    a, b = res
    da = lax.dot_general(_mxu(g), _mxu(b), (((1,), (1,)), ((), ())), preferred_element_type=F32)
    db = lax.dot_general(_mxu(a), _mxu(g), (((0,), (0,)), ((), ())), preferred_element_type=F32)
    return da.astype(a.dtype), db.astype(b.dtype)


_bdot.defvjp(_bdot_fwd, _bdot_bwd)


@jax.custom_vjp
def _bdot_nt(a, b):
    return lax.dot_general(_mxu(a), _mxu(b), (((1,), (1,)), ((), ())), preferred_element_type=F32)


def _bdot_nt_fwd(a, b):
    return _bdot_nt(a, b), (a, b)


def _bdot_nt_bwd(res, g):
    a, b = res
    da = jnp.dot(_mxu(g), _mxu(b), preferred_element_type=F32)
    db = lax.dot_general(_mxu(g), _mxu(a), (((0,), (0,)), ((), ())), preferred_element_type=F32)
    return da.astype(a.dtype), db.astype(b.dtype)


_bdot_nt.defvjp(_bdot_nt_fwd, _bdot_nt_bwd)


def _sigmoid(x):
    return 1.0 / (1.0 + jnp.exp(-x))


def _softplus(x):
    return jnp.maximum(x, 0.0) + jnp.log(1.0 + jnp.exp(jnp.minimum(x, -x)))


def _norm_mod(x, gain, scale, shift):
    inv = lax.rsqrt(jnp.mean(x * x, axis=-1, keepdims=True) + RMS_EPS)
    return (x * inv) * gain * (1.0 + scale) + shift


def _prep(mixed, decay_w0, lora_up, iclr_a0, gate_up, k_k, k_a):
    r = mixed[:, 0:RW]
    k = mixed[:, RW:2 * RW]
    v = mixed[:, 2 * RW:3 * RW]
    z = mixed[:, 3 * RW:3 * RW + 128]
    xg = mixed[:, 3 * RW + 128:]
    lane = lax.broadcasted_iota(jnp.int32, z.shape, 1)
    tz = jnp.where(lane < 64, jnp.tanh(z), z)
    lo = _bdot(tz, lora_up)
    w_log = -_softplus(-(decay_w0 + lo[:, :RW])) - 0.5
    lw = -jnp.exp(w_log)
    a_ic = _sigmoid(iclr_a0 + lo[:, RW:])
    g = _bdot(_sigmoid(xg), gate_up)
    kk = k * k_k
    kk = kk / jnp.maximum(jnp.sqrt(_segsum(kk * kk)), 1e-12)
    k_mod = k * (1.0 + (a_ic - 1.0) * k_a)
    return jnp.concatenate([r, lw, k_mod, v, -kk, kk * a_ic, g], axis=1)


def _post(y, r, k, v, g, lnx_gain, lnx_bias, r_k):
    mu = _segsum(y) * (1.0 / HD)
    yc = y - mu
    var = _segsum(yc * yc) * (1.0 / HD)
    yn = yc * lax.rsqrt(var + GN_EPS) * lnx_gain + lnx_bias
    bonus = _segsum(r * k * r_k) * v
    return (yn + bonus) * g


def _merge(pg, ma, mb, bias):
    gates = _sigmoid(pg + bias)
    return gates[:, :D] * ma + gates[:, D:] * mb


def _swiglu(u, v):
    return u * _sigmoid(u) * v


@functools.partial(jax.custom_vjp, nondiff_argnums=(1,))
def _lane_roll(x, s):
    return pltpu.roll(x, s, 1)


def _lane_roll_fwd(x, s):
    return pltpu.roll(x, s, 1), None


def _lane_roll_bwd(s, _, g):
    n = g.shape[1]
    return (pltpu.roll(g, (n - s) % n, 1),)


_lane_roll.defvjp(_lane_roll_fwd, _lane_roll_bwd)


def _rope(x, cos, sin_lo, sin_hi):
    n = x.shape[1]
    return x * cos + _lane_roll(x, n - 8) * sin_lo + _lane_roll(x, 8) * sin_hi


def _head_rms(x, gain):
    return x * lax.rsqrt(_segsum(x * x) * (1.0 / HD) + RMS_EPS) * gain


def _attn_block(qkv_c, qkv_p, tab_c, tab_p, qg, kg, sinks, first):
    def tabs(tab, n):
        return tab[:, 0:n], tab[:, RW:RW + n], tab[:, 2 * RW:2 * RW + n]

    qg = jnp.concatenate([qg] * NH, axis=1)
    kg = jnp.concatenate([kg] * 2, axis=1)
    q = _rope(_head_rms(qkv_c[:, :RW], qg), *tabs(tab_c, RW))
    k_c = _rope(_head_rms(qkv_c[:, RW:RW + 128], kg), *tabs(tab_c, 128))
    k_p = _rope(_head_rms(qkv_p[:, RW:RW + 128], kg), *tabs(tab_p, 128))
    kband = jnp.concatenate([k_p, k_c], axis=0)
    vband = jnp.concatenate([qkv_p[:, RW + 128:], qkv_c[:, RW + 128:]], axis=0)
    G = 4
    qi = lax.broadcasted_iota(jnp.int32, (G * BLK, 2 * BLK), 0) % BLK
    kj = lax.broadcasted_iota(jnp.int32, (G * BLK, 2 * BLK), 1)
    dist = qi + BLK - kj
    valid = (dist >= 0) & (dist < BLK) & (jnp.logical_not(first) | (kj >= BLK))
    row_g = lax.broadcasted_iota(jnp.int32, (G * BLK, 1), 0) // BLK
    outs = []
    for kvh in range(2):
        kb = kband[:, kvh * HD:(kvh + 1) * HD]
        vb = vband[:, kvh * HD:(kvh + 1) * HD]
        qs = jnp.concatenate([q[:, (G * kvh + g) * HD:(G * kvh + g + 1) * HD] for g in range(G)], axis=0)
        s = _bdot_nt(qs, kb) * (HD ** -0.5)
        s = jnp.where(valid, s, NEG_INF)
        sink = jnp.zeros((G * BLK, 1), F32)
        for g in range(G):
            sink = jnp.where(row_g == g, sinks[:, G * kvh + g:G * kvh + g + 1], sink)
        m = lax.stop_gradient(jnp.maximum(jnp.max(s, axis=-1, keepdims=True), sink))
        e = jnp.exp(s - m)
        p = e / (jnp.sum(e, axis=-1, keepdims=True) + jnp.exp(sink - m))
        o = _bdot(p, vb)
        outs += [o[g * BLK:(g + 1) * BLK] for g in range(G)]
    return jnp.concatenate(outs, axis=1)


def _heads(x):
    return jnp.stack([x[:, h * HD:(h + 1) * HD] for h in range(NH)], axis=0)


def _unheads(x):
    return jnp.concatenate([x[h] for h in range(NH)], axis=1)


def _split(x, n):
    parts, rest = [], x
    for _ in range(n):
        p = rest.astype(MXU)
        parts.append(p)
        rest = rest - p.astype(F32)
    return parts


def _bdot_batched(a, b, ca, cb):
    return lax.dot_general(a, b, (((ca,), (cb,)), ((0,), (0,))), preferred_element_type=F32)


def _bmm_passes(a, b, ca, cb, passes):
    if MXU == F32:
        return lax.dot_general(a, b, (((ca,), (cb,)), ((0,), (0,))), precision=HI, preferred_element_type=F32)
    if passes == 1:
        return _bdot_batched(a.astype(MXU), b.astype(MXU), ca, cb)
    (a0, a1), (b0, b1) = _split(a, 2), _split(b, 2)
    return _bdot_batched(a0, b0, ca, cb) + (_bdot_batched(a0, b1, ca, cb) + _bdot_batched(a1, b0, ca, cb))


@functools.partial(jax.custom_vjp, nondiff_argnums=(2, 3, 4))
def _bmm(a, b, ca, cb, passes=1):
    return _bmm_passes(a, b, ca, cb, passes)


def _bmm_fwd(a, b, ca, cb, passes):
    return _bmm_passes(a, b, ca, cb, passes), (a, b)


def _bmm_bwd(ca, cb, passes, res, g):
    a, b = res
    if (ca, cb) == (2, 1):
        return _bmm_passes(g, b, 2, 2, passes), _bmm_passes(a, g, 1, 1, passes)
    if (ca, cb) == (2, 2):
        return _bmm_passes(g, b, 2, 1, passes), _bmm_passes(g, a, 1, 1, passes)
    return _bmm_passes(b, g, 2, 2, passes), _bmm_passes(a, g, 2, 1, passes)


_bmm.defvjp(_bmm_fwd, _bmm_bwd)


def _tri_dot(x, transpose):
    C = x.shape[1]
    ri = lax.broadcasted_iota(jnp.int32, (C, C), 0)
    ci = lax.broadcasted_iota(jnp.int32, (C, C), 1)
    tri = jnp.broadcast_to(((ri <= ci) if transpose else (ri >= ci)).astype(MXU), (x.shape[0], C, C))
    if MXU == F32:
        return lax.dot_general(tri, x, (((2,), (1,)), ((0,), (0,))), precision=HI, preferred_element_type=F32)
    p0, p1, p2 = _split(x, 3)
    return _bdot_batched(tri, p0, 2, 1) + (_bdot_batched(tri, p1, 2, 1) + _bdot_batched(tri, p2, 2, 1))


@jax.custom_vjp
def _cumsum_rows(x):
    return _tri_dot(x, False)


def _cumsum_rows_fwd(x):
    return _tri_dot(x, False), None


def _cumsum_rows_bwd(_, g):
    return (_tri_dot(g, True),)


_cumsum_rows.defvjp(_cumsum_rows_fwd, _cumsum_rows_bwd)

P_SCORE = 1
P_SOLVE = 1
P_STATE = 1
SCAN_CHUNKS = (4, 2, 1)


def _neumann(l):
    C = l.shape[1]
    eye = (lax.broadcasted_iota(jnp.int32, (C, C), 0) == lax.broadcasted_iota(jnp.int32, (C, C), 1)).astype(F32)
    x, lp = eye + l, l
    for _ in range(int(math.log2(C)) - 1):
        lp = _bmm(lp, lp, 2, 1, P_SOLVE)
        x = x + _bmm(x, lp, 2, 1, P_SOLVE)
    return x


@jax.custom_vjp
def _unit_lower_inverse(l):
    return _neumann(l)


def _unit_lower_inverse_fwd(l):
    x = _neumann(l)
    return x, x


def _unit_lower_inverse_bwd(x, g):
    return (_bmm(_bmm(x, g, 1, 1, P_SOLVE), x, 2, 2, P_SOLVE),)


_unit_lower_inverse.defvjp(_unit_lower_inverse_fwd, _unit_lower_inverse_bwd)


def _chunk(S0, r, lw, k, v, a, b):
    C = CHUNK
    n = r.shape[1] // C
    fold = lambda t: t.reshape(NH * n, C, HD)
    r, lw, k, v, a, b = (fold(t) for t in (r, lw, k, v, a, b))
    ri = lax.broadcasted_iota(jnp.int32, (C, C), 0)
    ci = lax.broadcasted_iota(jnp.int32, (C, C), 1)
    incl = (ri >= ci)
    strict = (ri > ci)
    eye = (ri == ci).astype(F32)
    cum = _cumsum_rows(lw)
    p_in = jnp.exp(cum)
    p_ex = jnp.exp(cum - lw)
    p_inv = jnp.exp(-cum)
    at, rt, bt, kt = a * p_ex, r * p_in, b * p_inv, k * p_inv
    lhs = jnp.concatenate([at, rt], axis=1)
    rhs_ = jnp.concatenate([bt, kt], axis=1)
    sc = _bmm(lhs, rhs_, 2, 2, P_SCORE)
    a_ab = jnp.where(strict, sc[:, :C, :C], 0.0)
    a_ak = jnp.where(strict, sc[:, :C, C:], 0.0)
    incl2 = (lax.broadcasted_iota(jnp.int32, (C, 2 * C), 0) >= lax.broadcasted_iota(jnp.int32, (C, 2 * C), 1) % C)
    a_r = jnp.where(incl2, sc[:, C:, :], 0.0)
    av = _bmm(a_ak, v, 2, 1, P_SCORE)
    x = _unit_lower_inverse(a_ab)
    p_last = jnp.exp(cum[:, C - 1:C, :])
    per_chunk = lambda t: t.reshape((NH, n) + t.shape[1:])
    lhs, rhs_, a_r, av, x, v, p_last = (per_chunk(t) for t in (lhs, rhs_, a_r, av, x, v, p_last))
    S, ys = S0, []
    for c in range(n):
        s0 = _bmm(lhs[:, c], S, 2, 2, P_STATE)
        u = _bmm(x[:, c], s0[:, :C] + av[:, c], 2, 1, P_SOLVE)
        uv = jnp.concatenate([u, v[:, c]], axis=1)
        ys.append(s0[:, C:] + _bmm(a_r[:, c], uv, 2, 1, P_SCORE))
        S = (S + _bmm(uv, rhs_[:, c], 1, 1, P_STATE)) * p_last[:, c]
    return jnp.concatenate(ys, axis=1), S


def _hosting(body, ex, n_in, n_out, n_scratch, n_steps):
    if ex is None:
        return body

    def wrapped(*refs):
        a = n_in
        b = a + ex.n_in
        c = b + n_out
        d = c + ex.n_out
        e = d + n_scratch
        ex_refs = (refs[a:b], refs[c:d], refs[e:])
        grid = n_steps if isinstance(n_steps, tuple) else (n_steps,)
        first = last = True
        for ax_, size in enumerate(grid):
            first = first & (pl.program_id(ax_) == 0)
            last = last & (pl.program_id(ax_) == size - 1)

        @pl.when(first)
        def _():
            ex.start(*ex_refs)

        body(*refs[:a], *refs[b:c], *refs[d:e])

        @pl.when(last)
        def _():
            ex.wait(*ex_refs)

    return wrapped


def _hosted_args(ex):
    if ex is None:
        return [], [], [], [], []
    any_spec = pl.BlockSpec(memory_space=pl.ANY)
    return list(ex.arrays), [any_spec] * ex.n_in, [any_spec] * ex.n_out, list(ex.out_shape), list(ex.scratch)


def _scan_fwd(rw, *, name, hosted=None):
    T = rw.shape[0]
    rows = CHUNK * _pick(T // CHUNK, SCAN_CHUNKS)
    n = T // rows
    h_in, h_in_specs, h_out_specs, h_out_shape, h_scratch = _hosted_args(hosted)

    def body(r_ref, lw_ref, k_ref, v_ref, a_ref, b_ref, y_ref, ck_ref, s_ref):
        @pl.when(pl.program_id(0) == 0)
        def _():
            s_ref[...] = jnp.zeros_like(s_ref)

        S0 = s_ref[...]
        ck_ref[0] = S0
        y, S1 = _chunk(S0, *[_heads(ref[...]) for ref in (r_ref, lw_ref, k_ref, v_ref, a_ref, b_ref)])
        y_ref[...] = _unheads(y)
        s_ref[...] = S1

    col = lambda j: pl.BlockSpec((rows, RW), lambda i: (i, j))
    return pl.pallas_call(
        _hosting(body, hosted, 6, 2, 1, n), name=name, grid=(n,),
        in_specs=[col(j) for j in range(6)] + h_in_specs,
        out_specs=[pl.BlockSpec((rows, RW), lambda i: (i, 0)),
                   pl.BlockSpec((1, NH, HD, HD), lambda i: (i, 0, 0, 0))] + h_out_specs,
        out_shape=[jax.ShapeDtypeStruct((T, RW), F32), jax.ShapeDtypeStruct((n, NH, HD, HD), F32)] + h_out_shape,
        scratch_shapes=[pltpu.VMEM((NH, HD, HD), F32)] + h_scratch,
        compiler_params=_cparams(("arbitrary",)),
    )(rw, rw, rw, rw, rw, rw, *h_in)


def _scan_bwd(rw, ck, dy, *, name, hosted=None):
    T = rw.shape[0]
    rows = CHUNK * _pick(T // CHUNK, SCAN_CHUNKS)
    n = T // rows

    def body(r_ref, lw_ref, k_ref, v_ref, a_ref, b_ref, ck_ref, dy_ref, o_ref, ds_ref):
        @pl.when(pl.program_id(0) == 0)
        def _():
            ds_ref[...] = jnp.zeros_like(ds_ref)

        prim = [_heads(ref[...]) for ref in (r_ref, lw_ref, k_ref, v_ref, a_ref, b_ref)]
        _, vjp = jax.vjp(_chunk, ck_ref[0], *prim)
        grads = vjp((_heads(dy_ref[...]), ds_ref[...]))
        ds_ref[...] = grads[0]
        o_ref[...] = jnp.concatenate([_unheads(g) for g in grads[1:]], axis=1)

    h_in, h_in_specs, h_out_specs, h_out_shape, h_scratch = _hosted_args(hosted)
    col = lambda j: pl.BlockSpec((rows, RW), lambda i: (n - 1 - i, j))
    return pl.pallas_call(
        _hosting(body, hosted, 8, 1, 1, n), name=name, grid=(n,),
        in_specs=[col(j) for j in range(6)] + [pl.BlockSpec((1, NH, HD, HD), lambda i: (n - 1 - i, 0, 0, 0)),
                                               pl.BlockSpec((rows, RW), lambda i: (n - 1 - i, 0))] + h_in_specs,
        out_specs=[pl.BlockSpec((rows, 6 * RW), lambda i: (n - 1 - i, 0))] + h_out_specs,
        out_shape=[jax.ShapeDtypeStruct((T, 6 * RW), F32)] + h_out_shape,
        scratch_shapes=[pltpu.VMEM((NH, HD, HD), F32)] + h_scratch,
        compiler_params=_cparams(("arbitrary",)),
    )(rw, rw, rw, rw, rw, rw, ck, dy, *h_in)


ATTN_BLOCKS = (4, 2, 1)

def _attn_fwd(qkv, tab, qg, kg, sinks, *, name, hosted=None):
    T = qkv.shape[0]
    nb = _pick(T // BLK, ATTN_BLOCKS)
    n = T // (BLK * nb)
    h_in, h_in_specs, h_out_specs, h_out_shape, h_scratch = _hosted_args(hosted)

    def body(c_ref, p_ref, tc_ref, tp_ref, qg_ref, kg_ref, s_ref, o_ref):
        for b in range(nb):
            rows = slice(b * BLK, (b + 1) * BLK)
            before = slice((b - 1) * BLK, b * BLK)
            p, tp = (p_ref[...], tp_ref[...]) if b == 0 else (c_ref[before, :], tc_ref[before, :])
            first = (pl.program_id(0) == 0) if b == 0 else False
            o_ref[rows, :] = _attn_block(c_ref[rows, :], p, tc_ref[rows, :], tp, qg_ref[...], kg_ref[...],
                                         s_ref[...], first).astype(o_ref.dtype)

    cur = lambda w: pl.BlockSpec((nb * BLK, w), lambda i: (i, 0))
    prev = lambda w: pl.BlockSpec((BLK, w), lambda i: (jnp.maximum(i * nb - 1, 0), 0))
    return pl.pallas_call(
        _hosting(body, hosted, 7, 1, 0, n), name=name, grid=(n,),
        in_specs=[cur(QKV_W), prev(QKV_W), cur(3 * RW), prev(3 * RW), _full_spec(qg), _full_spec(kg),
                  _full_spec(sinks)] + h_in_specs,
        out_specs=[cur(RW)] + h_out_specs, out_shape=[jax.ShapeDtypeStruct((T, RW), MXU)] + h_out_shape,
        scratch_shapes=h_scratch,
        compiler_params=_cparams(("arbitrary",)),
    )(qkv, qkv, tab, tab, qg, kg, sinks, *h_in)


def _attn_bwd(qkv, tab, qg, kg, sinks, dy, *, name, hosted=None):
    T = qkv.shape[0]
    nb = _pick(T // BLK, ATTN_BLOCKS)
    n = T // (BLK * nb)
    h_in, h_in_specs, h_out_specs, h_out_shape, h_scratch = _hosted_args(hosted)

    def body(c_ref, p_ref, tc_ref, tp_ref, qg_ref, kg_ref, s_ref, dy_ref, dqkv_ref, dqg_ref, dkg_ref, ds_ref, carry_ref):
        i = pl.program_id(0)

        @pl.when(i == 0)
        def _():
            carry_ref[...] = jnp.zeros_like(carry_ref)
            dqg_ref[...] = jnp.zeros_like(dqg_ref)
            dkg_ref[...] = jnp.zeros_like(dkg_ref)
            ds_ref[...] = jnp.zeros_like(ds_ref)

        carry = carry_ref[...]
        dqg_t, dkg_t, ds_t = jnp.zeros_like(dqg_ref), jnp.zeros_like(dkg_ref), jnp.zeros_like(ds_ref)
        for b in reversed(range(nb)):
            rows = slice(b * BLK, (b + 1) * BLK)
            before = slice((b - 1) * BLK, b * BLK)
            tc = tc_ref[rows, :]
            p, tp = (p_ref[...], tp_ref[...]) if b == 0 else (c_ref[before, :], tc_ref[before, :])
            first = (i == n - 1) if b == 0 else False
            f = lambda c, p_, qg_, kg_, sk, tc=tc, tp=tp, first=first: _attn_block(c, p_, tc, tp, qg_, kg_, sk, first)
            _, vjp = jax.vjp(f, c_ref[rows, :], p, qg_ref[...], kg_ref[...], s_ref[...])
            dc, dp, dqg, dkg, dsk = vjp(dy_ref[rows, :].astype(F32))
            dqkv_ref[rows, :] = (dc + carry).astype(dqkv_ref.dtype)
            carry = dp
            dqg_t, dkg_t, ds_t = dqg_t + dqg, dkg_t + dkg, ds_t + dsk
        carry_ref[...] = carry
        dqg_ref[...] += dqg_t
        dkg_ref[...] += dkg_t
        ds_ref[...] += ds_t

    cur = lambda w: pl.BlockSpec((nb * BLK, w), lambda i: (n - 1 - i, 0))
    prev = lambda w: pl.BlockSpec((BLK, w), lambda i: (jnp.maximum((n - 1 - i) * nb - 1, 0), 0))
    return pl.pallas_call(
        _hosting(body, hosted, 8, 4, 1, n), name=name, grid=(n,),
        in_specs=[cur(QKV_W), prev(QKV_W), cur(3 * RW), prev(3 * RW), _full_spec(qg), _full_spec(kg), _full_spec(sinks),
                  cur(RW)] + h_in_specs,
        out_specs=[cur(QKV_W), _full_spec(qg), _full_spec(kg), _full_spec(sinks)] + h_out_specs,
        out_shape=[jax.ShapeDtypeStruct((T, QKV_W), MXU), jax.ShapeDtypeStruct(qg.shape, F32),
                   jax.ShapeDtypeStruct(kg.shape, F32), jax.ShapeDtypeStruct(sinks.shape, F32)] + h_out_shape,
        scratch_shapes=[pltpu.VMEM((BLK, QKV_W), F32)] + h_scratch,
        compiler_params=_cparams(("arbitrary",)),
    )(qkv, qkv, tab, tab, qg, kg, sinks, dy, *h_in)


def _shift_down(cur, prev8, i):
    rolled = pltpu.roll(cur, 1, 0)
    first_row = jnp.where(i > 0, prev8[7:8, :], 0.0)
    row = lax.broadcasted_iota(jnp.int32, cur.shape, 0)
    return jnp.where(row == 0, first_row, rolled)


def _shift_up(cur, next8, i, n):
    tm = cur.shape[0]
    rolled = pltpu.roll(cur, tm - 1, 0)
    last_row = jnp.where(i < n - 1, next8[0:1, :], 0.0)
    row = lax.broadcasted_iota(jnp.int32, cur.shape, 0)
    return jnp.where(row == tm - 1, last_row, rolled)


def _ada_parts(ada):
    return [ada[:, j * D:(j + 1) * D] for j in range(6)]


def _rope_table(positions):
    half = HD // 8
    inv_freq = 500000.0 ** (-jnp.arange(half, dtype=F32) / half)
    ang = positions.astype(F32)[:, None] * inv_freq
    cos, sin = jnp.cos(ang), jnp.sin(ang)
    T = positions.shape[0]
    pad = HD - 2 * half
    c64 = jnp.concatenate([cos, cos, jnp.ones((T, pad), F32)], axis=1)
    lo64 = jnp.concatenate([-sin, jnp.zeros((T, HD - half), F32)], axis=1)
    hi64 = jnp.concatenate([jnp.zeros((T, half), F32), sin, jnp.zeros((T, pad), F32)], axis=1)
    return jnp.concatenate([jnp.tile(t, (1, NH)) for t in (c64, lo64, hi64)], axis=1)


GATHER_BEHIND = {"f_proj_shift": ['w_out'], "f_proj_gates": ['w_branch_a', 'w_branch_b'], "f_prep": ['ffn_w2_lo'],
                 "f_scan": ['ffn_w1'], "f_post": ['ffn_w2_hi'], "f_attn": ['ffn_w3']}
LATE = ['w_out', 'w_branch_a', 'w_branch_b', 'ffn_w1', 'ffn_w3', 'ffn_w2']
BACK_ATTN = ['w_out', 'w_branch_a', 'w_branch_b']
BACK_SCAN = ['ffn_w1', 'ffn_w3', 'ffn_w2']
BACK_LAST = ['w_in', 'decay_up', 'iclr_up', 'gate_up']


def _full_weight(g, ax):
    return g.reshape(-1, g.shape[2]) if ax == 0 else jnp.concatenate([g[j] for j in range(4)], axis=1)


def _local_step(x, target, ada, tab, w, s, shards=None):
    T = x.shape[0]
    tm = _pick(T, (512, 256, 128))
    tm_wide = _pick(T, (256, 128))
    tm_vjp = _pick(T, (128,))
    row = lambda n, dt=F32: (n, dt, 'row')
    acc = lambda n, r=1: (n, F32, r)

    def f_norm1(i, n, x_, g, ada_):
        sh, sc = ada_[:, 0:D], ada_[:, D:2 * D]
        return (_norm_mod(x_, g, sc, sh),)
    (h1,) = _rowwise(f_norm1, [x], [s['norm1_gain'], ada], [row(D, MXU)], tm=tm, name="f_norm1")

    pieces, late = {}, {}
    if shards:
        half = shards['ffn_w2'].shape[0] // 2
        pieces = dict(shards, ffn_w2_lo=shards['ffn_w2'][:half], ffn_w2_hi=shards['ffn_w2'][half:])

    def behind(kernel_name):
        return _GatherChips([pieces[n] for n in GATHER_BEHIND[kernel_name]]) if shards else None

    def took(kernel_name, got):
        late.update(zip(GATHER_BEHIND[kernel_name], got))

    def mm_behind(a_, w_, kernel_name, **kw):
        ex = behind(kernel_name)
        res = _mm_nn(a_, w_, name=kernel_name, hosted=ex, **kw)
        if ex:
            took(kernel_name, res[1:])
            return res[0]
        return res

    proj = mm_behind(h1, w['w_in'][:, :SHIFT_W], "f_proj_shift")
    proj_qkv = _mm_nn(h1, w['w_in'][:, SHIFT_W:SHIFT_W + QKV_W], name="f_proj_qkv")
    proj_g = mm_behind(h1, w['w_in'][:, SHIFT_W + QKV_W:], "f_proj_gates", out_dtype=MXU)
    prep_consts = [s['decay_w0'], s['lora_up'], s['iclr_a0'], s['gate_up'], s['k_k'], s['k_a']]

    def f_prep(i, n, cur, prev8, mu, *params):
        mixed = cur + (_shift_down(cur, prev8, i) - cur) * mu
        return (_prep(mixed, *params),)
    rw, *got = _rowwise(f_prep, [(proj, SHIFT_W)], [s['tshift_mu']] + prep_consts, [row(7 * RW)], tm=tm_wide,
                        name="f_prep", halo=[(proj, SHIFT_W, 'prev')], hosted=behind("f_prep"))
    took("f_prep", got)
    y, ck, *got = _scan_fwd(rw, name="f_scan", hosted=behind("f_scan"))
    took("f_scan", got)
    post_consts = [s['lnx_gain'], s['lnx_bias'], s['r_k']]

    def post_of(y_, rw_, *params):
        return _post(y_, rw_[:, 0:RW], rw_[:, 2 * RW:3 * RW], rw_[:, 3 * RW:4 * RW], rw_[:, 6 * RW:7 * RW], *params)

    def f_post(i, n, y_, rw_, *params):
        return (post_of(y_, rw_, *params),)
    ya, *got = _rowwise(f_post, [y, rw], post_consts, [row(RW, MXU)], tm=tm_wide, name="f_post", hosted=behind("f_post"))
    took("f_post", got)
    yb, *got = _attn_fwd(proj_qkv, tab, s['q_norm_gain'], s['k_norm_gain'], s['attn_sinks'], name="f_attn",
                         hosted=behind("f_attn"))
    took("f_attn", got)
    w = dict(w)
    if shards:
        ax = dict(SHARDED)
        w.update({n: _full_weight(late[n], ax[n]) for n in ('w_branch_a', 'w_branch_b', 'w_out')})
        w['ffn_w13'] = jnp.concatenate([_full_weight(late[n], 1) for n in ('ffn_w1', 'ffn_w3')], axis=1)
        w['ffn_w2'] = jnp.concatenate([late['ffn_w2_lo'], late['ffn_w2_hi']], axis=1).reshape(-1, D)
    ma = _mm_nn(ya, w['w_branch_a'], name="f_branch_a", out_dtype=MXU)
    mb = _mm_nn(yb, w['w_branch_b'], name="f_branch_b", out_dtype=MXU)

    def f_merge(i, n, pg, ma_, mb_, bias):
        return (_merge(pg.astype(F32), ma_.astype(F32), mb_.astype(F32), bias),)
    (merged,) = _rowwise(f_merge, [proj_g, ma, mb], [s['branch_gate_b']], [row(D, MXU)], tm=tm, name="f_merge")
    mo = _mm_nn(merged, w['w_out'], name="f_out")

    def f_res1(i, n, x_, mo_, g, ada_):
        x1_ = x_ + ada_[:, 2 * D:3 * D] * mo_
        return x1_, _norm_mod(x1_, g, ada_[:, 4 * D:5 * D], ada_[:, 3 * D:4 * D])
    x1, h2 = _rowwise(f_res1, [x, mo], [s['norm2_gain'], ada], [row(D), row(D, MXU)], tm=tm, name="f_res1")
    uv = _mm_nn(h2, w['ffn_w13'], name="f_ffn_in", out_dtype=MXU)

    def f_act(i, n, uv_):
        return (_swiglu(uv_[:, :DFF].astype(F32), uv_[:, DFF:].astype(F32)),)
    (act,) = _rowwise(f_act, [uv], [], [row(DFF, MXU)], tm=tm_wide, name="f_act")
    ff = _mm_nn(act, w['ffn_w2'], name="f_ffn_out")

    def f_loss(i, n, x1_, ff_, tgt, ada_):
        g2 = ada_[:, 5 * D:6 * D]
        err = x1_ + g2 * ff_ - tgt
        dx2 = err * (1.0 / D)
        loss = 0.5 * jnp.sum(jnp.sum(err * err, axis=1, keepdims=True) * (1.0 / D), axis=0, keepdims=True)
        return dx2, (dx2 * g2), jnp.broadcast_to(loss, (1, 128)), jnp.sum(dx2 * ff_, axis=0, keepdims=True)
    dx2, dff, loss, dgate2 = _rowwise(f_loss, [x1, ff, target], [ada], [row(D), row(D, MXU), acc(128), acc(D)],
                                      tm=tm, name="f_loss")

    dact = _mm_nt(dff, w['ffn_w2'], name="b_ffn_out_dx", out_dtype=MXU)
    g_w2 = _mm_tn(act, dff, name="b_ffn_out_dw", out_dtype=MXU)

    def b_act(i, n, uv_, dact_):
        _, vjp = jax.vjp(_swiglu, uv_[:, :DFF].astype(F32), uv_[:, DFF:].astype(F32))
        du, dv = vjp(dact_.astype(F32))
        return (jnp.concatenate([du, dv], axis=1),)
    (duv,) = _rowwise(b_act, [uv, dact], [], [row(2 * DFF, MXU)], tm=tm_wide, name="b_act")
    dh2 = _mm_nt(duv, w['ffn_w13'], name="b_ffn_in_dx")
    g_w13 = _mm_tn(h2, duv, name="b_ffn_in_dw", out_dtype=MXU)

    def b_res1(i, n, x1_, dh2_, dx2_, mo_, g, ada_):
        _, vjp = jax.vjp(_norm_mod, x1_, g, ada_[:, 4 * D:5 * D], ada_[:, 3 * D:4 * D])
        dxn, dg, dsc, dsh = vjp(dh2_)
        dx1_ = dxn + dx2_
        g1 = ada_[:, 2 * D:3 * D]
        return dx1_, dx1_ * g1, dg, dsc, dsh, jnp.sum(dx1_ * mo_, axis=0, keepdims=True)
    dx1, dmo, d_gain2, d_scale2, d_shift2, dgate1 = _rowwise(
        b_res1, [x1, dh2, dx2, mo], [s['norm2_gain'], ada], [row(D), row(D, MXU), acc(D), acc(D), acc(D), acc(D)],
        tm=tm, name="b_res1")
    dmerged = _mm_nt(dmo, w['w_out'], name="b_out_dx", out_dtype=MXU)
    g_wout = _mm_tn(merged, dmo, name="b_out_dw", out_dtype=MXU)

    def b_merge(i, n, pg, ma_, mb_, dm, bias):
        _, vjp = jax.vjp(_merge, pg.astype(F32), ma_.astype(F32), mb_.astype(F32), bias)
        dpg, dma_, dmb_, dbias = vjp(dm.astype(F32))
        return dpg, dma_, dmb_, dbias
    dpg, dma, dmb, d_bias = _rowwise(b_merge, [proj_g, ma, mb, dmerged], [s['branch_gate_b']],
                                     [row(GATE_W, MXU), row(D, MXU), row(D, MXU), acc(GATE_W)], tm=tm_wide, name="b_merge")
    dya = _mm_nt(dma, w['w_branch_a'], name="b_branch_a_dx")
    g_wa = _mm_tn(ya, dma, name="b_branch_a_dw", out_dtype=MXU, col_shards=4)
    dyb = _mm_nt(dmb, w['w_branch_b'], name="b_branch_b_dx", out_dtype=F32)
    g_wb = _mm_tn(yb, dmb, name="b_branch_b_dw", out_dtype=MXU, col_shards=4)
    fs = DFF // 4
    gw = dict(w_branch_a=g_wa, w_branch_b=g_wb, w_out=g_wout.reshape(4, D // 4, D),
              ffn_w1=jnp.stack([g_w13[:, j * fs:(j + 1) * fs] for j in range(4)]),
              ffn_w3=jnp.stack([g_w13[:, DFF + j * fs:DFF + (j + 1) * fs] for j in range(4)]),
              ffn_w2=g_w2.reshape(4, fs, D))
    recv = {}
    dqkv, d_qg, d_kg, d_sinks, *got = _attn_bwd(
        proj_qkv, tab, s['q_norm_gain'], s['k_norm_gain'], s['attn_sinks'], dyb, name="b_attn",
        hosted=shards and _ScatterChips([gw[n] for n in BACK_ATTN]))
    recv.update(zip(BACK_ATTN, got))

    def b_post(i, n, y_, rw_, dya_, *params):
        _, vjp = jax.vjp(post_of, y_, rw_, *params)
        return vjp(dya_)
    dy, drw_post, d_lnx_gain, d_lnx_bias, d_r_k = _rowwise(
        b_post, [y, rw, dya], post_consts, [row(RW), row(7 * RW), acc(RW), acc(RW), acc(RW)], tm=tm_vjp, name="b_post")
    dscan, *got = _scan_bwd(rw, ck, dy, name="b_scan",
                            hosted=shards and _ScatterChips([gw[n] for n in BACK_SCAN]))
    recv.update(zip(BACK_SCAN, got))

    def b_prep(i, n, cur, drw_, dscan_, prev8, mu, *params):
        shifted = _shift_down(cur, prev8, i)
        mixed = cur + (shifted - cur) * mu
        _, vjp = jax.vjp(_prep, mixed, *params)
        ct = drw_ + jnp.concatenate([dscan_, jnp.zeros((dscan_.shape[0], RW), F32)], axis=1)
        grads = vjp(ct)
        dmixed = grads[0]
        return (dmixed, jnp.sum(dmixed * (shifted - cur), axis=0, keepdims=True)) + tuple(grads[1:])
    dmixed, d_mu, d_w0, d_lora, d_a0, d_gate_up, d_kk, d_ka = _rowwise(
        b_prep, [(proj, SHIFT_W), drw_post, dscan], [s['tshift_mu']] + prep_consts,
        [row(SHIFT_W), acc(SHIFT_W), acc(RW), acc(2 * RW, 128), acc(RW), acc(RW, 128), acc(RW), acc(RW)],
        tm=tm_vjp, name="b_prep", halo=[(proj, SHIFT_W, 'prev')])

    def b_gather(i, n, dm, dqkv_, dpg_, next8, mu):
        dcur = dm * (1.0 - mu) + _shift_up(dm, next8, i, n) * mu
        return (jnp.concatenate([dcur.astype(MXU), dqkv_, dpg_], axis=1),)
    (dproj,) = _rowwise(b_gather, [dmixed, dqkv, dpg], [s['tshift_mu']], [row(IN_W, MXU)], tm=tm_wide, name="b_gather",
                        halo=[(dmixed, SHIFT_W, 'next')])
    g_win = _mm_tn(h1, dproj, name="b_proj_dw", out_dtype=MXU, col_shards=4)

    def col_blocks(g):
        k, n = g.shape
        return g.reshape(k, 4, n // 4).transpose(1, 0, 2).astype(MXU)
    gw.update(w_in=g_win, decay_up=col_blocks(d_lora[:64, :RW]), iclr_up=col_blocks(d_lora[64:, RW:]),
              gate_up=col_blocks(d_gate_up))
    top, bottom = None, None
    if shards:
        top = _ScatterChips([gw['w_in'][:, :D // 2]] + [gw[n] for n in BACK_LAST[1:]])
        bottom = _ScatterChips([gw['w_in'][:, D // 2:]])
        dh1, *got_top = _mm_nt(dproj, w['w_in'], name="b_proj_dx", hosted=top)
    else:
        dh1 = _mm_nt(dproj, w['w_in'], name="b_proj_dx")

    def b_norm1(i, n, x_, dh1_, dx1_, g, ada_):
        _, vjp = jax.vjp(_norm_mod, x_, g, ada_[:, D:2 * D], ada_[:, 0:D])
        dxn, dg, dsc, dsh = vjp(dh1_)
        return dxn + dx1_, dg, dsc, dsh
    dx, d_gain1, d_scale1, d_shift1, *got_bottom = _rowwise(
        b_norm1, [x, dh1, dx1], [s['norm1_gain'], ada], [row(D), acc(D), acc(D), acc(D)], tm=tm, name="b_norm1",
        hosted=bottom)
    if shards:
        recv.update(zip(BACK_LAST[1:], got_top[1:]))
        recv['w_in'] = jnp.concatenate([got_top[0], got_bottom[0]], axis=1)

    d_ada = jnp.concatenate([d_shift1, d_scale1, dgate1, d_shift2, d_scale2, dgate2], axis=1)
    gs = dict(norm1_gain=d_gain1, norm2_gain=d_gain2, tshift_mu=d_mu, decay_w0=d_w0, iclr_a0=d_a0, k_k=d_kk, k_a=d_ka,
              r_k=d_r_k, lnx_gain=d_lnx_gain, lnx_bias=d_lnx_bias, q_norm_gain=d_qg, k_norm_gain=d_kg,
              attn_sinks=d_sinks, branch_gate_b=d_bias)
    return loss, dx, d_ada, gw, gs, recv


ANY = pl.BlockSpec(memory_space=pl.ANY)


def _place():
    x, y, c = lax.axis_index("x"), lax.axis_index("y"), lax.axis_index("c")
    return x, y, c, [(1 - x, y), (x, 1 - y), (1 - x, 1 - y)]


def _all_gather8(x_shard, *, name):
    m_per, n = x_shard.shape

    def body(x_ref, out_ref, send_sems, recv_sems, local_sem):
        x, y, c, chips = _place()
        me, sibling = (x, y, c), (x, y, 1 - c)

        def rows(px, py, pc):
            return out_ref.at[pl.ds((4 * px + 2 * py + pc) * m_per, m_per), :]

        def copy(k, block, to, src=None):
            return pltpu.make_async_remote_copy(
                src_ref=rows(*block) if src is None else src, dst_ref=rows(*block),
                send_sem=send_sems.at[k], recv_sem=recv_sems.at[k], device_id=to, device_id_type=MESH)

        mine = pltpu.make_async_copy(x_ref, rows(*me), local_sem)
        mine.start()
        first = [copy(0, me, sibling, src=x_ref)]
        first += [copy(1 + j, me, (*chip, c), src=x_ref) for j, chip in enumerate(chips)]
        for cp in first:
            cp.start()
        passed = [copy(4 + j, (*chip, c), sibling) for j, chip in enumerate(chips)]
        for j, chip in enumerate(chips):
            copy(1 + j, (*chip, c), me).wait_recv()
            passed[j].start()
        copy(0, sibling, me).wait_recv()
        for j, chip in enumerate(chips):
            copy(4 + j, (*chip, 1 - c), me).wait_recv()
        for cp in first + passed:
            cp.wait_send()
        mine.wait()

    return pl.pallas_call(
        body, name=name, out_shape=jax.ShapeDtypeStruct((8 * m_per, n), x_shard.dtype),
        in_specs=[pl.BlockSpec(memory_space=pltpu.VMEM)], out_specs=pl.BlockSpec(memory_space=pltpu.VMEM),
        scratch_shapes=[pltpu.SemaphoreType.DMA((7,)), pltpu.SemaphoreType.DMA((7,)), pltpu.SemaphoreType.DMA],
    )(x_shard)


class _GatherChips:
    def __init__(self, shards):
        n = len(shards)
        self.arrays, self.n_in, self.n_out = list(shards), n, n
        self.out_shape = [jax.ShapeDtypeStruct((4,) + s.shape, s.dtype) for s in shards]
        self.scratch = [pltpu.SemaphoreType.DMA((3 * n,)), pltpu.SemaphoreType.DMA((3 * n,)),
                        pltpu.SemaphoreType.DMA((n,))]

    def _copies(self, x_refs, out_refs, sems, receiving):
        send_sems, recv_sems, local_sems = sems
        x, y, c, chips = _place()
        s_me = 2 * x + y
        n = self.n_in

        def copy(a, k, s):
            return pltpu.make_async_remote_copy(
                src_ref=x_refs[a], dst_ref=out_refs[a].at[s], send_sem=send_sems.at[3 * a + k],
                recv_sem=recv_sems.at[3 * a + k], device_id=(*chips[k], c), device_id_type=MESH)

        mine = [pltpu.make_async_copy(x_refs[a], out_refs[a].at[s_me], local_sems.at[a]) for a in range(n)]
        sends = [copy(a, k, s_me) for a in range(n) for k in range(3)]
        if not receiving:
            return mine, sends
        return mine, sends, [copy(a, k, 2 * px + py) for a in range(n) for k, (px, py) in enumerate(chips)]

    def start(self, x_refs, out_refs, sems):
        mine, sends = self._copies(x_refs, out_refs, sems, False)
        for cp in mine + sends:
            cp.start()

    def wait(self, x_refs, out_refs, sems):
        mine, sends, recvs = self._copies(x_refs, out_refs, sems, True)
        for cp in recvs:
            cp.wait_recv()
        for cp in sends:
            cp.wait_send()
        for cp in mine:
            cp.wait()


class _ScatterChips:
    def __init__(self, parts):
        n = len(parts)
        self.arrays, self.n_in, self.n_out = list(parts), n, n
        self.out_shape = [jax.ShapeDtypeStruct((3,) + p.shape[1:], p.dtype) for p in parts]
        self.scratch = [pltpu.SemaphoreType.DMA((3 * n,)), pltpu.SemaphoreType.DMA((3 * n,))]

    def _copies(self, g_refs, out_refs, sems):
        send_sems, recv_sems = sems
        x, y, c, chips = _place()
        return [pltpu.make_async_remote_copy(
            src_ref=g_refs[a].at[2 * px + py], dst_ref=out_refs[a].at[k], send_sem=send_sems.at[3 * a + k],
            recv_sem=recv_sems.at[3 * a + k], device_id=(px, py, c), device_id_type=MESH)
            for a in range(self.n_in) for k, (px, py) in enumerate(chips)]

    def start(self, g_refs, out_refs, sems):
        for cp in self._copies(g_refs, out_refs, sems):
            cp.start()

    def wait(self, g_refs, out_refs, sems):
        sends = self._copies(g_refs, out_refs, sems)
        for cp in sends:
            cp.wait_recv()
        for cp in sends:
            cp.wait_send()


def _exchange_call(ex, *, name):
    def body(*refs):
        parts = (refs[:ex.n_in], refs[ex.n_in:ex.n_in + ex.n_out], refs[ex.n_in + ex.n_out:])
        ex.start(*parts)
        ex.wait(*parts)

    return pl.pallas_call(body, name=name, out_shape=ex.out_shape, in_specs=[ANY] * ex.n_in,
                          out_specs=[ANY] * ex.n_out, scratch_shapes=ex.scratch)(*ex.arrays)


def _swap_sibling(vs, *, name):
    n = len(vs)

    def body(*refs):
        v_refs, out_refs = refs[:n], refs[n:2 * n]
        send_sems, recv_sems = refs[2 * n:]
        x, y, c, _ = _place()
        cps = [pltpu.make_async_remote_copy(src_ref=v_refs[a], dst_ref=out_refs[a], send_sem=send_sems.at[a],
                                            recv_sem=recv_sems.at[a], device_id=(x, y, 1 - c), device_id_type=MESH)
               for a in range(n)]
        for cp in cps:
            cp.start()
        for cp in cps:
            cp.wait()

    return pl.pallas_call(
        body, name=name, out_shape=[jax.ShapeDtypeStruct(v.shape, v.dtype) for v in vs],
        in_specs=[ANY] * n, out_specs=[ANY] * n,
        scratch_shapes=[pltpu.SemaphoreType.DMA((n,)), pltpu.SemaphoreType.DMA((n,))],
    )(*vs)


def _sum_parts(own, others, *, name):
    R, C = own.shape
    tm = _pick(R, (256, 128, 64))

    def body(own_ref, o0_ref, o1_ref, o2_ref, out_ref):
        tot = own_ref[...].astype(F32)
        for ref in (o0_ref, o1_ref, o2_ref):
            tot = tot + ref[...].astype(F32)
        out_ref[...] = tot

    part = lambda k: pl.BlockSpec((None, tm, C), lambda i: (k, i, 0))
    return pl.pallas_call(
        body, name=name, grid=(R // tm,),
        in_specs=[pl.BlockSpec((tm, C), lambda i: (i, 0)), part(0), part(1), part(2)],
        out_specs=pl.BlockSpec((tm, C), lambda i: (i, 0)), out_shape=jax.ShapeDtypeStruct((R, C), F32),
        compiler_params=_cparams(("arbitrary",)),
    )(own, others, others, others)


def _adam_math(w_, m_, v_, g):
    m2 = ADAM_B1 * m_ + (1.0 - ADAM_B1) * g
    v2 = ADAM_B2 * v_ + (1.0 - ADAM_B2) * jnp.square(g)
    m_hat = m2 / (1.0 - ADAM_B1 ** ADAM_STEP)
    v_hat = v2 / (1.0 - ADAM_B2 ** ADAM_STEP)
    delta = -ADAM_LR * (m_hat / (jnp.sqrt(v_hat) + ADAM_EPS) + ADAM_WD * w_)
    return delta, m2, v2


SMALL_SLOTS = 16
SMALL_COLS = 6 * D


def _pack_small(grads, *, name):
    n = len(grads)

    def body(*refs):
        out_ref = refs[n]
        out_ref[...] = jnp.zeros_like(out_ref)
        for i, ref in enumerate(refs[:n]):
            out_ref[i:i + 1, 0:ref.shape[1]] = ref[...]

    return pl.pallas_call(body, name=name, out_shape=jax.ShapeDtypeStruct((SMALL_SLOTS, SMALL_COLS), F32))(*grads)


def _adamw_small(ws, ms, vs, gathered, *, name):
    n = len(ws)

    def body(*refs):
        w_refs, m_refs, v_refs, g_ref = refs[:n], refs[n:2 * n], refs[2 * n:3 * n], refs[3 * n]
        outs = refs[3 * n + 1:]
        for i in range(n):
            nc = w_refs[i].shape[1]
            g = g_ref[i:i + 1, 0:nc]
            for d in range(1, 8):
                g = g + g_ref[d * SMALL_SLOTS + i:d * SMALL_SLOTS + i + 1, 0:nc]
            delta, m2, v2 = _adam_math(w_refs[i][...], m_refs[i][...], v_refs[i][...], g)
            for k, val in enumerate((g, delta, m2, v2)):
                outs[k * n + i][...] = val

    shapes = [jax.ShapeDtypeStruct(w.shape, F32) for w in ws]
    res = pl.pallas_call(body, name=name, out_shape=shapes * 4,
                         compiler_params=pltpu.CompilerParams(vmem_limit_bytes=VMEM_LIMIT))(*ws, *ms, *vs, gathered)
    return [res[k * n:(k + 1) * n] for k in range(4)]


def _adamw(w, m, v, gparts, *, tm, name):
    def fn(i, n, w_, m_, v_, *gs):
        g = gs[0]
        for p in gs[1:]:
            g = g + p
        return (g,) + _adam_math(w_, m_, v_, g)
    nc = w.shape[1]
    return _rowwise(fn, [w, m, v] + list(gparts), [], [(nc, F32, 'row')] * 4, tm=tm, name=name)


WEIGHTS = ['ada_w', 'ada_b', 'norm1_gain', 'norm2_gain', 'w_in', 'tshift_mu', 'decay_w0', 'decay_up', 'iclr_a0',
           'iclr_up', 'gate_up', 'k_k', 'k_a', 'r_k', 'lnx_gain', 'lnx_bias', 'q_norm_gain', 'k_norm_gain', 'attn_sinks',
           'branch_gate_b', 'w_branch_a', 'w_branch_b', 'w_out', 'ffn_w1', 'ffn_w3', 'ffn_w2']
SHARDED = [('w_in', 1), ('decay_up', 1), ('iclr_up', 1), ('gate_up', 1), ('w_branch_a', 1), ('w_branch_b', 1),
           ('w_out', 0), ('ffn_w1', 1), ('ffn_w3', 1), ('ffn_w2', 0)]
SMALL = ['ada_b', 'norm1_gain', 'norm2_gain', 'tshift_mu', 'decay_w0', 'iclr_a0', 'k_k', 'k_a', 'r_k', 'lnx_gain',
         'lnx_bias', 'q_norm_gain', 'k_norm_gain', 'attn_sinks', 'branch_gate_b']


def kernel(x, c, positions, ada_w, ada_b, norm1_gain, norm2_gain, w_in, tshift_mu, decay_w0, decay_up, iclr_a0, iclr_up, gate_up, k_k, k_a, r_k, lnx_gain, lnx_bias, q_norm_gain, k_norm_gain, attn_sinks, branch_gate_b, w_branch_a, w_branch_b, w_out, ffn_w1, ffn_w3, ffn_w2, loss_target, m_ada_w, m_ada_b, m_norm1_gain, m_norm2_gain, m_w_in, m_tshift_mu, m_decay_w0, m_decay_up, m_iclr_a0, m_iclr_up, m_gate_up, m_k_k, m_k_a, m_r_k, m_lnx_gain, m_lnx_bias, m_q_norm_gain, m_k_norm_gain, m_attn_sinks, m_branch_gate_b, m_w_branch_a, m_w_branch_b, m_w_out, m_ffn_w1, m_ffn_w3, m_ffn_w2, v_ada_w, v_ada_b, v_norm1_gain, v_norm2_gain, v_w_in, v_tshift_mu, v_decay_w0, v_decay_up, v_iclr_a0, v_iclr_up, v_gate_up, v_k_k, v_k_a, v_r_k, v_lnx_gain, v_lnx_bias, v_q_norm_gain, v_k_norm_gain, v_attn_sinks, v_branch_gate_b, v_w_branch_a, v_w_branch_b, v_w_out, v_ffn_w1, v_ffn_w3, v_ffn_w2):
    a = dict(locals())
    W = {n: a[n] for n in WEIGHTS}
    M = {n: a['m_' + n] for n in WEIGHTS}
    V = {n: a['v_' + n] for n in WEIGHTS}
    xi, yi, ci = lax.axis_index("x"), lax.axis_index("y"), lax.axis_index("c")
    me = 4 * xi + 2 * yi + ci
    shard = 2 * xi + yi
    mat = lambda t: t.reshape(t.shape[-2], t.shape[-1])
    sharded = [n for n, _ in SHARDED]

    ax = dict(SHARDED)
    late = LATE
    early = [n for n in sharded if n not in late]
    shards = {n: mat(W[n]).astype(MXU) for n in sharded}
    gathered = _exchange_call(_GatherChips([shards[n] for n in early]), name="gather_weights")
    full = {n: _full_weight(g, ax[n]) for n, g in zip(early, gathered, strict=True)}

    c_all = _all_gather8(jnp.broadcast_to(c, (8, D)), name="gather_c")[0::8]
    pad_rows = lambda t: jnp.concatenate([t, jnp.zeros((BLK - 8, t.shape[1]), t.dtype)])
    c_all = pad_rows(c_all.astype(MXU))
    ada_cols = _mm_nn(c_all, mat(ada_w).astype(MXU), name="f_ada")[:8]
    ada_all = _all_gather8(ada_cols, name="gather_ada").reshape(2, 2, 2, 8, 6 * D // 4)
    ada_mine = lax.dynamic_index_in_dim(ada_all[:, :, 0], me, axis=2, keepdims=False)
    ada = ada_mine.reshape(1, 6 * D) + mat(ada_b)

    zero = jnp.zeros((64, RW), MXU)
    lora = jnp.concatenate([jnp.concatenate([full['decay_up'], zero], axis=1),
                            jnp.concatenate([zero, full['iclr_up']], axis=1)], axis=0)
    s = {n: W[n].reshape(1, -1) for n in SMALL if n != 'ada_b'}
    s['lora_up'] = lora.astype(F32)
    s['gate_up'] = full['gate_up'].astype(F32)
    tab = _rope_table(positions.reshape(-1))
    loss, dx, d_ada, gw, gs, from_chips = _local_step(x[0], loss_target[0], ada, tab, dict(w_in=full['w_in']), s,
                                                      shards={n: shards[n] for n in late})
    loss = lax.psum(loss[0, 0], ("x", "y", "c"))

    gs['ada_b'] = d_ada
    gsmall = _pack_small([gs[n] for n in SMALL], name="pack_small_grads")
    gsmall_all = _all_gather8(gsmall, name="gather_small_grads")
    row = lambda src: [src[n].reshape(1, -1) for n in SMALL]
    sm_out = _adamw_small(row(W), row(M), row(V), gsmall_all, name="adamw_small")
    sm_out = [{n: o.reshape(W[n].shape) for n, o in zip(SMALL, outs_k, strict=True)} for outs_k in sm_out]

    d_ada_all = gsmall_all[0::SMALL_SLOTS]
    d_ada_cols = lax.dynamic_slice_in_dim(d_ada_all, shard * (6 * D // 4), 6 * D // 4, axis=1)
    g_ada_w = _mm_tn(c_all, pad_rows(d_ada_cols.astype(MXU)), name="b_ada")
    ada_out = _adamw(mat(ada_w), mat(m_ada_w), mat(v_ada_w), [g_ada_w], tm=256, name="adamw_ada")

    parts =[_sum_parts(lax.dynamic_index_in_dim(gw[n], shard, axis=0, keepdims=False), from_chips[n], name="sum_" + n)
             for n in sharded]
    others = _swap_sibling(parts, name="swap_grads")
    sh_out = {}
    for n, part, other in zip(sharded, parts, others, strict=True):
        sh_out[n] = _adamw(mat(W[n]), mat(M[n]), mat(V[n]), [part, other], tm=_pick(part.shape[0], (256, 128, 64)),
                           name="adamw_" + n)

    def leaf(k, n):
        if n == 'ada_w':
            return ada_out[k].reshape(W[n].shape)
        if n in sharded:
            return sh_out[n][k].reshape(W[n].shape)
        return sm_out[k][n]
    outs = [leaf(k, n) for k in range(4) for n in WEIGHTS]
    return (loss, dx[None], *outs)
```

```python
import functools
import math

import jax
import jax.numpy as jnp
from jax import lax
from jax.experimental import pallas as pl
from jax.experimental.pallas import tpu as pltpu

F32 = jnp.float32
BF16 = jnp.bfloat16
MXU = BF16
HI = lax.Precision.HIGHEST

D = 1024
HD = 64
NH = 8
RW = NH * HD
SHIFT_W = 3 * RW + 64 + 64 + 128
QKV_W = RW + 2 * 128
GATE_W = 2 * D
IN_W = SHIFT_W + QKV_W + GATE_W
DFF = 2816
BLK = 128
CHUNK = 64
RMS_EPS = 1e-6
GN_EPS = 64e-5
NEG_INF = -1e30
ADAM_LR, ADAM_B1, ADAM_B2, ADAM_EPS, ADAM_WD, ADAM_STEP = 0.001, 0.9, 0.999, 1e-08, 0.01, 10
VMEM_LIMIT = 56 * 1024 * 1024
MESH = pl.DeviceIdType.MESH


def _cparams(sem=None):
    return pltpu.CompilerParams(dimension_semantics=sem, vmem_limit_bytes=VMEM_LIMIT)


def _full_spec(a):
    nd = a.ndim
    return pl.BlockSpec(a.shape, lambda *_: (0,) * nd)


def _rowwise(fn, rows, consts, outs, *, tm, name, halo=(), hosted=None):
    rows = [(a + (0,))[:3] if isinstance(a, tuple) else (a, a.shape[1], 0) for a in rows]
    T = rows[0][0].shape[0]
    assert T % tm == 0 and tm % 8 == 0
    n_tiles = T // tm
    n_in = len(rows) + len(halo) + len(consts)
    in_specs = [pl.BlockSpec((tm, nc), lambda i, j=j: (i, j)) for _, nc, j in rows]
    args = [a for a, _, _ in rows]
    for a, nc, kind in halo:
        if kind == 'prev':
            in_specs.append(pl.BlockSpec((8, nc), lambda i: (jnp.maximum(i * (tm // 8) - 1, 0), 0)))
        else:
            in_specs.append(pl.BlockSpec((8, nc), lambda i: (jnp.minimum((i + 1) * (tm // 8), T // 8 - 1), 0)))
        args.append(a)
    in_specs += [_full_spec(a) for a in consts]
    args += list(consts)
    out_shape, out_specs = [], []
    for ncols, dtype, kind in outs:
        if kind == 'row':
            out_shape.append(jax.ShapeDtypeStruct((T, ncols), dtype))
            out_specs.append(pl.BlockSpec((tm, ncols), lambda i: (i, 0)))
        else:
            out_shape.append(jax.ShapeDtypeStruct((kind, ncols), dtype))
            out_specs.append(pl.BlockSpec((kind, ncols), lambda i: (0, 0)))

    def body(*refs):
        i = pl.program_id(0)
        vals = [r[...] for r in refs[:n_in]]
        res = fn(i, n_tiles, *vals)
        for (ncols, dtype, kind), o_ref, val in zip(outs, refs[n_in:], res, strict=True):
            if kind == 'row':
                o_ref[...] = val.astype(dtype)
            else:
                @pl.when(i == 0)
                def _():
                    o_ref[...] = jnp.zeros_like(o_ref)
                o_ref[...] += val.astype(dtype)

    h_in, h_in_specs, h_out_specs, h_out_shape, h_scratch = _hosted_args(hosted)
    res = pl.pallas_call(
        _hosting(body, hosted, n_in, len(outs), 0, n_tiles), name=name, grid=(n_tiles,),
        in_specs=in_specs + h_in_specs, out_specs=out_specs + h_out_specs, out_shape=out_shape + h_out_shape,
        scratch_shapes=h_scratch, compiler_params=_cparams(("arbitrary",)),
    )(*args, *h_in)
    return res


def _pick(n, cands):
    for c in cands:
        if n % c == 0:
            return c
    return n


MM_ROWS = (1024, 512, 256, 128)
MM_COLS = (1536, 1408, 1024, 896, 768, 512, 256, 128)
MM_WIDE = 3000


def _mm_nn(a, w, *, name, out_dtype=F32, hosted=None):
    T, K = a.shape
    N = w.shape[1]
    tm = _pick(T, MM_ROWS)
    tn = _pick(N, MM_COLS)
    grid = (N // tn, T // tm)
    h_in, h_in_specs, h_out_specs, h_out_shape, h_scratch = _hosted_args(hosted)

    def body(a_ref, w_ref, o_ref):
        o_ref[...] = jnp.dot(a_ref[...], w_ref[...], preferred_element_type=F32).astype(out_dtype)

    res = pl.pallas_call(
        _hosting(body, hosted, 2, 1, 0, grid), name=name, grid=grid,
        in_specs=[pl.BlockSpec((tm, K), lambda j, i: (i, 0)), pl.BlockSpec((K, tn), lambda j, i: (0, j))] + h_in_specs,
        out_specs=[pl.BlockSpec((tm, tn), lambda j, i: (i, j))] + h_out_specs,
        out_shape=[jax.ShapeDtypeStruct((T, N), out_dtype)] + h_out_shape, scratch_shapes=h_scratch,
        compiler_params=_cparams(("arbitrary", "arbitrary")),
    )(a, w, *h_in)
    return res if hosted else res[0]


def _mm_nt(dy, w, *, name, out_dtype=F32, hosted=None):
    T, N = dy.shape
    K = w.shape[0]
    tm = _pick(T, MM_ROWS if N <= MM_WIDE else MM_ROWS[1:])
    tk = _pick(K, MM_COLS[1:])
    grid = (K // tk, T // tm)
    h_in, h_in_specs, h_out_specs, h_out_shape, h_scratch = _hosted_args(hosted)

    def body(dy_ref, w_ref, o_ref):
        o_ref[...] = lax.dot_general(dy_ref[...], w_ref[...], (((1,), (1,)), ((), ())),
                                     preferred_element_type=F32).astype(out_dtype)

    res = pl.pallas_call(
        _hosting(body, hosted, 2, 1, 0, grid), name=name, grid=grid,
        in_specs=[pl.BlockSpec((tm, N), lambda j, i: (i, 0)), pl.BlockSpec((tk, N), lambda j, i: (j, 0))] + h_in_specs,
        out_specs=[pl.BlockSpec((tm, tk), lambda j, i: (i, j))] + h_out_specs,
        out_shape=[jax.ShapeDtypeStruct((T, K), out_dtype)] + h_out_shape, scratch_shapes=h_scratch,
        compiler_params=_cparams(("arbitrary", "arbitrary")),
    )(dy, w, *h_in)
    return res if hosted else res[0]


def _mm_tn(a, dy, *, name, out_dtype=F32, col_shards=None):
    T, K = a.shape
    N = dy.shape[1]
    tm = _pick(T, MM_ROWS)
    tn = N // col_shards if col_shards else _pick(N, MM_COLS[1:])
    n_t = T // tm

    def body(a_ref, dy_ref, o_ref, acc_ref):
        i = pl.program_id(1)

        @pl.when(i == 0)
        def _():
            acc_ref[...] = jnp.zeros_like(acc_ref)

        acc_ref[...] += lax.dot_general(a_ref[...], dy_ref[...], (((0,), (0,)), ((), ())), preferred_element_type=F32)

        @pl.when(i == n_t - 1)
        def _():
            o_ref[...] = acc_ref[...].astype(out_dtype)

    if col_shards:
        out_specs = pl.BlockSpec((None, K, tn), lambda j, i: (j, 0, 0))
        out_shape = jax.ShapeDtypeStruct((col_shards, K, tn), out_dtype)
    else:
        out_specs = pl.BlockSpec((K, tn), lambda j, i: (0, j))
        out_shape = jax.ShapeDtypeStruct((K, N), out_dtype)
    return pl.pallas_call(
        body, name=name, grid=(N // tn, n_t),
        in_specs=[pl.BlockSpec((tm, K), lambda j, i: (i, 0)), pl.BlockSpec((tm, tn), lambda j, i: (i, j))],
        out_specs=out_specs, out_shape=out_shape, scratch_shapes=[pltpu.VMEM((K, tn), F32)],
        compiler_params=_cparams(("arbitrary", "arbitrary")),
    )(a, dy)


def _seg_ones(n):
    r = lax.broadcasted_iota(jnp.int32, (n, n), 0) // HD
    c = lax.broadcasted_iota(jnp.int32, (n, n), 1) // HD
    return (r == c).astype(F32)


def _segsum_raw(x):
    ones = _seg_ones(x.shape[1])
    if MXU == F32:
        return jnp.dot(x, ones, precision=HI, preferred_element_type=F32)
    hi = x.astype(MXU)
    lo = (x - hi.astype(F32)).astype(MXU)
    ones = ones.astype(MXU)
    return jnp.dot(hi, ones, preferred_element_type=F32) + jnp.dot(lo, ones, preferred_element_type=F32)


@jax.custom_vjp
def _segsum(x):
    return _segsum_raw(x)


def _segsum_fwd(x):
    return _segsum_raw(x), None


def _segsum_bwd(_, g):
    return (_segsum_raw(g),)


_segsum.defvjp(_segsum_fwd, _segsum_bwd)


def _mxu(x):
    return x.astype(MXU)


@jax.custom_vjp
def _bdot(a, b):
    return jnp.dot(_mxu(a), _mxu(b), preferred_element_type=F32)


def _bdot_fwd(a, b):
    return _bdot(a, b), (a, b)


def _bdot_bwd(res, g):
    a, b = res
    da = lax.dot_general(_mxu(g), _mxu(b), (((1,), (1,)), ((), ())), preferred_element_type=F32)
    db = lax.dot_general(_mxu(a), _mxu(g), (((0,), (0,)), ((), ())), preferred_element_type=F32)
    return da.astype(a.dtype), db.astype(b.dtype)


_bdot.defvjp(_bdot_fwd, _bdot_bwd)


@jax.custom_vjp
def _bdot_nt(a, b):
    return lax.dot_general(_mxu(a), _mxu(b), (((1,), (1,)), ((), ())), preferred_element_type=F32)


def _bdot_nt_fwd(a, b):
    return _bdot_nt(a, b), (a, b)


def _bdot_nt_bwd(res, g):
    a, b = res
    da = jnp.dot(_mxu(g), _mxu(b), preferred_element_type=F32)
    db = lax.dot_general(_mxu(g), _mxu(a), (((0,), (0,)), ((), ())), preferred_element_type=F32)
    return da.astype(a.dtype), db.astype(b.dtype)


_bdot_nt.defvjp(_bdot_nt_fwd, _bdot_nt_bwd)


def _sigmoid(x):
    return 1.0 / (1.0 + jnp.exp(-x))


def _softplus(x):
    return jnp.maximum(x, 0.0) + jnp.log(1.0 + jnp.exp(jnp.minimum(x, -x)))


def _norm_mod(x, gain, scale, shift):
    inv = lax.rsqrt(jnp.mean(x * x, axis=-1, keepdims=True) + RMS_EPS)
    return (x * inv) * gain * (1.0 + scale) + shift


def _prep(mixed, decay_w0, lora_up, iclr_a0, gate_up, k_k, k_a):
    r = mixed[:, 0:RW]
    k = mixed[:, RW:2 * RW]
    v = mixed[:, 2 * RW:3 * RW]
    z = mixed[:, 3 * RW:3 * RW + 128]
    xg = mixed[:, 3 * RW + 128:]
    lane = lax.broadcasted_iota(jnp.int32, z.shape, 1)
    tz = jnp.where(lane < 64, jnp.tanh(z), z)
    lo = _bdot(tz, lora_up)
    w_log = -_softplus(-(decay_w0 + lo[:, :RW])) - 0.5
    lw = -jnp.exp(w_log)
    a_ic = _sigmoid(iclr_a0 + lo[:, RW:])
    g = _bdot(_sigmoid(xg), gate_up)
    kk = k * k_k
    kk = kk / jnp.maximum(jnp.sqrt(_segsum(kk * kk)), 1e-12)
    k_mod = k * (1.0 + (a_ic - 1.0) * k_a)
    return jnp.concatenate([r, lw, k_mod, v, -kk, kk * a_ic, g], axis=1)


def _post(y, r, k, v, g, lnx_gain, lnx_bias, r_k):
    mu = _segsum(y) * (1.0 / HD)
    yc = y - mu
    var = _segsum(yc * yc) * (1.0 / HD)
    yn = yc * lax.rsqrt(var + GN_EPS) * lnx_gain + lnx_bias
    bonus = _segsum(r * k * r_k) * v
    return (yn + bonus) * g


def _merge(pg, ma, mb, bias):
    gates = _sigmoid(pg + bias)
    return gates[:, :D] * ma + gates[:, D:] * mb


def _swiglu(u, v):
    return u * _sigmoid(u) * v


@functools.partial(jax.custom_vjp, nondiff_argnums=(1,))
def _lane_roll(x, s):
    return pltpu.roll(x, s, 1)


def _lane_roll_fwd(x, s):
    return pltpu.roll(x, s, 1), None


def _lane_roll_bwd(s, _, g):
    n = g.shape[1]
    return (pltpu.roll(g, (n - s) % n, 1),)


_lane_roll.defvjp(_lane_roll_fwd, _lane_roll_bwd)


def _rope(x, cos, sin_lo, sin_hi):
    n = x.shape[1]
    return x * cos + _lane_roll(x, n - 8) * sin_lo + _lane_roll(x, 8) * sin_hi


def _head_rms(x, gain):
    return x * lax.rsqrt(_segsum(x * x) * (1.0 / HD) + RMS_EPS) * gain


def _attn_block(qkv_c, qkv_p, tab_c, tab_p, qg, kg, sinks, first):
    def tabs(tab, n):
        return tab[:, 0:n], tab[:, RW:RW + n], tab[:, 2 * RW:2 * RW + n]

    qg = jnp.concatenate([qg] * NH, axis=1)
    kg = jnp.concatenate([kg] * 2, axis=1)
    q = _rope(_head_rms(qkv_c[:, :RW], qg), *tabs(tab_c, RW))
    k_c = _rope(_head_rms(qkv_c[:, RW:RW + 128], kg), *tabs(tab_c, 128))
    k_p = _rope(_head_rms(qkv_p[:, RW:RW + 128], kg), *tabs(tab_p, 128))
    kband = jnp.concatenate([k_p, k_c], axis=0)
    vband = jnp.concatenate([qkv_p[:, RW + 128:], qkv_c[:, RW + 128:]], axis=0)
    G = 4
    qi = lax.broadcasted_iota(jnp.int32, (G * BLK, 2 * BLK), 0) % BLK
    kj = lax.broadcasted_iota(jnp.int32, (G * BLK, 2 * BLK), 1)
    dist = qi + BLK - kj
    valid = (dist >= 0) & (dist < BLK) & (jnp.logical_not(first) | (kj >= BLK))
    row_g = lax.broadcasted_iota(jnp.int32, (G * BLK, 1), 0) // BLK
    outs = []
    for kvh in range(2):
        kb = kband[:, kvh * HD:(kvh + 1) * HD]
        vb = vband[:, kvh * HD:(kvh + 1) * HD]
        qs = jnp.concatenate([q[:, (G * kvh + g) * HD:(G * kvh + g + 1) * HD] for g in range(G)], axis=0)
        s = _bdot_nt(qs, kb) * (HD ** -0.5)
        s = jnp.where(valid, s, NEG_INF)
        sink = jnp.zeros((G * BLK, 1), F32)
        for g in range(G):
            sink = jnp.where(row_g == g, sinks[:, G * kvh + g:G * kvh + g + 1], sink)
        m = lax.stop_gradient(jnp.maximum(jnp.max(s, axis=-1, keepdims=True), sink))
        e = jnp.exp(s - m)
        p = e / (jnp.sum(e, axis=-1, keepdims=True) + jnp.exp(sink - m))
        o = _bdot(p, vb)
        outs += [o[g * BLK:(g + 1) * BLK] for g in range(G)]
    return jnp.concatenate(outs, axis=1)


def _heads(x):
    return jnp.stack([x[:, h * HD:(h + 1) * HD] for h in range(NH)], axis=0)


def _unheads(x):
    return jnp.concatenate([x[h] for h in range(NH)], axis=1)


def _split(x, n):
    parts, rest = [], x
    for _ in range(n):
        p = rest.astype(MXU)
        parts.append(p)
        rest = rest - p.astype(F32)
    return parts


def _bdot_batched(a, b, ca, cb):
    return lax.dot_general(a, b, (((ca,), (cb,)), ((0,), (0,))), preferred_element_type=F32)


def _bmm_passes(a, b, ca, cb, passes):
    if MXU == F32:
        return lax.dot_general(a, b, (((ca,), (cb,)), ((0,), (0,))), precision=HI, preferred_element_type=F32)
    if passes == 1:
        return _bdot_batched(a.astype(MXU), b.astype(MXU), ca, cb)
    (a0, a1), (b0, b1) = _split(a, 2), _split(b, 2)
    return _bdot_batched(a0, b0, ca, cb) + (_bdot_batched(a0, b1, ca, cb) + _bdot_batched(a1, b0, ca, cb))


@functools.partial(jax.custom_vjp, nondiff_argnums=(2, 3, 4))
def _bmm(a, b, ca, cb, passes=1):
    return _bmm_passes(a, b, ca, cb, passes)


def _bmm_fwd(a, b, ca, cb, passes):
    return _bmm_passes(a, b, ca, cb, passes), (a, b)


def _bmm_bwd(ca, cb, passes, res, g):
    a, b = res
    if (ca, cb) == (2, 1):
        return _bmm_passes(g, b, 2, 2, passes), _bmm_passes(a, g, 1, 1, passes)
    if (ca, cb) == (2, 2):
        return _bmm_passes(g, b, 2, 1, passes), _bmm_passes(g, a, 1, 1, passes)
    return _bmm_passes(b, g, 2, 2, passes), _bmm_passes(a, g, 2, 1, passes)


_bmm.defvjp(_bmm_fwd, _bmm_bwd)


def _tri_dot(x, transpose):
    C = x.shape[1]
    ri = lax.broadcasted_iota(jnp.int32, (C, C), 0)
    ci = lax.broadcasted_iota(jnp.int32, (C, C), 1)
    tri = jnp.broadcast_to(((ri <= ci) if transpose else (ri >= ci)).astype(MXU), (x.shape[0], C, C))
    if MXU == F32:
        return lax.dot_general(tri, x, (((2,), (1,)), ((0,), (0,))), precision=HI, preferred_element_type=F32)
    p0, p1, p2 = _split(x, 3)
    return _bdot_batched(tri, p0, 2, 1) + (_bdot_batched(tri, p1, 2, 1) + _bdot_batched(tri, p2, 2, 1))


@jax.custom_vjp
def _cumsum_rows(x):
    return _tri_dot(x, False)


def _cumsum_rows_fwd(x):
    return _tri_dot(x, False), None


def _cumsum_rows_bwd(_, g):
    return (_tri_dot(g, True),)


_cumsum_rows.defvjp(_cumsum_rows_fwd, _cumsum_rows_bwd)

P_SCORE = 1
P_SOLVE = 1
P_STATE = 1
SCAN_CHUNKS = (4, 2, 1)


def _neumann(l):
    C = l.shape[1]
    eye = (lax.broadcasted_iota(jnp.int32, (C, C), 0) == lax.broadcasted_iota(jnp.int32, (C, C), 1)).astype(F32)
    x, lp = eye + l, l
    for _ in range(int(math.log2(C)) - 1):
        lp = _bmm(lp, lp, 2, 1, P_SOLVE)
        x = x + _bmm(x, lp, 2, 1, P_SOLVE)
    return x


@jax.custom_vjp
def _unit_lower_inverse(l):
    return _neumann(l)


def _unit_lower_inverse_fwd(l):
    x = _neumann(l)
    return x, x


def _unit_lower_inverse_bwd(x, g):
    return (_bmm(_bmm(x, g, 1, 1, P_SOLVE), x, 2, 2, P_SOLVE),)


_unit_lower_inverse.defvjp(_unit_lower_inverse_fwd, _unit_lower_inverse_bwd)


def _known_inverse(x):
    @jax.custom_vjp
    def f(l):
        return x

    f.defvjp(lambda l: (x, None), lambda _, g: (_bmm(_bmm(x, g, 1, 1, P_SOLVE), x, 2, 2, P_SOLVE),))
    return f


def _chunk(S0, r, lw, k, v, a, b, inverse=None):
    C = CHUNK
    n = r.shape[1] // C
    fold = lambda t: t.reshape(NH * n, C, HD)
    r, lw, k, v, a, b = (fold(t) for t in (r, lw, k, v, a, b))
    ri = lax.broadcasted_iota(jnp.int32, (C, C), 0)
    ci = lax.broadcasted_iota(jnp.int32, (C, C), 1)
    incl = (ri >= ci)
    strict = (ri > ci)
    eye = (ri == ci).astype(F32)
    cum = _cumsum_rows(lw)
    p_in = jnp.exp(cum)
    p_ex = jnp.exp(cum - lw)
    p_inv = jnp.exp(-cum)
    at, rt, bt, kt = a * p_ex, r * p_in, b * p_inv, k * p_inv
    lhs = jnp.concatenate([at, rt], axis=1)
    rhs_ = jnp.concatenate([bt, kt], axis=1)
    sc = _bmm(lhs, rhs_, 2, 2, P_SCORE)
    a_ab = jnp.where(strict, sc[:, :C, :C], 0.0)
    a_ak = jnp.where(strict, sc[:, :C, C:], 0.0)
    incl2 = (lax.broadcasted_iota(jnp.int32, (C, 2 * C), 0) >= lax.broadcasted_iota(jnp.int32, (C, 2 * C), 1) % C)
    a_r = jnp.where(incl2, sc[:, C:, :], 0.0)
    av = _bmm(a_ak, v, 2, 1, P_SCORE)
    x = x_all = (_unit_lower_inverse if inverse is None else _known_inverse(inverse))(a_ab)
    p_last = jnp.exp(cum[:, C - 1:C, :])
    per_chunk = lambda t: t.reshape((NH, n) + t.shape[1:])
    lhs, rhs_, a_r, av, x, v, p_last = (per_chunk(t) for t in (lhs, rhs_, a_r, av, x, v, p_last))
    S, ys = S0, []
    for c in range(n):
        s0 = _bmm(lhs[:, c], S, 2, 2, P_STATE)
        u = _bmm(x[:, c], s0[:, :C] + av[:, c], 2, 1, P_SOLVE)
        uv = jnp.concatenate([u, v[:, c]], axis=1)
        ys.append(s0[:, C:] + _bmm(a_r[:, c], uv, 2, 1, P_SCORE))
        S = (S + _bmm(uv, rhs_[:, c], 1, 1, P_STATE)) * p_last[:, c]
    return jnp.concatenate(ys, axis=1), S, x_all


def _hosting(body, ex, n_in, n_out, n_scratch, n_steps):
    if ex is None:
        return body

    def wrapped(*refs):
        a = n_in
        b = a + ex.n_in
        c = b + n_out
        d = c + ex.n_out
        e = d + n_scratch
        ex_refs = (refs[a:b], refs[c:d], refs[e:])
        grid = n_steps if isinstance(n_steps, tuple) else (n_steps,)
        first = last = True
        for ax_, size in enumerate(grid):
            first = first & (pl.program_id(ax_) == 0)
            last = last & (pl.program_id(ax_) == size - 1)

        @pl.when(first)
        def _():
            ex.start(*ex_refs)

        body(*refs[:a], *refs[b:c], *refs[d:e])

        @pl.when(last)
        def _():
            ex.wait(*ex_refs)

    return wrapped


def _hosted_args(ex):
    if ex is None:
        return [], [], [], [], []
    any_spec = pl.BlockSpec(memory_space=pl.ANY)
    return list(ex.arrays), [any_spec] * ex.n_in, [any_spec] * ex.n_out, list(ex.out_shape), list(ex.scratch)


def _scan_fwd(rw, *, name, hosted=None):
    T = rw.shape[0]
    nc = _pick(T // CHUNK, SCAN_CHUNKS)
    rows = CHUNK * nc
    n = T // rows
    h_in, h_in_specs, h_out_specs, h_out_shape, h_scratch = _hosted_args(hosted)

    def body(r_ref, lw_ref, k_ref, v_ref, a_ref, b_ref, y_ref, ck_ref, inv_ref, s_ref):
        @pl.when(pl.program_id(0) == 0)
        def _():
            s_ref[...] = jnp.zeros_like(s_ref)

        S0 = s_ref[...]
        ck_ref[0] = S0
        y, S1, inv = _chunk(S0, *[_heads(ref[...]) for ref in (r_ref, lw_ref, k_ref, v_ref, a_ref, b_ref)])
        y_ref[...] = _unheads(y)
        inv_ref[0] = inv
        s_ref[...] = S1

    col = lambda j: pl.BlockSpec((rows, RW), lambda i: (i, j))
    return pl.pallas_call(
        _hosting(body, hosted, 6, 3, 1, n), name=name, grid=(n,),
        in_specs=[col(j) for j in range(6)] + h_in_specs,
        out_specs=[pl.BlockSpec((rows, RW), lambda i: (i, 0)),
                   pl.BlockSpec((1, NH, HD, HD), lambda i: (i, 0, 0, 0)),
                   pl.BlockSpec((1, NH * nc, CHUNK, CHUNK), lambda i: (i, 0, 0, 0))] + h_out_specs,
        out_shape=[jax.ShapeDtypeStruct((T, RW), F32), jax.ShapeDtypeStruct((n, NH, HD, HD), F32),
                   jax.ShapeDtypeStruct((n, NH * nc, CHUNK, CHUNK), F32)] + h_out_shape,
        scratch_shapes=[pltpu.VMEM((NH, HD, HD), F32)] + h_scratch,
        compiler_params=_cparams(("arbitrary",)),
    )(rw, rw, rw, rw, rw, rw, *h_in)


def _scan_bwd(rw, ck, inv, dy, *, name, hosted=None):
    T = rw.shape[0]
    nc = _pick(T // CHUNK, SCAN_CHUNKS)
    rows = CHUNK * nc
    n = T // rows

    def body(r_ref, lw_ref, k_ref, v_ref, a_ref, b_ref, ck_ref, inv_ref, dy_ref, o_ref, ds_ref):
        @pl.when(pl.program_id(0) == 0)
        def _():
            ds_ref[...] = jnp.zeros_like(ds_ref)

        prim = [_heads(ref[...]) for ref in (r_ref, lw_ref, k_ref, v_ref, a_ref, b_ref)]
        known = inv_ref[0]
        _, vjp = jax.vjp(lambda S0, *p: _chunk(S0, *p, inverse=known)[:2], ck_ref[0], *prim)
        grads = vjp((_heads(dy_ref[...]), ds_ref[...]))
        ds_ref[...] = grads[0]
        o_ref[...] = jnp.concatenate([_unheads(g) for g in grads[1:]], axis=1)

    h_in, h_in_specs, h_out_specs, h_out_shape, h_scratch = _hosted_args(hosted)
    col = lambda j: pl.BlockSpec((rows, RW), lambda i: (n - 1 - i, j))
    return pl.pallas_call(
        _hosting(body, hosted, 9, 1, 1, n), name=name, grid=(n,),
        in_specs=[col(j) for j in range(6)] + [pl.BlockSpec((1, NH, HD, HD), lambda i: (n - 1 - i, 0, 0, 0)),
                                               pl.BlockSpec((1, NH * nc, CHUNK, CHUNK), lambda i: (n - 1 - i, 0, 0, 0)),
                                               pl.BlockSpec((rows, RW), lambda i: (n - 1 - i, 0))] + h_in_specs,
        out_specs=[pl.BlockSpec((rows, 6 * RW), lambda i: (n - 1 - i, 0))] + h_out_specs,
        out_shape=[jax.ShapeDtypeStruct((T, 6 * RW), F32)] + h_out_shape,
        scratch_shapes=[pltpu.VMEM((NH, HD, HD), F32)] + h_scratch,
        compiler_params=_cparams(("arbitrary",)),
    )(rw, rw, rw, rw, rw, rw, ck, inv, dy, *h_in)


ATTN_BLOCKS = (4, 2, 1)

def _attn_fwd(qkv, tab, qg, kg, sinks, *, name, hosted=None):
    T = qkv.shape[0]
    nb = _pick(T // BLK, ATTN_BLOCKS)
    n = T // (BLK * nb)
    h_in, h_in_specs, h_out_specs, h_out_shape, h_scratch = _hosted_args(hosted)

    def body(c_ref, p_ref, tc_ref, tp_ref, qg_ref, kg_ref, s_ref, o_ref):
        for b in range(nb):
            rows = slice(b * BLK, (b + 1) * BLK)
            before = slice((b - 1) * BLK, b * BLK)
            p, tp = (p_ref[...], tp_ref[...]) if b == 0 else (c_ref[before, :], tc_ref[before, :])
            first = (pl.program_id(0) == 0) if b == 0 else False
            o_ref[rows, :] = _attn_block(c_ref[rows, :], p, tc_ref[rows, :], tp, qg_ref[...], kg_ref[...],
                                         s_ref[...], first).astype(o_ref.dtype)

    cur = lambda w: pl.BlockSpec((nb * BLK, w), lambda i: (i, 0))
    prev = lambda w: pl.BlockSpec((BLK, w), lambda i: (jnp.maximum(i * nb - 1, 0), 0))
    return pl.pallas_call(
        _hosting(body, hosted, 7, 1, 0, n), name=name, grid=(n,),
        in_specs=[cur(QKV_W), prev(QKV_W), cur(3 * RW), prev(3 * RW), _full_spec(qg), _full_spec(kg),
                  _full_spec(sinks)] + h_in_specs,
        out_specs=[cur(RW)] + h_out_specs, out_shape=[jax.ShapeDtypeStruct((T, RW), MXU)] + h_out_shape,
        scratch_shapes=h_scratch,
        compiler_params=_cparams(("arbitrary",)),
    )(qkv, qkv, tab, tab, qg, kg, sinks, *h_in)


def _attn_bwd(qkv, tab, qg, kg, sinks, dy, *, name, hosted=None):
    T = qkv.shape[0]
    nb = _pick(T // BLK, ATTN_BLOCKS)
    n = T // (BLK * nb)
    h_in, h_in_specs, h_out_specs, h_out_shape, h_scratch = _hosted_args(hosted)

    def body(c_ref, p_ref, tc_ref, tp_ref, qg_ref, kg_ref, s_ref, dy_ref, dqkv_ref, dqg_ref, dkg_ref, ds_ref, carry_ref):
        i = pl.program_id(0)

        @pl.when(i == 0)
        def _():
            carry_ref[...] = jnp.zeros_like(carry_ref)
            dqg_ref[...] = jnp.zeros_like(dqg_ref)
            dkg_ref[...] = jnp.zeros_like(dkg_ref)
            ds_ref[...] = jnp.zeros_like(ds_ref)

        carry = carry_ref[...]
        dqg_t, dkg_t, ds_t = jnp.zeros_like(dqg_ref), jnp.zeros_like(dkg_ref), jnp.zeros_like(ds_ref)
        for b in reversed(range(nb)):
            rows = slice(b * BLK, (b + 1) * BLK)
            before = slice((b - 1) * BLK, b * BLK)
            tc = tc_ref[rows, :]
            p, tp = (p_ref[...], tp_ref[...]) if b == 0 else (c_ref[before, :], tc_ref[before, :])
            first = (i == n - 1) if b == 0 else False
            f = lambda c, p_, qg_, kg_, sk, tc=tc, tp=tp, first=first: _attn_block(c, p_, tc, tp, qg_, kg_, sk, first)
            _, vjp = jax.vjp(f, c_ref[rows, :], p, qg_ref[...], kg_ref[...], s_ref[...])
            dc, dp, dqg, dkg, dsk = vjp(dy_ref[rows, :].astype(F32))
            dqkv_ref[rows, :] = (dc + carry).astype(dqkv_ref.dtype)
            carry = dp
            dqg_t, dkg_t, ds_t = dqg_t + dqg, dkg_t + dkg, ds_t + dsk
        carry_ref[...] = carry
        dqg_ref[...] += dqg_t
        dkg_ref[...] += dkg_t
        ds_ref[...] += ds_t

    cur = lambda w: pl.BlockSpec((nb * BLK, w), lambda i: (n - 1 - i, 0))
    prev = lambda w: pl.BlockSpec((BLK, w), lambda i: (jnp.maximum((n - 1 - i) * nb - 1, 0), 0))
    return pl.pallas_call(
        _hosting(body, hosted, 8, 4, 1, n), name=name, grid=(n,),
        in_specs=[cur(QKV_W), prev(QKV_W), cur(3 * RW), prev(3 * RW), _full_spec(qg), _full_spec(kg), _full_spec(sinks),
                  cur(RW)] + h_in_specs,
        out_specs=[cur(QKV_W), _full_spec(qg), _full_spec(kg), _full_spec(sinks)] + h_out_specs,
        out_shape=[jax.ShapeDtypeStruct((T, QKV_W), MXU), jax.ShapeDtypeStruct(qg.shape, F32),
                   jax.ShapeDtypeStruct(kg.shape, F32), jax.ShapeDtypeStruct(sinks.shape, F32)] + h_out_shape,
        scratch_shapes=[pltpu.VMEM((BLK, QKV_W), F32)] + h_scratch,
        compiler_params=_cparams(("arbitrary",)),
    )(qkv, qkv, tab, tab, qg, kg, sinks, dy, *h_in)


def _shift_down(cur, prev8, i):
    rolled = pltpu.roll(cur, 1, 0)
    first_row = jnp.where(i > 0, prev8[7:8, :], 0.0)
    row = lax.broadcasted_iota(jnp.int32, cur.shape, 0)
    return jnp.where(row == 0, first_row, rolled)


def _shift_up(cur, next8, i, n):
    tm = cur.shape[0]
    rolled = pltpu.roll(cur, tm - 1, 0)
    last_row = jnp.where(i < n - 1, next8[0:1, :], 0.0)
    row = lax.broadcasted_iota(jnp.int32, cur.shape, 0)
    return jnp.where(row == tm - 1, last_row, rolled)


def _ada_parts(ada):
    return [ada[:, j * D:(j + 1) * D] for j in range(6)]


def _rope_table(positions):
    half = HD // 8
    inv_freq = 500000.0 ** (-jnp.arange(half, dtype=F32) / half)
    ang = positions.astype(F32)[:, None] * inv_freq
    cos, sin = jnp.cos(ang), jnp.sin(ang)
    T = positions.shape[0]
    pad = HD - 2 * half
    c64 = jnp.concatenate([cos, cos, jnp.ones((T, pad), F32)], axis=1)
    lo64 = jnp.concatenate([-sin, jnp.zeros((T, HD - half), F32)], axis=1)
    hi64 = jnp.concatenate([jnp.zeros((T, half), F32), sin, jnp.zeros((T, pad), F32)], axis=1)
    return jnp.concatenate([jnp.tile(t, (1, NH)) for t in (c64, lo64, hi64)], axis=1)


GATHER_BEHIND = {"f_proj_shift": ['w_out'], "f_proj_gates": ['w_branch_a', 'w_branch_b'], "f_prep": ['ffn_w2_lo'],
                 "f_scan": ['ffn_w1'], "f_post": ['ffn_w2_hi'], "f_attn": ['ffn_w3']}
LATE = ['w_out', 'w_branch_a', 'w_branch_b', 'ffn_w1', 'ffn_w3', 'ffn_w2']
BACK_ATTN = ['w_out', 'w_branch_a', 'w_branch_b']
BACK_SCAN = ['ffn_w1', 'ffn_w3', 'ffn_w2']
BACK_LAST = ['w_in', 'decay_up', 'iclr_up', 'gate_up']


def _full_weight(g, ax):
    return g.reshape(-1, g.shape[2]) if ax == 0 else jnp.concatenate([g[j] for j in range(4)], axis=1)


def _local_step(x, target, ada, tab, w, s, shards=None):
    T = x.shape[0]
    tm = _pick(T, (512, 256, 128))
    tm_wide = _pick(T, (256, 128))
    tm_vjp = _pick(T, (128,))
    row = lambda n, dt=F32: (n, dt, 'row')
    acc = lambda n, r=1: (n, F32, r)

    def f_norm1(i, n, x_, g, ada_):
        sh, sc = ada_[:, 0:D], ada_[:, D:2 * D]
        return (_norm_mod(x_, g, sc, sh),)
    (h1,) = _rowwise(f_norm1, [x], [s['norm1_gain'], ada], [row(D, MXU)], tm=tm, name="f_norm1")

    pieces, late = {}, {}
    if shards:
        half = shards['ffn_w2'].shape[0] // 2
        pieces = dict(shards, ffn_w2_lo=shards['ffn_w2'][:half], ffn_w2_hi=shards['ffn_w2'][half:])

    def behind(kernel_name):
        return _GatherChips([pieces[n] for n in GATHER_BEHIND[kernel_name]]) if shards else None

    def took(kernel_name, got):
        late.update(zip(GATHER_BEHIND[kernel_name], got))

    def mm_behind(a_, w_, kernel_name, **kw):
        ex = behind(kernel_name)
        res = _mm_nn(a_, w_, name=kernel_name, hosted=ex, **kw)
        if ex:
            took(kernel_name, res[1:])
            return res[0]
        return res

    proj = mm_behind(h1, w['w_in'][:, :SHIFT_W], "f_proj_shift")
    proj_qkv = _mm_nn(h1, w['w_in'][:, SHIFT_W:SHIFT_W + QKV_W], name="f_proj_qkv")
    proj_g = mm_behind(h1, w['w_in'][:, SHIFT_W + QKV_W:], "f_proj_gates", out_dtype=MXU)
    prep_consts = [s['decay_w0'], s['lora_up'], s['iclr_a0'], s['gate_up'], s['k_k'], s['k_a']]

    def f_prep(i, n, cur, prev8, mu, *params):
        mixed = cur + (_shift_down(cur, prev8, i) - cur) * mu
        return (_prep(mixed, *params),)
    rw, *got = _rowwise(f_prep, [(proj, SHIFT_W)], [s['tshift_mu']] + prep_consts, [row(7 * RW)], tm=tm_wide,
                        name="f_prep", halo=[(proj, SHIFT_W, 'prev')], hosted=behind("f_prep"))
    took("f_prep", got)
    y, ck, inv, *got = _scan_fwd(rw, name="f_scan", hosted=behind("f_scan"))
    took("f_scan", got)
    post_consts = [s['lnx_gain'], s['lnx_bias'], s['r_k']]

    rkvg = [(rw, RW, j) for j in (0, 2, 3, 6)]

    def f_post(i, n, *args):
        return (_post(*args),)
    ya, *got = _rowwise(f_post, [y] + rkvg, post_consts, [row(RW, MXU)], tm=tm_wide, name="f_post",
                        hosted=behind("f_post"))
    took("f_post", got)
    yb, *got = _attn_fwd(proj_qkv, tab, s['q_norm_gain'], s['k_norm_gain'], s['attn_sinks'], name="f_attn",
                         hosted=behind("f_attn"))
    took("f_attn", got)
    w = dict(w)
    if shards:
        ax = dict(SHARDED)
        w.update({n: _full_weight(late[n], ax[n]) for n in ('w_branch_a', 'w_branch_b', 'w_out')})
        w['ffn_w13'] = jnp.concatenate([_full_weight(late[n], 1) for n in ('ffn_w1', 'ffn_w3')], axis=1)
        w['ffn_w2'] = jnp.concatenate([late['ffn_w2_lo'], late['ffn_w2_hi']], axis=1).reshape(-1, D)
    ma = _mm_nn(ya, w['w_branch_a'], name="f_branch_a", out_dtype=MXU)
    mb = _mm_nn(yb, w['w_branch_b'], name="f_branch_b", out_dtype=MXU)

    def f_merge(i, n, pg, ma_, mb_, bias):
        return (_merge(pg.astype(F32), ma_.astype(F32), mb_.astype(F32), bias),)
    (merged,) = _rowwise(f_merge, [proj_g, ma, mb], [s['branch_gate_b']], [row(D, MXU)], tm=tm, name="f_merge")
    mo = _mm_nn(merged, w['w_out'], name="f_out")

    def f_res1(i, n, x_, mo_, g, ada_):
        x1_ = x_ + ada_[:, 2 * D:3 * D] * mo_
        return x1_, _norm_mod(x1_, g, ada_[:, 4 * D:5 * D], ada_[:, 3 * D:4 * D])
    x1, h2 = _rowwise(f_res1, [x, mo], [s['norm2_gain'], ada], [row(D), row(D, MXU)], tm=tm, name="f_res1")
    uv = _mm_nn(h2, w['ffn_w13'], name="f_ffn_in", out_dtype=MXU)

    def f_act(i, n, uv_):
        return (_swiglu(uv_[:, :DFF].astype(F32), uv_[:, DFF:].astype(F32)),)
    (act,) = _rowwise(f_act, [uv], [], [row(DFF, MXU)], tm=tm_wide, name="f_act")
    ff = _mm_nn(act, w['ffn_w2'], name="f_ffn_out")

    def f_loss(i, n, x1_, ff_, tgt, ada_):
        g2 = ada_[:, 5 * D:6 * D]
        err = x1_ + g2 * ff_ - tgt
        dx2 = err * (1.0 / D)
        loss = 0.5 * jnp.sum(jnp.sum(err * err, axis=1, keepdims=True) * (1.0 / D), axis=0, keepdims=True)
        return dx2, (dx2 * g2), jnp.broadcast_to(loss, (1, 128)), jnp.sum(dx2 * ff_, axis=0, keepdims=True)
    dx2, dff, loss, dgate2 = _rowwise(f_loss, [x1, ff, target], [ada], [row(D), row(D, MXU), acc(128), acc(D)],
                                      tm=tm, name="f_loss")

    dact = _mm_nt(dff, w['ffn_w2'], name="b_ffn_out_dx", out_dtype=MXU)
    g_w2 = _mm_tn(act, dff, name="b_ffn_out_dw", out_dtype=MXU)

    def b_act(i, n, uv_, dact_):
        _, vjp = jax.vjp(_swiglu, uv_[:, :DFF].astype(F32), uv_[:, DFF:].astype(F32))
        du, dv = vjp(dact_.astype(F32))
        return (jnp.concatenate([du, dv], axis=1),)
    (duv,) = _rowwise(b_act, [uv, dact], [], [row(2 * DFF, MXU)], tm=tm_wide, name="b_act")
    dh2 = _mm_nt(duv, w['ffn_w13'], name="b_ffn_in_dx")
    g_w13 = _mm_tn(h2, duv, name="b_ffn_in_dw", out_dtype=MXU)

    def b_res1(i, n, x1_, dh2_, dx2_, mo_, g, ada_):
        _, vjp = jax.vjp(_norm_mod, x1_, g, ada_[:, 4 * D:5 * D], ada_[:, 3 * D:4 * D])
        dxn, dg, dsc, dsh = vjp(dh2_)
        dx1_ = dxn + dx2_
        g1 = ada_[:, 2 * D:3 * D]
        return dx1_, dx1_ * g1, dg, dsc, dsh, jnp.sum(dx1_ * mo_, axis=0, keepdims=True)
    dx1, dmo, d_gain2, d_scale2, d_shift2, dgate1 = _rowwise(
        b_res1, [x1, dh2, dx2, mo], [s['norm2_gain'], ada], [row(D), row(D, MXU), acc(D), acc(D), acc(D), acc(D)],
        tm=tm, name="b_res1")
    dmerged = _mm_nt(dmo, w['w_out'], name="b_out_dx", out_dtype=MXU)
    g_wout = _mm_tn(merged, dmo, name="b_out_dw", out_dtype=MXU)

    def b_merge(i, n, pg, ma_, mb_, dm, bias):
        _, vjp = jax.vjp(_merge, pg.astype(F32), ma_.astype(F32), mb_.astype(F32), bias)
        dpg, dma_, dmb_, dbias = vjp(dm.astype(F32))
        return dpg, dma_, dmb_, dbias
    dpg, dma, dmb, d_bias = _rowwise(b_merge, [proj_g, ma, mb, dmerged], [s['branch_gate_b']],
                                     [row(GATE_W, MXU), row(D, MXU), row(D, MXU), acc(GATE_W)], tm=tm_wide, name="b_merge")
    dya = _mm_nt(dma, w['w_branch_a'], name="b_branch_a_dx")
    g_wa = _mm_tn(ya, dma, name="b_branch_a_dw", out_dtype=MXU, col_shards=4)
    dyb = _mm_nt(dmb, w['w_branch_b'], name="b_branch_b_dx", out_dtype=F32)
    g_wb = _mm_tn(yb, dmb, name="b_branch_b_dw", out_dtype=MXU, col_shards=4)
    fs = DFF // 4
    gw = dict(w_branch_a=g_wa, w_branch_b=g_wb, w_out=g_wout.reshape(4, D // 4, D),
              ffn_w1=jnp.stack([g_w13[:, j * fs:(j + 1) * fs] for j in range(4)]),
              ffn_w3=jnp.stack([g_w13[:, DFF + j * fs:DFF + (j + 1) * fs] for j in range(4)]),
              ffn_w2=g_w2.reshape(4, fs, D))
    recv = {}
    dqkv, d_qg, d_kg, d_sinks, *got = _attn_bwd(
        proj_qkv, tab, s['q_norm_gain'], s['k_norm_gain'], s['attn_sinks'], dyb, name="b_attn",
        hosted=shards and _ScatterChips([gw[n] for n in BACK_ATTN]))
    recv.update(zip(BACK_ATTN, got))

    def b_post(i, n, y_, r_, k_, v_, g_, dya_, *params):
        _, vjp = jax.vjp(_post, y_, r_, k_, v_, g_, *params)
        dy_, dr_, dk_, dv_, dg_, *dparams = vjp(dya_)
        return (dy_, jnp.concatenate([dr_, dk_, dv_, dg_], axis=1), *dparams)
    dy, drkvg, d_lnx_gain, d_lnx_bias, d_r_k = _rowwise(
        b_post, [y] + rkvg + [dya], post_consts, [row(RW), row(4 * RW), acc(RW), acc(RW), acc(RW)], tm=tm_wide,
        name="b_post")
    dscan, *got = _scan_bwd(rw, ck, inv, dy, name="b_scan",
                            hosted=shards and _ScatterChips([gw[n] for n in BACK_SCAN]))
    recv.update(zip(BACK_SCAN, got))

    def b_prep(i, n, cur, drw_, dscan_, prev8, mu, *params):
        shifted = _shift_down(cur, prev8, i)
        mixed = cur + (shifted - cur) * mu
        _, vjp = jax.vjp(_prep, mixed, *params)
        blk = lambda t, j: t[:, j * RW:(j + 1) * RW]
        ct = jnp.concatenate([blk(dscan_, 0) + blk(drw_, 0), blk(dscan_, 1), blk(dscan_, 2) + blk(drw_, 1),
                              blk(dscan_, 3) + blk(drw_, 2), blk(dscan_, 4), blk(dscan_, 5), blk(drw_, 3)], axis=1)
        grads = vjp(ct)
        dmixed = grads[0]
        return (dmixed, jnp.sum(dmixed * (shifted - cur), axis=0, keepdims=True)) + tuple(grads[1:])
    dmixed, d_mu, d_w0, d_lora, d_a0, d_gate_up, d_kk, d_ka = _rowwise(
        b_prep, [(proj, SHIFT_W), drkvg, dscan], [s['tshift_mu']] + prep_consts,
        [row(SHIFT_W), acc(SHIFT_W), acc(RW), acc(2 * RW, 128), acc(RW), acc(RW, 128), acc(RW), acc(RW)],
        tm=tm_vjp, name="b_prep", halo=[(proj, SHIFT_W, 'prev')])

    def b_gather(i, n, dm, dqkv_, dpg_, next8, mu):
        dcur = dm * (1.0 - mu) + _shift_up(dm, next8, i, n) * mu
        return (jnp.concatenate([dcur.astype(MXU), dqkv_, dpg_], axis=1),)
    (dproj,) = _rowwise(b_gather, [dmixed, dqkv, dpg], [s['tshift_mu']], [row(IN_W, MXU)], tm=tm_wide, name="b_gather",
                        halo=[(dmixed, SHIFT_W, 'next')])
    g_win = _mm_tn(h1, dproj, name="b_proj_dw", out_dtype=MXU, col_shards=4)

    def col_blocks(g):
        k, n = g.shape
        return g.reshape(k, 4, n // 4).transpose(1, 0, 2).astype(MXU)
    gw.update(w_in=g_win, decay_up=col_blocks(d_lora[:64, :RW]), iclr_up=col_blocks(d_lora[64:, RW:]),
              gate_up=col_blocks(d_gate_up))
    top, bottom = None, None
    if shards:
        top = _ScatterChips([gw['w_in'][:, :D // 2]] + [gw[n] for n in BACK_LAST[1:]])
        bottom = _ScatterChips([gw['w_in'][:, D // 2:]])
        dh1, *got_top = _mm_nt(dproj, w['w_in'], name="b_proj_dx", hosted=top)
    else:
        dh1 = _mm_nt(dproj, w['w_in'], name="b_proj_dx")

    def b_norm1(i, n, x_, dh1_, dx1_, g, ada_):
        _, vjp = jax.vjp(_norm_mod, x_, g, ada_[:, D:2 * D], ada_[:, 0:D])
        dxn, dg, dsc, dsh = vjp(dh1_)
        return dxn + dx1_, dg, dsc, dsh
    dx, d_gain1, d_scale1, d_shift1, *got_bottom = _rowwise(
        b_norm1, [x, dh1, dx1], [s['norm1_gain'], ada], [row(D), acc(D), acc(D), acc(D)], tm=tm, name="b_norm1",
        hosted=bottom)
    if shards:
        recv.update(zip(BACK_LAST[1:], got_top[1:]))
        recv['w_in'] = jnp.concatenate([got_top[0], got_bottom[0]], axis=1)

    d_ada = jnp.concatenate([d_shift1, d_scale1, dgate1, d_shift2, d_scale2, dgate2], axis=1)
    gs = dict(norm1_gain=d_gain1, norm2_gain=d_gain2, tshift_mu=d_mu, decay_w0=d_w0, iclr_a0=d_a0, k_k=d_kk, k_a=d_ka,
              r_k=d_r_k, lnx_gain=d_lnx_gain, lnx_bias=d_lnx_bias, q_norm_gain=d_qg, k_norm_gain=d_kg,
              attn_sinks=d_sinks, branch_gate_b=d_bias)
    return loss, dx, d_ada, gw, gs, recv


ANY = pl.BlockSpec(memory_space=pl.ANY)


def _place():
    x, y, c = lax.axis_index("x"), lax.axis_index("y"), lax.axis_index("c")
    return x, y, c, [(1 - x, y), (x, 1 - y), (1 - x, 1 - y)]


def _all_gather8(x_shard, *, name):
    m_per, n = x_shard.shape

    def body(x_ref, out_ref, send_sems, recv_sems, local_sem):
        x, y, c, chips = _place()
        me, sibling = (x, y, c), (x, y, 1 - c)

        def rows(px, py, pc):
            return out_ref.at[pl.ds((4 * px + 2 * py + pc) * m_per, m_per), :]

        def copy(k, block, to, src=None):
            return pltpu.make_async_remote_copy(
                src_ref=rows(*block) if src is None else src, dst_ref=rows(*block),
                send_sem=send_sems.at[k], recv_sem=recv_sems.at[k], device_id=to, device_id_type=MESH)

        mine = pltpu.make_async_copy(x_ref, rows(*me), local_sem)
        mine.start()
        first = [copy(0, me, sibling, src=x_ref)]
        first += [copy(1 + j, me, (*chip, c), src=x_ref) for j, chip in enumerate(chips)]
        for cp in first:
            cp.start()
        passed = [copy(4 + j, (*chip, c), sibling) for j, chip in enumerate(chips)]
        for j, chip in enumerate(chips):
            copy(1 + j, (*chip, c), me).wait_recv()
            passed[j].start()
        copy(0, sibling, me).wait_recv()
        for j, chip in enumerate(chips):
            copy(4 + j, (*chip, 1 - c), me).wait_recv()
        for cp in first + passed:
            cp.wait_send()
        mine.wait()

    return pl.pallas_call(
        body, name=name, out_shape=jax.ShapeDtypeStruct((8 * m_per, n), x_shard.dtype),
        in_specs=[pl.BlockSpec(memory_space=pltpu.VMEM)], out_specs=pl.BlockSpec(memory_space=pltpu.VMEM),
        scratch_shapes=[pltpu.SemaphoreType.DMA((7,)), pltpu.SemaphoreType.DMA((7,)), pltpu.SemaphoreType.DMA],
    )(x_shard)


class _GatherChips:
    def __init__(self, shards):
        n = len(shards)
        self.arrays, self.n_in, self.n_out = list(shards), n, n
        self.out_shape = [jax.ShapeDtypeStruct((4,) + s.shape, s.dtype) for s in shards]
        self.scratch = [pltpu.SemaphoreType.DMA((3 * n,)), pltpu.SemaphoreType.DMA((3 * n,)),
                        pltpu.SemaphoreType.DMA((n,))]

    def _copies(self, x_refs, out_refs, sems, receiving):
        send_sems, recv_sems, local_sems = sems
        x, y, c, chips = _place()
        s_me = 2 * x + y
        n = self.n_in

        def copy(a, k, s):
            return pltpu.make_async_remote_copy(
                src_ref=x_refs[a], dst_ref=out_refs[a].at[s], send_sem=send_sems.at[3 * a + k],
                recv_sem=recv_sems.at[3 * a + k], device_id=(*chips[k], c), device_id_type=MESH)

        mine = [pltpu.make_async_copy(x_refs[a], out_refs[a].at[s_me], local_sems.at[a]) for a in range(n)]
        sends = [copy(a, k, s_me) for a in range(n) for k in range(3)]
        if not receiving:
            return mine, sends
        return mine, sends, [copy(a, k, 2 * px + py) for a in range(n) for k, (px, py) in enumerate(chips)]

    def start(self, x_refs, out_refs, sems):
        mine, sends = self._copies(x_refs, out_refs, sems, False)
        for cp in mine + sends:
            cp.start()

    def wait(self, x_refs, out_refs, sems):
        mine, sends, recvs = self._copies(x_refs, out_refs, sems, True)
        for cp in recvs:
            cp.wait_recv()
        for cp in sends:
            cp.wait_send()
        for cp in mine:
            cp.wait()


class _ScatterChips:
    def __init__(self, parts):
        n = len(parts)
        self.arrays, self.n_in, self.n_out = list(parts), n, n
        self.out_shape = [jax.ShapeDtypeStruct((3,) + p.shape[1:], p.dtype) for p in parts]
        self.scratch = [pltpu.SemaphoreType.DMA((3 * n,)), pltpu.SemaphoreType.DMA((3 * n,))]

    def _copies(self, g_refs, out_refs, sems):
        send_sems, recv_sems = sems
        x, y, c, chips = _place()
        return [pltpu.make_async_remote_copy(
            src_ref=g_refs[a].at[2 * px + py], dst_ref=out_refs[a].at[k], send_sem=send_sems.at[3 * a + k],
            recv_sem=recv_sems.at[3 * a + k], device_id=(px, py, c), device_id_type=MESH)
            for a in range(self.n_in) for k, (px, py) in enumerate(chips)]

    def start(self, g_refs, out_refs, sems):
        for cp in self._copies(g_refs, out_refs, sems):
            cp.start()

    def wait(self, g_refs, out_refs, sems):
        sends = self._copies(g_refs, out_refs, sems)
        for cp in sends:
            cp.wait_recv()
        for cp in sends:
            cp.wait_send()


def _exchange_call(ex, *, name):
    def body(*refs):
        parts = (refs[:ex.n_in], refs[ex.n_in:ex.n_in + ex.n_out], refs[ex.n_in + ex.n_out:])
        ex.start(*parts)
        ex.wait(*parts)

    return pl.pallas_call(body, name=name, out_shape=ex.out_shape, in_specs=[ANY] * ex.n_in,
                          out_specs=[ANY] * ex.n_out, scratch_shapes=ex.scratch)(*ex.arrays)


def _swap_sibling(vs, *, name):
    n = len(vs)

    def body(*refs):
        v_refs, out_refs = refs[:n], refs[n:2 * n]
        send_sems, recv_sems = refs[2 * n:]
        x, y, c, _ = _place()
        cps = [pltpu.make_async_remote_copy(src_ref=v_refs[a], dst_ref=out_refs[a], send_sem=send_sems.at[a],
                                            recv_sem=recv_sems.at[a], device_id=(x, y, 1 - c), device_id_type=MESH)
               for a in range(n)]
        for cp in cps:
            cp.start()
        for cp in cps:
            cp.wait()

    return pl.pallas_call(
        body, name=name, out_shape=[jax.ShapeDtypeStruct(v.shape, v.dtype) for v in vs],
        in_specs=[ANY] * n, out_specs=[ANY] * n,
        scratch_shapes=[pltpu.SemaphoreType.DMA((n,)), pltpu.SemaphoreType.DMA((n,))],
    )(*vs)


def _sum_parts(own, others, *, name):
    R, C = own.shape
    tm = _pick(R, (256, 128, 64))

    def body(own_ref, o0_ref, o1_ref, o2_ref, out_ref):
        tot = own_ref[...].astype(F32)
        for ref in (o0_ref, o1_ref, o2_ref):
            tot = tot + ref[...].astype(F32)
        out_ref[...] = tot

    part = lambda k: pl.BlockSpec((None, tm, C), lambda i: (k, i, 0))
    return pl.pallas_call(
        body, name=name, grid=(R // tm,),
        in_specs=[pl.BlockSpec((tm, C), lambda i: (i, 0)), part(0), part(1), part(2)],
        out_specs=pl.BlockSpec((tm, C), lambda i: (i, 0)), out_shape=jax.ShapeDtypeStruct((R, C), F32),
        compiler_params=_cparams(("arbitrary",)),
    )(own, others, others, others)


def _adam_math(w_, m_, v_, g):
    m2 = ADAM_B1 * m_ + (1.0 - ADAM_B1) * g
    v2 = ADAM_B2 * v_ + (1.0 - ADAM_B2) * jnp.square(g)
    m_hat = m2 / (1.0 - ADAM_B1 ** ADAM_STEP)
    v_hat = v2 / (1.0 - ADAM_B2 ** ADAM_STEP)
    delta = -ADAM_LR * (m_hat / (jnp.sqrt(v_hat) + ADAM_EPS) + ADAM_WD * w_)
    return delta, m2, v2


SMALL_SLOTS = 16
SMALL_COLS = 6 * D


def _pack_small(grads, *, name):
    n = len(grads)

    def body(*refs):
        out_ref = refs[n]
        out_ref[...] = jnp.zeros_like(out_ref)
        for i, ref in enumerate(refs[:n]):
            out_ref[i:i + 1, 0:ref.shape[1]] = ref[...]

    return pl.pallas_call(body, name=name, out_shape=jax.ShapeDtypeStruct((SMALL_SLOTS, SMALL_COLS), F32))(*grads)


def _adamw_small(ws, ms, vs, gathered, *, name):
    n = len(ws)

    def body(*refs):
        w_refs, m_refs, v_refs, g_ref = refs[:n], refs[n:2 * n], refs[2 * n:3 * n], refs[3 * n]
        outs = refs[3 * n + 1:]
        for i in range(n):
            nc = w_refs[i].shape[1]
            g = g_ref[i:i + 1, 0:nc]
            for d in range(1, 8):
                g = g + g_ref[d * SMALL_SLOTS + i:d * SMALL_SLOTS + i + 1, 0:nc]
            delta, m2, v2 = _adam_math(w_refs[i][...], m_refs[i][...], v_refs[i][...], g)
            for k, val in enumerate((g, delta, m2, v2)):
                outs[k * n + i][...] = val

    shapes = [jax.ShapeDtypeStruct(w.shape, F32) for w in ws]
    res = pl.pallas_call(body, name=name, out_shape=shapes * 4,
                         compiler_params=pltpu.CompilerParams(vmem_limit_bytes=VMEM_LIMIT))(*ws, *ms, *vs, gathered)
    return [res[k * n:(k + 1) * n] for k in range(4)]


def _adamw(w, m, v, gparts, *, tm, name):
    def fn(i, n, w_, m_, v_, *gs):
        g = gs[0]
        for p in gs[1:]:
            g = g + p
        return (g,) + _adam_math(w_, m_, v_, g)
    nc = w.shape[1]
    return _rowwise(fn, [w, m, v] + list(gparts), [], [(nc, F32, 'row')] * 4, tm=tm, name=name)


WEIGHTS = ['ada_w', 'ada_b', 'norm1_gain', 'norm2_gain', 'w_in', 'tshift_mu', 'decay_w0', 'decay_up', 'iclr_a0',
           'iclr_up', 'gate_up', 'k_k', 'k_a', 'r_k', 'lnx_gain', 'lnx_bias', 'q_norm_gain', 'k_norm_gain', 'attn_sinks',
           'branch_gate_b', 'w_branch_a', 'w_branch_b', 'w_out', 'ffn_w1', 'ffn_w3', 'ffn_w2']
SHARDED = [('w_in', 1), ('decay_up', 1), ('iclr_up', 1), ('gate_up', 1), ('w_branch_a', 1), ('w_branch_b', 1),
           ('w_out', 0), ('ffn_w1', 1), ('ffn_w3', 1), ('ffn_w2', 0)]
SMALL = ['ada_b', 'norm1_gain', 'norm2_gain', 'tshift_mu', 'decay_w0', 'iclr_a0', 'k_k', 'k_a', 'r_k', 'lnx_gain',
         'lnx_bias', 'q_norm_gain', 'k_norm_gain', 'attn_sinks', 'branch_gate_b']


def kernel(x, c, positions, ada_w, ada_b, norm1_gain, norm2_gain, w_in, tshift_mu, decay_w0, decay_up, iclr_a0, iclr_up, gate_up, k_k, k_a, r_k, lnx_gain, lnx_bias, q_norm_gain, k_norm_gain, attn_sinks, branch_gate_b, w_branch_a, w_branch_b, w_out, ffn_w1, ffn_w3, ffn_w2, loss_target, m_ada_w, m_ada_b, m_norm1_gain, m_norm2_gain, m_w_in, m_tshift_mu, m_decay_w0, m_decay_up, m_iclr_a0, m_iclr_up, m_gate_up, m_k_k, m_k_a, m_r_k, m_lnx_gain, m_lnx_bias, m_q_norm_gain, m_k_norm_gain, m_attn_sinks, m_branch_gate_b, m_w_branch_a, m_w_branch_b, m_w_out, m_ffn_w1, m_ffn_w3, m_ffn_w2, v_ada_w, v_ada_b, v_norm1_gain, v_norm2_gain, v_w_in, v_tshift_mu, v_decay_w0, v_decay_up, v_iclr_a0, v_iclr_up, v_gate_up, v_k_k, v_k_a, v_r_k, v_lnx_gain, v_lnx_bias, v_q_norm_gain, v_k_norm_gain, v_attn_sinks, v_branch_gate_b, v_w_branch_a, v_w_branch_b, v_w_out, v_ffn_w1, v_ffn_w3, v_ffn_w2):
    a = dict(locals())
    W = {n: a[n] for n in WEIGHTS}
    M = {n: a['m_' + n] for n in WEIGHTS}
    V = {n: a['v_' + n] for n in WEIGHTS}
    xi, yi, ci = lax.axis_index("x"), lax.axis_index("y"), lax.axis_index("c")
    me = 4 * xi + 2 * yi + ci
    shard = 2 * xi + yi
    mat = lambda t: t.reshape(t.shape[-2], t.shape[-1])
    sharded = [n for n, _ in SHARDED]

    ax = dict(SHARDED)
    late = LATE
    early = [n for n in sharded if n not in late]
    shards = {n: mat(W[n]).astype(MXU) for n in sharded}
    gathered = _exchange_call(_GatherChips([shards[n] for n in early]), name="gather_weights")
    full = {n: _full_weight(g, ax[n]) for n, g in zip(early, gathered, strict=True)}

    c_all = _all_gather8(jnp.broadcast_to(c, (8, D)), name="gather_c")[0::8]
    pad_rows = lambda t: jnp.concatenate([t, jnp.zeros((BLK - 8, t.shape[1]), t.dtype)])
    c_all = pad_rows(c_all.astype(MXU))
    ada_cols = _mm_nn(c_all, mat(ada_w).astype(MXU), name="f_ada")[:8]
    ada_all = _all_gather8(ada_cols, name="gather_ada").reshape(2, 2, 2, 8, 6 * D // 4)
    ada_mine = lax.dynamic_index_in_dim(ada_all[:, :, 0], me, axis=2, keepdims=False)
    ada = ada_mine.reshape(1, 6 * D) + mat(ada_b)

    zero = jnp.zeros((64, RW), MXU)
    lora = jnp.concatenate([jnp.concatenate([full['decay_up'], zero], axis=1),
                            jnp.concatenate([zero, full['iclr_up']], axis=1)], axis=0)
    s = {n: W[n].reshape(1, -1) for n in SMALL if n != 'ada_b'}
    s['lora_up'] = lora.astype(F32)
    s['gate_up'] = full['gate_up'].astype(F32)
    tab = _rope_table(positions.reshape(-1))
    loss, dx, d_ada, gw, gs, from_chips = _local_step(x[0], loss_target[0], ada, tab, dict(w_in=full['w_in']), s,
                                                      shards={n: shards[n] for n in late})
    loss = lax.psum(loss[0, 0], ("x", "y", "c"))

    gs['ada_b'] = d_ada
    gsmall = _pack_small([gs[n] for n in SMALL], name="pack_small_grads")
    gsmall_all = _all_gather8(gsmall, name="gather_small_grads")
    row = lambda src: [src[n].reshape(1, -1) for n in SMALL]
    sm_out = _adamw_small(row(W), row(M), row(V), gsmall_all, name="adamw_small")
    sm_out = [{n: o.reshape(W[n].shape) for n, o in zip(SMALL, outs_k, strict=True)} for outs_k in sm_out]

    d_ada_all = gsmall_all[0::SMALL_SLOTS]
    d_ada_cols = lax.dynamic_slice_in_dim(d_ada_all, shard * (6 * D // 4), 6 * D // 4, axis=1)
    g_ada_w = _mm_tn(c_all, pad_rows(d_ada_cols.astype(MXU)), name="b_ada")
    ada_out = _adamw(mat(ada_w), mat(m_ada_w), mat(v_ada_w), [g_ada_w], tm=256, name="adamw_ada")

    parts =[_sum_parts(lax.dynamic_index_in_dim(gw[n], shard, axis=0, keepdims=False), from_chips[n], name="sum_" + n)
             for n in sharded]
    others = _swap_sibling(parts, name="swap_grads")
    sh_out = {}
    for n, part, other in zip(sharded, parts, others, strict=True):
        sh_out[n] = _adamw(mat(W[n]), mat(M[n]), mat(V[n]), [part, other], tm=_pick(part.shape[0], (256, 128, 64)),
                           name="adamw_" + n)

    def leaf(k, n):
        if n == 'ada_w':
            return ada_out[k].reshape(W[n].shape)
        if n in sharded:
            return sh_out[n][k].reshape(W[n].shape)
        return sm_out[k][n]
    outs = [leaf(k, n) for k in range(4) for n in WEIGHTS]
    return (loss, dx[None], *outs)
```

```python
import functools
import math

import jax
import jax.numpy as jnp
from jax import lax
from jax.experimental import pallas as pl
from jax.experimental.pallas import tpu as pltpu

F32 = jnp.float32
BF16 = jnp.bfloat16
MXU = BF16
HI = lax.Precision.HIGHEST

D = 1024
HD = 64
NH = 8
RW = NH * HD
SHIFT_W = 3 * RW + 64 + 64 + 128
QKV_W = RW + 2 * 128
GATE_W = 2 * D
IN_W = SHIFT_W + QKV_W + GATE_W
DFF = 2816
BLK = 128
CHUNK = 64
RMS_EPS = 1e-6
GN_EPS = 64e-5
NEG_INF = -1e30
ADAM_LR, ADAM_B1, ADAM_B2, ADAM_EPS, ADAM_WD, ADAM_STEP = 0.001, 0.9, 0.999, 1e-08, 0.01, 10
VMEM_LIMIT = 56 * 1024 * 1024
MESH = pl.DeviceIdType.MESH


def _cparams(sem=None):
    return pltpu.CompilerParams(dimension_semantics=sem, vmem_limit_bytes=VMEM_LIMIT)


def _full_spec(a):
    nd = a.ndim
    return pl.BlockSpec(a.shape, lambda *_: (0,) * nd)


def _rowwise(fn, rows, consts, outs, *, tm, name, halo=(), hosted=None):
    rows = [(a + (0,))[:3] if isinstance(a, tuple) else (a, a.shape[1], 0) for a in rows]
    T = rows[0][0].shape[0]
    assert T % tm == 0 and tm % 8 == 0
    n_tiles = T // tm
    n_in = len(rows) + len(halo) + len(consts)
    in_specs = [pl.BlockSpec((tm, nc), lambda i, j=j: (i, j)) for _, nc, j in rows]
    args = [a for a, _, _ in rows]
    for a, nc, kind in halo:
        if kind == 'prev':
            in_specs.append(pl.BlockSpec((8, nc), lambda i: (jnp.maximum(i * (tm // 8) - 1, 0), 0)))
        else:
            in_specs.append(pl.BlockSpec((8, nc), lambda i: (jnp.minimum((i + 1) * (tm // 8), T // 8 - 1), 0)))
        args.append(a)
    in_specs += [_full_spec(a) for a in consts]
    args += list(consts)
    out_shape, out_specs = [], []
    for ncols, dtype, kind in outs:
        if kind == 'row':
            out_shape.append(jax.ShapeDtypeStruct((T, ncols), dtype))
            out_specs.append(pl.BlockSpec((tm, ncols), lambda i: (i, 0)))
        else:
            out_shape.append(jax.ShapeDtypeStruct((kind, ncols), dtype))
            out_specs.append(pl.BlockSpec((kind, ncols), lambda i: (0, 0)))

    def body(*refs):
        i = pl.program_id(0)
        vals = [r[...] for r in refs[:n_in]]
        res = fn(i, n_tiles, *vals)
        for (ncols, dtype, kind), o_ref, val in zip(outs, refs[n_in:], res, strict=True):
            if kind == 'row':
                o_ref[...] = val.astype(dtype)
            else:
                @pl.when(i == 0)
                def _():
                    o_ref[...] = jnp.zeros_like(o_ref)
                o_ref[...] += val.astype(dtype)

    h_in, h_in_specs, h_out_specs, h_out_shape, h_scratch = _hosted_args(hosted)
    res = pl.pallas_call(
        _hosting(body, hosted, n_in, len(outs), 0, n_tiles), name=name, grid=(n_tiles,),
        in_specs=in_specs + h_in_specs, out_specs=out_specs + h_out_specs, out_shape=out_shape + h_out_shape,
        scratch_shapes=h_scratch, compiler_params=_cparams(("arbitrary",)),
    )(*args, *h_in)
    return res


def _pick(n, cands):
    for c in cands:
        if n % c == 0:
            return c
    return n


MM_ROWS = (1024, 512, 256, 128)
MM_COLS = (1536, 1408, 1024, 896, 768, 512, 256, 128)
MM_WIDE = 3000


def _mm_nn(a, w, *, name, out_dtype=F32, hosted=None):
    T, K = a.shape
    N = w.shape[1]
    tm = _pick(T, MM_ROWS)
    tn = _pick(N, MM_COLS)
    grid = (N // tn, T // tm)
    h_in, h_in_specs, h_out_specs, h_out_shape, h_scratch = _hosted_args(hosted)

    def body(a_ref, w_ref, o_ref):
        o_ref[...] = jnp.dot(a_ref[...], w_ref[...], preferred_element_type=F32).astype(out_dtype)

    res = pl.pallas_call(
        _hosting(body, hosted, 2, 1, 0, grid), name=name, grid=grid,
        in_specs=[pl.BlockSpec((tm, K), lambda j, i: (i, 0)), pl.BlockSpec((K, tn), lambda j, i: (0, j))] + h_in_specs,
        out_specs=[pl.BlockSpec((tm, tn), lambda j, i: (i, j))] + h_out_specs,
        out_shape=[jax.ShapeDtypeStruct((T, N), out_dtype)] + h_out_shape, scratch_shapes=h_scratch,
        compiler_params=_cparams(("arbitrary", "arbitrary")),
    )(a, w, *h_in)
    return res if hosted else res[0]


def _mm_nt(dy, w, *, name, out_dtype=F32, hosted=None):
    T, N = dy.shape
    K = w.shape[0]
    tm = _pick(T, MM_ROWS if N <= MM_WIDE else MM_ROWS[1:])
    tk = _pick(K, MM_COLS[1:])
    grid = (K // tk, T // tm)
    h_in, h_in_specs, h_out_specs, h_out_shape, h_scratch = _hosted_args(hosted)

    def body(dy_ref, w_ref, o_ref):
        o_ref[...] = lax.dot_general(dy_ref[...], w_ref[...], (((1,), (1,)), ((), ())),
                                     preferred_element_type=F32).astype(out_dtype)

    res = pl.pallas_call(
        _hosting(body, hosted, 2, 1, 0, grid), name=name, grid=grid,
        in_specs=[pl.BlockSpec((tm, N), lambda j, i: (i, 0)), pl.BlockSpec((tk, N), lambda j, i: (j, 0))] + h_in_specs,
        out_specs=[pl.BlockSpec((tm, tk), lambda j, i: (i, j))] + h_out_specs,
        out_shape=[jax.ShapeDtypeStruct((T, K), out_dtype)] + h_out_shape, scratch_shapes=h_scratch,
        compiler_params=_cparams(("arbitrary", "arbitrary")),
    )(dy, w, *h_in)
    return res if hosted else res[0]


def _mm_tn(a, dy, *, name, out_dtype=F32, col_shards=None):
    T, K = a.shape
    N = dy.shape[1]
    tm = _pick(T, MM_ROWS)
    tn = N // col_shards if col_shards else _pick(N, MM_COLS[1:])
    n_t = T // tm

    def body(a_ref, dy_ref, o_ref, acc_ref):
        i = pl.program_id(1)

        @pl.when(i == 0)
        def _():
            acc_ref[...] = jnp.zeros_like(acc_ref)

        acc_ref[...] += lax.dot_general(a_ref[...], dy_ref[...], (((0,), (0,)), ((), ())), preferred_element_type=F32)

        @pl.when(i == n_t - 1)
        def _():
            o_ref[...] = acc_ref[...].astype(out_dtype)

    if col_shards:
        out_specs = pl.BlockSpec((None, K, tn), lambda j, i: (j, 0, 0))
        out_shape = jax.ShapeDtypeStruct((col_shards, K, tn), out_dtype)
    else:
        out_specs = pl.BlockSpec((K, tn), lambda j, i: (0, j))
        out_shape = jax.ShapeDtypeStruct((K, N), out_dtype)
    return pl.pallas_call(
        body, name=name, grid=(N // tn, n_t),
        in_specs=[pl.BlockSpec((tm, K), lambda j, i: (i, 0)), pl.BlockSpec((tm, tn), lambda j, i: (i, j))],
        out_specs=out_specs, out_shape=out_shape, scratch_shapes=[pltpu.VMEM((K, tn), F32)],
        compiler_params=_cparams(("arbitrary", "arbitrary")),
    )(a, dy)


def _seg_ones(n):
    r = lax.broadcasted_iota(jnp.int32, (n, n), 0) // HD
    c = lax.broadcasted_iota(jnp.int32, (n, n), 1) // HD
    return (r == c).astype(F32)


def _segsum_raw(x):
    ones = _seg_ones(x.shape[1])
    if MXU == F32:
        return jnp.dot(x, ones, precision=HI, preferred_element_type=F32)
    hi = x.astype(MXU)
    lo = (x - hi.astype(F32)).astype(MXU)
    ones = ones.astype(MXU)
    return jnp.dot(hi, ones, preferred_element_type=F32) + jnp.dot(lo, ones, preferred_element_type=F32)


@jax.custom_vjp
def _segsum(x):
    return _segsum_raw(x)


def _segsum_fwd(x):
    return _segsum_raw(x), None


def _segsum_bwd(_, g):
    return (_segsum_raw(g),)


_segsum.defvjp(_segsum_fwd, _segsum_bwd)


def _mxu(x):
    return x.astype(MXU)


@jax.custom_vjp
def _bdot(a, b):
    return jnp.dot(_mxu(a), _mxu(b), preferred_element_type=F32)


def _bdot_fwd(a, b):
    return _bdot(a, b), (a, b)


def _bdot_bwd(res, g):
    a, b = res
    da = lax.dot_general(_mxu(g), _mxu(b), (((1,), (1,)), ((), ())), preferred_element_type=F32)
    db = lax.dot_general(_mxu(a), _mxu(g), (((0,), (0,)), ((), ())), preferred_element_type=F32)
    return da.astype(a.dtype), db.astype(b.dtype)


_bdot.defvjp(_bdot_fwd, _bdot_bwd)


@jax.custom_vjp
def _bdot_nt(a, b):
    return lax.dot_general(_mxu(a), _mxu(b), (((1,), (1,)), ((), ())), preferred_element_type=F32)


def _bdot_nt_fwd(a, b):
    return _bdot_nt(a, b), (a, b)


def _bdot_nt_bwd(res, g):
    a, b = res
    da = jnp.dot(_mxu(g), _mxu(b), preferred_element_type=F32)
    db = lax.dot_general(_mxu(g), _mxu(a), (((0,), (0,)), ((), ())), preferred_element_type=F32)
    return da.astype(a.dtype), db.astype(b.dtype)


_bdot_nt.defvjp(_bdot_nt_fwd, _bdot_nt_bwd)


def _sigmoid(x):
    return 1.0 / (1.0 + jnp.exp(-x))


def _softplus(x):
    return jnp.maximum(x, 0.0) + jnp.log(1.0 + jnp.exp(jnp.minimum(x, -x)))


def _norm_mod(x, gain, scale, shift):
    inv = lax.rsqrt(jnp.mean(x * x, axis=-1, keepdims=True) + RMS_EPS)
    return (x * inv) * gain * (1.0 + scale) + shift


def _prep(mixed, decay_w0, lora_up, iclr_a0, gate_up, k_k, k_a):
    r = mixed[:, 0:RW]
    k = mixed[:, RW:2 * RW]
    v = mixed[:, 2 * RW:3 * RW]
    z = mixed[:, 3 * RW:3 * RW + 128]
    xg = mixed[:, 3 * RW + 128:]
    lane = lax.broadcasted_iota(jnp.int32, z.shape, 1)
    tz = jnp.where(lane < 64, jnp.tanh(z), z)
    lo = _bdot(tz, lora_up)
    w_log = -_softplus(-(decay_w0 + lo[:, :RW])) - 0.5
    lw = -jnp.exp(w_log)
    a_ic = _sigmoid(iclr_a0 + lo[:, RW:])
    g = _bdot(_sigmoid(xg), gate_up)
    kk = k * k_k
    kk = kk / jnp.maximum(jnp.sqrt(_segsum(kk * kk)), 1e-12)
    k_mod = k * (1.0 + (a_ic - 1.0) * k_a)
    return jnp.concatenate([r, lw, k_mod, v, -kk, kk * a_ic, g], axis=1)


def _post(y, r, k, v, g, lnx_gain, lnx_bias, r_k):
    mu = _segsum(y) * (1.0 / HD)
    yc = y - mu
    var = _segsum(yc * yc) * (1.0 / HD)
    yn = yc * lax.rsqrt(var + GN_EPS) * lnx_gain + lnx_bias
    bonus = _segsum(r * k * r_k) * v
    return (yn + bonus) * g


def _merge(pg, ma, mb, bias):
    gates = _sigmoid(pg + bias)
    return gates[:, :D] * ma + gates[:, D:] * mb


def _swiglu(u, v):
    return u * _sigmoid(u) * v


@functools.partial(jax.custom_vjp, nondiff_argnums=(1,))
def _lane_roll(x, s):
    return pltpu.roll(x, s, 1)


def _lane_roll_fwd(x, s):
    return pltpu.roll(x, s, 1), None


def _lane_roll_bwd(s, _, g):
    n = g.shape[1]
    return (pltpu.roll(g, (n - s) % n, 1),)


_lane_roll.defvjp(_lane_roll_fwd, _lane_roll_bwd)


def _rope(x, cos, sin_lo, sin_hi):
    n = x.shape[1]
    return x * cos + _lane_roll(x, n - 8) * sin_lo + _lane_roll(x, 8) * sin_hi


def _head_rms(x, gain):
    return x * lax.rsqrt(_segsum(x * x) * (1.0 / HD) + RMS_EPS) * gain


def _attn_block(qkv_c, qkv_p, tab_c, tab_p, qg, kg, sinks, first):
    def tabs(tab, n):
        return tab[:, 0:n], tab[:, RW:RW + n], tab[:, 2 * RW:2 * RW + n]

    qg = jnp.concatenate([qg] * NH, axis=1)
    kg = jnp.concatenate([kg] * 2, axis=1)
    q = _rope(_head_rms(qkv_c[:, :RW], qg), *tabs(tab_c, RW))
    k_c = _rope(_head_rms(qkv_c[:, RW:RW + 128], kg), *tabs(tab_c, 128))
    k_p = _rope(_head_rms(qkv_p[:, RW:RW + 128], kg), *tabs(tab_p, 128))
    kband = jnp.concatenate([k_p, k_c], axis=0)
    vband = jnp.concatenate([qkv_p[:, RW + 128:], qkv_c[:, RW + 128:]], axis=0)
    G = 4
    qi = lax.broadcasted_iota(jnp.int32, (G * BLK, 2 * BLK), 0) % BLK
    kj = lax.broadcasted_iota(jnp.int32, (G * BLK, 2 * BLK), 1)
    dist = qi + BLK - kj
    valid = (dist >= 0) & (dist < BLK) & (jnp.logical_not(first) | (kj >= BLK))
    row_g = lax.broadcasted_iota(jnp.int32, (G * BLK, 1), 0) // BLK
    outs = []
    for kvh in range(2):
        kb = kband[:, kvh * HD:(kvh + 1) * HD]
        vb = vband[:, kvh * HD:(kvh + 1) * HD]
        qs = jnp.concatenate([q[:, (G * kvh + g) * HD:(G * kvh + g + 1) * HD] for g in range(G)], axis=0)
        s = _bdot_nt(qs, kb) * (HD ** -0.5)
        s = jnp.where(valid, s, NEG_INF)
        sink = jnp.zeros((G * BLK, 1), F32)
        for g in range(G):
            sink = jnp.where(row_g == g, sinks[:, G * kvh + g:G * kvh + g + 1], sink)
        m = lax.stop_gradient(jnp.maximum(jnp.max(s, axis=-1, keepdims=True), sink))
        e = jnp.exp(s - m)
        p = e / (jnp.sum(e, axis=-1, keepdims=True) + jnp.exp(sink - m))
        o = _bdot(p, vb)
        outs += [o[g * BLK:(g + 1) * BLK] for g in range(G)]
    return jnp.concatenate(outs, axis=1)


def _heads(x):
    return jnp.stack([x[:, h * HD:(h + 1) * HD] for h in range(NH)], axis=0)


def _unheads(x):
    return jnp.concatenate([x[h] for h in range(NH)], axis=1)


def _split(x, n):
    parts, rest = [], x
    for _ in range(n):
        p = rest.astype(MXU)
        parts.append(p)
        rest = rest - p.astype(F32)
    return parts


def _bdot_batched(a, b, ca, cb):
    return lax.dot_general(a, b, (((ca,), (cb,)), ((0,), (0,))), preferred_element_type=F32)


def _bmm_passes(a, b, ca, cb, passes):
    if MXU == F32:
        return lax.dot_general(a, b, (((ca,), (cb,)), ((0,), (0,))), precision=HI, preferred_element_type=F32)
    if passes == 1:
        return _bdot_batched(a.astype(MXU), b.astype(MXU), ca, cb)
    (a0, a1), (b0, b1) = _split(a, 2), _split(b, 2)
    return _bdot_batched(a0, b0, ca, cb) + (_bdot_batched(a0, b1, ca, cb) + _bdot_batched(a1, b0, ca, cb))


@functools.partial(jax.custom_vjp, nondiff_argnums=(2, 3, 4))
def _bmm(a, b, ca, cb, passes=1):
    return _bmm_passes(a, b, ca, cb, passes)


def _bmm_fwd(a, b, ca, cb, passes):
    return _bmm_passes(a, b, ca, cb, passes), (a, b)


def _bmm_bwd(ca, cb, passes, res, g):
    a, b = res
    if (ca, cb) == (2, 1):
        return _bmm_passes(g, b, 2, 2, passes), _bmm_passes(a, g, 1, 1, passes)
    if (ca, cb) == (2, 2):
        return _bmm_passes(g, b, 2, 1, passes), _bmm_passes(g, a, 1, 1, passes)
    return _bmm_passes(b, g, 2, 2, passes), _bmm_passes(a, g, 2, 1, passes)


_bmm.defvjp(_bmm_fwd, _bmm_bwd)


def _tri_dot(x, transpose):
    C = x.shape[1]
    ri = lax.broadcasted_iota(jnp.int32, (C, C), 0)
    ci = lax.broadcasted_iota(jnp.int32, (C, C), 1)
    tri = jnp.broadcast_to(((ri <= ci) if transpose else (ri >= ci)).astype(MXU), (x.shape[0], C, C))
    if MXU == F32:
        return lax.dot_general(tri, x, (((2,), (1,)), ((0,), (0,))), precision=HI, preferred_element_type=F32)
    p0, p1, p2 = _split(x, 3)
    return _bdot_batched(tri, p0, 2, 1) + (_bdot_batched(tri, p1, 2, 1) + _bdot_batched(tri, p2, 2, 1))


@jax.custom_vjp
def _cumsum_rows(x):
    return _tri_dot(x, False)


def _cumsum_rows_fwd(x):
    return _tri_dot(x, False), None


def _cumsum_rows_bwd(_, g):
    return (_tri_dot(g, True),)


_cumsum_rows.defvjp(_cumsum_rows_fwd, _cumsum_rows_bwd)

P_SCORE = 1
P_SOLVE = 1
P_STATE = 1
SCAN_CHUNKS = (4, 2, 1)


def _neumann(l):
    C = l.shape[1]
    eye = (lax.broadcasted_iota(jnp.int32, (C, C), 0) == lax.broadcasted_iota(jnp.int32, (C, C), 1)).astype(F32)
    x, lp = eye + l, l
    for _ in range(int(math.log2(C)) - 1):
        lp = _bmm(lp, lp, 2, 1, P_SOLVE)
        x = x + _bmm(x, lp, 2, 1, P_SOLVE)
    return x


@jax.custom_vjp
def _unit_lower_inverse(l):
    return _neumann(l)


def _unit_lower_inverse_fwd(l):
    x = _neumann(l)
    return x, x


def _unit_lower_inverse_bwd(x, g):
    return (_bmm(_bmm(x, g, 1, 1, P_SOLVE), x, 2, 2, P_SOLVE),)


_unit_lower_inverse.defvjp(_unit_lower_inverse_fwd, _unit_lower_inverse_bwd)


def _known_inverse(x):
    @jax.custom_vjp
    def f(l):
        return x

    f.defvjp(lambda l: (x, None), lambda _, g: (_bmm(_bmm(x, g, 1, 1, P_SOLVE), x, 2, 2, P_SOLVE),))
    return f


def _chunk(S0, r, lw, k, v, a, b, inverse=None):
    C = CHUNK
    n = r.shape[1] // C
    fold = lambda t: t.reshape(NH * n, C, HD)
    r, lw, k, v, a, b = (fold(t) for t in (r, lw, k, v, a, b))
    ri = lax.broadcasted_iota(jnp.int32, (C, C), 0)
    ci = lax.broadcasted_iota(jnp.int32, (C, C), 1)
    incl = (ri >= ci)
    strict = (ri > ci)
    eye = (ri == ci).astype(F32)
    cum = _cumsum_rows(lw)
    p_in = jnp.exp(cum)
    p_ex = jnp.exp(cum - lw)
    p_inv = jnp.exp(-cum)
    at, rt, bt, kt = a * p_ex, r * p_in, b * p_inv, k * p_inv
    lhs = jnp.concatenate([at, rt], axis=1)
    rhs_ = jnp.concatenate([bt, kt], axis=1)
    sc = _bmm(lhs, rhs_, 2, 2, P_SCORE)
    a_ab = jnp.where(strict, sc[:, :C, :C], 0.0)
    a_ak = jnp.where(strict, sc[:, :C, C:], 0.0)
    incl2 = (lax.broadcasted_iota(jnp.int32, (C, 2 * C), 0) >= lax.broadcasted_iota(jnp.int32, (C, 2 * C), 1) % C)
    a_r = jnp.where(incl2, sc[:, C:, :], 0.0)
    av = _bmm(a_ak, v, 2, 1, P_SCORE)
    x = x_all = (_unit_lower_inverse if inverse is None else _known_inverse(inverse))(a_ab)
    p_last = jnp.exp(cum[:, C - 1:C, :])
    per_chunk = lambda t: t.reshape((NH, n) + t.shape[1:])
    lhs, rhs_, a_r, av, x, v, p_last = (per_chunk(t) for t in (lhs, rhs_, a_r, av, x, v, p_last))
    S, ys = S0, []
    for c in range(n):
        s0 = _bmm(lhs[:, c], S, 2, 2, P_STATE)
        u = _bmm(x[:, c], s0[:, :C] + av[:, c], 2, 1, P_SOLVE)
        uv = jnp.concatenate([u, v[:, c]], axis=1)
        ys.append(s0[:, C:] + _bmm(a_r[:, c], uv, 2, 1, P_SCORE))
        S = (S + _bmm(uv, rhs_[:, c], 1, 1, P_STATE)) * p_last[:, c]
    return jnp.concatenate(ys, axis=1), S, x_all


def _hosting(body, ex, n_in, n_out, n_scratch, n_steps):
    if ex is None:
        return body

    def wrapped(*refs):
        a = n_in
        b = a + ex.n_in
        c = b + n_out
        d = c + ex.n_out
        e = d + n_scratch
        ex_refs = (refs[a:b], refs[c:d], refs[e:])
        grid = n_steps if isinstance(n_steps, tuple) else (n_steps,)
        first = last = True
        for ax_, size in enumerate(grid):
            first = first & (pl.program_id(ax_) == 0)
            last = last & (pl.program_id(ax_) == size - 1)

        @pl.when(first)
        def _():
            ex.start(*ex_refs)

        body(*refs[:a], *refs[b:c], *refs[d:e])

        @pl.when(last)
        def _():
            ex.wait(*ex_refs)

    return wrapped


def _hosted_args(ex):
    if ex is None:
        return [], [], [], [], []
    any_spec = pl.BlockSpec(memory_space=pl.ANY)
    return list(ex.arrays), [any_spec] * ex.n_in, [any_spec] * ex.n_out, list(ex.out_shape), list(ex.scratch)


def _scan_fwd(rw, *, name, hosted=None):
    T = rw.shape[0]
    nc = _pick(T // CHUNK, SCAN_CHUNKS)
    rows = CHUNK * nc
    n = T // rows
    h_in, h_in_specs, h_out_specs, h_out_shape, h_scratch = _hosted_args(hosted)

    def body(r_ref, lw_ref, k_ref, v_ref, a_ref, b_ref, y_ref, ck_ref, inv_ref, s_ref):
        @pl.when(pl.program_id(0) == 0)
        def _():
            s_ref[...] = jnp.zeros_like(s_ref)

        S0 = s_ref[...]
        ck_ref[0] = S0
        y, S1, inv = _chunk(S0, *[_heads(ref[...]) for ref in (r_ref, lw_ref, k_ref, v_ref, a_ref, b_ref)])
        y_ref[...] = _unheads(y)
        inv_ref[0] = inv
        s_ref[...] = S1

    col = lambda j: pl.BlockSpec((rows, RW), lambda i: (i, j))
    return pl.pallas_call(
        _hosting(body, hosted, 6, 3, 1, n), name=name, grid=(n,),
        in_specs=[col(j) for j in range(6)] + h_in_specs,
        out_specs=[pl.BlockSpec((rows, RW), lambda i: (i, 0)),
                   pl.BlockSpec((1, NH, HD, HD), lambda i: (i, 0, 0, 0)),
                   pl.BlockSpec((1, NH * nc, CHUNK, CHUNK), lambda i: (i, 0, 0, 0))] + h_out_specs,
        out_shape=[jax.ShapeDtypeStruct((T, RW), F32), jax.ShapeDtypeStruct((n, NH, HD, HD), F32),
                   jax.ShapeDtypeStruct((n, NH * nc, CHUNK, CHUNK), F32)] + h_out_shape,
        scratch_shapes=[pltpu.VMEM((NH, HD, HD), F32)] + h_scratch,
        compiler_params=_cparams(("arbitrary",)),
    )(rw, rw, rw, rw, rw, rw, *h_in)


def _scan_bwd(rw, ck, inv, dy, *, name, hosted=None):
    T = rw.shape[0]
    nc = _pick(T // CHUNK, SCAN_CHUNKS)
    rows = CHUNK * nc
    n = T // rows

    def body(r_ref, lw_ref, k_ref, v_ref, a_ref, b_ref, ck_ref, inv_ref, dy_ref, o_ref, ds_ref):
        @pl.when(pl.program_id(0) == 0)
        def _():
            ds_ref[...] = jnp.zeros_like(ds_ref)

        prim = [_heads(ref[...]) for ref in (r_ref, lw_ref, k_ref, v_ref, a_ref, b_ref)]
        known = inv_ref[0]
        _, vjp = jax.vjp(lambda S0, *p: _chunk(S0, *p, inverse=known)[:2], ck_ref[0], *prim)
        grads = vjp((_heads(dy_ref[...]), ds_ref[...]))
        ds_ref[...] = grads[0]
        o_ref[...] = jnp.concatenate([_unheads(g) for g in grads[1:]], axis=1)

    h_in, h_in_specs, h_out_specs, h_out_shape, h_scratch = _hosted_args(hosted)
    col = lambda j: pl.BlockSpec((rows, RW), lambda i: (n - 1 - i, j))
    return pl.pallas_call(
        _hosting(body, hosted, 9, 1, 1, n), name=name, grid=(n,),
        in_specs=[col(j) for j in range(6)] + [pl.BlockSpec((1, NH, HD, HD), lambda i: (n - 1 - i, 0, 0, 0)),
                                               pl.BlockSpec((1, NH * nc, CHUNK, CHUNK), lambda i: (n - 1 - i, 0, 0, 0)),
                                               pl.BlockSpec((rows, RW), lambda i: (n - 1 - i, 0))] + h_in_specs,
        out_specs=[pl.BlockSpec((rows, 6 * RW), lambda i: (n - 1 - i, 0))] + h_out_specs,
        out_shape=[jax.ShapeDtypeStruct((T, 6 * RW), F32)] + h_out_shape,
        scratch_shapes=[pltpu.VMEM((NH, HD, HD), F32)] + h_scratch,
        compiler_params=_cparams(("arbitrary",)),
    )(rw, rw, rw, rw, rw, rw, ck, inv, dy, *h_in)


ATTN_BLOCKS = (4, 2, 1)

def _attn_fwd(qkv, tab, qg, kg, sinks, *, name, hosted=None):
    T = qkv.shape[0]
    nb = _pick(T // BLK, ATTN_BLOCKS)
    n = T // (BLK * nb)
    h_in, h_in_specs, h_out_specs, h_out_shape, h_scratch = _hosted_args(hosted)

    def body(c_ref, p_ref, tc_ref, tp_ref, qg_ref, kg_ref, s_ref, o_ref):
        for b in range(nb):
            rows = slice(b * BLK, (b + 1) * BLK)
            before = slice((b - 1) * BLK, b * BLK)
            p, tp = (p_ref[...], tp_ref[...]) if b == 0 else (c_ref[before, :], tc_ref[before, :])
            first = (pl.program_id(0) == 0) if b == 0 else False
            o_ref[rows, :] = _attn_block(c_ref[rows, :], p, tc_ref[rows, :], tp, qg_ref[...], kg_ref[...],
                                         s_ref[...], first).astype(o_ref.dtype)

    cur = lambda w: pl.BlockSpec((nb * BLK, w), lambda i: (i, 0))
    prev = lambda w: pl.BlockSpec((BLK, w), lambda i: (jnp.maximum(i * nb - 1, 0), 0))
    return pl.pallas_call(
        _hosting(body, hosted, 7, 1, 0, n), name=name, grid=(n,),
        in_specs=[cur(QKV_W), prev(QKV_W), cur(3 * RW), prev(3 * RW), _full_spec(qg), _full_spec(kg),
                  _full_spec(sinks)] + h_in_specs,
        out_specs=[cur(RW)] + h_out_specs, out_shape=[jax.ShapeDtypeStruct((T, RW), MXU)] + h_out_shape,
        scratch_shapes=h_scratch,
        compiler_params=_cparams(("arbitrary",)),
    )(qkv, qkv, tab, tab, qg, kg, sinks, *h_in)


def _attn_bwd(qkv, tab, qg, kg, sinks, dy, *, name, hosted=None):
    T = qkv.shape[0]
    nb = _pick(T // BLK, ATTN_BLOCKS)
    n = T // (BLK * nb)
    h_in, h_in_specs, h_out_specs, h_out_shape, h_scratch = _hosted_args(hosted)

    def body(c_ref, p_ref, tc_ref, tp_ref, qg_ref, kg_ref, s_ref, dy_ref, dqkv_ref, dqg_ref, dkg_ref, ds_ref, carry_ref):
        i = pl.program_id(0)

        @pl.when(i == 0)
        def _():
            carry_ref[...] = jnp.zeros_like(carry_ref)
            dqg_ref[...] = jnp.zeros_like(dqg_ref)
            dkg_ref[...] = jnp.zeros_like(dkg_ref)
            ds_ref[...] = jnp.zeros_like(ds_ref)

        carry = carry_ref[...]
        dqg_t, dkg_t, ds_t = jnp.zeros_like(dqg_ref), jnp.zeros_like(dkg_ref), jnp.zeros_like(ds_ref)
        for b in reversed(range(nb)):
            rows = slice(b * BLK, (b + 1) * BLK)
            before = slice((b - 1) * BLK, b * BLK)
            tc = tc_ref[rows, :]
            p, tp = (p_ref[...], tp_ref[...]) if b == 0 else (c_ref[before, :], tc_ref[before, :])
            first = (i == n - 1) if b == 0 else False
            f = lambda c, p_, qg_, kg_, sk, tc=tc, tp=tp, first=first: _attn_block(c, p_, tc, tp, qg_, kg_, sk, first)
            _, vjp = jax.vjp(f, c_ref[rows, :], p, qg_ref[...], kg_ref[...], s_ref[...])
            dc, dp, dqg, dkg, dsk = vjp(dy_ref[rows, :].astype(F32))
            dqkv_ref[rows, :] = (dc + carry).astype(dqkv_ref.dtype)
            carry = dp
            dqg_t, dkg_t, ds_t = dqg_t + dqg, dkg_t + dkg, ds_t + dsk
        carry_ref[...] = carry
        dqg_ref[...] += dqg_t
        dkg_ref[...] += dkg_t
        ds_ref[...] += ds_t

    cur = lambda w: pl.BlockSpec((nb * BLK, w), lambda i: (n - 1 - i, 0))
    prev = lambda w: pl.BlockSpec((BLK, w), lambda i: (jnp.maximum((n - 1 - i) * nb - 1, 0), 0))
    return pl.pallas_call(
        _hosting(body, hosted, 8, 4, 1, n), name=name, grid=(n,),
        in_specs=[cur(QKV_W), prev(QKV_W), cur(3 * RW), prev(3 * RW), _full_spec(qg), _full_spec(kg), _full_spec(sinks),
                  cur(RW)] + h_in_specs,
        out_specs=[cur(QKV_W), _full_spec(qg), _full_spec(kg), _full_spec(sinks)] + h_out_specs,
        out_shape=[jax.ShapeDtypeStruct((T, QKV_W), MXU), jax.ShapeDtypeStruct(qg.shape, F32),
                   jax.ShapeDtypeStruct(kg.shape, F32), jax.ShapeDtypeStruct(sinks.shape, F32)] + h_out_shape,
        scratch_shapes=[pltpu.VMEM((BLK, QKV_W), F32)] + h_scratch,
        compiler_params=_cparams(("arbitrary",)),
    )(qkv, qkv, tab, tab, qg, kg, sinks, dy, *h_in)


def _shift_down(cur, prev8, i):
    rolled = pltpu.roll(cur, 1, 0)
    first_row = jnp.where(i > 0, prev8[7:8, :], 0.0)
    row = lax.broadcasted_iota(jnp.int32, cur.shape, 0)
    return jnp.where(row == 0, first_row, rolled)


def _shift_up(cur, next8, i, n):
    tm = cur.shape[0]
    rolled = pltpu.roll(cur, tm - 1, 0)
    last_row = jnp.where(i < n - 1, next8[0:1, :], 0.0)
    row = lax.broadcasted_iota(jnp.int32, cur.shape, 0)
    return jnp.where(row == tm - 1, last_row, rolled)


def _ada_parts(ada):
    return [ada[:, j * D:(j + 1) * D] for j in range(6)]


def _rope_table(positions):
    half = HD // 8
    inv_freq = 500000.0 ** (-jnp.arange(half, dtype=F32) / half)
    ang = positions.astype(F32)[:, None] * inv_freq
    cos, sin = jnp.cos(ang), jnp.sin(ang)
    T = positions.shape[0]
    pad = HD - 2 * half
    c64 = jnp.concatenate([cos, cos, jnp.ones((T, pad), F32)], axis=1)
    lo64 = jnp.concatenate([-sin, jnp.zeros((T, HD - half), F32)], axis=1)
    hi64 = jnp.concatenate([jnp.zeros((T, half), F32), sin, jnp.zeros((T, pad), F32)], axis=1)
    return jnp.concatenate([jnp.tile(t, (1, NH)) for t in (c64, lo64, hi64)], axis=1)


GATHER_BEHIND = {"f_proj_shift": ['w_out'], "f_proj_gates": ['w_branch_a', 'w_branch_b'], "f_prep": ['ffn_w2_lo'],
                 "f_scan": ['ffn_w1'], "f_post": ['ffn_w2_hi'], "f_attn": ['ffn_w3']}
LATE = ['w_out', 'w_branch_a', 'w_branch_b', 'ffn_w1', 'ffn_w3', 'ffn_w2']
BACK_ATTN = ['w_out', 'w_branch_a', 'w_branch_b', 'ffn_w2']
BACK_SCAN = ['ffn_w1', 'ffn_w3']
BACK_LAST = ['w_in', 'decay_up', 'iclr_up', 'gate_up']


def _full_weight(g, ax):
    return g.reshape(-1, g.shape[2]) if ax == 0 else jnp.concatenate([g[j] for j in range(4)], axis=1)


def _local_step(x, target, ada, tab, w, s, shards=None):
    T = x.shape[0]
    tm = _pick(T, (512, 256, 128))
    tm_wide = _pick(T, (256, 128))
    tm_vjp = _pick(T, (128,))
    row = lambda n, dt=F32: (n, dt, 'row')
    acc = lambda n, r=1: (n, F32, r)

    def f_norm1(i, n, x_, g, ada_):
        sh, sc = ada_[:, 0:D], ada_[:, D:2 * D]
        return (_norm_mod(x_, g, sc, sh),)
    (h1,) = _rowwise(f_norm1, [x], [s['norm1_gain'], ada], [row(D, MXU)], tm=tm, name="f_norm1")

    pieces, late = {}, {}
    if shards:
        half = shards['ffn_w2'].shape[0] // 2
        pieces = dict(shards, ffn_w2_lo=shards['ffn_w2'][:half], ffn_w2_hi=shards['ffn_w2'][half:])

    def behind(kernel_name):
        return _GatherChips([pieces[n] for n in GATHER_BEHIND[kernel_name]]) if shards else None

    def took(kernel_name, got):
        late.update(zip(GATHER_BEHIND[kernel_name], got))

    def mm_behind(a_, w_, kernel_name, **kw):
        ex = behind(kernel_name)
        res = _mm_nn(a_, w_, name=kernel_name, hosted=ex, **kw)
        if ex:
            took(kernel_name, res[1:])
            return res[0]
        return res

    proj = mm_behind(h1, w['w_in'][:, :SHIFT_W], "f_proj_shift")
    proj_qkv = _mm_nn(h1, w['w_in'][:, SHIFT_W:SHIFT_W + QKV_W], name="f_proj_qkv")
    proj_g = mm_behind(h1, w['w_in'][:, SHIFT_W + QKV_W:], "f_proj_gates", out_dtype=MXU)
    prep_consts = [s['decay_w0'], s['lora_up'], s['iclr_a0'], s['gate_up'], s['k_k'], s['k_a']]

    def f_prep(i, n, cur, prev8, mu, *params):
        mixed = cur + (_shift_down(cur, prev8, i) - cur) * mu
        return (_prep(mixed, *params),)
    rw, *got = _rowwise(f_prep, [(proj, SHIFT_W)], [s['tshift_mu']] + prep_consts, [row(7 * RW)], tm=tm_wide,
                        name="f_prep", halo=[(proj, SHIFT_W, 'prev')], hosted=behind("f_prep"))
    took("f_prep", got)
    y, ck, inv, *got = _scan_fwd(rw, name="f_scan", hosted=behind("f_scan"))
    took("f_scan", got)
    post_consts = [s['lnx_gain'], s['lnx_bias'], s['r_k']]

    rkvg = [(rw, RW, j) for j in (0, 2, 3, 6)]

    def f_post(i, n, *args):
        return (_post(*args),)
    ya, *got = _rowwise(f_post, [y] + rkvg, post_consts, [row(RW, MXU)], tm=tm_wide, name="f_post",
                        hosted=behind("f_post"))
    took("f_post", got)
    yb, *got = _attn_fwd(proj_qkv, tab, s['q_norm_gain'], s['k_norm_gain'], s['attn_sinks'], name="f_attn",
                         hosted=behind("f_attn"))
    took("f_attn", got)
    w = dict(w)
    if shards:
        ax = dict(SHARDED)
        w.update({n: _full_weight(late[n], ax[n]) for n in ('w_branch_a', 'w_branch_b', 'w_out')})
        w['ffn_w13'] = jnp.concatenate([_full_weight(late[n], 1) for n in ('ffn_w1', 'ffn_w3')], axis=1)
        w['ffn_w2'] = jnp.concatenate([late['ffn_w2_lo'], late['ffn_w2_hi']], axis=1).reshape(-1, D)
    ma = _mm_nn(ya, w['w_branch_a'], name="f_branch_a", out_dtype=MXU)
    mb = _mm_nn(yb, w['w_branch_b'], name="f_branch_b", out_dtype=MXU)

    def f_merge(i, n, pg, ma_, mb_, bias):
        return (_merge(pg.astype(F32), ma_.astype(F32), mb_.astype(F32), bias),)
    (merged,) = _rowwise(f_merge, [proj_g, ma, mb], [s['branch_gate_b']], [row(D, MXU)], tm=tm, name="f_merge")
    mo = _mm_nn(merged, w['w_out'], name="f_out")

    def f_res1(i, n, x_, mo_, g, ada_):
        x1_ = x_ + ada_[:, 2 * D:3 * D] * mo_
        return x1_, _norm_mod(x1_, g, ada_[:, 4 * D:5 * D], ada_[:, 3 * D:4 * D])
    x1, h2 = _rowwise(f_res1, [x, mo], [s['norm2_gain'], ada], [row(D), row(D, MXU)], tm=tm, name="f_res1")
    uv = _mm_nn(h2, w['ffn_w13'], name="f_ffn_in", out_dtype=MXU)

    def f_act(i, n, uv_):
        return (_swiglu(uv_[:, :DFF].astype(F32), uv_[:, DFF:].astype(F32)),)
    (act,) = _rowwise(f_act, [uv], [], [row(DFF, MXU)], tm=tm_wide, name="f_act")
    ff = _mm_nn(act, w['ffn_w2'], name="f_ffn_out")

    def f_loss(i, n, x1_, ff_, tgt, ada_):
        g2 = ada_[:, 5 * D:6 * D]
        err = x1_ + g2 * ff_ - tgt
        dx2 = err * (1.0 / D)
        loss = 0.5 * jnp.sum(jnp.sum(err * err, axis=1, keepdims=True) * (1.0 / D), axis=0, keepdims=True)
        return dx2, (dx2 * g2), jnp.broadcast_to(loss, (1, 128)), jnp.sum(dx2 * ff_, axis=0, keepdims=True)
    dx2, dff, loss, dgate2 = _rowwise(f_loss, [x1, ff, target], [ada], [row(D), row(D, MXU), acc(128), acc(D)],
                                      tm=tm, name="f_loss")

    dact = _mm_nt(dff, w['ffn_w2'], name="b_ffn_out_dx", out_dtype=MXU)
    g_w2 = _mm_tn(act, dff, name="b_ffn_out_dw", out_dtype=MXU)

    def b_act(i, n, uv_, dact_):
        _, vjp = jax.vjp(_swiglu, uv_[:, :DFF].astype(F32), uv_[:, DFF:].astype(F32))
        du, dv = vjp(dact_.astype(F32))
        return (jnp.concatenate([du, dv], axis=1),)
    (duv,) = _rowwise(b_act, [uv, dact], [], [row(2 * DFF, MXU)], tm=tm_wide, name="b_act")
    dh2 = _mm_nt(duv, w['ffn_w13'], name="b_ffn_in_dx")
    g_w13 = _mm_tn(h2, duv, name="b_ffn_in_dw", out_dtype=MXU)

    def b_res1(i, n, x1_, dh2_, dx2_, mo_, g, ada_):
        _, vjp = jax.vjp(_norm_mod, x1_, g, ada_[:, 4 * D:5 * D], ada_[:, 3 * D:4 * D])
        dxn, dg, dsc, dsh = vjp(dh2_)
        dx1_ = dxn + dx2_
        g1 = ada_[:, 2 * D:3 * D]
        return dx1_, dx1_ * g1, dg, dsc, dsh, jnp.sum(dx1_ * mo_, axis=0, keepdims=True)
    dx1, dmo, d_gain2, d_scale2, d_shift2, dgate1 = _rowwise(
        b_res1, [x1, dh2, dx2, mo], [s['norm2_gain'], ada], [row(D), row(D, MXU), acc(D), acc(D), acc(D), acc(D)],
        tm=tm, name="b_res1")
    dmerged = _mm_nt(dmo, w['w_out'], name="b_out_dx", out_dtype=MXU)
    g_wout = _mm_tn(merged, dmo, name="b_out_dw", out_dtype=MXU)

    def b_merge(i, n, pg, ma_, mb_, dm, bias):
        _, vjp = jax.vjp(_merge, pg.astype(F32), ma_.astype(F32), mb_.astype(F32), bias)
        dpg, dma_, dmb_, dbias = vjp(dm.astype(F32))
        return dpg, dma_, dmb_, dbias
    dpg, dma, dmb, d_bias = _rowwise(b_merge, [proj_g, ma, mb, dmerged], [s['branch_gate_b']],
                                     [row(GATE_W, MXU), row(D, MXU), row(D, MXU), acc(GATE_W)], tm=tm_wide, name="b_merge")
    dya = _mm_nt(dma, w['w_branch_a'], name="b_branch_a_dx")
    g_wa = _mm_tn(ya, dma, name="b_branch_a_dw", out_dtype=MXU, col_shards=4)
    dyb = _mm_nt(dmb, w['w_branch_b'], name="b_branch_b_dx", out_dtype=F32)
    g_wb = _mm_tn(yb, dmb, name="b_branch_b_dw", out_dtype=MXU, col_shards=4)
    fs = DFF // 4
    gw = dict(w_branch_a=g_wa, w_branch_b=g_wb, w_out=g_wout.reshape(4, D // 4, D),
              ffn_w1=jnp.stack([g_w13[:, j * fs:(j + 1) * fs] for j in range(4)]),
              ffn_w3=jnp.stack([g_w13[:, DFF + j * fs:DFF + (j + 1) * fs] for j in range(4)]),
              ffn_w2=g_w2.reshape(4, fs, D))
    recv = {}
    dqkv, d_qg, d_kg, d_sinks, *got = _attn_bwd(
        proj_qkv, tab, s['q_norm_gain'], s['k_norm_gain'], s['attn_sinks'], dyb, name="b_attn",
        hosted=shards and _ScatterChips([gw[n] for n in BACK_ATTN]))
    recv.update(zip(BACK_ATTN, got))

    def b_post(i, n, y_, r_, k_, v_, g_, dya_, *params):
        _, vjp = jax.vjp(_post, y_, r_, k_, v_, g_, *params)
        dy_, dr_, dk_, dv_, dg_, *dparams = vjp(dya_)
        return (dy_, jnp.concatenate([dr_, dk_, dv_, dg_], axis=1), *dparams)
    dy, drkvg, d_lnx_gain, d_lnx_bias, d_r_k = _rowwise(
        b_post, [y] + rkvg + [dya], post_consts, [row(RW), row(4 * RW), acc(RW), acc(RW), acc(RW)], tm=tm_wide,
        name="b_post")
    dscan, *got = _scan_bwd(rw, ck, inv, dy, name="b_scan",
                            hosted=shards and _ScatterChips([gw[n] for n in BACK_SCAN]))
    recv.update(zip(BACK_SCAN, got))

    def b_prep(i, n, cur, drw_, dscan_, prev8, mu, *params):
        shifted = _shift_down(cur, prev8, i)
        mixed = cur + (shifted - cur) * mu
        _, vjp = jax.vjp(_prep, mixed, *params)
        blk = lambda t, j: t[:, j * RW:(j + 1) * RW]
        ct = jnp.concatenate([blk(dscan_, 0) + blk(drw_, 0), blk(dscan_, 1), blk(dscan_, 2) + blk(drw_, 1),
                              blk(dscan_, 3) + blk(drw_, 2), blk(dscan_, 4), blk(dscan_, 5), blk(drw_, 3)], axis=1)
        grads = vjp(ct)
        dmixed = grads[0]
        return (dmixed, jnp.sum(dmixed * (shifted - cur), axis=0, keepdims=True)) + tuple(grads[1:])
    dmixed, d_mu, d_w0, d_lora, d_a0, d_gate_up, d_kk, d_ka = _rowwise(
        b_prep, [(proj, SHIFT_W), drkvg, dscan], [s['tshift_mu']] + prep_consts,
        [row(SHIFT_W), acc(SHIFT_W), acc(RW), acc(2 * RW, 128), acc(RW), acc(RW, 128), acc(RW), acc(RW)],
        tm=tm_vjp, name="b_prep", halo=[(proj, SHIFT_W, 'prev')])

    def b_gather(i, n, dm, dqkv_, dpg_, next8, mu):
        dcur = dm * (1.0 - mu) + _shift_up(dm, next8, i, n) * mu
        return (jnp.concatenate([dcur.astype(MXU), dqkv_, dpg_], axis=1),)
    (dproj,) = _rowwise(b_gather, [dmixed, dqkv, dpg], [s['tshift_mu']], [row(IN_W, MXU)], tm=tm_wide, name="b_gather",
                        halo=[(dmixed, SHIFT_W, 'next')])
    g_win = _mm_tn(h1, dproj, name="b_proj_dw", out_dtype=MXU, col_shards=4)

    def col_blocks(g):
        k, n = g.shape
        return g.reshape(k, 4, n // 4).transpose(1, 0, 2).astype(MXU)
    gw.update(w_in=g_win, decay_up=col_blocks(d_lora[:64, :RW]), iclr_up=col_blocks(d_lora[64:, RW:]),
              gate_up=col_blocks(d_gate_up))
    top, bottom = None, None
    if shards:
        top = _ScatterChips([gw['w_in'][:, :D // 2]] + [gw[n] for n in BACK_LAST[1:]])
        bottom = _ScatterChips([gw['w_in'][:, D // 2:]])
        dh1, *got_top = _mm_nt(dproj, w['w_in'], name="b_proj_dx", hosted=top)
    else:
        dh1 = _mm_nt(dproj, w['w_in'], name="b_proj_dx")

    def b_norm1(i, n, x_, dh1_, dx1_, g, ada_):
        _, vjp = jax.vjp(_norm_mod, x_, g, ada_[:, D:2 * D], ada_[:, 0:D])
        dxn, dg, dsc, dsh = vjp(dh1_)
        return dxn + dx1_, dg, dsc, dsh
    dx, d_gain1, d_scale1, d_shift1, *got_bottom = _rowwise(
        b_norm1, [x, dh1, dx1], [s['norm1_gain'], ada], [row(D), acc(D), acc(D), acc(D)], tm=tm, name="b_norm1",
        hosted=bottom)
    if shards:
        recv.update(zip(BACK_LAST[1:], got_top[1:]))
        recv['w_in'] = jnp.concatenate([got_top[0], got_bottom[0]], axis=1)

    d_ada = jnp.concatenate([d_shift1, d_scale1, dgate1, d_shift2, d_scale2, dgate2], axis=1)
    gs = dict(norm1_gain=d_gain1, norm2_gain=d_gain2, tshift_mu=d_mu, decay_w0=d_w0, iclr_a0=d_a0, k_k=d_kk, k_a=d_ka,
              r_k=d_r_k, lnx_gain=d_lnx_gain, lnx_bias=d_lnx_bias, q_norm_gain=d_qg, k_norm_gain=d_kg,
              attn_sinks=d_sinks, branch_gate_b=d_bias)
    return loss, dx, d_ada, gw, gs, recv


ANY = pl.BlockSpec(memory_space=pl.ANY)


def _place():
    x, y, c = lax.axis_index("x"), lax.axis_index("y"), lax.axis_index("c")
    return x, y, c, [(1 - x, y), (x, 1 - y), (1 - x, 1 - y)]


def _all_gather8(x_shard, *, name):
    m_per, n = x_shard.shape

    def body(x_ref, out_ref, send_sems, recv_sems, local_sem):
        x, y, c, chips = _place()
        me, sibling = (x, y, c), (x, y, 1 - c)

        def rows(px, py, pc):
            return out_ref.at[pl.ds((4 * px + 2 * py + pc) * m_per, m_per), :]

        def copy(k, block, to, src=None):
            return pltpu.make_async_remote_copy(
                src_ref=rows(*block) if src is None else src, dst_ref=rows(*block),
                send_sem=send_sems.at[k], recv_sem=recv_sems.at[k], device_id=to, device_id_type=MESH)

        mine = pltpu.make_async_copy(x_ref, rows(*me), local_sem)
        mine.start()
        first = [copy(0, me, sibling, src=x_ref)]
        first += [copy(1 + j, me, (*chip, c), src=x_ref) for j, chip in enumerate(chips)]
        for cp in first:
            cp.start()
        passed = [copy(4 + j, (*chip, c), sibling) for j, chip in enumerate(chips)]
        for j, chip in enumerate(chips):
            copy(1 + j, (*chip, c), me).wait_recv()
            passed[j].start()
        copy(0, sibling, me).wait_recv()
        for j, chip in enumerate(chips):
            copy(4 + j, (*chip, 1 - c), me).wait_recv()
        for cp in first + passed:
            cp.wait_send()
        mine.wait()

    return pl.pallas_call(
        body, name=name, out_shape=jax.ShapeDtypeStruct((8 * m_per, n), x_shard.dtype),
        in_specs=[pl.BlockSpec(memory_space=pltpu.VMEM)], out_specs=pl.BlockSpec(memory_space=pltpu.VMEM),
        scratch_shapes=[pltpu.SemaphoreType.DMA((7,)), pltpu.SemaphoreType.DMA((7,)), pltpu.SemaphoreType.DMA],
    )(x_shard)


class _GatherChips:
    def __init__(self, shards):
        n = len(shards)
        self.arrays, self.n_in, self.n_out = list(shards), n, n
        self.out_shape = [jax.ShapeDtypeStruct((4,) + s.shape, s.dtype) for s in shards]
        self.scratch = [pltpu.SemaphoreType.DMA((3 * n,)), pltpu.SemaphoreType.DMA((3 * n,)),
                        pltpu.SemaphoreType.DMA((n,))]

    def _copies(self, x_refs, out_refs, sems, receiving):
        send_sems, recv_sems, local_sems = sems
        x, y, c, chips = _place()
        s_me = 2 * x + y
        n = self.n_in

        def copy(a, k, s):
            return pltpu.make_async_remote_copy(
                src_ref=x_refs[a], dst_ref=out_refs[a].at[s], send_sem=send_sems.at[3 * a + k],
                recv_sem=recv_sems.at[3 * a + k], device_id=(*chips[k], c), device_id_type=MESH)

        mine = [pltpu.make_async_copy(x_refs[a], out_refs[a].at[s_me], local_sems.at[a]) for a in range(n)]
        sends = [copy(a, k, s_me) for a in range(n) for k in range(3)]
        if not receiving:
            return mine, sends
        return mine, sends, [copy(a, k, 2 * px + py) for a in range(n) for k, (px, py) in enumerate(chips)]

    def start(self, x_refs, out_refs, sems):
        mine, sends = self._copies(x_refs, out_refs, sems, False)
        for cp in mine + sends:
            cp.start()

    def wait(self, x_refs, out_refs, sems):
        mine, sends, recvs = self._copies(x_refs, out_refs, sems, True)
        for cp in recvs:
            cp.wait_recv()
        for cp in sends:
            cp.wait_send()
        for cp in mine:
            cp.wait()


class _GatherChipsHalved(_GatherChips):
    def __init__(self, shards):
        super().__init__(shards)
        n = self.n_in
        self.scratch = [pltpu.SemaphoreType.DMA((6 * n,)), pltpu.SemaphoreType.DMA((6 * n,)),
                        pltpu.SemaphoreType.DMA((n,))]

    def _copies(self, x_refs, out_refs, sems, receiving):
        send_sems, recv_sems, local_sems = sems
        x, y, c, chips = _place()
        s_me = 2 * x + y
        n = self.n_in

        def half(a, who):
            rows = x_refs[a].shape[0] // 2
            return pl.ds(who * rows, rows)

        def over_chips(a, k, s):
            return pltpu.make_async_remote_copy(
                src_ref=x_refs[a].at[half(a, c)], dst_ref=out_refs[a].at[s, half(a, c)],
                send_sem=send_sems.at[3 * a + k], recv_sem=recv_sems.at[3 * a + k],
                device_id=(*chips[k], c), device_id_type=MESH)

        def to_sibling(a, k, s, who):
            return pltpu.make_async_remote_copy(
                src_ref=out_refs[a].at[s, half(a, who)], dst_ref=out_refs[a].at[s, half(a, who)],
                send_sem=send_sems.at[3 * n + 3 * a + k], recv_sem=recv_sems.at[3 * n + 3 * a + k],
                device_id=(x, y, 1 - c), device_id_type=MESH)

        mine = [pltpu.make_async_copy(x_refs[a], out_refs[a].at[s_me], local_sems.at[a]) for a in range(n)]
        sends = [over_chips(a, k, s_me) for a in range(n) for k in range(3)]
        if not receiving:
            return mine, sends
        pairs = [(a, k, 2 * px + py) for a in range(n) for k, (px, py) in enumerate(chips)]
        landed = [over_chips(a, k, s) for a, k, s in pairs]
        passed_on = [to_sibling(a, k, s, c) for a, k, s in pairs]
        from_sibling = [to_sibling(a, k, s, 1 - c) for a, k, s in pairs]
        return mine, sends, landed, passed_on, from_sibling

    def wait(self, x_refs, out_refs, sems):
        mine, sends, landed, passed_on, from_sibling = self._copies(x_refs, out_refs, sems, True)
        for got, fwd in zip(landed, passed_on, strict=True):
            got.wait_recv()
            fwd.start()
        for cp in from_sibling:
            cp.wait_recv()
        for cp in sends + passed_on:
            cp.wait_send()
        for cp in mine:
            cp.wait()


class _ScatterChips:
    def __init__(self, parts):
        n = len(parts)
        self.arrays, self.n_in, self.n_out = list(parts), n, n
        self.out_shape = [jax.ShapeDtypeStruct((3,) + p.shape[1:], p.dtype) for p in parts]
        self.scratch = [pltpu.SemaphoreType.DMA((3 * n,)), pltpu.SemaphoreType.DMA((3 * n,))]

    def _copies(self, g_refs, out_refs, sems):
        send_sems, recv_sems = sems
        x, y, c, chips = _place()
        return [pltpu.make_async_remote_copy(
            src_ref=g_refs[a].at[2 * px + py], dst_ref=out_refs[a].at[k], send_sem=send_sems.at[3 * a + k],
            recv_sem=recv_sems.at[3 * a + k], device_id=(px, py, c), device_id_type=MESH)
            for a in range(self.n_in) for k, (px, py) in enumerate(chips)]

    def start(self, g_refs, out_refs, sems):
        for cp in self._copies(g_refs, out_refs, sems):
            cp.start()

    def wait(self, g_refs, out_refs, sems):
        sends = self._copies(g_refs, out_refs, sems)
        for cp in sends:
            cp.wait_recv()
        for cp in sends:
            cp.wait_send()


def _exchange_call(ex, *, name):
    def body(*refs):
        parts = (refs[:ex.n_in], refs[ex.n_in:ex.n_in + ex.n_out], refs[ex.n_in + ex.n_out:])
        ex.start(*parts)
        ex.wait(*parts)

    return pl.pallas_call(body, name=name, out_shape=ex.out_shape, in_specs=[ANY] * ex.n_in,
                          out_specs=[ANY] * ex.n_out, scratch_shapes=ex.scratch)(*ex.arrays)


def _swap_sibling(vs, *, name):
    n = len(vs)

    def body(*refs):
        v_refs, out_refs = refs[:n], refs[n:2 * n]
        send_sems, recv_sems = refs[2 * n:]
        x, y, c, _ = _place()
        cps = [pltpu.make_async_remote_copy(src_ref=v_refs[a], dst_ref=out_refs[a], send_sem=send_sems.at[a],
                                            recv_sem=recv_sems.at[a], device_id=(x, y, 1 - c), device_id_type=MESH)
               for a in range(n)]
        for cp in cps:
            cp.start()
        for cp in cps:
            cp.wait()

    return pl.pallas_call(
        body, name=name, out_shape=[jax.ShapeDtypeStruct(v.shape, v.dtype) for v in vs],
        in_specs=[ANY] * n, out_specs=[ANY] * n,
        scratch_shapes=[pltpu.SemaphoreType.DMA((n,)), pltpu.SemaphoreType.DMA((n,))],
    )(*vs)


def _sum_parts(own, others, *, name):
    R, C = own.shape
    tm = _pick(R, (256, 128, 64))

    def body(own_ref, o0_ref, o1_ref, o2_ref, out_ref):
        tot = own_ref[...].astype(F32)
        for ref in (o0_ref, o1_ref, o2_ref):
            tot = tot + ref[...].astype(F32)
        out_ref[...] = tot

    part = lambda k: pl.BlockSpec((None, tm, C), lambda i: (k, i, 0))
    return pl.pallas_call(
        body, name=name, grid=(R // tm,),
        in_specs=[pl.BlockSpec((tm, C), lambda i: (i, 0)), part(0), part(1), part(2)],
        out_specs=pl.BlockSpec((tm, C), lambda i: (i, 0)), out_shape=jax.ShapeDtypeStruct((R, C), F32),
        compiler_params=_cparams(("arbitrary",)),
    )(own, others, others, others)


def _adam_math(w_, m_, v_, g):
    m2 = ADAM_B1 * m_ + (1.0 - ADAM_B1) * g
    v2 = ADAM_B2 * v_ + (1.0 - ADAM_B2) * jnp.square(g)
    m_hat = m2 / (1.0 - ADAM_B1 ** ADAM_STEP)
    v_hat = v2 / (1.0 - ADAM_B2 ** ADAM_STEP)
    delta = -ADAM_LR * (m_hat / (jnp.sqrt(v_hat) + ADAM_EPS) + ADAM_WD * w_)
    return delta, m2, v2


SMALL_SLOTS = 16
SMALL_COLS = 6 * D


def _pack_small(grads, *, name):
    n = len(grads)

    def body(*refs):
        out_ref = refs[n]
        out_ref[...] = jnp.zeros_like(out_ref)
        for i, ref in enumerate(refs[:n]):
            out_ref[i:i + 1, 0:ref.shape[1]] = ref[...]

    return pl.pallas_call(body, name=name, out_shape=jax.ShapeDtypeStruct((SMALL_SLOTS, SMALL_COLS), F32))(*grads)


def _adamw_small(ws, ms, vs, gathered, *, name):
    n = len(ws)

    def body(*refs):
        w_refs, m_refs, v_refs, g_ref = refs[:n], refs[n:2 * n], refs[2 * n:3 * n], refs[3 * n]
        outs = refs[3 * n + 1:]
        for i in range(n):
            nc = w_refs[i].shape[1]
            g = g_ref[i:i + 1, 0:nc]
            for d in range(1, 8):
                g = g + g_ref[d * SMALL_SLOTS + i:d * SMALL_SLOTS + i + 1, 0:nc]
            delta, m2, v2 = _adam_math(w_refs[i][...], m_refs[i][...], v_refs[i][...], g)
            for k, val in enumerate((g, delta, m2, v2)):
                outs[k * n + i][...] = val

    shapes = [jax.ShapeDtypeStruct(w.shape, F32) for w in ws]
    res = pl.pallas_call(body, name=name, out_shape=shapes * 4,
                         compiler_params=pltpu.CompilerParams(vmem_limit_bytes=VMEM_LIMIT))(*ws, *ms, *vs, gathered)
    return [res[k * n:(k + 1) * n] for k in range(4)]


def _adamw(w, m, v, gparts, *, tm, name):
    def fn(i, n, w_, m_, v_, *gs):
        g = gs[0]
        for p in gs[1:]:
            g = g + p
        return (g,) + _adam_math(w_, m_, v_, g)
    nc = w.shape[1]
    return _rowwise(fn, [w, m, v] + list(gparts), [], [(nc, F32, 'row')] * 4, tm=tm, name=name)


WEIGHTS = ['ada_w', 'ada_b', 'norm1_gain', 'norm2_gain', 'w_in', 'tshift_mu', 'decay_w0', 'decay_up', 'iclr_a0',
           'iclr_up', 'gate_up', 'k_k', 'k_a', 'r_k', 'lnx_gain', 'lnx_bias', 'q_norm_gain', 'k_norm_gain', 'attn_sinks',
           'branch_gate_b', 'w_branch_a', 'w_branch_b', 'w_out', 'ffn_w1', 'ffn_w3', 'ffn_w2']
SHARDED = [('w_in', 1), ('decay_up', 1), ('iclr_up', 1), ('gate_up', 1), ('w_branch_a', 1), ('w_branch_b', 1),
           ('w_out', 0), ('ffn_w1', 1), ('ffn_w3', 1), ('ffn_w2', 0)]
SMALL = ['ada_b', 'norm1_gain', 'norm2_gain', 'tshift_mu', 'decay_w0', 'iclr_a0', 'k_k', 'k_a', 'r_k', 'lnx_gain',
         'lnx_bias', 'q_norm_gain', 'k_norm_gain', 'attn_sinks', 'branch_gate_b']


def kernel(x, c, positions, ada_w, ada_b, norm1_gain, norm2_gain, w_in, tshift_mu, decay_w0, decay_up, iclr_a0, iclr_up, gate_up, k_k, k_a, r_k, lnx_gain, lnx_bias, q_norm_gain, k_norm_gain, attn_sinks, branch_gate_b, w_branch_a, w_branch_b, w_out, ffn_w1, ffn_w3, ffn_w2, loss_target, m_ada_w, m_ada_b, m_norm1_gain, m_norm2_gain, m_w_in, m_tshift_mu, m_decay_w0, m_decay_up, m_iclr_a0, m_iclr_up, m_gate_up, m_k_k, m_k_a, m_r_k, m_lnx_gain, m_lnx_bias, m_q_norm_gain, m_k_norm_gain, m_attn_sinks, m_branch_gate_b, m_w_branch_a, m_w_branch_b, m_w_out, m_ffn_w1, m_ffn_w3, m_ffn_w2, v_ada_w, v_ada_b, v_norm1_gain, v_norm2_gain, v_w_in, v_tshift_mu, v_decay_w0, v_decay_up, v_iclr_a0, v_iclr_up, v_gate_up, v_k_k, v_k_a, v_r_k, v_lnx_gain, v_lnx_bias, v_q_norm_gain, v_k_norm_gain, v_attn_sinks, v_branch_gate_b, v_w_branch_a, v_w_branch_b, v_w_out, v_ffn_w1, v_ffn_w3, v_ffn_w2):
    a = dict(locals())
    W = {n: a[n] for n in WEIGHTS}
    M = {n: a['m_' + n] for n in WEIGHTS}
    V = {n: a['v_' + n] for n in WEIGHTS}
    xi, yi, ci = lax.axis_index("x"), lax.axis_index("y"), lax.axis_index("c")
    me = 4 * xi + 2 * yi + ci
    shard = 2 * xi + yi
    mat = lambda t: t.reshape(t.shape[-2], t.shape[-1])
    sharded = [n for n, _ in SHARDED]

    ax = dict(SHARDED)
    late = LATE
    early = [n for n in sharded if n not in late]
    shards = {n: mat(W[n]).astype(MXU) for n in sharded}
    gathered = _exchange_call(_GatherChipsHalved([shards[n] for n in early]), name="gather_weights")
    full = {n: _full_weight(g, ax[n]) for n, g in zip(early, gathered, strict=True)}

    c_all = _all_gather8(jnp.broadcast_to(c, (8, D)), name="gather_c")[0::8]
    pad_rows = lambda t: jnp.concatenate([t, jnp.zeros((BLK - 8, t.shape[1]), t.dtype)])
    c_all = pad_rows(c_all.astype(MXU))
    ada_cols = _mm_nn(c_all, mat(ada_w).astype(MXU), name="f_ada")[:8]
    ada_all = _all_gather8(ada_cols, name="gather_ada").reshape(2, 2, 2, 8, 6 * D // 4)
    ada_mine = lax.dynamic_index_in_dim(ada_all[:, :, 0], me, axis=2, keepdims=False)
    ada = ada_mine.reshape(1, 6 * D) + mat(ada_b)

    zero = jnp.zeros((64, RW), MXU)
    lora = jnp.concatenate([jnp.concatenate([full['decay_up'], zero], axis=1),
                            jnp.concatenate([zero, full['iclr_up']], axis=1)], axis=0)
    s = {n: W[n].reshape(1, -1) for n in SMALL if n != 'ada_b'}
    s['lora_up'] = lora.astype(F32)
    s['gate_up'] = full['gate_up'].astype(F32)
    tab = _rope_table(positions.reshape(-1))
    loss, dx, d_ada, gw, gs, from_chips = _local_step(x[0], loss_target[0], ada, tab, dict(w_in=full['w_in']), s,
                                                      shards={n: shards[n] for n in late})
    loss = lax.psum(loss[0, 0], ("x", "y", "c"))

    gs['ada_b'] = d_ada
    gsmall = _pack_small([gs[n] for n in SMALL], name="pack_small_grads")
    gsmall_all = _all_gather8(gsmall, name="gather_small_grads")
    row = lambda src: [src[n].reshape(1, -1) for n in SMALL]
    sm_out = _adamw_small(row(W), row(M), row(V), gsmall_all, name="adamw_small")
    sm_out = [{n: o.reshape(W[n].shape) for n, o in zip(SMALL, outs_k, strict=True)} for outs_k in sm_out]

    d_ada_all = gsmall_all[0::SMALL_SLOTS]
    d_ada_cols = lax.dynamic_slice_in_dim(d_ada_all, shard * (6 * D // 4), 6 * D // 4, axis=1)
    g_ada_w = _mm_tn(c_all, pad_rows(d_ada_cols.astype(MXU)), name="b_ada")
    ada_out = _adamw(mat(ada_w), mat(m_ada_w), mat(v_ada_w), [g_ada_w], tm=256, name="adamw_ada")

    parts =[_sum_parts(lax.dynamic_index_in_dim(gw[n], shard, axis=0, keepdims=False), from_chips[n], name="sum_" + n)
             for n in sharded]
    others = _swap_sibling(parts, name="swap_grads")
    sh_out = {}
    for n, part, other in zip(sharded, parts, others, strict=True):
        sh_out[n] = _adamw(mat(W[n]), mat(M[n]), mat(V[n]), [part, other], tm=_pick(part.shape[0], (256, 128, 64)),
                           name="adamw_" + n)

    def leaf(k, n):
        if n == 'ada_w':
            return ada_out[k].reshape(W[n].shape)
        if n in sharded:
            return sh_out[n][k].reshape(W[n].shape)
        return sm_out[k][n]
    outs = [leaf(k, n) for k in range(4) for n in WEIGHTS]
    return (loss, dx[None], *outs)
```

```python
import functools
import math

import jax
import jax.numpy as jnp
from jax import lax
from jax.experimental import pallas as pl
from jax.experimental.pallas import tpu as pltpu

F32 = jnp.float32
BF16 = jnp.bfloat16
MXU = BF16
HI = lax.Precision.HIGHEST

D = 1024
HD = 64
NH = 8
RW = NH * HD
SHIFT_W = 3 * RW + 64 + 64 + 128
QKV_W = RW + 2 * 128
GATE_W = 2 * D
IN_W = SHIFT_W + QKV_W + GATE_W
DFF = 2816
BLK = 128
CHUNK = 64
RMS_EPS = 1e-6
GN_EPS = 64e-5
NEG_INF = -1e30
ADAM_LR, ADAM_B1, ADAM_B2, ADAM_EPS, ADAM_WD, ADAM_STEP = 0.001, 0.9, 0.999, 1e-08, 0.01, 10
VMEM_LIMIT = 56 * 1024 * 1024
MESH = pl.DeviceIdType.MESH


def _cparams(sem=None):
    return pltpu.CompilerParams(dimension_semantics=sem, vmem_limit_bytes=VMEM_LIMIT)


def _full_spec(a):
    nd = a.ndim
    return pl.BlockSpec(a.shape, lambda *_: (0,) * nd)


def _rowwise(fn, rows, consts, outs, *, tm, name, halo=(), hosted=None):
    rows = [(a + (0,))[:3] if isinstance(a, tuple) else (a, a.shape[1], 0) for a in rows]
    T = rows[0][0].shape[0]
    assert T % tm == 0 and tm % 8 == 0
    n_tiles = T // tm
    n_in = len(rows) + len(halo) + len(consts)
    in_specs = [pl.BlockSpec((tm, nc), lambda i, j=j: (i, j)) for _, nc, j in rows]
    args = [a for a, _, _ in rows]
    for a, nc, kind in halo:
        if kind == 'prev':
            in_specs.append(pl.BlockSpec((8, nc), lambda i: (jnp.maximum(i * (tm // 8) - 1, 0), 0)))
        else:
            in_specs.append(pl.BlockSpec((8, nc), lambda i: (jnp.minimum((i + 1) * (tm // 8), T // 8 - 1), 0)))
        args.append(a)
    in_specs += [_full_spec(a) for a in consts]
    args += list(consts)
    out_shape, out_specs = [], []
    for ncols, dtype, kind in outs:
        if kind == 'row':
            out_shape.append(jax.ShapeDtypeStruct((T, ncols), dtype))
            out_specs.append(pl.BlockSpec((tm, ncols), lambda i: (i, 0)))
        else:
            out_shape.append(jax.ShapeDtypeStruct((kind, ncols), dtype))
            out_specs.append(pl.BlockSpec((kind, ncols), lambda i: (0, 0)))

    def body(*refs):
        i = pl.program_id(0)
        vals = [r[...] for r in refs[:n_in]]
        res = fn(i, n_tiles, *vals)
        for (ncols, dtype, kind), o_ref, val in zip(outs, refs[n_in:], res, strict=True):
            if kind == 'row':
                o_ref[...] = val.astype(dtype)
            else:
                @pl.when(i == 0)
                def _():
                    o_ref[...] = jnp.zeros_like(o_ref)
                o_ref[...] += val.astype(dtype)

    h_in, h_in_specs, h_out_specs, h_out_shape, h_scratch = _hosted_args(hosted)
    res = pl.pallas_call(
        _hosting(body, hosted, n_in, len(outs), 0, n_tiles), name=name, grid=(n_tiles,),
        in_specs=in_specs + h_in_specs, out_specs=out_specs + h_out_specs, out_shape=out_shape + h_out_shape,
        scratch_shapes=h_scratch, compiler_params=_cparams(("arbitrary",)),
    )(*args, *h_in)
    return res


def _pick(n, cands):
    for c in cands:
        if n % c == 0:
            return c
    return n


MM_ROWS = (1024, 512, 256, 128)
MM_COLS = (1536, 1408, 1024, 896, 768, 512, 256, 128)
MM_WIDE = 3000


def _mm_nn(a, w, *, name, out_dtype=F32, hosted=None):
    T, K = a.shape
    N = w.shape[1]
    tm = _pick(T, MM_ROWS)
    tn = _pick(N, MM_COLS)
    grid = (N // tn, T // tm)
    h_in, h_in_specs, h_out_specs, h_out_shape, h_scratch = _hosted_args(hosted)

    def body(a_ref, w_ref, o_ref):
        o_ref[...] = jnp.dot(a_ref[...], w_ref[...], preferred_element_type=F32).astype(out_dtype)

    res = pl.pallas_call(
        _hosting(body, hosted, 2, 1, 0, grid), name=name, grid=grid,
        in_specs=[pl.BlockSpec((tm, K), lambda j, i: (i, 0)), pl.BlockSpec((K, tn), lambda j, i: (0, j))] + h_in_specs,
        out_specs=[pl.BlockSpec((tm, tn), lambda j, i: (i, j))] + h_out_specs,
        out_shape=[jax.ShapeDtypeStruct((T, N), out_dtype)] + h_out_shape, scratch_shapes=h_scratch,
        compiler_params=_cparams(("arbitrary", "arbitrary")),
    )(a, w, *h_in)
    return res if hosted else res[0]


def _mm_nt(dy, w, *, name, out_dtype=F32, hosted=None):
    T, N = dy.shape
    K = w.shape[0]
    tm = _pick(T, MM_ROWS if N <= MM_WIDE else MM_ROWS[1:])
    tk = _pick(K, MM_COLS[1:])
    grid = (K // tk, T // tm)
    h_in, h_in_specs, h_out_specs, h_out_shape, h_scratch = _hosted_args(hosted)

    def body(dy_ref, w_ref, o_ref):
        o_ref[...] = lax.dot_general(dy_ref[...], w_ref[...], (((1,), (1,)), ((), ())),
                                     preferred_element_type=F32).astype(out_dtype)

    res = pl.pallas_call(
        _hosting(body, hosted, 2, 1, 0, grid), name=name, grid=grid,
        in_specs=[pl.BlockSpec((tm, N), lambda j, i: (i, 0)), pl.BlockSpec((tk, N), lambda j, i: (j, 0))] + h_in_specs,
        out_specs=[pl.BlockSpec((tm, tk), lambda j, i: (i, j))] + h_out_specs,
        out_shape=[jax.ShapeDtypeStruct((T, K), out_dtype)] + h_out_shape, scratch_shapes=h_scratch,
        compiler_params=_cparams(("arbitrary", "arbitrary")),
    )(dy, w, *h_in)
    return res if hosted else res[0]


def _mm_tn(a, dy, *, name, out_dtype=F32, col_shards=None):
    T, K = a.shape
    N = dy.shape[1]
    tm = _pick(T, MM_ROWS)
    tn = N // col_shards if col_shards else _pick(N, MM_COLS[1:])
    n_t = T // tm

    def body(a_ref, dy_ref, o_ref, acc_ref):
        i = pl.program_id(1)

        @pl.when(i == 0)
        def _():
            acc_ref[...] = jnp.zeros_like(acc_ref)

        acc_ref[...] += lax.dot_general(a_ref[...], dy_ref[...], (((0,), (0,)), ((), ())), preferred_element_type=F32)

        @pl.when(i == n_t - 1)
        def _():
            o_ref[...] = acc_ref[...].astype(out_dtype)

    if col_shards:
        out_specs = pl.BlockSpec((None, K, tn), lambda j, i: (j, 0, 0))
        out_shape = jax.ShapeDtypeStruct((col_shards, K, tn), out_dtype)
    else:
        out_specs = pl.BlockSpec((K, tn), lambda j, i: (0, j))
        out_shape = jax.ShapeDtypeStruct((K, N), out_dtype)
    return pl.pallas_call(
        body, name=name, grid=(N // tn, n_t),
        in_specs=[pl.BlockSpec((tm, K), lambda j, i: (i, 0)), pl.BlockSpec((tm, tn), lambda j, i: (i, j))],
        out_specs=out_specs, out_shape=out_shape, scratch_shapes=[pltpu.VMEM((K, tn), F32)],
        compiler_params=_cparams(("arbitrary", "arbitrary")),
    )(a, dy)


def _seg_ones(n):
    r = lax.broadcasted_iota(jnp.int32, (n, n), 0) // HD
    c = lax.broadcasted_iota(jnp.int32, (n, n), 1) // HD
    return (r == c).astype(F32)


def _segsum_raw(x):
    ones = _seg_ones(x.shape[1])
    if MXU == F32:
        return jnp.dot(x, ones, precision=HI, preferred_element_type=F32)
    hi = x.astype(MXU)
    lo = (x - hi.astype(F32)).astype(MXU)
    ones = ones.astype(MXU)
    return jnp.dot(hi, ones, preferred_element_type=F32) + jnp.dot(lo, ones, preferred_element_type=F32)


@jax.custom_vjp
def _segsum(x):
    return _segsum_raw(x)


def _segsum_fwd(x):
    return _segsum_raw(x), None


def _segsum_bwd(_, g):
    return (_segsum_raw(g),)


_segsum.defvjp(_segsum_fwd, _segsum_bwd)


def _mxu(x):
    return x.astype(MXU)


@jax.custom_vjp
def _bdot(a, b):
    return jnp.dot(_mxu(a), _mxu(b), preferred_element_type=F32)


def _bdot_fwd(a, b):
    return _bdot(a, b), (a, b)


def _bdot_bwd(res, g):
    a, b = res
    da = lax.dot_general(_mxu(g), _mxu(b), (((1,), (1,)), ((), ())), preferred_element_type=F32)
    db = lax.dot_general(_mxu(a), _mxu(g), (((0,), (0,)), ((), ())), preferred_element_type=F32)
    return da.astype(a.dtype), db.astype(b.dtype)


_bdot.defvjp(_bdot_fwd, _bdot_bwd)


@jax.custom_vjp
def _bdot_nt(a, b):
    return lax.dot_general(_mxu(a), _mxu(b), (((1,), (1,)), ((), ())), preferred_element_type=F32)


def _bdot_nt_fwd(a, b):
    return _bdot_nt(a, b), (a, b)


def _bdot_nt_bwd(res, g):
    a, b = res
    da = jnp.dot(_mxu(g), _mxu(b), preferred_element_type=F32)
    db = lax.dot_general(_mxu(g), _mxu(a), (((0,), (0,)), ((), ())), preferred_element_type=F32)
    return da.astype(a.dtype), db.astype(b.dtype)


_bdot_nt.defvjp(_bdot_nt_fwd, _bdot_nt_bwd)


def _sigmoid(x):
    return 1.0 / (1.0 + jnp.exp(-x))


def _softplus(x):
    return jnp.maximum(x, 0.0) + jnp.log(1.0 + jnp.exp(jnp.minimum(x, -x)))


def _norm_mod(x, gain, scale, shift):
    inv = lax.rsqrt(jnp.mean(x * x, axis=-1, keepdims=True) + RMS_EPS)
    return (x * inv) * gain * (1.0 + scale) + shift


def _prep(mixed, decay_w0, lora_up, iclr_a0, gate_up, k_k, k_a):
    r = mixed[:, 0:RW]
    k = mixed[:, RW:2 * RW]
    v = mixed[:, 2 * RW:3 * RW]
    z = mixed[:, 3 * RW:3 * RW + 128]
    xg = mixed[:, 3 * RW + 128:]
    lane = lax.broadcasted_iota(jnp.int32, z.shape, 1)
    tz = jnp.where(lane < 64, jnp.tanh(z), z)
    lo = _bdot(tz, lora_up)
    w_log = -_softplus(-(decay_w0 + lo[:, :RW])) - 0.5
    lw = -jnp.exp(w_log)
    a_ic = _sigmoid(iclr_a0 + lo[:, RW:])
    g = _bdot(_sigmoid(xg), gate_up)
    kk = k * k_k
    kk = kk / jnp.maximum(jnp.sqrt(_segsum(kk * kk)), 1e-12)
    k_mod = k * (1.0 + (a_ic - 1.0) * k_a)
    return jnp.concatenate([r, lw, k_mod, v, -kk, kk * a_ic, g], axis=1)


def _post(y, r, k, v, g, lnx_gain, lnx_bias, r_k):
    mu = _segsum(y) * (1.0 / HD)
    yc = y - mu
    var = _segsum(yc * yc) * (1.0 / HD)
    yn = yc * lax.rsqrt(var + GN_EPS) * lnx_gain + lnx_bias
    bonus = _segsum(r * k * r_k) * v
    return (yn + bonus) * g


def _merge(pg, ma, mb, bias):
    gates = _sigmoid(pg + bias)
    return gates[:, :D] * ma + gates[:, D:] * mb


def _swiglu(u, v):
    return u * _sigmoid(u) * v


def _ffn_in(h, w1, w3, *, name):
    T, K = h.shape
    F = w1.shape[1]
    tm = _pick(T, MM_ROWS)
    tn = _pick(F, MM_COLS[1:])

    def body(h_ref, w1_ref, w3_ref, u_ref, v_ref, a_ref):
        u = jnp.dot(h_ref[...], w1_ref[...], preferred_element_type=F32).astype(MXU)
        v = jnp.dot(h_ref[...], w3_ref[...], preferred_element_type=F32).astype(MXU)
        u_ref[...] = u
        v_ref[...] = v
        a_ref[...] = _swiglu(u.astype(F32), v.astype(F32)).astype(MXU)

    wspec = pl.BlockSpec((K, tn), lambda j, i: (0, j))
    ospec = pl.BlockSpec((tm, tn), lambda j, i: (i, j))
    return pl.pallas_call(
        body, name=name, grid=(F // tn, T // tm),
        in_specs=[pl.BlockSpec((tm, K), lambda j, i: (i, 0)), wspec, wspec],
        out_specs=[ospec] * 3, out_shape=[jax.ShapeDtypeStruct((T, F), MXU)] * 3,
        compiler_params=_cparams(("arbitrary", "arbitrary")),
    )(h, w1, w3)


def _ffn_act_bwd(dff, w2, u, v, *, name):
    T, N = dff.shape
    F = w2.shape[0]
    tm = _pick(T, MM_ROWS)
    tk = _pick(F, MM_COLS[1:])

    def body(dy_ref, w_ref, u_ref, v_ref, du_ref, dv_ref):
        dact = lax.dot_general(dy_ref[...], w_ref[...], (((1,), (1,)), ((), ())), preferred_element_type=F32)
        _, vjp = jax.vjp(_swiglu, u_ref[...].astype(F32), v_ref[...].astype(F32))
        du, dv = vjp(dact)
        du_ref[...] = du.astype(MXU)
        dv_ref[...] = dv.astype(MXU)

    tile = pl.BlockSpec((tm, tk), lambda j, i: (i, j))
    return pl.pallas_call(
        body, name=name, grid=(F // tk, T // tm),
        in_specs=[pl.BlockSpec((tm, N), lambda j, i: (i, 0)), pl.BlockSpec((tk, N), lambda j, i: (j, 0)), tile, tile],
        out_specs=[tile, tile], out_shape=[jax.ShapeDtypeStruct((T, F), MXU)] * 2,
        compiler_params=_cparams(("arbitrary", "arbitrary")),
    )(dff, w2, u, v)


def _mm_nt2(dy1, w1, dy2, w2, *, name):
    T, N = dy1.shape
    K = w1.shape[0]
    tm = _pick(T, MM_ROWS[1:])
    tk = _pick(K, MM_COLS[1:])

    def body(d1_ref, w1_ref, d2_ref, w2_ref, o_ref):
        nt = lambda a, b: lax.dot_general(a[...], b[...], (((1,), (1,)), ((), ())), preferred_element_type=F32)
        o_ref[...] = nt(d1_ref, w1_ref) + nt(d2_ref, w2_ref)

    dspec = pl.BlockSpec((tm, N), lambda j, i: (i, 0))
    wspec = pl.BlockSpec((tk, N), lambda j, i: (j, 0))
    return pl.pallas_call(
        body, name=name, grid=(K // tk, T // tm), in_specs=[dspec, wspec, dspec, wspec],
        out_specs=pl.BlockSpec((tm, tk), lambda j, i: (i, j)), out_shape=jax.ShapeDtypeStruct((T, K), F32),
        compiler_params=_cparams(("arbitrary", "arbitrary")),
    )(dy1, w1, dy2, w2)


@functools.partial(jax.custom_vjp, nondiff_argnums=(1,))
def _lane_roll(x, s):
    return pltpu.roll(x, s, 1)


def _lane_roll_fwd(x, s):
    return pltpu.roll(x, s, 1), None


def _lane_roll_bwd(s, _, g):
    n = g.shape[1]
    return (pltpu.roll(g, (n - s) % n, 1),)


_lane_roll.defvjp(_lane_roll_fwd, _lane_roll_bwd)


def _rope(x, cos, sin_lo, sin_hi):
    n = x.shape[1]
    return x * cos + _lane_roll(x, n - 8) * sin_lo + _lane_roll(x, 8) * sin_hi


def _head_rms(x, gain):
    return x * lax.rsqrt(_segsum(x * x) * (1.0 / HD) + RMS_EPS) * gain


def _attn_block(qkv_c, qkv_p, tab_c, tab_p, qg, kg, sinks, first):
    def tabs(tab, n):
        return tab[:, 0:n], tab[:, RW:RW + n], tab[:, 2 * RW:2 * RW + n]

    qg = jnp.concatenate([qg] * NH, axis=1)
    kg = jnp.concatenate([kg] * 2, axis=1)
    q = _rope(_head_rms(qkv_c[:, :RW], qg), *tabs(tab_c, RW))
    k_c = _rope(_head_rms(qkv_c[:, RW:RW + 128], kg), *tabs(tab_c, 128))
    k_p = _rope(_head_rms(qkv_p[:, RW:RW + 128], kg), *tabs(tab_p, 128))
    kband = jnp.concatenate([k_p, k_c], axis=0)
    vband = jnp.concatenate([qkv_p[:, RW + 128:], qkv_c[:, RW + 128:]], axis=0)
    G = ATTN_STACK
    qi = lax.broadcasted_iota(jnp.int32, (G * BLK, 2 * BLK), 0) % BLK
    kj = lax.broadcasted_iota(jnp.int32, (G * BLK, 2 * BLK), 1)
    dist = qi + BLK - kj
    valid = (dist >= 0) & (dist < BLK) & (jnp.logical_not(first) | (kj >= BLK))
    row_g = lax.broadcasted_iota(jnp.int32, (G * BLK, 1), 0) // BLK
    outs = []
    for h0 in range(0, NH, G):
        kvh = h0 // 4
        kb = kband[:, kvh * HD:(kvh + 1) * HD]
        vb = vband[:, kvh * HD:(kvh + 1) * HD]
        qs = jnp.concatenate([q[:, (h0 + g) * HD:(h0 + g + 1) * HD] for g in range(G)], axis=0)
        s = _bdot_nt(qs, kb) * (HD ** -0.5)
        s = jnp.where(valid, s, NEG_INF)
        sink = jnp.zeros((G * BLK, 1), F32)
        for g in range(G):
            sink = jnp.where(row_g == g, sinks[:, h0 + g:h0 + g + 1], sink)
        m = lax.stop_gradient(jnp.maximum(jnp.max(s, axis=-1, keepdims=True), sink))
        e = jnp.exp(s - m)
        p = e * (1.0 / (jnp.sum(e, axis=-1, keepdims=True) + jnp.exp(sink - m)))
        o = _bdot(p, vb)
        outs += [o[g * BLK:(g + 1) * BLK] for g in range(G)]
    return jnp.concatenate(outs, axis=1)


def _heads(x):
    return jnp.stack([x[:, h * HD:(h + 1) * HD] for h in range(NH)], axis=0)


def _unheads(x):
    return jnp.concatenate([x[h] for h in range(NH)], axis=1)


def _split(x, n):
    parts, rest = [], x
    for _ in range(n):
        p = rest.astype(MXU)
        parts.append(p)
        rest = rest - p.astype(F32)
    return parts


def _bdot_batched(a, b, ca, cb):
    return lax.dot_general(a, b, (((ca,), (cb,)), ((0,), (0,))), preferred_element_type=F32)


def _bmm_passes(a, b, ca, cb, passes):
    if MXU == F32:
        return lax.dot_general(a, b, (((ca,), (cb,)), ((0,), (0,))), precision=HI, preferred_element_type=F32)
    if passes == 1:
        return _bdot_batched(a.astype(MXU), b.astype(MXU), ca, cb)
    (a0, a1), (b0, b1) = _split(a, 2), _split(b, 2)
    return _bdot_batched(a0, b0, ca, cb) + (_bdot_batched(a0, b1, ca, cb) + _bdot_batched(a1, b0, ca, cb))


@functools.partial(jax.custom_vjp, nondiff_argnums=(2, 3, 4))
def _bmm(a, b, ca, cb, passes=1):
    return _bmm_passes(a, b, ca, cb, passes)


def _bmm_fwd(a, b, ca, cb, passes):
    return _bmm_passes(a, b, ca, cb, passes), (a, b)


def _bmm_bwd(ca, cb, passes, res, g):
    a, b = res
    if (ca, cb) == (2, 1):
        return _bmm_passes(g, b, 2, 2, passes), _bmm_passes(a, g, 1, 1, passes)
    if (ca, cb) == (2, 2):
        return _bmm_passes(g, b, 2, 1, passes), _bmm_passes(g, a, 1, 1, passes)
    return _bmm_passes(b, g, 2, 2, passes), _bmm_passes(a, g, 2, 1, passes)


_bmm.defvjp(_bmm_fwd, _bmm_bwd)


def _tri_dot(x, transpose):
    C = x.shape[1]
    ri = lax.broadcasted_iota(jnp.int32, (C, C), 0)
    ci = lax.broadcasted_iota(jnp.int32, (C, C), 1)
    tri = jnp.broadcast_to(((ri <= ci) if transpose else (ri >= ci)).astype(MXU), (x.shape[0], C, C))
    if MXU == F32:
        return lax.dot_general(tri, x, (((2,), (1,)), ((0,), (0,))), precision=HI, preferred_element_type=F32)
    p0, p1, p2 = _split(x, 3)
    return _bdot_batched(tri, p0, 2, 1) + (_bdot_batched(tri, p1, 2, 1) + _bdot_batched(tri, p2, 2, 1))


@jax.custom_vjp
def _cumsum_rows(x):
    return _tri_dot(x, False)


def _cumsum_rows_fwd(x):
    return _tri_dot(x, False), None


def _cumsum_rows_bwd(_, g):
    return (_tri_dot(g, True),)


_cumsum_rows.defvjp(_cumsum_rows_fwd, _cumsum_rows_bwd)

P_SCORE = 1
P_SOLVE = 1
P_STATE = 1
SCAN_CHUNKS = (4, 2, 1)


def _neumann(l):
    C = l.shape[1]
    eye = (lax.broadcasted_iota(jnp.int32, (C, C), 0) == lax.broadcasted_iota(jnp.int32, (C, C), 1)).astype(F32)
    x, lp = eye + l, l
    for _ in range(int(math.log2(C)) - 1):
        lp = _bmm(lp, lp, 2, 1, P_SOLVE)
        x = x + _bmm(x, lp, 2, 1, P_SOLVE)
    return x


@jax.custom_vjp
def _unit_lower_inverse(l):
    return _neumann(l)


def _unit_lower_inverse_fwd(l):
    x = _neumann(l)
    return x, x


def _unit_lower_inverse_bwd(x, g):
    return (_bmm(_bmm(x, g, 1, 1, P_SOLVE), x, 2, 2, P_SOLVE),)


_unit_lower_inverse.defvjp(_unit_lower_inverse_fwd, _unit_lower_inverse_bwd)


def _known_inverse(x):
    @jax.custom_vjp
    def f(l):
        return x

    f.defvjp(lambda l: (x, None), lambda _, g: (_bmm(_bmm(x, g, 1, 1, P_SOLVE), x, 2, 2, P_SOLVE),))
    return f


def _chunk(S0, r, lw, k, v, a, b, inverse=None):
    C = CHUNK
    n = r.shape[1] // C
    fold = lambda t: t.reshape(NH * n, C, HD)
    r, lw, k, v, a, b = (fold(t) for t in (r, lw, k, v, a, b))
    ri = lax.broadcasted_iota(jnp.int32, (C, C), 0)
    ci = lax.broadcasted_iota(jnp.int32, (C, C), 1)
    incl = (ri >= ci)
    strict = (ri > ci)
    eye = (ri == ci).astype(F32)
    cum = _cumsum_rows(lw)
    p_in = jnp.exp(cum)
    p_ex = jnp.exp(cum - lw)
    p_inv = jnp.exp(-cum)
    at, rt, bt, kt = a * p_ex, r * p_in, b * p_inv, k * p_inv
    lhs = jnp.concatenate([at, rt], axis=1)
    rhs_ = jnp.concatenate([bt, kt], axis=1)
    sc = _bmm(lhs, rhs_, 2, 2, P_SCORE)
    a_ab = jnp.where(strict, sc[:, :C, :C], 0.0)
    a_ak = jnp.where(strict, sc[:, :C, C:], 0.0)
    incl2 = (lax.broadcasted_iota(jnp.int32, (C, 2 * C), 0) >= lax.broadcasted_iota(jnp.int32, (C, 2 * C), 1) % C)
    a_r = jnp.where(incl2, sc[:, C:, :], 0.0)
    av = _bmm(a_ak, v, 2, 1, P_SCORE)
    x = x_all = (_unit_lower_inverse if inverse is None else _known_inverse(inverse))(a_ab)
    p_last = jnp.exp(cum[:, C - 1:C, :])
    per_chunk = lambda t: t.reshape((NH, n) + t.shape[1:])
    lhs, rhs_, a_r, av, x, v, p_last = (per_chunk(t) for t in (lhs, rhs_, a_r, av, x, v, p_last))
    S, ys = S0, []
    for c in range(n):
        s0 = _bmm(lhs[:, c], S, 2, 2, P_STATE)
        u = _bmm(x[:, c], s0[:, :C] + av[:, c], 2, 1, P_SOLVE)
        uv = jnp.concatenate([u, v[:, c]], axis=1)
        ys.append(s0[:, C:] + _bmm(a_r[:, c], uv, 2, 1, P_SCORE))
        S = (S + _bmm(uv, rhs_[:, c], 1, 1, P_STATE)) * p_last[:, c]
    return jnp.concatenate(ys, axis=1), S, x_all


def _hosting(body, ex, n_in, n_out, n_scratch, n_steps):
    if ex is None:
        return body

    def wrapped(*refs):
        a = n_in
        b = a + ex.n_in
        c = b + n_out
        d = c + ex.n_out
        e = d + n_scratch
        ex_refs = (refs[a:b], refs[c:d], refs[e:])
        grid = n_steps if isinstance(n_steps, tuple) else (n_steps,)
        first = last = True
        for ax_, size in enumerate(grid):
            first = first & (pl.program_id(ax_) == 0)
            last = last & (pl.program_id(ax_) == size - 1)

        @pl.when(first)
        def _():
            ex.start(*ex_refs)

        body(*refs[:a], *refs[b:c], *refs[d:e])

        @pl.when(last)
        def _():
            ex.wait(*ex_refs)

    return wrapped


def _hosted_args(ex):
    if ex is None:
        return [], [], [], [], []
    any_spec = pl.BlockSpec(memory_space=pl.ANY)
    return list(ex.arrays), [any_spec] * ex.n_in, [any_spec] * ex.n_out, list(ex.out_shape), list(ex.scratch)


def _scan_fwd(rw, *, name, hosted=None):
    T = rw.shape[0]
    nc = _pick(T // CHUNK, SCAN_CHUNKS)
    rows = CHUNK * nc
    n = T // rows
    h_in, h_in_specs, h_out_specs, h_out_shape, h_scratch = _hosted_args(hosted)

    def body(r_ref, lw_ref, k_ref, v_ref, a_ref, b_ref, y_ref, ck_ref, inv_ref, s_ref):
        @pl.when(pl.program_id(0) == 0)
        def _():
            s_ref[...] = jnp.zeros_like(s_ref)

        S0 = s_ref[...]
        ck_ref[0] = S0
        y, S1, inv = _chunk(S0, *[_heads(ref[...]) for ref in (r_ref, lw_ref, k_ref, v_ref, a_ref, b_ref)])
        y_ref[...] = _unheads(y)
        inv_ref[0] = inv
        s_ref[...] = S1

    col = lambda j: pl.BlockSpec((rows, RW), lambda i: (i, j))
    return pl.pallas_call(
        _hosting(body, hosted, 6, 3, 1, n), name=name, grid=(n,),
        in_specs=[col(j) for j in range(6)] + h_in_specs,
        out_specs=[pl.BlockSpec((rows, RW), lambda i: (i, 0)),
                   pl.BlockSpec((1, NH, HD, HD), lambda i: (i, 0, 0, 0)),
                   pl.BlockSpec((1, NH * nc, CHUNK, CHUNK), lambda i: (i, 0, 0, 0))] + h_out_specs,
        out_shape=[jax.ShapeDtypeStruct((T, RW), F32), jax.ShapeDtypeStruct((n, NH, HD, HD), F32),
                   jax.ShapeDtypeStruct((n, NH * nc, CHUNK, CHUNK), F32)] + h_out_shape,
        scratch_shapes=[pltpu.VMEM((NH, HD, HD), F32)] + h_scratch,
        compiler_params=_cparams(("arbitrary",)),
    )(rw, rw, rw, rw, rw, rw, *h_in)


def _scan_bwd(rw, ck, inv, dy, *, name, hosted=None):
    T = rw.shape[0]
    nc = _pick(T // CHUNK, SCAN_CHUNKS)
    rows = CHUNK * nc
    n = T // rows

    def body(r_ref, lw_ref, k_ref, v_ref, a_ref, b_ref, ck_ref, inv_ref, dy_ref, o_ref, ds_ref):
        @pl.when(pl.program_id(0) == 0)
        def _():
            ds_ref[...] = jnp.zeros_like(ds_ref)

        prim = [_heads(ref[...]) for ref in (r_ref, lw_ref, k_ref, v_ref, a_ref, b_ref)]
        known = inv_ref[0]
        _, vjp = jax.vjp(lambda S0, *p: _chunk(S0, *p, inverse=known)[:2], ck_ref[0], *prim)
        grads = vjp((_heads(dy_ref[...]), ds_ref[...]))
        ds_ref[...] = grads[0]
        o_ref[...] = jnp.concatenate([_unheads(g) for g in grads[1:]], axis=1).astype(o_ref.dtype)

    h_in, h_in_specs, h_out_specs, h_out_shape, h_scratch = _hosted_args(hosted)
    col = lambda j: pl.BlockSpec((rows, RW), lambda i: (n - 1 - i, j))
    return pl.pallas_call(
        _hosting(body, hosted, 9, 1, 1, n), name=name, grid=(n,),
        in_specs=[col(j) for j in range(6)] + [pl.BlockSpec((1, NH, HD, HD), lambda i: (n - 1 - i, 0, 0, 0)),
                                               pl.BlockSpec((1, NH * nc, CHUNK, CHUNK), lambda i: (n - 1 - i, 0, 0, 0)),
                                               pl.BlockSpec((rows, RW), lambda i: (n - 1 - i, 0))] + h_in_specs,
        out_specs=[pl.BlockSpec((rows, 6 * RW), lambda i: (n - 1 - i, 0))] + h_out_specs,
        out_shape=[jax.ShapeDtypeStruct((T, 6 * RW), MXU)] + h_out_shape,
        scratch_shapes=[pltpu.VMEM((NH, HD, HD), F32)] + h_scratch,
        compiler_params=_cparams(("arbitrary",)),
    )(rw, rw, rw, rw, rw, rw, ck, inv, dy, *h_in)


ATTN_BLOCKS = (4, 2, 1)
ATTN_STACK = 4

def _attn_fwd(qkv, tab, qg, kg, sinks, *, name, hosted=None):
    T = qkv.shape[0]
    nb = _pick(T // BLK, ATTN_BLOCKS)
    n = T // (BLK * nb)
    h_in, h_in_specs, h_out_specs, h_out_shape, h_scratch = _hosted_args(hosted)

    def body(c_ref, p_ref, tc_ref, tp_ref, qg_ref, kg_ref, s_ref, o_ref):
        for b in range(nb):
            rows = slice(b * BLK, (b + 1) * BLK)
            before = slice((b - 1) * BLK, b * BLK)
            p, tp = (p_ref[...], tp_ref[...]) if b == 0 else (c_ref[before, :], tc_ref[before, :])
            first = (pl.program_id(0) == 0) if b == 0 else False
            o_ref[rows, :] = _attn_block(c_ref[rows, :], p, tc_ref[rows, :], tp, qg_ref[...], kg_ref[...],
                                         s_ref[...], first).astype(o_ref.dtype)

    cur = lambda w: pl.BlockSpec((nb * BLK, w), lambda i: (i, 0))
    prev = lambda w: pl.BlockSpec((BLK, w), lambda i: (jnp.maximum(i * nb - 1, 0), 0))
    return pl.pallas_call(
        _hosting(body, hosted, 7, 1, 0, n), name=name, grid=(n,),
        in_specs=[cur(QKV_W), prev(QKV_W), cur(3 * RW), prev(3 * RW), _full_spec(qg), _full_spec(kg),
                  _full_spec(sinks)] + h_in_specs,
        out_specs=[cur(RW)] + h_out_specs, out_shape=[jax.ShapeDtypeStruct((T, RW), MXU)] + h_out_shape,
        scratch_shapes=h_scratch,
        compiler_params=_cparams(("arbitrary",)),
    )(qkv, qkv, tab, tab, qg, kg, sinks, *h_in)


def _attn_bwd(qkv, tab, qg, kg, sinks, dy, *, name, hosted=None):
    T = qkv.shape[0]
    nb = _pick(T // BLK, ATTN_BLOCKS)
    n = T // (BLK * nb)
    h_in, h_in_specs, h_out_specs, h_out_shape, h_scratch = _hosted_args(hosted)

    def body(c_ref, p_ref, tc_ref, tp_ref, qg_ref, kg_ref, s_ref, dy_ref, dqkv_ref, dqg_ref, dkg_ref, ds_ref, carry_ref):
        i = pl.program_id(0)

        @pl.when(i == 0)
        def _():
            carry_ref[...] = jnp.zeros_like(carry_ref)
            dqg_ref[...] = jnp.zeros_like(dqg_ref)
            dkg_ref[...] = jnp.zeros_like(dkg_ref)
            ds_ref[...] = jnp.zeros_like(ds_ref)

        carry = carry_ref[...]
        dqg_t, dkg_t, ds_t = jnp.zeros_like(dqg_ref), jnp.zeros_like(dkg_ref), jnp.zeros_like(ds_ref)
        for b in reversed(range(nb)):
            rows = slice(b * BLK, (b + 1) * BLK)
            before = slice((b - 1) * BLK, b * BLK)
            tc = tc_ref[rows, :]
            p, tp = (p_ref[...], tp_ref[...]) if b == 0 else (c_ref[before, :], tc_ref[before, :])
            first = (i == n - 1) if b == 0 else False
            f = lambda c, p_, qg_, kg_, sk, tc=tc, tp=tp, first=first: _attn_block(c, p_, tc, tp, qg_, kg_, sk, first)
            _, vjp = jax.vjp(f, c_ref[rows, :], p, qg_ref[...], kg_ref[...], s_ref[...])
            dc, dp, dqg, dkg, dsk = vjp(dy_ref[rows, :].astype(F32))
            dqkv_ref[rows, :] = (dc + carry).astype(dqkv_ref.dtype)
            carry = dp
            dqg_t, dkg_t, ds_t = dqg_t + dqg, dkg_t + dkg, ds_t + dsk
        carry_ref[...] = carry
        dqg_ref[...] += dqg_t
        dkg_ref[...] += dkg_t
        ds_ref[...] += ds_t

    cur = lambda w: pl.BlockSpec((nb * BLK, w), lambda i: (n - 1 - i, 0))
    prev = lambda w: pl.BlockSpec((BLK, w), lambda i: (jnp.maximum((n - 1 - i) * nb - 1, 0), 0))
    return pl.pallas_call(
        _hosting(body, hosted, 8, 4, 1, n), name=name, grid=(n,),
        in_specs=[cur(QKV_W), prev(QKV_W), cur(3 * RW), prev(3 * RW), _full_spec(qg), _full_spec(kg), _full_spec(sinks),
                  cur(RW)] + h_in_specs,
        out_specs=[cur(QKV_W), _full_spec(qg), _full_spec(kg), _full_spec(sinks)] + h_out_specs,
        out_shape=[jax.ShapeDtypeStruct((T, QKV_W), MXU), jax.ShapeDtypeStruct(qg.shape, F32),
                   jax.ShapeDtypeStruct(kg.shape, F32), jax.ShapeDtypeStruct(sinks.shape, F32)] + h_out_shape,
        scratch_shapes=[pltpu.VMEM((BLK, QKV_W), F32)] + h_scratch,
        compiler_params=_cparams(("arbitrary",)),
    )(qkv, qkv, tab, tab, qg, kg, sinks, dy, *h_in)


def _shift_down(cur, prev8, i):
    rolled = pltpu.roll(cur, 1, 0)
    first_row = jnp.where(i > 0, prev8[7:8, :], 0.0)
    row = lax.broadcasted_iota(jnp.int32, cur.shape, 0)
    return jnp.where(row == 0, first_row, rolled)


def _shift_up(cur, next8, i, n):
    tm = cur.shape[0]
    rolled = pltpu.roll(cur, tm - 1, 0)
    last_row = jnp.where(i < n - 1, next8[0:1, :], 0.0)
    row = lax.broadcasted_iota(jnp.int32, cur.shape, 0)
    return jnp.where(row == tm - 1, last_row, rolled)


def _ada_parts(ada):
    return [ada[:, j * D:(j + 1) * D] for j in range(6)]


def _rope_table(positions):
    half = HD // 8
    inv_freq = 500000.0 ** (-jnp.arange(half, dtype=F32) / half)
    ang = positions.astype(F32)[:, None] * inv_freq
    cos, sin = jnp.cos(ang), jnp.sin(ang)
    T = positions.shape[0]
    pad = HD - 2 * half
    c64 = jnp.concatenate([cos, cos, jnp.ones((T, pad), F32)], axis=1)
    lo64 = jnp.concatenate([-sin, jnp.zeros((T, HD - half), F32)], axis=1)
    hi64 = jnp.concatenate([jnp.zeros((T, half), F32), sin, jnp.zeros((T, pad), F32)], axis=1)
    return jnp.concatenate([jnp.tile(t, (1, NH)) for t in (c64, lo64, hi64)], axis=1)


GATHER_BEHIND = {"f_proj_shift": ['w_out'], "f_proj_gates": ['w_branch_a', 'w_branch_b'], "f_prep": ['ffn_w2_lo'],
                 "f_scan": ['ffn_w1'], "f_post": ['ffn_w2_hi'], "f_attn": ['ffn_w3']}
LATE = ['w_out', 'w_branch_a', 'w_branch_b', 'ffn_w1', 'ffn_w3', 'ffn_w2']
BACK_ATTN = ['w_out', 'w_branch_a', 'w_branch_b', 'ffn_w2']
BACK_SCAN = ['ffn_w1', 'ffn_w3']
BACK_LAST = ['w_in', 'decay_up', 'iclr_up', 'gate_up']


def _full_weight(g, ax):
    return g.reshape(-1, g.shape[2]) if ax == 0 else jnp.concatenate([g[j] for j in range(4)], axis=1)


def _local_step(x, target, ada, tab, w, s, shards=None):
    T = x.shape[0]
    tm = _pick(T, (512, 256, 128))
    tm_wide = _pick(T, (256, 128))
    tm_vjp = _pick(T, (128,))
    row = lambda n, dt=F32: (n, dt, 'row')
    acc = lambda n, r=1: (n, F32, r)

    def f_norm1(i, n, x_, g, ada_):
        sh, sc = ada_[:, 0:D], ada_[:, D:2 * D]
        return (_norm_mod(x_, g, sc, sh),)
    (h1,) = _rowwise(f_norm1, [x], [s['norm1_gain'], ada], [row(D, MXU)], tm=tm, name="f_norm1")

    pieces, late = {}, {}
    if shards:
        half = shards['ffn_w2'].shape[0] // 2
        pieces = dict(shards, ffn_w2_lo=shards['ffn_w2'][:half], ffn_w2_hi=shards['ffn_w2'][half:])

    def behind(kernel_name):
        return _GatherChips([pieces[n] for n in GATHER_BEHIND[kernel_name]]) if shards else None

    def took(kernel_name, got):
        late.update(zip(GATHER_BEHIND[kernel_name], got))

    def mm_behind(a_, w_, kernel_name, **kw):
        ex = behind(kernel_name)
        res = _mm_nn(a_, w_, name=kernel_name, hosted=ex, **kw)
        if ex:
            took(kernel_name, res[1:])
            return res[0]
        return res

    proj = mm_behind(h1, w['w_in'][:, :SHIFT_W], "f_proj_shift")
    proj_qkv = _mm_nn(h1, w['w_in'][:, SHIFT_W:SHIFT_W + QKV_W], name="f_proj_qkv")
    proj_g = mm_behind(h1, w['w_in'][:, SHIFT_W + QKV_W:], "f_proj_gates", out_dtype=MXU)
    prep_consts = [s['decay_w0'], s['lora_up'], s['iclr_a0'], s['gate_up'], s['k_k'], s['k_a']]

    def f_prep(i, n, cur, prev8, mu, *params):
        mixed = cur + (_shift_down(cur, prev8, i) - cur) * mu
        return (_prep(mixed, *params),)
    rw, *got = _rowwise(f_prep, [(proj, SHIFT_W)], [s['tshift_mu']] + prep_consts, [row(7 * RW)], tm=tm_wide,
                        name="f_prep", halo=[(proj, SHIFT_W, 'prev')], hosted=behind("f_prep"))
    took("f_prep", got)
    y, ck, inv, *got = _scan_fwd(rw, name="f_scan", hosted=behind("f_scan"))
    took("f_scan", got)
    post_consts = [s['lnx_gain'], s['lnx_bias'], s['r_k']]

    rkvg = [(rw, RW, j) for j in (0, 2, 3, 6)]

    def f_post(i, n, *args):
        return (_post(*args),)
    ya, *got = _rowwise(f_post, [y] + rkvg, post_consts, [row(RW, MXU)], tm=tm_wide, name="f_post",
                        hosted=behind("f_post"))
    took("f_post", got)
    yb, *got = _attn_fwd(proj_qkv, tab, s['q_norm_gain'], s['k_norm_gain'], s['attn_sinks'], name="f_attn",
                         hosted=behind("f_attn"))
    took("f_attn", got)
    w = dict(w)
    if shards:
        ax = dict(SHARDED)
        w.update({n: _full_weight(late[n], ax[n]) for n in ('w_branch_a', 'w_branch_b', 'w_out')})
        w.update({n: _full_weight(late[n], 1) for n in ('ffn_w1', 'ffn_w3')})
        w['ffn_w2'] = jnp.concatenate([late['ffn_w2_lo'], late['ffn_w2_hi']], axis=1).reshape(-1, D)
    ma = _mm_nn(ya, w['w_branch_a'], name="f_branch_a", out_dtype=MXU)
    mb = _mm_nn(yb, w['w_branch_b'], name="f_branch_b", out_dtype=MXU)

    def f_merge(i, n, pg, ma_, mb_, bias):
        return (_merge(pg.astype(F32), ma_.astype(F32), mb_.astype(F32), bias),)
    (merged,) = _rowwise(f_merge, [proj_g, ma, mb], [s['branch_gate_b']], [row(D, MXU)], tm=tm, name="f_merge")
    mo = _mm_nn(merged, w['w_out'], name="f_out")

    def f_res1(i, n, x_, mo_, g, ada_):
        x1_ = x_ + ada_[:, 2 * D:3 * D] * mo_
        return x1_, _norm_mod(x1_, g, ada_[:, 4 * D:5 * D], ada_[:, 3 * D:4 * D])
    x1, h2 = _rowwise(f_res1, [x, mo], [s['norm2_gain'], ada], [row(D), row(D, MXU)], tm=tm, name="f_res1")
    u, v, act = _ffn_in(h2, w['ffn_w1'], w['ffn_w3'], name="f_ffn_in")
    ff = _mm_nn(act, w['ffn_w2'], name="f_ffn_out")

    def f_loss(i, n, x1_, ff_, tgt, ada_):
        g2 = ada_[:, 5 * D:6 * D]
        err = x1_ + g2 * ff_ - tgt
        dx2 = err * (1.0 / D)
        loss = 0.5 * jnp.sum(jnp.sum(err * err, axis=1, keepdims=True) * (1.0 / D), axis=0, keepdims=True)
        return dx2, (dx2 * g2), jnp.broadcast_to(loss, (1, 128)), jnp.sum(dx2 * ff_, axis=0, keepdims=True)
    dx2, dff, loss, dgate2 = _rowwise(f_loss, [x1, ff, target], [ada], [row(D), row(D, MXU), acc(128), acc(D)],
                                      tm=tm, name="f_loss")

    du, dv = _ffn_act_bwd(dff, w['ffn_w2'], u, v, name="b_ffn_out_dx")
    g_w2 = _mm_tn(act, dff, name="b_ffn_out_dw", out_dtype=MXU)
    dh2 = _mm_nt2(du, w['ffn_w1'], dv, w['ffn_w3'], name="b_ffn_in_dx")
    g_w1 = _mm_tn(h2, du, name="b_ffn_w1_dw", out_dtype=MXU)
    g_w3 = _mm_tn(h2, dv, name="b_ffn_w3_dw", out_dtype=MXU)

    def b_res1(i, n, x1_, dh2_, dx2_, mo_, g, ada_):
        _, vjp = jax.vjp(_norm_mod, x1_, g, ada_[:, 4 * D:5 * D], ada_[:, 3 * D:4 * D])
        dxn, dg, dsc, dsh = vjp(dh2_)
        dx1_ = dxn + dx2_
        g1 = ada_[:, 2 * D:3 * D]
        return dx1_, dx1_ * g1, dg, dsc, dsh, jnp.sum(dx1_ * mo_, axis=0, keepdims=True)
    dx1, dmo, d_gain2, d_scale2, d_shift2, dgate1 = _rowwise(
        b_res1, [x1, dh2, dx2, mo], [s['norm2_gain'], ada], [row(D), row(D, MXU), acc(D), acc(D), acc(D), acc(D)],
        tm=tm, name="b_res1")
    dmerged = _mm_nt(dmo, w['w_out'], name="b_out_dx", out_dtype=MXU)
    g_wout = _mm_tn(merged, dmo, name="b_out_dw", out_dtype=MXU)

    def b_merge(i, n, pg, ma_, mb_, dm, bias):
        _, vjp = jax.vjp(_merge, pg.astype(F32), ma_.astype(F32), mb_.astype(F32), bias)
        dpg, dma_, dmb_, dbias = vjp(dm.astype(F32))
        return dpg, dma_, dmb_, dbias
    dpg, dma, dmb, d_bias = _rowwise(b_merge, [proj_g, ma, mb, dmerged], [s['branch_gate_b']],
                                     [row(GATE_W, MXU), row(D, MXU), row(D, MXU), acc(GATE_W)], tm=tm_wide, name="b_merge")
    dya = _mm_nt(dma, w['w_branch_a'], name="b_branch_a_dx")
    g_wa = _mm_tn(ya, dma, name="b_branch_a_dw", out_dtype=MXU, col_shards=4)
    dyb = _mm_nt(dmb, w['w_branch_b'], name="b_branch_b_dx", out_dtype=F32)
    g_wb = _mm_tn(yb, dmb, name="b_branch_b_dw", out_dtype=MXU, col_shards=4)
    fs = DFF // 4
    gw = dict(w_branch_a=g_wa, w_branch_b=g_wb, w_out=g_wout.reshape(4, D // 4, D),
              ffn_w1=jnp.stack([g_w1[:, j * fs:(j + 1) * fs] for j in range(4)]),
              ffn_w3=jnp.stack([g_w3[:, j * fs:(j + 1) * fs] for j in range(4)]),
              ffn_w2=g_w2.reshape(4, fs, D))
    recv = {}
    dqkv, d_qg, d_kg, d_sinks, *got = _attn_bwd(
        proj_qkv, tab, s['q_norm_gain'], s['k_norm_gain'], s['attn_sinks'], dyb, name="b_attn",
        hosted=shards and _ScatterChips([gw[n] for n in BACK_ATTN]))
    recv.update(zip(BACK_ATTN, got))

    def b_post(i, n, y_, r_, k_, v_, g_, dya_, *params):
        _, vjp = jax.vjp(_post, y_, r_, k_, v_, g_, *params)
        dy_, dr_, dk_, dv_, dg_, *dparams = vjp(dya_)
        return (dy_, jnp.concatenate([dr_, dk_, dv_, dg_], axis=1), *dparams)
    dy, drkvg, d_lnx_gain, d_lnx_bias, d_r_k = _rowwise(
        b_post, [y] + rkvg + [dya], post_consts, [row(RW), row(4 * RW, MXU), acc(RW), acc(RW), acc(RW)], tm=tm_wide,
        name="b_post")
    dscan, *got = _scan_bwd(rw, ck, inv, dy, name="b_scan",
                            hosted=shards and _ScatterChips([gw[n] for n in BACK_SCAN]))
    recv.update(zip(BACK_SCAN, got))

    def b_prep(i, n, cur, drw_, dscan_, prev8, mu, *params):
        shifted = _shift_down(cur, prev8, i)
        mixed = cur + (shifted - cur) * mu
        _, vjp = jax.vjp(_prep, mixed, *params)
        blk = lambda t, j: t[:, j * RW:(j + 1) * RW].astype(F32)
        ct = jnp.concatenate([blk(dscan_, 0) + blk(drw_, 0), blk(dscan_, 1), blk(dscan_, 2) + blk(drw_, 1),
                              blk(dscan_, 3) + blk(drw_, 2), blk(dscan_, 4), blk(dscan_, 5), blk(drw_, 3)], axis=1)
        grads = vjp(ct)
        dmixed = grads[0]
        return (dmixed, jnp.sum(dmixed * (shifted - cur), axis=0, keepdims=True)) + tuple(grads[1:])
    dmixed, d_mu, d_w0, d_lora, d_a0, d_gate_up, d_kk, d_ka = _rowwise(
        b_prep, [(proj, SHIFT_W), drkvg, dscan], [s['tshift_mu']] + prep_consts,
        [row(SHIFT_W), acc(SHIFT_W), acc(RW), acc(2 * RW, 128), acc(RW), acc(RW, 128), acc(RW), acc(RW)],
        tm=tm_vjp, name="b_prep", halo=[(proj, SHIFT_W, 'prev')])

    def b_gather(i, n, dm, dqkv_, dpg_, next8, mu):
        dcur = dm * (1.0 - mu) + _shift_up(dm, next8, i, n) * mu
        return (jnp.concatenate([dcur.astype(MXU), dqkv_, dpg_], axis=1),)
    (dproj,) = _rowwise(b_gather, [dmixed, dqkv, dpg], [s['tshift_mu']], [row(IN_W, MXU)], tm=tm_wide, name="b_gather",
                        halo=[(dmixed, SHIFT_W, 'next')])
    g_win = _mm_tn(h1, dproj, name="b_proj_dw", out_dtype=MXU, col_shards=4)

    def col_blocks(g):
        k, n = g.shape
        return g.reshape(k, 4, n // 4).transpose(1, 0, 2).astype(MXU)
    gw.update(w_in=g_win, decay_up=col_blocks(d_lora[:64, :RW]), iclr_up=col_blocks(d_lora[64:, RW:]),
              gate_up=col_blocks(d_gate_up))
    top, bottom = None, None
    if shards:
        top = _ScatterChips([gw['w_in'][:, :D // 2]] + [gw[n] for n in BACK_LAST[1:]])
        bottom = _ScatterChips([gw['w_in'][:, D // 2:]])
        dh1, *got_top = _mm_nt(dproj, w['w_in'], name="b_proj_dx", hosted=top)
    else:
        dh1 = _mm_nt(dproj, w['w_in'], name="b_proj_dx")

    def b_norm1(i, n, x_, dh1_, dx1_, g, ada_):
        _, vjp = jax.vjp(_norm_mod, x_, g, ada_[:, D:2 * D], ada_[:, 0:D])
        dxn, dg, dsc, dsh = vjp(dh1_)
        return dxn + dx1_, dg, dsc, dsh
    dx, d_gain1, d_scale1, d_shift1, *got_bottom = _rowwise(
        b_norm1, [x, dh1, dx1], [s['norm1_gain'], ada], [row(D), acc(D), acc(D), acc(D)], tm=tm, name="b_norm1",
        hosted=bottom)
    if shards:
        recv.update(zip(BACK_LAST[1:], got_top[1:]))
        recv['w_in'] = jnp.concatenate([got_top[0], got_bottom[0]], axis=1)

    d_ada = jnp.concatenate([d_shift1, d_scale1, dgate1, d_shift2, d_scale2, dgate2], axis=1)
    gs = dict(norm1_gain=d_gain1, norm2_gain=d_gain2, tshift_mu=d_mu, decay_w0=d_w0, iclr_a0=d_a0, k_k=d_kk, k_a=d_ka,
              r_k=d_r_k, lnx_gain=d_lnx_gain, lnx_bias=d_lnx_bias, q_norm_gain=d_qg, k_norm_gain=d_kg,
              attn_sinks=d_sinks, branch_gate_b=d_bias)
    return loss, dx, d_ada, gw, gs, recv


ANY = pl.BlockSpec(memory_space=pl.ANY)


def _place():
    x, y, c = lax.axis_index("x"), lax.axis_index("y"), lax.axis_index("c")
    return x, y, c, [(1 - x, y), (x, 1 - y), (1 - x, 1 - y)]


def _all_gather8(x_shard, *, name):
    m_per, n = x_shard.shape

    def body(x_ref, out_ref, send_sems, recv_sems, local_sem):
        x, y, c, chips = _place()
        me, sibling = (x, y, c), (x, y, 1 - c)

        def rows(px, py, pc):
            return out_ref.at[pl.ds((4 * px + 2 * py + pc) * m_per, m_per), :]

        def copy(k, block, to, src=None):
            return pltpu.make_async_remote_copy(
                src_ref=rows(*block) if src is None else src, dst_ref=rows(*block),
                send_sem=send_sems.at[k], recv_sem=recv_sems.at[k], device_id=to, device_id_type=MESH)

        mine = pltpu.make_async_copy(x_ref, rows(*me), local_sem)
        mine.start()
        first = [copy(0, me, sibling, src=x_ref)]
        first += [copy(1 + j, me, (*chip, c), src=x_ref) for j, chip in enumerate(chips)]
        for cp in first:
            cp.start()
        passed = [copy(4 + j, (*chip, c), sibling) for j, chip in enumerate(chips)]
        for j, chip in enumerate(chips):
            copy(1 + j, (*chip, c), me).wait_recv()
            passed[j].start()
        copy(0, sibling, me).wait_recv()
        for j, chip in enumerate(chips):
            copy(4 + j, (*chip, 1 - c), me).wait_recv()
        for cp in first + passed:
            cp.wait_send()
        mine.wait()

    return pl.pallas_call(
        body, name=name, out_shape=jax.ShapeDtypeStruct((8 * m_per, n), x_shard.dtype),
        in_specs=[pl.BlockSpec(memory_space=pltpu.VMEM)], out_specs=pl.BlockSpec(memory_space=pltpu.VMEM),
        scratch_shapes=[pltpu.SemaphoreType.DMA((7,)), pltpu.SemaphoreType.DMA((7,)), pltpu.SemaphoreType.DMA],
    )(x_shard)


class _GatherChips:
    def __init__(self, shards):
        n = len(shards)
        self.arrays, self.n_in, self.n_out = list(shards), n, n
        self.out_shape = [jax.ShapeDtypeStruct((4,) + s.shape, s.dtype) for s in shards]
        self.scratch = [pltpu.SemaphoreType.DMA((3 * n,)), pltpu.SemaphoreType.DMA((3 * n,)),
                        pltpu.SemaphoreType.DMA((n,))]

    def _copies(self, x_refs, out_refs, sems, receiving):
        send_sems, recv_sems, local_sems = sems
        x, y, c, chips = _place()
        s_me = 2 * x + y
        n = self.n_in

        def copy(a, k, s):
            return pltpu.make_async_remote_copy(
                src_ref=x_refs[a], dst_ref=out_refs[a].at[s], send_sem=send_sems.at[3 * a + k],
                recv_sem=recv_sems.at[3 * a + k], device_id=(*chips[k], c), device_id_type=MESH)

        mine = [pltpu.make_async_copy(x_refs[a], out_refs[a].at[s_me], local_sems.at[a]) for a in range(n)]
        sends = [copy(a, k, s_me) for a in range(n) for k in range(3)]
        if not receiving:
            return mine, sends
        return mine, sends, [copy(a, k, 2 * px + py) for a in range(n) for k, (px, py) in enumerate(chips)]

    def start(self, x_refs, out_refs, sems):
        mine, sends = self._copies(x_refs, out_refs, sems, False)
        for cp in mine + sends:
            cp.start()

    def wait(self, x_refs, out_refs, sems):
        mine, sends, recvs = self._copies(x_refs, out_refs, sems, True)
        for cp in recvs:
            cp.wait_recv()
        for cp in sends:
            cp.wait_send()
        for cp in mine:
            cp.wait()


class _GatherChipsHalved(_GatherChips):
    def __init__(self, shards):
        super().__init__(shards)
        n = self.n_in
        self.scratch = [pltpu.SemaphoreType.DMA((6 * n,)), pltpu.SemaphoreType.DMA((6 * n,)),
                        pltpu.SemaphoreType.DMA((n,))]

    def _copies(self, x_refs, out_refs, sems, receiving):
        send_sems, recv_sems, local_sems = sems
        x, y, c, chips = _place()
        s_me = 2 * x + y
        n = self.n_in

        def half(a, who):
            rows = x_refs[a].shape[0] // 2
            return pl.ds(who * rows, rows)

        def over_chips(a, k, s):
            return pltpu.make_async_remote_copy(
                src_ref=x_refs[a].at[half(a, c)], dst_ref=out_refs[a].at[s, half(a, c)],
                send_sem=send_sems.at[3 * a + k], recv_sem=recv_sems.at[3 * a + k],
                device_id=(*chips[k], c), device_id_type=MESH)

        def to_sibling(a, k, s, who):
            return pltpu.make_async_remote_copy(
                src_ref=out_refs[a].at[s, half(a, who)], dst_ref=out_refs[a].at[s, half(a, who)],
                send_sem=send_sems.at[3 * n + 3 * a + k], recv_sem=recv_sems.at[3 * n + 3 * a + k],
                device_id=(x, y, 1 - c), device_id_type=MESH)

        mine = [pltpu.make_async_copy(x_refs[a], out_refs[a].at[s_me], local_sems.at[a]) for a in range(n)]
        sends = [over_chips(a, k, s_me) for a in range(n) for k in range(3)]
        if not receiving:
            return mine, sends
        pairs = [(a, k, 2 * px + py) for a in range(n) for k, (px, py) in enumerate(chips)]
        landed = [over_chips(a, k, s) for a, k, s in pairs]
        passed_on = [to_sibling(a, k, s, c) for a, k, s in pairs]
        from_sibling = [to_sibling(a, k, s, 1 - c) for a, k, s in pairs]
        return mine, sends, landed, passed_on, from_sibling

    def wait(self, x_refs, out_refs, sems):
        mine, sends, landed, passed_on, from_sibling = self._copies(x_refs, out_refs, sems, True)
        for got, fwd in zip(landed, passed_on, strict=True):
            got.wait_recv()
            fwd.start()
        for cp in from_sibling:
            cp.wait_recv()
        for cp in sends + passed_on:
            cp.wait_send()
        for cp in mine:
            cp.wait()


class _ScatterChips:
    def __init__(self, parts):
        n = len(parts)
        self.arrays, self.n_in, self.n_out = list(parts), n, n
        self.out_shape = [jax.ShapeDtypeStruct((3,) + p.shape[1:], p.dtype) for p in parts]
        self.scratch = [pltpu.SemaphoreType.DMA((3 * n,)), pltpu.SemaphoreType.DMA((3 * n,))]

    def _copies(self, g_refs, out_refs, sems):
        send_sems, recv_sems = sems
        x, y, c, chips = _place()
        return [pltpu.make_async_remote_copy(
            src_ref=g_refs[a].at[2 * px + py], dst_ref=out_refs[a].at[k], send_sem=send_sems.at[3 * a + k],
            recv_sem=recv_sems.at[3 * a + k], device_id=(px, py, c), device_id_type=MESH)
            for a in range(self.n_in) for k, (px, py) in enumerate(chips)]

    def start(self, g_refs, out_refs, sems):
        for cp in self._copies(g_refs, out_refs, sems):
            cp.start()

    def wait(self, g_refs, out_refs, sems):
        sends = self._copies(g_refs, out_refs, sems)
        for cp in sends:
            cp.wait_recv()
        for cp in sends:
            cp.wait_send()


def _exchange_call(ex, *, name):
    def body(*refs):
        parts = (refs[:ex.n_in], refs[ex.n_in:ex.n_in + ex.n_out], refs[ex.n_in + ex.n_out:])
        ex.start(*parts)
        ex.wait(*parts)

    return pl.pallas_call(body, name=name, out_shape=ex.out_shape, in_specs=[ANY] * ex.n_in,
                          out_specs=[ANY] * ex.n_out, scratch_shapes=ex.scratch)(*ex.arrays)


def _swap_sibling(vs, *, name):
    n = len(vs)

    def body(*refs):
        v_refs, out_refs = refs[:n], refs[n:2 * n]
        send_sems, recv_sems = refs[2 * n:]
        x, y, c, _ = _place()
        cps = [pltpu.make_async_remote_copy(src_ref=v_refs[a], dst_ref=out_refs[a], send_sem=send_sems.at[a],
                                            recv_sem=recv_sems.at[a], device_id=(x, y, 1 - c), device_id_type=MESH)
               for a in range(n)]
        for cp in cps:
            cp.start()
        for cp in cps:
            cp.wait()

    return pl.pallas_call(
        body, name=name, out_shape=[jax.ShapeDtypeStruct(v.shape, v.dtype) for v in vs],
        in_specs=[ANY] * n, out_specs=[ANY] * n,
        scratch_shapes=[pltpu.SemaphoreType.DMA((n,)), pltpu.SemaphoreType.DMA((n,))],
    )(*vs)


def _sum_parts(own, others, *, name):
    R, C = own.shape
    tm = _pick(R, (256, 128, 64))

    def body(own_ref, o0_ref, o1_ref, o2_ref, out_ref):
        tot = own_ref[...].astype(F32)
        for ref in (o0_ref, o1_ref, o2_ref):
            tot = tot + ref[...].astype(F32)
        out_ref[...] = tot

    part = lambda k: pl.BlockSpec((None, tm, C), lambda i: (k, i, 0))
    return pl.pallas_call(
        body, name=name, grid=(R // tm,),
        in_specs=[pl.BlockSpec((tm, C), lambda i: (i, 0)), part(0), part(1), part(2)],
        out_specs=pl.BlockSpec((tm, C), lambda i: (i, 0)), out_shape=jax.ShapeDtypeStruct((R, C), F32),
        compiler_params=_cparams(("arbitrary",)),
    )(own, others, others, others)


def _adam_math(w_, m_, v_, g):
    m2 = ADAM_B1 * m_ + (1.0 - ADAM_B1) * g
    v2 = ADAM_B2 * v_ + (1.0 - ADAM_B2) * jnp.square(g)
    m_hat = m2 / (1.0 - ADAM_B1 ** ADAM_STEP)
    v_hat = v2 / (1.0 - ADAM_B2 ** ADAM_STEP)
    delta = -ADAM_LR * (m_hat / (jnp.sqrt(v_hat) + ADAM_EPS) + ADAM_WD * w_)
    return delta, m2, v2


SMALL_SLOTS = 16
SMALL_COLS = 6 * D


def _pack_small(grads, *, name):
    n = len(grads)

    def body(*refs):
        out_ref = refs[n]
        out_ref[...] = jnp.zeros_like(out_ref)
        for i, ref in enumerate(refs[:n]):
            out_ref[i:i + 1, 0:ref.shape[1]] = ref[...]

    return pl.pallas_call(body, name=name, out_shape=jax.ShapeDtypeStruct((SMALL_SLOTS, SMALL_COLS), F32))(*grads)


def _adamw_small(ws, ms, vs, gathered, *, name):
    n = len(ws)

    def body(*refs):
        w_refs, m_refs, v_refs, g_ref = refs[:n], refs[n:2 * n], refs[2 * n:3 * n], refs[3 * n]
        outs = refs[3 * n + 1:]
        for i in range(n):
            nc = w_refs[i].shape[1]
            g = g_ref[i:i + 1, 0:nc]
            for d in range(1, 8):
                g = g + g_ref[d * SMALL_SLOTS + i:d * SMALL_SLOTS + i + 1, 0:nc]
            delta, m2, v2 = _adam_math(w_refs[i][...], m_refs[i][...], v_refs[i][...], g)
            for k, val in enumerate((g, delta, m2, v2)):
                outs[k * n + i][...] = val

    shapes = [jax.ShapeDtypeStruct(w.shape, F32) for w in ws]
    res = pl.pallas_call(body, name=name, out_shape=shapes * 4,
                         compiler_params=pltpu.CompilerParams(vmem_limit_bytes=VMEM_LIMIT))(*ws, *ms, *vs, gathered)
    return [res[k * n:(k + 1) * n] for k in range(4)]


def _adamw(w, m, v, gparts, *, tm, name):
    def fn(i, n, w_, m_, v_, *gs):
        g = gs[0]
        for p in gs[1:]:
            g = g + p
        return (g,) + _adam_math(w_, m_, v_, g)
    nc = w.shape[1]
    return _rowwise(fn, [w, m, v] + list(gparts), [], [(nc, F32, 'row')] * 4, tm=tm, name=name)


WEIGHTS = ['ada_w', 'ada_b', 'norm1_gain', 'norm2_gain', 'w_in', 'tshift_mu', 'decay_w0', 'decay_up', 'iclr_a0',
           'iclr_up', 'gate_up', 'k_k', 'k_a', 'r_k', 'lnx_gain', 'lnx_bias', 'q_norm_gain', 'k_norm_gain', 'attn_sinks',
           'branch_gate_b', 'w_branch_a', 'w_branch_b', 'w_out', 'ffn_w1', 'ffn_w3', 'ffn_w2']
SHARDED = [('w_in', 1), ('decay_up', 1), ('iclr_up', 1), ('gate_up', 1), ('w_branch_a', 1), ('w_branch_b', 1),
           ('w_out', 0), ('ffn_w1', 1), ('ffn_w3', 1), ('ffn_w2', 0)]
SMALL = ['ada_b', 'norm1_gain', 'norm2_gain', 'tshift_mu', 'decay_w0', 'iclr_a0', 'k_k', 'k_a', 'r_k', 'lnx_gain',
         'lnx_bias', 'q_norm_gain', 'k_norm_gain', 'attn_sinks', 'branch_gate_b']


def kernel(x, c, positions, ada_w, ada_b, norm1_gain, norm2_gain, w_in, tshift_mu, decay_w0, decay_up, iclr_a0, iclr_up, gate_up, k_k, k_a, r_k, lnx_gain, lnx_bias, q_norm_gain, k_norm_gain, attn_sinks, branch_gate_b, w_branch_a, w_branch_b, w_out, ffn_w1, ffn_w3, ffn_w2, loss_target, m_ada_w, m_ada_b, m_norm1_gain, m_norm2_gain, m_w_in, m_tshift_mu, m_decay_w0, m_decay_up, m_iclr_a0, m_iclr_up, m_gate_up, m_k_k, m_k_a, m_r_k, m_lnx_gain, m_lnx_bias, m_q_norm_gain, m_k_norm_gain, m_attn_sinks, m_branch_gate_b, m_w_branch_a, m_w_branch_b, m_w_out, m_ffn_w1, m_ffn_w3, m_ffn_w2, v_ada_w, v_ada_b, v_norm1_gain, v_norm2_gain, v_w_in, v_tshift_mu, v_decay_w0, v_decay_up, v_iclr_a0, v_iclr_up, v_gate_up, v_k_k, v_k_a, v_r_k, v_lnx_gain, v_lnx_bias, v_q_norm_gain, v_k_norm_gain, v_attn_sinks, v_branch_gate_b, v_w_branch_a, v_w_branch_b, v_w_out, v_ffn_w1, v_ffn_w3, v_ffn_w2):
    a = dict(locals())
    W = {n: a[n] for n in WEIGHTS}
    M = {n: a['m_' + n] for n in WEIGHTS}
    V = {n: a['v_' + n] for n in WEIGHTS}
    xi, yi, ci = lax.axis_index("x"), lax.axis_index("y"), lax.axis_index("c")
    me = 4 * xi + 2 * yi + ci
    shard = 2 * xi + yi
    mat = lambda t: t.reshape(t.shape[-2], t.shape[-1])
    sharded = [n for n, _ in SHARDED]

    ax = dict(SHARDED)
    late = LATE
    early = [n for n in sharded if n not in late]
    shards = {n: mat(W[n]).astype(MXU) for n in sharded}
    gathered = _exchange_call(_GatherChipsHalved([shards[n] for n in early]), name="gather_weights")
    full = {n: _full_weight(g, ax[n]) for n, g in zip(early, gathered, strict=True)}

    c_all = _all_gather8(jnp.broadcast_to(c, (8, D)), name="gather_c")[0::8]
    pad_rows = lambda t: jnp.concatenate([t, jnp.zeros((BLK - 8, t.shape[1]), t.dtype)])
    c_all = pad_rows(c_all.astype(MXU))
    ada_cols = _mm_nn(c_all, mat(ada_w).astype(MXU), name="f_ada")[:8]
    ada_all = _all_gather8(ada_cols, name="gather_ada").reshape(2, 2, 2, 8, 6 * D // 4)
    ada_mine = lax.dynamic_index_in_dim(ada_all[:, :, 0], me, axis=2, keepdims=False)
    ada = ada_mine.reshape(1, 6 * D) + mat(ada_b)

    zero = jnp.zeros((64, RW), MXU)
    lora = jnp.concatenate([jnp.concatenate([full['decay_up'], zero], axis=1),
                            jnp.concatenate([zero, full['iclr_up']], axis=1)], axis=0)
    s = {n: W[n].reshape(1, -1) for n in SMALL if n != 'ada_b'}
    s['lora_up'] = lora.astype(F32)
    s['gate_up'] = full['gate_up'].astype(F32)
    tab = _rope_table(positions.reshape(-1))
    loss, dx, d_ada, gw, gs, from_chips = _local_step(x[0], loss_target[0], ada, tab, dict(w_in=full['w_in']), s,
                                                      shards={n: shards[n] for n in late})
    loss = lax.psum(loss[0, 0], ("x", "y", "c"))

    gs['ada_b'] = d_ada
    gsmall = _pack_small([gs[n] for n in SMALL], name="pack_small_grads")
    gsmall_all = _all_gather8(gsmall, name="gather_small_grads")
    row = lambda src: [src[n].reshape(1, -1) for n in SMALL]
    sm_out = _adamw_small(row(W), row(M), row(V), gsmall_all, name="adamw_small")
    sm_out = [{n: o.reshape(W[n].shape) for n, o in zip(SMALL, outs_k, strict=True)} for outs_k in sm_out]

    d_ada_all = gsmall_all[0::SMALL_SLOTS]
    d_ada_cols = lax.dynamic_slice_in_dim(d_ada_all, shard * (6 * D // 4), 6 * D // 4, axis=1)
    g_ada_w = _mm_tn(c_all, pad_rows(d_ada_cols.astype(MXU)), name="b_ada")
    ada_out = _adamw(mat(ada_w), mat(m_ada_w), mat(v_ada_w), [g_ada_w], tm=256, name="adamw_ada")

    parts =[_sum_parts(lax.dynamic_index_in_dim(gw[n], shard, axis=0, keepdims=False), from_chips[n], name="sum_" + n)
             for n in sharded]
    others = _swap_sibling(parts, name="swap_grads")
    sh_out = {}
    for n, part, other in zip(sharded, parts, others, strict=True):
        sh_out[n] = _adamw(mat(W[n]), mat(M[n]), mat(V[n]), [part, other], tm=_pick(part.shape[0], (256, 128, 64)),
                           name="adamw_" + n)

    def leaf(k, n):
        if n == 'ada_w':
            return ada_out[k].reshape(W[n].shape)
        if n in sharded:
            return sh_out[n][k].reshape(W[n].shape)
        return sm_out[k][n]
    outs = [leaf(k, n) for k in range(4) for n in WEIGHTS]
    return (loss, dx[None], *outs)
```

```python
import functools
import math

import jax
import jax.numpy as jnp
from jax import lax
from jax.experimental import pallas as pl
from jax.experimental.pallas import tpu as pltpu

F32 = jnp.float32
BF16 = jnp.bfloat16
MXU = BF16
HI = lax.Precision.HIGHEST

D = 1024
HD = 64
NH = 8
RW = NH * HD
SHIFT_W = 3 * RW + 64 + 64 + 128
QKV_W = RW + 2 * 128
GATE_W = 2 * D
IN_W = SHIFT_W + QKV_W + GATE_W
DFF = 2816
BLK = 128
CHUNK = 64
RMS_EPS = 1e-6
GN_EPS = 64e-5
NEG_INF = -1e30
ADAM_LR, ADAM_B1, ADAM_B2, ADAM_EPS, ADAM_WD, ADAM_STEP = 0.001, 0.9, 0.999, 1e-08, 0.01, 10
VMEM_LIMIT = 56 * 1024 * 1024
MESH = pl.DeviceIdType.MESH


def _cparams(sem=None):
    return pltpu.CompilerParams(dimension_semantics=sem, vmem_limit_bytes=VMEM_LIMIT)


def _full_spec(a):
    nd = a.ndim
    return pl.BlockSpec(a.shape, lambda *_: (0,) * nd)


def _rowwise(fn, rows, consts, outs, *, tm, name, halo=(), hosted=None):
    rows = [(a + (0,))[:3] if isinstance(a, tuple) else (a, a.shape[1], 0) for a in rows]
    T = rows[0][0].shape[0]
    assert T % tm == 0 and tm % 8 == 0
    n_tiles = T // tm
    n_in = len(rows) + len(halo) + len(consts)
    in_specs = [pl.BlockSpec((tm, nc), lambda i, j=j: (i, j)) for _, nc, j in rows]
    args = [a for a, _, _ in rows]
    for a, nc, kind in halo:
        if kind == 'prev':
            in_specs.append(pl.BlockSpec((8, nc), lambda i: (jnp.maximum(i * (tm // 8) - 1, 0), 0)))
        else:
            in_specs.append(pl.BlockSpec((8, nc), lambda i: (jnp.minimum((i + 1) * (tm // 8), T // 8 - 1), 0)))
        args.append(a)
    in_specs += [_full_spec(a) for a in consts]
    args += list(consts)
    out_shape, out_specs = [], []
    for ncols, dtype, kind in outs:
        if kind == 'row':
            out_shape.append(jax.ShapeDtypeStruct((T, ncols), dtype))
            out_specs.append(pl.BlockSpec((tm, ncols), lambda i: (i, 0)))
        else:
            out_shape.append(jax.ShapeDtypeStruct((kind, ncols), dtype))
            out_specs.append(pl.BlockSpec((kind, ncols), lambda i: (0, 0)))

    def body(*refs):
        i = pl.program_id(0)
        vals = [r[...] for r in refs[:n_in]]
        res = fn(i, n_tiles, *vals)
        for (ncols, dtype, kind), o_ref, val in zip(outs, refs[n_in:], res, strict=True):
            if kind == 'row':
                o_ref[...] = val.astype(dtype)
            else:
                @pl.when(i == 0)
                def _():
                    o_ref[...] = jnp.zeros_like(o_ref)
                o_ref[...] += val.astype(dtype)

    h_in, h_in_specs, h_out_specs, h_out_shape, h_scratch = _hosted_args(hosted)
    res = pl.pallas_call(
        _hosting(body, hosted, n_in, len(outs), 0, n_tiles), name=name, grid=(n_tiles,),
        in_specs=in_specs + h_in_specs, out_specs=out_specs + h_out_specs, out_shape=out_shape + h_out_shape,
        scratch_shapes=h_scratch, compiler_params=_cparams(("arbitrary",)),
    )(*args, *h_in)
    return res


def _pick(n, cands):
    for c in cands:
        if n % c == 0:
            return c
    return n


MM_ROWS = (1024, 512, 256, 128)
MM_COLS = (1536, 1408, 1024, 896, 768, 512, 256, 128)
MM_WIDE = 3000


def _mm_nn(a, w, *, name, out_dtype=F32, hosted=None):
    T, K = a.shape
    N = w.shape[1]
    tm = _pick(T, MM_ROWS)
    tn = _pick(N, MM_COLS)
    grid = (N // tn, T // tm)
    h_in, h_in_specs, h_out_specs, h_out_shape, h_scratch = _hosted_args(hosted)

    def body(a_ref, w_ref, o_ref):
        o_ref[...] = jnp.dot(a_ref[...], w_ref[...], preferred_element_type=F32).astype(out_dtype)

    res = pl.pallas_call(
        _hosting(body, hosted, 2, 1, 0, grid), name=name, grid=grid,
        in_specs=[pl.BlockSpec((tm, K), lambda j, i: (i, 0)), pl.BlockSpec((K, tn), lambda j, i: (0, j))] + h_in_specs,
        out_specs=[pl.BlockSpec((tm, tn), lambda j, i: (i, j))] + h_out_specs,
        out_shape=[jax.ShapeDtypeStruct((T, N), out_dtype)] + h_out_shape, scratch_shapes=h_scratch,
        compiler_params=_cparams(("arbitrary", "arbitrary")),
    )(a, w, *h_in)
    return res if hosted else res[0]


def _mm_nt(dy, w, *, name, out_dtype=F32, hosted=None):
    T, N = dy.shape
    K = w.shape[0]
    tm = _pick(T, MM_ROWS if N <= MM_WIDE else MM_ROWS[1:])
    tk = _pick(K, MM_COLS[1:])
    grid = (K // tk, T // tm)
    h_in, h_in_specs, h_out_specs, h_out_shape, h_scratch = _hosted_args(hosted)

    def body(dy_ref, w_ref, o_ref):
        o_ref[...] = lax.dot_general(dy_ref[...], w_ref[...], (((1,), (1,)), ((), ())),
                                     preferred_element_type=F32).astype(out_dtype)

    res = pl.pallas_call(
        _hosting(body, hosted, 2, 1, 0, grid), name=name, grid=grid,
        in_specs=[pl.BlockSpec((tm, N), lambda j, i: (i, 0)), pl.BlockSpec((tk, N), lambda j, i: (j, 0))] + h_in_specs,
        out_specs=[pl.BlockSpec((tm, tk), lambda j, i: (i, j))] + h_out_specs,
        out_shape=[jax.ShapeDtypeStruct((T, K), out_dtype)] + h_out_shape, scratch_shapes=h_scratch,
        compiler_params=_cparams(("arbitrary", "arbitrary")),
    )(dy, w, *h_in)
    return res if hosted else res[0]


def _mm_tn(a, dy, *, name, out_dtype=F32, col_shards=None):
    T, K = a.shape
    N = dy.shape[1]
    tm = _pick(T, MM_ROWS)
    tn = N // col_shards if col_shards else _pick(N, MM_COLS[1:])
    n_t = T // tm

    def body(a_ref, dy_ref, o_ref, acc_ref):
        i = pl.program_id(1)

        @pl.when(i == 0)
        def _():
            acc_ref[...] = jnp.zeros_like(acc_ref)

        acc_ref[...] += lax.dot_general(a_ref[...], dy_ref[...], (((0,), (0,)), ((), ())), preferred_element_type=F32)

        @pl.when(i == n_t - 1)
        def _():
            o_ref[...] = acc_ref[...].astype(out_dtype)

    if col_shards:
        out_specs = pl.BlockSpec((None, K, tn), lambda j, i: (j, 0, 0))
        out_shape = jax.ShapeDtypeStruct((col_shards, K, tn), out_dtype)
    else:
        out_specs = pl.BlockSpec((K, tn), lambda j, i: (0, j))
        out_shape = jax.ShapeDtypeStruct((K, N), out_dtype)
    return pl.pallas_call(
        body, name=name, grid=(N // tn, n_t),
        in_specs=[pl.BlockSpec((tm, K), lambda j, i: (i, 0)), pl.BlockSpec((tm, tn), lambda j, i: (i, j))],
        out_specs=out_specs, out_shape=out_shape, scratch_shapes=[pltpu.VMEM((K, tn), F32)],
        compiler_params=_cparams(("arbitrary", "arbitrary")),
    )(a, dy)


def _seg_ones(n):
    r = lax.broadcasted_iota(jnp.int32, (n, n), 0) // HD
    c = lax.broadcasted_iota(jnp.int32, (n, n), 1) // HD
    return (r == c).astype(F32)


def _segsum_raw(x):
    ones = _seg_ones(x.shape[1])
    if MXU == F32:
        return jnp.dot(x, ones, precision=HI, preferred_element_type=F32)
    hi = x.astype(MXU)
    lo = (x - hi.astype(F32)).astype(MXU)
    ones = ones.astype(MXU)
    return jnp.dot(hi, ones, preferred_element_type=F32) + jnp.dot(lo, ones, preferred_element_type=F32)


@jax.custom_vjp
def _segsum(x):
    return _segsum_raw(x)


def _segsum_fwd(x):
    return _segsum_raw(x), None


def _segsum_bwd(_, g):
    return (_segsum_raw(g),)


_segsum.defvjp(_segsum_fwd, _segsum_bwd)


def _mxu(x):
    return x.astype(MXU)


@jax.custom_vjp
def _bdot(a, b):
    return jnp.dot(_mxu(a), _mxu(b), preferred_element_type=F32)


def _bdot_fwd(a, b):
    return _bdot(a, b), (a, b)


def _bdot_bwd(res, g):
    a, b = res
    da = lax.dot_general(_mxu(g), _mxu(b), (((1,), (1,)), ((), ())), preferred_element_type=F32)
    db = lax.dot_general(_mxu(a), _mxu(g), (((0,), (0,)), ((), ())), preferred_element_type=F32)
    return da.astype(a.dtype), db.astype(b.dtype)


_bdot.defvjp(_bdot_fwd, _bdot_bwd)


@jax.custom_vjp
def _bdot_nt(a, b):
    return lax.dot_general(_mxu(a), _mxu(b), (((1,), (1,)), ((), ())), preferred_element_type=F32)


def _bdot_nt_fwd(a, b):
    return _bdot_nt(a, b), (a, b)


def _bdot_nt_bwd(res, g):
    a, b = res
    da = jnp.dot(_mxu(g), _mxu(b), preferred_element_type=F32)
    db = lax.dot_general(_mxu(g), _mxu(a), (((0,), (0,)), ((), ())), preferred_element_type=F32)
    return da.astype(a.dtype), db.astype(b.dtype)


_bdot_nt.defvjp(_bdot_nt_fwd, _bdot_nt_bwd)


def _sigmoid(x):
    return 1.0 / (1.0 + jnp.exp(-x))


def _softplus(x):
    return jnp.maximum(x, 0.0) + jnp.log(1.0 + jnp.exp(jnp.minimum(x, -x)))


def _norm_mod(x, gain, scale, shift):
    inv = lax.rsqrt(jnp.mean(x * x, axis=-1, keepdims=True) + RMS_EPS)
    return (x * inv) * gain * (1.0 + scale) + shift


def _prep(mixed, decay_w0, lora_up, iclr_a0, gate_up, k_k, k_a):
    r = mixed[:, 0:RW]
    k = mixed[:, RW:2 * RW]
    v = mixed[:, 2 * RW:3 * RW]
    z = mixed[:, 3 * RW:3 * RW + 128]
    xg = mixed[:, 3 * RW + 128:]
    lane = lax.broadcasted_iota(jnp.int32, z.shape, 1)
    tz = jnp.where(lane < 64, jnp.tanh(z), z)
    lo = _bdot(tz, lora_up)
    w_log = -_softplus(-(decay_w0 + lo[:, :RW])) - 0.5
    lw = -jnp.exp(w_log)
    a_ic = _sigmoid(iclr_a0 + lo[:, RW:])
    g = _bdot(_sigmoid(xg), gate_up)
    kk = k * k_k
    kk = kk / jnp.maximum(jnp.sqrt(_segsum(kk * kk)), 1e-12)
    k_mod = k * (1.0 + (a_ic - 1.0) * k_a)
    return jnp.concatenate([r, lw, k_mod, v, -kk, kk * a_ic, g], axis=1)


def _post(y, r, k, v, g, lnx_gain, lnx_bias, r_k):
    mu = _segsum(y) * (1.0 / HD)
    yc = y - mu
    var = _segsum(yc * yc) * (1.0 / HD)
    yn = yc * lax.rsqrt(var + GN_EPS) * lnx_gain + lnx_bias
    bonus = _segsum(r * k * r_k) * v
    return (yn + bonus) * g


def _merge(pg, ma, mb, bias):
    gates = _sigmoid(pg + bias)
    return gates[:, :D] * ma + gates[:, D:] * mb


def _swiglu(u, v):
    return u * _sigmoid(u) * v


def _ffn_in(h, w1, w3, *, name):
    T, K = h.shape
    F = w1.shape[1]
    tm = _pick(T, MM_ROWS)
    tn = _pick(F, MM_COLS[1:])

    def body(h_ref, w1_ref, w3_ref, u_ref, v_ref, a_ref):
        u = jnp.dot(h_ref[...], w1_ref[...], preferred_element_type=F32).astype(MXU)
        v = jnp.dot(h_ref[...], w3_ref[...], preferred_element_type=F32).astype(MXU)
        u_ref[...] = u
        v_ref[...] = v
        a_ref[...] = _swiglu(u.astype(F32), v.astype(F32)).astype(MXU)

    wspec = pl.BlockSpec((K, tn), lambda j, i: (0, j))
    ospec = pl.BlockSpec((tm, tn), lambda j, i: (i, j))
    return pl.pallas_call(
        body, name=name, grid=(F // tn, T // tm),
        in_specs=[pl.BlockSpec((tm, K), lambda j, i: (i, 0)), wspec, wspec],
        out_specs=[ospec] * 3, out_shape=[jax.ShapeDtypeStruct((T, F), MXU)] * 3,
        compiler_params=_cparams(("arbitrary", "arbitrary")),
    )(h, w1, w3)


def _ffn_act_bwd(dff, w2, u, v, *, name):
    T, N = dff.shape
    F = w2.shape[0]
    tm = _pick(T, MM_ROWS)
    tk = _pick(F, MM_COLS[1:])

    def body(dy_ref, w_ref, u_ref, v_ref, du_ref, dv_ref):
        dact = lax.dot_general(dy_ref[...], w_ref[...], (((1,), (1,)), ((), ())), preferred_element_type=F32)
        _, vjp = jax.vjp(_swiglu, u_ref[...].astype(F32), v_ref[...].astype(F32))
        du, dv = vjp(dact)
        du_ref[...] = du.astype(MXU)
        dv_ref[...] = dv.astype(MXU)

    tile = pl.BlockSpec((tm, tk), lambda j, i: (i, j))
    return pl.pallas_call(
        body, name=name, grid=(F // tk, T // tm),
        in_specs=[pl.BlockSpec((tm, N), lambda j, i: (i, 0)), pl.BlockSpec((tk, N), lambda j, i: (j, 0)), tile, tile],
        out_specs=[tile, tile], out_shape=[jax.ShapeDtypeStruct((T, F), MXU)] * 2,
        compiler_params=_cparams(("arbitrary", "arbitrary")),
    )(dff, w2, u, v)


def _mm_nt2(dy1, w1, dy2, w2, *, name):
    T, N = dy1.shape
    K = w1.shape[0]
    tm = _pick(T, MM_ROWS[1:])
    tk = _pick(K, MM_COLS[1:])

    def body(d1_ref, w1_ref, d2_ref, w2_ref, o_ref):
        nt = lambda a, b: lax.dot_general(a[...], b[...], (((1,), (1,)), ((), ())), preferred_element_type=F32)
        o_ref[...] = nt(d1_ref, w1_ref) + nt(d2_ref, w2_ref)

    dspec = pl.BlockSpec((tm, N), lambda j, i: (i, 0))
    wspec = pl.BlockSpec((tk, N), lambda j, i: (j, 0))
    return pl.pallas_call(
        body, name=name, grid=(K // tk, T // tm), in_specs=[dspec, wspec, dspec, wspec],
        out_specs=pl.BlockSpec((tm, tk), lambda j, i: (i, j)), out_shape=jax.ShapeDtypeStruct((T, K), F32),
        compiler_params=_cparams(("arbitrary", "arbitrary")),
    )(dy1, w1, dy2, w2)


@functools.partial(jax.custom_vjp, nondiff_argnums=(1,))
def _lane_roll(x, s):
    return pltpu.roll(x, s, 1)


def _lane_roll_fwd(x, s):
    return pltpu.roll(x, s, 1), None


def _lane_roll_bwd(s, _, g):
    n = g.shape[1]
    return (pltpu.roll(g, (n - s) % n, 1),)


_lane_roll.defvjp(_lane_roll_fwd, _lane_roll_bwd)


def _rope(x, cos, sin_lo, sin_hi):
    n = x.shape[1]
    return x * cos + _lane_roll(x, n - 8) * sin_lo + _lane_roll(x, 8) * sin_hi


def _head_rms(x, gain):
    return x * lax.rsqrt(_segsum(x * x) * (1.0 / HD) + RMS_EPS) * gain


def _attn_block(qkv_c, qkv_p, tab_c, tab_p, qg, kg, sinks, first):
    def tabs(tab, n):
        return tab[:, 0:n], tab[:, RW:RW + n], tab[:, 2 * RW:2 * RW + n]

    qg = jnp.concatenate([qg] * NH, axis=1)
    kg = jnp.concatenate([kg] * 2, axis=1)
    q = _rope(_head_rms(qkv_c[:, :RW], qg), *tabs(tab_c, RW))
    k_c = _rope(_head_rms(qkv_c[:, RW:RW + 128], kg), *tabs(tab_c, 128))
    k_p = _rope(_head_rms(qkv_p[:, RW:RW + 128], kg), *tabs(tab_p, 128))
    kband = jnp.concatenate([k_p, k_c], axis=0)
    vband = jnp.concatenate([qkv_p[:, RW + 128:], qkv_c[:, RW + 128:]], axis=0)
    G = ATTN_STACK
    qi = lax.broadcasted_iota(jnp.int32, (G * BLK, 2 * BLK), 0) % BLK
    kj = lax.broadcasted_iota(jnp.int32, (G * BLK, 2 * BLK), 1)
    dist = qi + BLK - kj
    valid = (dist >= 0) & (dist < BLK) & (jnp.logical_not(first) | (kj >= BLK))
    row_g = lax.broadcasted_iota(jnp.int32, (G * BLK, 1), 0) // BLK
    outs = []
    for h0 in range(0, NH, G):
        kvh = h0 // 4
        kb = kband[:, kvh * HD:(kvh + 1) * HD]
        vb = vband[:, kvh * HD:(kvh + 1) * HD]
        qs = jnp.concatenate([q[:, (h0 + g) * HD:(h0 + g + 1) * HD] for g in range(G)], axis=0)
        s = _bdot_nt(qs, kb) * (HD ** -0.5)
        s = jnp.where(valid, s, NEG_INF)
        sink = jnp.zeros((G * BLK, 1), F32)
        for g in range(G):
            sink = jnp.where(row_g == g, sinks[:, h0 + g:h0 + g + 1], sink)
        m = lax.stop_gradient(jnp.maximum(jnp.max(s, axis=-1, keepdims=True), sink))
        e = jnp.exp(s - m)
        p = e * (1.0 / (jnp.sum(e, axis=-1, keepdims=True) + jnp.exp(sink - m)))
        o = _bdot(p, vb)
        outs += [o[g * BLK:(g + 1) * BLK] for g in range(G)]
    return jnp.concatenate(outs, axis=1)


def _attn_blocks(qkv_c, qkv_p, tab_c, tab_p, qg, kg, sinks, first):
    nb = qkv_c.shape[0] // BLK
    G = 4

    def tabs(tab, n):
        return tab[:, 0:n], tab[:, RW:RW + n], tab[:, 2 * RW:2 * RW + n]

    qg = jnp.concatenate([qg] * NH, axis=1)
    kg = jnp.concatenate([kg] * 2, axis=1)
    q = _rope(_head_rms(qkv_c[:, :RW], qg), *tabs(tab_c, RW))
    k_in = jnp.concatenate([qkv_p[:, RW:RW + 128], qkv_c[:, RW:RW + 128]], axis=0)
    k = _rope(_head_rms(k_in, kg), *tabs(jnp.concatenate([tab_p, tab_c], axis=0), 128))
    v = jnp.concatenate([qkv_p[:, RW + 128:], qkv_c[:, RW + 128:]], axis=0)

    pile = lambda xs: jnp.concatenate([x_[None] for x_ in xs], axis=0)

    def bands(t):
        return pile([t[b * BLK:(b + 2) * BLK, kvh * HD:(kvh + 1) * HD] for kvh in range(2) for b in range(nb)])

    qs = pile([jnp.concatenate([q[b * BLK:(b + 1) * BLK, (G * kvh + g) * HD:(G * kvh + g + 1) * HD]
                                for g in range(G)], axis=0) for kvh in range(2) for b in range(nb)])
    s = _bmm(qs, bands(k), 2, 2, 1) * (HD ** -0.5)
    qi = lax.broadcasted_iota(jnp.int32, (G * BLK, 2 * BLK), 0) % BLK
    kj = lax.broadcasted_iota(jnp.int32, (G * BLK, 2 * BLK), 1)
    dist = qi + BLK - kj
    in_band = (dist >= 0) & (dist < BLK)
    pair = lax.broadcasted_iota(jnp.int32, (2 * nb, 1, 1), 0)
    no_prev = (pair % nb == 0) & first
    valid = in_band[None] & (jnp.logical_not(no_prev) | (kj >= BLK)[None])
    s = jnp.where(valid, s, NEG_INF)
    row_g = lax.broadcasted_iota(jnp.int32, (G * BLK, 1), 0) // BLK
    sink = []
    for kvh in range(2):
        col = jnp.zeros((G * BLK, 1), F32)
        for g in range(G):
            col = jnp.where(row_g == g, sinks[:, G * kvh + g:G * kvh + g + 1], col)
        sink += [col] * nb
    sink = pile(sink)
    m = lax.stop_gradient(jnp.maximum(jnp.max(s, axis=-1, keepdims=True), sink))
    e = jnp.exp(s - m)
    p = e * (1.0 / (jnp.sum(e, axis=-1, keepdims=True) + jnp.exp(sink - m)))
    o = _bmm(p, bands(v), 2, 1, 1)
    return jnp.concatenate([jnp.concatenate([o[kvh * nb + b, g * BLK:(g + 1) * BLK] for kvh in range(2)
                                             for g in range(G)], axis=1) for b in range(nb)], axis=0)


def _heads(x):
    return jnp.stack([x[:, h * HD:(h + 1) * HD] for h in range(NH)], axis=0)


def _unheads(x):
    return jnp.concatenate([x[h] for h in range(NH)], axis=1)


def _split(x, n):
    parts, rest = [], x
    for _ in range(n):
        p = rest.astype(MXU)
        parts.append(p)
        rest = rest - p.astype(F32)
    return parts


def _bdot_batched(a, b, ca, cb):
    return lax.dot_general(a, b, (((ca,), (cb,)), ((0,), (0,))), preferred_element_type=F32)


def _bmm_passes(a, b, ca, cb, passes):
    if MXU == F32:
        return lax.dot_general(a, b, (((ca,), (cb,)), ((0,), (0,))), precision=HI, preferred_element_type=F32)
    if passes == 1:
        return _bdot_batched(a.astype(MXU), b.astype(MXU), ca, cb)
    (a0, a1), (b0, b1) = _split(a, 2), _split(b, 2)
    return _bdot_batched(a0, b0, ca, cb) + (_bdot_batched(a0, b1, ca, cb) + _bdot_batched(a1, b0, ca, cb))


@functools.partial(jax.custom_vjp, nondiff_argnums=(2, 3, 4))
def _bmm(a, b, ca, cb, passes=1):
    return _bmm_passes(a, b, ca, cb, passes)


def _bmm_fwd(a, b, ca, cb, passes):
    return _bmm_passes(a, b, ca, cb, passes), (a, b)


def _bmm_bwd(ca, cb, passes, res, g):
    a, b = res
    if (ca, cb) == (2, 1):
        return _bmm_passes(g, b, 2, 2, passes), _bmm_passes(a, g, 1, 1, passes)
    if (ca, cb) == (2, 2):
        return _bmm_passes(g, b, 2, 1, passes), _bmm_passes(g, a, 1, 1, passes)
    return _bmm_passes(b, g, 2, 2, passes), _bmm_passes(a, g, 2, 1, passes)


_bmm.defvjp(_bmm_fwd, _bmm_bwd)


def _tri_dot(x, transpose):
    C = x.shape[1]
    ri = lax.broadcasted_iota(jnp.int32, (C, C), 0)
    ci = lax.broadcasted_iota(jnp.int32, (C, C), 1)
    tri = jnp.broadcast_to(((ri <= ci) if transpose else (ri >= ci)).astype(MXU), (x.shape[0], C, C))
    if MXU == F32:
        return lax.dot_general(tri, x, (((2,), (1,)), ((0,), (0,))), precision=HI, preferred_element_type=F32)
    p0, p1, p2 = _split(x, 3)
    return _bdot_batched(tri, p0, 2, 1) + (_bdot_batched(tri, p1, 2, 1) + _bdot_batched(tri, p2, 2, 1))


@jax.custom_vjp
def _cumsum_rows(x):
    return _tri_dot(x, False)


def _cumsum_rows_fwd(x):
    return _tri_dot(x, False), None


def _cumsum_rows_bwd(_, g):
    return (_tri_dot(g, True),)


_cumsum_rows.defvjp(_cumsum_rows_fwd, _cumsum_rows_bwd)

P_SCORE = 1
P_SOLVE = 1
P_STATE = 1
SCAN_CHUNKS = (4, 2, 1)


def _neumann(l):
    C = l.shape[1]
    eye = (lax.broadcasted_iota(jnp.int32, (C, C), 0) == lax.broadcasted_iota(jnp.int32, (C, C), 1)).astype(F32)
    x, lp = eye + l, l
    for _ in range(int(math.log2(C)) - 1):
        lp = _bmm(lp, lp, 2, 1, P_SOLVE)
        x = x + _bmm(x, lp, 2, 1, P_SOLVE)
    return x


@jax.custom_vjp
def _unit_lower_inverse(l):
    return _neumann(l)


def _unit_lower_inverse_fwd(l):
    x = _neumann(l)
    return x, x


def _unit_lower_inverse_bwd(x, g):
    return (_bmm(_bmm(x, g, 1, 1, P_SOLVE), x, 2, 2, P_SOLVE),)


_unit_lower_inverse.defvjp(_unit_lower_inverse_fwd, _unit_lower_inverse_bwd)


def _known_inverse(x):
    @jax.custom_vjp
    def f(l):
        return x

    f.defvjp(lambda l: (x, None), lambda _, g: (_bmm(_bmm(x, g, 1, 1, P_SOLVE), x, 2, 2, P_SOLVE),))
    return f


def _chunk(S0, r, lw, k, v, a, b, inverse=None):
    C = CHUNK
    n = r.shape[1] // C
    fold = lambda t: t.reshape(NH * n, C, HD)
    r, lw, k, v, a, b = (fold(t) for t in (r, lw, k, v, a, b))
    ri = lax.broadcasted_iota(jnp.int32, (C, C), 0)
    ci = lax.broadcasted_iota(jnp.int32, (C, C), 1)
    incl = (ri >= ci)
    strict = (ri > ci)
    eye = (ri == ci).astype(F32)
    cum = _cumsum_rows(lw)
    p_in = jnp.exp(cum)
    p_ex = jnp.exp(cum - lw)
    p_inv = jnp.exp(-cum)
    at, rt, bt, kt = a * p_ex, r * p_in, b * p_inv, k * p_inv
    lhs = jnp.concatenate([at, rt], axis=1)
    rhs_ = jnp.concatenate([bt, kt], axis=1)
    sc = _bmm(lhs, rhs_, 2, 2, P_SCORE)
    a_ab = jnp.where(strict, sc[:, :C, :C], 0.0)
    a_ak = jnp.where(strict, sc[:, :C, C:], 0.0)
    incl2 = (lax.broadcasted_iota(jnp.int32, (C, 2 * C), 0) >= lax.broadcasted_iota(jnp.int32, (C, 2 * C), 1) % C)
    a_r = jnp.where(incl2, sc[:, C:, :], 0.0)
    av = _bmm(a_ak, v, 2, 1, P_SCORE)
    x = x_all = (_unit_lower_inverse if inverse is None else _known_inverse(inverse))(a_ab)
    p_last = jnp.exp(cum[:, C - 1:C, :])
    per_chunk = lambda t: t.reshape((NH, n) + t.shape[1:])
    lhs, rhs_, a_r, av, x, v, p_last = (per_chunk(t) for t in (lhs, rhs_, a_r, av, x, v, p_last))
    S, ys = S0, []
    for c in range(n):
        s0 = _bmm(lhs[:, c], S, 2, 2, P_STATE)
        u = _bmm(x[:, c], s0[:, :C] + av[:, c], 2, 1, P_SOLVE)
        uv = jnp.concatenate([u, v[:, c]], axis=1)
        ys.append(s0[:, C:] + _bmm(a_r[:, c], uv, 2, 1, P_SCORE))
        S = (S + _bmm(uv, rhs_[:, c], 1, 1, P_STATE)) * p_last[:, c]
    return jnp.concatenate(ys, axis=1), S, x_all


def _hosting(body, ex, n_in, n_out, n_scratch, n_steps):
    if ex is None:
        return body

    def wrapped(*refs):
        a = n_in
        b = a + ex.n_in
        c = b + n_out
        d = c + ex.n_out
        e = d + n_scratch
        ex_refs = (refs[a:b], refs[c:d], refs[e:])
        grid = n_steps if isinstance(n_steps, tuple) else (n_steps,)
        first = last = True
        for ax_, size in enumerate(grid):
            first = first & (pl.program_id(ax_) == 0)
            last = last & (pl.program_id(ax_) == size - 1)

        @pl.when(first)
        def _():
            ex.start(*ex_refs)

        body(*refs[:a], *refs[b:c], *refs[d:e])

        @pl.when(last)
        def _():
            ex.wait(*ex_refs)

    return wrapped


def _hosted_args(ex):
    if ex is None:
        return [], [], [], [], []
    any_spec = pl.BlockSpec(memory_space=pl.ANY)
    return list(ex.arrays), [any_spec] * ex.n_in, [any_spec] * ex.n_out, list(ex.out_shape), list(ex.scratch)


def _scan_fwd(rw, *, name, hosted=None):
    T = rw.shape[0]
    nc = _pick(T // CHUNK, SCAN_CHUNKS)
    rows = CHUNK * nc
    n = T // rows
    h_in, h_in_specs, h_out_specs, h_out_shape, h_scratch = _hosted_args(hosted)

    def body(r_ref, lw_ref, k_ref, v_ref, a_ref, b_ref, y_ref, ck_ref, inv_ref, s_ref):
        @pl.when(pl.program_id(0) == 0)
        def _():
            s_ref[...] = jnp.zeros_like(s_ref)

        S0 = s_ref[...]
        ck_ref[0] = S0
        y, S1, inv = _chunk(S0, *[_heads(ref[...]) for ref in (r_ref, lw_ref, k_ref, v_ref, a_ref, b_ref)])
        y_ref[...] = _unheads(y)
        inv_ref[0] = inv
        s_ref[...] = S1

    col = lambda j: pl.BlockSpec((rows, RW), lambda i: (i, j))
    return pl.pallas_call(
        _hosting(body, hosted, 6, 3, 1, n), name=name, grid=(n,),
        in_specs=[col(j) for j in range(6)] + h_in_specs,
        out_specs=[pl.BlockSpec((rows, RW), lambda i: (i, 0)),
                   pl.BlockSpec((1, NH, HD, HD), lambda i: (i, 0, 0, 0)),
                   pl.BlockSpec((1, NH * nc, CHUNK, CHUNK), lambda i: (i, 0, 0, 0))] + h_out_specs,
        out_shape=[jax.ShapeDtypeStruct((T, RW), F32), jax.ShapeDtypeStruct((n, NH, HD, HD), F32),
                   jax.ShapeDtypeStruct((n, NH * nc, CHUNK, CHUNK), F32)] + h_out_shape,
        scratch_shapes=[pltpu.VMEM((NH, HD, HD), F32)] + h_scratch,
        compiler_params=_cparams(("arbitrary",)),
    )(rw, rw, rw, rw, rw, rw, *h_in)


def _scan_bwd(rw, ck, inv, dy, *, name, hosted=None):
    T = rw.shape[0]
    nc = _pick(T // CHUNK, SCAN_CHUNKS)
    rows = CHUNK * nc
    n = T // rows

    def body(r_ref, lw_ref, k_ref, v_ref, a_ref, b_ref, ck_ref, inv_ref, dy_ref, o_ref, ds_ref):
        @pl.when(pl.program_id(0) == 0)
        def _():
            ds_ref[...] = jnp.zeros_like(ds_ref)

        prim = [_heads(ref[...]) for ref in (r_ref, lw_ref, k_ref, v_ref, a_ref, b_ref)]
        known = inv_ref[0]
        _, vjp = jax.vjp(lambda S0, *p: _chunk(S0, *p, inverse=known)[:2], ck_ref[0], *prim)
        grads = vjp((_heads(dy_ref[...]), ds_ref[...]))
        ds_ref[...] = grads[0]
        o_ref[...] = jnp.concatenate([_unheads(g) for g in grads[1:]], axis=1).astype(o_ref.dtype)

    h_in, h_in_specs, h_out_specs, h_out_shape, h_scratch = _hosted_args(hosted)
    col = lambda j: pl.BlockSpec((rows, RW), lambda i: (n - 1 - i, j))
    return pl.pallas_call(
        _hosting(body, hosted, 9, 1, 1, n), name=name, grid=(n,),
        in_specs=[col(j) for j in range(6)] + [pl.BlockSpec((1, NH, HD, HD), lambda i: (n - 1 - i, 0, 0, 0)),
                                               pl.BlockSpec((1, NH * nc, CHUNK, CHUNK), lambda i: (n - 1 - i, 0, 0, 0)),
                                               pl.BlockSpec((rows, RW), lambda i: (n - 1 - i, 0))] + h_in_specs,
        out_specs=[pl.BlockSpec((rows, 6 * RW), lambda i: (n - 1 - i, 0))] + h_out_specs,
        out_shape=[jax.ShapeDtypeStruct((T, 6 * RW), MXU)] + h_out_shape,
        scratch_shapes=[pltpu.VMEM((NH, HD, HD), F32)] + h_scratch,
        compiler_params=_cparams(("arbitrary",)),
    )(rw, rw, rw, rw, rw, rw, ck, inv, dy, *h_in)


ATTN_BLOCKS = (4, 2, 1)
ATTN_STACK = 4

def _attn_fwd(qkv, tab, qg, kg, sinks, *, name, hosted=None):
    T = qkv.shape[0]
    nb = _pick(T // BLK, ATTN_BLOCKS)
    n = T // (BLK * nb)
    h_in, h_in_specs, h_out_specs, h_out_shape, h_scratch = _hosted_args(hosted)

    def body(c_ref, p_ref, tc_ref, tp_ref, qg_ref, kg_ref, s_ref, o_ref):
        o_ref[...] = _attn_blocks(c_ref[...], p_ref[...], tc_ref[...], tp_ref[...], qg_ref[...], kg_ref[...],
                                  s_ref[...], pl.program_id(0) == 0).astype(o_ref.dtype)

    cur = lambda w: pl.BlockSpec((nb * BLK, w), lambda i: (i, 0))
    prev = lambda w: pl.BlockSpec((BLK, w), lambda i: (jnp.maximum(i * nb - 1, 0), 0))
    return pl.pallas_call(
        _hosting(body, hosted, 7, 1, 0, n), name=name, grid=(n,),
        in_specs=[cur(QKV_W), prev(QKV_W), cur(3 * RW), prev(3 * RW), _full_spec(qg), _full_spec(kg),
                  _full_spec(sinks)] + h_in_specs,
        out_specs=[cur(RW)] + h_out_specs, out_shape=[jax.ShapeDtypeStruct((T, RW), MXU)] + h_out_shape,
        scratch_shapes=h_scratch,
        compiler_params=_cparams(("arbitrary",)),
    )(qkv, qkv, tab, tab, qg, kg, sinks, *h_in)


def _attn_bwd(qkv, tab, qg, kg, sinks, dy, *, name, hosted=None):
    T = qkv.shape[0]
    nb = _pick(T // BLK, ATTN_BLOCKS)
    n = T // (BLK * nb)
    h_in, h_in_specs, h_out_specs, h_out_shape, h_scratch = _hosted_args(hosted)

    def body(c_ref, p_ref, tc_ref, tp_ref, qg_ref, kg_ref, s_ref, dy_ref, dqkv_ref, dqg_ref, dkg_ref, ds_ref, carry_ref):
        i = pl.program_id(0)

        @pl.when(i == 0)
        def _():
            carry_ref[...] = jnp.zeros_like(carry_ref)
            dqg_ref[...] = jnp.zeros_like(dqg_ref)
            dkg_ref[...] = jnp.zeros_like(dkg_ref)
            ds_ref[...] = jnp.zeros_like(ds_ref)

        tc, tp = tc_ref[...], tp_ref[...]
        f = lambda c, p_, qg_, kg_, sk: _attn_blocks(c, p_, tc, tp, qg_, kg_, sk, i == n - 1)
        _, vjp = jax.vjp(f, c_ref[...], p_ref[...], qg_ref[...], kg_ref[...], s_ref[...])
        dc, dp, dqg, dkg, dsk = vjp(dy_ref[...].astype(F32))
        last = slice((nb - 1) * BLK, nb * BLK)
        dqkv_ref[...] = dc.astype(dqkv_ref.dtype)
        dqkv_ref[last, :] = (dc[last] + carry_ref[...]).astype(dqkv_ref.dtype)
        carry_ref[...] = dp
        dqg_ref[...] += dqg
        dkg_ref[...] += dkg
        ds_ref[...] += dsk

    cur = lambda w: pl.BlockSpec((nb * BLK, w), lambda i: (n - 1 - i, 0))
    prev = lambda w: pl.BlockSpec((BLK, w), lambda i: (jnp.maximum((n - 1 - i) * nb - 1, 0), 0))
    return pl.pallas_call(
        _hosting(body, hosted, 8, 4, 1, n), name=name, grid=(n,),
        in_specs=[cur(QKV_W), prev(QKV_W), cur(3 * RW), prev(3 * RW), _full_spec(qg), _full_spec(kg), _full_spec(sinks),
                  cur(RW)] + h_in_specs,
        out_specs=[cur(QKV_W), _full_spec(qg), _full_spec(kg), _full_spec(sinks)] + h_out_specs,
        out_shape=[jax.ShapeDtypeStruct((T, QKV_W), MXU), jax.ShapeDtypeStruct(qg.shape, F32),
                   jax.ShapeDtypeStruct(kg.shape, F32), jax.ShapeDtypeStruct(sinks.shape, F32)] + h_out_shape,
        scratch_shapes=[pltpu.VMEM((BLK, QKV_W), F32)] + h_scratch,
        compiler_params=_cparams(("arbitrary",)),
    )(qkv, qkv, tab, tab, qg, kg, sinks, dy, *h_in)


def _shift_down(cur, prev8, i):
    rolled = pltpu.roll(cur, 1, 0)
    first_row = jnp.where(i > 0, prev8[7:8, :], 0.0)
    row = lax.broadcasted_iota(jnp.int32, cur.shape, 0)
    return jnp.where(row == 0, first_row, rolled)


def _shift_up(cur, next8, i, n):
    tm = cur.shape[0]
    rolled = pltpu.roll(cur, tm - 1, 0)
    last_row = jnp.where(i < n - 1, next8[0:1, :], 0.0)
    row = lax.broadcasted_iota(jnp.int32, cur.shape, 0)
    return jnp.where(row == tm - 1, last_row, rolled)


def _ada_parts(ada):
    return [ada[:, j * D:(j + 1) * D] for j in range(6)]


def _rope_table(positions):
    half = HD // 8
    inv_freq = 500000.0 ** (-jnp.arange(half, dtype=F32) / half)
    ang = positions.astype(F32)[:, None] * inv_freq
    cos, sin = jnp.cos(ang), jnp.sin(ang)
    T = positions.shape[0]
    pad = HD - 2 * half
    c64 = jnp.concatenate([cos, cos, jnp.ones((T, pad), F32)], axis=1)
    lo64 = jnp.concatenate([-sin, jnp.zeros((T, HD - half), F32)], axis=1)
    hi64 = jnp.concatenate([jnp.zeros((T, half), F32), sin, jnp.zeros((T, pad), F32)], axis=1)
    return jnp.concatenate([jnp.tile(t, (1, NH)) for t in (c64, lo64, hi64)], axis=1)


GATHER_BEHIND = {"f_proj_shift": [('ffn_w3', 512, 768)], "f_proj_gates": [('ffn_w3', 768, 1024)],
                 "f_prep": [('ffn_w2', 352, 704)],
                 "f_scan": [('ffn_w1', 0, 1024), ('w_branch_a', 0, 512), ('w_branch_b', 0, 512)],
                 "f_post": [('ffn_w3', 0, 512)], "f_attn": [('ffn_w2', 0, 352), ('w_out', 0, 256)]}
LATE = ['w_out', 'w_branch_a', 'w_branch_b', 'ffn_w1', 'ffn_w3', 'ffn_w2']
BACK_ATTN = ['w_out', 'w_branch_a', 'w_branch_b', 'ffn_w2']
BACK_SCAN = ['ffn_w1', 'ffn_w3']
BACK_LAST = ['w_in', 'decay_up', 'iclr_up', 'gate_up']


def _full_weight(g, ax):
    return g.reshape(-1, g.shape[2]) if ax == 0 else jnp.concatenate([g[j] for j in range(4)], axis=1)


def _local_step(x, target, ada, tab, w, s, shards=None):
    T = x.shape[0]
    tm = _pick(T, (512, 256, 128))
    tm_wide = _pick(T, (256, 128))
    tm_vjp = _pick(T, (128,))
    row = lambda n, dt=F32: (n, dt, 'row')
    acc = lambda n, r=1: (n, F32, r)

    def f_norm1(i, n, x_, g, ada_):
        sh, sc = ada_[:, 0:D], ada_[:, D:2 * D]
        return (_norm_mod(x_, g, sc, sh),)
    (h1,) = _rowwise(f_norm1, [x], [s['norm1_gain'], ada], [row(D, MXU)], tm=tm, name="f_norm1")

    landed = {}

    def behind(kernel_name):
        if not shards:
            return None
        return _GatherChips([shards[n] if hi - lo == shards[n].shape[0] else shards[n][lo:hi]
                             for n, lo, hi in GATHER_BEHIND[kernel_name]])

    def took(kernel_name, got):
        landed.update(zip(GATHER_BEHIND[kernel_name], got))

    def mm_behind(a_, w_, kernel_name, **kw):
        ex = behind(kernel_name)
        res = _mm_nn(a_, w_, name=kernel_name, hosted=ex, **kw)
        if ex:
            took(kernel_name, res[1:])
            return res[0]
        return res

    proj = mm_behind(h1, w['w_in'][:, :SHIFT_W], "f_proj_shift")
    proj_qkv = _mm_nn(h1, w['w_in'][:, SHIFT_W:SHIFT_W + QKV_W], name="f_proj_qkv")
    proj_g = mm_behind(h1, w['w_in'][:, SHIFT_W + QKV_W:], "f_proj_gates", out_dtype=MXU)
    prep_consts = [s['decay_w0'], s['lora_up'], s['iclr_a0'], s['gate_up'], s['k_k'], s['k_a']]

    def f_prep(i, n, cur, prev8, mu, *params):
        mixed = cur + (_shift_down(cur, prev8, i) - cur) * mu
        return (_prep(mixed, *params),)
    rw, *got = _rowwise(f_prep, [(proj, SHIFT_W)], [s['tshift_mu']] + prep_consts, [row(7 * RW)], tm=tm_wide,
                        name="f_prep", halo=[(proj, SHIFT_W, 'prev')], hosted=behind("f_prep"))
    took("f_prep", got)
    y, ck, inv, *got = _scan_fwd(rw, name="f_scan", hosted=behind("f_scan"))
    took("f_scan", got)
    post_consts = [s['lnx_gain'], s['lnx_bias'], s['r_k']]

    rkvg = [(rw, RW, j) for j in (0, 2, 3, 6)]

    def f_post(i, n, *args):
        return (_post(*args),)
    ya, *got = _rowwise(f_post, [y] + rkvg, post_consts, [row(RW, MXU)], tm=tm_wide, name="f_post",
                        hosted=behind("f_post"))
    took("f_post", got)
    yb, *got = _attn_fwd(proj_qkv, tab, s['q_norm_gain'], s['k_norm_gain'], s['attn_sinks'], name="f_attn",
                         hosted=behind("f_attn"))
    took("f_attn", got)
    w = dict(w)
    if shards:
        ax = dict(SHARDED)
        for n in LATE:
            rows = [landed[key] for key in sorted(k_ for k_ in landed if k_[0] == n)]
            w[n] = _full_weight(rows[0] if len(rows) == 1 else jnp.concatenate(rows, axis=1), ax[n])
    ma = _mm_nn(ya, w['w_branch_a'], name="f_branch_a", out_dtype=MXU)
    mb = _mm_nn(yb, w['w_branch_b'], name="f_branch_b", out_dtype=MXU)

    def f_merge(i, n, pg, ma_, mb_, bias):
        return (_merge(pg.astype(F32), ma_.astype(F32), mb_.astype(F32), bias),)
    (merged,) = _rowwise(f_merge, [proj_g, ma, mb], [s['branch_gate_b']], [row(D, MXU)], tm=tm, name="f_merge")
    mo = _mm_nn(merged, w['w_out'], name="f_out")

    def f_res1(i, n, x_, mo_, g, ada_):
        x1_ = x_ + ada_[:, 2 * D:3 * D] * mo_
        return x1_, _norm_mod(x1_, g, ada_[:, 4 * D:5 * D], ada_[:, 3 * D:4 * D])
    x1, h2 = _rowwise(f_res1, [x, mo], [s['norm2_gain'], ada], [row(D), row(D, MXU)], tm=tm, name="f_res1")
    u, v, act = _ffn_in(h2, w['ffn_w1'], w['ffn_w3'], name="f_ffn_in")
    ff = _mm_nn(act, w['ffn_w2'], name="f_ffn_out")

    def f_loss(i, n, x1_, ff_, tgt, ada_):
        g2 = ada_[:, 5 * D:6 * D]
        err = x1_ + g2 * ff_ - tgt
        dx2 = err * (1.0 / D)
        loss = 0.5 * jnp.sum(jnp.sum(err * err, axis=1, keepdims=True) * (1.0 / D), axis=0, keepdims=True)
        return dx2, (dx2 * g2), jnp.broadcast_to(loss, (1, 128)), jnp.sum(dx2 * ff_, axis=0, keepdims=True)
    dx2, dff, loss, dgate2 = _rowwise(f_loss, [x1, ff, target], [ada], [row(D), row(D, MXU), acc(128), acc(D)],
                                      tm=tm, name="f_loss")

    du, dv = _ffn_act_bwd(dff, w['ffn_w2'], u, v, name="b_ffn_out_dx")
    g_w2 = _mm_tn(act, dff, name="b_ffn_out_dw", out_dtype=MXU)
    dh2 = _mm_nt2(du, w['ffn_w1'], dv, w['ffn_w3'], name="b_ffn_in_dx")
    g_w1 = _mm_tn(h2, du, name="b_ffn_w1_dw", out_dtype=MXU)
    g_w3 = _mm_tn(h2, dv, name="b_ffn_w3_dw", out_dtype=MXU)

    def b_res1(i, n, x1_, dh2_, dx2_, mo_, g, ada_):
        _, vjp = jax.vjp(_norm_mod, x1_, g, ada_[:, 4 * D:5 * D], ada_[:, 3 * D:4 * D])
        dxn, dg, dsc, dsh = vjp(dh2_)
        dx1_ = dxn + dx2_
        g1 = ada_[:, 2 * D:3 * D]
        return dx1_, dx1_ * g1, dg, dsc, dsh, jnp.sum(dx1_ * mo_, axis=0, keepdims=True)
    dx1, dmo, d_gain2, d_scale2, d_shift2, dgate1 = _rowwise(
        b_res1, [x1, dh2, dx2, mo], [s['norm2_gain'], ada], [row(D), row(D, MXU), acc(D), acc(D), acc(D), acc(D)],
        tm=tm, name="b_res1")
    dmerged = _mm_nt(dmo, w['w_out'], name="b_out_dx", out_dtype=MXU)
    g_wout = _mm_tn(merged, dmo, name="b_out_dw", out_dtype=MXU)

    def b_merge(i, n, pg, ma_, mb_, dm, bias):
        _, vjp = jax.vjp(_merge, pg.astype(F32), ma_.astype(F32), mb_.astype(F32), bias)
        dpg, dma_, dmb_, dbias = vjp(dm.astype(F32))
        return dpg, dma_, dmb_, dbias
    dpg, dma, dmb, d_bias = _rowwise(b_merge, [proj_g, ma, mb, dmerged], [s['branch_gate_b']],
                                     [row(GATE_W, MXU), row(D, MXU), row(D, MXU), acc(GATE_W)], tm=tm_wide, name="b_merge")
    dya = _mm_nt(dma, w['w_branch_a'], name="b_branch_a_dx")
    g_wa = _mm_tn(ya, dma, name="b_branch_a_dw", out_dtype=MXU, col_shards=4)
    dyb = _mm_nt(dmb, w['w_branch_b'], name="b_branch_b_dx", out_dtype=F32)
    g_wb = _mm_tn(yb, dmb, name="b_branch_b_dw", out_dtype=MXU, col_shards=4)
    fs = DFF // 4
    gw = dict(w_branch_a=g_wa, w_branch_b=g_wb, w_out=g_wout.reshape(4, D // 4, D),
              ffn_w1=jnp.stack([g_w1[:, j * fs:(j + 1) * fs] for j in range(4)]),
              ffn_w3=jnp.stack([g_w3[:, j * fs:(j + 1) * fs] for j in range(4)]),
              ffn_w2=g_w2.reshape(4, fs, D))
    recv = {}
    dqkv, d_qg, d_kg, d_sinks, *got = _attn_bwd(
        proj_qkv, tab, s['q_norm_gain'], s['k_norm_gain'], s['attn_sinks'], dyb, name="b_attn",
        hosted=shards and _ScatterChips([gw[n] for n in BACK_ATTN]))
    recv.update(zip(BACK_ATTN, got))

    def b_post(i, n, y_, r_, k_, v_, g_, dya_, *params):
        _, vjp = jax.vjp(_post, y_, r_, k_, v_, g_, *params)
        dy_, dr_, dk_, dv_, dg_, *dparams = vjp(dya_)
        return (dy_, jnp.concatenate([dr_, dk_, dv_, dg_], axis=1), *dparams)
    dy, drkvg, d_lnx_gain, d_lnx_bias, d_r_k = _rowwise(
        b_post, [y] + rkvg + [dya], post_consts, [row(RW), row(4 * RW, MXU), acc(RW), acc(RW), acc(RW)], tm=tm_wide,
        name="b_post")
    dscan, *got = _scan_bwd(rw, ck, inv, dy, name="b_scan",
                            hosted=shards and _ScatterChips([gw[n] for n in BACK_SCAN]))
    recv.update(zip(BACK_SCAN, got))

    def b_prep(i, n, cur, drw_, dscan_, prev8, mu, *params):
        shifted = _shift_down(cur, prev8, i)
        mixed = cur + (shifted - cur) * mu
        _, vjp = jax.vjp(_prep, mixed, *params)
        blk = lambda t, j: t[:, j * RW:(j + 1) * RW].astype(F32)
        ct = jnp.concatenate([blk(dscan_, 0) + blk(drw_, 0), blk(dscan_, 1), blk(dscan_, 2) + blk(drw_, 1),
                              blk(dscan_, 3) + blk(drw_, 2), blk(dscan_, 4), blk(dscan_, 5), blk(drw_, 3)], axis=1)
        grads = vjp(ct)
        dmixed = grads[0]
        return (dmixed, jnp.sum(dmixed * (shifted - cur), axis=0, keepdims=True)) + tuple(grads[1:])
    dmixed, d_mu, d_w0, d_lora, d_a0, d_gate_up, d_kk, d_ka = _rowwise(
        b_prep, [(proj, SHIFT_W), drkvg, dscan], [s['tshift_mu']] + prep_consts,
        [row(SHIFT_W), acc(SHIFT_W), acc(RW), acc(2 * RW, 128), acc(RW), acc(RW, 128), acc(RW), acc(RW)],
        tm=tm_vjp, name="b_prep", halo=[(proj, SHIFT_W, 'prev')])

    def b_gather(i, n, dm, dqkv_, dpg_, next8, mu):
        dcur = dm * (1.0 - mu) + _shift_up(dm, next8, i, n) * mu
        return (jnp.concatenate([dcur.astype(MXU), dqkv_, dpg_], axis=1),)
    (dproj,) = _rowwise(b_gather, [dmixed, dqkv, dpg], [s['tshift_mu']], [row(IN_W, MXU)], tm=tm_wide, name="b_gather",
                        halo=[(dmixed, SHIFT_W, 'next')])
    g_win = _mm_tn(h1, dproj, name="b_proj_dw", out_dtype=MXU, col_shards=4)

    def col_blocks(g):
        k, n = g.shape
        return g.reshape(k, 4, n // 4).transpose(1, 0, 2).astype(MXU)
    gw.update(w_in=g_win, decay_up=col_blocks(d_lora[:64, :RW]), iclr_up=col_blocks(d_lora[64:, RW:]),
              gate_up=col_blocks(d_gate_up))
    top, bottom = None, None
    if shards:
        top = _ScatterChips([gw['w_in'][:, :D // 2]] + [gw[n] for n in BACK_LAST[1:]])
        bottom = _ScatterChips([gw['w_in'][:, D // 2:]])
        dh1, *got_top = _mm_nt(dproj, w['w_in'], name="b_proj_dx", hosted=top)
    else:
        dh1 = _mm_nt(dproj, w['w_in'], name="b_proj_dx")

    def b_norm1(i, n, x_, dh1_, dx1_, g, ada_):
        _, vjp = jax.vjp(_norm_mod, x_, g, ada_[:, D:2 * D], ada_[:, 0:D])
        dxn, dg, dsc, dsh = vjp(dh1_)
        return dxn + dx1_, dg, dsc, dsh
    dx, d_gain1, d_scale1, d_shift1, *got_bottom = _rowwise(
        b_norm1, [x, dh1, dx1], [s['norm1_gain'], ada], [row(D), acc(D), acc(D), acc(D)], tm=tm, name="b_norm1",
        hosted=bottom)
    if shards:
        recv.update(zip(BACK_LAST[1:], got_top[1:]))
        recv['w_in'] = jnp.concatenate([got_top[0], got_bottom[0]], axis=1)

    d_ada = jnp.concatenate([d_shift1, d_scale1, dgate1, d_shift2, d_scale2, dgate2], axis=1)
    gs = dict(norm1_gain=d_gain1, norm2_gain=d_gain2, tshift_mu=d_mu, decay_w0=d_w0, iclr_a0=d_a0, k_k=d_kk, k_a=d_ka,
              r_k=d_r_k, lnx_gain=d_lnx_gain, lnx_bias=d_lnx_bias, q_norm_gain=d_qg, k_norm_gain=d_kg,
              attn_sinks=d_sinks, branch_gate_b=d_bias)
    return loss, dx, d_ada, gw, gs, recv


ANY = pl.BlockSpec(memory_space=pl.ANY)


def _place():
    x, y, c = lax.axis_index("x"), lax.axis_index("y"), lax.axis_index("c")
    return x, y, c, [(1 - x, y), (x, 1 - y), (1 - x, 1 - y)]


def _all_gather8(x_shard, *, name):
    m_per, n = x_shard.shape

    def body(x_ref, out_ref, send_sems, recv_sems, local_sem):
        x, y, c, chips = _place()
        me, sibling = (x, y, c), (x, y, 1 - c)

        def rows(px, py, pc):
            return out_ref.at[pl.ds((4 * px + 2 * py + pc) * m_per, m_per), :]

        def copy(k, block, to, src=None):
            return pltpu.make_async_remote_copy(
                src_ref=rows(*block) if src is None else src, dst_ref=rows(*block),
                send_sem=send_sems.at[k], recv_sem=recv_sems.at[k], device_id=to, device_id_type=MESH)

        mine = pltpu.make_async_copy(x_ref, rows(*me), local_sem)
        mine.start()
        first = [copy(0, me, sibling, src=x_ref)]
        first += [copy(1 + j, me, (*chip, c), src=x_ref) for j, chip in enumerate(chips)]
        for cp in first:
            cp.start()
        passed = [copy(4 + j, (*chip, c), sibling) for j, chip in enumerate(chips)]
        for j, chip in enumerate(chips):
            copy(1 + j, (*chip, c), me).wait_recv()
            passed[j].start()
        copy(0, sibling, me).wait_recv()
        for j, chip in enumerate(chips):
            copy(4 + j, (*chip, 1 - c), me).wait_recv()
        for cp in first + passed:
            cp.wait_send()
        mine.wait()

    return pl.pallas_call(
        body, name=name, out_shape=jax.ShapeDtypeStruct((8 * m_per, n), x_shard.dtype),
        in_specs=[pl.BlockSpec(memory_space=pltpu.VMEM)], out_specs=pl.BlockSpec(memory_space=pltpu.VMEM),
        scratch_shapes=[pltpu.SemaphoreType.DMA((7,)), pltpu.SemaphoreType.DMA((7,)), pltpu.SemaphoreType.DMA],
    )(x_shard)


class _GatherChips:
    def __init__(self, shards):
        n = len(shards)
        self.arrays, self.n_in, self.n_out = list(shards), n, n
        self.out_shape = [jax.ShapeDtypeStruct((4,) + s.shape, s.dtype) for s in shards]
        self.scratch = [pltpu.SemaphoreType.DMA((3 * n,)), pltpu.SemaphoreType.DMA((3 * n,)),
                        pltpu.SemaphoreType.DMA((n,))]

    def _copies(self, x_refs, out_refs, sems, receiving):
        send_sems, recv_sems, local_sems = sems
        x, y, c, chips = _place()
        s_me = 2 * x + y
        n = self.n_in

        def copy(a, k, s):
            return pltpu.make_async_remote_copy(
                src_ref=x_refs[a], dst_ref=out_refs[a].at[s], send_sem=send_sems.at[3 * a + k],
                recv_sem=recv_sems.at[3 * a + k], device_id=(*chips[k], c), device_id_type=MESH)

        mine = [pltpu.make_async_copy(x_refs[a], out_refs[a].at[s_me], local_sems.at[a]) for a in range(n)]
        sends = [copy(a, k, s_me) for a in range(n) for k in range(3)]
        if not receiving:
            return mine, sends
        return mine, sends, [copy(a, k, 2 * px + py) for a in range(n) for k, (px, py) in enumerate(chips)]

    def start(self, x_refs, out_refs, sems):
        mine, sends = self._copies(x_refs, out_refs, sems, False)
        for cp in mine + sends:
            cp.start()

    def wait(self, x_refs, out_refs, sems):
        mine, sends, recvs = self._copies(x_refs, out_refs, sems, True)
        for cp in recvs:
            cp.wait_recv()
        for cp in sends:
            cp.wait_send()
        for cp in mine:
            cp.wait()


class _GatherChipsHalved(_GatherChips):
    def __init__(self, shards):
        super().__init__(shards)
        n = self.n_in
        self.scratch = [pltpu.SemaphoreType.DMA((6 * n,)), pltpu.SemaphoreType.DMA((6 * n,)),
                        pltpu.SemaphoreType.DMA((n,))]

    def _copies(self, x_refs, out_refs, sems, receiving):
        send_sems, recv_sems, local_sems = sems
        x, y, c, chips = _place()
        s_me = 2 * x + y
        n = self.n_in

        def half(a, who):
            rows = x_refs[a].shape[0] // 2
            return pl.ds(who * rows, rows)

        def over_chips(a, k, s):
            return pltpu.make_async_remote_copy(
                src_ref=x_refs[a].at[half(a, c)], dst_ref=out_refs[a].at[s, half(a, c)],
                send_sem=send_sems.at[3 * a + k], recv_sem=recv_sems.at[3 * a + k],
                device_id=(*chips[k], c), device_id_type=MESH)

        def to_sibling(a, k, s, who):
            return pltpu.make_async_remote_copy(
                src_ref=out_refs[a].at[s, half(a, who)], dst_ref=out_refs[a].at[s, half(a, who)],
                send_sem=send_sems.at[3 * n + 3 * a + k], recv_sem=recv_sems.at[3 * n + 3 * a + k],
                device_id=(x, y, 1 - c), device_id_type=MESH)

        mine = [pltpu.make_async_copy(x_refs[a], out_refs[a].at[s_me], local_sems.at[a]) for a in range(n)]
        sends = [over_chips(a, k, s_me) for a in range(n) for k in range(3)]
        if not receiving:
            return mine, sends
        pairs = [(a, k, 2 * px + py) for a in range(n) for k, (px, py) in enumerate(chips)]
        landed = [over_chips(a, k, s) for a, k, s in pairs]
        passed_on = [to_sibling(a, k, s, c) for a, k, s in pairs]
        from_sibling = [to_sibling(a, k, s, 1 - c) for a, k, s in pairs]
        return mine, sends, landed, passed_on, from_sibling

    def wait(self, x_refs, out_refs, sems):
        mine, sends, landed, passed_on, from_sibling = self._copies(x_refs, out_refs, sems, True)
        for got, fwd in zip(landed, passed_on, strict=True):
            got.wait_recv()
            fwd.start()
        for cp in from_sibling:
            cp.wait_recv()
        for cp in sends + passed_on:
            cp.wait_send()
        for cp in mine:
            cp.wait()


class _ScatterChips:
    def __init__(self, parts):
        n = len(parts)
        self.arrays, self.n_in, self.n_out = list(parts), n, n
        self.out_shape = [jax.ShapeDtypeStruct((3,) + p.shape[1:], p.dtype) for p in parts]
        self.scratch = [pltpu.SemaphoreType.DMA((3 * n,)), pltpu.SemaphoreType.DMA((3 * n,))]

    def _copies(self, g_refs, out_refs, sems):
        send_sems, recv_sems = sems
        x, y, c, chips = _place()
        return [pltpu.make_async_remote_copy(
            src_ref=g_refs[a].at[2 * px + py], dst_ref=out_refs[a].at[k], send_sem=send_sems.at[3 * a + k],
            recv_sem=recv_sems.at[3 * a + k], device_id=(px, py, c), device_id_type=MESH)
            for a in range(self.n_in) for k, (px, py) in enumerate(chips)]

    def start(self, g_refs, out_refs, sems):
        for cp in self._copies(g_refs, out_refs, sems):
            cp.start()

    def wait(self, g_refs, out_refs, sems):
        sends = self._copies(g_refs, out_refs, sems)
        for cp in sends:
            cp.wait_recv()
        for cp in sends:
            cp.wait_send()


def _exchange_call(ex, *, name):
    def body(*refs):
        parts = (refs[:ex.n_in], refs[ex.n_in:ex.n_in + ex.n_out], refs[ex.n_in + ex.n_out:])
        ex.start(*parts)
        ex.wait(*parts)

    return pl.pallas_call(body, name=name, out_shape=ex.out_shape, in_specs=[ANY] * ex.n_in,
                          out_specs=[ANY] * ex.n_out, scratch_shapes=ex.scratch)(*ex.arrays)


def _swap_sibling(vs, *, name):
    n = len(vs)

    def body(*refs):
        v_refs, out_refs = refs[:n], refs[n:2 * n]
        send_sems, recv_sems = refs[2 * n:]
        x, y, c, _ = _place()
        cps = [pltpu.make_async_remote_copy(src_ref=v_refs[a], dst_ref=out_refs[a], send_sem=send_sems.at[a],
                                            recv_sem=recv_sems.at[a], device_id=(x, y, 1 - c), device_id_type=MESH)
               for a in range(n)]
        for cp in cps:
            cp.start()
        for cp in cps:
            cp.wait()

    return pl.pallas_call(
        body, name=name, out_shape=[jax.ShapeDtypeStruct(v.shape, v.dtype) for v in vs],
        in_specs=[ANY] * n, out_specs=[ANY] * n,
        scratch_shapes=[pltpu.SemaphoreType.DMA((n,)), pltpu.SemaphoreType.DMA((n,))],
    )(*vs)


def _sum_parts(own, others, *, name):
    R, C = own.shape
    tm = _pick(R, (256, 128, 64))

    def body(own_ref, o0_ref, o1_ref, o2_ref, out_ref):
        tot = own_ref[...].astype(F32)
        for ref in (o0_ref, o1_ref, o2_ref):
            tot = tot + ref[...].astype(F32)
        out_ref[...] = tot

    part = lambda k: pl.BlockSpec((None, tm, C), lambda i: (k, i, 0))
    return pl.pallas_call(
        body, name=name, grid=(R // tm,),
        in_specs=[pl.BlockSpec((tm, C), lambda i: (i, 0)), part(0), part(1), part(2)],
        out_specs=pl.BlockSpec((tm, C), lambda i: (i, 0)), out_shape=jax.ShapeDtypeStruct((R, C), F32),
        compiler_params=_cparams(("arbitrary",)),
    )(own, others, others, others)


def _adam_math(w_, m_, v_, g):
    m2 = ADAM_B1 * m_ + (1.0 - ADAM_B1) * g
    v2 = ADAM_B2 * v_ + (1.0 - ADAM_B2) * jnp.square(g)
    m_hat = m2 / (1.0 - ADAM_B1 ** ADAM_STEP)
    v_hat = v2 / (1.0 - ADAM_B2 ** ADAM_STEP)
    delta = -ADAM_LR * (m_hat / (jnp.sqrt(v_hat) + ADAM_EPS) + ADAM_WD * w_)
    return delta, m2, v2


SMALL_SLOTS = 16
SMALL_COLS = 6 * D


def _pack_small(grads, *, name):
    n = len(grads)

    def body(*refs):
        out_ref = refs[n]
        out_ref[...] = jnp.zeros_like(out_ref)
        for i, ref in enumerate(refs[:n]):
            out_ref[i:i + 1, 0:ref.shape[1]] = ref[...]

    return pl.pallas_call(body, name=name, out_shape=jax.ShapeDtypeStruct((SMALL_SLOTS, SMALL_COLS), F32))(*grads)


def _adamw_small(ws, ms, vs, gathered, *, name):
    n = len(ws)

    def body(*refs):
        w_refs, m_refs, v_refs, g_ref = refs[:n], refs[n:2 * n], refs[2 * n:3 * n], refs[3 * n]
        outs = refs[3 * n + 1:]
        for i in range(n):
            nc = w_refs[i].shape[1]
            g = g_ref[i:i + 1, 0:nc]
            for d in range(1, 8):
                g = g + g_ref[d * SMALL_SLOTS + i:d * SMALL_SLOTS + i + 1, 0:nc]
            delta, m2, v2 = _adam_math(w_refs[i][...], m_refs[i][...], v_refs[i][...], g)
            for k, val in enumerate((g, delta, m2, v2)):
                outs[k * n + i][...] = val

    shapes = [jax.ShapeDtypeStruct(w.shape, F32) for w in ws]
    res = pl.pallas_call(body, name=name, out_shape=shapes * 4,
                         compiler_params=pltpu.CompilerParams(vmem_limit_bytes=VMEM_LIMIT))(*ws, *ms, *vs, gathered)
    return [res[k * n:(k + 1) * n] for k in range(4)]


def _adamw(w, m, v, gparts, *, tm, name):
    def fn(i, n, w_, m_, v_, *gs):
        g = gs[0]
        for p in gs[1:]:
            g = g + p
        return (g,) + _adam_math(w_, m_, v_, g)
    nc = w.shape[1]
    return _rowwise(fn, [w, m, v] + list(gparts), [], [(nc, F32, 'row')] * 4, tm=tm, name=name)


WEIGHTS = ['ada_w', 'ada_b', 'norm1_gain', 'norm2_gain', 'w_in', 'tshift_mu', 'decay_w0', 'decay_up', 'iclr_a0',
           'iclr_up', 'gate_up', 'k_k', 'k_a', 'r_k', 'lnx_gain', 'lnx_bias', 'q_norm_gain', 'k_norm_gain', 'attn_sinks',
           'branch_gate_b', 'w_branch_a', 'w_branch_b', 'w_out', 'ffn_w1', 'ffn_w3', 'ffn_w2']
SHARDED = [('w_in', 1), ('decay_up', 1), ('iclr_up', 1), ('gate_up', 1), ('w_branch_a', 1), ('w_branch_b', 1),
           ('w_out', 0), ('ffn_w1', 1), ('ffn_w3', 1), ('ffn_w2', 0)]
SMALL = ['ada_b', 'norm1_gain', 'norm2_gain', 'tshift_mu', 'decay_w0', 'iclr_a0', 'k_k', 'k_a', 'r_k', 'lnx_gain',
         'lnx_bias', 'q_norm_gain', 'k_norm_gain', 'attn_sinks', 'branch_gate_b']


def kernel(x, c, positions, ada_w, ada_b, norm1_gain, norm2_gain, w_in, tshift_mu, decay_w0, decay_up, iclr_a0, iclr_up, gate_up, k_k, k_a, r_k, lnx_gain, lnx_bias, q_norm_gain, k_norm_gain, attn_sinks, branch_gate_b, w_branch_a, w_branch_b, w_out, ffn_w1, ffn_w3, ffn_w2, loss_target, m_ada_w, m_ada_b, m_norm1_gain, m_norm2_gain, m_w_in, m_tshift_mu, m_decay_w0, m_decay_up, m_iclr_a0, m_iclr_up, m_gate_up, m_k_k, m_k_a, m_r_k, m_lnx_gain, m_lnx_bias, m_q_norm_gain, m_k_norm_gain, m_attn_sinks, m_branch_gate_b, m_w_branch_a, m_w_branch_b, m_w_out, m_ffn_w1, m_ffn_w3, m_ffn_w2, v_ada_w, v_ada_b, v_norm1_gain, v_norm2_gain, v_w_in, v_tshift_mu, v_decay_w0, v_decay_up, v_iclr_a0, v_iclr_up, v_gate_up, v_k_k, v_k_a, v_r_k, v_lnx_gain, v_lnx_bias, v_q_norm_gain, v_k_norm_gain, v_attn_sinks, v_branch_gate_b, v_w_branch_a, v_w_branch_b, v_w_out, v_ffn_w1, v_ffn_w3, v_ffn_w2):
    a = dict(locals())
    W = {n: a[n] for n in WEIGHTS}
    M = {n: a['m_' + n] for n in WEIGHTS}
    V = {n: a['v_' + n] for n in WEIGHTS}
    xi, yi, ci = lax.axis_index("x"), lax.axis_index("y"), lax.axis_index("c")
    me = 4 * xi + 2 * yi + ci
    shard = 2 * xi + yi
    mat = lambda t: t.reshape(t.shape[-2], t.shape[-1])
    sharded = [n for n, _ in SHARDED]

    ax = dict(SHARDED)
    late = LATE
    early = [n for n in sharded if n not in late]
    shards = {n: mat(W[n]).astype(MXU) for n in sharded}
    gathered = _exchange_call(_GatherChipsHalved([shards[n] for n in early]), name="gather_weights")
    full = {n: _full_weight(g, ax[n]) for n, g in zip(early, gathered, strict=True)}

    c_all = _all_gather8(jnp.broadcast_to(c, (8, D)), name="gather_c")[0::8]
    pad_rows = lambda t: jnp.concatenate([t, jnp.zeros((BLK - 8, t.shape[1]), t.dtype)])
    c_all = pad_rows(c_all.astype(MXU))
    ada_cols = _mm_nn(c_all, mat(ada_w).astype(MXU), name="f_ada")[:8]
    ada_all = _all_gather8(ada_cols, name="gather_ada").reshape(2, 2, 2, 8, 6 * D // 4)
    ada_mine = lax.dynamic_index_in_dim(ada_all[:, :, 0], me, axis=2, keepdims=False)
    ada = ada_mine.reshape(1, 6 * D) + mat(ada_b)

    zero = jnp.zeros((64, RW), MXU)
    lora = jnp.concatenate([jnp.concatenate([full['decay_up'], zero], axis=1),
                            jnp.concatenate([zero, full['iclr_up']], axis=1)], axis=0)
    s = {n: W[n].reshape(1, -1) for n in SMALL if n != 'ada_b'}
    s['lora_up'] = lora.astype(F32)
    s['gate_up'] = full['gate_up'].astype(F32)
    tab = _rope_table(positions.reshape(-1))
    loss, dx, d_ada, gw, gs, from_chips = _local_step(x[0], loss_target[0], ada, tab, dict(w_in=full['w_in']), s,
                                                      shards={n: shards[n] for n in late})
    loss = lax.psum(loss[0, 0], ("x", "y", "c"))

    gs['ada_b'] = d_ada
    gsmall = _pack_small([gs[n] for n in SMALL], name="pack_small_grads")
    gsmall_all = _all_gather8(gsmall, name="gather_small_grads")
    row = lambda src: [src[n].reshape(1, -1) for n in SMALL]
    sm_out = _adamw_small(row(W), row(M), row(V), gsmall_all, name="adamw_small")
    sm_out = [{n: o.reshape(W[n].shape) for n, o in zip(SMALL, outs_k, strict=True)} for outs_k in sm_out]

    d_ada_all = gsmall_all[0::SMALL_SLOTS]
    d_ada_cols = lax.dynamic_slice_in_dim(d_ada_all, shard * (6 * D // 4), 6 * D // 4, axis=1)
    g_ada_w = _mm_tn(c_all, pad_rows(d_ada_cols.astype(MXU)), name="b_ada")
    ada_out = _adamw(mat(ada_w), mat(m_ada_w), mat(v_ada_w), [g_ada_w], tm=256, name="adamw_ada")

    parts =[_sum_parts(lax.dynamic_index_in_dim(gw[n], shard, axis=0, keepdims=False), from_chips[n], name="sum_" + n)
             for n in sharded]
    others = _swap_sibling(parts, name="swap_grads")
    sh_out = {}
    for n, part, other in zip(sharded, parts, others, strict=True):
        sh_out[n] = _adamw(mat(W[n]), mat(M[n]), mat(V[n]), [part, other], tm=_pick(part.shape[0], (256, 128, 64)),
                           name="adamw_" + n)

    def leaf(k, n):
        if n == 'ada_w':
            return ada_out[k].reshape(W[n].shape)
        if n in sharded:
            return sh_out[n][k].reshape(W[n].shape)
        return sm_out[k][n]
    outs = [leaf(k, n) for k in range(4) for n in WEIGHTS]
    return (loss, dx[None], *outs)
```

```python
import functools
import math

import jax
import jax.numpy as jnp
from jax import lax
from jax.experimental import pallas as pl
from jax.experimental.pallas import tpu as pltpu

F32 = jnp.float32
BF16 = jnp.bfloat16
MXU = BF16
HI = lax.Precision.HIGHEST

D = 1024
HD = 64
NH = 8
RW = NH * HD
SHIFT_W = 3 * RW + 64 + 64 + 128
QKV_W = RW + 2 * 128
GATE_W = 2 * D
IN_W = SHIFT_W + QKV_W + GATE_W
DFF = 2816
BLK = 128
CHUNK = 64
RMS_EPS = 1e-6
GN_EPS = 64e-5
NEG_INF = -1e30
ADAM_LR, ADAM_B1, ADAM_B2, ADAM_EPS, ADAM_WD, ADAM_STEP = 0.001, 0.9, 0.999, 1e-08, 0.01, 10
VMEM_LIMIT = 56 * 1024 * 1024
MESH = pl.DeviceIdType.MESH


def _cparams(sem=None):
    return pltpu.CompilerParams(dimension_semantics=sem, vmem_limit_bytes=VMEM_LIMIT)


def _full_spec(a):
    nd = a.ndim
    return pl.BlockSpec(a.shape, lambda *_: (0,) * nd)


def _rowwise(fn, rows, consts, outs, *, tm, name, halo=(), hosted=None):
    rows = [(a + (0,))[:3] if isinstance(a, tuple) else (a, a.shape[1], 0) for a in rows]
    T = rows[0][0].shape[0]
    assert T % tm == 0 and tm % 8 == 0
    n_tiles = T // tm
    n_in = len(rows) + len(halo) + len(consts)
    in_specs = [pl.BlockSpec((tm, nc), lambda i, j=j: (i, j)) for _, nc, j in rows]
    args = [a for a, _, _ in rows]
    for a, nc, kind in halo:
        if kind == 'prev':
            in_specs.append(pl.BlockSpec((8, nc), lambda i: (jnp.maximum(i * (tm // 8) - 1, 0), 0)))
        else:
            in_specs.append(pl.BlockSpec((8, nc), lambda i: (jnp.minimum((i + 1) * (tm // 8), T // 8 - 1), 0)))
        args.append(a)
    in_specs += [_full_spec(a) for a in consts]
    args += list(consts)
    out_shape, out_specs = [], []
    for ncols, dtype, kind in outs:
        if kind == 'row':
            out_shape.append(jax.ShapeDtypeStruct((T, ncols), dtype))
            out_specs.append(pl.BlockSpec((tm, ncols), lambda i: (i, 0)))
        else:
            out_shape.append(jax.ShapeDtypeStruct((kind, ncols), dtype))
            out_specs.append(pl.BlockSpec((kind, ncols), lambda i: (0, 0)))

    def body(*refs):
        i = pl.program_id(0)
        vals = [r[...] for r in refs[:n_in]]
        res = fn(i, n_tiles, *vals)
        for (ncols, dtype, kind), o_ref, val in zip(outs, refs[n_in:], res, strict=True):
            if kind == 'row':
                o_ref[...] = val.astype(dtype)
            else:
                @pl.when(i == 0)
                def _():
                    o_ref[...] = jnp.zeros_like(o_ref)
                o_ref[...] += val.astype(dtype)

    h_in, h_in_specs, h_out_specs, h_out_shape, h_scratch = _hosted_args(hosted)
    res = pl.pallas_call(
        _hosting(body, hosted, n_in, len(outs), 0, n_tiles), name=name, grid=(n_tiles,),
        in_specs=in_specs + h_in_specs, out_specs=out_specs + h_out_specs, out_shape=out_shape + h_out_shape,
        scratch_shapes=h_scratch, compiler_params=_cparams(("arbitrary",)),
    )(*args, *h_in)
    return res


def _pick(n, cands):
    for c in cands:
        if n % c == 0:
            return c
    return n


MM_ROWS = (1024, 512, 256, 128)
MM_COLS = (1536, 1408, 1024, 896, 768, 512, 256, 128)
MM_WIDE = 3000


def _mm_nn(a, w, *, name, out_dtype=F32, hosted=None):
    T, K = a.shape
    N = w.shape[1]
    tm = _pick(T, MM_ROWS)
    tn = _pick(N, MM_COLS)
    grid = (N // tn, T // tm)
    h_in, h_in_specs, h_out_specs, h_out_shape, h_scratch = _hosted_args(hosted)

    def body(a_ref, w_ref, o_ref):
        o_ref[...] = jnp.dot(a_ref[...], w_ref[...], preferred_element_type=F32).astype(out_dtype)

    res = pl.pallas_call(
        _hosting(body, hosted, 2, 1, 0, grid), name=name, grid=grid,
        in_specs=[pl.BlockSpec((tm, K), lambda j, i: (i, 0)), pl.BlockSpec((K, tn), lambda j, i: (0, j))] + h_in_specs,
        out_specs=[pl.BlockSpec((tm, tn), lambda j, i: (i, j))] + h_out_specs,
        out_shape=[jax.ShapeDtypeStruct((T, N), out_dtype)] + h_out_shape, scratch_shapes=h_scratch,
        compiler_params=_cparams(("arbitrary", "arbitrary")),
    )(a, w, *h_in)
    return res if hosted else res[0]


def _mm_nt(dy, w, *, name, out_dtype=F32, hosted=None):
    T, N = dy.shape
    K = w.shape[0]
    tm = _pick(T, MM_ROWS if N <= MM_WIDE else MM_ROWS[1:])
    tk = _pick(K, MM_COLS[1:])
    grid = (K // tk, T // tm)
    h_in, h_in_specs, h_out_specs, h_out_shape, h_scratch = _hosted_args(hosted)

    def body(dy_ref, w_ref, o_ref):
        o_ref[...] = lax.dot_general(dy_ref[...], w_ref[...], (((1,), (1,)), ((), ())),
                                     preferred_element_type=F32).astype(out_dtype)

    res = pl.pallas_call(
        _hosting(body, hosted, 2, 1, 0, grid), name=name, grid=grid,
        in_specs=[pl.BlockSpec((tm, N), lambda j, i: (i, 0)), pl.BlockSpec((tk, N), lambda j, i: (j, 0))] + h_in_specs,
        out_specs=[pl.BlockSpec((tm, tk), lambda j, i: (i, j))] + h_out_specs,
        out_shape=[jax.ShapeDtypeStruct((T, K), out_dtype)] + h_out_shape, scratch_shapes=h_scratch,
        compiler_params=_cparams(("arbitrary", "arbitrary")),
    )(dy, w, *h_in)
    return res if hosted else res[0]


def _mm_tn(a, dy, *, name, out_dtype=F32, col_shards=None):
    T, K = a.shape
    N = dy.shape[1]
    tm = _pick(T, MM_ROWS)
    tn = N // col_shards if col_shards else _pick(N, MM_COLS[1:])
    n_t = T // tm

    def body(a_ref, dy_ref, o_ref, acc_ref):
        i = pl.program_id(1)

        @pl.when(i == 0)
        def _():
            acc_ref[...] = jnp.zeros_like(acc_ref)

        acc_ref[...] += lax.dot_general(a_ref[...], dy_ref[...], (((0,), (0,)), ((), ())), preferred_element_type=F32)

        @pl.when(i == n_t - 1)
        def _():
            o_ref[...] = acc_ref[...].astype(out_dtype)

    if col_shards:
        out_specs = pl.BlockSpec((None, K, tn), lambda j, i: (j, 0, 0))
        out_shape = jax.ShapeDtypeStruct((col_shards, K, tn), out_dtype)
    else:
        out_specs = pl.BlockSpec((K, tn), lambda j, i: (0, j))
        out_shape = jax.ShapeDtypeStruct((K, N), out_dtype)
    return pl.pallas_call(
        body, name=name, grid=(N // tn, n_t),
        in_specs=[pl.BlockSpec((tm, K), lambda j, i: (i, 0)), pl.BlockSpec((tm, tn), lambda j, i: (i, j))],
        out_specs=out_specs, out_shape=out_shape, scratch_shapes=[pltpu.VMEM((K, tn), F32)],
        compiler_params=_cparams(("arbitrary", "arbitrary")),
    )(a, dy)


def _seg_ones(n):
    r = lax.broadcasted_iota(jnp.int32, (n, n), 0) // HD
    c = lax.broadcasted_iota(jnp.int32, (n, n), 1) // HD
    return (r == c).astype(F32)


def _segsum_raw(x):
    ones = _seg_ones(x.shape[1])
    if MXU == F32:
        return jnp.dot(x, ones, precision=HI, preferred_element_type=F32)
    hi = x.astype(MXU)
    lo = (x - hi.astype(F32)).astype(MXU)
    ones = ones.astype(MXU)
    return jnp.dot(hi, ones, preferred_element_type=F32) + jnp.dot(lo, ones, preferred_element_type=F32)


@jax.custom_vjp
def _segsum(x):
    return _segsum_raw(x)


def _segsum_fwd(x):
    return _segsum_raw(x), None


def _segsum_bwd(_, g):
    return (_segsum_raw(g),)


_segsum.defvjp(_segsum_fwd, _segsum_bwd)


def _mxu(x):
    return x.astype(MXU)


@jax.custom_vjp
def _bdot(a, b):
    return jnp.dot(_mxu(a), _mxu(b), preferred_element_type=F32)


def _bdot_fwd(a, b):
    return _bdot(a, b), (a, b)


def _bdot_bwd(res, g):
    a, b = res
    da = lax.dot_general(_mxu(g), _mxu(b), (((1,), (1,)), ((), ())), preferred_element_type=F32)
    db = lax.dot_general(_mxu(a), _mxu(g), (((0,), (0,)), ((), ())), preferred_element_type=F32)
    return da.astype(a.dtype), db.astype(b.dtype)


_bdot.defvjp(_bdot_fwd, _bdot_bwd)


@jax.custom_vjp
def _bdot_nt(a, b):
    return lax.dot_general(_mxu(a), _mxu(b), (((1,), (1,)), ((), ())), preferred_element_type=F32)


def _bdot_nt_fwd(a, b):
    return _bdot_nt(a, b), (a, b)


def _bdot_nt_bwd(res, g):
    a, b = res
    da = jnp.dot(_mxu(g), _mxu(b), preferred_element_type=F32)
    db = lax.dot_general(_mxu(g), _mxu(a), (((0,), (0,)), ((), ())), preferred_element_type=F32)
    return da.astype(a.dtype), db.astype(b.dtype)


_bdot_nt.defvjp(_bdot_nt_fwd, _bdot_nt_bwd)


def _sigmoid(x):
    return 1.0 / (1.0 + jnp.exp(-x))


def _softplus(x):
    return jnp.maximum(x, 0.0) + jnp.log(1.0 + jnp.exp(jnp.minimum(x, -x)))


def _norm_mod(x, gain, scale, shift):
    inv = lax.rsqrt(jnp.mean(x * x, axis=-1, keepdims=True) + RMS_EPS)
    return (x * inv) * gain * (1.0 + scale) + shift


def _prep(mixed, decay_w0, lora_up, iclr_a0, gate_up, k_k, k_a):
    r = mixed[:, 0:RW]
    k = mixed[:, RW:2 * RW]
    v = mixed[:, 2 * RW:3 * RW]
    z = mixed[:, 3 * RW:3 * RW + 128]
    xg = mixed[:, 3 * RW + 128:]
    lane = lax.broadcasted_iota(jnp.int32, z.shape, 1)
    tz = jnp.where(lane < 64, jnp.tanh(z), z)
    lo = _bdot(tz, lora_up)
    w_log = -_softplus(-(decay_w0 + lo[:, :RW])) - 0.5
    lw = -jnp.exp(w_log)
    a_ic = _sigmoid(iclr_a0 + lo[:, RW:])
    g = _bdot(_sigmoid(xg), gate_up)
    kk = k * k_k
    kk = kk / jnp.maximum(jnp.sqrt(_segsum(kk * kk)), 1e-12)
    k_mod = k * (1.0 + (a_ic - 1.0) * k_a)
    return jnp.concatenate([r, lw, k_mod, v, -kk, kk * a_ic, g], axis=1)


def _post(y, r, k, v, g, lnx_gain, lnx_bias, r_k):
    mu = _segsum(y) * (1.0 / HD)
    yc = y - mu
    var = _segsum(yc * yc) * (1.0 / HD)
    yn = yc * lax.rsqrt(var + GN_EPS) * lnx_gain + lnx_bias
    bonus = _segsum(r * k * r_k) * v
    return (yn + bonus) * g


def _merge(pg, ma, mb, bias):
    gates = _sigmoid(pg + bias)
    return gates[:, :D] * ma + gates[:, D:] * mb


def _swiglu(u, v):
    return u * _sigmoid(u) * v


def _ffn_in(h, w1, w3, *, name):
    T, K = h.shape
    F = w1.shape[1]
    tm = _pick(T, MM_ROWS)
    tn = _pick(F, MM_COLS[1:])

    def body(h_ref, w1_ref, w3_ref, u_ref, v_ref, a_ref):
        u = jnp.dot(h_ref[...], w1_ref[...], preferred_element_type=F32).astype(MXU)
        v = jnp.dot(h_ref[...], w3_ref[...], preferred_element_type=F32).astype(MXU)
        u_ref[...] = u
        v_ref[...] = v
        a_ref[...] = _swiglu(u.astype(F32), v.astype(F32)).astype(MXU)

    wspec = pl.BlockSpec((K, tn), lambda j, i: (0, j))
    ospec = pl.BlockSpec((tm, tn), lambda j, i: (i, j))
    return pl.pallas_call(
        body, name=name, grid=(F // tn, T // tm),
        in_specs=[pl.BlockSpec((tm, K), lambda j, i: (i, 0)), wspec, wspec],
        out_specs=[ospec] * 3, out_shape=[jax.ShapeDtypeStruct((T, F), MXU)] * 3,
        compiler_params=_cparams(("arbitrary", "arbitrary")),
    )(h, w1, w3)


def _ffn_act_bwd(dff, w2, u, v, *, name):
    T, N = dff.shape
    F = w2.shape[0]
    tm = _pick(T, MM_ROWS)
    tk = _pick(F, MM_COLS[1:])

    def body(dy_ref, w_ref, u_ref, v_ref, du_ref, dv_ref):
        dact = lax.dot_general(dy_ref[...], w_ref[...], (((1,), (1,)), ((), ())), preferred_element_type=F32)
        _, vjp = jax.vjp(_swiglu, u_ref[...].astype(F32), v_ref[...].astype(F32))
        du, dv = vjp(dact)
        du_ref[...] = du.astype(MXU)
        dv_ref[...] = dv.astype(MXU)

    tile = pl.BlockSpec((tm, tk), lambda j, i: (i, j))
    return pl.pallas_call(
        body, name=name, grid=(F // tk, T // tm),
        in_specs=[pl.BlockSpec((tm, N), lambda j, i: (i, 0)), pl.BlockSpec((tk, N), lambda j, i: (j, 0)), tile, tile],
        out_specs=[tile, tile], out_shape=[jax.ShapeDtypeStruct((T, F), MXU)] * 2,
        compiler_params=_cparams(("arbitrary", "arbitrary")),
    )(dff, w2, u, v)


def _mm_nt2(dy1, w1, dy2, w2, *, name):
    T, N = dy1.shape
    K = w1.shape[0]
    tm = _pick(T, MM_ROWS[1:])
    tk = _pick(K, MM_COLS[1:])

    def body(d1_ref, w1_ref, d2_ref, w2_ref, o_ref):
        nt = lambda a, b: lax.dot_general(a[...], b[...], (((1,), (1,)), ((), ())), preferred_element_type=F32)
        o_ref[...] = nt(d1_ref, w1_ref) + nt(d2_ref, w2_ref)

    dspec = pl.BlockSpec((tm, N), lambda j, i: (i, 0))
    wspec = pl.BlockSpec((tk, N), lambda j, i: (j, 0))
    return pl.pallas_call(
        body, name=name, grid=(K // tk, T // tm), in_specs=[dspec, wspec, dspec, wspec],
        out_specs=pl.BlockSpec((tm, tk), lambda j, i: (i, j)), out_shape=jax.ShapeDtypeStruct((T, K), F32),
        compiler_params=_cparams(("arbitrary", "arbitrary")),
    )(dy1, w1, dy2, w2)


@functools.partial(jax.custom_vjp, nondiff_argnums=(1,))
def _lane_roll(x, s):
    return pltpu.roll(x, s, 1)


def _lane_roll_fwd(x, s):
    return pltpu.roll(x, s, 1), None


def _lane_roll_bwd(s, _, g):
    n = g.shape[1]
    return (pltpu.roll(g, (n - s) % n, 1),)


_lane_roll.defvjp(_lane_roll_fwd, _lane_roll_bwd)


def _rope(x, cos, sin_lo, sin_hi):
    n = x.shape[1]
    return x * cos + _lane_roll(x, n - 8) * sin_lo + _lane_roll(x, 8) * sin_hi


def _head_rms(x, gain):
    return x * lax.rsqrt(_segsum(x * x) * (1.0 / HD) + RMS_EPS) * gain


def _attn_block(qkv_c, qkv_p, tab_c, tab_p, qg, kg, sinks, first):
    def tabs(tab, n):
        return tab[:, 0:n], tab[:, RW:RW + n], tab[:, 2 * RW:2 * RW + n]

    qg = jnp.concatenate([qg] * NH, axis=1)
    kg = jnp.concatenate([kg] * 2, axis=1)
    q = _rope(_head_rms(qkv_c[:, :RW], qg), *tabs(tab_c, RW))
    k_c = _rope(_head_rms(qkv_c[:, RW:RW + 128], kg), *tabs(tab_c, 128))
    k_p = _rope(_head_rms(qkv_p[:, RW:RW + 128], kg), *tabs(tab_p, 128))
    kband = jnp.concatenate([k_p, k_c], axis=0)
    vband = jnp.concatenate([qkv_p[:, RW + 128:], qkv_c[:, RW + 128:]], axis=0)
    G = ATTN_STACK
    qi = lax.broadcasted_iota(jnp.int32, (G * BLK, 2 * BLK), 0) % BLK
    kj = lax.broadcasted_iota(jnp.int32, (G * BLK, 2 * BLK), 1)
    dist = qi + BLK - kj
    valid = (dist >= 0) & (dist < BLK) & (jnp.logical_not(first) | (kj >= BLK))
    row_g = lax.broadcasted_iota(jnp.int32, (G * BLK, 1), 0) // BLK
    outs = []
    for h0 in range(0, NH, G):
        kvh = h0 // 4
        kb = kband[:, kvh * HD:(kvh + 1) * HD]
        vb = vband[:, kvh * HD:(kvh + 1) * HD]
        qs = jnp.concatenate([q[:, (h0 + g) * HD:(h0 + g + 1) * HD] for g in range(G)], axis=0)
        s = _bdot_nt(qs, kb) * (HD ** -0.5)
        s = jnp.where(valid, s, NEG_INF)
        sink = jnp.zeros((G * BLK, 1), F32)
        for g in range(G):
            sink = jnp.where(row_g == g, sinks[:, h0 + g:h0 + g + 1], sink)
        m = lax.stop_gradient(jnp.maximum(jnp.max(s, axis=-1, keepdims=True), sink))
        e = jnp.exp(s - m)
        p = e * (1.0 / (jnp.sum(e, axis=-1, keepdims=True) + jnp.exp(sink - m)))
        o = _bdot(p, vb)
        outs += [o[g * BLK:(g + 1) * BLK] for g in range(G)]
    return jnp.concatenate(outs, axis=1)


def _attn_blocks(qkv_c, qkv_p, tab_c, tab_p, qg, kg, sinks, first):
    nb = qkv_c.shape[0] // BLK
    G = 4

    def tabs(tab, n):
        return tab[:, 0:n], tab[:, RW:RW + n], tab[:, 2 * RW:2 * RW + n]

    qg = jnp.concatenate([qg] * NH, axis=1)
    kg = jnp.concatenate([kg] * 2, axis=1)
    q = _rope(_head_rms(qkv_c[:, :RW], qg), *tabs(tab_c, RW))
    k_in = jnp.concatenate([qkv_p[:, RW:RW + 128], qkv_c[:, RW:RW + 128]], axis=0)
    k = _rope(_head_rms(k_in, kg), *tabs(jnp.concatenate([tab_p, tab_c], axis=0), 128))
    v = jnp.concatenate([qkv_p[:, RW + 128:], qkv_c[:, RW + 128:]], axis=0)

    pile = lambda xs: jnp.concatenate([x_[None] for x_ in xs], axis=0)

    def bands(t):
        return pile([t[b * BLK:(b + 2) * BLK, kvh * HD:(kvh + 1) * HD] for kvh in range(2) for b in range(nb)])

    qs = pile([jnp.concatenate([q[b * BLK:(b + 1) * BLK, (G * kvh + g) * HD:(G * kvh + g + 1) * HD]
                                for g in range(G)], axis=0) for kvh in range(2) for b in range(nb)])
    s = _bmm(qs, bands(k), 2, 2, 1) * (HD ** -0.5)
    qi = lax.broadcasted_iota(jnp.int32, (G * BLK, 2 * BLK), 0) % BLK
    kj = lax.broadcasted_iota(jnp.int32, (G * BLK, 2 * BLK), 1)
    dist = qi + BLK - kj
    in_band = (dist >= 0) & (dist < BLK)
    pair = lax.broadcasted_iota(jnp.int32, (2 * nb, 1, 1), 0)
    no_prev = (pair % nb == 0) & first
    valid = in_band[None] & (jnp.logical_not(no_prev) | (kj >= BLK)[None])
    s = jnp.where(valid, s, NEG_INF)
    row_g = lax.broadcasted_iota(jnp.int32, (G * BLK, 1), 0) // BLK
    sink = []
    for kvh in range(2):
        col = jnp.zeros((G * BLK, 1), F32)
        for g in range(G):
            col = jnp.where(row_g == g, sinks[:, G * kvh + g:G * kvh + g + 1], col)
        sink += [col] * nb
    sink = pile(sink)
    m = lax.stop_gradient(jnp.maximum(jnp.max(s, axis=-1, keepdims=True), sink))
    e = jnp.exp(s - m)
    p = e * (1.0 / (jnp.sum(e, axis=-1, keepdims=True) + jnp.exp(sink - m)))
    o = _bmm(p, bands(v), 2, 1, 1)
    return jnp.concatenate([jnp.concatenate([o[kvh * nb + b, g * BLK:(g + 1) * BLK] for kvh in range(2)
                                             for g in range(G)], axis=1) for b in range(nb)], axis=0)


def _heads(x):
    return jnp.stack([x[:, h * HD:(h + 1) * HD] for h in range(NH)], axis=0)


def _unheads(x):
    return jnp.concatenate([x[h] for h in range(NH)], axis=1)


def _split(x, n):
    parts, rest = [], x
    for _ in range(n):
        p = rest.astype(MXU)
        parts.append(p)
        rest = rest - p.astype(F32)
    return parts


def _bdot_batched(a, b, ca, cb):
    return lax.dot_general(a, b, (((ca,), (cb,)), ((0,), (0,))), preferred_element_type=F32)


def _bmm_passes(a, b, ca, cb, passes):
    if MXU == F32:
        return lax.dot_general(a, b, (((ca,), (cb,)), ((0,), (0,))), precision=HI, preferred_element_type=F32)
    if passes == 1:
        return _bdot_batched(a.astype(MXU), b.astype(MXU), ca, cb)
    (a0, a1), (b0, b1) = _split(a, 2), _split(b, 2)
    return _bdot_batched(a0, b0, ca, cb) + (_bdot_batched(a0, b1, ca, cb) + _bdot_batched(a1, b0, ca, cb))


@functools.partial(jax.custom_vjp, nondiff_argnums=(2, 3, 4))
def _bmm(a, b, ca, cb, passes=1):
    return _bmm_passes(a, b, ca, cb, passes)


def _bmm_fwd(a, b, ca, cb, passes):
    return _bmm_passes(a, b, ca, cb, passes), (a, b)


def _bmm_bwd(ca, cb, passes, res, g):
    a, b = res
    if (ca, cb) == (2, 1):
        return _bmm_passes(g, b, 2, 2, passes), _bmm_passes(a, g, 1, 1, passes)
    if (ca, cb) == (2, 2):
        return _bmm_passes(g, b, 2, 1, passes), _bmm_passes(g, a, 1, 1, passes)
    return _bmm_passes(b, g, 2, 2, passes), _bmm_passes(a, g, 2, 1, passes)


_bmm.defvjp(_bmm_fwd, _bmm_bwd)


def _tri_dot(x, transpose):
    C = x.shape[1]
    ri = lax.broadcasted_iota(jnp.int32, (C, C), 0)
    ci = lax.broadcasted_iota(jnp.int32, (C, C), 1)
    tri = jnp.broadcast_to(((ri <= ci) if transpose else (ri >= ci)).astype(MXU), (x.shape[0], C, C))
    if MXU == F32:
        return lax.dot_general(tri, x, (((2,), (1,)), ((0,), (0,))), precision=HI, preferred_element_type=F32)
    p0, p1, p2 = _split(x, 3)
    return _bdot_batched(tri, p0, 2, 1) + (_bdot_batched(tri, p1, 2, 1) + _bdot_batched(tri, p2, 2, 1))


@jax.custom_vjp
def _cumsum_rows(x):
    return _tri_dot(x, False)


def _cumsum_rows_fwd(x):
    return _tri_dot(x, False), None


def _cumsum_rows_bwd(_, g):
    return (_tri_dot(g, True),)


_cumsum_rows.defvjp(_cumsum_rows_fwd, _cumsum_rows_bwd)

P_SCORE = 1
P_SOLVE = 1
P_STATE = 1
SCAN_CHUNKS = (4, 2, 1)


def _neumann(l):
    C = l.shape[1]
    eye = (lax.broadcasted_iota(jnp.int32, (C, C), 0) == lax.broadcasted_iota(jnp.int32, (C, C), 1)).astype(F32)
    x, lp = eye + l, l
    for _ in range(int(math.log2(C)) - 1):
        lp = _bmm(lp, lp, 2, 1, P_SOLVE)
        x = x + _bmm(x, lp, 2, 1, P_SOLVE)
    return x


@jax.custom_vjp
def _unit_lower_inverse(l):
    return _neumann(l)


def _unit_lower_inverse_fwd(l):
    x = _neumann(l)
    return x, x


def _unit_lower_inverse_bwd(x, g):
    return (_bmm(_bmm(x, g, 1, 1, P_SOLVE), x, 2, 2, P_SOLVE),)


_unit_lower_inverse.defvjp(_unit_lower_inverse_fwd, _unit_lower_inverse_bwd)


def _known_inverse(x):
    @jax.custom_vjp
    def f(l):
        return x

    f.defvjp(lambda l: (x, None), lambda _, g: (_bmm(_bmm(x, g, 1, 1, P_SOLVE), x, 2, 2, P_SOLVE),))
    return f


def _chunk(S0, r, lw, k, v, a, b, inverse=None):
    C = CHUNK
    n = r.shape[1] // C
    fold = lambda t: t.reshape(NH * n, C, HD)
    r, lw, k, v, a, b = (fold(t) for t in (r, lw, k, v, a, b))
    ri = lax.broadcasted_iota(jnp.int32, (C, C), 0)
    ci = lax.broadcasted_iota(jnp.int32, (C, C), 1)
    incl = (ri >= ci)
    strict = (ri > ci)
    eye = (ri == ci).astype(F32)
    cum = _cumsum_rows(lw)
    p_in = jnp.exp(cum)
    p_ex = jnp.exp(cum - lw)
    p_inv = jnp.exp(-cum)
    at, rt, bt, kt = a * p_ex, r * p_in, b * p_inv, k * p_inv
    lhs = jnp.concatenate([at, rt], axis=1)
    rhs_ = jnp.concatenate([bt, kt], axis=1)
    sc = _bmm(lhs, rhs_, 2, 2, P_SCORE)
    a_ab = jnp.where(strict, sc[:, :C, :C], 0.0)
    a_ak = jnp.where(strict, sc[:, :C, C:], 0.0)
    incl2 = (lax.broadcasted_iota(jnp.int32, (C, 2 * C), 0) >= lax.broadcasted_iota(jnp.int32, (C, 2 * C), 1) % C)
    a_r = jnp.where(incl2, sc[:, C:, :], 0.0)
    av = _bmm(a_ak, v, 2, 1, P_SCORE)
    x = x_all = (_unit_lower_inverse if inverse is None else _known_inverse(inverse))(a_ab)
    p_last = jnp.exp(cum[:, C - 1:C, :])
    per_chunk = lambda t: t.reshape((NH, n) + t.shape[1:])
    lhs, rhs_, a_r, av, x, v, p_last = (per_chunk(t) for t in (lhs, rhs_, a_r, av, x, v, p_last))
    S, ys = S0, []
    for c in range(n):
        s0 = _bmm(lhs[:, c], S, 2, 2, P_STATE)
        u = _bmm(x[:, c], s0[:, :C] + av[:, c], 2, 1, P_SOLVE)
        uv = jnp.concatenate([u, v[:, c]], axis=1)
        ys.append(s0[:, C:] + _bmm(a_r[:, c], uv, 2, 1, P_SCORE))
        S = (S + _bmm(uv, rhs_[:, c], 1, 1, P_STATE)) * p_last[:, c]
    return jnp.concatenate(ys, axis=1), S, x_all


def _hosting(body, ex, n_in, n_out, n_scratch, n_steps):
    if ex is None:
        return body

    def wrapped(*refs):
        a = n_in
        b = a + ex.n_in
        c = b + n_out
        d = c + ex.n_out
        e = d + n_scratch
        ex_refs = (refs[a:b], refs[c:d], refs[e:])
        grid = n_steps if isinstance(n_steps, tuple) else (n_steps,)
        first = last = True
        for ax_, size in enumerate(grid):
            first = first & (pl.program_id(ax_) == 0)
            last = last & (pl.program_id(ax_) == size - 1)

        @pl.when(first)
        def _():
            ex.start(*ex_refs)

        body(*refs[:a], *refs[b:c], *refs[d:e])

        @pl.when(last)
        def _():
            ex.wait(*ex_refs)

    return wrapped


def _hosted_args(ex):
    if ex is None:
        return [], [], [], [], []
    any_spec = pl.BlockSpec(memory_space=pl.ANY)
    return list(ex.arrays), [any_spec] * ex.n_in, [any_spec] * ex.n_out, list(ex.out_shape), list(ex.scratch)


def _scan_fwd(rw, *, name, hosted=None):
    T = rw.shape[0]
    nc = _pick(T // CHUNK, SCAN_CHUNKS)
    rows = CHUNK * nc
    n = T // rows
    h_in, h_in_specs, h_out_specs, h_out_shape, h_scratch = _hosted_args(hosted)

    def body(r_ref, lw_ref, k_ref, v_ref, a_ref, b_ref, y_ref, ck_ref, inv_ref, s_ref):
        @pl.when(pl.program_id(0) == 0)
        def _():
            s_ref[...] = jnp.zeros_like(s_ref)

        S0 = s_ref[...]
        ck_ref[0] = S0
        y, S1, inv = _chunk(S0, *[_heads(ref[...]) for ref in (r_ref, lw_ref, k_ref, v_ref, a_ref, b_ref)])
        y_ref[...] = _unheads(y)
        inv_ref[0] = inv
        s_ref[...] = S1

    col = lambda j: pl.BlockSpec((rows, RW), lambda i: (i, j))
    return pl.pallas_call(
        _hosting(body, hosted, 6, 3, 1, n), name=name, grid=(n,),
        in_specs=[col(j) for j in range(6)] + h_in_specs,
        out_specs=[pl.BlockSpec((rows, RW), lambda i: (i, 0)),
                   pl.BlockSpec((1, NH, HD, HD), lambda i: (i, 0, 0, 0)),
                   pl.BlockSpec((1, NH * nc, CHUNK, CHUNK), lambda i: (i, 0, 0, 0))] + h_out_specs,
        out_shape=[jax.ShapeDtypeStruct((T, RW), F32), jax.ShapeDtypeStruct((n, NH, HD, HD), F32),
                   jax.ShapeDtypeStruct((n, NH * nc, CHUNK, CHUNK), F32)] + h_out_shape,
        scratch_shapes=[pltpu.VMEM((NH, HD, HD), F32)] + h_scratch,
        compiler_params=_cparams(("arbitrary",)),
    )(rw, rw, rw, rw, rw, rw, *h_in)


def _scan_bwd(rw, ck, inv, dy, *, name, hosted=None):
    T = rw.shape[0]
    nc = _pick(T // CHUNK, SCAN_CHUNKS)
    rows = CHUNK * nc
    n = T // rows

    def body(r_ref, lw_ref, k_ref, v_ref, a_ref, b_ref, ck_ref, inv_ref, dy_ref, o_ref, ds_ref):
        @pl.when(pl.program_id(0) == 0)
        def _():
            ds_ref[...] = jnp.zeros_like(ds_ref)

        prim = [_heads(ref[...]) for ref in (r_ref, lw_ref, k_ref, v_ref, a_ref, b_ref)]
        known = inv_ref[0]
        _, vjp = jax.vjp(lambda S0, *p: _chunk(S0, *p, inverse=known)[:2], ck_ref[0], *prim)
        grads = vjp((_heads(dy_ref[...]), ds_ref[...]))
        ds_ref[...] = grads[0]
        o_ref[...] = jnp.concatenate([_unheads(g) for g in grads[1:]], axis=1).astype(o_ref.dtype)

    h_in, h_in_specs, h_out_specs, h_out_shape, h_scratch = _hosted_args(hosted)
    col = lambda j: pl.BlockSpec((rows, RW), lambda i: (n - 1 - i, j))
    return pl.pallas_call(
        _hosting(body, hosted, 9, 1, 1, n), name=name, grid=(n,),
        in_specs=[col(j) for j in range(6)] + [pl.BlockSpec((1, NH, HD, HD), lambda i: (n - 1 - i, 0, 0, 0)),
                                               pl.BlockSpec((1, NH * nc, CHUNK, CHUNK), lambda i: (n - 1 - i, 0, 0, 0)),
                                               pl.BlockSpec((rows, RW), lambda i: (n - 1 - i, 0))] + h_in_specs,
        out_specs=[pl.BlockSpec((rows, 6 * RW), lambda i: (n - 1 - i, 0))] + h_out_specs,
        out_shape=[jax.ShapeDtypeStruct((T, 6 * RW), MXU)] + h_out_shape,
        scratch_shapes=[pltpu.VMEM((NH, HD, HD), F32)] + h_scratch,
        compiler_params=_cparams(("arbitrary",)),
    )(rw, rw, rw, rw, rw, rw, ck, inv, dy, *h_in)


ATTN_BLOCKS = (4, 2, 1)
ATTN_STACK = 4

def _attn_fwd(qkv, tab, qg, kg, sinks, *, name, hosted=None):
    T = qkv.shape[0]
    nb = _pick(T // BLK, ATTN_BLOCKS)
    n = T // (BLK * nb)
    h_in, h_in_specs, h_out_specs, h_out_shape, h_scratch = _hosted_args(hosted)

    def body(c_ref, p_ref, tc_ref, tp_ref, qg_ref, kg_ref, s_ref, o_ref):
        o_ref[...] = _attn_blocks(c_ref[...], p_ref[...], tc_ref[...], tp_ref[...], qg_ref[...], kg_ref[...],
                                  s_ref[...], pl.program_id(0) == 0).astype(o_ref.dtype)

    cur = lambda w: pl.BlockSpec((nb * BLK, w), lambda i: (i, 0))
    prev = lambda w: pl.BlockSpec((BLK, w), lambda i: (jnp.maximum(i * nb - 1, 0), 0))
    return pl.pallas_call(
        _hosting(body, hosted, 7, 1, 0, n), name=name, grid=(n,),
        in_specs=[cur(QKV_W), prev(QKV_W), cur(3 * RW), prev(3 * RW), _full_spec(qg), _full_spec(kg),
                  _full_spec(sinks)] + h_in_specs,
        out_specs=[cur(RW)] + h_out_specs, out_shape=[jax.ShapeDtypeStruct((T, RW), MXU)] + h_out_shape,
        scratch_shapes=h_scratch,
        compiler_params=_cparams(("arbitrary",)),
    )(qkv, qkv, tab, tab, qg, kg, sinks, *h_in)


def _attn_bwd(qkv, tab, qg, kg, sinks, dy, *, name, hosted=None):
    T = qkv.shape[0]
    nb = _pick(T // BLK, ATTN_BLOCKS)
    n = T // (BLK * nb)
    h_in, h_in_specs, h_out_specs, h_out_shape, h_scratch = _hosted_args(hosted)

    def body(c_ref, p_ref, tc_ref, tp_ref, qg_ref, kg_ref, s_ref, dy_ref, dqkv_ref, dqg_ref, dkg_ref, ds_ref, carry_ref):
        i = pl.program_id(0)

        @pl.when(i == 0)
        def _():
            carry_ref[...] = jnp.zeros_like(carry_ref)
            dqg_ref[...] = jnp.zeros_like(dqg_ref)
            dkg_ref[...] = jnp.zeros_like(dkg_ref)
            ds_ref[...] = jnp.zeros_like(ds_ref)

        tc, tp = tc_ref[...], tp_ref[...]
        f = lambda c, p_, qg_, kg_, sk: _attn_blocks(c, p_, tc, tp, qg_, kg_, sk, i == n - 1)
        _, vjp = jax.vjp(f, c_ref[...], p_ref[...], qg_ref[...], kg_ref[...], s_ref[...])
        dc, dp, dqg, dkg, dsk = vjp(dy_ref[...].astype(F32))
        last = slice((nb - 1) * BLK, nb * BLK)
        dqkv_ref[...] = dc.astype(dqkv_ref.dtype)
        dqkv_ref[last, :] = (dc[last] + carry_ref[...]).astype(dqkv_ref.dtype)
        carry_ref[...] = dp
        dqg_ref[...] += dqg
        dkg_ref[...] += dkg
        ds_ref[...] += dsk

    cur = lambda w: pl.BlockSpec((nb * BLK, w), lambda i: (n - 1 - i, 0))
    prev = lambda w: pl.BlockSpec((BLK, w), lambda i: (jnp.maximum((n - 1 - i) * nb - 1, 0), 0))
    return pl.pallas_call(
        _hosting(body, hosted, 8, 4, 1, n), name=name, grid=(n,),
        in_specs=[cur(QKV_W), prev(QKV_W), cur(3 * RW), prev(3 * RW), _full_spec(qg), _full_spec(kg), _full_spec(sinks),
                  cur(RW)] + h_in_specs,
        out_specs=[cur(QKV_W), _full_spec(qg), _full_spec(kg), _full_spec(sinks)] + h_out_specs,
        out_shape=[jax.ShapeDtypeStruct((T, QKV_W), MXU), jax.ShapeDtypeStruct(qg.shape, F32),
                   jax.ShapeDtypeStruct(kg.shape, F32), jax.ShapeDtypeStruct(sinks.shape, F32)] + h_out_shape,
        scratch_shapes=[pltpu.VMEM((BLK, QKV_W), F32)] + h_scratch,
        compiler_params=_cparams(("arbitrary",)),
    )(qkv, qkv, tab, tab, qg, kg, sinks, dy, *h_in)


def _shift_down(cur, prev8, i):
    rolled = pltpu.roll(cur, 1, 0)
    first_row = jnp.where(i > 0, prev8[7:8, :], 0.0)
    row = lax.broadcasted_iota(jnp.int32, cur.shape, 0)
    return jnp.where(row == 0, first_row, rolled)


def _shift_up(cur, next8, i, n):
    tm = cur.shape[0]
    rolled = pltpu.roll(cur, tm - 1, 0)
    last_row = jnp.where(i < n - 1, next8[0:1, :], 0.0)
    row = lax.broadcasted_iota(jnp.int32, cur.shape, 0)
    return jnp.where(row == tm - 1, last_row, rolled)


def _ada_parts(ada):
    return [ada[:, j * D:(j + 1) * D] for j in range(6)]


def _rope_table(positions):
    half = HD // 8
    inv_freq = 500000.0 ** (-jnp.arange(half, dtype=F32) / half)
    ang = positions.astype(F32)[:, None] * inv_freq
    cos, sin = jnp.cos(ang), jnp.sin(ang)
    T = positions.shape[0]
    pad = HD - 2 * half
    c64 = jnp.concatenate([cos, cos, jnp.ones((T, pad), F32)], axis=1)
    lo64 = jnp.concatenate([-sin, jnp.zeros((T, HD - half), F32)], axis=1)
    hi64 = jnp.concatenate([jnp.zeros((T, half), F32), sin, jnp.zeros((T, pad), F32)], axis=1)
    return jnp.concatenate([jnp.tile(t, (1, NH)) for t in (c64, lo64, hi64)], axis=1)


GATHER_BEHIND = {"f_proj_shift": [('ffn_w3', 512, 768)], "f_proj_gates": [('ffn_w3', 768, 1024)],
                 "f_prep": [('ffn_w2', 352, 704)],
                 "f_scan": [('ffn_w1', 0, 1024), ('w_branch_a', 0, 512), ('w_branch_b', 0, 512)],
                 "f_post": [('ffn_w3', 0, 512)], "f_attn": [('ffn_w2', 0, 352), ('w_out', 0, 256)]}
LATE = ['w_out', 'w_branch_a', 'w_branch_b', 'ffn_w1', 'ffn_w3', 'ffn_w2']
BACK_ATTN = ['w_out', 'w_branch_a', 'w_branch_b', 'ffn_w2']
BACK_SCAN = ['ffn_w1', 'ffn_w3']
BACK_LAST = ['w_in', 'decay_up', 'iclr_up', 'gate_up']


def _full_weight(g, ax):
    return g.reshape(-1, g.shape[2]) if ax == 0 else jnp.concatenate([g[j] for j in range(4)], axis=1)


def _local_step(x, target, ada, tab, w, s, shards=None):
    T = x.shape[0]
    tm = _pick(T, (512, 256, 128))
    tm_wide = _pick(T, (256, 128))
    tm_vjp = _pick(T, (128,))
    row = lambda n, dt=F32: (n, dt, 'row')
    acc = lambda n, r=1: (n, F32, r)

    def f_norm1(i, n, x_, g, ada_):
        sh, sc = ada_[:, 0:D], ada_[:, D:2 * D]
        return (_norm_mod(x_, g, sc, sh),)
    (h1,) = _rowwise(f_norm1, [x], [s['norm1_gain'], ada], [row(D, MXU)], tm=tm, name="f_norm1")

    landed = {}

    def behind(kernel_name):
        if not shards:
            return None
        return _GatherChips([shards[n] if hi - lo == shards[n].shape[0] else shards[n][lo:hi]
                             for n, lo, hi in GATHER_BEHIND[kernel_name]])

    def took(kernel_name, got):
        landed.update(zip(GATHER_BEHIND[kernel_name], got))

    def mm_behind(a_, w_, kernel_name, **kw):
        ex = behind(kernel_name)
        res = _mm_nn(a_, w_, name=kernel_name, hosted=ex, **kw)
        if ex:
            took(kernel_name, res[1:])
            return res[0]
        return res

    proj = mm_behind(h1, w['w_in'][:, :SHIFT_W], "f_proj_shift")
    proj_qkv = _mm_nn(h1, w['w_in'][:, SHIFT_W:SHIFT_W + QKV_W], name="f_proj_qkv")
    proj_g = mm_behind(h1, w['w_in'][:, SHIFT_W + QKV_W:], "f_proj_gates", out_dtype=MXU)
    prep_consts = [s['decay_w0'], s['lora_up'], s['iclr_a0'], s['gate_up'], s['k_k'], s['k_a']]

    def f_prep(i, n, cur, prev8, mu, *params):
        mixed = cur + (_shift_down(cur, prev8, i) - cur) * mu
        return (_prep(mixed, *params),)
    rw, *got = _rowwise(f_prep, [(proj, SHIFT_W)], [s['tshift_mu']] + prep_consts, [row(7 * RW)], tm=tm_wide,
                        name="f_prep", halo=[(proj, SHIFT_W, 'prev')], hosted=behind("f_prep"))
    took("f_prep", got)
    y, ck, inv, *got = _scan_fwd(rw, name="f_scan", hosted=behind("f_scan"))
    took("f_scan", got)
    post_consts = [s['lnx_gain'], s['lnx_bias'], s['r_k']]

    rkvg = [(rw, RW, j) for j in (0, 2, 3, 6)]

    def f_post(i, n, *args):
        return (_post(*args),)
    ya, *got = _rowwise(f_post, [y] + rkvg, post_consts, [row(RW, MXU)], tm=tm_wide, name="f_post",
                        hosted=behind("f_post"))
    took("f_post", got)
    yb, *got = _attn_fwd(proj_qkv, tab, s['q_norm_gain'], s['k_norm_gain'], s['attn_sinks'], name="f_attn",
                         hosted=behind("f_attn"))
    took("f_attn", got)
    w = dict(w)
    if shards:
        ax = dict(SHARDED)
        for n in LATE:
            rows = [landed[key] for key in sorted(k_ for k_ in landed if k_[0] == n)]
            w[n] = _full_weight(rows[0] if len(rows) == 1 else jnp.concatenate(rows, axis=1), ax[n])
    ma = _mm_nn(ya, w['w_branch_a'], name="f_branch_a", out_dtype=MXU)
    mb = _mm_nn(yb, w['w_branch_b'], name="f_branch_b", out_dtype=MXU)

    def f_merge(i, n, pg, ma_, mb_, bias):
        return (_merge(pg.astype(F32), ma_.astype(F32), mb_.astype(F32), bias),)
    (merged,) = _rowwise(f_merge, [proj_g, ma, mb], [s['branch_gate_b']], [row(D, MXU)], tm=tm, name="f_merge")
    mo = _mm_nn(merged, w['w_out'], name="f_out")

    def f_res1(i, n, x_, mo_, g, ada_):
        x1_ = x_ + ada_[:, 2 * D:3 * D] * mo_
        return x1_, _norm_mod(x1_, g, ada_[:, 4 * D:5 * D], ada_[:, 3 * D:4 * D])
    x1, h2 = _rowwise(f_res1, [x, mo], [s['norm2_gain'], ada], [row(D), row(D, MXU)], tm=tm, name="f_res1")
    u, v, act = _ffn_in(h2, w['ffn_w1'], w['ffn_w3'], name="f_ffn_in")
    ff = _mm_nn(act, w['ffn_w2'], name="f_ffn_out")

    def f_loss(i, n, x1_, ff_, tgt, ada_):
        g2 = ada_[:, 5 * D:6 * D]
        err = x1_ + g2 * ff_ - tgt
        dx2 = err * (1.0 / D)
        loss = 0.5 * jnp.sum(jnp.sum(err * err, axis=1, keepdims=True) * (1.0 / D), axis=0, keepdims=True)
        return dx2, (dx2 * g2), jnp.broadcast_to(loss, (1, 128)), jnp.sum(dx2 * ff_, axis=0, keepdims=True)
    dx2, dff, loss, dgate2 = _rowwise(f_loss, [x1, ff, target], [ada], [row(D), row(D, MXU), acc(128), acc(D)],
                                      tm=tm, name="f_loss")

    du, dv = _ffn_act_bwd(dff, w['ffn_w2'], u, v, name="b_ffn_out_dx")
    g_w2 = _mm_tn(act, dff, name="b_ffn_out_dw", out_dtype=MXU)
    dh2 = _mm_nt2(du, w['ffn_w1'], dv, w['ffn_w3'], name="b_ffn_in_dx")
    g_w1 = _mm_tn(h2, du, name="b_ffn_w1_dw", out_dtype=MXU)
    g_w3 = _mm_tn(h2, dv, name="b_ffn_w3_dw", out_dtype=MXU)

    def b_res1(i, n, x1_, dh2_, dx2_, mo_, g, ada_):
        _, vjp = jax.vjp(_norm_mod, x1_, g, ada_[:, 4 * D:5 * D], ada_[:, 3 * D:4 * D])
        dxn, dg, dsc, dsh = vjp(dh2_)
        dx1_ = dxn + dx2_
        g1 = ada_[:, 2 * D:3 * D]
        return dx1_, dx1_ * g1, dg, dsc, dsh, jnp.sum(dx1_ * mo_, axis=0, keepdims=True)
    dx1, dmo, d_gain2, d_scale2, d_shift2, dgate1 = _rowwise(
        b_res1, [x1, dh2, dx2, mo], [s['norm2_gain'], ada], [row(D), row(D, MXU), acc(D), acc(D), acc(D), acc(D)],
        tm=tm, name="b_res1")
    dmerged = _mm_nt(dmo, w['w_out'], name="b_out_dx", out_dtype=MXU)
    g_wout = _mm_tn(merged, dmo, name="b_out_dw", out_dtype=MXU)

    def b_merge(i, n, pg, ma_, mb_, dm, bias):
        _, vjp = jax.vjp(_merge, pg.astype(F32), ma_.astype(F32), mb_.astype(F32), bias)
        dpg, dma_, dmb_, dbias = vjp(dm.astype(F32))
        return dpg, dma_, dmb_, dbias
    dpg, dma, dmb, d_bias = _rowwise(b_merge, [proj_g, ma, mb, dmerged], [s['branch_gate_b']],
                                     [row(GATE_W, MXU), row(D, MXU), row(D, MXU), acc(GATE_W)], tm=tm_wide, name="b_merge")
    dya = _mm_nt(dma, w['w_branch_a'], name="b_branch_a_dx")
    g_wa = _mm_tn(ya, dma, name="b_branch_a_dw", out_dtype=MXU, col_shards=4)
    dyb = _mm_nt(dmb, w['w_branch_b'], name="b_branch_b_dx", out_dtype=F32)
    g_wb = _mm_tn(yb, dmb, name="b_branch_b_dw", out_dtype=MXU, col_shards=4)
    fs = DFF // 4
    gw = dict(w_branch_a=g_wa, w_branch_b=g_wb, w_out=g_wout.reshape(4, D // 4, D),
              ffn_w1=jnp.stack([g_w1[:, j * fs:(j + 1) * fs] for j in range(4)]),
              ffn_w3=jnp.stack([g_w3[:, j * fs:(j + 1) * fs] for j in range(4)]),
              ffn_w2=g_w2.reshape(4, fs, D))
    recv = {}
    dqkv, d_qg, d_kg, d_sinks, *got = _attn_bwd(
        proj_qkv, tab, s['q_norm_gain'], s['k_norm_gain'], s['attn_sinks'], dyb, name="b_attn",
        hosted=shards and _ScatterChips([gw[n] for n in BACK_ATTN]))
    recv.update(zip(BACK_ATTN, got))

    def b_post(i, n, y_, r_, k_, v_, g_, dya_, *params):
        _, vjp = jax.vjp(_post, y_, r_, k_, v_, g_, *params)
        dy_, dr_, dk_, dv_, dg_, *dparams = vjp(dya_)
        return (dy_, jnp.concatenate([dr_, dk_, dv_, dg_], axis=1), *dparams)
    dy, drkvg, d_lnx_gain, d_lnx_bias, d_r_k = _rowwise(
        b_post, [y] + rkvg + [dya], post_consts, [row(RW), row(4 * RW, MXU), acc(RW), acc(RW), acc(RW)], tm=tm_wide,
        name="b_post")
    dscan, *got = _scan_bwd(rw, ck, inv, dy, name="b_scan",
                            hosted=shards and _ScatterChips([gw[n] for n in BACK_SCAN]))
    recv.update(zip(BACK_SCAN, got))

    def b_prep(i, n, cur, drw_, dscan_, prev8, mu, *params):
        shifted = _shift_down(cur, prev8, i)
        mixed = cur + (shifted - cur) * mu
        _, vjp = jax.vjp(_prep, mixed, *params)
        blk = lambda t, j: t[:, j * RW:(j + 1) * RW].astype(F32)
        ct = jnp.concatenate([blk(dscan_, 0) + blk(drw_, 0), blk(dscan_, 1), blk(dscan_, 2) + blk(drw_, 1),
                              blk(dscan_, 3) + blk(drw_, 2), blk(dscan_, 4), blk(dscan_, 5), blk(drw_, 3)], axis=1)
        grads = vjp(ct)
        dmixed = grads[0]
        return (dmixed, jnp.sum(dmixed * (shifted - cur), axis=0, keepdims=True)) + tuple(grads[1:])
    dmixed, d_mu, d_w0, d_lora, d_a0, d_gate_up, d_kk, d_ka = _rowwise(
        b_prep, [(proj, SHIFT_W), drkvg, dscan], [s['tshift_mu']] + prep_consts,
        [row(SHIFT_W), acc(SHIFT_W), acc(RW), acc(2 * RW, 128), acc(RW), acc(RW, 128), acc(RW), acc(RW)],
        tm=tm_vjp, name="b_prep", halo=[(proj, SHIFT_W, 'prev')])

    def b_gather(i, n, dm, dqkv_, dpg_, next8, mu):
        dcur = dm * (1.0 - mu) + _shift_up(dm, next8, i, n) * mu
        return (jnp.concatenate([dcur.astype(MXU), dqkv_, dpg_], axis=1),)
    (dproj,) = _rowwise(b_gather, [dmixed, dqkv, dpg], [s['tshift_mu']], [row(IN_W, MXU)], tm=tm_wide, name="b_gather",
                        halo=[(dmixed, SHIFT_W, 'next')])
    g_win = _mm_tn(h1, dproj, name="b_proj_dw", out_dtype=MXU, col_shards=4)

    def col_blocks(g):
        k, n = g.shape
        return g.reshape(k, 4, n // 4).transpose(1, 0, 2).astype(MXU)
    gw.update(w_in=g_win, decay_up=col_blocks(d_lora[:64, :RW]), iclr_up=col_blocks(d_lora[64:, RW:]),
              gate_up=col_blocks(d_gate_up))
    top, bottom = None, None
    if shards:
        top = _ScatterChips([gw['w_in'][:, :D // 2]] + [gw[n] for n in BACK_LAST[1:]])
        bottom = _ScatterChips([gw['w_in'][:, D // 2:3 * D // 4]])
        dh1, *got_top = _mm_nt(dproj, w['w_in'], name="b_proj_dx", hosted=top)
    else:
        dh1 = _mm_nt(dproj, w['w_in'], name="b_proj_dx")

    def b_norm1(i, n, x_, dh1_, dx1_, g, ada_):
        _, vjp = jax.vjp(_norm_mod, x_, g, ada_[:, D:2 * D], ada_[:, 0:D])
        dxn, dg, dsc, dsh = vjp(dh1_)
        return dxn + dx1_, dg, dsc, dsh
    dx, d_gain1, d_scale1, d_shift1, *got_bottom = _rowwise(
        b_norm1, [x, dh1, dx1], [s['norm1_gain'], ada], [row(D), acc(D), acc(D), acc(D)], tm=tm, name="b_norm1",
        hosted=bottom)
    if shards:
        recv.update(zip(BACK_LAST[1:], got_top[1:]))
        recv['w_in'] = [got_top[0], got_bottom[0]]

    d_ada = jnp.concatenate([d_shift1, d_scale1, dgate1, d_shift2, d_scale2, dgate2], axis=1)
    gs = dict(norm1_gain=d_gain1, norm2_gain=d_gain2, tshift_mu=d_mu, decay_w0=d_w0, iclr_a0=d_a0, k_k=d_kk, k_a=d_ka,
              r_k=d_r_k, lnx_gain=d_lnx_gain, lnx_bias=d_lnx_bias, q_norm_gain=d_qg, k_norm_gain=d_kg,
              attn_sinks=d_sinks, branch_gate_b=d_bias)
    return loss, dx, d_ada, gw, gs, recv


ANY = pl.BlockSpec(memory_space=pl.ANY)


def _place():
    x, y, c = lax.axis_index("x"), lax.axis_index("y"), lax.axis_index("c")
    return x, y, c, [(1 - x, y), (x, 1 - y), (1 - x, 1 - y)]


def _all_gather8(x_shard, *, name):
    m_per, n = x_shard.shape

    def body(x_ref, out_ref, send_sems, recv_sems, local_sem):
        x, y, c, chips = _place()
        me, sibling = (x, y, c), (x, y, 1 - c)

        def rows(px, py, pc):
            return out_ref.at[pl.ds((4 * px + 2 * py + pc) * m_per, m_per), :]

        def copy(k, block, to, src=None):
            return pltpu.make_async_remote_copy(
                src_ref=rows(*block) if src is None else src, dst_ref=rows(*block),
                send_sem=send_sems.at[k], recv_sem=recv_sems.at[k], device_id=to, device_id_type=MESH)

        mine = pltpu.make_async_copy(x_ref, rows(*me), local_sem)
        mine.start()
        first = [copy(0, me, sibling, src=x_ref)]
        first += [copy(1 + j, me, (*chip, c), src=x_ref) for j, chip in enumerate(chips)]
        for cp in first:
            cp.start()
        passed = [copy(4 + j, (*chip, c), sibling) for j, chip in enumerate(chips)]
        for j, chip in enumerate(chips):
            copy(1 + j, (*chip, c), me).wait_recv()
            passed[j].start()
        copy(0, sibling, me).wait_recv()
        for j, chip in enumerate(chips):
            copy(4 + j, (*chip, 1 - c), me).wait_recv()
        for cp in first + passed:
            cp.wait_send()
        mine.wait()

    return pl.pallas_call(
        body, name=name, out_shape=jax.ShapeDtypeStruct((8 * m_per, n), x_shard.dtype),
        in_specs=[pl.BlockSpec(memory_space=pltpu.VMEM)], out_specs=pl.BlockSpec(memory_space=pltpu.VMEM),
        scratch_shapes=[pltpu.SemaphoreType.DMA((7,)), pltpu.SemaphoreType.DMA((7,)), pltpu.SemaphoreType.DMA],
    )(x_shard)


class _GatherChips:
    def __init__(self, shards):
        n = len(shards)
        self.arrays, self.n_in, self.n_out = list(shards), n, n
        self.out_shape = [jax.ShapeDtypeStruct((4,) + s.shape, s.dtype) for s in shards]
        self.scratch = [pltpu.SemaphoreType.DMA((3 * n,)), pltpu.SemaphoreType.DMA((3 * n,)),
                        pltpu.SemaphoreType.DMA((n,))]

    def _copies(self, x_refs, out_refs, sems, receiving):
        send_sems, recv_sems, local_sems = sems
        x, y, c, chips = _place()
        s_me = 2 * x + y
        n = self.n_in

        def copy(a, k, s):
            return pltpu.make_async_remote_copy(
                src_ref=x_refs[a], dst_ref=out_refs[a].at[s], send_sem=send_sems.at[3 * a + k],
                recv_sem=recv_sems.at[3 * a + k], device_id=(*chips[k], c), device_id_type=MESH)

        mine = [pltpu.make_async_copy(x_refs[a], out_refs[a].at[s_me], local_sems.at[a]) for a in range(n)]
        sends = [copy(a, k, s_me) for a in range(n) for k in range(3)]
        if not receiving:
            return mine, sends
        return mine, sends, [copy(a, k, 2 * px + py) for a in range(n) for k, (px, py) in enumerate(chips)]

    def start(self, x_refs, out_refs, sems):
        mine, sends = self._copies(x_refs, out_refs, sems, False)
        for cp in mine + sends:
            cp.start()

    def wait(self, x_refs, out_refs, sems):
        mine, sends, recvs = self._copies(x_refs, out_refs, sems, True)
        for cp in recvs:
            cp.wait_recv()
        for cp in sends:
            cp.wait_send()
        for cp in mine:
            cp.wait()


class _GatherChipsHalved(_GatherChips):
    def __init__(self, shards):
        super().__init__(shards)
        n = self.n_in
        self.scratch = [pltpu.SemaphoreType.DMA((6 * n,)), pltpu.SemaphoreType.DMA((6 * n,)),
                        pltpu.SemaphoreType.DMA((n,))]

    def _copies(self, x_refs, out_refs, sems, receiving):
        send_sems, recv_sems, local_sems = sems
        x, y, c, chips = _place()
        s_me = 2 * x + y
        n = self.n_in

        def half(a, who):
            rows = x_refs[a].shape[0] // 2
            return pl.ds(who * rows, rows)

        def over_chips(a, k, s):
            return pltpu.make_async_remote_copy(
                src_ref=x_refs[a].at[half(a, c)], dst_ref=out_refs[a].at[s, half(a, c)],
                send_sem=send_sems.at[3 * a + k], recv_sem=recv_sems.at[3 * a + k],
                device_id=(*chips[k], c), device_id_type=MESH)

        def to_sibling(a, k, s, who):
            return pltpu.make_async_remote_copy(
                src_ref=out_refs[a].at[s, half(a, who)], dst_ref=out_refs[a].at[s, half(a, who)],
                send_sem=send_sems.at[3 * n + 3 * a + k], recv_sem=recv_sems.at[3 * n + 3 * a + k],
                device_id=(x, y, 1 - c), device_id_type=MESH)

        mine = [pltpu.make_async_copy(x_refs[a], out_refs[a].at[s_me], local_sems.at[a]) for a in range(n)]
        sends = [over_chips(a, k, s_me) for a in range(n) for k in range(3)]
        if not receiving:
            return mine, sends
        pairs = [(a, k, 2 * px + py) for a in range(n) for k, (px, py) in enumerate(chips)]
        landed = [over_chips(a, k, s) for a, k, s in pairs]
        passed_on = [to_sibling(a, k, s, c) for a, k, s in pairs]
        from_sibling = [to_sibling(a, k, s, 1 - c) for a, k, s in pairs]
        return mine, sends, landed, passed_on, from_sibling

    def wait(self, x_refs, out_refs, sems):
        mine, sends, landed, passed_on, from_sibling = self._copies(x_refs, out_refs, sems, True)
        for got, fwd in zip(landed, passed_on, strict=True):
            got.wait_recv()
            fwd.start()
        for cp in from_sibling:
            cp.wait_recv()
        for cp in sends + passed_on:
            cp.wait_send()
        for cp in mine:
            cp.wait()


class _ScatterChips:
    def __init__(self, parts):
        n = len(parts)
        self.arrays, self.n_in, self.n_out = list(parts), n, n
        self.out_shape = [jax.ShapeDtypeStruct((3,) + p.shape[1:], p.dtype) for p in parts]
        self.scratch = [pltpu.SemaphoreType.DMA((3 * n,)), pltpu.SemaphoreType.DMA((3 * n,))]

    def _copies(self, g_refs, out_refs, sems):
        send_sems, recv_sems = sems
        x, y, c, chips = _place()
        return [pltpu.make_async_remote_copy(
            src_ref=g_refs[a].at[2 * px + py], dst_ref=out_refs[a].at[k], send_sem=send_sems.at[3 * a + k],
            recv_sem=recv_sems.at[3 * a + k], device_id=(px, py, c), device_id_type=MESH)
            for a in range(self.n_in) for k, (px, py) in enumerate(chips)]

    def start(self, g_refs, out_refs, sems):
        for cp in self._copies(g_refs, out_refs, sems):
            cp.start()

    def wait(self, g_refs, out_refs, sems):
        sends = self._copies(g_refs, out_refs, sems)
        for cp in sends:
            cp.wait_recv()
        for cp in sends:
            cp.wait_send()


def _exchange_call(ex, *, name):
    def body(*refs):
        parts = (refs[:ex.n_in], refs[ex.n_in:ex.n_in + ex.n_out], refs[ex.n_in + ex.n_out:])
        ex.start(*parts)
        ex.wait(*parts)

    return pl.pallas_call(body, name=name, out_shape=ex.out_shape, in_specs=[ANY] * ex.n_in,
                          out_specs=[ANY] * ex.n_out, scratch_shapes=ex.scratch)(*ex.arrays)


class _SwapSibling:
    def __init__(self, vs):
        n = len(vs)
        self.arrays, self.n_in, self.n_out = list(vs), n, n
        self.out_shape = [jax.ShapeDtypeStruct(v.shape, v.dtype) for v in vs]
        self.scratch = [pltpu.SemaphoreType.DMA((n,)), pltpu.SemaphoreType.DMA((n,))]

    def _copies(self, v_refs, out_refs, sems):
        send_sems, recv_sems = sems
        x, y, c, _ = _place()
        return [pltpu.make_async_remote_copy(src_ref=v_refs[a], dst_ref=out_refs[a], send_sem=send_sems.at[a],
                                             recv_sem=recv_sems.at[a], device_id=(x, y, 1 - c), device_id_type=MESH)
                for a in range(self.n_in)]

    def start(self, v_refs, out_refs, sems):
        for cp in self._copies(v_refs, out_refs, sems):
            cp.start()

    def wait(self, v_refs, out_refs, sems):
        for cp in self._copies(v_refs, out_refs, sems):
            cp.wait()


class _Both:
    def __init__(self, first, second):
        self.parts = (first, second)
        self.arrays = first.arrays + second.arrays
        self.n_in, self.n_out = first.n_in + second.n_in, first.n_out + second.n_out
        self.out_shape = first.out_shape + second.out_shape
        self.scratch = first.scratch + second.scratch

    def _split(self, in_refs, out_refs, sems):
        a, b = self.parts
        return ((a, in_refs[:a.n_in], out_refs[:a.n_out], sems[:len(a.scratch)]),
                (b, in_refs[a.n_in:], out_refs[a.n_out:], sems[len(a.scratch):]))

    def start(self, in_refs, out_refs, sems):
        for ex, *refs in self._split(in_refs, out_refs, sems):
            ex.start(*refs)

    def wait(self, in_refs, out_refs, sems):
        for ex, *refs in self._split(in_refs, out_refs, sems):
            ex.wait(*refs)


def _sum_parts(own, others, *, name):
    R, C = own.shape
    tm = _pick(R, (256, 128, 64))

    def body(own_ref, o0_ref, o1_ref, o2_ref, out_ref):
        tot = own_ref[...].astype(F32)
        for ref in (o0_ref, o1_ref, o2_ref):
            tot = tot + ref[...].astype(F32)
        out_ref[...] = tot

    part = lambda k: pl.BlockSpec((None, tm, C), lambda i: (k, i, 0))
    return pl.pallas_call(
        body, name=name, grid=(R // tm,),
        in_specs=[pl.BlockSpec((tm, C), lambda i: (i, 0)), part(0), part(1), part(2)],
        out_specs=pl.BlockSpec((tm, C), lambda i: (i, 0)), out_shape=jax.ShapeDtypeStruct((R, C), F32),
        compiler_params=_cparams(("arbitrary",)),
    )(own, others, others, others)


def _adam_math(w_, m_, v_, g):
    m2 = ADAM_B1 * m_ + (1.0 - ADAM_B1) * g
    v2 = ADAM_B2 * v_ + (1.0 - ADAM_B2) * jnp.square(g)
    m_hat = m2 / (1.0 - ADAM_B1 ** ADAM_STEP)
    v_hat = v2 / (1.0 - ADAM_B2 ** ADAM_STEP)
    delta = -ADAM_LR * (m_hat / (jnp.sqrt(v_hat) + ADAM_EPS) + ADAM_WD * w_)
    return delta, m2, v2


SMALL_SLOTS = 16
SMALL_COLS = 6 * D


def _pack_small(grads, *, name):
    n = len(grads)

    def body(*refs):
        out_ref = refs[n]
        out_ref[...] = jnp.zeros_like(out_ref)
        for i, ref in enumerate(refs[:n]):
            out_ref[i:i + 1, 0:ref.shape[1]] = ref[...]

    return pl.pallas_call(body, name=name, out_shape=jax.ShapeDtypeStruct((SMALL_SLOTS, SMALL_COLS), F32))(*grads)


def _adamw_small(ws, ms, vs, gathered, *, name):
    n = len(ws)

    def body(*refs):
        w_refs, m_refs, v_refs, g_ref = refs[:n], refs[n:2 * n], refs[2 * n:3 * n], refs[3 * n]
        outs = refs[3 * n + 1:]
        for i in range(n):
            nc = w_refs[i].shape[1]
            g = g_ref[i:i + 1, 0:nc]
            for d in range(1, 8):
                g = g + g_ref[d * SMALL_SLOTS + i:d * SMALL_SLOTS + i + 1, 0:nc]
            delta, m2, v2 = _adam_math(w_refs[i][...], m_refs[i][...], v_refs[i][...], g)
            for k, val in enumerate((g, delta, m2, v2)):
                outs[k * n + i][...] = val

    shapes = [jax.ShapeDtypeStruct(w.shape, F32) for w in ws]
    res = pl.pallas_call(body, name=name, out_shape=shapes * 4,
                         compiler_params=pltpu.CompilerParams(vmem_limit_bytes=VMEM_LIMIT))(*ws, *ms, *vs, gathered)
    return [res[k * n:(k + 1) * n] for k in range(4)]


def _adamw(w, m, v, gparts, *, tm, name, hosted=None):
    def fn(i, n, w_, m_, v_, *gs):
        g = gs[0]
        for p in gs[1:]:
            g = g + p
        return (g,) + _adam_math(w_, m_, v_, g)
    nc = w.shape[1]
    return _rowwise(fn, [w, m, v] + list(gparts), [], [(nc, F32, 'row')] * 4, tm=tm, name=name, hosted=hosted)


WEIGHTS = ['ada_w', 'ada_b', 'norm1_gain', 'norm2_gain', 'w_in', 'tshift_mu', 'decay_w0', 'decay_up', 'iclr_a0',
           'iclr_up', 'gate_up', 'k_k', 'k_a', 'r_k', 'lnx_gain', 'lnx_bias', 'q_norm_gain', 'k_norm_gain', 'attn_sinks',
           'branch_gate_b', 'w_branch_a', 'w_branch_b', 'w_out', 'ffn_w1', 'ffn_w3', 'ffn_w2']
SHARDED = [('w_in', 1), ('decay_up', 1), ('iclr_up', 1), ('gate_up', 1), ('w_branch_a', 1), ('w_branch_b', 1),
           ('w_out', 0), ('ffn_w1', 1), ('ffn_w3', 1), ('ffn_w2', 0)]
SMALL = ['ada_b', 'norm1_gain', 'norm2_gain', 'tshift_mu', 'decay_w0', 'iclr_a0', 'k_k', 'k_a', 'r_k', 'lnx_gain',
         'lnx_bias', 'q_norm_gain', 'k_norm_gain', 'attn_sinks', 'branch_gate_b']


def kernel(x, c, positions, ada_w, ada_b, norm1_gain, norm2_gain, w_in, tshift_mu, decay_w0, decay_up, iclr_a0, iclr_up, gate_up, k_k, k_a, r_k, lnx_gain, lnx_bias, q_norm_gain, k_norm_gain, attn_sinks, branch_gate_b, w_branch_a, w_branch_b, w_out, ffn_w1, ffn_w3, ffn_w2, loss_target, m_ada_w, m_ada_b, m_norm1_gain, m_norm2_gain, m_w_in, m_tshift_mu, m_decay_w0, m_decay_up, m_iclr_a0, m_iclr_up, m_gate_up, m_k_k, m_k_a, m_r_k, m_lnx_gain, m_lnx_bias, m_q_norm_gain, m_k_norm_gain, m_attn_sinks, m_branch_gate_b, m_w_branch_a, m_w_branch_b, m_w_out, m_ffn_w1, m_ffn_w3, m_ffn_w2, v_ada_w, v_ada_b, v_norm1_gain, v_norm2_gain, v_w_in, v_tshift_mu, v_decay_w0, v_decay_up, v_iclr_a0, v_iclr_up, v_gate_up, v_k_k, v_k_a, v_r_k, v_lnx_gain, v_lnx_bias, v_q_norm_gain, v_k_norm_gain, v_attn_sinks, v_branch_gate_b, v_w_branch_a, v_w_branch_b, v_w_out, v_ffn_w1, v_ffn_w3, v_ffn_w2):
    a = dict(locals())
    W = {n: a[n] for n in WEIGHTS}
    M = {n: a['m_' + n] for n in WEIGHTS}
    V = {n: a['v_' + n] for n in WEIGHTS}
    xi, yi, ci = lax.axis_index("x"), lax.axis_index("y"), lax.axis_index("c")
    me = 4 * xi + 2 * yi + ci
    shard = 2 * xi + yi
    mat = lambda t: t.reshape(t.shape[-2], t.shape[-1])
    sharded = [n for n, _ in SHARDED]

    ax = dict(SHARDED)
    late = LATE
    early = [n for n in sharded if n not in late]
    shards = {n: mat(W[n]).astype(MXU) for n in sharded}
    gathered = _exchange_call(_GatherChipsHalved([shards[n] for n in early]), name="gather_weights")
    full = {n: _full_weight(g, ax[n]) for n, g in zip(early, gathered, strict=True)}

    c_all = _all_gather8(jnp.broadcast_to(c, (8, D)), name="gather_c")[0::8]
    pad_rows = lambda t: jnp.concatenate([t, jnp.zeros((BLK - 8, t.shape[1]), t.dtype)])
    c_all = pad_rows(c_all.astype(MXU))
    ada_cols = _mm_nn(c_all, mat(ada_w).astype(MXU), name="f_ada")[:8]
    ada_all = _all_gather8(ada_cols, name="gather_ada").reshape(2, 2, 2, 8, 6 * D // 4)
    ada_mine = lax.dynamic_index_in_dim(ada_all[:, :, 0], me, axis=2, keepdims=False)
    ada = ada_mine.reshape(1, 6 * D) + mat(ada_b)

    zero = jnp.zeros((64, RW), MXU)
    lora = jnp.concatenate([jnp.concatenate([full['decay_up'], zero], axis=1),
                            jnp.concatenate([zero, full['iclr_up']], axis=1)], axis=0)
    s = {n: W[n].reshape(1, -1) for n in SMALL if n != 'ada_b'}
    s['lora_up'] = lora.astype(F32)
    s['gate_up'] = full['gate_up'].astype(F32)
    tab = _rope_table(positions.reshape(-1))
    loss, dx, d_ada, gw, gs, from_chips = _local_step(x[0], loss_target[0], ada, tab, dict(w_in=full['w_in']), s,
                                                      shards={n: shards[n] for n in late})
    loss = lax.psum(loss[0, 0], ("x", "y", "c"))

    gs['ada_b'] = d_ada
    gsmall = _pack_small([gs[n] for n in SMALL], name="pack_small_grads")
    gsmall_all = _all_gather8(gsmall, name="gather_small_grads")
    row = lambda src: [src[n].reshape(1, -1) for n in SMALL]
    sm_out = _adamw_small(row(W), row(M), row(V), gsmall_all, name="adamw_small")
    sm_out = [{n: o.reshape(W[n].shape) for n, o in zip(SMALL, outs_k, strict=True)} for outs_k in sm_out]

    d_ada_all = gsmall_all[0::SMALL_SLOTS]
    d_ada_cols = lax.dynamic_slice_in_dim(d_ada_all, shard * (6 * D // 4), 6 * D // 4, axis=1)
    g_ada_w = _mm_tn(c_all, pad_rows(d_ada_cols.astype(MXU)), name="b_ada")

    rest = [n for n in sharded if n != 'w_in']
    parts = {n: _sum_parts(lax.dynamic_index_in_dim(gw[n], shard, axis=0, keepdims=False), from_chips[n],
                           name="sum_" + n) for n in rest}
    tail = _Both(_ScatterChips([gw['w_in'][:, 3 * D // 4:]]), _SwapSibling([parts[n] for n in rest]))
    res = _adamw(mat(ada_w), mat(m_ada_w), mat(v_ada_w), [g_ada_w], tm=256, name="adamw_ada", hosted=tail)
    ada_out, last_quarter, others = res[:4], res[4], dict(zip(rest, res[5:], strict=True))
    parts['w_in'] = _sum_parts(lax.dynamic_index_in_dim(gw['w_in'], shard, axis=0, keepdims=False),
                               jnp.concatenate(from_chips['w_in'] + [last_quarter], axis=1), name="sum_w_in")
    others['w_in'] = _exchange_call(_SwapSibling([parts['w_in']]), name="swap_w_in")[0]
    sh_out = {}
    for n in sharded:
        part, other = parts[n], others[n]
        sh_out[n] = _adamw(mat(W[n]), mat(M[n]), mat(V[n]), [part, other], tm=_pick(part.shape[0], (256, 128, 64)),
                           name="adamw_" + n)

    def leaf(k, n):
        if n == 'ada_w':
            return ada_out[k].reshape(W[n].shape)
        if n in sharded:
            return sh_out[n][k].reshape(W[n].shape)
        return sm_out[k][n]
    outs = [leaf(k, n) for k in range(4) for n in WEIGHTS]
    return (loss, dx[None], *outs)
```

```python
import functools
import math

import jax
import jax.numpy as jnp
from jax import lax
from jax.experimental import pallas as pl
from jax.experimental.pallas import tpu as pltpu

F32 = jnp.float32
BF16 = jnp.bfloat16
MXU = BF16
HI = lax.Precision.HIGHEST

D = 1024
HD = 64
NH = 8
RW = NH * HD
SHIFT_W = 3 * RW + 64 + 64 + 128
QKV_W = RW + 2 * 128
GATE_W = 2 * D
IN_W = SHIFT_W + QKV_W + GATE_W
DFF = 2816
BLK = 128
CHUNK = 64
RMS_EPS = 1e-6
GN_EPS = 64e-5
NEG_INF = -1e30
ADAM_LR, ADAM_B1, ADAM_B2, ADAM_EPS, ADAM_WD, ADAM_STEP = 0.001, 0.9, 0.999, 1e-08, 0.01, 10
VMEM_LIMIT = 56 * 1024 * 1024
MESH = pl.DeviceIdType.MESH


def _cparams(sem=None):
    return pltpu.CompilerParams(dimension_semantics=sem, vmem_limit_bytes=VMEM_LIMIT)


def _full_spec(a):
    nd = a.ndim
    return pl.BlockSpec(a.shape, lambda *_: (0,) * nd)


def _rowwise(fn, rows, consts, outs, *, tm, name, halo=(), hosted=None):
    rows = [(a + (0,))[:3] if isinstance(a, tuple) else (a, a.shape[1], 0) for a in rows]
    T = rows[0][0].shape[0]
    assert T % tm == 0 and tm % 8 == 0
    n_tiles = T // tm
    n_in = len(rows) + len(halo) + len(consts)
    in_specs = [pl.BlockSpec((tm, nc), lambda i, j=j: (i, j)) for _, nc, j in rows]
    args = [a for a, _, _ in rows]
    for a, nc, kind in halo:
        if kind == 'prev':
            in_specs.append(pl.BlockSpec((8, nc), lambda i: (jnp.maximum(i * (tm // 8) - 1, 0), 0)))
        else:
            in_specs.append(pl.BlockSpec((8, nc), lambda i: (jnp.minimum((i + 1) * (tm // 8), T // 8 - 1), 0)))
        args.append(a)
    in_specs += [_full_spec(a) for a in consts]
    args += list(consts)
    out_shape, out_specs = [], []
    for ncols, dtype, kind in outs:
        if kind == 'row':
            out_shape.append(jax.ShapeDtypeStruct((T, ncols), dtype))
            out_specs.append(pl.BlockSpec((tm, ncols), lambda i: (i, 0)))
        else:
            out_shape.append(jax.ShapeDtypeStruct((kind, ncols), dtype))
            out_specs.append(pl.BlockSpec((kind, ncols), lambda i: (0, 0)))

    def body(*refs):
        i = pl.program_id(0)
        vals = [r[...] for r in refs[:n_in]]
        res = fn(i, n_tiles, *vals)
        for (ncols, dtype, kind), o_ref, val in zip(outs, refs[n_in:], res, strict=True):
            if kind == 'row':
                o_ref[...] = val.astype(dtype)
            else:
                @pl.when(i == 0)
                def _():
                    o_ref[...] = jnp.zeros_like(o_ref)
                o_ref[...] += val.astype(dtype)

    h_in, h_in_specs, h_out_specs, h_out_shape, h_scratch = _hosted_args(hosted)
    res = pl.pallas_call(
        _hosting(body, hosted, n_in, len(outs), 0, n_tiles), name=name, grid=(n_tiles,),
        in_specs=in_specs + h_in_specs, out_specs=out_specs + h_out_specs, out_shape=out_shape + h_out_shape,
        scratch_shapes=h_scratch, compiler_params=_cparams(("arbitrary",)),
    )(*args, *h_in)
    return res


def _pick(n, cands):
    for c in cands:
        if n % c == 0:
            return c
    return n


MM_ROWS = (1024, 512, 256, 128)
MM_COLS = (1536, 1408, 1024, 896, 768, 512, 256, 128)
MM_WIDE = 3000


def _mm_nn(a, w, *, name, out_dtype=F32, hosted=None):
    T, K = a.shape
    N = w.shape[1]
    tm = _pick(T, MM_ROWS)
    tn = _pick(N, MM_COLS)
    grid = (N // tn, T // tm)
    h_in, h_in_specs, h_out_specs, h_out_shape, h_scratch = _hosted_args(hosted)

    def body(a_ref, w_ref, o_ref):
        o_ref[...] = jnp.dot(a_ref[...], w_ref[...], preferred_element_type=F32).astype(out_dtype)

    res = pl.pallas_call(
        _hosting(body, hosted, 2, 1, 0, grid), name=name, grid=grid,
        in_specs=[pl.BlockSpec((tm, K), lambda j, i: (i, 0)), pl.BlockSpec((K, tn), lambda j, i: (0, j))] + h_in_specs,
        out_specs=[pl.BlockSpec((tm, tn), lambda j, i: (i, j))] + h_out_specs,
        out_shape=[jax.ShapeDtypeStruct((T, N), out_dtype)] + h_out_shape, scratch_shapes=h_scratch,
        compiler_params=_cparams(("arbitrary", "arbitrary")),
    )(a, w, *h_in)
    return res if hosted else res[0]


def _mm_nt(dy, w, *, name, out_dtype=F32, hosted=None):
    T, N = dy.shape
    K = w.shape[0]
    tm = _pick(T, MM_ROWS if N <= MM_WIDE else MM_ROWS[1:])
    tk = _pick(K, MM_COLS[1:])
    grid = (K // tk, T // tm)
    h_in, h_in_specs, h_out_specs, h_out_shape, h_scratch = _hosted_args(hosted)

    def body(dy_ref, w_ref, o_ref):
        o_ref[...] = lax.dot_general(dy_ref[...], w_ref[...], (((1,), (1,)), ((), ())),
                                     preferred_element_type=F32).astype(out_dtype)

    res = pl.pallas_call(
        _hosting(body, hosted, 2, 1, 0, grid), name=name, grid=grid,
        in_specs=[pl.BlockSpec((tm, N), lambda j, i: (i, 0)), pl.BlockSpec((tk, N), lambda j, i: (j, 0))] + h_in_specs,
        out_specs=[pl.BlockSpec((tm, tk), lambda j, i: (i, j))] + h_out_specs,
        out_shape=[jax.ShapeDtypeStruct((T, K), out_dtype)] + h_out_shape, scratch_shapes=h_scratch,
        compiler_params=_cparams(("arbitrary", "arbitrary")),
    )(dy, w, *h_in)
    return res if hosted else res[0]


def _mm_tn(a, dy, *, name, out_dtype=F32, col_shards=None):
    T, K = a.shape
    N = dy.shape[1]
    tm = _pick(T, MM_ROWS)
    tn = N // col_shards if col_shards else _pick(N, MM_COLS[1:])
    n_t = T // tm

    def body(a_ref, dy_ref, o_ref, acc_ref):
        i = pl.program_id(1)

        @pl.when(i == 0)
        def _():
            acc_ref[...] = jnp.zeros_like(acc_ref)

        acc_ref[...] += lax.dot_general(a_ref[...], dy_ref[...], (((0,), (0,)), ((), ())), preferred_element_type=F32)

        @pl.when(i == n_t - 1)
        def _():
            o_ref[...] = acc_ref[...].astype(out_dtype)

    if col_shards:
        out_specs = pl.BlockSpec((None, K, tn), lambda j, i: (j, 0, 0))
        out_shape = jax.ShapeDtypeStruct((col_shards, K, tn), out_dtype)
    else:
        out_specs = pl.BlockSpec((K, tn), lambda j, i: (0, j))
        out_shape = jax.ShapeDtypeStruct((K, N), out_dtype)
    return pl.pallas_call(
        body, name=name, grid=(N // tn, n_t),
        in_specs=[pl.BlockSpec((tm, K), lambda j, i: (i, 0)), pl.BlockSpec((tm, tn), lambda j, i: (i, j))],
        out_specs=out_specs, out_shape=out_shape, scratch_shapes=[pltpu.VMEM((K, tn), F32)],
        compiler_params=_cparams(("arbitrary", "arbitrary")),
    )(a, dy)


def _mm_then(products, fn, rows, consts, outs, *, tm, name):
    T = products[0][0].shape[0]
    n_tiles = T // tm
    n_p, n_r = len(products), len(rows)
    in_specs, args = [], []
    for a, w, _ in products:
        in_specs += [pl.BlockSpec((tm, a.shape[1]), lambda i: (i, 0)), _full_spec(w)]
        args += [a, w]
    in_specs += [pl.BlockSpec((tm, a.shape[1]), lambda i: (i, 0)) for a in rows] + [_full_spec(c_) for c_ in consts]
    args += list(rows) + list(consts)
    n_in = len(args)
    out_shape, out_specs = [], []
    for ncols, dtype, kind in outs:
        if kind == 'row':
            out_shape.append(jax.ShapeDtypeStruct((T, ncols), dtype))
            out_specs.append(pl.BlockSpec((tm, ncols), lambda i: (i, 0)))
        else:
            out_shape.append(jax.ShapeDtypeStruct((kind, ncols), dtype))
            out_specs.append(pl.BlockSpec((kind, ncols), lambda i: (0, 0)))

    def body(*refs):
        i = pl.program_id(0)
        y = None
        for k, (_, _, form) in enumerate(products):
            a_ref, w_ref = refs[2 * k], refs[2 * k + 1]
            dims = (((1,), (0,)), ((), ())) if form == 'nn' else (((1,), (1,)), ((), ()))
            t = lax.dot_general(a_ref[...], w_ref[...], dims, preferred_element_type=F32)
            y = t if y is None else y + t
        res = fn(i, n_tiles, y, *[r[...] for r in refs[2 * n_p:n_in]])
        for (ncols, dtype, kind), o_ref, val in zip(outs, refs[n_in:], res, strict=True):
            if kind == 'row':
                o_ref[...] = val.astype(dtype)
            else:
                @pl.when(i == 0)
                def _():
                    o_ref[...] = jnp.zeros_like(o_ref)
                o_ref[...] += val.astype(dtype)

    return pl.pallas_call(body, name=name, grid=(n_tiles,), in_specs=in_specs, out_specs=out_specs,
                          out_shape=out_shape, compiler_params=_cparams(("arbitrary",)))(*args)


def _seg_ones(n):
    r = lax.broadcasted_iota(jnp.int32, (n, n), 0) // HD
    c = lax.broadcasted_iota(jnp.int32, (n, n), 1) // HD
    return (r == c).astype(F32)


def _segsum_raw(x):
    ones = _seg_ones(x.shape[1])
    if MXU == F32:
        return jnp.dot(x, ones, precision=HI, preferred_element_type=F32)
    hi = x.astype(MXU)
    lo = (x - hi.astype(F32)).astype(MXU)
    ones = ones.astype(MXU)
    return jnp.dot(hi, ones, preferred_element_type=F32) + jnp.dot(lo, ones, preferred_element_type=F32)


@jax.custom_vjp
def _segsum(x):
    return _segsum_raw(x)


def _segsum_fwd(x):
    return _segsum_raw(x), None


def _segsum_bwd(_, g):
    return (_segsum_raw(g),)


_segsum.defvjp(_segsum_fwd, _segsum_bwd)


def _mxu(x):
    return x.astype(MXU)


@jax.custom_vjp
def _bdot(a, b):
    return jnp.dot(_mxu(a), _mxu(b), preferred_element_type=F32)


def _bdot_fwd(a, b):
    return _bdot(a, b), (a, b)


def _bdot_bwd(res, g):
    a, b = res
    da = lax.dot_general(_mxu(g), _mxu(b), (((1,), (1,)), ((), ())), preferred_element_type=F32)
    db = lax.dot_general(_mxu(a), _mxu(g), (((0,), (0,)), ((), ())), preferred_element_type=F32)
    return da.astype(a.dtype), db.astype(b.dtype)


_bdot.defvjp(_bdot_fwd, _bdot_bwd)


@jax.custom_vjp
def _bdot_nt(a, b):
    return lax.dot_general(_mxu(a), _mxu(b), (((1,), (1,)), ((), ())), preferred_element_type=F32)


def _bdot_nt_fwd(a, b):
    return _bdot_nt(a, b), (a, b)


def _bdot_nt_bwd(res, g):
    a, b = res
    da = jnp.dot(_mxu(g), _mxu(b), preferred_element_type=F32)
    db = lax.dot_general(_mxu(g), _mxu(a), (((0,), (0,)), ((), ())), preferred_element_type=F32)
    return da.astype(a.dtype), db.astype(b.dtype)


_bdot_nt.defvjp(_bdot_nt_fwd, _bdot_nt_bwd)


def _sigmoid(x):
    return 1.0 / (1.0 + jnp.exp(-x))


def _softplus(x):
    return jnp.maximum(x, 0.0) + jnp.log(1.0 + jnp.exp(jnp.minimum(x, -x)))


def _norm_mod(x, gain, scale, shift):
    inv = lax.rsqrt(jnp.mean(x * x, axis=-1, keepdims=True) + RMS_EPS)
    return (x * inv) * gain * (1.0 + scale) + shift


def _prep(mixed, decay_w0, lora_up, iclr_a0, gate_up, k_k, k_a):
    r = mixed[:, 0:RW]
    k = mixed[:, RW:2 * RW]
    v = mixed[:, 2 * RW:3 * RW]
    z = mixed[:, 3 * RW:3 * RW + 128]
    xg = mixed[:, 3 * RW + 128:]
    lane = lax.broadcasted_iota(jnp.int32, z.shape, 1)
    tz = jnp.where(lane < 64, jnp.tanh(z), z)
    lo = _bdot(tz, lora_up)
    w_log = -_softplus(-(decay_w0 + lo[:, :RW])) - 0.5
    lw = -jnp.exp(w_log)
    a_ic = _sigmoid(iclr_a0 + lo[:, RW:])
    g = _bdot(_sigmoid(xg), gate_up)
    kk = k * k_k
    kk = kk / jnp.maximum(jnp.sqrt(_segsum(kk * kk)), 1e-12)
    k_mod = k * (1.0 + (a_ic - 1.0) * k_a)
    return jnp.concatenate([r, lw, k_mod, v, -kk, kk * a_ic, g], axis=1)


def _post(y, r, k, v, g, lnx_gain, lnx_bias, r_k):
    mu = _segsum(y) * (1.0 / HD)
    yc = y - mu
    var = _segsum(yc * yc) * (1.0 / HD)
    yn = yc * lax.rsqrt(var + GN_EPS) * lnx_gain + lnx_bias
    bonus = _segsum(r * k * r_k) * v
    return (yn + bonus) * g


def _merge(pg, ma, mb, bias):
    gates = _sigmoid(pg + bias)
    return gates[:, :D] * ma + gates[:, D:] * mb


def _swiglu(u, v):
    return u * _sigmoid(u) * v


def _ffn_in(h, w1, w3, *, name):
    T, K = h.shape
    F = w1.shape[1]
    tm = _pick(T, MM_ROWS)
    tn = _pick(F, MM_COLS[1:])

    def body(h_ref, w1_ref, w3_ref, u_ref, v_ref, a_ref):
        u = jnp.dot(h_ref[...], w1_ref[...], preferred_element_type=F32).astype(MXU)
        v = jnp.dot(h_ref[...], w3_ref[...], preferred_element_type=F32).astype(MXU)
        u_ref[...] = u
        v_ref[...] = v
        a_ref[...] = _swiglu(u.astype(F32), v.astype(F32)).astype(MXU)

    wspec = pl.BlockSpec((K, tn), lambda j, i: (0, j))
    ospec = pl.BlockSpec((tm, tn), lambda j, i: (i, j))
    return pl.pallas_call(
        body, name=name, grid=(F // tn, T // tm),
        in_specs=[pl.BlockSpec((tm, K), lambda j, i: (i, 0)), wspec, wspec],
        out_specs=[ospec] * 3, out_shape=[jax.ShapeDtypeStruct((T, F), MXU)] * 3,
        compiler_params=_cparams(("arbitrary", "arbitrary")),
    )(h, w1, w3)


def _ffn_act_bwd(dff, w2, u, v, *, name):
    T, N = dff.shape
    F = w2.shape[0]
    tm = _pick(T, MM_ROWS)
    tk = _pick(F, MM_COLS[1:])

    def body(dy_ref, w_ref, u_ref, v_ref, du_ref, dv_ref):
        dact = lax.dot_general(dy_ref[...], w_ref[...], (((1,), (1,)), ((), ())), preferred_element_type=F32)
        _, vjp = jax.vjp(_swiglu, u_ref[...].astype(F32), v_ref[...].astype(F32))
        du, dv = vjp(dact)
        du_ref[...] = du.astype(MXU)
        dv_ref[...] = dv.astype(MXU)

    tile = pl.BlockSpec((tm, tk), lambda j, i: (i, j))
    return pl.pallas_call(
        body, name=name, grid=(F // tk, T // tm),
        in_specs=[pl.BlockSpec((tm, N), lambda j, i: (i, 0)), pl.BlockSpec((tk, N), lambda j, i: (j, 0)), tile, tile],
        out_specs=[tile, tile], out_shape=[jax.ShapeDtypeStruct((T, F), MXU)] * 2,
        compiler_params=_cparams(("arbitrary", "arbitrary")),
    )(dff, w2, u, v)


def _mm_nt2(dy1, w1, dy2, w2, *, name):
    T, N = dy1.shape
    K = w1.shape[0]
    tm = _pick(T, MM_ROWS[1:])
    tk = _pick(K, MM_COLS[1:])

    def body(d1_ref, w1_ref, d2_ref, w2_ref, o_ref):
        nt = lambda a, b: lax.dot_general(a[...], b[...], (((1,), (1,)), ((), ())), preferred_element_type=F32)
        o_ref[...] = nt(d1_ref, w1_ref) + nt(d2_ref, w2_ref)

    dspec = pl.BlockSpec((tm, N), lambda j, i: (i, 0))
    wspec = pl.BlockSpec((tk, N), lambda j, i: (j, 0))
    return pl.pallas_call(
        body, name=name, grid=(K // tk, T // tm), in_specs=[dspec, wspec, dspec, wspec],
        out_specs=pl.BlockSpec((tm, tk), lambda j, i: (i, j)), out_shape=jax.ShapeDtypeStruct((T, K), F32),
        compiler_params=_cparams(("arbitrary", "arbitrary")),
    )(dy1, w1, dy2, w2)


@functools.partial(jax.custom_vjp, nondiff_argnums=(1,))
def _lane_roll(x, s):
    return pltpu.roll(x, s, 1)


def _lane_roll_fwd(x, s):
    return pltpu.roll(x, s, 1), None


def _lane_roll_bwd(s, _, g):
    n = g.shape[1]
    return (pltpu.roll(g, (n - s) % n, 1),)


_lane_roll.defvjp(_lane_roll_fwd, _lane_roll_bwd)


def _rope(x, cos, sin_lo, sin_hi):
    n = x.shape[1]
    return x * cos + _lane_roll(x, n - 8) * sin_lo + _lane_roll(x, 8) * sin_hi


def _head_rms(x, gain):
    return x * lax.rsqrt(_segsum(x * x) * (1.0 / HD) + RMS_EPS) * gain


def _attn_block(qkv_c, qkv_p, tab_c, tab_p, qg, kg, sinks, first):
    def tabs(tab, n):
        return tab[:, 0:n], tab[:, RW:RW + n], tab[:, 2 * RW:2 * RW + n]

    qg = jnp.concatenate([qg] * NH, axis=1)
    kg = jnp.concatenate([kg] * 2, axis=1)
    q = _rope(_head_rms(qkv_c[:, :RW], qg), *tabs(tab_c, RW))
    k_c = _rope(_head_rms(qkv_c[:, RW:RW + 128], kg), *tabs(tab_c, 128))
    k_p = _rope(_head_rms(qkv_p[:, RW:RW + 128], kg), *tabs(tab_p, 128))
    kband = jnp.concatenate([k_p, k_c], axis=0)
    vband = jnp.concatenate([qkv_p[:, RW + 128:], qkv_c[:, RW + 128:]], axis=0)
    G = ATTN_STACK
    qi = lax.broadcasted_iota(jnp.int32, (G * BLK, 2 * BLK), 0) % BLK
    kj = lax.broadcasted_iota(jnp.int32, (G * BLK, 2 * BLK), 1)
    dist = qi + BLK - kj
    valid = (dist >= 0) & (dist < BLK) & (jnp.logical_not(first) | (kj >= BLK))
    row_g = lax.broadcasted_iota(jnp.int32, (G * BLK, 1), 0) // BLK
    outs = []
    for h0 in range(0, NH, G):
        kvh = h0 // 4
        kb = kband[:, kvh * HD:(kvh + 1) * HD]
        vb = vband[:, kvh * HD:(kvh + 1) * HD]
        qs = jnp.concatenate([q[:, (h0 + g) * HD:(h0 + g + 1) * HD] for g in range(G)], axis=0)
        s = _bdot_nt(qs, kb) * (HD ** -0.5)
        s = jnp.where(valid, s, NEG_INF)
        sink = jnp.zeros((G * BLK, 1), F32)
        for g in range(G):
            sink = jnp.where(row_g == g, sinks[:, h0 + g:h0 + g + 1], sink)
        m = lax.stop_gradient(jnp.maximum(jnp.max(s, axis=-1, keepdims=True), sink))
        e = jnp.exp(s - m)
        p = e * (1.0 / (jnp.sum(e, axis=-1, keepdims=True) + jnp.exp(sink - m)))
        o = _bdot(p, vb)
        outs += [o[g * BLK:(g + 1) * BLK] for g in range(G)]
    return jnp.concatenate(outs, axis=1)


def _attn_blocks(qkv_c, qkv_p, tab_c, tab_p, qg, kg, sinks, first):
    nb = qkv_c.shape[0] // BLK
    G = 4

    def tabs(tab, n):
        return tab[:, 0:n], tab[:, RW:RW + n], tab[:, 2 * RW:2 * RW + n]

    qg = jnp.concatenate([qg] * NH, axis=1)
    kg = jnp.concatenate([kg] * 2, axis=1)
    q = _rope(_head_rms(qkv_c[:, :RW], qg), *tabs(tab_c, RW))
    k_in = jnp.concatenate([qkv_p[:, RW:RW + 128], qkv_c[:, RW:RW + 128]], axis=0)
    k = _rope(_head_rms(k_in, kg), *tabs(jnp.concatenate([tab_p, tab_c], axis=0), 128))
    v = jnp.concatenate([qkv_p[:, RW + 128:], qkv_c[:, RW + 128:]], axis=0)

    pile = lambda xs: jnp.concatenate([x_[None] for x_ in xs], axis=0)

    def bands(t):
        return pile([t[b * BLK:(b + 2) * BLK, kvh * HD:(kvh + 1) * HD] for kvh in range(2) for b in range(nb)])

    qs = pile([jnp.concatenate([q[b * BLK:(b + 1) * BLK, (G * kvh + g) * HD:(G * kvh + g + 1) * HD]
                                for g in range(G)], axis=0) for kvh in range(2) for b in range(nb)])
    s = _bmm(qs, bands(k), 2, 2, 1) * (HD ** -0.5)
    qi = lax.broadcasted_iota(jnp.int32, (G * BLK, 2 * BLK), 0) % BLK
    kj = lax.broadcasted_iota(jnp.int32, (G * BLK, 2 * BLK), 1)
    dist = qi + BLK - kj
    in_band = (dist >= 0) & (dist < BLK)
    pair = lax.broadcasted_iota(jnp.int32, (2 * nb, 1, 1), 0)
    no_prev = (pair % nb == 0) & first
    valid = in_band[None] & (jnp.logical_not(no_prev) | (kj >= BLK)[None])
    s = jnp.where(valid, s, NEG_INF)
    row_g = lax.broadcasted_iota(jnp.int32, (G * BLK, 1), 0) // BLK
    sink = []
    for kvh in range(2):
        col = jnp.zeros((G * BLK, 1), F32)
        for g in range(G):
            col = jnp.where(row_g == g, sinks[:, G * kvh + g:G * kvh + g + 1], col)
        sink += [col] * nb
    sink = pile(sink)
    m = lax.stop_gradient(jnp.maximum(jnp.max(s, axis=-1, keepdims=True), sink))
    e = jnp.exp(s - m)
    p = e * (1.0 / (jnp.sum(e, axis=-1, keepdims=True) + jnp.exp(sink - m)))
    o = _bmm(p, bands(v), 2, 1, 1)
    return jnp.concatenate([jnp.concatenate([o[kvh * nb + b, g * BLK:(g + 1) * BLK] for kvh in range(2)
                                             for g in range(G)], axis=1) for b in range(nb)], axis=0)


def _heads(x):
    return jnp.stack([x[:, h * HD:(h + 1) * HD] for h in range(NH)], axis=0)


def _unheads(x):
    return jnp.concatenate([x[h] for h in range(NH)], axis=1)


def _split(x, n):
    parts, rest = [], x
    for _ in range(n):
        p = rest.astype(MXU)
        parts.append(p)
        rest = rest - p.astype(F32)
    return parts


def _bdot_batched(a, b, ca, cb):
    return lax.dot_general(a, b, (((ca,), (cb,)), ((0,), (0,))), preferred_element_type=F32)


def _bmm_passes(a, b, ca, cb, passes):
    if MXU == F32:
        return lax.dot_general(a, b, (((ca,), (cb,)), ((0,), (0,))), precision=HI, preferred_element_type=F32)
    if passes == 1:
        return _bdot_batched(a.astype(MXU), b.astype(MXU), ca, cb)
    (a0, a1), (b0, b1) = _split(a, 2), _split(b, 2)
    return _bdot_batched(a0, b0, ca, cb) + (_bdot_batched(a0, b1, ca, cb) + _bdot_batched(a1, b0, ca, cb))


@functools.partial(jax.custom_vjp, nondiff_argnums=(2, 3, 4))
def _bmm(a, b, ca, cb, passes=1):
    return _bmm_passes(a, b, ca, cb, passes)


def _bmm_fwd(a, b, ca, cb, passes):
    return _bmm_passes(a, b, ca, cb, passes), (a, b)


def _bmm_bwd(ca, cb, passes, res, g):
    a, b = res
    if (ca, cb) == (2, 1):
        return _bmm_passes(g, b, 2, 2, passes), _bmm_passes(a, g, 1, 1, passes)
    if (ca, cb) == (2, 2):
        return _bmm_passes(g, b, 2, 1, passes), _bmm_passes(g, a, 1, 1, passes)
    return _bmm_passes(b, g, 2, 2, passes), _bmm_passes(a, g, 2, 1, passes)


_bmm.defvjp(_bmm_fwd, _bmm_bwd)


def _tri_dot(x, transpose):
    C = x.shape[1]
    ri = lax.broadcasted_iota(jnp.int32, (C, C), 0)
    ci = lax.broadcasted_iota(jnp.int32, (C, C), 1)
    tri = jnp.broadcast_to(((ri <= ci) if transpose else (ri >= ci)).astype(MXU), (x.shape[0], C, C))
    if MXU == F32:
        return lax.dot_general(tri, x, (((2,), (1,)), ((0,), (0,))), precision=HI, preferred_element_type=F32)
    p0, p1, p2 = _split(x, 3)
    return _bdot_batched(tri, p0, 2, 1) + (_bdot_batched(tri, p1, 2, 1) + _bdot_batched(tri, p2, 2, 1))


@jax.custom_vjp
def _cumsum_rows(x):
    return _tri_dot(x, False)


def _cumsum_rows_fwd(x):
    return _tri_dot(x, False), None


def _cumsum_rows_bwd(_, g):
    return (_tri_dot(g, True),)


_cumsum_rows.defvjp(_cumsum_rows_fwd, _cumsum_rows_bwd)

P_SCORE = 1
P_SOLVE = 1
P_STATE = 1
SCAN_CHUNKS = (4, 2, 1)


def _neumann(l):
    C = l.shape[1]
    eye = (lax.broadcasted_iota(jnp.int32, (C, C), 0) == lax.broadcasted_iota(jnp.int32, (C, C), 1)).astype(F32)
    x, lp = eye + l, l
    for _ in range(int(math.log2(C)) - 1):
        lp = _bmm(lp, lp, 2, 1, P_SOLVE)
        x = x + _bmm(x, lp, 2, 1, P_SOLVE)
    return x


@jax.custom_vjp
def _unit_lower_inverse(l):
    return _neumann(l)


def _unit_lower_inverse_fwd(l):
    x = _neumann(l)
    return x, x


def _unit_lower_inverse_bwd(x, g):
    return (_bmm(_bmm(x, g, 1, 1, P_SOLVE), x, 2, 2, P_SOLVE),)


_unit_lower_inverse.defvjp(_unit_lower_inverse_fwd, _unit_lower_inverse_bwd)


def _known_inverse(x):
    @jax.custom_vjp
    def f(l):
        return x

    f.defvjp(lambda l: (x, None), lambda _, g: (_bmm(_bmm(x, g, 1, 1, P_SOLVE), x, 2, 2, P_SOLVE),))
    return f


def _chunk(S0, r, lw, k, v, a, b, inverse=None):
    C = CHUNK
    n = r.shape[1] // C
    fold = lambda t: t.reshape(NH * n, C, HD)
    r, lw, k, v, a, b = (fold(t) for t in (r, lw, k, v, a, b))
    ri = lax.broadcasted_iota(jnp.int32, (C, C), 0)
    ci = lax.broadcasted_iota(jnp.int32, (C, C), 1)
    incl = (ri >= ci)
    strict = (ri > ci)
    eye = (ri == ci).astype(F32)
    cum = _cumsum_rows(lw)
    p_in = jnp.exp(cum)
    p_ex = jnp.exp(cum - lw)
    p_inv = jnp.exp(-cum)
    at, rt, bt, kt = a * p_ex, r * p_in, b * p_inv, k * p_inv
    lhs = jnp.concatenate([at, rt], axis=1)
    rhs_ = jnp.concatenate([bt, kt], axis=1)
    sc = _bmm(lhs, rhs_, 2, 2, P_SCORE)
    a_ab = jnp.where(strict, sc[:, :C, :C], 0.0)
    a_ak = jnp.where(strict, sc[:, :C, C:], 0.0)
    incl2 = (lax.broadcasted_iota(jnp.int32, (C, 2 * C), 0) >= lax.broadcasted_iota(jnp.int32, (C, 2 * C), 1) % C)
    a_r = jnp.where(incl2, sc[:, C:, :], 0.0)
    av = _bmm(a_ak, v, 2, 1, P_SCORE)
    x = x_all = (_unit_lower_inverse if inverse is None else _known_inverse(inverse))(a_ab)
    p_last = jnp.exp(cum[:, C - 1:C, :])
    per_chunk = lambda t: t.reshape((NH, n) + t.shape[1:])
    lhs, rhs_, a_r, av, x, v, p_last = (per_chunk(t) for t in (lhs, rhs_, a_r, av, x, v, p_last))
    S, ys = S0, []
    for c in range(n):
        s0 = _bmm(lhs[:, c], S, 2, 2, P_STATE)
        u = _bmm(x[:, c], s0[:, :C] + av[:, c], 2, 1, P_SOLVE)
        uv = jnp.concatenate([u, v[:, c]], axis=1)
        ys.append(s0[:, C:] + _bmm(a_r[:, c], uv, 2, 1, P_SCORE))
        S = (S + _bmm(uv, rhs_[:, c], 1, 1, P_STATE)) * p_last[:, c]
    return jnp.concatenate(ys, axis=1), S, x_all


def _hosting(body, ex, n_in, n_out, n_scratch, n_steps):
    if ex is None:
        return body

    def wrapped(*refs):
        a = n_in
        b = a + ex.n_in
        c = b + n_out
        d = c + ex.n_out
        e = d + n_scratch
        ex_refs = (refs[a:b], refs[c:d], refs[e:])
        grid = n_steps if isinstance(n_steps, tuple) else (n_steps,)
        first = last = True
        for ax_, size in enumerate(grid):
            first = first & (pl.program_id(ax_) == 0)
            last = last & (pl.program_id(ax_) == size - 1)

        @pl.when(first)
        def _():
            ex.start(*ex_refs)

        body(*refs[:a], *refs[b:c], *refs[d:e])

        @pl.when(last)
        def _():
            ex.wait(*ex_refs)

    return wrapped


def _hosted_args(ex):
    if ex is None:
        return [], [], [], [], []
    any_spec = pl.BlockSpec(memory_space=pl.ANY)
    return list(ex.arrays), [any_spec] * ex.n_in, [any_spec] * ex.n_out, list(ex.out_shape), list(ex.scratch)


def _scan_fwd(rw, *, name, hosted=None):
    T = rw.shape[0]
    nc = _pick(T // CHUNK, SCAN_CHUNKS)
    rows = CHUNK * nc
    n = T // rows
    h_in, h_in_specs, h_out_specs, h_out_shape, h_scratch = _hosted_args(hosted)

    def body(r_ref, lw_ref, k_ref, v_ref, a_ref, b_ref, y_ref, ck_ref, inv_ref, s_ref):
        @pl.when(pl.program_id(0) == 0)
        def _():
            s_ref[...] = jnp.zeros_like(s_ref)

        S0 = s_ref[...]
        ck_ref[0] = S0
        y, S1, inv = _chunk(S0, *[_heads(ref[...]) for ref in (r_ref, lw_ref, k_ref, v_ref, a_ref, b_ref)])
        y_ref[...] = _unheads(y)
        inv_ref[0] = inv
        s_ref[...] = S1

    col = lambda j: pl.BlockSpec((rows, RW), lambda i: (i, j))
    return pl.pallas_call(
        _hosting(body, hosted, 6, 3, 1, n), name=name, grid=(n,),
        in_specs=[col(j) for j in range(6)] + h_in_specs,
        out_specs=[pl.BlockSpec((rows, RW), lambda i: (i, 0)),
                   pl.BlockSpec((1, NH, HD, HD), lambda i: (i, 0, 0, 0)),
                   pl.BlockSpec((1, NH * nc, CHUNK, CHUNK), lambda i: (i, 0, 0, 0))] + h_out_specs,
        out_shape=[jax.ShapeDtypeStruct((T, RW), F32), jax.ShapeDtypeStruct((n, NH, HD, HD), F32),
                   jax.ShapeDtypeStruct((n, NH * nc, CHUNK, CHUNK), F32)] + h_out_shape,
        scratch_shapes=[pltpu.VMEM((NH, HD, HD), F32)] + h_scratch,
        compiler_params=_cparams(("arbitrary",)),
    )(rw, rw, rw, rw, rw, rw, *h_in)


def _scan_bwd(rw, ck, inv, dy, *, name, hosted=None):
    T = rw.shape[0]
    nc = _pick(T // CHUNK, SCAN_CHUNKS)
    rows = CHUNK * nc
    n = T // rows

    def body(r_ref, lw_ref, k_ref, v_ref, a_ref, b_ref, ck_ref, inv_ref, dy_ref, o_ref, ds_ref):
        @pl.when(pl.program_id(0) == 0)
        def _():
            ds_ref[...] = jnp.zeros_like(ds_ref)

        prim = [_heads(ref[...]) for ref in (r_ref, lw_ref, k_ref, v_ref, a_ref, b_ref)]
        known = inv_ref[0]
        _, vjp = jax.vjp(lambda S0, *p: _chunk(S0, *p, inverse=known)[:2], ck_ref[0], *prim)
        grads = vjp((_heads(dy_ref[...]), ds_ref[...]))
        ds_ref[...] = grads[0]
        o_ref[...] = jnp.concatenate([_unheads(g) for g in grads[1:]], axis=1).astype(o_ref.dtype)

    h_in, h_in_specs, h_out_specs, h_out_shape, h_scratch = _hosted_args(hosted)
    col = lambda j: pl.BlockSpec((rows, RW), lambda i: (n - 1 - i, j))
    return pl.pallas_call(
        _hosting(body, hosted, 9, 1, 1, n), name=name, grid=(n,),
        in_specs=[col(j) for j in range(6)] + [pl.BlockSpec((1, NH, HD, HD), lambda i: (n - 1 - i, 0, 0, 0)),
                                               pl.BlockSpec((1, NH * nc, CHUNK, CHUNK), lambda i: (n - 1 - i, 0, 0, 0)),
                                               pl.BlockSpec((rows, RW), lambda i: (n - 1 - i, 0))] + h_in_specs,
        out_specs=[pl.BlockSpec((rows, 6 * RW), lambda i: (n - 1 - i, 0))] + h_out_specs,
        out_shape=[jax.ShapeDtypeStruct((T, 6 * RW), MXU)] + h_out_shape,
        scratch_shapes=[pltpu.VMEM((NH, HD, HD), F32)] + h_scratch,
        compiler_params=_cparams(("arbitrary",)),
    )(rw, rw, rw, rw, rw, rw, ck, inv, dy, *h_in)


ATTN_BLOCKS = (4, 2, 1)
ATTN_STACK = 4

def _attn_fwd(qkv, tab, qg, kg, sinks, *, name, hosted=None):
    T = qkv.shape[0]
    nb = _pick(T // BLK, ATTN_BLOCKS)
    n = T // (BLK * nb)
    h_in, h_in_specs, h_out_specs, h_out_shape, h_scratch = _hosted_args(hosted)

    def body(c_ref, p_ref, tc_ref, tp_ref, qg_ref, kg_ref, s_ref, o_ref):
        o_ref[...] = _attn_blocks(c_ref[...], p_ref[...], tc_ref[...], tp_ref[...], qg_ref[...], kg_ref[...],
                                  s_ref[...], pl.program_id(0) == 0).astype(o_ref.dtype)

    cur = lambda w: pl.BlockSpec((nb * BLK, w), lambda i: (i, 0))
    prev = lambda w: pl.BlockSpec((BLK, w), lambda i: (jnp.maximum(i * nb - 1, 0), 0))
    return pl.pallas_call(
        _hosting(body, hosted, 7, 1, 0, n), name=name, grid=(n,),
        in_specs=[cur(QKV_W), prev(QKV_W), cur(3 * RW), prev(3 * RW), _full_spec(qg), _full_spec(kg),
                  _full_spec(sinks)] + h_in_specs,
        out_specs=[cur(RW)] + h_out_specs, out_shape=[jax.ShapeDtypeStruct((T, RW), MXU)] + h_out_shape,
        scratch_shapes=h_scratch,
        compiler_params=_cparams(("arbitrary",)),
    )(qkv, qkv, tab, tab, qg, kg, sinks, *h_in)


def _attn_bwd(qkv, tab, qg, kg, sinks, dy, *, name, hosted=None):
    T = qkv.shape[0]
    nb = _pick(T // BLK, ATTN_BLOCKS)
    n = T // (BLK * nb)
    h_in, h_in_specs, h_out_specs, h_out_shape, h_scratch = _hosted_args(hosted)

    def body(c_ref, p_ref, tc_ref, tp_ref, qg_ref, kg_ref, s_ref, dy_ref, dqkv_ref, dqg_ref, dkg_ref, ds_ref, carry_ref):
        i = pl.program_id(0)

        @pl.when(i == 0)
        def _():
            carry_ref[...] = jnp.zeros_like(carry_ref)
            dqg_ref[...] = jnp.zeros_like(dqg_ref)
            dkg_ref[...] = jnp.zeros_like(dkg_ref)
            ds_ref[...] = jnp.zeros_like(ds_ref)

        tc, tp = tc_ref[...], tp_ref[...]
        f = lambda c, p_, qg_, kg_, sk: _attn_blocks(c, p_, tc, tp, qg_, kg_, sk, i == n - 1)
        _, vjp = jax.vjp(f, c_ref[...], p_ref[...], qg_ref[...], kg_ref[...], s_ref[...])
        dc, dp, dqg, dkg, dsk = vjp(dy_ref[...].astype(F32))
        last = slice((nb - 1) * BLK, nb * BLK)
        dqkv_ref[...] = dc.astype(dqkv_ref.dtype)
        dqkv_ref[last, :] = (dc[last] + carry_ref[...]).astype(dqkv_ref.dtype)
        carry_ref[...] = dp
        dqg_ref[...] += dqg
        dkg_ref[...] += dkg
        ds_ref[...] += dsk

    cur = lambda w: pl.BlockSpec((nb * BLK, w), lambda i: (n - 1 - i, 0))
    prev = lambda w: pl.BlockSpec((BLK, w), lambda i: (jnp.maximum((n - 1 - i) * nb - 1, 0), 0))
    return pl.pallas_call(
        _hosting(body, hosted, 8, 4, 1, n), name=name, grid=(n,),
        in_specs=[cur(QKV_W), prev(QKV_W), cur(3 * RW), prev(3 * RW), _full_spec(qg), _full_spec(kg), _full_spec(sinks),
                  cur(RW)] + h_in_specs,
        out_specs=[cur(QKV_W), _full_spec(qg), _full_spec(kg), _full_spec(sinks)] + h_out_specs,
        out_shape=[jax.ShapeDtypeStruct((T, QKV_W), MXU), jax.ShapeDtypeStruct(qg.shape, F32),
                   jax.ShapeDtypeStruct(kg.shape, F32), jax.ShapeDtypeStruct(sinks.shape, F32)] + h_out_shape,
        scratch_shapes=[pltpu.VMEM((BLK, QKV_W), F32)] + h_scratch,
        compiler_params=_cparams(("arbitrary",)),
    )(qkv, qkv, tab, tab, qg, kg, sinks, dy, *h_in)


def _shift_down(cur, prev8, i):
    rolled = pltpu.roll(cur, 1, 0)
    first_row = jnp.where(i > 0, prev8[7:8, :], 0.0)
    row = lax.broadcasted_iota(jnp.int32, cur.shape, 0)
    return jnp.where(row == 0, first_row, rolled)


def _shift_up(cur, next8, i, n):
    tm = cur.shape[0]
    rolled = pltpu.roll(cur, tm - 1, 0)
    last_row = jnp.where(i < n - 1, next8[0:1, :], 0.0)
    row = lax.broadcasted_iota(jnp.int32, cur.shape, 0)
    return jnp.where(row == tm - 1, last_row, rolled)


def _ada_parts(ada):
    return [ada[:, j * D:(j + 1) * D] for j in range(6)]


def _rope_table(positions):
    half = HD // 8
    inv_freq = 500000.0 ** (-jnp.arange(half, dtype=F32) / half)
    ang = positions.astype(F32)[:, None] * inv_freq
    cos, sin = jnp.cos(ang), jnp.sin(ang)
    T = positions.shape[0]
    pad = HD - 2 * half
    c64 = jnp.concatenate([cos, cos, jnp.ones((T, pad), F32)], axis=1)
    lo64 = jnp.concatenate([-sin, jnp.zeros((T, HD - half), F32)], axis=1)
    hi64 = jnp.concatenate([jnp.zeros((T, half), F32), sin, jnp.zeros((T, pad), F32)], axis=1)
    return jnp.concatenate([jnp.tile(t, (1, NH)) for t in (c64, lo64, hi64)], axis=1)


GATHER_BEHIND = {"f_proj_shift": [('ffn_w3', 512, 768)], "f_proj_gates": [('ffn_w3', 768, 1024)],
                 "f_prep": [('ffn_w2', 352, 704)],
                 "f_scan": [('ffn_w1', 0, 1024), ('w_branch_a', 0, 512), ('w_branch_b', 0, 512)],
                 "f_post": [('ffn_w3', 0, 512)], "f_attn": [('ffn_w2', 0, 352), ('w_out', 0, 256)]}
LATE = ['w_out', 'w_branch_a', 'w_branch_b', 'ffn_w1', 'ffn_w3', 'ffn_w2']
BACK_ATTN = ['w_out', 'w_branch_a', 'w_branch_b', 'ffn_w2']
BACK_SCAN = ['ffn_w1', 'ffn_w3']
BACK_LAST = ['w_in', 'decay_up', 'iclr_up', 'gate_up']


def _full_weight(g, ax):
    return g.reshape(-1, g.shape[2]) if ax == 0 else jnp.concatenate([g[j] for j in range(4)], axis=1)


def _local_step(x, target, ada, tab, w, s, shards=None):
    T = x.shape[0]
    tm = _pick(T, (512, 256, 128))
    tm_wide = _pick(T, (256, 128))
    tm_vjp = _pick(T, (128,))
    row = lambda n, dt=F32: (n, dt, 'row')
    acc = lambda n, r=1: (n, F32, r)

    def f_norm1(i, n, x_, g, ada_):
        sh, sc = ada_[:, 0:D], ada_[:, D:2 * D]
        return (_norm_mod(x_, g, sc, sh),)
    (h1,) = _rowwise(f_norm1, [x], [s['norm1_gain'], ada], [row(D, MXU)], tm=tm, name="f_norm1")

    landed = {}

    def behind(kernel_name):
        if not shards:
            return None
        return _GatherChips([shards[n] if hi - lo == shards[n].shape[0] else shards[n][lo:hi]
                             for n, lo, hi in GATHER_BEHIND[kernel_name]])

    def took(kernel_name, got):
        landed.update(zip(GATHER_BEHIND[kernel_name], got))

    def mm_behind(a_, w_, kernel_name, **kw):
        ex = behind(kernel_name)
        res = _mm_nn(a_, w_, name=kernel_name, hosted=ex, **kw)
        if ex:
            took(kernel_name, res[1:])
            return res[0]
        return res

    proj = mm_behind(h1, w['w_in'][:, :SHIFT_W], "f_proj_shift")
    proj_qkv = _mm_nn(h1, w['w_in'][:, SHIFT_W:SHIFT_W + QKV_W], name="f_proj_qkv")
    proj_g = mm_behind(h1, w['w_in'][:, SHIFT_W + QKV_W:], "f_proj_gates", out_dtype=MXU)
    prep_consts = [s['decay_w0'], s['lora_up'], s['iclr_a0'], s['gate_up'], s['k_k'], s['k_a']]

    def f_prep(i, n, cur, prev8, mu, *params):
        mixed = cur + (_shift_down(cur, prev8, i) - cur) * mu
        return (_prep(mixed, *params),)
    rw, *got = _rowwise(f_prep, [(proj, SHIFT_W)], [s['tshift_mu']] + prep_consts, [row(7 * RW)], tm=tm_wide,
                        name="f_prep", halo=[(proj, SHIFT_W, 'prev')], hosted=behind("f_prep"))
    took("f_prep", got)
    y, ck, inv, *got = _scan_fwd(rw, name="f_scan", hosted=behind("f_scan"))
    took("f_scan", got)
    post_consts = [s['lnx_gain'], s['lnx_bias'], s['r_k']]

    rkvg = [(rw, RW, j) for j in (0, 2, 3, 6)]

    def f_post(i, n, *args):
        return (_post(*args),)
    ya, *got = _rowwise(f_post, [y] + rkvg, post_consts, [row(RW, MXU)], tm=tm_wide, name="f_post",
                        hosted=behind("f_post"))
    took("f_post", got)
    yb, *got = _attn_fwd(proj_qkv, tab, s['q_norm_gain'], s['k_norm_gain'], s['attn_sinks'], name="f_attn",
                         hosted=behind("f_attn"))
    took("f_attn", got)
    w = dict(w)
    if shards:
        ax = dict(SHARDED)
        for n in LATE:
            rows = [landed[key] for key in sorted(k_ for k_ in landed if k_[0] == n)]
            w[n] = _full_weight(rows[0] if len(rows) == 1 else jnp.concatenate(rows, axis=1), ax[n])
    ma = _mm_nn(ya, w['w_branch_a'], name="f_branch_a", out_dtype=MXU)
    mb = _mm_nn(yb, w['w_branch_b'], name="f_branch_b", out_dtype=MXU)

    def f_merge(i, n, pg, ma_, mb_, bias):
        return (_merge(pg.astype(F32), ma_.astype(F32), mb_.astype(F32), bias),)
    (merged,) = _rowwise(f_merge, [proj_g, ma, mb], [s['branch_gate_b']], [row(D, MXU)], tm=tm, name="f_merge")

    def f_res1(i, n, mo_, x_, g, ada_):
        x1_ = x_ + ada_[:, 2 * D:3 * D] * mo_
        return mo_, x1_, _norm_mod(x1_, g, ada_[:, 4 * D:5 * D], ada_[:, 3 * D:4 * D])
    mo, x1, h2 = _mm_then([(merged, w['w_out'], 'nn')], f_res1, [x], [s['norm2_gain'], ada],
                          [row(D), row(D), row(D, MXU)], tm=tm, name="f_out")
    u, v, act = _ffn_in(h2, w['ffn_w1'], w['ffn_w3'], name="f_ffn_in")

    def f_loss(i, n, ff_, x1_, tgt, ada_):
        g2 = ada_[:, 5 * D:6 * D]
        err = x1_ + g2 * ff_ - tgt
        dx2 = err * (1.0 / D)
        loss = 0.5 * jnp.sum(jnp.sum(err * err, axis=1, keepdims=True) * (1.0 / D), axis=0, keepdims=True)
        return dx2, (dx2 * g2), jnp.broadcast_to(loss, (1, 128)), jnp.sum(dx2 * ff_, axis=0, keepdims=True)
    dx2, dff, loss, dgate2 = _mm_then([(act, w['ffn_w2'], 'nn')], f_loss, [x1, target], [ada],
                                      [row(D), row(D, MXU), acc(128), acc(D)], tm=tm, name="f_ffn_out")

    du, dv = _ffn_act_bwd(dff, w['ffn_w2'], u, v, name="b_ffn_out_dx")
    g_w2 = _mm_tn(act, dff, name="b_ffn_out_dw", out_dtype=MXU)
    g_w1 = _mm_tn(h2, du, name="b_ffn_w1_dw", out_dtype=MXU)
    g_w3 = _mm_tn(h2, dv, name="b_ffn_w3_dw", out_dtype=MXU)

    def b_res1(i, n, dh2_, x1_, dx2_, mo_, g, ada_):
        _, vjp = jax.vjp(_norm_mod, x1_, g, ada_[:, 4 * D:5 * D], ada_[:, 3 * D:4 * D])
        dxn, dg, dsc, dsh = vjp(dh2_)
        dx1_ = dxn + dx2_
        g1 = ada_[:, 2 * D:3 * D]
        return dx1_, dx1_ * g1, dg, dsc, dsh, jnp.sum(dx1_ * mo_, axis=0, keepdims=True)
    dx1, dmo, d_gain2, d_scale2, d_shift2, dgate1 = _mm_then(
        [(du, w['ffn_w1'], 'nt'), (dv, w['ffn_w3'], 'nt')], b_res1, [x1, dx2, mo], [s['norm2_gain'], ada],
        [row(D), row(D, MXU), acc(D), acc(D), acc(D), acc(D)], tm=tm_wide, name="b_ffn_in_dx")
    g_wout = _mm_tn(merged, dmo, name="b_out_dw", out_dtype=MXU)

    def b_merge(i, n, dm, pg, ma_, mb_, bias):
        _, vjp = jax.vjp(_merge, pg.astype(F32), ma_.astype(F32), mb_.astype(F32), bias)
        dpg, dma_, dmb_, dbias = vjp(dm)
        return dpg, dma_, dmb_, dbias
    dpg, dma, dmb, d_bias = _mm_then([(dmo, w['w_out'], 'nt')], b_merge, [proj_g, ma, mb], [s['branch_gate_b']],
                                     [row(GATE_W, MXU), row(D, MXU), row(D, MXU), acc(GATE_W)], tm=tm_wide,
                                     name="b_out_dx")
    dya = _mm_nt(dma, w['w_branch_a'], name="b_branch_a_dx")
    g_wa = _mm_tn(ya, dma, name="b_branch_a_dw", out_dtype=MXU, col_shards=4)
    dyb = _mm_nt(dmb, w['w_branch_b'], name="b_branch_b_dx", out_dtype=F32)
    g_wb = _mm_tn(yb, dmb, name="b_branch_b_dw", out_dtype=MXU, col_shards=4)
    fs = DFF // 4
    gw = dict(w_branch_a=g_wa, w_branch_b=g_wb, w_out=g_wout.reshape(4, D // 4, D),
              ffn_w1=jnp.stack([g_w1[:, j * fs:(j + 1) * fs] for j in range(4)]),
              ffn_w3=jnp.stack([g_w3[:, j * fs:(j + 1) * fs] for j in range(4)]),
              ffn_w2=g_w2.reshape(4, fs, D))
    recv = {}
    dqkv, d_qg, d_kg, d_sinks, *got = _attn_bwd(
        proj_qkv, tab, s['q_norm_gain'], s['k_norm_gain'], s['attn_sinks'], dyb, name="b_attn",
        hosted=shards and _ScatterChips([gw[n] for n in BACK_ATTN]))
    recv.update(zip(BACK_ATTN, got))

    def b_post(i, n, y_, r_, k_, v_, g_, dya_, *params):
        _, vjp = jax.vjp(_post, y_, r_, k_, v_, g_, *params)
        dy_, dr_, dk_, dv_, dg_, *dparams = vjp(dya_)
        return (dy_, jnp.concatenate([dr_, dk_, dv_, dg_], axis=1), *dparams)
    dy, drkvg, d_lnx_gain, d_lnx_bias, d_r_k = _rowwise(
        b_post, [y] + rkvg + [dya], post_consts, [row(RW), row(4 * RW, MXU), acc(RW), acc(RW), acc(RW)], tm=tm_wide,
        name="b_post")
    dscan, *got = _scan_bwd(rw, ck, inv, dy, name="b_scan",
                            hosted=shards and _ScatterChips([gw[n] for n in BACK_SCAN]))
    recv.update(zip(BACK_SCAN, got))

    def b_prep(i, n, cur, drw_, dscan_, prev8, mu, *params):
        shifted = _shift_down(cur, prev8, i)
        mixed = cur + (shifted - cur) * mu
        _, vjp = jax.vjp(_prep, mixed, *params)
        blk = lambda t, j: t[:, j * RW:(j + 1) * RW].astype(F32)
        ct = jnp.concatenate([blk(dscan_, 0) + blk(drw_, 0), blk(dscan_, 1), blk(dscan_, 2) + blk(drw_, 1),
                              blk(dscan_, 3) + blk(drw_, 2), blk(dscan_, 4), blk(dscan_, 5), blk(drw_, 3)], axis=1)
        grads = vjp(ct)
        dmixed = grads[0]
        return (dmixed, jnp.sum(dmixed * (shifted - cur), axis=0, keepdims=True)) + tuple(grads[1:])
    dmixed, d_mu, d_w0, d_lora, d_a0, d_gate_up, d_kk, d_ka = _rowwise(
        b_prep, [(proj, SHIFT_W), drkvg, dscan], [s['tshift_mu']] + prep_consts,
        [row(SHIFT_W), acc(SHIFT_W), acc(RW), acc(2 * RW, 128), acc(RW), acc(RW, 128), acc(RW), acc(RW)],
        tm=tm_vjp, name="b_prep", halo=[(proj, SHIFT_W, 'prev')])

    def b_gather(i, n, dm, dqkv_, dpg_, next8, mu):
        dcur = dm * (1.0 - mu) + _shift_up(dm, next8, i, n) * mu
        return (jnp.concatenate([dcur.astype(MXU), dqkv_, dpg_], axis=1),)
    (dproj,) = _rowwise(b_gather, [dmixed, dqkv, dpg], [s['tshift_mu']], [row(IN_W, MXU)], tm=tm_wide, name="b_gather",
                        halo=[(dmixed, SHIFT_W, 'next')])
    g_win = _mm_tn(h1, dproj, name="b_proj_dw", out_dtype=MXU, col_shards=4)

    def col_blocks(g):
        k, n = g.shape
        return g.reshape(k, 4, n // 4).transpose(1, 0, 2).astype(MXU)
    gw.update(w_in=g_win, decay_up=col_blocks(d_lora[:64, :RW]), iclr_up=col_blocks(d_lora[64:, RW:]),
              gate_up=col_blocks(d_gate_up))
    top, bottom = None, None
    if shards:
        top = _ScatterChips([gw['w_in'][:, :D // 2]] + [gw[n] for n in BACK_LAST[1:]])
        bottom = _ScatterChips([gw['w_in'][:, D // 2:3 * D // 4]])
        dh1, *got_top = _mm_nt(dproj, w['w_in'], name="b_proj_dx", hosted=top)
    else:
        dh1 = _mm_nt(dproj, w['w_in'], name="b_proj_dx")

    def b_norm1(i, n, x_, dh1_, dx1_, g, ada_):
        _, vjp = jax.vjp(_norm_mod, x_, g, ada_[:, D:2 * D], ada_[:, 0:D])
        dxn, dg, dsc, dsh = vjp(dh1_)
        return dxn + dx1_, dg, dsc, dsh
    dx, d_gain1, d_scale1, d_shift1, *got_bottom = _rowwise(
        b_norm1, [x, dh1, dx1], [s['norm1_gain'], ada], [row(D), acc(D), acc(D), acc(D)], tm=tm, name="b_norm1",
        hosted=bottom)
    if shards:
        recv.update(zip(BACK_LAST[1:], got_top[1:]))
        recv['w_in'] = [got_top[0], got_bottom[0]]

    d_ada = jnp.concatenate([d_shift1, d_scale1, dgate1, d_shift2, d_scale2, dgate2], axis=1)
    gs = dict(norm1_gain=d_gain1, norm2_gain=d_gain2, tshift_mu=d_mu, decay_w0=d_w0, iclr_a0=d_a0, k_k=d_kk, k_a=d_ka,
              r_k=d_r_k, lnx_gain=d_lnx_gain, lnx_bias=d_lnx_bias, q_norm_gain=d_qg, k_norm_gain=d_kg,
              attn_sinks=d_sinks, branch_gate_b=d_bias)
    return loss, dx, d_ada, gw, gs, recv


ANY = pl.BlockSpec(memory_space=pl.ANY)


def _place():
    x, y, c = lax.axis_index("x"), lax.axis_index("y"), lax.axis_index("c")
    return x, y, c, [(1 - x, y), (x, 1 - y), (1 - x, 1 - y)]


def _all_gather8(x_shard, *, name):
    m_per, n = x_shard.shape

    def body(x_ref, out_ref, send_sems, recv_sems, local_sem):
        x, y, c, chips = _place()
        me, sibling = (x, y, c), (x, y, 1 - c)

        def rows(px, py, pc):
            return out_ref.at[pl.ds((4 * px + 2 * py + pc) * m_per, m_per), :]

        def copy(k, block, to, src=None):
            return pltpu.make_async_remote_copy(
                src_ref=rows(*block) if src is None else src, dst_ref=rows(*block),
                send_sem=send_sems.at[k], recv_sem=recv_sems.at[k], device_id=to, device_id_type=MESH)

        mine = pltpu.make_async_copy(x_ref, rows(*me), local_sem)
        mine.start()
        first = [copy(0, me, sibling, src=x_ref)]
        first += [copy(1 + j, me, (*chip, c), src=x_ref) for j, chip in enumerate(chips)]
        for cp in first:
            cp.start()
        passed = [copy(4 + j, (*chip, c), sibling) for j, chip in enumerate(chips)]
        for j, chip in enumerate(chips):
            copy(1 + j, (*chip, c), me).wait_recv()
            passed[j].start()
        copy(0, sibling, me).wait_recv()
        for j, chip in enumerate(chips):
            copy(4 + j, (*chip, 1 - c), me).wait_recv()
        for cp in first + passed:
            cp.wait_send()
        mine.wait()

    return pl.pallas_call(
        body, name=name, out_shape=jax.ShapeDtypeStruct((8 * m_per, n), x_shard.dtype),
        in_specs=[pl.BlockSpec(memory_space=pltpu.VMEM)], out_specs=pl.BlockSpec(memory_space=pltpu.VMEM),
        scratch_shapes=[pltpu.SemaphoreType.DMA((7,)), pltpu.SemaphoreType.DMA((7,)), pltpu.SemaphoreType.DMA],
    )(x_shard)


class _GatherChips:
    def __init__(self, shards):
        n = len(shards)
        self.arrays, self.n_in, self.n_out = list(shards), n, n
        self.out_shape = [jax.ShapeDtypeStruct((4,) + s.shape, s.dtype) for s in shards]
        self.scratch = [pltpu.SemaphoreType.DMA((3 * n,)), pltpu.SemaphoreType.DMA((3 * n,)),
                        pltpu.SemaphoreType.DMA((n,))]

    def _copies(self, x_refs, out_refs, sems, receiving):
        send_sems, recv_sems, local_sems = sems
        x, y, c, chips = _place()
        s_me = 2 * x + y
        n = self.n_in

        def copy(a, k, s):
            return pltpu.make_async_remote_copy(
                src_ref=x_refs[a], dst_ref=out_refs[a].at[s], send_sem=send_sems.at[3 * a + k],
                recv_sem=recv_sems.at[3 * a + k], device_id=(*chips[k], c), device_id_type=MESH)

        mine = [pltpu.make_async_copy(x_refs[a], out_refs[a].at[s_me], local_sems.at[a]) for a in range(n)]
        sends = [copy(a, k, s_me) for a in range(n) for k in range(3)]
        if not receiving:
            return mine, sends
        return mine, sends, [copy(a, k, 2 * px + py) for a in range(n) for k, (px, py) in enumerate(chips)]

    def start(self, x_refs, out_refs, sems):
        mine, sends = self._copies(x_refs, out_refs, sems, False)
        for cp in mine + sends:
            cp.start()

    def wait(self, x_refs, out_refs, sems):
        mine, sends, recvs = self._copies(x_refs, out_refs, sems, True)
        for cp in recvs:
            cp.wait_recv()
        for cp in sends:
            cp.wait_send()
        for cp in mine:
            cp.wait()


class _GatherChipsHalved(_GatherChips):
    def __init__(self, shards):
        super().__init__(shards)
        n = self.n_in
        self.scratch = [pltpu.SemaphoreType.DMA((6 * n,)), pltpu.SemaphoreType.DMA((6 * n,)),
                        pltpu.SemaphoreType.DMA((n,))]

    def _copies(self, x_refs, out_refs, sems, receiving):
        send_sems, recv_sems, local_sems = sems
        x, y, c, chips = _place()
        s_me = 2 * x + y
        n = self.n_in

        def half(a, who):
            rows = x_refs[a].shape[0] // 2
            return pl.ds(who * rows, rows)

        def over_chips(a, k, s):
            return pltpu.make_async_remote_copy(
                src_ref=x_refs[a].at[half(a, c)], dst_ref=out_refs[a].at[s, half(a, c)],
                send_sem=send_sems.at[3 * a + k], recv_sem=recv_sems.at[3 * a + k],
                device_id=(*chips[k], c), device_id_type=MESH)

        def to_sibling(a, k, s, who):
            return pltpu.make_async_remote_copy(
                src_ref=out_refs[a].at[s, half(a, who)], dst_ref=out_refs[a].at[s, half(a, who)],
                send_sem=send_sems.at[3 * n + 3 * a + k], recv_sem=recv_sems.at[3 * n + 3 * a + k],
                device_id=(x, y, 1 - c), device_id_type=MESH)

        mine = [pltpu.make_async_copy(x_refs[a], out_refs[a].at[s_me], local_sems.at[a]) for a in range(n)]
        sends = [over_chips(a, k, s_me) for a in range(n) for k in range(3)]
        if not receiving:
            return mine, sends
        pairs = [(a, k, 2 * px + py) for a in range(n) for k, (px, py) in enumerate(chips)]
        landed = [over_chips(a, k, s) for a, k, s in pairs]
        passed_on = [to_sibling(a, k, s, c) for a, k, s in pairs]
        from_sibling = [to_sibling(a, k, s, 1 - c) for a, k, s in pairs]
        return mine, sends, landed, passed_on, from_sibling

    def wait(self, x_refs, out_refs, sems):
        mine, sends, landed, passed_on, from_sibling = self._copies(x_refs, out_refs, sems, True)
        for got, fwd in zip(landed, passed_on, strict=True):
            got.wait_recv()
            fwd.start()
        for cp in from_sibling:
            cp.wait_recv()
        for cp in sends + passed_on:
            cp.wait_send()
        for cp in mine:
            cp.wait()


class _ScatterChips:
    def __init__(self, parts):
        n = len(parts)
        self.arrays, self.n_in, self.n_out = list(parts), n, n
        self.out_shape = [jax.ShapeDtypeStruct((3,) + p.shape[1:], p.dtype) for p in parts]
        self.scratch = [pltpu.SemaphoreType.DMA((3 * n,)), pltpu.SemaphoreType.DMA((3 * n,))]

    def _copies(self, g_refs, out_refs, sems):
        send_sems, recv_sems = sems
        x, y, c, chips = _place()
        return [pltpu.make_async_remote_copy(
            src_ref=g_refs[a].at[2 * px + py], dst_ref=out_refs[a].at[k], send_sem=send_sems.at[3 * a + k],
            recv_sem=recv_sems.at[3 * a + k], device_id=(px, py, c), device_id_type=MESH)
            for a in range(self.n_in) for k, (px, py) in enumerate(chips)]

    def start(self, g_refs, out_refs, sems):
        for cp in self._copies(g_refs, out_refs, sems):
            cp.start()

    def wait(self, g_refs, out_refs, sems):
        sends = self._copies(g_refs, out_refs, sems)
        for cp in sends:
            cp.wait_recv()
        for cp in sends:
            cp.wait_send()


def _exchange_call(ex, *, name):
    def body(*refs):
        parts = (refs[:ex.n_in], refs[ex.n_in:ex.n_in + ex.n_out], refs[ex.n_in + ex.n_out:])
        ex.start(*parts)
        ex.wait(*parts)

    return pl.pallas_call(body, name=name, out_shape=ex.out_shape, in_specs=[ANY] * ex.n_in,
                          out_specs=[ANY] * ex.n_out, scratch_shapes=ex.scratch)(*ex.arrays)


class _SwapSibling:
    def __init__(self, vs):
        n = len(vs)
        self.arrays, self.n_in, self.n_out = list(vs), n, n
        self.out_shape = [jax.ShapeDtypeStruct(v.shape, v.dtype) for v in vs]
        self.scratch = [pltpu.SemaphoreType.DMA((n,)), pltpu.SemaphoreType.DMA((n,))]

    def _copies(self, v_refs, out_refs, sems):
        send_sems, recv_sems = sems
        x, y, c, _ = _place()
        return [pltpu.make_async_remote_copy(src_ref=v_refs[a], dst_ref=out_refs[a], send_sem=send_sems.at[a],
                                             recv_sem=recv_sems.at[a], device_id=(x, y, 1 - c), device_id_type=MESH)
                for a in range(self.n_in)]

    def start(self, v_refs, out_refs, sems):
        for cp in self._copies(v_refs, out_refs, sems):
            cp.start()

    def wait(self, v_refs, out_refs, sems):
        for cp in self._copies(v_refs, out_refs, sems):
            cp.wait()


class _Both:
    def __init__(self, first, second):
        self.parts = (first, second)
        self.arrays = first.arrays + second.arrays
        self.n_in, self.n_out = first.n_in + second.n_in, first.n_out + second.n_out
        self.out_shape = first.out_shape + second.out_shape
        self.scratch = first.scratch + second.scratch

    def _split(self, in_refs, out_refs, sems):
        a, b = self.parts
        return ((a, in_refs[:a.n_in], out_refs[:a.n_out], sems[:len(a.scratch)]),
                (b, in_refs[a.n_in:], out_refs[a.n_out:], sems[len(a.scratch):]))

    def start(self, in_refs, out_refs, sems):
        for ex, *refs in self._split(in_refs, out_refs, sems):
            ex.start(*refs)

    def wait(self, in_refs, out_refs, sems):
        for ex, *refs in self._split(in_refs, out_refs, sems):
            ex.wait(*refs)


def _sum_parts(own, others, *, name):
    R, C = own.shape
    tm = _pick(R, (256, 128, 64))

    def body(own_ref, o0_ref, o1_ref, o2_ref, out_ref):
        tot = own_ref[...].astype(F32)
        for ref in (o0_ref, o1_ref, o2_ref):
            tot = tot + ref[...].astype(F32)
        out_ref[...] = tot

    part = lambda k: pl.BlockSpec((None, tm, C), lambda i: (k, i, 0))
    return pl.pallas_call(
        body, name=name, grid=(R // tm,),
        in_specs=[pl.BlockSpec((tm, C), lambda i: (i, 0)), part(0), part(1), part(2)],
        out_specs=pl.BlockSpec((tm, C), lambda i: (i, 0)), out_shape=jax.ShapeDtypeStruct((R, C), F32),
        compiler_params=_cparams(("arbitrary",)),
    )(own, others, others, others)


def _adam_math(w_, m_, v_, g):
    m2 = ADAM_B1 * m_ + (1.0 - ADAM_B1) * g
    v2 = ADAM_B2 * v_ + (1.0 - ADAM_B2) * jnp.square(g)
    m_hat = m2 / (1.0 - ADAM_B1 ** ADAM_STEP)
    v_hat = v2 / (1.0 - ADAM_B2 ** ADAM_STEP)
    delta = -ADAM_LR * (m_hat / (jnp.sqrt(v_hat) + ADAM_EPS) + ADAM_WD * w_)
    return delta, m2, v2


SMALL_SLOTS = 16
SMALL_COLS = 6 * D


def _pack_small(grads, *, name):
    n = len(grads)

    def body(*refs):
        out_ref = refs[n]
        out_ref[...] = jnp.zeros_like(out_ref)
        for i, ref in enumerate(refs[:n]):
            out_ref[i:i + 1, 0:ref.shape[1]] = ref[...]

    return pl.pallas_call(body, name=name, out_shape=jax.ShapeDtypeStruct((SMALL_SLOTS, SMALL_COLS), F32))(*grads)


def _adamw_small(ws, ms, vs, gathered, *, name):
    n = len(ws)

    def body(*refs):
        w_refs, m_refs, v_refs, g_ref = refs[:n], refs[n:2 * n], refs[2 * n:3 * n], refs[3 * n]
        outs = refs[3 * n + 1:]
        for i in range(n):
            nc = w_refs[i].shape[1]
            g = g_ref[i:i + 1, 0:nc]
            for d in range(1, 8):
                g = g + g_ref[d * SMALL_SLOTS + i:d * SMALL_SLOTS + i + 1, 0:nc]
            delta, m2, v2 = _adam_math(w_refs[i][...], m_refs[i][...], v_refs[i][...], g)
            for k, val in enumerate((g, delta, m2, v2)):
                outs[k * n + i][...] = val

    shapes = [jax.ShapeDtypeStruct(w.shape, F32) for w in ws]
    res = pl.pallas_call(body, name=name, out_shape=shapes * 4,
                         compiler_params=pltpu.CompilerParams(vmem_limit_bytes=VMEM_LIMIT))(*ws, *ms, *vs, gathered)
    return [res[k * n:(k + 1) * n] for k in range(4)]


def _adamw(w, m, v, gparts, *, tm, name, hosted=None):
    def fn(i, n, w_, m_, v_, *gs):
        g = gs[0]
        for p in gs[1:]:
            g = g + p
        return (g,) + _adam_math(w_, m_, v_, g)
    nc = w.shape[1]
    return _rowwise(fn, [w, m, v] + list(gparts), [], [(nc, F32, 'row')] * 4, tm=tm, name=name, hosted=hosted)


WEIGHTS = ['ada_w', 'ada_b', 'norm1_gain', 'norm2_gain', 'w_in', 'tshift_mu', 'decay_w0', 'decay_up', 'iclr_a0',
           'iclr_up', 'gate_up', 'k_k', 'k_a', 'r_k', 'lnx_gain', 'lnx_bias', 'q_norm_gain', 'k_norm_gain', 'attn_sinks',
           'branch_gate_b', 'w_branch_a', 'w_branch_b', 'w_out', 'ffn_w1', 'ffn_w3', 'ffn_w2']
SHARDED = [('w_in', 1), ('decay_up', 1), ('iclr_up', 1), ('gate_up', 1), ('w_branch_a', 1), ('w_branch_b', 1),
           ('w_out', 0), ('ffn_w1', 1), ('ffn_w3', 1), ('ffn_w2', 0)]
SMALL = ['ada_b', 'norm1_gain', 'norm2_gain', 'tshift_mu', 'decay_w0', 'iclr_a0', 'k_k', 'k_a', 'r_k', 'lnx_gain',
         'lnx_bias', 'q_norm_gain', 'k_norm_gain', 'attn_sinks', 'branch_gate_b']


def kernel(x, c, positions, ada_w, ada_b, norm1_gain, norm2_gain, w_in, tshift_mu, decay_w0, decay_up, iclr_a0, iclr_up, gate_up, k_k, k_a, r_k, lnx_gain, lnx_bias, q_norm_gain, k_norm_gain, attn_sinks, branch_gate_b, w_branch_a, w_branch_b, w_out, ffn_w1, ffn_w3, ffn_w2, loss_target, m_ada_w, m_ada_b, m_norm1_gain, m_norm2_gain, m_w_in, m_tshift_mu, m_decay_w0, m_decay_up, m_iclr_a0, m_iclr_up, m_gate_up, m_k_k, m_k_a, m_r_k, m_lnx_gain, m_lnx_bias, m_q_norm_gain, m_k_norm_gain, m_attn_sinks, m_branch_gate_b, m_w_branch_a, m_w_branch_b, m_w_out, m_ffn_w1, m_ffn_w3, m_ffn_w2, v_ada_w, v_ada_b, v_norm1_gain, v_norm2_gain, v_w_in, v_tshift_mu, v_decay_w0, v_decay_up, v_iclr_a0, v_iclr_up, v_gate_up, v_k_k, v_k_a, v_r_k, v_lnx_gain, v_lnx_bias, v_q_norm_gain, v_k_norm_gain, v_attn_sinks, v_branch_gate_b, v_w_branch_a, v_w_branch_b, v_w_out, v_ffn_w1, v_ffn_w3, v_ffn_w2):
    a = dict(locals())
    W = {n: a[n] for n in WEIGHTS}
    M = {n: a['m_' + n] for n in WEIGHTS}
    V = {n: a['v_' + n] for n in WEIGHTS}
    xi, yi, ci = lax.axis_index("x"), lax.axis_index("y"), lax.axis_index("c")
    me = 4 * xi + 2 * yi + ci
    shard = 2 * xi + yi
    mat = lambda t: t.reshape(t.shape[-2], t.shape[-1])
    sharded = [n for n, _ in SHARDED]

    ax = dict(SHARDED)
    late = LATE
    early = [n for n in sharded if n not in late]
    shards = {n: mat(W[n]).astype(MXU) for n in sharded}
    gathered = _exchange_call(_GatherChipsHalved([shards[n] for n in early]), name="gather_weights")
    full = {n: _full_weight(g, ax[n]) for n, g in zip(early, gathered, strict=True)}

    c_all = _all_gather8(jnp.broadcast_to(c, (8, D)), name="gather_c")[0::8]
    pad_rows = lambda t: jnp.concatenate([t, jnp.zeros((BLK - 8, t.shape[1]), t.dtype)])
    c_all = pad_rows(c_all.astype(MXU))
    ada_cols = _mm_nn(c_all, mat(ada_w).astype(MXU), name="f_ada")[:8]
    ada_all = _all_gather8(ada_cols, name="gather_ada").reshape(2, 2, 2, 8, 6 * D // 4)
    ada_mine = lax.dynamic_index_in_dim(ada_all[:, :, 0], me, axis=2, keepdims=False)
    ada = ada_mine.reshape(1, 6 * D) + mat(ada_b)

    zero = jnp.zeros((64, RW), MXU)
    lora = jnp.concatenate([jnp.concatenate([full['decay_up'], zero], axis=1),
                            jnp.concatenate([zero, full['iclr_up']], axis=1)], axis=0)
    s = {n: W[n].reshape(1, -1) for n in SMALL if n != 'ada_b'}
    s['lora_up'] = lora.astype(F32)
    s['gate_up'] = full['gate_up'].astype(F32)
    tab = _rope_table(positions.reshape(-1))
    loss, dx, d_ada, gw, gs, from_chips = _local_step(x[0], loss_target[0], ada, tab, dict(w_in=full['w_in']), s,
                                                      shards={n: shards[n] for n in late})
    loss = lax.psum(loss[0, 0], ("x", "y", "c"))

    gs['ada_b'] = d_ada
    gsmall = _pack_small([gs[n] for n in SMALL], name="pack_small_grads")
    gsmall_all = _all_gather8(gsmall, name="gather_small_grads")
    row = lambda src: [src[n].reshape(1, -1) for n in SMALL]
    sm_out = _adamw_small(row(W), row(M), row(V), gsmall_all, name="adamw_small")
    sm_out = [{n: o.reshape(W[n].shape) for n, o in zip(SMALL, outs_k, strict=True)} for outs_k in sm_out]

    d_ada_all = gsmall_all[0::SMALL_SLOTS]
    d_ada_cols = lax.dynamic_slice_in_dim(d_ada_all, shard * (6 * D // 4), 6 * D // 4, axis=1)
    g_ada_w = _mm_tn(c_all, pad_rows(d_ada_cols.astype(MXU)), name="b_ada")

    rest = [n for n in sharded if n != 'w_in']
    parts = {n: _sum_parts(lax.dynamic_index_in_dim(gw[n], shard, axis=0, keepdims=False), from_chips[n],
                           name="sum_" + n) for n in rest}
    tail = _Both(_ScatterChips([gw['w_in'][:, 3 * D // 4:]]), _SwapSibling([parts[n] for n in rest]))
    res = _adamw(mat(ada_w), mat(m_ada_w), mat(v_ada_w), [g_ada_w], tm=256, name="adamw_ada", hosted=tail)
    ada_out, last_quarter, others = res[:4], res[4], dict(zip(rest, res[5:], strict=True))
    parts['w_in'] = _sum_parts(lax.dynamic_index_in_dim(gw['w_in'], shard, axis=0, keepdims=False),
                               jnp.concatenate(from_chips['w_in'] + [last_quarter], axis=1), name="sum_w_in")
    others['w_in'] = _exchange_call(_SwapSibling([parts['w_in']]), name="swap_w_in")[0]
    sh_out = {}
    for n in sharded:
        part, other = parts[n], others[n]
        sh_out[n] = _adamw(mat(W[n]), mat(M[n]), mat(V[n]), [part, other], tm=_pick(part.shape[0], (256, 128, 64)),
                           name="adamw_" + n)

    def leaf(k, n):
        if n == 'ada_w':
            return ada_out[k].reshape(W[n].shape)
        if n in sharded:
            return sh_out[n][k].reshape(W[n].shape)
        return sm_out[k][n]
    outs = [leaf(k, n) for k in range(4) for n in WEIGHTS]
    return (loss, dx[None], *outs)
```

```python
import functools
import math

import jax
import jax.numpy as jnp
from jax import lax
from jax.experimental import pallas as pl
from jax.experimental.pallas import tpu as pltpu

F32 = jnp.float32
BF16 = jnp.bfloat16
MXU = BF16
HI = lax.Precision.HIGHEST

D = 1024
HD = 64
NH = 8
RW = NH * HD
SHIFT_W = 3 * RW + 64 + 64 + 128
QKV_W = RW + 2 * 128
GATE_W = 2 * D
IN_W = SHIFT_W + QKV_W + GATE_W
DFF = 2816
BLK = 128
CHUNK = 64
RMS_EPS = 1e-6
GN_EPS = 64e-5
NEG_INF = -1e30
ADAM_LR, ADAM_B1, ADAM_B2, ADAM_EPS, ADAM_WD, ADAM_STEP = 0.001, 0.9, 0.999, 1e-08, 0.01, 10
VMEM_LIMIT = 56 * 1024 * 1024
MESH = pl.DeviceIdType.MESH


def _cparams(sem=None):
    return pltpu.CompilerParams(dimension_semantics=sem, vmem_limit_bytes=VMEM_LIMIT)


def _full_spec(a):
    nd = a.ndim
    return pl.BlockSpec(a.shape, lambda *_: (0,) * nd)


def _rowwise(fn, rows, consts, outs, *, tm, name, halo=(), hosted=None):
    rows = [(a + (0,))[:3] if isinstance(a, tuple) else (a, a.shape[1], 0) for a in rows]
    T = rows[0][0].shape[0]
    assert T % tm == 0 and tm % 8 == 0
    n_tiles = T // tm
    n_in = len(rows) + len(halo) + len(consts)
    in_specs = [pl.BlockSpec((tm, nc), lambda i, j=j: (i, j)) for _, nc, j in rows]
    args = [a for a, _, _ in rows]
    for a, nc, kind in halo:
        if kind == 'prev':
            in_specs.append(pl.BlockSpec((8, nc), lambda i: (jnp.maximum(i * (tm // 8) - 1, 0), 0)))
        else:
            in_specs.append(pl.BlockSpec((8, nc), lambda i: (jnp.minimum((i + 1) * (tm // 8), T // 8 - 1), 0)))
        args.append(a)
    in_specs += [_full_spec(a) for a in consts]
    args += list(consts)
    out_shape, out_specs = [], []
    for ncols, dtype, kind in outs:
        if kind == 'row':
            out_shape.append(jax.ShapeDtypeStruct((T, ncols), dtype))
            out_specs.append(pl.BlockSpec((tm, ncols), lambda i: (i, 0)))
        else:
            out_shape.append(jax.ShapeDtypeStruct((kind, ncols), dtype))
            out_specs.append(pl.BlockSpec((kind, ncols), lambda i: (0, 0)))

    def body(*refs):
        i = pl.program_id(0)
        vals = [r[...] for r in refs[:n_in]]
        res = fn(i, n_tiles, *vals)
        for (ncols, dtype, kind), o_ref, val in zip(outs, refs[n_in:], res, strict=True):
            if kind == 'row':
                o_ref[...] = val.astype(dtype)
            else:
                @pl.when(i == 0)
                def _():
                    o_ref[...] = jnp.zeros_like(o_ref)
                o_ref[...] += val.astype(dtype)

    h_in, h_in_specs, h_out_specs, h_out_shape, h_scratch = _hosted_args(hosted)
    res = pl.pallas_call(
        _hosting(body, hosted, n_in, len(outs), 0, n_tiles), name=name, grid=(n_tiles,),
        in_specs=in_specs + h_in_specs, out_specs=out_specs + h_out_specs, out_shape=out_shape + h_out_shape,
        scratch_shapes=h_scratch, compiler_params=_cparams(("arbitrary",)),
    )(*args, *h_in)
    return res


def _pick(n, cands):
    for c in cands:
        if n % c == 0:
            return c
    return n


MM_ROWS = (1024, 512, 256, 128)
MM_COLS = (1536, 1408, 1024, 896, 768, 512, 256, 128)
MM_WIDE = 3000


def _mm_nn(a, w, *, name, out_dtype=F32, hosted=None):
    T, K = a.shape
    N = w.shape[1]
    tm = _pick(T, MM_ROWS)
    tn = _pick(N, MM_COLS)
    grid = (N // tn, T // tm)
    h_in, h_in_specs, h_out_specs, h_out_shape, h_scratch = _hosted_args(hosted)

    def body(a_ref, w_ref, o_ref):
        o_ref[...] = jnp.dot(a_ref[...], w_ref[...], preferred_element_type=F32).astype(out_dtype)

    res = pl.pallas_call(
        _hosting(body, hosted, 2, 1, 0, grid), name=name, grid=grid,
        in_specs=[pl.BlockSpec((tm, K), lambda j, i: (i, 0)), pl.BlockSpec((K, tn), lambda j, i: (0, j))] + h_in_specs,
        out_specs=[pl.BlockSpec((tm, tn), lambda j, i: (i, j))] + h_out_specs,
        out_shape=[jax.ShapeDtypeStruct((T, N), out_dtype)] + h_out_shape, scratch_shapes=h_scratch,
        compiler_params=_cparams(("arbitrary", "arbitrary")),
    )(a, w, *h_in)
    return res if hosted else res[0]


def _mm_nt(dy, w, *, name, out_dtype=F32, hosted=None):
    T, N = dy.shape
    K = w.shape[0]
    tm = _pick(T, MM_ROWS if N <= MM_WIDE else MM_ROWS[1:])
    tk = _pick(K, MM_COLS[1:])
    grid = (K // tk, T // tm)
    h_in, h_in_specs, h_out_specs, h_out_shape, h_scratch = _hosted_args(hosted)

    def body(dy_ref, w_ref, o_ref):
        o_ref[...] = lax.dot_general(dy_ref[...], w_ref[...], (((1,), (1,)), ((), ())),
                                     preferred_element_type=F32).astype(out_dtype)

    res = pl.pallas_call(
        _hosting(body, hosted, 2, 1, 0, grid), name=name, grid=grid,
        in_specs=[pl.BlockSpec((tm, N), lambda j, i: (i, 0)), pl.BlockSpec((tk, N), lambda j, i: (j, 0))] + h_in_specs,
        out_specs=[pl.BlockSpec((tm, tk), lambda j, i: (i, j))] + h_out_specs,
        out_shape=[jax.ShapeDtypeStruct((T, K), out_dtype)] + h_out_shape, scratch_shapes=h_scratch,
        compiler_params=_cparams(("arbitrary", "arbitrary")),
    )(dy, w, *h_in)
    return res if hosted else res[0]


def _mm_tn(a, dy, *, name, out_dtype=F32, col_shards=None):
    T, K = a.shape
    N = dy.shape[1]
    tm = _pick(T, MM_ROWS)
    tn = N // col_shards if col_shards else _pick(N, MM_COLS[1:])
    n_t = T // tm

    def body(a_ref, dy_ref, o_ref, acc_ref):
        i = pl.program_id(1)

        @pl.when(i == 0)
        def _():
            acc_ref[...] = jnp.zeros_like(acc_ref)

        acc_ref[...] += lax.dot_general(a_ref[...], dy_ref[...], (((0,), (0,)), ((), ())), preferred_element_type=F32)

        @pl.when(i == n_t - 1)
        def _():
            o_ref[...] = acc_ref[...].astype(out_dtype)

    if col_shards:
        out_specs = pl.BlockSpec((None, K, tn), lambda j, i: (j, 0, 0))
        out_shape = jax.ShapeDtypeStruct((col_shards, K, tn), out_dtype)
    else:
        out_specs = pl.BlockSpec((K, tn), lambda j, i: (0, j))
        out_shape = jax.ShapeDtypeStruct((K, N), out_dtype)
    return pl.pallas_call(
        body, name=name, grid=(N // tn, n_t),
        in_specs=[pl.BlockSpec((tm, K), lambda j, i: (i, 0)), pl.BlockSpec((tm, tn), lambda j, i: (i, j))],
        out_specs=out_specs, out_shape=out_shape, scratch_shapes=[pltpu.VMEM((K, tn), F32)],
        compiler_params=_cparams(("arbitrary", "arbitrary")),
    )(a, dy)


def _mm_then(products, fn, rows, consts, outs, *, tm, name):
    T = products[0][0].shape[0]
    n_tiles = T // tm
    n_p, n_r = len(products), len(rows)
    in_specs, args = [], []
    for a, w, _ in products:
        in_specs += [pl.BlockSpec((tm, a.shape[1]), lambda i: (i, 0)), _full_spec(w)]
        args += [a, w]
    in_specs += [pl.BlockSpec((tm, a.shape[1]), lambda i: (i, 0)) for a in rows] + [_full_spec(c_) for c_ in consts]
    args += list(rows) + list(consts)
    n_in = len(args)
    out_shape, out_specs = [], []
    for ncols, dtype, kind in outs:
        if kind == 'row':
            out_shape.append(jax.ShapeDtypeStruct((T, ncols), dtype))
            out_specs.append(pl.BlockSpec((tm, ncols), lambda i: (i, 0)))
        else:
            out_shape.append(jax.ShapeDtypeStruct((kind, ncols), dtype))
            out_specs.append(pl.BlockSpec((kind, ncols), lambda i: (0, 0)))

    def body(*refs):
        i = pl.program_id(0)
        y = None
        for k, (_, _, form) in enumerate(products):
            a_ref, w_ref = refs[2 * k], refs[2 * k + 1]
            dims = (((1,), (0,)), ((), ())) if form == 'nn' else (((1,), (1,)), ((), ()))
            t = lax.dot_general(a_ref[...], w_ref[...], dims, preferred_element_type=F32)
            y = t if y is None else y + t
        res = fn(i, n_tiles, y, *[r[...] for r in refs[2 * n_p:n_in]])
        for (ncols, dtype, kind), o_ref, val in zip(outs, refs[n_in:], res, strict=True):
            if kind == 'row':
                o_ref[...] = val.astype(dtype)
            else:
                @pl.when(i == 0)
                def _():
                    o_ref[...] = jnp.zeros_like(o_ref)
                o_ref[...] += val.astype(dtype)

    return pl.pallas_call(body, name=name, grid=(n_tiles,), in_specs=in_specs, out_specs=out_specs,
                          out_shape=out_shape, compiler_params=_cparams(("arbitrary",)))(*args)


def _seg_ones(n):
    r = lax.broadcasted_iota(jnp.int32, (n, n), 0) // HD
    c = lax.broadcasted_iota(jnp.int32, (n, n), 1) // HD
    return (r == c).astype(F32)


def _segsum_raw(x):
    ones = _seg_ones(x.shape[1])
    if MXU == F32:
        return jnp.dot(x, ones, precision=HI, preferred_element_type=F32)
    hi = x.astype(MXU)
    lo = (x - hi.astype(F32)).astype(MXU)
    ones = ones.astype(MXU)
    return jnp.dot(hi, ones, preferred_element_type=F32) + jnp.dot(lo, ones, preferred_element_type=F32)


@jax.custom_vjp
def _segsum(x):
    return _segsum_raw(x)


def _segsum_fwd(x):
    return _segsum_raw(x), None


def _segsum_bwd(_, g):
    return (_segsum_raw(g),)


_segsum.defvjp(_segsum_fwd, _segsum_bwd)


def _mxu(x):
    return x.astype(MXU)


@jax.custom_vjp
def _bdot(a, b):
    return jnp.dot(_mxu(a), _mxu(b), preferred_element_type=F32)


def _bdot_fwd(a, b):
    return _bdot(a, b), (a, b)


def _bdot_bwd(res, g):
    a, b = res
    da = lax.dot_general(_mxu(g), _mxu(b), (((1,), (1,)), ((), ())), preferred_element_type=F32)
    db = lax.dot_general(_mxu(a), _mxu(g), (((0,), (0,)), ((), ())), preferred_element_type=F32)
    return da.astype(a.dtype), db.astype(b.dtype)


_bdot.defvjp(_bdot_fwd, _bdot_bwd)


def _sigmoid(x):
    return 1.0 / (1.0 + jnp.exp(-x))


def _softplus(x):
    return jnp.maximum(x, 0.0) + jnp.log(1.0 + jnp.exp(jnp.minimum(x, -x)))


def _norm_mod(x, gain, scale, shift):
    inv = lax.rsqrt(jnp.mean(x * x, axis=-1, keepdims=True) + RMS_EPS)
    return (x * inv) * gain * (1.0 + scale) + shift


def _prep(mixed, decay_w0, lora_up, iclr_a0, gate_up, k_k, k_a):
    r = mixed[:, 0:RW]
    k = mixed[:, RW:2 * RW]
    v = mixed[:, 2 * RW:3 * RW]
    z = mixed[:, 3 * RW:3 * RW + 128]
    xg = mixed[:, 3 * RW + 128:]
    lane = lax.broadcasted_iota(jnp.int32, z.shape, 1)
    tz = jnp.where(lane < 64, jnp.tanh(z), z)
    lo = _bdot(tz, lora_up)
    w_log = -_softplus(-(decay_w0 + lo[:, :RW])) - 0.5
    lw = -jnp.exp(w_log)
    a_ic = _sigmoid(iclr_a0 + lo[:, RW:])
    g = _bdot(_sigmoid(xg), gate_up)
    kk = k * k_k
    kk = kk / jnp.maximum(jnp.sqrt(_segsum(kk * kk)), 1e-12)
    k_mod = k * (1.0 + (a_ic - 1.0) * k_a)
    return jnp.concatenate([r, lw, k_mod, v, -kk, kk * a_ic, g], axis=1)


def _post(y, r, k, v, g, lnx_gain, lnx_bias, r_k):
    mu = _segsum(y) * (1.0 / HD)
    yc = y - mu
    var = _segsum(yc * yc) * (1.0 / HD)
    yn = yc * lax.rsqrt(var + GN_EPS) * lnx_gain + lnx_bias
    bonus = _segsum(r * k * r_k) * v
    return (yn + bonus) * g


def _merge(pg, ma, mb, bias):
    gates = _sigmoid(pg + bias)
    return gates[:, :D] * ma + gates[:, D:] * mb


def _swiglu(u, v):
    return u * _sigmoid(u) * v


def _ffn_in(h, w1, w3, *, name):
    T, K = h.shape
    F = w1.shape[1]
    tm = _pick(T, MM_ROWS)
    tn = _pick(F, MM_COLS[1:])

    def body(h_ref, w1_ref, w3_ref, u_ref, v_ref, a_ref):
        u = jnp.dot(h_ref[...], w1_ref[...], preferred_element_type=F32).astype(MXU)
        v = jnp.dot(h_ref[...], w3_ref[...], preferred_element_type=F32).astype(MXU)
        u_ref[...] = u
        v_ref[...] = v
        a_ref[...] = _swiglu(u.astype(F32), v.astype(F32)).astype(MXU)

    wspec = pl.BlockSpec((K, tn), lambda j, i: (0, j))
    ospec = pl.BlockSpec((tm, tn), lambda j, i: (i, j))
    return pl.pallas_call(
        body, name=name, grid=(F // tn, T // tm),
        in_specs=[pl.BlockSpec((tm, K), lambda j, i: (i, 0)), wspec, wspec],
        out_specs=[ospec] * 3, out_shape=[jax.ShapeDtypeStruct((T, F), MXU)] * 3,
        compiler_params=_cparams(("arbitrary", "arbitrary")),
    )(h, w1, w3)


def _ffn_act_bwd(dff, w2, u, v, *, name):
    T, N = dff.shape
    F = w2.shape[0]
    tm = _pick(T, MM_ROWS)
    tk = _pick(F, MM_COLS[1:])

    def body(dy_ref, w_ref, u_ref, v_ref, du_ref, dv_ref):
        dact = lax.dot_general(dy_ref[...], w_ref[...], (((1,), (1,)), ((), ())), preferred_element_type=F32)
        _, vjp = jax.vjp(_swiglu, u_ref[...].astype(F32), v_ref[...].astype(F32))
        du, dv = vjp(dact)
        du_ref[...] = du.astype(MXU)
        dv_ref[...] = dv.astype(MXU)

    tile = pl.BlockSpec((tm, tk), lambda j, i: (i, j))
    return pl.pallas_call(
        body, name=name, grid=(F // tk, T // tm),
        in_specs=[pl.BlockSpec((tm, N), lambda j, i: (i, 0)), pl.BlockSpec((tk, N), lambda j, i: (j, 0)), tile, tile],
        out_specs=[tile, tile], out_shape=[jax.ShapeDtypeStruct((T, F), MXU)] * 2,
        compiler_params=_cparams(("arbitrary", "arbitrary")),
    )(dff, w2, u, v)


def _mm_nt2(dy1, w1, dy2, w2, *, name):
    T, N = dy1.shape
    K = w1.shape[0]
    tm = _pick(T, MM_ROWS[1:])
    tk = _pick(K, MM_COLS[1:])

    def body(d1_ref, w1_ref, d2_ref, w2_ref, o_ref):
        nt = lambda a, b: lax.dot_general(a[...], b[...], (((1,), (1,)), ((), ())), preferred_element_type=F32)
        o_ref[...] = nt(d1_ref, w1_ref) + nt(d2_ref, w2_ref)

    dspec = pl.BlockSpec((tm, N), lambda j, i: (i, 0))
    wspec = pl.BlockSpec((tk, N), lambda j, i: (j, 0))
    return pl.pallas_call(
        body, name=name, grid=(K // tk, T // tm), in_specs=[dspec, wspec, dspec, wspec],
        out_specs=pl.BlockSpec((tm, tk), lambda j, i: (i, j)), out_shape=jax.ShapeDtypeStruct((T, K), F32),
        compiler_params=_cparams(("arbitrary", "arbitrary")),
    )(dy1, w1, dy2, w2)


@functools.partial(jax.custom_vjp, nondiff_argnums=(1,))
def _lane_roll(x, s):
    return pltpu.roll(x, s, 1)


def _lane_roll_fwd(x, s):
    return pltpu.roll(x, s, 1), None


def _lane_roll_bwd(s, _, g):
    n = g.shape[1]
    return (pltpu.roll(g, (n - s) % n, 1),)


_lane_roll.defvjp(_lane_roll_fwd, _lane_roll_bwd)


def _rope(x, cos, sin_lo, sin_hi):
    n = x.shape[1]
    return x * cos + _lane_roll(x, n - 8) * sin_lo + _lane_roll(x, 8) * sin_hi


def _head_rms(x, gain):
    return x * lax.rsqrt(_segsum(x * x) * (1.0 / HD) + RMS_EPS) * gain


def _attn_blocks(qkv_c, qkv_p, tab_c, tab_p, qg, kg, sinks, first):
    nb = qkv_c.shape[0] // BLK
    G = 4

    def tabs(tab, n):
        return [jnp.tile(tab[:, j * 128:(j + 1) * 128], (1, n // 128)) for j in range(3)]

    qg = jnp.concatenate([qg] * NH, axis=1)
    kg = jnp.concatenate([kg] * 2, axis=1)
    q = _rope(_head_rms(qkv_c[:, :RW], qg), *tabs(tab_c, RW))
    k_in = jnp.concatenate([qkv_p[:, RW:RW + 128], qkv_c[:, RW:RW + 128]], axis=0)
    k = _rope(_head_rms(k_in, kg), *tabs(jnp.concatenate([tab_p, tab_c], axis=0), 128))
    v = jnp.concatenate([qkv_p[:, RW + 128:], qkv_c[:, RW + 128:]], axis=0)

    pile = lambda xs: jnp.concatenate([x_[None] for x_ in xs], axis=0)

    def bands(t):
        return pile([t[b * BLK:(b + 2) * BLK, kvh * HD:(kvh + 1) * HD] for kvh in range(2) for b in range(nb)])

    qs = pile([jnp.concatenate([q[b * BLK:(b + 1) * BLK, (G * kvh + g) * HD:(G * kvh + g + 1) * HD]
                                for g in range(G)], axis=0) for kvh in range(2) for b in range(nb)])
    s = _bmm(qs, bands(k), 2, 2, 1) * (HD ** -0.5)
    qi = lax.broadcasted_iota(jnp.int32, (G * BLK, 2 * BLK), 0) % BLK
    kj = lax.broadcasted_iota(jnp.int32, (G * BLK, 2 * BLK), 1)
    dist = qi + BLK - kj
    in_band = (dist >= 0) & (dist < BLK)
    pair = lax.broadcasted_iota(jnp.int32, (2 * nb, 1, 1), 0)
    no_prev = (pair % nb == 0) & first
    valid = in_band[None] & (jnp.logical_not(no_prev) | (kj >= BLK)[None])
    s = jnp.where(valid, s, NEG_INF)
    row_g = lax.broadcasted_iota(jnp.int32, (G * BLK, 1), 0) // BLK
    sink = []
    for kvh in range(2):
        col = jnp.zeros((G * BLK, 1), F32)
        for g in range(G):
            col = jnp.where(row_g == g, sinks[:, G * kvh + g:G * kvh + g + 1], col)
        sink += [col] * nb
    sink = pile(sink)
    m = lax.stop_gradient(jnp.maximum(jnp.max(s, axis=-1, keepdims=True), sink))
    e = jnp.exp(s - m)
    p = e * (1.0 / (jnp.sum(e, axis=-1, keepdims=True) + jnp.exp(sink - m)))
    o = _bmm(p, bands(v), 2, 1, 1)
    return jnp.concatenate([jnp.concatenate([o[kvh * nb + b, g * BLK:(g + 1) * BLK] for kvh in range(2)
                                             for g in range(G)], axis=1) for b in range(nb)], axis=0)


def _heads(x):
    return jnp.stack([x[:, h * HD:(h + 1) * HD] for h in range(NH)], axis=0)


def _unheads(x):
    return jnp.concatenate([x[h] for h in range(NH)], axis=1)


def _split(x, n):
    parts, rest = [], x
    for _ in range(n):
        p = rest.astype(MXU)
        parts.append(p)
        rest = rest - p.astype(F32)
    return parts


def _bdot_batched(a, b, ca, cb):
    return lax.dot_general(a, b, (((ca,), (cb,)), ((0,), (0,))), preferred_element_type=F32)


def _bmm_passes(a, b, ca, cb, passes):
    if MXU == F32:
        return lax.dot_general(a, b, (((ca,), (cb,)), ((0,), (0,))), precision=HI, preferred_element_type=F32)
    if passes == 1:
        return _bdot_batched(a.astype(MXU), b.astype(MXU), ca, cb)
    (a0, a1), (b0, b1) = _split(a, 2), _split(b, 2)
    return _bdot_batched(a0, b0, ca, cb) + (_bdot_batched(a0, b1, ca, cb) + _bdot_batched(a1, b0, ca, cb))


@functools.partial(jax.custom_vjp, nondiff_argnums=(2, 3, 4))
def _bmm(a, b, ca, cb, passes=1):
    return _bmm_passes(a, b, ca, cb, passes)


def _bmm_fwd(a, b, ca, cb, passes):
    return _bmm_passes(a, b, ca, cb, passes), (a, b)


def _bmm_bwd(ca, cb, passes, res, g):
    a, b = res
    if (ca, cb) == (2, 1):
        return _bmm_passes(g, b, 2, 2, passes), _bmm_passes(a, g, 1, 1, passes)
    if (ca, cb) == (2, 2):
        return _bmm_passes(g, b, 2, 1, passes), _bmm_passes(g, a, 1, 1, passes)
    return _bmm_passes(b, g, 2, 2, passes), _bmm_passes(a, g, 2, 1, passes)


_bmm.defvjp(_bmm_fwd, _bmm_bwd)


def _tri_dot(x, transpose):
    C = x.shape[1]
    ri = lax.broadcasted_iota(jnp.int32, (C, C), 0)
    ci = lax.broadcasted_iota(jnp.int32, (C, C), 1)
    tri = jnp.broadcast_to(((ri <= ci) if transpose else (ri >= ci)).astype(MXU), (x.shape[0], C, C))
    if MXU == F32:
        return lax.dot_general(tri, x, (((2,), (1,)), ((0,), (0,))), precision=HI, preferred_element_type=F32)
    p0, p1, p2 = _split(x, 3)
    return _bdot_batched(tri, p0, 2, 1) + (_bdot_batched(tri, p1, 2, 1) + _bdot_batched(tri, p2, 2, 1))


@jax.custom_vjp
def _cumsum_rows(x):
    return _tri_dot(x, False)


def _cumsum_rows_fwd(x):
    return _tri_dot(x, False), None


def _cumsum_rows_bwd(_, g):
    return (_tri_dot(g, True),)


_cumsum_rows.defvjp(_cumsum_rows_fwd, _cumsum_rows_bwd)

P_SCORE = 1
P_SOLVE = 1
P_STATE = 1
SCAN_CHUNKS = (4, 2, 1)


def _neumann(l):
    C = l.shape[1]
    eye = (lax.broadcasted_iota(jnp.int32, (C, C), 0) == lax.broadcasted_iota(jnp.int32, (C, C), 1)).astype(F32)
    x, lp = eye + l, l
    for _ in range(int(math.log2(C)) - 1):
        lp = _bmm(lp, lp, 2, 1, P_SOLVE)
        x = x + _bmm(x, lp, 2, 1, P_SOLVE)
    return x


@jax.custom_vjp
def _unit_lower_inverse(l):
    return _neumann(l)


def _unit_lower_inverse_fwd(l):
    x = _neumann(l)
    return x, x


def _unit_lower_inverse_bwd(x, g):
    return (_bmm(_bmm(x, g, 1, 1, P_SOLVE), x, 2, 2, P_SOLVE),)


_unit_lower_inverse.defvjp(_unit_lower_inverse_fwd, _unit_lower_inverse_bwd)


def _known_inverse(x):
    @jax.custom_vjp
    def f(l):
        return x

    f.defvjp(lambda l: (x, None), lambda _, g: (_bmm(_bmm(x, g, 1, 1, P_SOLVE), x, 2, 2, P_SOLVE),))
    return f


def _chunk(S0, r, lw, k, v, a, b, inverse=None):
    C = CHUNK
    n = r.shape[1] // C
    fold = lambda t: t.reshape(NH * n, C, HD)
    r, lw, k, v, a, b = (fold(t) for t in (r, lw, k, v, a, b))
    ri = lax.broadcasted_iota(jnp.int32, (C, C), 0)
    ci = lax.broadcasted_iota(jnp.int32, (C, C), 1)
    incl = (ri >= ci)
    strict = (ri > ci)
    eye = (ri == ci).astype(F32)
    cum = _cumsum_rows(lw)
    p_in = jnp.exp(cum)
    p_ex = jnp.exp(cum - lw)
    p_inv = jnp.exp(-cum)
    at, rt, bt, kt = a * p_ex, r * p_in, b * p_inv, k * p_inv
    lhs = jnp.concatenate([at, rt], axis=1)
    rhs_ = jnp.concatenate([bt, kt], axis=1)
    sc = _bmm(lhs, rhs_, 2, 2, P_SCORE)
    a_ab = jnp.where(strict, sc[:, :C, :C], 0.0)
    a_ak = jnp.where(strict, sc[:, :C, C:], 0.0)
    incl2 = (lax.broadcasted_iota(jnp.int32, (C, 2 * C), 0) >= lax.broadcasted_iota(jnp.int32, (C, 2 * C), 1) % C)
    a_r = jnp.where(incl2, sc[:, C:, :], 0.0)
    av = _bmm(a_ak, v, 2, 1, P_SCORE)
    x = x_all = (_unit_lower_inverse if inverse is None else _known_inverse(inverse))(a_ab)
    p_last = jnp.exp(cum[:, C - 1:C, :])
    per_chunk = lambda t: t.reshape((NH, n) + t.shape[1:])
    lhs, rhs_, a_r, av, x, v, p_last = (per_chunk(t) for t in (lhs, rhs_, a_r, av, x, v, p_last))
    S, ys = S0, []
    for c in range(n):
        s0 = _bmm(lhs[:, c], S, 2, 2, P_STATE)
        u = _bmm(x[:, c], s0[:, :C] + av[:, c], 2, 1, P_SOLVE)
        uv = jnp.concatenate([u, v[:, c]], axis=1)
        ys.append(s0[:, C:] + _bmm(a_r[:, c], uv, 2, 1, P_SCORE))
        S = (S + _bmm(uv, rhs_[:, c], 1, 1, P_STATE)) * p_last[:, c]
    return jnp.concatenate(ys, axis=1), S, x_all


def _hosting(body, ex, n_in, n_out, n_scratch, n_steps):
    if ex is None:
        return body

    def wrapped(*refs):
        a = n_in
        b = a + ex.n_in
        c = b + n_out
        d = c + ex.n_out
        e = d + n_scratch
        ex_refs = (refs[a:b], refs[c:d], refs[e:])
        grid = n_steps if isinstance(n_steps, tuple) else (n_steps,)
        first = last = True
        for ax_, size in enumerate(grid):
            first = first & (pl.program_id(ax_) == 0)
            last = last & (pl.program_id(ax_) == size - 1)

        @pl.when(first)
        def _():
            ex.start(*ex_refs)

        body(*refs[:a], *refs[b:c], *refs[d:e])

        @pl.when(last)
        def _():
            ex.wait(*ex_refs)

    return wrapped


def _hosted_args(ex):
    if ex is None:
        return [], [], [], [], []
    any_spec = pl.BlockSpec(memory_space=pl.ANY)
    return list(ex.arrays), [any_spec] * ex.n_in, [any_spec] * ex.n_out, list(ex.out_shape), list(ex.scratch)


def _scan_fwd(rw, *, name, hosted=None):
    T = rw.shape[0]
    nc = _pick(T // CHUNK, SCAN_CHUNKS)
    rows = CHUNK * nc
    n = T // rows
    h_in, h_in_specs, h_out_specs, h_out_shape, h_scratch = _hosted_args(hosted)

    def body(r_ref, lw_ref, k_ref, v_ref, a_ref, b_ref, y_ref, ck_ref, inv_ref, s_ref):
        @pl.when(pl.program_id(0) == 0)
        def _():
            s_ref[...] = jnp.zeros_like(s_ref)

        S0 = s_ref[...]
        ck_ref[0] = S0
        y, S1, inv = _chunk(S0, *[_heads(ref[...]) for ref in (r_ref, lw_ref, k_ref, v_ref, a_ref, b_ref)])
        y_ref[...] = _unheads(y)
        inv_ref[0] = inv
        s_ref[...] = S1

    col = lambda j: pl.BlockSpec((rows, RW), lambda i: (i, j))
    return pl.pallas_call(
        _hosting(body, hosted, 6, 3, 1, n), name=name, grid=(n,),
        in_specs=[col(j) for j in range(6)] + h_in_specs,
        out_specs=[pl.BlockSpec((rows, RW), lambda i: (i, 0)),
                   pl.BlockSpec((1, NH, HD, HD), lambda i: (i, 0, 0, 0)),
                   pl.BlockSpec((1, NH * nc, CHUNK, CHUNK), lambda i: (i, 0, 0, 0))] + h_out_specs,
        out_shape=[jax.ShapeDtypeStruct((T, RW), F32), jax.ShapeDtypeStruct((n, NH, HD, HD), F32),
                   jax.ShapeDtypeStruct((n, NH * nc, CHUNK, CHUNK), F32)] + h_out_shape,
        scratch_shapes=[pltpu.VMEM((NH, HD, HD), F32)] + h_scratch,
        compiler_params=_cparams(("arbitrary",)),
    )(rw, rw, rw, rw, rw, rw, *h_in)


def _scan_bwd(rw, ck, inv, dy, *, name, hosted=None):
    T = rw.shape[0]
    nc = _pick(T // CHUNK, SCAN_CHUNKS)
    rows = CHUNK * nc
    n = T // rows

    def body(r_ref, lw_ref, k_ref, v_ref, a_ref, b_ref, ck_ref, inv_ref, dy_ref, o_ref, ds_ref):
        @pl.when(pl.program_id(0) == 0)
        def _():
            ds_ref[...] = jnp.zeros_like(ds_ref)

        prim = [_heads(ref[...]) for ref in (r_ref, lw_ref, k_ref, v_ref, a_ref, b_ref)]
        known = inv_ref[0]
        _, vjp = jax.vjp(lambda S0, *p: _chunk(S0, *p, inverse=known)[:2], ck_ref[0], *prim)
        grads = vjp((_heads(dy_ref[...]), ds_ref[...]))
        ds_ref[...] = grads[0]
        o_ref[...] = jnp.concatenate([_unheads(g) for g in grads[1:]], axis=1).astype(o_ref.dtype)

    h_in, h_in_specs, h_out_specs, h_out_shape, h_scratch = _hosted_args(hosted)
    col = lambda j: pl.BlockSpec((rows, RW), lambda i: (n - 1 - i, j))
    return pl.pallas_call(
        _hosting(body, hosted, 9, 1, 1, n), name=name, grid=(n,),
        in_specs=[col(j) for j in range(6)] + [pl.BlockSpec((1, NH, HD, HD), lambda i: (n - 1 - i, 0, 0, 0)),
                                               pl.BlockSpec((1, NH * nc, CHUNK, CHUNK), lambda i: (n - 1 - i, 0, 0, 0)),
                                               pl.BlockSpec((rows, RW), lambda i: (n - 1 - i, 0))] + h_in_specs,
        out_specs=[pl.BlockSpec((rows, 6 * RW), lambda i: (n - 1 - i, 0))] + h_out_specs,
        out_shape=[jax.ShapeDtypeStruct((T, 6 * RW), MXU)] + h_out_shape,
        scratch_shapes=[pltpu.VMEM((NH, HD, HD), F32)] + h_scratch,
        compiler_params=_cparams(("arbitrary",)),
    )(rw, rw, rw, rw, rw, rw, ck, inv, dy, *h_in)


ATTN_BLOCKS = (4, 2, 1)

def _attn_fwd(qkv, tab, qg, kg, sinks, *, name, hosted=None):
    T = qkv.shape[0]
    nb = _pick(T // BLK, ATTN_BLOCKS)
    n = T // (BLK * nb)
    h_in, h_in_specs, h_out_specs, h_out_shape, h_scratch = _hosted_args(hosted)

    def body(c_ref, p_ref, tc_ref, tp_ref, qg_ref, kg_ref, s_ref, o_ref):
        o_ref[...] = _attn_blocks(c_ref[...], p_ref[...], tc_ref[...], tp_ref[...], qg_ref[...], kg_ref[...],
                                  s_ref[...], pl.program_id(0) == 0).astype(o_ref.dtype)

    cur = lambda w: pl.BlockSpec((nb * BLK, w), lambda i: (i, 0))
    prev = lambda w: pl.BlockSpec((BLK, w), lambda i: (jnp.maximum(i * nb - 1, 0), 0))
    return pl.pallas_call(
        _hosting(body, hosted, 7, 1, 0, n), name=name, grid=(n,),
        in_specs=[cur(QKV_W), prev(QKV_W), cur(3 * 128), prev(3 * 128), _full_spec(qg), _full_spec(kg),
                  _full_spec(sinks)] + h_in_specs,
        out_specs=[cur(RW)] + h_out_specs, out_shape=[jax.ShapeDtypeStruct((T, RW), MXU)] + h_out_shape,
        scratch_shapes=h_scratch,
        compiler_params=_cparams(("arbitrary",)),
    )(qkv, qkv, tab, tab, qg, kg, sinks, *h_in)


def _attn_bwd(qkv, tab, qg, kg, sinks, dy, *, name, hosted=None):
    T = qkv.shape[0]
    nb = _pick(T // BLK, ATTN_BLOCKS)
    n = T // (BLK * nb)
    h_in, h_in_specs, h_out_specs, h_out_shape, h_scratch = _hosted_args(hosted)

    def body(c_ref, p_ref, tc_ref, tp_ref, qg_ref, kg_ref, s_ref, dy_ref, dqkv_ref, dqg_ref, dkg_ref, ds_ref, carry_ref):
        i = pl.program_id(0)

        @pl.when(i == 0)
        def _():
            carry_ref[...] = jnp.zeros_like(carry_ref)
            dqg_ref[...] = jnp.zeros_like(dqg_ref)
            dkg_ref[...] = jnp.zeros_like(dkg_ref)
            ds_ref[...] = jnp.zeros_like(ds_ref)

        tc, tp = tc_ref[...], tp_ref[...]
        f = lambda c, p_, qg_, kg_, sk: _attn_blocks(c, p_, tc, tp, qg_, kg_, sk, i == n - 1)
        _, vjp = jax.vjp(f, c_ref[...], p_ref[...], qg_ref[...], kg_ref[...], s_ref[...])
        dc, dp, dqg, dkg, dsk = vjp(dy_ref[...].astype(F32))
        last = slice((nb - 1) * BLK, nb * BLK)
        dqkv_ref[...] = dc.astype(dqkv_ref.dtype)
        dqkv_ref[last, :] = (dc[last] + carry_ref[...]).astype(dqkv_ref.dtype)
        carry_ref[...] = dp
        dqg_ref[...] += dqg
        dkg_ref[...] += dkg
        ds_ref[...] += dsk

    cur = lambda w: pl.BlockSpec((nb * BLK, w), lambda i: (n - 1 - i, 0))
    prev = lambda w: pl.BlockSpec((BLK, w), lambda i: (jnp.maximum((n - 1 - i) * nb - 1, 0), 0))
    return pl.pallas_call(
        _hosting(body, hosted, 8, 4, 1, n), name=name, grid=(n,),
        in_specs=[cur(QKV_W), prev(QKV_W), cur(3 * 128), prev(3 * 128), _full_spec(qg), _full_spec(kg), _full_spec(sinks),
                  cur(RW)] + h_in_specs,
        out_specs=[cur(QKV_W), _full_spec(qg), _full_spec(kg), _full_spec(sinks)] + h_out_specs,
        out_shape=[jax.ShapeDtypeStruct((T, QKV_W), MXU), jax.ShapeDtypeStruct(qg.shape, F32),
                   jax.ShapeDtypeStruct(kg.shape, F32), jax.ShapeDtypeStruct(sinks.shape, F32)] + h_out_shape,
        scratch_shapes=[pltpu.VMEM((BLK, QKV_W), F32)] + h_scratch,
        compiler_params=_cparams(("arbitrary",)),
    )(qkv, qkv, tab, tab, qg, kg, sinks, dy, *h_in)


def _shift_down(cur, prev8, i):
    rolled = pltpu.roll(cur, 1, 0)
    first_row = jnp.where(i > 0, prev8[7:8, :], 0.0)
    row = lax.broadcasted_iota(jnp.int32, cur.shape, 0)
    return jnp.where(row == 0, first_row, rolled)


def _shift_up(cur, next8, i, n):
    tm = cur.shape[0]
    rolled = pltpu.roll(cur, tm - 1, 0)
    last_row = jnp.where(i < n - 1, next8[0:1, :], 0.0)
    row = lax.broadcasted_iota(jnp.int32, cur.shape, 0)
    return jnp.where(row == tm - 1, last_row, rolled)


def _ada_parts(ada):
    return [ada[:, j * D:(j + 1) * D] for j in range(6)]


def _rope_table(positions):
    half = HD // 8
    inv_freq = 500000.0 ** (-jnp.arange(half, dtype=F32) / half)
    ang = positions.astype(F32)[:, None] * inv_freq
    cos, sin = jnp.cos(ang), jnp.sin(ang)
    T = positions.shape[0]
    pad = HD - 2 * half
    c64 = jnp.concatenate([cos, cos, jnp.ones((T, pad), F32)], axis=1)
    lo64 = jnp.concatenate([-sin, jnp.zeros((T, HD - half), F32)], axis=1)
    hi64 = jnp.concatenate([jnp.zeros((T, half), F32), sin, jnp.zeros((T, pad), F32)], axis=1)
    return jnp.concatenate([c64, c64, lo64, lo64, hi64, hi64], axis=1)


GATHER_BEHIND = {"f_proj_shift": [('ffn_w3', 512, 768)], "f_proj_gates": [('ffn_w3', 768, 1024)],
                 "f_prep": [('ffn_w2', 352, 704)],
                 "f_scan": [('ffn_w1', 0, 1024), ('w_branch_a', 0, 512), ('w_branch_b', 0, 512)],
                 "f_post": [('ffn_w3', 0, 512)], "f_attn": [('ffn_w2', 0, 352), ('w_out', 0, 256)]}
LATE = ['w_out', 'w_branch_a', 'w_branch_b', 'ffn_w1', 'ffn_w3', 'ffn_w2']
BACK_ATTN = ['w_out', 'w_branch_a', 'w_branch_b', 'ffn_w2']
BACK_SCAN = ['ffn_w1', 'ffn_w3']
BACK_LAST = ['w_in', 'decay_up', 'iclr_up', 'gate_up']


def _full_weight(g, ax):
    return g.reshape(-1, g.shape[2]) if ax == 0 else jnp.concatenate([g[j] for j in range(4)], axis=1)


def _local_step(x, target, ada, tab, w, s, shards=None):
    T = x.shape[0]
    tm = _pick(T, (512, 256, 128))
    tm_wide = _pick(T, (256, 128))
    tm_vjp = _pick(T, (256, 128))
    row = lambda n, dt=F32: (n, dt, 'row')
    acc = lambda n, r=1: (n, F32, r)

    def f_norm1(i, n, x_, g, ada_):
        sh, sc = ada_[:, 0:D], ada_[:, D:2 * D]
        return (_norm_mod(x_, g, sc, sh),)
    (h1,) = _rowwise(f_norm1, [x], [s['norm1_gain'], ada], [row(D, MXU)], tm=tm, name="f_norm1")

    landed = {}

    def behind(kernel_name):
        if not shards:
            return None
        return _GatherChips([shards[n] if hi - lo == shards[n].shape[0] else shards[n][lo:hi]
                             for n, lo, hi in GATHER_BEHIND[kernel_name]])

    def took(kernel_name, got):
        landed.update(zip(GATHER_BEHIND[kernel_name], got))

    def mm_behind(a_, w_, kernel_name, **kw):
        ex = behind(kernel_name)
        res = _mm_nn(a_, w_, name=kernel_name, hosted=ex, **kw)
        if ex:
            took(kernel_name, res[1:])
            return res[0]
        return res

    proj = mm_behind(h1, w['w_in'][:, :SHIFT_W], "f_proj_shift")
    proj_qkv = _mm_nn(h1, w['w_in'][:, SHIFT_W:SHIFT_W + QKV_W], name="f_proj_qkv")
    proj_g = mm_behind(h1, w['w_in'][:, SHIFT_W + QKV_W:], "f_proj_gates", out_dtype=MXU)
    prep_consts = [s['decay_w0'], s['lora_up'], s['iclr_a0'], s['gate_up'], s['k_k'], s['k_a']]

    def f_prep(i, n, cur, prev8, mu, *params):
        mixed = cur + (_shift_down(cur, prev8, i) - cur) * mu
        return (_prep(mixed, *params),)
    rw, *got = _rowwise(f_prep, [(proj, SHIFT_W)], [s['tshift_mu']] + prep_consts, [row(7 * RW)], tm=tm_wide,
                        name="f_prep", halo=[(proj, SHIFT_W, 'prev')], hosted=behind("f_prep"))
    took("f_prep", got)
    y, ck, inv, *got = _scan_fwd(rw, name="f_scan", hosted=behind("f_scan"))
    took("f_scan", got)
    post_consts = [s['lnx_gain'], s['lnx_bias'], s['r_k']]

    rkvg = [(rw, RW, j) for j in (0, 2, 3, 6)]

    def f_post(i, n, *args):
        return (_post(*args),)
    ya, *got = _rowwise(f_post, [y] + rkvg, post_consts, [row(RW, MXU)], tm=tm_wide, name="f_post",
                        hosted=behind("f_post"))
    took("f_post", got)
    yb, *got = _attn_fwd(proj_qkv, tab, s['q_norm_gain'], s['k_norm_gain'], s['attn_sinks'], name="f_attn",
                         hosted=behind("f_attn"))
    took("f_attn", got)
    w = dict(w)
    if shards:
        ax = dict(SHARDED)
        for n in LATE:
            rows = [landed[key] for key in sorted(k_ for k_ in landed if k_[0] == n)]
            w[n] = _full_weight(rows[0] if len(rows) == 1 else jnp.concatenate(rows, axis=1), ax[n])
    ma = _mm_nn(ya, w['w_branch_a'], name="f_branch_a", out_dtype=MXU)
    mb = _mm_nn(yb, w['w_branch_b'], name="f_branch_b", out_dtype=MXU)

    def f_merge(i, n, pg, ma_, mb_, bias):
        return (_merge(pg.astype(F32), ma_.astype(F32), mb_.astype(F32), bias),)
    (merged,) = _rowwise(f_merge, [proj_g, ma, mb], [s['branch_gate_b']], [row(D, MXU)], tm=tm, name="f_merge")

    def f_res1(i, n, mo_, x_, g, ada_):
        x1_ = x_ + ada_[:, 2 * D:3 * D] * mo_
        return mo_, x1_, _norm_mod(x1_, g, ada_[:, 4 * D:5 * D], ada_[:, 3 * D:4 * D])
    mo, x1, h2 = _mm_then([(merged, w['w_out'], 'nn')], f_res1, [x], [s['norm2_gain'], ada],
                          [row(D), row(D), row(D, MXU)], tm=tm, name="f_out")
    u, v, act = _ffn_in(h2, w['ffn_w1'], w['ffn_w3'], name="f_ffn_in")

    def f_loss(i, n, ff_, x1_, tgt, ada_):
        g2 = ada_[:, 5 * D:6 * D]
        err = x1_ + g2 * ff_ - tgt
        dx2 = err * (1.0 / D)
        loss = 0.5 * jnp.sum(jnp.sum(err * err, axis=1, keepdims=True) * (1.0 / D), axis=0, keepdims=True)
        return dx2, (dx2 * g2), jnp.broadcast_to(loss, (1, 128)), jnp.sum(dx2 * ff_, axis=0, keepdims=True)
    dx2, dff, loss, dgate2 = _mm_then([(act, w['ffn_w2'], 'nn')], f_loss, [x1, target], [ada],
                                      [row(D), row(D, MXU), acc(128), acc(D)], tm=tm, name="f_ffn_out")

    du, dv = _ffn_act_bwd(dff, w['ffn_w2'], u, v, name="b_ffn_out_dx")
    g_w2 = _mm_tn(act, dff, name="b_ffn_out_dw", out_dtype=MXU)
    g_w1 = _mm_tn(h2, du, name="b_ffn_w1_dw", out_dtype=MXU)
    g_w3 = _mm_tn(h2, dv, name="b_ffn_w3_dw", out_dtype=MXU)

    def b_res1(i, n, dh2_, x1_, dx2_, mo_, g, ada_):
        _, vjp = jax.vjp(_norm_mod, x1_, g, ada_[:, 4 * D:5 * D], ada_[:, 3 * D:4 * D])
        dxn, dg, dsc, dsh = vjp(dh2_)
        dx1_ = dxn + dx2_
        g1 = ada_[:, 2 * D:3 * D]
        return dx1_, dx1_ * g1, dg, dsc, dsh, jnp.sum(dx1_ * mo_, axis=0, keepdims=True)
    dx1, dmo, d_gain2, d_scale2, d_shift2, dgate1 = _mm_then(
        [(du, w['ffn_w1'], 'nt'), (dv, w['ffn_w3'], 'nt')], b_res1, [x1, dx2, mo], [s['norm2_gain'], ada],
        [row(D), row(D, MXU), acc(D), acc(D), acc(D), acc(D)], tm=tm_wide, name="b_ffn_in_dx")
    g_wout = _mm_tn(merged, dmo, name="b_out_dw", out_dtype=MXU)

    def b_merge(i, n, dm, pg, ma_, mb_, bias):
        _, vjp = jax.vjp(_merge, pg.astype(F32), ma_.astype(F32), mb_.astype(F32), bias)
        dpg, dma_, dmb_, dbias = vjp(dm)
        return dpg, dma_, dmb_, dbias
    dpg, dma, dmb, d_bias = _mm_then([(dmo, w['w_out'], 'nt')], b_merge, [proj_g, ma, mb], [s['branch_gate_b']],
                                     [row(GATE_W, MXU), row(D, MXU), row(D, MXU), acc(GATE_W)], tm=tm_wide,
                                     name="b_out_dx")
    dya = _mm_nt(dma, w['w_branch_a'], name="b_branch_a_dx")
    g_wa = _mm_tn(ya, dma, name="b_branch_a_dw", out_dtype=MXU, col_shards=4)
    dyb = _mm_nt(dmb, w['w_branch_b'], name="b_branch_b_dx", out_dtype=F32)
    g_wb = _mm_tn(yb, dmb, name="b_branch_b_dw", out_dtype=MXU, col_shards=4)
    fs = DFF // 4
    gw = dict(w_branch_a=g_wa, w_branch_b=g_wb, w_out=g_wout.reshape(4, D // 4, D),
              ffn_w1=jnp.stack([g_w1[:, j * fs:(j + 1) * fs] for j in range(4)]),
              ffn_w3=jnp.stack([g_w3[:, j * fs:(j + 1) * fs] for j in range(4)]),
              ffn_w2=g_w2.reshape(4, fs, D))
    recv = {}
    dqkv, d_qg, d_kg, d_sinks, *got = _attn_bwd(
        proj_qkv, tab, s['q_norm_gain'], s['k_norm_gain'], s['attn_sinks'], dyb, name="b_attn",
        hosted=shards and _ScatterChips([gw[n] for n in BACK_ATTN]))
    recv.update(zip(BACK_ATTN, got))

    def b_post(i, n, y_, r_, k_, v_, g_, dya_, *params):
        _, vjp = jax.vjp(_post, y_, r_, k_, v_, g_, *params)
        dy_, dr_, dk_, dv_, dg_, *dparams = vjp(dya_)
        return (dy_, jnp.concatenate([dr_, dk_, dv_, dg_], axis=1), *dparams)
    dy, drkvg, d_lnx_gain, d_lnx_bias, d_r_k = _rowwise(
        b_post, [y] + rkvg + [dya], post_consts, [row(RW), row(4 * RW, MXU), acc(RW), acc(RW), acc(RW)], tm=tm_wide,
        name="b_post")
    dscan, *got = _scan_bwd(rw, ck, inv, dy, name="b_scan",
                            hosted=shards and _ScatterChips([gw[n] for n in BACK_SCAN]))
    recv.update(zip(BACK_SCAN, got))

    def b_prep(i, n, cur, drw_, dscan_, prev8, mu, *params):
        shifted = _shift_down(cur, prev8, i)
        mixed = cur + (shifted - cur) * mu
        _, vjp = jax.vjp(_prep, mixed, *params)
        blk = lambda t, j: t[:, j * RW:(j + 1) * RW].astype(F32)
        ct = jnp.concatenate([blk(dscan_, 0) + blk(drw_, 0), blk(dscan_, 1), blk(dscan_, 2) + blk(drw_, 1),
                              blk(dscan_, 3) + blk(drw_, 2), blk(dscan_, 4), blk(dscan_, 5), blk(drw_, 3)], axis=1)
        grads = vjp(ct)
        dmixed = grads[0]
        return (dmixed, jnp.sum(dmixed * (shifted - cur), axis=0, keepdims=True)) + tuple(grads[1:])
    dmixed, d_mu, d_w0, d_lora, d_a0, d_gate_up, d_kk, d_ka = _rowwise(
        b_prep, [(proj, SHIFT_W), drkvg, dscan], [s['tshift_mu']] + prep_consts,
        [row(SHIFT_W), acc(SHIFT_W), acc(RW), acc(2 * RW, 128), acc(RW), acc(RW, 128), acc(RW), acc(RW)],
        tm=tm_vjp, name="b_prep", halo=[(proj, SHIFT_W, 'prev')])

    def b_gather(i, n, dm, dqkv_, dpg_, next8, mu):
        dcur = dm * (1.0 - mu) + _shift_up(dm, next8, i, n) * mu
        return (jnp.concatenate([dcur.astype(MXU), dqkv_, dpg_], axis=1),)
    (dproj,) = _rowwise(b_gather, [dmixed, dqkv, dpg], [s['tshift_mu']], [row(IN_W, MXU)], tm=tm_wide, name="b_gather",
                        halo=[(dmixed, SHIFT_W, 'next')])
    g_win = _mm_tn(h1, dproj, name="b_proj_dw", out_dtype=MXU, col_shards=4)

    def col_blocks(g):
        k, n = g.shape
        return g.reshape(k, 4, n // 4).transpose(1, 0, 2).astype(MXU)
    gw.update(w_in=g_win, decay_up=col_blocks(d_lora[:64, :RW]), iclr_up=col_blocks(d_lora[64:, RW:]),
              gate_up=col_blocks(d_gate_up))
    top, bottom = None, None
    if shards:
        top = _ScatterChips([gw['w_in'][:, :D // 2]] + [gw[n] for n in BACK_LAST[1:]])
        bottom = _ScatterChips([gw['w_in'][:, D // 2:3 * D // 4]])
        dh1, *got_top = _mm_nt(dproj, w['w_in'], name="b_proj_dx", hosted=top)
    else:
        dh1 = _mm_nt(dproj, w['w_in'], name="b_proj_dx")

    def b_norm1(i, n, x_, dh1_, dx1_, g, ada_):
        _, vjp = jax.vjp(_norm_mod, x_, g, ada_[:, D:2 * D], ada_[:, 0:D])
        dxn, dg, dsc, dsh = vjp(dh1_)
        return dxn + dx1_, dg, dsc, dsh
    dx, d_gain1, d_scale1, d_shift1, *got_bottom = _rowwise(
        b_norm1, [x, dh1, dx1], [s['norm1_gain'], ada], [row(D), acc(D), acc(D), acc(D)], tm=tm, name="b_norm1",
        hosted=bottom)
    if shards:
        recv.update(zip(BACK_LAST[1:], got_top[1:]))
        recv['w_in'] = [got_top[0], got_bottom[0]]

    d_ada = jnp.concatenate([d_shift1, d_scale1, dgate1, d_shift2, d_scale2, dgate2], axis=1)
    gs = dict(norm1_gain=d_gain1, norm2_gain=d_gain2, tshift_mu=d_mu, decay_w0=d_w0, iclr_a0=d_a0, k_k=d_kk, k_a=d_ka,
              r_k=d_r_k, lnx_gain=d_lnx_gain, lnx_bias=d_lnx_bias, q_norm_gain=d_qg, k_norm_gain=d_kg,
              attn_sinks=d_sinks, branch_gate_b=d_bias)
    return loss, dx, d_ada, gw, gs, recv


ANY = pl.BlockSpec(memory_space=pl.ANY)


def _place():
    x, y, c = lax.axis_index("x"), lax.axis_index("y"), lax.axis_index("c")
    return x, y, c, [(1 - x, y), (x, 1 - y), (1 - x, 1 - y)]


def _all_gather8(x_shard, *, name):
    m_per, n = x_shard.shape

    def body(x_ref, out_ref, send_sems, recv_sems, local_sem):
        x, y, c, chips = _place()
        me, sibling = (x, y, c), (x, y, 1 - c)

        def rows(px, py, pc):
            return out_ref.at[pl.ds((4 * px + 2 * py + pc) * m_per, m_per), :]

        def copy(k, block, to, src=None):
            return pltpu.make_async_remote_copy(
                src_ref=rows(*block) if src is None else src, dst_ref=rows(*block),
                send_sem=send_sems.at[k], recv_sem=recv_sems.at[k], device_id=to, device_id_type=MESH)

        mine = pltpu.make_async_copy(x_ref, rows(*me), local_sem)
        mine.start()
        first = [copy(0, me, sibling, src=x_ref)]
        first += [copy(1 + j, me, (*chip, c), src=x_ref) for j, chip in enumerate(chips)]
        for cp in first:
            cp.start()
        passed = [copy(4 + j, (*chip, c), sibling) for j, chip in enumerate(chips)]
        for j, chip in enumerate(chips):
            copy(1 + j, (*chip, c), me).wait_recv()
            passed[j].start()
        copy(0, sibling, me).wait_recv()
        for j, chip in enumerate(chips):
            copy(4 + j, (*chip, 1 - c), me).wait_recv()
        for cp in first + passed:
            cp.wait_send()
        mine.wait()

    return pl.pallas_call(
        body, name=name, out_shape=jax.ShapeDtypeStruct((8 * m_per, n), x_shard.dtype),
        in_specs=[pl.BlockSpec(memory_space=pltpu.VMEM)], out_specs=pl.BlockSpec(memory_space=pltpu.VMEM),
        scratch_shapes=[pltpu.SemaphoreType.DMA((7,)), pltpu.SemaphoreType.DMA((7,)), pltpu.SemaphoreType.DMA],
    )(x_shard)


class _GatherChips:
    def __init__(self, shards):
        n = len(shards)
        self.arrays, self.n_in, self.n_out = list(shards), n, n
        self.out_shape = [jax.ShapeDtypeStruct((4,) + s.shape, s.dtype) for s in shards]
        self.scratch = [pltpu.SemaphoreType.DMA((3 * n,)), pltpu.SemaphoreType.DMA((3 * n,)),
                        pltpu.SemaphoreType.DMA((n,))]

    def _copies(self, x_refs, out_refs, sems, receiving):
        send_sems, recv_sems, local_sems = sems
        x, y, c, chips = _place()
        s_me = 2 * x + y
        n = self.n_in

        def copy(a, k, s):
            return pltpu.make_async_remote_copy(
                src_ref=x_refs[a], dst_ref=out_refs[a].at[s], send_sem=send_sems.at[3 * a + k],
                recv_sem=recv_sems.at[3 * a + k], device_id=(*chips[k], c), device_id_type=MESH)

        mine = [pltpu.make_async_copy(x_refs[a], out_refs[a].at[s_me], local_sems.at[a]) for a in range(n)]
        sends = [copy(a, k, s_me) for a in range(n) for k in range(3)]
        if not receiving:
            return mine, sends
        return mine, sends, [copy(a, k, 2 * px + py) for a in range(n) for k, (px, py) in enumerate(chips)]

    def start(self, x_refs, out_refs, sems):
        mine, sends = self._copies(x_refs, out_refs, sems, False)
        for cp in mine + sends:
            cp.start()

    def wait(self, x_refs, out_refs, sems):
        mine, sends, recvs = self._copies(x_refs, out_refs, sems, True)
        for cp in recvs:
            cp.wait_recv()
        for cp in sends:
            cp.wait_send()
        for cp in mine:
            cp.wait()


class _GatherChipsHalved(_GatherChips):
    def __init__(self, shards):
        super().__init__(shards)
        n = self.n_in
        self.scratch = [pltpu.SemaphoreType.DMA((6 * n,)), pltpu.SemaphoreType.DMA((6 * n,)),
                        pltpu.SemaphoreType.DMA((n,))]

    def _copies(self, x_refs, out_refs, sems, receiving):
        send_sems, recv_sems, local_sems = sems
        x, y, c, chips = _place()
        s_me = 2 * x + y
        n = self.n_in

        def half(a, who):
            rows = x_refs[a].shape[0] // 2
            return pl.ds(who * rows, rows)

        def over_chips(a, k, s):
            return pltpu.make_async_remote_copy(
                src_ref=x_refs[a].at[half(a, c)], dst_ref=out_refs[a].at[s, half(a, c)],
                send_sem=send_sems.at[3 * a + k], recv_sem=recv_sems.at[3 * a + k],
                device_id=(*chips[k], c), device_id_type=MESH)

        def to_sibling(a, k, s, who):
            return pltpu.make_async_remote_copy(
                src_ref=out_refs[a].at[s, half(a, who)], dst_ref=out_refs[a].at[s, half(a, who)],
                send_sem=send_sems.at[3 * n + 3 * a + k], recv_sem=recv_sems.at[3 * n + 3 * a + k],
                device_id=(x, y, 1 - c), device_id_type=MESH)

        mine = [pltpu.make_async_copy(x_refs[a], out_refs[a].at[s_me], local_sems.at[a]) for a in range(n)]
        sends = [over_chips(a, k, s_me) for a in range(n) for k in range(3)]
        if not receiving:
            return mine, sends
        pairs = [(a, k, 2 * px + py) for a in range(n) for k, (px, py) in enumerate(chips)]
        landed = [over_chips(a, k, s) for a, k, s in pairs]
        passed_on = [to_sibling(a, k, s, c) for a, k, s in pairs]
        from_sibling = [to_sibling(a, k, s, 1 - c) for a, k, s in pairs]
        return mine, sends, landed, passed_on, from_sibling

    def wait(self, x_refs, out_refs, sems):
        mine, sends, landed, passed_on, from_sibling = self._copies(x_refs, out_refs, sems, True)
        for got, fwd in zip(landed, passed_on, strict=True):
            got.wait_recv()
            fwd.start()
        for cp in from_sibling:
            cp.wait_recv()
        for cp in sends + passed_on:
            cp.wait_send()
        for cp in mine:
            cp.wait()


class _ScatterChips:
    def __init__(self, parts):
        n = len(parts)
        self.arrays, self.n_in, self.n_out = list(parts), n, n
        self.out_shape = [jax.ShapeDtypeStruct((3,) + p.shape[1:], p.dtype) for p in parts]
        self.scratch = [pltpu.SemaphoreType.DMA((3 * n,)), pltpu.SemaphoreType.DMA((3 * n,))]

    def _copies(self, g_refs, out_refs, sems):
        send_sems, recv_sems = sems
        x, y, c, chips = _place()
        return [pltpu.make_async_remote_copy(
            src_ref=g_refs[a].at[2 * px + py], dst_ref=out_refs[a].at[k], send_sem=send_sems.at[3 * a + k],
            recv_sem=recv_sems.at[3 * a + k], device_id=(px, py, c), device_id_type=MESH)
            for a in range(self.n_in) for k, (px, py) in enumerate(chips)]

    def start(self, g_refs, out_refs, sems):
        for cp in self._copies(g_refs, out_refs, sems):
            cp.start()

    def wait(self, g_refs, out_refs, sems):
        sends = self._copies(g_refs, out_refs, sems)
        for cp in sends:
            cp.wait_recv()
        for cp in sends:
            cp.wait_send()


def _exchange_call(ex, *, name):
    def body(*refs):
        parts = (refs[:ex.n_in], refs[ex.n_in:ex.n_in + ex.n_out], refs[ex.n_in + ex.n_out:])
        ex.start(*parts)
        ex.wait(*parts)

    return pl.pallas_call(body, name=name, out_shape=ex.out_shape, in_specs=[ANY] * ex.n_in,
                          out_specs=[ANY] * ex.n_out, scratch_shapes=ex.scratch)(*ex.arrays)


class _SwapSibling:
    def __init__(self, vs):
        n = len(vs)
        self.arrays, self.n_in, self.n_out = list(vs), n, n
        self.out_shape = [jax.ShapeDtypeStruct(v.shape, v.dtype) for v in vs]
        self.scratch = [pltpu.SemaphoreType.DMA((n,)), pltpu.SemaphoreType.DMA((n,))]

    def _copies(self, v_refs, out_refs, sems):
        send_sems, recv_sems = sems
        x, y, c, _ = _place()
        return [pltpu.make_async_remote_copy(src_ref=v_refs[a], dst_ref=out_refs[a], send_sem=send_sems.at[a],
                                             recv_sem=recv_sems.at[a], device_id=(x, y, 1 - c), device_id_type=MESH)
                for a in range(self.n_in)]

    def start(self, v_refs, out_refs, sems):
        for cp in self._copies(v_refs, out_refs, sems):
            cp.start()

    def wait(self, v_refs, out_refs, sems):
        for cp in self._copies(v_refs, out_refs, sems):
            cp.wait()


class _Both:
    def __init__(self, first, second):
        self.parts = (first, second)
        self.arrays = first.arrays + second.arrays
        self.n_in, self.n_out = first.n_in + second.n_in, first.n_out + second.n_out
        self.out_shape = first.out_shape + second.out_shape
        self.scratch = first.scratch + second.scratch

    def _split(self, in_refs, out_refs, sems):
        a, b = self.parts
        return ((a, in_refs[:a.n_in], out_refs[:a.n_out], sems[:len(a.scratch)]),
                (b, in_refs[a.n_in:], out_refs[a.n_out:], sems[len(a.scratch):]))

    def start(self, in_refs, out_refs, sems):
        for ex, *refs in self._split(in_refs, out_refs, sems):
            ex.start(*refs)

    def wait(self, in_refs, out_refs, sems):
        for ex, *refs in self._split(in_refs, out_refs, sems):
            ex.wait(*refs)


def _sum_parts(own, others, *, name):
    R, C = own.shape
    tm = _pick(R, (256, 128, 64))

    def body(own_ref, o0_ref, o1_ref, o2_ref, out_ref):
        tot = own_ref[...].astype(F32)
        for ref in (o0_ref, o1_ref, o2_ref):
            tot = tot + ref[...].astype(F32)
        out_ref[...] = tot

    part = lambda k: pl.BlockSpec((None, tm, C), lambda i: (k, i, 0))
    return pl.pallas_call(
        body, name=name, grid=(R // tm,),
        in_specs=[pl.BlockSpec((tm, C), lambda i: (i, 0)), part(0), part(1), part(2)],
        out_specs=pl.BlockSpec((tm, C), lambda i: (i, 0)), out_shape=jax.ShapeDtypeStruct((R, C), F32),
        compiler_params=_cparams(("arbitrary",)),
    )(own, others, others, others)


def _adam_math(w_, m_, v_, g):
    m2 = ADAM_B1 * m_ + (1.0 - ADAM_B1) * g
    v2 = ADAM_B2 * v_ + (1.0 - ADAM_B2) * jnp.square(g)
    m_hat = m2 / (1.0 - ADAM_B1 ** ADAM_STEP)
    v_hat = v2 / (1.0 - ADAM_B2 ** ADAM_STEP)
    delta = -ADAM_LR * (m_hat / (jnp.sqrt(v_hat) + ADAM_EPS) + ADAM_WD * w_)
    return delta, m2, v2


SMALL_SLOTS = 16
SMALL_COLS = 6 * D


def _pack_small(grads, *, name):
    n = len(grads)

    def body(*refs):
        out_ref = refs[n]
        out_ref[...] = jnp.zeros_like(out_ref)
        for i, ref in enumerate(refs[:n]):
            out_ref[i:i + 1, 0:ref.shape[1]] = ref[...]

    return pl.pallas_call(body, name=name, out_shape=jax.ShapeDtypeStruct((SMALL_SLOTS, SMALL_COLS), F32))(*grads)


def _adamw_small(ws, ms, vs, gathered, *, name):
    n = len(ws)

    def total(g_ref, i, nc):
        g = g_ref[i:i + 1, 0:nc]
        for d in range(1, 8):
            g = g + g_ref[d * SMALL_SLOTS + i:d * SMALL_SLOTS + i + 1, 0:nc]
        return g

    def body(*refs):
        w_refs, m_refs, v_refs, g_ref = refs[:n], refs[n:2 * n], refs[2 * n:3 * n], refs[3 * n]
        outs = refs[3 * n + 1:]
        for i in range(n):
            g = total(g_ref, i, w_refs[i].shape[1])
            delta, m2, v2 = _adam_math(w_refs[i][...], m_refs[i][...], v_refs[i][...], g)
            for k, val in enumerate((g, delta, m2, v2)):
                outs[k * n + i][...] = val
        outs[4 * n][...] = total(g_ref, n, 128)

    shapes = [jax.ShapeDtypeStruct(w.shape, F32) for w in ws]
    res = pl.pallas_call(body, name=name, out_shape=shapes * 4 + [jax.ShapeDtypeStruct((1, 128), F32)],
                         compiler_params=pltpu.CompilerParams(vmem_limit_bytes=VMEM_LIMIT))(*ws, *ms, *vs, gathered)
    return [res[k * n:(k + 1) * n] for k in range(4)], res[4 * n]


def _adamw(w, m, v, gparts, *, tm, name, hosted=None):
    def fn(i, n, w_, m_, v_, *gs):
        g = gs[0]
        for p in gs[1:]:
            g = g + p
        return (g,) + _adam_math(w_, m_, v_, g)
    nc = w.shape[1]
    return _rowwise(fn, [w, m, v] + list(gparts), [], [(nc, F32, 'row')] * 4, tm=tm, name=name, hosted=hosted)


WEIGHTS = ['ada_w', 'ada_b', 'norm1_gain', 'norm2_gain', 'w_in', 'tshift_mu', 'decay_w0', 'decay_up', 'iclr_a0',
           'iclr_up', 'gate_up', 'k_k', 'k_a', 'r_k', 'lnx_gain', 'lnx_bias', 'q_norm_gain', 'k_norm_gain', 'attn_sinks',
           'branch_gate_b', 'w_branch_a', 'w_branch_b', 'w_out', 'ffn_w1', 'ffn_w3', 'ffn_w2']
SHARDED = [('w_in', 1), ('decay_up', 1), ('iclr_up', 1), ('gate_up', 1), ('w_branch_a', 1), ('w_branch_b', 1),
           ('w_out', 0), ('ffn_w1', 1), ('ffn_w3', 1), ('ffn_w2', 0)]
SMALL = ['ada_b', 'norm1_gain', 'norm2_gain', 'tshift_mu', 'decay_w0', 'iclr_a0', 'k_k', 'k_a', 'r_k', 'lnx_gain',
         'lnx_bias', 'q_norm_gain', 'k_norm_gain', 'attn_sinks', 'branch_gate_b']


def kernel(x, c, positions, ada_w, ada_b, norm1_gain, norm2_gain, w_in, tshift_mu, decay_w0, decay_up, iclr_a0, iclr_up, gate_up, k_k, k_a, r_k, lnx_gain, lnx_bias, q_norm_gain, k_norm_gain, attn_sinks, branch_gate_b, w_branch_a, w_branch_b, w_out, ffn_w1, ffn_w3, ffn_w2, loss_target, m_ada_w, m_ada_b, m_norm1_gain, m_norm2_gain, m_w_in, m_tshift_mu, m_decay_w0, m_decay_up, m_iclr_a0, m_iclr_up, m_gate_up, m_k_k, m_k_a, m_r_k, m_lnx_gain, m_lnx_bias, m_q_norm_gain, m_k_norm_gain, m_attn_sinks, m_branch_gate_b, m_w_branch_a, m_w_branch_b, m_w_out, m_ffn_w1, m_ffn_w3, m_ffn_w2, v_ada_w, v_ada_b, v_norm1_gain, v_norm2_gain, v_w_in, v_tshift_mu, v_decay_w0, v_decay_up, v_iclr_a0, v_iclr_up, v_gate_up, v_k_k, v_k_a, v_r_k, v_lnx_gain, v_lnx_bias, v_q_norm_gain, v_k_norm_gain, v_attn_sinks, v_branch_gate_b, v_w_branch_a, v_w_branch_b, v_w_out, v_ffn_w1, v_ffn_w3, v_ffn_w2):
    a = dict(locals())
    W = {n: a[n] for n in WEIGHTS}
    M = {n: a['m_' + n] for n in WEIGHTS}
    V = {n: a['v_' + n] for n in WEIGHTS}
    xi, yi, ci = lax.axis_index("x"), lax.axis_index("y"), lax.axis_index("c")
    me = 4 * xi + 2 * yi + ci
    shard = 2 * xi + yi
    mat = lambda t: t.reshape(t.shape[-2], t.shape[-1])
    sharded = [n for n, _ in SHARDED]

    ax = dict(SHARDED)
    late = LATE
    early = [n for n in sharded if n not in late]
    shards = {n: mat(W[n]).astype(MXU) for n in sharded}
    gathered = _exchange_call(_GatherChipsHalved([shards[n] for n in early]), name="gather_weights")
    full = {n: _full_weight(g, ax[n]) for n, g in zip(early, gathered, strict=True)}

    c_all = _all_gather8(jnp.broadcast_to(c, (8, D)), name="gather_c")[0::8]
    pad_rows = lambda t: jnp.concatenate([t, jnp.zeros((BLK - 8, t.shape[1]), t.dtype)])
    c_all = pad_rows(c_all.astype(MXU))
    ada_cols = _mm_nn(c_all, mat(ada_w).astype(MXU), name="f_ada")[:8]
    ada_all = _all_gather8(ada_cols, name="gather_ada").reshape(2, 2, 2, 8, 6 * D // 4)
    ada_mine = lax.dynamic_index_in_dim(ada_all[:, :, 0], me, axis=2, keepdims=False)
    ada = ada_mine.reshape(1, 6 * D) + mat(ada_b)

    zero = jnp.zeros((64, RW), MXU)
    lora = jnp.concatenate([jnp.concatenate([full['decay_up'], zero], axis=1),
                            jnp.concatenate([zero, full['iclr_up']], axis=1)], axis=0)
    s = {n: W[n].reshape(1, -1) for n in SMALL if n != 'ada_b'}
    s['lora_up'] = lora.astype(F32)
    s['gate_up'] = full['gate_up'].astype(F32)
    tab = _rope_table(positions.reshape(-1))
    loss, dx, d_ada, gw, gs, from_chips = _local_step(x[0], loss_target[0], ada, tab, dict(w_in=full['w_in']), s,
                                                      shards={n: shards[n] for n in late})

    gs['ada_b'] = d_ada
    gsmall = _pack_small([gs[n] for n in SMALL] + [loss], name="pack_small_grads")
    gsmall_all = _all_gather8(gsmall, name="gather_small_grads")
    row = lambda src: [src[n].reshape(1, -1) for n in SMALL]
    sm_out, loss = _adamw_small(row(W), row(M), row(V), gsmall_all, name="adamw_small")
    sm_out = [{n: o.reshape(W[n].shape) for n, o in zip(SMALL, outs_k, strict=True)} for outs_k in sm_out]
    loss = loss[0, 0]

    d_ada_all = gsmall_all[0::SMALL_SLOTS]
    d_ada_cols = lax.dynamic_slice_in_dim(d_ada_all, shard * (6 * D // 4), 6 * D // 4, axis=1)
    g_ada_w = _mm_tn(c_all, pad_rows(d_ada_cols.astype(MXU)), name="b_ada")

    rest = [n for n in sharded if n != 'w_in']
    parts = {n: _sum_parts(lax.dynamic_index_in_dim(gw[n], shard, axis=0, keepdims=False), from_chips[n],
                           name="sum_" + n) for n in rest}
    tail = _Both(_ScatterChips([gw['w_in'][:, 3 * D // 4:]]), _SwapSibling([parts[n] for n in rest]))
    res = _adamw(mat(ada_w), mat(m_ada_w), mat(v_ada_w), [g_ada_w], tm=256, name="adamw_ada", hosted=tail)
    ada_out, last_quarter, others = res[:4], res[4], dict(zip(rest, res[5:], strict=True))
    parts['w_in'] = _sum_parts(lax.dynamic_index_in_dim(gw['w_in'], shard, axis=0, keepdims=False),
                               jnp.concatenate(from_chips['w_in'] + [last_quarter], axis=1), name="sum_w_in")
    others['w_in'] = _exchange_call(_SwapSibling([parts['w_in']]), name="swap_w_in")[0]
    sh_out = {}
    for n in sharded:
        part, other = parts[n], others[n]
        sh_out[n] = _adamw(mat(W[n]), mat(M[n]), mat(V[n]), [part, other], tm=_pick(part.shape[0], (256, 128, 64)),
                           name="adamw_" + n)

    def leaf(k, n):
        if n == 'ada_w':
            return ada_out[k].reshape(W[n].shape)
        if n in sharded:
            return sh_out[n][k].reshape(W[n].shape)
        return sm_out[k][n]
    outs = [leaf(k, n) for k in range(4) for n in WEIGHTS]
    return (loss, dx[None], *outs)
```

```python
import functools
import math

import jax
import jax.numpy as jnp
from jax import lax
from jax.experimental import pallas as pl
from jax.experimental.pallas import tpu as pltpu

F32 = jnp.float32
BF16 = jnp.bfloat16
MXU = BF16
HI = lax.Precision.HIGHEST

D = 1024
HD = 64
NH = 8
RW = NH * HD
SHIFT_W = 3 * RW + 64 + 64 + 128
QKV_W = RW + 2 * 128
GATE_W = 2 * D
IN_W = SHIFT_W + QKV_W + GATE_W
DFF = 2816
BLK = 128
CHUNK = 64
RMS_EPS = 1e-6
GN_EPS = 64e-5
NEG_INF = -1e30
ADAM_LR, ADAM_B1, ADAM_B2, ADAM_EPS, ADAM_WD, ADAM_STEP = 0.001, 0.9, 0.999, 1e-08, 0.01, 10
VMEM_LIMIT = 56 * 1024 * 1024
MESH = pl.DeviceIdType.MESH


def _cparams(sem=None):
    return pltpu.CompilerParams(dimension_semantics=sem, vmem_limit_bytes=VMEM_LIMIT)


def _full_spec(a):
    nd = a.ndim
    return pl.BlockSpec(a.shape, lambda *_: (0,) * nd)


def _rowwise(fn, rows, consts, outs, *, tm, name, halo=(), hosted=None):
    rows = [(a + (0,))[:3] if isinstance(a, tuple) else (a, a.shape[1], 0) for a in rows]
    T = rows[0][0].shape[0]
    assert T % tm == 0 and tm % 8 == 0
    n_tiles = T // tm
    n_in = len(rows) + len(halo) + len(consts)
    in_specs = [pl.BlockSpec((tm, nc), lambda i, j=j: (i, j)) for _, nc, j in rows]
    args = [a for a, _, _ in rows]
    for a, nc, kind in halo:
        if kind == 'prev':
            in_specs.append(pl.BlockSpec((8, nc), lambda i: (jnp.maximum(i * (tm // 8) - 1, 0), 0)))
        else:
            in_specs.append(pl.BlockSpec((8, nc), lambda i: (jnp.minimum((i + 1) * (tm // 8), T // 8 - 1), 0)))
        args.append(a)
    in_specs += [_full_spec(a) for a in consts]
    args += list(consts)
    out_shape, out_specs = [], []
    for ncols, dtype, kind in outs:
        if kind == 'row':
            out_shape.append(jax.ShapeDtypeStruct((T, ncols), dtype))
            out_specs.append(pl.BlockSpec((tm, ncols), lambda i: (i, 0)))
        else:
            out_shape.append(jax.ShapeDtypeStruct((kind, ncols), dtype))
            out_specs.append(pl.BlockSpec((kind, ncols), lambda i: (0, 0)))

    def body(*refs):
        i = pl.program_id(0)
        vals = [r[...] for r in refs[:n_in]]
        res = fn(i, n_tiles, *vals)
        for (ncols, dtype, kind), o_ref, val in zip(outs, refs[n_in:], res, strict=True):
            if kind == 'row':
                o_ref[...] = val.astype(dtype)
            else:
                @pl.when(i == 0)
                def _():
                    o_ref[...] = jnp.zeros_like(o_ref)
                o_ref[...] += val.astype(dtype)

    h_in, h_in_specs, h_out_specs, h_out_shape, h_scratch = _hosted_args(hosted)
    res = pl.pallas_call(
        _hosting(body, hosted, n_in, len(outs), 0, n_tiles), name=name, grid=(n_tiles,),
        in_specs=in_specs + h_in_specs, out_specs=out_specs + h_out_specs, out_shape=out_shape + h_out_shape,
        scratch_shapes=h_scratch, compiler_params=_cparams(("arbitrary",)),
    )(*args, *h_in)
    return res


def _pick(n, cands):
    for c in cands:
        if n % c == 0:
            return c
    return n


MM_ROWS = (1024, 512, 256, 128)
MM_COLS = (1536, 1408, 1024, 896, 768, 512, 256, 128)
MM_WIDE = 3000


def _mm_nn(a, w, *, name, out_dtype=F32, hosted=None):
    T, K = a.shape
    N = w.shape[1]
    tm = _pick(T, MM_ROWS)
    tn = _pick(N, MM_COLS)
    grid = (N // tn, T // tm)
    h_in, h_in_specs, h_out_specs, h_out_shape, h_scratch = _hosted_args(hosted)

    def body(a_ref, w_ref, o_ref):
        o_ref[...] = jnp.dot(a_ref[...], w_ref[...], preferred_element_type=F32).astype(out_dtype)

    res = pl.pallas_call(
        _hosting(body, hosted, 2, 1, 0, grid), name=name, grid=grid,
        in_specs=[pl.BlockSpec((tm, K), lambda j, i: (i, 0)), pl.BlockSpec((K, tn), lambda j, i: (0, j))] + h_in_specs,
        out_specs=[pl.BlockSpec((tm, tn), lambda j, i: (i, j))] + h_out_specs,
        out_shape=[jax.ShapeDtypeStruct((T, N), out_dtype)] + h_out_shape, scratch_shapes=h_scratch,
        compiler_params=_cparams(("arbitrary", "arbitrary")),
    )(a, w, *h_in)
    return res if hosted else res[0]


def _mm_nt(dy, w, *, name, out_dtype=F32, hosted=None):
    T, N = dy.shape
    K = w.shape[0]
    tm = _pick(T, MM_ROWS if N <= MM_WIDE else MM_ROWS[1:])
    tk = _pick(K, MM_COLS[1:])
    grid = (K // tk, T // tm)
    h_in, h_in_specs, h_out_specs, h_out_shape, h_scratch = _hosted_args(hosted)

    def body(dy_ref, w_ref, o_ref):
        o_ref[...] = lax.dot_general(dy_ref[...], w_ref[...], (((1,), (1,)), ((), ())),
                                     preferred_element_type=F32).astype(out_dtype)

    res = pl.pallas_call(
        _hosting(body, hosted, 2, 1, 0, grid), name=name, grid=grid,
        in_specs=[pl.BlockSpec((tm, N), lambda j, i: (i, 0)), pl.BlockSpec((tk, N), lambda j, i: (j, 0))] + h_in_specs,
        out_specs=[pl.BlockSpec((tm, tk), lambda j, i: (i, j))] + h_out_specs,
        out_shape=[jax.ShapeDtypeStruct((T, K), out_dtype)] + h_out_shape, scratch_shapes=h_scratch,
        compiler_params=_cparams(("arbitrary", "arbitrary")),
    )(dy, w, *h_in)
    return res if hosted else res[0]


def _mm_tn(a, dy, *, name, out_dtype=F32, col_shards=None):
    T, K = a.shape
    N = dy.shape[1]
    tm = _pick(T, MM_ROWS)
    tn = N // col_shards if col_shards else _pick(N, MM_COLS[1:])
    n_t = T // tm

    def body(a_ref, dy_ref, o_ref, acc_ref):
        i = pl.program_id(1)

        @pl.when(i == 0)
        def _():
            acc_ref[...] = jnp.zeros_like(acc_ref)

        acc_ref[...] += lax.dot_general(a_ref[...], dy_ref[...], (((0,), (0,)), ((), ())), preferred_element_type=F32)

        @pl.when(i == n_t - 1)
        def _():
            o_ref[...] = acc_ref[...].astype(out_dtype)

    if col_shards:
        out_specs = pl.BlockSpec((None, K, tn), lambda j, i: (j, 0, 0))
        out_shape = jax.ShapeDtypeStruct((col_shards, K, tn), out_dtype)
    else:
        out_specs = pl.BlockSpec((K, tn), lambda j, i: (0, j))
        out_shape = jax.ShapeDtypeStruct((K, N), out_dtype)
    return pl.pallas_call(
        body, name=name, grid=(N // tn, n_t),
        in_specs=[pl.BlockSpec((tm, K), lambda j, i: (i, 0)), pl.BlockSpec((tm, tn), lambda j, i: (i, j))],
        out_specs=out_specs, out_shape=out_shape, scratch_shapes=[pltpu.VMEM((K, tn), F32)],
        compiler_params=_cparams(("arbitrary", "arbitrary")),
    )(a, dy)


def _mm_then(products, fn, rows, consts, outs, *, tm, name, lhs_fn=None):
    T = (rows[0] if lhs_fn else products[0][0]).shape[0]
    n_tiles = T // tm
    in_specs, args = [], []
    for a, w, _ in products:
        if a is not None:
            in_specs.append(pl.BlockSpec((tm, a.shape[1]), lambda i: (i, 0)))
            args.append(a)
        in_specs.append(_full_spec(w))
        args.append(w)
    n_w = len(args)
    in_specs += [pl.BlockSpec((tm, a.shape[1]), lambda i: (i, 0)) for a in rows] + [_full_spec(c_) for c_ in consts]
    args += list(rows) + list(consts)
    n_in = len(args)
    out_shape, out_specs = [], []
    for ncols, dtype, kind in outs:
        if kind == 'row':
            out_shape.append(jax.ShapeDtypeStruct((T, ncols), dtype))
            out_specs.append(pl.BlockSpec((tm, ncols), lambda i: (i, 0)))
        else:
            out_shape.append(jax.ShapeDtypeStruct((kind, ncols), dtype))
            out_specs.append(pl.BlockSpec((kind, ncols), lambda i: (0, 0)))

    def body(*refs):
        i = pl.program_id(0)
        tiles = [r[...] for r in refs[n_w:n_in]]
        made = [lhs_fn(*tiles).astype(MXU)] if lhs_fn else []
        y, pos = None, 0
        for a, _, form in products:
            if a is None:
                lhs = made[0]
            else:
                lhs, pos = refs[pos][...], pos + 1
            dims = (((1,), (0,)), ((), ())) if form == 'nn' else (((1,), (1,)), ((), ()))
            t = lax.dot_general(lhs, refs[pos][...], dims, preferred_element_type=F32)
            pos += 1
            y = t if y is None else y + t
        res = fn(i, n_tiles, y, *made, *tiles)
        for (ncols, dtype, kind), o_ref, val in zip(outs, refs[n_in:], res, strict=True):
            if kind == 'row':
                o_ref[...] = val.astype(dtype)
            else:
                @pl.when(i == 0)
                def _():
                    o_ref[...] = jnp.zeros_like(o_ref)
                o_ref[...] += val.astype(dtype)

    return pl.pallas_call(body, name=name, grid=(n_tiles,), in_specs=in_specs, out_specs=out_specs,
                          out_shape=out_shape, compiler_params=_cparams(("arbitrary",)))(*args)


def _seg_ones(n):
    r = lax.broadcasted_iota(jnp.int32, (n, n), 0) // HD
    c = lax.broadcasted_iota(jnp.int32, (n, n), 1) // HD
    return (r == c).astype(F32)


def _segsum_raw(x):
    ones = _seg_ones(x.shape[1])
    if MXU == F32:
        return jnp.dot(x, ones, precision=HI, preferred_element_type=F32)
    hi = x.astype(MXU)
    lo = (x - hi.astype(F32)).astype(MXU)
    ones = ones.astype(MXU)
    return jnp.dot(hi, ones, preferred_element_type=F32) + jnp.dot(lo, ones, preferred_element_type=F32)


@jax.custom_vjp
def _segsum(x):
    return _segsum_raw(x)


def _segsum_fwd(x):
    return _segsum_raw(x), None


def _segsum_bwd(_, g):
    return (_segsum_raw(g),)


_segsum.defvjp(_segsum_fwd, _segsum_bwd)


def _mxu(x):
    return x.astype(MXU)


@jax.custom_vjp
def _bdot(a, b):
    return jnp.dot(_mxu(a), _mxu(b), preferred_element_type=F32)


def _bdot_fwd(a, b):
    return _bdot(a, b), (a, b)


def _bdot_bwd(res, g):
    a, b = res
    da = lax.dot_general(_mxu(g), _mxu(b), (((1,), (1,)), ((), ())), preferred_element_type=F32)
    db = lax.dot_general(_mxu(a), _mxu(g), (((0,), (0,)), ((), ())), preferred_element_type=F32)
    return da.astype(a.dtype), db.astype(b.dtype)


_bdot.defvjp(_bdot_fwd, _bdot_bwd)


def _sigmoid(x):
    return 1.0 / (1.0 + jnp.exp(-x))


def _softplus(x):
    return jnp.maximum(x, 0.0) + jnp.log(1.0 + jnp.exp(jnp.minimum(x, -x)))


def _norm_mod(x, gain, scale, shift):
    inv = lax.rsqrt(jnp.mean(x * x, axis=-1, keepdims=True) + RMS_EPS)
    return (x * inv) * gain * (1.0 + scale) + shift


def _prep(mixed, decay_w0, lora_up, iclr_a0, gate_up, k_k, k_a):
    r = mixed[:, 0:RW]
    k = mixed[:, RW:2 * RW]
    v = mixed[:, 2 * RW:3 * RW]
    z = mixed[:, 3 * RW:3 * RW + 128]
    xg = mixed[:, 3 * RW + 128:]
    lane = lax.broadcasted_iota(jnp.int32, z.shape, 1)
    tz = jnp.where(lane < 64, jnp.tanh(z), z)
    lo = _bdot(tz, lora_up)
    w_log = -_softplus(-(decay_w0 + lo[:, :RW])) - 0.5
    lw = -jnp.exp(w_log)
    a_ic = _sigmoid(iclr_a0 + lo[:, RW:])
    g = _bdot(_sigmoid(xg), gate_up)
    kk = k * k_k
    kk = kk / jnp.maximum(jnp.sqrt(_segsum(kk * kk)), 1e-12)
    k_mod = k * (1.0 + (a_ic - 1.0) * k_a)
    return jnp.concatenate([r, lw, k_mod, v, -kk, kk * a_ic, g], axis=1)


def _post(y, r, k, v, g, lnx_gain, lnx_bias, r_k):
    mu = _segsum(y) * (1.0 / HD)
    yc = y - mu
    var = _segsum(yc * yc) * (1.0 / HD)
    yn = yc * lax.rsqrt(var + GN_EPS) * lnx_gain + lnx_bias
    bonus = _segsum(r * k * r_k) * v
    return (yn + bonus) * g


def _merge(pg, ma, mb, bias):
    gates = _sigmoid(pg + bias)
    return gates[:, :D] * ma + gates[:, D:] * mb


def _swiglu(u, v):
    return u * _sigmoid(u) * v


def _ffn_in(h, w1, w3, *, name):
    T, K = h.shape
    F = w1.shape[1]
    tm = _pick(T, MM_ROWS)
    tn = _pick(F, MM_COLS[1:])

    def body(h_ref, w1_ref, w3_ref, u_ref, v_ref, a_ref):
        u = jnp.dot(h_ref[...], w1_ref[...], preferred_element_type=F32).astype(MXU)
        v = jnp.dot(h_ref[...], w3_ref[...], preferred_element_type=F32).astype(MXU)
        u_ref[...] = u
        v_ref[...] = v
        a_ref[...] = _swiglu(u.astype(F32), v.astype(F32)).astype(MXU)

    wspec = pl.BlockSpec((K, tn), lambda j, i: (0, j))
    ospec = pl.BlockSpec((tm, tn), lambda j, i: (i, j))
    return pl.pallas_call(
        body, name=name, grid=(F // tn, T // tm),
        in_specs=[pl.BlockSpec((tm, K), lambda j, i: (i, 0)), wspec, wspec],
        out_specs=[ospec] * 3, out_shape=[jax.ShapeDtypeStruct((T, F), MXU)] * 3,
        compiler_params=_cparams(("arbitrary", "arbitrary")),
    )(h, w1, w3)


def _ffn_act_bwd(dff, w2, u, v, *, name):
    T, N = dff.shape
    F = w2.shape[0]
    tm = _pick(T, MM_ROWS)
    tk = _pick(F, MM_COLS[1:])

    def body(dy_ref, w_ref, u_ref, v_ref, du_ref, dv_ref):
        dact = lax.dot_general(dy_ref[...], w_ref[...], (((1,), (1,)), ((), ())), preferred_element_type=F32)
        _, vjp = jax.vjp(_swiglu, u_ref[...].astype(F32), v_ref[...].astype(F32))
        du, dv = vjp(dact)
        du_ref[...] = du.astype(MXU)
        dv_ref[...] = dv.astype(MXU)

    tile = pl.BlockSpec((tm, tk), lambda j, i: (i, j))
    return pl.pallas_call(
        body, name=name, grid=(F // tk, T // tm),
        in_specs=[pl.BlockSpec((tm, N), lambda j, i: (i, 0)), pl.BlockSpec((tk, N), lambda j, i: (j, 0)), tile, tile],
        out_specs=[tile, tile], out_shape=[jax.ShapeDtypeStruct((T, F), MXU)] * 2,
        compiler_params=_cparams(("arbitrary", "arbitrary")),
    )(dff, w2, u, v)


def _mm_nt2(dy1, w1, dy2, w2, *, name):
    T, N = dy1.shape
    K = w1.shape[0]
    tm = _pick(T, MM_ROWS[1:])
    tk = _pick(K, MM_COLS[1:])

    def body(d1_ref, w1_ref, d2_ref, w2_ref, o_ref):
        nt = lambda a, b: lax.dot_general(a[...], b[...], (((1,), (1,)), ((), ())), preferred_element_type=F32)
        o_ref[...] = nt(d1_ref, w1_ref) + nt(d2_ref, w2_ref)

    dspec = pl.BlockSpec((tm, N), lambda j, i: (i, 0))
    wspec = pl.BlockSpec((tk, N), lambda j, i: (j, 0))
    return pl.pallas_call(
        body, name=name, grid=(K // tk, T // tm), in_specs=[dspec, wspec, dspec, wspec],
        out_specs=pl.BlockSpec((tm, tk), lambda j, i: (i, j)), out_shape=jax.ShapeDtypeStruct((T, K), F32),
        compiler_params=_cparams(("arbitrary", "arbitrary")),
    )(dy1, w1, dy2, w2)


@functools.partial(jax.custom_vjp, nondiff_argnums=(1,))
def _lane_roll(x, s):
    return pltpu.roll(x, s, 1)


def _lane_roll_fwd(x, s):
    return pltpu.roll(x, s, 1), None


def _lane_roll_bwd(s, _, g):
    n = g.shape[1]
    return (pltpu.roll(g, (n - s) % n, 1),)


_lane_roll.defvjp(_lane_roll_fwd, _lane_roll_bwd)


def _rope(x, cos, sin_lo, sin_hi):
    n = x.shape[1]
    return x * cos + _lane_roll(x, n - 8) * sin_lo + _lane_roll(x, 8) * sin_hi


def _head_rms(x, gain):
    return x * lax.rsqrt(_segsum(x * x) * (1.0 / HD) + RMS_EPS) * gain


def _attn_blocks(qkv_c, qkv_p, tab_c, tab_p, qg, kg, sinks, first):
    nb = qkv_c.shape[0] // BLK
    G = 4

    def tabs(tab, n):
        return [jnp.tile(tab[:, j * 128:(j + 1) * 128], (1, n // 128)) for j in range(3)]

    qg = jnp.concatenate([qg] * NH, axis=1)
    kg = jnp.concatenate([kg] * 2, axis=1)
    q = _rope(_head_rms(qkv_c[:, :RW], qg), *tabs(tab_c, RW))
    k_in = jnp.concatenate([qkv_p[:, RW:RW + 128], qkv_c[:, RW:RW + 128]], axis=0)
    k = _rope(_head_rms(k_in, kg), *tabs(jnp.concatenate([tab_p, tab_c], axis=0), 128))
    v = jnp.concatenate([qkv_p[:, RW + 128:], qkv_c[:, RW + 128:]], axis=0)

    pile = lambda xs: jnp.concatenate([x_[None] for x_ in xs], axis=0)

    def bands(t):
        return pile([t[b * BLK:(b + 2) * BLK, kvh * HD:(kvh + 1) * HD] for kvh in range(2) for b in range(nb)])

    qs = pile([jnp.concatenate([q[b * BLK:(b + 1) * BLK, (G * kvh + g) * HD:(G * kvh + g + 1) * HD]
                                for g in range(G)], axis=0) for kvh in range(2) for b in range(nb)])
    s = _bmm(qs, bands(k), 2, 2, 1) * (HD ** -0.5)
    qi = lax.broadcasted_iota(jnp.int32, (G * BLK, 2 * BLK), 0) % BLK
    kj = lax.broadcasted_iota(jnp.int32, (G * BLK, 2 * BLK), 1)
    dist = qi + BLK - kj
    in_band = (dist >= 0) & (dist < BLK)
    pair = lax.broadcasted_iota(jnp.int32, (2 * nb, 1, 1), 0)
    no_prev = (pair % nb == 0) & first
    valid = in_band[None] & (jnp.logical_not(no_prev) | (kj >= BLK)[None])
    s = jnp.where(valid, s, NEG_INF)
    row_g = lax.broadcasted_iota(jnp.int32, (G * BLK, 1), 0) // BLK
    sink = []
    for kvh in range(2):
        col = jnp.zeros((G * BLK, 1), F32)
        for g in range(G):
            col = jnp.where(row_g == g, sinks[:, G * kvh + g:G * kvh + g + 1], col)
        sink += [col] * nb
    sink = pile(sink)
    m = lax.stop_gradient(jnp.maximum(jnp.max(s, axis=-1, keepdims=True), sink))
    e = jnp.exp(s - m)
    p = e * (1.0 / (jnp.sum(e, axis=-1, keepdims=True) + jnp.exp(sink - m)))
    o = _bmm(p, bands(v), 2, 1, 1)
    return jnp.concatenate([jnp.concatenate([o[kvh * nb + b, g * BLK:(g + 1) * BLK] for kvh in range(2)
                                             for g in range(G)], axis=1) for b in range(nb)], axis=0)


def _heads(x):
    return jnp.stack([x[:, h * HD:(h + 1) * HD] for h in range(NH)], axis=0)


def _unheads(x):
    return jnp.concatenate([x[h] for h in range(NH)], axis=1)


def _split(x, n):
    parts, rest = [], x
    for _ in range(n):
        p = rest.astype(MXU)
        parts.append(p)
        rest = rest - p.astype(F32)
    return parts


def _bdot_batched(a, b, ca, cb):
    return lax.dot_general(a, b, (((ca,), (cb,)), ((0,), (0,))), preferred_element_type=F32)


def _bmm_passes(a, b, ca, cb, passes):
    if MXU == F32:
        return lax.dot_general(a, b, (((ca,), (cb,)), ((0,), (0,))), precision=HI, preferred_element_type=F32)
    if passes == 1:
        return _bdot_batched(a.astype(MXU), b.astype(MXU), ca, cb)
    (a0, a1), (b0, b1) = _split(a, 2), _split(b, 2)
    return _bdot_batched(a0, b0, ca, cb) + (_bdot_batched(a0, b1, ca, cb) + _bdot_batched(a1, b0, ca, cb))


@functools.partial(jax.custom_vjp, nondiff_argnums=(2, 3, 4))
def _bmm(a, b, ca, cb, passes=1):
    return _bmm_passes(a, b, ca, cb, passes)


def _bmm_fwd(a, b, ca, cb, passes):
    return _bmm_passes(a, b, ca, cb, passes), (a, b)


def _bmm_bwd(ca, cb, passes, res, g):
    a, b = res
    if (ca, cb) == (2, 1):
        return _bmm_passes(g, b, 2, 2, passes), _bmm_passes(a, g, 1, 1, passes)
    if (ca, cb) == (2, 2):
        return _bmm_passes(g, b, 2, 1, passes), _bmm_passes(g, a, 1, 1, passes)
    return _bmm_passes(b, g, 2, 2, passes), _bmm_passes(a, g, 2, 1, passes)


_bmm.defvjp(_bmm_fwd, _bmm_bwd)


def _tri_dot(x, transpose):
    C = x.shape[1]
    ri = lax.broadcasted_iota(jnp.int32, (C, C), 0)
    ci = lax.broadcasted_iota(jnp.int32, (C, C), 1)
    tri = jnp.broadcast_to(((ri <= ci) if transpose else (ri >= ci)).astype(MXU), (x.shape[0], C, C))
    if MXU == F32:
        return lax.dot_general(tri, x, (((2,), (1,)), ((0,), (0,))), precision=HI, preferred_element_type=F32)
    p0, p1, p2 = _split(x, 3)
    return _bdot_batched(tri, p0, 2, 1) + (_bdot_batched(tri, p1, 2, 1) + _bdot_batched(tri, p2, 2, 1))


@jax.custom_vjp
def _cumsum_rows(x):
    return _tri_dot(x, False)


def _cumsum_rows_fwd(x):
    return _tri_dot(x, False), None


def _cumsum_rows_bwd(_, g):
    return (_tri_dot(g, True),)


_cumsum_rows.defvjp(_cumsum_rows_fwd, _cumsum_rows_bwd)

P_SCORE = 1
P_SOLVE = 1
P_STATE = 1
SCAN_CHUNKS = (4, 2, 1)


def _neumann(l):
    C = l.shape[1]
    eye = (lax.broadcasted_iota(jnp.int32, (C, C), 0) == lax.broadcasted_iota(jnp.int32, (C, C), 1)).astype(F32)
    x, lp = eye + l, l
    for _ in range(int(math.log2(C)) - 1):
        lp = _bmm(lp, lp, 2, 1, P_SOLVE)
        x = x + _bmm(x, lp, 2, 1, P_SOLVE)
    return x


@jax.custom_vjp
def _unit_lower_inverse(l):
    return _neumann(l)


def _unit_lower_inverse_fwd(l):
    x = _neumann(l)
    return x, x


def _unit_lower_inverse_bwd(x, g):
    return (_bmm(_bmm(x, g, 1, 1, P_SOLVE), x, 2, 2, P_SOLVE),)


_unit_lower_inverse.defvjp(_unit_lower_inverse_fwd, _unit_lower_inverse_bwd)


def _known_inverse(x):
    @jax.custom_vjp
    def f(l):
        return x

    f.defvjp(lambda l: (x, None), lambda _, g: (_bmm(_bmm(x, g, 1, 1, P_SOLVE), x, 2, 2, P_SOLVE),))
    return f


def _chunk(S0, r, lw, k, v, a, b, inverse=None):
    C = CHUNK
    n = r.shape[1] // C
    fold = lambda t: t.reshape(NH * n, C, HD)
    r, lw, k, v, a, b = (fold(t) for t in (r, lw, k, v, a, b))
    ri = lax.broadcasted_iota(jnp.int32, (C, C), 0)
    ci = lax.broadcasted_iota(jnp.int32, (C, C), 1)
    incl = (ri >= ci)
    strict = (ri > ci)
    eye = (ri == ci).astype(F32)
    cum = _cumsum_rows(lw)
    p_in = jnp.exp(cum)
    p_ex = jnp.exp(cum - lw)
    p_inv = jnp.exp(-cum)
    at, rt, bt, kt = a * p_ex, r * p_in, b * p_inv, k * p_inv
    lhs = jnp.concatenate([at, rt], axis=1)
    rhs_ = jnp.concatenate([bt, kt], axis=1)
    sc = _bmm(lhs, rhs_, 2, 2, P_SCORE)
    a_ab = jnp.where(strict, sc[:, :C, :C], 0.0)
    a_ak = jnp.where(strict, sc[:, :C, C:], 0.0)
    incl2 = (lax.broadcasted_iota(jnp.int32, (C, 2 * C), 0) >= lax.broadcasted_iota(jnp.int32, (C, 2 * C), 1) % C)
    a_r = jnp.where(incl2, sc[:, C:, :], 0.0)
    av = _bmm(a_ak, v, 2, 1, P_SCORE)
    x = x_all = (_unit_lower_inverse if inverse is None else _known_inverse(inverse))(a_ab)
    p_last = jnp.exp(cum[:, C - 1:C, :])
    per_chunk = lambda t: t.reshape((NH, n) + t.shape[1:])
    lhs, rhs_, a_r, av, x, v, p_last = (per_chunk(t) for t in (lhs, rhs_, a_r, av, x, v, p_last))
    S, ys = S0, []
    for c in range(n):
        s0 = _bmm(lhs[:, c], S, 2, 2, P_STATE)
        u = _bmm(x[:, c], s0[:, :C] + av[:, c], 2, 1, P_SOLVE)
        uv = jnp.concatenate([u, v[:, c]], axis=1)
        ys.append(s0[:, C:] + _bmm(a_r[:, c], uv, 2, 1, P_SCORE))
        S = (S + _bmm(uv, rhs_[:, c], 1, 1, P_STATE)) * p_last[:, c]
    return jnp.concatenate(ys, axis=1), S, x_all


def _hosting(body, ex, n_in, n_out, n_scratch, n_steps):
    if ex is None:
        return body

    def wrapped(*refs):
        a = n_in
        b = a + ex.n_in
        c = b + n_out
        d = c + ex.n_out
        e = d + n_scratch
        ex_refs = (refs[a:b], refs[c:d], refs[e:])
        grid = n_steps if isinstance(n_steps, tuple) else (n_steps,)
        first = last = True
        for ax_, size in enumerate(grid):
            first = first & (pl.program_id(ax_) == 0)
            last = last & (pl.program_id(ax_) == size - 1)

        @pl.when(first)
        def _():
            ex.start(*ex_refs)

        body(*refs[:a], *refs[b:c], *refs[d:e])

        @pl.when(last)
        def _():
            ex.wait(*ex_refs)

    return wrapped


def _hosted_args(ex):
    if ex is None:
        return [], [], [], [], []
    any_spec = pl.BlockSpec(memory_space=pl.ANY)
    return list(ex.arrays), [any_spec] * ex.n_in, [any_spec] * ex.n_out, list(ex.out_shape), list(ex.scratch)


def _scan_fwd(rw, *, name, hosted=None):
    T = rw.shape[0]
    nc = _pick(T // CHUNK, SCAN_CHUNKS)
    rows = CHUNK * nc
    n = T // rows
    h_in, h_in_specs, h_out_specs, h_out_shape, h_scratch = _hosted_args(hosted)

    def body(r_ref, lw_ref, k_ref, v_ref, a_ref, b_ref, y_ref, ck_ref, inv_ref, s_ref):
        @pl.when(pl.program_id(0) == 0)
        def _():
            s_ref[...] = jnp.zeros_like(s_ref)

        S0 = s_ref[...]
        ck_ref[0] = S0
        y, S1, inv = _chunk(S0, *[_heads(ref[...]) for ref in (r_ref, lw_ref, k_ref, v_ref, a_ref, b_ref)])
        y_ref[...] = _unheads(y)
        inv_ref[0] = inv
        s_ref[...] = S1

    col = lambda j: pl.BlockSpec((rows, RW), lambda i: (i, j))
    return pl.pallas_call(
        _hosting(body, hosted, 6, 3, 1, n), name=name, grid=(n,),
        in_specs=[col(j) for j in range(6)] + h_in_specs,
        out_specs=[pl.BlockSpec((rows, RW), lambda i: (i, 0)),
                   pl.BlockSpec((1, NH, HD, HD), lambda i: (i, 0, 0, 0)),
                   pl.BlockSpec((1, NH * nc, CHUNK, CHUNK), lambda i: (i, 0, 0, 0))] + h_out_specs,
        out_shape=[jax.ShapeDtypeStruct((T, RW), F32), jax.ShapeDtypeStruct((n, NH, HD, HD), F32),
                   jax.ShapeDtypeStruct((n, NH * nc, CHUNK, CHUNK), F32)] + h_out_shape,
        scratch_shapes=[pltpu.VMEM((NH, HD, HD), F32)] + h_scratch,
        compiler_params=_cparams(("arbitrary",)),
    )(rw, rw, rw, rw, rw, rw, *h_in)


def _scan_bwd(rw, ck, inv, dy, *, name, hosted=None):
    T = rw.shape[0]
    nc = _pick(T // CHUNK, SCAN_CHUNKS)
    rows = CHUNK * nc
    n = T // rows

    def body(r_ref, lw_ref, k_ref, v_ref, a_ref, b_ref, ck_ref, inv_ref, dy_ref, o_ref, ds_ref):
        @pl.when(pl.program_id(0) == 0)
        def _():
            ds_ref[...] = jnp.zeros_like(ds_ref)

        prim = [_heads(ref[...]) for ref in (r_ref, lw_ref, k_ref, v_ref, a_ref, b_ref)]
        known = inv_ref[0]
        _, vjp = jax.vjp(lambda S0, *p: _chunk(S0, *p, inverse=known)[:2], ck_ref[0], *prim)
        grads = vjp((_heads(dy_ref[...]), ds_ref[...]))
        ds_ref[...] = grads[0]
        o_ref[...] = jnp.concatenate([_unheads(g) for g in grads[1:]], axis=1).astype(o_ref.dtype)

    h_in, h_in_specs, h_out_specs, h_out_shape, h_scratch = _hosted_args(hosted)
    col = lambda j: pl.BlockSpec((rows, RW), lambda i: (n - 1 - i, j))
    return pl.pallas_call(
        _hosting(body, hosted, 9, 1, 1, n), name=name, grid=(n,),
        in_specs=[col(j) for j in range(6)] + [pl.BlockSpec((1, NH, HD, HD), lambda i: (n - 1 - i, 0, 0, 0)),
                                               pl.BlockSpec((1, NH * nc, CHUNK, CHUNK), lambda i: (n - 1 - i, 0, 0, 0)),
                                               pl.BlockSpec((rows, RW), lambda i: (n - 1 - i, 0))] + h_in_specs,
        out_specs=[pl.BlockSpec((rows, 6 * RW), lambda i: (n - 1 - i, 0))] + h_out_specs,
        out_shape=[jax.ShapeDtypeStruct((T, 6 * RW), MXU)] + h_out_shape,
        scratch_shapes=[pltpu.VMEM((NH, HD, HD), F32)] + h_scratch,
        compiler_params=_cparams(("arbitrary",)),
    )(rw, rw, rw, rw, rw, rw, ck, inv, dy, *h_in)


ATTN_BLOCKS = (4, 2, 1)

def _attn_fwd(qkv, tab, qg, kg, sinks, *, name, hosted=None):
    T = qkv.shape[0]
    nb = _pick(T // BLK, ATTN_BLOCKS)
    n = T // (BLK * nb)
    h_in, h_in_specs, h_out_specs, h_out_shape, h_scratch = _hosted_args(hosted)

    def body(c_ref, p_ref, tc_ref, tp_ref, qg_ref, kg_ref, s_ref, o_ref):
        o_ref[...] = _attn_blocks(c_ref[...], p_ref[...], tc_ref[...], tp_ref[...], qg_ref[...], kg_ref[...],
                                  s_ref[...], pl.program_id(0) == 0).astype(o_ref.dtype)

    cur = lambda w: pl.BlockSpec((nb * BLK, w), lambda i: (i, 0))
    prev = lambda w: pl.BlockSpec((BLK, w), lambda i: (jnp.maximum(i * nb - 1, 0), 0))
    return pl.pallas_call(
        _hosting(body, hosted, 7, 1, 0, n), name=name, grid=(n,),
        in_specs=[cur(QKV_W), prev(QKV_W), cur(3 * 128), prev(3 * 128), _full_spec(qg), _full_spec(kg),
                  _full_spec(sinks)] + h_in_specs,
        out_specs=[cur(RW)] + h_out_specs, out_shape=[jax.ShapeDtypeStruct((T, RW), MXU)] + h_out_shape,
        scratch_shapes=h_scratch,
        compiler_params=_cparams(("arbitrary",)),
    )(qkv, qkv, tab, tab, qg, kg, sinks, *h_in)


def _attn_bwd(qkv, tab, qg, kg, sinks, dy, *, name, hosted=None):
    T = qkv.shape[0]
    nb = _pick(T // BLK, ATTN_BLOCKS)
    n = T // (BLK * nb)
    h_in, h_in_specs, h_out_specs, h_out_shape, h_scratch = _hosted_args(hosted)

    def body(c_ref, p_ref, tc_ref, tp_ref, qg_ref, kg_ref, s_ref, dy_ref, dqkv_ref, dqg_ref, dkg_ref, ds_ref, carry_ref):
        i = pl.program_id(0)

        @pl.when(i == 0)
        def _():
            carry_ref[...] = jnp.zeros_like(carry_ref)
            dqg_ref[...] = jnp.zeros_like(dqg_ref)
            dkg_ref[...] = jnp.zeros_like(dkg_ref)
            ds_ref[...] = jnp.zeros_like(ds_ref)

        tc, tp = tc_ref[...], tp_ref[...]
        f = lambda c, p_, qg_, kg_, sk: _attn_blocks(c, p_, tc, tp, qg_, kg_, sk, i == n - 1)
        _, vjp = jax.vjp(f, c_ref[...], p_ref[...], qg_ref[...], kg_ref[...], s_ref[...])
        dc, dp, dqg, dkg, dsk = vjp(dy_ref[...].astype(F32))
        last = slice((nb - 1) * BLK, nb * BLK)
        dqkv_ref[...] = dc.astype(dqkv_ref.dtype)
        dqkv_ref[last, :] = (dc[last] + carry_ref[...]).astype(dqkv_ref.dtype)
        carry_ref[...] = dp
        dqg_ref[...] += dqg
        dkg_ref[...] += dkg
        ds_ref[...] += dsk

    cur = lambda w: pl.BlockSpec((nb * BLK, w), lambda i: (n - 1 - i, 0))
    prev = lambda w: pl.BlockSpec((BLK, w), lambda i: (jnp.maximum((n - 1 - i) * nb - 1, 0), 0))
    return pl.pallas_call(
        _hosting(body, hosted, 8, 4, 1, n), name=name, grid=(n,),
        in_specs=[cur(QKV_W), prev(QKV_W), cur(3 * 128), prev(3 * 128), _full_spec(qg), _full_spec(kg), _full_spec(sinks),
                  cur(RW)] + h_in_specs,
        out_specs=[cur(QKV_W), _full_spec(qg), _full_spec(kg), _full_spec(sinks)] + h_out_specs,
        out_shape=[jax.ShapeDtypeStruct((T, QKV_W), MXU), jax.ShapeDtypeStruct(qg.shape, F32),
                   jax.ShapeDtypeStruct(kg.shape, F32), jax.ShapeDtypeStruct(sinks.shape, F32)] + h_out_shape,
        scratch_shapes=[pltpu.VMEM((BLK, QKV_W), F32)] + h_scratch,
        compiler_params=_cparams(("arbitrary",)),
    )(qkv, qkv, tab, tab, qg, kg, sinks, dy, *h_in)


def _shift_down(cur, prev8, i):
    rolled = pltpu.roll(cur, 1, 0)
    first_row = jnp.where(i > 0, prev8[7:8, :], 0.0)
    row = lax.broadcasted_iota(jnp.int32, cur.shape, 0)
    return jnp.where(row == 0, first_row, rolled)


def _shift_up(cur, next8, i, n):
    tm = cur.shape[0]
    rolled = pltpu.roll(cur, tm - 1, 0)
    last_row = jnp.where(i < n - 1, next8[0:1, :], 0.0)
    row = lax.broadcasted_iota(jnp.int32, cur.shape, 0)
    return jnp.where(row == tm - 1, last_row, rolled)


def _ada_parts(ada):
    return [ada[:, j * D:(j + 1) * D] for j in range(6)]


def _rope_table(positions):
    half = HD // 8
    inv_freq = 500000.0 ** (-jnp.arange(half, dtype=F32) / half)
    lane = jnp.arange(128) % HD
    rotary = lane < 2 * half
    freq = jnp.where(rotary, inv_freq[lane % half], 0.0)
    ang = positions.astype(F32)[:, None] * freq[None, :]
    cos, sin = jnp.cos(ang), jnp.sin(ang)
    return jnp.concatenate([jnp.where(rotary, cos, 1.0), jnp.where(lane < half, -sin, 0.0),
                            jnp.where(rotary & (lane >= half), sin, 0.0)], axis=1)


GATHER_BEHIND = {"f_proj_shift": [('ffn_w3', 512, 768)], "f_proj_gates": [('ffn_w3', 768, 1024)],
                 "f_prep": [('ffn_w2', 352, 704)],
                 "f_scan": [('ffn_w1', 0, 1024), ('w_branch_a', 0, 512), ('w_branch_b', 0, 512)],
                 "f_post": [('ffn_w3', 0, 512)], "f_attn": [('ffn_w2', 0, 352), ('w_out', 0, 256)]}
LATE = ['w_out', 'w_branch_a', 'w_branch_b', 'ffn_w1', 'ffn_w3', 'ffn_w2']
BACK_ATTN = ['w_out', 'w_branch_a', 'w_branch_b', 'ffn_w2']
BACK_SCAN = ['ffn_w1', 'ffn_w3']
BACK_LAST = ['w_in', 'decay_up', 'iclr_up', 'gate_up']


def _full_weight(g, ax):
    return g.reshape(-1, g.shape[2]) if ax == 0 else jnp.concatenate([g[j] for j in range(4)], axis=1)


def _local_step(x, target, ada, tab, w, s, shards=None):
    T = x.shape[0]
    tm = _pick(T, (512, 256, 128))
    tm_wide = _pick(T, (256, 128))
    tm_vjp = _pick(T, (256, 128))
    row = lambda n, dt=F32: (n, dt, 'row')
    acc = lambda n, r=1: (n, F32, r)

    def f_norm1(i, n, x_, g, ada_):
        sh, sc = ada_[:, 0:D], ada_[:, D:2 * D]
        return (_norm_mod(x_, g, sc, sh),)
    (h1,) = _rowwise(f_norm1, [x], [s['norm1_gain'], ada], [row(D, MXU)], tm=tm, name="f_norm1")

    landed = {}

    def behind(kernel_name):
        if not shards:
            return None
        return _GatherChips([shards[n] if hi - lo == shards[n].shape[0] else shards[n][lo:hi]
                             for n, lo, hi in GATHER_BEHIND[kernel_name]])

    def took(kernel_name, got):
        landed.update(zip(GATHER_BEHIND[kernel_name], got))

    def mm_behind(a_, w_, kernel_name, **kw):
        ex = behind(kernel_name)
        res = _mm_nn(a_, w_, name=kernel_name, hosted=ex, **kw)
        if ex:
            took(kernel_name, res[1:])
            return res[0]
        return res

    proj = mm_behind(h1, w['w_in'][:, :SHIFT_W], "f_proj_shift")
    proj_qkv = _mm_nn(h1, w['w_in'][:, SHIFT_W:SHIFT_W + QKV_W], name="f_proj_qkv")
    proj_g = mm_behind(h1, w['w_in'][:, SHIFT_W + QKV_W:], "f_proj_gates", out_dtype=MXU)
    prep_consts = [s['decay_w0'], s['lora_up'], s['iclr_a0'], s['gate_up'], s['k_k'], s['k_a']]

    def f_prep(i, n, cur, prev8, mu, *params):
        mixed = cur + (_shift_down(cur, prev8, i) - cur) * mu
        return (_prep(mixed, *params),)
    rw, *got = _rowwise(f_prep, [(proj, SHIFT_W)], [s['tshift_mu']] + prep_consts, [row(7 * RW)], tm=tm_wide,
                        name="f_prep", halo=[(proj, SHIFT_W, 'prev')], hosted=behind("f_prep"))
    took("f_prep", got)
    y, ck, inv, *got = _scan_fwd(rw, name="f_scan", hosted=behind("f_scan"))
    took("f_scan", got)
    post_consts = [s['lnx_gain'], s['lnx_bias'], s['r_k']]

    rkvg = [(rw, RW, j) for j in (0, 2, 3, 6)]

    def f_post(i, n, *args):
        return (_post(*args),)
    ya, *got = _rowwise(f_post, [y] + rkvg, post_consts, [row(RW, MXU)], tm=tm_wide, name="f_post",
                        hosted=behind("f_post"))
    took("f_post", got)
    yb, *got = _attn_fwd(proj_qkv, tab, s['q_norm_gain'], s['k_norm_gain'], s['attn_sinks'], name="f_attn",
                         hosted=behind("f_attn"))
    took("f_attn", got)
    w = dict(w)
    if shards:
        ax = dict(SHARDED)
        for n in LATE:
            rows = [landed[key] for key in sorted(k_ for k_ in landed if k_[0] == n)]
            w[n] = _full_weight(rows[0] if len(rows) == 1 else jnp.concatenate(rows, axis=1), ax[n])
    ma = _mm_nn(ya, w['w_branch_a'], name="f_branch_a", out_dtype=MXU)
    mb = _mm_nn(yb, w['w_branch_b'], name="f_branch_b", out_dtype=MXU)

    def f_merge(pg, ma_, mb_, x_, bias, g, ada_):
        return _merge(pg.astype(F32), ma_.astype(F32), mb_.astype(F32), bias)

    def f_res1(i, n, mo_, merged_, pg, ma_, mb_, x_, bias, g, ada_):
        x1_ = x_ + ada_[:, 2 * D:3 * D] * mo_
        return merged_, mo_, x1_, _norm_mod(x1_, g, ada_[:, 4 * D:5 * D], ada_[:, 3 * D:4 * D])
    merged, mo, x1, h2 = _mm_then([(None, w['w_out'], 'nn')], f_res1, [proj_g, ma, mb, x],
                                  [s['branch_gate_b'], s['norm2_gain'], ada],
                                  [row(D, MXU), row(D), row(D), row(D, MXU)], tm=tm, name="f_out", lhs_fn=f_merge)
    u, v, act = _ffn_in(h2, w['ffn_w1'], w['ffn_w3'], name="f_ffn_in")

    def f_loss(i, n, ff_, x1_, tgt, ada_):
        g2 = ada_[:, 5 * D:6 * D]
        err = x1_ + g2 * ff_ - tgt
        dx2 = err * (1.0 / D)
        loss = 0.5 * jnp.sum(jnp.sum(err * err, axis=1, keepdims=True) * (1.0 / D), axis=0, keepdims=True)
        return dx2, (dx2 * g2), jnp.broadcast_to(loss, (1, 128)), jnp.sum(dx2 * ff_, axis=0, keepdims=True)
    dx2, dff, loss, dgate2 = _mm_then([(act, w['ffn_w2'], 'nn')], f_loss, [x1, target], [ada],
                                      [row(D), row(D, MXU), acc(128), acc(D)], tm=tm, name="f_ffn_out")

    du, dv = _ffn_act_bwd(dff, w['ffn_w2'], u, v, name="b_ffn_out_dx")
    g_w2 = _mm_tn(act, dff, name="b_ffn_out_dw", out_dtype=MXU)
    g_w1 = _mm_tn(h2, du, name="b_ffn_w1_dw", out_dtype=MXU)
    g_w3 = _mm_tn(h2, dv, name="b_ffn_w3_dw", out_dtype=MXU)

    def b_res1(i, n, dh2_, x1_, dx2_, mo_, g, ada_):
        _, vjp = jax.vjp(_norm_mod, x1_, g, ada_[:, 4 * D:5 * D], ada_[:, 3 * D:4 * D])
        dxn, dg, dsc, dsh = vjp(dh2_)
        dx1_ = dxn + dx2_
        g1 = ada_[:, 2 * D:3 * D]
        return dx1_, dx1_ * g1, dg, dsc, dsh, jnp.sum(dx1_ * mo_, axis=0, keepdims=True)
    dx1, dmo, d_gain2, d_scale2, d_shift2, dgate1 = _mm_then(
        [(du, w['ffn_w1'], 'nt'), (dv, w['ffn_w3'], 'nt')], b_res1, [x1, dx2, mo], [s['norm2_gain'], ada],
        [row(D), row(D, MXU), acc(D), acc(D), acc(D), acc(D)], tm=tm_wide, name="b_ffn_in_dx")
    g_wout = _mm_tn(merged, dmo, name="b_out_dw", out_dtype=MXU)

    def b_merge(i, n, dm, pg, ma_, mb_, bias):
        _, vjp = jax.vjp(_merge, pg.astype(F32), ma_.astype(F32), mb_.astype(F32), bias)
        dpg, dma_, dmb_, dbias = vjp(dm)
        return dpg, dma_, dmb_, dbias
    dpg, dma, dmb, d_bias = _mm_then([(dmo, w['w_out'], 'nt')], b_merge, [proj_g, ma, mb], [s['branch_gate_b']],
                                     [row(GATE_W, MXU), row(D, MXU), row(D, MXU), acc(GATE_W)], tm=tm_wide,
                                     name="b_out_dx")
    dya = _mm_nt(dma, w['w_branch_a'], name="b_branch_a_dx")
    g_wa = _mm_tn(ya, dma, name="b_branch_a_dw", out_dtype=MXU, col_shards=4)
    dyb = _mm_nt(dmb, w['w_branch_b'], name="b_branch_b_dx", out_dtype=F32)
    g_wb = _mm_tn(yb, dmb, name="b_branch_b_dw", out_dtype=MXU, col_shards=4)
    fs = DFF // 4
    gw = dict(w_branch_a=g_wa, w_branch_b=g_wb, w_out=g_wout.reshape(4, D // 4, D),
              ffn_w1=jnp.stack([g_w1[:, j * fs:(j + 1) * fs] for j in range(4)]),
              ffn_w3=jnp.stack([g_w3[:, j * fs:(j + 1) * fs] for j in range(4)]),
              ffn_w2=g_w2.reshape(4, fs, D))
    recv = {}
    dqkv, d_qg, d_kg, d_sinks, *got = _attn_bwd(
        proj_qkv, tab, s['q_norm_gain'], s['k_norm_gain'], s['attn_sinks'], dyb, name="b_attn",
        hosted=shards and _ScatterChips([gw[n] for n in BACK_ATTN]))
    recv.update(zip(BACK_ATTN, got))

    def b_post(i, n, y_, r_, k_, v_, g_, dya_, *params):
        _, vjp = jax.vjp(_post, y_, r_, k_, v_, g_, *params)
        dy_, dr_, dk_, dv_, dg_, *dparams = vjp(dya_)
        return (dy_, jnp.concatenate([dr_, dk_, dv_, dg_], axis=1), *dparams)
    dy, drkvg, d_lnx_gain, d_lnx_bias, d_r_k = _rowwise(
        b_post, [y] + rkvg + [dya], post_consts, [row(RW), row(4 * RW, MXU), acc(RW), acc(RW), acc(RW)], tm=tm_wide,
        name="b_post")
    dscan, *got = _scan_bwd(rw, ck, inv, dy, name="b_scan",
                            hosted=shards and _ScatterChips([gw[n] for n in BACK_SCAN]))
    recv.update(zip(BACK_SCAN, got))

    def b_prep(i, n, cur, drw_, dscan_, prev8, mu, *params):
        shifted = _shift_down(cur, prev8, i)
        mixed = cur + (shifted - cur) * mu
        _, vjp = jax.vjp(_prep, mixed, *params)
        blk = lambda t, j: t[:, j * RW:(j + 1) * RW].astype(F32)
        ct = jnp.concatenate([blk(dscan_, 0) + blk(drw_, 0), blk(dscan_, 1), blk(dscan_, 2) + blk(drw_, 1),
                              blk(dscan_, 3) + blk(drw_, 2), blk(dscan_, 4), blk(dscan_, 5), blk(drw_, 3)], axis=1)
        grads = vjp(ct)
        dmixed = grads[0]
        return (dmixed, jnp.sum(dmixed * (shifted - cur), axis=0, keepdims=True)) + tuple(grads[1:])
    dmixed, d_mu, d_w0, d_lora, d_a0, d_gate_up, d_kk, d_ka = _rowwise(
        b_prep, [(proj, SHIFT_W), drkvg, dscan], [s['tshift_mu']] + prep_consts,
        [row(SHIFT_W), acc(SHIFT_W), acc(RW), acc(2 * RW, 128), acc(RW), acc(RW, 128), acc(RW), acc(RW)],
        tm=tm_vjp, name="b_prep", halo=[(proj, SHIFT_W, 'prev')])

    def b_gather(i, n, dm, dqkv_, dpg_, next8, mu):
        dcur = dm * (1.0 - mu) + _shift_up(dm, next8, i, n) * mu
        return (jnp.concatenate([dcur.astype(MXU), dqkv_, dpg_], axis=1),)
    (dproj,) = _rowwise(b_gather, [dmixed, dqkv, dpg], [s['tshift_mu']], [row(IN_W, MXU)], tm=tm_wide, name="b_gather",
                        halo=[(dmixed, SHIFT_W, 'next')])
    g_win = _mm_tn(h1, dproj, name="b_proj_dw", out_dtype=MXU, col_shards=4)

    def col_blocks(g):
        k, n = g.shape
        return g.reshape(k, 4, n // 4).transpose(1, 0, 2).astype(MXU)
    gw.update(w_in=g_win, decay_up=col_blocks(d_lora[:64, :RW]), iclr_up=col_blocks(d_lora[64:, RW:]),
              gate_up=col_blocks(d_gate_up))
    top, bottom = None, None
    if shards:
        top = _ScatterChips([gw['w_in'][:, :D // 2]] + [gw[n] for n in BACK_LAST[1:]])
        bottom = _ScatterChips([gw['w_in'][:, D // 2:3 * D // 4]])
        dh1, *got_top = _mm_nt(dproj, w['w_in'], name="b_proj_dx", hosted=top)
    else:
        dh1 = _mm_nt(dproj, w['w_in'], name="b_proj_dx")

    def b_norm1(i, n, x_, dh1_, dx1_, g, ada_):
        _, vjp = jax.vjp(_norm_mod, x_, g, ada_[:, D:2 * D], ada_[:, 0:D])
        dxn, dg, dsc, dsh = vjp(dh1_)
        return dxn + dx1_, dg, dsc, dsh
    dx, d_gain1, d_scale1, d_shift1, *got_bottom = _rowwise(
        b_norm1, [x, dh1, dx1], [s['norm1_gain'], ada], [row(D), acc(D), acc(D), acc(D)], tm=tm, name="b_norm1",
        hosted=bottom)
    if shards:
        recv.update(zip(BACK_LAST[1:], got_top[1:]))
        recv['w_in'] = [got_top[0], got_bottom[0]]

    d_ada = jnp.concatenate([d_shift1, d_scale1, dgate1, d_shift2, d_scale2, dgate2], axis=1)
    gs = dict(norm1_gain=d_gain1, norm2_gain=d_gain2, tshift_mu=d_mu, decay_w0=d_w0, iclr_a0=d_a0, k_k=d_kk, k_a=d_ka,
              r_k=d_r_k, lnx_gain=d_lnx_gain, lnx_bias=d_lnx_bias, q_norm_gain=d_qg, k_norm_gain=d_kg,
              attn_sinks=d_sinks, branch_gate_b=d_bias)
    return loss, dx, d_ada, gw, gs, recv


ANY = pl.BlockSpec(memory_space=pl.ANY)


def _place():
    x, y, c = lax.axis_index("x"), lax.axis_index("y"), lax.axis_index("c")
    return x, y, c, [(1 - x, y), (x, 1 - y), (1 - x, 1 - y)]


def _all_gather8(x_shard, *, name):
    m_per, n = x_shard.shape

    def body(x_ref, out_ref, send_sems, recv_sems, local_sem):
        x, y, c, chips = _place()
        me, sibling = (x, y, c), (x, y, 1 - c)

        def rows(px, py, pc):
            return out_ref.at[pl.ds((4 * px + 2 * py + pc) * m_per, m_per), :]

        def copy(k, block, to, src=None):
            return pltpu.make_async_remote_copy(
                src_ref=rows(*block) if src is None else src, dst_ref=rows(*block),
                send_sem=send_sems.at[k], recv_sem=recv_sems.at[k], device_id=to, device_id_type=MESH)

        mine = pltpu.make_async_copy(x_ref, rows(*me), local_sem)
        mine.start()
        first = [copy(0, me, sibling, src=x_ref)]
        first += [copy(1 + j, me, (*chip, c), src=x_ref) for j, chip in enumerate(chips)]
        for cp in first:
            cp.start()
        passed = [copy(4 + j, (*chip, c), sibling) for j, chip in enumerate(chips)]
        for j, chip in enumerate(chips):
            copy(1 + j, (*chip, c), me).wait_recv()
            passed[j].start()
        copy(0, sibling, me).wait_recv()
        for j, chip in enumerate(chips):
            copy(4 + j, (*chip, 1 - c), me).wait_recv()
        for cp in first + passed:
            cp.wait_send()
        mine.wait()

    return pl.pallas_call(
        body, name=name, out_shape=jax.ShapeDtypeStruct((8 * m_per, n), x_shard.dtype),
        in_specs=[pl.BlockSpec(memory_space=pltpu.VMEM)], out_specs=pl.BlockSpec(memory_space=pltpu.VMEM),
        scratch_shapes=[pltpu.SemaphoreType.DMA((7,)), pltpu.SemaphoreType.DMA((7,)), pltpu.SemaphoreType.DMA],
    )(x_shard)


class _GatherChips:
    def __init__(self, shards):
        n = len(shards)
        self.arrays, self.n_in, self.n_out = list(shards), n, n
        self.out_shape = [jax.ShapeDtypeStruct((4,) + s.shape, s.dtype) for s in shards]
        self.scratch = [pltpu.SemaphoreType.DMA((3 * n,)), pltpu.SemaphoreType.DMA((3 * n,)),
                        pltpu.SemaphoreType.DMA((n,))]

    def _copies(self, x_refs, out_refs, sems, receiving):
        send_sems, recv_sems, local_sems = sems
        x, y, c, chips = _place()
        s_me = 2 * x + y
        n = self.n_in

        def copy(a, k, s):
            return pltpu.make_async_remote_copy(
                src_ref=x_refs[a], dst_ref=out_refs[a].at[s], send_sem=send_sems.at[3 * a + k],
                recv_sem=recv_sems.at[3 * a + k], device_id=(*chips[k], c), device_id_type=MESH)

        mine = [pltpu.make_async_copy(x_refs[a], out_refs[a].at[s_me], local_sems.at[a]) for a in range(n)]
        sends = [copy(a, k, s_me) for a in range(n) for k in range(3)]
        if not receiving:
            return mine, sends
        return mine, sends, [copy(a, k, 2 * px + py) for a in range(n) for k, (px, py) in enumerate(chips)]

    def start(self, x_refs, out_refs, sems):
        mine, sends = self._copies(x_refs, out_refs, sems, False)
        for cp in mine + sends:
            cp.start()

    def wait(self, x_refs, out_refs, sems):
        mine, sends, recvs = self._copies(x_refs, out_refs, sems, True)
        for cp in recvs:
            cp.wait_recv()
        for cp in sends:
            cp.wait_send()
        for cp in mine:
            cp.wait()


class _GatherChipsHalved(_GatherChips):
    def __init__(self, shards):
        super().__init__(shards)
        n = self.n_in
        self.scratch = [pltpu.SemaphoreType.DMA((6 * n,)), pltpu.SemaphoreType.DMA((6 * n,)),
                        pltpu.SemaphoreType.DMA((n,))]

    def _copies(self, x_refs, out_refs, sems, receiving):
        send_sems, recv_sems, local_sems = sems
        x, y, c, chips = _place()
        s_me = 2 * x + y
        n = self.n_in

        def half(a, who):
            rows = x_refs[a].shape[0] // 2
            return pl.ds(who * rows, rows)

        def over_chips(a, k, s):
            return pltpu.make_async_remote_copy(
                src_ref=x_refs[a].at[half(a, c)], dst_ref=out_refs[a].at[s, half(a, c)],
                send_sem=send_sems.at[3 * a + k], recv_sem=recv_sems.at[3 * a + k],
                device_id=(*chips[k], c), device_id_type=MESH)

        def to_sibling(a, k, s, who):
            return pltpu.make_async_remote_copy(
                src_ref=out_refs[a].at[s, half(a, who)], dst_ref=out_refs[a].at[s, half(a, who)],
                send_sem=send_sems.at[3 * n + 3 * a + k], recv_sem=recv_sems.at[3 * n + 3 * a + k],
                device_id=(x, y, 1 - c), device_id_type=MESH)

        mine = [pltpu.make_async_copy(x_refs[a], out_refs[a].at[s_me], local_sems.at[a]) for a in range(n)]
        sends = [over_chips(a, k, s_me) for a in range(n) for k in range(3)]
        if not receiving:
            return mine, sends
        pairs = [(a, k, 2 * px + py) for a in range(n) for k, (px, py) in enumerate(chips)]
        landed = [over_chips(a, k, s) for a, k, s in pairs]
        passed_on = [to_sibling(a, k, s, c) for a, k, s in pairs]
        from_sibling = [to_sibling(a, k, s, 1 - c) for a, k, s in pairs]
        return mine, sends, landed, passed_on, from_sibling

    def wait(self, x_refs, out_refs, sems):
        mine, sends, landed, passed_on, from_sibling = self._copies(x_refs, out_refs, sems, True)
        for got, fwd in zip(landed, passed_on, strict=True):
            got.wait_recv()
            fwd.start()
        for cp in from_sibling:
            cp.wait_recv()
        for cp in sends + passed_on:
            cp.wait_send()
        for cp in mine:
            cp.wait()


class _ScatterChips:
    def __init__(self, parts):
        n = len(parts)
        self.arrays, self.n_in, self.n_out = list(parts), n, n
        self.out_shape = [jax.ShapeDtypeStruct((3,) + p.shape[1:], p.dtype) for p in parts]
        self.scratch = [pltpu.SemaphoreType.DMA((3 * n,)), pltpu.SemaphoreType.DMA((3 * n,))]

    def _copies(self, g_refs, out_refs, sems):
        send_sems, recv_sems = sems
        x, y, c, chips = _place()
        return [pltpu.make_async_remote_copy(
            src_ref=g_refs[a].at[2 * px + py], dst_ref=out_refs[a].at[k], send_sem=send_sems.at[3 * a + k],
            recv_sem=recv_sems.at[3 * a + k], device_id=(px, py, c), device_id_type=MESH)
            for a in range(self.n_in) for k, (px, py) in enumerate(chips)]

    def start(self, g_refs, out_refs, sems):
        for cp in self._copies(g_refs, out_refs, sems):
            cp.start()

    def wait(self, g_refs, out_refs, sems):
        sends = self._copies(g_refs, out_refs, sems)
        for cp in sends:
            cp.wait_recv()
        for cp in sends:
            cp.wait_send()


def _exchange_call(ex, *, name):
    def body(*refs):
        parts = (refs[:ex.n_in], refs[ex.n_in:ex.n_in + ex.n_out], refs[ex.n_in + ex.n_out:])
        ex.start(*parts)
        ex.wait(*parts)

    return pl.pallas_call(body, name=name, out_shape=ex.out_shape, in_specs=[ANY] * ex.n_in,
                          out_specs=[ANY] * ex.n_out, scratch_shapes=ex.scratch)(*ex.arrays)


class _SwapSibling:
    def __init__(self, vs):
        n = len(vs)
        self.arrays, self.n_in, self.n_out = list(vs), n, n
        self.out_shape = [jax.ShapeDtypeStruct(v.shape, v.dtype) for v in vs]
        self.scratch = [pltpu.SemaphoreType.DMA((n,)), pltpu.SemaphoreType.DMA((n,))]

    def _copies(self, v_refs, out_refs, sems):
        send_sems, recv_sems = sems
        x, y, c, _ = _place()
        return [pltpu.make_async_remote_copy(src_ref=v_refs[a], dst_ref=out_refs[a], send_sem=send_sems.at[a],
                                             recv_sem=recv_sems.at[a], device_id=(x, y, 1 - c), device_id_type=MESH)
                for a in range(self.n_in)]

    def start(self, v_refs, out_refs, sems):
        for cp in self._copies(v_refs, out_refs, sems):
            cp.start()

    def wait(self, v_refs, out_refs, sems):
        for cp in self._copies(v_refs, out_refs, sems):
            cp.wait()


class _Both:
    def __init__(self, first, second):
        self.parts = (first, second)
        self.arrays = first.arrays + second.arrays
        self.n_in, self.n_out = first.n_in + second.n_in, first.n_out + second.n_out
        self.out_shape = first.out_shape + second.out_shape
        self.scratch = first.scratch + second.scratch

    def _split(self, in_refs, out_refs, sems):
        a, b = self.parts
        return ((a, in_refs[:a.n_in], out_refs[:a.n_out], sems[:len(a.scratch)]),
                (b, in_refs[a.n_in:], out_refs[a.n_out:], sems[len(a.scratch):]))

    def start(self, in_refs, out_refs, sems):
        for ex, *refs in self._split(in_refs, out_refs, sems):
            ex.start(*refs)

    def wait(self, in_refs, out_refs, sems):
        for ex, *refs in self._split(in_refs, out_refs, sems):
            ex.wait(*refs)


def _sum_parts(own, others, *, name):
    R, C = own.shape
    tm = _pick(R, (256, 128, 64))

    def body(own_ref, o0_ref, o1_ref, o2_ref, out_ref):
        tot = own_ref[...].astype(F32)
        for ref in (o0_ref, o1_ref, o2_ref):
            tot = tot + ref[...].astype(F32)
        out_ref[...] = tot

    part = lambda k: pl.BlockSpec((None, tm, C), lambda i: (k, i, 0))
    return pl.pallas_call(
        body, name=name, grid=(R // tm,),
        in_specs=[pl.BlockSpec((tm, C), lambda i: (i, 0)), part(0), part(1), part(2)],
        out_specs=pl.BlockSpec((tm, C), lambda i: (i, 0)), out_shape=jax.ShapeDtypeStruct((R, C), F32),
        compiler_params=_cparams(("arbitrary",)),
    )(own, others, others, others)


def _adam_math(w_, m_, v_, g):
    m2 = ADAM_B1 * m_ + (1.0 - ADAM_B1) * g
    v2 = ADAM_B2 * v_ + (1.0 - ADAM_B2) * jnp.square(g)
    m_hat = m2 / (1.0 - ADAM_B1 ** ADAM_STEP)
    v_hat = v2 / (1.0 - ADAM_B2 ** ADAM_STEP)
    delta = -ADAM_LR * (m_hat / (jnp.sqrt(v_hat) + ADAM_EPS) + ADAM_WD * w_)
    return delta, m2, v2


SMALL_SLOTS = 16
SMALL_COLS = 6 * D


def _pack_small(grads, *, name):
    n = len(grads)

    def body(*refs):
        out_ref = refs[n]
        out_ref[...] = jnp.zeros_like(out_ref)
        for i, ref in enumerate(refs[:n]):
            out_ref[i:i + 1, 0:ref.shape[1]] = ref[...]

    return pl.pallas_call(body, name=name, out_shape=jax.ShapeDtypeStruct((SMALL_SLOTS, SMALL_COLS), F32))(*grads)


def _adamw_small(ws, ms, vs, gathered, *, name):
    n = len(ws)

    def total(g_ref, i, nc):
        g = g_ref[i:i + 1, 0:nc]
        for d in range(1, 8):
            g = g + g_ref[d * SMALL_SLOTS + i:d * SMALL_SLOTS + i + 1, 0:nc]
        return g

    def body(*refs):
        w_refs, m_refs, v_refs, g_ref = refs[:n], refs[n:2 * n], refs[2 * n:3 * n], refs[3 * n]
        outs = refs[3 * n + 1:]
        for i in range(n):
            g = total(g_ref, i, w_refs[i].shape[1])
            delta, m2, v2 = _adam_math(w_refs[i][...], m_refs[i][...], v_refs[i][...], g)
            for k, val in enumerate((g, delta, m2, v2)):
                outs[k * n + i][...] = val
        outs[4 * n][...] = total(g_ref, n, 128)

    shapes = [jax.ShapeDtypeStruct(w.shape, F32) for w in ws]
    res = pl.pallas_call(body, name=name, out_shape=shapes * 4 + [jax.ShapeDtypeStruct((1, 128), F32)],
                         compiler_params=pltpu.CompilerParams(vmem_limit_bytes=VMEM_LIMIT))(*ws, *ms, *vs, gathered)
    return [res[k * n:(k + 1) * n] for k in range(4)], res[4 * n]


def _adamw(w, m, v, gparts, *, tm, name, hosted=None):
    def fn(i, n, w_, m_, v_, *gs):
        g = gs[0]
        for p in gs[1:]:
            g = g + p
        return (g,) + _adam_math(w_, m_, v_, g)
    nc = w.shape[1]
    return _rowwise(fn, [w, m, v] + list(gparts), [], [(nc, F32, 'row')] * 4, tm=tm, name=name, hosted=hosted)


WEIGHTS = ['ada_w', 'ada_b', 'norm1_gain', 'norm2_gain', 'w_in', 'tshift_mu', 'decay_w0', 'decay_up', 'iclr_a0',
           'iclr_up', 'gate_up', 'k_k', 'k_a', 'r_k', 'lnx_gain', 'lnx_bias', 'q_norm_gain', 'k_norm_gain', 'attn_sinks',
           'branch_gate_b', 'w_branch_a', 'w_branch_b', 'w_out', 'ffn_w1', 'ffn_w3', 'ffn_w2']
SHARDED = [('w_in', 1), ('decay_up', 1), ('iclr_up', 1), ('gate_up', 1), ('w_branch_a', 1), ('w_branch_b', 1),
           ('w_out', 0), ('ffn_w1', 1), ('ffn_w3', 1), ('ffn_w2', 0)]
SMALL = ['ada_b', 'norm1_gain', 'norm2_gain', 'tshift_mu', 'decay_w0', 'iclr_a0', 'k_k', 'k_a', 'r_k', 'lnx_gain',
         'lnx_bias', 'q_norm_gain', 'k_norm_gain', 'attn_sinks', 'branch_gate_b']


def kernel(x, c, positions, ada_w, ada_b, norm1_gain, norm2_gain, w_in, tshift_mu, decay_w0, decay_up, iclr_a0, iclr_up, gate_up, k_k, k_a, r_k, lnx_gain, lnx_bias, q_norm_gain, k_norm_gain, attn_sinks, branch_gate_b, w_branch_a, w_branch_b, w_out, ffn_w1, ffn_w3, ffn_w2, loss_target, m_ada_w, m_ada_b, m_norm1_gain, m_norm2_gain, m_w_in, m_tshift_mu, m_decay_w0, m_decay_up, m_iclr_a0, m_iclr_up, m_gate_up, m_k_k, m_k_a, m_r_k, m_lnx_gain, m_lnx_bias, m_q_norm_gain, m_k_norm_gain, m_attn_sinks, m_branch_gate_b, m_w_branch_a, m_w_branch_b, m_w_out, m_ffn_w1, m_ffn_w3, m_ffn_w2, v_ada_w, v_ada_b, v_norm1_gain, v_norm2_gain, v_w_in, v_tshift_mu, v_decay_w0, v_decay_up, v_iclr_a0, v_iclr_up, v_gate_up, v_k_k, v_k_a, v_r_k, v_lnx_gain, v_lnx_bias, v_q_norm_gain, v_k_norm_gain, v_attn_sinks, v_branch_gate_b, v_w_branch_a, v_w_branch_b, v_w_out, v_ffn_w1, v_ffn_w3, v_ffn_w2):
    a = dict(locals())
    W = {n: a[n] for n in WEIGHTS}
    M = {n: a['m_' + n] for n in WEIGHTS}
    V = {n: a['v_' + n] for n in WEIGHTS}
    xi, yi, ci = lax.axis_index("x"), lax.axis_index("y"), lax.axis_index("c")
    me = 4 * xi + 2 * yi + ci
    shard = 2 * xi + yi
    mat = lambda t: t.reshape(t.shape[-2], t.shape[-1])
    sharded = [n for n, _ in SHARDED]

    ax = dict(SHARDED)
    late = LATE
    early = [n for n in sharded if n not in late]
    shards = {n: mat(W[n]).astype(MXU) for n in sharded}
    gathered = _exchange_call(_GatherChipsHalved([shards[n] for n in early]), name="gather_weights")
    full = {n: _full_weight(g, ax[n]) for n, g in zip(early, gathered, strict=True)}

    c_all = _all_gather8(jnp.broadcast_to(c, (8, D)), name="gather_c")[0::8]
    pad_rows = lambda t: jnp.concatenate([t, jnp.zeros((BLK - 8, t.shape[1]), t.dtype)])
    c_all = pad_rows(c_all.astype(MXU))
    ada_cols = _mm_nn(c_all, mat(ada_w).astype(MXU), name="f_ada")[:8]
    ada_all = _all_gather8(ada_cols, name="gather_ada").reshape(2, 2, 2, 8, 6 * D // 4)
    ada_mine = lax.dynamic_index_in_dim(ada_all[:, :, 0], me, axis=2, keepdims=False)
    ada = ada_mine.reshape(1, 6 * D) + mat(ada_b)

    zero = jnp.zeros((64, RW), MXU)
    lora = jnp.concatenate([jnp.concatenate([full['decay_up'], zero], axis=1),
                            jnp.concatenate([zero, full['iclr_up']], axis=1)], axis=0)
    s = {n: W[n].reshape(1, -1) for n in SMALL if n != 'ada_b'}
    s['lora_up'] = lora.astype(F32)
    s['gate_up'] = full['gate_up'].astype(F32)
    tab = _rope_table(positions.reshape(-1))
    loss, dx, d_ada, gw, gs, from_chips = _local_step(x[0], loss_target[0], ada, tab, dict(w_in=full['w_in']), s,
                                                      shards={n: shards[n] for n in late})

    gs['ada_b'] = d_ada
    gsmall = _pack_small([gs[n] for n in SMALL] + [loss], name="pack_small_grads")
    gsmall_all = _all_gather8(gsmall, name="gather_small_grads")
    row = lambda src: [src[n].reshape(1, -1) for n in SMALL]
    sm_out, loss = _adamw_small(row(W), row(M), row(V), gsmall_all, name="adamw_small")
    sm_out = [{n: o.reshape(W[n].shape) for n, o in zip(SMALL, outs_k, strict=True)} for outs_k in sm_out]
    loss = loss[0, 0]

    d_ada_all = gsmall_all[0::SMALL_SLOTS]
    d_ada_cols = lax.dynamic_slice_in_dim(d_ada_all, shard * (6 * D // 4), 6 * D // 4, axis=1)
    g_ada_w = _mm_tn(c_all, pad_rows(d_ada_cols.astype(MXU)), name="b_ada")

    rest = [n for n in sharded if n != 'w_in']
    parts = {n: _sum_parts(lax.dynamic_index_in_dim(gw[n], shard, axis=0, keepdims=False), from_chips[n],
                           name="sum_" + n) for n in rest}
    tail = _Both(_ScatterChips([gw['w_in'][:, 3 * D // 4:]]), _SwapSibling([parts[n] for n in rest]))
    res = _adamw(mat(ada_w), mat(m_ada_w), mat(v_ada_w), [g_ada_w], tm=256, name="adamw_ada", hosted=tail)
    ada_out, last_quarter, others = res[:4], res[4], dict(zip(rest, res[5:], strict=True))
    parts['w_in'] = _sum_parts(lax.dynamic_index_in_dim(gw['w_in'], shard, axis=0, keepdims=False),
                               jnp.concatenate(from_chips['w_in'] + [last_quarter], axis=1), name="sum_w_in")
    others['w_in'] = _exchange_call(_SwapSibling([parts['w_in']]), name="swap_w_in")[0]
    sh_out = {}
    for n in sharded:
        part, other = parts[n], others[n]
        sh_out[n] = _adamw(mat(W[n]), mat(M[n]), mat(V[n]), [part, other], tm=_pick(part.shape[0], (256, 128, 64)),
                           name="adamw_" + n)

    def leaf(k, n):
        if n == 'ada_w':
            return ada_out[k].reshape(W[n].shape)
        if n in sharded:
            return sh_out[n][k].reshape(W[n].shape)
        return sm_out[k][n]
    outs = [leaf(k, n) for k in range(4) for n in WEIGHTS]
    return (loss, dx[None], *outs)
```

```python
import functools
import math

import jax
import jax.numpy as jnp
from jax import lax
from jax.experimental import pallas as pl
from jax.experimental.pallas import tpu as pltpu

F32 = jnp.float32
BF16 = jnp.bfloat16
MXU = BF16
HI = lax.Precision.HIGHEST

D = 1024
HD = 64
NH = 8
RW = NH * HD
SHIFT_W = 3 * RW + 64 + 64 + 128
QKV_W = RW + 2 * 128
GATE_W = 2 * D
IN_W = SHIFT_W + QKV_W + GATE_W
DFF = 2816
BLK = 128
CHUNK = 64
RMS_EPS = 1e-6
GN_EPS = 64e-5
NEG_INF = -1e30
ADAM_LR, ADAM_B1, ADAM_B2, ADAM_EPS, ADAM_WD, ADAM_STEP = 0.001, 0.9, 0.999, 1e-08, 0.01, 10
VMEM_LIMIT = 56 * 1024 * 1024
MESH = pl.DeviceIdType.MESH


def _cparams(sem=None):
    return pltpu.CompilerParams(dimension_semantics=sem, vmem_limit_bytes=VMEM_LIMIT)


def _full_spec(a):
    nd = a.ndim
    return pl.BlockSpec(a.shape, lambda *_: (0,) * nd)


def _rowwise(fn, rows, consts, outs, *, tm, name, halo=(), hosted=None):
    rows = [(a + (0,))[:3] if isinstance(a, tuple) else (a, a.shape[1], 0) for a in rows]
    T = rows[0][0].shape[0]
    assert T % tm == 0 and tm % 8 == 0
    n_tiles = T // tm
    n_in = len(rows) + len(halo) + len(consts)
    in_specs = [pl.BlockSpec((tm, nc), lambda i, j=j: (i, j)) for _, nc, j in rows]
    args = [a for a, _, _ in rows]
    for a, nc, kind in halo:
        if kind == 'prev':
            in_specs.append(pl.BlockSpec((8, nc), lambda i: (jnp.maximum(i * (tm // 8) - 1, 0), 0)))
        else:
            in_specs.append(pl.BlockSpec((8, nc), lambda i: (jnp.minimum((i + 1) * (tm // 8), T // 8 - 1), 0)))
        args.append(a)
    in_specs += [_full_spec(a) for a in consts]
    args += list(consts)
    out_shape, out_specs = [], []
    for ncols, dtype, kind in outs:
        if kind == 'row':
            out_shape.append(jax.ShapeDtypeStruct((T, ncols), dtype))
            out_specs.append(pl.BlockSpec((tm, ncols), lambda i: (i, 0)))
        else:
            out_shape.append(jax.ShapeDtypeStruct((kind, ncols), dtype))
            out_specs.append(pl.BlockSpec((kind, ncols), lambda i: (0, 0)))

    def body(*refs):
        i = pl.program_id(0)
        vals = [r[...] for r in refs[:n_in]]
        res = fn(i, n_tiles, *vals)
        for (ncols, dtype, kind), o_ref, val in zip(outs, refs[n_in:], res, strict=True):
            if kind == 'row':
                o_ref[...] = val.astype(dtype)
            else:
                @pl.when(i == 0)
                def _():
                    o_ref[...] = jnp.zeros_like(o_ref)
                o_ref[...] += val.astype(dtype)

    h_in, h_in_specs, h_out_specs, h_out_shape, h_scratch = _hosted_args(hosted)
    res = pl.pallas_call(
        _hosting(body, hosted, n_in, len(outs), 0, n_tiles), name=name, grid=(n_tiles,),
        in_specs=in_specs + h_in_specs, out_specs=out_specs + h_out_specs, out_shape=out_shape + h_out_shape,
        scratch_shapes=h_scratch, compiler_params=_cparams(("arbitrary",)),
    )(*args, *h_in)
    return res


def _pick(n, cands):
    for c in cands:
        if n % c == 0:
            return c
    return n


MM_ROWS = (1024, 512, 256, 128)
MM_COLS = (1536, 1408, 1024, 896, 768, 512, 256, 128)
MM_WIDE = 3000


def _mm_nn(a, w, *, name, out_dtype=F32, hosted=None):
    T, K = a.shape
    N = w.shape[1]
    tm = _pick(T, MM_ROWS)
    tn = _pick(N, MM_COLS)
    grid = (N // tn, T // tm)
    h_in, h_in_specs, h_out_specs, h_out_shape, h_scratch = _hosted_args(hosted)

    def body(a_ref, w_ref, o_ref):
        o_ref[...] = jnp.dot(a_ref[...], w_ref[...], preferred_element_type=F32).astype(out_dtype)

    res = pl.pallas_call(
        _hosting(body, hosted, 2, 1, 0, grid), name=name, grid=grid,
        in_specs=[pl.BlockSpec((tm, K), lambda j, i: (i, 0)), pl.BlockSpec((K, tn), lambda j, i: (0, j))] + h_in_specs,
        out_specs=[pl.BlockSpec((tm, tn), lambda j, i: (i, j))] + h_out_specs,
        out_shape=[jax.ShapeDtypeStruct((T, N), out_dtype)] + h_out_shape, scratch_shapes=h_scratch,
        compiler_params=_cparams(("arbitrary", "arbitrary")),
    )(a, w, *h_in)
    return res if hosted else res[0]


def _mm_nt(dy, w, *, name, out_dtype=F32, hosted=None):
    T, N = dy.shape
    K = w.shape[0]
    tm = _pick(T, MM_ROWS if N <= MM_WIDE else MM_ROWS[1:])
    tk = _pick(K, MM_COLS[1:])
    grid = (K // tk, T // tm)
    h_in, h_in_specs, h_out_specs, h_out_shape, h_scratch = _hosted_args(hosted)

    def body(dy_ref, w_ref, o_ref):
        o_ref[...] = lax.dot_general(dy_ref[...], w_ref[...], (((1,), (1,)), ((), ())),
                                     preferred_element_type=F32).astype(out_dtype)

    res = pl.pallas_call(
        _hosting(body, hosted, 2, 1, 0, grid), name=name, grid=grid,
        in_specs=[pl.BlockSpec((tm, N), lambda j, i: (i, 0)), pl.BlockSpec((tk, N), lambda j, i: (j, 0))] + h_in_specs,
        out_specs=[pl.BlockSpec((tm, tk), lambda j, i: (i, j))] + h_out_specs,
        out_shape=[jax.ShapeDtypeStruct((T, K), out_dtype)] + h_out_shape, scratch_shapes=h_scratch,
        compiler_params=_cparams(("arbitrary", "arbitrary")),
    )(dy, w, *h_in)
    return res if hosted else res[0]


def _mm_tn(a, dy, *, name, out_dtype=F32, col_shards=None):
    T, K = a.shape
    N = dy.shape[1]
    tm = _pick(T, MM_ROWS)
    tn = N // col_shards if col_shards else _pick(N, MM_COLS[1:])
    n_t = T // tm

    def body(a_ref, dy_ref, o_ref, acc_ref):
        i = pl.program_id(1)

        @pl.when(i == 0)
        def _():
            acc_ref[...] = jnp.zeros_like(acc_ref)

        acc_ref[...] += lax.dot_general(a_ref[...], dy_ref[...], (((0,), (0,)), ((), ())), preferred_element_type=F32)

        @pl.when(i == n_t - 1)
        def _():
            o_ref[...] = acc_ref[...].astype(out_dtype)

    if col_shards:
        out_specs = pl.BlockSpec((None, K, tn), lambda j, i: (j, 0, 0))
        out_shape = jax.ShapeDtypeStruct((col_shards, K, tn), out_dtype)
    else:
        out_specs = pl.BlockSpec((K, tn), lambda j, i: (0, j))
        out_shape = jax.ShapeDtypeStruct((K, N), out_dtype)
    return pl.pallas_call(
        body, name=name, grid=(N // tn, n_t),
        in_specs=[pl.BlockSpec((tm, K), lambda j, i: (i, 0)), pl.BlockSpec((tm, tn), lambda j, i: (i, j))],
        out_specs=out_specs, out_shape=out_shape, scratch_shapes=[pltpu.VMEM((K, tn), F32)],
        compiler_params=_cparams(("arbitrary", "arbitrary")),
    )(a, dy)


def _mm_then(products, fn, rows, consts, outs, *, tm, name, lhs_fn=None, hosted=None):
    T = (rows[0] if lhs_fn else products[0][0]).shape[0]
    n_tiles = T // tm
    in_specs, args = [], []
    for a, w, _ in products:
        if a is not None:
            in_specs.append(pl.BlockSpec((tm, a.shape[1]), lambda i: (i, 0)))
            args.append(a)
        in_specs.append(_full_spec(w))
        args.append(w)
    n_w = len(args)
    in_specs += [pl.BlockSpec((tm, a.shape[1]), lambda i: (i, 0)) for a in rows] + [_full_spec(c_) for c_ in consts]
    args += list(rows) + list(consts)
    n_in = len(args)
    out_shape, out_specs = [], []
    for ncols, dtype, kind in outs:
        if kind == 'row':
            out_shape.append(jax.ShapeDtypeStruct((T, ncols), dtype))
            out_specs.append(pl.BlockSpec((tm, ncols), lambda i: (i, 0)))
        else:
            out_shape.append(jax.ShapeDtypeStruct((kind, ncols), dtype))
            out_specs.append(pl.BlockSpec((kind, ncols), lambda i: (0, 0)))

    def body(*refs):
        i = pl.program_id(0)
        tiles = [r[...] for r in refs[n_w:n_in]]
        made = [lhs_fn(*tiles).astype(MXU)] if lhs_fn else []
        y, pos = None, 0
        for a, _, form in products:
            if a is None:
                lhs = made[0]
            else:
                lhs, pos = refs[pos][...], pos + 1
            dims = (((1,), (0,)), ((), ())) if form == 'nn' else (((1,), (1,)), ((), ()))
            t = lax.dot_general(lhs, refs[pos][...], dims, preferred_element_type=F32)
            pos += 1
            y = t if y is None else y + t
        res = fn(i, n_tiles, y, *made, *tiles)
        for (ncols, dtype, kind), o_ref, val in zip(outs, refs[n_in:], res, strict=True):
            if kind == 'row':
                o_ref[...] = val.astype(dtype)
            else:
                @pl.when(i == 0)
                def _():
                    o_ref[...] = jnp.zeros_like(o_ref)
                o_ref[...] += val.astype(dtype)

    h_in, h_in_specs, h_out_specs, h_out_shape, h_scratch = _hosted_args(hosted)
    return pl.pallas_call(
        _hosting(body, hosted, n_in, len(outs), 0, n_tiles), name=name, grid=(n_tiles,),
        in_specs=in_specs + h_in_specs, out_specs=out_specs + h_out_specs, out_shape=out_shape + h_out_shape,
        scratch_shapes=h_scratch, compiler_params=_cparams(("arbitrary",)))(*args, *h_in)


def _seg_ones(n):
    r = lax.broadcasted_iota(jnp.int32, (n, n), 0) // HD
    c = lax.broadcasted_iota(jnp.int32, (n, n), 1) // HD
    return (r == c).astype(F32)


def _segsum_raw(x):
    ones = _seg_ones(x.shape[1])
    if MXU == F32:
        return jnp.dot(x, ones, precision=HI, preferred_element_type=F32)
    hi = x.astype(MXU)
    lo = (x - hi.astype(F32)).astype(MXU)
    ones = ones.astype(MXU)
    return jnp.dot(hi, ones, preferred_element_type=F32) + jnp.dot(lo, ones, preferred_element_type=F32)


@jax.custom_vjp
def _segsum(x):
    return _segsum_raw(x)


def _segsum_fwd(x):
    return _segsum_raw(x), None


def _segsum_bwd(_, g):
    return (_segsum_raw(g),)


_segsum.defvjp(_segsum_fwd, _segsum_bwd)


def _mxu(x):
    return x.astype(MXU)


@jax.custom_vjp
def _bdot(a, b):
    return jnp.dot(_mxu(a), _mxu(b), preferred_element_type=F32)


def _bdot_fwd(a, b):
    return _bdot(a, b), (a, b)


def _bdot_bwd(res, g):
    a, b = res
    da = lax.dot_general(_mxu(g), _mxu(b), (((1,), (1,)), ((), ())), preferred_element_type=F32)
    db = lax.dot_general(_mxu(a), _mxu(g), (((0,), (0,)), ((), ())), preferred_element_type=F32)
    return da.astype(a.dtype), db.astype(b.dtype)


_bdot.defvjp(_bdot_fwd, _bdot_bwd)


def _sigmoid(x):
    return 1.0 / (1.0 + jnp.exp(-x))


def _softplus(x):
    return jnp.maximum(x, 0.0) + jnp.log(1.0 + jnp.exp(jnp.minimum(x, -x)))


def _norm_mod(x, gain, scale, shift):
    inv = lax.rsqrt(jnp.mean(x * x, axis=-1, keepdims=True) + RMS_EPS)
    return (x * inv) * gain * (1.0 + scale) + shift


def _prep(mixed, decay_w0, lora_up, iclr_a0, gate_up, k_k, k_a):
    r = mixed[:, 0:RW]
    k = mixed[:, RW:2 * RW]
    v = mixed[:, 2 * RW:3 * RW]
    z = mixed[:, 3 * RW:3 * RW + 128]
    xg = mixed[:, 3 * RW + 128:]
    lane = lax.broadcasted_iota(jnp.int32, z.shape, 1)
    tz = jnp.where(lane < 64, jnp.tanh(z), z)
    lo = _bdot(tz, lora_up)
    w_log = -_softplus(-(decay_w0 + lo[:, :RW])) - 0.5
    lw = -jnp.exp(w_log)
    a_ic = _sigmoid(iclr_a0 + lo[:, RW:])
    g = _bdot(_sigmoid(xg), gate_up)
    kk = k * k_k
    kk = kk / jnp.maximum(jnp.sqrt(_segsum(kk * kk)), 1e-12)
    k_mod = k * (1.0 + (a_ic - 1.0) * k_a)
    return jnp.concatenate([r, lw, k_mod, v, -kk, kk * a_ic, g], axis=1)


def _post(y, r, k, v, g, lnx_gain, lnx_bias, r_k):
    mu = _segsum(y) * (1.0 / HD)
    yc = y - mu
    var = _segsum(yc * yc) * (1.0 / HD)
    yn = yc * lax.rsqrt(var + GN_EPS) * lnx_gain + lnx_bias
    bonus = _segsum(r * k * r_k) * v
    return (yn + bonus) * g


def _merge(pg, ma, mb, bias):
    gates = _sigmoid(pg + bias)
    return gates[:, :D] * ma + gates[:, D:] * mb


def _swiglu(u, v):
    return u * _sigmoid(u) * v


def _ffn_in(h, w1, w3, *, name):
    T, K = h.shape
    F = w1.shape[1]
    tm = _pick(T, MM_ROWS)
    tn = _pick(F, MM_COLS[1:])

    def body(h_ref, w1_ref, w3_ref, u_ref, v_ref, a_ref):
        u = jnp.dot(h_ref[...], w1_ref[...], preferred_element_type=F32).astype(MXU)
        v = jnp.dot(h_ref[...], w3_ref[...], preferred_element_type=F32).astype(MXU)
        u_ref[...] = u
        v_ref[...] = v
        a_ref[...] = _swiglu(u.astype(F32), v.astype(F32)).astype(MXU)

    wspec = pl.BlockSpec((K, tn), lambda j, i: (0, j))
    ospec = pl.BlockSpec((tm, tn), lambda j, i: (i, j))
    return pl.pallas_call(
        body, name=name, grid=(F // tn, T // tm),
        in_specs=[pl.BlockSpec((tm, K), lambda j, i: (i, 0)), wspec, wspec],
        out_specs=[ospec] * 3, out_shape=[jax.ShapeDtypeStruct((T, F), MXU)] * 3,
        compiler_params=_cparams(("arbitrary", "arbitrary")),
    )(h, w1, w3)


def _ffn_act_bwd(dff, w2, u, v, *, name):
    T, N = dff.shape
    F = w2.shape[0]
    tm = _pick(T, MM_ROWS)
    tk = _pick(F, MM_COLS[1:])

    def body(dy_ref, w_ref, u_ref, v_ref, du_ref, dv_ref):
        dact = lax.dot_general(dy_ref[...], w_ref[...], (((1,), (1,)), ((), ())), preferred_element_type=F32)
        _, vjp = jax.vjp(_swiglu, u_ref[...].astype(F32), v_ref[...].astype(F32))
        du, dv = vjp(dact)
        du_ref[...] = du.astype(MXU)
        dv_ref[...] = dv.astype(MXU)

    tile = pl.BlockSpec((tm, tk), lambda j, i: (i, j))
    return pl.pallas_call(
        body, name=name, grid=(F // tk, T // tm),
        in_specs=[pl.BlockSpec((tm, N), lambda j, i: (i, 0)), pl.BlockSpec((tk, N), lambda j, i: (j, 0)), tile, tile],
        out_specs=[tile, tile], out_shape=[jax.ShapeDtypeStruct((T, F), MXU)] * 2,
        compiler_params=_cparams(("arbitrary", "arbitrary")),
    )(dff, w2, u, v)


def _mm_nt2(dy1, w1, dy2, w2, *, name):
    T, N = dy1.shape
    K = w1.shape[0]
    tm = _pick(T, MM_ROWS[1:])
    tk = _pick(K, MM_COLS[1:])

    def body(d1_ref, w1_ref, d2_ref, w2_ref, o_ref):
        nt = lambda a, b: lax.dot_general(a[...], b[...], (((1,), (1,)), ((), ())), preferred_element_type=F32)
        o_ref[...] = nt(d1_ref, w1_ref) + nt(d2_ref, w2_ref)

    dspec = pl.BlockSpec((tm, N), lambda j, i: (i, 0))
    wspec = pl.BlockSpec((tk, N), lambda j, i: (j, 0))
    return pl.pallas_call(
        body, name=name, grid=(K // tk, T // tm), in_specs=[dspec, wspec, dspec, wspec],
        out_specs=pl.BlockSpec((tm, tk), lambda j, i: (i, j)), out_shape=jax.ShapeDtypeStruct((T, K), F32),
        compiler_params=_cparams(("arbitrary", "arbitrary")),
    )(dy1, w1, dy2, w2)


@functools.partial(jax.custom_vjp, nondiff_argnums=(1,))
def _lane_roll(x, s):
    return pltpu.roll(x, s, 1)


def _lane_roll_fwd(x, s):
    return pltpu.roll(x, s, 1), None


def _lane_roll_bwd(s, _, g):
    n = g.shape[1]
    return (pltpu.roll(g, (n - s) % n, 1),)


_lane_roll.defvjp(_lane_roll_fwd, _lane_roll_bwd)


def _rope(x, cos, sin_lo, sin_hi):
    n = x.shape[1]
    return x * cos + _lane_roll(x, n - 8) * sin_lo + _lane_roll(x, 8) * sin_hi


def _head_rms(x, gain):
    return x * lax.rsqrt(_segsum(x * x) * (1.0 / HD) + RMS_EPS) * gain


def _attn_blocks(qkv_c, qkv_p, tab_c, tab_p, qg, kg, sinks, first):
    nb = qkv_c.shape[0] // BLK
    G = 4

    def tabs(tab, n):
        return [jnp.tile(tab[:, j * 128:(j + 1) * 128], (1, n // 128)) for j in range(3)]

    qg = jnp.concatenate([qg] * NH, axis=1)
    kg = jnp.concatenate([kg] * 2, axis=1)
    q = _rope(_head_rms(qkv_c[:, :RW], qg), *tabs(tab_c, RW))
    k_in = jnp.concatenate([qkv_p[:, RW:RW + 128], qkv_c[:, RW:RW + 128]], axis=0)
    k = _rope(_head_rms(k_in, kg), *tabs(jnp.concatenate([tab_p, tab_c], axis=0), 128))
    v = jnp.concatenate([qkv_p[:, RW + 128:], qkv_c[:, RW + 128:]], axis=0)

    pile = lambda xs: jnp.concatenate([x_[None] for x_ in xs], axis=0)

    def bands(t):
        return pile([t[b * BLK:(b + 2) * BLK, kvh * HD:(kvh + 1) * HD] for kvh in range(2) for b in range(nb)])

    qs = pile([jnp.concatenate([q[b * BLK:(b + 1) * BLK, (G * kvh + g) * HD:(G * kvh + g + 1) * HD]
                                for g in range(G)], axis=0) for kvh in range(2) for b in range(nb)])
    s = _bmm(qs, bands(k), 2, 2, 1) * (HD ** -0.5)
    qi = lax.broadcasted_iota(jnp.int32, (G * BLK, 2 * BLK), 0) % BLK
    kj = lax.broadcasted_iota(jnp.int32, (G * BLK, 2 * BLK), 1)
    dist = qi + BLK - kj
    in_band = (dist >= 0) & (dist < BLK)
    pair = lax.broadcasted_iota(jnp.int32, (2 * nb, 1, 1), 0)
    no_prev = (pair % nb == 0) & first
    valid = in_band[None] & (jnp.logical_not(no_prev) | (kj >= BLK)[None])
    s = jnp.where(valid, s, NEG_INF)
    row_g = lax.broadcasted_iota(jnp.int32, (G * BLK, 1), 0) // BLK
    sink = []
    for kvh in range(2):
        col = jnp.zeros((G * BLK, 1), F32)
        for g in range(G):
            col = jnp.where(row_g == g, sinks[:, G * kvh + g:G * kvh + g + 1], col)
        sink += [col] * nb
    sink = pile(sink)
    m = lax.stop_gradient(jnp.maximum(jnp.max(s, axis=-1, keepdims=True), sink))
    e = jnp.exp(s - m)
    p = e * (1.0 / (jnp.sum(e, axis=-1, keepdims=True) + jnp.exp(sink - m)))
    o = _bmm(p, bands(v), 2, 1, 1)
    return jnp.concatenate([jnp.concatenate([o[kvh * nb + b, g * BLK:(g + 1) * BLK] for kvh in range(2)
                                             for g in range(G)], axis=1) for b in range(nb)], axis=0)


def _heads(x):
    return jnp.stack([x[:, h * HD:(h + 1) * HD] for h in range(NH)], axis=0)


def _unheads(x):
    return jnp.concatenate([x[h] for h in range(NH)], axis=1)


def _split(x, n):
    parts, rest = [], x
    for _ in range(n):
        p = rest.astype(MXU)
        parts.append(p)
        rest = rest - p.astype(F32)
    return parts


def _bdot_batched(a, b, ca, cb):
    return lax.dot_general(a, b, (((ca,), (cb,)), ((0,), (0,))), preferred_element_type=F32)


def _bmm_passes(a, b, ca, cb, passes):
    if MXU == F32:
        return lax.dot_general(a, b, (((ca,), (cb,)), ((0,), (0,))), precision=HI, preferred_element_type=F32)
    if passes == 1:
        return _bdot_batched(a.astype(MXU), b.astype(MXU), ca, cb)
    (a0, a1), (b0, b1) = _split(a, 2), _split(b, 2)
    return _bdot_batched(a0, b0, ca, cb) + (_bdot_batched(a0, b1, ca, cb) + _bdot_batched(a1, b0, ca, cb))


@functools.partial(jax.custom_vjp, nondiff_argnums=(2, 3, 4))
def _bmm(a, b, ca, cb, passes=1):
    return _bmm_passes(a, b, ca, cb, passes)


def _bmm_fwd(a, b, ca, cb, passes):
    return _bmm_passes(a, b, ca, cb, passes), (a, b)


def _bmm_bwd(ca, cb, passes, res, g):
    a, b = res
    if (ca, cb) == (2, 1):
        return _bmm_passes(g, b, 2, 2, passes), _bmm_passes(a, g, 1, 1, passes)
    if (ca, cb) == (2, 2):
        return _bmm_passes(g, b, 2, 1, passes), _bmm_passes(g, a, 1, 1, passes)
    return _bmm_passes(b, g, 2, 2, passes), _bmm_passes(a, g, 2, 1, passes)


_bmm.defvjp(_bmm_fwd, _bmm_bwd)


def _tri_dot(x, transpose):
    C = x.shape[1]
    ri = lax.broadcasted_iota(jnp.int32, (C, C), 0)
    ci = lax.broadcasted_iota(jnp.int32, (C, C), 1)
    tri = jnp.broadcast_to(((ri <= ci) if transpose else (ri >= ci)).astype(MXU), (x.shape[0], C, C))
    if MXU == F32:
        return lax.dot_general(tri, x, (((2,), (1,)), ((0,), (0,))), precision=HI, preferred_element_type=F32)
    p0, p1, p2 = _split(x, 3)
    return _bdot_batched(tri, p0, 2, 1) + (_bdot_batched(tri, p1, 2, 1) + _bdot_batched(tri, p2, 2, 1))


@jax.custom_vjp
def _cumsum_rows(x):
    return _tri_dot(x, False)


def _cumsum_rows_fwd(x):
    return _tri_dot(x, False), None


def _cumsum_rows_bwd(_, g):
    return (_tri_dot(g, True),)


_cumsum_rows.defvjp(_cumsum_rows_fwd, _cumsum_rows_bwd)

P_SCORE = 1
P_SOLVE = 1
P_STATE = 1
SCAN_CHUNKS = (4, 2, 1)


def _neumann(l):
    C = l.shape[1]
    eye = (lax.broadcasted_iota(jnp.int32, (C, C), 0) == lax.broadcasted_iota(jnp.int32, (C, C), 1)).astype(F32)
    x, lp = eye + l, l
    for _ in range(int(math.log2(C)) - 1):
        lp = _bmm(lp, lp, 2, 1, P_SOLVE)
        x = x + _bmm(x, lp, 2, 1, P_SOLVE)
    return x


@jax.custom_vjp
def _unit_lower_inverse(l):
    return _neumann(l)


def _unit_lower_inverse_fwd(l):
    x = _neumann(l)
    return x, x


def _unit_lower_inverse_bwd(x, g):
    return (_bmm(_bmm(x, g, 1, 1, P_SOLVE), x, 2, 2, P_SOLVE),)


_unit_lower_inverse.defvjp(_unit_lower_inverse_fwd, _unit_lower_inverse_bwd)


def _known_inverse(x):
    @jax.custom_vjp
    def f(l):
        return x

    f.defvjp(lambda l: (x, None), lambda _, g: (_bmm(_bmm(x, g, 1, 1, P_SOLVE), x, 2, 2, P_SOLVE),))
    return f


def _chunk(S0, r, lw, k, v, a, b, inverse=None):
    C = CHUNK
    n = r.shape[1] // C
    fold = lambda t: t.reshape(NH * n, C, HD)
    r, lw, k, v, a, b = (fold(t) for t in (r, lw, k, v, a, b))
    ri = lax.broadcasted_iota(jnp.int32, (C, C), 0)
    ci = lax.broadcasted_iota(jnp.int32, (C, C), 1)
    incl = (ri >= ci)
    strict = (ri > ci)
    eye = (ri == ci).astype(F32)
    cum = _cumsum_rows(lw)
    p_in = jnp.exp(cum)
    p_ex = jnp.exp(cum - lw)
    p_inv = jnp.exp(-cum)
    at, rt, bt, kt = a * p_ex, r * p_in, b * p_inv, k * p_inv
    lhs = jnp.concatenate([at, rt], axis=1)
    rhs_ = jnp.concatenate([bt, kt], axis=1)
    sc = _bmm(lhs, rhs_, 2, 2, P_SCORE)
    a_ab = jnp.where(strict, sc[:, :C, :C], 0.0)
    a_ak = jnp.where(strict, sc[:, :C, C:], 0.0)
    incl2 = (lax.broadcasted_iota(jnp.int32, (C, 2 * C), 0) >= lax.broadcasted_iota(jnp.int32, (C, 2 * C), 1) % C)
    a_r = jnp.where(incl2, sc[:, C:, :], 0.0)
    av = _bmm(a_ak, v, 2, 1, P_SCORE)
    x = x_all = (_unit_lower_inverse if inverse is None else _known_inverse(inverse))(a_ab)
    p_last = jnp.exp(cum[:, C - 1:C, :])
    per_chunk = lambda t: t.reshape((NH, n) + t.shape[1:])
    lhs, rhs_, a_r, av, x, v, p_last = (per_chunk(t) for t in (lhs, rhs_, a_r, av, x, v, p_last))
    S, ys = S0, []
    for c in range(n):
        s0 = _bmm(lhs[:, c], S, 2, 2, P_STATE)
        u = _bmm(x[:, c], s0[:, :C] + av[:, c], 2, 1, P_SOLVE)
        uv = jnp.concatenate([u, v[:, c]], axis=1)
        ys.append(s0[:, C:] + _bmm(a_r[:, c], uv, 2, 1, P_SCORE))
        S = (S + _bmm(uv, rhs_[:, c], 1, 1, P_STATE)) * p_last[:, c]
    return jnp.concatenate(ys, axis=1), S, x_all


def _hosting(body, ex, n_in, n_out, n_scratch, n_steps):
    if ex is None:
        return body

    def wrapped(*refs):
        a = n_in
        b = a + ex.n_in
        c = b + n_out
        d = c + ex.n_out
        e = d + n_scratch
        ex_refs = (refs[a:b], refs[c:d], refs[e:])
        grid = n_steps if isinstance(n_steps, tuple) else (n_steps,)
        first = last = True
        for ax_, size in enumerate(grid):
            first = first & (pl.program_id(ax_) == 0)
            last = last & (pl.program_id(ax_) == size - 1)

        @pl.when(first)
        def _():
            ex.start(*ex_refs)

        body(*refs[:a], *refs[b:c], *refs[d:e])

        @pl.when(last)
        def _():
            ex.wait(*ex_refs)

    return wrapped


def _hosted_args(ex):
    if ex is None:
        return [], [], [], [], []
    any_spec = pl.BlockSpec(memory_space=pl.ANY)
    return list(ex.arrays), [any_spec] * ex.n_in, [any_spec] * ex.n_out, list(ex.out_shape), list(ex.scratch)


def _scan_fwd(rw, *, name, hosted=None):
    T = rw.shape[0]
    nc = _pick(T // CHUNK, SCAN_CHUNKS)
    rows = CHUNK * nc
    n = T // rows
    h_in, h_in_specs, h_out_specs, h_out_shape, h_scratch = _hosted_args(hosted)

    def body(r_ref, lw_ref, k_ref, v_ref, a_ref, b_ref, y_ref, ck_ref, inv_ref, s_ref):
        @pl.when(pl.program_id(0) == 0)
        def _():
            s_ref[...] = jnp.zeros_like(s_ref)

        S0 = s_ref[...]
        ck_ref[0] = S0
        y, S1, inv = _chunk(S0, *[_heads(ref[...]) for ref in (r_ref, lw_ref, k_ref, v_ref, a_ref, b_ref)])
        y_ref[...] = _unheads(y)
        inv_ref[0] = inv
        s_ref[...] = S1

    col = lambda j: pl.BlockSpec((rows, RW), lambda i: (i, j))
    return pl.pallas_call(
        _hosting(body, hosted, 6, 3, 1, n), name=name, grid=(n,),
        in_specs=[col(j) for j in range(6)] + h_in_specs,
        out_specs=[pl.BlockSpec((rows, RW), lambda i: (i, 0)),
                   pl.BlockSpec((1, NH, HD, HD), lambda i: (i, 0, 0, 0)),
                   pl.BlockSpec((1, NH * nc, CHUNK, CHUNK), lambda i: (i, 0, 0, 0))] + h_out_specs,
        out_shape=[jax.ShapeDtypeStruct((T, RW), F32), jax.ShapeDtypeStruct((n, NH, HD, HD), F32),
                   jax.ShapeDtypeStruct((n, NH * nc, CHUNK, CHUNK), F32)] + h_out_shape,
        scratch_shapes=[pltpu.VMEM((NH, HD, HD), F32)] + h_scratch,
        compiler_params=_cparams(("arbitrary",)),
    )(rw, rw, rw, rw, rw, rw, *h_in)


def _scan_bwd(rw, ck, inv, dy, *, name, hosted=None):
    T = rw.shape[0]
    nc = _pick(T // CHUNK, SCAN_CHUNKS)
    rows = CHUNK * nc
    n = T // rows

    def body(r_ref, lw_ref, k_ref, v_ref, a_ref, b_ref, ck_ref, inv_ref, dy_ref, o_ref, ds_ref):
        @pl.when(pl.program_id(0) == 0)
        def _():
            ds_ref[...] = jnp.zeros_like(ds_ref)

        prim = [_heads(ref[...]) for ref in (r_ref, lw_ref, k_ref, v_ref, a_ref, b_ref)]
        known = inv_ref[0]
        _, vjp = jax.vjp(lambda S0, *p: _chunk(S0, *p, inverse=known)[:2], ck_ref[0], *prim)
        grads = vjp((_heads(dy_ref[...]), ds_ref[...]))
        ds_ref[...] = grads[0]
        o_ref[...] = jnp.concatenate([_unheads(g) for g in grads[1:]], axis=1).astype(o_ref.dtype)

    h_in, h_in_specs, h_out_specs, h_out_shape, h_scratch = _hosted_args(hosted)
    col = lambda j: pl.BlockSpec((rows, RW), lambda i: (n - 1 - i, j))
    return pl.pallas_call(
        _hosting(body, hosted, 9, 1, 1, n), name=name, grid=(n,),
        in_specs=[col(j) for j in range(6)] + [pl.BlockSpec((1, NH, HD, HD), lambda i: (n - 1 - i, 0, 0, 0)),
                                               pl.BlockSpec((1, NH * nc, CHUNK, CHUNK), lambda i: (n - 1 - i, 0, 0, 0)),
                                               pl.BlockSpec((rows, RW), lambda i: (n - 1 - i, 0))] + h_in_specs,
        out_specs=[pl.BlockSpec((rows, 6 * RW), lambda i: (n - 1 - i, 0))] + h_out_specs,
        out_shape=[jax.ShapeDtypeStruct((T, 6 * RW), MXU)] + h_out_shape,
        scratch_shapes=[pltpu.VMEM((NH, HD, HD), F32)] + h_scratch,
        compiler_params=_cparams(("arbitrary",)),
    )(rw, rw, rw, rw, rw, rw, ck, inv, dy, *h_in)


ATTN_BLOCKS = (4, 2, 1)

def _attn_fwd(qkv, tab, qg, kg, sinks, *, name, hosted=None):
    T = qkv.shape[0]
    nb = _pick(T // BLK, ATTN_BLOCKS)
    n = T // (BLK * nb)
    h_in, h_in_specs, h_out_specs, h_out_shape, h_scratch = _hosted_args(hosted)

    def body(c_ref, p_ref, tc_ref, tp_ref, qg_ref, kg_ref, s_ref, o_ref):
        o_ref[...] = _attn_blocks(c_ref[...], p_ref[...], tc_ref[...], tp_ref[...], qg_ref[...], kg_ref[...],
                                  s_ref[...], pl.program_id(0) == 0).astype(o_ref.dtype)

    cur = lambda w: pl.BlockSpec((nb * BLK, w), lambda i: (i, 0))
    prev = lambda w: pl.BlockSpec((BLK, w), lambda i: (jnp.maximum(i * nb - 1, 0), 0))
    return pl.pallas_call(
        _hosting(body, hosted, 7, 1, 0, n), name=name, grid=(n,),
        in_specs=[cur(QKV_W), prev(QKV_W), cur(3 * 128), prev(3 * 128), _full_spec(qg), _full_spec(kg),
                  _full_spec(sinks)] + h_in_specs,
        out_specs=[cur(RW)] + h_out_specs, out_shape=[jax.ShapeDtypeStruct((T, RW), MXU)] + h_out_shape,
        scratch_shapes=h_scratch,
        compiler_params=_cparams(("arbitrary",)),
    )(qkv, qkv, tab, tab, qg, kg, sinks, *h_in)


def _attn_bwd(qkv, tab, qg, kg, sinks, dy, *, name, hosted=None):
    T = qkv.shape[0]
    nb = _pick(T // BLK, ATTN_BLOCKS)
    n = T // (BLK * nb)
    h_in, h_in_specs, h_out_specs, h_out_shape, h_scratch = _hosted_args(hosted)

    def body(c_ref, p_ref, tc_ref, tp_ref, qg_ref, kg_ref, s_ref, dy_ref, dqkv_ref, dqg_ref, dkg_ref, ds_ref, carry_ref):
        i = pl.program_id(0)

        @pl.when(i == 0)
        def _():
            carry_ref[...] = jnp.zeros_like(carry_ref)
            dqg_ref[...] = jnp.zeros_like(dqg_ref)
            dkg_ref[...] = jnp.zeros_like(dkg_ref)
            ds_ref[...] = jnp.zeros_like(ds_ref)

        tc, tp = tc_ref[...], tp_ref[...]
        f = lambda c, p_, qg_, kg_, sk: _attn_blocks(c, p_, tc, tp, qg_, kg_, sk, i == n - 1)
        _, vjp = jax.vjp(f, c_ref[...], p_ref[...], qg_ref[...], kg_ref[...], s_ref[...])
        dc, dp, dqg, dkg, dsk = vjp(dy_ref[...].astype(F32))
        last = slice((nb - 1) * BLK, nb * BLK)
        dqkv_ref[...] = dc.astype(dqkv_ref.dtype)
        dqkv_ref[last, :] = (dc[last] + carry_ref[...]).astype(dqkv_ref.dtype)
        carry_ref[...] = dp
        dqg_ref[...] += dqg
        dkg_ref[...] += dkg
        ds_ref[...] += dsk

    cur = lambda w: pl.BlockSpec((nb * BLK, w), lambda i: (n - 1 - i, 0))
    prev = lambda w: pl.BlockSpec((BLK, w), lambda i: (jnp.maximum((n - 1 - i) * nb - 1, 0), 0))
    return pl.pallas_call(
        _hosting(body, hosted, 8, 4, 1, n), name=name, grid=(n,),
        in_specs=[cur(QKV_W), prev(QKV_W), cur(3 * 128), prev(3 * 128), _full_spec(qg), _full_spec(kg), _full_spec(sinks),
                  cur(RW)] + h_in_specs,
        out_specs=[cur(QKV_W), _full_spec(qg), _full_spec(kg), _full_spec(sinks)] + h_out_specs,
        out_shape=[jax.ShapeDtypeStruct((T, QKV_W), MXU), jax.ShapeDtypeStruct(qg.shape, F32),
                   jax.ShapeDtypeStruct(kg.shape, F32), jax.ShapeDtypeStruct(sinks.shape, F32)] + h_out_shape,
        scratch_shapes=[pltpu.VMEM((BLK, QKV_W), F32)] + h_scratch,
        compiler_params=_cparams(("arbitrary",)),
    )(qkv, qkv, tab, tab, qg, kg, sinks, dy, *h_in)


def _shift_down(cur, prev8, i):
    rolled = pltpu.roll(cur, 1, 0)
    first_row = jnp.where(i > 0, prev8[7:8, :], 0.0)
    row = lax.broadcasted_iota(jnp.int32, cur.shape, 0)
    return jnp.where(row == 0, first_row, rolled)


def _shift_up(cur, next8, i, n):
    tm = cur.shape[0]
    rolled = pltpu.roll(cur, tm - 1, 0)
    last_row = jnp.where(i < n - 1, next8[0:1, :], 0.0)
    row = lax.broadcasted_iota(jnp.int32, cur.shape, 0)
    return jnp.where(row == tm - 1, last_row, rolled)


def _ada_parts(ada):
    return [ada[:, j * D:(j + 1) * D] for j in range(6)]


def _rope_table(positions):
    half = HD // 8
    inv_freq = 500000.0 ** (-jnp.arange(half, dtype=F32) / half)
    lane = jnp.arange(128) % HD
    rotary = lane < 2 * half
    freq = jnp.where(rotary, inv_freq[lane % half], 0.0)
    ang = positions.astype(F32)[:, None] * freq[None, :]
    cos, sin = jnp.cos(ang), jnp.sin(ang)
    return jnp.concatenate([jnp.where(rotary, cos, 1.0), jnp.where(lane < half, -sin, 0.0),
                            jnp.where(rotary & (lane >= half), sin, 0.0)], axis=1)


GATHER_BEHIND = {"f_proj_shift": [('ffn_w3', 512, 768)], "f_proj_gates": [('ffn_w3', 768, 1024)],
                 "f_prep": [('ffn_w2', 352, 704)],
                 "f_scan": [('ffn_w1', 0, 1024), ('w_branch_a', 0, 512), ('w_branch_b', 0, 512)],
                 "f_post": [('ffn_w3', 0, 512)], "f_attn": [('ffn_w2', 0, 352), ('w_out', 0, 256)]}
LATE = ['w_out', 'w_branch_a', 'w_branch_b', 'ffn_w1', 'ffn_w3', 'ffn_w2']
BACK_ATTN = ['w_out', 'w_branch_a', 'w_branch_b', 'ffn_w2']
BACK_SCAN = ['ffn_w1', 'ffn_w3']
BACK_LAST = ['w_in', 'decay_up', 'iclr_up', 'gate_up']


def _full_weight(g, ax):
    return g.reshape(-1, g.shape[2]) if ax == 0 else jnp.concatenate([g[j] for j in range(4)], axis=1)


def _local_step(x, target, ada, tab, w, s, shards=None):
    T = x.shape[0]
    tm = _pick(T, (512, 256, 128))
    tm_wide = _pick(T, (256, 128))
    tm_vjp = _pick(T, (256, 128))
    row = lambda n, dt=F32: (n, dt, 'row')
    acc = lambda n, r=1: (n, F32, r)

    def f_norm1(x_, g, ada_):
        return _norm_mod(x_, g, ada_[:, D:2 * D], ada_[:, 0:D])

    landed = {}

    def behind(kernel_name):
        if not shards:
            return None
        return _GatherChips([shards[n] if hi - lo == shards[n].shape[0] else shards[n][lo:hi]
                             for n, lo, hi in GATHER_BEHIND[kernel_name]])

    def took(kernel_name, got):
        landed.update(zip(GATHER_BEHIND[kernel_name], got))

    def mm_behind(a_, w_, kernel_name, **kw):
        ex = behind(kernel_name)
        res = _mm_nn(a_, w_, name=kernel_name, hosted=ex, **kw)
        if ex:
            took(kernel_name, res[1:])
            return res[0]
        return res

    h1, proj, *got = _mm_then([(None, w['w_in'][:, :SHIFT_W], 'nn')], lambda i, n, y, h, *_: (h, y), [x],
                              [s['norm1_gain'], ada], [row(D, MXU), row(SHIFT_W)], tm=tm, name="f_proj_shift",
                              lhs_fn=f_norm1, hosted=behind("f_proj_shift"))
    took("f_proj_shift", got)
    proj_qkv = _mm_nn(h1, w['w_in'][:, SHIFT_W:SHIFT_W + QKV_W], name="f_proj_qkv")
    proj_g = mm_behind(h1, w['w_in'][:, SHIFT_W + QKV_W:], "f_proj_gates", out_dtype=MXU)
    prep_consts = [s['decay_w0'], s['lora_up'], s['iclr_a0'], s['gate_up'], s['k_k'], s['k_a']]

    def f_prep(i, n, cur, prev8, mu, *params):
        mixed = cur + (_shift_down(cur, prev8, i) - cur) * mu
        return (_prep(mixed, *params),)
    rw, *got = _rowwise(f_prep, [(proj, SHIFT_W)], [s['tshift_mu']] + prep_consts, [row(7 * RW)], tm=tm_wide,
                        name="f_prep", halo=[(proj, SHIFT_W, 'prev')], hosted=behind("f_prep"))
    took("f_prep", got)
    y, ck, inv, *got = _scan_fwd(rw, name="f_scan", hosted=behind("f_scan"))
    took("f_scan", got)
    post_consts = [s['lnx_gain'], s['lnx_bias'], s['r_k']]

    rkvg = [(rw, RW, j) for j in (0, 2, 3, 6)]

    def f_post(i, n, *args):
        return (_post(*args),)
    ya, *got = _rowwise(f_post, [y] + rkvg, post_consts, [row(RW, MXU)], tm=tm_wide, name="f_post",
                        hosted=behind("f_post"))
    took("f_post", got)
    yb, *got = _attn_fwd(proj_qkv, tab, s['q_norm_gain'], s['k_norm_gain'], s['attn_sinks'], name="f_attn",
                         hosted=behind("f_attn"))
    took("f_attn", got)
    w = dict(w)
    if shards:
        ax = dict(SHARDED)
        for n in LATE:
            rows = [landed[key] for key in sorted(k_ for k_ in landed if k_[0] == n)]
            w[n] = _full_weight(rows[0] if len(rows) == 1 else jnp.concatenate(rows, axis=1), ax[n])
    ma = _mm_nn(ya, w['w_branch_a'], name="f_branch_a", out_dtype=MXU)
    mb = _mm_nn(yb, w['w_branch_b'], name="f_branch_b", out_dtype=MXU)

    def f_merge(pg, ma_, mb_, x_, bias, g, ada_):
        return _merge(pg.astype(F32), ma_.astype(F32), mb_.astype(F32), bias)

    def f_res1(i, n, mo_, merged_, pg, ma_, mb_, x_, bias, g, ada_):
        x1_ = x_ + ada_[:, 2 * D:3 * D] * mo_
        return merged_, mo_, x1_, _norm_mod(x1_, g, ada_[:, 4 * D:5 * D], ada_[:, 3 * D:4 * D])
    merged, mo, x1, h2 = _mm_then([(None, w['w_out'], 'nn')], f_res1, [proj_g, ma, mb, x],
                                  [s['branch_gate_b'], s['norm2_gain'], ada],
                                  [row(D, MXU), row(D), row(D), row(D, MXU)], tm=tm, name="f_out", lhs_fn=f_merge)
    u, v, act = _ffn_in(h2, w['ffn_w1'], w['ffn_w3'], name="f_ffn_in")

    def f_loss(i, n, ff_, x1_, tgt, ada_):
        g2 = ada_[:, 5 * D:6 * D]
        err = x1_ + g2 * ff_ - tgt
        dx2 = err * (1.0 / D)
        loss = 0.5 * jnp.sum(jnp.sum(err * err, axis=1, keepdims=True) * (1.0 / D), axis=0, keepdims=True)
        return dx2, (dx2 * g2), jnp.broadcast_to(loss, (1, 128)), jnp.sum(dx2 * ff_, axis=0, keepdims=True)
    dx2, dff, loss, dgate2 = _mm_then([(act, w['ffn_w2'], 'nn')], f_loss, [x1, target], [ada],
                                      [row(D), row(D, MXU), acc(128), acc(D)], tm=tm, name="f_ffn_out")

    du, dv = _ffn_act_bwd(dff, w['ffn_w2'], u, v, name="b_ffn_out_dx")
    g_w2 = _mm_tn(act, dff, name="b_ffn_out_dw", out_dtype=MXU)
    g_w1 = _mm_tn(h2, du, name="b_ffn_w1_dw", out_dtype=MXU)
    g_w3 = _mm_tn(h2, dv, name="b_ffn_w3_dw", out_dtype=MXU)

    def b_res1(i, n, dh2_, x1_, dx2_, mo_, g, ada_):
        _, vjp = jax.vjp(_norm_mod, x1_, g, ada_[:, 4 * D:5 * D], ada_[:, 3 * D:4 * D])
        dxn, dg, dsc, dsh = vjp(dh2_)
        dx1_ = dxn + dx2_
        g1 = ada_[:, 2 * D:3 * D]
        return dx1_, dx1_ * g1, dg, dsc, dsh, jnp.sum(dx1_ * mo_, axis=0, keepdims=True)
    dx1, dmo, d_gain2, d_scale2, d_shift2, dgate1 = _mm_then(
        [(du, w['ffn_w1'], 'nt'), (dv, w['ffn_w3'], 'nt')], b_res1, [x1, dx2, mo], [s['norm2_gain'], ada],
        [row(D), row(D, MXU), acc(D), acc(D), acc(D), acc(D)], tm=tm_wide, name="b_ffn_in_dx")
    g_wout = _mm_tn(merged, dmo, name="b_out_dw", out_dtype=MXU)

    def b_merge(i, n, dm, pg, ma_, mb_, bias):
        _, vjp = jax.vjp(_merge, pg.astype(F32), ma_.astype(F32), mb_.astype(F32), bias)
        dpg, dma_, dmb_, dbias = vjp(dm)
        return dpg, dma_, dmb_, dbias
    dpg, dma, dmb, d_bias = _mm_then([(dmo, w['w_out'], 'nt')], b_merge, [proj_g, ma, mb], [s['branch_gate_b']],
                                     [row(GATE_W, MXU), row(D, MXU), row(D, MXU), acc(GATE_W)], tm=tm_wide,
                                     name="b_out_dx")
    dya = _mm_nt(dma, w['w_branch_a'], name="b_branch_a_dx")
    g_wa = _mm_tn(ya, dma, name="b_branch_a_dw", out_dtype=MXU, col_shards=4)
    dyb = _mm_nt(dmb, w['w_branch_b'], name="b_branch_b_dx", out_dtype=F32)
    g_wb = _mm_tn(yb, dmb, name="b_branch_b_dw", out_dtype=MXU, col_shards=4)
    fs = DFF // 4
    gw = dict(w_branch_a=g_wa, w_branch_b=g_wb, w_out=g_wout.reshape(4, D // 4, D),
              ffn_w1=jnp.stack([g_w1[:, j * fs:(j + 1) * fs] for j in range(4)]),
              ffn_w3=jnp.stack([g_w3[:, j * fs:(j + 1) * fs] for j in range(4)]),
              ffn_w2=g_w2.reshape(4, fs, D))
    recv = {}
    dqkv, d_qg, d_kg, d_sinks, *got = _attn_bwd(
        proj_qkv, tab, s['q_norm_gain'], s['k_norm_gain'], s['attn_sinks'], dyb, name="b_attn",
        hosted=shards and _ScatterChips([gw[n] for n in BACK_ATTN]))
    recv.update(zip(BACK_ATTN, got))

    def b_post(i, n, y_, r_, k_, v_, g_, dya_, *params):
        _, vjp = jax.vjp(_post, y_, r_, k_, v_, g_, *params)
        dy_, dr_, dk_, dv_, dg_, *dparams = vjp(dya_)
        return (dy_, jnp.concatenate([dr_, dk_, dv_, dg_], axis=1), *dparams)
    dy, drkvg, d_lnx_gain, d_lnx_bias, d_r_k = _rowwise(
        b_post, [y] + rkvg + [dya], post_consts, [row(RW), row(4 * RW, MXU), acc(RW), acc(RW), acc(RW)], tm=tm_wide,
        name="b_post")
    dscan, *got = _scan_bwd(rw, ck, inv, dy, name="b_scan",
                            hosted=shards and _ScatterChips([gw[n] for n in BACK_SCAN]))
    recv.update(zip(BACK_SCAN, got))

    def b_prep(i, n, cur, drw_, dscan_, prev8, mu, *params):
        shifted = _shift_down(cur, prev8, i)
        mixed = cur + (shifted - cur) * mu
        _, vjp = jax.vjp(_prep, mixed, *params)
        blk = lambda t, j: t[:, j * RW:(j + 1) * RW].astype(F32)
        ct = jnp.concatenate([blk(dscan_, 0) + blk(drw_, 0), blk(dscan_, 1), blk(dscan_, 2) + blk(drw_, 1),
                              blk(dscan_, 3) + blk(drw_, 2), blk(dscan_, 4), blk(dscan_, 5), blk(drw_, 3)], axis=1)
        grads = vjp(ct)
        dmixed = grads[0]
        return (dmixed, jnp.sum(dmixed * (shifted - cur), axis=0, keepdims=True)) + tuple(grads[1:])
    dmixed, d_mu, d_w0, d_lora, d_a0, d_gate_up, d_kk, d_ka = _rowwise(
        b_prep, [(proj, SHIFT_W), drkvg, dscan], [s['tshift_mu']] + prep_consts,
        [row(SHIFT_W), acc(SHIFT_W), acc(RW), acc(2 * RW, 128), acc(RW), acc(RW, 128), acc(RW), acc(RW)],
        tm=tm_vjp, name="b_prep", halo=[(proj, SHIFT_W, 'prev')])

    def b_gather(i, n, dm, dqkv_, dpg_, next8, mu):
        dcur = dm * (1.0 - mu) + _shift_up(dm, next8, i, n) * mu
        return (jnp.concatenate([dcur.astype(MXU), dqkv_, dpg_], axis=1),)
    (dproj,) = _rowwise(b_gather, [dmixed, dqkv, dpg], [s['tshift_mu']], [row(IN_W, MXU)], tm=tm_wide, name="b_gather",
                        halo=[(dmixed, SHIFT_W, 'next')])
    g_win = _mm_tn(h1, dproj, name="b_proj_dw", out_dtype=MXU, col_shards=4)

    def col_blocks(g):
        k, n = g.shape
        return g.reshape(k, 4, n // 4).transpose(1, 0, 2).astype(MXU)
    gw.update(w_in=g_win, decay_up=col_blocks(d_lora[:64, :RW]), iclr_up=col_blocks(d_lora[64:, RW:]),
              gate_up=col_blocks(d_gate_up))
    top, bottom = None, None
    if shards:
        top = _ScatterChips([gw['w_in'][:, :D // 2]] + [gw[n] for n in BACK_LAST[1:]])
        bottom = _ScatterChips([gw['w_in'][:, D // 2:3 * D // 4]])
        dh1, *got_top = _mm_nt(dproj, w['w_in'], name="b_proj_dx", hosted=top)
    else:
        dh1 = _mm_nt(dproj, w['w_in'], name="b_proj_dx")

    def b_norm1(i, n, x_, dh1_, dx1_, g, ada_):
        _, vjp = jax.vjp(_norm_mod, x_, g, ada_[:, D:2 * D], ada_[:, 0:D])
        dxn, dg, dsc, dsh = vjp(dh1_)
        return dxn + dx1_, dg, dsc, dsh
    dx, d_gain1, d_scale1, d_shift1, *got_bottom = _rowwise(
        b_norm1, [x, dh1, dx1], [s['norm1_gain'], ada], [row(D), acc(D), acc(D), acc(D)], tm=tm, name="b_norm1",
        hosted=bottom)
    if shards:
        recv.update(zip(BACK_LAST[1:], got_top[1:]))
        recv['w_in'] = [got_top[0], got_bottom[0]]

    d_ada = jnp.concatenate([d_shift1, d_scale1, dgate1, d_shift2, d_scale2, dgate2], axis=1)
    gs = dict(norm1_gain=d_gain1, norm2_gain=d_gain2, tshift_mu=d_mu, decay_w0=d_w0, iclr_a0=d_a0, k_k=d_kk, k_a=d_ka,
              r_k=d_r_k, lnx_gain=d_lnx_gain, lnx_bias=d_lnx_bias, q_norm_gain=d_qg, k_norm_gain=d_kg,
              attn_sinks=d_sinks, branch_gate_b=d_bias)
    return loss, dx, d_ada, gw, gs, recv


ANY = pl.BlockSpec(memory_space=pl.ANY)


def _place():
    x, y, c = lax.axis_index("x"), lax.axis_index("y"), lax.axis_index("c")
    return x, y, c, [(1 - x, y), (x, 1 - y), (1 - x, 1 - y)]


def _all_gather8(x_shard, *, name):
    m_per, n = x_shard.shape

    def body(x_ref, out_ref, send_sems, recv_sems, local_sem):
        x, y, c, chips = _place()
        me, sibling = (x, y, c), (x, y, 1 - c)

        def rows(px, py, pc):
            return out_ref.at[pl.ds((4 * px + 2 * py + pc) * m_per, m_per), :]

        def copy(k, block, to, src=None):
            return pltpu.make_async_remote_copy(
                src_ref=rows(*block) if src is None else src, dst_ref=rows(*block),
                send_sem=send_sems.at[k], recv_sem=recv_sems.at[k], device_id=to, device_id_type=MESH)

        mine = pltpu.make_async_copy(x_ref, rows(*me), local_sem)
        mine.start()
        first = [copy(0, me, sibling, src=x_ref)]
        first += [copy(1 + j, me, (*chip, c), src=x_ref) for j, chip in enumerate(chips)]
        for cp in first:
            cp.start()
        passed = [copy(4 + j, (*chip, c), sibling) for j, chip in enumerate(chips)]
        for j, chip in enumerate(chips):
            copy(1 + j, (*chip, c), me).wait_recv()
            passed[j].start()
        copy(0, sibling, me).wait_recv()
        for j, chip in enumerate(chips):
            copy(4 + j, (*chip, 1 - c), me).wait_recv()
        for cp in first + passed:
            cp.wait_send()
        mine.wait()

    return pl.pallas_call(
        body, name=name, out_shape=jax.ShapeDtypeStruct((8 * m_per, n), x_shard.dtype),
        in_specs=[pl.BlockSpec(memory_space=pltpu.VMEM)], out_specs=pl.BlockSpec(memory_space=pltpu.VMEM),
        scratch_shapes=[pltpu.SemaphoreType.DMA((7,)), pltpu.SemaphoreType.DMA((7,)), pltpu.SemaphoreType.DMA],
    )(x_shard)


class _GatherChips:
    def __init__(self, shards):
        n = len(shards)
        self.arrays, self.n_in, self.n_out = list(shards), n, n
        self.out_shape = [jax.ShapeDtypeStruct((4,) + s.shape, s.dtype) for s in shards]
        self.scratch = [pltpu.SemaphoreType.DMA((3 * n,)), pltpu.SemaphoreType.DMA((3 * n,)),
                        pltpu.SemaphoreType.DMA((n,))]

    def _copies(self, x_refs, out_refs, sems, receiving):
        send_sems, recv_sems, local_sems = sems
        x, y, c, chips = _place()
        s_me = 2 * x + y
        n = self.n_in

        def copy(a, k, s):
            return pltpu.make_async_remote_copy(
                src_ref=x_refs[a], dst_ref=out_refs[a].at[s], send_sem=send_sems.at[3 * a + k],
                recv_sem=recv_sems.at[3 * a + k], device_id=(*chips[k], c), device_id_type=MESH)

        mine = [pltpu.make_async_copy(x_refs[a], out_refs[a].at[s_me], local_sems.at[a]) for a in range(n)]
        sends = [copy(a, k, s_me) for a in range(n) for k in range(3)]
        if not receiving:
            return mine, sends
        return mine, sends, [copy(a, k, 2 * px + py) for a in range(n) for k, (px, py) in enumerate(chips)]

    def start(self, x_refs, out_refs, sems):
        mine, sends = self._copies(x_refs, out_refs, sems, False)
        for cp in mine + sends:
            cp.start()

    def wait(self, x_refs, out_refs, sems):
        mine, sends, recvs = self._copies(x_refs, out_refs, sems, True)
        for cp in recvs:
            cp.wait_recv()
        for cp in sends:
            cp.wait_send()
        for cp in mine:
            cp.wait()


class _GatherChipsHalved(_GatherChips):
    def __init__(self, shards):
        super().__init__(shards)
        n = self.n_in
        self.scratch = [pltpu.SemaphoreType.DMA((6 * n,)), pltpu.SemaphoreType.DMA((6 * n,)),
                        pltpu.SemaphoreType.DMA((n,))]

    def _copies(self, x_refs, out_refs, sems, receiving):
        send_sems, recv_sems, local_sems = sems
        x, y, c, chips = _place()
        s_me = 2 * x + y
        n = self.n_in

        def half(a, who):
            rows = x_refs[a].shape[0] // 2
            return pl.ds(who * rows, rows)

        def over_chips(a, k, s):
            return pltpu.make_async_remote_copy(
                src_ref=x_refs[a].at[half(a, c)], dst_ref=out_refs[a].at[s, half(a, c)],
                send_sem=send_sems.at[3 * a + k], recv_sem=recv_sems.at[3 * a + k],
                device_id=(*chips[k], c), device_id_type=MESH)

        def to_sibling(a, k, s, who):
            return pltpu.make_async_remote_copy(
                src_ref=out_refs[a].at[s, half(a, who)], dst_ref=out_refs[a].at[s, half(a, who)],
                send_sem=send_sems.at[3 * n + 3 * a + k], recv_sem=recv_sems.at[3 * n + 3 * a + k],
                device_id=(x, y, 1 - c), device_id_type=MESH)

        mine = [pltpu.make_async_copy(x_refs[a], out_refs[a].at[s_me], local_sems.at[a]) for a in range(n)]
        sends = [over_chips(a, k, s_me) for a in range(n) for k in range(3)]
        if not receiving:
            return mine, sends
        pairs = [(a, k, 2 * px + py) for a in range(n) for k, (px, py) in enumerate(chips)]
        landed = [over_chips(a, k, s) for a, k, s in pairs]
        passed_on = [to_sibling(a, k, s, c) for a, k, s in pairs]
        from_sibling = [to_sibling(a, k, s, 1 - c) for a, k, s in pairs]
        return mine, sends, landed, passed_on, from_sibling

    def wait(self, x_refs, out_refs, sems):
        mine, sends, landed, passed_on, from_sibling = self._copies(x_refs, out_refs, sems, True)
        for got, fwd in zip(landed, passed_on, strict=True):
            got.wait_recv()
            fwd.start()
        for cp in from_sibling:
            cp.wait_recv()
        for cp in sends + passed_on:
            cp.wait_send()
        for cp in mine:
            cp.wait()


class _ScatterChips:
    def __init__(self, parts):
        n = len(parts)
        self.arrays, self.n_in, self.n_out = list(parts), n, n
        self.out_shape = [jax.ShapeDtypeStruct((3,) + p.shape[1:], p.dtype) for p in parts]
        self.scratch = [pltpu.SemaphoreType.DMA((3 * n,)), pltpu.SemaphoreType.DMA((3 * n,))]

    def _copies(self, g_refs, out_refs, sems):
        send_sems, recv_sems = sems
        x, y, c, chips = _place()
        return [pltpu.make_async_remote_copy(
            src_ref=g_refs[a].at[2 * px + py], dst_ref=out_refs[a].at[k], send_sem=send_sems.at[3 * a + k],
            recv_sem=recv_sems.at[3 * a + k], device_id=(px, py, c), device_id_type=MESH)
            for a in range(self.n_in) for k, (px, py) in enumerate(chips)]

    def start(self, g_refs, out_refs, sems):
        for cp in self._copies(g_refs, out_refs, sems):
            cp.start()

    def wait(self, g_refs, out_refs, sems):
        sends = self._copies(g_refs, out_refs, sems)
        for cp in sends:
            cp.wait_recv()
        for cp in sends:
            cp.wait_send()


def _exchange_call(ex, *, name):
    def body(*refs):
        parts = (refs[:ex.n_in], refs[ex.n_in:ex.n_in + ex.n_out], refs[ex.n_in + ex.n_out:])
        ex.start(*parts)
        ex.wait(*parts)

    return pl.pallas_call(body, name=name, out_shape=ex.out_shape, in_specs=[ANY] * ex.n_in,
                          out_specs=[ANY] * ex.n_out, scratch_shapes=ex.scratch)(*ex.arrays)


class _SwapSibling:
    def __init__(self, vs):
        n = len(vs)
        self.arrays, self.n_in, self.n_out = list(vs), n, n
        self.out_shape = [jax.ShapeDtypeStruct(v.shape, v.dtype) for v in vs]
        self.scratch = [pltpu.SemaphoreType.DMA((n,)), pltpu.SemaphoreType.DMA((n,))]

    def _copies(self, v_refs, out_refs, sems):
        send_sems, recv_sems = sems
        x, y, c, _ = _place()
        return [pltpu.make_async_remote_copy(src_ref=v_refs[a], dst_ref=out_refs[a], send_sem=send_sems.at[a],
                                             recv_sem=recv_sems.at[a], device_id=(x, y, 1 - c), device_id_type=MESH)
                for a in range(self.n_in)]

    def start(self, v_refs, out_refs, sems):
        for cp in self._copies(v_refs, out_refs, sems):
            cp.start()

    def wait(self, v_refs, out_refs, sems):
        for cp in self._copies(v_refs, out_refs, sems):
            cp.wait()


class _Both:
    def __init__(self, first, second):
        self.parts = (first, second)
        self.arrays = first.arrays + second.arrays
        self.n_in, self.n_out = first.n_in + second.n_in, first.n_out + second.n_out
        self.out_shape = first.out_shape + second.out_shape
        self.scratch = first.scratch + second.scratch

    def _split(self, in_refs, out_refs, sems):
        a, b = self.parts
        return ((a, in_refs[:a.n_in], out_refs[:a.n_out], sems[:len(a.scratch)]),
                (b, in_refs[a.n_in:], out_refs[a.n_out:], sems[len(a.scratch):]))

    def start(self, in_refs, out_refs, sems):
        for ex, *refs in self._split(in_refs, out_refs, sems):
            ex.start(*refs)

    def wait(self, in_refs, out_refs, sems):
        for ex, *refs in self._split(in_refs, out_refs, sems):
            ex.wait(*refs)


def _sum_parts(own, others, *, name):
    R, C = own.shape
    tm = _pick(R, (256, 128, 64))

    def body(own_ref, o0_ref, o1_ref, o2_ref, out_ref):
        tot = own_ref[...].astype(F32)
        for ref in (o0_ref, o1_ref, o2_ref):
            tot = tot + ref[...].astype(F32)
        out_ref[...] = tot

    part = lambda k: pl.BlockSpec((None, tm, C), lambda i: (k, i, 0))
    return pl.pallas_call(
        body, name=name, grid=(R // tm,),
        in_specs=[pl.BlockSpec((tm, C), lambda i: (i, 0)), part(0), part(1), part(2)],
        out_specs=pl.BlockSpec((tm, C), lambda i: (i, 0)), out_shape=jax.ShapeDtypeStruct((R, C), F32),
        compiler_params=_cparams(("arbitrary",)),
    )(own, others, others, others)


def _adam_math(w_, m_, v_, g):
    m2 = ADAM_B1 * m_ + (1.0 - ADAM_B1) * g
    v2 = ADAM_B2 * v_ + (1.0 - ADAM_B2) * jnp.square(g)
    m_hat = m2 / (1.0 - ADAM_B1 ** ADAM_STEP)
    v_hat = v2 / (1.0 - ADAM_B2 ** ADAM_STEP)
    delta = -ADAM_LR * (m_hat / (jnp.sqrt(v_hat) + ADAM_EPS) + ADAM_WD * w_)
    return delta, m2, v2


SMALL_SLOTS = 16
SMALL_COLS = 6 * D


def _pack_small(grads, *, name):
    n = len(grads)

    def body(*refs):
        out_ref = refs[n]
        out_ref[...] = jnp.zeros_like(out_ref)
        for i, ref in enumerate(refs[:n]):
            out_ref[i:i + 1, 0:ref.shape[1]] = ref[...]

    return pl.pallas_call(body, name=name, out_shape=jax.ShapeDtypeStruct((SMALL_SLOTS, SMALL_COLS), F32))(*grads)


def _adamw_small(ws, ms, vs, gathered, *, name):
    n = len(ws)

    def total(g_ref, i, nc):
        g = g_ref[i:i + 1, 0:nc]
        for d in range(1, 8):
            g = g + g_ref[d * SMALL_SLOTS + i:d * SMALL_SLOTS + i + 1, 0:nc]
        return g

    def body(*refs):
        w_refs, m_refs, v_refs, g_ref = refs[:n], refs[n:2 * n], refs[2 * n:3 * n], refs[3 * n]
        outs = refs[3 * n + 1:]
        for i in range(n):
            g = total(g_ref, i, w_refs[i].shape[1])
            delta, m2, v2 = _adam_math(w_refs[i][...], m_refs[i][...], v_refs[i][...], g)
            for k, val in enumerate((g, delta, m2, v2)):
                outs[k * n + i][...] = val
        outs[4 * n][...] = total(g_ref, n, 128)

    shapes = [jax.ShapeDtypeStruct(w.shape, F32) for w in ws]
    res = pl.pallas_call(body, name=name, out_shape=shapes * 4 + [jax.ShapeDtypeStruct((1, 128), F32)],
                         compiler_params=pltpu.CompilerParams(vmem_limit_bytes=VMEM_LIMIT))(*ws, *ms, *vs, gathered)
    return [res[k * n:(k + 1) * n] for k in range(4)], res[4 * n]


def _adamw(w, m, v, gparts, *, tm, name, hosted=None):
    def fn(i, n, w_, m_, v_, *gs):
        g = gs[0]
        for p in gs[1:]:
            g = g + p
        return (g,) + _adam_math(w_, m_, v_, g)
    nc = w.shape[1]
    return _rowwise(fn, [w, m, v] + list(gparts), [], [(nc, F32, 'row')] * 4, tm=tm, name=name, hosted=hosted)


WEIGHTS = ['ada_w', 'ada_b', 'norm1_gain', 'norm2_gain', 'w_in', 'tshift_mu', 'decay_w0', 'decay_up', 'iclr_a0',
           'iclr_up', 'gate_up', 'k_k', 'k_a', 'r_k', 'lnx_gain', 'lnx_bias', 'q_norm_gain', 'k_norm_gain', 'attn_sinks',
           'branch_gate_b', 'w_branch_a', 'w_branch_b', 'w_out', 'ffn_w1', 'ffn_w3', 'ffn_w2']
SHARDED = [('w_in', 1), ('decay_up', 1), ('iclr_up', 1), ('gate_up', 1), ('w_branch_a', 1), ('w_branch_b', 1),
           ('w_out', 0), ('ffn_w1', 1), ('ffn_w3', 1), ('ffn_w2', 0)]
SMALL = ['ada_b', 'norm1_gain', 'norm2_gain', 'tshift_mu', 'decay_w0', 'iclr_a0', 'k_k', 'k_a', 'r_k', 'lnx_gain',
         'lnx_bias', 'q_norm_gain', 'k_norm_gain', 'attn_sinks', 'branch_gate_b']


def kernel(x, c, positions, ada_w, ada_b, norm1_gain, norm2_gain, w_in, tshift_mu, decay_w0, decay_up, iclr_a0, iclr_up, gate_up, k_k, k_a, r_k, lnx_gain, lnx_bias, q_norm_gain, k_norm_gain, attn_sinks, branch_gate_b, w_branch_a, w_branch_b, w_out, ffn_w1, ffn_w3, ffn_w2, loss_target, m_ada_w, m_ada_b, m_norm1_gain, m_norm2_gain, m_w_in, m_tshift_mu, m_decay_w0, m_decay_up, m_iclr_a0, m_iclr_up, m_gate_up, m_k_k, m_k_a, m_r_k, m_lnx_gain, m_lnx_bias, m_q_norm_gain, m_k_norm_gain, m_attn_sinks, m_branch_gate_b, m_w_branch_a, m_w_branch_b, m_w_out, m_ffn_w1, m_ffn_w3, m_ffn_w2, v_ada_w, v_ada_b, v_norm1_gain, v_norm2_gain, v_w_in, v_tshift_mu, v_decay_w0, v_decay_up, v_iclr_a0, v_iclr_up, v_gate_up, v_k_k, v_k_a, v_r_k, v_lnx_gain, v_lnx_bias, v_q_norm_gain, v_k_norm_gain, v_attn_sinks, v_branch_gate_b, v_w_branch_a, v_w_branch_b, v_w_out, v_ffn_w1, v_ffn_w3, v_ffn_w2):
    a = dict(locals())
    W = {n: a[n] for n in WEIGHTS}
    M = {n: a['m_' + n] for n in WEIGHTS}
    V = {n: a['v_' + n] for n in WEIGHTS}
    xi, yi, ci = lax.axis_index("x"), lax.axis_index("y"), lax.axis_index("c")
    me = 4 * xi + 2 * yi + ci
    shard = 2 * xi + yi
    mat = lambda t: t.reshape(t.shape[-2], t.shape[-1])
    sharded = [n for n, _ in SHARDED]

    ax = dict(SHARDED)
    late = LATE
    early = [n for n in sharded if n not in late]
    shards = {n: mat(W[n]).astype(MXU) for n in sharded}
    gathered = _exchange_call(_GatherChipsHalved([shards[n] for n in early]), name="gather_weights")
    full = {n: _full_weight(g, ax[n]) for n, g in zip(early, gathered, strict=True)}

    c_all = _all_gather8(jnp.broadcast_to(c, (8, D)), name="gather_c")[0::8]
    pad_rows = lambda t: jnp.concatenate([t, jnp.zeros((BLK - 8, t.shape[1]), t.dtype)])
    c_all = pad_rows(c_all.astype(MXU))
    ada_cols = _mm_nn(c_all, mat(ada_w).astype(MXU), name="f_ada")[:8]
    ada_all = _all_gather8(ada_cols, name="gather_ada").reshape(2, 2, 2, 8, 6 * D // 4)
    ada_mine = lax.dynamic_index_in_dim(ada_all[:, :, 0], me, axis=2, keepdims=False)
    ada = ada_mine.reshape(1, 6 * D) + mat(ada_b)

    zero = jnp.zeros((64, RW), MXU)
    lora = jnp.concatenate([jnp.concatenate([full['decay_up'], zero], axis=1),
                            jnp.concatenate([zero, full['iclr_up']], axis=1)], axis=0)
    s = {n: W[n].reshape(1, -1) for n in SMALL if n != 'ada_b'}
    s['lora_up'] = lora.astype(F32)
    s['gate_up'] = full['gate_up'].astype(F32)
    tab = _rope_table(positions.reshape(-1))
    loss, dx, d_ada, gw, gs, from_chips = _local_step(x[0], loss_target[0], ada, tab, dict(w_in=full['w_in']), s,
                                                      shards={n: shards[n] for n in late})

    gs['ada_b'] = d_ada
    gsmall = _pack_small([gs[n] for n in SMALL] + [loss], name="pack_small_grads")
    gsmall_all = _all_gather8(gsmall, name="gather_small_grads")
    row = lambda src: [src[n].reshape(1, -1) for n in SMALL]
    sm_out, loss = _adamw_small(row(W), row(M), row(V), gsmall_all, name="adamw_small")
    sm_out = [{n: o.reshape(W[n].shape) for n, o in zip(SMALL, outs_k, strict=True)} for outs_k in sm_out]
    loss = loss[0, 0]

    d_ada_all = gsmall_all[0::SMALL_SLOTS]
    d_ada_cols = lax.dynamic_slice_in_dim(d_ada_all, shard * (6 * D // 4), 6 * D // 4, axis=1)
    g_ada_w = _mm_tn(c_all, pad_rows(d_ada_cols.astype(MXU)), name="b_ada")

    rest = [n for n in sharded if n != 'w_in']
    parts = {n: _sum_parts(lax.dynamic_index_in_dim(gw[n], shard, axis=0, keepdims=False), from_chips[n],
                           name="sum_" + n) for n in rest}
    tail = _Both(_ScatterChips([gw['w_in'][:, 3 * D // 4:]]), _SwapSibling([parts[n] for n in rest]))
    res = _adamw(mat(ada_w), mat(m_ada_w), mat(v_ada_w), [g_ada_w], tm=256, name="adamw_ada", hosted=tail)
    ada_out, last_quarter, others = res[:4], res[4], dict(zip(rest, res[5:], strict=True))
    parts['w_in'] = _sum_parts(lax.dynamic_index_in_dim(gw['w_in'], shard, axis=0, keepdims=False),
                               jnp.concatenate(from_chips['w_in'] + [last_quarter], axis=1), name="sum_w_in")
    others['w_in'] = _exchange_call(_SwapSibling([parts['w_in']]), name="swap_w_in")[0]
    sh_out = {}
    for n in sharded:
        part, other = parts[n], others[n]
        sh_out[n] = _adamw(mat(W[n]), mat(M[n]), mat(V[n]), [part, other], tm=_pick(part.shape[0], (256, 128, 64)),
                           name="adamw_" + n)

    def leaf(k, n):
        if n == 'ada_w':
            return ada_out[k].reshape(W[n].shape)
        if n in sharded:
            return sh_out[n][k].reshape(W[n].shape)
        return sm_out[k][n]
    outs = [leaf(k, n) for k in range(4) for n in WEIGHTS]
    return (loss, dx[None], *outs)
```

```python
import functools
import math

import jax
import jax.numpy as jnp
from jax import lax
from jax.experimental import pallas as pl
from jax.experimental.pallas import tpu as pltpu

F32 = jnp.float32
BF16 = jnp.bfloat16
MXU = BF16
HI = lax.Precision.HIGHEST

D = 1024
HD = 64
NH = 8
RW = NH * HD
SHIFT_W = 3 * RW + 64 + 64 + 128
QKV_W = RW + 2 * 128
GATE_W = 2 * D
IN_W = SHIFT_W + QKV_W + GATE_W
DFF = 2816
BLK = 128
CHUNK = 64
RMS_EPS = 1e-6
GN_EPS = 64e-5
NEG_INF = -1e30
ADAM_LR, ADAM_B1, ADAM_B2, ADAM_EPS, ADAM_WD, ADAM_STEP = 0.001, 0.9, 0.999, 1e-08, 0.01, 10
VMEM_LIMIT = 56 * 1024 * 1024
MESH = pl.DeviceIdType.MESH


def _cparams(sem=None):
    return pltpu.CompilerParams(dimension_semantics=sem, vmem_limit_bytes=VMEM_LIMIT)


def _full_spec(a):
    nd = a.ndim
    return pl.BlockSpec(a.shape, lambda *_: (0,) * nd)


def _rowwise(fn, rows, consts, outs, *, tm, name, halo=(), hosted=None):
    rows = [(a + (0,))[:3] if isinstance(a, tuple) else (a, a.shape[1], 0) for a in rows]
    T = rows[0][0].shape[0]
    assert T % tm == 0 and tm % 8 == 0
    n_tiles = T // tm
    n_in = len(rows) + len(halo) + len(consts)
    in_specs = [pl.BlockSpec((tm, nc), lambda i, j=j: (i, j)) for _, nc, j in rows]
    args = [a for a, _, _ in rows]
    for a, nc, kind in halo:
        if kind == 'prev':
            in_specs.append(pl.BlockSpec((8, nc), lambda i: (jnp.maximum(i * (tm // 8) - 1, 0), 0)))
        else:
            in_specs.append(pl.BlockSpec((8, nc), lambda i: (jnp.minimum((i + 1) * (tm // 8), T // 8 - 1), 0)))
        args.append(a)
    in_specs += [_full_spec(a) for a in consts]
    args += list(consts)
    out_shape, out_specs = [], []
    for ncols, dtype, kind in outs:
        if kind == 'row':
            out_shape.append(jax.ShapeDtypeStruct((T, ncols), dtype))
            out_specs.append(pl.BlockSpec((tm, ncols), lambda i: (i, 0)))
        else:
            out_shape.append(jax.ShapeDtypeStruct((kind, ncols), dtype))
            out_specs.append(pl.BlockSpec((kind, ncols), lambda i: (0, 0)))

    def body(*refs):
        i = pl.program_id(0)
        vals = [r[...] for r in refs[:n_in]]
        res = fn(i, n_tiles, *vals)
        for (ncols, dtype, kind), o_ref, val in zip(outs, refs[n_in:], res, strict=True):
            if kind == 'row':
                o_ref[...] = val.astype(dtype)
            else:
                @pl.when(i == 0)
                def _():
                    o_ref[...] = jnp.zeros_like(o_ref)
                o_ref[...] += val.astype(dtype)

    h_in, h_in_specs, h_out_specs, h_out_shape, h_scratch = _hosted_args(hosted)
    res = pl.pallas_call(
        _hosting(body, hosted, n_in, len(outs), 0, n_tiles), name=name, grid=(n_tiles,),
        in_specs=in_specs + h_in_specs, out_specs=out_specs + h_out_specs, out_shape=out_shape + h_out_shape,
        scratch_shapes=h_scratch, compiler_params=_cparams(("arbitrary",)),
    )(*args, *h_in)
    return res


def _pick(n, cands):
    for c in cands:
        if n % c == 0:
            return c
    return n


MM_ROWS = (1024, 512, 256, 128)
MM_COLS = (1536, 1408, 1024, 896, 768, 512, 256, 128)
MM_WIDE = 3000


def _mm_nn(a, w, *, name, out_dtype=F32, hosted=None):
    T, K = a.shape
    N = w.shape[1]
    tm = _pick(T, MM_ROWS)
    tn = _pick(N, MM_COLS)
    grid = (N // tn, T // tm)
    h_in, h_in_specs, h_out_specs, h_out_shape, h_scratch = _hosted_args(hosted)

    def body(a_ref, w_ref, o_ref):
        o_ref[...] = jnp.dot(a_ref[...], w_ref[...], preferred_element_type=F32).astype(out_dtype)

    res = pl.pallas_call(
        _hosting(body, hosted, 2, 1, 0, grid), name=name, grid=grid,
        in_specs=[pl.BlockSpec((tm, K), lambda j, i: (i, 0)), pl.BlockSpec((K, tn), lambda j, i: (0, j))] + h_in_specs,
        out_specs=[pl.BlockSpec((tm, tn), lambda j, i: (i, j))] + h_out_specs,
        out_shape=[jax.ShapeDtypeStruct((T, N), out_dtype)] + h_out_shape, scratch_shapes=h_scratch,
        compiler_params=_cparams(("arbitrary", "arbitrary")),
    )(a, w, *h_in)
    return res if hosted else res[0]


def _mm_nt(dy, w, *, name, out_dtype=F32, hosted=None):
    T, N = dy.shape
    K = w.shape[0]
    tm = _pick(T, MM_ROWS if N <= MM_WIDE else MM_ROWS[1:])
    tk = _pick(K, MM_COLS[1:])
    grid = (K // tk, T // tm)
    h_in, h_in_specs, h_out_specs, h_out_shape, h_scratch = _hosted_args(hosted)

    def body(dy_ref, w_ref, o_ref):
        o_ref[...] = lax.dot_general(dy_ref[...], w_ref[...], (((1,), (1,)), ((), ())),
                                     preferred_element_type=F32).astype(out_dtype)

    res = pl.pallas_call(
        _hosting(body, hosted, 2, 1, 0, grid), name=name, grid=grid,
        in_specs=[pl.BlockSpec((tm, N), lambda j, i: (i, 0)), pl.BlockSpec((tk, N), lambda j, i: (j, 0))] + h_in_specs,
        out_specs=[pl.BlockSpec((tm, tk), lambda j, i: (i, j))] + h_out_specs,
        out_shape=[jax.ShapeDtypeStruct((T, K), out_dtype)] + h_out_shape, scratch_shapes=h_scratch,
        compiler_params=_cparams(("arbitrary", "arbitrary")),
    )(dy, w, *h_in)
    return res if hosted else res[0]


def _mm_tn(a, dy, *, name, out_dtype=F32, col_shards=None):
    T, K = a.shape
    N = dy.shape[1]
    tm = _pick(T, MM_ROWS)
    tn = N // col_shards if col_shards else _pick(N, MM_COLS[1:])
    n_t = T // tm

    def body(a_ref, dy_ref, o_ref, acc_ref):
        i = pl.program_id(1)

        @pl.when(i == 0)
        def _():
            acc_ref[...] = jnp.zeros_like(acc_ref)

        acc_ref[...] += lax.dot_general(a_ref[...], dy_ref[...], (((0,), (0,)), ((), ())), preferred_element_type=F32)

        @pl.when(i == n_t - 1)
        def _():
            o_ref[...] = acc_ref[...].astype(out_dtype)

    if col_shards:
        out_specs = pl.BlockSpec((None, K, tn), lambda j, i: (j, 0, 0))
        out_shape = jax.ShapeDtypeStruct((col_shards, K, tn), out_dtype)
    else:
        out_specs = pl.BlockSpec((K, tn), lambda j, i: (0, j))
        out_shape = jax.ShapeDtypeStruct((K, N), out_dtype)
    return pl.pallas_call(
        body, name=name, grid=(N // tn, n_t),
        in_specs=[pl.BlockSpec((tm, K), lambda j, i: (i, 0)), pl.BlockSpec((tm, tn), lambda j, i: (i, j))],
        out_specs=out_specs, out_shape=out_shape, scratch_shapes=[pltpu.VMEM((K, tn), F32)],
        compiler_params=_cparams(("arbitrary", "arbitrary")),
    )(a, dy)


def _mm_then(products, fn, rows, consts, outs, *, tm, name, lhs_fn=None, hosted=None):
    T = (rows[0] if lhs_fn else products[0][0]).shape[0]
    n_tiles = T // tm
    in_specs, args = [], []
    for a, w, _ in products:
        if a is not None:
            in_specs.append(pl.BlockSpec((tm, a.shape[1]), lambda i: (i, 0)))
            args.append(a)
        in_specs.append(_full_spec(w))
        args.append(w)
    n_w = len(args)
    in_specs += [pl.BlockSpec((tm, a.shape[1]), lambda i: (i, 0)) for a in rows] + [_full_spec(c_) for c_ in consts]
    args += list(rows) + list(consts)
    n_in = len(args)
    out_shape, out_specs = [], []
    for ncols, dtype, kind in outs:
        if kind == 'row':
            out_shape.append(jax.ShapeDtypeStruct((T, ncols), dtype))
            out_specs.append(pl.BlockSpec((tm, ncols), lambda i: (i, 0)))
        else:
            out_shape.append(jax.ShapeDtypeStruct((kind, ncols), dtype))
            out_specs.append(pl.BlockSpec((kind, ncols), lambda i: (0, 0)))

    def body(*refs):
        i = pl.program_id(0)
        tiles = [r[...] for r in refs[n_w:n_in]]
        made = []
        if lhs_fn:
            made = lhs_fn(*tiles)
            made = list(made) if isinstance(made, tuple) else [made]
            made[0] = made[0].astype(MXU)
        y, pos = None, 0
        for a, _, form in products:
            if a is None:
                lhs = made[0]
            else:
                lhs, pos = refs[pos][...], pos + 1
            dims = (((1,), (0,)), ((), ())) if form == 'nn' else (((1,), (1,)), ((), ()))
            t = lax.dot_general(lhs, refs[pos][...], dims, preferred_element_type=F32)
            pos += 1
            y = t if y is None else y + t
        res = fn(i, n_tiles, y, *made, *tiles)
        for (ncols, dtype, kind), o_ref, val in zip(outs, refs[n_in:], res, strict=True):
            if kind == 'row':
                o_ref[...] = val.astype(dtype)
            else:
                @pl.when(i == 0)
                def _():
                    o_ref[...] = jnp.zeros_like(o_ref)
                o_ref[...] += val.astype(dtype)

    h_in, h_in_specs, h_out_specs, h_out_shape, h_scratch = _hosted_args(hosted)
    return pl.pallas_call(
        _hosting(body, hosted, n_in, len(outs), 0, n_tiles), name=name, grid=(n_tiles,),
        in_specs=in_specs + h_in_specs, out_specs=out_specs + h_out_specs, out_shape=out_shape + h_out_shape,
        scratch_shapes=h_scratch, compiler_params=_cparams(("arbitrary",)))(*args, *h_in)


def _seg_ones(n):
    r = lax.broadcasted_iota(jnp.int32, (n, n), 0) // HD
    c = lax.broadcasted_iota(jnp.int32, (n, n), 1) // HD
    return (r == c).astype(F32)


def _segsum_raw(x):
    ones = _seg_ones(x.shape[1])
    if MXU == F32:
        return jnp.dot(x, ones, precision=HI, preferred_element_type=F32)
    hi = x.astype(MXU)
    lo = (x - hi.astype(F32)).astype(MXU)
    ones = ones.astype(MXU)
    return jnp.dot(hi, ones, preferred_element_type=F32) + jnp.dot(lo, ones, preferred_element_type=F32)


@jax.custom_vjp
def _segsum(x):
    return _segsum_raw(x)


def _segsum_fwd(x):
    return _segsum_raw(x), None


def _segsum_bwd(_, g):
    return (_segsum_raw(g),)


_segsum.defvjp(_segsum_fwd, _segsum_bwd)


def _mxu(x):
    return x.astype(MXU)


@jax.custom_vjp
def _bdot(a, b):
    return jnp.dot(_mxu(a), _mxu(b), preferred_element_type=F32)


def _bdot_fwd(a, b):
    return _bdot(a, b), (a, b)


def _bdot_bwd(res, g):
    a, b = res
    da = lax.dot_general(_mxu(g), _mxu(b), (((1,), (1,)), ((), ())), preferred_element_type=F32)
    db = lax.dot_general(_mxu(a), _mxu(g), (((0,), (0,)), ((), ())), preferred_element_type=F32)
    return da.astype(a.dtype), db.astype(b.dtype)


_bdot.defvjp(_bdot_fwd, _bdot_bwd)


def _sigmoid(x):
    return 1.0 / (1.0 + jnp.exp(-x))


def _softplus(x):
    return jnp.maximum(x, 0.0) + jnp.log(1.0 + jnp.exp(jnp.minimum(x, -x)))


def _norm_mod(x, gain, scale, shift):
    inv = lax.rsqrt(jnp.mean(x * x, axis=-1, keepdims=True) + RMS_EPS)
    return (x * inv) * gain * (1.0 + scale) + shift


def _prep(mixed, decay_w0, lora_up, iclr_a0, gate_up, k_k, k_a):
    r = mixed[:, 0:RW]
    k = mixed[:, RW:2 * RW]
    v = mixed[:, 2 * RW:3 * RW]
    z = mixed[:, 3 * RW:3 * RW + 128]
    xg = mixed[:, 3 * RW + 128:]
    lane = lax.broadcasted_iota(jnp.int32, z.shape, 1)
    tz = jnp.where(lane < 64, jnp.tanh(z), z)
    lo = _bdot(tz, lora_up)
    w_log = -_softplus(-(decay_w0 + lo[:, :RW])) - 0.5
    lw = -jnp.exp(w_log)
    a_ic = _sigmoid(iclr_a0 + lo[:, RW:])
    g = _bdot(_sigmoid(xg), gate_up)
    kk = k * k_k
    kk = kk / jnp.maximum(jnp.sqrt(_segsum(kk * kk)), 1e-12)
    k_mod = k * (1.0 + (a_ic - 1.0) * k_a)
    return jnp.concatenate([r, lw, k_mod, v, -kk, kk * a_ic, g], axis=1)


def _post(y, r, k, v, g, lnx_gain, lnx_bias, r_k):
    mu = _segsum(y) * (1.0 / HD)
    yc = y - mu
    var = _segsum(yc * yc) * (1.0 / HD)
    yn = yc * lax.rsqrt(var + GN_EPS) * lnx_gain + lnx_bias
    bonus = _segsum(r * k * r_k) * v
    return (yn + bonus) * g


def _merge(pg, ma, mb, bias):
    gates = _sigmoid(pg + bias)
    return gates[:, :D] * ma + gates[:, D:] * mb


def _swiglu(u, v):
    return u * _sigmoid(u) * v


def _ffn_in(h, w1, w3, *, name):
    T, K = h.shape
    F = w1.shape[1]
    tm = _pick(T, MM_ROWS)
    tn = _pick(F, MM_COLS[1:])

    def body(h_ref, w1_ref, w3_ref, u_ref, v_ref, a_ref):
        u = jnp.dot(h_ref[...], w1_ref[...], preferred_element_type=F32).astype(MXU)
        v = jnp.dot(h_ref[...], w3_ref[...], preferred_element_type=F32).astype(MXU)
        u_ref[...] = u
        v_ref[...] = v
        a_ref[...] = _swiglu(u.astype(F32), v.astype(F32)).astype(MXU)

    wspec = pl.BlockSpec((K, tn), lambda j, i: (0, j))
    ospec = pl.BlockSpec((tm, tn), lambda j, i: (i, j))
    return pl.pallas_call(
        body, name=name, grid=(F // tn, T // tm),
        in_specs=[pl.BlockSpec((tm, K), lambda j, i: (i, 0)), wspec, wspec],
        out_specs=[ospec] * 3, out_shape=[jax.ShapeDtypeStruct((T, F), MXU)] * 3,
        compiler_params=_cparams(("arbitrary", "arbitrary")),
    )(h, w1, w3)


def _ffn_act_bwd(dff, w2, u, v, *, name):
    T, N = dff.shape
    F = w2.shape[0]
    tm = _pick(T, MM_ROWS)
    tk = _pick(F, MM_COLS[1:])

    def body(dy_ref, w_ref, u_ref, v_ref, du_ref, dv_ref):
        dact = lax.dot_general(dy_ref[...], w_ref[...], (((1,), (1,)), ((), ())), preferred_element_type=F32)
        _, vjp = jax.vjp(_swiglu, u_ref[...].astype(F32), v_ref[...].astype(F32))
        du, dv = vjp(dact)
        du_ref[...] = du.astype(MXU)
        dv_ref[...] = dv.astype(MXU)

    tile = pl.BlockSpec((tm, tk), lambda j, i: (i, j))
    return pl.pallas_call(
        body, name=name, grid=(F // tk, T // tm),
        in_specs=[pl.BlockSpec((tm, N), lambda j, i: (i, 0)), pl.BlockSpec((tk, N), lambda j, i: (j, 0)), tile, tile],
        out_specs=[tile, tile], out_shape=[jax.ShapeDtypeStruct((T, F), MXU)] * 2,
        compiler_params=_cparams(("arbitrary", "arbitrary")),
    )(dff, w2, u, v)


def _mm_nt2(dy1, w1, dy2, w2, *, name):
    T, N = dy1.shape
    K = w1.shape[0]
    tm = _pick(T, MM_ROWS[1:])
    tk = _pick(K, MM_COLS[1:])

    def body(d1_ref, w1_ref, d2_ref, w2_ref, o_ref):
        nt = lambda a, b: lax.dot_general(a[...], b[...], (((1,), (1,)), ((), ())), preferred_element_type=F32)
        o_ref[...] = nt(d1_ref, w1_ref) + nt(d2_ref, w2_ref)

    dspec = pl.BlockSpec((tm, N), lambda j, i: (i, 0))
    wspec = pl.BlockSpec((tk, N), lambda j, i: (j, 0))
    return pl.pallas_call(
        body, name=name, grid=(K // tk, T // tm), in_specs=[dspec, wspec, dspec, wspec],
        out_specs=pl.BlockSpec((tm, tk), lambda j, i: (i, j)), out_shape=jax.ShapeDtypeStruct((T, K), F32),
        compiler_params=_cparams(("arbitrary", "arbitrary")),
    )(dy1, w1, dy2, w2)


@functools.partial(jax.custom_vjp, nondiff_argnums=(1,))
def _lane_roll(x, s):
    return pltpu.roll(x, s, 1)


def _lane_roll_fwd(x, s):
    return pltpu.roll(x, s, 1), None


def _lane_roll_bwd(s, _, g):
    n = g.shape[1]
    return (pltpu.roll(g, (n - s) % n, 1),)


_lane_roll.defvjp(_lane_roll_fwd, _lane_roll_bwd)


def _rope(x, cos, sin_lo, sin_hi):
    n = x.shape[1]
    return x * cos + _lane_roll(x, n - 8) * sin_lo + _lane_roll(x, 8) * sin_hi


def _head_rms(x, gain):
    return x * lax.rsqrt(_segsum(x * x) * (1.0 / HD) + RMS_EPS) * gain


def _attn_blocks(qkv_c, qkv_p, tab_c, tab_p, qg, kg, sinks, first):
    nb = qkv_c.shape[0] // BLK
    G = 4

    def tabs(tab, n):
        return [jnp.tile(tab[:, j * 128:(j + 1) * 128], (1, n // 128)) for j in range(3)]

    qg = jnp.concatenate([qg] * NH, axis=1)
    kg = jnp.concatenate([kg] * 2, axis=1)
    q = _rope(_head_rms(qkv_c[:, :RW], qg), *tabs(tab_c, RW))
    k_in = jnp.concatenate([qkv_p[:, RW:RW + 128], qkv_c[:, RW:RW + 128]], axis=0)
    k = _rope(_head_rms(k_in, kg), *tabs(jnp.concatenate([tab_p, tab_c], axis=0), 128))
    v = jnp.concatenate([qkv_p[:, RW + 128:], qkv_c[:, RW + 128:]], axis=0)

    pile = lambda xs: jnp.concatenate([x_[None] for x_ in xs], axis=0)

    def bands(t):
        return pile([t[b * BLK:(b + 2) * BLK, kvh * HD:(kvh + 1) * HD] for kvh in range(2) for b in range(nb)])

    qs = pile([jnp.concatenate([q[b * BLK:(b + 1) * BLK, (G * kvh + g) * HD:(G * kvh + g + 1) * HD]
                                for g in range(G)], axis=0) for kvh in range(2) for b in range(nb)])
    s = _bmm(qs, bands(k), 2, 2, 1) * (HD ** -0.5)
    qi = lax.broadcasted_iota(jnp.int32, (G * BLK, 2 * BLK), 0) % BLK
    kj = lax.broadcasted_iota(jnp.int32, (G * BLK, 2 * BLK), 1)
    dist = qi + BLK - kj
    in_band = (dist >= 0) & (dist < BLK)
    pair = lax.broadcasted_iota(jnp.int32, (2 * nb, 1, 1), 0)
    no_prev = (pair % nb == 0) & first
    valid = in_band[None] & (jnp.logical_not(no_prev) | (kj >= BLK)[None])
    s = jnp.where(valid, s, NEG_INF)
    row_g = lax.broadcasted_iota(jnp.int32, (G * BLK, 1), 0) // BLK
    sink = []
    for kvh in range(2):
        col = jnp.zeros((G * BLK, 1), F32)
        for g in range(G):
            col = jnp.where(row_g == g, sinks[:, G * kvh + g:G * kvh + g + 1], col)
        sink += [col] * nb
    sink = pile(sink)
    m = lax.stop_gradient(jnp.maximum(jnp.max(s, axis=-1, keepdims=True), sink))
    e = jnp.exp(s - m)
    p = e * (1.0 / (jnp.sum(e, axis=-1, keepdims=True) + jnp.exp(sink - m)))
    o = _bmm(p, bands(v), 2, 1, 1)
    return jnp.concatenate([jnp.concatenate([o[kvh * nb + b, g * BLK:(g + 1) * BLK] for kvh in range(2)
                                             for g in range(G)], axis=1) for b in range(nb)], axis=0)


def _heads(x):
    return jnp.stack([x[:, h * HD:(h + 1) * HD] for h in range(NH)], axis=0)


def _unheads(x):
    return jnp.concatenate([x[h] for h in range(NH)], axis=1)


def _split(x, n):
    parts, rest = [], x
    for _ in range(n):
        p = rest.astype(MXU)
        parts.append(p)
        rest = rest - p.astype(F32)
    return parts


def _bdot_batched(a, b, ca, cb):
    return lax.dot_general(a, b, (((ca,), (cb,)), ((0,), (0,))), preferred_element_type=F32)


def _bmm_passes(a, b, ca, cb, passes):
    if MXU == F32:
        return lax.dot_general(a, b, (((ca,), (cb,)), ((0,), (0,))), precision=HI, preferred_element_type=F32)
    if passes == 1:
        return _bdot_batched(a.astype(MXU), b.astype(MXU), ca, cb)
    (a0, a1), (b0, b1) = _split(a, 2), _split(b, 2)
    return _bdot_batched(a0, b0, ca, cb) + (_bdot_batched(a0, b1, ca, cb) + _bdot_batched(a1, b0, ca, cb))


@functools.partial(jax.custom_vjp, nondiff_argnums=(2, 3, 4))
def _bmm(a, b, ca, cb, passes=1):
    return _bmm_passes(a, b, ca, cb, passes)


def _bmm_fwd(a, b, ca, cb, passes):
    return _bmm_passes(a, b, ca, cb, passes), (a, b)


def _bmm_bwd(ca, cb, passes, res, g):
    a, b = res
    if (ca, cb) == (2, 1):
        return _bmm_passes(g, b, 2, 2, passes), _bmm_passes(a, g, 1, 1, passes)
    if (ca, cb) == (2, 2):
        return _bmm_passes(g, b, 2, 1, passes), _bmm_passes(g, a, 1, 1, passes)
    return _bmm_passes(b, g, 2, 2, passes), _bmm_passes(a, g, 2, 1, passes)


_bmm.defvjp(_bmm_fwd, _bmm_bwd)


def _tri_dot(x, transpose):
    C = x.shape[1]
    ri = lax.broadcasted_iota(jnp.int32, (C, C), 0)
    ci = lax.broadcasted_iota(jnp.int32, (C, C), 1)
    tri = jnp.broadcast_to(((ri <= ci) if transpose else (ri >= ci)).astype(MXU), (x.shape[0], C, C))
    if MXU == F32:
        return lax.dot_general(tri, x, (((2,), (1,)), ((0,), (0,))), precision=HI, preferred_element_type=F32)
    p0, p1, p2 = _split(x, 3)
    return _bdot_batched(tri, p0, 2, 1) + (_bdot_batched(tri, p1, 2, 1) + _bdot_batched(tri, p2, 2, 1))


@jax.custom_vjp
def _cumsum_rows(x):
    return _tri_dot(x, False)


def _cumsum_rows_fwd(x):
    return _tri_dot(x, False), None


def _cumsum_rows_bwd(_, g):
    return (_tri_dot(g, True),)


_cumsum_rows.defvjp(_cumsum_rows_fwd, _cumsum_rows_bwd)

P_SCORE = 1
P_SOLVE = 1
P_STATE = 1
SCAN_CHUNKS = (4, 2, 1)


def _neumann(l):
    C = l.shape[1]
    eye = (lax.broadcasted_iota(jnp.int32, (C, C), 0) == lax.broadcasted_iota(jnp.int32, (C, C), 1)).astype(F32)
    x, lp = eye + l, l
    for _ in range(int(math.log2(C)) - 1):
        lp = _bmm(lp, lp, 2, 1, P_SOLVE)
        x = x + _bmm(x, lp, 2, 1, P_SOLVE)
    return x


@jax.custom_vjp
def _unit_lower_inverse(l):
    return _neumann(l)


def _unit_lower_inverse_fwd(l):
    x = _neumann(l)
    return x, x


def _unit_lower_inverse_bwd(x, g):
    return (_bmm(_bmm(x, g, 1, 1, P_SOLVE), x, 2, 2, P_SOLVE),)


_unit_lower_inverse.defvjp(_unit_lower_inverse_fwd, _unit_lower_inverse_bwd)


def _known_inverse(x):
    @jax.custom_vjp
    def f(l):
        return x

    f.defvjp(lambda l: (x, None), lambda _, g: (_bmm(_bmm(x, g, 1, 1, P_SOLVE), x, 2, 2, P_SOLVE),))
    return f


def _chunk(S0, r, lw, k, v, a, b, inverse=None):
    C = CHUNK
    n = r.shape[1] // C
    fold = lambda t: t.reshape(NH * n, C, HD)
    r, lw, k, v, a, b = (fold(t) for t in (r, lw, k, v, a, b))
    ri = lax.broadcasted_iota(jnp.int32, (C, C), 0)
    ci = lax.broadcasted_iota(jnp.int32, (C, C), 1)
    incl = (ri >= ci)
    strict = (ri > ci)
    eye = (ri == ci).astype(F32)
    cum = _cumsum_rows(lw)
    p_in = jnp.exp(cum)
    p_ex = jnp.exp(cum - lw)
    p_inv = jnp.exp(-cum)
    at, rt, bt, kt = a * p_ex, r * p_in, b * p_inv, k * p_inv
    lhs = jnp.concatenate([at, rt], axis=1)
    rhs_ = jnp.concatenate([bt, kt], axis=1)
    sc = _bmm(lhs, rhs_, 2, 2, P_SCORE)
    a_ab = jnp.where(strict, sc[:, :C, :C], 0.0)
    a_ak = jnp.where(strict, sc[:, :C, C:], 0.0)
    incl2 = (lax.broadcasted_iota(jnp.int32, (C, 2 * C), 0) >= lax.broadcasted_iota(jnp.int32, (C, 2 * C), 1) % C)
    a_r = jnp.where(incl2, sc[:, C:, :], 0.0)
    av = _bmm(a_ak, v, 2, 1, P_SCORE)
    x = x_all = (_unit_lower_inverse if inverse is None else _known_inverse(inverse))(a_ab)
    p_last = jnp.exp(cum[:, C - 1:C, :])
    per_chunk = lambda t: t.reshape((NH, n) + t.shape[1:])
    lhs, rhs_, a_r, av, x, v, p_last = (per_chunk(t) for t in (lhs, rhs_, a_r, av, x, v, p_last))
    S, ys = S0, []
    for c in range(n):
        s0 = _bmm(lhs[:, c], S, 2, 2, P_STATE)
        u = _bmm(x[:, c], s0[:, :C] + av[:, c], 2, 1, P_SOLVE)
        uv = jnp.concatenate([u, v[:, c]], axis=1)
        ys.append(s0[:, C:] + _bmm(a_r[:, c], uv, 2, 1, P_SCORE))
        S = (S + _bmm(uv, rhs_[:, c], 1, 1, P_STATE)) * p_last[:, c]
    return jnp.concatenate(ys, axis=1), S, x_all


def _hosting(body, ex, n_in, n_out, n_scratch, n_steps):
    if ex is None:
        return body

    def wrapped(*refs):
        a = n_in
        b = a + ex.n_in
        c = b + n_out
        d = c + ex.n_out
        e = d + n_scratch
        ex_refs = (refs[a:b], refs[c:d], refs[e:])
        grid = n_steps if isinstance(n_steps, tuple) else (n_steps,)
        first = last = True
        for ax_, size in enumerate(grid):
            first = first & (pl.program_id(ax_) == 0)
            last = last & (pl.program_id(ax_) == size - 1)

        @pl.when(first)
        def _():
            ex.start(*ex_refs)

        body(*refs[:a], *refs[b:c], *refs[d:e])

        @pl.when(last)
        def _():
            ex.wait(*ex_refs)

    return wrapped


def _hosted_args(ex):
    if ex is None:
        return [], [], [], [], []
    any_spec = pl.BlockSpec(memory_space=pl.ANY)
    return list(ex.arrays), [any_spec] * ex.n_in, [any_spec] * ex.n_out, list(ex.out_shape), list(ex.scratch)


def _scan_fwd(rw, *, name, hosted=None):
    T = rw.shape[0]
    nc = _pick(T // CHUNK, SCAN_CHUNKS)
    rows = CHUNK * nc
    n = T // rows
    h_in, h_in_specs, h_out_specs, h_out_shape, h_scratch = _hosted_args(hosted)

    def body(r_ref, lw_ref, k_ref, v_ref, a_ref, b_ref, y_ref, ck_ref, inv_ref, s_ref):
        @pl.when(pl.program_id(0) == 0)
        def _():
            s_ref[...] = jnp.zeros_like(s_ref)

        S0 = s_ref[...]
        ck_ref[0] = S0
        y, S1, inv = _chunk(S0, *[_heads(ref[...]) for ref in (r_ref, lw_ref, k_ref, v_ref, a_ref, b_ref)])
        y_ref[...] = _unheads(y)
        inv_ref[0] = inv
        s_ref[...] = S1

    col = lambda j: pl.BlockSpec((rows, RW), lambda i: (i, j))
    return pl.pallas_call(
        _hosting(body, hosted, 6, 3, 1, n), name=name, grid=(n,),
        in_specs=[col(j) for j in range(6)] + h_in_specs,
        out_specs=[pl.BlockSpec((rows, RW), lambda i: (i, 0)),
                   pl.BlockSpec((1, NH, HD, HD), lambda i: (i, 0, 0, 0)),
                   pl.BlockSpec((1, NH * nc, CHUNK, CHUNK), lambda i: (i, 0, 0, 0))] + h_out_specs,
        out_shape=[jax.ShapeDtypeStruct((T, RW), F32), jax.ShapeDtypeStruct((n, NH, HD, HD), F32),
                   jax.ShapeDtypeStruct((n, NH * nc, CHUNK, CHUNK), F32)] + h_out_shape,
        scratch_shapes=[pltpu.VMEM((NH, HD, HD), F32)] + h_scratch,
        compiler_params=_cparams(("arbitrary",)),
    )(rw, rw, rw, rw, rw, rw, *h_in)


def _scan_bwd(rw, ck, inv, dy, *, name, hosted=None):
    T = rw.shape[0]
    nc = _pick(T // CHUNK, SCAN_CHUNKS)
    rows = CHUNK * nc
    n = T // rows

    def body(r_ref, lw_ref, k_ref, v_ref, a_ref, b_ref, ck_ref, inv_ref, dy_ref, o_ref, ds_ref):
        @pl.when(pl.program_id(0) == 0)
        def _():
            ds_ref[...] = jnp.zeros_like(ds_ref)

        prim = [_heads(ref[...]) for ref in (r_ref, lw_ref, k_ref, v_ref, a_ref, b_ref)]
        known = inv_ref[0]
        _, vjp = jax.vjp(lambda S0, *p: _chunk(S0, *p, inverse=known)[:2], ck_ref[0], *prim)
        grads = vjp((_heads(dy_ref[...]), ds_ref[...]))
        ds_ref[...] = grads[0]
        o_ref[...] = jnp.concatenate([_unheads(g) for g in grads[1:]], axis=1).astype(o_ref.dtype)

    h_in, h_in_specs, h_out_specs, h_out_shape, h_scratch = _hosted_args(hosted)
    col = lambda j: pl.BlockSpec((rows, RW), lambda i: (n - 1 - i, j))
    return pl.pallas_call(
        _hosting(body, hosted, 9, 1, 1, n), name=name, grid=(n,),
        in_specs=[col(j) for j in range(6)] + [pl.BlockSpec((1, NH, HD, HD), lambda i: (n - 1 - i, 0, 0, 0)),
                                               pl.BlockSpec((1, NH * nc, CHUNK, CHUNK), lambda i: (n - 1 - i, 0, 0, 0)),
                                               pl.BlockSpec((rows, RW), lambda i: (n - 1 - i, 0))] + h_in_specs,
        out_specs=[pl.BlockSpec((rows, 6 * RW), lambda i: (n - 1 - i, 0))] + h_out_specs,
        out_shape=[jax.ShapeDtypeStruct((T, 6 * RW), MXU)] + h_out_shape,
        scratch_shapes=[pltpu.VMEM((NH, HD, HD), F32)] + h_scratch,
        compiler_params=_cparams(("arbitrary",)),
    )(rw, rw, rw, rw, rw, rw, ck, inv, dy, *h_in)


ATTN_BLOCKS = (4, 2, 1)

def _attn_fwd(qkv, tab, qg, kg, sinks, *, name, hosted=None):
    T = qkv.shape[0]
    nb = _pick(T // BLK, ATTN_BLOCKS)
    n = T // (BLK * nb)
    h_in, h_in_specs, h_out_specs, h_out_shape, h_scratch = _hosted_args(hosted)

    def body(c_ref, p_ref, tc_ref, tp_ref, qg_ref, kg_ref, s_ref, o_ref):
        o_ref[...] = _attn_blocks(c_ref[...], p_ref[...], tc_ref[...], tp_ref[...], qg_ref[...], kg_ref[...],
                                  s_ref[...], pl.program_id(0) == 0).astype(o_ref.dtype)

    cur = lambda w: pl.BlockSpec((nb * BLK, w), lambda i: (i, 0))
    prev = lambda w: pl.BlockSpec((BLK, w), lambda i: (jnp.maximum(i * nb - 1, 0), 0))
    return pl.pallas_call(
        _hosting(body, hosted, 7, 1, 0, n), name=name, grid=(n,),
        in_specs=[cur(QKV_W), prev(QKV_W), cur(3 * 128), prev(3 * 128), _full_spec(qg), _full_spec(kg),
                  _full_spec(sinks)] + h_in_specs,
        out_specs=[cur(RW)] + h_out_specs, out_shape=[jax.ShapeDtypeStruct((T, RW), MXU)] + h_out_shape,
        scratch_shapes=h_scratch,
        compiler_params=_cparams(("arbitrary",)),
    )(qkv, qkv, tab, tab, qg, kg, sinks, *h_in)


def _attn_bwd(qkv, tab, qg, kg, sinks, dy, *, name, hosted=None):
    T = qkv.shape[0]
    nb = _pick(T // BLK, ATTN_BLOCKS)
    n = T // (BLK * nb)
    h_in, h_in_specs, h_out_specs, h_out_shape, h_scratch = _hosted_args(hosted)

    def body(c_ref, p_ref, tc_ref, tp_ref, qg_ref, kg_ref, s_ref, dy_ref, dqkv_ref, dqg_ref, dkg_ref, ds_ref, carry_ref):
        i = pl.program_id(0)

        @pl.when(i == 0)
        def _():
            carry_ref[...] = jnp.zeros_like(carry_ref)
            dqg_ref[...] = jnp.zeros_like(dqg_ref)
            dkg_ref[...] = jnp.zeros_like(dkg_ref)
            ds_ref[...] = jnp.zeros_like(ds_ref)

        tc, tp = tc_ref[...], tp_ref[...]
        f = lambda c, p_, qg_, kg_, sk: _attn_blocks(c, p_, tc, tp, qg_, kg_, sk, i == n - 1)
        _, vjp = jax.vjp(f, c_ref[...], p_ref[...], qg_ref[...], kg_ref[...], s_ref[...])
        dc, dp, dqg, dkg, dsk = vjp(dy_ref[...].astype(F32))
        last = slice((nb - 1) * BLK, nb * BLK)
        dqkv_ref[...] = dc.astype(dqkv_ref.dtype)
        dqkv_ref[last, :] = (dc[last] + carry_ref[...]).astype(dqkv_ref.dtype)
        carry_ref[...] = dp
        dqg_ref[...] += dqg
        dkg_ref[...] += dkg
        ds_ref[...] += dsk

    cur = lambda w: pl.BlockSpec((nb * BLK, w), lambda i: (n - 1 - i, 0))
    prev = lambda w: pl.BlockSpec((BLK, w), lambda i: (jnp.maximum((n - 1 - i) * nb - 1, 0), 0))
    return pl.pallas_call(
        _hosting(body, hosted, 8, 4, 1, n), name=name, grid=(n,),
        in_specs=[cur(QKV_W), prev(QKV_W), cur(3 * 128), prev(3 * 128), _full_spec(qg), _full_spec(kg), _full_spec(sinks),
                  cur(RW)] + h_in_specs,
        out_specs=[cur(QKV_W), _full_spec(qg), _full_spec(kg), _full_spec(sinks)] + h_out_specs,
        out_shape=[jax.ShapeDtypeStruct((T, QKV_W), MXU), jax.ShapeDtypeStruct(qg.shape, F32),
                   jax.ShapeDtypeStruct(kg.shape, F32), jax.ShapeDtypeStruct(sinks.shape, F32)] + h_out_shape,
        scratch_shapes=[pltpu.VMEM((BLK, QKV_W), F32)] + h_scratch,
        compiler_params=_cparams(("arbitrary",)),
    )(qkv, qkv, tab, tab, qg, kg, sinks, dy, *h_in)


def _shift_down(cur, prev8, i):
    rolled = pltpu.roll(cur, 1, 0)
    first_row = jnp.where(i > 0, prev8[7:8, :], 0.0)
    row = lax.broadcasted_iota(jnp.int32, cur.shape, 0)
    return jnp.where(row == 0, first_row, rolled)


def _shift_up(cur, next8, i, n):
    tm = cur.shape[0]
    rolled = pltpu.roll(cur, tm - 1, 0)
    last_row = jnp.where(i < n - 1, next8[0:1, :], 0.0)
    row = lax.broadcasted_iota(jnp.int32, cur.shape, 0)
    return jnp.where(row == tm - 1, last_row, rolled)


def _ada_parts(ada):
    return [ada[:, j * D:(j + 1) * D] for j in range(6)]


def _rope_table(positions):
    half = HD // 8
    inv_freq = 500000.0 ** (-jnp.arange(half, dtype=F32) / half)
    lane = jnp.arange(128) % HD
    rotary = lane < 2 * half
    freq = jnp.where(rotary, inv_freq[lane % half], 0.0)
    ang = positions.astype(F32)[:, None] * freq[None, :]
    cos, sin = jnp.cos(ang), jnp.sin(ang)
    return jnp.concatenate([jnp.where(rotary, cos, 1.0), jnp.where(lane < half, -sin, 0.0),
                            jnp.where(rotary & (lane >= half), sin, 0.0)], axis=1)


GATHER_BEHIND = {"f_proj_shift": [('ffn_w3', 512, 768)], "f_proj_gates": [('ffn_w3', 768, 1024)],
                 "f_prep": [('ffn_w2', 352, 704)],
                 "f_scan": [('ffn_w1', 0, 1024), ('w_branch_a', 0, 512), ('w_branch_b', 0, 512)],
                 "f_post": [('ffn_w3', 0, 512)], "f_attn": [('ffn_w2', 0, 352), ('w_out', 0, 256)]}
LATE = ['w_out', 'w_branch_a', 'w_branch_b', 'ffn_w1', 'ffn_w3', 'ffn_w2']
BACK_ATTN = ['w_out', 'w_branch_a', 'w_branch_b', 'ffn_w2']
BACK_SCAN = ['ffn_w1', 'ffn_w3']
BACK_LAST = ['w_in', 'decay_up', 'iclr_up', 'gate_up']


def _full_weight(g, ax):
    return g.reshape(-1, g.shape[2]) if ax == 0 else jnp.concatenate([g[j] for j in range(4)], axis=1)


def _local_step(x, target, ada, tab, w, s, shards=None):
    T = x.shape[0]
    tm = _pick(T, (512, 256, 128))
    tm_wide = _pick(T, (256, 128))
    tm_vjp = _pick(T, (256, 128))
    row = lambda n, dt=F32: (n, dt, 'row')
    acc = lambda n, r=1: (n, F32, r)

    def f_norm1(x_, g, ada_):
        return _norm_mod(x_, g, ada_[:, D:2 * D], ada_[:, 0:D])

    landed = {}

    def behind(kernel_name):
        if not shards:
            return None
        return _GatherChips([shards[n] if hi - lo == shards[n].shape[0] else shards[n][lo:hi]
                             for n, lo, hi in GATHER_BEHIND[kernel_name]])

    def took(kernel_name, got):
        landed.update(zip(GATHER_BEHIND[kernel_name], got))

    def mm_behind(a_, w_, kernel_name, **kw):
        ex = behind(kernel_name)
        res = _mm_nn(a_, w_, name=kernel_name, hosted=ex, **kw)
        if ex:
            took(kernel_name, res[1:])
            return res[0]
        return res

    h1, proj, *got = _mm_then([(None, w['w_in'][:, :SHIFT_W], 'nn')], lambda i, n, y, h, *_: (h, y), [x],
                              [s['norm1_gain'], ada], [row(D, MXU), row(SHIFT_W)], tm=tm, name="f_proj_shift",
                              lhs_fn=f_norm1, hosted=behind("f_proj_shift"))
    took("f_proj_shift", got)
    proj_qkv = _mm_nn(h1, w['w_in'][:, SHIFT_W:SHIFT_W + QKV_W], name="f_proj_qkv")
    proj_g = mm_behind(h1, w['w_in'][:, SHIFT_W + QKV_W:], "f_proj_gates", out_dtype=MXU)
    prep_consts = [s['decay_w0'], s['lora_up'], s['iclr_a0'], s['gate_up'], s['k_k'], s['k_a']]

    def f_prep(i, n, cur, prev8, mu, *params):
        mixed = cur + (_shift_down(cur, prev8, i) - cur) * mu
        return (_prep(mixed, *params),)
    rw, *got = _rowwise(f_prep, [(proj, SHIFT_W)], [s['tshift_mu']] + prep_consts, [row(7 * RW)], tm=tm_wide,
                        name="f_prep", halo=[(proj, SHIFT_W, 'prev')], hosted=behind("f_prep"))
    took("f_prep", got)
    y, ck, inv, *got = _scan_fwd(rw, name="f_scan", hosted=behind("f_scan"))
    took("f_scan", got)
    post_consts = [s['lnx_gain'], s['lnx_bias'], s['r_k']]

    rkvg = [(rw, RW, j) for j in (0, 2, 3, 6)]

    def f_post(i, n, *args):
        return (_post(*args),)
    ya, *got = _rowwise(f_post, [y] + rkvg, post_consts, [row(RW, MXU)], tm=tm_wide, name="f_post",
                        hosted=behind("f_post"))
    took("f_post", got)
    yb, *got = _attn_fwd(proj_qkv, tab, s['q_norm_gain'], s['k_norm_gain'], s['attn_sinks'], name="f_attn",
                         hosted=behind("f_attn"))
    took("f_attn", got)
    w = dict(w)
    if shards:
        ax = dict(SHARDED)
        for n in LATE:
            rows = [landed[key] for key in sorted(k_ for k_ in landed if k_[0] == n)]
            w[n] = _full_weight(rows[0] if len(rows) == 1 else jnp.concatenate(rows, axis=1), ax[n])
    def f_merge(pg, ya_, yb_, x_, wa, wb, bias, g, ada_):
        ma_ = jnp.dot(ya_, wa, preferred_element_type=F32).astype(MXU)
        mb_ = jnp.dot(yb_, wb, preferred_element_type=F32).astype(MXU)
        return _merge(pg.astype(F32), ma_.astype(F32), mb_.astype(F32), bias), ma_, mb_

    def f_res1(i, n, mo_, merged_, ma_, mb_, pg, ya_, yb_, x_, wa, wb, bias, g, ada_):
        x1_ = x_ + ada_[:, 2 * D:3 * D] * mo_
        return merged_, ma_, mb_, mo_, x1_, _norm_mod(x1_, g, ada_[:, 4 * D:5 * D], ada_[:, 3 * D:4 * D])
    merged, ma, mb, mo, x1, h2 = _mm_then(
        [(None, w['w_out'], 'nn')], f_res1, [proj_g, ya, yb, x],
        [w['w_branch_a'], w['w_branch_b'], s['branch_gate_b'], s['norm2_gain'], ada],
        [row(D, MXU), row(D, MXU), row(D, MXU), row(D), row(D), row(D, MXU)], tm=tm, name="f_out", lhs_fn=f_merge)
    u, v, act = _ffn_in(h2, w['ffn_w1'], w['ffn_w3'], name="f_ffn_in")

    def f_loss(i, n, ff_, x1_, tgt, ada_):
        g2 = ada_[:, 5 * D:6 * D]
        err = x1_ + g2 * ff_ - tgt
        dx2 = err * (1.0 / D)
        loss = 0.5 * jnp.sum(jnp.sum(err * err, axis=1, keepdims=True) * (1.0 / D), axis=0, keepdims=True)
        return dx2, (dx2 * g2), jnp.broadcast_to(loss, (1, 128)), jnp.sum(dx2 * ff_, axis=0, keepdims=True)
    dx2, dff, loss, dgate2 = _mm_then([(act, w['ffn_w2'], 'nn')], f_loss, [x1, target], [ada],
                                      [row(D), row(D, MXU), acc(128), acc(D)], tm=tm, name="f_ffn_out")

    du, dv = _ffn_act_bwd(dff, w['ffn_w2'], u, v, name="b_ffn_out_dx")
    g_w2 = _mm_tn(act, dff, name="b_ffn_out_dw", out_dtype=MXU)
    g_w1 = _mm_tn(h2, du, name="b_ffn_w1_dw", out_dtype=MXU)
    g_w3 = _mm_tn(h2, dv, name="b_ffn_w3_dw", out_dtype=MXU)

    def b_res1(i, n, dh2_, x1_, dx2_, mo_, g, ada_):
        _, vjp = jax.vjp(_norm_mod, x1_, g, ada_[:, 4 * D:5 * D], ada_[:, 3 * D:4 * D])
        dxn, dg, dsc, dsh = vjp(dh2_)
        dx1_ = dxn + dx2_
        g1 = ada_[:, 2 * D:3 * D]
        return dx1_, dx1_ * g1, dg, dsc, dsh, jnp.sum(dx1_ * mo_, axis=0, keepdims=True)
    dx1, dmo, d_gain2, d_scale2, d_shift2, dgate1 = _mm_then(
        [(du, w['ffn_w1'], 'nt'), (dv, w['ffn_w3'], 'nt')], b_res1, [x1, dx2, mo], [s['norm2_gain'], ada],
        [row(D), row(D, MXU), acc(D), acc(D), acc(D), acc(D)], tm=tm_wide, name="b_ffn_in_dx")
    g_wout = _mm_tn(merged, dmo, name="b_out_dw", out_dtype=MXU)

    def b_merge(i, n, dm, pg, ma_, mb_, wa, wb, bias):
        _, vjp = jax.vjp(_merge, pg.astype(F32), ma_.astype(F32), mb_.astype(F32), bias)
        dpg, dma_, dmb_, dbias = vjp(dm)
        dma_, dmb_ = dma_.astype(MXU), dmb_.astype(MXU)
        nt = lambda a_, b_: lax.dot_general(a_, b_, (((1,), (1,)), ((), ())), preferred_element_type=F32)
        return dpg, dma_, dmb_, nt(dma_, wa), nt(dmb_, wb), dbias
    dpg, dma, dmb, dya, dyb, d_bias = _mm_then(
        [(dmo, w['w_out'], 'nt')], b_merge, [proj_g, ma, mb], [w['w_branch_a'], w['w_branch_b'], s['branch_gate_b']],
        [row(GATE_W, MXU), row(D, MXU), row(D, MXU), row(RW), row(RW), acc(GATE_W)], tm=tm_wide, name="b_out_dx")
    g_wa = _mm_tn(ya, dma, name="b_branch_a_dw", out_dtype=MXU, col_shards=4)
    g_wb = _mm_tn(yb, dmb, name="b_branch_b_dw", out_dtype=MXU, col_shards=4)
    fs = DFF // 4
    gw = dict(w_branch_a=g_wa, w_branch_b=g_wb, w_out=g_wout.reshape(4, D // 4, D),
              ffn_w1=jnp.stack([g_w1[:, j * fs:(j + 1) * fs] for j in range(4)]),
              ffn_w3=jnp.stack([g_w3[:, j * fs:(j + 1) * fs] for j in range(4)]),
              ffn_w2=g_w2.reshape(4, fs, D))
    recv = {}
    dqkv, d_qg, d_kg, d_sinks, *got = _attn_bwd(
        proj_qkv, tab, s['q_norm_gain'], s['k_norm_gain'], s['attn_sinks'], dyb, name="b_attn",
        hosted=shards and _ScatterChips([gw[n] for n in BACK_ATTN]))
    recv.update(zip(BACK_ATTN, got))

    def b_post(i, n, y_, r_, k_, v_, g_, dya_, *params):
        _, vjp = jax.vjp(_post, y_, r_, k_, v_, g_, *params)
        dy_, dr_, dk_, dv_, dg_, *dparams = vjp(dya_)
        return (dy_, jnp.concatenate([dr_, dk_, dv_, dg_], axis=1), *dparams)
    dy, drkvg, d_lnx_gain, d_lnx_bias, d_r_k = _rowwise(
        b_post, [y] + rkvg + [dya], post_consts, [row(RW), row(4 * RW, MXU), acc(RW), acc(RW), acc(RW)], tm=tm_wide,
        name="b_post")
    dscan, *got = _scan_bwd(rw, ck, inv, dy, name="b_scan",
                            hosted=shards and _ScatterChips([gw[n] for n in BACK_SCAN]))
    recv.update(zip(BACK_SCAN, got))

    def b_prep(i, n, cur, drw_, dscan_, prev8, mu, *params):
        shifted = _shift_down(cur, prev8, i)
        mixed = cur + (shifted - cur) * mu
        _, vjp = jax.vjp(_prep, mixed, *params)
        blk = lambda t, j: t[:, j * RW:(j + 1) * RW].astype(F32)
        ct = jnp.concatenate([blk(dscan_, 0) + blk(drw_, 0), blk(dscan_, 1), blk(dscan_, 2) + blk(drw_, 1),
                              blk(dscan_, 3) + blk(drw_, 2), blk(dscan_, 4), blk(dscan_, 5), blk(drw_, 3)], axis=1)
        grads = vjp(ct)
        dmixed = grads[0]
        return (dmixed, jnp.sum(dmixed * (shifted - cur), axis=0, keepdims=True)) + tuple(grads[1:])
    dmixed, d_mu, d_w0, d_lora, d_a0, d_gate_up, d_kk, d_ka = _rowwise(
        b_prep, [(proj, SHIFT_W), drkvg, dscan], [s['tshift_mu']] + prep_consts,
        [row(SHIFT_W), acc(SHIFT_W), acc(RW), acc(2 * RW, 128), acc(RW), acc(RW, 128), acc(RW), acc(RW)],
        tm=tm_vjp, name="b_prep", halo=[(proj, SHIFT_W, 'prev')])

    def b_gather(i, n, dm, dqkv_, dpg_, next8, mu):
        dcur = dm * (1.0 - mu) + _shift_up(dm, next8, i, n) * mu
        return (jnp.concatenate([dcur.astype(MXU), dqkv_, dpg_], axis=1),)
    (dproj,) = _rowwise(b_gather, [dmixed, dqkv, dpg], [s['tshift_mu']], [row(IN_W, MXU)], tm=tm_wide, name="b_gather",
                        halo=[(dmixed, SHIFT_W, 'next')])
    g_win = _mm_tn(h1, dproj, name="b_proj_dw", out_dtype=MXU, col_shards=4)

    def col_blocks(g):
        k, n = g.shape
        return g.reshape(k, 4, n // 4).transpose(1, 0, 2).astype(MXU)
    gw.update(w_in=g_win, decay_up=col_blocks(d_lora[:64, :RW]), iclr_up=col_blocks(d_lora[64:, RW:]),
              gate_up=col_blocks(d_gate_up))
    top, bottom = None, None
    if shards:
        top = _ScatterChips([gw['w_in'][:, :D // 2]] + [gw[n] for n in BACK_LAST[1:]])
        bottom = _ScatterChips([gw['w_in'][:, D // 2:3 * D // 4]])
        dh1, *got_top = _mm_nt(dproj, w['w_in'], name="b_proj_dx", hosted=top)
    else:
        dh1 = _mm_nt(dproj, w['w_in'], name="b_proj_dx")

    def b_norm1(i, n, x_, dh1_, dx1_, g, ada_):
        _, vjp = jax.vjp(_norm_mod, x_, g, ada_[:, D:2 * D], ada_[:, 0:D])
        dxn, dg, dsc, dsh = vjp(dh1_)
        return dxn + dx1_, dg, dsc, dsh
    dx, d_gain1, d_scale1, d_shift1, *got_bottom = _rowwise(
        b_norm1, [x, dh1, dx1], [s['norm1_gain'], ada], [row(D), acc(D), acc(D), acc(D)], tm=tm, name="b_norm1",
        hosted=bottom)
    if shards:
        recv.update(zip(BACK_LAST[1:], got_top[1:]))
        recv['w_in'] = [got_top[0], got_bottom[0]]

    d_ada = jnp.concatenate([d_shift1, d_scale1, dgate1, d_shift2, d_scale2, dgate2], axis=1)
    gs = dict(norm1_gain=d_gain1, norm2_gain=d_gain2, tshift_mu=d_mu, decay_w0=d_w0, iclr_a0=d_a0, k_k=d_kk, k_a=d_ka,
              r_k=d_r_k, lnx_gain=d_lnx_gain, lnx_bias=d_lnx_bias, q_norm_gain=d_qg, k_norm_gain=d_kg,
              attn_sinks=d_sinks, branch_gate_b=d_bias)
    return loss, dx, d_ada, gw, gs, recv


ANY = pl.BlockSpec(memory_space=pl.ANY)


def _place():
    x, y, c = lax.axis_index("x"), lax.axis_index("y"), lax.axis_index("c")
    return x, y, c, [(1 - x, y), (x, 1 - y), (1 - x, 1 - y)]


def _all_gather8(x_shard, *, name):
    m_per, n = x_shard.shape

    def body(x_ref, out_ref, send_sems, recv_sems, local_sem):
        x, y, c, chips = _place()
        me, sibling = (x, y, c), (x, y, 1 - c)

        def rows(px, py, pc):
            return out_ref.at[pl.ds((4 * px + 2 * py + pc) * m_per, m_per), :]

        def copy(k, block, to, src=None):
            return pltpu.make_async_remote_copy(
                src_ref=rows(*block) if src is None else src, dst_ref=rows(*block),
                send_sem=send_sems.at[k], recv_sem=recv_sems.at[k], device_id=to, device_id_type=MESH)

        mine = pltpu.make_async_copy(x_ref, rows(*me), local_sem)
        mine.start()
        first = [copy(0, me, sibling, src=x_ref)]
        first += [copy(1 + j, me, (*chip, c), src=x_ref) for j, chip in enumerate(chips)]
        for cp in first:
            cp.start()
        passed = [copy(4 + j, (*chip, c), sibling) for j, chip in enumerate(chips)]
        for j, chip in enumerate(chips):
            copy(1 + j, (*chip, c), me).wait_recv()
            passed[j].start()
        copy(0, sibling, me).wait_recv()
        for j, chip in enumerate(chips):
            copy(4 + j, (*chip, 1 - c), me).wait_recv()
        for cp in first + passed:
            cp.wait_send()
        mine.wait()

    return pl.pallas_call(
        body, name=name, out_shape=jax.ShapeDtypeStruct((8 * m_per, n), x_shard.dtype),
        in_specs=[pl.BlockSpec(memory_space=pltpu.VMEM)], out_specs=pl.BlockSpec(memory_space=pltpu.VMEM),
        scratch_shapes=[pltpu.SemaphoreType.DMA((7,)), pltpu.SemaphoreType.DMA((7,)), pltpu.SemaphoreType.DMA],
    )(x_shard)


class _GatherChips:
    def __init__(self, shards):
        n = len(shards)
        self.arrays, self.n_in, self.n_out = list(shards), n, n
        self.out_shape = [jax.ShapeDtypeStruct((4,) + s.shape, s.dtype) for s in shards]
        self.scratch = [pltpu.SemaphoreType.DMA((3 * n,)), pltpu.SemaphoreType.DMA((3 * n,)),
                        pltpu.SemaphoreType.DMA((n,))]

    def _copies(self, x_refs, out_refs, sems, receiving):
        send_sems, recv_sems, local_sems = sems
        x, y, c, chips = _place()
        s_me = 2 * x + y
        n = self.n_in

        def copy(a, k, s):
            return pltpu.make_async_remote_copy(
                src_ref=x_refs[a], dst_ref=out_refs[a].at[s], send_sem=send_sems.at[3 * a + k],
                recv_sem=recv_sems.at[3 * a + k], device_id=(*chips[k], c), device_id_type=MESH)

        mine = [pltpu.make_async_copy(x_refs[a], out_refs[a].at[s_me], local_sems.at[a]) for a in range(n)]
        sends = [copy(a, k, s_me) for a in range(n) for k in range(3)]
        if not receiving:
            return mine, sends
        return mine, sends, [copy(a, k, 2 * px + py) for a in range(n) for k, (px, py) in enumerate(chips)]

    def start(self, x_refs, out_refs, sems):
        mine, sends = self._copies(x_refs, out_refs, sems, False)
        for cp in mine + sends:
            cp.start()

    def wait(self, x_refs, out_refs, sems):
        mine, sends, recvs = self._copies(x_refs, out_refs, sems, True)
        for cp in recvs:
            cp.wait_recv()
        for cp in sends:
            cp.wait_send()
        for cp in mine:
            cp.wait()


class _GatherChipsHalved(_GatherChips):
    def __init__(self, shards):
        super().__init__(shards)
        n = self.n_in
        self.scratch = [pltpu.SemaphoreType.DMA((6 * n,)), pltpu.SemaphoreType.DMA((6 * n,)),
                        pltpu.SemaphoreType.DMA((n,))]

    def _copies(self, x_refs, out_refs, sems, receiving):
        send_sems, recv_sems, local_sems = sems
        x, y, c, chips = _place()
        s_me = 2 * x + y
        n = self.n_in

        def half(a, who):
            rows = x_refs[a].shape[0] // 2
            return pl.ds(who * rows, rows)

        def over_chips(a, k, s):
            return pltpu.make_async_remote_copy(
                src_ref=x_refs[a].at[half(a, c)], dst_ref=out_refs[a].at[s, half(a, c)],
                send_sem=send_sems.at[3 * a + k], recv_sem=recv_sems.at[3 * a + k],
                device_id=(*chips[k], c), device_id_type=MESH)

        def to_sibling(a, k, s, who):
            return pltpu.make_async_remote_copy(
                src_ref=out_refs[a].at[s, half(a, who)], dst_ref=out_refs[a].at[s, half(a, who)],
                send_sem=send_sems.at[3 * n + 3 * a + k], recv_sem=recv_sems.at[3 * n + 3 * a + k],
                device_id=(x, y, 1 - c), device_id_type=MESH)

        mine = [pltpu.make_async_copy(x_refs[a], out_refs[a].at[s_me], local_sems.at[a]) for a in range(n)]
        sends = [over_chips(a, k, s_me) for a in range(n) for k in range(3)]
        if not receiving:
            return mine, sends
        pairs = [(a, k, 2 * px + py) for a in range(n) for k, (px, py) in enumerate(chips)]
        landed = [over_chips(a, k, s) for a, k, s in pairs]
        passed_on = [to_sibling(a, k, s, c) for a, k, s in pairs]
        from_sibling = [to_sibling(a, k, s, 1 - c) for a, k, s in pairs]
        return mine, sends, landed, passed_on, from_sibling

    def wait(self, x_refs, out_refs, sems):
        mine, sends, landed, passed_on, from_sibling = self._copies(x_refs, out_refs, sems, True)
        for got, fwd in zip(landed, passed_on, strict=True):
            got.wait_recv()
            fwd.start()
        for cp in from_sibling:
            cp.wait_recv()
        for cp in sends + passed_on:
            cp.wait_send()
        for cp in mine:
            cp.wait()


class _ScatterChips:
    def __init__(self, parts):
        n = len(parts)
        self.arrays, self.n_in, self.n_out = list(parts), n, n
        self.out_shape = [jax.ShapeDtypeStruct((3,) + p.shape[1:], p.dtype) for p in parts]
        self.scratch = [pltpu.SemaphoreType.DMA((3 * n,)), pltpu.SemaphoreType.DMA((3 * n,))]

    def _copies(self, g_refs, out_refs, sems):
        send_sems, recv_sems = sems
        x, y, c, chips = _place()
        return [pltpu.make_async_remote_copy(
            src_ref=g_refs[a].at[2 * px + py], dst_ref=out_refs[a].at[k], send_sem=send_sems.at[3 * a + k],
            recv_sem=recv_sems.at[3 * a + k], device_id=(px, py, c), device_id_type=MESH)
            for a in range(self.n_in) for k, (px, py) in enumerate(chips)]

    def start(self, g_refs, out_refs, sems):
        for cp in self._copies(g_refs, out_refs, sems):
            cp.start()

    def wait(self, g_refs, out_refs, sems):
        sends = self._copies(g_refs, out_refs, sems)
        for cp in sends:
            cp.wait_recv()
        for cp in sends:
            cp.wait_send()


def _exchange_call(ex, *, name):
    def body(*refs):
        parts = (refs[:ex.n_in], refs[ex.n_in:ex.n_in + ex.n_out], refs[ex.n_in + ex.n_out:])
        ex.start(*parts)
        ex.wait(*parts)

    return pl.pallas_call(body, name=name, out_shape=ex.out_shape, in_specs=[ANY] * ex.n_in,
                          out_specs=[ANY] * ex.n_out, scratch_shapes=ex.scratch)(*ex.arrays)


class _SwapSibling:
    def __init__(self, vs):
        n = len(vs)
        self.arrays, self.n_in, self.n_out = list(vs), n, n
        self.out_shape = [jax.ShapeDtypeStruct(v.shape, v.dtype) for v in vs]
        self.scratch = [pltpu.SemaphoreType.DMA((n,)), pltpu.SemaphoreType.DMA((n,))]

    def _copies(self, v_refs, out_refs, sems):
        send_sems, recv_sems = sems
        x, y, c, _ = _place()
        return [pltpu.make_async_remote_copy(src_ref=v_refs[a], dst_ref=out_refs[a], send_sem=send_sems.at[a],
                                             recv_sem=recv_sems.at[a], device_id=(x, y, 1 - c), device_id_type=MESH)
                for a in range(self.n_in)]

    def start(self, v_refs, out_refs, sems):
        for cp in self._copies(v_refs, out_refs, sems):
            cp.start()

    def wait(self, v_refs, out_refs, sems):
        for cp in self._copies(v_refs, out_refs, sems):
            cp.wait()


class _Both:
    def __init__(self, first, second):
        self.parts = (first, second)
        self.arrays = first.arrays + second.arrays
        self.n_in, self.n_out = first.n_in + second.n_in, first.n_out + second.n_out
        self.out_shape = first.out_shape + second.out_shape
        self.scratch = first.scratch + second.scratch

    def _split(self, in_refs, out_refs, sems):
        a, b = self.parts
        return ((a, in_refs[:a.n_in], out_refs[:a.n_out], sems[:len(a.scratch)]),
                (b, in_refs[a.n_in:], out_refs[a.n_out:], sems[len(a.scratch):]))

    def start(self, in_refs, out_refs, sems):
        for ex, *refs in self._split(in_refs, out_refs, sems):
            ex.start(*refs)

    def wait(self, in_refs, out_refs, sems):
        for ex, *refs in self._split(in_refs, out_refs, sems):
            ex.wait(*refs)


def _sum_parts(own, others, *, name):
    R, C = own.shape
    tm = _pick(R, (256, 128, 64))

    def body(own_ref, o0_ref, o1_ref, o2_ref, out_ref):
        tot = own_ref[...].astype(F32)
        for ref in (o0_ref, o1_ref, o2_ref):
            tot = tot + ref[...].astype(F32)
        out_ref[...] = tot

    part = lambda k: pl.BlockSpec((None, tm, C), lambda i: (k, i, 0))
    return pl.pallas_call(
        body, name=name, grid=(R // tm,),
        in_specs=[pl.BlockSpec((tm, C), lambda i: (i, 0)), part(0), part(1), part(2)],
        out_specs=pl.BlockSpec((tm, C), lambda i: (i, 0)), out_shape=jax.ShapeDtypeStruct((R, C), F32),
        compiler_params=_cparams(("arbitrary",)),
    )(own, others, others, others)


def _adam_math(w_, m_, v_, g):
    m2 = ADAM_B1 * m_ + (1.0 - ADAM_B1) * g
    v2 = ADAM_B2 * v_ + (1.0 - ADAM_B2) * jnp.square(g)
    m_hat = m2 / (1.0 - ADAM_B1 ** ADAM_STEP)
    v_hat = v2 / (1.0 - ADAM_B2 ** ADAM_STEP)
    delta = -ADAM_LR * (m_hat / (jnp.sqrt(v_hat) + ADAM_EPS) + ADAM_WD * w_)
    return delta, m2, v2


SMALL_SLOTS = 16
SMALL_COLS = 6 * D


def _pack_small(grads, *, name):
    n = len(grads)

    def body(*refs):
        out_ref = refs[n]
        out_ref[...] = jnp.zeros_like(out_ref)
        for i, ref in enumerate(refs[:n]):
            out_ref[i:i + 1, 0:ref.shape[1]] = ref[...]

    return pl.pallas_call(body, name=name, out_shape=jax.ShapeDtypeStruct((SMALL_SLOTS, SMALL_COLS), F32))(*grads)


def _adamw_small(ws, ms, vs, gathered, *, name):
    n = len(ws)

    def total(g_ref, i, nc):
        g = g_ref[i:i + 1, 0:nc]
        for d in range(1, 8):
            g = g + g_ref[d * SMALL_SLOTS + i:d * SMALL_SLOTS + i + 1, 0:nc]
        return g

    def body(*refs):
        w_refs, m_refs, v_refs, g_ref = refs[:n], refs[n:2 * n], refs[2 * n:3 * n], refs[3 * n]
        outs = refs[3 * n + 1:]
        for i in range(n):
            g = total(g_ref, i, w_refs[i].shape[1])
            delta, m2, v2 = _adam_math(w_refs[i][...], m_refs[i][...], v_refs[i][...], g)
            for k, val in enumerate((g, delta, m2, v2)):
                outs[k * n + i][...] = val
        outs[4 * n][...] = total(g_ref, n, 128)

    shapes = [jax.ShapeDtypeStruct(w.shape, F32) for w in ws]
    res = pl.pallas_call(body, name=name, out_shape=shapes * 4 + [jax.ShapeDtypeStruct((1, 128), F32)],
                         compiler_params=pltpu.CompilerParams(vmem_limit_bytes=VMEM_LIMIT))(*ws, *ms, *vs, gathered)
    return [res[k * n:(k + 1) * n] for k in range(4)], res[4 * n]


def _adamw(w, m, v, gparts, *, tm, name, hosted=None):
    def fn(i, n, w_, m_, v_, *gs):
        g = gs[0]
        for p in gs[1:]:
            g = g + p
        return (g,) + _adam_math(w_, m_, v_, g)
    nc = w.shape[1]
    return _rowwise(fn, [w, m, v] + list(gparts), [], [(nc, F32, 'row')] * 4, tm=tm, name=name, hosted=hosted)


WEIGHTS = ['ada_w', 'ada_b', 'norm1_gain', 'norm2_gain', 'w_in', 'tshift_mu', 'decay_w0', 'decay_up', 'iclr_a0',
           'iclr_up', 'gate_up', 'k_k', 'k_a', 'r_k', 'lnx_gain', 'lnx_bias', 'q_norm_gain', 'k_norm_gain', 'attn_sinks',
           'branch_gate_b', 'w_branch_a', 'w_branch_b', 'w_out', 'ffn_w1', 'ffn_w3', 'ffn_w2']
SHARDED = [('w_in', 1), ('decay_up', 1), ('iclr_up', 1), ('gate_up', 1), ('w_branch_a', 1), ('w_branch_b', 1),
           ('w_out', 0), ('ffn_w1', 1), ('ffn_w3', 1), ('ffn_w2', 0)]
SMALL = ['ada_b', 'norm1_gain', 'norm2_gain', 'tshift_mu', 'decay_w0', 'iclr_a0', 'k_k', 'k_a', 'r_k', 'lnx_gain',
         'lnx_bias', 'q_norm_gain', 'k_norm_gain', 'attn_sinks', 'branch_gate_b']


def kernel(x, c, positions, ada_w, ada_b, norm1_gain, norm2_gain, w_in, tshift_mu, decay_w0, decay_up, iclr_a0, iclr_up, gate_up, k_k, k_a, r_k, lnx_gain, lnx_bias, q_norm_gain, k_norm_gain, attn_sinks, branch_gate_b, w_branch_a, w_branch_b, w_out, ffn_w1, ffn_w3, ffn_w2, loss_target, m_ada_w, m_ada_b, m_norm1_gain, m_norm2_gain, m_w_in, m_tshift_mu, m_decay_w0, m_decay_up, m_iclr_a0, m_iclr_up, m_gate_up, m_k_k, m_k_a, m_r_k, m_lnx_gain, m_lnx_bias, m_q_norm_gain, m_k_norm_gain, m_attn_sinks, m_branch_gate_b, m_w_branch_a, m_w_branch_b, m_w_out, m_ffn_w1, m_ffn_w3, m_ffn_w2, v_ada_w, v_ada_b, v_norm1_gain, v_norm2_gain, v_w_in, v_tshift_mu, v_decay_w0, v_decay_up, v_iclr_a0, v_iclr_up, v_gate_up, v_k_k, v_k_a, v_r_k, v_lnx_gain, v_lnx_bias, v_q_norm_gain, v_k_norm_gain, v_attn_sinks, v_branch_gate_b, v_w_branch_a, v_w_branch_b, v_w_out, v_ffn_w1, v_ffn_w3, v_ffn_w2):
    a = dict(locals())
    W = {n: a[n] for n in WEIGHTS}
    M = {n: a['m_' + n] for n in WEIGHTS}
    V = {n: a['v_' + n] for n in WEIGHTS}
    xi, yi, ci = lax.axis_index("x"), lax.axis_index("y"), lax.axis_index("c")
    me = 4 * xi + 2 * yi + ci
    shard = 2 * xi + yi
    mat = lambda t: t.reshape(t.shape[-2], t.shape[-1])
    sharded = [n for n, _ in SHARDED]

    ax = dict(SHARDED)
    late = LATE
    early = [n for n in sharded if n not in late]
    shards = {n: mat(W[n]).astype(MXU) for n in sharded}
    gathered = _exchange_call(_GatherChipsHalved([shards[n] for n in early]), name="gather_weights")
    full = {n: _full_weight(g, ax[n]) for n, g in zip(early, gathered, strict=True)}

    c_all = _all_gather8(jnp.broadcast_to(c, (8, D)), name="gather_c")[0::8]
    pad_rows = lambda t: jnp.concatenate([t, jnp.zeros((BLK - 8, t.shape[1]), t.dtype)])
    c_all = pad_rows(c_all.astype(MXU))
    ada_cols = _mm_nn(c_all, mat(ada_w).astype(MXU), name="f_ada")[:8]
    ada_all = _all_gather8(ada_cols, name="gather_ada").reshape(2, 2, 2, 8, 6 * D // 4)
    ada_mine = lax.dynamic_index_in_dim(ada_all[:, :, 0], me, axis=2, keepdims=False)
    ada = ada_mine.reshape(1, 6 * D) + mat(ada_b)

    zero = jnp.zeros((64, RW), MXU)
    lora = jnp.concatenate([jnp.concatenate([full['decay_up'], zero], axis=1),
                            jnp.concatenate([zero, full['iclr_up']], axis=1)], axis=0)
    s = {n: W[n].reshape(1, -1) for n in SMALL if n != 'ada_b'}
    s['lora_up'] = lora.astype(F32)
    s['gate_up'] = full['gate_up'].astype(F32)
    tab = _rope_table(positions.reshape(-1))
    loss, dx, d_ada, gw, gs, from_chips = _local_step(x[0], loss_target[0], ada, tab, dict(w_in=full['w_in']), s,
                                                      shards={n: shards[n] for n in late})

    gs['ada_b'] = d_ada
    gsmall = _pack_small([gs[n] for n in SMALL] + [loss], name="pack_small_grads")
    gsmall_all = _all_gather8(gsmall, name="gather_small_grads")
    row = lambda src: [src[n].reshape(1, -1) for n in SMALL]
    sm_out, loss = _adamw_small(row(W), row(M), row(V), gsmall_all, name="adamw_small")
    sm_out = [{n: o.reshape(W[n].shape) for n, o in zip(SMALL, outs_k, strict=True)} for outs_k in sm_out]
    loss = loss[0, 0]

    d_ada_all = gsmall_all[0::SMALL_SLOTS]
    d_ada_cols = lax.dynamic_slice_in_dim(d_ada_all, shard * (6 * D // 4), 6 * D // 4, axis=1)
    g_ada_w = _mm_tn(c_all, pad_rows(d_ada_cols.astype(MXU)), name="b_ada")

    rest = [n for n in sharded if n != 'w_in']
    parts = {n: _sum_parts(lax.dynamic_index_in_dim(gw[n], shard, axis=0, keepdims=False), from_chips[n],
                           name="sum_" + n) for n in rest}
    tail = _Both(_ScatterChips([gw['w_in'][:, 3 * D // 4:]]), _SwapSibling([parts[n] for n in rest]))
    res = _adamw(mat(ada_w), mat(m_ada_w), mat(v_ada_w), [g_ada_w], tm=256, name="adamw_ada", hosted=tail)
    ada_out, last_quarter, others = res[:4], res[4], dict(zip(rest, res[5:], strict=True))
    parts['w_in'] = _sum_parts(lax.dynamic_index_in_dim(gw['w_in'], shard, axis=0, keepdims=False),
                               jnp.concatenate(from_chips['w_in'] + [last_quarter], axis=1), name="sum_w_in")
    others['w_in'] = _exchange_call(_SwapSibling([parts['w_in']]), name="swap_w_in")[0]
    sh_out = {}
    for n in sharded:
        part, other = parts[n], others[n]
        sh_out[n] = _adamw(mat(W[n]), mat(M[n]), mat(V[n]), [part, other], tm=_pick(part.shape[0], (256, 128, 64)),
                           name="adamw_" + n)

    def leaf(k, n):
        if n == 'ada_w':
            return ada_out[k].reshape(W[n].shape)
        if n in sharded:
            return sh_out[n][k].reshape(W[n].shape)
        return sm_out[k][n]
    outs = [leaf(k, n) for k in range(4) for n in WEIGHTS]
    return (loss, dx[None], *outs)
```

```python
import functools
import math

import jax
import jax.numpy as jnp
from jax import lax
from jax.experimental import pallas as pl
from jax.experimental.pallas import tpu as pltpu

F32 = jnp.float32
BF16 = jnp.bfloat16
MXU = BF16
HI = lax.Precision.HIGHEST

D = 1024
HD = 64
NH = 8
RW = NH * HD
SHIFT_W = 3 * RW + 64 + 64 + 128
QKV_W = RW + 2 * 128
GATE_W = 2 * D
IN_W = SHIFT_W + QKV_W + GATE_W
DFF = 2816
BLK = 128
CHUNK = 64
RMS_EPS = 1e-6
GN_EPS = 64e-5
NEG_INF = -1e30
ADAM_LR, ADAM_B1, ADAM_B2, ADAM_EPS, ADAM_WD, ADAM_STEP = 0.001, 0.9, 0.999, 1e-08, 0.01, 10
VMEM_LIMIT = 56 * 1024 * 1024
MESH = pl.DeviceIdType.MESH


def _cparams(sem=None):
    return pltpu.CompilerParams(dimension_semantics=sem, vmem_limit_bytes=VMEM_LIMIT)


def _full_spec(a):
    nd = a.ndim
    return pl.BlockSpec(a.shape, lambda *_: (0,) * nd)


def _rowwise(fn, rows, consts, outs, *, tm, name, halo=(), hosted=None):
    rows = [(a + (0,))[:3] if isinstance(a, tuple) else (a, a.shape[1], 0) for a in rows]
    T = rows[0][0].shape[0]
    assert T % tm == 0 and tm % 8 == 0
    n_tiles = T // tm
    n_in = len(rows) + len(halo) + len(consts)
    in_specs = [pl.BlockSpec((tm, nc), lambda i, j=j: (i, j)) for _, nc, j in rows]
    args = [a for a, _, _ in rows]
    for a, nc, kind in halo:
        if kind == 'prev':
            in_specs.append(pl.BlockSpec((8, nc), lambda i: (jnp.maximum(i * (tm // 8) - 1, 0), 0)))
        else:
            in_specs.append(pl.BlockSpec((8, nc), lambda i: (jnp.minimum((i + 1) * (tm // 8), T // 8 - 1), 0)))
        args.append(a)
    in_specs += [_full_spec(a) for a in consts]
    args += list(consts)
    out_shape, out_specs = [], []
    for ncols, dtype, kind in outs:
        if kind == 'row':
            out_shape.append(jax.ShapeDtypeStruct((T, ncols), dtype))
            out_specs.append(pl.BlockSpec((tm, ncols), lambda i: (i, 0)))
        else:
            out_shape.append(jax.ShapeDtypeStruct((kind, ncols), dtype))
            out_specs.append(pl.BlockSpec((kind, ncols), lambda i: (0, 0)))

    def body(*refs):
        i = pl.program_id(0)
        vals = [r[...] for r in refs[:n_in]]
        res = fn(i, n_tiles, *vals)
        for (ncols, dtype, kind), o_ref, val in zip(outs, refs[n_in:], res, strict=True):
            if kind == 'row':
                o_ref[...] = val.astype(dtype)
            else:
                @pl.when(i == 0)
                def _():
                    o_ref[...] = jnp.zeros_like(o_ref)
                o_ref[...] += val.astype(dtype)

    h_in, h_in_specs, h_out_specs, h_out_shape, h_scratch = _hosted_args(hosted)
    res = pl.pallas_call(
        _hosting(body, hosted, n_in, len(outs), 0, n_tiles), name=name, grid=(n_tiles,),
        in_specs=in_specs + h_in_specs, out_specs=out_specs + h_out_specs, out_shape=out_shape + h_out_shape,
        scratch_shapes=h_scratch, compiler_params=_cparams(("arbitrary",)),
    )(*args, *h_in)
    return res


def _pick(n, cands):
    for c in cands:
        if n % c == 0:
            return c
    return n


MM_ROWS = (1024, 512, 256, 128)
MM_COLS = (1536, 1408, 1024, 896, 768, 512, 256, 128)
MM_WIDE = 3000


def _mm_nn(a, w, *, name, out_dtype=F32, hosted=None):
    T, K = a.shape
    N = w.shape[1]
    tm = _pick(T, MM_ROWS)
    tn = _pick(N, MM_COLS)
    grid = (N // tn, T // tm)
    h_in, h_in_specs, h_out_specs, h_out_shape, h_scratch = _hosted_args(hosted)

    def body(a_ref, w_ref, o_ref):
        o_ref[...] = jnp.dot(a_ref[...], w_ref[...], preferred_element_type=F32).astype(out_dtype)

    res = pl.pallas_call(
        _hosting(body, hosted, 2, 1, 0, grid), name=name, grid=grid,
        in_specs=[pl.BlockSpec((tm, K), lambda j, i: (i, 0)), pl.BlockSpec((K, tn), lambda j, i: (0, j))] + h_in_specs,
        out_specs=[pl.BlockSpec((tm, tn), lambda j, i: (i, j))] + h_out_specs,
        out_shape=[jax.ShapeDtypeStruct((T, N), out_dtype)] + h_out_shape, scratch_shapes=h_scratch,
        compiler_params=_cparams(("arbitrary", "arbitrary")),
    )(a, w, *h_in)
    return res if hosted else res[0]


def _mm_nt(dy, w, *, name, out_dtype=F32, hosted=None):
    T, N = dy.shape
    K = w.shape[0]
    tm = _pick(T, MM_ROWS if N <= MM_WIDE else MM_ROWS[1:])
    tk = _pick(K, MM_COLS[1:])
    grid = (K // tk, T // tm)
    h_in, h_in_specs, h_out_specs, h_out_shape, h_scratch = _hosted_args(hosted)

    def body(dy_ref, w_ref, o_ref):
        o_ref[...] = lax.dot_general(dy_ref[...], w_ref[...], (((1,), (1,)), ((), ())),
                                     preferred_element_type=F32).astype(out_dtype)

    res = pl.pallas_call(
        _hosting(body, hosted, 2, 1, 0, grid), name=name, grid=grid,
        in_specs=[pl.BlockSpec((tm, N), lambda j, i: (i, 0)), pl.BlockSpec((tk, N), lambda j, i: (j, 0))] + h_in_specs,
        out_specs=[pl.BlockSpec((tm, tk), lambda j, i: (i, j))] + h_out_specs,
        out_shape=[jax.ShapeDtypeStruct((T, K), out_dtype)] + h_out_shape, scratch_shapes=h_scratch,
        compiler_params=_cparams(("arbitrary", "arbitrary")),
    )(dy, w, *h_in)
    return res if hosted else res[0]


def _mm_tn(a, dy, *, name, out_dtype=F32, col_shards=None):
    T, K = a.shape
    N = dy.shape[1]
    tm = _pick(T, MM_ROWS)
    tn = N // col_shards if col_shards else _pick(N, MM_COLS[1:])
    n_t = T // tm

    def body(a_ref, dy_ref, o_ref, acc_ref):
        i = pl.program_id(1)

        @pl.when(i == 0)
        def _():
            acc_ref[...] = jnp.zeros_like(acc_ref)

        acc_ref[...] += lax.dot_general(a_ref[...], dy_ref[...], (((0,), (0,)), ((), ())), preferred_element_type=F32)

        @pl.when(i == n_t - 1)
        def _():
            o_ref[...] = acc_ref[...].astype(out_dtype)

    if col_shards:
        out_specs = pl.BlockSpec((None, K, tn), lambda j, i: (j, 0, 0))
        out_shape = jax.ShapeDtypeStruct((col_shards, K, tn), out_dtype)
    else:
        out_specs = pl.BlockSpec((K, tn), lambda j, i: (0, j))
        out_shape = jax.ShapeDtypeStruct((K, N), out_dtype)
    return pl.pallas_call(
        body, name=name, grid=(N // tn, n_t),
        in_specs=[pl.BlockSpec((tm, K), lambda j, i: (i, 0)), pl.BlockSpec((tm, tn), lambda j, i: (i, j))],
        out_specs=out_specs, out_shape=out_shape, scratch_shapes=[pltpu.VMEM((K, tn), F32)],
        compiler_params=_cparams(("arbitrary", "arbitrary")),
    )(a, dy)


def _mm_then(products, fn, rows, consts, outs, *, tm, name, lhs_fn=None, hosted=None):
    T = (rows[0] if lhs_fn else products[0][0]).shape[0]
    n_tiles = T // tm
    in_specs, args = [], []
    for a, w, _ in products:
        if a is not None:
            in_specs.append(pl.BlockSpec((tm, a.shape[1]), lambda i: (i, 0)))
            args.append(a)
        in_specs.append(_full_spec(w))
        args.append(w)
    n_w = len(args)
    in_specs += [pl.BlockSpec((tm, a.shape[1]), lambda i: (i, 0)) for a in rows] + [_full_spec(c_) for c_ in consts]
    args += list(rows) + list(consts)
    n_in = len(args)
    out_shape, out_specs = [], []
    for ncols, dtype, kind in outs:
        if kind == 'row':
            out_shape.append(jax.ShapeDtypeStruct((T, ncols), dtype))
            out_specs.append(pl.BlockSpec((tm, ncols), lambda i: (i, 0)))
        else:
            out_shape.append(jax.ShapeDtypeStruct((kind, ncols), dtype))
            out_specs.append(pl.BlockSpec((kind, ncols), lambda i: (0, 0)))

    def body(*refs):
        i = pl.program_id(0)
        tiles = [r[...] for r in refs[n_w:n_in]]
        made = []
        if lhs_fn:
            made = lhs_fn(*tiles)
            made = list(made) if isinstance(made, tuple) else [made]
            made[0] = made[0].astype(MXU)
        y, pos = None, 0
        for a, _, form in products:
            if a is None:
                lhs = made[0]
            else:
                lhs, pos = refs[pos][...], pos + 1
            dims = (((1,), (0,)), ((), ())) if form == 'nn' else (((1,), (1,)), ((), ()))
            t = lax.dot_general(lhs, refs[pos][...], dims, preferred_element_type=F32)
            pos += 1
            y = t if y is None else y + t
        res = fn(i, n_tiles, y, *made, *tiles)
        for (ncols, dtype, kind), o_ref, val in zip(outs, refs[n_in:], res, strict=True):
            if kind == 'row':
                o_ref[...] = val.astype(dtype)
            else:
                @pl.when(i == 0)
                def _():
                    o_ref[...] = jnp.zeros_like(o_ref)
                o_ref[...] += val.astype(dtype)

    h_in, h_in_specs, h_out_specs, h_out_shape, h_scratch = _hosted_args(hosted)
    return pl.pallas_call(
        _hosting(body, hosted, n_in, len(outs), 0, n_tiles), name=name, grid=(n_tiles,),
        in_specs=in_specs + h_in_specs, out_specs=out_specs + h_out_specs, out_shape=out_shape + h_out_shape,
        scratch_shapes=h_scratch, compiler_params=_cparams(("arbitrary",)))(*args, *h_in)


def _seg_ones(n):
    r = lax.broadcasted_iota(jnp.int32, (n, n), 0) // HD
    c = lax.broadcasted_iota(jnp.int32, (n, n), 1) // HD
    return (r == c).astype(F32)


def _segsum_raw(x):
    ones = _seg_ones(x.shape[1])
    if MXU == F32:
        return jnp.dot(x, ones, precision=HI, preferred_element_type=F32)
    hi = x.astype(MXU)
    lo = (x - hi.astype(F32)).astype(MXU)
    ones = ones.astype(MXU)
    return jnp.dot(hi, ones, preferred_element_type=F32) + jnp.dot(lo, ones, preferred_element_type=F32)


@jax.custom_vjp
def _segsum(x):
    return _segsum_raw(x)


def _segsum_fwd(x):
    return _segsum_raw(x), None


def _segsum_bwd(_, g):
    return (_segsum_raw(g),)


_segsum.defvjp(_segsum_fwd, _segsum_bwd)


def _mxu(x):
    return x.astype(MXU)


@jax.custom_vjp
def _bdot(a, b):
    return jnp.dot(_mxu(a), _mxu(b), preferred_element_type=F32)


def _bdot_fwd(a, b):
    return _bdot(a, b), (a, b)


def _bdot_bwd(res, g):
    a, b = res
    da = lax.dot_general(_mxu(g), _mxu(b), (((1,), (1,)), ((), ())), preferred_element_type=F32)
    db = lax.dot_general(_mxu(a), _mxu(g), (((0,), (0,)), ((), ())), preferred_element_type=F32)
    return da.astype(a.dtype), db.astype(b.dtype)


_bdot.defvjp(_bdot_fwd, _bdot_bwd)


def _sigmoid(x):
    return 1.0 / (1.0 + jnp.exp(-x))


def _softplus(x):
    return jnp.maximum(x, 0.0) + jnp.log(1.0 + jnp.exp(jnp.minimum(x, -x)))


def _norm_mod(x, gain, scale, shift):
    inv = lax.rsqrt(jnp.mean(x * x, axis=-1, keepdims=True) + RMS_EPS)
    return (x * inv) * gain * (1.0 + scale) + shift


def _prep(mixed, decay_w0, lora_up, iclr_a0, gate_up, k_k, k_a):
    r = mixed[:, 0:RW]
    k = mixed[:, RW:2 * RW]
    v = mixed[:, 2 * RW:3 * RW]
    z = mixed[:, 3 * RW:3 * RW + 128]
    xg = mixed[:, 3 * RW + 128:]
    lane = lax.broadcasted_iota(jnp.int32, z.shape, 1)
    tz = jnp.where(lane < 64, jnp.tanh(z), z)
    lo = _bdot(tz, lora_up)
    w_log = -_softplus(-(decay_w0 + lo[:, :RW])) - 0.5
    lw = -jnp.exp(w_log)
    a_ic = _sigmoid(iclr_a0 + lo[:, RW:])
    g = _bdot(_sigmoid(xg), gate_up)
    kk = k * k_k
    kk = kk / jnp.maximum(jnp.sqrt(_segsum(kk * kk)), 1e-12)
    k_mod = k * (1.0 + (a_ic - 1.0) * k_a)
    return jnp.concatenate([r, lw, k_mod, v, -kk, kk * a_ic, g], axis=1)


def _post(y, r, k, v, g, lnx_gain, lnx_bias, r_k):
    mu = _segsum(y) * (1.0 / HD)
    yc = y - mu
    var = _segsum(yc * yc) * (1.0 / HD)
    yn = yc * lax.rsqrt(var + GN_EPS) * lnx_gain + lnx_bias
    bonus = _segsum(r * k * r_k) * v
    return (yn + bonus) * g


def _merge(pg, ma, mb, bias):
    gates = _sigmoid(pg + bias)
    return gates[:, :D] * ma + gates[:, D:] * mb


def _swiglu(u, v):
    return u * _sigmoid(u) * v


def _ffn_in(h, w1, w3, *, name):
    T, K = h.shape
    F = w1.shape[1]
    tm = _pick(T, MM_ROWS)
    tn = _pick(F, MM_COLS[1:])

    def body(h_ref, w1_ref, w3_ref, u_ref, v_ref, a_ref):
        u = jnp.dot(h_ref[...], w1_ref[...], preferred_element_type=F32).astype(MXU)
        v = jnp.dot(h_ref[...], w3_ref[...], preferred_element_type=F32).astype(MXU)
        u_ref[...] = u
        v_ref[...] = v
        a_ref[...] = _swiglu(u.astype(F32), v.astype(F32)).astype(MXU)

    wspec = pl.BlockSpec((K, tn), lambda j, i: (0, j))
    ospec = pl.BlockSpec((tm, tn), lambda j, i: (i, j))
    return pl.pallas_call(
        body, name=name, grid=(F // tn, T // tm),
        in_specs=[pl.BlockSpec((tm, K), lambda j, i: (i, 0)), wspec, wspec],
        out_specs=[ospec] * 3, out_shape=[jax.ShapeDtypeStruct((T, F), MXU)] * 3,
        compiler_params=_cparams(("arbitrary", "arbitrary")),
    )(h, w1, w3)


def _ffn_act_bwd(dff, w2, u, v, *, name):
    T, N = dff.shape
    F = w2.shape[0]
    tm = _pick(T, MM_ROWS)
    tk = _pick(F, MM_COLS[1:])

    def body(dy_ref, w_ref, u_ref, v_ref, du_ref, dv_ref):
        dact = lax.dot_general(dy_ref[...], w_ref[...], (((1,), (1,)), ((), ())), preferred_element_type=F32)
        _, vjp = jax.vjp(_swiglu, u_ref[...].astype(F32), v_ref[...].astype(F32))
        du, dv = vjp(dact)
        du_ref[...] = du.astype(MXU)
        dv_ref[...] = dv.astype(MXU)

    tile = pl.BlockSpec((tm, tk), lambda j, i: (i, j))
    return pl.pallas_call(
        body, name=name, grid=(F // tk, T // tm),
        in_specs=[pl.BlockSpec((tm, N), lambda j, i: (i, 0)), pl.BlockSpec((tk, N), lambda j, i: (j, 0)), tile, tile],
        out_specs=[tile, tile], out_shape=[jax.ShapeDtypeStruct((T, F), MXU)] * 2,
        compiler_params=_cparams(("arbitrary", "arbitrary")),
    )(dff, w2, u, v)


@functools.partial(jax.custom_vjp, nondiff_argnums=(1,))
def _lane_roll(x, s):
    return pltpu.roll(x, s, 1)


def _lane_roll_fwd(x, s):
    return pltpu.roll(x, s, 1), None


def _lane_roll_bwd(s, _, g):
    n = g.shape[1]
    return (pltpu.roll(g, (n - s) % n, 1),)


_lane_roll.defvjp(_lane_roll_fwd, _lane_roll_bwd)


def _rope(x, cos, sin_lo, sin_hi):
    n = x.shape[1]
    return x * cos + _lane_roll(x, n - 8) * sin_lo + _lane_roll(x, 8) * sin_hi


def _head_rms(x, gain):
    return x * lax.rsqrt(_segsum(x * x) * (1.0 / HD) + RMS_EPS) * gain


def _attn_blocks(qkv_c, qkv_p, tab_c, tab_p, qg, kg, sinks, first):
    nb = qkv_c.shape[0] // BLK
    G = 4

    def tabs(tab, n):
        return [jnp.tile(tab[:, j * 128:(j + 1) * 128], (1, n // 128)) for j in range(3)]

    qg = jnp.concatenate([qg] * NH, axis=1)
    kg = jnp.concatenate([kg] * 2, axis=1)
    q = _rope(_head_rms(qkv_c[:, :RW], qg), *tabs(tab_c, RW))
    k_in = jnp.concatenate([qkv_p[:, RW:RW + 128], qkv_c[:, RW:RW + 128]], axis=0)
    k = _rope(_head_rms(k_in, kg), *tabs(jnp.concatenate([tab_p, tab_c], axis=0), 128))
    v = jnp.concatenate([qkv_p[:, RW + 128:], qkv_c[:, RW + 128:]], axis=0)

    pile = lambda xs: jnp.concatenate([x_[None] for x_ in xs], axis=0)

    def bands(t):
        return pile([t[b * BLK:(b + 2) * BLK, kvh * HD:(kvh + 1) * HD] for kvh in range(2) for b in range(nb)])

    qs = pile([jnp.concatenate([q[b * BLK:(b + 1) * BLK, (G * kvh + g) * HD:(G * kvh + g + 1) * HD]
                                for g in range(G)], axis=0) for kvh in range(2) for b in range(nb)])
    s = _bmm(qs, bands(k), 2, 2, 1) * (HD ** -0.5)
    qi = lax.broadcasted_iota(jnp.int32, (G * BLK, 2 * BLK), 0) % BLK
    kj = lax.broadcasted_iota(jnp.int32, (G * BLK, 2 * BLK), 1)
    dist = qi + BLK - kj
    in_band = (dist >= 0) & (dist < BLK)
    pair = lax.broadcasted_iota(jnp.int32, (2 * nb, 1, 1), 0)
    no_prev = (pair % nb == 0) & first
    valid = in_band[None] & (jnp.logical_not(no_prev) | (kj >= BLK)[None])
    s = jnp.where(valid, s, NEG_INF)
    row_g = lax.broadcasted_iota(jnp.int32, (G * BLK, 1), 0) // BLK
    sink = []
    for kvh in range(2):
        col = jnp.zeros((G * BLK, 1), F32)
        for g in range(G):
            col = jnp.where(row_g == g, sinks[:, G * kvh + g:G * kvh + g + 1], col)
        sink += [col] * nb
    sink = pile(sink)
    m = lax.stop_gradient(jnp.maximum(jnp.max(s, axis=-1, keepdims=True), sink))
    e = jnp.exp(s - m)
    p = e * (1.0 / (jnp.sum(e, axis=-1, keepdims=True) + jnp.exp(sink - m)))
    o = _bmm(p, bands(v), 2, 1, 1)
    return jnp.concatenate([jnp.concatenate([o[kvh * nb + b, g * BLK:(g + 1) * BLK] for kvh in range(2)
                                             for g in range(G)], axis=1) for b in range(nb)], axis=0)


def _heads(x):
    return jnp.stack([x[:, h * HD:(h + 1) * HD] for h in range(NH)], axis=0)


def _unheads(x):
    return jnp.concatenate([x[h] for h in range(NH)], axis=1)


def _split(x, n):
    parts, rest = [], x
    for _ in range(n):
        p = rest.astype(MXU)
        parts.append(p)
        rest = rest - p.astype(F32)
    return parts


def _bdot_batched(a, b, ca, cb):
    return lax.dot_general(a, b, (((ca,), (cb,)), ((0,), (0,))), preferred_element_type=F32)


def _bmm_passes(a, b, ca, cb, passes):
    if MXU == F32:
        return lax.dot_general(a, b, (((ca,), (cb,)), ((0,), (0,))), precision=HI, preferred_element_type=F32)
    if passes == 1:
        return _bdot_batched(a.astype(MXU), b.astype(MXU), ca, cb)
    (a0, a1), (b0, b1) = _split(a, 2), _split(b, 2)
    return _bdot_batched(a0, b0, ca, cb) + (_bdot_batched(a0, b1, ca, cb) + _bdot_batched(a1, b0, ca, cb))


@functools.partial(jax.custom_vjp, nondiff_argnums=(2, 3, 4))
def _bmm(a, b, ca, cb, passes=1):
    return _bmm_passes(a, b, ca, cb, passes)


def _bmm_fwd(a, b, ca, cb, passes):
    return _bmm_passes(a, b, ca, cb, passes), (a, b)


def _bmm_bwd(ca, cb, passes, res, g):
    a, b = res
    if (ca, cb) == (2, 1):
        return _bmm_passes(g, b, 2, 2, passes), _bmm_passes(a, g, 1, 1, passes)
    if (ca, cb) == (2, 2):
        return _bmm_passes(g, b, 2, 1, passes), _bmm_passes(g, a, 1, 1, passes)
    return _bmm_passes(b, g, 2, 2, passes), _bmm_passes(a, g, 2, 1, passes)


_bmm.defvjp(_bmm_fwd, _bmm_bwd)


def _tri_dot(x, transpose):
    C = x.shape[1]
    ri = lax.broadcasted_iota(jnp.int32, (C, C), 0)
    ci = lax.broadcasted_iota(jnp.int32, (C, C), 1)
    tri = jnp.broadcast_to(((ri <= ci) if transpose else (ri >= ci)).astype(MXU), (x.shape[0], C, C))
    if MXU == F32:
        return lax.dot_general(tri, x, (((2,), (1,)), ((0,), (0,))), precision=HI, preferred_element_type=F32)
    p0, p1, p2 = _split(x, 3)
    return _bdot_batched(tri, p0, 2, 1) + (_bdot_batched(tri, p1, 2, 1) + _bdot_batched(tri, p2, 2, 1))


@jax.custom_vjp
def _cumsum_rows(x):
    return _tri_dot(x, False)


def _cumsum_rows_fwd(x):
    return _tri_dot(x, False), None


def _cumsum_rows_bwd(_, g):
    return (_tri_dot(g, True),)


_cumsum_rows.defvjp(_cumsum_rows_fwd, _cumsum_rows_bwd)

P_SCORE = 1
P_SOLVE = 1
P_STATE = 1
SCAN_CHUNKS = (4, 2, 1)


def _neumann(l):
    C = l.shape[1]
    eye = (lax.broadcasted_iota(jnp.int32, (C, C), 0) == lax.broadcasted_iota(jnp.int32, (C, C), 1)).astype(F32)
    x, lp = eye + l, l
    for _ in range(int(math.log2(C)) - 1):
        lp = _bmm(lp, lp, 2, 1, P_SOLVE)
        x = x + _bmm(x, lp, 2, 1, P_SOLVE)
    return x


@jax.custom_vjp
def _unit_lower_inverse(l):
    return _neumann(l)


def _unit_lower_inverse_fwd(l):
    x = _neumann(l)
    return x, x


def _unit_lower_inverse_bwd(x, g):
    return (_bmm(_bmm(x, g, 1, 1, P_SOLVE), x, 2, 2, P_SOLVE),)


_unit_lower_inverse.defvjp(_unit_lower_inverse_fwd, _unit_lower_inverse_bwd)


def _known_inverse(x):
    @jax.custom_vjp
    def f(l):
        return x

    f.defvjp(lambda l: (x, None), lambda _, g: (_bmm(_bmm(x, g, 1, 1, P_SOLVE), x, 2, 2, P_SOLVE),))
    return f


def _chunk(S0, r, lw, k, v, a, b, inverse=None):
    C = CHUNK
    n = r.shape[1] // C
    fold = lambda t: t.reshape(NH * n, C, HD)
    r, lw, k, v, a, b = (fold(t) for t in (r, lw, k, v, a, b))
    ri = lax.broadcasted_iota(jnp.int32, (C, C), 0)
    ci = lax.broadcasted_iota(jnp.int32, (C, C), 1)
    strict = (ri > ci)
    cum = _cumsum_rows(lw)
    p_in = jnp.exp(cum)
    p_ex = jnp.exp(cum - lw)
    p_inv = jnp.exp(-cum)
    at, rt, bt, kt = a * p_ex, r * p_in, b * p_inv, k * p_inv
    lhs = jnp.concatenate([at, rt], axis=1)
    rhs_ = jnp.concatenate([bt, kt], axis=1)
    sc = _bmm(lhs, rhs_, 2, 2, P_SCORE)
    a_ab = jnp.where(strict, sc[:, :C, :C], 0.0)
    a_ak = jnp.where(strict, sc[:, :C, C:], 0.0)
    incl2 = (lax.broadcasted_iota(jnp.int32, (C, 2 * C), 0) >= lax.broadcasted_iota(jnp.int32, (C, 2 * C), 1) % C)
    a_r = jnp.where(incl2, sc[:, C:, :], 0.0)
    av = _bmm(a_ak, v, 2, 1, P_SCORE)
    x = x_all = (_unit_lower_inverse if inverse is None else _known_inverse(inverse))(a_ab)
    p_last = jnp.exp(cum[:, C - 1:C, :])
    per_chunk = lambda t: t.reshape((NH, n) + t.shape[1:])
    lhs, rhs_, a_r, av, x, v, p_last = (per_chunk(t) for t in (lhs, rhs_, a_r, av, x, v, p_last))
    S, ys = S0, []
    for c in range(n):
        s0 = _bmm(lhs[:, c], S, 2, 2, P_STATE)
        u = _bmm(x[:, c], s0[:, :C] + av[:, c], 2, 1, P_SOLVE)
        uv = jnp.concatenate([u, v[:, c]], axis=1)
        ys.append(s0[:, C:] + _bmm(a_r[:, c], uv, 2, 1, P_SCORE))
        S = (S + _bmm(uv, rhs_[:, c], 1, 1, P_STATE)) * p_last[:, c]
    return jnp.concatenate(ys, axis=1), S, x_all


def _hosting(body, ex, n_in, n_out, n_scratch, n_steps):
    if ex is None:
        return body

    def wrapped(*refs):
        a = n_in
        b = a + ex.n_in
        c = b + n_out
        d = c + ex.n_out
        e = d + n_scratch
        ex_refs = (refs[a:b], refs[c:d], refs[e:])
        grid = n_steps if isinstance(n_steps, tuple) else (n_steps,)
        first = last = True
        for ax_, size in enumerate(grid):
            first = first & (pl.program_id(ax_) == 0)
            last = last & (pl.program_id(ax_) == size - 1)

        @pl.when(first)
        def _():
            ex.start(*ex_refs)

        body(*refs[:a], *refs[b:c], *refs[d:e])

        @pl.when(last)
        def _():
            ex.wait(*ex_refs)

    return wrapped


def _hosted_args(ex):
    if ex is None:
        return [], [], [], [], []
    any_spec = pl.BlockSpec(memory_space=pl.ANY)
    return list(ex.arrays), [any_spec] * ex.n_in, [any_spec] * ex.n_out, list(ex.out_shape), list(ex.scratch)


def _scan_fwd(rw, *, name, hosted=None):
    T = rw.shape[0]
    nc = _pick(T // CHUNK, SCAN_CHUNKS)
    rows = CHUNK * nc
    n = T // rows
    h_in, h_in_specs, h_out_specs, h_out_shape, h_scratch = _hosted_args(hosted)

    def body(r_ref, lw_ref, k_ref, v_ref, a_ref, b_ref, y_ref, ck_ref, inv_ref, s_ref):
        @pl.when(pl.program_id(0) == 0)
        def _():
            s_ref[...] = jnp.zeros_like(s_ref)

        S0 = s_ref[...]
        ck_ref[0] = S0
        y, S1, inv = _chunk(S0, *[_heads(ref[...]) for ref in (r_ref, lw_ref, k_ref, v_ref, a_ref, b_ref)])
        y_ref[...] = _unheads(y)
        inv_ref[0] = inv
        s_ref[...] = S1

    col = lambda j: pl.BlockSpec((rows, RW), lambda i: (i, j))
    return pl.pallas_call(
        _hosting(body, hosted, 6, 3, 1, n), name=name, grid=(n,),
        in_specs=[col(j) for j in range(6)] + h_in_specs,
        out_specs=[pl.BlockSpec((rows, RW), lambda i: (i, 0)),
                   pl.BlockSpec((1, NH, HD, HD), lambda i: (i, 0, 0, 0)),
                   pl.BlockSpec((1, NH * nc, CHUNK, CHUNK), lambda i: (i, 0, 0, 0))] + h_out_specs,
        out_shape=[jax.ShapeDtypeStruct((T, RW), F32), jax.ShapeDtypeStruct((n, NH, HD, HD), F32),
                   jax.ShapeDtypeStruct((n, NH * nc, CHUNK, CHUNK), F32)] + h_out_shape,
        scratch_shapes=[pltpu.VMEM((NH, HD, HD), F32)] + h_scratch,
        compiler_params=_cparams(("arbitrary",)),
    )(rw, rw, rw, rw, rw, rw, *h_in)


def _scan_bwd(rw, ck, inv, dy, *, name, hosted=None):
    T = rw.shape[0]
    nc = _pick(T // CHUNK, SCAN_CHUNKS)
    rows = CHUNK * nc
    n = T // rows

    def body(r_ref, lw_ref, k_ref, v_ref, a_ref, b_ref, ck_ref, inv_ref, dy_ref, o_ref, ds_ref):
        @pl.when(pl.program_id(0) == 0)
        def _():
            ds_ref[...] = jnp.zeros_like(ds_ref)

        prim = [_heads(ref[...]) for ref in (r_ref, lw_ref, k_ref, v_ref, a_ref, b_ref)]
        known = inv_ref[0]
        _, vjp = jax.vjp(lambda S0, *p: _chunk(S0, *p, inverse=known)[:2], ck_ref[0], *prim)
        grads = vjp((_heads(dy_ref[...]), ds_ref[...]))
        ds_ref[...] = grads[0]
        o_ref[...] = jnp.concatenate([_unheads(g) for g in grads[1:]], axis=1).astype(o_ref.dtype)

    h_in, h_in_specs, h_out_specs, h_out_shape, h_scratch = _hosted_args(hosted)
    col = lambda j: pl.BlockSpec((rows, RW), lambda i: (n - 1 - i, j))
    return pl.pallas_call(
        _hosting(body, hosted, 9, 1, 1, n), name=name, grid=(n,),
        in_specs=[col(j) for j in range(6)] + [pl.BlockSpec((1, NH, HD, HD), lambda i: (n - 1 - i, 0, 0, 0)),
                                               pl.BlockSpec((1, NH * nc, CHUNK, CHUNK), lambda i: (n - 1 - i, 0, 0, 0)),
                                               pl.BlockSpec((rows, RW), lambda i: (n - 1 - i, 0))] + h_in_specs,
        out_specs=[pl.BlockSpec((rows, 6 * RW), lambda i: (n - 1 - i, 0))] + h_out_specs,
        out_shape=[jax.ShapeDtypeStruct((T, 6 * RW), MXU)] + h_out_shape,
        scratch_shapes=[pltpu.VMEM((NH, HD, HD), F32)] + h_scratch,
        compiler_params=_cparams(("arbitrary",)),
    )(rw, rw, rw, rw, rw, rw, ck, inv, dy, *h_in)


ATTN_BLOCKS = (4, 2, 1)

def _attn_fwd(qkv, tab, qg, kg, sinks, *, name, hosted=None):
    T = qkv.shape[0]
    nb = _pick(T // BLK, ATTN_BLOCKS)
    n = T // (BLK * nb)
    h_in, h_in_specs, h_out_specs, h_out_shape, h_scratch = _hosted_args(hosted)

    def body(c_ref, p_ref, tc_ref, tp_ref, qg_ref, kg_ref, s_ref, o_ref):
        o_ref[...] = _attn_blocks(c_ref[...], p_ref[...], tc_ref[...], tp_ref[...], qg_ref[...], kg_ref[...],
                                  s_ref[...], pl.program_id(0) == 0).astype(o_ref.dtype)

    cur = lambda w: pl.BlockSpec((nb * BLK, w), lambda i: (i, 0))
    prev = lambda w: pl.BlockSpec((BLK, w), lambda i: (jnp.maximum(i * nb - 1, 0), 0))
    return pl.pallas_call(
        _hosting(body, hosted, 7, 1, 0, n), name=name, grid=(n,),
        in_specs=[cur(QKV_W), prev(QKV_W), cur(3 * 128), prev(3 * 128), _full_spec(qg), _full_spec(kg),
                  _full_spec(sinks)] + h_in_specs,
        out_specs=[cur(RW)] + h_out_specs, out_shape=[jax.ShapeDtypeStruct((T, RW), MXU)] + h_out_shape,
        scratch_shapes=h_scratch,
        compiler_params=_cparams(("arbitrary",)),
    )(qkv, qkv, tab, tab, qg, kg, sinks, *h_in)


def _attn_bwd(qkv, tab, qg, kg, sinks, dy, *, name, hosted=None):
    T = qkv.shape[0]
    nb = _pick(T // BLK, ATTN_BLOCKS)
    n = T // (BLK * nb)
    h_in, h_in_specs, h_out_specs, h_out_shape, h_scratch = _hosted_args(hosted)

    def body(c_ref, p_ref, tc_ref, tp_ref, qg_ref, kg_ref, s_ref, dy_ref, dqkv_ref, dqg_ref, dkg_ref, ds_ref, carry_ref):
        i = pl.program_id(0)

        @pl.when(i == 0)
        def _():
            carry_ref[...] = jnp.zeros_like(carry_ref)
            dqg_ref[...] = jnp.zeros_like(dqg_ref)
            dkg_ref[...] = jnp.zeros_like(dkg_ref)
            ds_ref[...] = jnp.zeros_like(ds_ref)

        tc, tp = tc_ref[...], tp_ref[...]
        f = lambda c, p_, qg_, kg_, sk: _attn_blocks(c, p_, tc, tp, qg_, kg_, sk, i == n - 1)
        _, vjp = jax.vjp(f, c_ref[...], p_ref[...], qg_ref[...], kg_ref[...], s_ref[...])
        dc, dp, dqg, dkg, dsk = vjp(dy_ref[...].astype(F32))
        last = slice((nb - 1) * BLK, nb * BLK)
        dqkv_ref[...] = dc.astype(dqkv_ref.dtype)
        dqkv_ref[last, :] = (dc[last] + carry_ref[...]).astype(dqkv_ref.dtype)
        carry_ref[...] = dp
        dqg_ref[...] += dqg
        dkg_ref[...] += dkg
        ds_ref[...] += dsk

    cur = lambda w: pl.BlockSpec((nb * BLK, w), lambda i: (n - 1 - i, 0))
    prev = lambda w: pl.BlockSpec((BLK, w), lambda i: (jnp.maximum((n - 1 - i) * nb - 1, 0), 0))
    return pl.pallas_call(
        _hosting(body, hosted, 8, 4, 1, n), name=name, grid=(n,),
        in_specs=[cur(QKV_W), prev(QKV_W), cur(3 * 128), prev(3 * 128), _full_spec(qg), _full_spec(kg), _full_spec(sinks),
                  cur(RW)] + h_in_specs,
        out_specs=[cur(QKV_W), _full_spec(qg), _full_spec(kg), _full_spec(sinks)] + h_out_specs,
        out_shape=[jax.ShapeDtypeStruct((T, QKV_W), MXU), jax.ShapeDtypeStruct(qg.shape, F32),
                   jax.ShapeDtypeStruct(kg.shape, F32), jax.ShapeDtypeStruct(sinks.shape, F32)] + h_out_shape,
        scratch_shapes=[pltpu.VMEM((BLK, QKV_W), F32)] + h_scratch,
        compiler_params=_cparams(("arbitrary",)),
    )(qkv, qkv, tab, tab, qg, kg, sinks, dy, *h_in)


def _shift_down(cur, prev8, i):
    rolled = pltpu.roll(cur, 1, 0)
    first_row = jnp.where(i > 0, prev8[7:8, :], 0.0)
    row = lax.broadcasted_iota(jnp.int32, cur.shape, 0)
    return jnp.where(row == 0, first_row, rolled)


def _shift_up(cur, next8, i, n):
    tm = cur.shape[0]
    rolled = pltpu.roll(cur, tm - 1, 0)
    last_row = jnp.where(i < n - 1, next8[0:1, :], 0.0)
    row = lax.broadcasted_iota(jnp.int32, cur.shape, 0)
    return jnp.where(row == tm - 1, last_row, rolled)


def _rope_table(positions):
    half = HD // 8
    inv_freq = 500000.0 ** (-jnp.arange(half, dtype=F32) / half)
    lane = jnp.arange(128) % HD
    rotary = lane < 2 * half
    freq = jnp.where(rotary, inv_freq[lane % half], 0.0)
    ang = positions.astype(F32)[:, None] * freq[None, :]
    cos, sin = jnp.cos(ang), jnp.sin(ang)
    return jnp.concatenate([jnp.where(rotary, cos, 1.0), jnp.where(lane < half, -sin, 0.0),
                            jnp.where(rotary & (lane >= half), sin, 0.0)], axis=1)


GATHER_BEHIND = {"f_proj_shift": [('ffn_w3', 512, 768)], "f_proj_gates": [('ffn_w3', 768, 1024)],
                 "f_prep": [('ffn_w2', 352, 704)],
                 "f_scan": [('ffn_w1', 0, 1024), ('w_branch_a', 0, 512), ('w_branch_b', 0, 512)],
                 "f_post": [('ffn_w3', 0, 512)], "f_attn": [('ffn_w2', 0, 352), ('w_out', 0, 256)]}
LATE = ['w_out', 'w_branch_a', 'w_branch_b', 'ffn_w1', 'ffn_w3', 'ffn_w2']
BACK_ATTN = ['w_out', 'w_branch_a', 'w_branch_b', 'ffn_w2']
BACK_SCAN = ['ffn_w1', 'ffn_w3']
BACK_LAST = ['w_in', 'decay_up', 'iclr_up', 'gate_up']


def _full_weight(g, ax):
    return g.reshape(-1, g.shape[2]) if ax == 0 else jnp.concatenate([g[j] for j in range(4)], axis=1)


def _local_step(x, target, ada, tab, w, s, shards=None):
    T = x.shape[0]
    tm = _pick(T, (512, 256, 128))
    tm_wide = _pick(T, (256, 128))
    tm_vjp = _pick(T, (256, 128))
    row = lambda n, dt=F32: (n, dt, 'row')
    acc = lambda n, r=1: (n, F32, r)

    def f_norm1(x_, g, ada_):
        return _norm_mod(x_, g, ada_[:, D:2 * D], ada_[:, 0:D])

    landed = {}

    def behind(kernel_name):
        if not shards:
            return None
        return _GatherChips([shards[n] if hi - lo == shards[n].shape[0] else shards[n][lo:hi]
                             for n, lo, hi in GATHER_BEHIND[kernel_name]])

    def took(kernel_name, got):
        landed.update(zip(GATHER_BEHIND[kernel_name], got))

    def mm_behind(a_, w_, kernel_name, **kw):
        ex = behind(kernel_name)
        res = _mm_nn(a_, w_, name=kernel_name, hosted=ex, **kw)
        if ex:
            took(kernel_name, res[1:])
            return res[0]
        return res

    h1, proj, *got = _mm_then([(None, w['w_in'][:, :SHIFT_W], 'nn')], lambda i, n, y, h, *_: (h, y), [x],
                              [s['norm1_gain'], ada], [row(D, MXU), row(SHIFT_W)], tm=tm, name="f_proj_shift",
                              lhs_fn=f_norm1, hosted=behind("f_proj_shift"))
    took("f_proj_shift", got)
    proj_qkv = _mm_nn(h1, w['w_in'][:, SHIFT_W:SHIFT_W + QKV_W], name="f_proj_qkv")
    proj_g = mm_behind(h1, w['w_in'][:, SHIFT_W + QKV_W:], "f_proj_gates", out_dtype=MXU)
    prep_consts = [s['decay_w0'], s['lora_up'], s['iclr_a0'], s['gate_up'], s['k_k'], s['k_a']]

    def f_prep(i, n, cur, prev8, mu, *params):
        mixed = cur + (_shift_down(cur, prev8, i) - cur) * mu
        return (_prep(mixed, *params),)
    rw, *got = _rowwise(f_prep, [(proj, SHIFT_W)], [s['tshift_mu']] + prep_consts, [row(7 * RW)], tm=tm_wide,
                        name="f_prep", halo=[(proj, SHIFT_W, 'prev')], hosted=behind("f_prep"))
    took("f_prep", got)
    y, ck, inv, *got = _scan_fwd(rw, name="f_scan", hosted=behind("f_scan"))
    took("f_scan", got)
    post_consts = [s['lnx_gain'], s['lnx_bias'], s['r_k']]

    rkvg = [(rw, RW, j) for j in (0, 2, 3, 6)]

    def f_post(i, n, *args):
        return (_post(*args),)
    ya, *got = _rowwise(f_post, [y] + rkvg, post_consts, [row(RW, MXU)], tm=tm_wide, name="f_post",
                        hosted=behind("f_post"))
    took("f_post", got)
    yb, *got = _attn_fwd(proj_qkv, tab, s['q_norm_gain'], s['k_norm_gain'], s['attn_sinks'], name="f_attn",
                         hosted=behind("f_attn"))
    took("f_attn", got)
    w = dict(w)
    if shards:
        ax = dict(SHARDED)
        for n in LATE:
            rows = [landed[key] for key in sorted(k_ for k_ in landed if k_[0] == n)]
            w[n] = _full_weight(rows[0] if len(rows) == 1 else jnp.concatenate(rows, axis=1), ax[n])
    def f_merge(pg, ya_, yb_, x_, wa, wb, bias, g, ada_):
        ma_ = jnp.dot(ya_, wa, preferred_element_type=F32).astype(MXU)
        mb_ = jnp.dot(yb_, wb, preferred_element_type=F32).astype(MXU)
        return _merge(pg.astype(F32), ma_.astype(F32), mb_.astype(F32), bias), ma_, mb_

    def f_res1(i, n, mo_, merged_, ma_, mb_, pg, ya_, yb_, x_, wa, wb, bias, g, ada_):
        x1_ = x_ + ada_[:, 2 * D:3 * D] * mo_
        return merged_, ma_, mb_, mo_, x1_, _norm_mod(x1_, g, ada_[:, 4 * D:5 * D], ada_[:, 3 * D:4 * D])
    merged, ma, mb, mo, x1, h2 = _mm_then(
        [(None, w['w_out'], 'nn')], f_res1, [proj_g, ya, yb, x],
        [w['w_branch_a'], w['w_branch_b'], s['branch_gate_b'], s['norm2_gain'], ada],
        [row(D, MXU), row(D, MXU), row(D, MXU), row(D), row(D), row(D, MXU)], tm=tm, name="f_out", lhs_fn=f_merge)
    u, v, act = _ffn_in(h2, w['ffn_w1'], w['ffn_w3'], name="f_ffn_in")

    def f_loss(i, n, ff_, x1_, tgt, ada_):
        g2 = ada_[:, 5 * D:6 * D]
        err = x1_ + g2 * ff_ - tgt
        dx2 = err * (1.0 / D)
        loss = 0.5 * jnp.sum(jnp.sum(err * err, axis=1, keepdims=True) * (1.0 / D), axis=0, keepdims=True)
        return dx2, (dx2 * g2), jnp.broadcast_to(loss, (1, 128)), jnp.sum(dx2 * ff_, axis=0, keepdims=True)
    dx2, dff, loss, dgate2 = _mm_then([(act, w['ffn_w2'], 'nn')], f_loss, [x1, target], [ada],
                                      [row(D), row(D, MXU), acc(128), acc(D)], tm=tm, name="f_ffn_out")

    du, dv = _ffn_act_bwd(dff, w['ffn_w2'], u, v, name="b_ffn_out_dx")
    g_w2 = _mm_tn(act, dff, name="b_ffn_out_dw", out_dtype=MXU)
    g_w1 = _mm_tn(h2, du, name="b_ffn_w1_dw", out_dtype=MXU)
    g_w3 = _mm_tn(h2, dv, name="b_ffn_w3_dw", out_dtype=MXU)

    def b_res1(i, n, dh2_, x1_, dx2_, mo_, g, ada_):
        _, vjp = jax.vjp(_norm_mod, x1_, g, ada_[:, 4 * D:5 * D], ada_[:, 3 * D:4 * D])
        dxn, dg, dsc, dsh = vjp(dh2_)
        dx1_ = dxn + dx2_
        g1 = ada_[:, 2 * D:3 * D]
        return dx1_, dx1_ * g1, dg, dsc, dsh, jnp.sum(dx1_ * mo_, axis=0, keepdims=True)
    dx1, dmo, d_gain2, d_scale2, d_shift2, dgate1 = _mm_then(
        [(du, w['ffn_w1'], 'nt'), (dv, w['ffn_w3'], 'nt')], b_res1, [x1, dx2, mo], [s['norm2_gain'], ada],
        [row(D), row(D, MXU), acc(D), acc(D), acc(D), acc(D)], tm=tm_wide, name="b_ffn_in_dx")
    g_wout = _mm_tn(merged, dmo, name="b_out_dw", out_dtype=MXU)

    def b_merge(i, n, dm, pg, ma_, mb_, wa, wb, bias):
        _, vjp = jax.vjp(_merge, pg.astype(F32), ma_.astype(F32), mb_.astype(F32), bias)
        dpg, dma_, dmb_, dbias = vjp(dm)
        dma_, dmb_ = dma_.astype(MXU), dmb_.astype(MXU)
        nt = lambda a_, b_: lax.dot_general(a_, b_, (((1,), (1,)), ((), ())), preferred_element_type=F32)
        return dpg, dma_, dmb_, nt(dma_, wa), nt(dmb_, wb), dbias
    dpg, dma, dmb, dya, dyb, d_bias = _mm_then(
        [(dmo, w['w_out'], 'nt')], b_merge, [proj_g, ma, mb], [w['w_branch_a'], w['w_branch_b'], s['branch_gate_b']],
        [row(GATE_W, MXU), row(D, MXU), row(D, MXU), row(RW), row(RW), acc(GATE_W)], tm=tm_wide, name="b_out_dx")
    g_wa = _mm_tn(ya, dma, name="b_branch_a_dw", out_dtype=MXU, col_shards=4)
    g_wb = _mm_tn(yb, dmb, name="b_branch_b_dw", out_dtype=MXU, col_shards=4)
    fs = DFF // 4
    gw = dict(w_branch_a=g_wa, w_branch_b=g_wb, w_out=g_wout.reshape(4, D // 4, D),
              ffn_w1=jnp.stack([g_w1[:, j * fs:(j + 1) * fs] for j in range(4)]),
              ffn_w3=jnp.stack([g_w3[:, j * fs:(j + 1) * fs] for j in range(4)]),
              ffn_w2=g_w2.reshape(4, fs, D))
    recv = {}
    dqkv, d_qg, d_kg, d_sinks, *got = _attn_bwd(
        proj_qkv, tab, s['q_norm_gain'], s['k_norm_gain'], s['attn_sinks'], dyb, name="b_attn",
        hosted=shards and _ScatterChips([gw[n] for n in BACK_ATTN]))
    recv.update(zip(BACK_ATTN, got))

    def b_post(i, n, y_, r_, k_, v_, g_, dya_, *params):
        _, vjp = jax.vjp(_post, y_, r_, k_, v_, g_, *params)
        dy_, dr_, dk_, dv_, dg_, *dparams = vjp(dya_)
        return (dy_, jnp.concatenate([dr_, dk_, dv_, dg_], axis=1), *dparams)
    dy, drkvg, d_lnx_gain, d_lnx_bias, d_r_k = _rowwise(
        b_post, [y] + rkvg + [dya], post_consts, [row(RW), row(4 * RW, MXU), acc(RW), acc(RW), acc(RW)], tm=tm_wide,
        name="b_post")
    dscan, *got = _scan_bwd(rw, ck, inv, dy, name="b_scan",
                            hosted=shards and _ScatterChips([gw[n] for n in BACK_SCAN]))
    recv.update(zip(BACK_SCAN, got))

    def b_prep(i, n, cur, drw_, dscan_, prev8, mu, *params):
        shifted = _shift_down(cur, prev8, i)
        mixed = cur + (shifted - cur) * mu
        _, vjp = jax.vjp(_prep, mixed, *params)
        blk = lambda t, j: t[:, j * RW:(j + 1) * RW].astype(F32)
        ct = jnp.concatenate([blk(dscan_, 0) + blk(drw_, 0), blk(dscan_, 1), blk(dscan_, 2) + blk(drw_, 1),
                              blk(dscan_, 3) + blk(drw_, 2), blk(dscan_, 4), blk(dscan_, 5), blk(drw_, 3)], axis=1)
        grads = vjp(ct)
        dmixed = grads[0]
        return (dmixed, jnp.sum(dmixed * (shifted - cur), axis=0, keepdims=True)) + tuple(grads[1:])
    dmixed, d_mu, d_w0, d_lora, d_a0, d_gate_up, d_kk, d_ka = _rowwise(
        b_prep, [(proj, SHIFT_W), drkvg, dscan], [s['tshift_mu']] + prep_consts,
        [row(SHIFT_W), acc(SHIFT_W), acc(RW), acc(2 * RW, 128), acc(RW), acc(RW, 128), acc(RW), acc(RW)],
        tm=tm_vjp, name="b_prep", halo=[(proj, SHIFT_W, 'prev')])

    def b_gather(i, n, dm, dqkv_, dpg_, next8, mu):
        dcur = dm * (1.0 - mu) + _shift_up(dm, next8, i, n) * mu
        return (jnp.concatenate([dcur.astype(MXU), dqkv_, dpg_], axis=1),)
    (dproj,) = _rowwise(b_gather, [dmixed, dqkv, dpg], [s['tshift_mu']], [row(IN_W, MXU)], tm=tm_wide, name="b_gather",
                        halo=[(dmixed, SHIFT_W, 'next')])
    g_win = _mm_tn(h1, dproj, name="b_proj_dw", out_dtype=MXU, col_shards=4)

    def col_blocks(g):
        k, n = g.shape
        return g.reshape(k, 4, n // 4).transpose(1, 0, 2).astype(MXU)
    gw.update(w_in=g_win, decay_up=col_blocks(d_lora[:64, :RW]), iclr_up=col_blocks(d_lora[64:, RW:]),
              gate_up=col_blocks(d_gate_up))
    top, bottom = None, None
    if shards:
        top = _ScatterChips([gw['w_in'][:, :D // 2]] + [gw[n] for n in BACK_LAST[1:]])
        bottom = _ScatterChips([gw['w_in'][:, D // 2:3 * D // 4]])
        dh1, *got_top = _mm_nt(dproj, w['w_in'], name="b_proj_dx", hosted=top)
    else:
        dh1 = _mm_nt(dproj, w['w_in'], name="b_proj_dx")

    def b_norm1(i, n, x_, dh1_, dx1_, g, ada_):
        _, vjp = jax.vjp(_norm_mod, x_, g, ada_[:, D:2 * D], ada_[:, 0:D])
        dxn, dg, dsc, dsh = vjp(dh1_)
        return dxn + dx1_, dg, dsc, dsh
    dx, d_gain1, d_scale1, d_shift1, *got_bottom = _rowwise(
        b_norm1, [x, dh1, dx1], [s['norm1_gain'], ada], [row(D), acc(D), acc(D), acc(D)], tm=tm, name="b_norm1",
        hosted=bottom)
    if shards:
        recv.update(zip(BACK_LAST[1:], got_top[1:]))
        recv['w_in'] = [got_top[0], got_bottom[0]]

    d_ada = jnp.concatenate([d_shift1, d_scale1, dgate1, d_shift2, d_scale2, dgate2], axis=1)
    gs = dict(norm1_gain=d_gain1, norm2_gain=d_gain2, tshift_mu=d_mu, decay_w0=d_w0, iclr_a0=d_a0, k_k=d_kk, k_a=d_ka,
              r_k=d_r_k, lnx_gain=d_lnx_gain, lnx_bias=d_lnx_bias, q_norm_gain=d_qg, k_norm_gain=d_kg,
              attn_sinks=d_sinks, branch_gate_b=d_bias)
    return loss, dx, d_ada, gw, gs, recv


ANY = pl.BlockSpec(memory_space=pl.ANY)


def _place():
    x, y, c = lax.axis_index("x"), lax.axis_index("y"), lax.axis_index("c")
    return x, y, c, [(1 - x, y), (x, 1 - y), (1 - x, 1 - y)]


def _all_gather8(x_shard, *, name):
    m_per, n = x_shard.shape

    def body(x_ref, out_ref, send_sems, recv_sems, local_sem):
        x, y, c, chips = _place()
        me, sibling = (x, y, c), (x, y, 1 - c)

        def rows(px, py, pc):
            return out_ref.at[pl.ds((4 * px + 2 * py + pc) * m_per, m_per), :]

        def copy(k, block, to, src=None):
            return pltpu.make_async_remote_copy(
                src_ref=rows(*block) if src is None else src, dst_ref=rows(*block),
                send_sem=send_sems.at[k], recv_sem=recv_sems.at[k], device_id=to, device_id_type=MESH)

        mine = pltpu.make_async_copy(x_ref, rows(*me), local_sem)
        mine.start()
        first = [copy(0, me, sibling, src=x_ref)]
        first += [copy(1 + j, me, (*chip, c), src=x_ref) for j, chip in enumerate(chips)]
        for cp in first:
            cp.start()
        passed = [copy(4 + j, (*chip, c), sibling) for j, chip in enumerate(chips)]
        for j, chip in enumerate(chips):
            copy(1 + j, (*chip, c), me).wait_recv()
            passed[j].start()
        copy(0, sibling, me).wait_recv()
        for j, chip in enumerate(chips):
            copy(4 + j, (*chip, 1 - c), me).wait_recv()
        for cp in first + passed:
            cp.wait_send()
        mine.wait()

    return pl.pallas_call(
        body, name=name, out_shape=jax.ShapeDtypeStruct((8 * m_per, n), x_shard.dtype),
        in_specs=[pl.BlockSpec(memory_space=pltpu.VMEM)], out_specs=pl.BlockSpec(memory_space=pltpu.VMEM),
        scratch_shapes=[pltpu.SemaphoreType.DMA((7,)), pltpu.SemaphoreType.DMA((7,)), pltpu.SemaphoreType.DMA],
    )(x_shard)


class _GatherChips:
    def __init__(self, shards):
        n = len(shards)
        self.arrays, self.n_in, self.n_out = list(shards), n, n
        self.out_shape = [jax.ShapeDtypeStruct((4,) + s.shape, s.dtype) for s in shards]
        self.scratch = [pltpu.SemaphoreType.DMA((3 * n,)), pltpu.SemaphoreType.DMA((3 * n,)),
                        pltpu.SemaphoreType.DMA((n,))]

    def _copies(self, x_refs, out_refs, sems, receiving):
        send_sems, recv_sems, local_sems = sems
        x, y, c, chips = _place()
        s_me = 2 * x + y
        n = self.n_in

        def copy(a, k, s):
            return pltpu.make_async_remote_copy(
                src_ref=x_refs[a], dst_ref=out_refs[a].at[s], send_sem=send_sems.at[3 * a + k],
                recv_sem=recv_sems.at[3 * a + k], device_id=(*chips[k], c), device_id_type=MESH)

        mine = [pltpu.make_async_copy(x_refs[a], out_refs[a].at[s_me], local_sems.at[a]) for a in range(n)]
        sends = [copy(a, k, s_me) for a in range(n) for k in range(3)]
        if not receiving:
            return mine, sends
        return mine, sends, [copy(a, k, 2 * px + py) for a in range(n) for k, (px, py) in enumerate(chips)]

    def start(self, x_refs, out_refs, sems):
        mine, sends = self._copies(x_refs, out_refs, sems, False)
        for cp in mine + sends:
            cp.start()

    def wait(self, x_refs, out_refs, sems):
        mine, sends, recvs = self._copies(x_refs, out_refs, sems, True)
        for cp in recvs:
            cp.wait_recv()
        for cp in sends:
            cp.wait_send()
        for cp in mine:
            cp.wait()


class _GatherChipsHalved(_GatherChips):
    def __init__(self, shards):
        super().__init__(shards)
        n = self.n_in
        self.scratch = [pltpu.SemaphoreType.DMA((6 * n,)), pltpu.SemaphoreType.DMA((6 * n,)),
                        pltpu.SemaphoreType.DMA((n,))]

    def _copies(self, x_refs, out_refs, sems, receiving):
        send_sems, recv_sems, local_sems = sems
        x, y, c, chips = _place()
        s_me = 2 * x + y
        n = self.n_in

        def half(a, who):
            rows = x_refs[a].shape[0] // 2
            return pl.ds(who * rows, rows)

        def over_chips(a, k, s):
            return pltpu.make_async_remote_copy(
                src_ref=x_refs[a].at[half(a, c)], dst_ref=out_refs[a].at[s, half(a, c)],
                send_sem=send_sems.at[3 * a + k], recv_sem=recv_sems.at[3 * a + k],
                device_id=(*chips[k], c), device_id_type=MESH)

        def to_sibling(a, k, s, who):
            return pltpu.make_async_remote_copy(
                src_ref=out_refs[a].at[s, half(a, who)], dst_ref=out_refs[a].at[s, half(a, who)],
                send_sem=send_sems.at[3 * n + 3 * a + k], recv_sem=recv_sems.at[3 * n + 3 * a + k],
                device_id=(x, y, 1 - c), device_id_type=MESH)

        mine = [pltpu.make_async_copy(x_refs[a], out_refs[a].at[s_me], local_sems.at[a]) for a in range(n)]
        sends = [over_chips(a, k, s_me) for a in range(n) for k in range(3)]
        if not receiving:
            return mine, sends
        pairs = [(a, k, 2 * px + py) for a in range(n) for k, (px, py) in enumerate(chips)]
        landed = [over_chips(a, k, s) for a, k, s in pairs]
        passed_on = [to_sibling(a, k, s, c) for a, k, s in pairs]
        from_sibling = [to_sibling(a, k, s, 1 - c) for a, k, s in pairs]
        return mine, sends, landed, passed_on, from_sibling

    def wait(self, x_refs, out_refs, sems):
        mine, sends, landed, passed_on, from_sibling = self._copies(x_refs, out_refs, sems, True)
        for got, fwd in zip(landed, passed_on, strict=True):
            got.wait_recv()
            fwd.start()
        for cp in from_sibling:
            cp.wait_recv()
        for cp in sends + passed_on:
            cp.wait_send()
        for cp in mine:
            cp.wait()


class _ScatterChips:
    def __init__(self, parts):
        n = len(parts)
        self.arrays, self.n_in, self.n_out = list(parts), n, n
        self.out_shape = [jax.ShapeDtypeStruct((3,) + p.shape[1:], p.dtype) for p in parts]
        self.scratch = [pltpu.SemaphoreType.DMA((3 * n,)), pltpu.SemaphoreType.DMA((3 * n,))]

    def _copies(self, g_refs, out_refs, sems):
        send_sems, recv_sems = sems
        x, y, c, chips = _place()
        return [pltpu.make_async_remote_copy(
            src_ref=g_refs[a].at[2 * px + py], dst_ref=out_refs[a].at[k], send_sem=send_sems.at[3 * a + k],
            recv_sem=recv_sems.at[3 * a + k], device_id=(px, py, c), device_id_type=MESH)
            for a in range(self.n_in) for k, (px, py) in enumerate(chips)]

    def start(self, g_refs, out_refs, sems):
        for cp in self._copies(g_refs, out_refs, sems):
            cp.start()

    def wait(self, g_refs, out_refs, sems):
        sends = self._copies(g_refs, out_refs, sems)
        for cp in sends:
            cp.wait_recv()
        for cp in sends:
            cp.wait_send()


def _exchange_call(ex, *, name):
    def body(*refs):
        parts = (refs[:ex.n_in], refs[ex.n_in:ex.n_in + ex.n_out], refs[ex.n_in + ex.n_out:])
        ex.start(*parts)
        ex.wait(*parts)

    return pl.pallas_call(body, name=name, out_shape=ex.out_shape, in_specs=[ANY] * ex.n_in,
                          out_specs=[ANY] * ex.n_out, scratch_shapes=ex.scratch)(*ex.arrays)


class _SwapSibling:
    def __init__(self, vs):
        n = len(vs)
        self.arrays, self.n_in, self.n_out = list(vs), n, n
        self.out_shape = [jax.ShapeDtypeStruct(v.shape, v.dtype) for v in vs]
        self.scratch = [pltpu.SemaphoreType.DMA((n,)), pltpu.SemaphoreType.DMA((n,))]

    def _copies(self, v_refs, out_refs, sems):
        send_sems, recv_sems = sems
        x, y, c, _ = _place()
        return [pltpu.make_async_remote_copy(src_ref=v_refs[a], dst_ref=out_refs[a], send_sem=send_sems.at[a],
                                             recv_sem=recv_sems.at[a], device_id=(x, y, 1 - c), device_id_type=MESH)
                for a in range(self.n_in)]

    def start(self, v_refs, out_refs, sems):
        for cp in self._copies(v_refs, out_refs, sems):
            cp.start()

    def wait(self, v_refs, out_refs, sems):
        for cp in self._copies(v_refs, out_refs, sems):
            cp.wait()


class _Both:
    def __init__(self, first, second):
        self.parts = (first, second)
        self.arrays = first.arrays + second.arrays
        self.n_in, self.n_out = first.n_in + second.n_in, first.n_out + second.n_out
        self.out_shape = first.out_shape + second.out_shape
        self.scratch = first.scratch + second.scratch

    def _split(self, in_refs, out_refs, sems):
        a, b = self.parts
        return ((a, in_refs[:a.n_in], out_refs[:a.n_out], sems[:len(a.scratch)]),
                (b, in_refs[a.n_in:], out_refs[a.n_out:], sems[len(a.scratch):]))

    def start(self, in_refs, out_refs, sems):
        for ex, *refs in self._split(in_refs, out_refs, sems):
            ex.start(*refs)

    def wait(self, in_refs, out_refs, sems):
        for ex, *refs in self._split(in_refs, out_refs, sems):
            ex.wait(*refs)


def _sum_parts(own, others, *, name):
    R, C = own.shape
    tm = _pick(R, (256, 128, 64))

    def body(own_ref, o0_ref, o1_ref, o2_ref, out_ref):
        tot = own_ref[...].astype(F32)
        for ref in (o0_ref, o1_ref, o2_ref):
            tot = tot + ref[...].astype(F32)
        out_ref[...] = tot

    part = lambda k: pl.BlockSpec((None, tm, C), lambda i: (k, i, 0))
    return pl.pallas_call(
        body, name=name, grid=(R // tm,),
        in_specs=[pl.BlockSpec((tm, C), lambda i: (i, 0)), part(0), part(1), part(2)],
        out_specs=pl.BlockSpec((tm, C), lambda i: (i, 0)), out_shape=jax.ShapeDtypeStruct((R, C), F32),
        compiler_params=_cparams(("arbitrary",)),
    )(own, others, others, others)


def _adam_math(w_, m_, v_, g):
    m2 = ADAM_B1 * m_ + (1.0 - ADAM_B1) * g
    v2 = ADAM_B2 * v_ + (1.0 - ADAM_B2) * jnp.square(g)
    m_hat = m2 / (1.0 - ADAM_B1 ** ADAM_STEP)
    v_hat = v2 / (1.0 - ADAM_B2 ** ADAM_STEP)
    delta = -ADAM_LR * (m_hat / (jnp.sqrt(v_hat) + ADAM_EPS) + ADAM_WD * w_)
    return delta, m2, v2


SMALL_SLOTS = 24
SMALL_COLS = 2 * D


def _small_rows(widths):
    firsts, row = [], 0
    for n_i in widths:
        firsts.append(row)
        row += -(-n_i // SMALL_COLS)
    assert row <= SMALL_SLOTS
    return firsts


def _pack_small(grads, *, name):
    n = len(grads)
    firsts = _small_rows([g.shape[1] for g in grads])

    def body(*refs):
        out_ref = refs[n]
        out_ref[...] = jnp.zeros_like(out_ref)
        for first, ref in zip(firsts, refs[:n], strict=True):
            for lo in range(0, ref.shape[1], SMALL_COLS):
                width = min(SMALL_COLS, ref.shape[1] - lo)
                row = first + lo // SMALL_COLS
                out_ref[row:row + 1, 0:width] = ref[:, lo:lo + width]

    return pl.pallas_call(body, name=name, out_shape=jax.ShapeDtypeStruct((SMALL_SLOTS, SMALL_COLS), F32))(*grads)


def _adamw_small(ws, ms, vs, gathered, *, name):
    n = len(ws)
    firsts = _small_rows([w.shape[1] for w in ws] + [128])

    def total(g_ref, first, nc):
        pieces = []
        for lo in range(0, nc, SMALL_COLS):
            width, row = min(SMALL_COLS, nc - lo), first + lo // SMALL_COLS
            g = g_ref[row:row + 1, 0:width]
            for d in range(1, 8):
                g = g + g_ref[d * SMALL_SLOTS + row:d * SMALL_SLOTS + row + 1, 0:width]
            pieces.append(g)
        return pieces[0] if len(pieces) == 1 else jnp.concatenate(pieces, axis=1)

    def body(*refs):
        w_refs, m_refs, v_refs, g_ref = refs[:n], refs[n:2 * n], refs[2 * n:3 * n], refs[3 * n]
        outs = refs[3 * n + 1:]
        for i in range(n):
            g = total(g_ref, firsts[i], w_refs[i].shape[1])
            delta, m2, v2 = _adam_math(w_refs[i][...], m_refs[i][...], v_refs[i][...], g)
            for k, val in enumerate((g, delta, m2, v2)):
                outs[k * n + i][...] = val
        outs[4 * n][...] = total(g_ref, firsts[n], 128)

    shapes = [jax.ShapeDtypeStruct(w.shape, F32) for w in ws]
    res = pl.pallas_call(body, name=name, out_shape=shapes * 4 + [jax.ShapeDtypeStruct((1, 128), F32)],
                         compiler_params=pltpu.CompilerParams(vmem_limit_bytes=VMEM_LIMIT))(*ws, *ms, *vs, gathered)
    return [res[k * n:(k + 1) * n] for k in range(4)], res[4 * n]


def _adamw(w, m, v, gparts, *, tm, name, hosted=None):
    def fn(i, n, w_, m_, v_, *gs):
        g = gs[0]
        for p in gs[1:]:
            g = g + p
        return (g,) + _adam_math(w_, m_, v_, g)
    nc = w.shape[1]
    return _rowwise(fn, [w, m, v] + list(gparts), [], [(nc, F32, 'row')] * 4, tm=tm, name=name, hosted=hosted)


WEIGHTS = ['ada_w', 'ada_b', 'norm1_gain', 'norm2_gain', 'w_in', 'tshift_mu', 'decay_w0', 'decay_up', 'iclr_a0',
           'iclr_up', 'gate_up', 'k_k', 'k_a', 'r_k', 'lnx_gain', 'lnx_bias', 'q_norm_gain', 'k_norm_gain', 'attn_sinks',
           'branch_gate_b', 'w_branch_a', 'w_branch_b', 'w_out', 'ffn_w1', 'ffn_w3', 'ffn_w2']
SHARDED = [('w_in', 1), ('decay_up', 1), ('iclr_up', 1), ('gate_up', 1), ('w_branch_a', 1), ('w_branch_b', 1),
           ('w_out', 0), ('ffn_w1', 1), ('ffn_w3', 1), ('ffn_w2', 0)]
SMALL = ['ada_b', 'norm1_gain', 'norm2_gain', 'tshift_mu', 'decay_w0', 'iclr_a0', 'k_k', 'k_a', 'r_k', 'lnx_gain',
         'lnx_bias', 'q_norm_gain', 'k_norm_gain', 'attn_sinks', 'branch_gate_b']


def kernel(x, c, positions, ada_w, ada_b, norm1_gain, norm2_gain, w_in, tshift_mu, decay_w0, decay_up, iclr_a0, iclr_up, gate_up, k_k, k_a, r_k, lnx_gain, lnx_bias, q_norm_gain, k_norm_gain, attn_sinks, branch_gate_b, w_branch_a, w_branch_b, w_out, ffn_w1, ffn_w3, ffn_w2, loss_target, m_ada_w, m_ada_b, m_norm1_gain, m_norm2_gain, m_w_in, m_tshift_mu, m_decay_w0, m_decay_up, m_iclr_a0, m_iclr_up, m_gate_up, m_k_k, m_k_a, m_r_k, m_lnx_gain, m_lnx_bias, m_q_norm_gain, m_k_norm_gain, m_attn_sinks, m_branch_gate_b, m_w_branch_a, m_w_branch_b, m_w_out, m_ffn_w1, m_ffn_w3, m_ffn_w2, v_ada_w, v_ada_b, v_norm1_gain, v_norm2_gain, v_w_in, v_tshift_mu, v_decay_w0, v_decay_up, v_iclr_a0, v_iclr_up, v_gate_up, v_k_k, v_k_a, v_r_k, v_lnx_gain, v_lnx_bias, v_q_norm_gain, v_k_norm_gain, v_attn_sinks, v_branch_gate_b, v_w_branch_a, v_w_branch_b, v_w_out, v_ffn_w1, v_ffn_w3, v_ffn_w2):
    a = dict(locals())
    W = {n: a[n] for n in WEIGHTS}
    M = {n: a['m_' + n] for n in WEIGHTS}
    V = {n: a['v_' + n] for n in WEIGHTS}
    xi, yi, ci = lax.axis_index("x"), lax.axis_index("y"), lax.axis_index("c")
    me = 4 * xi + 2 * yi + ci
    shard = 2 * xi + yi
    mat = lambda t: t.reshape(t.shape[-2], t.shape[-1])
    sharded = [n for n, _ in SHARDED]

    ax = dict(SHARDED)
    late = LATE
    early = [n for n in sharded if n not in late]
    shards = {n: mat(W[n]).astype(MXU) for n in sharded}
    gathered = _exchange_call(_GatherChipsHalved([shards[n] for n in early]), name="gather_weights")
    full = {n: _full_weight(g, ax[n]) for n, g in zip(early, gathered, strict=True)}

    c_all = _all_gather8(jnp.broadcast_to(c, (8, D)), name="gather_c")[0::8]
    pad_rows = lambda t: jnp.concatenate([t, jnp.zeros((BLK - 8, t.shape[1]), t.dtype)])
    c_all = pad_rows(c_all.astype(MXU))
    ada_cols = _mm_nn(c_all, mat(ada_w).astype(MXU), name="f_ada")[:8]
    ada_all = _all_gather8(ada_cols, name="gather_ada").reshape(2, 2, 2, 8, 6 * D // 4)
    ada_mine = lax.dynamic_index_in_dim(ada_all[:, :, 0], me, axis=2, keepdims=False)
    ada = ada_mine.reshape(1, 6 * D) + mat(ada_b)

    zero = jnp.zeros((64, RW), MXU)
    lora = jnp.concatenate([jnp.concatenate([full['decay_up'], zero], axis=1),
                            jnp.concatenate([zero, full['iclr_up']], axis=1)], axis=0)
    s = {n: W[n].reshape(1, -1) for n in SMALL if n != 'ada_b'}
    s['lora_up'] = lora.astype(F32)
    s['gate_up'] = full['gate_up'].astype(F32)
    tab = _rope_table(positions.reshape(-1))
    loss, dx, d_ada, gw, gs, from_chips = _local_step(x[0], loss_target[0], ada, tab, dict(w_in=full['w_in']), s,
                                                      shards={n: shards[n] for n in late})

    gs['ada_b'] = d_ada
    gsmall = _pack_small([gs[n] for n in SMALL] + [loss], name="pack_small_grads")
    gsmall_all = _all_gather8(gsmall, name="gather_small_grads")
    row = lambda src: [src[n].reshape(1, -1) for n in SMALL]
    sm_out, loss = _adamw_small(row(W), row(M), row(V), gsmall_all, name="adamw_small")
    sm_out = [{n: o.reshape(W[n].shape) for n, o in zip(SMALL, outs_k, strict=True)} for outs_k in sm_out]
    loss = loss[0, 0]

    ada_rows = 6 * D // SMALL_COLS
    d_ada_all = gsmall_all.reshape(8, SMALL_SLOTS, SMALL_COLS)[:, :ada_rows].reshape(8, 6 * D)
    d_ada_cols = lax.dynamic_slice_in_dim(d_ada_all, shard * (6 * D // 4), 6 * D // 4, axis=1)
    g_ada_w = _mm_tn(c_all, pad_rows(d_ada_cols.astype(MXU)), name="b_ada")

    rest = [n for n in sharded if n != 'w_in']
    parts = {n: _sum_parts(lax.dynamic_index_in_dim(gw[n], shard, axis=0, keepdims=False), from_chips[n],
                           name="sum_" + n) for n in rest}
    tail = _Both(_ScatterChips([gw['w_in'][:, 3 * D // 4:]]), _SwapSibling([parts[n] for n in rest]))
    res = _adamw(mat(ada_w), mat(m_ada_w), mat(v_ada_w), [g_ada_w], tm=256, name="adamw_ada", hosted=tail)
    ada_out, last_quarter, others = res[:4], res[4], dict(zip(rest, res[5:], strict=True))
    parts['w_in'] = _sum_parts(lax.dynamic_index_in_dim(gw['w_in'], shard, axis=0, keepdims=False),
                               jnp.concatenate(from_chips['w_in'] + [last_quarter], axis=1), name="sum_w_in")
    others['w_in'] = _exchange_call(_SwapSibling([parts['w_in']]), name="swap_w_in")[0]
    sh_out = {}
    for n in sharded:
        part, other = parts[n], others[n]
        sh_out[n] = _adamw(mat(W[n]), mat(M[n]), mat(V[n]), [part, other], tm=_pick(part.shape[0], (256, 128, 64)),
                           name="adamw_" + n)

    def leaf(k, n):
        if n == 'ada_w':
            return ada_out[k].reshape(W[n].shape)
        if n in sharded:
            return sh_out[n][k].reshape(W[n].shape)
        return sm_out[k][n]
    outs = [leaf(k, n) for k in range(4) for n in WEIGHTS]
    return (loss, dx[None], *outs)
```

```python
import functools
import math

import jax
import jax.numpy as jnp
from jax import lax
from jax.experimental import pallas as pl
from jax.experimental.pallas import tpu as pltpu

F32 = jnp.float32
BF16 = jnp.bfloat16
MXU = BF16
HI = lax.Precision.HIGHEST

D = 1024
HD = 64
NH = 8
RW = NH * HD
SHIFT_W = 3 * RW + 64 + 64 + 128
QKV_W = RW + 2 * 128
GATE_W = 2 * D
IN_W = SHIFT_W + QKV_W + GATE_W
DFF = 2816
BLK = 128
CHUNK = 64
RMS_EPS = 1e-6
GN_EPS = 64e-5
NEG_INF = -1e30
ADAM_LR, ADAM_B1, ADAM_B2, ADAM_EPS, ADAM_WD, ADAM_STEP = 0.001, 0.9, 0.999, 1e-08, 0.01, 10
VMEM_LIMIT = 56 * 1024 * 1024
MESH = pl.DeviceIdType.MESH


def _cparams(sem=None):
    return pltpu.CompilerParams(dimension_semantics=sem, vmem_limit_bytes=VMEM_LIMIT)


def _full_spec(a):
    nd = a.ndim
    return pl.BlockSpec(a.shape, lambda *_: (0,) * nd)


def _rowwise(fn, rows, consts, outs, *, tm, name, halo=(), hosted=None):
    rows = [(a + (0,))[:3] if isinstance(a, tuple) else (a, a.shape[1], 0) for a in rows]
    T = rows[0][0].shape[0]
    assert T % tm == 0 and tm % 8 == 0
    n_tiles = T // tm
    n_in = len(rows) + len(halo) + len(consts)
    in_specs = [pl.BlockSpec((tm, nc), lambda i, j=j: (i, j)) for _, nc, j in rows]
    args = [a for a, _, _ in rows]
    for a, nc, kind in halo:
        if kind == 'prev':
            in_specs.append(pl.BlockSpec((8, nc), lambda i: (jnp.maximum(i * (tm // 8) - 1, 0), 0)))
        else:
            in_specs.append(pl.BlockSpec((8, nc), lambda i: (jnp.minimum((i + 1) * (tm // 8), T // 8 - 1), 0)))
        args.append(a)
    in_specs += [_full_spec(a) for a in consts]
    args += list(consts)
    out_shape, out_specs = [], []
    for ncols, dtype, kind in outs:
        if kind == 'row':
            out_shape.append(jax.ShapeDtypeStruct((T, ncols), dtype))
            out_specs.append(pl.BlockSpec((tm, ncols), lambda i: (i, 0)))
        else:
            out_shape.append(jax.ShapeDtypeStruct((kind, ncols), dtype))
            out_specs.append(pl.BlockSpec((kind, ncols), lambda i: (0, 0)))

    def body(*refs):
        i = pl.program_id(0)
        vals = [r[...] for r in refs[:n_in]]
        res = fn(i, n_tiles, *vals)
        for (ncols, dtype, kind), o_ref, val in zip(outs, refs[n_in:], res, strict=True):
            if kind == 'row':
                o_ref[...] = val.astype(dtype)
            else:
                @pl.when(i == 0)
                def _():
                    o_ref[...] = jnp.zeros_like(o_ref)
                o_ref[...] += val.astype(dtype)

    h_in, h_in_specs, h_out_specs, h_out_shape, h_scratch = _hosted_args(hosted)
    res = pl.pallas_call(
        _hosting(body, hosted, n_in, len(outs), 0, n_tiles), name=name, grid=(n_tiles,),
        in_specs=in_specs + h_in_specs, out_specs=out_specs + h_out_specs, out_shape=out_shape + h_out_shape,
        scratch_shapes=h_scratch, compiler_params=_cparams(("arbitrary",)),
    )(*args, *h_in)
    return res


def _pick(n, cands):
    for c in cands:
        if n % c == 0:
            return c
    return n


MM_ROWS = (1024, 512, 256, 128)
MM_COLS = (1536, 1408, 1024, 896, 768, 512, 256, 128)
MM_WIDE = 3000


def _mm_nn(a, w, *, name, out_dtype=F32, hosted=None):
    T, K = a.shape
    N = w.shape[1]
    tm = _pick(T, MM_ROWS)
    tn = _pick(N, MM_COLS)
    grid = (N // tn, T // tm)
    h_in, h_in_specs, h_out_specs, h_out_shape, h_scratch = _hosted_args(hosted)

    def body(a_ref, w_ref, o_ref):
        o_ref[...] = jnp.dot(a_ref[...], w_ref[...], preferred_element_type=F32).astype(out_dtype)

    res = pl.pallas_call(
        _hosting(body, hosted, 2, 1, 0, grid), name=name, grid=grid,
        in_specs=[pl.BlockSpec((tm, K), lambda j, i: (i, 0)), pl.BlockSpec((K, tn), lambda j, i: (0, j))] + h_in_specs,
        out_specs=[pl.BlockSpec((tm, tn), lambda j, i: (i, j))] + h_out_specs,
        out_shape=[jax.ShapeDtypeStruct((T, N), out_dtype)] + h_out_shape, scratch_shapes=h_scratch,
        compiler_params=_cparams(("arbitrary", "arbitrary")),
    )(a, w, *h_in)
    return res if hosted else res[0]


def _mm_nt(dy, w, *, name, out_dtype=F32, hosted=None):
    T, N = dy.shape
    K = w.shape[0]
    tm = _pick(T, MM_ROWS if N <= MM_WIDE else MM_ROWS[1:])
    tk = _pick(K, MM_COLS[1:])
    grid = (K // tk, T // tm)
    h_in, h_in_specs, h_out_specs, h_out_shape, h_scratch = _hosted_args(hosted)

    def body(dy_ref, w_ref, o_ref):
        o_ref[...] = lax.dot_general(dy_ref[...], w_ref[...], (((1,), (1,)), ((), ())),
                                     preferred_element_type=F32).astype(out_dtype)

    res = pl.pallas_call(
        _hosting(body, hosted, 2, 1, 0, grid), name=name, grid=grid,
        in_specs=[pl.BlockSpec((tm, N), lambda j, i: (i, 0)), pl.BlockSpec((tk, N), lambda j, i: (j, 0))] + h_in_specs,
        out_specs=[pl.BlockSpec((tm, tk), lambda j, i: (i, j))] + h_out_specs,
        out_shape=[jax.ShapeDtypeStruct((T, K), out_dtype)] + h_out_shape, scratch_shapes=h_scratch,
        compiler_params=_cparams(("arbitrary", "arbitrary")),
    )(dy, w, *h_in)
    return res if hosted else res[0]


def _mm_tn(a, dy, *, name, out_dtype=F32, col_shards=None):
    T, K = a.shape
    N = dy.shape[1]
    tm = _pick(T, MM_ROWS)
    tn = N // col_shards if col_shards else _pick(N, MM_COLS[1:])
    n_t = T // tm

    def body(a_ref, dy_ref, o_ref, acc_ref):
        i = pl.program_id(1)

        @pl.when(i == 0)
        def _():
            acc_ref[...] = jnp.zeros_like(acc_ref)

        acc_ref[...] += lax.dot_general(a_ref[...], dy_ref[...], (((0,), (0,)), ((), ())), preferred_element_type=F32)

        @pl.when(i == n_t - 1)
        def _():
            o_ref[...] = acc_ref[...].astype(out_dtype)

    if col_shards:
        out_specs = pl.BlockSpec((None, K, tn), lambda j, i: (j, 0, 0))
        out_shape = jax.ShapeDtypeStruct((col_shards, K, tn), out_dtype)
    else:
        out_specs = pl.BlockSpec((K, tn), lambda j, i: (0, j))
        out_shape = jax.ShapeDtypeStruct((K, N), out_dtype)
    return pl.pallas_call(
        body, name=name, grid=(N // tn, n_t),
        in_specs=[pl.BlockSpec((tm, K), lambda j, i: (i, 0)), pl.BlockSpec((tm, tn), lambda j, i: (i, j))],
        out_specs=out_specs, out_shape=out_shape, scratch_shapes=[pltpu.VMEM((K, tn), F32)],
        compiler_params=_cparams(("arbitrary", "arbitrary")),
    )(a, dy)


def _mm_then(products, fn, rows, consts, outs, *, tm, name, lhs_fn=None, hosted=None):
    T = (rows[0] if lhs_fn else products[0][0]).shape[0]
    n_tiles = T // tm
    in_specs, args = [], []
    for a, w, _ in products:
        if a is not None:
            in_specs.append(pl.BlockSpec((tm, a.shape[1]), lambda i: (i, 0)))
            args.append(a)
        in_specs.append(_full_spec(w))
        args.append(w)
    n_w = len(args)
    in_specs += [pl.BlockSpec((tm, a.shape[1]), lambda i: (i, 0)) for a in rows] + [_full_spec(c_) for c_ in consts]
    args += list(rows) + list(consts)
    n_in = len(args)
    out_shape, out_specs = [], []
    for ncols, dtype, kind in outs:
        if kind == 'row':
            out_shape.append(jax.ShapeDtypeStruct((T, ncols), dtype))
            out_specs.append(pl.BlockSpec((tm, ncols), lambda i: (i, 0)))
        else:
            out_shape.append(jax.ShapeDtypeStruct((kind, ncols), dtype))
            out_specs.append(pl.BlockSpec((kind, ncols), lambda i: (0, 0)))

    def body(*refs):
        i = pl.program_id(0)
        tiles = [r[...] for r in refs[n_w:n_in]]
        made = []
        if lhs_fn:
            made = lhs_fn(*tiles)
            made = list(made) if isinstance(made, tuple) else [made]
            made[0] = made[0].astype(MXU)
        y, pos = None, 0
        for a, _, form in products:
            if a is None:
                lhs = made[0]
            else:
                lhs, pos = refs[pos][...], pos + 1
            dims = (((1,), (0,)), ((), ())) if form == 'nn' else (((1,), (1,)), ((), ()))
            t = lax.dot_general(lhs, refs[pos][...], dims, preferred_element_type=F32)
            pos += 1
            y = t if y is None else y + t
        res = fn(i, n_tiles, y, *made, *tiles)
        for (ncols, dtype, kind), o_ref, val in zip(outs, refs[n_in:], res, strict=True):
            if kind == 'row':
                o_ref[...] = val.astype(dtype)
            else:
                @pl.when(i == 0)
                def _():
                    o_ref[...] = jnp.zeros_like(o_ref)
                o_ref[...] += val.astype(dtype)

    h_in, h_in_specs, h_out_specs, h_out_shape, h_scratch = _hosted_args(hosted)
    return pl.pallas_call(
        _hosting(body, hosted, n_in, len(outs), 0, n_tiles), name=name, grid=(n_tiles,),
        in_specs=in_specs + h_in_specs, out_specs=out_specs + h_out_specs, out_shape=out_shape + h_out_shape,
        scratch_shapes=h_scratch, compiler_params=_cparams(("arbitrary",)))(*args, *h_in)


def _seg_ones(n):
    r = lax.broadcasted_iota(jnp.int32, (n, n), 0) // HD
    c = lax.broadcasted_iota(jnp.int32, (n, n), 1) // HD
    return (r == c).astype(F32)


def _segsum_raw(x):
    ones = _seg_ones(x.shape[1])
    if MXU == F32:
        return jnp.dot(x, ones, precision=HI, preferred_element_type=F32)
    hi = x.astype(MXU)
    lo = (x - hi.astype(F32)).astype(MXU)
    ones = ones.astype(MXU)
    return jnp.dot(hi, ones, preferred_element_type=F32) + jnp.dot(lo, ones, preferred_element_type=F32)


@jax.custom_vjp
def _segsum(x):
    return _segsum_raw(x)


def _segsum_fwd(x):
    return _segsum_raw(x), None


def _segsum_bwd(_, g):
    return (_segsum_raw(g),)


_segsum.defvjp(_segsum_fwd, _segsum_bwd)


def _mxu(x):
    return x.astype(MXU)


@jax.custom_vjp
def _bdot(a, b):
    return jnp.dot(_mxu(a), _mxu(b), preferred_element_type=F32)


def _bdot_fwd(a, b):
    return _bdot(a, b), (a, b)


def _bdot_bwd(res, g):
    a, b = res
    da = lax.dot_general(_mxu(g), _mxu(b), (((1,), (1,)), ((), ())), preferred_element_type=F32)
    db = lax.dot_general(_mxu(a), _mxu(g), (((0,), (0,)), ((), ())), preferred_element_type=F32)
    return da.astype(a.dtype), db.astype(b.dtype)


_bdot.defvjp(_bdot_fwd, _bdot_bwd)


def _sigmoid(x):
    return 1.0 / (1.0 + jnp.exp(-x))


def _softplus(x):
    return jnp.maximum(x, 0.0) + jnp.log(1.0 + jnp.exp(jnp.minimum(x, -x)))


def _norm_mod(x, gain, scale, shift):
    inv = lax.rsqrt(jnp.mean(x * x, axis=-1, keepdims=True) + RMS_EPS)
    return (x * inv) * gain * (1.0 + scale) + shift


def _prep(mixed, decay_w0, lora_up, iclr_a0, gate_up, k_k, k_a):
    r = mixed[:, 0:RW]
    k = mixed[:, RW:2 * RW]
    v = mixed[:, 2 * RW:3 * RW]
    z = mixed[:, 3 * RW:3 * RW + 128]
    xg = mixed[:, 3 * RW + 128:]
    lane = lax.broadcasted_iota(jnp.int32, z.shape, 1)
    tz = jnp.where(lane < 64, jnp.tanh(z), z)
    lo = _bdot(tz, lora_up)
    w_log = -_softplus(-(decay_w0 + lo[:, :RW])) - 0.5
    lw = -jnp.exp(w_log)
    a_ic = _sigmoid(iclr_a0 + lo[:, RW:])
    g = _bdot(_sigmoid(xg), gate_up)
    kk = k * k_k
    kk = kk / jnp.maximum(jnp.sqrt(_segsum(kk * kk)), 1e-12)
    k_mod = k * (1.0 + (a_ic - 1.0) * k_a)
    return jnp.concatenate([r, lw, k_mod, v, -kk, kk * a_ic, g], axis=1)


def _post(y, r, k, v, g, lnx_gain, lnx_bias, r_k):
    mu = _segsum(y) * (1.0 / HD)
    yc = y - mu
    var = _segsum(yc * yc) * (1.0 / HD)
    yn = yc * lax.rsqrt(var + GN_EPS) * lnx_gain + lnx_bias
    bonus = _segsum(r * k * r_k) * v
    return (yn + bonus) * g


def _merge(pg, ma, mb, bias):
    gates = _sigmoid(pg + bias)
    return gates[:, :D] * ma + gates[:, D:] * mb


def _swiglu(u, v):
    return u * _sigmoid(u) * v


def _ffn_in(h, w1, w3, *, name):
    T, K = h.shape
    F = w1.shape[1]
    tm = _pick(T, MM_ROWS)
    tn = _pick(F, MM_COLS[1:])

    def body(h_ref, w1_ref, w3_ref, u_ref, v_ref, a_ref):
        u = jnp.dot(h_ref[...], w1_ref[...], preferred_element_type=F32).astype(MXU)
        v = jnp.dot(h_ref[...], w3_ref[...], preferred_element_type=F32).astype(MXU)
        u_ref[...] = u
        v_ref[...] = v
        a_ref[...] = _swiglu(u.astype(F32), v.astype(F32)).astype(MXU)

    wspec = pl.BlockSpec((K, tn), lambda j, i: (0, j))
    ospec = pl.BlockSpec((tm, tn), lambda j, i: (i, j))
    return pl.pallas_call(
        body, name=name, grid=(F // tn, T // tm),
        in_specs=[pl.BlockSpec((tm, K), lambda j, i: (i, 0)), wspec, wspec],
        out_specs=[ospec] * 3, out_shape=[jax.ShapeDtypeStruct((T, F), MXU)] * 3,
        compiler_params=_cparams(("arbitrary", "arbitrary")),
    )(h, w1, w3)


def _ffn_act_bwd(dff, w2, u, v, *, name):
    T, N = dff.shape
    F = w2.shape[0]
    tm = _pick(T, MM_ROWS)
    tk = _pick(F, MM_COLS[1:])

    def body(dy_ref, w_ref, u_ref, v_ref, du_ref, dv_ref):
        dact = lax.dot_general(dy_ref[...], w_ref[...], (((1,), (1,)), ((), ())), preferred_element_type=F32)
        _, vjp = jax.vjp(_swiglu, u_ref[...].astype(F32), v_ref[...].astype(F32))
        du, dv = vjp(dact)
        du_ref[...] = du.astype(MXU)
        dv_ref[...] = dv.astype(MXU)

    tile = pl.BlockSpec((tm, tk), lambda j, i: (i, j))
    return pl.pallas_call(
        body, name=name, grid=(F // tk, T // tm),
        in_specs=[pl.BlockSpec((tm, N), lambda j, i: (i, 0)), pl.BlockSpec((tk, N), lambda j, i: (j, 0)), tile, tile],
        out_specs=[tile, tile], out_shape=[jax.ShapeDtypeStruct((T, F), MXU)] * 2,
        compiler_params=_cparams(("arbitrary", "arbitrary")),
    )(dff, w2, u, v)


@functools.partial(jax.custom_vjp, nondiff_argnums=(1,))
def _lane_roll(x, s):
    return pltpu.roll(x, s, 1)


def _lane_roll_fwd(x, s):
    return pltpu.roll(x, s, 1), None


def _lane_roll_bwd(s, _, g):
    n = g.shape[1]
    return (pltpu.roll(g, (n - s) % n, 1),)


_lane_roll.defvjp(_lane_roll_fwd, _lane_roll_bwd)


def _rope(x, cos, sin_lo, sin_hi):
    n = x.shape[1]
    return x * cos + _lane_roll(x, n - 8) * sin_lo + _lane_roll(x, 8) * sin_hi


def _head_rms(x, gain):
    return x * lax.rsqrt(_segsum(x * x) * (1.0 / HD) + RMS_EPS) * gain


def _attn_blocks(qkv_c, qkv_p, tab_c, tab_p, qg, kg, sinks, first):
    nb = qkv_c.shape[0] // BLK
    G = 4

    def tabs(tab, n):
        return [jnp.tile(tab[:, j * 128:(j + 1) * 128], (1, n // 128)) for j in range(3)]

    qg = jnp.concatenate([qg] * NH, axis=1)
    kg = jnp.concatenate([kg] * 2, axis=1)
    q = _rope(_head_rms(qkv_c[:, :RW], qg), *tabs(tab_c, RW))
    k_in = jnp.concatenate([qkv_p[:, RW:RW + 128], qkv_c[:, RW:RW + 128]], axis=0)
    k = _rope(_head_rms(k_in, kg), *tabs(jnp.concatenate([tab_p, tab_c], axis=0), 128))
    v = jnp.concatenate([qkv_p[:, RW + 128:], qkv_c[:, RW + 128:]], axis=0)

    pile = lambda xs: jnp.concatenate([x_[None] for x_ in xs], axis=0)

    def bands(t):
        return pile([t[b * BLK:(b + 2) * BLK, kvh * HD:(kvh + 1) * HD] for kvh in range(2) for b in range(nb)])

    qs = pile([jnp.concatenate([q[b * BLK:(b + 1) * BLK, (G * kvh + g) * HD:(G * kvh + g + 1) * HD]
                                for g in range(G)], axis=0) for kvh in range(2) for b in range(nb)])
    s = _bmm(qs, bands(k), 2, 2, 1) * (HD ** -0.5)
    qi = lax.broadcasted_iota(jnp.int32, (G * BLK, 2 * BLK), 0) % BLK
    kj = lax.broadcasted_iota(jnp.int32, (G * BLK, 2 * BLK), 1)
    dist = qi + BLK - kj
    in_band = (dist >= 0) & (dist < BLK)
    pair = lax.broadcasted_iota(jnp.int32, (2 * nb, 1, 1), 0)
    no_prev = (pair % nb == 0) & first
    valid = in_band[None] & (jnp.logical_not(no_prev) | (kj >= BLK)[None])
    s = jnp.where(valid, s, NEG_INF)
    row_g = lax.broadcasted_iota(jnp.int32, (G * BLK, 1), 0) // BLK
    sink = []
    for kvh in range(2):
        col = jnp.zeros((G * BLK, 1), F32)
        for g in range(G):
            col = jnp.where(row_g == g, sinks[:, G * kvh + g:G * kvh + g + 1], col)
        sink += [col] * nb
    sink = pile(sink)
    m = lax.stop_gradient(jnp.maximum(jnp.max(s, axis=-1, keepdims=True), sink))
    e = jnp.exp(s - m)
    p = e * (1.0 / (jnp.sum(e, axis=-1, keepdims=True) + jnp.exp(sink - m)))
    o = _bmm(p, bands(v), 2, 1, 1)
    return jnp.concatenate([jnp.concatenate([o[kvh * nb + b, g * BLK:(g + 1) * BLK] for kvh in range(2)
                                             for g in range(G)], axis=1) for b in range(nb)], axis=0)


def _heads(x):
    return jnp.stack([x[:, h * HD:(h + 1) * HD] for h in range(NH)], axis=0)


def _unheads(x):
    return jnp.concatenate([x[h] for h in range(NH)], axis=1)


def _split(x, n):
    parts, rest = [], x
    for _ in range(n):
        p = rest.astype(MXU)
        parts.append(p)
        rest = rest - p.astype(F32)
    return parts


def _bdot_batched(a, b, ca, cb):
    return lax.dot_general(a, b, (((ca,), (cb,)), ((0,), (0,))), preferred_element_type=F32)


def _bmm_passes(a, b, ca, cb, passes):
    if MXU == F32:
        return lax.dot_general(a, b, (((ca,), (cb,)), ((0,), (0,))), precision=HI, preferred_element_type=F32)
    if passes == 1:
        return _bdot_batched(a.astype(MXU), b.astype(MXU), ca, cb)
    (a0, a1), (b0, b1) = _split(a, 2), _split(b, 2)
    return _bdot_batched(a0, b0, ca, cb) + (_bdot_batched(a0, b1, ca, cb) + _bdot_batched(a1, b0, ca, cb))


@functools.partial(jax.custom_vjp, nondiff_argnums=(2, 3, 4))
def _bmm(a, b, ca, cb, passes=1):
    return _bmm_passes(a, b, ca, cb, passes)


def _bmm_fwd(a, b, ca, cb, passes):
    return _bmm_passes(a, b, ca, cb, passes), (a, b)


def _bmm_bwd(ca, cb, passes, res, g):
    a, b = res
    if (ca, cb) == (2, 1):
        return _bmm_passes(g, b, 2, 2, passes), _bmm_passes(a, g, 1, 1, passes)
    if (ca, cb) == (2, 2):
        return _bmm_passes(g, b, 2, 1, passes), _bmm_passes(g, a, 1, 1, passes)
    return _bmm_passes(b, g, 2, 2, passes), _bmm_passes(a, g, 2, 1, passes)


_bmm.defvjp(_bmm_fwd, _bmm_bwd)


def _tri_dot(x, transpose):
    C = x.shape[1]
    ri = lax.broadcasted_iota(jnp.int32, (C, C), 0)
    ci = lax.broadcasted_iota(jnp.int32, (C, C), 1)
    tri = jnp.broadcast_to(((ri <= ci) if transpose else (ri >= ci)).astype(MXU), (x.shape[0], C, C))
    if MXU == F32:
        return lax.dot_general(tri, x, (((2,), (1,)), ((0,), (0,))), precision=HI, preferred_element_type=F32)
    p0, p1, p2 = _split(x, 3)
    return _bdot_batched(tri, p0, 2, 1) + (_bdot_batched(tri, p1, 2, 1) + _bdot_batched(tri, p2, 2, 1))


@jax.custom_vjp
def _cumsum_rows(x):
    return _tri_dot(x, False)


def _cumsum_rows_fwd(x):
    return _tri_dot(x, False), None


def _cumsum_rows_bwd(_, g):
    return (_tri_dot(g, True),)


_cumsum_rows.defvjp(_cumsum_rows_fwd, _cumsum_rows_bwd)

P_SCORE = 1
P_SOLVE = 1
P_STATE = 1
SCAN_CHUNKS = (4, 2, 1)


def _neumann(l):
    C = l.shape[1]
    eye = (lax.broadcasted_iota(jnp.int32, (C, C), 0) == lax.broadcasted_iota(jnp.int32, (C, C), 1)).astype(F32)
    x, lp = eye + l, l
    for _ in range(int(math.log2(C)) - 1):
        lp = _bmm(lp, lp, 2, 1, P_SOLVE)
        x = x + _bmm(x, lp, 2, 1, P_SOLVE)
    return x


@jax.custom_vjp
def _unit_lower_inverse(l):
    return _neumann(l)


def _unit_lower_inverse_fwd(l):
    x = _neumann(l)
    return x, x


def _unit_lower_inverse_bwd(x, g):
    return (_bmm(_bmm(x, g, 1, 1, P_SOLVE), x, 2, 2, P_SOLVE),)


_unit_lower_inverse.defvjp(_unit_lower_inverse_fwd, _unit_lower_inverse_bwd)


def _known_inverse(x):
    @jax.custom_vjp
    def f(l):
        return x

    f.defvjp(lambda l: (x, None), lambda _, g: (_bmm(_bmm(x, g, 1, 1, P_SOLVE), x, 2, 2, P_SOLVE),))
    return f


def _chunk(S0, r, lw, k, v, a, b, inverse=None):
    C = CHUNK
    n = r.shape[1] // C
    fold = lambda t: t.reshape(NH * n, C, HD)
    r, lw, k, v, a, b = (fold(t) for t in (r, lw, k, v, a, b))
    ri = lax.broadcasted_iota(jnp.int32, (C, C), 0)
    ci = lax.broadcasted_iota(jnp.int32, (C, C), 1)
    strict = (ri > ci)
    cum = _cumsum_rows(lw)
    p_in = jnp.exp(cum)
    p_ex = jnp.exp(cum - lw)
    p_inv = jnp.exp(-cum)
    at, rt, bt, kt = a * p_ex, r * p_in, b * p_inv, k * p_inv
    lhs = jnp.concatenate([at, rt], axis=1)
    rhs_ = jnp.concatenate([bt, kt], axis=1)
    sc = _bmm(lhs, rhs_, 2, 2, P_SCORE)
    a_ab = jnp.where(strict, sc[:, :C, :C], 0.0)
    a_ak = jnp.where(strict, sc[:, :C, C:], 0.0)
    incl2 = (lax.broadcasted_iota(jnp.int32, (C, 2 * C), 0) >= lax.broadcasted_iota(jnp.int32, (C, 2 * C), 1) % C)
    a_r = jnp.where(incl2, sc[:, C:, :], 0.0)
    av = _bmm(a_ak, v, 2, 1, P_SCORE)
    x = x_all = (_unit_lower_inverse if inverse is None else _known_inverse(inverse))(a_ab)
    p_last = jnp.exp(cum[:, C - 1:C, :])
    per_chunk = lambda t: t.reshape((NH, n) + t.shape[1:])
    lhs, rhs_, a_r, av, x, v, p_last = (per_chunk(t) for t in (lhs, rhs_, a_r, av, x, v, p_last))
    S, ys = S0, []
    for c in range(n):
        s0 = _bmm(lhs[:, c], S, 2, 2, P_STATE)
        u = _bmm(x[:, c], s0[:, :C] + av[:, c], 2, 1, P_SOLVE)
        uv = jnp.concatenate([u, v[:, c]], axis=1)
        ys.append(s0[:, C:] + _bmm(a_r[:, c], uv, 2, 1, P_SCORE))
        S = (S + _bmm(uv, rhs_[:, c], 1, 1, P_STATE)) * p_last[:, c]
    return jnp.concatenate(ys, axis=1), S, x_all


def _hosting(body, ex, n_in, n_out, n_scratch, n_steps):
    if ex is None:
        return body

    def wrapped(*refs):
        a = n_in
        b = a + ex.n_in
        c = b + n_out
        d = c + ex.n_out
        e = d + n_scratch
        ex_refs = (refs[a:b], refs[c:d], refs[e:])
        grid = n_steps if isinstance(n_steps, tuple) else (n_steps,)
        first = last = True
        for ax_, size in enumerate(grid):
            first = first & (pl.program_id(ax_) == 0)
            last = last & (pl.program_id(ax_) == size - 1)

        @pl.when(first)
        def _():
            ex.start(*ex_refs)

        body(*refs[:a], *refs[b:c], *refs[d:e])

        @pl.when(last)
        def _():
            ex.wait(*ex_refs)

    return wrapped


def _hosted_args(ex):
    if ex is None:
        return [], [], [], [], []
    any_spec = pl.BlockSpec(memory_space=pl.ANY)
    return list(ex.arrays), [any_spec] * ex.n_in, [any_spec] * ex.n_out, list(ex.out_shape), list(ex.scratch)


def _scan_fwd(rw, *, name, hosted=None):
    T = rw.shape[0]
    nc = _pick(T // CHUNK, SCAN_CHUNKS)
    rows = CHUNK * nc
    n = T // rows
    h_in, h_in_specs, h_out_specs, h_out_shape, h_scratch = _hosted_args(hosted)

    def body(r_ref, lw_ref, k_ref, v_ref, a_ref, b_ref, y_ref, ck_ref, inv_ref, s_ref):
        @pl.when(pl.program_id(0) == 0)
        def _():
            s_ref[...] = jnp.zeros_like(s_ref)

        S0 = s_ref[...]
        ck_ref[0] = S0
        y, S1, inv = _chunk(S0, *[_heads(ref[...]) for ref in (r_ref, lw_ref, k_ref, v_ref, a_ref, b_ref)])
        y_ref[...] = _unheads(y)
        inv_ref[0] = inv
        s_ref[...] = S1

    col = lambda j: pl.BlockSpec((rows, RW), lambda i: (i, j))
    return pl.pallas_call(
        _hosting(body, hosted, 6, 3, 1, n), name=name, grid=(n,),
        in_specs=[col(j) for j in range(6)] + h_in_specs,
        out_specs=[pl.BlockSpec((rows, RW), lambda i: (i, 0)),
                   pl.BlockSpec((1, NH, HD, HD), lambda i: (i, 0, 0, 0)),
                   pl.BlockSpec((1, NH * nc, CHUNK, CHUNK), lambda i: (i, 0, 0, 0))] + h_out_specs,
        out_shape=[jax.ShapeDtypeStruct((T, RW), F32), jax.ShapeDtypeStruct((n, NH, HD, HD), F32),
                   jax.ShapeDtypeStruct((n, NH * nc, CHUNK, CHUNK), F32)] + h_out_shape,
        scratch_shapes=[pltpu.VMEM((NH, HD, HD), F32)] + h_scratch,
        compiler_params=_cparams(("arbitrary",)),
    )(rw, rw, rw, rw, rw, rw, *h_in)


def _scan_bwd(rw, ck, inv, dy, *, name, hosted=None):
    T = rw.shape[0]
    nc = _pick(T // CHUNK, SCAN_CHUNKS)
    rows = CHUNK * nc
    n = T // rows

    def body(r_ref, lw_ref, k_ref, v_ref, a_ref, b_ref, ck_ref, inv_ref, dy_ref, o_ref, ds_ref):
        @pl.when(pl.program_id(0) == 0)
        def _():
            ds_ref[...] = jnp.zeros_like(ds_ref)

        prim = [_heads(ref[...]) for ref in (r_ref, lw_ref, k_ref, v_ref, a_ref, b_ref)]
        known = inv_ref[0]
        _, vjp = jax.vjp(lambda S0, *p: _chunk(S0, *p, inverse=known)[:2], ck_ref[0], *prim)
        grads = vjp((_heads(dy_ref[...]), ds_ref[...]))
        ds_ref[...] = grads[0]
        o_ref[...] = jnp.concatenate([_unheads(g) for g in grads[1:]], axis=1).astype(o_ref.dtype)

    h_in, h_in_specs, h_out_specs, h_out_shape, h_scratch = _hosted_args(hosted)
    col = lambda j: pl.BlockSpec((rows, RW), lambda i: (n - 1 - i, j))
    return pl.pallas_call(
        _hosting(body, hosted, 9, 1, 1, n), name=name, grid=(n,),
        in_specs=[col(j) for j in range(6)] + [pl.BlockSpec((1, NH, HD, HD), lambda i: (n - 1 - i, 0, 0, 0)),
                                               pl.BlockSpec((1, NH * nc, CHUNK, CHUNK), lambda i: (n - 1 - i, 0, 0, 0)),
                                               pl.BlockSpec((rows, RW), lambda i: (n - 1 - i, 0))] + h_in_specs,
        out_specs=[pl.BlockSpec((rows, 6 * RW), lambda i: (n - 1 - i, 0))] + h_out_specs,
        out_shape=[jax.ShapeDtypeStruct((T, 6 * RW), MXU)] + h_out_shape,
        scratch_shapes=[pltpu.VMEM((NH, HD, HD), F32)] + h_scratch,
        compiler_params=_cparams(("arbitrary",)),
    )(rw, rw, rw, rw, rw, rw, ck, inv, dy, *h_in)


ATTN_BLOCKS = (4, 2, 1)

def _attn_fwd(qkv, tab, qg, kg, sinks, *, name, hosted=None):
    T = qkv.shape[0]
    nb = _pick(T // BLK, ATTN_BLOCKS)
    n = T // (BLK * nb)
    h_in, h_in_specs, h_out_specs, h_out_shape, h_scratch = _hosted_args(hosted)

    def body(c_ref, p_ref, tc_ref, tp_ref, qg_ref, kg_ref, s_ref, o_ref):
        o_ref[...] = _attn_blocks(c_ref[...], p_ref[...], tc_ref[...], tp_ref[...], qg_ref[...], kg_ref[...],
                                  s_ref[...], pl.program_id(0) == 0).astype(o_ref.dtype)

    cur = lambda w: pl.BlockSpec((nb * BLK, w), lambda i: (i, 0))
    prev = lambda w: pl.BlockSpec((BLK, w), lambda i: (jnp.maximum(i * nb - 1, 0), 0))
    return pl.pallas_call(
        _hosting(body, hosted, 7, 1, 0, n), name=name, grid=(n,),
        in_specs=[cur(QKV_W), prev(QKV_W), cur(3 * 128), prev(3 * 128), _full_spec(qg), _full_spec(kg),
                  _full_spec(sinks)] + h_in_specs,
        out_specs=[cur(RW)] + h_out_specs, out_shape=[jax.ShapeDtypeStruct((T, RW), MXU)] + h_out_shape,
        scratch_shapes=h_scratch,
        compiler_params=_cparams(("arbitrary",)),
    )(qkv, qkv, tab, tab, qg, kg, sinks, *h_in)


def _attn_bwd(qkv, tab, qg, kg, sinks, dy, *, name, hosted=None):
    T = qkv.shape[0]
    nb = _pick(T // BLK, ATTN_BLOCKS)
    n = T // (BLK * nb)
    h_in, h_in_specs, h_out_specs, h_out_shape, h_scratch = _hosted_args(hosted)

    def body(c_ref, p_ref, tc_ref, tp_ref, qg_ref, kg_ref, s_ref, dy_ref, dqkv_ref, dqg_ref, dkg_ref, ds_ref, carry_ref):
        i = pl.program_id(0)

        @pl.when(i == 0)
        def _():
            carry_ref[...] = jnp.zeros_like(carry_ref)
            dqg_ref[...] = jnp.zeros_like(dqg_ref)
            dkg_ref[...] = jnp.zeros_like(dkg_ref)
            ds_ref[...] = jnp.zeros_like(ds_ref)

        tc, tp = tc_ref[...], tp_ref[...]
        f = lambda c, p_, qg_, kg_, sk: _attn_blocks(c, p_, tc, tp, qg_, kg_, sk, i == n - 1)
        _, vjp = jax.vjp(f, c_ref[...], p_ref[...], qg_ref[...], kg_ref[...], s_ref[...])
        dc, dp, dqg, dkg, dsk = vjp(dy_ref[...].astype(F32))
        last = slice((nb - 1) * BLK, nb * BLK)
        dqkv_ref[...] = dc.astype(dqkv_ref.dtype)
        dqkv_ref[last, :] = (dc[last] + carry_ref[...]).astype(dqkv_ref.dtype)
        carry_ref[...] = dp
        dqg_ref[...] += dqg
        dkg_ref[...] += dkg
        ds_ref[...] += dsk

    cur = lambda w: pl.BlockSpec((nb * BLK, w), lambda i: (n - 1 - i, 0))
    prev = lambda w: pl.BlockSpec((BLK, w), lambda i: (jnp.maximum((n - 1 - i) * nb - 1, 0), 0))
    return pl.pallas_call(
        _hosting(body, hosted, 8, 4, 1, n), name=name, grid=(n,),
        in_specs=[cur(QKV_W), prev(QKV_W), cur(3 * 128), prev(3 * 128), _full_spec(qg), _full_spec(kg), _full_spec(sinks),
                  cur(RW)] + h_in_specs,
        out_specs=[cur(QKV_W), _full_spec(qg), _full_spec(kg), _full_spec(sinks)] + h_out_specs,
        out_shape=[jax.ShapeDtypeStruct((T, QKV_W), MXU), jax.ShapeDtypeStruct(qg.shape, F32),
                   jax.ShapeDtypeStruct(kg.shape, F32), jax.ShapeDtypeStruct(sinks.shape, F32)] + h_out_shape,
        scratch_shapes=[pltpu.VMEM((BLK, QKV_W), F32)] + h_scratch,
        compiler_params=_cparams(("arbitrary",)),
    )(qkv, qkv, tab, tab, qg, kg, sinks, dy, *h_in)


def _shift_down(cur, prev8, i):
    rolled = pltpu.roll(cur, 1, 0)
    first_row = jnp.where(i > 0, prev8[7:8, :], 0.0)
    row = lax.broadcasted_iota(jnp.int32, cur.shape, 0)
    return jnp.where(row == 0, first_row, rolled)


def _shift_up(cur, next8, i, n):
    tm = cur.shape[0]
    rolled = pltpu.roll(cur, tm - 1, 0)
    last_row = jnp.where(i < n - 1, next8[0:1, :], 0.0)
    row = lax.broadcasted_iota(jnp.int32, cur.shape, 0)
    return jnp.where(row == tm - 1, last_row, rolled)


def _rope_table(positions):
    half = HD // 8
    inv_freq = 500000.0 ** (-jnp.arange(half, dtype=F32) / half)
    lane = jnp.arange(128) % HD
    rotary = lane < 2 * half
    freq = jnp.where(rotary, inv_freq[lane % half], 0.0)
    ang = positions.astype(F32)[:, None] * freq[None, :]
    cos, sin = jnp.cos(ang), jnp.sin(ang)
    return jnp.concatenate([jnp.where(rotary, cos, 1.0), jnp.where(lane < half, -sin, 0.0),
                            jnp.where(rotary & (lane >= half), sin, 0.0)], axis=1)


GATHER_BEHIND = {"f_proj_shift": [('ffn_w3', 512, 768)], "f_proj_gates": [('ffn_w3', 768, 1024)],
                 "f_prep": [('ffn_w2', 352, 704)],
                 "f_scan": [('ffn_w1', 0, 1024), ('w_branch_a', 0, 512), ('w_branch_b', 0, 512)],
                 "f_post": [('ffn_w3', 0, 512)], "f_attn": [('ffn_w2', 0, 352), ('w_out', 0, 256)]}
LATE = ['w_out', 'w_branch_a', 'w_branch_b', 'ffn_w1', 'ffn_w3', 'ffn_w2']
BACK_ATTN = ['w_out', 'w_branch_a', 'w_branch_b', 'ffn_w2']
BACK_SCAN = ['ffn_w1', 'ffn_w3']
BACK_LAST = ['w_in', 'decay_up', 'iclr_up', 'gate_up']


def _full_weight(g, ax):
    return g.reshape(-1, g.shape[2]) if ax == 0 else jnp.concatenate([g[j] for j in range(4)], axis=1)


def _local_step(x, target, ada, tab, w, s, shards=None):
    T = x.shape[0]
    tm = _pick(T, (512, 256, 128))
    tm_wide = _pick(T, (256, 128))
    tm_vjp = _pick(T, (256, 128))
    row = lambda n, dt=F32: (n, dt, 'row')
    acc = lambda n, r=1: (n, F32, r)

    def f_norm1(x_, g, ada_):
        return _norm_mod(x_, g, ada_[:, D:2 * D], ada_[:, 0:D])

    landed = {}

    def behind(kernel_name):
        if not shards:
            return None
        return _GatherChips([shards[n] if hi - lo == shards[n].shape[0] else shards[n][lo:hi]
                             for n, lo, hi in GATHER_BEHIND[kernel_name]])

    def took(kernel_name, got):
        landed.update(zip(GATHER_BEHIND[kernel_name], got))

    def mm_behind(a_, w_, kernel_name, **kw):
        ex = behind(kernel_name)
        res = _mm_nn(a_, w_, name=kernel_name, hosted=ex, **kw)
        if ex:
            took(kernel_name, res[1:])
            return res[0]
        return res

    h1, proj, *got = _mm_then([(None, w['w_in'][:, :SHIFT_W], 'nn')], lambda i, n, y, h, *_: (h, y), [x],
                              [s['norm1_gain'], ada], [row(D, MXU), row(SHIFT_W)], tm=tm, name="f_proj_shift",
                              lhs_fn=f_norm1, hosted=behind("f_proj_shift"))
    took("f_proj_shift", got)
    proj_qkv = _mm_nn(h1, w['w_in'][:, SHIFT_W:SHIFT_W + QKV_W], name="f_proj_qkv")
    proj_g = mm_behind(h1, w['w_in'][:, SHIFT_W + QKV_W:], "f_proj_gates", out_dtype=MXU)
    prep_consts = [s['decay_w0'], s['lora_up'], s['iclr_a0'], s['gate_up'], s['k_k'], s['k_a']]

    def f_prep(i, n, cur, prev8, mu, *params):
        mixed = cur + (_shift_down(cur, prev8, i) - cur) * mu
        return (_prep(mixed, *params),)
    rw, *got = _rowwise(f_prep, [(proj, SHIFT_W)], [s['tshift_mu']] + prep_consts, [row(7 * RW)], tm=tm_wide,
                        name="f_prep", halo=[(proj, SHIFT_W, 'prev')], hosted=behind("f_prep"))
    took("f_prep", got)
    y, ck, inv, *got = _scan_fwd(rw, name="f_scan", hosted=behind("f_scan"))
    took("f_scan", got)
    post_consts = [s['lnx_gain'], s['lnx_bias'], s['r_k']]

    rkvg = [(rw, RW, j) for j in (0, 2, 3, 6)]

    def f_post(i, n, *args):
        return (_post(*args),)
    ya, *got = _rowwise(f_post, [y] + rkvg, post_consts, [row(RW, MXU)], tm=tm_wide, name="f_post",
                        hosted=behind("f_post"))
    took("f_post", got)
    yb, *got = _attn_fwd(proj_qkv, tab, s['q_norm_gain'], s['k_norm_gain'], s['attn_sinks'], name="f_attn",
                         hosted=behind("f_attn"))
    took("f_attn", got)
    w = dict(w)
    if shards:
        ax = dict(SHARDED)
        for n in LATE:
            rows = [landed[key] for key in sorted(k_ for k_ in landed if k_[0] == n)]
            w[n] = _full_weight(rows[0] if len(rows) == 1 else jnp.concatenate(rows, axis=1), ax[n])
    def f_merge(pg, ya_, yb_, x_, wa, wb, bias, g, ada_):
        ma_ = jnp.dot(ya_, wa, preferred_element_type=F32).astype(MXU)
        mb_ = jnp.dot(yb_, wb, preferred_element_type=F32).astype(MXU)
        return _merge(pg.astype(F32), ma_.astype(F32), mb_.astype(F32), bias), ma_, mb_

    def f_res1(i, n, mo_, merged_, ma_, mb_, pg, ya_, yb_, x_, wa, wb, bias, g, ada_):
        x1_ = x_ + ada_[:, 2 * D:3 * D] * mo_
        return merged_, ma_, mb_, mo_, x1_, _norm_mod(x1_, g, ada_[:, 4 * D:5 * D], ada_[:, 3 * D:4 * D])
    merged, ma, mb, mo, x1, h2 = _mm_then(
        [(None, w['w_out'], 'nn')], f_res1, [proj_g, ya, yb, x],
        [w['w_branch_a'], w['w_branch_b'], s['branch_gate_b'], s['norm2_gain'], ada],
        [row(D, MXU), row(D, MXU), row(D, MXU), row(D), row(D), row(D, MXU)], tm=tm, name="f_out", lhs_fn=f_merge)
    u, v, act = _ffn_in(h2, w['ffn_w1'], w['ffn_w3'], name="f_ffn_in")

    def f_loss(i, n, ff_, x1_, tgt, ada_):
        g2 = ada_[:, 5 * D:6 * D]
        err = x1_ + g2 * ff_ - tgt
        dx2 = err * (1.0 / D)
        loss = 0.5 * jnp.sum(jnp.sum(err * err, axis=1, keepdims=True) * (1.0 / D), axis=0, keepdims=True)
        return dx2, (dx2 * g2), jnp.broadcast_to(loss, (1, 128)), jnp.sum(dx2 * ff_, axis=0, keepdims=True)
    dx2, dff, loss, dgate2 = _mm_then([(act, w['ffn_w2'], 'nn')], f_loss, [x1, target], [ada],
                                      [row(D), row(D, MXU), acc(128), acc(D)], tm=tm, name="f_ffn_out")

    du, dv = _ffn_act_bwd(dff, w['ffn_w2'], u, v, name="b_ffn_out_dx")
    g_w2 = _mm_tn(act, dff, name="b_ffn_out_dw", out_dtype=MXU)
    g_w1 = _mm_tn(h2, du, name="b_ffn_w1_dw", out_dtype=MXU)
    g_w3 = _mm_tn(h2, dv, name="b_ffn_w3_dw", out_dtype=MXU)

    def b_res1(i, n, dh2_, x1_, dx2_, mo_, g, ada_):
        _, vjp = jax.vjp(_norm_mod, x1_, g, ada_[:, 4 * D:5 * D], ada_[:, 3 * D:4 * D])
        dxn, dg, dsc, dsh = vjp(dh2_)
        dx1_ = dxn + dx2_
        g1 = ada_[:, 2 * D:3 * D]
        return dx1_, dx1_ * g1, dg, dsc, dsh, jnp.sum(dx1_ * mo_, axis=0, keepdims=True)
    dx1, dmo, d_gain2, d_scale2, d_shift2, dgate1 = _mm_then(
        [(du, w['ffn_w1'], 'nt'), (dv, w['ffn_w3'], 'nt')], b_res1, [x1, dx2, mo], [s['norm2_gain'], ada],
        [row(D), row(D, MXU), acc(D), acc(D), acc(D), acc(D)], tm=tm, name="b_ffn_in_dx")
    g_wout = _mm_tn(merged, dmo, name="b_out_dw", out_dtype=MXU)

    def b_merge(i, n, dm, pg, ma_, mb_, wa, wb, bias):
        _, vjp = jax.vjp(_merge, pg.astype(F32), ma_.astype(F32), mb_.astype(F32), bias)
        dpg, dma_, dmb_, dbias = vjp(dm)
        dma_, dmb_ = dma_.astype(MXU), dmb_.astype(MXU)
        nt = lambda a_, b_: lax.dot_general(a_, b_, (((1,), (1,)), ((), ())), preferred_element_type=F32)
        return dpg, dma_, dmb_, nt(dma_, wa), nt(dmb_, wb), dbias
    dpg, dma, dmb, dya, dyb, d_bias = _mm_then(
        [(dmo, w['w_out'], 'nt')], b_merge, [proj_g, ma, mb], [w['w_branch_a'], w['w_branch_b'], s['branch_gate_b']],
        [row(GATE_W, MXU), row(D, MXU), row(D, MXU), row(RW), row(RW), acc(GATE_W)], tm=tm, name="b_out_dx")
    g_wa = _mm_tn(ya, dma, name="b_branch_a_dw", out_dtype=MXU, col_shards=4)
    g_wb = _mm_tn(yb, dmb, name="b_branch_b_dw", out_dtype=MXU, col_shards=4)
    fs = DFF // 4
    gw = dict(w_branch_a=g_wa, w_branch_b=g_wb, w_out=g_wout.reshape(4, D // 4, D),
              ffn_w1=jnp.stack([g_w1[:, j * fs:(j + 1) * fs] for j in range(4)]),
              ffn_w3=jnp.stack([g_w3[:, j * fs:(j + 1) * fs] for j in range(4)]),
              ffn_w2=g_w2.reshape(4, fs, D))
    recv = {}
    dqkv, d_qg, d_kg, d_sinks, *got = _attn_bwd(
        proj_qkv, tab, s['q_norm_gain'], s['k_norm_gain'], s['attn_sinks'], dyb, name="b_attn",
        hosted=shards and _ScatterChips([gw[n] for n in BACK_ATTN]))
    recv.update(zip(BACK_ATTN, got))

    def b_post(i, n, y_, r_, k_, v_, g_, dya_, *params):
        _, vjp = jax.vjp(_post, y_, r_, k_, v_, g_, *params)
        dy_, dr_, dk_, dv_, dg_, *dparams = vjp(dya_)
        return (dy_, jnp.concatenate([dr_, dk_, dv_, dg_], axis=1), *dparams)
    dy, drkvg, d_lnx_gain, d_lnx_bias, d_r_k = _rowwise(
        b_post, [y] + rkvg + [dya], post_consts, [row(RW), row(4 * RW, MXU), acc(RW), acc(RW), acc(RW)], tm=tm_wide,
        name="b_post")
    dscan, *got = _scan_bwd(rw, ck, inv, dy, name="b_scan",
                            hosted=shards and _ScatterChips([gw[n] for n in BACK_SCAN]))
    recv.update(zip(BACK_SCAN, got))

    def b_prep(i, n, cur, drw_, dscan_, prev8, mu, *params):
        shifted = _shift_down(cur, prev8, i)
        mixed = cur + (shifted - cur) * mu
        _, vjp = jax.vjp(_prep, mixed, *params)
        blk = lambda t, j: t[:, j * RW:(j + 1) * RW].astype(F32)
        ct = jnp.concatenate([blk(dscan_, 0) + blk(drw_, 0), blk(dscan_, 1), blk(dscan_, 2) + blk(drw_, 1),
                              blk(dscan_, 3) + blk(drw_, 2), blk(dscan_, 4), blk(dscan_, 5), blk(drw_, 3)], axis=1)
        grads = vjp(ct)
        dmixed = grads[0]
        return (dmixed, jnp.sum(dmixed * (shifted - cur), axis=0, keepdims=True)) + tuple(grads[1:])
    dmixed, d_mu, d_w0, d_lora, d_a0, d_gate_up, d_kk, d_ka = _rowwise(
        b_prep, [(proj, SHIFT_W), drkvg, dscan], [s['tshift_mu']] + prep_consts,
        [row(SHIFT_W), acc(SHIFT_W), acc(RW), acc(2 * RW, 128), acc(RW), acc(RW, 128), acc(RW), acc(RW)],
        tm=tm_vjp, name="b_prep", halo=[(proj, SHIFT_W, 'prev')])

    def b_gather(i, n, dm, dqkv_, dpg_, next8, mu):
        dcur = dm * (1.0 - mu) + _shift_up(dm, next8, i, n) * mu
        return (jnp.concatenate([dcur.astype(MXU), dqkv_, dpg_], axis=1),)
    (dproj,) = _rowwise(b_gather, [dmixed, dqkv, dpg], [s['tshift_mu']], [row(IN_W, MXU)], tm=tm_wide, name="b_gather",
                        halo=[(dmixed, SHIFT_W, 'next')])
    g_win = _mm_tn(h1, dproj, name="b_proj_dw", out_dtype=MXU, col_shards=4)

    def col_blocks(g):
        k, n = g.shape
        return g.reshape(k, 4, n // 4).transpose(1, 0, 2).astype(MXU)
    gw.update(w_in=g_win, decay_up=col_blocks(d_lora[:64, :RW]), iclr_up=col_blocks(d_lora[64:, RW:]),
              gate_up=col_blocks(d_gate_up))
    top, bottom = None, None
    if shards:
        top = _ScatterChips([gw['w_in'][:, :D // 2]] + [gw[n] for n in BACK_LAST[1:]])
        bottom = _ScatterChips([gw['w_in'][:, D // 2:3 * D // 4]])
        dh1, *got_top = _mm_nt(dproj, w['w_in'], name="b_proj_dx", hosted=top)
    else:
        dh1 = _mm_nt(dproj, w['w_in'], name="b_proj_dx")

    def b_norm1(i, n, x_, dh1_, dx1_, g, ada_):
        _, vjp = jax.vjp(_norm_mod, x_, g, ada_[:, D:2 * D], ada_[:, 0:D])
        dxn, dg, dsc, dsh = vjp(dh1_)
        return dxn + dx1_, dg, dsc, dsh
    dx, d_gain1, d_scale1, d_shift1, *got_bottom = _rowwise(
        b_norm1, [x, dh1, dx1], [s['norm1_gain'], ada], [row(D), acc(D), acc(D), acc(D)], tm=tm, name="b_norm1",
        hosted=bottom)
    if shards:
        recv.update(zip(BACK_LAST[1:], got_top[1:]))
        recv['w_in'] = [got_top[0], got_bottom[0]]

    d_ada = jnp.concatenate([d_shift1, d_scale1, dgate1, d_shift2, d_scale2, dgate2], axis=1)
    gs = dict(norm1_gain=d_gain1, norm2_gain=d_gain2, tshift_mu=d_mu, decay_w0=d_w0, iclr_a0=d_a0, k_k=d_kk, k_a=d_ka,
              r_k=d_r_k, lnx_gain=d_lnx_gain, lnx_bias=d_lnx_bias, q_norm_gain=d_qg, k_norm_gain=d_kg,
              attn_sinks=d_sinks, branch_gate_b=d_bias)
    return loss, dx, d_ada, gw, gs, recv


ANY = pl.BlockSpec(memory_space=pl.ANY)


def _place():
    x, y, c = lax.axis_index("x"), lax.axis_index("y"), lax.axis_index("c")
    return x, y, c, [(1 - x, y), (x, 1 - y), (1 - x, 1 - y)]


def _all_gather8(x_shard, *, name):
    m_per, n = x_shard.shape

    def body(x_ref, out_ref, send_sems, recv_sems, local_sem):
        x, y, c, chips = _place()
        me, sibling = (x, y, c), (x, y, 1 - c)

        def rows(px, py, pc):
            return out_ref.at[pl.ds((4 * px + 2 * py + pc) * m_per, m_per), :]

        def copy(k, block, to, src=None):
            return pltpu.make_async_remote_copy(
                src_ref=rows(*block) if src is None else src, dst_ref=rows(*block),
                send_sem=send_sems.at[k], recv_sem=recv_sems.at[k], device_id=to, device_id_type=MESH)

        mine = pltpu.make_async_copy(x_ref, rows(*me), local_sem)
        mine.start()
        first = [copy(0, me, sibling, src=x_ref)]
        first += [copy(1 + j, me, (*chip, c), src=x_ref) for j, chip in enumerate(chips)]
        for cp in first:
            cp.start()
        passed = [copy(4 + j, (*chip, c), sibling) for j, chip in enumerate(chips)]
        for j, chip in enumerate(chips):
            copy(1 + j, (*chip, c), me).wait_recv()
            passed[j].start()
        copy(0, sibling, me).wait_recv()
        for j, chip in enumerate(chips):
            copy(4 + j, (*chip, 1 - c), me).wait_recv()
        for cp in first + passed:
            cp.wait_send()
        mine.wait()

    return pl.pallas_call(
        body, name=name, out_shape=jax.ShapeDtypeStruct((8 * m_per, n), x_shard.dtype),
        in_specs=[pl.BlockSpec(memory_space=pltpu.VMEM)], out_specs=pl.BlockSpec(memory_space=pltpu.VMEM),
        scratch_shapes=[pltpu.SemaphoreType.DMA((7,)), pltpu.SemaphoreType.DMA((7,)), pltpu.SemaphoreType.DMA],
    )(x_shard)


class _GatherChips:
    def __init__(self, shards):
        n = len(shards)
        self.arrays, self.n_in, self.n_out = list(shards), n, n
        self.out_shape = [jax.ShapeDtypeStruct((4,) + s.shape, s.dtype) for s in shards]
        self.scratch = [pltpu.SemaphoreType.DMA((3 * n,)), pltpu.SemaphoreType.DMA((3 * n,)),
                        pltpu.SemaphoreType.DMA((n,))]

    def _copies(self, x_refs, out_refs, sems, receiving):
        send_sems, recv_sems, local_sems = sems
        x, y, c, chips = _place()
        s_me = 2 * x + y
        n = self.n_in

        def copy(a, k, s):
            return pltpu.make_async_remote_copy(
                src_ref=x_refs[a], dst_ref=out_refs[a].at[s], send_sem=send_sems.at[3 * a + k],
                recv_sem=recv_sems.at[3 * a + k], device_id=(*chips[k], c), device_id_type=MESH)

        mine = [pltpu.make_async_copy(x_refs[a], out_refs[a].at[s_me], local_sems.at[a]) for a in range(n)]
        sends = [copy(a, k, s_me) for a in range(n) for k in range(3)]
        if not receiving:
            return mine, sends
        return mine, sends, [copy(a, k, 2 * px + py) for a in range(n) for k, (px, py) in enumerate(chips)]

    def start(self, x_refs, out_refs, sems):
        mine, sends = self._copies(x_refs, out_refs, sems, False)
        for cp in mine + sends:
            cp.start()

    def wait(self, x_refs, out_refs, sems):
        mine, sends, recvs = self._copies(x_refs, out_refs, sems, True)
        for cp in recvs:
            cp.wait_recv()
        for cp in sends:
            cp.wait_send()
        for cp in mine:
            cp.wait()


class _GatherChipsHalved(_GatherChips):
    def __init__(self, shards):
        super().__init__(shards)
        n = self.n_in
        self.scratch = [pltpu.SemaphoreType.DMA((6 * n,)), pltpu.SemaphoreType.DMA((6 * n,)),
                        pltpu.SemaphoreType.DMA((n,))]

    def _copies(self, x_refs, out_refs, sems, receiving):
        send_sems, recv_sems, local_sems = sems
        x, y, c, chips = _place()
        s_me = 2 * x + y
        n = self.n_in

        def half(a, who):
            rows = x_refs[a].shape[0] // 2
            return pl.ds(who * rows, rows)

        def over_chips(a, k, s):
            return pltpu.make_async_remote_copy(
                src_ref=x_refs[a].at[half(a, c)], dst_ref=out_refs[a].at[s, half(a, c)],
                send_sem=send_sems.at[3 * a + k], recv_sem=recv_sems.at[3 * a + k],
                device_id=(*chips[k], c), device_id_type=MESH)

        def to_sibling(a, k, s, who):
            return pltpu.make_async_remote_copy(
                src_ref=out_refs[a].at[s, half(a, who)], dst_ref=out_refs[a].at[s, half(a, who)],
                send_sem=send_sems.at[3 * n + 3 * a + k], recv_sem=recv_sems.at[3 * n + 3 * a + k],
                device_id=(x, y, 1 - c), device_id_type=MESH)

        mine = [pltpu.make_async_copy(x_refs[a], out_refs[a].at[s_me], local_sems.at[a]) for a in range(n)]
        sends = [over_chips(a, k, s_me) for a in range(n) for k in range(3)]
        if not receiving:
            return mine, sends
        pairs = [(a, k, 2 * px + py) for a in range(n) for k, (px, py) in enumerate(chips)]
        landed = [over_chips(a, k, s) for a, k, s in pairs]
        passed_on = [to_sibling(a, k, s, c) for a, k, s in pairs]
        from_sibling = [to_sibling(a, k, s, 1 - c) for a, k, s in pairs]
        return mine, sends, landed, passed_on, from_sibling

    def wait(self, x_refs, out_refs, sems):
        mine, sends, landed, passed_on, from_sibling = self._copies(x_refs, out_refs, sems, True)
        for got, fwd in zip(landed, passed_on, strict=True):
            got.wait_recv()
            fwd.start()
        for cp in from_sibling:
            cp.wait_recv()
        for cp in sends + passed_on:
            cp.wait_send()
        for cp in mine:
            cp.wait()


class _ScatterChips:
    def __init__(self, parts):
        n = len(parts)
        self.arrays, self.n_in, self.n_out = list(parts), n, n
        self.out_shape = [jax.ShapeDtypeStruct((3,) + p.shape[1:], p.dtype) for p in parts]
        self.scratch = [pltpu.SemaphoreType.DMA((3 * n,)), pltpu.SemaphoreType.DMA((3 * n,))]

    def _copies(self, g_refs, out_refs, sems):
        send_sems, recv_sems = sems
        x, y, c, chips = _place()
        return [pltpu.make_async_remote_copy(
            src_ref=g_refs[a].at[2 * px + py], dst_ref=out_refs[a].at[k], send_sem=send_sems.at[3 * a + k],
            recv_sem=recv_sems.at[3 * a + k], device_id=(px, py, c), device_id_type=MESH)
            for a in range(self.n_in) for k, (px, py) in enumerate(chips)]

    def start(self, g_refs, out_refs, sems):
        for cp in self._copies(g_refs, out_refs, sems):
            cp.start()

    def wait(self, g_refs, out_refs, sems):
        sends = self._copies(g_refs, out_refs, sems)
        for cp in sends:
            cp.wait_recv()
        for cp in sends:
            cp.wait_send()


def _exchange_call(ex, *, name):
    def body(*refs):
        parts = (refs[:ex.n_in], refs[ex.n_in:ex.n_in + ex.n_out], refs[ex.n_in + ex.n_out:])
        ex.start(*parts)
        ex.wait(*parts)

    return pl.pallas_call(body, name=name, out_shape=ex.out_shape, in_specs=[ANY] * ex.n_in,
                          out_specs=[ANY] * ex.n_out, scratch_shapes=ex.scratch)(*ex.arrays)


class _SwapSibling:
    def __init__(self, vs):
        n = len(vs)
        self.arrays, self.n_in, self.n_out = list(vs), n, n
        self.out_shape = [jax.ShapeDtypeStruct(v.shape, v.dtype) for v in vs]
        self.scratch = [pltpu.SemaphoreType.DMA((n,)), pltpu.SemaphoreType.DMA((n,))]

    def _copies(self, v_refs, out_refs, sems):
        send_sems, recv_sems = sems
        x, y, c, _ = _place()
        return [pltpu.make_async_remote_copy(src_ref=v_refs[a], dst_ref=out_refs[a], send_sem=send_sems.at[a],
                                             recv_sem=recv_sems.at[a], device_id=(x, y, 1 - c), device_id_type=MESH)
                for a in range(self.n_in)]

    def start(self, v_refs, out_refs, sems):
        for cp in self._copies(v_refs, out_refs, sems):
            cp.start()

    def wait(self, v_refs, out_refs, sems):
        for cp in self._copies(v_refs, out_refs, sems):
            cp.wait()


class _Both:
    def __init__(self, first, second):
        self.parts = (first, second)
        self.arrays = first.arrays + second.arrays
        self.n_in, self.n_out = first.n_in + second.n_in, first.n_out + second.n_out
        self.out_shape = first.out_shape + second.out_shape
        self.scratch = first.scratch + second.scratch

    def _split(self, in_refs, out_refs, sems):
        a, b = self.parts
        return ((a, in_refs[:a.n_in], out_refs[:a.n_out], sems[:len(a.scratch)]),
                (b, in_refs[a.n_in:], out_refs[a.n_out:], sems[len(a.scratch):]))

    def start(self, in_refs, out_refs, sems):
        for ex, *refs in self._split(in_refs, out_refs, sems):
            ex.start(*refs)

    def wait(self, in_refs, out_refs, sems):
        for ex, *refs in self._split(in_refs, out_refs, sems):
            ex.wait(*refs)


def _sum_parts(own, others, *, name):
    R, C = own.shape
    tm = _pick(R, (256, 128, 64))

    def body(own_ref, o0_ref, o1_ref, o2_ref, out_ref):
        tot = own_ref[...].astype(F32)
        for ref in (o0_ref, o1_ref, o2_ref):
            tot = tot + ref[...].astype(F32)
        out_ref[...] = tot

    part = lambda k: pl.BlockSpec((None, tm, C), lambda i: (k, i, 0))
    return pl.pallas_call(
        body, name=name, grid=(R // tm,),
        in_specs=[pl.BlockSpec((tm, C), lambda i: (i, 0)), part(0), part(1), part(2)],
        out_specs=pl.BlockSpec((tm, C), lambda i: (i, 0)), out_shape=jax.ShapeDtypeStruct((R, C), F32),
        compiler_params=_cparams(("arbitrary",)),
    )(own, others, others, others)


def _adam_math(w_, m_, v_, g):
    m2 = ADAM_B1 * m_ + (1.0 - ADAM_B1) * g
    v2 = ADAM_B2 * v_ + (1.0 - ADAM_B2) * jnp.square(g)
    m_hat = m2 / (1.0 - ADAM_B1 ** ADAM_STEP)
    v_hat = v2 / (1.0 - ADAM_B2 ** ADAM_STEP)
    delta = -ADAM_LR * (m_hat / (jnp.sqrt(v_hat) + ADAM_EPS) + ADAM_WD * w_)
    return delta, m2, v2


SMALL_SLOTS = 24
SMALL_COLS = 2 * D


def _small_rows(widths):
    firsts, row = [], 0
    for n_i in widths:
        firsts.append(row)
        row += -(-n_i // SMALL_COLS)
    assert row <= SMALL_SLOTS
    return firsts


def _pack_small(grads, *, name):
    n = len(grads)
    firsts = _small_rows([g.shape[1] for g in grads])

    def body(*refs):
        out_ref = refs[n]
        out_ref[...] = jnp.zeros_like(out_ref)
        for first, ref in zip(firsts, refs[:n], strict=True):
            for lo in range(0, ref.shape[1], SMALL_COLS):
                width = min(SMALL_COLS, ref.shape[1] - lo)
                row = first + lo // SMALL_COLS
                out_ref[row:row + 1, 0:width] = ref[:, lo:lo + width]

    return pl.pallas_call(body, name=name, out_shape=jax.ShapeDtypeStruct((SMALL_SLOTS, SMALL_COLS), F32))(*grads)


def _adamw_small(ws, ms, vs, gathered, *, name):
    n = len(ws)
    firsts = _small_rows([w.shape[1] for w in ws] + [128])

    def total(g_ref, first, nc):
        pieces = []
        for lo in range(0, nc, SMALL_COLS):
            width, row = min(SMALL_COLS, nc - lo), first + lo // SMALL_COLS
            g = g_ref[row:row + 1, 0:width]
            for d in range(1, 8):
                g = g + g_ref[d * SMALL_SLOTS + row:d * SMALL_SLOTS + row + 1, 0:width]
            pieces.append(g)
        return pieces[0] if len(pieces) == 1 else jnp.concatenate(pieces, axis=1)

    def body(*refs):
        w_refs, m_refs, v_refs, g_ref = refs[:n], refs[n:2 * n], refs[2 * n:3 * n], refs[3 * n]
        outs = refs[3 * n + 1:]
        for i in range(n):
            g = total(g_ref, firsts[i], w_refs[i].shape[1])
            delta, m2, v2 = _adam_math(w_refs[i][...], m_refs[i][...], v_refs[i][...], g)
            for k, val in enumerate((g, delta, m2, v2)):
                outs[k * n + i][...] = val
        outs[4 * n][...] = total(g_ref, firsts[n], 128)

    shapes = [jax.ShapeDtypeStruct(w.shape, F32) for w in ws]
    res = pl.pallas_call(body, name=name, out_shape=shapes * 4 + [jax.ShapeDtypeStruct((1, 128), F32)],
                         compiler_params=pltpu.CompilerParams(vmem_limit_bytes=VMEM_LIMIT))(*ws, *ms, *vs, gathered)
    return [res[k * n:(k + 1) * n] for k in range(4)], res[4 * n]


def _adamw(w, m, v, gparts, *, tm, name, hosted=None):
    def fn(i, n, w_, m_, v_, *gs):
        g = gs[0]
        for p in gs[1:]:
            g = g + p
        return (g,) + _adam_math(w_, m_, v_, g)
    nc = w.shape[1]
    return _rowwise(fn, [w, m, v] + list(gparts), [], [(nc, F32, 'row')] * 4, tm=tm, name=name, hosted=hosted)


WEIGHTS = ['ada_w', 'ada_b', 'norm1_gain', 'norm2_gain', 'w_in', 'tshift_mu', 'decay_w0', 'decay_up', 'iclr_a0',
           'iclr_up', 'gate_up', 'k_k', 'k_a', 'r_k', 'lnx_gain', 'lnx_bias', 'q_norm_gain', 'k_norm_gain', 'attn_sinks',
           'branch_gate_b', 'w_branch_a', 'w_branch_b', 'w_out', 'ffn_w1', 'ffn_w3', 'ffn_w2']
SHARDED = [('w_in', 1), ('decay_up', 1), ('iclr_up', 1), ('gate_up', 1), ('w_branch_a', 1), ('w_branch_b', 1),
           ('w_out', 0), ('ffn_w1', 1), ('ffn_w3', 1), ('ffn_w2', 0)]
SMALL = ['ada_b', 'norm1_gain', 'norm2_gain', 'tshift_mu', 'decay_w0', 'iclr_a0', 'k_k', 'k_a', 'r_k', 'lnx_gain',
         'lnx_bias', 'q_norm_gain', 'k_norm_gain', 'attn_sinks', 'branch_gate_b']


def kernel(x, c, positions, ada_w, ada_b, norm1_gain, norm2_gain, w_in, tshift_mu, decay_w0, decay_up, iclr_a0, iclr_up, gate_up, k_k, k_a, r_k, lnx_gain, lnx_bias, q_norm_gain, k_norm_gain, attn_sinks, branch_gate_b, w_branch_a, w_branch_b, w_out, ffn_w1, ffn_w3, ffn_w2, loss_target, m_ada_w, m_ada_b, m_norm1_gain, m_norm2_gain, m_w_in, m_tshift_mu, m_decay_w0, m_decay_up, m_iclr_a0, m_iclr_up, m_gate_up, m_k_k, m_k_a, m_r_k, m_lnx_gain, m_lnx_bias, m_q_norm_gain, m_k_norm_gain, m_attn_sinks, m_branch_gate_b, m_w_branch_a, m_w_branch_b, m_w_out, m_ffn_w1, m_ffn_w3, m_ffn_w2, v_ada_w, v_ada_b, v_norm1_gain, v_norm2_gain, v_w_in, v_tshift_mu, v_decay_w0, v_decay_up, v_iclr_a0, v_iclr_up, v_gate_up, v_k_k, v_k_a, v_r_k, v_lnx_gain, v_lnx_bias, v_q_norm_gain, v_k_norm_gain, v_attn_sinks, v_branch_gate_b, v_w_branch_a, v_w_branch_b, v_w_out, v_ffn_w1, v_ffn_w3, v_ffn_w2):
    a = dict(locals())
    W = {n: a[n] for n in WEIGHTS}
    M = {n: a['m_' + n] for n in WEIGHTS}
    V = {n: a['v_' + n] for n in WEIGHTS}
    xi, yi, ci = lax.axis_index("x"), lax.axis_index("y"), lax.axis_index("c")
    me = 4 * xi + 2 * yi + ci
    shard = 2 * xi + yi
    mat = lambda t: t.reshape(t.shape[-2], t.shape[-1])
    sharded = [n for n, _ in SHARDED]

    ax = dict(SHARDED)
    late = LATE
    early = [n for n in sharded if n not in late]
    shards = {n: mat(W[n]).astype(MXU) for n in sharded}
    gathered = _exchange_call(_GatherChipsHalved([shards[n] for n in early]), name="gather_weights")
    full = {n: _full_weight(g, ax[n]) for n, g in zip(early, gathered, strict=True)}

    c_all = _all_gather8(jnp.broadcast_to(c, (8, D)), name="gather_c")[0::8]
    pad_rows = lambda t: jnp.concatenate([t, jnp.zeros((BLK - 8, t.shape[1]), t.dtype)])
    c_all = pad_rows(c_all.astype(MXU))
    ada_cols = _mm_nn(c_all, mat(ada_w).astype(MXU), name="f_ada")[:8]
    ada_all = _all_gather8(ada_cols, name="gather_ada").reshape(2, 2, 2, 8, 6 * D // 4)
    ada_mine = lax.dynamic_index_in_dim(ada_all[:, :, 0], me, axis=2, keepdims=False)
    ada = ada_mine.reshape(1, 6 * D) + mat(ada_b)

    zero = jnp.zeros((64, RW), MXU)
    lora = jnp.concatenate([jnp.concatenate([full['decay_up'], zero], axis=1),
                            jnp.concatenate([zero, full['iclr_up']], axis=1)], axis=0)
    s = {n: W[n].reshape(1, -1) for n in SMALL if n != 'ada_b'}
    s['lora_up'] = lora.astype(F32)
    s['gate_up'] = full['gate_up'].astype(F32)
    tab = _rope_table(positions.reshape(-1))
    loss, dx, d_ada, gw, gs, from_chips = _local_step(x[0], loss_target[0], ada, tab, dict(w_in=full['w_in']), s,
                                                      shards={n: shards[n] for n in late})

    gs['ada_b'] = d_ada
    gsmall = _pack_small([gs[n] for n in SMALL] + [loss], name="pack_small_grads")
    gsmall_all = _all_gather8(gsmall, name="gather_small_grads")
    row = lambda src: [src[n].reshape(1, -1) for n in SMALL]
    sm_out, loss = _adamw_small(row(W), row(M), row(V), gsmall_all, name="adamw_small")
    sm_out = [{n: o.reshape(W[n].shape) for n, o in zip(SMALL, outs_k, strict=True)} for outs_k in sm_out]
    loss = loss[0, 0]

    ada_rows = 6 * D // SMALL_COLS
    d_ada_all = gsmall_all.reshape(8, SMALL_SLOTS, SMALL_COLS)[:, :ada_rows].reshape(8, 6 * D)
    d_ada_cols = lax.dynamic_slice_in_dim(d_ada_all, shard * (6 * D // 4), 6 * D // 4, axis=1)
    g_ada_w = _mm_tn(c_all, pad_rows(d_ada_cols.astype(MXU)), name="b_ada")

    rest = [n for n in sharded if n != 'w_in']
    parts = {n: _sum_parts(lax.dynamic_index_in_dim(gw[n], shard, axis=0, keepdims=False), from_chips[n],
                           name="sum_" + n) for n in rest}
    tail = _Both(_ScatterChips([gw['w_in'][:, 3 * D // 4:]]), _SwapSibling([parts[n] for n in rest]))
    res = _adamw(mat(ada_w), mat(m_ada_w), mat(v_ada_w), [g_ada_w], tm=256, name="adamw_ada", hosted=tail)
    ada_out, last_quarter, others = res[:4], res[4], dict(zip(rest, res[5:], strict=True))
    parts['w_in'] = _sum_parts(lax.dynamic_index_in_dim(gw['w_in'], shard, axis=0, keepdims=False),
                               jnp.concatenate(from_chips['w_in'] + [last_quarter], axis=1), name="sum_w_in")
    others['w_in'] = _exchange_call(_SwapSibling([parts['w_in']]), name="swap_w_in")[0]
    sh_out = {}
    for n in sharded:
        part, other = parts[n], others[n]
        sh_out[n] = _adamw(mat(W[n]), mat(M[n]), mat(V[n]), [part, other], tm=_pick(part.shape[0], (256, 128, 64)),
                           name="adamw_" + n)

    def leaf(k, n):
        if n == 'ada_w':
            return ada_out[k].reshape(W[n].shape)
        if n in sharded:
            return sh_out[n][k].reshape(W[n].shape)
        return sm_out[k][n]
    outs = [leaf(k, n) for k in range(4) for n in WEIGHTS]
    return (loss, dx[None], *outs)
```

```python
import functools
import math

import jax
import jax.numpy as jnp
from jax import lax
from jax.experimental import pallas as pl
from jax.experimental.pallas import tpu as pltpu

F32 = jnp.float32
BF16 = jnp.bfloat16
MXU = BF16
HI = lax.Precision.HIGHEST

D = 1024
HD = 64
NH = 8
RW = NH * HD
SHIFT_W = 3 * RW + 64 + 64 + 128
QKV_W = RW + 2 * 128
GATE_W = 2 * D
IN_W = SHIFT_W + QKV_W + GATE_W
DFF = 2816
BLK = 128
CHUNK = 64
RMS_EPS = 1e-6
GN_EPS = 64e-5
NEG_INF = -1e30
ADAM_LR, ADAM_B1, ADAM_B2, ADAM_EPS, ADAM_WD, ADAM_STEP = 0.001, 0.9, 0.999, 1e-08, 0.01, 10
VMEM_LIMIT = 56 * 1024 * 1024
MESH = pl.DeviceIdType.MESH


def _cparams(sem=None):
    return pltpu.CompilerParams(dimension_semantics=sem, vmem_limit_bytes=VMEM_LIMIT)


def _full_spec(a):
    nd = a.ndim
    return pl.BlockSpec(a.shape, lambda *_: (0,) * nd)


def _rowwise(fn, rows, consts, outs, *, tm, name, halo=(), hosted=None):
    rows = [(a + (0,))[:3] if isinstance(a, tuple) else (a, a.shape[1], 0) for a in rows]
    T = rows[0][0].shape[0]
    assert T % tm == 0 and tm % 8 == 0
    n_tiles = T // tm
    n_in = len(rows) + len(halo) + len(consts)
    in_specs = [pl.BlockSpec((tm, nc), lambda i, j=j: (i, j)) for _, nc, j in rows]
    args = [a for a, _, _ in rows]
    for a, nc, kind in halo:
        if kind == 'prev':
            in_specs.append(pl.BlockSpec((8, nc), lambda i: (jnp.maximum(i * (tm // 8) - 1, 0), 0)))
        else:
            in_specs.append(pl.BlockSpec((8, nc), lambda i: (jnp.minimum((i + 1) * (tm // 8), T // 8 - 1), 0)))
        args.append(a)
    in_specs += [_full_spec(a) for a in consts]
    args += list(consts)
    out_shape, out_specs = [], []
    for ncols, dtype, kind in outs:
        if kind == 'row':
            out_shape.append(jax.ShapeDtypeStruct((T, ncols), dtype))
            out_specs.append(pl.BlockSpec((tm, ncols), lambda i: (i, 0)))
        else:
            out_shape.append(jax.ShapeDtypeStruct((kind, ncols), dtype))
            out_specs.append(pl.BlockSpec((kind, ncols), lambda i: (0, 0)))

    def body(*refs):
        i = pl.program_id(0)
        vals = [r[...] for r in refs[:n_in]]
        res = fn(i, n_tiles, *vals)
        for (ncols, dtype, kind), o_ref, val in zip(outs, refs[n_in:], res, strict=True):
            if kind == 'row':
                o_ref[...] = val.astype(dtype)
            else:
                @pl.when(i == 0)
                def _():
                    o_ref[...] = jnp.zeros_like(o_ref)
                o_ref[...] += val.astype(dtype)

    h_in, h_in_specs, h_out_specs, h_out_shape, h_scratch = _hosted_args(hosted)
    res = pl.pallas_call(
        _hosting(body, hosted, n_in, len(outs), 0, n_tiles), name=name, grid=(n_tiles,),
        in_specs=in_specs + h_in_specs, out_specs=out_specs + h_out_specs, out_shape=out_shape + h_out_shape,
        scratch_shapes=h_scratch, compiler_params=_cparams(("arbitrary",)),
    )(*args, *h_in)
    return res


def _pick(n, cands):
    for c in cands:
        if n % c == 0:
            return c
    return n


MM_ROWS = (1024, 512, 256, 128)
MM_COLS = (1536, 1408, 1024, 896, 768, 512, 256, 128)
MM_WIDE = 3000


def _mm_nn(a, w, *, name, out_dtype=F32, hosted=None):
    T, K = a.shape
    N = w.shape[1]
    tm = _pick(T, MM_ROWS)
    tn = _pick(N, MM_COLS)
    grid = (N // tn, T // tm)
    h_in, h_in_specs, h_out_specs, h_out_shape, h_scratch = _hosted_args(hosted)

    def body(a_ref, w_ref, o_ref):
        o_ref[...] = jnp.dot(a_ref[...], w_ref[...], preferred_element_type=F32).astype(out_dtype)

    res = pl.pallas_call(
        _hosting(body, hosted, 2, 1, 0, grid), name=name, grid=grid,
        in_specs=[pl.BlockSpec((tm, K), lambda j, i: (i, 0)), pl.BlockSpec((K, tn), lambda j, i: (0, j))] + h_in_specs,
        out_specs=[pl.BlockSpec((tm, tn), lambda j, i: (i, j))] + h_out_specs,
        out_shape=[jax.ShapeDtypeStruct((T, N), out_dtype)] + h_out_shape, scratch_shapes=h_scratch,
        compiler_params=_cparams(("arbitrary", "arbitrary")),
    )(a, w, *h_in)
    return res if hosted else res[0]


def _mm_nt(dy, w, *, name, out_dtype=F32, hosted=None):
    T, N = dy.shape
    K = w.shape[0]
    tm = _pick(T, MM_ROWS if N <= MM_WIDE else MM_ROWS[1:])
    tk = _pick(K, MM_COLS[1:])
    grid = (K // tk, T // tm)
    h_in, h_in_specs, h_out_specs, h_out_shape, h_scratch = _hosted_args(hosted)

    def body(dy_ref, w_ref, o_ref):
        o_ref[...] = lax.dot_general(dy_ref[...], w_ref[...], (((1,), (1,)), ((), ())),
                                     preferred_element_type=F32).astype(out_dtype)

    res = pl.pallas_call(
        _hosting(body, hosted, 2, 1, 0, grid), name=name, grid=grid,
        in_specs=[pl.BlockSpec((tm, N), lambda j, i: (i, 0)), pl.BlockSpec((tk, N), lambda j, i: (j, 0))] + h_in_specs,
        out_specs=[pl.BlockSpec((tm, tk), lambda j, i: (i, j))] + h_out_specs,
        out_shape=[jax.ShapeDtypeStruct((T, K), out_dtype)] + h_out_shape, scratch_shapes=h_scratch,
        compiler_params=_cparams(("arbitrary", "arbitrary")),
    )(dy, w, *h_in)
    return res if hosted else res[0]


def _mm_tn(a, dy, *, name, out_dtype=F32, col_shards=None):
    T, K = a.shape
    N = dy.shape[1]
    tm = _pick(T, MM_ROWS)
    tn = N // col_shards if col_shards else _pick(N, MM_COLS[1:])
    n_t = T // tm

    def body(a_ref, dy_ref, o_ref, acc_ref):
        i = pl.program_id(1)

        @pl.when(i == 0)
        def _():
            acc_ref[...] = jnp.zeros_like(acc_ref)

        acc_ref[...] += lax.dot_general(a_ref[...], dy_ref[...], (((0,), (0,)), ((), ())), preferred_element_type=F32)

        @pl.when(i == n_t - 1)
        def _():
            o_ref[...] = acc_ref[...].astype(out_dtype)

    if col_shards:
        out_specs = pl.BlockSpec((None, K, tn), lambda j, i: (j, 0, 0))
        out_shape = jax.ShapeDtypeStruct((col_shards, K, tn), out_dtype)
    else:
        out_specs = pl.BlockSpec((K, tn), lambda j, i: (0, j))
        out_shape = jax.ShapeDtypeStruct((K, N), out_dtype)
    return pl.pallas_call(
        body, name=name, grid=(N // tn, n_t),
        in_specs=[pl.BlockSpec((tm, K), lambda j, i: (i, 0)), pl.BlockSpec((tm, tn), lambda j, i: (i, j))],
        out_specs=out_specs, out_shape=out_shape, scratch_shapes=[pltpu.VMEM((K, tn), F32)],
        compiler_params=_cparams(("arbitrary", "arbitrary")),
    )(a, dy)


def _mm_then(products, fn, rows, consts, outs, *, tm, name, lhs_fn=None, hosted=None):
    products = [(p + (None,))[:4] for p in products]
    T = (rows[0] if lhs_fn else products[0][0]).shape[-2]
    n_tiles = T // tm
    in_specs, args = [], []
    for a, w, _, j in products:
        if a is not None and j is None:
            in_specs.append(pl.BlockSpec((tm, a.shape[1]), lambda i: (i, 0)))
            args.append(a)
        elif a is not None:
            in_specs.append(pl.BlockSpec((None, tm, a.shape[2]), lambda i, j=j: (j, i, 0)))
            args.append(a)
        in_specs.append(_full_spec(w) if j is None else
                        pl.BlockSpec((None,) + w.shape[1:], lambda i, j=j: (j, 0, 0)))
        args.append(w)
    n_w = len(args)
    in_specs += [pl.BlockSpec((tm, a.shape[1]), lambda i: (i, 0)) for a in rows] + [_full_spec(c_) for c_ in consts]
    args += list(rows) + list(consts)
    n_in = len(args)
    out_shape, out_specs = [], []
    for ncols, dtype, kind in outs:
        if kind == 'row':
            out_shape.append(jax.ShapeDtypeStruct((T, ncols), dtype))
            out_specs.append(pl.BlockSpec((tm, ncols), lambda i: (i, 0)))
        else:
            out_shape.append(jax.ShapeDtypeStruct((kind, ncols), dtype))
            out_specs.append(pl.BlockSpec((kind, ncols), lambda i: (0, 0)))

    def body(*refs):
        i = pl.program_id(0)
        tiles = [r[...] for r in refs[n_w:n_in]]
        made = []
        if lhs_fn:
            made = lhs_fn(*tiles)
            made = list(made) if isinstance(made, tuple) else [made]
            made[0] = made[0].astype(MXU)
        y, pos = None, 0
        for a, _, form, _ in products:
            if a is None:
                lhs = made[0]
            else:
                lhs, pos = refs[pos][...], pos + 1
            dims = (((1,), (0,)), ((), ())) if form == 'nn' else (((1,), (1,)), ((), ()))
            t = lax.dot_general(lhs, refs[pos][...], dims, preferred_element_type=F32)
            pos += 1
            y = t if y is None else y + t
        res = fn(i, n_tiles, y, *made, *tiles)
        for (ncols, dtype, kind), o_ref, val in zip(outs, refs[n_in:], res, strict=True):
            if kind == 'row':
                o_ref[...] = val.astype(dtype)
            else:
                @pl.when(i == 0)
                def _():
                    o_ref[...] = jnp.zeros_like(o_ref)
                o_ref[...] += val.astype(dtype)

    h_in, h_in_specs, h_out_specs, h_out_shape, h_scratch = _hosted_args(hosted)
    return pl.pallas_call(
        _hosting(body, hosted, n_in, len(outs), 0, n_tiles), name=name, grid=(n_tiles,),
        in_specs=in_specs + h_in_specs, out_specs=out_specs + h_out_specs, out_shape=out_shape + h_out_shape,
        scratch_shapes=h_scratch, compiler_params=_cparams(("arbitrary",)))(*args, *h_in)


def _seg_ones(n):
    r = lax.broadcasted_iota(jnp.int32, (n, n), 0) // HD
    c = lax.broadcasted_iota(jnp.int32, (n, n), 1) // HD
    return (r == c).astype(F32)


def _segsum_raw(x):
    ones = _seg_ones(x.shape[1])
    if MXU == F32:
        return jnp.dot(x, ones, precision=HI, preferred_element_type=F32)
    hi = x.astype(MXU)
    lo = (x - hi.astype(F32)).astype(MXU)
    ones = ones.astype(MXU)
    return jnp.dot(hi, ones, preferred_element_type=F32) + jnp.dot(lo, ones, preferred_element_type=F32)


@jax.custom_vjp
def _segsum(x):
    return _segsum_raw(x)


def _segsum_fwd(x):
    return _segsum_raw(x), None


def _segsum_bwd(_, g):
    return (_segsum_raw(g),)


_segsum.defvjp(_segsum_fwd, _segsum_bwd)


def _mxu(x):
    return x.astype(MXU)


@jax.custom_vjp
def _bdot(a, b):
    return jnp.dot(_mxu(a), _mxu(b), preferred_element_type=F32)


def _bdot_fwd(a, b):
    return _bdot(a, b), (a, b)


def _bdot_bwd(res, g):
    a, b = res
    da = lax.dot_general(_mxu(g), _mxu(b), (((1,), (1,)), ((), ())), preferred_element_type=F32)
    db = lax.dot_general(_mxu(a), _mxu(g), (((0,), (0,)), ((), ())), preferred_element_type=F32)
    return da.astype(a.dtype), db.astype(b.dtype)


_bdot.defvjp(_bdot_fwd, _bdot_bwd)


def _sigmoid(x):
    return 1.0 / (1.0 + jnp.exp(-x))


def _softplus(x):
    return jnp.maximum(x, 0.0) + jnp.log(1.0 + jnp.exp(jnp.minimum(x, -x)))


def _norm_mod(x, gain, scale, shift):
    inv = lax.rsqrt(jnp.mean(x * x, axis=-1, keepdims=True) + RMS_EPS)
    return (x * inv) * gain * (1.0 + scale) + shift


def _prep(mixed, decay_w0, lora_up, iclr_a0, gate_up, k_k, k_a):
    r = mixed[:, 0:RW]
    k = mixed[:, RW:2 * RW]
    v = mixed[:, 2 * RW:3 * RW]
    z = mixed[:, 3 * RW:3 * RW + 128]
    xg = mixed[:, 3 * RW + 128:]
    lane = lax.broadcasted_iota(jnp.int32, z.shape, 1)
    tz = jnp.where(lane < 64, jnp.tanh(z), z)
    lo = _bdot(tz, lora_up)
    w_log = -_softplus(-(decay_w0 + lo[:, :RW])) - 0.5
    lw = -jnp.exp(w_log)
    a_ic = _sigmoid(iclr_a0 + lo[:, RW:])
    g = _bdot(_sigmoid(xg), gate_up)
    kk = k * k_k
    kk = kk / jnp.maximum(jnp.sqrt(_segsum(kk * kk)), 1e-12)
    k_mod = k * (1.0 + (a_ic - 1.0) * k_a)
    return jnp.concatenate([r, lw, k_mod, v, -kk, kk * a_ic, g], axis=1)


def _post(y, r, k, v, g, lnx_gain, lnx_bias, r_k):
    mu = _segsum(y) * (1.0 / HD)
    yc = y - mu
    var = _segsum(yc * yc) * (1.0 / HD)
    yn = yc * lax.rsqrt(var + GN_EPS) * lnx_gain + lnx_bias
    bonus = _segsum(r * k * r_k) * v
    return (yn + bonus) * g


def _merge(pg, ma, mb, bias):
    gates = _sigmoid(pg + bias)
    return gates[:, :D] * ma + gates[:, D:] * mb


def _swiglu(u, v):
    return u * _sigmoid(u) * v


def _ffn_in(h, w1, w3, *, name):
    T, K = h.shape
    ns, _, Fs = w1.shape
    tm = _pick(T, MM_ROWS)

    def body(h_ref, w1_ref, w3_ref, u_ref, v_ref, a_ref):
        u = jnp.dot(h_ref[...], w1_ref[...], preferred_element_type=F32).astype(MXU)
        v = jnp.dot(h_ref[...], w3_ref[...], preferred_element_type=F32).astype(MXU)
        u_ref[...] = u
        v_ref[...] = v
        a_ref[...] = _swiglu(u.astype(F32), v.astype(F32)).astype(MXU)

    wspec = pl.BlockSpec((None, K, Fs), lambda j, i: (j, 0, 0))
    ospec = pl.BlockSpec((None, tm, Fs), lambda j, i: (j, i, 0))
    return pl.pallas_call(
        body, name=name, grid=(ns, T // tm),
        in_specs=[pl.BlockSpec((tm, K), lambda j, i: (i, 0)), wspec, wspec],
        out_specs=[ospec] * 3, out_shape=[jax.ShapeDtypeStruct((ns, T, Fs), MXU)] * 3,
        compiler_params=_cparams(("arbitrary", "arbitrary")),
    )(h, w1, w3)


def _ffn_act_bwd(dff, w2, u, v, *, name):
    T, N = dff.shape
    ns, Fs, _ = w2.shape
    tm = _pick(T, MM_ROWS)

    def body(dy_ref, w_ref, u_ref, v_ref, du_ref, dv_ref):
        dact = lax.dot_general(dy_ref[...], w_ref[...], (((1,), (1,)), ((), ())), preferred_element_type=F32)
        _, vjp = jax.vjp(_swiglu, u_ref[...].astype(F32), v_ref[...].astype(F32))
        du, dv = vjp(dact)
        du_ref[...] = du.astype(MXU)
        dv_ref[...] = dv.astype(MXU)

    tile = pl.BlockSpec((None, tm, Fs), lambda j, i: (j, i, 0))
    return pl.pallas_call(
        body, name=name, grid=(ns, T // tm),
        in_specs=[pl.BlockSpec((tm, N), lambda j, i: (i, 0)), pl.BlockSpec((None, Fs, N), lambda j, i: (j, 0, 0)),
                  tile, tile],
        out_specs=[tile, tile], out_shape=[jax.ShapeDtypeStruct((ns, T, Fs), MXU)] * 2,
        compiler_params=_cparams(("arbitrary", "arbitrary")),
    )(dff, w2, u, v)


def _mm_tn_blocks(a, dy, *, name, out_dtype):
    a3, d3 = a.ndim == 3, dy.ndim == 3
    ns = a.shape[0] if a3 else dy.shape[0]
    T, K, N = a.shape[-2], a.shape[-1], dy.shape[-1]
    tm = _pick(T, MM_ROWS)
    n_t = T // tm

    def body(a_ref, dy_ref, o_ref, acc_ref):
        i = pl.program_id(1)

        @pl.when(i == 0)
        def _():
            acc_ref[...] = jnp.zeros_like(acc_ref)

        acc_ref[...] += lax.dot_general(a_ref[...], dy_ref[...], (((0,), (0,)), ((), ())), preferred_element_type=F32)

        @pl.when(i == n_t - 1)
        def _():
            o_ref[...] = acc_ref[...].astype(out_dtype)

    spec = lambda is3, n: (pl.BlockSpec((None, tm, n), lambda j, i: (j, i, 0)) if is3
                           else pl.BlockSpec((tm, n), lambda j, i: (i, 0)))
    return pl.pallas_call(
        body, name=name, grid=(ns, n_t), in_specs=[spec(a3, K), spec(d3, N)],
        out_specs=pl.BlockSpec((None, K, N), lambda j, i: (j, 0, 0)),
        out_shape=jax.ShapeDtypeStruct((ns, K, N), out_dtype), scratch_shapes=[pltpu.VMEM((K, N), F32)],
        compiler_params=_cparams(("arbitrary", "arbitrary")),
    )(a, dy)


@functools.partial(jax.custom_vjp, nondiff_argnums=(1,))
def _lane_roll(x, s):
    return pltpu.roll(x, s, 1)


def _lane_roll_fwd(x, s):
    return pltpu.roll(x, s, 1), None


def _lane_roll_bwd(s, _, g):
    n = g.shape[1]
    return (pltpu.roll(g, (n - s) % n, 1),)


_lane_roll.defvjp(_lane_roll_fwd, _lane_roll_bwd)


def _rope(x, cos, sin_lo, sin_hi):
    n = x.shape[1]
    return x * cos + _lane_roll(x, n - 8) * sin_lo + _lane_roll(x, 8) * sin_hi


def _head_rms(x, gain):
    return x * lax.rsqrt(_segsum(x * x) * (1.0 / HD) + RMS_EPS) * gain


def _attn_blocks(qkv_c, qkv_p, tab_c, tab_p, qg, kg, sinks, first):
    nb = qkv_c.shape[0] // BLK
    G = 4

    def tabs(tab, n):
        return [jnp.tile(tab[:, j * 128:(j + 1) * 128], (1, n // 128)) for j in range(3)]

    qg = jnp.concatenate([qg] * NH, axis=1)
    kg = jnp.concatenate([kg] * 2, axis=1)
    q = _rope(_head_rms(qkv_c[:, :RW], qg), *tabs(tab_c, RW))
    k_in = jnp.concatenate([qkv_p[:, RW:RW + 128], qkv_c[:, RW:RW + 128]], axis=0)
    k = _rope(_head_rms(k_in, kg), *tabs(jnp.concatenate([tab_p, tab_c], axis=0), 128))
    v = jnp.concatenate([qkv_p[:, RW + 128:], qkv_c[:, RW + 128:]], axis=0)

    pile = lambda xs: jnp.concatenate([x_[None] for x_ in xs], axis=0)

    def bands(t):
        return pile([t[b * BLK:(b + 2) * BLK, kvh * HD:(kvh + 1) * HD] for kvh in range(2) for b in range(nb)])

    qs = pile([jnp.concatenate([q[b * BLK:(b + 1) * BLK, (G * kvh + g) * HD:(G * kvh + g + 1) * HD]
                                for g in range(G)], axis=0) for kvh in range(2) for b in range(nb)])
    s = _bmm(qs, bands(k), 2, 2, 1) * (HD ** -0.5)
    qi = lax.broadcasted_iota(jnp.int32, (G * BLK, 2 * BLK), 0) % BLK
    kj = lax.broadcasted_iota(jnp.int32, (G * BLK, 2 * BLK), 1)
    dist = qi + BLK - kj
    in_band = (dist >= 0) & (dist < BLK)
    pair = lax.broadcasted_iota(jnp.int32, (2 * nb, 1, 1), 0)
    no_prev = (pair % nb == 0) & first
    valid = in_band[None] & (jnp.logical_not(no_prev) | (kj >= BLK)[None])
    s = jnp.where(valid, s, NEG_INF)
    row_g = lax.broadcasted_iota(jnp.int32, (G * BLK, 1), 0) // BLK
    sink = []
    for kvh in range(2):
        col = jnp.zeros((G * BLK, 1), F32)
        for g in range(G):
            col = jnp.where(row_g == g, sinks[:, G * kvh + g:G * kvh + g + 1], col)
        sink += [col] * nb
    sink = pile(sink)
    m = lax.stop_gradient(jnp.maximum(jnp.max(s, axis=-1, keepdims=True), sink))
    e = jnp.exp(s - m)
    p = e * (1.0 / (jnp.sum(e, axis=-1, keepdims=True) + jnp.exp(sink - m)))
    o = _bmm(p, bands(v), 2, 1, 1)
    return jnp.concatenate([jnp.concatenate([o[kvh * nb + b, g * BLK:(g + 1) * BLK] for kvh in range(2)
                                             for g in range(G)], axis=1) for b in range(nb)], axis=0)


def _heads(x):
    return jnp.stack([x[:, h * HD:(h + 1) * HD] for h in range(NH)], axis=0)


def _unheads(x):
    return jnp.concatenate([x[h] for h in range(NH)], axis=1)


def _split(x, n):
    parts, rest = [], x
    for _ in range(n):
        p = rest.astype(MXU)
        parts.append(p)
        rest = rest - p.astype(F32)
    return parts


def _bdot_batched(a, b, ca, cb):
    return lax.dot_general(a, b, (((ca,), (cb,)), ((0,), (0,))), preferred_element_type=F32)


def _bmm_passes(a, b, ca, cb, passes):
    if MXU == F32:
        return lax.dot_general(a, b, (((ca,), (cb,)), ((0,), (0,))), precision=HI, preferred_element_type=F32)
    if passes == 1:
        return _bdot_batched(a.astype(MXU), b.astype(MXU), ca, cb)
    (a0, a1), (b0, b1) = _split(a, 2), _split(b, 2)
    return _bdot_batched(a0, b0, ca, cb) + (_bdot_batched(a0, b1, ca, cb) + _bdot_batched(a1, b0, ca, cb))


@functools.partial(jax.custom_vjp, nondiff_argnums=(2, 3, 4))
def _bmm(a, b, ca, cb, passes=1):
    return _bmm_passes(a, b, ca, cb, passes)


def _bmm_fwd(a, b, ca, cb, passes):
    return _bmm_passes(a, b, ca, cb, passes), (a, b)


def _bmm_bwd(ca, cb, passes, res, g):
    a, b = res
    if (ca, cb) == (2, 1):
        return _bmm_passes(g, b, 2, 2, passes), _bmm_passes(a, g, 1, 1, passes)
    if (ca, cb) == (2, 2):
        return _bmm_passes(g, b, 2, 1, passes), _bmm_passes(g, a, 1, 1, passes)
    return _bmm_passes(b, g, 2, 2, passes), _bmm_passes(a, g, 2, 1, passes)


_bmm.defvjp(_bmm_fwd, _bmm_bwd)


def _tri_dot(x, transpose):
    C = x.shape[1]
    ri = lax.broadcasted_iota(jnp.int32, (C, C), 0)
    ci = lax.broadcasted_iota(jnp.int32, (C, C), 1)
    tri = jnp.broadcast_to(((ri <= ci) if transpose else (ri >= ci)).astype(MXU), (x.shape[0], C, C))
    if MXU == F32:
        return lax.dot_general(tri, x, (((2,), (1,)), ((0,), (0,))), precision=HI, preferred_element_type=F32)
    p0, p1, p2 = _split(x, 3)
    return _bdot_batched(tri, p0, 2, 1) + (_bdot_batched(tri, p1, 2, 1) + _bdot_batched(tri, p2, 2, 1))


@jax.custom_vjp
def _cumsum_rows(x):
    return _tri_dot(x, False)


def _cumsum_rows_fwd(x):
    return _tri_dot(x, False), None


def _cumsum_rows_bwd(_, g):
    return (_tri_dot(g, True),)


_cumsum_rows.defvjp(_cumsum_rows_fwd, _cumsum_rows_bwd)

P_SCORE = 1
P_SOLVE = 1
P_STATE = 1
SCAN_CHUNKS = (4, 2, 1)


def _neumann(l):
    C = l.shape[1]
    eye = (lax.broadcasted_iota(jnp.int32, (C, C), 0) == lax.broadcasted_iota(jnp.int32, (C, C), 1)).astype(F32)
    x, lp = eye + l, l
    for _ in range(int(math.log2(C)) - 1):
        lp = _bmm(lp, lp, 2, 1, P_SOLVE)
        x = x + _bmm(x, lp, 2, 1, P_SOLVE)
    return x


@jax.custom_vjp
def _unit_lower_inverse(l):
    return _neumann(l)


def _unit_lower_inverse_fwd(l):
    x = _neumann(l)
    return x, x


def _unit_lower_inverse_bwd(x, g):
    return (_bmm(_bmm(x, g, 1, 1, P_SOLVE), x, 2, 2, P_SOLVE),)


_unit_lower_inverse.defvjp(_unit_lower_inverse_fwd, _unit_lower_inverse_bwd)


def _known_inverse(x):
    @jax.custom_vjp
    def f(l):
        return x

    f.defvjp(lambda l: (x, None), lambda _, g: (_bmm(_bmm(x, g, 1, 1, P_SOLVE), x, 2, 2, P_SOLVE),))
    return f


def _chunk(S0, r, lw, k, v, a, b, inverse=None):
    C = CHUNK
    n = r.shape[1] // C
    fold = lambda t: t.reshape(NH * n, C, HD)
    r, lw, k, v, a, b = (fold(t) for t in (r, lw, k, v, a, b))
    ri = lax.broadcasted_iota(jnp.int32, (C, C), 0)
    ci = lax.broadcasted_iota(jnp.int32, (C, C), 1)
    strict = (ri > ci)
    cum = _cumsum_rows(lw)
    p_in = jnp.exp(cum)
    p_ex = jnp.exp(cum - lw)
    p_inv = jnp.exp(-cum)
    at, rt, bt, kt = a * p_ex, r * p_in, b * p_inv, k * p_inv
    lhs = jnp.concatenate([at, rt], axis=1)
    rhs_ = jnp.concatenate([bt, kt], axis=1)
    sc = _bmm(lhs, rhs_, 2, 2, P_SCORE)
    a_ab = jnp.where(strict, sc[:, :C, :C], 0.0)
    a_ak = jnp.where(strict, sc[:, :C, C:], 0.0)
    incl2 = (lax.broadcasted_iota(jnp.int32, (C, 2 * C), 0) >= lax.broadcasted_iota(jnp.int32, (C, 2 * C), 1) % C)
    a_r = jnp.where(incl2, sc[:, C:, :], 0.0)
    av = _bmm(a_ak, v, 2, 1, P_SCORE)
    x = x_all = (_unit_lower_inverse if inverse is None else _known_inverse(inverse))(a_ab)
    p_last = jnp.exp(cum[:, C - 1:C, :])
    per_chunk = lambda t: t.reshape((NH, n) + t.shape[1:])
    lhs, rhs_, a_r, av, x, v, p_last = (per_chunk(t) for t in (lhs, rhs_, a_r, av, x, v, p_last))
    S, ys = S0, []
    for c in range(n):
        s0 = _bmm(lhs[:, c], S, 2, 2, P_STATE)
        u = _bmm(x[:, c], s0[:, :C] + av[:, c], 2, 1, P_SOLVE)
        uv = jnp.concatenate([u, v[:, c]], axis=1)
        ys.append(s0[:, C:] + _bmm(a_r[:, c], uv, 2, 1, P_SCORE))
        S = (S + _bmm(uv, rhs_[:, c], 1, 1, P_STATE)) * p_last[:, c]
    return jnp.concatenate(ys, axis=1), S, x_all


def _hosting(body, ex, n_in, n_out, n_scratch, n_steps):
    if ex is None:
        return body

    def wrapped(*refs):
        a = n_in
        b = a + ex.n_in
        c = b + n_out
        d = c + ex.n_out
        e = d + n_scratch
        ex_refs = (refs[a:b], refs[c:d], refs[e:])
        grid = n_steps if isinstance(n_steps, tuple) else (n_steps,)
        first = last = True
        for ax_, size in enumerate(grid):
            first = first & (pl.program_id(ax_) == 0)
            last = last & (pl.program_id(ax_) == size - 1)

        @pl.when(first)
        def _():
            ex.start(*ex_refs)

        body(*refs[:a], *refs[b:c], *refs[d:e])

        @pl.when(last)
        def _():
            ex.wait(*ex_refs)

    return wrapped


def _hosted_args(ex):
    if ex is None:
        return [], [], [], [], []
    any_spec = pl.BlockSpec(memory_space=pl.ANY)
    return list(ex.arrays), [any_spec] * ex.n_in, [any_spec] * ex.n_out, list(ex.out_shape), list(ex.scratch)


def _scan_fwd(rw, *, name, hosted=None):
    T = rw.shape[0]
    nc = _pick(T // CHUNK, SCAN_CHUNKS)
    rows = CHUNK * nc
    n = T // rows
    h_in, h_in_specs, h_out_specs, h_out_shape, h_scratch = _hosted_args(hosted)

    def body(r_ref, lw_ref, k_ref, v_ref, a_ref, b_ref, y_ref, ck_ref, inv_ref, s_ref):
        @pl.when(pl.program_id(0) == 0)
        def _():
            s_ref[...] = jnp.zeros_like(s_ref)

        S0 = s_ref[...]
        ck_ref[0] = S0
        y, S1, inv = _chunk(S0, *[_heads(ref[...]) for ref in (r_ref, lw_ref, k_ref, v_ref, a_ref, b_ref)])
        y_ref[...] = _unheads(y)
        inv_ref[0] = inv
        s_ref[...] = S1

    col = lambda j: pl.BlockSpec((rows, RW), lambda i: (i, j))
    return pl.pallas_call(
        _hosting(body, hosted, 6, 3, 1, n), name=name, grid=(n,),
        in_specs=[col(j) for j in range(6)] + h_in_specs,
        out_specs=[pl.BlockSpec((rows, RW), lambda i: (i, 0)),
                   pl.BlockSpec((1, NH, HD, HD), lambda i: (i, 0, 0, 0)),
                   pl.BlockSpec((1, NH * nc, CHUNK, CHUNK), lambda i: (i, 0, 0, 0))] + h_out_specs,
        out_shape=[jax.ShapeDtypeStruct((T, RW), F32), jax.ShapeDtypeStruct((n, NH, HD, HD), F32),
                   jax.ShapeDtypeStruct((n, NH * nc, CHUNK, CHUNK), F32)] + h_out_shape,
        scratch_shapes=[pltpu.VMEM((NH, HD, HD), F32)] + h_scratch,
        compiler_params=_cparams(("arbitrary",)),
    )(rw, rw, rw, rw, rw, rw, *h_in)


def _scan_bwd(rw, ck, inv, dy, *, name, hosted=None):
    T = rw.shape[0]
    nc = _pick(T // CHUNK, SCAN_CHUNKS)
    rows = CHUNK * nc
    n = T // rows

    def body(r_ref, lw_ref, k_ref, v_ref, a_ref, b_ref, ck_ref, inv_ref, dy_ref, o_ref, ds_ref):
        @pl.when(pl.program_id(0) == 0)
        def _():
            ds_ref[...] = jnp.zeros_like(ds_ref)

        prim = [_heads(ref[...]) for ref in (r_ref, lw_ref, k_ref, v_ref, a_ref, b_ref)]
        known = inv_ref[0]
        _, vjp = jax.vjp(lambda S0, *p: _chunk(S0, *p, inverse=known)[:2], ck_ref[0], *prim)
        grads = vjp((_heads(dy_ref[...]), ds_ref[...]))
        ds_ref[...] = grads[0]
        o_ref[...] = jnp.concatenate([_unheads(g) for g in grads[1:]], axis=1).astype(o_ref.dtype)

    h_in, h_in_specs, h_out_specs, h_out_shape, h_scratch = _hosted_args(hosted)
    col = lambda j: pl.BlockSpec((rows, RW), lambda i: (n - 1 - i, j))
    return pl.pallas_call(
        _hosting(body, hosted, 9, 1, 1, n), name=name, grid=(n,),
        in_specs=[col(j) for j in range(6)] + [pl.BlockSpec((1, NH, HD, HD), lambda i: (n - 1 - i, 0, 0, 0)),
                                               pl.BlockSpec((1, NH * nc, CHUNK, CHUNK), lambda i: (n - 1 - i, 0, 0, 0)),
                                               pl.BlockSpec((rows, RW), lambda i: (n - 1 - i, 0))] + h_in_specs,
        out_specs=[pl.BlockSpec((rows, 6 * RW), lambda i: (n - 1 - i, 0))] + h_out_specs,
        out_shape=[jax.ShapeDtypeStruct((T, 6 * RW), MXU)] + h_out_shape,
        scratch_shapes=[pltpu.VMEM((NH, HD, HD), F32)] + h_scratch,
        compiler_params=_cparams(("arbitrary",)),
    )(rw, rw, rw, rw, rw, rw, ck, inv, dy, *h_in)


ATTN_BLOCKS = (4, 2, 1)

def _attn_fwd(qkv, tab, qg, kg, sinks, *, name, hosted=None):
    T = qkv.shape[0]
    nb = _pick(T // BLK, ATTN_BLOCKS)
    n = T // (BLK * nb)
    h_in, h_in_specs, h_out_specs, h_out_shape, h_scratch = _hosted_args(hosted)

    def body(c_ref, p_ref, tc_ref, tp_ref, qg_ref, kg_ref, s_ref, o_ref):
        o_ref[...] = _attn_blocks(c_ref[...], p_ref[...], tc_ref[...], tp_ref[...], qg_ref[...], kg_ref[...],
                                  s_ref[...], pl.program_id(0) == 0).astype(o_ref.dtype)

    cur = lambda w: pl.BlockSpec((nb * BLK, w), lambda i: (i, 0))
    prev = lambda w: pl.BlockSpec((BLK, w), lambda i: (jnp.maximum(i * nb - 1, 0), 0))
    return pl.pallas_call(
        _hosting(body, hosted, 7, 1, 0, n), name=name, grid=(n,),
        in_specs=[cur(QKV_W), prev(QKV_W), cur(3 * 128), prev(3 * 128), _full_spec(qg), _full_spec(kg),
                  _full_spec(sinks)] + h_in_specs,
        out_specs=[cur(RW)] + h_out_specs, out_shape=[jax.ShapeDtypeStruct((T, RW), MXU)] + h_out_shape,
        scratch_shapes=h_scratch,
        compiler_params=_cparams(("arbitrary",)),
    )(qkv, qkv, tab, tab, qg, kg, sinks, *h_in)


def _attn_bwd(qkv, tab, qg, kg, sinks, dy, *, name, hosted=None):
    T = qkv.shape[0]
    nb = _pick(T // BLK, ATTN_BLOCKS)
    n = T // (BLK * nb)
    h_in, h_in_specs, h_out_specs, h_out_shape, h_scratch = _hosted_args(hosted)

    def body(c_ref, p_ref, tc_ref, tp_ref, qg_ref, kg_ref, s_ref, dy_ref, dqkv_ref, dqg_ref, dkg_ref, ds_ref, carry_ref):
        i = pl.program_id(0)

        @pl.when(i == 0)
        def _():
            carry_ref[...] = jnp.zeros_like(carry_ref)
            dqg_ref[...] = jnp.zeros_like(dqg_ref)
            dkg_ref[...] = jnp.zeros_like(dkg_ref)
            ds_ref[...] = jnp.zeros_like(ds_ref)

        tc, tp = tc_ref[...], tp_ref[...]
        f = lambda c, p_, qg_, kg_, sk: _attn_blocks(c, p_, tc, tp, qg_, kg_, sk, i == n - 1)
        _, vjp = jax.vjp(f, c_ref[...], p_ref[...], qg_ref[...], kg_ref[...], s_ref[...])
        dc, dp, dqg, dkg, dsk = vjp(dy_ref[...].astype(F32))
        last = slice((nb - 1) * BLK, nb * BLK)
        dqkv_ref[...] = dc.astype(dqkv_ref.dtype)
        dqkv_ref[last, :] = (dc[last] + carry_ref[...]).astype(dqkv_ref.dtype)
        carry_ref[...] = dp
        dqg_ref[...] += dqg
        dkg_ref[...] += dkg
        ds_ref[...] += dsk

    cur = lambda w: pl.BlockSpec((nb * BLK, w), lambda i: (n - 1 - i, 0))
    prev = lambda w: pl.BlockSpec((BLK, w), lambda i: (jnp.maximum((n - 1 - i) * nb - 1, 0), 0))
    return pl.pallas_call(
        _hosting(body, hosted, 8, 4, 1, n), name=name, grid=(n,),
        in_specs=[cur(QKV_W), prev(QKV_W), cur(3 * 128), prev(3 * 128), _full_spec(qg), _full_spec(kg), _full_spec(sinks),
                  cur(RW)] + h_in_specs,
        out_specs=[cur(QKV_W), _full_spec(qg), _full_spec(kg), _full_spec(sinks)] + h_out_specs,
        out_shape=[jax.ShapeDtypeStruct((T, QKV_W), MXU), jax.ShapeDtypeStruct(qg.shape, F32),
                   jax.ShapeDtypeStruct(kg.shape, F32), jax.ShapeDtypeStruct(sinks.shape, F32)] + h_out_shape,
        scratch_shapes=[pltpu.VMEM((BLK, QKV_W), F32)] + h_scratch,
        compiler_params=_cparams(("arbitrary",)),
    )(qkv, qkv, tab, tab, qg, kg, sinks, dy, *h_in)


def _shift_down(cur, prev8, i):
    rolled = pltpu.roll(cur, 1, 0)
    first_row = jnp.where(i > 0, prev8[7:8, :], 0.0)
    row = lax.broadcasted_iota(jnp.int32, cur.shape, 0)
    return jnp.where(row == 0, first_row, rolled)


def _shift_up(cur, next8, i, n):
    tm = cur.shape[0]
    rolled = pltpu.roll(cur, tm - 1, 0)
    last_row = jnp.where(i < n - 1, next8[0:1, :], 0.0)
    row = lax.broadcasted_iota(jnp.int32, cur.shape, 0)
    return jnp.where(row == tm - 1, last_row, rolled)


def _rope_table(positions):
    half = HD // 8
    inv_freq = 500000.0 ** (-jnp.arange(half, dtype=F32) / half)
    lane = jnp.arange(128) % HD
    rotary = lane < 2 * half
    freq = jnp.where(rotary, inv_freq[lane % half], 0.0)
    ang = positions.astype(F32)[:, None] * freq[None, :]
    cos, sin = jnp.cos(ang), jnp.sin(ang)
    return jnp.concatenate([jnp.where(rotary, cos, 1.0), jnp.where(lane < half, -sin, 0.0),
                            jnp.where(rotary & (lane >= half), sin, 0.0)], axis=1)


GATHER_BEHIND = {"f_proj_shift": [('ffn_w3', 512, 768)], "f_proj_gates": [('ffn_w3', 768, 1024)],
                 "f_prep": [('ffn_w2', 352, 704)],
                 "f_scan": [('ffn_w1', 0, 1024), ('w_branch_a', 0, 512), ('w_branch_b', 0, 512)],
                 "f_post": [('ffn_w3', 0, 512)], "f_attn": [('ffn_w2', 0, 352), ('w_out', 0, 256)]}
LATE = ['w_out', 'w_branch_a', 'w_branch_b', 'ffn_w1', 'ffn_w3', 'ffn_w2']
FFN = ('ffn_w1', 'ffn_w3', 'ffn_w2')
BACK_ATTN = ['w_out', 'w_branch_a', 'w_branch_b', 'ffn_w2']
BACK_SCAN = ['ffn_w1', 'ffn_w3']
BACK_LAST = ['w_in', 'decay_up', 'iclr_up', 'gate_up']


def _full_weight(g, ax):
    return g.reshape(-1, g.shape[2]) if ax == 0 else jnp.concatenate([g[j] for j in range(4)], axis=1)


def _local_step(x, target, ada, tab, w, s, shards=None):
    T = x.shape[0]
    tm = _pick(T, (512, 256, 128))
    tm_wide = _pick(T, (256, 128))
    tm_vjp = _pick(T, (256, 128))
    row = lambda n, dt=F32: (n, dt, 'row')
    acc = lambda n, r=1: (n, F32, r)

    def f_norm1(x_, g, ada_):
        return _norm_mod(x_, g, ada_[:, D:2 * D], ada_[:, 0:D])

    landed = {}

    def behind(kernel_name):
        if not shards:
            return None
        return _GatherChips([shards[n] if hi - lo == shards[n].shape[0] else shards[n][lo:hi]
                             for n, lo, hi in GATHER_BEHIND[kernel_name]])

    def took(kernel_name, got):
        landed.update(zip(GATHER_BEHIND[kernel_name], got))

    def mm_behind(a_, w_, kernel_name, **kw):
        ex = behind(kernel_name)
        res = _mm_nn(a_, w_, name=kernel_name, hosted=ex, **kw)
        if ex:
            took(kernel_name, res[1:])
            return res[0]
        return res

    h1, proj, *got = _mm_then([(None, w['w_in'][:, :SHIFT_W], 'nn')], lambda i, n, y, h, *_: (h, y), [x],
                              [s['norm1_gain'], ada], [row(D, MXU), row(SHIFT_W)], tm=tm, name="f_proj_shift",
                              lhs_fn=f_norm1, hosted=behind("f_proj_shift"))
    took("f_proj_shift", got)
    proj_qkv = _mm_nn(h1, w['w_in'][:, SHIFT_W:SHIFT_W + QKV_W], name="f_proj_qkv")
    proj_g = mm_behind(h1, w['w_in'][:, SHIFT_W + QKV_W:], "f_proj_gates", out_dtype=MXU)
    prep_consts = [s['decay_w0'], s['lora_up'], s['iclr_a0'], s['gate_up'], s['k_k'], s['k_a']]

    def f_prep(i, n, cur, prev8, mu, *params):
        mixed = cur + (_shift_down(cur, prev8, i) - cur) * mu
        return (_prep(mixed, *params),)
    rw, *got = _rowwise(f_prep, [(proj, SHIFT_W)], [s['tshift_mu']] + prep_consts, [row(7 * RW)], tm=tm_wide,
                        name="f_prep", halo=[(proj, SHIFT_W, 'prev')], hosted=behind("f_prep"))
    took("f_prep", got)
    y, ck, inv, *got = _scan_fwd(rw, name="f_scan", hosted=behind("f_scan"))
    took("f_scan", got)
    post_consts = [s['lnx_gain'], s['lnx_bias'], s['r_k']]

    rkvg = [(rw, RW, j) for j in (0, 2, 3, 6)]

    def f_post(i, n, *args):
        return (_post(*args),)
    ya, *got = _rowwise(f_post, [y] + rkvg, post_consts, [row(RW, MXU)], tm=tm_wide, name="f_post",
                        hosted=behind("f_post"))
    took("f_post", got)
    yb, *got = _attn_fwd(proj_qkv, tab, s['q_norm_gain'], s['k_norm_gain'], s['attn_sinks'], name="f_attn",
                         hosted=behind("f_attn"))
    took("f_attn", got)
    w = dict(w)
    if shards:
        ax = dict(SHARDED)
        for n in LATE:
            rows = [landed[key] for key in sorted(k_ for k_ in landed if k_[0] == n)]
            blocks = rows[0] if len(rows) == 1 else jnp.concatenate(rows, axis=1)
            w[n] = blocks if n in FFN else _full_weight(blocks, ax[n])
    else:
        fs = DFF // 4
        w.update({n: w[n].reshape(D, 4, fs).transpose(1, 0, 2) for n in ('ffn_w1', 'ffn_w3')})
        w['ffn_w2'] = w['ffn_w2'].reshape(4, fs, D)
    def f_merge(pg, ya_, yb_, x_, wa, wb, bias, g, ada_):
        ma_ = jnp.dot(ya_, wa, preferred_element_type=F32).astype(MXU)
        mb_ = jnp.dot(yb_, wb, preferred_element_type=F32).astype(MXU)
        return _merge(pg.astype(F32), ma_.astype(F32), mb_.astype(F32), bias), ma_, mb_

    def f_res1(i, n, mo_, merged_, ma_, mb_, pg, ya_, yb_, x_, wa, wb, bias, g, ada_):
        x1_ = x_ + ada_[:, 2 * D:3 * D] * mo_
        return merged_, ma_, mb_, mo_, x1_, _norm_mod(x1_, g, ada_[:, 4 * D:5 * D], ada_[:, 3 * D:4 * D])
    merged, ma, mb, mo, x1, h2 = _mm_then(
        [(None, w['w_out'], 'nn')], f_res1, [proj_g, ya, yb, x],
        [w['w_branch_a'], w['w_branch_b'], s['branch_gate_b'], s['norm2_gain'], ada],
        [row(D, MXU), row(D, MXU), row(D, MXU), row(D), row(D), row(D, MXU)], tm=tm, name="f_out", lhs_fn=f_merge)
    u, v, act = _ffn_in(h2, w['ffn_w1'], w['ffn_w3'], name="f_ffn_in")

    def f_loss(i, n, ff_, x1_, tgt, ada_):
        g2 = ada_[:, 5 * D:6 * D]
        err = x1_ + g2 * ff_ - tgt
        dx2 = err * (1.0 / D)
        loss = 0.5 * jnp.sum(jnp.sum(err * err, axis=1, keepdims=True) * (1.0 / D), axis=0, keepdims=True)
        return dx2, (dx2 * g2), jnp.broadcast_to(loss, (1, 128)), jnp.sum(dx2 * ff_, axis=0, keepdims=True)
    dx2, dff, loss, dgate2 = _mm_then([(act, w['ffn_w2'], 'nn', j) for j in range(4)], f_loss, [x1, target], [ada],
                                      [row(D), row(D, MXU), acc(128), acc(D)], tm=tm, name="f_ffn_out")

    du, dv = _ffn_act_bwd(dff, w['ffn_w2'], u, v, name="b_ffn_out_dx")
    g_w2 = _mm_tn_blocks(act, dff, name="b_ffn_out_dw", out_dtype=MXU)
    g_w1 = _mm_tn_blocks(h2, du, name="b_ffn_w1_dw", out_dtype=MXU)
    g_w3 = _mm_tn_blocks(h2, dv, name="b_ffn_w3_dw", out_dtype=MXU)

    def b_res1(i, n, dh2_, x1_, dx2_, mo_, g, ada_):
        _, vjp = jax.vjp(_norm_mod, x1_, g, ada_[:, 4 * D:5 * D], ada_[:, 3 * D:4 * D])
        dxn, dg, dsc, dsh = vjp(dh2_)
        dx1_ = dxn + dx2_
        g1 = ada_[:, 2 * D:3 * D]
        return dx1_, dx1_ * g1, dg, dsc, dsh, jnp.sum(dx1_ * mo_, axis=0, keepdims=True)
    dx1, dmo, d_gain2, d_scale2, d_shift2, dgate1 = _mm_then(
        [(t, w[n], 'nt', j) for t, n in ((du, 'ffn_w1'), (dv, 'ffn_w3')) for j in range(4)], b_res1, [x1, dx2, mo],
        [s['norm2_gain'], ada],
        [row(D), row(D, MXU), acc(D), acc(D), acc(D), acc(D)], tm=tm_wide, name="b_ffn_in_dx")
    g_wout = _mm_tn(merged, dmo, name="b_out_dw", out_dtype=MXU)

    def b_merge(i, n, dm, pg, ma_, mb_, wa, wb, bias):
        _, vjp = jax.vjp(_merge, pg.astype(F32), ma_.astype(F32), mb_.astype(F32), bias)
        dpg, dma_, dmb_, dbias = vjp(dm)
        dma_, dmb_ = dma_.astype(MXU), dmb_.astype(MXU)
        nt = lambda a_, b_: lax.dot_general(a_, b_, (((1,), (1,)), ((), ())), preferred_element_type=F32)
        return dpg, dma_, dmb_, nt(dma_, wa), nt(dmb_, wb), dbias
    dpg, dma, dmb, dya, dyb, d_bias = _mm_then(
        [(dmo, w['w_out'], 'nt')], b_merge, [proj_g, ma, mb], [w['w_branch_a'], w['w_branch_b'], s['branch_gate_b']],
        [row(GATE_W, MXU), row(D, MXU), row(D, MXU), row(RW), row(RW), acc(GATE_W)], tm=tm, name="b_out_dx")
    g_wa = _mm_tn(ya, dma, name="b_branch_a_dw", out_dtype=MXU, col_shards=4)
    g_wb = _mm_tn(yb, dmb, name="b_branch_b_dw", out_dtype=MXU, col_shards=4)
    gw = dict(w_branch_a=g_wa, w_branch_b=g_wb, w_out=g_wout.reshape(4, D // 4, D), ffn_w1=g_w1, ffn_w3=g_w3,
              ffn_w2=g_w2)
    recv = {}
    dqkv, d_qg, d_kg, d_sinks, *got = _attn_bwd(
        proj_qkv, tab, s['q_norm_gain'], s['k_norm_gain'], s['attn_sinks'], dyb, name="b_attn",
        hosted=shards and _ScatterChips([gw[n] for n in BACK_ATTN]))
    recv.update(zip(BACK_ATTN, got))

    def b_post(i, n, y_, r_, k_, v_, g_, dya_, *params):
        _, vjp = jax.vjp(_post, y_, r_, k_, v_, g_, *params)
        dy_, dr_, dk_, dv_, dg_, *dparams = vjp(dya_)
        return (dy_, jnp.concatenate([dr_, dk_, dv_, dg_], axis=1), *dparams)
    dy, drkvg, d_lnx_gain, d_lnx_bias, d_r_k = _rowwise(
        b_post, [y] + rkvg + [dya], post_consts, [row(RW), row(4 * RW, MXU), acc(RW), acc(RW), acc(RW)], tm=tm_wide,
        name="b_post")
    dscan, *got = _scan_bwd(rw, ck, inv, dy, name="b_scan",
                            hosted=shards and _ScatterChips([gw[n] for n in BACK_SCAN]))
    recv.update(zip(BACK_SCAN, got))

    def b_prep(i, n, cur, drw_, dscan_, prev8, mu, *params):
        shifted = _shift_down(cur, prev8, i)
        mixed = cur + (shifted - cur) * mu
        _, vjp = jax.vjp(_prep, mixed, *params)
        blk = lambda t, j: t[:, j * RW:(j + 1) * RW].astype(F32)
        ct = jnp.concatenate([blk(dscan_, 0) + blk(drw_, 0), blk(dscan_, 1), blk(dscan_, 2) + blk(drw_, 1),
                              blk(dscan_, 3) + blk(drw_, 2), blk(dscan_, 4), blk(dscan_, 5), blk(drw_, 3)], axis=1)
        grads = vjp(ct)
        dmixed = grads[0]
        return (dmixed, jnp.sum(dmixed * (shifted - cur), axis=0, keepdims=True)) + tuple(grads[1:])
    dmixed, d_mu, d_w0, d_lora, d_a0, d_gate_up, d_kk, d_ka = _rowwise(
        b_prep, [(proj, SHIFT_W), drkvg, dscan], [s['tshift_mu']] + prep_consts,
        [row(SHIFT_W), acc(SHIFT_W), acc(RW), acc(2 * RW, 128), acc(RW), acc(RW, 128), acc(RW), acc(RW)],
        tm=tm_vjp, name="b_prep", halo=[(proj, SHIFT_W, 'prev')])

    def b_gather(i, n, dm, dqkv_, dpg_, next8, mu):
        dcur = dm * (1.0 - mu) + _shift_up(dm, next8, i, n) * mu
        return (jnp.concatenate([dcur.astype(MXU), dqkv_, dpg_], axis=1),)
    (dproj,) = _rowwise(b_gather, [dmixed, dqkv, dpg], [s['tshift_mu']], [row(IN_W, MXU)], tm=tm_wide, name="b_gather",
                        halo=[(dmixed, SHIFT_W, 'next')])
    g_win = _mm_tn(h1, dproj, name="b_proj_dw", out_dtype=MXU, col_shards=4)

    def col_blocks(g):
        k, n = g.shape
        return g.reshape(k, 4, n // 4).transpose(1, 0, 2).astype(MXU)
    gw.update(w_in=g_win, decay_up=col_blocks(d_lora[:64, :RW]), iclr_up=col_blocks(d_lora[64:, RW:]),
              gate_up=col_blocks(d_gate_up))
    top, bottom = None, None
    if shards:
        top = _ScatterChips([gw['w_in'][:, :D // 2]] + [gw[n] for n in BACK_LAST[1:]])
        bottom = _ScatterChips([gw['w_in'][:, D // 2:3 * D // 4]])
        dh1, *got_top = _mm_nt(dproj, w['w_in'], name="b_proj_dx", hosted=top)
    else:
        dh1 = _mm_nt(dproj, w['w_in'], name="b_proj_dx")

    def b_norm1(i, n, x_, dh1_, dx1_, g, ada_):
        _, vjp = jax.vjp(_norm_mod, x_, g, ada_[:, D:2 * D], ada_[:, 0:D])
        dxn, dg, dsc, dsh = vjp(dh1_)
        return dxn + dx1_, dg, dsc, dsh
    dx, d_gain1, d_scale1, d_shift1, *got_bottom = _rowwise(
        b_norm1, [x, dh1, dx1], [s['norm1_gain'], ada], [row(D), acc(D), acc(D), acc(D)], tm=tm, name="b_norm1",
        hosted=bottom)
    if shards:
        recv.update(zip(BACK_LAST[1:], got_top[1:]))
        recv['w_in'] = [got_top[0], got_bottom[0]]

    d_ada = jnp.concatenate([d_shift1, d_scale1, dgate1, d_shift2, d_scale2, dgate2], axis=1)
    gs = dict(norm1_gain=d_gain1, norm2_gain=d_gain2, tshift_mu=d_mu, decay_w0=d_w0, iclr_a0=d_a0, k_k=d_kk, k_a=d_ka,
              r_k=d_r_k, lnx_gain=d_lnx_gain, lnx_bias=d_lnx_bias, q_norm_gain=d_qg, k_norm_gain=d_kg,
              attn_sinks=d_sinks, branch_gate_b=d_bias)
    return loss, dx, d_ada, gw, gs, recv


ANY = pl.BlockSpec(memory_space=pl.ANY)


def _place():
    x, y, c = lax.axis_index("x"), lax.axis_index("y"), lax.axis_index("c")
    return x, y, c, [(1 - x, y), (x, 1 - y), (1 - x, 1 - y)]


def _all_gather8(x_shard, *, name):
    m_per, n = x_shard.shape

    def body(x_ref, out_ref, send_sems, recv_sems, local_sem):
        x, y, c, chips = _place()
        me, sibling = (x, y, c), (x, y, 1 - c)

        def rows(px, py, pc):
            return out_ref.at[pl.ds((4 * px + 2 * py + pc) * m_per, m_per), :]

        def copy(k, block, to, src=None):
            return pltpu.make_async_remote_copy(
                src_ref=rows(*block) if src is None else src, dst_ref=rows(*block),
                send_sem=send_sems.at[k], recv_sem=recv_sems.at[k], device_id=to, device_id_type=MESH)

        mine = pltpu.make_async_copy(x_ref, rows(*me), local_sem)
        mine.start()
        first = [copy(0, me, sibling, src=x_ref)]
        first += [copy(1 + j, me, (*chip, c), src=x_ref) for j, chip in enumerate(chips)]
        for cp in first:
            cp.start()
        passed = [copy(4 + j, (*chip, c), sibling) for j, chip in enumerate(chips)]
        for j, chip in enumerate(chips):
            copy(1 + j, (*chip, c), me).wait_recv()
            passed[j].start()
        copy(0, sibling, me).wait_recv()
        for j, chip in enumerate(chips):
            copy(4 + j, (*chip, 1 - c), me).wait_recv()
        for cp in first + passed:
            cp.wait_send()
        mine.wait()

    return pl.pallas_call(
        body, name=name, out_shape=jax.ShapeDtypeStruct((8 * m_per, n), x_shard.dtype),
        in_specs=[pl.BlockSpec(memory_space=pltpu.VMEM)], out_specs=pl.BlockSpec(memory_space=pltpu.VMEM),
        scratch_shapes=[pltpu.SemaphoreType.DMA((7,)), pltpu.SemaphoreType.DMA((7,)), pltpu.SemaphoreType.DMA],
    )(x_shard)


class _GatherChips:
    def __init__(self, shards):
        n = len(shards)
        self.arrays, self.n_in, self.n_out = list(shards), n, n
        self.out_shape = [jax.ShapeDtypeStruct((4,) + s.shape, s.dtype) for s in shards]
        self.scratch = [pltpu.SemaphoreType.DMA((3 * n,)), pltpu.SemaphoreType.DMA((3 * n,)),
                        pltpu.SemaphoreType.DMA((n,))]

    def _copies(self, x_refs, out_refs, sems, receiving):
        send_sems, recv_sems, local_sems = sems
        x, y, c, chips = _place()
        s_me = 2 * x + y
        n = self.n_in

        def copy(a, k, s):
            return pltpu.make_async_remote_copy(
                src_ref=x_refs[a], dst_ref=out_refs[a].at[s], send_sem=send_sems.at[3 * a + k],
                recv_sem=recv_sems.at[3 * a + k], device_id=(*chips[k], c), device_id_type=MESH)

        mine = [pltpu.make_async_copy(x_refs[a], out_refs[a].at[s_me], local_sems.at[a]) for a in range(n)]
        sends = [copy(a, k, s_me) for a in range(n) for k in range(3)]
        if not receiving:
            return mine, sends
        return mine, sends, [copy(a, k, 2 * px + py) for a in range(n) for k, (px, py) in enumerate(chips)]

    def start(self, x_refs, out_refs, sems):
        mine, sends = self._copies(x_refs, out_refs, sems, False)
        for cp in mine + sends:
            cp.start()

    def wait(self, x_refs, out_refs, sems):
        mine, sends, recvs = self._copies(x_refs, out_refs, sems, True)
        for cp in recvs:
            cp.wait_recv()
        for cp in sends:
            cp.wait_send()
        for cp in mine:
            cp.wait()


class _GatherChipsHalved(_GatherChips):
    def __init__(self, shards):
        super().__init__(shards)
        n = self.n_in
        self.scratch = [pltpu.SemaphoreType.DMA((6 * n,)), pltpu.SemaphoreType.DMA((6 * n,)),
                        pltpu.SemaphoreType.DMA((n,))]

    def _copies(self, x_refs, out_refs, sems, receiving):
        send_sems, recv_sems, local_sems = sems
        x, y, c, chips = _place()
        s_me = 2 * x + y
        n = self.n_in

        def half(a, who):
            rows = x_refs[a].shape[0] // 2
            return pl.ds(who * rows, rows)

        def over_chips(a, k, s):
            return pltpu.make_async_remote_copy(
                src_ref=x_refs[a].at[half(a, c)], dst_ref=out_refs[a].at[s, half(a, c)],
                send_sem=send_sems.at[3 * a + k], recv_sem=recv_sems.at[3 * a + k],
                device_id=(*chips[k], c), device_id_type=MESH)

        def to_sibling(a, k, s, who):
            return pltpu.make_async_remote_copy(
                src_ref=out_refs[a].at[s, half(a, who)], dst_ref=out_refs[a].at[s, half(a, who)],
                send_sem=send_sems.at[3 * n + 3 * a + k], recv_sem=recv_sems.at[3 * n + 3 * a + k],
                device_id=(x, y, 1 - c), device_id_type=MESH)

        mine = [pltpu.make_async_copy(x_refs[a], out_refs[a].at[s_me], local_sems.at[a]) for a in range(n)]
        sends = [over_chips(a, k, s_me) for a in range(n) for k in range(3)]
        if not receiving:
            return mine, sends
        pairs = [(a, k, 2 * px + py) for a in range(n) for k, (px, py) in enumerate(chips)]
        landed = [over_chips(a, k, s) for a, k, s in pairs]
        passed_on = [to_sibling(a, k, s, c) for a, k, s in pairs]
        from_sibling = [to_sibling(a, k, s, 1 - c) for a, k, s in pairs]
        return mine, sends, landed, passed_on, from_sibling

    def wait(self, x_refs, out_refs, sems):
        mine, sends, landed, passed_on, from_sibling = self._copies(x_refs, out_refs, sems, True)
        for got, fwd in zip(landed, passed_on, strict=True):
            got.wait_recv()
            fwd.start()
        for cp in from_sibling:
            cp.wait_recv()
        for cp in sends + passed_on:
            cp.wait_send()
        for cp in mine:
            cp.wait()


class _ScatterChips:
    def __init__(self, parts):
        n = len(parts)
        self.arrays, self.n_in, self.n_out = list(parts), n, n
        self.out_shape = [jax.ShapeDtypeStruct((3,) + p.shape[1:], p.dtype) for p in parts]
        self.scratch = [pltpu.SemaphoreType.DMA((3 * n,)), pltpu.SemaphoreType.DMA((3 * n,))]

    def _copies(self, g_refs, out_refs, sems):
        send_sems, recv_sems = sems
        x, y, c, chips = _place()
        return [pltpu.make_async_remote_copy(
            src_ref=g_refs[a].at[2 * px + py], dst_ref=out_refs[a].at[k], send_sem=send_sems.at[3 * a + k],
            recv_sem=recv_sems.at[3 * a + k], device_id=(px, py, c), device_id_type=MESH)
            for a in range(self.n_in) for k, (px, py) in enumerate(chips)]

    def start(self, g_refs, out_refs, sems):
        for cp in self._copies(g_refs, out_refs, sems):
            cp.start()

    def wait(self, g_refs, out_refs, sems):
        sends = self._copies(g_refs, out_refs, sems)
        for cp in sends:
            cp.wait_recv()
        for cp in sends:
            cp.wait_send()


def _exchange_call(ex, *, name):
    def body(*refs):
        parts = (refs[:ex.n_in], refs[ex.n_in:ex.n_in + ex.n_out], refs[ex.n_in + ex.n_out:])
        ex.start(*parts)
        ex.wait(*parts)

    return pl.pallas_call(body, name=name, out_shape=ex.out_shape, in_specs=[ANY] * ex.n_in,
                          out_specs=[ANY] * ex.n_out, scratch_shapes=ex.scratch)(*ex.arrays)


class _SwapSibling:
    def __init__(self, vs):
        n = len(vs)
        self.arrays, self.n_in, self.n_out = list(vs), n, n
        self.out_shape = [jax.ShapeDtypeStruct(v.shape, v.dtype) for v in vs]
        self.scratch = [pltpu.SemaphoreType.DMA((n,)), pltpu.SemaphoreType.DMA((n,))]

    def _copies(self, v_refs, out_refs, sems):
        send_sems, recv_sems = sems
        x, y, c, _ = _place()
        return [pltpu.make_async_remote_copy(src_ref=v_refs[a], dst_ref=out_refs[a], send_sem=send_sems.at[a],
                                             recv_sem=recv_sems.at[a], device_id=(x, y, 1 - c), device_id_type=MESH)
                for a in range(self.n_in)]

    def start(self, v_refs, out_refs, sems):
        for cp in self._copies(v_refs, out_refs, sems):
            cp.start()

    def wait(self, v_refs, out_refs, sems):
        for cp in self._copies(v_refs, out_refs, sems):
            cp.wait()


class _Both:
    def __init__(self, first, second):
        self.parts = (first, second)
        self.arrays = first.arrays + second.arrays
        self.n_in, self.n_out = first.n_in + second.n_in, first.n_out + second.n_out
        self.out_shape = first.out_shape + second.out_shape
        self.scratch = first.scratch + second.scratch

    def _split(self, in_refs, out_refs, sems):
        a, b = self.parts
        return ((a, in_refs[:a.n_in], out_refs[:a.n_out], sems[:len(a.scratch)]),
                (b, in_refs[a.n_in:], out_refs[a.n_out:], sems[len(a.scratch):]))

    def start(self, in_refs, out_refs, sems):
        for ex, *refs in self._split(in_refs, out_refs, sems):
            ex.start(*refs)

    def wait(self, in_refs, out_refs, sems):
        for ex, *refs in self._split(in_refs, out_refs, sems):
            ex.wait(*refs)


def _sum_parts(own, others, *, name):
    R, C = own.shape
    tm = _pick(R, (256, 128, 64))

    def body(own_ref, o0_ref, o1_ref, o2_ref, out_ref):
        tot = own_ref[...].astype(F32)
        for ref in (o0_ref, o1_ref, o2_ref):
            tot = tot + ref[...].astype(F32)
        out_ref[...] = tot

    part = lambda k: pl.BlockSpec((None, tm, C), lambda i: (k, i, 0))
    return pl.pallas_call(
        body, name=name, grid=(R // tm,),
        in_specs=[pl.BlockSpec((tm, C), lambda i: (i, 0)), part(0), part(1), part(2)],
        out_specs=pl.BlockSpec((tm, C), lambda i: (i, 0)), out_shape=jax.ShapeDtypeStruct((R, C), F32),
        compiler_params=_cparams(("arbitrary",)),
    )(own, others, others, others)


def _adam_math(w_, m_, v_, g):
    m2 = ADAM_B1 * m_ + (1.0 - ADAM_B1) * g
    v2 = ADAM_B2 * v_ + (1.0 - ADAM_B2) * jnp.square(g)
    m_hat = m2 / (1.0 - ADAM_B1 ** ADAM_STEP)
    v_hat = v2 / (1.0 - ADAM_B2 ** ADAM_STEP)
    delta = -ADAM_LR * (m_hat / (jnp.sqrt(v_hat) + ADAM_EPS) + ADAM_WD * w_)
    return delta, m2, v2


SMALL_SLOTS = 24
SMALL_COLS = 2 * D


def _small_rows(widths):
    firsts, row = [], 0
    for n_i in widths:
        firsts.append(row)
        row += -(-n_i // SMALL_COLS)
    assert row <= SMALL_SLOTS
    return firsts


def _pack_small(grads, *, name):
    n = len(grads)
    firsts = _small_rows([g.shape[1] for g in grads])

    def body(*refs):
        out_ref = refs[n]
        out_ref[...] = jnp.zeros_like(out_ref)
        for first, ref in zip(firsts, refs[:n], strict=True):
            for lo in range(0, ref.shape[1], SMALL_COLS):
                width = min(SMALL_COLS, ref.shape[1] - lo)
                row = first + lo // SMALL_COLS
                out_ref[row:row + 1, 0:width] = ref[:, lo:lo + width]

    return pl.pallas_call(body, name=name, out_shape=jax.ShapeDtypeStruct((SMALL_SLOTS, SMALL_COLS), F32))(*grads)


def _adamw_small(ws, ms, vs, gathered, *, name):
    n = len(ws)
    firsts = _small_rows([w.shape[1] for w in ws] + [128])

    def total(g_ref, first, nc):
        pieces = []
        for lo in range(0, nc, SMALL_COLS):
            width, row = min(SMALL_COLS, nc - lo), first + lo // SMALL_COLS
            g = g_ref[row:row + 1, 0:width]
            for d in range(1, 8):
                g = g + g_ref[d * SMALL_SLOTS + row:d * SMALL_SLOTS + row + 1, 0:width]
            pieces.append(g)
        return pieces[0] if len(pieces) == 1 else jnp.concatenate(pieces, axis=1)

    def body(*refs):
        w_refs, m_refs, v_refs, g_ref = refs[:n], refs[n:2 * n], refs[2 * n:3 * n], refs[3 * n]
        outs = refs[3 * n + 1:]
        for i in range(n):
            g = total(g_ref, firsts[i], w_refs[i].shape[1])
            delta, m2, v2 = _adam_math(w_refs[i][...], m_refs[i][...], v_refs[i][...], g)
            for k, val in enumerate((g, delta, m2, v2)):
                outs[k * n + i][...] = val
        outs[4 * n][...] = total(g_ref, firsts[n], 128)

    shapes = [jax.ShapeDtypeStruct(w.shape, F32) for w in ws]
    res = pl.pallas_call(body, name=name, out_shape=shapes * 4 + [jax.ShapeDtypeStruct((1, 128), F32)],
                         compiler_params=pltpu.CompilerParams(vmem_limit_bytes=VMEM_LIMIT))(*ws, *ms, *vs, gathered)
    return [res[k * n:(k + 1) * n] for k in range(4)], res[4 * n]


def _adamw(w, m, v, gparts, *, tm, name, hosted=None):
    def fn(i, n, w_, m_, v_, *gs):
        g = gs[0]
        for p in gs[1:]:
            g = g + p
        return (g,) + _adam_math(w_, m_, v_, g)
    nc = w.shape[1]
    return _rowwise(fn, [w, m, v] + list(gparts), [], [(nc, F32, 'row')] * 4, tm=tm, name=name, hosted=hosted)


WEIGHTS = ['ada_w', 'ada_b', 'norm1_gain', 'norm2_gain', 'w_in', 'tshift_mu', 'decay_w0', 'decay_up', 'iclr_a0',
           'iclr_up', 'gate_up', 'k_k', 'k_a', 'r_k', 'lnx_gain', 'lnx_bias', 'q_norm_gain', 'k_norm_gain', 'attn_sinks',
           'branch_gate_b', 'w_branch_a', 'w_branch_b', 'w_out', 'ffn_w1', 'ffn_w3', 'ffn_w2']
SHARDED = [('w_in', 1), ('decay_up', 1), ('iclr_up', 1), ('gate_up', 1), ('w_branch_a', 1), ('w_branch_b', 1),
           ('w_out', 0), ('ffn_w1', 1), ('ffn_w3', 1), ('ffn_w2', 0)]
SMALL = ['ada_b', 'norm1_gain', 'norm2_gain', 'tshift_mu', 'decay_w0', 'iclr_a0', 'k_k', 'k_a', 'r_k', 'lnx_gain',
         'lnx_bias', 'q_norm_gain', 'k_norm_gain', 'attn_sinks', 'branch_gate_b']


def kernel(x, c, positions, ada_w, ada_b, norm1_gain, norm2_gain, w_in, tshift_mu, decay_w0, decay_up, iclr_a0, iclr_up, gate_up, k_k, k_a, r_k, lnx_gain, lnx_bias, q_norm_gain, k_norm_gain, attn_sinks, branch_gate_b, w_branch_a, w_branch_b, w_out, ffn_w1, ffn_w3, ffn_w2, loss_target, m_ada_w, m_ada_b, m_norm1_gain, m_norm2_gain, m_w_in, m_tshift_mu, m_decay_w0, m_decay_up, m_iclr_a0, m_iclr_up, m_gate_up, m_k_k, m_k_a, m_r_k, m_lnx_gain, m_lnx_bias, m_q_norm_gain, m_k_norm_gain, m_attn_sinks, m_branch_gate_b, m_w_branch_a, m_w_branch_b, m_w_out, m_ffn_w1, m_ffn_w3, m_ffn_w2, v_ada_w, v_ada_b, v_norm1_gain, v_norm2_gain, v_w_in, v_tshift_mu, v_decay_w0, v_decay_up, v_iclr_a0, v_iclr_up, v_gate_up, v_k_k, v_k_a, v_r_k, v_lnx_gain, v_lnx_bias, v_q_norm_gain, v_k_norm_gain, v_attn_sinks, v_branch_gate_b, v_w_branch_a, v_w_branch_b, v_w_out, v_ffn_w1, v_ffn_w3, v_ffn_w2):
    a = dict(locals())
    W = {n: a[n] for n in WEIGHTS}
    M = {n: a['m_' + n] for n in WEIGHTS}
    V = {n: a['v_' + n] for n in WEIGHTS}
    xi, yi, ci = lax.axis_index("x"), lax.axis_index("y"), lax.axis_index("c")
    me = 4 * xi + 2 * yi + ci
    shard = 2 * xi + yi
    mat = lambda t: t.reshape(t.shape[-2], t.shape[-1])
    sharded = [n for n, _ in SHARDED]

    ax = dict(SHARDED)
    late = LATE
    early = [n for n in sharded if n not in late]
    shards = {n: mat(W[n]).astype(MXU) for n in sharded}
    gathered = _exchange_call(_GatherChipsHalved([shards[n] for n in early]), name="gather_weights")
    full = {n: _full_weight(g, ax[n]) for n, g in zip(early, gathered, strict=True)}

    c_all = _all_gather8(jnp.broadcast_to(c, (8, D)), name="gather_c")[0::8]
    pad_rows = lambda t: jnp.concatenate([t, jnp.zeros((BLK - 8, t.shape[1]), t.dtype)])
    c_all = pad_rows(c_all.astype(MXU))
    ada_cols = _mm_nn(c_all, mat(ada_w).astype(MXU), name="f_ada")[:8]
    ada_all = _all_gather8(ada_cols, name="gather_ada").reshape(2, 2, 2, 8, 6 * D // 4)
    ada_mine = lax.dynamic_index_in_dim(ada_all[:, :, 0], me, axis=2, keepdims=False)
    ada = ada_mine.reshape(1, 6 * D) + mat(ada_b)

    zero = jnp.zeros((64, RW), MXU)
    lora = jnp.concatenate([jnp.concatenate([full['decay_up'], zero], axis=1),
                            jnp.concatenate([zero, full['iclr_up']], axis=1)], axis=0)
    s = {n: W[n].reshape(1, -1) for n in SMALL if n != 'ada_b'}
    s['lora_up'] = lora.astype(F32)
    s['gate_up'] = full['gate_up'].astype(F32)
    tab = _rope_table(positions.reshape(-1))
    loss, dx, d_ada, gw, gs, from_chips = _local_step(x[0], loss_target[0], ada, tab, dict(w_in=full['w_in']), s,
                                                      shards={n: shards[n] for n in late})

    gs['ada_b'] = d_ada
    gsmall = _pack_small([gs[n] for n in SMALL] + [loss], name="pack_small_grads")
    gsmall_all = _all_gather8(gsmall, name="gather_small_grads")
    row = lambda src: [src[n].reshape(1, -1) for n in SMALL]
    sm_out, loss = _adamw_small(row(W), row(M), row(V), gsmall_all, name="adamw_small")
    sm_out = [{n: o.reshape(W[n].shape) for n, o in zip(SMALL, outs_k, strict=True)} for outs_k in sm_out]
    loss = loss[0, 0]

    ada_rows = 6 * D // SMALL_COLS
    d_ada_all = gsmall_all.reshape(8, SMALL_SLOTS, SMALL_COLS)[:, :ada_rows].reshape(8, 6 * D)
    d_ada_cols = lax.dynamic_slice_in_dim(d_ada_all, shard * (6 * D // 4), 6 * D // 4, axis=1)
    g_ada_w = _mm_tn(c_all, pad_rows(d_ada_cols.astype(MXU)), name="b_ada")

    rest = [n for n in sharded if n != 'w_in']
    parts = {n: _sum_parts(lax.dynamic_index_in_dim(gw[n], shard, axis=0, keepdims=False), from_chips[n],
                           name="sum_" + n) for n in rest}
    tail = _Both(_ScatterChips([gw['w_in'][:, 3 * D // 4:]]), _SwapSibling([parts[n] for n in rest]))
    res = _adamw(mat(ada_w), mat(m_ada_w), mat(v_ada_w), [g_ada_w], tm=256, name="adamw_ada", hosted=tail)
    ada_out, last_quarter, others = res[:4], res[4], dict(zip(rest, res[5:], strict=True))
    parts['w_in'] = _sum_parts(lax.dynamic_index_in_dim(gw['w_in'], shard, axis=0, keepdims=False),
                               jnp.concatenate(from_chips['w_in'] + [last_quarter], axis=1), name="sum_w_in")
    others['w_in'] = _exchange_call(_SwapSibling([parts['w_in']]), name="swap_w_in")[0]
    sh_out = {}
    for n in sharded:
        part, other = parts[n], others[n]
        sh_out[n] = _adamw(mat(W[n]), mat(M[n]), mat(V[n]), [part, other], tm=_pick(part.shape[0], (256, 128, 64)),
                           name="adamw_" + n)

    def leaf(k, n):
        if n == 'ada_w':
            return ada_out[k].reshape(W[n].shape)
        if n in sharded:
            return sh_out[n][k].reshape(W[n].shape)
        return sm_out[k][n]
    outs = [leaf(k, n) for k in range(4) for n in WEIGHTS]
    return (loss, dx[None], *outs)
```

```python
import functools
import math

import jax
import jax.numpy as jnp
from jax import lax
from jax.experimental import pallas as pl
from jax.experimental.pallas import tpu as pltpu

F32 = jnp.float32
BF16 = jnp.bfloat16
MXU = BF16
HI = lax.Precision.HIGHEST

D = 1024
HD = 64
NH = 8
RW = NH * HD
SHIFT_W = 3 * RW + 64 + 64 + 128
QKV_W = RW + 2 * 128
GATE_W = 2 * D
IN_W = SHIFT_W + QKV_W + GATE_W
DFF = 2816
BLK = 128
CHUNK = 64
RMS_EPS = 1e-6
GN_EPS = 64e-5
NEG_INF = -1e30
ADAM_LR, ADAM_B1, ADAM_B2, ADAM_EPS, ADAM_WD, ADAM_STEP = 0.001, 0.9, 0.999, 1e-08, 0.01, 10
VMEM_LIMIT = 56 * 1024 * 1024
MESH = pl.DeviceIdType.MESH


def _cparams(sem=None):
    return pltpu.CompilerParams(dimension_semantics=sem, vmem_limit_bytes=VMEM_LIMIT)


def _full_spec(a):
    nd = a.ndim
    return pl.BlockSpec(a.shape, lambda *_: (0,) * nd)


def _rowwise(fn, rows, consts, outs, *, tm, name, halo=(), hosted=None):
    rows = [(a + (0,))[:3] if isinstance(a, tuple) else (a, a.shape[1], 0) for a in rows]
    T = rows[0][0].shape[0]
    assert T % tm == 0 and tm % 8 == 0
    n_tiles = T // tm
    n_in = len(rows) + len(halo) + len(consts)
    in_specs = [pl.BlockSpec((tm, nc), lambda i, j=j: (i, j)) for _, nc, j in rows]
    args = [a for a, _, _ in rows]
    for a, nc, kind in halo:
        if kind == 'prev':
            in_specs.append(pl.BlockSpec((8, nc), lambda i: (jnp.maximum(i * (tm // 8) - 1, 0), 0)))
        else:
            in_specs.append(pl.BlockSpec((8, nc), lambda i: (jnp.minimum((i + 1) * (tm // 8), T // 8 - 1), 0)))
        args.append(a)
    in_specs += [_full_spec(a) for a in consts]
    args += list(consts)
    out_shape, out_specs = [], []
    for ncols, dtype, kind in outs:
        if kind == 'row':
            out_shape.append(jax.ShapeDtypeStruct((T, ncols), dtype))
            out_specs.append(pl.BlockSpec((tm, ncols), lambda i: (i, 0)))
        else:
            out_shape.append(jax.ShapeDtypeStruct((kind, ncols), dtype))
            out_specs.append(pl.BlockSpec((kind, ncols), lambda i: (0, 0)))

    def body(*refs):
        i = pl.program_id(0)
        vals = [r[...] for r in refs[:n_in]]
        res = fn(i, n_tiles, *vals)
        for (ncols, dtype, kind), o_ref, val in zip(outs, refs[n_in:], res, strict=True):
            if kind == 'row':
                o_ref[...] = val.astype(dtype)
            else:
                @pl.when(i == 0)
                def _():
                    o_ref[...] = jnp.zeros_like(o_ref)
                o_ref[...] += val.astype(dtype)

    h_in, h_in_specs, h_out_specs, h_out_shape, h_scratch = _hosted_args(hosted)
    res = pl.pallas_call(
        _hosting(body, hosted, n_in, len(outs), 0, n_tiles), name=name, grid=(n_tiles,),
        in_specs=in_specs + h_in_specs, out_specs=out_specs + h_out_specs, out_shape=out_shape + h_out_shape,
        scratch_shapes=h_scratch, compiler_params=_cparams(("arbitrary",)),
    )(*args, *h_in)
    return res


def _pick(n, cands):
    for c in cands:
        if n % c == 0:
            return c
    return n


MM_ROWS = (1024, 512, 256, 128)
MM_COLS = (1536, 1408, 1024, 896, 768, 512, 256, 128)
MM_WIDE = 3000


def _mm_nn(a, w, *, name, out_dtype=F32, hosted=None):
    T, K = a.shape
    N = w.shape[1]
    tm = _pick(T, MM_ROWS)
    tn = _pick(N, MM_COLS)
    grid = (N // tn, T // tm)
    h_in, h_in_specs, h_out_specs, h_out_shape, h_scratch = _hosted_args(hosted)

    def body(a_ref, w_ref, o_ref):
        o_ref[...] = jnp.dot(a_ref[...], w_ref[...], preferred_element_type=F32).astype(out_dtype)

    res = pl.pallas_call(
        _hosting(body, hosted, 2, 1, 0, grid), name=name, grid=grid,
        in_specs=[pl.BlockSpec((tm, K), lambda j, i: (i, 0)), pl.BlockSpec((K, tn), lambda j, i: (0, j))] + h_in_specs,
        out_specs=[pl.BlockSpec((tm, tn), lambda j, i: (i, j))] + h_out_specs,
        out_shape=[jax.ShapeDtypeStruct((T, N), out_dtype)] + h_out_shape, scratch_shapes=h_scratch,
        compiler_params=_cparams(("arbitrary", "arbitrary")),
    )(a, w, *h_in)
    return res if hosted else res[0]


def _mm_nt(dy, w, *, name, out_dtype=F32, hosted=None):
    T, N = dy.shape
    K = w.shape[0]
    tm = _pick(T, MM_ROWS if N <= MM_WIDE else MM_ROWS[1:])
    tk = _pick(K, MM_COLS[1:])
    grid = (K // tk, T // tm)
    h_in, h_in_specs, h_out_specs, h_out_shape, h_scratch = _hosted_args(hosted)

    def body(dy_ref, w_ref, o_ref):
        o_ref[...] = lax.dot_general(dy_ref[...], w_ref[...], (((1,), (1,)), ((), ())),
                                     preferred_element_type=F32).astype(out_dtype)

    res = pl.pallas_call(
        _hosting(body, hosted, 2, 1, 0, grid), name=name, grid=grid,
        in_specs=[pl.BlockSpec((tm, N), lambda j, i: (i, 0)), pl.BlockSpec((tk, N), lambda j, i: (j, 0))] + h_in_specs,
        out_specs=[pl.BlockSpec((tm, tk), lambda j, i: (i, j))] + h_out_specs,
        out_shape=[jax.ShapeDtypeStruct((T, K), out_dtype)] + h_out_shape, scratch_shapes=h_scratch,
        compiler_params=_cparams(("arbitrary", "arbitrary")),
    )(dy, w, *h_in)
    return res if hosted else res[0]


def _mm_tn(a, dy, *, name, out_dtype=F32, col_shards=None):
    T, K = a.shape
    N = dy.shape[1]
    tm = _pick(T, MM_ROWS)
    tn = N // col_shards if col_shards else _pick(N, MM_COLS[1:])
    n_t = T // tm

    def body(a_ref, dy_ref, o_ref, acc_ref):
        i = pl.program_id(1)

        @pl.when(i == 0)
        def _():
            acc_ref[...] = jnp.zeros_like(acc_ref)

        acc_ref[...] += lax.dot_general(a_ref[...], dy_ref[...], (((0,), (0,)), ((), ())), preferred_element_type=F32)

        @pl.when(i == n_t - 1)
        def _():
            o_ref[...] = acc_ref[...].astype(out_dtype)

    if col_shards:
        out_specs = pl.BlockSpec((None, K, tn), lambda j, i: (j, 0, 0))
        out_shape = jax.ShapeDtypeStruct((col_shards, K, tn), out_dtype)
    else:
        out_specs = pl.BlockSpec((K, tn), lambda j, i: (0, j))
        out_shape = jax.ShapeDtypeStruct((K, N), out_dtype)
    return pl.pallas_call(
        body, name=name, grid=(N // tn, n_t),
        in_specs=[pl.BlockSpec((tm, K), lambda j, i: (i, 0)), pl.BlockSpec((tm, tn), lambda j, i: (i, j))],
        out_specs=out_specs, out_shape=out_shape, scratch_shapes=[pltpu.VMEM((K, tn), F32)],
        compiler_params=_cparams(("arbitrary", "arbitrary")),
    )(a, dy)


def _mm_then(products, fn, rows, consts, outs, *, tm, name, lhs_fn=None, hosted=None):
    products = [(p + (None,))[:4] for p in products]
    T = (rows[0] if lhs_fn else products[0][0]).shape[-2]
    n_tiles = T // tm
    in_specs, args = [], []
    for a, w, _, j in products:
        if a is not None and j is None:
            in_specs.append(pl.BlockSpec((tm, a.shape[1]), lambda i: (i, 0)))
            args.append(a)
        elif a is not None:
            in_specs.append(pl.BlockSpec((None, tm, a.shape[2]), lambda i, j=j: (j, i, 0)))
            args.append(a)
        in_specs.append(_full_spec(w) if j is None else
                        pl.BlockSpec((None,) + w.shape[1:], lambda i, j=j: (j, 0, 0)))
        args.append(w)
    n_w = len(args)
    in_specs += [pl.BlockSpec((tm, a.shape[1]), lambda i: (i, 0)) for a in rows] + [_full_spec(c_) for c_ in consts]
    args += list(rows) + list(consts)
    n_in = len(args)
    out_shape, out_specs = [], []
    for ncols, dtype, kind in outs:
        if kind == 'row':
            out_shape.append(jax.ShapeDtypeStruct((T, ncols), dtype))
            out_specs.append(pl.BlockSpec((tm, ncols), lambda i: (i, 0)))
        else:
            out_shape.append(jax.ShapeDtypeStruct((kind, ncols), dtype))
            out_specs.append(pl.BlockSpec((kind, ncols), lambda i: (0, 0)))

    def body(*refs):
        i = pl.program_id(0)
        tiles = [r[...] for r in refs[n_w:n_in]]
        made = []
        if lhs_fn:
            made = lhs_fn(*tiles)
            made = list(made) if isinstance(made, tuple) else [made]
            made[0] = made[0].astype(MXU)
        y, pos = None, 0
        for a, _, form, _ in products:
            if a is None:
                lhs = made[0]
            else:
                lhs, pos = refs[pos][...], pos + 1
            dims = (((1,), (0,)), ((), ())) if form == 'nn' else (((1,), (1,)), ((), ()))
            t = lax.dot_general(lhs, refs[pos][...], dims, preferred_element_type=F32)
            pos += 1
            y = t if y is None else y + t
        res = fn(i, n_tiles, y, *made, *tiles)
        for (ncols, dtype, kind), o_ref, val in zip(outs, refs[n_in:], res, strict=True):
            if kind == 'row':
                o_ref[...] = val.astype(dtype)
            else:
                @pl.when(i == 0)
                def _():
                    o_ref[...] = jnp.zeros_like(o_ref)
                o_ref[...] += val.astype(dtype)

    h_in, h_in_specs, h_out_specs, h_out_shape, h_scratch = _hosted_args(hosted)
    return pl.pallas_call(
        _hosting(body, hosted, n_in, len(outs), 0, n_tiles), name=name, grid=(n_tiles,),
        in_specs=in_specs + h_in_specs, out_specs=out_specs + h_out_specs, out_shape=out_shape + h_out_shape,
        scratch_shapes=h_scratch, compiler_params=_cparams(("arbitrary",)))(*args, *h_in)


def _seg_ones(n):
    r = lax.broadcasted_iota(jnp.int32, (n, n), 0) // HD
    c = lax.broadcasted_iota(jnp.int32, (n, n), 1) // HD
    return (r == c).astype(F32)


def _segsum_raw(x):
    ones = _seg_ones(x.shape[1])
    if MXU == F32:
        return jnp.dot(x, ones, precision=HI, preferred_element_type=F32)
    hi = x.astype(MXU)
    lo = (x - hi.astype(F32)).astype(MXU)
    ones = ones.astype(MXU)
    return jnp.dot(hi, ones, preferred_element_type=F32) + jnp.dot(lo, ones, preferred_element_type=F32)


@jax.custom_vjp
def _segsum(x):
    return _segsum_raw(x)


def _segsum_fwd(x):
    return _segsum_raw(x), None


def _segsum_bwd(_, g):
    return (_segsum_raw(g),)


_segsum.defvjp(_segsum_fwd, _segsum_bwd)


def _mxu(x):
    return x.astype(MXU)


@jax.custom_vjp
def _bdot(a, b):
    return jnp.dot(_mxu(a), _mxu(b), preferred_element_type=F32)


def _bdot_fwd(a, b):
    return _bdot(a, b), (a, b)


def _bdot_bwd(res, g):
    a, b = res
    da = lax.dot_general(_mxu(g), _mxu(b), (((1,), (1,)), ((), ())), preferred_element_type=F32)
    db = lax.dot_general(_mxu(a), _mxu(g), (((0,), (0,)), ((), ())), preferred_element_type=F32)
    return da.astype(a.dtype), db.astype(b.dtype)


_bdot.defvjp(_bdot_fwd, _bdot_bwd)


def _sigmoid(x):
    return 1.0 / (1.0 + jnp.exp(-x))


def _softplus(x):
    return jnp.maximum(x, 0.0) + jnp.log(1.0 + jnp.exp(jnp.minimum(x, -x)))


def _norm_mod(x, gain, scale, shift):
    inv = lax.rsqrt(jnp.mean(x * x, axis=-1, keepdims=True) + RMS_EPS)
    return (x * inv) * gain * (1.0 + scale) + shift


def _prep(mixed, decay_w0, lora_up, iclr_a0, gate_up, k_k, k_a):
    r = mixed[:, 0:RW]
    k = mixed[:, RW:2 * RW]
    v = mixed[:, 2 * RW:3 * RW]
    z = mixed[:, 3 * RW:3 * RW + 128]
    xg = mixed[:, 3 * RW + 128:]
    lane = lax.broadcasted_iota(jnp.int32, z.shape, 1)
    tz = jnp.where(lane < 64, jnp.tanh(z), z)
    lo = _bdot(tz, lora_up)
    w_log = -_softplus(-(decay_w0 + lo[:, :RW])) - 0.5
    lw = -jnp.exp(w_log)
    a_ic = _sigmoid(iclr_a0 + lo[:, RW:])
    g = _bdot(_sigmoid(xg), gate_up)
    kk = k * k_k
    kk = kk / jnp.maximum(jnp.sqrt(_segsum(kk * kk)), 1e-12)
    k_mod = k * (1.0 + (a_ic - 1.0) * k_a)
    return jnp.concatenate([r, lw, k_mod, v, -kk, kk * a_ic, g], axis=1)


def _post(y, r, k, v, g, lnx_gain, lnx_bias, r_k):
    mu = _segsum(y) * (1.0 / HD)
    yc = y - mu
    var = _segsum(yc * yc) * (1.0 / HD)
    yn = yc * lax.rsqrt(var + GN_EPS) * lnx_gain + lnx_bias
    bonus = _segsum(r * k * r_k) * v
    return (yn + bonus) * g


def _merge(pg, ma, mb, bias):
    gates = _sigmoid(pg + bias)
    return gates[:, :D] * ma + gates[:, D:] * mb


def _swiglu(u, v):
    return u * _sigmoid(u) * v


def _ffn_in(h, w1, w3, *, name):
    T, K = h.shape
    ns, _, Fs = w1.shape
    tm = _pick(T, MM_ROWS)

    def body(h_ref, w1_ref, w3_ref, u_ref, v_ref, a_ref):
        u = jnp.dot(h_ref[...], w1_ref[...], preferred_element_type=F32).astype(MXU)
        v = jnp.dot(h_ref[...], w3_ref[...], preferred_element_type=F32).astype(MXU)
        u_ref[...] = u
        v_ref[...] = v
        a_ref[...] = _swiglu(u.astype(F32), v.astype(F32)).astype(MXU)

    wspec = pl.BlockSpec((None, K, Fs), lambda j, i: (j, 0, 0))
    ospec = pl.BlockSpec((None, tm, Fs), lambda j, i: (j, i, 0))
    return pl.pallas_call(
        body, name=name, grid=(ns, T // tm),
        in_specs=[pl.BlockSpec((tm, K), lambda j, i: (i, 0)), wspec, wspec],
        out_specs=[ospec] * 3, out_shape=[jax.ShapeDtypeStruct((ns, T, Fs), MXU)] * 3,
        compiler_params=_cparams(("arbitrary", "arbitrary")),
    )(h, w1, w3)


def _ffn_act_bwd(dff, w2, u, v, *, name):
    T, N = dff.shape
    ns, Fs, _ = w2.shape
    tm = _pick(T, MM_ROWS)

    def body(dy_ref, w_ref, u_ref, v_ref, du_ref, dv_ref):
        dact = lax.dot_general(dy_ref[...], w_ref[...], (((1,), (1,)), ((), ())), preferred_element_type=F32)
        _, vjp = jax.vjp(_swiglu, u_ref[...].astype(F32), v_ref[...].astype(F32))
        du, dv = vjp(dact)
        du_ref[...] = du.astype(MXU)
        dv_ref[...] = dv.astype(MXU)

    tile = pl.BlockSpec((None, tm, Fs), lambda j, i: (j, i, 0))
    return pl.pallas_call(
        body, name=name, grid=(ns, T // tm),
        in_specs=[pl.BlockSpec((tm, N), lambda j, i: (i, 0)), pl.BlockSpec((None, Fs, N), lambda j, i: (j, 0, 0)),
                  tile, tile],
        out_specs=[tile, tile], out_shape=[jax.ShapeDtypeStruct((ns, T, Fs), MXU)] * 2,
        compiler_params=_cparams(("arbitrary", "arbitrary")),
    )(dff, w2, u, v)


def _mm_tn_blocks(a, dy, *, name, out_dtype):
    a3, d3 = a.ndim == 3, dy.ndim == 3
    ns = a.shape[0] if a3 else dy.shape[0]
    T, K, N = a.shape[-2], a.shape[-1], dy.shape[-1]
    tm = _pick(T, MM_ROWS)
    n_t = T // tm

    def body(a_ref, dy_ref, o_ref, acc_ref):
        i = pl.program_id(1)

        @pl.when(i == 0)
        def _():
            acc_ref[...] = jnp.zeros_like(acc_ref)

        acc_ref[...] += lax.dot_general(a_ref[...], dy_ref[...], (((0,), (0,)), ((), ())), preferred_element_type=F32)

        @pl.when(i == n_t - 1)
        def _():
            o_ref[...] = acc_ref[...].astype(out_dtype)

    spec = lambda is3, n: (pl.BlockSpec((None, tm, n), lambda j, i: (j, i, 0)) if is3
                           else pl.BlockSpec((tm, n), lambda j, i: (i, 0)))
    return pl.pallas_call(
        body, name=name, grid=(ns, n_t), in_specs=[spec(a3, K), spec(d3, N)],
        out_specs=pl.BlockSpec((None, K, N), lambda j, i: (j, 0, 0)),
        out_shape=jax.ShapeDtypeStruct((ns, K, N), out_dtype), scratch_shapes=[pltpu.VMEM((K, N), F32)],
        compiler_params=_cparams(("arbitrary", "arbitrary")),
    )(a, dy)


@functools.partial(jax.custom_vjp, nondiff_argnums=(1,))
def _lane_roll(x, s):
    return pltpu.roll(x, s, 1)


def _lane_roll_fwd(x, s):
    return pltpu.roll(x, s, 1), None


def _lane_roll_bwd(s, _, g):
    n = g.shape[1]
    return (pltpu.roll(g, (n - s) % n, 1),)


_lane_roll.defvjp(_lane_roll_fwd, _lane_roll_bwd)


def _rope(x, cos, sin_lo, sin_hi):
    n = x.shape[1]
    return x * cos + _lane_roll(x, n - 8) * sin_lo + _lane_roll(x, 8) * sin_hi


def _head_rms(x, gain):
    return x * lax.rsqrt(_segsum(x * x) * (1.0 / HD) + RMS_EPS) * gain


def _attn_blocks(qkv_c, qkv_p, tab_c, tab_p, qg, kg, sinks, first):
    nb = qkv_c.shape[0] // BLK
    G = 4

    def tabs(tab, n):
        return [jnp.tile(tab[:, j * 128:(j + 1) * 128], (1, n // 128)) for j in range(3)]

    qg = jnp.concatenate([qg] * NH, axis=1)
    kg = jnp.concatenate([kg] * 2, axis=1)
    q = _rope(_head_rms(qkv_c[:, :RW], qg), *tabs(tab_c, RW))
    k_in = jnp.concatenate([qkv_p[:, RW:RW + 128], qkv_c[:, RW:RW + 128]], axis=0)
    k = _rope(_head_rms(k_in, kg), *tabs(jnp.concatenate([tab_p, tab_c], axis=0), 128))
    v = jnp.concatenate([qkv_p[:, RW + 128:], qkv_c[:, RW + 128:]], axis=0)

    pile = lambda xs: jnp.concatenate([x_[None] for x_ in xs], axis=0)

    def bands(t):
        return pile([t[b * BLK:(b + 2) * BLK, kvh * HD:(kvh + 1) * HD] for kvh in range(2) for b in range(nb)])

    qs = pile([jnp.concatenate([q[b * BLK:(b + 1) * BLK, (G * kvh + g) * HD:(G * kvh + g + 1) * HD]
                                for g in range(G)], axis=0) for kvh in range(2) for b in range(nb)])
    s = _bmm(qs, bands(k), 2, 2, 1) * (HD ** -0.5)
    qi = lax.broadcasted_iota(jnp.int32, (G * BLK, 2 * BLK), 0) % BLK
    kj = lax.broadcasted_iota(jnp.int32, (G * BLK, 2 * BLK), 1)
    dist = qi + BLK - kj
    in_band = (dist >= 0) & (dist < BLK)
    pair = lax.broadcasted_iota(jnp.int32, (2 * nb, 1, 1), 0)
    no_prev = (pair % nb == 0) & first
    valid = in_band[None] & (jnp.logical_not(no_prev) | (kj >= BLK)[None])
    s = jnp.where(valid, s, NEG_INF)
    row_g = lax.broadcasted_iota(jnp.int32, (G * BLK, 1), 0) // BLK
    sink = []
    for kvh in range(2):
        col = jnp.zeros((G * BLK, 1), F32)
        for g in range(G):
            col = jnp.where(row_g == g, sinks[:, G * kvh + g:G * kvh + g + 1], col)
        sink += [col] * nb
    sink = pile(sink)
    m = lax.stop_gradient(jnp.maximum(jnp.max(s, axis=-1, keepdims=True), sink))
    e = jnp.exp(s - m)
    p = e * (1.0 / (jnp.sum(e, axis=-1, keepdims=True) + jnp.exp(sink - m)))
    o = _bmm(p, bands(v), 2, 1, 1)
    return jnp.concatenate([jnp.concatenate([o[kvh * nb + b, g * BLK:(g + 1) * BLK] for kvh in range(2)
                                             for g in range(G)], axis=1) for b in range(nb)], axis=0)


def _heads(x):
    return jnp.stack([x[:, h * HD:(h + 1) * HD] for h in range(NH)], axis=0)


def _unheads(x):
    return jnp.concatenate([x[h] for h in range(NH)], axis=1)


def _split(x, n):
    parts, rest = [], x
    for _ in range(n):
        p = rest.astype(MXU)
        parts.append(p)
        rest = rest - p.astype(F32)
    return parts


def _bdot_batched(a, b, ca, cb):
    return lax.dot_general(a, b, (((ca,), (cb,)), ((0,), (0,))), preferred_element_type=F32)


def _bmm_passes(a, b, ca, cb, passes):
    if MXU == F32:
        return lax.dot_general(a, b, (((ca,), (cb,)), ((0,), (0,))), precision=HI, preferred_element_type=F32)
    if passes == 1:
        return _bdot_batched(a.astype(MXU), b.astype(MXU), ca, cb)
    (a0, a1), (b0, b1) = _split(a, 2), _split(b, 2)
    return _bdot_batched(a0, b0, ca, cb) + (_bdot_batched(a0, b1, ca, cb) + _bdot_batched(a1, b0, ca, cb))


@functools.partial(jax.custom_vjp, nondiff_argnums=(2, 3, 4))
def _bmm(a, b, ca, cb, passes=1):
    return _bmm_passes(a, b, ca, cb, passes)


def _bmm_fwd(a, b, ca, cb, passes):
    return _bmm_passes(a, b, ca, cb, passes), (a, b)


def _bmm_bwd(ca, cb, passes, res, g):
    a, b = res
    if (ca, cb) == (2, 1):
        return _bmm_passes(g, b, 2, 2, passes), _bmm_passes(a, g, 1, 1, passes)
    if (ca, cb) == (2, 2):
        return _bmm_passes(g, b, 2, 1, passes), _bmm_passes(g, a, 1, 1, passes)
    return _bmm_passes(b, g, 2, 2, passes), _bmm_passes(a, g, 2, 1, passes)


_bmm.defvjp(_bmm_fwd, _bmm_bwd)


def _tri_dot(x, transpose):
    C = x.shape[1]
    ri = lax.broadcasted_iota(jnp.int32, (C, C), 0)
    ci = lax.broadcasted_iota(jnp.int32, (C, C), 1)
    tri = jnp.broadcast_to(((ri <= ci) if transpose else (ri >= ci)).astype(MXU), (x.shape[0], C, C))
    if MXU == F32:
        return lax.dot_general(tri, x, (((2,), (1,)), ((0,), (0,))), precision=HI, preferred_element_type=F32)
    p0, p1, p2 = _split(x, 3)
    return _bdot_batched(tri, p0, 2, 1) + (_bdot_batched(tri, p1, 2, 1) + _bdot_batched(tri, p2, 2, 1))


@jax.custom_vjp
def _cumsum_rows(x):
    return _tri_dot(x, False)


def _cumsum_rows_fwd(x):
    return _tri_dot(x, False), None


def _cumsum_rows_bwd(_, g):
    return (_tri_dot(g, True),)


_cumsum_rows.defvjp(_cumsum_rows_fwd, _cumsum_rows_bwd)

P_SCORE = 1
P_SOLVE = 1
P_STATE = 1
SCAN_CHUNKS = (4, 2, 1)


def _neumann(l):
    C = l.shape[1]
    eye = (lax.broadcasted_iota(jnp.int32, (C, C), 0) == lax.broadcasted_iota(jnp.int32, (C, C), 1)).astype(F32)
    x, lp = eye + l, l
    for _ in range(int(math.log2(C)) - 1):
        lp = _bmm(lp, lp, 2, 1, P_SOLVE)
        x = x + _bmm(x, lp, 2, 1, P_SOLVE)
    return x


@jax.custom_vjp
def _unit_lower_inverse(l):
    return _neumann(l)


def _unit_lower_inverse_fwd(l):
    x = _neumann(l)
    return x, x


def _unit_lower_inverse_bwd(x, g):
    return (_bmm(_bmm(x, g, 1, 1, P_SOLVE), x, 2, 2, P_SOLVE),)


_unit_lower_inverse.defvjp(_unit_lower_inverse_fwd, _unit_lower_inverse_bwd)


def _known_inverse(x):
    @jax.custom_vjp
    def f(l):
        return x

    f.defvjp(lambda l: (x, None), lambda _, g: (_bmm(_bmm(x, g, 1, 1, P_SOLVE), x, 2, 2, P_SOLVE),))
    return f


def _chunk(S0, r, lw, k, v, a, b, inverse=None):
    C = CHUNK
    n = r.shape[1] // C
    fold = lambda t: t.reshape(NH * n, C, HD)
    r, lw, k, v, a, b = (fold(t) for t in (r, lw, k, v, a, b))
    ri = lax.broadcasted_iota(jnp.int32, (C, C), 0)
    ci = lax.broadcasted_iota(jnp.int32, (C, C), 1)
    strict = (ri > ci)
    cum = _cumsum_rows(lw)
    p_in = jnp.exp(cum)
    p_ex = jnp.exp(cum - lw)
    p_inv = jnp.exp(-cum)
    at, rt, bt, kt = a * p_ex, r * p_in, b * p_inv, k * p_inv
    lhs = jnp.concatenate([at, rt], axis=1)
    rhs_ = jnp.concatenate([bt, kt], axis=1)
    sc = _bmm(lhs, rhs_, 2, 2, P_SCORE)
    a_ab = jnp.where(strict, sc[:, :C, :C], 0.0)
    a_ak = jnp.where(strict, sc[:, :C, C:], 0.0)
    incl2 = (lax.broadcasted_iota(jnp.int32, (C, 2 * C), 0) >= lax.broadcasted_iota(jnp.int32, (C, 2 * C), 1) % C)
    a_r = jnp.where(incl2, sc[:, C:, :], 0.0)
    av = _bmm(a_ak, v, 2, 1, P_SCORE)
    x = x_all = (_unit_lower_inverse if inverse is None else _known_inverse(inverse))(a_ab)
    p_last = jnp.exp(cum[:, C - 1:C, :])
    per_chunk = lambda t: t.reshape((NH, n) + t.shape[1:])
    lhs, rhs_, a_r, av, x, v, p_last = (per_chunk(t) for t in (lhs, rhs_, a_r, av, x, v, p_last))
    S, ys = S0, []
    for c in range(n):
        s0 = _bmm(lhs[:, c], S, 2, 2, P_STATE)
        u = _bmm(x[:, c], s0[:, :C] + av[:, c], 2, 1, P_SOLVE)
        uv = jnp.concatenate([u, v[:, c]], axis=1)
        ys.append(s0[:, C:] + _bmm(a_r[:, c], uv, 2, 1, P_SCORE))
        S = (S + _bmm(uv, rhs_[:, c], 1, 1, P_STATE)) * p_last[:, c]
    return jnp.concatenate(ys, axis=1), S, x_all


def _hosting(body, ex, n_in, n_out, n_scratch, n_steps):
    if ex is None:
        return body

    def wrapped(*refs):
        a = n_in
        b = a + ex.n_in
        c = b + n_out
        d = c + ex.n_out
        e = d + n_scratch
        ex_refs = (refs[a:b], refs[c:d], refs[e:])
        grid = n_steps if isinstance(n_steps, tuple) else (n_steps,)
        first = last = True
        for ax_, size in enumerate(grid):
            first = first & (pl.program_id(ax_) == 0)
            last = last & (pl.program_id(ax_) == size - 1)

        @pl.when(first)
        def _():
            ex.start(*ex_refs)

        body(*refs[:a], *refs[b:c], *refs[d:e])

        @pl.when(last)
        def _():
            ex.wait(*ex_refs)

    return wrapped


def _hosted_args(ex):
    if ex is None:
        return [], [], [], [], []
    any_spec = pl.BlockSpec(memory_space=pl.ANY)
    return list(ex.arrays), [any_spec] * ex.n_in, [any_spec] * ex.n_out, list(ex.out_shape), list(ex.scratch)


def _scan_fwd(rw, *, name, hosted=None):
    T = rw.shape[0]
    nc = _pick(T // CHUNK, SCAN_CHUNKS)
    rows = CHUNK * nc
    n = T // rows
    h_in, h_in_specs, h_out_specs, h_out_shape, h_scratch = _hosted_args(hosted)

    def body(r_ref, lw_ref, k_ref, v_ref, a_ref, b_ref, y_ref, ck_ref, inv_ref, s_ref):
        @pl.when(pl.program_id(0) == 0)
        def _():
            s_ref[...] = jnp.zeros_like(s_ref)

        S0 = s_ref[...]
        ck_ref[0] = S0
        y, S1, inv = _chunk(S0, *[_heads(ref[...]) for ref in (r_ref, lw_ref, k_ref, v_ref, a_ref, b_ref)])
        y_ref[...] = _unheads(y)
        inv_ref[0] = inv
        s_ref[...] = S1

    col = lambda j: pl.BlockSpec((rows, RW), lambda i: (i, j))
    return pl.pallas_call(
        _hosting(body, hosted, 6, 3, 1, n), name=name, grid=(n,),
        in_specs=[col(j) for j in range(6)] + h_in_specs,
        out_specs=[pl.BlockSpec((rows, RW), lambda i: (i, 0)),
                   pl.BlockSpec((1, NH, HD, HD), lambda i: (i, 0, 0, 0)),
                   pl.BlockSpec((1, NH * nc, CHUNK, CHUNK), lambda i: (i, 0, 0, 0))] + h_out_specs,
        out_shape=[jax.ShapeDtypeStruct((T, RW), F32), jax.ShapeDtypeStruct((n, NH, HD, HD), F32),
                   jax.ShapeDtypeStruct((n, NH * nc, CHUNK, CHUNK), F32)] + h_out_shape,
        scratch_shapes=[pltpu.VMEM((NH, HD, HD), F32)] + h_scratch,
        compiler_params=_cparams(("arbitrary",)),
    )(rw, rw, rw, rw, rw, rw, *h_in)


def _scan_bwd(rw, ck, inv, dy, *, name, hosted=None):
    T = rw.shape[0]
    nc = _pick(T // CHUNK, SCAN_CHUNKS)
    rows = CHUNK * nc
    n = T // rows

    def body(r_ref, lw_ref, k_ref, v_ref, a_ref, b_ref, ck_ref, inv_ref, dy_ref, o_ref, ds_ref):
        @pl.when(pl.program_id(0) == 0)
        def _():
            ds_ref[...] = jnp.zeros_like(ds_ref)

        prim = [_heads(ref[...]) for ref in (r_ref, lw_ref, k_ref, v_ref, a_ref, b_ref)]
        known = inv_ref[0]
        _, vjp = jax.vjp(lambda S0, *p: _chunk(S0, *p, inverse=known)[:2], ck_ref[0], *prim)
        grads = vjp((_heads(dy_ref[...]), ds_ref[...]))
        ds_ref[...] = grads[0]
        o_ref[...] = jnp.concatenate([_unheads(g) for g in grads[1:]], axis=1).astype(o_ref.dtype)

    h_in, h_in_specs, h_out_specs, h_out_shape, h_scratch = _hosted_args(hosted)
    col = lambda j: pl.BlockSpec((rows, RW), lambda i: (n - 1 - i, j))
    return pl.pallas_call(
        _hosting(body, hosted, 9, 1, 1, n), name=name, grid=(n,),
        in_specs=[col(j) for j in range(6)] + [pl.BlockSpec((1, NH, HD, HD), lambda i: (n - 1 - i, 0, 0, 0)),
                                               pl.BlockSpec((1, NH * nc, CHUNK, CHUNK), lambda i: (n - 1 - i, 0, 0, 0)),
                                               pl.BlockSpec((rows, RW), lambda i: (n - 1 - i, 0))] + h_in_specs,
        out_specs=[pl.BlockSpec((rows, 6 * RW), lambda i: (n - 1 - i, 0))] + h_out_specs,
        out_shape=[jax.ShapeDtypeStruct((T, 6 * RW), MXU)] + h_out_shape,
        scratch_shapes=[pltpu.VMEM((NH, HD, HD), F32)] + h_scratch,
        compiler_params=_cparams(("arbitrary",)),
    )(rw, rw, rw, rw, rw, rw, ck, inv, dy, *h_in)


ATTN_BLOCKS = (4, 2, 1)

def _attn_fwd(qkv, tab, qg, kg, sinks, *, name, hosted=None):
    T = qkv.shape[0]
    nb = _pick(T // BLK, ATTN_BLOCKS)
    n = T // (BLK * nb)
    h_in, h_in_specs, h_out_specs, h_out_shape, h_scratch = _hosted_args(hosted)

    def body(c_ref, p_ref, tc_ref, tp_ref, qg_ref, kg_ref, s_ref, o_ref):
        o_ref[...] = _attn_blocks(c_ref[...], p_ref[...], tc_ref[...], tp_ref[...], qg_ref[...], kg_ref[...],
                                  s_ref[...], pl.program_id(0) == 0).astype(o_ref.dtype)

    cur = lambda w: pl.BlockSpec((nb * BLK, w), lambda i: (i, 0))
    prev = lambda w: pl.BlockSpec((BLK, w), lambda i: (jnp.maximum(i * nb - 1, 0), 0))
    return pl.pallas_call(
        _hosting(body, hosted, 7, 1, 0, n), name=name, grid=(n,),
        in_specs=[cur(QKV_W), prev(QKV_W), cur(3 * 128), prev(3 * 128), _full_spec(qg), _full_spec(kg),
                  _full_spec(sinks)] + h_in_specs,
        out_specs=[cur(RW)] + h_out_specs, out_shape=[jax.ShapeDtypeStruct((T, RW), MXU)] + h_out_shape,
        scratch_shapes=h_scratch,
        compiler_params=_cparams(("arbitrary",)),
    )(qkv, qkv, tab, tab, qg, kg, sinks, *h_in)


def _attn_bwd(qkv, tab, qg, kg, sinks, dy, *, name, hosted=None):
    T = qkv.shape[0]
    nb = _pick(T // BLK, ATTN_BLOCKS)
    n = T // (BLK * nb)
    h_in, h_in_specs, h_out_specs, h_out_shape, h_scratch = _hosted_args(hosted)

    def body(c_ref, p_ref, tc_ref, tp_ref, qg_ref, kg_ref, s_ref, dy_ref, dqkv_ref, dqg_ref, dkg_ref, ds_ref, carry_ref):
        i = pl.program_id(0)

        @pl.when(i == 0)
        def _():
            carry_ref[...] = jnp.zeros_like(carry_ref)
            dqg_ref[...] = jnp.zeros_like(dqg_ref)
            dkg_ref[...] = jnp.zeros_like(dkg_ref)
            ds_ref[...] = jnp.zeros_like(ds_ref)

        tc, tp = tc_ref[...], tp_ref[...]
        f = lambda c, p_, qg_, kg_, sk: _attn_blocks(c, p_, tc, tp, qg_, kg_, sk, i == n - 1)
        _, vjp = jax.vjp(f, c_ref[...], p_ref[...], qg_ref[...], kg_ref[...], s_ref[...])
        dc, dp, dqg, dkg, dsk = vjp(dy_ref[...].astype(F32))
        last = slice((nb - 1) * BLK, nb * BLK)
        dqkv_ref[...] = dc.astype(dqkv_ref.dtype)
        dqkv_ref[last, :] = (dc[last] + carry_ref[...]).astype(dqkv_ref.dtype)
        carry_ref[...] = dp
        dqg_ref[...] += dqg
        dkg_ref[...] += dkg
        ds_ref[...] += dsk

    cur = lambda w: pl.BlockSpec((nb * BLK, w), lambda i: (n - 1 - i, 0))
    prev = lambda w: pl.BlockSpec((BLK, w), lambda i: (jnp.maximum((n - 1 - i) * nb - 1, 0), 0))
    return pl.pallas_call(
        _hosting(body, hosted, 8, 4, 1, n), name=name, grid=(n,),
        in_specs=[cur(QKV_W), prev(QKV_W), cur(3 * 128), prev(3 * 128), _full_spec(qg), _full_spec(kg), _full_spec(sinks),
                  cur(RW)] + h_in_specs,
        out_specs=[cur(QKV_W), _full_spec(qg), _full_spec(kg), _full_spec(sinks)] + h_out_specs,
        out_shape=[jax.ShapeDtypeStruct((T, QKV_W), MXU), jax.ShapeDtypeStruct(qg.shape, F32),
                   jax.ShapeDtypeStruct(kg.shape, F32), jax.ShapeDtypeStruct(sinks.shape, F32)] + h_out_shape,
        scratch_shapes=[pltpu.VMEM((BLK, QKV_W), F32)] + h_scratch,
        compiler_params=_cparams(("arbitrary",)),
    )(qkv, qkv, tab, tab, qg, kg, sinks, dy, *h_in)


def _shift_down(cur, prev8, i):
    rolled = pltpu.roll(cur, 1, 0)
    first_row = jnp.where(i > 0, prev8[7:8, :], 0.0)
    row = lax.broadcasted_iota(jnp.int32, cur.shape, 0)
    return jnp.where(row == 0, first_row, rolled)


def _shift_up(cur, next8, i, n):
    tm = cur.shape[0]
    rolled = pltpu.roll(cur, tm - 1, 0)
    last_row = jnp.where(i < n - 1, next8[0:1, :], 0.0)
    row = lax.broadcasted_iota(jnp.int32, cur.shape, 0)
    return jnp.where(row == tm - 1, last_row, rolled)


def _rope_table(positions):
    half = HD // 8
    inv_freq = 500000.0 ** (-jnp.arange(half, dtype=F32) / half)
    lane = jnp.arange(128) % HD
    rotary = lane < 2 * half
    freq = jnp.where(rotary, inv_freq[lane % half], 0.0)
    ang = positions.astype(F32)[:, None] * freq[None, :]
    cos, sin = jnp.cos(ang), jnp.sin(ang)
    return jnp.concatenate([jnp.where(rotary, cos, 1.0), jnp.where(lane < half, -sin, 0.0),
                            jnp.where(rotary & (lane >= half), sin, 0.0)], axis=1)


GATHER_BEHIND = {"f_proj_shift": [('ffn_w3', 512, 768)], "f_proj_gates": [('ffn_w3', 768, 1024)],
                 "f_prep": [('ffn_w2', 352, 704)],
                 "f_scan": [('ffn_w1', 0, 1024), ('w_branch_a', 0, 512), ('w_branch_b', 0, 512)],
                 "f_post": [('ffn_w3', 0, 512)], "f_attn": [('ffn_w2', 0, 352), ('w_out', 0, 256)]}
LATE = ['w_out', 'w_branch_a', 'w_branch_b', 'ffn_w1', 'ffn_w3', 'ffn_w2']
FFN = ('ffn_w1', 'ffn_w3', 'ffn_w2')
BACK_ATTN = ['w_out', 'w_branch_a', 'w_branch_b', 'ffn_w2']
BACK_SCAN = ['ffn_w1', 'ffn_w3']
BACK_LAST = ['w_in', 'decay_up', 'iclr_up', 'gate_up']


def _full_weight(g, ax):
    return g.reshape(-1, g.shape[2]) if ax == 0 else jnp.concatenate([g[j] for j in range(4)], axis=1)


def _local_step(x, target, ada, tab, w, s, shards=None):
    T = x.shape[0]
    tm = _pick(T, (512, 256, 128))
    tm_wide = _pick(T, (256, 128))
    tm_vjp = _pick(T, (256, 128))
    row = lambda n, dt=F32: (n, dt, 'row')
    acc = lambda n, r=1: (n, F32, r)

    def f_norm1(x_, g, ada_):
        return _norm_mod(x_, g, ada_[:, D:2 * D], ada_[:, 0:D])

    landed = {}

    def behind(kernel_name):
        if not shards:
            return None
        pieces = GATHER_BEHIND[kernel_name]
        return _GatherChips([shards[n] for n, _, _ in pieces], rows=[(lo, hi) for _, lo, hi in pieces])

    def took(kernel_name, got):
        landed.update(zip(GATHER_BEHIND[kernel_name], got))

    def mm_behind(a_, w_, kernel_name, **kw):
        ex = behind(kernel_name)
        res = _mm_nn(a_, w_, name=kernel_name, hosted=ex, **kw)
        if ex:
            took(kernel_name, res[1:])
            return res[0]
        return res

    h1, proj, *got = _mm_then([(None, w['w_in'][:, :SHIFT_W], 'nn')], lambda i, n, y, h, *_: (h, y), [x],
                              [s['norm1_gain'], ada], [row(D, MXU), row(SHIFT_W)], tm=tm, name="f_proj_shift",
                              lhs_fn=f_norm1, hosted=behind("f_proj_shift"))
    took("f_proj_shift", got)
    proj_qkv = _mm_nn(h1, w['w_in'][:, SHIFT_W:SHIFT_W + QKV_W], name="f_proj_qkv")
    proj_g = mm_behind(h1, w['w_in'][:, SHIFT_W + QKV_W:], "f_proj_gates", out_dtype=MXU)
    prep_consts = [s['decay_w0'], s['lora_up'], s['iclr_a0'], s['gate_up'], s['k_k'], s['k_a']]

    def f_prep(i, n, cur, prev8, mu, *params):
        mixed = cur + (_shift_down(cur, prev8, i) - cur) * mu
        return (_prep(mixed, *params),)
    rw, *got = _rowwise(f_prep, [(proj, SHIFT_W)], [s['tshift_mu']] + prep_consts, [row(7 * RW)], tm=tm_wide,
                        name="f_prep", halo=[(proj, SHIFT_W, 'prev')], hosted=behind("f_prep"))
    took("f_prep", got)
    y, ck, inv, *got = _scan_fwd(rw, name="f_scan", hosted=behind("f_scan"))
    took("f_scan", got)
    post_consts = [s['lnx_gain'], s['lnx_bias'], s['r_k']]

    rkvg = [(rw, RW, j) for j in (0, 2, 3, 6)]

    def f_post(i, n, *args):
        return (_post(*args),)
    ya, *got = _rowwise(f_post, [y] + rkvg, post_consts, [row(RW, MXU)], tm=tm_wide, name="f_post",
                        hosted=behind("f_post"))
    took("f_post", got)
    yb, *got = _attn_fwd(proj_qkv, tab, s['q_norm_gain'], s['k_norm_gain'], s['attn_sinks'], name="f_attn",
                         hosted=behind("f_attn"))
    took("f_attn", got)
    w = dict(w)
    if shards:
        ax = dict(SHARDED)
        for n in LATE:
            rows = [landed[key] for key in sorted(k_ for k_ in landed if k_[0] == n)]
            blocks = rows[0] if len(rows) == 1 else jnp.concatenate(rows, axis=1)
            w[n] = blocks if n in FFN else _full_weight(blocks, ax[n])
    else:
        fs = DFF // 4
        w.update({n: w[n].reshape(D, 4, fs).transpose(1, 0, 2) for n in ('ffn_w1', 'ffn_w3')})
        w['ffn_w2'] = w['ffn_w2'].reshape(4, fs, D)
    def f_merge(pg, ya_, yb_, x_, wa, wb, bias, g, ada_):
        ma_ = jnp.dot(ya_, wa, preferred_element_type=F32).astype(MXU)
        mb_ = jnp.dot(yb_, wb, preferred_element_type=F32).astype(MXU)
        return _merge(pg.astype(F32), ma_.astype(F32), mb_.astype(F32), bias), ma_, mb_

    def f_res1(i, n, mo_, merged_, ma_, mb_, pg, ya_, yb_, x_, wa, wb, bias, g, ada_):
        x1_ = x_ + ada_[:, 2 * D:3 * D] * mo_
        return merged_, ma_, mb_, mo_, x1_, _norm_mod(x1_, g, ada_[:, 4 * D:5 * D], ada_[:, 3 * D:4 * D])
    merged, ma, mb, mo, x1, h2 = _mm_then(
        [(None, w['w_out'], 'nn')], f_res1, [proj_g, ya, yb, x],
        [w['w_branch_a'], w['w_branch_b'], s['branch_gate_b'], s['norm2_gain'], ada],
        [row(D, MXU), row(D, MXU), row(D, MXU), row(D), row(D), row(D, MXU)], tm=tm, name="f_out", lhs_fn=f_merge)
    u, v, act = _ffn_in(h2, w['ffn_w1'], w['ffn_w3'], name="f_ffn_in")

    def f_loss(i, n, ff_, x1_, tgt, ada_):
        g2 = ada_[:, 5 * D:6 * D]
        err = x1_ + g2 * ff_ - tgt
        dx2 = err * (1.0 / D)
        loss = 0.5 * jnp.sum(jnp.sum(err * err, axis=1, keepdims=True) * (1.0 / D), axis=0, keepdims=True)
        return dx2, (dx2 * g2), jnp.broadcast_to(loss, (1, 128)), jnp.sum(dx2 * ff_, axis=0, keepdims=True)
    dx2, dff, loss, dgate2 = _mm_then([(act, w['ffn_w2'], 'nn', j) for j in range(4)], f_loss, [x1, target], [ada],
                                      [row(D), row(D, MXU), acc(128), acc(D)], tm=tm, name="f_ffn_out")

    du, dv = _ffn_act_bwd(dff, w['ffn_w2'], u, v, name="b_ffn_out_dx")
    g_w2 = _mm_tn_blocks(act, dff, name="b_ffn_out_dw", out_dtype=MXU)
    g_w1 = _mm_tn_blocks(h2, du, name="b_ffn_w1_dw", out_dtype=MXU)
    g_w3 = _mm_tn_blocks(h2, dv, name="b_ffn_w3_dw", out_dtype=MXU)

    def b_res1(i, n, dh2_, x1_, dx2_, mo_, g, ada_):
        _, vjp = jax.vjp(_norm_mod, x1_, g, ada_[:, 4 * D:5 * D], ada_[:, 3 * D:4 * D])
        dxn, dg, dsc, dsh = vjp(dh2_)
        dx1_ = dxn + dx2_
        g1 = ada_[:, 2 * D:3 * D]
        return dx1_, dx1_ * g1, dg, dsc, dsh, jnp.sum(dx1_ * mo_, axis=0, keepdims=True)
    dx1, dmo, d_gain2, d_scale2, d_shift2, dgate1 = _mm_then(
        [(t, w[n], 'nt', j) for t, n in ((du, 'ffn_w1'), (dv, 'ffn_w3')) for j in range(4)], b_res1, [x1, dx2, mo],
        [s['norm2_gain'], ada],
        [row(D), row(D, MXU), acc(D), acc(D), acc(D), acc(D)], tm=tm_wide, name="b_ffn_in_dx")
    g_wout = _mm_tn(merged, dmo, name="b_out_dw", out_dtype=MXU)

    def b_merge(i, n, dm, pg, ma_, mb_, wa, wb, bias):
        _, vjp = jax.vjp(_merge, pg.astype(F32), ma_.astype(F32), mb_.astype(F32), bias)
        dpg, dma_, dmb_, dbias = vjp(dm)
        dma_, dmb_ = dma_.astype(MXU), dmb_.astype(MXU)
        nt = lambda a_, b_: lax.dot_general(a_, b_, (((1,), (1,)), ((), ())), preferred_element_type=F32)
        return dpg, dma_, dmb_, nt(dma_, wa), nt(dmb_, wb), dbias
    dpg, dma, dmb, dya, dyb, d_bias = _mm_then(
        [(dmo, w['w_out'], 'nt')], b_merge, [proj_g, ma, mb], [w['w_branch_a'], w['w_branch_b'], s['branch_gate_b']],
        [row(GATE_W, MXU), row(D, MXU), row(D, MXU), row(RW), row(RW), acc(GATE_W)], tm=tm, name="b_out_dx")
    g_wa = _mm_tn(ya, dma, name="b_branch_a_dw", out_dtype=MXU, col_shards=4)
    g_wb = _mm_tn(yb, dmb, name="b_branch_b_dw", out_dtype=MXU, col_shards=4)
    gw = dict(w_branch_a=g_wa, w_branch_b=g_wb, w_out=g_wout.reshape(4, D // 4, D), ffn_w1=g_w1, ffn_w3=g_w3,
              ffn_w2=g_w2)
    recv = {}
    dqkv, d_qg, d_kg, d_sinks, *got = _attn_bwd(
        proj_qkv, tab, s['q_norm_gain'], s['k_norm_gain'], s['attn_sinks'], dyb, name="b_attn",
        hosted=shards and _ScatterChips([gw[n] for n in BACK_ATTN]))
    recv.update(zip(BACK_ATTN, got))

    def b_post(i, n, y_, r_, k_, v_, g_, dya_, *params):
        _, vjp = jax.vjp(_post, y_, r_, k_, v_, g_, *params)
        dy_, dr_, dk_, dv_, dg_, *dparams = vjp(dya_)
        return (dy_, jnp.concatenate([dr_, dk_, dv_, dg_], axis=1), *dparams)
    dy, drkvg, d_lnx_gain, d_lnx_bias, d_r_k = _rowwise(
        b_post, [y] + rkvg + [dya], post_consts, [row(RW), row(4 * RW, MXU), acc(RW), acc(RW), acc(RW)], tm=tm_wide,
        name="b_post")
    dscan, *got = _scan_bwd(rw, ck, inv, dy, name="b_scan",
                            hosted=shards and _ScatterChips([gw[n] for n in BACK_SCAN]))
    recv.update(zip(BACK_SCAN, got))

    def b_prep(i, n, cur, drw_, dscan_, prev8, mu, *params):
        shifted = _shift_down(cur, prev8, i)
        mixed = cur + (shifted - cur) * mu
        _, vjp = jax.vjp(_prep, mixed, *params)
        blk = lambda t, j: t[:, j * RW:(j + 1) * RW].astype(F32)
        ct = jnp.concatenate([blk(dscan_, 0) + blk(drw_, 0), blk(dscan_, 1), blk(dscan_, 2) + blk(drw_, 1),
                              blk(dscan_, 3) + blk(drw_, 2), blk(dscan_, 4), blk(dscan_, 5), blk(drw_, 3)], axis=1)
        grads = vjp(ct)
        dmixed = grads[0]
        return (dmixed, jnp.sum(dmixed * (shifted - cur), axis=0, keepdims=True)) + tuple(grads[1:])
    dmixed, d_mu, d_w0, d_lora, d_a0, d_gate_up, d_kk, d_ka = _rowwise(
        b_prep, [(proj, SHIFT_W), drkvg, dscan], [s['tshift_mu']] + prep_consts,
        [row(SHIFT_W), acc(SHIFT_W), acc(RW), acc(2 * RW, 128), acc(RW), acc(RW, 128), acc(RW), acc(RW)],
        tm=tm_vjp, name="b_prep", halo=[(proj, SHIFT_W, 'prev')])

    def b_gather(i, n, dm, dqkv_, dpg_, next8, mu):
        dcur = dm * (1.0 - mu) + _shift_up(dm, next8, i, n) * mu
        return (jnp.concatenate([dcur.astype(MXU), dqkv_, dpg_], axis=1),)
    (dproj,) = _rowwise(b_gather, [dmixed, dqkv, dpg], [s['tshift_mu']], [row(IN_W, MXU)], tm=tm_wide, name="b_gather",
                        halo=[(dmixed, SHIFT_W, 'next')])
    g_win = _mm_tn(h1, dproj, name="b_proj_dw", out_dtype=MXU, col_shards=4)

    def col_blocks(g):
        k, n = g.shape
        return g.reshape(k, 4, n // 4).transpose(1, 0, 2).astype(MXU)
    gw.update(w_in=g_win, decay_up=col_blocks(d_lora[:64, :RW]), iclr_up=col_blocks(d_lora[64:, RW:]),
              gate_up=col_blocks(d_gate_up))
    top, bottom = None, None
    if shards:
        top = _ScatterChips([gw[n] for n in BACK_LAST], rows=[(0, D // 2)] + [None] * (len(BACK_LAST) - 1))
        bottom = _ScatterChips([gw['w_in']], rows=[(D // 2, 3 * D // 4)])
        dh1, *got_top = _mm_nt(dproj, w['w_in'], name="b_proj_dx", hosted=top)
    else:
        dh1 = _mm_nt(dproj, w['w_in'], name="b_proj_dx")

    def b_norm1(i, n, x_, dh1_, dx1_, g, ada_):
        _, vjp = jax.vjp(_norm_mod, x_, g, ada_[:, D:2 * D], ada_[:, 0:D])
        dxn, dg, dsc, dsh = vjp(dh1_)
        return dxn + dx1_, dg, dsc, dsh
    dx, d_gain1, d_scale1, d_shift1, *got_bottom = _rowwise(
        b_norm1, [x, dh1, dx1], [s['norm1_gain'], ada], [row(D), acc(D), acc(D), acc(D)], tm=tm, name="b_norm1",
        hosted=bottom)
    if shards:
        recv.update(zip(BACK_LAST[1:], got_top[1:]))
        recv['w_in'] = [got_top[0], got_bottom[0]]

    d_ada = jnp.concatenate([d_shift1, d_scale1, dgate1, d_shift2, d_scale2, dgate2], axis=1)
    gs = dict(norm1_gain=d_gain1, norm2_gain=d_gain2, tshift_mu=d_mu, decay_w0=d_w0, iclr_a0=d_a0, k_k=d_kk, k_a=d_ka,
              r_k=d_r_k, lnx_gain=d_lnx_gain, lnx_bias=d_lnx_bias, q_norm_gain=d_qg, k_norm_gain=d_kg,
              attn_sinks=d_sinks, branch_gate_b=d_bias)
    return loss, dx, d_ada, gw, gs, recv


ANY = pl.BlockSpec(memory_space=pl.ANY)


def _place():
    x, y, c = lax.axis_index("x"), lax.axis_index("y"), lax.axis_index("c")
    return x, y, c, [(1 - x, y), (x, 1 - y), (1 - x, 1 - y)]


def _all_gather8(x_shard, *, name):
    m_per, n = x_shard.shape

    def body(x_ref, out_ref, send_sems, recv_sems, local_sem):
        x, y, c, chips = _place()
        me, sibling = (x, y, c), (x, y, 1 - c)

        def rows(px, py, pc):
            return out_ref.at[pl.ds((4 * px + 2 * py + pc) * m_per, m_per), :]

        def copy(k, block, to, src=None):
            return pltpu.make_async_remote_copy(
                src_ref=rows(*block) if src is None else src, dst_ref=rows(*block),
                send_sem=send_sems.at[k], recv_sem=recv_sems.at[k], device_id=to, device_id_type=MESH)

        mine = pltpu.make_async_copy(x_ref, rows(*me), local_sem)
        mine.start()
        first = [copy(0, me, sibling, src=x_ref)]
        first += [copy(1 + j, me, (*chip, c), src=x_ref) for j, chip in enumerate(chips)]
        for cp in first:
            cp.start()
        passed = [copy(4 + j, (*chip, c), sibling) for j, chip in enumerate(chips)]
        for j, chip in enumerate(chips):
            copy(1 + j, (*chip, c), me).wait_recv()
            passed[j].start()
        copy(0, sibling, me).wait_recv()
        for j, chip in enumerate(chips):
            copy(4 + j, (*chip, 1 - c), me).wait_recv()
        for cp in first + passed:
            cp.wait_send()
        mine.wait()

    return pl.pallas_call(
        body, name=name, out_shape=jax.ShapeDtypeStruct((8 * m_per, n), x_shard.dtype),
        in_specs=[pl.BlockSpec(memory_space=pltpu.VMEM)], out_specs=pl.BlockSpec(memory_space=pltpu.VMEM),
        scratch_shapes=[pltpu.SemaphoreType.DMA((7,)), pltpu.SemaphoreType.DMA((7,)), pltpu.SemaphoreType.DMA],
    )(x_shard)


class _GatherChips:
    def __init__(self, shards, rows=None):
        n = len(shards)
        self.rows = [r or (0, s.shape[0]) for s, r in zip(shards, rows or [None] * n, strict=True)]
        self.arrays, self.n_in, self.n_out = list(shards), n, n
        self.out_shape = [jax.ShapeDtypeStruct((4, hi - lo) + s.shape[1:], s.dtype)
                          for s, (lo, hi) in zip(shards, self.rows, strict=True)]
        self.scratch = [pltpu.SemaphoreType.DMA((3 * n,)), pltpu.SemaphoreType.DMA((3 * n,)),
                        pltpu.SemaphoreType.DMA((n,))]

    def _copies(self, x_refs, out_refs, sems, receiving):
        send_sems, recv_sems, local_sems = sems
        x, y, c, chips = _place()
        s_me = 2 * x + y
        n = self.n_in
        src = [x_refs[a].at[pl.ds(lo, hi - lo)] for a, (lo, hi) in enumerate(self.rows)]

        def copy(a, k, s):
            return pltpu.make_async_remote_copy(
                src_ref=src[a], dst_ref=out_refs[a].at[s], send_sem=send_sems.at[3 * a + k],
                recv_sem=recv_sems.at[3 * a + k], device_id=(*chips[k], c), device_id_type=MESH)

        mine = [pltpu.make_async_copy(src[a], out_refs[a].at[s_me], local_sems.at[a]) for a in range(n)]
        sends = [copy(a, k, s_me) for a in range(n) for k in range(3)]
        if not receiving:
            return mine, sends
        return mine, sends, [copy(a, k, 2 * px + py) for a in range(n) for k, (px, py) in enumerate(chips)]

    def start(self, x_refs, out_refs, sems):
        mine, sends = self._copies(x_refs, out_refs, sems, False)
        for cp in mine + sends:
            cp.start()

    def wait(self, x_refs, out_refs, sems):
        mine, sends, recvs = self._copies(x_refs, out_refs, sems, True)
        for cp in recvs:
            cp.wait_recv()
        for cp in sends:
            cp.wait_send()
        for cp in mine:
            cp.wait()


class _GatherChipsHalved(_GatherChips):
    def __init__(self, shards):
        super().__init__(shards)
        n = self.n_in
        self.scratch = [pltpu.SemaphoreType.DMA((6 * n,)), pltpu.SemaphoreType.DMA((6 * n,)),
                        pltpu.SemaphoreType.DMA((n,))]

    def _copies(self, x_refs, out_refs, sems, receiving):
        send_sems, recv_sems, local_sems = sems
        x, y, c, chips = _place()
        s_me = 2 * x + y
        n = self.n_in

        def half(a, who):
            rows = x_refs[a].shape[0] // 2
            return pl.ds(who * rows, rows)

        def over_chips(a, k, s):
            return pltpu.make_async_remote_copy(
                src_ref=x_refs[a].at[half(a, c)], dst_ref=out_refs[a].at[s, half(a, c)],
                send_sem=send_sems.at[3 * a + k], recv_sem=recv_sems.at[3 * a + k],
                device_id=(*chips[k], c), device_id_type=MESH)

        def to_sibling(a, k, s, who):
            return pltpu.make_async_remote_copy(
                src_ref=out_refs[a].at[s, half(a, who)], dst_ref=out_refs[a].at[s, half(a, who)],
                send_sem=send_sems.at[3 * n + 3 * a + k], recv_sem=recv_sems.at[3 * n + 3 * a + k],
                device_id=(x, y, 1 - c), device_id_type=MESH)

        mine = [pltpu.make_async_copy(x_refs[a], out_refs[a].at[s_me], local_sems.at[a]) for a in range(n)]
        sends = [over_chips(a, k, s_me) for a in range(n) for k in range(3)]
        if not receiving:
            return mine, sends
        pairs = [(a, k, 2 * px + py) for a in range(n) for k, (px, py) in enumerate(chips)]
        landed = [over_chips(a, k, s) for a, k, s in pairs]
        passed_on = [to_sibling(a, k, s, c) for a, k, s in pairs]
        from_sibling = [to_sibling(a, k, s, 1 - c) for a, k, s in pairs]
        return mine, sends, landed, passed_on, from_sibling

    def wait(self, x_refs, out_refs, sems):
        mine, sends, landed, passed_on, from_sibling = self._copies(x_refs, out_refs, sems, True)
        for got, fwd in zip(landed, passed_on, strict=True):
            got.wait_recv()
            fwd.start()
        for cp in from_sibling:
            cp.wait_recv()
        for cp in sends + passed_on:
            cp.wait_send()
        for cp in mine:
            cp.wait()


class _ScatterChips:
    def __init__(self, parts, rows=None):
        n = len(parts)
        self.rows = [r or (0, p.shape[1]) for p, r in zip(parts, rows or [None] * n, strict=True)]
        self.arrays, self.n_in, self.n_out = list(parts), n, n
        self.out_shape = [jax.ShapeDtypeStruct((3, hi - lo) + p.shape[2:], p.dtype)
                          for p, (lo, hi) in zip(parts, self.rows, strict=True)]
        self.scratch = [pltpu.SemaphoreType.DMA((3 * n,)), pltpu.SemaphoreType.DMA((3 * n,))]

    def _copies(self, g_refs, out_refs, sems):
        send_sems, recv_sems = sems
        x, y, c, chips = _place()
        return [pltpu.make_async_remote_copy(
            src_ref=g_refs[a].at[2 * px + py, pl.ds(lo, hi - lo)], dst_ref=out_refs[a].at[k],
            send_sem=send_sems.at[3 * a + k], recv_sem=recv_sems.at[3 * a + k], device_id=(px, py, c),
            device_id_type=MESH)
            for a, (lo, hi) in enumerate(self.rows) for k, (px, py) in enumerate(chips)]

    def start(self, g_refs, out_refs, sems):
        for cp in self._copies(g_refs, out_refs, sems):
            cp.start()

    def wait(self, g_refs, out_refs, sems):
        sends = self._copies(g_refs, out_refs, sems)
        for cp in sends:
            cp.wait_recv()
        for cp in sends:
            cp.wait_send()


def _exchange_call(ex, *, name):
    def body(*refs):
        parts = (refs[:ex.n_in], refs[ex.n_in:ex.n_in + ex.n_out], refs[ex.n_in + ex.n_out:])
        ex.start(*parts)
        ex.wait(*parts)

    return pl.pallas_call(body, name=name, out_shape=ex.out_shape, in_specs=[ANY] * ex.n_in,
                          out_specs=[ANY] * ex.n_out, scratch_shapes=ex.scratch)(*ex.arrays)


class _SwapSibling:
    def __init__(self, vs):
        n = len(vs)
        self.arrays, self.n_in, self.n_out = list(vs), n, n
        self.out_shape = [jax.ShapeDtypeStruct(v.shape, v.dtype) for v in vs]
        self.scratch = [pltpu.SemaphoreType.DMA((n,)), pltpu.SemaphoreType.DMA((n,))]

    def _copies(self, v_refs, out_refs, sems):
        send_sems, recv_sems = sems
        x, y, c, _ = _place()
        return [pltpu.make_async_remote_copy(src_ref=v_refs[a], dst_ref=out_refs[a], send_sem=send_sems.at[a],
                                             recv_sem=recv_sems.at[a], device_id=(x, y, 1 - c), device_id_type=MESH)
                for a in range(self.n_in)]

    def start(self, v_refs, out_refs, sems):
        for cp in self._copies(v_refs, out_refs, sems):
            cp.start()

    def wait(self, v_refs, out_refs, sems):
        for cp in self._copies(v_refs, out_refs, sems):
            cp.wait()


class _Both:
    def __init__(self, first, second):
        self.parts = (first, second)
        self.arrays = first.arrays + second.arrays
        self.n_in, self.n_out = first.n_in + second.n_in, first.n_out + second.n_out
        self.out_shape = first.out_shape + second.out_shape
        self.scratch = first.scratch + second.scratch

    def _split(self, in_refs, out_refs, sems):
        a, b = self.parts
        return ((a, in_refs[:a.n_in], out_refs[:a.n_out], sems[:len(a.scratch)]),
                (b, in_refs[a.n_in:], out_refs[a.n_out:], sems[len(a.scratch):]))

    def start(self, in_refs, out_refs, sems):
        for ex, *refs in self._split(in_refs, out_refs, sems):
            ex.start(*refs)

    def wait(self, in_refs, out_refs, sems):
        for ex, *refs in self._split(in_refs, out_refs, sems):
            ex.wait(*refs)


def _sum_parts(own, others, *, name):
    R, C = own.shape
    tm = _pick(R, (256, 128, 64))

    def body(own_ref, o0_ref, o1_ref, o2_ref, out_ref):
        tot = own_ref[...].astype(F32)
        for ref in (o0_ref, o1_ref, o2_ref):
            tot = tot + ref[...].astype(F32)
        out_ref[...] = tot

    part = lambda k: pl.BlockSpec((None, tm, C), lambda i: (k, i, 0))
    return pl.pallas_call(
        body, name=name, grid=(R // tm,),
        in_specs=[pl.BlockSpec((tm, C), lambda i: (i, 0)), part(0), part(1), part(2)],
        out_specs=pl.BlockSpec((tm, C), lambda i: (i, 0)), out_shape=jax.ShapeDtypeStruct((R, C), F32),
        compiler_params=_cparams(("arbitrary",)),
    )(own, others, others, others)


def _adam_math(w_, m_, v_, g):
    m2 = ADAM_B1 * m_ + (1.0 - ADAM_B1) * g
    v2 = ADAM_B2 * v_ + (1.0 - ADAM_B2) * jnp.square(g)
    m_hat = m2 / (1.0 - ADAM_B1 ** ADAM_STEP)
    v_hat = v2 / (1.0 - ADAM_B2 ** ADAM_STEP)
    delta = -ADAM_LR * (m_hat / (jnp.sqrt(v_hat) + ADAM_EPS) + ADAM_WD * w_)
    return delta, m2, v2


SMALL_SLOTS = 24
SMALL_COLS = 2 * D


def _small_rows(widths):
    firsts, row = [], 0
    for n_i in widths:
        firsts.append(row)
        row += -(-n_i // SMALL_COLS)
    assert row <= SMALL_SLOTS
    return firsts


def _pack_small(grads, *, name):
    n = len(grads)
    firsts = _small_rows([g.shape[1] for g in grads])

    def body(*refs):
        out_ref = refs[n]
        out_ref[...] = jnp.zeros_like(out_ref)
        for first, ref in zip(firsts, refs[:n], strict=True):
            for lo in range(0, ref.shape[1], SMALL_COLS):
                width = min(SMALL_COLS, ref.shape[1] - lo)
                row = first + lo // SMALL_COLS
                out_ref[row:row + 1, 0:width] = ref[:, lo:lo + width]

    return pl.pallas_call(body, name=name, out_shape=jax.ShapeDtypeStruct((SMALL_SLOTS, SMALL_COLS), F32))(*grads)


def _adamw_small(ws, ms, vs, gathered, *, name):
    n = len(ws)
    firsts = _small_rows([w.shape[1] for w in ws] + [128])

    def total(g_ref, first, nc):
        pieces = []
        for lo in range(0, nc, SMALL_COLS):
            width, row = min(SMALL_COLS, nc - lo), first + lo // SMALL_COLS
            g = g_ref[row:row + 1, 0:width]
            for d in range(1, 8):
                g = g + g_ref[d * SMALL_SLOTS + row:d * SMALL_SLOTS + row + 1, 0:width]
            pieces.append(g)
        return pieces[0] if len(pieces) == 1 else jnp.concatenate(pieces, axis=1)

    def body(*refs):
        w_refs, m_refs, v_refs, g_ref = refs[:n], refs[n:2 * n], refs[2 * n:3 * n], refs[3 * n]
        outs = refs[3 * n + 1:]
        for i in range(n):
            g = total(g_ref, firsts[i], w_refs[i].shape[1])
            delta, m2, v2 = _adam_math(w_refs[i][...], m_refs[i][...], v_refs[i][...], g)
            for k, val in enumerate((g, delta, m2, v2)):
                outs[k * n + i][...] = val
        outs[4 * n][...] = total(g_ref, firsts[n], 128)

    shapes = [jax.ShapeDtypeStruct(w.shape, F32) for w in ws]
    res = pl.pallas_call(body, name=name, out_shape=shapes * 4 + [jax.ShapeDtypeStruct((1, 128), F32)],
                         compiler_params=pltpu.CompilerParams(vmem_limit_bytes=VMEM_LIMIT))(*ws, *ms, *vs, gathered)
    return [res[k * n:(k + 1) * n] for k in range(4)], res[4 * n]


def _adamw(w, m, v, gparts, *, tm, name, hosted=None):
    def fn(i, n, w_, m_, v_, *gs):
        g = gs[0]
        for p in gs[1:]:
            g = g + p
        return (g,) + _adam_math(w_, m_, v_, g)
    nc = w.shape[1]
    return _rowwise(fn, [w, m, v] + list(gparts), [], [(nc, F32, 'row')] * 4, tm=tm, name=name, hosted=hosted)


WEIGHTS = ['ada_w', 'ada_b', 'norm1_gain', 'norm2_gain', 'w_in', 'tshift_mu', 'decay_w0', 'decay_up', 'iclr_a0',
           'iclr_up', 'gate_up', 'k_k', 'k_a', 'r_k', 'lnx_gain', 'lnx_bias', 'q_norm_gain', 'k_norm_gain', 'attn_sinks',
           'branch_gate_b', 'w_branch_a', 'w_branch_b', 'w_out', 'ffn_w1', 'ffn_w3', 'ffn_w2']
SHARDED = [('w_in', 1), ('decay_up', 1), ('iclr_up', 1), ('gate_up', 1), ('w_branch_a', 1), ('w_branch_b', 1),
           ('w_out', 0), ('ffn_w1', 1), ('ffn_w3', 1), ('ffn_w2', 0)]
SMALL = ['ada_b', 'norm1_gain', 'norm2_gain', 'tshift_mu', 'decay_w0', 'iclr_a0', 'k_k', 'k_a', 'r_k', 'lnx_gain',
         'lnx_bias', 'q_norm_gain', 'k_norm_gain', 'attn_sinks', 'branch_gate_b']


def kernel(x, c, positions, ada_w, ada_b, norm1_gain, norm2_gain, w_in, tshift_mu, decay_w0, decay_up, iclr_a0, iclr_up, gate_up, k_k, k_a, r_k, lnx_gain, lnx_bias, q_norm_gain, k_norm_gain, attn_sinks, branch_gate_b, w_branch_a, w_branch_b, w_out, ffn_w1, ffn_w3, ffn_w2, loss_target, m_ada_w, m_ada_b, m_norm1_gain, m_norm2_gain, m_w_in, m_tshift_mu, m_decay_w0, m_decay_up, m_iclr_a0, m_iclr_up, m_gate_up, m_k_k, m_k_a, m_r_k, m_lnx_gain, m_lnx_bias, m_q_norm_gain, m_k_norm_gain, m_attn_sinks, m_branch_gate_b, m_w_branch_a, m_w_branch_b, m_w_out, m_ffn_w1, m_ffn_w3, m_ffn_w2, v_ada_w, v_ada_b, v_norm1_gain, v_norm2_gain, v_w_in, v_tshift_mu, v_decay_w0, v_decay_up, v_iclr_a0, v_iclr_up, v_gate_up, v_k_k, v_k_a, v_r_k, v_lnx_gain, v_lnx_bias, v_q_norm_gain, v_k_norm_gain, v_attn_sinks, v_branch_gate_b, v_w_branch_a, v_w_branch_b, v_w_out, v_ffn_w1, v_ffn_w3, v_ffn_w2):
    a = dict(locals())
    W = {n: a[n] for n in WEIGHTS}
    M = {n: a['m_' + n] for n in WEIGHTS}
    V = {n: a['v_' + n] for n in WEIGHTS}
    xi, yi, ci = lax.axis_index("x"), lax.axis_index("y"), lax.axis_index("c")
    me = 4 * xi + 2 * yi + ci
    shard = 2 * xi + yi
    mat = lambda t: t.reshape(t.shape[-2], t.shape[-1])
    sharded = [n for n, _ in SHARDED]

    ax = dict(SHARDED)
    late = LATE
    early = [n for n in sharded if n not in late]
    shards = {n: mat(W[n]).astype(MXU) for n in sharded}
    gathered = _exchange_call(_GatherChipsHalved([shards[n] for n in early]), name="gather_weights")
    full = {n: _full_weight(g, ax[n]) for n, g in zip(early, gathered, strict=True)}

    c_all = _all_gather8(jnp.broadcast_to(c, (8, D)), name="gather_c")[0::8]
    pad_rows = lambda t: jnp.concatenate([t, jnp.zeros((BLK - 8, t.shape[1]), t.dtype)])
    c_all = pad_rows(c_all.astype(MXU))
    ada_cols = _mm_nn(c_all, mat(ada_w).astype(MXU), name="f_ada")[:8]
    ada_all = _all_gather8(ada_cols, name="gather_ada").reshape(2, 2, 2, 8, 6 * D // 4)
    ada_mine = lax.dynamic_index_in_dim(ada_all[:, :, 0], me, axis=2, keepdims=False)
    ada = ada_mine.reshape(1, 6 * D) + mat(ada_b)

    zero = jnp.zeros((64, RW), MXU)
    lora = jnp.concatenate([jnp.concatenate([full['decay_up'], zero], axis=1),
                            jnp.concatenate([zero, full['iclr_up']], axis=1)], axis=0)
    s = {n: W[n].reshape(1, -1) for n in SMALL if n != 'ada_b'}
    s['lora_up'] = lora.astype(F32)
    s['gate_up'] = full['gate_up'].astype(F32)
    tab = _rope_table(positions.reshape(-1))
    loss, dx, d_ada, gw, gs, from_chips = _local_step(x[0], loss_target[0], ada, tab, dict(w_in=full['w_in']), s,
                                                      shards={n: shards[n] for n in late})

    gs['ada_b'] = d_ada
    gsmall = _pack_small([gs[n] for n in SMALL] + [loss], name="pack_small_grads")
    gsmall_all = _all_gather8(gsmall, name="gather_small_grads")
    row = lambda src: [src[n].reshape(1, -1) for n in SMALL]
    sm_out, loss = _adamw_small(row(W), row(M), row(V), gsmall_all, name="adamw_small")
    sm_out = [{n: o.reshape(W[n].shape) for n, o in zip(SMALL, outs_k, strict=True)} for outs_k in sm_out]
    loss = loss[0, 0]

    ada_rows = 6 * D // SMALL_COLS
    d_ada_all = gsmall_all.reshape(8, SMALL_SLOTS, SMALL_COLS)[:, :ada_rows].reshape(8, 6 * D)
    d_ada_cols = lax.dynamic_slice_in_dim(d_ada_all, shard * (6 * D // 4), 6 * D // 4, axis=1)
    g_ada_w = _mm_tn(c_all, pad_rows(d_ada_cols.astype(MXU)), name="b_ada")

    rest = [n for n in sharded if n != 'w_in']
    parts = {n: _sum_parts(lax.dynamic_index_in_dim(gw[n], shard, axis=0, keepdims=False), from_chips[n],
                           name="sum_" + n) for n in rest}
    tail = _Both(_ScatterChips([gw['w_in']], rows=[(3 * D // 4, D)]), _SwapSibling([parts[n] for n in rest]))
    res = _adamw(mat(ada_w), mat(m_ada_w), mat(v_ada_w), [g_ada_w], tm=256, name="adamw_ada", hosted=tail)
    ada_out, last_quarter, others = res[:4], res[4], dict(zip(rest, res[5:], strict=True))
    parts['w_in'] = _sum_parts(lax.dynamic_index_in_dim(gw['w_in'], shard, axis=0, keepdims=False),
                               jnp.concatenate(from_chips['w_in'] + [last_quarter], axis=1), name="sum_w_in")
    others['w_in'] = _exchange_call(_SwapSibling([parts['w_in']]), name="swap_w_in")[0]
    sh_out = {}
    for n in sharded:
        part, other = parts[n], others[n]
        sh_out[n] = _adamw(mat(W[n]), mat(M[n]), mat(V[n]), [part, other], tm=_pick(part.shape[0], (256, 128, 64)),
                           name="adamw_" + n)

    def leaf(k, n):
        if n == 'ada_w':
            return ada_out[k].reshape(W[n].shape)
        if n in sharded:
            return sh_out[n][k].reshape(W[n].shape)
        return sm_out[k][n]
    outs = [leaf(k, n) for k in range(4) for n in WEIGHTS]
    return (loss, dx[None], *outs)
```

```python
import functools
import math

import jax
import jax.numpy as jnp
from jax import lax
from jax.experimental import pallas as pl
from jax.experimental.pallas import tpu as pltpu

F32 = jnp.float32
BF16 = jnp.bfloat16
MXU = BF16
HI = lax.Precision.HIGHEST

D = 1024
HD = 64
NH = 8
RW = NH * HD
SHIFT_W = 3 * RW + 64 + 64 + 128
QKV_W = RW + 2 * 128
GATE_W = 2 * D
IN_W = SHIFT_W + QKV_W + GATE_W
DFF = 2816
BLK = 128
CHUNK = 64
RMS_EPS = 1e-6
GN_EPS = 64e-5
NEG_INF = -1e30
ADAM_LR, ADAM_B1, ADAM_B2, ADAM_EPS, ADAM_WD, ADAM_STEP = 0.001, 0.9, 0.999, 1e-08, 0.01, 10
VMEM_LIMIT = 56 * 1024 * 1024
MESH = pl.DeviceIdType.MESH


def _cparams(sem=None):
    return pltpu.CompilerParams(dimension_semantics=sem, vmem_limit_bytes=VMEM_LIMIT)


def _full_spec(a):
    nd = a.ndim
    return pl.BlockSpec(a.shape, lambda *_: (0,) * nd)


def _rowwise(fn, rows, consts, outs, *, tm, name, halo=(), hosted=None):
    rows = [(a + (0,))[:3] if isinstance(a, tuple) else (a, a.shape[1], 0) for a in rows]
    T = rows[0][0].shape[0]
    assert T % tm == 0 and tm % 8 == 0
    n_tiles = T // tm
    n_in = len(rows) + len(halo) + len(consts)
    in_specs = [pl.BlockSpec((tm, nc), lambda i, j=j: (i, j)) for _, nc, j in rows]
    args = [a for a, _, _ in rows]
    for a, nc, kind in halo:
        if kind == 'prev':
            in_specs.append(pl.BlockSpec((8, nc), lambda i: (jnp.maximum(i * (tm // 8) - 1, 0), 0)))
        else:
            in_specs.append(pl.BlockSpec((8, nc), lambda i: (jnp.minimum((i + 1) * (tm // 8), T // 8 - 1), 0)))
        args.append(a)
    in_specs += [_full_spec(a) for a in consts]
    args += list(consts)
    out_shape, out_specs = [], []
    for ncols, dtype, kind in outs:
        if kind == 'row':
            out_shape.append(jax.ShapeDtypeStruct((T, ncols), dtype))
            out_specs.append(pl.BlockSpec((tm, ncols), lambda i: (i, 0)))
        else:
            out_shape.append(jax.ShapeDtypeStruct((kind, ncols), dtype))
            out_specs.append(pl.BlockSpec((kind, ncols), lambda i: (0, 0)))

    def body(*refs):
        i = pl.program_id(0)
        vals = [r[...] for r in refs[:n_in]]
        res = fn(i, n_tiles, *vals)
        for (ncols, dtype, kind), o_ref, val in zip(outs, refs[n_in:], res, strict=True):
            if kind == 'row':
                o_ref[...] = val.astype(dtype)
            else:
                @pl.when(i == 0)
                def _():
                    o_ref[...] = jnp.zeros_like(o_ref)
                o_ref[...] += val.astype(dtype)

    h_in, h_in_specs, h_out_specs, h_out_shape, h_scratch = _hosted_args(hosted)
    res = pl.pallas_call(
        _hosting(body, hosted, n_in, len(outs), 0, n_tiles), name=name, grid=(n_tiles,),
        in_specs=in_specs + h_in_specs, out_specs=out_specs + h_out_specs, out_shape=out_shape + h_out_shape,
        scratch_shapes=h_scratch, compiler_params=_cparams(("arbitrary",)),
    )(*args, *h_in)
    return res


def _pick(n, cands):
    for c in cands:
        if n % c == 0:
            return c
    return n


MM_ROWS = (1024, 512, 256, 128)
MM_COLS = (1536, 1408, 1024, 896, 768, 512, 256, 128)
MM_WIDE = 3000


def _mm_nn(a, w, *, name, out_dtype=F32, hosted=None):
    T, K = a.shape
    N = w.shape[1]
    tm = _pick(T, MM_ROWS)
    tn = _pick(N, MM_COLS)
    grid = (N // tn, T // tm)
    h_in, h_in_specs, h_out_specs, h_out_shape, h_scratch = _hosted_args(hosted)

    def body(a_ref, w_ref, o_ref):
        o_ref[...] = jnp.dot(a_ref[...], w_ref[...], preferred_element_type=F32).astype(out_dtype)

    res = pl.pallas_call(
        _hosting(body, hosted, 2, 1, 0, grid), name=name, grid=grid,
        in_specs=[pl.BlockSpec((tm, K), lambda j, i: (i, 0)), pl.BlockSpec((K, tn), lambda j, i: (0, j))] + h_in_specs,
        out_specs=[pl.BlockSpec((tm, tn), lambda j, i: (i, j))] + h_out_specs,
        out_shape=[jax.ShapeDtypeStruct((T, N), out_dtype)] + h_out_shape, scratch_shapes=h_scratch,
        compiler_params=_cparams(("arbitrary", "arbitrary")),
    )(a, w, *h_in)
    return res if hosted else res[0]


def _mm_nt(dy, w, *, name, out_dtype=F32, hosted=None):
    T, N = dy.shape
    K = w.shape[0]
    tm = _pick(T, MM_ROWS if N <= MM_WIDE else MM_ROWS[1:])
    tk = _pick(K, MM_COLS[1:])
    grid = (K // tk, T // tm)
    h_in, h_in_specs, h_out_specs, h_out_shape, h_scratch = _hosted_args(hosted)

    def body(dy_ref, w_ref, o_ref):
        o_ref[...] = lax.dot_general(dy_ref[...], w_ref[...], (((1,), (1,)), ((), ())),
                                     preferred_element_type=F32).astype(out_dtype)

    res = pl.pallas_call(
        _hosting(body, hosted, 2, 1, 0, grid), name=name, grid=grid,
        in_specs=[pl.BlockSpec((tm, N), lambda j, i: (i, 0)), pl.BlockSpec((tk, N), lambda j, i: (j, 0))] + h_in_specs,
        out_specs=[pl.BlockSpec((tm, tk), lambda j, i: (i, j))] + h_out_specs,
        out_shape=[jax.ShapeDtypeStruct((T, K), out_dtype)] + h_out_shape, scratch_shapes=h_scratch,
        compiler_params=_cparams(("arbitrary", "arbitrary")),
    )(dy, w, *h_in)
    return res if hosted else res[0]


def _mm_tn(a, dy, *, name, out_dtype=F32, col_shards=None):
    T, K = a.shape
    N = dy.shape[1]
    tm = _pick(T, MM_ROWS)
    tn = N // col_shards if col_shards else _pick(N, MM_COLS[1:])
    n_t = T // tm

    def body(a_ref, dy_ref, o_ref, acc_ref):
        i = pl.program_id(1)

        @pl.when(i == 0)
        def _():
            acc_ref[...] = jnp.zeros_like(acc_ref)

        acc_ref[...] += lax.dot_general(a_ref[...], dy_ref[...], (((0,), (0,)), ((), ())), preferred_element_type=F32)

        @pl.when(i == n_t - 1)
        def _():
            o_ref[...] = acc_ref[...].astype(out_dtype)

    if col_shards:
        out_specs = pl.BlockSpec((None, K, tn), lambda j, i: (j, 0, 0))
        out_shape = jax.ShapeDtypeStruct((col_shards, K, tn), out_dtype)
    else:
        out_specs = pl.BlockSpec((K, tn), lambda j, i: (0, j))
        out_shape = jax.ShapeDtypeStruct((K, N), out_dtype)
    return pl.pallas_call(
        body, name=name, grid=(N // tn, n_t),
        in_specs=[pl.BlockSpec((tm, K), lambda j, i: (i, 0)), pl.BlockSpec((tm, tn), lambda j, i: (i, j))],
        out_specs=out_specs, out_shape=out_shape, scratch_shapes=[pltpu.VMEM((K, tn), F32)],
        compiler_params=_cparams(("arbitrary", "arbitrary")),
    )(a, dy)


def _mm_then(products, fn, rows, consts, outs, *, tm, name, lhs_fn=None, hosted=None):
    products = [(p + (None,))[:4] for p in products]
    T = (rows[0] if lhs_fn else products[0][0]).shape[-2]
    n_tiles = T // tm
    in_specs, args = [], []
    for a, w, _, j in products:
        if a is not None and j is None:
            in_specs.append(pl.BlockSpec((tm, a.shape[1]), lambda i: (i, 0)))
            args.append(a)
        elif a is not None:
            in_specs.append(pl.BlockSpec((None, tm, a.shape[2]), lambda i, j=j: (j, i, 0)))
            args.append(a)
        in_specs.append(_full_spec(w) if j is None else
                        pl.BlockSpec((None,) + w.shape[1:], lambda i, j=j: (j, 0, 0)))
        args.append(w)
    n_w = len(args)
    in_specs += [pl.BlockSpec((tm, a.shape[1]), lambda i: (i, 0)) for a in rows] + [_full_spec(c_) for c_ in consts]
    args += list(rows) + list(consts)
    n_in = len(args)
    out_shape, out_specs = [], []
    for ncols, dtype, kind in outs:
        if kind == 'row':
            out_shape.append(jax.ShapeDtypeStruct((T, ncols), dtype))
            out_specs.append(pl.BlockSpec((tm, ncols), lambda i: (i, 0)))
        else:
            out_shape.append(jax.ShapeDtypeStruct((kind, ncols), dtype))
            out_specs.append(pl.BlockSpec((kind, ncols), lambda i: (0, 0)))

    def body(*refs):
        i = pl.program_id(0)
        tiles = [r[...] for r in refs[n_w:n_in]]
        made = []
        if lhs_fn:
            made = lhs_fn(*tiles)
            made = list(made) if isinstance(made, tuple) else [made]
            made[0] = made[0].astype(MXU)
        y, pos = None, 0
        for a, _, form, _ in products:
            if a is None:
                lhs = made[0]
            else:
                lhs, pos = refs[pos][...], pos + 1
            dims = (((1,), (0,)), ((), ())) if form == 'nn' else (((1,), (1,)), ((), ()))
            t = lax.dot_general(lhs, refs[pos][...], dims, preferred_element_type=F32)
            pos += 1
            y = t if y is None else y + t
        res = fn(i, n_tiles, y, *made, *tiles)
        for (ncols, dtype, kind), o_ref, val in zip(outs, refs[n_in:], res, strict=True):
            if kind == 'row':
                o_ref[...] = val.astype(dtype)
            else:
                @pl.when(i == 0)
                def _():
                    o_ref[...] = jnp.zeros_like(o_ref)
                o_ref[...] += val.astype(dtype)

    h_in, h_in_specs, h_out_specs, h_out_shape, h_scratch = _hosted_args(hosted)
    return pl.pallas_call(
        _hosting(body, hosted, n_in, len(outs), 0, n_tiles), name=name, grid=(n_tiles,),
        in_specs=in_specs + h_in_specs, out_specs=out_specs + h_out_specs, out_shape=out_shape + h_out_shape,
        scratch_shapes=h_scratch, compiler_params=_cparams(("arbitrary",)))(*args, *h_in)


def _seg_ones(n):
    r = lax.broadcasted_iota(jnp.int32, (n, n), 0) // HD
    c = lax.broadcasted_iota(jnp.int32, (n, n), 1) // HD
    return (r == c).astype(F32)


def _segsum_raw(x):
    ones = _seg_ones(x.shape[1])
    if MXU == F32:
        return jnp.dot(x, ones, precision=HI, preferred_element_type=F32)
    hi = x.astype(MXU)
    lo = (x - hi.astype(F32)).astype(MXU)
    ones = ones.astype(MXU)
    return jnp.dot(hi, ones, preferred_element_type=F32) + jnp.dot(lo, ones, preferred_element_type=F32)


@jax.custom_vjp
def _segsum(x):
    return _segsum_raw(x)


def _segsum_fwd(x):
    return _segsum_raw(x), None


def _segsum_bwd(_, g):
    return (_segsum_raw(g),)


_segsum.defvjp(_segsum_fwd, _segsum_bwd)


def _mxu(x):
    return x.astype(MXU)


@jax.custom_vjp
def _bdot(a, b):
    return jnp.dot(_mxu(a), _mxu(b), preferred_element_type=F32)


def _bdot_fwd(a, b):
    return _bdot(a, b), (a, b)


def _bdot_bwd(res, g):
    a, b = res
    da = lax.dot_general(_mxu(g), _mxu(b), (((1,), (1,)), ((), ())), preferred_element_type=F32)
    db = lax.dot_general(_mxu(a), _mxu(g), (((0,), (0,)), ((), ())), preferred_element_type=F32)
    return da.astype(a.dtype), db.astype(b.dtype)


_bdot.defvjp(_bdot_fwd, _bdot_bwd)


def _sigmoid(x):
    return 1.0 / (1.0 + jnp.exp(-x))


def _softplus(x):
    return jnp.maximum(x, 0.0) + jnp.log(1.0 + jnp.exp(jnp.minimum(x, -x)))


def _norm_mod(x, gain, scale, shift):
    inv = lax.rsqrt(jnp.mean(x * x, axis=-1, keepdims=True) + RMS_EPS)
    return (x * inv) * gain * (1.0 + scale) + shift


def _prep(mixed, decay_w0, lora_up, iclr_a0, gate_up, k_k, k_a):
    r = mixed[:, 0:RW]
    k = mixed[:, RW:2 * RW]
    v = mixed[:, 2 * RW:3 * RW]
    z = mixed[:, 3 * RW:3 * RW + 128]
    xg = mixed[:, 3 * RW + 128:]
    lane = lax.broadcasted_iota(jnp.int32, z.shape, 1)
    tz = jnp.where(lane < 64, jnp.tanh(z), z)
    lo = _bdot(tz, lora_up)
    w_log = -_softplus(-(decay_w0 + lo[:, :RW])) - 0.5
    lw = -jnp.exp(w_log)
    a_ic = _sigmoid(iclr_a0 + lo[:, RW:])
    g = _bdot(_sigmoid(xg), gate_up)
    kk = k * k_k
    kk = kk / jnp.maximum(jnp.sqrt(_segsum(kk * kk)), 1e-12)
    k_mod = k * (1.0 + (a_ic - 1.0) * k_a)
    return jnp.concatenate([r, lw, k_mod, v, -kk, kk * a_ic, g], axis=1)


def _post(y, r, k, v, g, lnx_gain, lnx_bias, r_k):
    mu = _segsum(y) * (1.0 / HD)
    yc = y - mu
    var = _segsum(yc * yc) * (1.0 / HD)
    yn = yc * lax.rsqrt(var + GN_EPS) * lnx_gain + lnx_bias
    bonus = _segsum(r * k * r_k) * v
    return (yn + bonus) * g


def _merge(pg, ma, mb, bias):
    gates = _sigmoid(pg + bias)
    return gates[:, :D] * ma + gates[:, D:] * mb


def _swiglu(u, v):
    return u * _sigmoid(u) * v


def _ffn_in(h, w1, w3, *, name):
    T, K = h.shape
    ns, _, Fs = w1.shape
    tm = _pick(T, MM_ROWS)

    def body(h_ref, w1_ref, w3_ref, u_ref, v_ref, a_ref):
        u = jnp.dot(h_ref[...], w1_ref[...], preferred_element_type=F32).astype(MXU)
        v = jnp.dot(h_ref[...], w3_ref[...], preferred_element_type=F32).astype(MXU)
        u_ref[...] = u
        v_ref[...] = v
        a_ref[...] = _swiglu(u.astype(F32), v.astype(F32)).astype(MXU)

    wspec = pl.BlockSpec((None, K, Fs), lambda j, i: (j, 0, 0))
    ospec = pl.BlockSpec((None, tm, Fs), lambda j, i: (j, i, 0))
    return pl.pallas_call(
        body, name=name, grid=(ns, T // tm),
        in_specs=[pl.BlockSpec((tm, K), lambda j, i: (i, 0)), wspec, wspec],
        out_specs=[ospec] * 3, out_shape=[jax.ShapeDtypeStruct((ns, T, Fs), MXU)] * 3,
        compiler_params=_cparams(("arbitrary", "arbitrary")),
    )(h, w1, w3)


def _ffn_act_bwd(dff, w2, u, v, *, name):
    T, N = dff.shape
    ns, Fs, _ = w2.shape
    tm = _pick(T, MM_ROWS)

    def body(dy_ref, w_ref, u_ref, v_ref, du_ref, dv_ref):
        dact = lax.dot_general(dy_ref[...], w_ref[...], (((1,), (1,)), ((), ())), preferred_element_type=F32)
        _, vjp = jax.vjp(_swiglu, u_ref[...].astype(F32), v_ref[...].astype(F32))
        du, dv = vjp(dact)
        du_ref[...] = du.astype(MXU)
        dv_ref[...] = dv.astype(MXU)

    tile = pl.BlockSpec((None, tm, Fs), lambda j, i: (j, i, 0))
    return pl.pallas_call(
        body, name=name, grid=(ns, T // tm),
        in_specs=[pl.BlockSpec((tm, N), lambda j, i: (i, 0)), pl.BlockSpec((None, Fs, N), lambda j, i: (j, 0, 0)),
                  tile, tile],
        out_specs=[tile, tile], out_shape=[jax.ShapeDtypeStruct((ns, T, Fs), MXU)] * 2,
        compiler_params=_cparams(("arbitrary", "arbitrary")),
    )(dff, w2, u, v)


def _mm_tn_blocks(a, dy, *, name, out_dtype):
    a3, d3 = a.ndim == 3, dy.ndim == 3
    ns = a.shape[0] if a3 else dy.shape[0]
    T, K, N = a.shape[-2], a.shape[-1], dy.shape[-1]
    tm = _pick(T, MM_ROWS)
    n_t = T // tm

    def body(a_ref, dy_ref, o_ref, acc_ref):
        i = pl.program_id(1)

        @pl.when(i == 0)
        def _():
            acc_ref[...] = jnp.zeros_like(acc_ref)

        acc_ref[...] += lax.dot_general(a_ref[...], dy_ref[...], (((0,), (0,)), ((), ())), preferred_element_type=F32)

        @pl.when(i == n_t - 1)
        def _():
            o_ref[...] = acc_ref[...].astype(out_dtype)

    spec = lambda is3, n: (pl.BlockSpec((None, tm, n), lambda j, i: (j, i, 0)) if is3
                           else pl.BlockSpec((tm, n), lambda j, i: (i, 0)))
    return pl.pallas_call(
        body, name=name, grid=(ns, n_t), in_specs=[spec(a3, K), spec(d3, N)],
        out_specs=pl.BlockSpec((None, K, N), lambda j, i: (j, 0, 0)),
        out_shape=jax.ShapeDtypeStruct((ns, K, N), out_dtype), scratch_shapes=[pltpu.VMEM((K, N), F32)],
        compiler_params=_cparams(("arbitrary", "arbitrary")),
    )(a, dy)


@functools.partial(jax.custom_vjp, nondiff_argnums=(1,))
def _lane_roll(x, s):
    return pltpu.roll(x, s, 1)


def _lane_roll_fwd(x, s):
    return pltpu.roll(x, s, 1), None


def _lane_roll_bwd(s, _, g):
    n = g.shape[1]
    return (pltpu.roll(g, (n - s) % n, 1),)


_lane_roll.defvjp(_lane_roll_fwd, _lane_roll_bwd)


def _rope(x, cos, sin_lo, sin_hi):
    n = x.shape[1]
    return x * cos + _lane_roll(x, n - 8) * sin_lo + _lane_roll(x, 8) * sin_hi


def _head_rms(x, gain):
    return x * lax.rsqrt(_segsum(x * x) * (1.0 / HD) + RMS_EPS) * gain


def _attn_blocks(qkv_c, qkv_p, tab_c, tab_p, qg, kg, sinks, first):
    nb = qkv_c.shape[0] // BLK
    G = 4

    def tabs(tab, n):
        return [jnp.tile(tab[:, j * 128:(j + 1) * 128], (1, n // 128)) for j in range(3)]

    qg = jnp.concatenate([qg] * NH, axis=1)
    kg = jnp.concatenate([kg] * 2, axis=1)
    q = _rope(_head_rms(qkv_c[:, :RW], qg), *tabs(tab_c, RW))
    k_in = jnp.concatenate([qkv_p[:, RW:RW + 128], qkv_c[:, RW:RW + 128]], axis=0)
    k = _rope(_head_rms(k_in, kg), *tabs(jnp.concatenate([tab_p, tab_c], axis=0), 128))
    v = jnp.concatenate([qkv_p[:, RW + 128:], qkv_c[:, RW + 128:]], axis=0)

    pile = lambda xs: jnp.concatenate([x_[None] for x_ in xs], axis=0)

    def bands(t):
        return pile([t[b * BLK:(b + 2) * BLK, kvh * HD:(kvh + 1) * HD] for kvh in range(2) for b in range(nb)])

    qs = pile([jnp.concatenate([q[b * BLK:(b + 1) * BLK, (G * kvh + g) * HD:(G * kvh + g + 1) * HD]
                                for g in range(G)], axis=0) for kvh in range(2) for b in range(nb)])
    s = _bmm(qs, bands(k), 2, 2, 1) * (HD ** -0.5)
    qi = lax.broadcasted_iota(jnp.int32, (G * BLK, 2 * BLK), 0) % BLK
    kj = lax.broadcasted_iota(jnp.int32, (G * BLK, 2 * BLK), 1)
    dist = qi + BLK - kj
    in_band = (dist >= 0) & (dist < BLK)
    pair = lax.broadcasted_iota(jnp.int32, (2 * nb, 1, 1), 0)
    no_prev = (pair % nb == 0) & first
    valid = in_band[None] & (jnp.logical_not(no_prev) | (kj >= BLK)[None])
    s = jnp.where(valid, s, NEG_INF)
    row_g = lax.broadcasted_iota(jnp.int32, (G * BLK, 1), 0) // BLK
    sink = []
    for kvh in range(2):
        col = jnp.zeros((G * BLK, 1), F32)
        for g in range(G):
            col = jnp.where(row_g == g, sinks[:, G * kvh + g:G * kvh + g + 1], col)
        sink += [col] * nb
    sink = pile(sink)
    m = lax.stop_gradient(jnp.maximum(jnp.max(s, axis=-1, keepdims=True), sink))
    e = jnp.exp(s - m)
    p = e * (1.0 / (jnp.sum(e, axis=-1, keepdims=True) + jnp.exp(sink - m)))
    o = _bmm(p, bands(v), 2, 1, 1)
    return jnp.concatenate([jnp.concatenate([o[kvh * nb + b, g * BLK:(g + 1) * BLK] for kvh in range(2)
                                             for g in range(G)], axis=1) for b in range(nb)], axis=0)


def _heads(x):
    return jnp.stack([x[:, h * HD:(h + 1) * HD] for h in range(NH)], axis=0)


def _unheads(x):
    return jnp.concatenate([x[h] for h in range(NH)], axis=1)


def _split(x, n):
    parts, rest = [], x
    for _ in range(n):
        p = rest.astype(MXU)
        parts.append(p)
        rest = rest - p.astype(F32)
    return parts


def _bdot_batched(a, b, ca, cb):
    return lax.dot_general(a, b, (((ca,), (cb,)), ((0,), (0,))), preferred_element_type=F32)


def _bmm_passes(a, b, ca, cb, passes):
    if MXU == F32:
        return lax.dot_general(a, b, (((ca,), (cb,)), ((0,), (0,))), precision=HI, preferred_element_type=F32)
    if passes == 1:
        return _bdot_batched(a.astype(MXU), b.astype(MXU), ca, cb)
    (a0, a1), (b0, b1) = _split(a, 2), _split(b, 2)
    return _bdot_batched(a0, b0, ca, cb) + (_bdot_batched(a0, b1, ca, cb) + _bdot_batched(a1, b0, ca, cb))


@functools.partial(jax.custom_vjp, nondiff_argnums=(2, 3, 4))
def _bmm(a, b, ca, cb, passes=1):
    return _bmm_passes(a, b, ca, cb, passes)


def _bmm_fwd(a, b, ca, cb, passes):
    return _bmm_passes(a, b, ca, cb, passes), (a, b)


def _bmm_bwd(ca, cb, passes, res, g):
    a, b = res
    if (ca, cb) == (2, 1):
        return _bmm_passes(g, b, 2, 2, passes), _bmm_passes(a, g, 1, 1, passes)
    if (ca, cb) == (2, 2):
        return _bmm_passes(g, b, 2, 1, passes), _bmm_passes(g, a, 1, 1, passes)
    return _bmm_passes(b, g, 2, 2, passes), _bmm_passes(a, g, 2, 1, passes)


_bmm.defvjp(_bmm_fwd, _bmm_bwd)


def _tri_dot(x, transpose):
    C = x.shape[1]
    ri = lax.broadcasted_iota(jnp.int32, (C, C), 0)
    ci = lax.broadcasted_iota(jnp.int32, (C, C), 1)
    tri = jnp.broadcast_to(((ri <= ci) if transpose else (ri >= ci)).astype(MXU), (x.shape[0], C, C))
    if MXU == F32:
        return lax.dot_general(tri, x, (((2,), (1,)), ((0,), (0,))), precision=HI, preferred_element_type=F32)
    p0, p1, p2 = _split(x, 3)
    return _bdot_batched(tri, p0, 2, 1) + (_bdot_batched(tri, p1, 2, 1) + _bdot_batched(tri, p2, 2, 1))


@jax.custom_vjp
def _cumsum_rows(x):
    return _tri_dot(x, False)


def _cumsum_rows_fwd(x):
    return _tri_dot(x, False), None


def _cumsum_rows_bwd(_, g):
    return (_tri_dot(g, True),)


_cumsum_rows.defvjp(_cumsum_rows_fwd, _cumsum_rows_bwd)

P_SCORE = 1
P_SOLVE = 1
P_STATE = 1
SCAN_CHUNKS = (4, 2, 1)


def _neumann(l):
    C = l.shape[1]
    eye = (lax.broadcasted_iota(jnp.int32, (C, C), 0) == lax.broadcasted_iota(jnp.int32, (C, C), 1)).astype(F32)
    x, lp = eye + l, l
    for _ in range(int(math.log2(C)) - 1):
        lp = _bmm(lp, lp, 2, 1, P_SOLVE)
        x = x + _bmm(x, lp, 2, 1, P_SOLVE)
    return x


@jax.custom_vjp
def _unit_lower_inverse(l):
    return _neumann(l)


def _unit_lower_inverse_fwd(l):
    x = _neumann(l)
    return x, x


def _unit_lower_inverse_bwd(x, g):
    return (_bmm(_bmm(x, g, 1, 1, P_SOLVE), x, 2, 2, P_SOLVE),)


_unit_lower_inverse.defvjp(_unit_lower_inverse_fwd, _unit_lower_inverse_bwd)


def _known_inverse(x):
    @jax.custom_vjp
    def f(l):
        return x

    f.defvjp(lambda l: (x, None), lambda _, g: (_bmm(_bmm(x, g, 1, 1, P_SOLVE), x, 2, 2, P_SOLVE),))
    return f


def _chunk(S0, r, lw, k, v, a, b, inverse=None):
    C = CHUNK
    n = r.shape[1] // C
    fold = lambda t: t.reshape(NH * n, C, HD)
    r, lw, k, v, a, b = (fold(t) for t in (r, lw, k, v, a, b))
    ri = lax.broadcasted_iota(jnp.int32, (C, C), 0)
    ci = lax.broadcasted_iota(jnp.int32, (C, C), 1)
    strict = (ri > ci)
    cum = _cumsum_rows(lw)
    p_in = jnp.exp(cum)
    p_ex = jnp.exp(cum - lw)
    p_inv = jnp.exp(-cum)
    at, rt, bt, kt = a * p_ex, r * p_in, b * p_inv, k * p_inv
    lhs = jnp.concatenate([at, rt], axis=1)
    rhs_ = jnp.concatenate([bt, kt], axis=1)
    sc = _bmm(lhs, rhs_, 2, 2, P_SCORE)
    a_ab = jnp.where(strict, sc[:, :C, :C], 0.0)
    a_ak = jnp.where(strict, sc[:, :C, C:], 0.0)
    incl2 = (lax.broadcasted_iota(jnp.int32, (C, 2 * C), 0) >= lax.broadcasted_iota(jnp.int32, (C, 2 * C), 1) % C)
    a_r = jnp.where(incl2, sc[:, C:, :], 0.0)
    av = _bmm(a_ak, v, 2, 1, P_SCORE)
    x = x_all = (_unit_lower_inverse if inverse is None else _known_inverse(inverse))(a_ab)
    p_last = jnp.exp(cum[:, C - 1:C, :])
    per_chunk = lambda t: t.reshape((NH, n) + t.shape[1:])
    lhs, rhs_, a_r, av, x, v, p_last = (per_chunk(t) for t in (lhs, rhs_, a_r, av, x, v, p_last))
    S, ys = S0, []
    for c in range(n):
        s0 = _bmm(lhs[:, c], S, 2, 2, P_STATE)
        u = _bmm(x[:, c], s0[:, :C] + av[:, c], 2, 1, P_SOLVE)
        uv = jnp.concatenate([u, v[:, c]], axis=1)
        ys.append(s0[:, C:] + _bmm(a_r[:, c], uv, 2, 1, P_SCORE))
        S = (S + _bmm(uv, rhs_[:, c], 1, 1, P_STATE)) * p_last[:, c]
    return jnp.concatenate(ys, axis=1), S, x_all


def _hosting(body, ex, n_in, n_out, n_scratch, n_steps):
    if ex is None:
        return body

    def wrapped(*refs):
        a = n_in
        b = a + ex.n_in
        c = b + n_out
        d = c + ex.n_out
        e = d + n_scratch
        ex_refs = (refs[a:b], refs[c:d], refs[e:])
        grid = n_steps if isinstance(n_steps, tuple) else (n_steps,)
        first = last = True
        for ax_, size in enumerate(grid):
            first = first & (pl.program_id(ax_) == 0)
            last = last & (pl.program_id(ax_) == size - 1)

        @pl.when(first)
        def _():
            ex.start(*ex_refs)

        body(*refs[:a], *refs[b:c], *refs[d:e])

        @pl.when(last)
        def _():
            ex.wait(*ex_refs)

    return wrapped


def _hosted_args(ex):
    if ex is None:
        return [], [], [], [], []
    any_spec = pl.BlockSpec(memory_space=pl.ANY)
    return list(ex.arrays), [any_spec] * ex.n_in, [any_spec] * ex.n_out, list(ex.out_shape), list(ex.scratch)


def _scan_fwd(rw, *, name, hosted=None):
    T = rw.shape[0]
    nc = _pick(T // CHUNK, SCAN_CHUNKS)
    rows = CHUNK * nc
    n = T // rows
    h_in, h_in_specs, h_out_specs, h_out_shape, h_scratch = _hosted_args(hosted)

    def body(r_ref, lw_ref, k_ref, v_ref, a_ref, b_ref, y_ref, ck_ref, inv_ref, s_ref):
        @pl.when(pl.program_id(0) == 0)
        def _():
            s_ref[...] = jnp.zeros_like(s_ref)

        S0 = s_ref[...]
        ck_ref[0] = S0
        y, S1, inv = _chunk(S0, *[_heads(ref[...]) for ref in (r_ref, lw_ref, k_ref, v_ref, a_ref, b_ref)])
        y_ref[...] = _unheads(y)
        inv_ref[0] = inv
        s_ref[...] = S1

    col = lambda j: pl.BlockSpec((rows, RW), lambda i: (i, j))
    return pl.pallas_call(
        _hosting(body, hosted, 6, 3, 1, n), name=name, grid=(n,),
        in_specs=[col(j) for j in range(6)] + h_in_specs,
        out_specs=[pl.BlockSpec((rows, RW), lambda i: (i, 0)),
                   pl.BlockSpec((1, NH, HD, HD), lambda i: (i, 0, 0, 0)),
                   pl.BlockSpec((1, NH * nc, CHUNK, CHUNK), lambda i: (i, 0, 0, 0))] + h_out_specs,
        out_shape=[jax.ShapeDtypeStruct((T, RW), F32), jax.ShapeDtypeStruct((n, NH, HD, HD), F32),
                   jax.ShapeDtypeStruct((n, NH * nc, CHUNK, CHUNK), F32)] + h_out_shape,
        scratch_shapes=[pltpu.VMEM((NH, HD, HD), F32)] + h_scratch,
        compiler_params=_cparams(("arbitrary",)),
    )(rw, rw, rw, rw, rw, rw, *h_in)


def _scan_bwd(rw, ck, inv, y, dya, post_consts, *, name, hosted=None):
    T = rw.shape[0]
    nc = _pick(T // CHUNK, SCAN_CHUNKS)
    rows = CHUNK * nc
    n = T // rows

    def body(r_ref, lw_ref, k_ref, v_ref, a_ref, b_ref, g_ref, ck_ref, inv_ref, y_ref, dya_ref, c0_ref, c1_ref, c2_ref,
             o_ref, d0_ref, d1_ref, d2_ref, ds_ref):
        @pl.when(pl.program_id(0) == 0)
        def _():
            ds_ref[...] = jnp.zeros_like(ds_ref)
            for d_ref in (d0_ref, d1_ref, d2_ref):
                d_ref[...] = jnp.zeros_like(d_ref)

        r, k, v = r_ref[...], k_ref[...], v_ref[...]
        _, post_vjp = jax.vjp(_post, y_ref[...], r, k, v, g_ref[...], c0_ref[...], c1_ref[...], c2_ref[...])
        dy, dr_p, dk_p, dv_p, dg, *dconsts = post_vjp(dya_ref[...])
        prim = [_heads(t) for t in (r, lw_ref[...], k, v, a_ref[...], b_ref[...])]
        known = inv_ref[0]
        _, vjp = jax.vjp(lambda S0, *p: _chunk(S0, *p, inverse=known)[:2], ck_ref[0], *prim)
        grads = vjp((_heads(dy), ds_ref[...]))
        ds_ref[...] = grads[0]
        dr, dlw, dk, dv, da, db = [_unheads(t) for t in grads[1:]]
        o_ref[...] = jnp.concatenate([dr + dr_p, dlw, dk + dk_p, dv + dv_p, da, db, dg], axis=1).astype(o_ref.dtype)
        for d_ref, val in zip((d0_ref, d1_ref, d2_ref), dconsts, strict=True):
            d_ref[...] += val

    h_in, h_in_specs, h_out_specs, h_out_shape, h_scratch = _hosted_args(hosted)
    col = lambda j: pl.BlockSpec((rows, RW), lambda i: (n - 1 - i, j))
    blk = lambda w_: pl.BlockSpec((rows, w_), lambda i: (n - 1 - i, 0))
    small = [_full_spec(c_) for c_ in post_consts]
    return pl.pallas_call(
        _hosting(body, hosted, 14, 4, 1, n), name=name, grid=(n,),
        in_specs=[col(j) for j in range(7)] + [pl.BlockSpec((1, NH, HD, HD), lambda i: (n - 1 - i, 0, 0, 0)),
                                               pl.BlockSpec((1, NH * nc, CHUNK, CHUNK), lambda i: (n - 1 - i, 0, 0, 0)),
                                               blk(RW), blk(RW)] + small + h_in_specs,
        out_specs=[blk(7 * RW)] + small + h_out_specs,
        out_shape=[jax.ShapeDtypeStruct((T, 7 * RW), MXU)]
        + [jax.ShapeDtypeStruct(c_.shape, F32) for c_ in post_consts] + h_out_shape,
        scratch_shapes=[pltpu.VMEM((NH, HD, HD), F32)] + h_scratch,
        compiler_params=_cparams(("arbitrary",)),
    )(rw, rw, rw, rw, rw, rw, rw, ck, inv, y, dya, *post_consts, *h_in)


ATTN_BLOCKS = (4, 2, 1)

def _attn_fwd(qkv, tab, qg, kg, sinks, *, name, hosted=None):
    T = qkv.shape[0]
    nb = _pick(T // BLK, ATTN_BLOCKS)
    n = T // (BLK * nb)
    h_in, h_in_specs, h_out_specs, h_out_shape, h_scratch = _hosted_args(hosted)

    def body(c_ref, p_ref, tc_ref, tp_ref, qg_ref, kg_ref, s_ref, o_ref):
        o_ref[...] = _attn_blocks(c_ref[...], p_ref[...], tc_ref[...], tp_ref[...], qg_ref[...], kg_ref[...],
                                  s_ref[...], pl.program_id(0) == 0).astype(o_ref.dtype)

    cur = lambda w: pl.BlockSpec((nb * BLK, w), lambda i: (i, 0))
    prev = lambda w: pl.BlockSpec((BLK, w), lambda i: (jnp.maximum(i * nb - 1, 0), 0))
    return pl.pallas_call(
        _hosting(body, hosted, 7, 1, 0, n), name=name, grid=(n,),
        in_specs=[cur(QKV_W), prev(QKV_W), cur(3 * 128), prev(3 * 128), _full_spec(qg), _full_spec(kg),
                  _full_spec(sinks)] + h_in_specs,
        out_specs=[cur(RW)] + h_out_specs, out_shape=[jax.ShapeDtypeStruct((T, RW), MXU)] + h_out_shape,
        scratch_shapes=h_scratch,
        compiler_params=_cparams(("arbitrary",)),
    )(qkv, qkv, tab, tab, qg, kg, sinks, *h_in)


def _attn_bwd(qkv, tab, qg, kg, sinks, dy, *, name, hosted=None):
    T = qkv.shape[0]
    nb = _pick(T // BLK, ATTN_BLOCKS)
    n = T // (BLK * nb)
    h_in, h_in_specs, h_out_specs, h_out_shape, h_scratch = _hosted_args(hosted)

    def body(c_ref, p_ref, tc_ref, tp_ref, qg_ref, kg_ref, s_ref, dy_ref, dqkv_ref, dqg_ref, dkg_ref, ds_ref, carry_ref):
        i = pl.program_id(0)

        @pl.when(i == 0)
        def _():
            carry_ref[...] = jnp.zeros_like(carry_ref)
            dqg_ref[...] = jnp.zeros_like(dqg_ref)
            dkg_ref[...] = jnp.zeros_like(dkg_ref)
            ds_ref[...] = jnp.zeros_like(ds_ref)

        tc, tp = tc_ref[...], tp_ref[...]
        f = lambda c, p_, qg_, kg_, sk: _attn_blocks(c, p_, tc, tp, qg_, kg_, sk, i == n - 1)
        _, vjp = jax.vjp(f, c_ref[...], p_ref[...], qg_ref[...], kg_ref[...], s_ref[...])
        dc, dp, dqg, dkg, dsk = vjp(dy_ref[...].astype(F32))
        last = slice((nb - 1) * BLK, nb * BLK)
        dqkv_ref[...] = dc.astype(dqkv_ref.dtype)
        dqkv_ref[last, :] = (dc[last] + carry_ref[...]).astype(dqkv_ref.dtype)
        carry_ref[...] = dp
        dqg_ref[...] += dqg
        dkg_ref[...] += dkg
        ds_ref[...] += dsk

    cur = lambda w: pl.BlockSpec((nb * BLK, w), lambda i: (n - 1 - i, 0))
    prev = lambda w: pl.BlockSpec((BLK, w), lambda i: (jnp.maximum((n - 1 - i) * nb - 1, 0), 0))
    return pl.pallas_call(
        _hosting(body, hosted, 8, 4, 1, n), name=name, grid=(n,),
        in_specs=[cur(QKV_W), prev(QKV_W), cur(3 * 128), prev(3 * 128), _full_spec(qg), _full_spec(kg), _full_spec(sinks),
                  cur(RW)] + h_in_specs,
        out_specs=[cur(QKV_W), _full_spec(qg), _full_spec(kg), _full_spec(sinks)] + h_out_specs,
        out_shape=[jax.ShapeDtypeStruct((T, QKV_W), MXU), jax.ShapeDtypeStruct(qg.shape, F32),
                   jax.ShapeDtypeStruct(kg.shape, F32), jax.ShapeDtypeStruct(sinks.shape, F32)] + h_out_shape,
        scratch_shapes=[pltpu.VMEM((BLK, QKV_W), F32)] + h_scratch,
        compiler_params=_cparams(("arbitrary",)),
    )(qkv, qkv, tab, tab, qg, kg, sinks, dy, *h_in)


def _shift_down(cur, prev8, i):
    rolled = pltpu.roll(cur, 1, 0)
    first_row = jnp.where(i > 0, prev8[7:8, :], 0.0)
    row = lax.broadcasted_iota(jnp.int32, cur.shape, 0)
    return jnp.where(row == 0, first_row, rolled)


def _shift_up(cur, next8, i, n):
    tm = cur.shape[0]
    rolled = pltpu.roll(cur, tm - 1, 0)
    last_row = jnp.where(i < n - 1, next8[0:1, :], 0.0)
    row = lax.broadcasted_iota(jnp.int32, cur.shape, 0)
    return jnp.where(row == tm - 1, last_row, rolled)


def _rope_table(positions):
    half = HD // 8
    inv_freq = 500000.0 ** (-jnp.arange(half, dtype=F32) / half)
    lane = jnp.arange(128) % HD
    rotary = lane < 2 * half
    freq = jnp.where(rotary, inv_freq[lane % half], 0.0)
    ang = positions.astype(F32)[:, None] * freq[None, :]
    cos, sin = jnp.cos(ang), jnp.sin(ang)
    return jnp.concatenate([jnp.where(rotary, cos, 1.0), jnp.where(lane < half, -sin, 0.0),
                            jnp.where(rotary & (lane >= half), sin, 0.0)], axis=1)


GATHER_BEHIND = {"f_proj_shift": [('ffn_w3', 512, 768)], "f_proj_gates": [('ffn_w3', 768, 1024)],
                 "f_prep": [('ffn_w2', 352, 704)],
                 "f_scan": [('ffn_w1', 0, 1024), ('w_branch_a', 0, 512), ('w_branch_b', 0, 512)],
                 "f_post": [('ffn_w3', 0, 512)], "f_attn": [('ffn_w2', 0, 352), ('w_out', 0, 256)]}
LATE = ['w_out', 'w_branch_a', 'w_branch_b', 'ffn_w1', 'ffn_w3', 'ffn_w2']
FFN = ('ffn_w1', 'ffn_w3', 'ffn_w2')
BACK_ATTN = ['w_out', 'w_branch_a', 'w_branch_b', 'ffn_w2']
BACK_SCAN = ['ffn_w1', 'ffn_w3']
BACK_LAST = ['w_in', 'decay_up', 'iclr_up', 'gate_up']


def _full_weight(g, ax):
    return g.reshape(-1, g.shape[2]) if ax == 0 else jnp.concatenate([g[j] for j in range(4)], axis=1)


def _local_step(x, target, ada, tab, w, s, shards=None):
    T = x.shape[0]
    tm = _pick(T, (512, 256, 128))
    tm_wide = _pick(T, (256, 128))
    tm_vjp = _pick(T, (256, 128))
    row = lambda n, dt=F32: (n, dt, 'row')
    acc = lambda n, r=1: (n, F32, r)

    def f_norm1(x_, g, ada_):
        return _norm_mod(x_, g, ada_[:, D:2 * D], ada_[:, 0:D])

    landed = {}

    def behind(kernel_name):
        if not shards:
            return None
        pieces = GATHER_BEHIND[kernel_name]
        return _GatherChips([shards[n] for n, _, _ in pieces], rows=[(lo, hi) for _, lo, hi in pieces])

    def took(kernel_name, got):
        landed.update(zip(GATHER_BEHIND[kernel_name], got))

    def mm_behind(a_, w_, kernel_name, **kw):
        ex = behind(kernel_name)
        res = _mm_nn(a_, w_, name=kernel_name, hosted=ex, **kw)
        if ex:
            took(kernel_name, res[1:])
            return res[0]
        return res

    h1, proj, *got = _mm_then([(None, w['w_in'][:, :SHIFT_W], 'nn')], lambda i, n, y, h, *_: (h, y), [x],
                              [s['norm1_gain'], ada], [row(D, MXU), row(SHIFT_W)], tm=tm, name="f_proj_shift",
                              lhs_fn=f_norm1, hosted=behind("f_proj_shift"))
    took("f_proj_shift", got)
    proj_qkv = _mm_nn(h1, w['w_in'][:, SHIFT_W:SHIFT_W + QKV_W], name="f_proj_qkv")
    proj_g = mm_behind(h1, w['w_in'][:, SHIFT_W + QKV_W:], "f_proj_gates", out_dtype=MXU)
    prep_consts = [s['decay_w0'], s['lora_up'], s['iclr_a0'], s['gate_up'], s['k_k'], s['k_a']]

    def f_prep(i, n, cur, prev8, mu, *params):
        mixed = cur + (_shift_down(cur, prev8, i) - cur) * mu
        return (_prep(mixed, *params),)
    rw, *got = _rowwise(f_prep, [(proj, SHIFT_W)], [s['tshift_mu']] + prep_consts, [row(7 * RW)], tm=tm_wide,
                        name="f_prep", halo=[(proj, SHIFT_W, 'prev')], hosted=behind("f_prep"))
    took("f_prep", got)
    y, ck, inv, *got = _scan_fwd(rw, name="f_scan", hosted=behind("f_scan"))
    took("f_scan", got)
    post_consts = [s['lnx_gain'], s['lnx_bias'], s['r_k']]

    rkvg = [(rw, RW, j) for j in (0, 2, 3, 6)]

    def f_post(i, n, *args):
        return (_post(*args),)
    ya, *got = _rowwise(f_post, [y] + rkvg, post_consts, [row(RW, MXU)], tm=tm_wide, name="f_post",
                        hosted=behind("f_post"))
    took("f_post", got)
    yb, *got = _attn_fwd(proj_qkv, tab, s['q_norm_gain'], s['k_norm_gain'], s['attn_sinks'], name="f_attn",
                         hosted=behind("f_attn"))
    took("f_attn", got)
    w = dict(w)
    if shards:
        ax = dict(SHARDED)
        for n in LATE:
            rows = [landed[key] for key in sorted(k_ for k_ in landed if k_[0] == n)]
            blocks = rows[0] if len(rows) == 1 else jnp.concatenate(rows, axis=1)
            w[n] = blocks if n in FFN else _full_weight(blocks, ax[n])
    else:
        fs = DFF // 4
        w.update({n: w[n].reshape(D, 4, fs).transpose(1, 0, 2) for n in ('ffn_w1', 'ffn_w3')})
        w['ffn_w2'] = w['ffn_w2'].reshape(4, fs, D)
    def f_merge(pg, ya_, yb_, x_, wa, wb, bias, g, ada_):
        ma_ = jnp.dot(ya_, wa, preferred_element_type=F32).astype(MXU)
        mb_ = jnp.dot(yb_, wb, preferred_element_type=F32).astype(MXU)
        return _merge(pg.astype(F32), ma_.astype(F32), mb_.astype(F32), bias), ma_, mb_

    def f_res1(i, n, mo_, merged_, ma_, mb_, pg, ya_, yb_, x_, wa, wb, bias, g, ada_):
        x1_ = x_ + ada_[:, 2 * D:3 * D] * mo_
        return merged_, ma_, mb_, mo_, x1_, _norm_mod(x1_, g, ada_[:, 4 * D:5 * D], ada_[:, 3 * D:4 * D])
    merged, ma, mb, mo, x1, h2 = _mm_then(
        [(None, w['w_out'], 'nn')], f_res1, [proj_g, ya, yb, x],
        [w['w_branch_a'], w['w_branch_b'], s['branch_gate_b'], s['norm2_gain'], ada],
        [row(D, MXU), row(D, MXU), row(D, MXU), row(D), row(D), row(D, MXU)], tm=tm, name="f_out", lhs_fn=f_merge)
    u, v, act = _ffn_in(h2, w['ffn_w1'], w['ffn_w3'], name="f_ffn_in")

    def f_loss(i, n, ff_, x1_, tgt, ada_):
        g2 = ada_[:, 5 * D:6 * D]
        err = x1_ + g2 * ff_ - tgt
        dx2 = err * (1.0 / D)
        loss = 0.5 * jnp.sum(jnp.sum(err * err, axis=1, keepdims=True) * (1.0 / D), axis=0, keepdims=True)
        return dx2, (dx2 * g2), jnp.broadcast_to(loss, (1, 128)), jnp.sum(dx2 * ff_, axis=0, keepdims=True)
    dx2, dff, loss, dgate2 = _mm_then([(act, w['ffn_w2'], 'nn', j) for j in range(4)], f_loss, [x1, target], [ada],
                                      [row(D), row(D, MXU), acc(128), acc(D)], tm=tm, name="f_ffn_out")

    du, dv = _ffn_act_bwd(dff, w['ffn_w2'], u, v, name="b_ffn_out_dx")
    g_w2 = _mm_tn_blocks(act, dff, name="b_ffn_out_dw", out_dtype=MXU)
    g_w1 = _mm_tn_blocks(h2, du, name="b_ffn_w1_dw", out_dtype=MXU)
    g_w3 = _mm_tn_blocks(h2, dv, name="b_ffn_w3_dw", out_dtype=MXU)

    def b_res1(i, n, dh2_, x1_, dx2_, mo_, g, ada_):
        _, vjp = jax.vjp(_norm_mod, x1_, g, ada_[:, 4 * D:5 * D], ada_[:, 3 * D:4 * D])
        dxn, dg, dsc, dsh = vjp(dh2_)
        dx1_ = dxn + dx2_
        g1 = ada_[:, 2 * D:3 * D]
        return dx1_, dx1_ * g1, dg, dsc, dsh, jnp.sum(dx1_ * mo_, axis=0, keepdims=True)
    dx1, dmo, d_gain2, d_scale2, d_shift2, dgate1 = _mm_then(
        [(t, w[n], 'nt', j) for t, n in ((du, 'ffn_w1'), (dv, 'ffn_w3')) for j in range(4)], b_res1, [x1, dx2, mo],
        [s['norm2_gain'], ada],
        [row(D), row(D, MXU), acc(D), acc(D), acc(D), acc(D)], tm=tm_wide, name="b_ffn_in_dx")
    g_wout = _mm_tn(merged, dmo, name="b_out_dw", out_dtype=MXU)

    def b_merge(i, n, dm, pg, ma_, mb_, wa, wb, bias):
        _, vjp = jax.vjp(_merge, pg.astype(F32), ma_.astype(F32), mb_.astype(F32), bias)
        dpg, dma_, dmb_, dbias = vjp(dm)
        dma_, dmb_ = dma_.astype(MXU), dmb_.astype(MXU)
        nt = lambda a_, b_: lax.dot_general(a_, b_, (((1,), (1,)), ((), ())), preferred_element_type=F32)
        return dpg, dma_, dmb_, nt(dma_, wa), nt(dmb_, wb), dbias
    dpg, dma, dmb, dya, dyb, d_bias = _mm_then(
        [(dmo, w['w_out'], 'nt')], b_merge, [proj_g, ma, mb], [w['w_branch_a'], w['w_branch_b'], s['branch_gate_b']],
        [row(GATE_W, MXU), row(D, MXU), row(D, MXU), row(RW), row(RW), acc(GATE_W)], tm=tm, name="b_out_dx")
    g_wa = _mm_tn(ya, dma, name="b_branch_a_dw", out_dtype=MXU, col_shards=4)
    g_wb = _mm_tn(yb, dmb, name="b_branch_b_dw", out_dtype=MXU, col_shards=4)
    gw = dict(w_branch_a=g_wa, w_branch_b=g_wb, w_out=g_wout.reshape(4, D // 4, D), ffn_w1=g_w1, ffn_w3=g_w3,
              ffn_w2=g_w2)
    recv = {}
    dqkv, d_qg, d_kg, d_sinks, *got = _attn_bwd(
        proj_qkv, tab, s['q_norm_gain'], s['k_norm_gain'], s['attn_sinks'], dyb, name="b_attn",
        hosted=shards and _ScatterChips([gw[n] for n in BACK_ATTN]))
    recv.update(zip(BACK_ATTN, got))

    d_rw, d_lnx_gain, d_lnx_bias, d_r_k, *got = _scan_bwd(
        rw, ck, inv, y, dya, post_consts, name="b_scan", hosted=shards and _ScatterChips([gw[n] for n in BACK_SCAN]))
    recv.update(zip(BACK_SCAN, got))

    def b_prep(i, n, cur, d_rw_, prev8, mu, *params):
        shifted = _shift_down(cur, prev8, i)
        mixed = cur + (shifted - cur) * mu
        _, vjp = jax.vjp(_prep, mixed, *params)
        grads = vjp(d_rw_.astype(F32))
        dmixed = grads[0]
        return (dmixed, jnp.sum(dmixed * (shifted - cur), axis=0, keepdims=True)) + tuple(grads[1:])
    dmixed, d_mu, d_w0, d_lora, d_a0, d_gate_up, d_kk, d_ka = _rowwise(
        b_prep, [(proj, SHIFT_W), d_rw], [s['tshift_mu']] + prep_consts,
        [row(SHIFT_W), acc(SHIFT_W), acc(RW), acc(2 * RW, 128), acc(RW), acc(RW, 128), acc(RW), acc(RW)],
        tm=tm_vjp, name="b_prep", halo=[(proj, SHIFT_W, 'prev')])

    def b_gather(i, n, dm, dqkv_, dpg_, next8, mu):
        dcur = dm * (1.0 - mu) + _shift_up(dm, next8, i, n) * mu
        return (jnp.concatenate([dcur.astype(MXU), dqkv_, dpg_], axis=1),)
    (dproj,) = _rowwise(b_gather, [dmixed, dqkv, dpg], [s['tshift_mu']], [row(IN_W, MXU)], tm=tm_wide, name="b_gather",
                        halo=[(dmixed, SHIFT_W, 'next')])
    g_win = _mm_tn(h1, dproj, name="b_proj_dw", out_dtype=MXU, col_shards=4)

    def col_blocks(g):
        k, n = g.shape
        return g.reshape(k, 4, n // 4).transpose(1, 0, 2).astype(MXU)
    gw.update(w_in=g_win, decay_up=col_blocks(d_lora[:64, :RW]), iclr_up=col_blocks(d_lora[64:, RW:]),
              gate_up=col_blocks(d_gate_up))
    top, bottom = None, None
    if shards:
        top = _ScatterChips([gw[n] for n in BACK_LAST], rows=[(0, D // 2)] + [None] * (len(BACK_LAST) - 1))
        bottom = _ScatterChips([gw['w_in']], rows=[(D // 2, 3 * D // 4)])
        dh1, *got_top = _mm_nt(dproj, w['w_in'], name="b_proj_dx", hosted=top)
    else:
        dh1 = _mm_nt(dproj, w['w_in'], name="b_proj_dx")

    def b_norm1(i, n, x_, dh1_, dx1_, g, ada_):
        _, vjp = jax.vjp(_norm_mod, x_, g, ada_[:, D:2 * D], ada_[:, 0:D])
        dxn, dg, dsc, dsh = vjp(dh1_)
        return dxn + dx1_, dg, dsc, dsh
    dx, d_gain1, d_scale1, d_shift1, *got_bottom = _rowwise(
        b_norm1, [x, dh1, dx1], [s['norm1_gain'], ada], [row(D), acc(D), acc(D), acc(D)], tm=tm, name="b_norm1",
        hosted=bottom)
    if shards:
        recv.update(zip(BACK_LAST[1:], got_top[1:]))
        recv['w_in'] = [got_top[0], got_bottom[0]]

    d_ada = jnp.concatenate([d_shift1, d_scale1, dgate1, d_shift2, d_scale2, dgate2], axis=1)
    gs = dict(norm1_gain=d_gain1, norm2_gain=d_gain2, tshift_mu=d_mu, decay_w0=d_w0, iclr_a0=d_a0, k_k=d_kk, k_a=d_ka,
              r_k=d_r_k, lnx_gain=d_lnx_gain, lnx_bias=d_lnx_bias, q_norm_gain=d_qg, k_norm_gain=d_kg,
              attn_sinks=d_sinks, branch_gate_b=d_bias)
    return loss, dx, d_ada, gw, gs, recv


ANY = pl.BlockSpec(memory_space=pl.ANY)


def _place():
    x, y, c = lax.axis_index("x"), lax.axis_index("y"), lax.axis_index("c")
    return x, y, c, [(1 - x, y), (x, 1 - y), (1 - x, 1 - y)]


def _all_gather8(x_shard, *, name):
    m_per, n = x_shard.shape

    def body(x_ref, out_ref, send_sems, recv_sems, local_sem):
        x, y, c, chips = _place()
        me, sibling = (x, y, c), (x, y, 1 - c)

        def rows(px, py, pc):
            return out_ref.at[pl.ds((4 * px + 2 * py + pc) * m_per, m_per), :]

        def copy(k, block, to, src=None):
            return pltpu.make_async_remote_copy(
                src_ref=rows(*block) if src is None else src, dst_ref=rows(*block),
                send_sem=send_sems.at[k], recv_sem=recv_sems.at[k], device_id=to, device_id_type=MESH)

        mine = pltpu.make_async_copy(x_ref, rows(*me), local_sem)
        mine.start()
        first = [copy(0, me, sibling, src=x_ref)]
        first += [copy(1 + j, me, (*chip, c), src=x_ref) for j, chip in enumerate(chips)]
        for cp in first:
            cp.start()
        passed = [copy(4 + j, (*chip, c), sibling) for j, chip in enumerate(chips)]
        for j, chip in enumerate(chips):
            copy(1 + j, (*chip, c), me).wait_recv()
            passed[j].start()
        copy(0, sibling, me).wait_recv()
        for j, chip in enumerate(chips):
            copy(4 + j, (*chip, 1 - c), me).wait_recv()
        for cp in first + passed:
            cp.wait_send()
        mine.wait()

    return pl.pallas_call(
        body, name=name, out_shape=jax.ShapeDtypeStruct((8 * m_per, n), x_shard.dtype),
        in_specs=[pl.BlockSpec(memory_space=pltpu.VMEM)], out_specs=pl.BlockSpec(memory_space=pltpu.VMEM),
        scratch_shapes=[pltpu.SemaphoreType.DMA((7,)), pltpu.SemaphoreType.DMA((7,)), pltpu.SemaphoreType.DMA],
    )(x_shard)


class _GatherChips:
    def __init__(self, shards, rows=None):
        n = len(shards)
        self.rows = [r or (0, s.shape[0]) for s, r in zip(shards, rows or [None] * n, strict=True)]
        self.arrays, self.n_in, self.n_out = list(shards), n, n
        self.out_shape = [jax.ShapeDtypeStruct((4, hi - lo) + s.shape[1:], s.dtype)
                          for s, (lo, hi) in zip(shards, self.rows, strict=True)]
        self.scratch = [pltpu.SemaphoreType.DMA((3 * n,)), pltpu.SemaphoreType.DMA((3 * n,)),
                        pltpu.SemaphoreType.DMA((n,))]

    def _copies(self, x_refs, out_refs, sems, receiving):
        send_sems, recv_sems, local_sems = sems
        x, y, c, chips = _place()
        s_me = 2 * x + y
        n = self.n_in
        src = [x_refs[a].at[pl.ds(lo, hi - lo)] for a, (lo, hi) in enumerate(self.rows)]

        def copy(a, k, s):
            return pltpu.make_async_remote_copy(
                src_ref=src[a], dst_ref=out_refs[a].at[s], send_sem=send_sems.at[3 * a + k],
                recv_sem=recv_sems.at[3 * a + k], device_id=(*chips[k], c), device_id_type=MESH)

        mine = [pltpu.make_async_copy(src[a], out_refs[a].at[s_me], local_sems.at[a]) for a in range(n)]
        sends = [copy(a, k, s_me) for a in range(n) for k in range(3)]
        if not receiving:
            return mine, sends
        return mine, sends, [copy(a, k, 2 * px + py) for a in range(n) for k, (px, py) in enumerate(chips)]

    def start(self, x_refs, out_refs, sems):
        mine, sends = self._copies(x_refs, out_refs, sems, False)
        for cp in mine + sends:
            cp.start()

    def wait(self, x_refs, out_refs, sems):
        mine, sends, recvs = self._copies(x_refs, out_refs, sems, True)
        for cp in recvs:
            cp.wait_recv()
        for cp in sends:
            cp.wait_send()
        for cp in mine:
            cp.wait()


class _GatherChipsHalved(_GatherChips):
    def __init__(self, shards):
        super().__init__(shards)
        n = self.n_in
        self.scratch = [pltpu.SemaphoreType.DMA((6 * n,)), pltpu.SemaphoreType.DMA((6 * n,)),
                        pltpu.SemaphoreType.DMA((n,))]

    def _copies(self, x_refs, out_refs, sems, receiving):
        send_sems, recv_sems, local_sems = sems
        x, y, c, chips = _place()
        s_me = 2 * x + y
        n = self.n_in

        def half(a, who):
            rows = x_refs[a].shape[0] // 2
            return pl.ds(who * rows, rows)

        def over_chips(a, k, s):
            return pltpu.make_async_remote_copy(
                src_ref=x_refs[a].at[half(a, c)], dst_ref=out_refs[a].at[s, half(a, c)],
                send_sem=send_sems.at[3 * a + k], recv_sem=recv_sems.at[3 * a + k],
                device_id=(*chips[k], c), device_id_type=MESH)

        def to_sibling(a, k, s, who):
            return pltpu.make_async_remote_copy(
                src_ref=out_refs[a].at[s, half(a, who)], dst_ref=out_refs[a].at[s, half(a, who)],
                send_sem=send_sems.at[3 * n + 3 * a + k], recv_sem=recv_sems.at[3 * n + 3 * a + k],
                device_id=(x, y, 1 - c), device_id_type=MESH)

        mine = [pltpu.make_async_copy(x_refs[a], out_refs[a].at[s_me], local_sems.at[a]) for a in range(n)]
        sends = [over_chips(a, k, s_me) for a in range(n) for k in range(3)]
        if not receiving:
            return mine, sends
        pairs = [(a, k, 2 * px + py) for a in range(n) for k, (px, py) in enumerate(chips)]
        landed = [over_chips(a, k, s) for a, k, s in pairs]
        passed_on = [to_sibling(a, k, s, c) for a, k, s in pairs]
        from_sibling = [to_sibling(a, k, s, 1 - c) for a, k, s in pairs]
        return mine, sends, landed, passed_on, from_sibling

    def wait(self, x_refs, out_refs, sems):
        mine, sends, landed, passed_on, from_sibling = self._copies(x_refs, out_refs, sems, True)
        for got, fwd in zip(landed, passed_on, strict=True):
            got.wait_recv()
            fwd.start()
        for cp in from_sibling:
            cp.wait_recv()
        for cp in sends + passed_on:
            cp.wait_send()
        for cp in mine:
            cp.wait()


class _ScatterChips:
    def __init__(self, parts, rows=None):
        n = len(parts)
        self.rows = [r or (0, p.shape[1]) for p, r in zip(parts, rows or [None] * n, strict=True)]
        self.arrays, self.n_in, self.n_out = list(parts), n, n
        self.out_shape = [jax.ShapeDtypeStruct((3, hi - lo) + p.shape[2:], p.dtype)
                          for p, (lo, hi) in zip(parts, self.rows, strict=True)]
        self.scratch = [pltpu.SemaphoreType.DMA((3 * n,)), pltpu.SemaphoreType.DMA((3 * n,))]

    def _copies(self, g_refs, out_refs, sems):
        send_sems, recv_sems = sems
        x, y, c, chips = _place()
        return [pltpu.make_async_remote_copy(
            src_ref=g_refs[a].at[2 * px + py, pl.ds(lo, hi - lo)], dst_ref=out_refs[a].at[k],
            send_sem=send_sems.at[3 * a + k], recv_sem=recv_sems.at[3 * a + k], device_id=(px, py, c),
            device_id_type=MESH)
            for a, (lo, hi) in enumerate(self.rows) for k, (px, py) in enumerate(chips)]

    def start(self, g_refs, out_refs, sems):
        for cp in self._copies(g_refs, out_refs, sems):
            cp.start()

    def wait(self, g_refs, out_refs, sems):
        sends = self._copies(g_refs, out_refs, sems)
        for cp in sends:
            cp.wait_recv()
        for cp in sends:
            cp.wait_send()


def _exchange_call(ex, *, name):
    def body(*refs):
        parts = (refs[:ex.n_in], refs[ex.n_in:ex.n_in + ex.n_out], refs[ex.n_in + ex.n_out:])
        ex.start(*parts)
        ex.wait(*parts)

    return pl.pallas_call(body, name=name, out_shape=ex.out_shape, in_specs=[ANY] * ex.n_in,
                          out_specs=[ANY] * ex.n_out, scratch_shapes=ex.scratch)(*ex.arrays)


class _SwapSibling:
    def __init__(self, vs):
        n = len(vs)
        self.arrays, self.n_in, self.n_out = list(vs), n, n
        self.out_shape = [jax.ShapeDtypeStruct(v.shape, v.dtype) for v in vs]
        self.scratch = [pltpu.SemaphoreType.DMA((n,)), pltpu.SemaphoreType.DMA((n,))]

    def _copies(self, v_refs, out_refs, sems):
        send_sems, recv_sems = sems
        x, y, c, _ = _place()
        return [pltpu.make_async_remote_copy(src_ref=v_refs[a], dst_ref=out_refs[a], send_sem=send_sems.at[a],
                                             recv_sem=recv_sems.at[a], device_id=(x, y, 1 - c), device_id_type=MESH)
                for a in range(self.n_in)]

    def start(self, v_refs, out_refs, sems):
        for cp in self._copies(v_refs, out_refs, sems):
            cp.start()

    def wait(self, v_refs, out_refs, sems):
        for cp in self._copies(v_refs, out_refs, sems):
            cp.wait()


class _Both:
    def __init__(self, first, second):
        self.parts = (first, second)
        self.arrays = first.arrays + second.arrays
        self.n_in, self.n_out = first.n_in + second.n_in, first.n_out + second.n_out
        self.out_shape = first.out_shape + second.out_shape
        self.scratch = first.scratch + second.scratch

    def _split(self, in_refs, out_refs, sems):
        a, b = self.parts
        return ((a, in_refs[:a.n_in], out_refs[:a.n_out], sems[:len(a.scratch)]),
                (b, in_refs[a.n_in:], out_refs[a.n_out:], sems[len(a.scratch):]))

    def start(self, in_refs, out_refs, sems):
        for ex, *refs in self._split(in_refs, out_refs, sems):
            ex.start(*refs)

    def wait(self, in_refs, out_refs, sems):
        for ex, *refs in self._split(in_refs, out_refs, sems):
            ex.wait(*refs)


def _sum_parts(own, others, *, name):
    R, C = own.shape
    tm = _pick(R, (256, 128, 64))

    def body(own_ref, o0_ref, o1_ref, o2_ref, out_ref):
        tot = own_ref[...].astype(F32)
        for ref in (o0_ref, o1_ref, o2_ref):
            tot = tot + ref[...].astype(F32)
        out_ref[...] = tot

    part = lambda k: pl.BlockSpec((None, tm, C), lambda i: (k, i, 0))
    return pl.pallas_call(
        body, name=name, grid=(R // tm,),
        in_specs=[pl.BlockSpec((tm, C), lambda i: (i, 0)), part(0), part(1), part(2)],
        out_specs=pl.BlockSpec((tm, C), lambda i: (i, 0)), out_shape=jax.ShapeDtypeStruct((R, C), F32),
        compiler_params=_cparams(("arbitrary",)),
    )(own, others, others, others)


def _adam_math(w_, m_, v_, g):
    m2 = ADAM_B1 * m_ + (1.0 - ADAM_B1) * g
    v2 = ADAM_B2 * v_ + (1.0 - ADAM_B2) * jnp.square(g)
    m_hat = m2 / (1.0 - ADAM_B1 ** ADAM_STEP)
    v_hat = v2 / (1.0 - ADAM_B2 ** ADAM_STEP)
    delta = -ADAM_LR * (m_hat / (jnp.sqrt(v_hat) + ADAM_EPS) + ADAM_WD * w_)
    return delta, m2, v2


SMALL_SLOTS = 24
SMALL_COLS = 2 * D


def _small_rows(widths):
    firsts, row = [], 0
    for n_i in widths:
        firsts.append(row)
        row += -(-n_i // SMALL_COLS)
    assert row <= SMALL_SLOTS
    return firsts


def _pack_small(grads, *, name):
    n = len(grads)
    firsts = _small_rows([g.shape[1] for g in grads])

    def body(*refs):
        out_ref = refs[n]
        out_ref[...] = jnp.zeros_like(out_ref)
        for first, ref in zip(firsts, refs[:n], strict=True):
            for lo in range(0, ref.shape[1], SMALL_COLS):
                width = min(SMALL_COLS, ref.shape[1] - lo)
                row = first + lo // SMALL_COLS
                out_ref[row:row + 1, 0:width] = ref[:, lo:lo + width]

    return pl.pallas_call(body, name=name, out_shape=jax.ShapeDtypeStruct((SMALL_SLOTS, SMALL_COLS), F32))(*grads)


def _adamw_small(ws, ms, vs, gathered, *, name):
    n = len(ws)
    firsts = _small_rows([w.shape[1] for w in ws] + [128])

    def total(g_ref, first, nc):
        pieces = []
        for lo in range(0, nc, SMALL_COLS):
            width, row = min(SMALL_COLS, nc - lo), first + lo // SMALL_COLS
            g = g_ref[row:row + 1, 0:width]
            for d in range(1, 8):
                g = g + g_ref[d * SMALL_SLOTS + row:d * SMALL_SLOTS + row + 1, 0:width]
            pieces.append(g)
        return pieces[0] if len(pieces) == 1 else jnp.concatenate(pieces, axis=1)

    def body(*refs):
        w_refs, m_refs, v_refs, g_ref = refs[:n], refs[n:2 * n], refs[2 * n:3 * n], refs[3 * n]
        outs = refs[3 * n + 1:]
        for i in range(n):
            g = total(g_ref, firsts[i], w_refs[i].shape[1])
            delta, m2, v2 = _adam_math(w_refs[i][...], m_refs[i][...], v_refs[i][...], g)
            for k, val in enumerate((g, delta, m2, v2)):
                outs[k * n + i][...] = val
        outs[4 * n][...] = total(g_ref, firsts[n], 128)

    shapes = [jax.ShapeDtypeStruct(w.shape, F32) for w in ws]
    res = pl.pallas_call(body, name=name, out_shape=shapes * 4 + [jax.ShapeDtypeStruct((1, 128), F32)],
                         compiler_params=pltpu.CompilerParams(vmem_limit_bytes=VMEM_LIMIT))(*ws, *ms, *vs, gathered)
    return [res[k * n:(k + 1) * n] for k in range(4)], res[4 * n]


def _adamw(w, m, v, gparts, *, tm, name, hosted=None):
    def fn(i, n, w_, m_, v_, *gs):
        g = gs[0]
        for p in gs[1:]:
            g = g + p
        return (g,) + _adam_math(w_, m_, v_, g)
    nc = w.shape[1]
    return _rowwise(fn, [w, m, v] + list(gparts), [], [(nc, F32, 'row')] * 4, tm=tm, name=name, hosted=hosted)


WEIGHTS = ['ada_w', 'ada_b', 'norm1_gain', 'norm2_gain', 'w_in', 'tshift_mu', 'decay_w0', 'decay_up', 'iclr_a0',
           'iclr_up', 'gate_up', 'k_k', 'k_a', 'r_k', 'lnx_gain', 'lnx_bias', 'q_norm_gain', 'k_norm_gain', 'attn_sinks',
           'branch_gate_b', 'w_branch_a', 'w_branch_b', 'w_out', 'ffn_w1', 'ffn_w3', 'ffn_w2']
SHARDED = [('w_in', 1), ('decay_up', 1), ('iclr_up', 1), ('gate_up', 1), ('w_branch_a', 1), ('w_branch_b', 1),
           ('w_out', 0), ('ffn_w1', 1), ('ffn_w3', 1), ('ffn_w2', 0)]
SMALL = ['ada_b', 'norm1_gain', 'norm2_gain', 'tshift_mu', 'decay_w0', 'iclr_a0', 'k_k', 'k_a', 'r_k', 'lnx_gain',
         'lnx_bias', 'q_norm_gain', 'k_norm_gain', 'attn_sinks', 'branch_gate_b']


def kernel(x, c, positions, ada_w, ada_b, norm1_gain, norm2_gain, w_in, tshift_mu, decay_w0, decay_up, iclr_a0, iclr_up, gate_up, k_k, k_a, r_k, lnx_gain, lnx_bias, q_norm_gain, k_norm_gain, attn_sinks, branch_gate_b, w_branch_a, w_branch_b, w_out, ffn_w1, ffn_w3, ffn_w2, loss_target, m_ada_w, m_ada_b, m_norm1_gain, m_norm2_gain, m_w_in, m_tshift_mu, m_decay_w0, m_decay_up, m_iclr_a0, m_iclr_up, m_gate_up, m_k_k, m_k_a, m_r_k, m_lnx_gain, m_lnx_bias, m_q_norm_gain, m_k_norm_gain, m_attn_sinks, m_branch_gate_b, m_w_branch_a, m_w_branch_b, m_w_out, m_ffn_w1, m_ffn_w3, m_ffn_w2, v_ada_w, v_ada_b, v_norm1_gain, v_norm2_gain, v_w_in, v_tshift_mu, v_decay_w0, v_decay_up, v_iclr_a0, v_iclr_up, v_gate_up, v_k_k, v_k_a, v_r_k, v_lnx_gain, v_lnx_bias, v_q_norm_gain, v_k_norm_gain, v_attn_sinks, v_branch_gate_b, v_w_branch_a, v_w_branch_b, v_w_out, v_ffn_w1, v_ffn_w3, v_ffn_w2):
    a = dict(locals())
    W = {n: a[n] for n in WEIGHTS}
    M = {n: a['m_' + n] for n in WEIGHTS}
    V = {n: a['v_' + n] for n in WEIGHTS}
    xi, yi, ci = lax.axis_index("x"), lax.axis_index("y"), lax.axis_index("c")
    me = 4 * xi + 2 * yi + ci
    shard = 2 * xi + yi
    mat = lambda t: t.reshape(t.shape[-2], t.shape[-1])
    sharded = [n for n, _ in SHARDED]

    ax = dict(SHARDED)
    late = LATE
    early = [n for n in sharded if n not in late]
    shards = {n: mat(W[n]).astype(MXU) for n in sharded}
    gathered = _exchange_call(_GatherChipsHalved([shards[n] for n in early]), name="gather_weights")
    full = {n: _full_weight(g, ax[n]) for n, g in zip(early, gathered, strict=True)}

    c_all = _all_gather8(jnp.broadcast_to(c, (8, D)), name="gather_c")[0::8]
    pad_rows = lambda t: jnp.concatenate([t, jnp.zeros((BLK - 8, t.shape[1]), t.dtype)])
    c_all = pad_rows(c_all.astype(MXU))
    ada_cols = _mm_nn(c_all, mat(ada_w).astype(MXU), name="f_ada")[:8]
    ada_all = _all_gather8(ada_cols, name="gather_ada").reshape(2, 2, 2, 8, 6 * D // 4)
    ada_mine = lax.dynamic_index_in_dim(ada_all[:, :, 0], me, axis=2, keepdims=False)
    ada = ada_mine.reshape(1, 6 * D) + mat(ada_b)

    zero = jnp.zeros((64, RW), MXU)
    lora = jnp.concatenate([jnp.concatenate([full['decay_up'], zero], axis=1),
                            jnp.concatenate([zero, full['iclr_up']], axis=1)], axis=0)
    s = {n: W[n].reshape(1, -1) for n in SMALL if n != 'ada_b'}
    s['lora_up'] = lora.astype(F32)
    s['gate_up'] = full['gate_up'].astype(F32)
    tab = _rope_table(positions.reshape(-1))
    loss, dx, d_ada, gw, gs, from_chips = _local_step(x[0], loss_target[0], ada, tab, dict(w_in=full['w_in']), s,
                                                      shards={n: shards[n] for n in late})

    gs['ada_b'] = d_ada
    gsmall = _pack_small([gs[n] for n in SMALL] + [loss], name="pack_small_grads")
    gsmall_all = _all_gather8(gsmall, name="gather_small_grads")
    row = lambda src: [src[n].reshape(1, -1) for n in SMALL]
    sm_out, loss = _adamw_small(row(W), row(M), row(V), gsmall_all, name="adamw_small")
    sm_out = [{n: o.reshape(W[n].shape) for n, o in zip(SMALL, outs_k, strict=True)} for outs_k in sm_out]
    loss = loss[0, 0]

    ada_rows = 6 * D // SMALL_COLS
    d_ada_all = gsmall_all.reshape(8, SMALL_SLOTS, SMALL_COLS)[:, :ada_rows].reshape(8, 6 * D)
    d_ada_cols = lax.dynamic_slice_in_dim(d_ada_all, shard * (6 * D // 4), 6 * D // 4, axis=1)
    g_ada_w = _mm_tn(c_all, pad_rows(d_ada_cols.astype(MXU)), name="b_ada")

    rest = [n for n in sharded if n != 'w_in']
    parts = {n: _sum_parts(lax.dynamic_index_in_dim(gw[n], shard, axis=0, keepdims=False), from_chips[n],
                           name="sum_" + n) for n in rest}
    tail = _Both(_ScatterChips([gw['w_in']], rows=[(3 * D // 4, D)]), _SwapSibling([parts[n] for n in rest]))
    res = _adamw(mat(ada_w), mat(m_ada_w), mat(v_ada_w), [g_ada_w], tm=256, name="adamw_ada", hosted=tail)
    ada_out, last_quarter, others = res[:4], res[4], dict(zip(rest, res[5:], strict=True))
    parts['w_in'] = _sum_parts(lax.dynamic_index_in_dim(gw['w_in'], shard, axis=0, keepdims=False),
                               jnp.concatenate(from_chips['w_in'] + [last_quarter], axis=1), name="sum_w_in")
    others['w_in'] = _exchange_call(_SwapSibling([parts['w_in']]), name="swap_w_in")[0]
    sh_out = {}
    for n in sharded:
        part, other = parts[n], others[n]
        sh_out[n] = _adamw(mat(W[n]), mat(M[n]), mat(V[n]), [part, other], tm=_pick(part.shape[0], (256, 128, 64)),
                           name="adamw_" + n)

    def leaf(k, n):
        if n == 'ada_w':
            return ada_out[k].reshape(W[n].shape)
        if n in sharded:
            return sh_out[n][k].reshape(W[n].shape)
        return sm_out[k][n]
    outs = [leaf(k, n) for k in range(4) for n in WEIGHTS]
    return (loss, dx[None], *outs)
```

```python
import functools
import math

import jax
import jax.numpy as jnp
from jax import lax
from jax.experimental import pallas as pl
from jax.experimental.pallas import tpu as pltpu

F32 = jnp.float32
BF16 = jnp.bfloat16
MXU = BF16
HI = lax.Precision.HIGHEST

D = 1024
HD = 64
NH = 8
RW = NH * HD
SHIFT_W = 3 * RW + 64 + 64 + 128
QKV_W = RW + 2 * 128
GATE_W = 2 * D
IN_W = SHIFT_W + QKV_W + GATE_W
DFF = 2816
BLK = 128
CHUNK = 128
RMS_EPS = 1e-6
GN_EPS = 64e-5
NEG_INF = -1e30
ADAM_LR, ADAM_B1, ADAM_B2, ADAM_EPS, ADAM_WD, ADAM_STEP = 0.001, 0.9, 0.999, 1e-08, 0.01, 10
VMEM_LIMIT = 56 * 1024 * 1024
MESH = pl.DeviceIdType.MESH


def _cparams(sem=None):
    return pltpu.CompilerParams(dimension_semantics=sem, vmem_limit_bytes=VMEM_LIMIT)


def _full_spec(a):
    nd = a.ndim
    return pl.BlockSpec(a.shape, lambda *_: (0,) * nd)


def _rowwise(fn, rows, consts, outs, *, tm, name, halo=(), hosted=None):
    rows = [(a + (0,))[:3] if isinstance(a, tuple) else (a, a.shape[1], 0) for a in rows]
    T = rows[0][0].shape[0]
    assert T % tm == 0 and tm % 8 == 0
    n_tiles = T // tm
    n_in = len(rows) + len(halo) + len(consts)
    in_specs = [pl.BlockSpec((tm, nc), lambda i, j=j: (i, j)) for _, nc, j in rows]
    args = [a for a, _, _ in rows]
    for a, nc, kind in halo:
        if kind == 'prev':
            in_specs.append(pl.BlockSpec((8, nc), lambda i: (jnp.maximum(i * (tm // 8) - 1, 0), 0)))
        else:
            in_specs.append(pl.BlockSpec((8, nc), lambda i: (jnp.minimum((i + 1) * (tm // 8), T // 8 - 1), 0)))
        args.append(a)
    in_specs += [_full_spec(a) for a in consts]
    args += list(consts)
    out_shape, out_specs = [], []
    for ncols, dtype, kind in outs:
        if kind == 'row':
            out_shape.append(jax.ShapeDtypeStruct((T, ncols), dtype))
            out_specs.append(pl.BlockSpec((tm, ncols), lambda i: (i, 0)))
        else:
            out_shape.append(jax.ShapeDtypeStruct((kind, ncols), dtype))
            out_specs.append(pl.BlockSpec((kind, ncols), lambda i: (0, 0)))

    def body(*refs):
        i = pl.program_id(0)
        vals = [r[...] for r in refs[:n_in]]
        res = fn(i, n_tiles, *vals)
        for (ncols, dtype, kind), o_ref, val in zip(outs, refs[n_in:], res, strict=True):
            if kind == 'row':
                o_ref[...] = val.astype(dtype)
            else:
                @pl.when(i == 0)
                def _():
                    o_ref[...] = jnp.zeros_like(o_ref)
                o_ref[...] += val.astype(dtype)

    h_in, h_in_specs, h_out_specs, h_out_shape, h_scratch = _hosted_args(hosted)
    res = pl.pallas_call(
        _hosting(body, hosted, n_in, len(outs), 0, n_tiles), name=name, grid=(n_tiles,),
        in_specs=in_specs + h_in_specs, out_specs=out_specs + h_out_specs, out_shape=out_shape + h_out_shape,
        scratch_shapes=h_scratch, compiler_params=_cparams(("arbitrary",)),
    )(*args, *h_in)
    return res


def _pick(n, cands):
    for c in cands:
        if n % c == 0:
            return c
    return n


MM_ROWS = (1024, 512, 256, 128)
MM_COLS = (1536, 1408, 1024, 896, 768, 512, 256, 128)
MM_WIDE = 3000


def _mm_nn(a, w, *, name, out_dtype=F32, hosted=None):
    T, K = a.shape
    N = w.shape[1]
    tm = _pick(T, MM_ROWS)
    tn = _pick(N, MM_COLS)
    grid = (N // tn, T // tm)
    h_in, h_in_specs, h_out_specs, h_out_shape, h_scratch = _hosted_args(hosted)

    def body(a_ref, w_ref, o_ref):
        o_ref[...] = jnp.dot(a_ref[...], w_ref[...], preferred_element_type=F32).astype(out_dtype)

    res = pl.pallas_call(
        _hosting(body, hosted, 2, 1, 0, grid), name=name, grid=grid,
        in_specs=[pl.BlockSpec((tm, K), lambda j, i: (i, 0)), pl.BlockSpec((K, tn), lambda j, i: (0, j))] + h_in_specs,
        out_specs=[pl.BlockSpec((tm, tn), lambda j, i: (i, j))] + h_out_specs,
        out_shape=[jax.ShapeDtypeStruct((T, N), out_dtype)] + h_out_shape, scratch_shapes=h_scratch,
        compiler_params=_cparams(("arbitrary", "arbitrary")),
    )(a, w, *h_in)
    return res if hosted else res[0]


def _mm_nt(dy, w, *, name, out_dtype=F32, hosted=None):
    T, N = dy.shape
    K = w.shape[0]
    tm = _pick(T, MM_ROWS if N <= MM_WIDE else MM_ROWS[1:])
    tk = _pick(K, MM_COLS[1:])
    grid = (K // tk, T // tm)
    h_in, h_in_specs, h_out_specs, h_out_shape, h_scratch = _hosted_args(hosted)

    def body(dy_ref, w_ref, o_ref):
        o_ref[...] = lax.dot_general(dy_ref[...], w_ref[...], (((1,), (1,)), ((), ())),
                                     preferred_element_type=F32).astype(out_dtype)

    res = pl.pallas_call(
        _hosting(body, hosted, 2, 1, 0, grid), name=name, grid=grid,
        in_specs=[pl.BlockSpec((tm, N), lambda j, i: (i, 0)), pl.BlockSpec((tk, N), lambda j, i: (j, 0))] + h_in_specs,
        out_specs=[pl.BlockSpec((tm, tk), lambda j, i: (i, j))] + h_out_specs,
        out_shape=[jax.ShapeDtypeStruct((T, K), out_dtype)] + h_out_shape, scratch_shapes=h_scratch,
        compiler_params=_cparams(("arbitrary", "arbitrary")),
    )(dy, w, *h_in)
    return res if hosted else res[0]


def _mm_tn(a, dy, *, name, out_dtype=F32, col_shards=None):
    T, K = a.shape
    N = dy.shape[1]
    tm = _pick(T, MM_ROWS)
    tn = N // col_shards if col_shards else _pick(N, MM_COLS[1:])
    n_t = T // tm

    def body(a_ref, dy_ref, o_ref, acc_ref):
        i = pl.program_id(1)

        @pl.when(i == 0)
        def _():
            acc_ref[...] = jnp.zeros_like(acc_ref)

        acc_ref[...] += lax.dot_general(a_ref[...], dy_ref[...], (((0,), (0,)), ((), ())), preferred_element_type=F32)

        @pl.when(i == n_t - 1)
        def _():
            o_ref[...] = acc_ref[...].astype(out_dtype)

    if col_shards:
        out_specs = pl.BlockSpec((None, K, tn), lambda j, i: (j, 0, 0))
        out_shape = jax.ShapeDtypeStruct((col_shards, K, tn), out_dtype)
    else:
        out_specs = pl.BlockSpec((K, tn), lambda j, i: (0, j))
        out_shape = jax.ShapeDtypeStruct((K, N), out_dtype)
    return pl.pallas_call(
        body, name=name, grid=(N // tn, n_t),
        in_specs=[pl.BlockSpec((tm, K), lambda j, i: (i, 0)), pl.BlockSpec((tm, tn), lambda j, i: (i, j))],
        out_specs=out_specs, out_shape=out_shape, scratch_shapes=[pltpu.VMEM((K, tn), F32)],
        compiler_params=_cparams(("arbitrary", "arbitrary")),
    )(a, dy)


def _mm_then(products, fn, rows, consts, outs, *, tm, name, lhs_fn=None, hosted=None):
    products = [(p + (None,))[:4] for p in products]
    T = (rows[0] if lhs_fn else products[0][0]).shape[-2]
    n_tiles = T // tm
    in_specs, args = [], []
    for a, w, _, j in products:
        if a is not None and j is None:
            in_specs.append(pl.BlockSpec((tm, a.shape[1]), lambda i: (i, 0)))
            args.append(a)
        elif a is not None:
            in_specs.append(pl.BlockSpec((None, tm, a.shape[2]), lambda i, j=j: (j, i, 0)))
            args.append(a)
        in_specs.append(_full_spec(w) if j is None else
                        pl.BlockSpec((None,) + w.shape[1:], lambda i, j=j: (j, 0, 0)))
        args.append(w)
    n_w = len(args)
    in_specs += [pl.BlockSpec((tm, a.shape[1]), lambda i: (i, 0)) for a in rows] + [_full_spec(c_) for c_ in consts]
    args += list(rows) + list(consts)
    n_in = len(args)
    out_shape, out_specs = [], []
    for ncols, dtype, kind in outs:
        if kind == 'row':
            out_shape.append(jax.ShapeDtypeStruct((T, ncols), dtype))
            out_specs.append(pl.BlockSpec((tm, ncols), lambda i: (i, 0)))
        else:
            out_shape.append(jax.ShapeDtypeStruct((kind, ncols), dtype))
            out_specs.append(pl.BlockSpec((kind, ncols), lambda i: (0, 0)))

    def body(*refs):
        i = pl.program_id(0)
        tiles = [r[...] for r in refs[n_w:n_in]]
        made = []
        if lhs_fn:
            made = lhs_fn(*tiles)
            made = list(made) if isinstance(made, tuple) else [made]
            made[0] = made[0].astype(MXU)
        y, pos = None, 0
        for a, _, form, _ in products:
            if a is None:
                lhs = made[0]
            else:
                lhs, pos = refs[pos][...], pos + 1
            dims = (((1,), (0,)), ((), ())) if form == 'nn' else (((1,), (1,)), ((), ()))
            t = lax.dot_general(lhs, refs[pos][...], dims, preferred_element_type=F32)
            pos += 1
            y = t if y is None else y + t
        res = fn(i, n_tiles, y, *made, *tiles)
        for (ncols, dtype, kind), o_ref, val in zip(outs, refs[n_in:], res, strict=True):
            if kind == 'row':
                o_ref[...] = val.astype(dtype)
            else:
                @pl.when(i == 0)
                def _():
                    o_ref[...] = jnp.zeros_like(o_ref)
                o_ref[...] += val.astype(dtype)

    h_in, h_in_specs, h_out_specs, h_out_shape, h_scratch = _hosted_args(hosted)
    return pl.pallas_call(
        _hosting(body, hosted, n_in, len(outs), 0, n_tiles), name=name, grid=(n_tiles,),
        in_specs=in_specs + h_in_specs, out_specs=out_specs + h_out_specs, out_shape=out_shape + h_out_shape,
        scratch_shapes=h_scratch, compiler_params=_cparams(("arbitrary",)))(*args, *h_in)


def _seg_ones(n):
    r = lax.broadcasted_iota(jnp.int32, (n, n), 0) // HD
    c = lax.broadcasted_iota(jnp.int32, (n, n), 1) // HD
    return (r == c).astype(F32)


def _segsum_raw(x):
    ones = _seg_ones(x.shape[1])
    if MXU == F32:
        return jnp.dot(x, ones, precision=HI, preferred_element_type=F32)
    hi = x.astype(MXU)
    lo = (x - hi.astype(F32)).astype(MXU)
    ones = ones.astype(MXU)
    return jnp.dot(hi, ones, preferred_element_type=F32) + jnp.dot(lo, ones, preferred_element_type=F32)


@jax.custom_vjp
def _segsum(x):
    return _segsum_raw(x)


def _segsum_fwd(x):
    return _segsum_raw(x), None


def _segsum_bwd(_, g):
    return (_segsum_raw(g),)


_segsum.defvjp(_segsum_fwd, _segsum_bwd)


def _mxu(x):
    return x.astype(MXU)


@jax.custom_vjp
def _bdot(a, b):
    return jnp.dot(_mxu(a), _mxu(b), preferred_element_type=F32)


def _bdot_fwd(a, b):
    return _bdot(a, b), (a, b)


def _bdot_bwd(res, g):
    a, b = res
    da = lax.dot_general(_mxu(g), _mxu(b), (((1,), (1,)), ((), ())), preferred_element_type=F32)
    db = lax.dot_general(_mxu(a), _mxu(g), (((0,), (0,)), ((), ())), preferred_element_type=F32)
    return da.astype(a.dtype), db.astype(b.dtype)


_bdot.defvjp(_bdot_fwd, _bdot_bwd)


def _sigmoid(x):
    return 1.0 / (1.0 + jnp.exp(-x))


def _softplus(x):
    return jnp.maximum(x, 0.0) + jnp.log(1.0 + jnp.exp(jnp.minimum(x, -x)))


def _norm_mod(x, gain, scale, shift):
    inv = lax.rsqrt(jnp.mean(x * x, axis=-1, keepdims=True) + RMS_EPS)
    return (x * inv) * gain * (1.0 + scale) + shift


def _prep(mixed, decay_w0, lora_up, iclr_a0, gate_up, k_k, k_a):
    r = mixed[:, 0:RW]
    k = mixed[:, RW:2 * RW]
    v = mixed[:, 2 * RW:3 * RW]
    z = mixed[:, 3 * RW:3 * RW + 128]
    xg = mixed[:, 3 * RW + 128:]
    lane = lax.broadcasted_iota(jnp.int32, z.shape, 1)
    tz = jnp.where(lane < 64, jnp.tanh(z), z)
    lo = _bdot(tz, lora_up)
    w_log = -_softplus(-(decay_w0 + lo[:, :RW])) - 0.5
    lw = -jnp.exp(w_log)
    a_ic = _sigmoid(iclr_a0 + lo[:, RW:])
    g = _bdot(_sigmoid(xg), gate_up)
    kk = k * k_k
    kk = kk / jnp.maximum(jnp.sqrt(_segsum(kk * kk)), 1e-12)
    k_mod = k * (1.0 + (a_ic - 1.0) * k_a)
    return jnp.concatenate([r, lw, k_mod, v, -kk, kk * a_ic, g], axis=1)


def _post(y, r, k, v, g, lnx_gain, lnx_bias, r_k):
    mu = _segsum(y) * (1.0 / HD)
    yc = y - mu
    var = _segsum(yc * yc) * (1.0 / HD)
    yn = yc * lax.rsqrt(var + GN_EPS) * lnx_gain + lnx_bias
    bonus = _segsum(r * k * r_k) * v
    return (yn + bonus) * g


def _merge(pg, ma, mb, bias):
    gates = _sigmoid(pg + bias)
    return gates[:, :D] * ma + gates[:, D:] * mb


def _swiglu(u, v):
    return u * _sigmoid(u) * v


def _ffn_in(h, w1, w3, *, name):
    T, K = h.shape
    ns, _, Fs = w1.shape
    tm = _pick(T, MM_ROWS)

    def body(h_ref, w1_ref, w3_ref, u_ref, v_ref, a_ref):
        u = jnp.dot(h_ref[...], w1_ref[...], preferred_element_type=F32).astype(MXU)
        v = jnp.dot(h_ref[...], w3_ref[...], preferred_element_type=F32).astype(MXU)
        u_ref[...] = u
        v_ref[...] = v
        a_ref[...] = _swiglu(u.astype(F32), v.astype(F32)).astype(MXU)

    wspec = pl.BlockSpec((None, K, Fs), lambda j, i: (j, 0, 0))
    ospec = pl.BlockSpec((None, tm, Fs), lambda j, i: (j, i, 0))
    return pl.pallas_call(
        body, name=name, grid=(ns, T // tm),
        in_specs=[pl.BlockSpec((tm, K), lambda j, i: (i, 0)), wspec, wspec],
        out_specs=[ospec] * 3, out_shape=[jax.ShapeDtypeStruct((ns, T, Fs), MXU)] * 3,
        compiler_params=_cparams(("arbitrary", "arbitrary")),
    )(h, w1, w3)


def _ffn_act_bwd(dff, w2, u, v, *, name):
    T, N = dff.shape
    ns, Fs, _ = w2.shape
    tm = _pick(T, MM_ROWS)

    def body(dy_ref, w_ref, u_ref, v_ref, du_ref, dv_ref):
        dact = lax.dot_general(dy_ref[...], w_ref[...], (((1,), (1,)), ((), ())), preferred_element_type=F32)
        _, vjp = jax.vjp(_swiglu, u_ref[...].astype(F32), v_ref[...].astype(F32))
        du, dv = vjp(dact)
        du_ref[...] = du.astype(MXU)
        dv_ref[...] = dv.astype(MXU)

    tile = pl.BlockSpec((None, tm, Fs), lambda j, i: (j, i, 0))
    return pl.pallas_call(
        body, name=name, grid=(ns, T // tm),
        in_specs=[pl.BlockSpec((tm, N), lambda j, i: (i, 0)), pl.BlockSpec((None, Fs, N), lambda j, i: (j, 0, 0)),
                  tile, tile],
        out_specs=[tile, tile], out_shape=[jax.ShapeDtypeStruct((ns, T, Fs), MXU)] * 2,
        compiler_params=_cparams(("arbitrary", "arbitrary")),
    )(dff, w2, u, v)


def _mm_tn_blocks(a, dy, *, name, out_dtype):
    a3, d3 = a.ndim == 3, dy.ndim == 3
    ns = a.shape[0] if a3 else dy.shape[0]
    T, K, N = a.shape[-2], a.shape[-1], dy.shape[-1]
    tm = _pick(T, MM_ROWS)
    n_t = T // tm

    def body(a_ref, dy_ref, o_ref, acc_ref):
        i = pl.program_id(1)

        @pl.when(i == 0)
        def _():
            acc_ref[...] = jnp.zeros_like(acc_ref)

        acc_ref[...] += lax.dot_general(a_ref[...], dy_ref[...], (((0,), (0,)), ((), ())), preferred_element_type=F32)

        @pl.when(i == n_t - 1)
        def _():
            o_ref[...] = acc_ref[...].astype(out_dtype)

    spec = lambda is3, n: (pl.BlockSpec((None, tm, n), lambda j, i: (j, i, 0)) if is3
                           else pl.BlockSpec((tm, n), lambda j, i: (i, 0)))
    return pl.pallas_call(
        body, name=name, grid=(ns, n_t), in_specs=[spec(a3, K), spec(d3, N)],
        out_specs=pl.BlockSpec((None, K, N), lambda j, i: (j, 0, 0)),
        out_shape=jax.ShapeDtypeStruct((ns, K, N), out_dtype), scratch_shapes=[pltpu.VMEM((K, N), F32)],
        compiler_params=_cparams(("arbitrary", "arbitrary")),
    )(a, dy)


@functools.partial(jax.custom_vjp, nondiff_argnums=(1,))
def _lane_roll(x, s):
    return pltpu.roll(x, s, 1)


def _lane_roll_fwd(x, s):
    return pltpu.roll(x, s, 1), None


def _lane_roll_bwd(s, _, g):
    n = g.shape[1]
    return (pltpu.roll(g, (n - s) % n, 1),)


_lane_roll.defvjp(_lane_roll_fwd, _lane_roll_bwd)


def _rope(x, cos, sin_lo, sin_hi):
    n = x.shape[1]
    return x * cos + _lane_roll(x, n - 8) * sin_lo + _lane_roll(x, 8) * sin_hi


def _head_rms(x, gain):
    return x * lax.rsqrt(_segsum(x * x) * (1.0 / HD) + RMS_EPS) * gain


def _attn_blocks(qkv_c, qkv_p, tab_c, tab_p, qg, kg, sinks, first):
    nb = qkv_c.shape[0] // BLK
    G = 4

    def tabs(tab, n):
        return [jnp.tile(tab[:, j * 128:(j + 1) * 128], (1, n // 128)) for j in range(3)]

    qg = jnp.concatenate([qg] * NH, axis=1)
    kg = jnp.concatenate([kg] * 2, axis=1)
    q = _rope(_head_rms(qkv_c[:, :RW], qg), *tabs(tab_c, RW))
    k_in = jnp.concatenate([qkv_p[:, RW:RW + 128], qkv_c[:, RW:RW + 128]], axis=0)
    k = _rope(_head_rms(k_in, kg), *tabs(jnp.concatenate([tab_p, tab_c], axis=0), 128))
    v = jnp.concatenate([qkv_p[:, RW + 128:], qkv_c[:, RW + 128:]], axis=0)

    pile = lambda xs: jnp.concatenate([x_[None] for x_ in xs], axis=0)

    def bands(t):
        return pile([t[b * BLK:(b + 2) * BLK, kvh * HD:(kvh + 1) * HD] for kvh in range(2) for b in range(nb)])

    qs = pile([jnp.concatenate([q[b * BLK:(b + 1) * BLK, (G * kvh + g) * HD:(G * kvh + g + 1) * HD]
                                for g in range(G)], axis=0) for kvh in range(2) for b in range(nb)])
    s = _bmm(qs, bands(k), 2, 2, 1) * (HD ** -0.5)
    qi = lax.broadcasted_iota(jnp.int32, (G * BLK, 2 * BLK), 0) % BLK
    kj = lax.broadcasted_iota(jnp.int32, (G * BLK, 2 * BLK), 1)
    dist = qi + BLK - kj
    in_band = (dist >= 0) & (dist < BLK)
    pair = lax.broadcasted_iota(jnp.int32, (2 * nb, 1, 1), 0)
    no_prev = (pair % nb == 0) & first
    valid = in_band[None] & (jnp.logical_not(no_prev) | (kj >= BLK)[None])
    s = jnp.where(valid, s, NEG_INF)
    row_g = lax.broadcasted_iota(jnp.int32, (G * BLK, 1), 0) // BLK
    sink = []
    for kvh in range(2):
        col = jnp.zeros((G * BLK, 1), F32)
        for g in range(G):
            col = jnp.where(row_g == g, sinks[:, G * kvh + g:G * kvh + g + 1], col)
        sink += [col] * nb
    sink = pile(sink)
    m = lax.stop_gradient(jnp.maximum(jnp.max(s, axis=-1, keepdims=True), sink))
    e = jnp.exp(s - m)
    p = e * (1.0 / (jnp.sum(e, axis=-1, keepdims=True) + jnp.exp(sink - m)))
    o = _bmm(p, bands(v), 2, 1, 1)
    return jnp.concatenate([jnp.concatenate([o[kvh * nb + b, g * BLK:(g + 1) * BLK] for kvh in range(2)
                                             for g in range(G)], axis=1) for b in range(nb)], axis=0)


def _heads(x):
    return jnp.stack([x[:, h * HD:(h + 1) * HD] for h in range(NH)], axis=0)


def _unheads(x):
    return jnp.concatenate([x[h] for h in range(NH)], axis=1)


def _split(x, n):
    parts, rest = [], x
    for _ in range(n):
        p = rest.astype(MXU)
        parts.append(p)
        rest = rest - p.astype(F32)
    return parts


def _bdot_batched(a, b, ca, cb):
    return lax.dot_general(a, b, (((ca,), (cb,)), ((0,), (0,))), preferred_element_type=F32)


def _bmm_passes(a, b, ca, cb, passes):
    if MXU == F32:
        return lax.dot_general(a, b, (((ca,), (cb,)), ((0,), (0,))), precision=HI, preferred_element_type=F32)
    if passes == 1:
        return _bdot_batched(a.astype(MXU), b.astype(MXU), ca, cb)
    (a0, a1), (b0, b1) = _split(a, 2), _split(b, 2)
    return _bdot_batched(a0, b0, ca, cb) + (_bdot_batched(a0, b1, ca, cb) + _bdot_batched(a1, b0, ca, cb))


@functools.partial(jax.custom_vjp, nondiff_argnums=(2, 3, 4))
def _bmm(a, b, ca, cb, passes=1):
    return _bmm_passes(a, b, ca, cb, passes)


def _bmm_fwd(a, b, ca, cb, passes):
    return _bmm_passes(a, b, ca, cb, passes), (a, b)


def _bmm_bwd(ca, cb, passes, res, g):
    a, b = res
    if (ca, cb) == (2, 1):
        return _bmm_passes(g, b, 2, 2, passes), _bmm_passes(a, g, 1, 1, passes)
    if (ca, cb) == (2, 2):
        return _bmm_passes(g, b, 2, 1, passes), _bmm_passes(g, a, 1, 1, passes)
    return _bmm_passes(b, g, 2, 2, passes), _bmm_passes(a, g, 2, 1, passes)


_bmm.defvjp(_bmm_fwd, _bmm_bwd)


def _tri_dot(x, transpose):
    C = x.shape[1]
    ri = lax.broadcasted_iota(jnp.int32, (C, C), 0)
    ci = lax.broadcasted_iota(jnp.int32, (C, C), 1)
    tri = jnp.broadcast_to(((ri <= ci) if transpose else (ri >= ci)).astype(MXU), (x.shape[0], C, C))
    if MXU == F32:
        return lax.dot_general(tri, x, (((2,), (1,)), ((0,), (0,))), precision=HI, preferred_element_type=F32)
    p0, p1, p2 = _split(x, 3)
    return _bdot_batched(tri, p0, 2, 1) + (_bdot_batched(tri, p1, 2, 1) + _bdot_batched(tri, p2, 2, 1))


@jax.custom_vjp
def _cumsum_rows(x):
    return _tri_dot(x, False)


def _cumsum_rows_fwd(x):
    return _tri_dot(x, False), None


def _cumsum_rows_bwd(_, g):
    return (_tri_dot(g, True),)


_cumsum_rows.defvjp(_cumsum_rows_fwd, _cumsum_rows_bwd)

P_SCORE = 1
P_SOLVE = 1
P_STATE = 1
SCAN_CHUNKS = (2, 1)


def _neumann(l):
    C = l.shape[1]
    eye = (lax.broadcasted_iota(jnp.int32, (C, C), 0) == lax.broadcasted_iota(jnp.int32, (C, C), 1)).astype(F32)
    x, lp = eye + l, l
    for _ in range(int(math.log2(C)) - 1):
        lp = _bmm(lp, lp, 2, 1, P_SOLVE)
        x = x + _bmm(x, lp, 2, 1, P_SOLVE)
    return x


@jax.custom_vjp
def _unit_lower_inverse(l):
    return _neumann(l)


def _unit_lower_inverse_fwd(l):
    x = _neumann(l)
    return x, x


def _unit_lower_inverse_bwd(x, g):
    return (_bmm(_bmm(x, g, 1, 1, P_SOLVE), x, 2, 2, P_SOLVE),)


_unit_lower_inverse.defvjp(_unit_lower_inverse_fwd, _unit_lower_inverse_bwd)


def _known_inverse(x):
    @jax.custom_vjp
    def f(l):
        return x

    f.defvjp(lambda l: (x, None), lambda _, g: (_bmm(_bmm(x, g, 1, 1, P_SOLVE), x, 2, 2, P_SOLVE),))
    return f


def _chunk(S0, r, lw, k, v, a, b, inverse=None):
    C = CHUNK
    n = r.shape[1] // C
    fold = lambda t: t.reshape(NH * n, C, HD)
    r, lw, k, v, a, b = (fold(t) for t in (r, lw, k, v, a, b))
    ri = lax.broadcasted_iota(jnp.int32, (C, C), 0)
    ci = lax.broadcasted_iota(jnp.int32, (C, C), 1)
    strict = (ri > ci)
    cum = _cumsum_rows(lw)
    p_in = jnp.exp(cum)
    p_ex = jnp.exp(cum - lw)
    p_inv = jnp.exp(-cum)
    at, rt, bt, kt = a * p_ex, r * p_in, b * p_inv, k * p_inv
    lhs = jnp.concatenate([at, rt], axis=1)
    rhs_ = jnp.concatenate([bt, kt], axis=1)
    sc = _bmm(lhs, rhs_, 2, 2, P_SCORE)
    a_ab = jnp.where(strict, sc[:, :C, :C], 0.0)
    a_ak = jnp.where(strict, sc[:, :C, C:], 0.0)
    incl2 = (lax.broadcasted_iota(jnp.int32, (C, 2 * C), 0) >= lax.broadcasted_iota(jnp.int32, (C, 2 * C), 1) % C)
    a_r = jnp.where(incl2, sc[:, C:, :], 0.0)
    av = _bmm(a_ak, v, 2, 1, P_SCORE)
    x = x_all = (_unit_lower_inverse if inverse is None else _known_inverse(inverse))(a_ab)
    p_last = jnp.exp(cum[:, C - 1:C, :])
    per_chunk = lambda t: t.reshape((NH, n) + t.shape[1:])
    lhs, rhs_, a_r, av, x, v, p_last = (per_chunk(t) for t in (lhs, rhs_, a_r, av, x, v, p_last))
    S, ys = S0, []
    for c in range(n):
        s0 = _bmm(lhs[:, c], S, 2, 2, P_STATE)
        u = _bmm(x[:, c], s0[:, :C] + av[:, c], 2, 1, P_SOLVE)
        uv = jnp.concatenate([u, v[:, c]], axis=1)
        ys.append(s0[:, C:] + _bmm(a_r[:, c], uv, 2, 1, P_SCORE))
        S = (S + _bmm(uv, rhs_[:, c], 1, 1, P_STATE)) * p_last[:, c]
    return jnp.concatenate(ys, axis=1), S, x_all


def _hosting(body, ex, n_in, n_out, n_scratch, n_steps):
    if ex is None:
        return body

    def wrapped(*refs):
        a = n_in
        b = a + ex.n_in
        c = b + n_out
        d = c + ex.n_out
        e = d + n_scratch
        ex_refs = (refs[a:b], refs[c:d], refs[e:])
        grid = n_steps if isinstance(n_steps, tuple) else (n_steps,)
        first = last = True
        for ax_, size in enumerate(grid):
            first = first & (pl.program_id(ax_) == 0)
            last = last & (pl.program_id(ax_) == size - 1)

        @pl.when(first)
        def _():
            ex.start(*ex_refs)

        body(*refs[:a], *refs[b:c], *refs[d:e])

        @pl.when(last)
        def _():
            ex.wait(*ex_refs)

    return wrapped


def _hosted_args(ex):
    if ex is None:
        return [], [], [], [], []
    any_spec = pl.BlockSpec(memory_space=pl.ANY)
    return list(ex.arrays), [any_spec] * ex.n_in, [any_spec] * ex.n_out, list(ex.out_shape), list(ex.scratch)


def _scan_fwd(rw, *, name, hosted=None):
    T = rw.shape[0]
    nc = _pick(T // CHUNK, SCAN_CHUNKS)
    rows = CHUNK * nc
    n = T // rows
    h_in, h_in_specs, h_out_specs, h_out_shape, h_scratch = _hosted_args(hosted)

    def body(r_ref, lw_ref, k_ref, v_ref, a_ref, b_ref, y_ref, ck_ref, inv_ref, s_ref):
        @pl.when(pl.program_id(0) == 0)
        def _():
            s_ref[...] = jnp.zeros_like(s_ref)

        S0 = s_ref[...]
        ck_ref[0] = S0
        y, S1, inv = _chunk(S0, *[_heads(ref[...]) for ref in (r_ref, lw_ref, k_ref, v_ref, a_ref, b_ref)])
        y_ref[...] = _unheads(y)
        inv_ref[0] = inv
        s_ref[...] = S1

    col = lambda j: pl.BlockSpec((rows, RW), lambda i: (i, j))
    return pl.pallas_call(
        _hosting(body, hosted, 6, 3, 1, n), name=name, grid=(n,),
        in_specs=[col(j) for j in range(6)] + h_in_specs,
        out_specs=[pl.BlockSpec((rows, RW), lambda i: (i, 0)),
                   pl.BlockSpec((1, NH, HD, HD), lambda i: (i, 0, 0, 0)),
                   pl.BlockSpec((1, NH * nc, CHUNK, CHUNK), lambda i: (i, 0, 0, 0))] + h_out_specs,
        out_shape=[jax.ShapeDtypeStruct((T, RW), F32), jax.ShapeDtypeStruct((n, NH, HD, HD), F32),
                   jax.ShapeDtypeStruct((n, NH * nc, CHUNK, CHUNK), F32)] + h_out_shape,
        scratch_shapes=[pltpu.VMEM((NH, HD, HD), F32)] + h_scratch,
        compiler_params=_cparams(("arbitrary",)),
    )(rw, rw, rw, rw, rw, rw, *h_in)


def _scan_bwd(rw, ck, inv, dy, *, name, hosted=None):
    T = rw.shape[0]
    nc = _pick(T // CHUNK, SCAN_CHUNKS)
    rows = CHUNK * nc
    n = T // rows

    def body(r_ref, lw_ref, k_ref, v_ref, a_ref, b_ref, ck_ref, inv_ref, dy_ref, o_ref, ds_ref):
        @pl.when(pl.program_id(0) == 0)
        def _():
            ds_ref[...] = jnp.zeros_like(ds_ref)

        prim = [_heads(ref[...]) for ref in (r_ref, lw_ref, k_ref, v_ref, a_ref, b_ref)]
        known = inv_ref[0]
        _, vjp = jax.vjp(lambda S0, *p: _chunk(S0, *p, inverse=known)[:2], ck_ref[0], *prim)
        grads = vjp((_heads(dy_ref[...]), ds_ref[...]))
        ds_ref[...] = grads[0]
        o_ref[...] = jnp.concatenate([_unheads(g) for g in grads[1:]], axis=1).astype(o_ref.dtype)

    h_in, h_in_specs, h_out_specs, h_out_shape, h_scratch = _hosted_args(hosted)
    col = lambda j: pl.BlockSpec((rows, RW), lambda i: (n - 1 - i, j))
    return pl.pallas_call(
        _hosting(body, hosted, 9, 1, 1, n), name=name, grid=(n,),
        in_specs=[col(j) for j in range(6)] + [pl.BlockSpec((1, NH, HD, HD), lambda i: (n - 1 - i, 0, 0, 0)),
                                               pl.BlockSpec((1, NH * nc, CHUNK, CHUNK), lambda i: (n - 1 - i, 0, 0, 0)),
                                               pl.BlockSpec((rows, RW), lambda i: (n - 1 - i, 0))] + h_in_specs,
        out_specs=[pl.BlockSpec((rows, 6 * RW), lambda i: (n - 1 - i, 0))] + h_out_specs,
        out_shape=[jax.ShapeDtypeStruct((T, 6 * RW), MXU)] + h_out_shape,
        scratch_shapes=[pltpu.VMEM((NH, HD, HD), F32)] + h_scratch,
        compiler_params=_cparams(("arbitrary",)),
    )(rw, rw, rw, rw, rw, rw, ck, inv, dy, *h_in)


ATTN_BLOCKS = (4, 2, 1)

def _attn_fwd(qkv, tab, qg, kg, sinks, *, name, hosted=None):
    T = qkv.shape[0]
    nb = _pick(T // BLK, ATTN_BLOCKS)
    n = T // (BLK * nb)
    h_in, h_in_specs, h_out_specs, h_out_shape, h_scratch = _hosted_args(hosted)

    def body(c_ref, p_ref, tc_ref, tp_ref, qg_ref, kg_ref, s_ref, o_ref):
        o_ref[...] = _attn_blocks(c_ref[...], p_ref[...], tc_ref[...], tp_ref[...], qg_ref[...], kg_ref[...],
                                  s_ref[...], pl.program_id(0) == 0).astype(o_ref.dtype)

    cur = lambda w: pl.BlockSpec((nb * BLK, w), lambda i: (i, 0))
    prev = lambda w: pl.BlockSpec((BLK, w), lambda i: (jnp.maximum(i * nb - 1, 0), 0))
    return pl.pallas_call(
        _hosting(body, hosted, 7, 1, 0, n), name=name, grid=(n,),
        in_specs=[cur(QKV_W), prev(QKV_W), cur(3 * 128), prev(3 * 128), _full_spec(qg), _full_spec(kg),
                  _full_spec(sinks)] + h_in_specs,
        out_specs=[cur(RW)] + h_out_specs, out_shape=[jax.ShapeDtypeStruct((T, RW), MXU)] + h_out_shape,
        scratch_shapes=h_scratch,
        compiler_params=_cparams(("arbitrary",)),
    )(qkv, qkv, tab, tab, qg, kg, sinks, *h_in)


def _attn_bwd(qkv, tab, qg, kg, sinks, dy, *, name, hosted=None):
    T = qkv.shape[0]
    nb = _pick(T // BLK, ATTN_BLOCKS)
    n = T // (BLK * nb)
    h_in, h_in_specs, h_out_specs, h_out_shape, h_scratch = _hosted_args(hosted)

    def body(c_ref, p_ref, tc_ref, tp_ref, qg_ref, kg_ref, s_ref, dy_ref, dqkv_ref, dqg_ref, dkg_ref, ds_ref, carry_ref):
        i = pl.program_id(0)

        @pl.when(i == 0)
        def _():
            carry_ref[...] = jnp.zeros_like(carry_ref)
            dqg_ref[...] = jnp.zeros_like(dqg_ref)
            dkg_ref[...] = jnp.zeros_like(dkg_ref)
            ds_ref[...] = jnp.zeros_like(ds_ref)

        tc, tp = tc_ref[...], tp_ref[...]
        f = lambda c, p_, qg_, kg_, sk: _attn_blocks(c, p_, tc, tp, qg_, kg_, sk, i == n - 1)
        _, vjp = jax.vjp(f, c_ref[...], p_ref[...], qg_ref[...], kg_ref[...], s_ref[...])
        dc, dp, dqg, dkg, dsk = vjp(dy_ref[...].astype(F32))
        last = slice((nb - 1) * BLK, nb * BLK)
        dqkv_ref[...] = dc.astype(dqkv_ref.dtype)
        dqkv_ref[last, :] = (dc[last] + carry_ref[...]).astype(dqkv_ref.dtype)
        carry_ref[...] = dp
        dqg_ref[...] += dqg
        dkg_ref[...] += dkg
        ds_ref[...] += dsk

    cur = lambda w: pl.BlockSpec((nb * BLK, w), lambda i: (n - 1 - i, 0))
    prev = lambda w: pl.BlockSpec((BLK, w), lambda i: (jnp.maximum((n - 1 - i) * nb - 1, 0), 0))
    return pl.pallas_call(
        _hosting(body, hosted, 8, 4, 1, n), name=name, grid=(n,),
        in_specs=[cur(QKV_W), prev(QKV_W), cur(3 * 128), prev(3 * 128), _full_spec(qg), _full_spec(kg), _full_spec(sinks),
                  cur(RW)] + h_in_specs,
        out_specs=[cur(QKV_W), _full_spec(qg), _full_spec(kg), _full_spec(sinks)] + h_out_specs,
        out_shape=[jax.ShapeDtypeStruct((T, QKV_W), MXU), jax.ShapeDtypeStruct(qg.shape, F32),
                   jax.ShapeDtypeStruct(kg.shape, F32), jax.ShapeDtypeStruct(sinks.shape, F32)] + h_out_shape,
        scratch_shapes=[pltpu.VMEM((BLK, QKV_W), F32)] + h_scratch,
        compiler_params=_cparams(("arbitrary",)),
    )(qkv, qkv, tab, tab, qg, kg, sinks, dy, *h_in)


def _shift_down(cur, prev8, i):
    rolled = pltpu.roll(cur, 1, 0)
    first_row = jnp.where(i > 0, prev8[7:8, :], 0.0)
    row = lax.broadcasted_iota(jnp.int32, cur.shape, 0)
    return jnp.where(row == 0, first_row, rolled)


def _shift_up(cur, next8, i, n):
    tm = cur.shape[0]
    rolled = pltpu.roll(cur, tm - 1, 0)
    last_row = jnp.where(i < n - 1, next8[0:1, :], 0.0)
    row = lax.broadcasted_iota(jnp.int32, cur.shape, 0)
    return jnp.where(row == tm - 1, last_row, rolled)


def _rope_table(positions):
    half = HD // 8
    inv_freq = 500000.0 ** (-jnp.arange(half, dtype=F32) / half)
    lane = jnp.arange(128) % HD
    rotary = lane < 2 * half
    freq = jnp.where(rotary, inv_freq[lane % half], 0.0)
    ang = positions.astype(F32)[:, None] * freq[None, :]
    cos, sin = jnp.cos(ang), jnp.sin(ang)
    return jnp.concatenate([jnp.where(rotary, cos, 1.0), jnp.where(lane < half, -sin, 0.0),
                            jnp.where(rotary & (lane >= half), sin, 0.0)], axis=1)


GATHER_BEHIND = {"f_proj_shift": [('ffn_w3', 512, 768)], "f_proj_gates": [('ffn_w3', 768, 1024)],
                 "f_prep": [('ffn_w2', 352, 704)],
                 "f_scan": [('ffn_w1', 0, 1024), ('w_branch_a', 0, 512), ('w_branch_b', 0, 512)],
                 "f_post": [('ffn_w3', 0, 512)], "f_attn": [('ffn_w2', 0, 352), ('w_out', 0, 256)]}
LATE = ['w_out', 'w_branch_a', 'w_branch_b', 'ffn_w1', 'ffn_w3', 'ffn_w2']
FFN = ('ffn_w1', 'ffn_w3', 'ffn_w2')
BACK_ATTN = ['w_out', 'w_branch_a', 'w_branch_b', 'ffn_w2']
BACK_SCAN = ['ffn_w1', 'ffn_w3']
BACK_LAST = ['w_in', 'decay_up', 'iclr_up', 'gate_up']


def _full_weight(g, ax):
    return g.reshape(-1, g.shape[2]) if ax == 0 else jnp.concatenate([g[j] for j in range(4)], axis=1)


def _local_step(x, target, ada, tab, w, s, shards=None):
    T = x.shape[0]
    tm = _pick(T, (512, 256, 128))
    tm_wide = _pick(T, (256, 128))
    tm_vjp = _pick(T, (256, 128))
    row = lambda n, dt=F32: (n, dt, 'row')
    acc = lambda n, r=1: (n, F32, r)

    def f_norm1(x_, g, ada_):
        return _norm_mod(x_, g, ada_[:, D:2 * D], ada_[:, 0:D])

    landed = {}

    def behind(kernel_name):
        if not shards:
            return None
        pieces = GATHER_BEHIND[kernel_name]
        return _GatherChips([shards[n] for n, _, _ in pieces], rows=[(lo, hi) for _, lo, hi in pieces])

    def took(kernel_name, got):
        landed.update(zip(GATHER_BEHIND[kernel_name], got))

    def mm_behind(a_, w_, kernel_name, **kw):
        ex = behind(kernel_name)
        res = _mm_nn(a_, w_, name=kernel_name, hosted=ex, **kw)
        if ex:
            took(kernel_name, res[1:])
            return res[0]
        return res

    h1, proj, *got = _mm_then([(None, w['w_in'][:, :SHIFT_W], 'nn')], lambda i, n, y, h, *_: (h, y), [x],
                              [s['norm1_gain'], ada], [row(D, MXU), row(SHIFT_W)], tm=tm, name="f_proj_shift",
                              lhs_fn=f_norm1, hosted=behind("f_proj_shift"))
    took("f_proj_shift", got)
    proj_qkv = _mm_nn(h1, w['w_in'][:, SHIFT_W:SHIFT_W + QKV_W], name="f_proj_qkv")
    proj_g = mm_behind(h1, w['w_in'][:, SHIFT_W + QKV_W:], "f_proj_gates", out_dtype=MXU)
    prep_consts = [s['decay_w0'], s['lora_up'], s['iclr_a0'], s['gate_up'], s['k_k'], s['k_a']]

    def f_prep(i, n, cur, prev8, mu, *params):
        mixed = cur + (_shift_down(cur, prev8, i) - cur) * mu
        return (_prep(mixed, *params),)
    rw, *got = _rowwise(f_prep, [(proj, SHIFT_W)], [s['tshift_mu']] + prep_consts, [row(7 * RW)], tm=tm_wide,
                        name="f_prep", halo=[(proj, SHIFT_W, 'prev')], hosted=behind("f_prep"))
    took("f_prep", got)
    y, ck, inv, *got = _scan_fwd(rw, name="f_scan", hosted=behind("f_scan"))
    took("f_scan", got)
    post_consts = [s['lnx_gain'], s['lnx_bias'], s['r_k']]

    rkvg = [(rw, RW, j) for j in (0, 2, 3, 6)]

    def f_post(i, n, *args):
        return (_post(*args),)
    ya, *got = _rowwise(f_post, [y] + rkvg, post_consts, [row(RW, MXU)], tm=tm_wide, name="f_post",
                        hosted=behind("f_post"))
    took("f_post", got)
    yb, *got = _attn_fwd(proj_qkv, tab, s['q_norm_gain'], s['k_norm_gain'], s['attn_sinks'], name="f_attn",
                         hosted=behind("f_attn"))
    took("f_attn", got)
    w = dict(w)
    if shards:
        ax = dict(SHARDED)
        for n in LATE:
            rows = [landed[key] for key in sorted(k_ for k_ in landed if k_[0] == n)]
            blocks = rows[0] if len(rows) == 1 else jnp.concatenate(rows, axis=1)
            w[n] = blocks if n in FFN else _full_weight(blocks, ax[n])
    else:
        fs = DFF // 4
        w.update({n: w[n].reshape(D, 4, fs).transpose(1, 0, 2) for n in ('ffn_w1', 'ffn_w3')})
        w['ffn_w2'] = w['ffn_w2'].reshape(4, fs, D)
    def f_merge(pg, ya_, yb_, x_, wa, wb, bias, g, ada_):
        ma_ = jnp.dot(ya_, wa, preferred_element_type=F32).astype(MXU)
        mb_ = jnp.dot(yb_, wb, preferred_element_type=F32).astype(MXU)
        return _merge(pg.astype(F32), ma_.astype(F32), mb_.astype(F32), bias), ma_, mb_

    def f_res1(i, n, mo_, merged_, ma_, mb_, pg, ya_, yb_, x_, wa, wb, bias, g, ada_):
        x1_ = x_ + ada_[:, 2 * D:3 * D] * mo_
        return merged_, ma_, mb_, mo_, x1_, _norm_mod(x1_, g, ada_[:, 4 * D:5 * D], ada_[:, 3 * D:4 * D])
    merged, ma, mb, mo, x1, h2 = _mm_then(
        [(None, w['w_out'], 'nn')], f_res1, [proj_g, ya, yb, x],
        [w['w_branch_a'], w['w_branch_b'], s['branch_gate_b'], s['norm2_gain'], ada],
        [row(D, MXU), row(D, MXU), row(D, MXU), row(D), row(D), row(D, MXU)], tm=tm, name="f_out", lhs_fn=f_merge)
    u, v, act = _ffn_in(h2, w['ffn_w1'], w['ffn_w3'], name="f_ffn_in")

    def f_loss(i, n, ff_, x1_, tgt, ada_):
        g2 = ada_[:, 5 * D:6 * D]
        err = x1_ + g2 * ff_ - tgt
        dx2 = err * (1.0 / D)
        loss = 0.5 * jnp.sum(jnp.sum(err * err, axis=1, keepdims=True) * (1.0 / D), axis=0, keepdims=True)
        return dx2, (dx2 * g2), jnp.broadcast_to(loss, (1, 128)), jnp.sum(dx2 * ff_, axis=0, keepdims=True)
    dx2, dff, loss, dgate2 = _mm_then([(act, w['ffn_w2'], 'nn', j) for j in range(4)], f_loss, [x1, target], [ada],
                                      [row(D), row(D, MXU), acc(128), acc(D)], tm=tm, name="f_ffn_out")

    du, dv = _ffn_act_bwd(dff, w['ffn_w2'], u, v, name="b_ffn_out_dx")
    g_w2 = _mm_tn_blocks(act, dff, name="b_ffn_out_dw", out_dtype=MXU)
    g_w1 = _mm_tn_blocks(h2, du, name="b_ffn_w1_dw", out_dtype=MXU)
    g_w3 = _mm_tn_blocks(h2, dv, name="b_ffn_w3_dw", out_dtype=MXU)

    def b_res1(i, n, dh2_, x1_, dx2_, mo_, g, ada_):
        _, vjp = jax.vjp(_norm_mod, x1_, g, ada_[:, 4 * D:5 * D], ada_[:, 3 * D:4 * D])
        dxn, dg, dsc, dsh = vjp(dh2_)
        dx1_ = dxn + dx2_
        g1 = ada_[:, 2 * D:3 * D]
        return dx1_, dx1_ * g1, dg, dsc, dsh, jnp.sum(dx1_ * mo_, axis=0, keepdims=True)
    dx1, dmo, d_gain2, d_scale2, d_shift2, dgate1 = _mm_then(
        [(t, w[n], 'nt', j) for t, n in ((du, 'ffn_w1'), (dv, 'ffn_w3')) for j in range(4)], b_res1, [x1, dx2, mo],
        [s['norm2_gain'], ada],
        [row(D), row(D, MXU), acc(D), acc(D), acc(D), acc(D)], tm=tm_wide, name="b_ffn_in_dx")
    g_wout = _mm_tn(merged, dmo, name="b_out_dw", out_dtype=MXU)

    def b_merge(i, n, dm, pg, ma_, mb_, wa, wb, bias):
        _, vjp = jax.vjp(_merge, pg.astype(F32), ma_.astype(F32), mb_.astype(F32), bias)
        dpg, dma_, dmb_, dbias = vjp(dm)
        dma_, dmb_ = dma_.astype(MXU), dmb_.astype(MXU)
        nt = lambda a_, b_: lax.dot_general(a_, b_, (((1,), (1,)), ((), ())), preferred_element_type=F32)
        return dpg, dma_, dmb_, nt(dma_, wa), nt(dmb_, wb), dbias
    dpg, dma, dmb, dya, dyb, d_bias = _mm_then(
        [(dmo, w['w_out'], 'nt')], b_merge, [proj_g, ma, mb], [w['w_branch_a'], w['w_branch_b'], s['branch_gate_b']],
        [row(GATE_W, MXU), row(D, MXU), row(D, MXU), row(RW), row(RW), acc(GATE_W)], tm=tm, name="b_out_dx")
    g_wa = _mm_tn(ya, dma, name="b_branch_a_dw", out_dtype=MXU, col_shards=4)
    g_wb = _mm_tn(yb, dmb, name="b_branch_b_dw", out_dtype=MXU, col_shards=4)
    gw = dict(w_branch_a=g_wa, w_branch_b=g_wb, w_out=g_wout.reshape(4, D // 4, D), ffn_w1=g_w1, ffn_w3=g_w3,
              ffn_w2=g_w2)
    recv = {}
    dqkv, d_qg, d_kg, d_sinks, *got = _attn_bwd(
        proj_qkv, tab, s['q_norm_gain'], s['k_norm_gain'], s['attn_sinks'], dyb, name="b_attn",
        hosted=shards and _ScatterChips([gw[n] for n in BACK_ATTN]))
    recv.update(zip(BACK_ATTN, got))

    def b_post(i, n, y_, r_, k_, v_, g_, dya_, *params):
        _, vjp = jax.vjp(_post, y_, r_, k_, v_, g_, *params)
        dy_, dr_, dk_, dv_, dg_, *dparams = vjp(dya_)
        return (dy_, jnp.concatenate([dr_, dk_, dv_, dg_], axis=1), *dparams)
    dy, drkvg, d_lnx_gain, d_lnx_bias, d_r_k = _rowwise(
        b_post, [y] + rkvg + [dya], post_consts, [row(RW), row(4 * RW, MXU), acc(RW), acc(RW), acc(RW)], tm=tm_wide,
        name="b_post")
    dscan, *got = _scan_bwd(rw, ck, inv, dy, name="b_scan",
                            hosted=shards and _ScatterChips([gw[n] for n in BACK_SCAN]))
    recv.update(zip(BACK_SCAN, got))

    def b_prep(i, n, cur, drw_, dscan_, prev8, mu, *params):
        shifted = _shift_down(cur, prev8, i)
        mixed = cur + (shifted - cur) * mu
        _, vjp = jax.vjp(_prep, mixed, *params)
        blk = lambda t, j: t[:, j * RW:(j + 1) * RW].astype(F32)
        ct = jnp.concatenate([blk(dscan_, 0) + blk(drw_, 0), blk(dscan_, 1), blk(dscan_, 2) + blk(drw_, 1),
                              blk(dscan_, 3) + blk(drw_, 2), blk(dscan_, 4), blk(dscan_, 5), blk(drw_, 3)], axis=1)
        grads = vjp(ct)
        dmixed = grads[0]
        return (dmixed, jnp.sum(dmixed * (shifted - cur), axis=0, keepdims=True)) + tuple(grads[1:])
    dmixed, d_mu, d_w0, d_lora, d_a0, d_gate_up, d_kk, d_ka = _rowwise(
        b_prep, [(proj, SHIFT_W), drkvg, dscan], [s['tshift_mu']] + prep_consts,
        [row(SHIFT_W), acc(SHIFT_W), acc(RW), acc(2 * RW, 128), acc(RW), acc(RW, 128), acc(RW), acc(RW)],
        tm=tm_vjp, name="b_prep", halo=[(proj, SHIFT_W, 'prev')])

    def b_gather(i, n, dm, dqkv_, dpg_, next8, mu):
        dcur = dm * (1.0 - mu) + _shift_up(dm, next8, i, n) * mu
        return (jnp.concatenate([dcur.astype(MXU), dqkv_, dpg_], axis=1),)
    (dproj,) = _rowwise(b_gather, [dmixed, dqkv, dpg], [s['tshift_mu']], [row(IN_W, MXU)], tm=tm_wide, name="b_gather",
                        halo=[(dmixed, SHIFT_W, 'next')])
    g_win = _mm_tn(h1, dproj, name="b_proj_dw", out_dtype=MXU, col_shards=4)

    def col_blocks(g):
        k, n = g.shape
        return g.reshape(k, 4, n // 4).transpose(1, 0, 2).astype(MXU)
    gw.update(w_in=g_win, decay_up=col_blocks(d_lora[:64, :RW]), iclr_up=col_blocks(d_lora[64:, RW:]),
              gate_up=col_blocks(d_gate_up))
    top, bottom = None, None
    if shards:
        top = _ScatterChips([gw[n] for n in BACK_LAST], rows=[(0, D // 2)] + [None] * (len(BACK_LAST) - 1))
        bottom = _ScatterChips([gw['w_in']], rows=[(D // 2, 3 * D // 4)])
        dh1, *got_top = _mm_nt(dproj, w['w_in'], name="b_proj_dx", hosted=top)
    else:
        dh1 = _mm_nt(dproj, w['w_in'], name="b_proj_dx")

    def b_norm1(i, n, x_, dh1_, dx1_, g, ada_):
        _, vjp = jax.vjp(_norm_mod, x_, g, ada_[:, D:2 * D], ada_[:, 0:D])
        dxn, dg, dsc, dsh = vjp(dh1_)
        return dxn + dx1_, dg, dsc, dsh
    dx, d_gain1, d_scale1, d_shift1, *got_bottom = _rowwise(
        b_norm1, [x, dh1, dx1], [s['norm1_gain'], ada], [row(D), acc(D), acc(D), acc(D)], tm=tm, name="b_norm1",
        hosted=bottom)
    if shards:
        recv.update(zip(BACK_LAST[1:], got_top[1:]))
        recv['w_in'] = [got_top[0], got_bottom[0]]

    d_ada = jnp.concatenate([d_shift1, d_scale1, dgate1, d_shift2, d_scale2, dgate2], axis=1)
    gs = dict(norm1_gain=d_gain1, norm2_gain=d_gain2, tshift_mu=d_mu, decay_w0=d_w0, iclr_a0=d_a0, k_k=d_kk, k_a=d_ka,
              r_k=d_r_k, lnx_gain=d_lnx_gain, lnx_bias=d_lnx_bias, q_norm_gain=d_qg, k_norm_gain=d_kg,
              attn_sinks=d_sinks, branch_gate_b=d_bias)
    return loss, dx, d_ada, gw, gs, recv


ANY = pl.BlockSpec(memory_space=pl.ANY)


def _place():
    x, y, c = lax.axis_index("x"), lax.axis_index("y"), lax.axis_index("c")
    return x, y, c, [(1 - x, y), (x, 1 - y), (1 - x, 1 - y)]


def _all_gather8(x_shard, *, name):
    m_per, n = x_shard.shape

    def body(x_ref, out_ref, send_sems, recv_sems, local_sem):
        x, y, c, chips = _place()
        me, sibling = (x, y, c), (x, y, 1 - c)

        def rows(px, py, pc):
            return out_ref.at[pl.ds((4 * px + 2 * py + pc) * m_per, m_per), :]

        def copy(k, block, to, src=None):
            return pltpu.make_async_remote_copy(
                src_ref=rows(*block) if src is None else src, dst_ref=rows(*block),
                send_sem=send_sems.at[k], recv_sem=recv_sems.at[k], device_id=to, device_id_type=MESH)

        mine = pltpu.make_async_copy(x_ref, rows(*me), local_sem)
        mine.start()
        first = [copy(0, me, sibling, src=x_ref)]
        first += [copy(1 + j, me, (*chip, c), src=x_ref) for j, chip in enumerate(chips)]
        for cp in first:
            cp.start()
        passed = [copy(4 + j, (*chip, c), sibling) for j, chip in enumerate(chips)]
        for j, chip in enumerate(chips):
            copy(1 + j, (*chip, c), me).wait_recv()
            passed[j].start()
        copy(0, sibling, me).wait_recv()
        for j, chip in enumerate(chips):
            copy(4 + j, (*chip, 1 - c), me).wait_recv()
        for cp in first + passed:
            cp.wait_send()
        mine.wait()

    return pl.pallas_call(
        body, name=name, out_shape=jax.ShapeDtypeStruct((8 * m_per, n), x_shard.dtype),
        in_specs=[pl.BlockSpec(memory_space=pltpu.VMEM)], out_specs=pl.BlockSpec(memory_space=pltpu.VMEM),
        scratch_shapes=[pltpu.SemaphoreType.DMA((7,)), pltpu.SemaphoreType.DMA((7,)), pltpu.SemaphoreType.DMA],
    )(x_shard)


class _GatherChips:
    def __init__(self, shards, rows=None):
        n = len(shards)
        self.rows = [r or (0, s.shape[0]) for s, r in zip(shards, rows or [None] * n, strict=True)]
        self.arrays, self.n_in, self.n_out = list(shards), n, n
        self.out_shape = [jax.ShapeDtypeStruct((4, hi - lo) + s.shape[1:], s.dtype)
                          for s, (lo, hi) in zip(shards, self.rows, strict=True)]
        self.scratch = [pltpu.SemaphoreType.DMA((3 * n,)), pltpu.SemaphoreType.DMA((3 * n,)),
                        pltpu.SemaphoreType.DMA((n,))]

    def _copies(self, x_refs, out_refs, sems, receiving):
        send_sems, recv_sems, local_sems = sems
        x, y, c, chips = _place()
        s_me = 2 * x + y
        n = self.n_in
        src = [x_refs[a].at[pl.ds(lo, hi - lo)] for a, (lo, hi) in enumerate(self.rows)]

        def copy(a, k, s):
            return pltpu.make_async_remote_copy(
                src_ref=src[a], dst_ref=out_refs[a].at[s], send_sem=send_sems.at[3 * a + k],
                recv_sem=recv_sems.at[3 * a + k], device_id=(*chips[k], c), device_id_type=MESH)

        mine = [pltpu.make_async_copy(src[a], out_refs[a].at[s_me], local_sems.at[a]) for a in range(n)]
        sends = [copy(a, k, s_me) for a in range(n) for k in range(3)]
        if not receiving:
            return mine, sends
        return mine, sends, [copy(a, k, 2 * px + py) for a in range(n) for k, (px, py) in enumerate(chips)]

    def start(self, x_refs, out_refs, sems):
        mine, sends = self._copies(x_refs, out_refs, sems, False)
        for cp in mine + sends:
            cp.start()

    def wait(self, x_refs, out_refs, sems):
        mine, sends, recvs = self._copies(x_refs, out_refs, sems, True)
        for cp in recvs:
            cp.wait_recv()
        for cp in sends:
            cp.wait_send()
        for cp in mine:
            cp.wait()


class _GatherChipsHalved(_GatherChips):
    def __init__(self, shards):
        super().__init__(shards)
        n = self.n_in
        self.scratch = [pltpu.SemaphoreType.DMA((6 * n,)), pltpu.SemaphoreType.DMA((6 * n,)),
                        pltpu.SemaphoreType.DMA((n,))]

    def _copies(self, x_refs, out_refs, sems, receiving):
        send_sems, recv_sems, local_sems = sems
        x, y, c, chips = _place()
        s_me = 2 * x + y
        n = self.n_in

        def half(a, who):
            rows = x_refs[a].shape[0] // 2
            return pl.ds(who * rows, rows)

        def over_chips(a, k, s):
            return pltpu.make_async_remote_copy(
                src_ref=x_refs[a].at[half(a, c)], dst_ref=out_refs[a].at[s, half(a, c)],
                send_sem=send_sems.at[3 * a + k], recv_sem=recv_sems.at[3 * a + k],
                device_id=(*chips[k], c), device_id_type=MESH)

        def to_sibling(a, k, s, who):
            return pltpu.make_async_remote_copy(
                src_ref=out_refs[a].at[s, half(a, who)], dst_ref=out_refs[a].at[s, half(a, who)],
                send_sem=send_sems.at[3 * n + 3 * a + k], recv_sem=recv_sems.at[3 * n + 3 * a + k],
                device_id=(x, y, 1 - c), device_id_type=MESH)

        mine = [pltpu.make_async_copy(x_refs[a], out_refs[a].at[s_me], local_sems.at[a]) for a in range(n)]
        sends = [over_chips(a, k, s_me) for a in range(n) for k in range(3)]
        if not receiving:
            return mine, sends
        pairs = [(a, k, 2 * px + py) for a in range(n) for k, (px, py) in enumerate(chips)]
        landed = [over_chips(a, k, s) for a, k, s in pairs]
        passed_on = [to_sibling(a, k, s, c) for a, k, s in pairs]
        from_sibling = [to_sibling(a, k, s, 1 - c) for a, k, s in pairs]
        return mine, sends, landed, passed_on, from_sibling

    def wait(self, x_refs, out_refs, sems):
        mine, sends, landed, passed_on, from_sibling = self._copies(x_refs, out_refs, sems, True)
        for got, fwd in zip(landed, passed_on, strict=True):
            got.wait_recv()
            fwd.start()
        for cp in from_sibling:
            cp.wait_recv()
        for cp in sends + passed_on:
            cp.wait_send()
        for cp in mine:
            cp.wait()


class _ScatterChips:
    def __init__(self, parts, rows=None):
        n = len(parts)
        self.rows = [r or (0, p.shape[1]) for p, r in zip(parts, rows or [None] * n, strict=True)]
        self.arrays, self.n_in, self.n_out = list(parts), n, n
        self.out_shape = [jax.ShapeDtypeStruct((3, hi - lo) + p.shape[2:], p.dtype)
                          for p, (lo, hi) in zip(parts, self.rows, strict=True)]
        self.scratch = [pltpu.SemaphoreType.DMA((3 * n,)), pltpu.SemaphoreType.DMA((3 * n,))]

    def _copies(self, g_refs, out_refs, sems):
        send_sems, recv_sems = sems
        x, y, c, chips = _place()
        return [pltpu.make_async_remote_copy(
            src_ref=g_refs[a].at[2 * px + py, pl.ds(lo, hi - lo)], dst_ref=out_refs[a].at[k],
            send_sem=send_sems.at[3 * a + k], recv_sem=recv_sems.at[3 * a + k], device_id=(px, py, c),
            device_id_type=MESH)
            for a, (lo, hi) in enumerate(self.rows) for k, (px, py) in enumerate(chips)]

    def start(self, g_refs, out_refs, sems):
        for cp in self._copies(g_refs, out_refs, sems):
            cp.start()

    def wait(self, g_refs, out_refs, sems):
        sends = self._copies(g_refs, out_refs, sems)
        for cp in sends:
            cp.wait_recv()
        for cp in sends:
            cp.wait_send()


def _exchange_call(ex, *, name):
    def body(*refs):
        parts = (refs[:ex.n_in], refs[ex.n_in:ex.n_in + ex.n_out], refs[ex.n_in + ex.n_out:])
        ex.start(*parts)
        ex.wait(*parts)

    return pl.pallas_call(body, name=name, out_shape=ex.out_shape, in_specs=[ANY] * ex.n_in,
                          out_specs=[ANY] * ex.n_out, scratch_shapes=ex.scratch)(*ex.arrays)


class _SwapSibling:
    def __init__(self, vs):
        n = len(vs)
        self.arrays, self.n_in, self.n_out = list(vs), n, n
        self.out_shape = [jax.ShapeDtypeStruct(v.shape, v.dtype) for v in vs]
        self.scratch = [pltpu.SemaphoreType.DMA((n,)), pltpu.SemaphoreType.DMA((n,))]

    def _copies(self, v_refs, out_refs, sems):
        send_sems, recv_sems = sems
        x, y, c, _ = _place()
        return [pltpu.make_async_remote_copy(src_ref=v_refs[a], dst_ref=out_refs[a], send_sem=send_sems.at[a],
                                             recv_sem=recv_sems.at[a], device_id=(x, y, 1 - c), device_id_type=MESH)
                for a in range(self.n_in)]

    def start(self, v_refs, out_refs, sems):
        for cp in self._copies(v_refs, out_refs, sems):
            cp.start()

    def wait(self, v_refs, out_refs, sems):
        for cp in self._copies(v_refs, out_refs, sems):
            cp.wait()


class _Both:
    def __init__(self, first, second):
        self.parts = (first, second)
        self.arrays = first.arrays + second.arrays
        self.n_in, self.n_out = first.n_in + second.n_in, first.n_out + second.n_out
        self.out_shape = first.out_shape + second.out_shape
        self.scratch = first.scratch + second.scratch

    def _split(self, in_refs, out_refs, sems):
        a, b = self.parts
        return ((a, in_refs[:a.n_in], out_refs[:a.n_out], sems[:len(a.scratch)]),
                (b, in_refs[a.n_in:], out_refs[a.n_out:], sems[len(a.scratch):]))

    def start(self, in_refs, out_refs, sems):
        for ex, *refs in self._split(in_refs, out_refs, sems):
            ex.start(*refs)

    def wait(self, in_refs, out_refs, sems):
        for ex, *refs in self._split(in_refs, out_refs, sems):
            ex.wait(*refs)


def _sum_parts(own, others, *, name):
    R, C = own.shape
    tm = _pick(R, (256, 128, 64))

    def body(own_ref, o0_ref, o1_ref, o2_ref, out_ref):
        tot = own_ref[...].astype(F32)
        for ref in (o0_ref, o1_ref, o2_ref):
            tot = tot + ref[...].astype(F32)
        out_ref[...] = tot

    part = lambda k: pl.BlockSpec((None, tm, C), lambda i: (k, i, 0))
    return pl.pallas_call(
        body, name=name, grid=(R // tm,),
        in_specs=[pl.BlockSpec((tm, C), lambda i: (i, 0)), part(0), part(1), part(2)],
        out_specs=pl.BlockSpec((tm, C), lambda i: (i, 0)), out_shape=jax.ShapeDtypeStruct((R, C), F32),
        compiler_params=_cparams(("arbitrary",)),
    )(own, others, others, others)


def _adam_math(w_, m_, v_, g):
    m2 = ADAM_B1 * m_ + (1.0 - ADAM_B1) * g
    v2 = ADAM_B2 * v_ + (1.0 - ADAM_B2) * jnp.square(g)
    m_hat = m2 / (1.0 - ADAM_B1 ** ADAM_STEP)
    v_hat = v2 / (1.0 - ADAM_B2 ** ADAM_STEP)
    delta = -ADAM_LR * (m_hat / (jnp.sqrt(v_hat) + ADAM_EPS) + ADAM_WD * w_)
    return delta, m2, v2


SMALL_SLOTS = 24
SMALL_COLS = 2 * D


def _small_rows(widths):
    firsts, row = [], 0
    for n_i in widths:
        firsts.append(row)
        row += -(-n_i // SMALL_COLS)
    assert row <= SMALL_SLOTS
    return firsts


def _pack_small(grads, *, name):
    n = len(grads)
    firsts = _small_rows([g.shape[1] for g in grads])

    def body(*refs):
        out_ref = refs[n]
        out_ref[...] = jnp.zeros_like(out_ref)
        for first, ref in zip(firsts, refs[:n], strict=True):
            for lo in range(0, ref.shape[1], SMALL_COLS):
                width = min(SMALL_COLS, ref.shape[1] - lo)
                row = first + lo // SMALL_COLS
                out_ref[row:row + 1, 0:width] = ref[:, lo:lo + width]

    return pl.pallas_call(body, name=name, out_shape=jax.ShapeDtypeStruct((SMALL_SLOTS, SMALL_COLS), F32))(*grads)


def _adamw_small(ws, ms, vs, gathered, *, name):
    n = len(ws)
    firsts = _small_rows([w.shape[1] for w in ws] + [128])

    def total(g_ref, first, nc):
        pieces = []
        for lo in range(0, nc, SMALL_COLS):
            width, row = min(SMALL_COLS, nc - lo), first + lo // SMALL_COLS
            g = g_ref[row:row + 1, 0:width]
            for d in range(1, 8):
                g = g + g_ref[d * SMALL_SLOTS + row:d * SMALL_SLOTS + row + 1, 0:width]
            pieces.append(g)
        return pieces[0] if len(pieces) == 1 else jnp.concatenate(pieces, axis=1)

    def body(*refs):
        w_refs, m_refs, v_refs, g_ref = refs[:n], refs[n:2 * n], refs[2 * n:3 * n], refs[3 * n]
        outs = refs[3 * n + 1:]
        for i in range(n):
            g = total(g_ref, firsts[i], w_refs[i].shape[1])
            delta, m2, v2 = _adam_math(w_refs[i][...], m_refs[i][...], v_refs[i][...], g)
            for k, val in enumerate((g, delta, m2, v2)):
                outs[k * n + i][...] = val
        outs[4 * n][...] = total(g_ref, firsts[n], 128)

    shapes = [jax.ShapeDtypeStruct(w.shape, F32) for w in ws]
    res = pl.pallas_call(body, name=name, out_shape=shapes * 4 + [jax.ShapeDtypeStruct((1, 128), F32)],
                         compiler_params=pltpu.CompilerParams(vmem_limit_bytes=VMEM_LIMIT))(*ws, *ms, *vs, gathered)
    return [res[k * n:(k + 1) * n] for k in range(4)], res[4 * n]


def _adamw(w, m, v, gparts, *, tm, name, hosted=None):
    def fn(i, n, w_, m_, v_, *gs):
        g = gs[0]
        for p in gs[1:]:
            g = g + p
        return (g,) + _adam_math(w_, m_, v_, g)
    nc = w.shape[1]
    return _rowwise(fn, [w, m, v] + list(gparts), [], [(nc, F32, 'row')] * 4, tm=tm, name=name, hosted=hosted)


WEIGHTS = ['ada_w', 'ada_b', 'norm1_gain', 'norm2_gain', 'w_in', 'tshift_mu', 'decay_w0', 'decay_up', 'iclr_a0',
           'iclr_up', 'gate_up', 'k_k', 'k_a', 'r_k', 'lnx_gain', 'lnx_bias', 'q_norm_gain', 'k_norm_gain', 'attn_sinks',
           'branch_gate_b', 'w_branch_a', 'w_branch_b', 'w_out', 'ffn_w1', 'ffn_w3', 'ffn_w2']
SHARDED = [('w_in', 1), ('decay_up', 1), ('iclr_up', 1), ('gate_up', 1), ('w_branch_a', 1), ('w_branch_b', 1),
           ('w_out', 0), ('ffn_w1', 1), ('ffn_w3', 1), ('ffn_w2', 0)]
SMALL = ['ada_b', 'norm1_gain', 'norm2_gain', 'tshift_mu', 'decay_w0', 'iclr_a0', 'k_k', 'k_a', 'r_k', 'lnx_gain',
         'lnx_bias', 'q_norm_gain', 'k_norm_gain', 'attn_sinks', 'branch_gate_b']


def kernel(x, c, positions, ada_w, ada_b, norm1_gain, norm2_gain, w_in, tshift_mu, decay_w0, decay_up, iclr_a0, iclr_up, gate_up, k_k, k_a, r_k, lnx_gain, lnx_bias, q_norm_gain, k_norm_gain, attn_sinks, branch_gate_b, w_branch_a, w_branch_b, w_out, ffn_w1, ffn_w3, ffn_w2, loss_target, m_ada_w, m_ada_b, m_norm1_gain, m_norm2_gain, m_w_in, m_tshift_mu, m_decay_w0, m_decay_up, m_iclr_a0, m_iclr_up, m_gate_up, m_k_k, m_k_a, m_r_k, m_lnx_gain, m_lnx_bias, m_q_norm_gain, m_k_norm_gain, m_attn_sinks, m_branch_gate_b, m_w_branch_a, m_w_branch_b, m_w_out, m_ffn_w1, m_ffn_w3, m_ffn_w2, v_ada_w, v_ada_b, v_norm1_gain, v_norm2_gain, v_w_in, v_tshift_mu, v_decay_w0, v_decay_up, v_iclr_a0, v_iclr_up, v_gate_up, v_k_k, v_k_a, v_r_k, v_lnx_gain, v_lnx_bias, v_q_norm_gain, v_k_norm_gain, v_attn_sinks, v_branch_gate_b, v_w_branch_a, v_w_branch_b, v_w_out, v_ffn_w1, v_ffn_w3, v_ffn_w2):
    a = dict(locals())
    W = {n: a[n] for n in WEIGHTS}
    M = {n: a['m_' + n] for n in WEIGHTS}
    V = {n: a['v_' + n] for n in WEIGHTS}
    xi, yi, ci = lax.axis_index("x"), lax.axis_index("y"), lax.axis_index("c")
    me = 4 * xi + 2 * yi + ci
    shard = 2 * xi + yi
    mat = lambda t: t.reshape(t.shape[-2], t.shape[-1])
    sharded = [n for n, _ in SHARDED]

    ax = dict(SHARDED)
    late = LATE
    early = [n for n in sharded if n not in late]
    shards = {n: mat(W[n]).astype(MXU) for n in sharded}
    gathered = _exchange_call(_GatherChipsHalved([shards[n] for n in early]), name="gather_weights")
    full = {n: _full_weight(g, ax[n]) for n, g in zip(early, gathered, strict=True)}

    c_all = _all_gather8(jnp.broadcast_to(c, (8, D)), name="gather_c")[0::8]
    pad_rows = lambda t: jnp.concatenate([t, jnp.zeros((BLK - 8, t.shape[1]), t.dtype)])
    c_all = pad_rows(c_all.astype(MXU))
    ada_cols = _mm_nn(c_all, mat(ada_w).astype(MXU), name="f_ada")[:8]
    ada_all = _all_gather8(ada_cols, name="gather_ada").reshape(2, 2, 2, 8, 6 * D // 4)
    ada_mine = lax.dynamic_index_in_dim(ada_all[:, :, 0], me, axis=2, keepdims=False)
    ada = ada_mine.reshape(1, 6 * D) + mat(ada_b)

    zero = jnp.zeros((64, RW), MXU)
    lora = jnp.concatenate([jnp.concatenate([full['decay_up'], zero], axis=1),
                            jnp.concatenate([zero, full['iclr_up']], axis=1)], axis=0)
    s = {n: W[n].reshape(1, -1) for n in SMALL if n != 'ada_b'}
    s['lora_up'] = lora.astype(F32)
    s['gate_up'] = full['gate_up'].astype(F32)
    tab = _rope_table(positions.reshape(-1))
    loss, dx, d_ada, gw, gs, from_chips = _local_step(x[0], loss_target[0], ada, tab, dict(w_in=full['w_in']), s,
                                                      shards={n: shards[n] for n in late})

    gs['ada_b'] = d_ada
    gsmall = _pack_small([gs[n] for n in SMALL] + [loss], name="pack_small_grads")
    gsmall_all = _all_gather8(gsmall, name="gather_small_grads")
    row = lambda src: [src[n].reshape(1, -1) for n in SMALL]
    sm_out, loss = _adamw_small(row(W), row(M), row(V), gsmall_all, name="adamw_small")
    sm_out = [{n: o.reshape(W[n].shape) for n, o in zip(SMALL, outs_k, strict=True)} for outs_k in sm_out]
    loss = loss[0, 0]

    ada_rows = 6 * D // SMALL_COLS
    d_ada_all = gsmall_all.reshape(8, SMALL_SLOTS, SMALL_COLS)[:, :ada_rows].reshape(8, 6 * D)
    d_ada_cols = lax.dynamic_slice_in_dim(d_ada_all, shard * (6 * D // 4), 6 * D // 4, axis=1)
    g_ada_w = _mm_tn(c_all, pad_rows(d_ada_cols.astype(MXU)), name="b_ada")

    rest = [n for n in sharded if n != 'w_in']
    parts = {n: _sum_parts(lax.dynamic_index_in_dim(gw[n], shard, axis=0, keepdims=False), from_chips[n],
                           name="sum_" + n) for n in rest}
    tail = _Both(_ScatterChips([gw['w_in']], rows=[(3 * D // 4, D)]), _SwapSibling([parts[n] for n in rest]))
    res = _adamw(mat(ada_w), mat(m_ada_w), mat(v_ada_w), [g_ada_w], tm=256, name="adamw_ada", hosted=tail)
    ada_out, last_quarter, others = res[:4], res[4], dict(zip(rest, res[5:], strict=True))
    parts['w_in'] = _sum_parts(lax.dynamic_index_in_dim(gw['w_in'], shard, axis=0, keepdims=False),
                               jnp.concatenate(from_chips['w_in'] + [last_quarter], axis=1), name="sum_w_in")
    others['w_in'] = _exchange_call(_SwapSibling([parts['w_in']]), name="swap_w_in")[0]
    sh_out = {}
    for n in sharded:
        part, other = parts[n], others[n]
        sh_out[n] = _adamw(mat(W[n]), mat(M[n]), mat(V[n]), [part, other], tm=_pick(part.shape[0], (256, 128, 64)),
                           name="adamw_" + n)

    def leaf(k, n):
        if n == 'ada_w':
            return ada_out[k].reshape(W[n].shape)
        if n in sharded:
            return sh_out[n][k].reshape(W[n].shape)
        return sm_out[k][n]
    outs = [leaf(k, n) for k in range(4) for n in WEIGHTS]
    return (loss, dx[None], *outs)
```

```python
import functools
import math

import jax
import jax.numpy as jnp
from jax import lax
from jax.experimental import pallas as pl
from jax.experimental.pallas import tpu as pltpu

F32 = jnp.float32
BF16 = jnp.bfloat16
MXU = BF16
HI = lax.Precision.HIGHEST

D = 1024
HD = 64
NH = 8
RW = NH * HD
SHIFT_W = 3 * RW + 64 + 64 + 128
QKV_W = RW + 2 * 128
GATE_W = 2 * D
IN_W = SHIFT_W + QKV_W + GATE_W
DFF = 2816
BLK = 128
CHUNK = 128
RMS_EPS = 1e-6
GN_EPS = 64e-5
NEG_INF = -1e30
ADAM_LR, ADAM_B1, ADAM_B2, ADAM_EPS, ADAM_WD, ADAM_STEP = 0.001, 0.9, 0.999, 1e-08, 0.01, 10
VMEM_LIMIT = 56 * 1024 * 1024
MESH = pl.DeviceIdType.MESH


def _cparams(sem=None):
    return pltpu.CompilerParams(dimension_semantics=sem, vmem_limit_bytes=VMEM_LIMIT)


def _full_spec(a):
    nd = a.ndim
    return pl.BlockSpec(a.shape, lambda *_: (0,) * nd)


def _rowwise(fn, rows, consts, outs, *, tm, name, halo=(), hosted=None):
    rows = [(a + (0,))[:3] if isinstance(a, tuple) else (a, a.shape[1], 0) for a in rows]
    T = rows[0][0].shape[0]
    assert T % tm == 0 and tm % 8 == 0
    n_tiles = T // tm
    n_in = len(rows) + len(halo) + len(consts)
    in_specs = [pl.BlockSpec((tm, nc), lambda i, j=j: (i, j)) for _, nc, j in rows]
    args = [a for a, _, _ in rows]
    for a, nc, kind in halo:
        if kind == 'prev':
            in_specs.append(pl.BlockSpec((8, nc), lambda i: (jnp.maximum(i * (tm // 8) - 1, 0), 0)))
        else:
            in_specs.append(pl.BlockSpec((8, nc), lambda i: (jnp.minimum((i + 1) * (tm // 8), T // 8 - 1), 0)))
        args.append(a)
    in_specs += [_full_spec(a) for a in consts]
    args += list(consts)
    out_shape, out_specs = [], []
    for ncols, dtype, kind in outs:
        if kind == 'row':
            out_shape.append(jax.ShapeDtypeStruct((T, ncols), dtype))
            out_specs.append(pl.BlockSpec((tm, ncols), lambda i: (i, 0)))
        else:
            out_shape.append(jax.ShapeDtypeStruct((kind, ncols), dtype))
            out_specs.append(pl.BlockSpec((kind, ncols), lambda i: (0, 0)))

    def body(*refs):
        i = pl.program_id(0)
        vals = [r[...] for r in refs[:n_in]]
        res = fn(i, n_tiles, *vals)
        for (ncols, dtype, kind), o_ref, val in zip(outs, refs[n_in:], res, strict=True):
            if kind == 'row':
                o_ref[...] = val.astype(dtype)
            else:
                @pl.when(i == 0)
                def _():
                    o_ref[...] = jnp.zeros_like(o_ref)
                o_ref[...] += val.astype(dtype)

    h_in, h_in_specs, h_out_specs, h_out_shape, h_scratch = _hosted_args(hosted)
    res = pl.pallas_call(
        _hosting(body, hosted, n_in, len(outs), 0, n_tiles), name=name, grid=(n_tiles,),
        in_specs=in_specs + h_in_specs, out_specs=out_specs + h_out_specs, out_shape=out_shape + h_out_shape,
        scratch_shapes=h_scratch, compiler_params=_cparams(("arbitrary",)),
    )(*args, *h_in)
    return res


def _pick(n, cands):
    for c in cands:
        if n % c == 0:
            return c
    return n


MM_ROWS = (1024, 512, 256, 128)
MM_COLS = (1536, 1408, 1024, 896, 768, 512, 256, 128)
MM_WIDE = 3000


def _mm_nn(a, w, *, name, out_dtype=F32, hosted=None):
    T, K = a.shape
    N = w.shape[1]
    tm = _pick(T, MM_ROWS)
    tn = _pick(N, MM_COLS)
    grid = (N // tn, T // tm)
    h_in, h_in_specs, h_out_specs, h_out_shape, h_scratch = _hosted_args(hosted)

    def body(a_ref, w_ref, o_ref):
        o_ref[...] = jnp.dot(a_ref[...], w_ref[...], preferred_element_type=F32).astype(out_dtype)

    res = pl.pallas_call(
        _hosting(body, hosted, 2, 1, 0, grid), name=name, grid=grid,
        in_specs=[pl.BlockSpec((tm, K), lambda j, i: (i, 0)), pl.BlockSpec((K, tn), lambda j, i: (0, j))] + h_in_specs,
        out_specs=[pl.BlockSpec((tm, tn), lambda j, i: (i, j))] + h_out_specs,
        out_shape=[jax.ShapeDtypeStruct((T, N), out_dtype)] + h_out_shape, scratch_shapes=h_scratch,
        compiler_params=_cparams(("arbitrary", "arbitrary")),
    )(a, w, *h_in)
    return res if hosted else res[0]


def _mm_nt(dy, w, *, name, out_dtype=F32, hosted=None):
    T, N = dy.shape
    K = w.shape[0]
    tm = _pick(T, MM_ROWS if N <= MM_WIDE else MM_ROWS[1:])
    tk = _pick(K, MM_COLS[1:])
    grid = (K // tk, T // tm)
    h_in, h_in_specs, h_out_specs, h_out_shape, h_scratch = _hosted_args(hosted)

    def body(dy_ref, w_ref, o_ref):
        o_ref[...] = lax.dot_general(dy_ref[...], w_ref[...], (((1,), (1,)), ((), ())),
                                     preferred_element_type=F32).astype(out_dtype)

    res = pl.pallas_call(
        _hosting(body, hosted, 2, 1, 0, grid), name=name, grid=grid,
        in_specs=[pl.BlockSpec((tm, N), lambda j, i: (i, 0)), pl.BlockSpec((tk, N), lambda j, i: (j, 0))] + h_in_specs,
        out_specs=[pl.BlockSpec((tm, tk), lambda j, i: (i, j))] + h_out_specs,
        out_shape=[jax.ShapeDtypeStruct((T, K), out_dtype)] + h_out_shape, scratch_shapes=h_scratch,
        compiler_params=_cparams(("arbitrary", "arbitrary")),
    )(dy, w, *h_in)
    return res if hosted else res[0]


def _mm_tn(a, dy, *, name, out_dtype=F32, col_shards=None):
    T, K = a.shape
    N = dy.shape[1]
    tm = _pick(T, MM_ROWS)
    tn = N // col_shards if col_shards else _pick(N, MM_COLS[1:])
    n_t = T // tm

    def body(a_ref, dy_ref, o_ref, acc_ref):
        i = pl.program_id(1)

        @pl.when(i == 0)
        def _():
            acc_ref[...] = jnp.zeros_like(acc_ref)

        acc_ref[...] += lax.dot_general(a_ref[...], dy_ref[...], (((0,), (0,)), ((), ())), preferred_element_type=F32)

        @pl.when(i == n_t - 1)
        def _():
            o_ref[...] = acc_ref[...].astype(out_dtype)

    if col_shards:
        out_specs = pl.BlockSpec((None, K, tn), lambda j, i: (j, 0, 0))
        out_shape = jax.ShapeDtypeStruct((col_shards, K, tn), out_dtype)
    else:
        out_specs = pl.BlockSpec((K, tn), lambda j, i: (0, j))
        out_shape = jax.ShapeDtypeStruct((K, N), out_dtype)
    return pl.pallas_call(
        body, name=name, grid=(N // tn, n_t),
        in_specs=[pl.BlockSpec((tm, K), lambda j, i: (i, 0)), pl.BlockSpec((tm, tn), lambda j, i: (i, j))],
        out_specs=out_specs, out_shape=out_shape, scratch_shapes=[pltpu.VMEM((K, tn), F32)],
        compiler_params=_cparams(("arbitrary", "arbitrary")),
    )(a, dy)


def _mm_then(products, fn, rows, consts, outs, *, tm, name, lhs_fn=None, hosted=None):
    products = [(p + (None,))[:4] for p in products]
    T = (rows[0] if lhs_fn else products[0][0]).shape[-2]
    n_tiles = T // tm
    in_specs, args = [], []
    for a, w, _, j in products:
        if a is not None and j is None:
            in_specs.append(pl.BlockSpec((tm, a.shape[1]), lambda i: (i, 0)))
            args.append(a)
        elif a is not None:
            in_specs.append(pl.BlockSpec((None, tm, a.shape[2]), lambda i, j=j: (j, i, 0)))
            args.append(a)
        in_specs.append(_full_spec(w) if j is None else
                        pl.BlockSpec((None,) + w.shape[1:], lambda i, j=j: (j, 0, 0)))
        args.append(w)
    n_w = len(args)
    in_specs += [pl.BlockSpec((tm, a.shape[1]), lambda i: (i, 0)) for a in rows] + [_full_spec(c_) for c_ in consts]
    args += list(rows) + list(consts)
    n_in = len(args)
    out_shape, out_specs = [], []
    for ncols, dtype, kind in outs:
        if kind == 'row':
            out_shape.append(jax.ShapeDtypeStruct((T, ncols), dtype))
            out_specs.append(pl.BlockSpec((tm, ncols), lambda i: (i, 0)))
        else:
            out_shape.append(jax.ShapeDtypeStruct((kind, ncols), dtype))
            out_specs.append(pl.BlockSpec((kind, ncols), lambda i: (0, 0)))

    def body(*refs):
        i = pl.program_id(0)
        tiles = [r[...] for r in refs[n_w:n_in]]
        made = []
        if lhs_fn:
            made = lhs_fn(*tiles)
            made = list(made) if isinstance(made, tuple) else [made]
            made[0] = made[0].astype(MXU)
        y, pos = None, 0
        for a, _, form, _ in products:
            if a is None:
                lhs = made[0]
            else:
                lhs, pos = refs[pos][...], pos + 1
            dims = (((1,), (0,)), ((), ())) if form == 'nn' else (((1,), (1,)), ((), ()))
            t = lax.dot_general(lhs, refs[pos][...], dims, preferred_element_type=F32)
            pos += 1
            y = t if y is None else y + t
        res = fn(i, n_tiles, y, *made, *tiles)
        for (ncols, dtype, kind), o_ref, val in zip(outs, refs[n_in:], res, strict=True):
            if kind == 'row':
                o_ref[...] = val.astype(dtype)
            else:
                @pl.when(i == 0)
                def _():
                    o_ref[...] = jnp.zeros_like(o_ref)
                o_ref[...] += val.astype(dtype)

    h_in, h_in_specs, h_out_specs, h_out_shape, h_scratch = _hosted_args(hosted)
    return pl.pallas_call(
        _hosting(body, hosted, n_in, len(outs), 0, n_tiles), name=name, grid=(n_tiles,),
        in_specs=in_specs + h_in_specs, out_specs=out_specs + h_out_specs, out_shape=out_shape + h_out_shape,
        scratch_shapes=h_scratch, compiler_params=_cparams(("arbitrary",)))(*args, *h_in)


def _seg_ones(n):
    r = lax.broadcasted_iota(jnp.int32, (n, n), 0) // HD
    c = lax.broadcasted_iota(jnp.int32, (n, n), 1) // HD
    return (r == c).astype(F32)


def _segsum_raw(x):
    ones = _seg_ones(x.shape[1])
    if MXU == F32:
        return jnp.dot(x, ones, precision=HI, preferred_element_type=F32)
    hi = x.astype(MXU)
    lo = (x - hi.astype(F32)).astype(MXU)
    ones = ones.astype(MXU)
    return jnp.dot(hi, ones, preferred_element_type=F32) + jnp.dot(lo, ones, preferred_element_type=F32)


@jax.custom_vjp
def _segsum(x):
    return _segsum_raw(x)


def _segsum_fwd(x):
    return _segsum_raw(x), None


def _segsum_bwd(_, g):
    return (_segsum_raw(g),)


_segsum.defvjp(_segsum_fwd, _segsum_bwd)


def _mxu(x):
    return x.astype(MXU)


@jax.custom_vjp
def _bdot(a, b):
    return jnp.dot(_mxu(a), _mxu(b), preferred_element_type=F32)


def _bdot_fwd(a, b):
    return _bdot(a, b), (a, b)


def _bdot_bwd(res, g):
    a, b = res
    da = lax.dot_general(_mxu(g), _mxu(b), (((1,), (1,)), ((), ())), preferred_element_type=F32)
    db = lax.dot_general(_mxu(a), _mxu(g), (((0,), (0,)), ((), ())), preferred_element_type=F32)
    return da.astype(a.dtype), db.astype(b.dtype)


_bdot.defvjp(_bdot_fwd, _bdot_bwd)


def _sigmoid(x):
    return 1.0 / (1.0 + jnp.exp(-x))


def _softplus(x):
    return jnp.maximum(x, 0.0) + jnp.log(1.0 + jnp.exp(jnp.minimum(x, -x)))


def _norm_mod(x, gain, scale, shift):
    inv = lax.rsqrt(jnp.mean(x * x, axis=-1, keepdims=True) + RMS_EPS)
    return (x * inv) * gain * (1.0 + scale) + shift


def _prep(mixed, decay_w0, lora_up, iclr_a0, gate_up, k_k, k_a):
    r = mixed[:, 0:RW]
    k = mixed[:, RW:2 * RW]
    v = mixed[:, 2 * RW:3 * RW]
    z = mixed[:, 3 * RW:3 * RW + 128]
    xg = mixed[:, 3 * RW + 128:]
    lane = lax.broadcasted_iota(jnp.int32, z.shape, 1)
    tz = jnp.where(lane < 64, jnp.tanh(z), z)
    lo = _bdot(tz, lora_up)
    w_log = -_softplus(-(decay_w0 + lo[:, :RW])) - 0.5
    lw = -jnp.exp(w_log)
    a_ic = _sigmoid(iclr_a0 + lo[:, RW:])
    g = _bdot(_sigmoid(xg), gate_up)
    kk = k * k_k
    kk = kk / jnp.maximum(jnp.sqrt(_segsum(kk * kk)), 1e-12)
    k_mod = k * (1.0 + (a_ic - 1.0) * k_a)
    return jnp.concatenate([r, lw, k_mod, v, -kk, kk * a_ic, g], axis=1)


def _post(y, r, k, v, g, lnx_gain, lnx_bias, r_k):
    mu = _segsum(y) * (1.0 / HD)
    yc = y - mu
    var = _segsum(yc * yc) * (1.0 / HD)
    yn = yc * lax.rsqrt(var + GN_EPS) * lnx_gain + lnx_bias
    bonus = _segsum(r * k * r_k) * v
    return (yn + bonus) * g


def _merge(pg, ma, mb, bias):
    gates = _sigmoid(pg + bias)
    return gates[:, :D] * ma + gates[:, D:] * mb


def _swiglu(u, v):
    return u * _sigmoid(u) * v


def _ffn_in(h, w1, w3, *, name):
    T, K = h.shape
    ns, _, Fs = w1.shape
    tm = _pick(T, MM_ROWS)

    def body(h_ref, w1_ref, w3_ref, u_ref, v_ref, a_ref):
        u = jnp.dot(h_ref[...], w1_ref[...], preferred_element_type=F32).astype(MXU)
        v = jnp.dot(h_ref[...], w3_ref[...], preferred_element_type=F32).astype(MXU)
        u_ref[...] = u
        v_ref[...] = v
        a_ref[...] = _swiglu(u.astype(F32), v.astype(F32)).astype(MXU)

    wspec = pl.BlockSpec((None, K, Fs), lambda j, i: (j, 0, 0))
    ospec = pl.BlockSpec((None, tm, Fs), lambda j, i: (j, i, 0))
    return pl.pallas_call(
        body, name=name, grid=(ns, T // tm),
        in_specs=[pl.BlockSpec((tm, K), lambda j, i: (i, 0)), wspec, wspec],
        out_specs=[ospec] * 3, out_shape=[jax.ShapeDtypeStruct((ns, T, Fs), MXU)] * 3,
        compiler_params=_cparams(("arbitrary", "arbitrary")),
    )(h, w1, w3)


def _ffn_act_bwd(dff, w2, u, v, *, name):
    T, N = dff.shape
    ns, Fs, _ = w2.shape
    tm = _pick(T, MM_ROWS)

    def body(dy_ref, w_ref, u_ref, v_ref, du_ref, dv_ref):
        dact = lax.dot_general(dy_ref[...], w_ref[...], (((1,), (1,)), ((), ())), preferred_element_type=F32)
        _, vjp = jax.vjp(_swiglu, u_ref[...].astype(F32), v_ref[...].astype(F32))
        du, dv = vjp(dact)
        du_ref[...] = du.astype(MXU)
        dv_ref[...] = dv.astype(MXU)

    tile = pl.BlockSpec((None, tm, Fs), lambda j, i: (j, i, 0))
    return pl.pallas_call(
        body, name=name, grid=(ns, T // tm),
        in_specs=[pl.BlockSpec((tm, N), lambda j, i: (i, 0)), pl.BlockSpec((None, Fs, N), lambda j, i: (j, 0, 0)),
                  tile, tile],
        out_specs=[tile, tile], out_shape=[jax.ShapeDtypeStruct((ns, T, Fs), MXU)] * 2,
        compiler_params=_cparams(("arbitrary", "arbitrary")),
    )(dff, w2, u, v)


def _mm_tn_blocks(a, dy, *, name, out_dtype):
    a3, d3 = a.ndim == 3, dy.ndim == 3
    ns = a.shape[0] if a3 else dy.shape[0]
    T, K, N = a.shape[-2], a.shape[-1], dy.shape[-1]
    tm = _pick(T, MM_ROWS)
    n_t = T // tm

    def body(a_ref, dy_ref, o_ref, acc_ref):
        i = pl.program_id(1)

        @pl.when(i == 0)
        def _():
            acc_ref[...] = jnp.zeros_like(acc_ref)

        acc_ref[...] += lax.dot_general(a_ref[...], dy_ref[...], (((0,), (0,)), ((), ())), preferred_element_type=F32)

        @pl.when(i == n_t - 1)
        def _():
            o_ref[...] = acc_ref[...].astype(out_dtype)

    spec = lambda is3, n: (pl.BlockSpec((None, tm, n), lambda j, i: (j, i, 0)) if is3
                           else pl.BlockSpec((tm, n), lambda j, i: (i, 0)))
    return pl.pallas_call(
        body, name=name, grid=(ns, n_t), in_specs=[spec(a3, K), spec(d3, N)],
        out_specs=pl.BlockSpec((None, K, N), lambda j, i: (j, 0, 0)),
        out_shape=jax.ShapeDtypeStruct((ns, K, N), out_dtype), scratch_shapes=[pltpu.VMEM((K, N), F32)],
        compiler_params=_cparams(("arbitrary", "arbitrary")),
    )(a, dy)


@functools.partial(jax.custom_vjp, nondiff_argnums=(1,))
def _lane_roll(x, s):
    return pltpu.roll(x, s, 1)


def _lane_roll_fwd(x, s):
    return pltpu.roll(x, s, 1), None


def _lane_roll_bwd(s, _, g):
    n = g.shape[1]
    return (pltpu.roll(g, (n - s) % n, 1),)


_lane_roll.defvjp(_lane_roll_fwd, _lane_roll_bwd)


def _rope(x, cos, sin_lo, sin_hi):
    n = x.shape[1]
    return x * cos + _lane_roll(x, n - 8) * sin_lo + _lane_roll(x, 8) * sin_hi


def _head_rms(x, gain):
    return x * lax.rsqrt(_segsum(x * x) * (1.0 / HD) + RMS_EPS) * gain


def _attn_blocks(qkv_c, qkv_p, tab_c, tab_p, qg, kg, sinks, first):
    nb = qkv_c.shape[0] // BLK
    G = 4

    def tabs(tab, n):
        return [jnp.tile(tab[:, j * 128:(j + 1) * 128], (1, n // 128)) for j in range(3)]

    qg = jnp.concatenate([qg] * NH, axis=1)
    kg = jnp.concatenate([kg] * 2, axis=1)
    q = _rope(_head_rms(qkv_c[:, :RW], qg), *tabs(tab_c, RW))
    k_in = jnp.concatenate([qkv_p[:, RW:RW + 128], qkv_c[:, RW:RW + 128]], axis=0)
    k = _rope(_head_rms(k_in, kg), *tabs(jnp.concatenate([tab_p, tab_c], axis=0), 128))
    v = jnp.concatenate([qkv_p[:, RW + 128:], qkv_c[:, RW + 128:]], axis=0)

    pile = lambda xs: jnp.concatenate([x_[None] for x_ in xs], axis=0)

    def bands(t):
        return pile([t[b * BLK:(b + 2) * BLK, kvh * HD:(kvh + 1) * HD] for kvh in range(2) for b in range(nb)])

    qs = pile([jnp.concatenate([q[b * BLK:(b + 1) * BLK, (G * kvh + g) * HD:(G * kvh + g + 1) * HD]
                                for g in range(G)], axis=0) for kvh in range(2) for b in range(nb)])
    s = _bmm(qs, bands(k), 2, 2, 1) * (HD ** -0.5)
    qi = lax.broadcasted_iota(jnp.int32, (G * BLK, 2 * BLK), 0) % BLK
    kj = lax.broadcasted_iota(jnp.int32, (G * BLK, 2 * BLK), 1)
    dist = qi + BLK - kj
    in_band = (dist >= 0) & (dist < BLK)
    pair = lax.broadcasted_iota(jnp.int32, (2 * nb, 1, 1), 0)
    no_prev = (pair % nb == 0) & first
    valid = in_band[None] & (jnp.logical_not(no_prev) | (kj >= BLK)[None])
    s = jnp.where(valid, s, NEG_INF)
    row_g = lax.broadcasted_iota(jnp.int32, (G * BLK, 1), 0) // BLK
    sink = []
    for kvh in range(2):
        col = jnp.zeros((G * BLK, 1), F32)
        for g in range(G):
            col = jnp.where(row_g == g, sinks[:, G * kvh + g:G * kvh + g + 1], col)
        sink += [col] * nb
    sink = pile(sink)
    m = lax.stop_gradient(jnp.maximum(jnp.max(s, axis=-1, keepdims=True), sink))
    e = jnp.exp(s - m)
    p = e * (1.0 / (jnp.sum(e, axis=-1, keepdims=True) + jnp.exp(sink - m)))
    o = _bmm(p, bands(v), 2, 1, 1)
    return jnp.concatenate([jnp.concatenate([o[kvh * nb + b, g * BLK:(g + 1) * BLK] for kvh in range(2)
                                             for g in range(G)], axis=1) for b in range(nb)], axis=0)


def _heads(x):
    return jnp.stack([x[:, h * HD:(h + 1) * HD] for h in range(NH)], axis=0)


def _unheads(x):
    return jnp.concatenate([x[h] for h in range(NH)], axis=1)


def _split(x, n):
    parts, rest = [], x
    for _ in range(n):
        p = rest.astype(MXU)
        parts.append(p)
        rest = rest - p.astype(F32)
    return parts


def _bdot_batched(a, b, ca, cb):
    return lax.dot_general(a, b, (((ca,), (cb,)), ((0,), (0,))), preferred_element_type=F32)


def _bmm_passes(a, b, ca, cb, passes):
    if MXU == F32:
        return lax.dot_general(a, b, (((ca,), (cb,)), ((0,), (0,))), precision=HI, preferred_element_type=F32)
    if passes == 1:
        return _bdot_batched(a.astype(MXU), b.astype(MXU), ca, cb)
    (a0, a1), (b0, b1) = _split(a, 2), _split(b, 2)
    return _bdot_batched(a0, b0, ca, cb) + (_bdot_batched(a0, b1, ca, cb) + _bdot_batched(a1, b0, ca, cb))


@functools.partial(jax.custom_vjp, nondiff_argnums=(2, 3, 4))
def _bmm(a, b, ca, cb, passes=1):
    return _bmm_passes(a, b, ca, cb, passes)


def _bmm_fwd(a, b, ca, cb, passes):
    return _bmm_passes(a, b, ca, cb, passes), (a, b)


def _bmm_bwd(ca, cb, passes, res, g):
    a, b = res
    if (ca, cb) == (2, 1):
        return _bmm_passes(g, b, 2, 2, passes), _bmm_passes(a, g, 1, 1, passes)
    if (ca, cb) == (2, 2):
        return _bmm_passes(g, b, 2, 1, passes), _bmm_passes(g, a, 1, 1, passes)
    return _bmm_passes(b, g, 2, 2, passes), _bmm_passes(a, g, 2, 1, passes)


_bmm.defvjp(_bmm_fwd, _bmm_bwd)


def _tri_dot(x, transpose):
    C = x.shape[1]
    ri = lax.broadcasted_iota(jnp.int32, (C, C), 0)
    ci = lax.broadcasted_iota(jnp.int32, (C, C), 1)
    tri = jnp.broadcast_to(((ri <= ci) if transpose else (ri >= ci)).astype(MXU), (x.shape[0], C, C))
    if MXU == F32:
        return lax.dot_general(tri, x, (((2,), (1,)), ((0,), (0,))), precision=HI, preferred_element_type=F32)
    p0, p1, p2 = _split(x, 3)
    return _bdot_batched(tri, p0, 2, 1) + (_bdot_batched(tri, p1, 2, 1) + _bdot_batched(tri, p2, 2, 1))


@jax.custom_vjp
def _cumsum_rows(x):
    return _tri_dot(x, False)


def _cumsum_rows_fwd(x):
    return _tri_dot(x, False), None


def _cumsum_rows_bwd(_, g):
    return (_tri_dot(g, True),)


_cumsum_rows.defvjp(_cumsum_rows_fwd, _cumsum_rows_bwd)

P_SCORE = 3
P_APPLY = 1
P_SOLVE = 1
P_STATE = 1
SCAN_CHUNKS = (2, 1)


def _neumann(l):
    C = l.shape[1]
    eye = (lax.broadcasted_iota(jnp.int32, (C, C), 0) == lax.broadcasted_iota(jnp.int32, (C, C), 1)).astype(F32)
    x, lp = eye + l, l
    for _ in range(int(math.log2(C)) - 1):
        lp = _bmm(lp, lp, 2, 1, P_SOLVE)
        x = x + _bmm(x, lp, 2, 1, P_SOLVE)
    return x


@jax.custom_vjp
def _unit_lower_inverse(l):
    return _neumann(l)


def _unit_lower_inverse_fwd(l):
    x = _neumann(l)
    return x, x


def _unit_lower_inverse_bwd(x, g):
    return (_bmm(_bmm(x, g, 1, 1, P_SOLVE), x, 2, 2, P_SOLVE),)


_unit_lower_inverse.defvjp(_unit_lower_inverse_fwd, _unit_lower_inverse_bwd)


def _known_inverse(x):
    @jax.custom_vjp
    def f(l):
        return x

    f.defvjp(lambda l: (x, None), lambda _, g: (_bmm(_bmm(x, g, 1, 1, P_SOLVE), x, 2, 2, P_SOLVE),))
    return f


def _chunk(S0, r, lw, k, v, a, b, inverse=None):
    C = CHUNK
    n = r.shape[1] // C
    fold = lambda t: t.reshape(NH * n, C, HD)
    r, lw, k, v, a, b = (fold(t) for t in (r, lw, k, v, a, b))
    ri = lax.broadcasted_iota(jnp.int32, (C, C), 0)
    ci = lax.broadcasted_iota(jnp.int32, (C, C), 1)
    strict = (ri > ci)
    cum = _cumsum_rows(lw)
    p_in = jnp.exp(cum)
    p_ex = jnp.exp(cum - lw)
    p_inv = jnp.exp(-cum)
    at, rt, bt, kt = a * p_ex, r * p_in, b * p_inv, k * p_inv
    lhs = jnp.concatenate([at, rt], axis=1)
    rhs_ = jnp.concatenate([bt, kt], axis=1)
    sc = _bmm(lhs, rhs_, 2, 2, P_SCORE)
    a_ab = jnp.where(strict, sc[:, :C, :C], 0.0)
    a_ak = jnp.where(strict, sc[:, :C, C:], 0.0)
    incl2 = (lax.broadcasted_iota(jnp.int32, (C, 2 * C), 0) >= lax.broadcasted_iota(jnp.int32, (C, 2 * C), 1) % C)
    a_r = jnp.where(incl2, sc[:, C:, :], 0.0)
    av = _bmm(a_ak, v, 2, 1, P_APPLY)
    x = x_all = (_unit_lower_inverse if inverse is None else _known_inverse(inverse))(a_ab)
    p_last = jnp.exp(cum[:, C - 1:C, :])
    per_chunk = lambda t: t.reshape((NH, n) + t.shape[1:])
    lhs, rhs_, a_r, av, x, v, p_last = (per_chunk(t) for t in (lhs, rhs_, a_r, av, x, v, p_last))
    S, ys = S0, []
    for c in range(n):
        s0 = _bmm(lhs[:, c], S, 2, 2, P_STATE)
        u = _bmm(x[:, c], s0[:, :C] + av[:, c], 2, 1, P_SOLVE)
        uv = jnp.concatenate([u, v[:, c]], axis=1)
        ys.append(s0[:, C:] + _bmm(a_r[:, c], uv, 2, 1, P_APPLY))
        S = (S + _bmm(uv, rhs_[:, c], 1, 1, P_STATE)) * p_last[:, c]
    return jnp.concatenate(ys, axis=1), S, x_all


def _hosting(body, ex, n_in, n_out, n_scratch, n_steps):
    if ex is None:
        return body

    def wrapped(*refs):
        a = n_in
        b = a + ex.n_in
        c = b + n_out
        d = c + ex.n_out
        e = d + n_scratch
        ex_refs = (refs[a:b], refs[c:d], refs[e:])
        grid = n_steps if isinstance(n_steps, tuple) else (n_steps,)
        first = last = True
        for ax_, size in enumerate(grid):
            first = first & (pl.program_id(ax_) == 0)
            last = last & (pl.program_id(ax_) == size - 1)

        @pl.when(first)
        def _():
            ex.start(*ex_refs)

        body(*refs[:a], *refs[b:c], *refs[d:e])

        @pl.when(last)
        def _():
            ex.wait(*ex_refs)

    return wrapped


def _hosted_args(ex):
    if ex is None:
        return [], [], [], [], []
    any_spec = pl.BlockSpec(memory_space=pl.ANY)
    return list(ex.arrays), [any_spec] * ex.n_in, [any_spec] * ex.n_out, list(ex.out_shape), list(ex.scratch)


def _scan_fwd(rw, *, name, hosted=None):
    T = rw.shape[0]
    nc = _pick(T // CHUNK, SCAN_CHUNKS)
    rows = CHUNK * nc
    n = T // rows
    h_in, h_in_specs, h_out_specs, h_out_shape, h_scratch = _hosted_args(hosted)

    def body(r_ref, lw_ref, k_ref, v_ref, a_ref, b_ref, y_ref, ck_ref, inv_ref, s_ref):
        @pl.when(pl.program_id(0) == 0)
        def _():
            s_ref[...] = jnp.zeros_like(s_ref)

        S0 = s_ref[...]
        ck_ref[0] = S0
        y, S1, inv = _chunk(S0, *[_heads(ref[...]) for ref in (r_ref, lw_ref, k_ref, v_ref, a_ref, b_ref)])
        y_ref[...] = _unheads(y)
        inv_ref[0] = inv
        s_ref[...] = S1

    col = lambda j: pl.BlockSpec((rows, RW), lambda i: (i, j))
    return pl.pallas_call(
        _hosting(body, hosted, 6, 3, 1, n), name=name, grid=(n,),
        in_specs=[col(j) for j in range(6)] + h_in_specs,
        out_specs=[pl.BlockSpec((rows, RW), lambda i: (i, 0)),
                   pl.BlockSpec((1, NH, HD, HD), lambda i: (i, 0, 0, 0)),
                   pl.BlockSpec((1, NH * nc, CHUNK, CHUNK), lambda i: (i, 0, 0, 0))] + h_out_specs,
        out_shape=[jax.ShapeDtypeStruct((T, RW), F32), jax.ShapeDtypeStruct((n, NH, HD, HD), F32),
                   jax.ShapeDtypeStruct((n, NH * nc, CHUNK, CHUNK), F32)] + h_out_shape,
        scratch_shapes=[pltpu.VMEM((NH, HD, HD), F32)] + h_scratch,
        compiler_params=_cparams(("arbitrary",)),
    )(rw, rw, rw, rw, rw, rw, *h_in)


def _scan_bwd(rw, ck, inv, dy, *, name, hosted=None):
    T = rw.shape[0]
    nc = _pick(T // CHUNK, SCAN_CHUNKS)
    rows = CHUNK * nc
    n = T // rows

    def body(r_ref, lw_ref, k_ref, v_ref, a_ref, b_ref, ck_ref, inv_ref, dy_ref, o_ref, ds_ref):
        @pl.when(pl.program_id(0) == 0)
        def _():
            ds_ref[...] = jnp.zeros_like(ds_ref)

        prim = [_heads(ref[...]) for ref in (r_ref, lw_ref, k_ref, v_ref, a_ref, b_ref)]
        known = inv_ref[0]
        _, vjp = jax.vjp(lambda S0, *p: _chunk(S0, *p, inverse=known)[:2], ck_ref[0], *prim)
        grads = vjp((_heads(dy_ref[...]), ds_ref[...]))
        ds_ref[...] = grads[0]
        o_ref[...] = jnp.concatenate([_unheads(g) for g in grads[1:]], axis=1).astype(o_ref.dtype)

    h_in, h_in_specs, h_out_specs, h_out_shape, h_scratch = _hosted_args(hosted)
    col = lambda j: pl.BlockSpec((rows, RW), lambda i: (n - 1 - i, j))
    return pl.pallas_call(
        _hosting(body, hosted, 9, 1, 1, n), name=name, grid=(n,),
        in_specs=[col(j) for j in range(6)] + [pl.BlockSpec((1, NH, HD, HD), lambda i: (n - 1 - i, 0, 0, 0)),
                                               pl.BlockSpec((1, NH * nc, CHUNK, CHUNK), lambda i: (n - 1 - i, 0, 0, 0)),
                                               pl.BlockSpec((rows, RW), lambda i: (n - 1 - i, 0))] + h_in_specs,
        out_specs=[pl.BlockSpec((rows, 6 * RW), lambda i: (n - 1 - i, 0))] + h_out_specs,
        out_shape=[jax.ShapeDtypeStruct((T, 6 * RW), MXU)] + h_out_shape,
        scratch_shapes=[pltpu.VMEM((NH, HD, HD), F32)] + h_scratch,
        compiler_params=_cparams(("arbitrary",)),
    )(rw, rw, rw, rw, rw, rw, ck, inv, dy, *h_in)


ATTN_BLOCKS = (4, 2, 1)

def _attn_fwd(qkv, tab, qg, kg, sinks, *, name, hosted=None):
    T = qkv.shape[0]
    nb = _pick(T // BLK, ATTN_BLOCKS)
    n = T // (BLK * nb)
    h_in, h_in_specs, h_out_specs, h_out_shape, h_scratch = _hosted_args(hosted)

    def body(c_ref, p_ref, tc_ref, tp_ref, qg_ref, kg_ref, s_ref, o_ref):
        o_ref[...] = _attn_blocks(c_ref[...], p_ref[...], tc_ref[...], tp_ref[...], qg_ref[...], kg_ref[...],
                                  s_ref[...], pl.program_id(0) == 0).astype(o_ref.dtype)

    cur = lambda w: pl.BlockSpec((nb * BLK, w), lambda i: (i, 0))
    prev = lambda w: pl.BlockSpec((BLK, w), lambda i: (jnp.maximum(i * nb - 1, 0), 0))
    return pl.pallas_call(
        _hosting(body, hosted, 7, 1, 0, n), name=name, grid=(n,),
        in_specs=[cur(QKV_W), prev(QKV_W), cur(3 * 128), prev(3 * 128), _full_spec(qg), _full_spec(kg),
                  _full_spec(sinks)] + h_in_specs,
        out_specs=[cur(RW)] + h_out_specs, out_shape=[jax.ShapeDtypeStruct((T, RW), MXU)] + h_out_shape,
        scratch_shapes=h_scratch,
        compiler_params=_cparams(("arbitrary",)),
    )(qkv, qkv, tab, tab, qg, kg, sinks, *h_in)


def _attn_bwd(qkv, tab, qg, kg, sinks, dy, *, name, hosted=None):
    T = qkv.shape[0]
    nb = _pick(T // BLK, ATTN_BLOCKS)
    n = T // (BLK * nb)
    h_in, h_in_specs, h_out_specs, h_out_shape, h_scratch = _hosted_args(hosted)

    def body(c_ref, p_ref, tc_ref, tp_ref, qg_ref, kg_ref, s_ref, dy_ref, dqkv_ref, dqg_ref, dkg_ref, ds_ref, carry_ref):
        i = pl.program_id(0)

        @pl.when(i == 0)
        def _():
            carry_ref[...] = jnp.zeros_like(carry_ref)
            dqg_ref[...] = jnp.zeros_like(dqg_ref)
            dkg_ref[...] = jnp.zeros_like(dkg_ref)
            ds_ref[...] = jnp.zeros_like(ds_ref)

        tc, tp = tc_ref[...], tp_ref[...]
        f = lambda c, p_, qg_, kg_, sk: _attn_blocks(c, p_, tc, tp, qg_, kg_, sk, i == n - 1)
        _, vjp = jax.vjp(f, c_ref[...], p_ref[...], qg_ref[...], kg_ref[...], s_ref[...])
        dc, dp, dqg, dkg, dsk = vjp(dy_ref[...].astype(F32))
        last = slice((nb - 1) * BLK, nb * BLK)
        dqkv_ref[...] = dc.astype(dqkv_ref.dtype)
        dqkv_ref[last, :] = (dc[last] + carry_ref[...]).astype(dqkv_ref.dtype)
        carry_ref[...] = dp
        dqg_ref[...] += dqg
        dkg_ref[...] += dkg
        ds_ref[...] += dsk

    cur = lambda w: pl.BlockSpec((nb * BLK, w), lambda i: (n - 1 - i, 0))
    prev = lambda w: pl.BlockSpec((BLK, w), lambda i: (jnp.maximum((n - 1 - i) * nb - 1, 0), 0))
    return pl.pallas_call(
        _hosting(body, hosted, 8, 4, 1, n), name=name, grid=(n,),
        in_specs=[cur(QKV_W), prev(QKV_W), cur(3 * 128), prev(3 * 128), _full_spec(qg), _full_spec(kg), _full_spec(sinks),
                  cur(RW)] + h_in_specs,
        out_specs=[cur(QKV_W), _full_spec(qg), _full_spec(kg), _full_spec(sinks)] + h_out_specs,
        out_shape=[jax.ShapeDtypeStruct((T, QKV_W), MXU), jax.ShapeDtypeStruct(qg.shape, F32),
                   jax.ShapeDtypeStruct(kg.shape, F32), jax.ShapeDtypeStruct(sinks.shape, F32)] + h_out_shape,
        scratch_shapes=[pltpu.VMEM((BLK, QKV_W), F32)] + h_scratch,
        compiler_params=_cparams(("arbitrary",)),
    )(qkv, qkv, tab, tab, qg, kg, sinks, dy, *h_in)


def _shift_down(cur, prev8, i):
    rolled = pltpu.roll(cur, 1, 0)
    first_row = jnp.where(i > 0, prev8[7:8, :], 0.0)
    row = lax.broadcasted_iota(jnp.int32, cur.shape, 0)
    return jnp.where(row == 0, first_row, rolled)


def _shift_up(cur, next8, i, n):
    tm = cur.shape[0]
    rolled = pltpu.roll(cur, tm - 1, 0)
    last_row = jnp.where(i < n - 1, next8[0:1, :], 0.0)
    row = lax.broadcasted_iota(jnp.int32, cur.shape, 0)
    return jnp.where(row == tm - 1, last_row, rolled)


def _rope_table(positions):
    half = HD // 8
    inv_freq = 500000.0 ** (-jnp.arange(half, dtype=F32) / half)
    lane = jnp.arange(128) % HD
    rotary = lane < 2 * half
    freq = jnp.where(rotary, inv_freq[lane % half], 0.0)
    ang = positions.astype(F32)[:, None] * freq[None, :]
    cos, sin = jnp.cos(ang), jnp.sin(ang)
    return jnp.concatenate([jnp.where(rotary, cos, 1.0), jnp.where(lane < half, -sin, 0.0),
                            jnp.where(rotary & (lane >= half), sin, 0.0)], axis=1)


GATHER_BEHIND = {"f_proj_shift": [('ffn_w3', 512, 768)], "f_proj_gates": [('ffn_w3', 768, 1024)],
                 "f_prep": [('ffn_w2', 352, 704)],
                 "f_scan": [('ffn_w1', 0, 1024), ('w_branch_a', 0, 512), ('w_branch_b', 0, 512)],
                 "f_post": [('ffn_w3', 0, 512)], "f_attn": [('ffn_w2', 0, 352), ('w_out', 0, 256)]}
LATE = ['w_out', 'w_branch_a', 'w_branch_b', 'ffn_w1', 'ffn_w3', 'ffn_w2']
FFN = ('ffn_w1', 'ffn_w3', 'ffn_w2')
BACK_ATTN = ['w_out', 'w_branch_a', 'w_branch_b', 'ffn_w2']
BACK_SCAN = ['ffn_w1', 'ffn_w3']
BACK_LAST = ['w_in', 'decay_up', 'iclr_up', 'gate_up']


def _full_weight(g, ax):
    return g.reshape(-1, g.shape[2]) if ax == 0 else jnp.concatenate([g[j] for j in range(4)], axis=1)


def _local_step(x, target, ada, tab, w, s, shards=None):
    T = x.shape[0]
    tm = _pick(T, (512, 256, 128))
    tm_wide = _pick(T, (256, 128))
    tm_vjp = _pick(T, (256, 128))
    row = lambda n, dt=F32: (n, dt, 'row')
    acc = lambda n, r=1: (n, F32, r)

    def f_norm1(x_, g, ada_):
        return _norm_mod(x_, g, ada_[:, D:2 * D], ada_[:, 0:D])

    landed = {}

    def behind(kernel_name):
        if not shards:
            return None
        pieces = GATHER_BEHIND[kernel_name]
        return _GatherChips([shards[n] for n, _, _ in pieces], rows=[(lo, hi) for _, lo, hi in pieces])

    def took(kernel_name, got):
        landed.update(zip(GATHER_BEHIND[kernel_name], got))

    def mm_behind(a_, w_, kernel_name, **kw):
        ex = behind(kernel_name)
        res = _mm_nn(a_, w_, name=kernel_name, hosted=ex, **kw)
        if ex:
            took(kernel_name, res[1:])
            return res[0]
        return res

    h1, proj, *got = _mm_then([(None, w['w_in'][:, :SHIFT_W], 'nn')], lambda i, n, y, h, *_: (h, y), [x],
                              [s['norm1_gain'], ada], [row(D, MXU), row(SHIFT_W)], tm=tm, name="f_proj_shift",
                              lhs_fn=f_norm1, hosted=behind("f_proj_shift"))
    took("f_proj_shift", got)
    proj_qkv = _mm_nn(h1, w['w_in'][:, SHIFT_W:SHIFT_W + QKV_W], name="f_proj_qkv")
    proj_g = mm_behind(h1, w['w_in'][:, SHIFT_W + QKV_W:], "f_proj_gates", out_dtype=MXU)
    prep_consts = [s['decay_w0'], s['lora_up'], s['iclr_a0'], s['gate_up'], s['k_k'], s['k_a']]

    def f_prep(i, n, cur, prev8, mu, *params):
        mixed = cur + (_shift_down(cur, prev8, i) - cur) * mu
        return (_prep(mixed, *params),)
    rw, *got = _rowwise(f_prep, [(proj, SHIFT_W)], [s['tshift_mu']] + prep_consts, [row(7 * RW)], tm=tm_wide,
                        name="f_prep", halo=[(proj, SHIFT_W, 'prev')], hosted=behind("f_prep"))
    took("f_prep", got)
    y, ck, inv, *got = _scan_fwd(rw, name="f_scan", hosted=behind("f_scan"))
    took("f_scan", got)
    post_consts = [s['lnx_gain'], s['lnx_bias'], s['r_k']]

    rkvg = [(rw, RW, j) for j in (0, 2, 3, 6)]

    def f_post(i, n, *args):
        return (_post(*args),)
    ya, *got = _rowwise(f_post, [y] + rkvg, post_consts, [row(RW, MXU)], tm=tm_wide, name="f_post",
                        hosted=behind("f_post"))
    took("f_post", got)
    yb, *got = _attn_fwd(proj_qkv, tab, s['q_norm_gain'], s['k_norm_gain'], s['attn_sinks'], name="f_attn",
                         hosted=behind("f_attn"))
    took("f_attn", got)
    w = dict(w)
    if shards:
        ax = dict(SHARDED)
        for n in LATE:
            rows = [landed[key] for key in sorted(k_ for k_ in landed if k_[0] == n)]
            blocks = rows[0] if len(rows) == 1 else jnp.concatenate(rows, axis=1)
            w[n] = blocks if n in FFN else _full_weight(blocks, ax[n])
    else:
        fs = DFF // 4
        w.update({n: w[n].reshape(D, 4, fs).transpose(1, 0, 2) for n in ('ffn_w1', 'ffn_w3')})
        w['ffn_w2'] = w['ffn_w2'].reshape(4, fs, D)
    def f_merge(pg, ya_, yb_, x_, wa, wb, bias, g, ada_):
        ma_ = jnp.dot(ya_, wa, preferred_element_type=F32).astype(MXU)
        mb_ = jnp.dot(yb_, wb, preferred_element_type=F32).astype(MXU)
        return _merge(pg.astype(F32), ma_.astype(F32), mb_.astype(F32), bias), ma_, mb_

    def f_res1(i, n, mo_, merged_, ma_, mb_, pg, ya_, yb_, x_, wa, wb, bias, g, ada_):
        x1_ = x_ + ada_[:, 2 * D:3 * D] * mo_
        return merged_, ma_, mb_, mo_, x1_, _norm_mod(x1_, g, ada_[:, 4 * D:5 * D], ada_[:, 3 * D:4 * D])
    merged, ma, mb, mo, x1, h2 = _mm_then(
        [(None, w['w_out'], 'nn')], f_res1, [proj_g, ya, yb, x],
        [w['w_branch_a'], w['w_branch_b'], s['branch_gate_b'], s['norm2_gain'], ada],
        [row(D, MXU), row(D, MXU), row(D, MXU), row(D), row(D), row(D, MXU)], tm=tm, name="f_out", lhs_fn=f_merge)
    u, v, act = _ffn_in(h2, w['ffn_w1'], w['ffn_w3'], name="f_ffn_in")

    def f_loss(i, n, ff_, x1_, tgt, ada_):
        g2 = ada_[:, 5 * D:6 * D]
        err = x1_ + g2 * ff_ - tgt
        dx2 = err * (1.0 / D)
        loss = 0.5 * jnp.sum(jnp.sum(err * err, axis=1, keepdims=True) * (1.0 / D), axis=0, keepdims=True)
        return dx2, (dx2 * g2), jnp.broadcast_to(loss, (1, 128)), jnp.sum(dx2 * ff_, axis=0, keepdims=True)
    dx2, dff, loss, dgate2 = _mm_then([(act, w['ffn_w2'], 'nn', j) for j in range(4)], f_loss, [x1, target], [ada],
                                      [row(D), row(D, MXU), acc(128), acc(D)], tm=tm, name="f_ffn_out")

    du, dv = _ffn_act_bwd(dff, w['ffn_w2'], u, v, name="b_ffn_out_dx")
    g_w2 = _mm_tn_blocks(act, dff, name="b_ffn_out_dw", out_dtype=MXU)
    g_w1 = _mm_tn_blocks(h2, du, name="b_ffn_w1_dw", out_dtype=MXU)
    g_w3 = _mm_tn_blocks(h2, dv, name="b_ffn_w3_dw", out_dtype=MXU)

    def b_res1(i, n, dh2_, x1_, dx2_, mo_, g, ada_):
        _, vjp = jax.vjp(_norm_mod, x1_, g, ada_[:, 4 * D:5 * D], ada_[:, 3 * D:4 * D])
        dxn, dg, dsc, dsh = vjp(dh2_)
        dx1_ = dxn + dx2_
        g1 = ada_[:, 2 * D:3 * D]
        return dx1_, dx1_ * g1, dg, dsc, dsh, jnp.sum(dx1_ * mo_, axis=0, keepdims=True)
    dx1, dmo, d_gain2, d_scale2, d_shift2, dgate1 = _mm_then(
        [(t, w[n], 'nt', j) for t, n in ((du, 'ffn_w1'), (dv, 'ffn_w3')) for j in range(4)], b_res1, [x1, dx2, mo],
        [s['norm2_gain'], ada],
        [row(D), row(D, MXU), acc(D), acc(D), acc(D), acc(D)], tm=tm_wide, name="b_ffn_in_dx")
    g_wout = _mm_tn(merged, dmo, name="b_out_dw", out_dtype=MXU)

    def b_merge(i, n, dm, pg, ma_, mb_, wa, wb, bias):
        _, vjp = jax.vjp(_merge, pg.astype(F32), ma_.astype(F32), mb_.astype(F32), bias)
        dpg, dma_, dmb_, dbias = vjp(dm)
        dma_, dmb_ = dma_.astype(MXU), dmb_.astype(MXU)
        nt = lambda a_, b_: lax.dot_general(a_, b_, (((1,), (1,)), ((), ())), preferred_element_type=F32)
        return dpg, dma_, dmb_, nt(dma_, wa), nt(dmb_, wb), dbias
    dpg, dma, dmb, dya, dyb, d_bias = _mm_then(
        [(dmo, w['w_out'], 'nt')], b_merge, [proj_g, ma, mb], [w['w_branch_a'], w['w_branch_b'], s['branch_gate_b']],
        [row(GATE_W, MXU), row(D, MXU), row(D, MXU), row(RW), row(RW), acc(GATE_W)], tm=tm, name="b_out_dx")
    g_wa = _mm_tn(ya, dma, name="b_branch_a_dw", out_dtype=MXU, col_shards=4)
    g_wb = _mm_tn(yb, dmb, name="b_branch_b_dw", out_dtype=MXU, col_shards=4)
    gw = dict(w_branch_a=g_wa, w_branch_b=g_wb, w_out=g_wout.reshape(4, D // 4, D), ffn_w1=g_w1, ffn_w3=g_w3,
              ffn_w2=g_w2)
    recv = {}
    dqkv, d_qg, d_kg, d_sinks, *got = _attn_bwd(
        proj_qkv, tab, s['q_norm_gain'], s['k_norm_gain'], s['attn_sinks'], dyb, name="b_attn",
        hosted=shards and _ScatterChips([gw[n] for n in BACK_ATTN]))
    recv.update(zip(BACK_ATTN, got))

    def b_post(i, n, y_, r_, k_, v_, g_, dya_, *params):
        _, vjp = jax.vjp(_post, y_, r_, k_, v_, g_, *params)
        dy_, dr_, dk_, dv_, dg_, *dparams = vjp(dya_)
        return (dy_, jnp.concatenate([dr_, dk_, dv_, dg_], axis=1), *dparams)
    dy, drkvg, d_lnx_gain, d_lnx_bias, d_r_k = _rowwise(
        b_post, [y] + rkvg + [dya], post_consts, [row(RW), row(4 * RW, MXU), acc(RW), acc(RW), acc(RW)], tm=tm_wide,
        name="b_post")
    dscan, *got = _scan_bwd(rw, ck, inv, dy, name="b_scan",
                            hosted=shards and _ScatterChips([gw[n] for n in BACK_SCAN]))
    recv.update(zip(BACK_SCAN, got))

    def b_prep(i, n, cur, drw_, dscan_, prev8, mu, *params):
        shifted = _shift_down(cur, prev8, i)
        mixed = cur + (shifted - cur) * mu
        _, vjp = jax.vjp(_prep, mixed, *params)
        blk = lambda t, j: t[:, j * RW:(j + 1) * RW].astype(F32)
        ct = jnp.concatenate([blk(dscan_, 0) + blk(drw_, 0), blk(dscan_, 1), blk(dscan_, 2) + blk(drw_, 1),
                              blk(dscan_, 3) + blk(drw_, 2), blk(dscan_, 4), blk(dscan_, 5), blk(drw_, 3)], axis=1)
        grads = vjp(ct)
        dmixed = grads[0]
        return (dmixed, jnp.sum(dmixed * (shifted - cur), axis=0, keepdims=True)) + tuple(grads[1:])
    dmixed, d_mu, d_w0, d_lora, d_a0, d_gate_up, d_kk, d_ka = _rowwise(
        b_prep, [(proj, SHIFT_W), drkvg, dscan], [s['tshift_mu']] + prep_consts,
        [row(SHIFT_W), acc(SHIFT_W), acc(RW), acc(2 * RW, 128), acc(RW), acc(RW, 128), acc(RW), acc(RW)],
        tm=tm_vjp, name="b_prep", halo=[(proj, SHIFT_W, 'prev')])

    def b_gather(i, n, dm, dqkv_, dpg_, next8, mu):
        dcur = dm * (1.0 - mu) + _shift_up(dm, next8, i, n) * mu
        return (jnp.concatenate([dcur.astype(MXU), dqkv_, dpg_], axis=1),)
    (dproj,) = _rowwise(b_gather, [dmixed, dqkv, dpg], [s['tshift_mu']], [row(IN_W, MXU)], tm=tm_wide, name="b_gather",
                        halo=[(dmixed, SHIFT_W, 'next')])
    g_win = _mm_tn(h1, dproj, name="b_proj_dw", out_dtype=MXU, col_shards=4)

    def col_blocks(g):
        k, n = g.shape
        return g.reshape(k, 4, n // 4).transpose(1, 0, 2).astype(MXU)
    gw.update(w_in=g_win, decay_up=col_blocks(d_lora[:64, :RW]), iclr_up=col_blocks(d_lora[64:, RW:]),
              gate_up=col_blocks(d_gate_up))
    top, bottom = None, None
    if shards:
        top = _ScatterChips([gw[n] for n in BACK_LAST], rows=[(0, D // 2)] + [None] * (len(BACK_LAST) - 1))
        bottom = _ScatterChips([gw['w_in']], rows=[(D // 2, 3 * D // 4)])
        dh1, *got_top = _mm_nt(dproj, w['w_in'], name="b_proj_dx", hosted=top)
    else:
        dh1 = _mm_nt(dproj, w['w_in'], name="b_proj_dx")

    def b_norm1(i, n, x_, dh1_, dx1_, g, ada_):
        _, vjp = jax.vjp(_norm_mod, x_, g, ada_[:, D:2 * D], ada_[:, 0:D])
        dxn, dg, dsc, dsh = vjp(dh1_)
        return dxn + dx1_, dg, dsc, dsh
    dx, d_gain1, d_scale1, d_shift1, *got_bottom = _rowwise(
        b_norm1, [x, dh1, dx1], [s['norm1_gain'], ada], [row(D), acc(D), acc(D), acc(D)], tm=tm, name="b_norm1",
        hosted=bottom)
    if shards:
        recv.update(zip(BACK_LAST[1:], got_top[1:]))
        recv['w_in'] = [got_top[0], got_bottom[0]]

    d_ada = jnp.concatenate([d_shift1, d_scale1, dgate1, d_shift2, d_scale2, dgate2], axis=1)
    gs = dict(norm1_gain=d_gain1, norm2_gain=d_gain2, tshift_mu=d_mu, decay_w0=d_w0, iclr_a0=d_a0, k_k=d_kk, k_a=d_ka,
              r_k=d_r_k, lnx_gain=d_lnx_gain, lnx_bias=d_lnx_bias, q_norm_gain=d_qg, k_norm_gain=d_kg,
              attn_sinks=d_sinks, branch_gate_b=d_bias)
    return loss, dx, d_ada, gw, gs, recv


ANY = pl.BlockSpec(memory_space=pl.ANY)


def _place():
    x, y, c = lax.axis_index("x"), lax.axis_index("y"), lax.axis_index("c")
    return x, y, c, [(1 - x, y), (x, 1 - y), (1 - x, 1 - y)]


def _all_gather8(x_shard, *, name):
    m_per, n = x_shard.shape

    def body(x_ref, out_ref, send_sems, recv_sems, local_sem):
        x, y, c, chips = _place()
        me, sibling = (x, y, c), (x, y, 1 - c)

        def rows(px, py, pc):
            return out_ref.at[pl.ds((4 * px + 2 * py + pc) * m_per, m_per), :]

        def copy(k, block, to, src=None):
            return pltpu.make_async_remote_copy(
                src_ref=rows(*block) if src is None else src, dst_ref=rows(*block),
                send_sem=send_sems.at[k], recv_sem=recv_sems.at[k], device_id=to, device_id_type=MESH)

        mine = pltpu.make_async_copy(x_ref, rows(*me), local_sem)
        mine.start()
        first = [copy(0, me, sibling, src=x_ref)]
        first += [copy(1 + j, me, (*chip, c), src=x_ref) for j, chip in enumerate(chips)]
        for cp in first:
            cp.start()
        passed = [copy(4 + j, (*chip, c), sibling) for j, chip in enumerate(chips)]
        for j, chip in enumerate(chips):
            copy(1 + j, (*chip, c), me).wait_recv()
            passed[j].start()
        copy(0, sibling, me).wait_recv()
        for j, chip in enumerate(chips):
            copy(4 + j, (*chip, 1 - c), me).wait_recv()
        for cp in first + passed:
            cp.wait_send()
        mine.wait()

    return pl.pallas_call(
        body, name=name, out_shape=jax.ShapeDtypeStruct((8 * m_per, n), x_shard.dtype),
        in_specs=[pl.BlockSpec(memory_space=pltpu.VMEM)], out_specs=pl.BlockSpec(memory_space=pltpu.VMEM),
        scratch_shapes=[pltpu.SemaphoreType.DMA((7,)), pltpu.SemaphoreType.DMA((7,)), pltpu.SemaphoreType.DMA],
    )(x_shard)


class _GatherChips:
    def __init__(self, shards, rows=None):
        n = len(shards)
        self.rows = [r or (0, s.shape[0]) for s, r in zip(shards, rows or [None] * n, strict=True)]
        self.arrays, self.n_in, self.n_out = list(shards), n, n
        self.out_shape = [jax.ShapeDtypeStruct((4, hi - lo) + s.shape[1:], s.dtype)
                          for s, (lo, hi) in zip(shards, self.rows, strict=True)]
        self.scratch = [pltpu.SemaphoreType.DMA((3 * n,)), pltpu.SemaphoreType.DMA((3 * n,)),
                        pltpu.SemaphoreType.DMA((n,))]

    def _copies(self, x_refs, out_refs, sems, receiving):
        send_sems, recv_sems, local_sems = sems
        x, y, c, chips = _place()
        s_me = 2 * x + y
        n = self.n_in
        src = [x_refs[a].at[pl.ds(lo, hi - lo)] for a, (lo, hi) in enumerate(self.rows)]

        def copy(a, k, s):
            return pltpu.make_async_remote_copy(
                src_ref=src[a], dst_ref=out_refs[a].at[s], send_sem=send_sems.at[3 * a + k],
                recv_sem=recv_sems.at[3 * a + k], device_id=(*chips[k], c), device_id_type=MESH)

        mine = [pltpu.make_async_copy(src[a], out_refs[a].at[s_me], local_sems.at[a]) for a in range(n)]
        sends = [copy(a, k, s_me) for a in range(n) for k in range(3)]
        if not receiving:
            return mine, sends
        return mine, sends, [copy(a, k, 2 * px + py) for a in range(n) for k, (px, py) in enumerate(chips)]

    def start(self, x_refs, out_refs, sems):
        mine, sends = self._copies(x_refs, out_refs, sems, False)
        for cp in mine + sends:
            cp.start()

    def wait(self, x_refs, out_refs, sems):
        mine, sends, recvs = self._copies(x_refs, out_refs, sems, True)
        for cp in recvs:
            cp.wait_recv()
        for cp in sends:
            cp.wait_send()
        for cp in mine:
            cp.wait()


class _GatherChipsHalved(_GatherChips):
    def __init__(self, shards):
        super().__init__(shards)
        n = self.n_in
        self.scratch = [pltpu.SemaphoreType.DMA((6 * n,)), pltpu.SemaphoreType.DMA((6 * n,)),
                        pltpu.SemaphoreType.DMA((n,))]

    def _copies(self, x_refs, out_refs, sems, receiving):
        send_sems, recv_sems, local_sems = sems
        x, y, c, chips = _place()
        s_me = 2 * x + y
        n = self.n_in

        def half(a, who):
            rows = x_refs[a].shape[0] // 2
            return pl.ds(who * rows, rows)

        def over_chips(a, k, s):
            return pltpu.make_async_remote_copy(
                src_ref=x_refs[a].at[half(a, c)], dst_ref=out_refs[a].at[s, half(a, c)],
                send_sem=send_sems.at[3 * a + k], recv_sem=recv_sems.at[3 * a + k],
                device_id=(*chips[k], c), device_id_type=MESH)

        def to_sibling(a, k, s, who):
            return pltpu.make_async_remote_copy(
                src_ref=out_refs[a].at[s, half(a, who)], dst_ref=out_refs[a].at[s, half(a, who)],
                send_sem=send_sems.at[3 * n + 3 * a + k], recv_sem=recv_sems.at[3 * n + 3 * a + k],
                device_id=(x, y, 1 - c), device_id_type=MESH)

        mine = [pltpu.make_async_copy(x_refs[a], out_refs[a].at[s_me], local_sems.at[a]) for a in range(n)]
        sends = [over_chips(a, k, s_me) for a in range(n) for k in range(3)]
        if not receiving:
            return mine, sends
        pairs = [(a, k, 2 * px + py) for a in range(n) for k, (px, py) in enumerate(chips)]
        landed = [over_chips(a, k, s) for a, k, s in pairs]
        passed_on = [to_sibling(a, k, s, c) for a, k, s in pairs]
        from_sibling = [to_sibling(a, k, s, 1 - c) for a, k, s in pairs]
        return mine, sends, landed, passed_on, from_sibling

    def wait(self, x_refs, out_refs, sems):
        mine, sends, landed, passed_on, from_sibling = self._copies(x_refs, out_refs, sems, True)
        for got, fwd in zip(landed, passed_on, strict=True):
            got.wait_recv()
            fwd.start()
        for cp in from_sibling:
            cp.wait_recv()
        for cp in sends + passed_on:
            cp.wait_send()
        for cp in mine:
            cp.wait()


class _ScatterChips:
    def __init__(self, parts, rows=None):
        n = len(parts)
        self.rows = [r or (0, p.shape[1]) for p, r in zip(parts, rows or [None] * n, strict=True)]
        self.arrays, self.n_in, self.n_out = list(parts), n, n
        self.out_shape = [jax.ShapeDtypeStruct((3, hi - lo) + p.shape[2:], p.dtype)
                          for p, (lo, hi) in zip(parts, self.rows, strict=True)]
        self.scratch = [pltpu.SemaphoreType.DMA((3 * n,)), pltpu.SemaphoreType.DMA((3 * n,))]

    def _copies(self, g_refs, out_refs, sems):
        send_sems, recv_sems = sems
        x, y, c, chips = _place()
        return [pltpu.make_async_remote_copy(
            src_ref=g_refs[a].at[2 * px + py, pl.ds(lo, hi - lo)], dst_ref=out_refs[a].at[k],
            send_sem=send_sems.at[3 * a + k], recv_sem=recv_sems.at[3 * a + k], device_id=(px, py, c),
            device_id_type=MESH)
            for a, (lo, hi) in enumerate(self.rows) for k, (px, py) in enumerate(chips)]

    def start(self, g_refs, out_refs, sems):
        for cp in self._copies(g_refs, out_refs, sems):
            cp.start()

    def wait(self, g_refs, out_refs, sems):
        sends = self._copies(g_refs, out_refs, sems)
        for cp in sends:
            cp.wait_recv()
        for cp in sends:
            cp.wait_send()


def _exchange_call(ex, *, name):
    def body(*refs):
        parts = (refs[:ex.n_in], refs[ex.n_in:ex.n_in + ex.n_out], refs[ex.n_in + ex.n_out:])
        ex.start(*parts)
        ex.wait(*parts)

    return pl.pallas_call(body, name=name, out_shape=ex.out_shape, in_specs=[ANY] * ex.n_in,
                          out_specs=[ANY] * ex.n_out, scratch_shapes=ex.scratch)(*ex.arrays)


class _SwapSibling:
    def __init__(self, vs):
        n = len(vs)
        self.arrays, self.n_in, self.n_out = list(vs), n, n
        self.out_shape = [jax.ShapeDtypeStruct(v.shape, v.dtype) for v in vs]
        self.scratch = [pltpu.SemaphoreType.DMA((n,)), pltpu.SemaphoreType.DMA((n,))]

    def _copies(self, v_refs, out_refs, sems):
        send_sems, recv_sems = sems
        x, y, c, _ = _place()
        return [pltpu.make_async_remote_copy(src_ref=v_refs[a], dst_ref=out_refs[a], send_sem=send_sems.at[a],
                                             recv_sem=recv_sems.at[a], device_id=(x, y, 1 - c), device_id_type=MESH)
                for a in range(self.n_in)]

    def start(self, v_refs, out_refs, sems):
        for cp in self._copies(v_refs, out_refs, sems):
            cp.start()

    def wait(self, v_refs, out_refs, sems):
        for cp in self._copies(v_refs, out_refs, sems):
            cp.wait()


class _Both:
    def __init__(self, first, second):
        self.parts = (first, second)
        self.arrays = first.arrays + second.arrays
        self.n_in, self.n_out = first.n_in + second.n_in, first.n_out + second.n_out
        self.out_shape = first.out_shape + second.out_shape
        self.scratch = first.scratch + second.scratch

    def _split(self, in_refs, out_refs, sems):
        a, b = self.parts
        return ((a, in_refs[:a.n_in], out_refs[:a.n_out], sems[:len(a.scratch)]),
                (b, in_refs[a.n_in:], out_refs[a.n_out:], sems[len(a.scratch):]))

    def start(self, in_refs, out_refs, sems):
        for ex, *refs in self._split(in_refs, out_refs, sems):
            ex.start(*refs)

    def wait(self, in_refs, out_refs, sems):
        for ex, *refs in self._split(in_refs, out_refs, sems):
            ex.wait(*refs)


def _sum_parts(own, others, *, name):
    R, C = own.shape
    tm = _pick(R, (256, 128, 64))

    def body(own_ref, o0_ref, o1_ref, o2_ref, out_ref):
        tot = own_ref[...].astype(F32)
        for ref in (o0_ref, o1_ref, o2_ref):
            tot = tot + ref[...].astype(F32)
        out_ref[...] = tot

    part = lambda k: pl.BlockSpec((None, tm, C), lambda i: (k, i, 0))
    return pl.pallas_call(
        body, name=name, grid=(R // tm,),
        in_specs=[pl.BlockSpec((tm, C), lambda i: (i, 0)), part(0), part(1), part(2)],
        out_specs=pl.BlockSpec((tm, C), lambda i: (i, 0)), out_shape=jax.ShapeDtypeStruct((R, C), F32),
        compiler_params=_cparams(("arbitrary",)),
    )(own, others, others, others)


def _adam_math(w_, m_, v_, g):
    m2 = ADAM_B1 * m_ + (1.0 - ADAM_B1) * g
    v2 = ADAM_B2 * v_ + (1.0 - ADAM_B2) * jnp.square(g)
    m_hat = m2 / (1.0 - ADAM_B1 ** ADAM_STEP)
    v_hat = v2 / (1.0 - ADAM_B2 ** ADAM_STEP)
    delta = -ADAM_LR * (m_hat / (jnp.sqrt(v_hat) + ADAM_EPS) + ADAM_WD * w_)
    return delta, m2, v2


SMALL_SLOTS = 24
SMALL_COLS = 2 * D


def _small_rows(widths):
    firsts, row = [], 0
    for n_i in widths:
        firsts.append(row)
        row += -(-n_i // SMALL_COLS)
    assert row <= SMALL_SLOTS
    return firsts


def _pack_small(grads, *, name):
    n = len(grads)
    firsts = _small_rows([g.shape[1] for g in grads])

    def body(*refs):
        out_ref = refs[n]
        out_ref[...] = jnp.zeros_like(out_ref)
        for first, ref in zip(firsts, refs[:n], strict=True):
            for lo in range(0, ref.shape[1], SMALL_COLS):
                width = min(SMALL_COLS, ref.shape[1] - lo)
                row = first + lo // SMALL_COLS
                out_ref[row:row + 1, 0:width] = ref[:, lo:lo + width]

    return pl.pallas_call(body, name=name, out_shape=jax.ShapeDtypeStruct((SMALL_SLOTS, SMALL_COLS), F32))(*grads)


def _adamw_small(ws, ms, vs, gathered, *, name):
    n = len(ws)
    firsts = _small_rows([w.shape[1] for w in ws] + [128])

    def total(g_ref, first, nc):
        pieces = []
        for lo in range(0, nc, SMALL_COLS):
            width, row = min(SMALL_COLS, nc - lo), first + lo // SMALL_COLS
            g = g_ref[row:row + 1, 0:width]
            for d in range(1, 8):
                g = g + g_ref[d * SMALL_SLOTS + row:d * SMALL_SLOTS + row + 1, 0:width]
            pieces.append(g)
        return pieces[0] if len(pieces) == 1 else jnp.concatenate(pieces, axis=1)

    def body(*refs):
        w_refs, m_refs, v_refs, g_ref = refs[:n], refs[n:2 * n], refs[2 * n:3 * n], refs[3 * n]
        outs = refs[3 * n + 1:]
        for i in range(n):
            g = total(g_ref, firsts[i], w_refs[i].shape[1])
            delta, m2, v2 = _adam_math(w_refs[i][...], m_refs[i][...], v_refs[i][...], g)
            for k, val in enumerate((g, delta, m2, v2)):
                outs[k * n + i][...] = val
        outs[4 * n][...] = total(g_ref, firsts[n], 128)

    shapes = [jax.ShapeDtypeStruct(w.shape, F32) for w in ws]
    res = pl.pallas_call(body, name=name, out_shape=shapes * 4 + [jax.ShapeDtypeStruct((1, 128), F32)],
                         compiler_params=pltpu.CompilerParams(vmem_limit_bytes=VMEM_LIMIT))(*ws, *ms, *vs, gathered)
    return [res[k * n:(k + 1) * n] for k in range(4)], res[4 * n]


def _adamw(w, m, v, gparts, *, tm, name, hosted=None):
    def fn(i, n, w_, m_, v_, *gs):
        g = gs[0]
        for p in gs[1:]:
            g = g + p
        return (g,) + _adam_math(w_, m_, v_, g)
    nc = w.shape[1]
    return _rowwise(fn, [w, m, v] + list(gparts), [], [(nc, F32, 'row')] * 4, tm=tm, name=name, hosted=hosted)


WEIGHTS = ['ada_w', 'ada_b', 'norm1_gain', 'norm2_gain', 'w_in', 'tshift_mu', 'decay_w0', 'decay_up', 'iclr_a0',
           'iclr_up', 'gate_up', 'k_k', 'k_a', 'r_k', 'lnx_gain', 'lnx_bias', 'q_norm_gain', 'k_norm_gain', 'attn_sinks',
           'branch_gate_b', 'w_branch_a', 'w_branch_b', 'w_out', 'ffn_w1', 'ffn_w3', 'ffn_w2']
SHARDED = [('w_in', 1), ('decay_up', 1), ('iclr_up', 1), ('gate_up', 1), ('w_branch_a', 1), ('w_branch_b', 1),
           ('w_out', 0), ('ffn_w1', 1), ('ffn_w3', 1), ('ffn_w2', 0)]
SMALL = ['ada_b', 'norm1_gain', 'norm2_gain', 'tshift_mu', 'decay_w0', 'iclr_a0', 'k_k', 'k_a', 'r_k', 'lnx_gain',
         'lnx_bias', 'q_norm_gain', 'k_norm_gain', 'attn_sinks', 'branch_gate_b']


def kernel(x, c, positions, ada_w, ada_b, norm1_gain, norm2_gain, w_in, tshift_mu, decay_w0, decay_up, iclr_a0, iclr_up, gate_up, k_k, k_a, r_k, lnx_gain, lnx_bias, q_norm_gain, k_norm_gain, attn_sinks, branch_gate_b, w_branch_a, w_branch_b, w_out, ffn_w1, ffn_w3, ffn_w2, loss_target, m_ada_w, m_ada_b, m_norm1_gain, m_norm2_gain, m_w_in, m_tshift_mu, m_decay_w0, m_decay_up, m_iclr_a0, m_iclr_up, m_gate_up, m_k_k, m_k_a, m_r_k, m_lnx_gain, m_lnx_bias, m_q_norm_gain, m_k_norm_gain, m_attn_sinks, m_branch_gate_b, m_w_branch_a, m_w_branch_b, m_w_out, m_ffn_w1, m_ffn_w3, m_ffn_w2, v_ada_w, v_ada_b, v_norm1_gain, v_norm2_gain, v_w_in, v_tshift_mu, v_decay_w0, v_decay_up, v_iclr_a0, v_iclr_up, v_gate_up, v_k_k, v_k_a, v_r_k, v_lnx_gain, v_lnx_bias, v_q_norm_gain, v_k_norm_gain, v_attn_sinks, v_branch_gate_b, v_w_branch_a, v_w_branch_b, v_w_out, v_ffn_w1, v_ffn_w3, v_ffn_w2):
    a = dict(locals())
    W = {n: a[n] for n in WEIGHTS}
    M = {n: a['m_' + n] for n in WEIGHTS}
    V = {n: a['v_' + n] for n in WEIGHTS}
    xi, yi, ci = lax.axis_index("x"), lax.axis_index("y"), lax.axis_index("c")
    me = 4 * xi + 2 * yi + ci
    shard = 2 * xi + yi
    mat = lambda t: t.reshape(t.shape[-2], t.shape[-1])
    sharded = [n for n, _ in SHARDED]

    ax = dict(SHARDED)
    late = LATE
    early = [n for n in sharded if n not in late]
    shards = {n: mat(W[n]).astype(MXU) for n in sharded}
    gathered = _exchange_call(_GatherChipsHalved([shards[n] for n in early]), name="gather_weights")
    full = {n: _full_weight(g, ax[n]) for n, g in zip(early, gathered, strict=True)}

    c_all = _all_gather8(jnp.broadcast_to(c, (8, D)), name="gather_c")[0::8]
    pad_rows = lambda t: jnp.concatenate([t, jnp.zeros((BLK - 8, t.shape[1]), t.dtype)])
    c_all = pad_rows(c_all.astype(MXU))
    ada_cols = _mm_nn(c_all, mat(ada_w).astype(MXU), name="f_ada")[:8]
    ada_all = _all_gather8(ada_cols, name="gather_ada").reshape(2, 2, 2, 8, 6 * D // 4)
    ada_mine = lax.dynamic_index_in_dim(ada_all[:, :, 0], me, axis=2, keepdims=False)
    ada = ada_mine.reshape(1, 6 * D) + mat(ada_b)

    zero = jnp.zeros((64, RW), MXU)
    lora = jnp.concatenate([jnp.concatenate([full['decay_up'], zero], axis=1),
                            jnp.concatenate([zero, full['iclr_up']], axis=1)], axis=0)
    s = {n: W[n].reshape(1, -1) for n in SMALL if n != 'ada_b'}
    s['lora_up'] = lora.astype(F32)
    s['gate_up'] = full['gate_up'].astype(F32)
    tab = _rope_table(positions.reshape(-1))
    loss, dx, d_ada, gw, gs, from_chips = _local_step(x[0], loss_target[0], ada, tab, dict(w_in=full['w_in']), s,
                                                      shards={n: shards[n] for n in late})

    gs['ada_b'] = d_ada
    gsmall = _pack_small([gs[n] for n in SMALL] + [loss], name="pack_small_grads")
    gsmall_all = _all_gather8(gsmall, name="gather_small_grads")
    row = lambda src: [src[n].reshape(1, -1) for n in SMALL]
    sm_out, loss = _adamw_small(row(W), row(M), row(V), gsmall_all, name="adamw_small")
    sm_out = [{n: o.reshape(W[n].shape) for n, o in zip(SMALL, outs_k, strict=True)} for outs_k in sm_out]
    loss = loss[0, 0]

    ada_rows = 6 * D // SMALL_COLS
    d_ada_all = gsmall_all.reshape(8, SMALL_SLOTS, SMALL_COLS)[:, :ada_rows].reshape(8, 6 * D)
    d_ada_cols = lax.dynamic_slice_in_dim(d_ada_all, shard * (6 * D // 4), 6 * D // 4, axis=1)
    g_ada_w = _mm_tn(c_all, pad_rows(d_ada_cols.astype(MXU)), name="b_ada")

    rest = [n for n in sharded if n != 'w_in']
    parts = {n: _sum_parts(lax.dynamic_index_in_dim(gw[n], shard, axis=0, keepdims=False), from_chips[n],
                           name="sum_" + n) for n in rest}
    tail = _Both(_ScatterChips([gw['w_in']], rows=[(3 * D // 4, D)]), _SwapSibling([parts[n] for n in rest]))
    res = _adamw(mat(ada_w), mat(m_ada_w), mat(v_ada_w), [g_ada_w], tm=256, name="adamw_ada", hosted=tail)
    ada_out, last_quarter, others = res[:4], res[4], dict(zip(rest, res[5:], strict=True))
    parts['w_in'] = _sum_parts(lax.dynamic_index_in_dim(gw['w_in'], shard, axis=0, keepdims=False),
                               jnp.concatenate(from_chips['w_in'] + [last_quarter], axis=1), name="sum_w_in")
    others['w_in'] = _exchange_call(_SwapSibling([parts['w_in']]), name="swap_w_in")[0]
    sh_out = {}
    for n in sharded:
        part, other = parts[n], others[n]
        sh_out[n] = _adamw(mat(W[n]), mat(M[n]), mat(V[n]), [part, other], tm=_pick(part.shape[0], (256, 128, 64)),
                           name="adamw_" + n)

    def leaf(k, n):
        if n == 'ada_w':
            return ada_out[k].reshape(W[n].shape)
        if n in sharded:
            return sh_out[n][k].reshape(W[n].shape)
        return sm_out[k][n]
    outs = [leaf(k, n) for k in range(4) for n in WEIGHTS]
    return (loss, dx[None], *outs)
```

```python
import functools
import math

import jax
import jax.numpy as jnp
from jax import lax
from jax.experimental import pallas as pl
from jax.experimental.pallas import tpu as pltpu

F32 = jnp.float32
BF16 = jnp.bfloat16
MXU = BF16
HI = lax.Precision.HIGHEST

D = 1024
HD = 64
NH = 8
RW = NH * HD
SHIFT_W = 3 * RW + 64 + 64 + 128
QKV_W = RW + 2 * 128
GATE_W = 2 * D
IN_W = SHIFT_W + QKV_W + GATE_W
DFF = 2816
BLK = 128
CHUNK = 128
RMS_EPS = 1e-6
GN_EPS = 64e-5
NEG_INF = -1e30
ADAM_LR, ADAM_B1, ADAM_B2, ADAM_EPS, ADAM_WD, ADAM_STEP = 0.001, 0.9, 0.999, 1e-08, 0.01, 10
VMEM_LIMIT = 56 * 1024 * 1024
MESH = pl.DeviceIdType.MESH


def _cparams(sem=None):
    return pltpu.CompilerParams(dimension_semantics=sem, vmem_limit_bytes=VMEM_LIMIT)


def _full_spec(a):
    nd = a.ndim
    return pl.BlockSpec(a.shape, lambda *_: (0,) * nd)


def _rowwise(fn, rows, consts, outs, *, tm, name, halo=(), hosted=None):
    rows = [(a + (0,))[:3] if isinstance(a, tuple) else (a, a.shape[1], 0) for a in rows]
    T = rows[0][0].shape[0]
    assert T % tm == 0 and tm % 8 == 0
    n_tiles = T // tm
    n_in = len(rows) + len(halo) + len(consts)
    in_specs = [pl.BlockSpec((tm, nc), lambda i, j=j: (i, j)) for _, nc, j in rows]
    args = [a for a, _, _ in rows]
    for a, nc, kind in halo:
        if kind == 'prev':
            in_specs.append(pl.BlockSpec((8, nc), lambda i: (jnp.maximum(i * (tm // 8) - 1, 0), 0)))
        else:
            in_specs.append(pl.BlockSpec((8, nc), lambda i: (jnp.minimum((i + 1) * (tm // 8), T // 8 - 1), 0)))
        args.append(a)
    in_specs += [_full_spec(a) for a in consts]
    args += list(consts)
    out_shape, out_specs = [], []
    for ncols, dtype, kind in outs:
        if kind == 'row':
            out_shape.append(jax.ShapeDtypeStruct((T, ncols), dtype))
            out_specs.append(pl.BlockSpec((tm, ncols), lambda i: (i, 0)))
        else:
            out_shape.append(jax.ShapeDtypeStruct((kind, ncols), dtype))
            out_specs.append(pl.BlockSpec((kind, ncols), lambda i: (0, 0)))

    def body(*refs):
        i = pl.program_id(0)
        vals = [r[...] for r in refs[:n_in]]
        res = fn(i, n_tiles, *vals)
        for (ncols, dtype, kind), o_ref, val in zip(outs, refs[n_in:], res, strict=True):
            if kind == 'row':
                o_ref[...] = val.astype(dtype)
            else:
                @pl.when(i == 0)
                def _():
                    o_ref[...] = jnp.zeros_like(o_ref)
                o_ref[...] += val.astype(dtype)

    h_in, h_in_specs, h_out_specs, h_out_shape, h_scratch = _hosted_args(hosted)
    res = pl.pallas_call(
        _hosting(body, hosted, n_in, len(outs), 0, n_tiles), name=name, grid=(n_tiles,),
        in_specs=in_specs + h_in_specs, out_specs=out_specs + h_out_specs, out_shape=out_shape + h_out_shape,
        scratch_shapes=h_scratch, compiler_params=_cparams(("arbitrary",)),
    )(*args, *h_in)
    return res


def _pick(n, cands):
    for c in cands:
        if n % c == 0:
            return c
    return n


MM_ROWS = (1024, 512, 256, 128)
MM_COLS = (1536, 1408, 1024, 896, 768, 512, 256, 128)
MM_WIDE = 3000


def _mm_nn(a, w, *, name, out_dtype=F32, hosted=None):
    T, K = a.shape
    N = w.shape[1]
    tm = _pick(T, MM_ROWS)
    tn = _pick(N, MM_COLS)
    grid = (N // tn, T // tm)
    h_in, h_in_specs, h_out_specs, h_out_shape, h_scratch = _hosted_args(hosted)

    def body(a_ref, w_ref, o_ref):
        o_ref[...] = jnp.dot(a_ref[...], w_ref[...], preferred_element_type=F32).astype(out_dtype)

    res = pl.pallas_call(
        _hosting(body, hosted, 2, 1, 0, grid), name=name, grid=grid,
        in_specs=[pl.BlockSpec((tm, K), lambda j, i: (i, 0)), pl.BlockSpec((K, tn), lambda j, i: (0, j))] + h_in_specs,
        out_specs=[pl.BlockSpec((tm, tn), lambda j, i: (i, j))] + h_out_specs,
        out_shape=[jax.ShapeDtypeStruct((T, N), out_dtype)] + h_out_shape, scratch_shapes=h_scratch,
        compiler_params=_cparams(("arbitrary", "arbitrary")),
    )(a, w, *h_in)
    return res if hosted else res[0]


def _mm_nt(dy, w, *, name, out_dtype=F32, hosted=None):
    T, N = dy.shape
    K = w.shape[0]
    tm = _pick(T, MM_ROWS if N <= MM_WIDE else MM_ROWS[1:])
    tk = _pick(K, MM_COLS[1:])
    grid = (K // tk, T // tm)
    h_in, h_in_specs, h_out_specs, h_out_shape, h_scratch = _hosted_args(hosted)

    def body(dy_ref, w_ref, o_ref):
        o_ref[...] = lax.dot_general(dy_ref[...], w_ref[...], (((1,), (1,)), ((), ())),
                                     preferred_element_type=F32).astype(out_dtype)

    res = pl.pallas_call(
        _hosting(body, hosted, 2, 1, 0, grid), name=name, grid=grid,
        in_specs=[pl.BlockSpec((tm, N), lambda j, i: (i, 0)), pl.BlockSpec((tk, N), lambda j, i: (j, 0))] + h_in_specs,
        out_specs=[pl.BlockSpec((tm, tk), lambda j, i: (i, j))] + h_out_specs,
        out_shape=[jax.ShapeDtypeStruct((T, K), out_dtype)] + h_out_shape, scratch_shapes=h_scratch,
        compiler_params=_cparams(("arbitrary", "arbitrary")),
    )(dy, w, *h_in)
    return res if hosted else res[0]


def _mm_tn(a, dy, *, name, out_dtype=F32, col_shards=None):
    T, K = a.shape
    N = dy.shape[1]
    tm = _pick(T, MM_ROWS)
    tn = N // col_shards if col_shards else _pick(N, MM_COLS[1:])
    n_t = T // tm

    def body(a_ref, dy_ref, o_ref, acc_ref):
        i = pl.program_id(1)

        @pl.when(i == 0)
        def _():
            acc_ref[...] = jnp.zeros_like(acc_ref)

        acc_ref[...] += lax.dot_general(a_ref[...], dy_ref[...], (((0,), (0,)), ((), ())), preferred_element_type=F32)

        @pl.when(i == n_t - 1)
        def _():
            o_ref[...] = acc_ref[...].astype(out_dtype)

    if col_shards:
        out_specs = pl.BlockSpec((None, K, tn), lambda j, i: (j, 0, 0))
        out_shape = jax.ShapeDtypeStruct((col_shards, K, tn), out_dtype)
    else:
        out_specs = pl.BlockSpec((K, tn), lambda j, i: (0, j))
        out_shape = jax.ShapeDtypeStruct((K, N), out_dtype)
    return pl.pallas_call(
        body, name=name, grid=(N // tn, n_t),
        in_specs=[pl.BlockSpec((tm, K), lambda j, i: (i, 0)), pl.BlockSpec((tm, tn), lambda j, i: (i, j))],
        out_specs=out_specs, out_shape=out_shape, scratch_shapes=[pltpu.VMEM((K, tn), F32)],
        compiler_params=_cparams(("arbitrary", "arbitrary")),
    )(a, dy)


def _mm_then(products, fn, rows, consts, outs, *, tm, name, lhs_fn=None, hosted=None):
    products = [(p + (None,))[:4] for p in products]
    T = (rows[0] if lhs_fn else products[0][0]).shape[-2]
    n_tiles = T // tm
    in_specs, args = [], []
    for a, w, _, j in products:
        if a is not None and j is None:
            in_specs.append(pl.BlockSpec((tm, a.shape[1]), lambda i: (i, 0)))
            args.append(a)
        elif a is not None:
            in_specs.append(pl.BlockSpec((None, tm, a.shape[2]), lambda i, j=j: (j, i, 0)))
            args.append(a)
        in_specs.append(_full_spec(w) if j is None else
                        pl.BlockSpec((None,) + w.shape[1:], lambda i, j=j: (j, 0, 0)))
        args.append(w)
    n_w = len(args)
    in_specs += [pl.BlockSpec((tm, a.shape[1]), lambda i: (i, 0)) for a in rows] + [_full_spec(c_) for c_ in consts]
    args += list(rows) + list(consts)
    n_in = len(args)
    out_shape, out_specs = [], []
    for ncols, dtype, kind in outs:
        if kind == 'row':
            out_shape.append(jax.ShapeDtypeStruct((T, ncols), dtype))
            out_specs.append(pl.BlockSpec((tm, ncols), lambda i: (i, 0)))
        else:
            out_shape.append(jax.ShapeDtypeStruct((kind, ncols), dtype))
            out_specs.append(pl.BlockSpec((kind, ncols), lambda i: (0, 0)))

    def body(*refs):
        i = pl.program_id(0)
        tiles = [r[...] for r in refs[n_w:n_in]]
        made = []
        if lhs_fn:
            made = lhs_fn(*tiles)
            made = list(made) if isinstance(made, tuple) else [made]
            made[0] = made[0].astype(MXU)
        y, pos = None, 0
        for a, _, form, _ in products:
            if a is None:
                lhs = made[0]
            else:
                lhs, pos = refs[pos][...], pos + 1
            dims = (((1,), (0,)), ((), ())) if form == 'nn' else (((1,), (1,)), ((), ()))
            t = lax.dot_general(lhs, refs[pos][...], dims, preferred_element_type=F32)
            pos += 1
            y = t if y is None else y + t
        res = fn(i, n_tiles, y, *made, *tiles)
        for (ncols, dtype, kind), o_ref, val in zip(outs, refs[n_in:], res, strict=True):
            if kind == 'row':
                o_ref[...] = val.astype(dtype)
            else:
                @pl.when(i == 0)
                def _():
                    o_ref[...] = jnp.zeros_like(o_ref)
                o_ref[...] += val.astype(dtype)

    h_in, h_in_specs, h_out_specs, h_out_shape, h_scratch = _hosted_args(hosted)
    return pl.pallas_call(
        _hosting(body, hosted, n_in, len(outs), 0, n_tiles), name=name, grid=(n_tiles,),
        in_specs=in_specs + h_in_specs, out_specs=out_specs + h_out_specs, out_shape=out_shape + h_out_shape,
        scratch_shapes=h_scratch, compiler_params=_cparams(("arbitrary",)))(*args, *h_in)


def _seg_ones(n):
    r = lax.broadcasted_iota(jnp.int32, (n, n), 0) // HD
    c = lax.broadcasted_iota(jnp.int32, (n, n), 1) // HD
    return (r == c).astype(F32)


def _segsum_raw(x):
    ones = _seg_ones(x.shape[1])
    if MXU == F32:
        return jnp.dot(x, ones, precision=HI, preferred_element_type=F32)
    hi = x.astype(MXU)
    lo = (x - hi.astype(F32)).astype(MXU)
    ones = ones.astype(MXU)
    return jnp.dot(hi, ones, preferred_element_type=F32) + jnp.dot(lo, ones, preferred_element_type=F32)


@jax.custom_vjp
def _segsum(x):
    return _segsum_raw(x)


def _segsum_fwd(x):
    return _segsum_raw(x), None


def _segsum_bwd(_, g):
    return (_segsum_raw(g),)


_segsum.defvjp(_segsum_fwd, _segsum_bwd)


def _mxu(x):
    return x.astype(MXU)


@jax.custom_vjp
def _bdot(a, b):
    return jnp.dot(_mxu(a), _mxu(b), preferred_element_type=F32)


def _bdot_fwd(a, b):
    return _bdot(a, b), (a, b)


def _bdot_bwd(res, g):
    a, b = res
    da = lax.dot_general(_mxu(g), _mxu(b), (((1,), (1,)), ((), ())), preferred_element_type=F32)
    db = lax.dot_general(_mxu(a), _mxu(g), (((0,), (0,)), ((), ())), preferred_element_type=F32)
    return da.astype(a.dtype), db.astype(b.dtype)


_bdot.defvjp(_bdot_fwd, _bdot_bwd)


def _sigmoid(x):
    return 1.0 / (1.0 + jnp.exp(-x))


def _softplus(x):
    return jnp.maximum(x, 0.0) + jnp.log(1.0 + jnp.exp(jnp.minimum(x, -x)))


def _norm_mod(x, gain, scale, shift):
    inv = lax.rsqrt(jnp.mean(x * x, axis=-1, keepdims=True) + RMS_EPS)
    return (x * inv) * gain * (1.0 + scale) + shift


def _prep(mixed, decay_w0, lora_up, iclr_a0, gate_up, k_k, k_a):
    r = mixed[:, 0:RW]
    k = mixed[:, RW:2 * RW]
    v = mixed[:, 2 * RW:3 * RW]
    z = mixed[:, 3 * RW:3 * RW + 128]
    xg = mixed[:, 3 * RW + 128:]
    lane = lax.broadcasted_iota(jnp.int32, z.shape, 1)
    tz = jnp.where(lane < 64, jnp.tanh(z), z)
    lo = _bdot(tz, lora_up)
    w_log = -_softplus(-(decay_w0 + lo[:, :RW])) - 0.5
    lw = -jnp.exp(w_log)
    a_ic = _sigmoid(iclr_a0 + lo[:, RW:])
    g = _bdot(_sigmoid(xg), gate_up)
    kk = k * k_k
    kk = kk / jnp.maximum(jnp.sqrt(_segsum(kk * kk)), 1e-12)
    k_mod = k * (1.0 + (a_ic - 1.0) * k_a)
    return jnp.concatenate([r, lw, k_mod, v, -kk, kk * a_ic, g], axis=1)


def _post(y, r, k, v, g, lnx_gain, lnx_bias, r_k):
    mu = _segsum(y) * (1.0 / HD)
    yc = y - mu
    var = _segsum(yc * yc) * (1.0 / HD)
    yn = yc * lax.rsqrt(var + GN_EPS) * lnx_gain + lnx_bias
    bonus = _segsum(r * k * r_k) * v
    return (yn + bonus) * g


def _merge(pg, ma, mb, bias):
    gates = _sigmoid(pg + bias)
    return gates[:, :D] * ma + gates[:, D:] * mb


def _swiglu(u, v):
    return u * _sigmoid(u) * v


def _ffn_in(h, w1, w3, *, name):
    T, K = h.shape
    ns, _, Fs = w1.shape
    tm = _pick(T, MM_ROWS)

    def body(h_ref, w1_ref, w3_ref, u_ref, v_ref, a_ref):
        u = jnp.dot(h_ref[...], w1_ref[...], preferred_element_type=F32).astype(MXU)
        v = jnp.dot(h_ref[...], w3_ref[...], preferred_element_type=F32).astype(MXU)
        u_ref[...] = u
        v_ref[...] = v
        a_ref[...] = _swiglu(u.astype(F32), v.astype(F32)).astype(MXU)

    wspec = pl.BlockSpec((None, K, Fs), lambda j, i: (j, 0, 0))
    ospec = pl.BlockSpec((None, tm, Fs), lambda j, i: (j, i, 0))
    return pl.pallas_call(
        body, name=name, grid=(ns, T // tm),
        in_specs=[pl.BlockSpec((tm, K), lambda j, i: (i, 0)), wspec, wspec],
        out_specs=[ospec] * 3, out_shape=[jax.ShapeDtypeStruct((ns, T, Fs), MXU)] * 3,
        compiler_params=_cparams(("arbitrary", "arbitrary")),
    )(h, w1, w3)


def _ffn_act_bwd(dff, w2, u, v, *, name):
    T, N = dff.shape
    ns, Fs, _ = w2.shape
    tm = _pick(T, MM_ROWS)

    def body(dy_ref, w_ref, u_ref, v_ref, du_ref, dv_ref):
        dact = lax.dot_general(dy_ref[...], w_ref[...], (((1,), (1,)), ((), ())), preferred_element_type=F32)
        _, vjp = jax.vjp(_swiglu, u_ref[...].astype(F32), v_ref[...].astype(F32))
        du, dv = vjp(dact)
        du_ref[...] = du.astype(MXU)
        dv_ref[...] = dv.astype(MXU)

    tile = pl.BlockSpec((None, tm, Fs), lambda j, i: (j, i, 0))
    return pl.pallas_call(
        body, name=name, grid=(ns, T // tm),
        in_specs=[pl.BlockSpec((tm, N), lambda j, i: (i, 0)), pl.BlockSpec((None, Fs, N), lambda j, i: (j, 0, 0)),
                  tile, tile],
        out_specs=[tile, tile], out_shape=[jax.ShapeDtypeStruct((ns, T, Fs), MXU)] * 2,
        compiler_params=_cparams(("arbitrary", "arbitrary")),
    )(dff, w2, u, v)


def _mm_tn_blocks(a, dy, *, name, out_dtype):
    a3, d3 = a.ndim == 3, dy.ndim == 3
    ns = a.shape[0] if a3 else dy.shape[0]
    T, K, N = a.shape[-2], a.shape[-1], dy.shape[-1]
    tm = _pick(T, MM_ROWS)
    n_t = T // tm

    def body(a_ref, dy_ref, o_ref, acc_ref):
        i = pl.program_id(1)

        @pl.when(i == 0)
        def _():
            acc_ref[...] = jnp.zeros_like(acc_ref)

        acc_ref[...] += lax.dot_general(a_ref[...], dy_ref[...], (((0,), (0,)), ((), ())), preferred_element_type=F32)

        @pl.when(i == n_t - 1)
        def _():
            o_ref[...] = acc_ref[...].astype(out_dtype)

    spec = lambda is3, n: (pl.BlockSpec((None, tm, n), lambda j, i: (j, i, 0)) if is3
                           else pl.BlockSpec((tm, n), lambda j, i: (i, 0)))
    return pl.pallas_call(
        body, name=name, grid=(ns, n_t), in_specs=[spec(a3, K), spec(d3, N)],
        out_specs=pl.BlockSpec((None, K, N), lambda j, i: (j, 0, 0)),
        out_shape=jax.ShapeDtypeStruct((ns, K, N), out_dtype), scratch_shapes=[pltpu.VMEM((K, N), F32)],
        compiler_params=_cparams(("arbitrary", "arbitrary")),
    )(a, dy)


@functools.partial(jax.custom_vjp, nondiff_argnums=(1,))
def _lane_roll(x, s):
    return pltpu.roll(x, s, 1)


def _lane_roll_fwd(x, s):
    return pltpu.roll(x, s, 1), None


def _lane_roll_bwd(s, _, g):
    n = g.shape[1]
    return (pltpu.roll(g, (n - s) % n, 1),)


_lane_roll.defvjp(_lane_roll_fwd, _lane_roll_bwd)


def _rope(x, cos, sin_lo, sin_hi):
    n = x.shape[1]
    return x * cos + _lane_roll(x, n - 8) * sin_lo + _lane_roll(x, 8) * sin_hi


def _head_rms(x, gain):
    return x * lax.rsqrt(_segsum(x * x) * (1.0 / HD) + RMS_EPS) * gain


def _attn_blocks(qkv_c, qkv_p, tab_c, tab_p, qg, kg, sinks, first):
    nb = qkv_c.shape[0] // BLK
    G = 4

    def tabs(tab, n):
        return [jnp.tile(tab[:, j * 128:(j + 1) * 128], (1, n // 128)) for j in range(3)]

    qg = jnp.concatenate([qg] * NH, axis=1)
    kg = jnp.concatenate([kg] * 2, axis=1)
    q = _rope(_head_rms(qkv_c[:, :RW], qg), *tabs(tab_c, RW))
    k_in = jnp.concatenate([qkv_p[:, RW:RW + 128], qkv_c[:, RW:RW + 128]], axis=0)
    k = _rope(_head_rms(k_in, kg), *tabs(jnp.concatenate([tab_p, tab_c], axis=0), 128))
    v = jnp.concatenate([qkv_p[:, RW + 128:], qkv_c[:, RW + 128:]], axis=0)

    pile = lambda xs: jnp.concatenate([x_[None] for x_ in xs], axis=0)

    def bands(t):
        return pile([t[b * BLK:(b + 2) * BLK, kvh * HD:(kvh + 1) * HD] for kvh in range(2) for b in range(nb)])

    qs = pile([jnp.concatenate([q[b * BLK:(b + 1) * BLK, (G * kvh + g) * HD:(G * kvh + g + 1) * HD]
                                for g in range(G)], axis=0) for kvh in range(2) for b in range(nb)])
    s = _bmm(qs, bands(k), 2, 2, 1) * (HD ** -0.5)
    qi = lax.broadcasted_iota(jnp.int32, (G * BLK, 2 * BLK), 0) % BLK
    kj = lax.broadcasted_iota(jnp.int32, (G * BLK, 2 * BLK), 1)
    dist = qi + BLK - kj
    in_band = (dist >= 0) & (dist < BLK)
    pair = lax.broadcasted_iota(jnp.int32, (2 * nb, 1, 1), 0)
    no_prev = (pair % nb == 0) & first
    valid = in_band[None] & (jnp.logical_not(no_prev) | (kj >= BLK)[None])
    s = jnp.where(valid, s, NEG_INF)
    row_g = lax.broadcasted_iota(jnp.int32, (G * BLK, 1), 0) // BLK
    sink = []
    for kvh in range(2):
        col = jnp.zeros((G * BLK, 1), F32)
        for g in range(G):
            col = jnp.where(row_g == g, sinks[:, G * kvh + g:G * kvh + g + 1], col)
        sink += [col] * nb
    sink = pile(sink)
    m = lax.stop_gradient(jnp.maximum(jnp.max(s, axis=-1, keepdims=True), sink))
    e = jnp.exp(s - m)
    p = e * (1.0 / (jnp.sum(e, axis=-1, keepdims=True) + jnp.exp(sink - m)))
    o = _bmm(p, bands(v), 2, 1, 1)
    return jnp.concatenate([jnp.concatenate([o[kvh * nb + b, g * BLK:(g + 1) * BLK] for kvh in range(2)
                                             for g in range(G)], axis=1) for b in range(nb)], axis=0)


def _heads(x):
    return jnp.stack([x[:, h * HD:(h + 1) * HD] for h in range(NH)], axis=0)


def _unheads(x):
    return jnp.concatenate([x[h] for h in range(NH)], axis=1)


def _split(x, n):
    parts, rest = [], x
    for _ in range(n):
        p = rest.astype(MXU)
        parts.append(p)
        rest = rest - p.astype(F32)
    return parts


def _bdot_batched(a, b, ca, cb):
    return lax.dot_general(a, b, (((ca,), (cb,)), ((0,), (0,))), preferred_element_type=F32)


def _bmm_passes(a, b, ca, cb, passes):
    if MXU == F32:
        return lax.dot_general(a, b, (((ca,), (cb,)), ((0,), (0,))), precision=HI, preferred_element_type=F32)
    if passes == 1:
        return _bdot_batched(a.astype(MXU), b.astype(MXU), ca, cb)
    (a0, a1), (b0, b1) = _split(a, 2), _split(b, 2)
    return _bdot_batched(a0, b0, ca, cb) + (_bdot_batched(a0, b1, ca, cb) + _bdot_batched(a1, b0, ca, cb))


@functools.partial(jax.custom_vjp, nondiff_argnums=(2, 3, 4))
def _bmm(a, b, ca, cb, passes=1):
    return _bmm_passes(a, b, ca, cb, passes)


def _bmm_fwd(a, b, ca, cb, passes):
    return _bmm_passes(a, b, ca, cb, passes), (a, b)


def _bmm_bwd(ca, cb, passes, res, g):
    a, b = res
    if (ca, cb) == (2, 1):
        return _bmm_passes(g, b, 2, 2, passes), _bmm_passes(a, g, 1, 1, passes)
    if (ca, cb) == (2, 2):
        return _bmm_passes(g, b, 2, 1, passes), _bmm_passes(g, a, 1, 1, passes)
    return _bmm_passes(b, g, 2, 2, passes), _bmm_passes(a, g, 2, 1, passes)


_bmm.defvjp(_bmm_fwd, _bmm_bwd)


def _tri_dot(x, transpose):
    C = x.shape[1]
    ri = lax.broadcasted_iota(jnp.int32, (C, C), 0)
    ci = lax.broadcasted_iota(jnp.int32, (C, C), 1)
    tri = jnp.broadcast_to(((ri <= ci) if transpose else (ri >= ci)).astype(MXU), (x.shape[0], C, C))
    if MXU == F32:
        return lax.dot_general(tri, x, (((2,), (1,)), ((0,), (0,))), precision=HI, preferred_element_type=F32)
    p0, p1, p2 = _split(x, 3)
    return _bdot_batched(tri, p0, 2, 1) + (_bdot_batched(tri, p1, 2, 1) + _bdot_batched(tri, p2, 2, 1))


@jax.custom_vjp
def _cumsum_rows(x):
    return _tri_dot(x, False)


def _cumsum_rows_fwd(x):
    return _tri_dot(x, False), None


def _cumsum_rows_bwd(_, g):
    return (_tri_dot(g, True),)


_cumsum_rows.defvjp(_cumsum_rows_fwd, _cumsum_rows_bwd)

P_SCORE = 1
P_SOLVE = 1
P_STATE = 1
SCAN_CHUNKS = (2, 1)


def _neumann(l):
    C = l.shape[1]
    eye = (lax.broadcasted_iota(jnp.int32, (C, C), 0) == lax.broadcasted_iota(jnp.int32, (C, C), 1)).astype(F32)
    x, lp = eye + l, l
    for _ in range(int(math.log2(C)) - 1):
        lp = _bmm(lp, lp, 2, 1, P_SOLVE)
        x = x + _bmm(x, lp, 2, 1, P_SOLVE)
    return x


@jax.custom_vjp
def _unit_lower_inverse(l):
    return _neumann(l)


def _unit_lower_inverse_fwd(l):
    x = _neumann(l)
    return x, x


def _unit_lower_inverse_bwd(x, g):
    return (_bmm(_bmm(x, g, 1, 1, P_SOLVE), x, 2, 2, P_SOLVE),)


_unit_lower_inverse.defvjp(_unit_lower_inverse_fwd, _unit_lower_inverse_bwd)


def _known_inverse(x):
    @jax.custom_vjp
    def f(l):
        return x

    f.defvjp(lambda l: (x, None), lambda _, g: (_bmm(_bmm(x, g, 1, 1, P_SOLVE), x, 2, 2, P_SOLVE),))
    return f


def _chunk(S0, r, lw, k, v, a, b, inverse=None):
    C = CHUNK
    n = r.shape[1] // C
    fold = lambda t: t.reshape(NH * n, C, HD)
    r, lw, k, v, a, b = (fold(t) for t in (r, lw, k, v, a, b))
    ri = lax.broadcasted_iota(jnp.int32, (C, C), 0)
    ci = lax.broadcasted_iota(jnp.int32, (C, C), 1)
    strict = (ri > ci)
    cum = _cumsum_rows(lw)
    p_in = jnp.exp(cum)
    p_ex = jnp.exp(cum - lw)
    p_inv = jnp.exp(-cum)
    at, rt, bt, kt = a * p_ex, r * p_in, b * p_inv, k * p_inv
    lhs = jnp.concatenate([at, rt], axis=1)
    rhs_ = jnp.concatenate([bt, kt], axis=1)
    sc = _bmm(lhs, rhs_, 2, 2, P_SCORE)
    a_ab = jnp.where(strict, sc[:, :C, :C], 0.0)
    a_ak = jnp.where(strict, sc[:, :C, C:], 0.0)
    incl2 = (lax.broadcasted_iota(jnp.int32, (C, 2 * C), 0) >= lax.broadcasted_iota(jnp.int32, (C, 2 * C), 1) % C)
    a_r = jnp.where(incl2, sc[:, C:, :], 0.0)
    av = _bmm(a_ak, v, 2, 1, P_SCORE)
    x = x_all = (_unit_lower_inverse if inverse is None else _known_inverse(inverse))(a_ab)
    p_last = jnp.exp(cum[:, C - 1:C, :])
    per_chunk = lambda t: t.reshape((NH, n) + t.shape[1:])
    lhs, rhs_, a_r, av, x, v, p_last = (per_chunk(t) for t in (lhs, rhs_, a_r, av, x, v, p_last))
    S, ys = S0, []
    for c in range(n):
        s0 = _bmm(lhs[:, c], S, 2, 2, P_STATE)
        u = _bmm(x[:, c], s0[:, :C] + av[:, c], 2, 1, P_SOLVE)
        uv = jnp.concatenate([u, v[:, c]], axis=1)
        ys.append(s0[:, C:] + _bmm(a_r[:, c], uv, 2, 1, P_SCORE))
        S = (S + _bmm(uv, rhs_[:, c], 1, 1, P_STATE)) * p_last[:, c]
    return jnp.concatenate(ys, axis=1), S, x_all


def _hosting(body, ex, n_in, n_out, n_scratch, n_steps):
    if ex is None:
        return body

    def wrapped(*refs):
        a = n_in
        b = a + ex.n_in
        c = b + n_out
        d = c + ex.n_out
        e = d + n_scratch
        ex_refs = (refs[a:b], refs[c:d], refs[e:])
        grid = n_steps if isinstance(n_steps, tuple) else (n_steps,)
        first = last = True
        for ax_, size in enumerate(grid):
            first = first & (pl.program_id(ax_) == 0)
            last = last & (pl.program_id(ax_) == size - 1)

        @pl.when(first)
        def _():
            ex.start(*ex_refs)

        body(*refs[:a], *refs[b:c], *refs[d:e])

        @pl.when(last)
        def _():
            ex.wait(*ex_refs)

    return wrapped


def _hosted_args(ex):
    if ex is None:
        return [], [], [], [], []
    any_spec = pl.BlockSpec(memory_space=pl.ANY)
    return list(ex.arrays), [any_spec] * ex.n_in, [any_spec] * ex.n_out, list(ex.out_shape), list(ex.scratch)


def _scan_fwd(rw, *, name, hosted=None):
    T = rw.shape[0]
    nc = _pick(T // CHUNK, SCAN_CHUNKS)
    rows = CHUNK * nc
    n = T // rows
    h_in, h_in_specs, h_out_specs, h_out_shape, h_scratch = _hosted_args(hosted)

    def body(r_ref, lw_ref, k_ref, v_ref, a_ref, b_ref, y_ref, ck_ref, inv_ref, s_ref):
        @pl.when(pl.program_id(0) == 0)
        def _():
            s_ref[...] = jnp.zeros_like(s_ref)

        S0 = s_ref[...]
        ck_ref[0] = S0
        y, S1, inv = _chunk(S0, *[_heads(ref[...]) for ref in (r_ref, lw_ref, k_ref, v_ref, a_ref, b_ref)])
        y_ref[...] = _unheads(y)
        inv_ref[0] = inv
        s_ref[...] = S1

    col = lambda j: pl.BlockSpec((rows, RW), lambda i: (i, j))
    return pl.pallas_call(
        _hosting(body, hosted, 6, 3, 1, n), name=name, grid=(n,),
        in_specs=[col(j) for j in range(6)] + h_in_specs,
        out_specs=[pl.BlockSpec((rows, RW), lambda i: (i, 0)),
                   pl.BlockSpec((1, NH, HD, HD), lambda i: (i, 0, 0, 0)),
                   pl.BlockSpec((1, NH * nc, CHUNK, CHUNK), lambda i: (i, 0, 0, 0))] + h_out_specs,
        out_shape=[jax.ShapeDtypeStruct((T, RW), F32), jax.ShapeDtypeStruct((n, NH, HD, HD), F32),
                   jax.ShapeDtypeStruct((n, NH * nc, CHUNK, CHUNK), F32)] + h_out_shape,
        scratch_shapes=[pltpu.VMEM((NH, HD, HD), F32)] + h_scratch,
        compiler_params=_cparams(("arbitrary",)),
    )(rw, rw, rw, rw, rw, rw, *h_in)


def _scan_bwd(rw, ck, inv, dy, *, name, hosted=None):
    T = rw.shape[0]
    nc = _pick(T // CHUNK, SCAN_CHUNKS)
    rows = CHUNK * nc
    n = T // rows

    def body(r_ref, lw_ref, k_ref, v_ref, a_ref, b_ref, ck_ref, inv_ref, dy_ref, o_ref, ds_ref):
        @pl.when(pl.program_id(0) == 0)
        def _():
            ds_ref[...] = jnp.zeros_like(ds_ref)

        prim = [_heads(ref[...]) for ref in (r_ref, lw_ref, k_ref, v_ref, a_ref, b_ref)]
        known = inv_ref[0]
        _, vjp = jax.vjp(lambda S0, *p: _chunk(S0, *p, inverse=known)[:2], ck_ref[0], *prim)
        grads = vjp((_heads(dy_ref[...]), ds_ref[...]))
        ds_ref[...] = grads[0]
        o_ref[...] = jnp.concatenate([_unheads(g) for g in grads[1:]], axis=1).astype(o_ref.dtype)

    h_in, h_in_specs, h_out_specs, h_out_shape, h_scratch = _hosted_args(hosted)
    col = lambda j: pl.BlockSpec((rows, RW), lambda i: (n - 1 - i, j))
    return pl.pallas_call(
        _hosting(body, hosted, 9, 1, 1, n), name=name, grid=(n,),
        in_specs=[col(j) for j in range(6)] + [pl.BlockSpec((1, NH, HD, HD), lambda i: (n - 1 - i, 0, 0, 0)),
                                               pl.BlockSpec((1, NH * nc, CHUNK, CHUNK), lambda i: (n - 1 - i, 0, 0, 0)),
                                               pl.BlockSpec((rows, RW), lambda i: (n - 1 - i, 0))] + h_in_specs,
        out_specs=[pl.BlockSpec((rows, 6 * RW), lambda i: (n - 1 - i, 0))] + h_out_specs,
        out_shape=[jax.ShapeDtypeStruct((T, 6 * RW), MXU)] + h_out_shape,
        scratch_shapes=[pltpu.VMEM((NH, HD, HD), F32)] + h_scratch,
        compiler_params=_cparams(("arbitrary",)),
    )(rw, rw, rw, rw, rw, rw, ck, inv, dy, *h_in)


ATTN_BLOCKS = (4, 2, 1)

def _attn_fwd(qkv, tab, qg, kg, sinks, *, name, hosted=None):
    T = qkv.shape[0]
    nb = _pick(T // BLK, ATTN_BLOCKS)
    n = T // (BLK * nb)
    h_in, h_in_specs, h_out_specs, h_out_shape, h_scratch = _hosted_args(hosted)

    def body(c_ref, p_ref, tc_ref, tp_ref, qg_ref, kg_ref, s_ref, o_ref):
        o_ref[...] = _attn_blocks(c_ref[...], p_ref[...], tc_ref[...], tp_ref[...], qg_ref[...], kg_ref[...],
                                  s_ref[...], pl.program_id(0) == 0).astype(o_ref.dtype)

    cur = lambda w: pl.BlockSpec((nb * BLK, w), lambda i: (i, 0))
    prev = lambda w: pl.BlockSpec((BLK, w), lambda i: (jnp.maximum(i * nb - 1, 0), 0))
    return pl.pallas_call(
        _hosting(body, hosted, 7, 1, 0, n), name=name, grid=(n,),
        in_specs=[cur(QKV_W), prev(QKV_W), cur(3 * 128), prev(3 * 128), _full_spec(qg), _full_spec(kg),
                  _full_spec(sinks)] + h_in_specs,
        out_specs=[cur(RW)] + h_out_specs, out_shape=[jax.ShapeDtypeStruct((T, RW), MXU)] + h_out_shape,
        scratch_shapes=h_scratch,
        compiler_params=_cparams(("arbitrary",)),
    )(qkv, qkv, tab, tab, qg, kg, sinks, *h_in)


def _attn_bwd(qkv, tab, qg, kg, sinks, dy, *, name, hosted=None):
    T = qkv.shape[0]
    nb = _pick(T // BLK, ATTN_BLOCKS)
    n = T // (BLK * nb)
    h_in, h_in_specs, h_out_specs, h_out_shape, h_scratch = _hosted_args(hosted)

    def body(c_ref, p_ref, tc_ref, tp_ref, qg_ref, kg_ref, s_ref, dy_ref, dqkv_ref, dqg_ref, dkg_ref, ds_ref, carry_ref):
        i = pl.program_id(0)

        @pl.when(i == 0)
        def _():
            carry_ref[...] = jnp.zeros_like(carry_ref)
            dqg_ref[...] = jnp.zeros_like(dqg_ref)
            dkg_ref[...] = jnp.zeros_like(dkg_ref)
            ds_ref[...] = jnp.zeros_like(ds_ref)

        tc, tp = tc_ref[...], tp_ref[...]
        f = lambda c, p_, qg_, kg_, sk: _attn_blocks(c, p_, tc, tp, qg_, kg_, sk, i == n - 1)
        _, vjp = jax.vjp(f, c_ref[...], p_ref[...], qg_ref[...], kg_ref[...], s_ref[...])
        dc, dp, dqg, dkg, dsk = vjp(dy_ref[...].astype(F32))
        last = slice((nb - 1) * BLK, nb * BLK)
        dqkv_ref[...] = dc.astype(dqkv_ref.dtype)
        dqkv_ref[last, :] = (dc[last] + carry_ref[...]).astype(dqkv_ref.dtype)
        carry_ref[...] = dp
        dqg_ref[...] += dqg
        dkg_ref[...] += dkg
        ds_ref[...] += dsk

    cur = lambda w: pl.BlockSpec((nb * BLK, w), lambda i: (n - 1 - i, 0))
    prev = lambda w: pl.BlockSpec((BLK, w), lambda i: (jnp.maximum((n - 1 - i) * nb - 1, 0), 0))
    return pl.pallas_call(
        _hosting(body, hosted, 8, 4, 1, n), name=name, grid=(n,),
        in_specs=[cur(QKV_W), prev(QKV_W), cur(3 * 128), prev(3 * 128), _full_spec(qg), _full_spec(kg), _full_spec(sinks),
                  cur(RW)] + h_in_specs,
        out_specs=[cur(QKV_W), _full_spec(qg), _full_spec(kg), _full_spec(sinks)] + h_out_specs,
        out_shape=[jax.ShapeDtypeStruct((T, QKV_W), MXU), jax.ShapeDtypeStruct(qg.shape, F32),
                   jax.ShapeDtypeStruct(kg.shape, F32), jax.ShapeDtypeStruct(sinks.shape, F32)] + h_out_shape,
        scratch_shapes=[pltpu.VMEM((BLK, QKV_W), F32)] + h_scratch,
        compiler_params=_cparams(("arbitrary",)),
    )(qkv, qkv, tab, tab, qg, kg, sinks, dy, *h_in)


def _shift_down(cur, prev8, i):
    rolled = pltpu.roll(cur, 1, 0)
    first_row = jnp.where(i > 0, prev8[7:8, :], 0.0)
    row = lax.broadcasted_iota(jnp.int32, cur.shape, 0)
    return jnp.where(row == 0, first_row, rolled)


def _shift_up(cur, next8, i, n):
    tm = cur.shape[0]
    rolled = pltpu.roll(cur, tm - 1, 0)
    last_row = jnp.where(i < n - 1, next8[0:1, :], 0.0)
    row = lax.broadcasted_iota(jnp.int32, cur.shape, 0)
    return jnp.where(row == tm - 1, last_row, rolled)


def _rope_table(positions):
    half = HD // 8
    inv_freq = 500000.0 ** (-jnp.arange(half, dtype=F32) / half)
    lane = jnp.arange(128) % HD
    rotary = lane < 2 * half
    freq = jnp.where(rotary, inv_freq[lane % half], 0.0)
    ang = positions.astype(F32)[:, None] * freq[None, :]
    cos, sin = jnp.cos(ang), jnp.sin(ang)
    return jnp.concatenate([jnp.where(rotary, cos, 1.0), jnp.where(lane < half, -sin, 0.0),
                            jnp.where(rotary & (lane >= half), sin, 0.0)], axis=1)


GATHER_BEHIND = {"f_proj_shift": [('ffn_w3', 512, 768)], "f_proj_gates": [('ffn_w3', 768, 1024)],
                 "f_prep": [('ffn_w2', 352, 704)],
                 "f_scan": [('ffn_w1', 0, 1024), ('w_branch_a', 0, 512), ('w_branch_b', 0, 512)],
                 "f_post": [('ffn_w3', 0, 512)], "f_attn": [('ffn_w2', 0, 352), ('w_out', 0, 256)]}
LATE = ['w_out', 'w_branch_a', 'w_branch_b', 'ffn_w1', 'ffn_w3', 'ffn_w2']
FFN = ('ffn_w1', 'ffn_w3', 'ffn_w2')
BACK_ATTN = ['w_out', 'w_branch_a', 'w_branch_b', 'ffn_w2']
BACK_SCAN = ['ffn_w1', 'ffn_w3']
BACK_LAST = ['w_in', 'decay_up', 'iclr_up', 'gate_up']


def _full_weight(g, ax):
    return g.reshape(-1, g.shape[2]) if ax == 0 else jnp.concatenate([g[j] for j in range(4)], axis=1)


def _local_step(x, target, ada, tab, w, s, shards=None):
    T = x.shape[0]
    tm = _pick(T, (512, 256, 128))
    tm_wide = _pick(T, (256, 128))
    tm_vjp = _pick(T, (256, 128))
    row = lambda n, dt=F32: (n, dt, 'row')
    acc = lambda n, r=1: (n, F32, r)

    def f_norm1(x_, g, ada_):
        return _norm_mod(x_, g, ada_[:, D:2 * D], ada_[:, 0:D])

    landed = {}

    def behind(kernel_name):
        if not shards:
            return None
        pieces = GATHER_BEHIND[kernel_name]
        arrays = [shards[n] for n, _, _ in pieces]
        if all(hi - lo == a_.shape[0] for a_, (_, lo, hi) in zip(arrays, pieces)):
            return _GatherChipsHalved(arrays)
        return _GatherChips(arrays, rows=[(lo, hi) for _, lo, hi in pieces])

    def took(kernel_name, got):
        landed.update(zip(GATHER_BEHIND[kernel_name], got))

    def mm_behind(a_, w_, kernel_name, **kw):
        ex = behind(kernel_name)
        res = _mm_nn(a_, w_, name=kernel_name, hosted=ex, **kw)
        if ex:
            took(kernel_name, res[1:])
            return res[0]
        return res

    h1, proj, *got = _mm_then([(None, w['w_in'][:, :SHIFT_W], 'nn')], lambda i, n, y, h, *_: (h, y), [x],
                              [s['norm1_gain'], ada], [row(D, MXU), row(SHIFT_W)], tm=tm, name="f_proj_shift",
                              lhs_fn=f_norm1, hosted=behind("f_proj_shift"))
    took("f_proj_shift", got)
    proj_qkv = _mm_nn(h1, w['w_in'][:, SHIFT_W:SHIFT_W + QKV_W], name="f_proj_qkv")
    proj_g = mm_behind(h1, w['w_in'][:, SHIFT_W + QKV_W:], "f_proj_gates", out_dtype=MXU)
    prep_consts = [s['decay_w0'], s['lora_up'], s['iclr_a0'], s['gate_up'], s['k_k'], s['k_a']]

    def f_prep(i, n, cur, prev8, mu, *params):
        mixed = cur + (_shift_down(cur, prev8, i) - cur) * mu
        return (_prep(mixed, *params),)
    rw, *got = _rowwise(f_prep, [(proj, SHIFT_W)], [s['tshift_mu']] + prep_consts, [row(7 * RW)], tm=tm_wide,
                        name="f_prep", halo=[(proj, SHIFT_W, 'prev')], hosted=behind("f_prep"))
    took("f_prep", got)
    y, ck, inv, *got = _scan_fwd(rw, name="f_scan", hosted=behind("f_scan"))
    took("f_scan", got)
    post_consts = [s['lnx_gain'], s['lnx_bias'], s['r_k']]

    rkvg = [(rw, RW, j) for j in (0, 2, 3, 6)]

    def f_post(i, n, *args):
        return (_post(*args),)
    ya, *got = _rowwise(f_post, [y] + rkvg, post_consts, [row(RW, MXU)], tm=tm_wide, name="f_post",
                        hosted=behind("f_post"))
    took("f_post", got)
    yb, *got = _attn_fwd(proj_qkv, tab, s['q_norm_gain'], s['k_norm_gain'], s['attn_sinks'], name="f_attn",
                         hosted=behind("f_attn"))
    took("f_attn", got)
    w = dict(w)
    if shards:
        ax = dict(SHARDED)
        for n in LATE:
            rows = [landed[key] for key in sorted(k_ for k_ in landed if k_[0] == n)]
            blocks = rows[0] if len(rows) == 1 else jnp.concatenate(rows, axis=1)
            w[n] = blocks if n in FFN else _full_weight(blocks, ax[n])
    else:
        fs = DFF // 4
        w.update({n: w[n].reshape(D, 4, fs).transpose(1, 0, 2) for n in ('ffn_w1', 'ffn_w3')})
        w['ffn_w2'] = w['ffn_w2'].reshape(4, fs, D)
    def f_merge(pg, ya_, yb_, x_, wa, wb, bias, g, ada_):
        ma_ = jnp.dot(ya_, wa, preferred_element_type=F32).astype(MXU)
        mb_ = jnp.dot(yb_, wb, preferred_element_type=F32).astype(MXU)
        return _merge(pg.astype(F32), ma_.astype(F32), mb_.astype(F32), bias), ma_, mb_

    def f_res1(i, n, mo_, merged_, ma_, mb_, pg, ya_, yb_, x_, wa, wb, bias, g, ada_):
        x1_ = x_ + ada_[:, 2 * D:3 * D] * mo_
        return merged_, ma_, mb_, mo_, x1_, _norm_mod(x1_, g, ada_[:, 4 * D:5 * D], ada_[:, 3 * D:4 * D])
    merged, ma, mb, mo, x1, h2 = _mm_then(
        [(None, w['w_out'], 'nn')], f_res1, [proj_g, ya, yb, x],
        [w['w_branch_a'], w['w_branch_b'], s['branch_gate_b'], s['norm2_gain'], ada],
        [row(D, MXU), row(D, MXU), row(D, MXU), row(D), row(D), row(D, MXU)], tm=tm, name="f_out", lhs_fn=f_merge)
    u, v, act = _ffn_in(h2, w['ffn_w1'], w['ffn_w3'], name="f_ffn_in")

    def f_loss(i, n, ff_, x1_, tgt, ada_):
        g2 = ada_[:, 5 * D:6 * D]
        err = x1_ + g2 * ff_ - tgt
        dx2 = err * (1.0 / D)
        loss = 0.5 * jnp.sum(jnp.sum(err * err, axis=1, keepdims=True) * (1.0 / D), axis=0, keepdims=True)
        return dx2, (dx2 * g2), jnp.broadcast_to(loss, (1, 128)), jnp.sum(dx2 * ff_, axis=0, keepdims=True)
    dx2, dff, loss, dgate2 = _mm_then([(act, w['ffn_w2'], 'nn', j) for j in range(4)], f_loss, [x1, target], [ada],
                                      [row(D), row(D, MXU), acc(128), acc(D)], tm=tm, name="f_ffn_out")

    du, dv = _ffn_act_bwd(dff, w['ffn_w2'], u, v, name="b_ffn_out_dx")
    g_w2 = _mm_tn_blocks(act, dff, name="b_ffn_out_dw", out_dtype=MXU)
    g_w1 = _mm_tn_blocks(h2, du, name="b_ffn_w1_dw", out_dtype=MXU)
    g_w3 = _mm_tn_blocks(h2, dv, name="b_ffn_w3_dw", out_dtype=MXU)

    def b_res1(i, n, dh2_, x1_, dx2_, mo_, g, ada_):
        _, vjp = jax.vjp(_norm_mod, x1_, g, ada_[:, 4 * D:5 * D], ada_[:, 3 * D:4 * D])
        dxn, dg, dsc, dsh = vjp(dh2_)
        dx1_ = dxn + dx2_
        g1 = ada_[:, 2 * D:3 * D]
        return dx1_, dx1_ * g1, dg, dsc, dsh, jnp.sum(dx1_ * mo_, axis=0, keepdims=True)
    dx1, dmo, d_gain2, d_scale2, d_shift2, dgate1 = _mm_then(
        [(t, w[n], 'nt', j) for t, n in ((du, 'ffn_w1'), (dv, 'ffn_w3')) for j in range(4)], b_res1, [x1, dx2, mo],
        [s['norm2_gain'], ada],
        [row(D), row(D, MXU), acc(D), acc(D), acc(D), acc(D)], tm=tm_wide, name="b_ffn_in_dx")
    g_wout = _mm_tn(merged, dmo, name="b_out_dw", out_dtype=MXU)

    def b_merge(i, n, dm, pg, ma_, mb_, wa, wb, bias):
        _, vjp = jax.vjp(_merge, pg.astype(F32), ma_.astype(F32), mb_.astype(F32), bias)
        dpg, dma_, dmb_, dbias = vjp(dm)
        dma_, dmb_ = dma_.astype(MXU), dmb_.astype(MXU)
        nt = lambda a_, b_: lax.dot_general(a_, b_, (((1,), (1,)), ((), ())), preferred_element_type=F32)
        return dpg, dma_, dmb_, nt(dma_, wa), nt(dmb_, wb), dbias
    dpg, dma, dmb, dya, dyb, d_bias = _mm_then(
        [(dmo, w['w_out'], 'nt')], b_merge, [proj_g, ma, mb], [w['w_branch_a'], w['w_branch_b'], s['branch_gate_b']],
        [row(GATE_W, MXU), row(D, MXU), row(D, MXU), row(RW), row(RW), acc(GATE_W)], tm=tm, name="b_out_dx")
    g_wa = _mm_tn(ya, dma, name="b_branch_a_dw", out_dtype=MXU, col_shards=4)
    g_wb = _mm_tn(yb, dmb, name="b_branch_b_dw", out_dtype=MXU, col_shards=4)
    gw = dict(w_branch_a=g_wa, w_branch_b=g_wb, w_out=g_wout.reshape(4, D // 4, D), ffn_w1=g_w1, ffn_w3=g_w3,
              ffn_w2=g_w2)
    recv = {}
    dqkv, d_qg, d_kg, d_sinks, *got = _attn_bwd(
        proj_qkv, tab, s['q_norm_gain'], s['k_norm_gain'], s['attn_sinks'], dyb, name="b_attn",
        hosted=shards and _ScatterChips([gw[n] for n in BACK_ATTN]))
    recv.update(zip(BACK_ATTN, got))

    def b_post(i, n, y_, r_, k_, v_, g_, dya_, *params):
        _, vjp = jax.vjp(_post, y_, r_, k_, v_, g_, *params)
        dy_, dr_, dk_, dv_, dg_, *dparams = vjp(dya_)
        return (dy_, jnp.concatenate([dr_, dk_, dv_, dg_], axis=1), *dparams)
    dy, drkvg, d_lnx_gain, d_lnx_bias, d_r_k = _rowwise(
        b_post, [y] + rkvg + [dya], post_consts, [row(RW), row(4 * RW, MXU), acc(RW), acc(RW), acc(RW)], tm=tm_wide,
        name="b_post")
    dscan, *got = _scan_bwd(rw, ck, inv, dy, name="b_scan",
                            hosted=shards and _ScatterChips([gw[n] for n in BACK_SCAN]))
    recv.update(zip(BACK_SCAN, got))

    def b_prep(i, n, cur, drw_, dscan_, prev8, mu, *params):
        shifted = _shift_down(cur, prev8, i)
        mixed = cur + (shifted - cur) * mu
        _, vjp = jax.vjp(_prep, mixed, *params)
        blk = lambda t, j: t[:, j * RW:(j + 1) * RW].astype(F32)
        ct = jnp.concatenate([blk(dscan_, 0) + blk(drw_, 0), blk(dscan_, 1), blk(dscan_, 2) + blk(drw_, 1),
                              blk(dscan_, 3) + blk(drw_, 2), blk(dscan_, 4), blk(dscan_, 5), blk(drw_, 3)], axis=1)
        grads = vjp(ct)
        dmixed = grads[0]
        return (dmixed, jnp.sum(dmixed * (shifted - cur), axis=0, keepdims=True)) + tuple(grads[1:])
    dmixed, d_mu, d_w0, d_lora, d_a0, d_gate_up, d_kk, d_ka = _rowwise(
        b_prep, [(proj, SHIFT_W), drkvg, dscan], [s['tshift_mu']] + prep_consts,
        [row(SHIFT_W), acc(SHIFT_W), acc(RW), acc(2 * RW, 128), acc(RW), acc(RW, 128), acc(RW), acc(RW)],
        tm=tm_vjp, name="b_prep", halo=[(proj, SHIFT_W, 'prev')])

    def b_gather(i, n, dm, dqkv_, dpg_, next8, mu):
        dcur = dm * (1.0 - mu) + _shift_up(dm, next8, i, n) * mu
        return (jnp.concatenate([dcur.astype(MXU), dqkv_, dpg_], axis=1),)
    (dproj,) = _rowwise(b_gather, [dmixed, dqkv, dpg], [s['tshift_mu']], [row(IN_W, MXU)], tm=tm_wide, name="b_gather",
                        halo=[(dmixed, SHIFT_W, 'next')])
    g_win = _mm_tn(h1, dproj, name="b_proj_dw", out_dtype=MXU, col_shards=4)

    def col_blocks(g):
        k, n = g.shape
        return g.reshape(k, 4, n // 4).transpose(1, 0, 2).astype(MXU)
    gw.update(w_in=g_win, decay_up=col_blocks(d_lora[:64, :RW]), iclr_up=col_blocks(d_lora[64:, RW:]),
              gate_up=col_blocks(d_gate_up))
    top, bottom = None, None
    if shards:
        top = _ScatterChips([gw[n] for n in BACK_LAST], rows=[(0, D // 2)] + [None] * (len(BACK_LAST) - 1))
        bottom = _ScatterChips([gw['w_in']], rows=[(D // 2, 3 * D // 4)])
        dh1, *got_top = _mm_nt(dproj, w['w_in'], name="b_proj_dx", hosted=top)
    else:
        dh1 = _mm_nt(dproj, w['w_in'], name="b_proj_dx")

    def b_norm1(i, n, x_, dh1_, dx1_, g, ada_):
        _, vjp = jax.vjp(_norm_mod, x_, g, ada_[:, D:2 * D], ada_[:, 0:D])
        dxn, dg, dsc, dsh = vjp(dh1_)
        return dxn + dx1_, dg, dsc, dsh
    dx, d_gain1, d_scale1, d_shift1, *got_bottom = _rowwise(
        b_norm1, [x, dh1, dx1], [s['norm1_gain'], ada], [row(D), acc(D), acc(D), acc(D)], tm=tm, name="b_norm1",
        hosted=bottom)
    if shards:
        recv.update(zip(BACK_LAST[1:], got_top[1:]))
        recv['w_in'] = [got_top[0], got_bottom[0]]

    d_ada = jnp.concatenate([d_shift1, d_scale1, dgate1, d_shift2, d_scale2, dgate2], axis=1)
    gs = dict(norm1_gain=d_gain1, norm2_gain=d_gain2, tshift_mu=d_mu, decay_w0=d_w0, iclr_a0=d_a0, k_k=d_kk, k_a=d_ka,
              r_k=d_r_k, lnx_gain=d_lnx_gain, lnx_bias=d_lnx_bias, q_norm_gain=d_qg, k_norm_gain=d_kg,
              attn_sinks=d_sinks, branch_gate_b=d_bias)
    return loss, dx, d_ada, gw, gs, recv


ANY = pl.BlockSpec(memory_space=pl.ANY)


def _place():
    x, y, c = lax.axis_index("x"), lax.axis_index("y"), lax.axis_index("c")
    return x, y, c, [(1 - x, y), (x, 1 - y), (1 - x, 1 - y)]


def _all_gather8(x_shard, *, name):
    m_per, n = x_shard.shape

    def body(x_ref, out_ref, send_sems, recv_sems, local_sem):
        x, y, c, chips = _place()
        me, sibling = (x, y, c), (x, y, 1 - c)

        def rows(px, py, pc):
            return out_ref.at[pl.ds((4 * px + 2 * py + pc) * m_per, m_per), :]

        def copy(k, block, to, src=None):
            return pltpu.make_async_remote_copy(
                src_ref=rows(*block) if src is None else src, dst_ref=rows(*block),
                send_sem=send_sems.at[k], recv_sem=recv_sems.at[k], device_id=to, device_id_type=MESH)

        mine = pltpu.make_async_copy(x_ref, rows(*me), local_sem)
        mine.start()
        first = [copy(0, me, sibling, src=x_ref)]
        first += [copy(1 + j, me, (*chip, c), src=x_ref) for j, chip in enumerate(chips)]
        for cp in first:
            cp.start()
        passed = [copy(4 + j, (*chip, c), sibling) for j, chip in enumerate(chips)]
        for j, chip in enumerate(chips):
            copy(1 + j, (*chip, c), me).wait_recv()
            passed[j].start()
        copy(0, sibling, me).wait_recv()
        for j, chip in enumerate(chips):
            copy(4 + j, (*chip, 1 - c), me).wait_recv()
        for cp in first + passed:
            cp.wait_send()
        mine.wait()

    return pl.pallas_call(
        body, name=name, out_shape=jax.ShapeDtypeStruct((8 * m_per, n), x_shard.dtype),
        in_specs=[pl.BlockSpec(memory_space=pltpu.VMEM)], out_specs=pl.BlockSpec(memory_space=pltpu.VMEM),
        scratch_shapes=[pltpu.SemaphoreType.DMA((7,)), pltpu.SemaphoreType.DMA((7,)), pltpu.SemaphoreType.DMA],
    )(x_shard)


class _GatherChips:
    def __init__(self, shards, rows=None):
        n = len(shards)
        self.rows = [r or (0, s.shape[0]) for s, r in zip(shards, rows or [None] * n, strict=True)]
        self.arrays, self.n_in, self.n_out = list(shards), n, n
        self.out_shape = [jax.ShapeDtypeStruct((4, hi - lo) + s.shape[1:], s.dtype)
                          for s, (lo, hi) in zip(shards, self.rows, strict=True)]
        self.scratch = [pltpu.SemaphoreType.DMA((3 * n,)), pltpu.SemaphoreType.DMA((3 * n,)),
                        pltpu.SemaphoreType.DMA((n,))]

    def _copies(self, x_refs, out_refs, sems, receiving):
        send_sems, recv_sems, local_sems = sems
        x, y, c, chips = _place()
        s_me = 2 * x + y
        n = self.n_in
        src = [x_refs[a].at[pl.ds(lo, hi - lo)] for a, (lo, hi) in enumerate(self.rows)]

        def copy(a, k, s):
            return pltpu.make_async_remote_copy(
                src_ref=src[a], dst_ref=out_refs[a].at[s], send_sem=send_sems.at[3 * a + k],
                recv_sem=recv_sems.at[3 * a + k], device_id=(*chips[k], c), device_id_type=MESH)

        mine = [pltpu.make_async_copy(src[a], out_refs[a].at[s_me], local_sems.at[a]) for a in range(n)]
        sends = [copy(a, k, s_me) for a in range(n) for k in range(3)]
        if not receiving:
            return mine, sends
        return mine, sends, [copy(a, k, 2 * px + py) for a in range(n) for k, (px, py) in enumerate(chips)]

    def start(self, x_refs, out_refs, sems):
        mine, sends = self._copies(x_refs, out_refs, sems, False)
        for cp in mine + sends:
            cp.start()

    def wait(self, x_refs, out_refs, sems):
        mine, sends, recvs = self._copies(x_refs, out_refs, sems, True)
        for cp in recvs:
            cp.wait_recv()
        for cp in sends:
            cp.wait_send()
        for cp in mine:
            cp.wait()


class _GatherChipsHalved(_GatherChips):
    def __init__(self, shards):
        super().__init__(shards)
        n = self.n_in
        self.scratch = [pltpu.SemaphoreType.DMA((6 * n,)), pltpu.SemaphoreType.DMA((6 * n,)),
                        pltpu.SemaphoreType.DMA((n,))]

    def _copies(self, x_refs, out_refs, sems, receiving):
        send_sems, recv_sems, local_sems = sems
        x, y, c, chips = _place()
        s_me = 2 * x + y
        n = self.n_in

        def half(a, who):
            rows = x_refs[a].shape[0] // 2
            return pl.ds(who * rows, rows)

        def over_chips(a, k, s):
            return pltpu.make_async_remote_copy(
                src_ref=x_refs[a].at[half(a, c)], dst_ref=out_refs[a].at[s, half(a, c)],
                send_sem=send_sems.at[3 * a + k], recv_sem=recv_sems.at[3 * a + k],
                device_id=(*chips[k], c), device_id_type=MESH)

        def to_sibling(a, k, s, who):
            return pltpu.make_async_remote_copy(
                src_ref=out_refs[a].at[s, half(a, who)], dst_ref=out_refs[a].at[s, half(a, who)],
                send_sem=send_sems.at[3 * n + 3 * a + k], recv_sem=recv_sems.at[3 * n + 3 * a + k],
                device_id=(x, y, 1 - c), device_id_type=MESH)

        mine = [pltpu.make_async_copy(x_refs[a], out_refs[a].at[s_me], local_sems.at[a]) for a in range(n)]
        sends = [over_chips(a, k, s_me) for a in range(n) for k in range(3)]
        if not receiving:
            return mine, sends
        pairs = [(a, k, 2 * px + py) for a in range(n) for k, (px, py) in enumerate(chips)]
        landed = [over_chips(a, k, s) for a, k, s in pairs]
        passed_on = [to_sibling(a, k, s, c) for a, k, s in pairs]
        from_sibling = [to_sibling(a, k, s, 1 - c) for a, k, s in pairs]
        return mine, sends, landed, passed_on, from_sibling

    def wait(self, x_refs, out_refs, sems):
        mine, sends, landed, passed_on, from_sibling = self._copies(x_refs, out_refs, sems, True)
        for got, fwd in zip(landed, passed_on, strict=True):
            got.wait_recv()
            fwd.start()
        for cp in from_sibling:
            cp.wait_recv()
        for cp in sends + passed_on:
            cp.wait_send()
        for cp in mine:
            cp.wait()


class _ScatterChips:
    def __init__(self, parts, rows=None):
        n = len(parts)
        self.rows = [r or (0, p.shape[1]) for p, r in zip(parts, rows or [None] * n, strict=True)]
        self.arrays, self.n_in, self.n_out = list(parts), n, n
        self.out_shape = [jax.ShapeDtypeStruct((3, hi - lo) + p.shape[2:], p.dtype)
                          for p, (lo, hi) in zip(parts, self.rows, strict=True)]
        self.scratch = [pltpu.SemaphoreType.DMA((3 * n,)), pltpu.SemaphoreType.DMA((3 * n,))]

    def _copies(self, g_refs, out_refs, sems):
        send_sems, recv_sems = sems
        x, y, c, chips = _place()
        return [pltpu.make_async_remote_copy(
            src_ref=g_refs[a].at[2 * px + py, pl.ds(lo, hi - lo)], dst_ref=out_refs[a].at[k],
            send_sem=send_sems.at[3 * a + k], recv_sem=recv_sems.at[3 * a + k], device_id=(px, py, c),
            device_id_type=MESH)
            for a, (lo, hi) in enumerate(self.rows) for k, (px, py) in enumerate(chips)]

    def start(self, g_refs, out_refs, sems):
        for cp in self._copies(g_refs, out_refs, sems):
            cp.start()

    def wait(self, g_refs, out_refs, sems):
        sends = self._copies(g_refs, out_refs, sems)
        for cp in sends:
            cp.wait_recv()
        for cp in sends:
            cp.wait_send()


def _exchange_call(ex, *, name):
    def body(*refs):
        parts = (refs[:ex.n_in], refs[ex.n_in:ex.n_in + ex.n_out], refs[ex.n_in + ex.n_out:])
        ex.start(*parts)
        ex.wait(*parts)

    return pl.pallas_call(body, name=name, out_shape=ex.out_shape, in_specs=[ANY] * ex.n_in,
                          out_specs=[ANY] * ex.n_out, scratch_shapes=ex.scratch)(*ex.arrays)


class _SwapSibling:
    def __init__(self, vs):
        n = len(vs)
        self.arrays, self.n_in, self.n_out = list(vs), n, n
        self.out_shape = [jax.ShapeDtypeStruct(v.shape, v.dtype) for v in vs]
        self.scratch = [pltpu.SemaphoreType.DMA((n,)), pltpu.SemaphoreType.DMA((n,))]

    def _copies(self, v_refs, out_refs, sems):
        send_sems, recv_sems = sems
        x, y, c, _ = _place()
        return [pltpu.make_async_remote_copy(src_ref=v_refs[a], dst_ref=out_refs[a], send_sem=send_sems.at[a],
                                             recv_sem=recv_sems.at[a], device_id=(x, y, 1 - c), device_id_type=MESH)
                for a in range(self.n_in)]

    def start(self, v_refs, out_refs, sems):
        for cp in self._copies(v_refs, out_refs, sems):
            cp.start()

    def wait(self, v_refs, out_refs, sems):
        for cp in self._copies(v_refs, out_refs, sems):
            cp.wait()


class _Both:
    def __init__(self, first, second):
        self.parts = (first, second)
        self.arrays = first.arrays + second.arrays
        self.n_in, self.n_out = first.n_in + second.n_in, first.n_out + second.n_out
        self.out_shape = first.out_shape + second.out_shape
        self.scratch = first.scratch + second.scratch

    def _split(self, in_refs, out_refs, sems):
        a, b = self.parts
        return ((a, in_refs[:a.n_in], out_refs[:a.n_out], sems[:len(a.scratch)]),
                (b, in_refs[a.n_in:], out_refs[a.n_out:], sems[len(a.scratch):]))

    def start(self, in_refs, out_refs, sems):
        for ex, *refs in self._split(in_refs, out_refs, sems):
            ex.start(*refs)

    def wait(self, in_refs, out_refs, sems):
        for ex, *refs in self._split(in_refs, out_refs, sems):
            ex.wait(*refs)


def _sum_parts(own, others, *, name):
    R, C = own.shape
    tm = _pick(R, (256, 128, 64))

    def body(own_ref, o0_ref, o1_ref, o2_ref, out_ref):
        tot = own_ref[...].astype(F32)
        for ref in (o0_ref, o1_ref, o2_ref):
            tot = tot + ref[...].astype(F32)
        out_ref[...] = tot

    part = lambda k: pl.BlockSpec((None, tm, C), lambda i: (k, i, 0))
    return pl.pallas_call(
        body, name=name, grid=(R // tm,),
        in_specs=[pl.BlockSpec((tm, C), lambda i: (i, 0)), part(0), part(1), part(2)],
        out_specs=pl.BlockSpec((tm, C), lambda i: (i, 0)), out_shape=jax.ShapeDtypeStruct((R, C), F32),
        compiler_params=_cparams(("arbitrary",)),
    )(own, others, others, others)


def _adam_math(w_, m_, v_, g):
    m2 = ADAM_B1 * m_ + (1.0 - ADAM_B1) * g
    v2 = ADAM_B2 * v_ + (1.0 - ADAM_B2) * jnp.square(g)
    m_hat = m2 / (1.0 - ADAM_B1 ** ADAM_STEP)
    v_hat = v2 / (1.0 - ADAM_B2 ** ADAM_STEP)
    delta = -ADAM_LR * (m_hat / (jnp.sqrt(v_hat) + ADAM_EPS) + ADAM_WD * w_)
    return delta, m2, v2


SMALL_SLOTS = 24
SMALL_COLS = 2 * D


def _small_rows(widths):
    firsts, row = [], 0
    for n_i in widths:
        firsts.append(row)
        row += -(-n_i // SMALL_COLS)
    assert row <= SMALL_SLOTS
    return firsts


def _pack_small(grads, *, name):
    n = len(grads)
    firsts = _small_rows([g.shape[1] for g in grads])

    def body(*refs):
        out_ref = refs[n]
        out_ref[...] = jnp.zeros_like(out_ref)
        for first, ref in zip(firsts, refs[:n], strict=True):
            for lo in range(0, ref.shape[1], SMALL_COLS):
                width = min(SMALL_COLS, ref.shape[1] - lo)
                row = first + lo // SMALL_COLS
                out_ref[row:row + 1, 0:width] = ref[:, lo:lo + width]

    return pl.pallas_call(body, name=name, out_shape=jax.ShapeDtypeStruct((SMALL_SLOTS, SMALL_COLS), F32))(*grads)


def _adamw_small(ws, ms, vs, gathered, *, name):
    n = len(ws)
    firsts = _small_rows([w.shape[1] for w in ws] + [128])

    def total(g_ref, first, nc):
        pieces = []
        for lo in range(0, nc, SMALL_COLS):
            width, row = min(SMALL_COLS, nc - lo), first + lo // SMALL_COLS
            g = g_ref[row:row + 1, 0:width]
            for d in range(1, 8):
                g = g + g_ref[d * SMALL_SLOTS + row:d * SMALL_SLOTS + row + 1, 0:width]
            pieces.append(g)
        return pieces[0] if len(pieces) == 1 else jnp.concatenate(pieces, axis=1)

    def body(*refs):
        w_refs, m_refs, v_refs, g_ref = refs[:n], refs[n:2 * n], refs[2 * n:3 * n], refs[3 * n]
        outs = refs[3 * n + 1:]
        for i in range(n):
            g = total(g_ref, firsts[i], w_refs[i].shape[1])
            delta, m2, v2 = _adam_math(w_refs[i][...], m_refs[i][...], v_refs[i][...], g)
            for k, val in enumerate((g, delta, m2, v2)):
                outs[k * n + i][...] = val
        outs[4 * n][...] = total(g_ref, firsts[n], 128)

    shapes = [jax.ShapeDtypeStruct(w.shape, F32) for w in ws]
    res = pl.pallas_call(body, name=name, out_shape=shapes * 4 + [jax.ShapeDtypeStruct((1, 128), F32)],
                         compiler_params=pltpu.CompilerParams(vmem_limit_bytes=VMEM_LIMIT))(*ws, *ms, *vs, gathered)
    return [res[k * n:(k + 1) * n] for k in range(4)], res[4 * n]


def _adamw(w, m, v, gparts, *, tm, name, hosted=None):
    def fn(i, n, w_, m_, v_, *gs):
        g = gs[0]
        for p in gs[1:]:
            g = g + p
        return (g,) + _adam_math(w_, m_, v_, g)
    nc = w.shape[1]
    return _rowwise(fn, [w, m, v] + list(gparts), [], [(nc, F32, 'row')] * 4, tm=tm, name=name, hosted=hosted)


WEIGHTS = ['ada_w', 'ada_b', 'norm1_gain', 'norm2_gain', 'w_in', 'tshift_mu', 'decay_w0', 'decay_up', 'iclr_a0',
           'iclr_up', 'gate_up', 'k_k', 'k_a', 'r_k', 'lnx_gain', 'lnx_bias', 'q_norm_gain', 'k_norm_gain', 'attn_sinks',
           'branch_gate_b', 'w_branch_a', 'w_branch_b', 'w_out', 'ffn_w1', 'ffn_w3', 'ffn_w2']
SHARDED = [('w_in', 1), ('decay_up', 1), ('iclr_up', 1), ('gate_up', 1), ('w_branch_a', 1), ('w_branch_b', 1),
           ('w_out', 0), ('ffn_w1', 1), ('ffn_w3', 1), ('ffn_w2', 0)]
SMALL = ['ada_b', 'norm1_gain', 'norm2_gain', 'tshift_mu', 'decay_w0', 'iclr_a0', 'k_k', 'k_a', 'r_k', 'lnx_gain',
         'lnx_bias', 'q_norm_gain', 'k_norm_gain', 'attn_sinks', 'branch_gate_b']


def kernel(x, c, positions, ada_w, ada_b, norm1_gain, norm2_gain, w_in, tshift_mu, decay_w0, decay_up, iclr_a0, iclr_up, gate_up, k_k, k_a, r_k, lnx_gain, lnx_bias, q_norm_gain, k_norm_gain, attn_sinks, branch_gate_b, w_branch_a, w_branch_b, w_out, ffn_w1, ffn_w3, ffn_w2, loss_target, m_ada_w, m_ada_b, m_norm1_gain, m_norm2_gain, m_w_in, m_tshift_mu, m_decay_w0, m_decay_up, m_iclr_a0, m_iclr_up, m_gate_up, m_k_k, m_k_a, m_r_k, m_lnx_gain, m_lnx_bias, m_q_norm_gain, m_k_norm_gain, m_attn_sinks, m_branch_gate_b, m_w_branch_a, m_w_branch_b, m_w_out, m_ffn_w1, m_ffn_w3, m_ffn_w2, v_ada_w, v_ada_b, v_norm1_gain, v_norm2_gain, v_w_in, v_tshift_mu, v_decay_w0, v_decay_up, v_iclr_a0, v_iclr_up, v_gate_up, v_k_k, v_k_a, v_r_k, v_lnx_gain, v_lnx_bias, v_q_norm_gain, v_k_norm_gain, v_attn_sinks, v_branch_gate_b, v_w_branch_a, v_w_branch_b, v_w_out, v_ffn_w1, v_ffn_w3, v_ffn_w2):
    a = dict(locals())
    W = {n: a[n] for n in WEIGHTS}
    M = {n: a['m_' + n] for n in WEIGHTS}
    V = {n: a['v_' + n] for n in WEIGHTS}
    xi, yi, ci = lax.axis_index("x"), lax.axis_index("y"), lax.axis_index("c")
    me = 4 * xi + 2 * yi + ci
    shard = 2 * xi + yi
    mat = lambda t: t.reshape(t.shape[-2], t.shape[-1])
    sharded = [n for n, _ in SHARDED]

    ax = dict(SHARDED)
    late = LATE
    early = [n for n in sharded if n not in late]
    shards = {n: mat(W[n]).astype(MXU) for n in sharded}
    gathered = _exchange_call(_GatherChipsHalved([shards[n] for n in early]), name="gather_weights")
    full = {n: _full_weight(g, ax[n]) for n, g in zip(early, gathered, strict=True)}

    c_all = _all_gather8(jnp.broadcast_to(c, (8, D)), name="gather_c")[0::8]
    pad_rows = lambda t: jnp.concatenate([t, jnp.zeros((BLK - 8, t.shape[1]), t.dtype)])
    c_all = pad_rows(c_all.astype(MXU))
    ada_cols = _mm_nn(c_all, mat(ada_w).astype(MXU), name="f_ada")[:8]
    ada_all = _all_gather8(ada_cols, name="gather_ada").reshape(2, 2, 2, 8, 6 * D // 4)
    ada_mine = lax.dynamic_index_in_dim(ada_all[:, :, 0], me, axis=2, keepdims=False)
    ada = ada_mine.reshape(1, 6 * D) + mat(ada_b)

    zero = jnp.zeros((64, RW), MXU)
    lora = jnp.concatenate([jnp.concatenate([full['decay_up'], zero], axis=1),
                            jnp.concatenate([zero, full['iclr_up']], axis=1)], axis=0)
    s = {n: W[n].reshape(1, -1) for n in SMALL if n != 'ada_b'}
    s['lora_up'] = lora.astype(F32)
    s['gate_up'] = full['gate_up'].astype(F32)
    tab = _rope_table(positions.reshape(-1))
    loss, dx, d_ada, gw, gs, from_chips = _local_step(x[0], loss_target[0], ada, tab, dict(w_in=full['w_in']), s,
                                                      shards={n: shards[n] for n in late})

    gs['ada_b'] = d_ada
    gsmall = _pack_small([gs[n] for n in SMALL] + [loss], name="pack_small_grads")
    gsmall_all = _all_gather8(gsmall, name="gather_small_grads")
    row = lambda src: [src[n].reshape(1, -1) for n in SMALL]
    sm_out, loss = _adamw_small(row(W), row(M), row(V), gsmall_all, name="adamw_small")
    sm_out = [{n: o.reshape(W[n].shape) for n, o in zip(SMALL, outs_k, strict=True)} for outs_k in sm_out]
    loss = loss[0, 0]

    ada_rows = 6 * D // SMALL_COLS
    d_ada_all = gsmall_all.reshape(8, SMALL_SLOTS, SMALL_COLS)[:, :ada_rows].reshape(8, 6 * D)
    d_ada_cols = lax.dynamic_slice_in_dim(d_ada_all, shard * (6 * D // 4), 6 * D // 4, axis=1)
    g_ada_w = _mm_tn(c_all, pad_rows(d_ada_cols.astype(MXU)), name="b_ada")

    rest = [n for n in sharded if n != 'w_in']
    parts = {n: _sum_parts(lax.dynamic_index_in_dim(gw[n], shard, axis=0, keepdims=False), from_chips[n],
                           name="sum_" + n) for n in rest}
    tail = _Both(_ScatterChips([gw['w_in']], rows=[(3 * D // 4, D)]), _SwapSibling([parts[n] for n in rest]))
    res = _adamw(mat(ada_w), mat(m_ada_w), mat(v_ada_w), [g_ada_w], tm=256, name="adamw_ada", hosted=tail)
    ada_out, last_quarter, others = res[:4], res[4], dict(zip(rest, res[5:], strict=True))
    parts['w_in'] = _sum_parts(lax.dynamic_index_in_dim(gw['w_in'], shard, axis=0, keepdims=False),
                               jnp.concatenate(from_chips['w_in'] + [last_quarter], axis=1), name="sum_w_in")
    others['w_in'] = _exchange_call(_SwapSibling([parts['w_in']]), name="swap_w_in")[0]
    sh_out = {}
    for n in sharded:
        part, other = parts[n], others[n]
        sh_out[n] = _adamw(mat(W[n]), mat(M[n]), mat(V[n]), [part, other], tm=_pick(part.shape[0], (256, 128, 64)),
                           name="adamw_" + n)

    def leaf(k, n):
        if n == 'ada_w':
            return ada_out[k].reshape(W[n].shape)
        if n in sharded:
            return sh_out[n][k].reshape(W[n].shape)
        return sm_out[k][n]
    outs = [leaf(k, n) for k in range(4) for n in WEIGHTS]
    return (loss, dx[None], *outs)
```

```python
import functools
import math

import jax
import jax.numpy as jnp
from jax import lax
from jax.experimental import pallas as pl
from jax.experimental.pallas import tpu as pltpu

F32 = jnp.float32
BF16 = jnp.bfloat16
MXU = BF16
HI = lax.Precision.HIGHEST

D = 1024
HD = 64
NH = 8
RW = NH * HD
SHIFT_W = 3 * RW + 64 + 64 + 128
QKV_W = RW + 2 * 128
GATE_W = 2 * D
IN_W = SHIFT_W + QKV_W + GATE_W
DFF = 2816
BLK = 128
CHUNK = 128
RMS_EPS = 1e-6
GN_EPS = 64e-5
NEG_INF = -1e30
ADAM_LR, ADAM_B1, ADAM_B2, ADAM_EPS, ADAM_WD, ADAM_STEP = 0.001, 0.9, 0.999, 1e-08, 0.01, 10
VMEM_LIMIT = 56 * 1024 * 1024
MESH = pl.DeviceIdType.MESH


def _cparams(sem=None):
    return pltpu.CompilerParams(dimension_semantics=sem, vmem_limit_bytes=VMEM_LIMIT)


def _full_spec(a):
    nd = a.ndim
    return pl.BlockSpec(a.shape, lambda *_: (0,) * nd)


def _rowwise(fn, rows, consts, outs, *, tm, name, halo=(), hosted=None):
    rows = [(a + (0,))[:3] if isinstance(a, tuple) else (a, a.shape[1], 0) for a in rows]
    T = rows[0][0].shape[0]
    assert T % tm == 0 and tm % 8 == 0
    n_tiles = T // tm
    n_in = len(rows) + len(halo) + len(consts)
    in_specs = [pl.BlockSpec((tm, nc), lambda i, j=j: (i, j)) for _, nc, j in rows]
    args = [a for a, _, _ in rows]
    for a, nc, kind in halo:
        if kind == 'prev':
            in_specs.append(pl.BlockSpec((8, nc), lambda i: (jnp.maximum(i * (tm // 8) - 1, 0), 0)))
        else:
            in_specs.append(pl.BlockSpec((8, nc), lambda i: (jnp.minimum((i + 1) * (tm // 8), T // 8 - 1), 0)))
        args.append(a)
    in_specs += [_full_spec(a) for a in consts]
    args += list(consts)
    out_shape, out_specs = [], []
    for ncols, dtype, kind in outs:
        if kind == 'row':
            out_shape.append(jax.ShapeDtypeStruct((T, ncols), dtype))
            out_specs.append(pl.BlockSpec((tm, ncols), lambda i: (i, 0)))
        else:
            out_shape.append(jax.ShapeDtypeStruct((kind, ncols), dtype))
            out_specs.append(pl.BlockSpec((kind, ncols), lambda i: (0, 0)))

    def body(*refs):
        i = pl.program_id(0)
        vals = [r[...] for r in refs[:n_in]]
        res = fn(i, n_tiles, *vals)
        for (ncols, dtype, kind), o_ref, val in zip(outs, refs[n_in:], res, strict=True):
            if kind == 'row':
                o_ref[...] = val.astype(dtype)
            else:
                @pl.when(i == 0)
                def _():
                    o_ref[...] = jnp.zeros_like(o_ref)
                o_ref[...] += val.astype(dtype)

    h_in, h_in_specs, h_out_specs, h_out_shape, h_scratch = _hosted_args(hosted)
    res = pl.pallas_call(
        _hosting(body, hosted, n_in, len(outs), 0, n_tiles), name=name, grid=(n_tiles,),
        in_specs=in_specs + h_in_specs, out_specs=out_specs + h_out_specs, out_shape=out_shape + h_out_shape,
        scratch_shapes=h_scratch, compiler_params=_cparams(("arbitrary",)),
    )(*args, *h_in)
    return res


def _pick(n, cands):
    for c in cands:
        if n % c == 0:
            return c
    return n


MM_ROWS = (1024, 512, 256, 128)
MM_COLS = (1536, 1408, 1024, 896, 768, 512, 256, 128)
MM_WIDE = 3000


def _mm_nn(a, w, *, name, out_dtype=F32, hosted=None):
    T, K = a.shape
    N = w.shape[1]
    tm = _pick(T, MM_ROWS)
    tn = _pick(N, MM_COLS)
    grid = (N // tn, T // tm)
    h_in, h_in_specs, h_out_specs, h_out_shape, h_scratch = _hosted_args(hosted)

    def body(a_ref, w_ref, o_ref):
        o_ref[...] = jnp.dot(a_ref[...], w_ref[...], preferred_element_type=F32).astype(out_dtype)

    res = pl.pallas_call(
        _hosting(body, hosted, 2, 1, 0, grid), name=name, grid=grid,
        in_specs=[pl.BlockSpec((tm, K), lambda j, i: (i, 0)), pl.BlockSpec((K, tn), lambda j, i: (0, j))] + h_in_specs,
        out_specs=[pl.BlockSpec((tm, tn), lambda j, i: (i, j))] + h_out_specs,
        out_shape=[jax.ShapeDtypeStruct((T, N), out_dtype)] + h_out_shape, scratch_shapes=h_scratch,
        compiler_params=_cparams(("arbitrary", "arbitrary")),
    )(a, w, *h_in)
    return res if hosted else res[0]


def _mm_nt(dy, w, *, name, out_dtype=F32, hosted=None):
    T, N = dy.shape
    K = w.shape[0]
    tm = _pick(T, MM_ROWS if N <= MM_WIDE else MM_ROWS[1:])
    tk = _pick(K, MM_COLS[1:])
    grid = (K // tk, T // tm)
    h_in, h_in_specs, h_out_specs, h_out_shape, h_scratch = _hosted_args(hosted)

    def body(dy_ref, w_ref, o_ref):
        o_ref[...] = lax.dot_general(dy_ref[...], w_ref[...], (((1,), (1,)), ((), ())),
                                     preferred_element_type=F32).astype(out_dtype)

    res = pl.pallas_call(
        _hosting(body, hosted, 2, 1, 0, grid), name=name, grid=grid,
        in_specs=[pl.BlockSpec((tm, N), lambda j, i: (i, 0)), pl.BlockSpec((tk, N), lambda j, i: (j, 0))] + h_in_specs,
        out_specs=[pl.BlockSpec((tm, tk), lambda j, i: (i, j))] + h_out_specs,
        out_shape=[jax.ShapeDtypeStruct((T, K), out_dtype)] + h_out_shape, scratch_shapes=h_scratch,
        compiler_params=_cparams(("arbitrary", "arbitrary")),
    )(dy, w, *h_in)
    return res if hosted else res[0]


def _mm_tn(a, dy, *, name, out_dtype=F32, col_shards=None):
    T, K = a.shape
    N = dy.shape[1]
    tm = _pick(T, MM_ROWS)
    tn = N // col_shards if col_shards else _pick(N, MM_COLS[1:])
    n_t = T // tm

    def body(a_ref, dy_ref, o_ref, acc_ref):
        i = pl.program_id(1)

        @pl.when(i == 0)
        def _():
            acc_ref[...] = jnp.zeros_like(acc_ref)

        acc_ref[...] += lax.dot_general(a_ref[...], dy_ref[...], (((0,), (0,)), ((), ())), preferred_element_type=F32)

        @pl.when(i == n_t - 1)
        def _():
            o_ref[...] = acc_ref[...].astype(out_dtype)

    if col_shards:
        out_specs = pl.BlockSpec((None, K, tn), lambda j, i: (j, 0, 0))
        out_shape = jax.ShapeDtypeStruct((col_shards, K, tn), out_dtype)
    else:
        out_specs = pl.BlockSpec((K, tn), lambda j, i: (0, j))
        out_shape = jax.ShapeDtypeStruct((K, N), out_dtype)
    return pl.pallas_call(
        body, name=name, grid=(N // tn, n_t),
        in_specs=[pl.BlockSpec((tm, K), lambda j, i: (i, 0)), pl.BlockSpec((tm, tn), lambda j, i: (i, j))],
        out_specs=out_specs, out_shape=out_shape, scratch_shapes=[pltpu.VMEM((K, tn), F32)],
        compiler_params=_cparams(("arbitrary", "arbitrary")),
    )(a, dy)


def _mm_then(products, fn, rows, consts, outs, *, tm, name, lhs_fn=None, hosted=None):
    products = [(p + (None,))[:4] for p in products]
    T = (rows[0] if lhs_fn else products[0][0]).shape[-2]
    n_tiles = T // tm
    in_specs, args = [], []
    for a, w, _, j in products:
        if a is not None and j is None:
            in_specs.append(pl.BlockSpec((tm, a.shape[1]), lambda i: (i, 0)))
            args.append(a)
        elif a is not None:
            in_specs.append(pl.BlockSpec((None, tm, a.shape[2]), lambda i, j=j: (j, i, 0)))
            args.append(a)
        in_specs.append(_full_spec(w) if j is None else
                        pl.BlockSpec((None,) + w.shape[1:], lambda i, j=j: (j, 0, 0)))
        args.append(w)
    n_w = len(args)
    in_specs += [pl.BlockSpec((tm, a.shape[1]), lambda i: (i, 0)) for a in rows] + [_full_spec(c_) for c_ in consts]
    args += list(rows) + list(consts)
    n_in = len(args)
    out_shape, out_specs = [], []
    for ncols, dtype, kind in outs:
        if kind == 'row':
            out_shape.append(jax.ShapeDtypeStruct((T, ncols), dtype))
            out_specs.append(pl.BlockSpec((tm, ncols), lambda i: (i, 0)))
        else:
            out_shape.append(jax.ShapeDtypeStruct((kind, ncols), dtype))
            out_specs.append(pl.BlockSpec((kind, ncols), lambda i: (0, 0)))

    def body(*refs):
        i = pl.program_id(0)
        tiles = [r[...] for r in refs[n_w:n_in]]
        made = []
        if lhs_fn:
            made = lhs_fn(*tiles)
            made = list(made) if isinstance(made, tuple) else [made]
            made[0] = made[0].astype(MXU)
        y, pos = None, 0
        for a, _, form, _ in products:
            if a is None:
                lhs = made[0]
            else:
                lhs, pos = refs[pos][...], pos + 1
            dims = (((1,), (0,)), ((), ())) if form == 'nn' else (((1,), (1,)), ((), ()))
            t = lax.dot_general(lhs, refs[pos][...], dims, preferred_element_type=F32)
            pos += 1
            y = t if y is None else y + t
        res = fn(i, n_tiles, y, *made, *tiles)
        for (ncols, dtype, kind), o_ref, val in zip(outs, refs[n_in:], res, strict=True):
            if kind == 'row':
                o_ref[...] = val.astype(dtype)
            else:
                @pl.when(i == 0)
                def _():
                    o_ref[...] = jnp.zeros_like(o_ref)
                o_ref[...] += val.astype(dtype)

    h_in, h_in_specs, h_out_specs, h_out_shape, h_scratch = _hosted_args(hosted)
    return pl.pallas_call(
        _hosting(body, hosted, n_in, len(outs), 0, n_tiles), name=name, grid=(n_tiles,),
        in_specs=in_specs + h_in_specs, out_specs=out_specs + h_out_specs, out_shape=out_shape + h_out_shape,
        scratch_shapes=h_scratch, compiler_params=_cparams(("arbitrary",)))(*args, *h_in)


def _seg_ones(n):
    r = lax.broadcasted_iota(jnp.int32, (n, n), 0) // HD
    c = lax.broadcasted_iota(jnp.int32, (n, n), 1) // HD
    return (r == c).astype(F32)


def _segsum_raw(x):
    ones = _seg_ones(x.shape[1])
    if MXU == F32:
        return jnp.dot(x, ones, precision=HI, preferred_element_type=F32)
    hi = x.astype(MXU)
    lo = (x - hi.astype(F32)).astype(MXU)
    ones = ones.astype(MXU)
    return jnp.dot(hi, ones, preferred_element_type=F32) + jnp.dot(lo, ones, preferred_element_type=F32)


@jax.custom_vjp
def _segsum(x):
    return _segsum_raw(x)


def _segsum_fwd(x):
    return _segsum_raw(x), None


def _segsum_bwd(_, g):
    return (_segsum_raw(g),)


_segsum.defvjp(_segsum_fwd, _segsum_bwd)


def _mxu(x):
    return x.astype(MXU)


@jax.custom_vjp
def _bdot(a, b):
    return jnp.dot(_mxu(a), _mxu(b), preferred_element_type=F32)


def _bdot_fwd(a, b):
    return _bdot(a, b), (a, b)


def _bdot_bwd(res, g):
    a, b = res
    da = lax.dot_general(_mxu(g), _mxu(b), (((1,), (1,)), ((), ())), preferred_element_type=F32)
    db = lax.dot_general(_mxu(a), _mxu(g), (((0,), (0,)), ((), ())), preferred_element_type=F32)
    return da.astype(a.dtype), db.astype(b.dtype)


_bdot.defvjp(_bdot_fwd, _bdot_bwd)


def _sigmoid(x):
    return 1.0 / (1.0 + jnp.exp(-x))


def _softplus(x):
    return jnp.maximum(x, 0.0) + jnp.log(1.0 + jnp.exp(jnp.minimum(x, -x)))


def _norm_mod(x, gain, scale, shift):
    inv = lax.rsqrt(jnp.mean(x * x, axis=-1, keepdims=True) + RMS_EPS)
    return (x * inv) * gain * (1.0 + scale) + shift


def _prep(mixed, decay_w0, lora_up, iclr_a0, gate_up, k_k, k_a):
    r = mixed[:, 0:RW]
    k = mixed[:, RW:2 * RW]
    v = mixed[:, 2 * RW:3 * RW]
    z = mixed[:, 3 * RW:3 * RW + 128]
    xg = mixed[:, 3 * RW + 128:]
    lane = lax.broadcasted_iota(jnp.int32, z.shape, 1)
    tz = jnp.where(lane < 64, jnp.tanh(z), z)
    lo = _bdot(tz, lora_up)
    w_log = -_softplus(-(decay_w0 + lo[:, :RW])) - 0.5
    lw = -jnp.exp(w_log)
    a_ic = _sigmoid(iclr_a0 + lo[:, RW:])
    g = _bdot(_sigmoid(xg), gate_up)
    kk = k * k_k
    kk = kk / jnp.maximum(jnp.sqrt(_segsum(kk * kk)), 1e-12)
    k_mod = k * (1.0 + (a_ic - 1.0) * k_a)
    return jnp.concatenate([r, lw, k_mod, v, -kk, kk * a_ic, g], axis=1)


def _post(y, r, k, v, g, lnx_gain, lnx_bias, r_k):
    mu = _segsum(y) * (1.0 / HD)
    yc = y - mu
    var = _segsum(yc * yc) * (1.0 / HD)
    yn = yc * lax.rsqrt(var + GN_EPS) * lnx_gain + lnx_bias
    bonus = _segsum(r * k * r_k) * v
    return (yn + bonus) * g


def _merge(pg, ma, mb, bias):
    gates = _sigmoid(pg + bias)
    return gates[:, :D] * ma + gates[:, D:] * mb


def _swiglu(u, v):
    return u * _sigmoid(u) * v


def _ffn_in(h, w1, w3, *, name):
    T, K = h.shape
    ns, _, Fs = w1.shape
    tm = _pick(T, MM_ROWS)

    def body(h_ref, w1_ref, w3_ref, u_ref, v_ref, a_ref):
        u = jnp.dot(h_ref[...], w1_ref[...], preferred_element_type=F32).astype(MXU)
        v = jnp.dot(h_ref[...], w3_ref[...], preferred_element_type=F32).astype(MXU)
        u_ref[...] = u
        v_ref[...] = v
        a_ref[...] = _swiglu(u.astype(F32), v.astype(F32)).astype(MXU)

    wspec = pl.BlockSpec((None, K, Fs), lambda j, i: (j, 0, 0))
    ospec = pl.BlockSpec((None, tm, Fs), lambda j, i: (j, i, 0))
    return pl.pallas_call(
        body, name=name, grid=(ns, T // tm),
        in_specs=[pl.BlockSpec((tm, K), lambda j, i: (i, 0)), wspec, wspec],
        out_specs=[ospec] * 3, out_shape=[jax.ShapeDtypeStruct((ns, T, Fs), MXU)] * 3,
        compiler_params=_cparams(("arbitrary", "arbitrary")),
    )(h, w1, w3)


def _ffn_act_bwd(dff, w2, u, v, *, name):
    T, N = dff.shape
    ns, Fs, _ = w2.shape
    tm = _pick(T, MM_ROWS)

    def body(dy_ref, w_ref, u_ref, v_ref, du_ref, dv_ref):
        dact = lax.dot_general(dy_ref[...], w_ref[...], (((1,), (1,)), ((), ())), preferred_element_type=F32)
        _, vjp = jax.vjp(_swiglu, u_ref[...].astype(F32), v_ref[...].astype(F32))
        du, dv = vjp(dact)
        du_ref[...] = du.astype(MXU)
        dv_ref[...] = dv.astype(MXU)

    tile = pl.BlockSpec((None, tm, Fs), lambda j, i: (j, i, 0))
    return pl.pallas_call(
        body, name=name, grid=(ns, T // tm),
        in_specs=[pl.BlockSpec((tm, N), lambda j, i: (i, 0)), pl.BlockSpec((None, Fs, N), lambda j, i: (j, 0, 0)),
                  tile, tile],
        out_specs=[tile, tile], out_shape=[jax.ShapeDtypeStruct((ns, T, Fs), MXU)] * 2,
        compiler_params=_cparams(("arbitrary", "arbitrary")),
    )(dff, w2, u, v)


def _mm_tn_blocks(a, dy, *, name, out_dtype):
    a3, d3 = a.ndim == 3, dy.ndim == 3
    ns = a.shape[0] if a3 else dy.shape[0]
    T, K, N = a.shape[-2], a.shape[-1], dy.shape[-1]
    tm = _pick(T, MM_ROWS)
    n_t = T // tm

    def body(a_ref, dy_ref, o_ref, acc_ref):
        i = pl.program_id(1)

        @pl.when(i == 0)
        def _():
            acc_ref[...] = jnp.zeros_like(acc_ref)

        acc_ref[...] += lax.dot_general(a_ref[...], dy_ref[...], (((0,), (0,)), ((), ())), preferred_element_type=F32)

        @pl.when(i == n_t - 1)
        def _():
            o_ref[...] = acc_ref[...].astype(out_dtype)

    spec = lambda is3, n: (pl.BlockSpec((None, tm, n), lambda j, i: (j, i, 0)) if is3
                           else pl.BlockSpec((tm, n), lambda j, i: (i, 0)))
    return pl.pallas_call(
        body, name=name, grid=(ns, n_t), in_specs=[spec(a3, K), spec(d3, N)],
        out_specs=pl.BlockSpec((None, K, N), lambda j, i: (j, 0, 0)),
        out_shape=jax.ShapeDtypeStruct((ns, K, N), out_dtype), scratch_shapes=[pltpu.VMEM((K, N), F32)],
        compiler_params=_cparams(("arbitrary", "arbitrary")),
    )(a, dy)


@functools.partial(jax.custom_vjp, nondiff_argnums=(1,))
def _lane_roll(x, s):
    return pltpu.roll(x, s, 1)


def _lane_roll_fwd(x, s):
    return pltpu.roll(x, s, 1), None


def _lane_roll_bwd(s, _, g):
    n = g.shape[1]
    return (pltpu.roll(g, (n - s) % n, 1),)


_lane_roll.defvjp(_lane_roll_fwd, _lane_roll_bwd)


def _rope(x, cos, sin_lo, sin_hi):
    n = x.shape[1]
    return x * cos + _lane_roll(x, n - 8) * sin_lo + _lane_roll(x, 8) * sin_hi


def _head_rms(x, gain):
    return x * lax.rsqrt(_segsum(x * x) * (1.0 / HD) + RMS_EPS) * gain


def _attn_blocks(qkv_c, qkv_p, tab_c, tab_p, qg, kg, sinks, first):
    nb = qkv_c.shape[0] // BLK
    G = 4

    def tabs(tab, n):
        return [jnp.tile(tab[:, j * 128:(j + 1) * 128], (1, n // 128)) for j in range(3)]

    qg = jnp.concatenate([qg] * NH, axis=1)
    kg = jnp.concatenate([kg] * 2, axis=1)
    q = _rope(_head_rms(qkv_c[:, :RW], qg), *tabs(tab_c, RW))
    k_in = jnp.concatenate([qkv_p[:, RW:RW + 128], qkv_c[:, RW:RW + 128]], axis=0)
    k = _rope(_head_rms(k_in, kg), *tabs(jnp.concatenate([tab_p, tab_c], axis=0), 128))
    v = jnp.concatenate([qkv_p[:, RW + 128:], qkv_c[:, RW + 128:]], axis=0)

    pile = lambda xs: jnp.concatenate([x_[None] for x_ in xs], axis=0)

    def bands(t):
        return pile([t[b * BLK:(b + 2) * BLK, kvh * HD:(kvh + 1) * HD] for kvh in range(2) for b in range(nb)])

    qs = pile([jnp.concatenate([q[b * BLK:(b + 1) * BLK, (G * kvh + g) * HD:(G * kvh + g + 1) * HD]
                                for g in range(G)], axis=0) for kvh in range(2) for b in range(nb)])
    s = _bmm(qs, bands(k), 2, 2, 1) * (HD ** -0.5)
    qi = lax.broadcasted_iota(jnp.int32, (G * BLK, 2 * BLK), 0) % BLK
    kj = lax.broadcasted_iota(jnp.int32, (G * BLK, 2 * BLK), 1)
    dist = qi + BLK - kj
    in_band = (dist >= 0) & (dist < BLK)
    pair = lax.broadcasted_iota(jnp.int32, (2 * nb, 1, 1), 0)
    no_prev = (pair % nb == 0) & first
    valid = in_band[None] & (jnp.logical_not(no_prev) | (kj >= BLK)[None])
    s = jnp.where(valid, s, NEG_INF)
    row_g = lax.broadcasted_iota(jnp.int32, (G * BLK, 1), 0) // BLK
    sink = []
    for kvh in range(2):
        col = jnp.zeros((G * BLK, 1), F32)
        for g in range(G):
            col = jnp.where(row_g == g, sinks[:, G * kvh + g:G * kvh + g + 1], col)
        sink += [col] * nb
    sink = pile(sink)
    m = lax.stop_gradient(jnp.maximum(jnp.max(s, axis=-1, keepdims=True), sink))
    e = jnp.exp(s - m)
    p = e * (1.0 / (jnp.sum(e, axis=-1, keepdims=True) + jnp.exp(sink - m)))
    o = _bmm(p, bands(v), 2, 1, 1)
    return jnp.concatenate([jnp.concatenate([o[kvh * nb + b, g * BLK:(g + 1) * BLK] for kvh in range(2)
                                             for g in range(G)], axis=1) for b in range(nb)], axis=0)


def _heads(x):
    return jnp.stack([x[:, h * HD:(h + 1) * HD] for h in range(NH)], axis=0)


def _unheads(x):
    return jnp.concatenate([x[h] for h in range(NH)], axis=1)


def _split(x, n):
    parts, rest = [], x
    for _ in range(n):
        p = rest.astype(MXU)
        parts.append(p)
        rest = rest - p.astype(F32)
    return parts


def _bdot_batched(a, b, ca, cb):
    return lax.dot_general(a, b, (((ca,), (cb,)), ((0,), (0,))), preferred_element_type=F32)


def _bmm_passes(a, b, ca, cb, passes):
    if MXU == F32:
        return lax.dot_general(a, b, (((ca,), (cb,)), ((0,), (0,))), precision=HI, preferred_element_type=F32)
    if passes == 1:
        return _bdot_batched(a.astype(MXU), b.astype(MXU), ca, cb)
    (a0, a1), (b0, b1) = _split(a, 2), _split(b, 2)
    return _bdot_batched(a0, b0, ca, cb) + (_bdot_batched(a0, b1, ca, cb) + _bdot_batched(a1, b0, ca, cb))


@functools.partial(jax.custom_vjp, nondiff_argnums=(2, 3, 4))
def _bmm(a, b, ca, cb, passes=1):
    return _bmm_passes(a, b, ca, cb, passes)


def _bmm_fwd(a, b, ca, cb, passes):
    return _bmm_passes(a, b, ca, cb, passes), (a, b)


def _bmm_bwd(ca, cb, passes, res, g):
    a, b = res
    if (ca, cb) == (2, 1):
        return _bmm_passes(g, b, 2, 2, passes), _bmm_passes(a, g, 1, 1, passes)
    if (ca, cb) == (2, 2):
        return _bmm_passes(g, b, 2, 1, passes), _bmm_passes(g, a, 1, 1, passes)
    return _bmm_passes(b, g, 2, 2, passes), _bmm_passes(a, g, 2, 1, passes)


_bmm.defvjp(_bmm_fwd, _bmm_bwd)


def _tri_dot(x, transpose):
    C = x.shape[1]
    ri = lax.broadcasted_iota(jnp.int32, (C, C), 0)
    ci = lax.broadcasted_iota(jnp.int32, (C, C), 1)
    tri = jnp.broadcast_to(((ri <= ci) if transpose else (ri >= ci)).astype(MXU), (x.shape[0], C, C))
    if MXU == F32:
        return lax.dot_general(tri, x, (((2,), (1,)), ((0,), (0,))), precision=HI, preferred_element_type=F32)
    p0, p1, p2 = _split(x, 3)
    return _bdot_batched(tri, p0, 2, 1) + (_bdot_batched(tri, p1, 2, 1) + _bdot_batched(tri, p2, 2, 1))


@jax.custom_vjp
def _cumsum_rows(x):
    return _tri_dot(x, False)


def _cumsum_rows_fwd(x):
    return _tri_dot(x, False), None


def _cumsum_rows_bwd(_, g):
    return (_tri_dot(g, True),)


_cumsum_rows.defvjp(_cumsum_rows_fwd, _cumsum_rows_bwd)

P_SCORE = 1
P_SOLVE = 1
P_STATE = 1
SCAN_CHUNKS = (2, 1)


def _neumann(l):
    C = l.shape[1]
    eye = (lax.broadcasted_iota(jnp.int32, (C, C), 0) == lax.broadcasted_iota(jnp.int32, (C, C), 1)).astype(F32)
    x, lp = eye + l, l
    for _ in range(int(math.log2(C)) - 1):
        lp = _bmm(lp, lp, 2, 1, P_SOLVE)
        x = x + _bmm(x, lp, 2, 1, P_SOLVE)
    return x


@jax.custom_vjp
def _unit_lower_inverse(l):
    return _neumann(l)


def _unit_lower_inverse_fwd(l):
    x = _neumann(l)
    return x, x


def _unit_lower_inverse_bwd(x, g):
    return (_bmm(_bmm(x, g, 1, 1, P_SOLVE), x, 2, 2, P_SOLVE),)


_unit_lower_inverse.defvjp(_unit_lower_inverse_fwd, _unit_lower_inverse_bwd)


def _known_inverse(x):
    @jax.custom_vjp
    def f(l):
        return x

    f.defvjp(lambda l: (x, None), lambda _, g: (_bmm(_bmm(x, g, 1, 1, P_SOLVE), x, 2, 2, P_SOLVE),))
    return f


def _chunk(S0, r, lw, k, v, a, b, inverse=None):
    C = CHUNK
    n = r.shape[1] // C
    fold = lambda t: t.reshape(NH * n, C, HD)
    r, lw, k, v, a, b = (fold(t) for t in (r, lw, k, v, a, b))
    ri = lax.broadcasted_iota(jnp.int32, (C, C), 0)
    ci = lax.broadcasted_iota(jnp.int32, (C, C), 1)
    strict = (ri > ci)
    cum = _cumsum_rows(lw)
    p_in = jnp.exp(cum)
    p_ex = jnp.exp(cum - lw)
    p_inv = jnp.exp(-cum)
    at, rt, bt, kt = a * p_ex, r * p_in, b * p_inv, k * p_inv
    lhs = jnp.concatenate([at, rt], axis=1)
    rhs_ = jnp.concatenate([bt, kt], axis=1)
    sc = _bmm(lhs, rhs_, 2, 2, P_SCORE)
    a_ab = jnp.where(strict, sc[:, :C, :C], 0.0)
    a_ak = jnp.where(strict, sc[:, :C, C:], 0.0)
    incl2 = (lax.broadcasted_iota(jnp.int32, (C, 2 * C), 0) >= lax.broadcasted_iota(jnp.int32, (C, 2 * C), 1) % C)
    a_r = jnp.where(incl2, sc[:, C:, :], 0.0)
    av = _bmm(a_ak, v, 2, 1, P_SCORE)
    x = x_all = (_unit_lower_inverse if inverse is None else _known_inverse(inverse))(a_ab)
    p_last = jnp.exp(cum[:, C - 1:C, :])
    per_chunk = lambda t: t.reshape((NH, n) + t.shape[1:])
    lhs, rhs_, a_r, av, x, v, p_last = (per_chunk(t) for t in (lhs, rhs_, a_r, av, x, v, p_last))
    S, ys = S0, []
    for c in range(n):
        s0 = _bmm(lhs[:, c], S, 2, 2, P_STATE)
        u = _bmm(x[:, c], s0[:, :C] + av[:, c], 2, 1, P_SOLVE)
        uv = jnp.concatenate([u, v[:, c]], axis=1)
        ys.append(s0[:, C:] + _bmm(a_r[:, c], uv, 2, 1, P_SCORE))
        S = (S + _bmm(uv, rhs_[:, c], 1, 1, P_STATE)) * p_last[:, c]
    return jnp.concatenate(ys, axis=1), S, x_all


def _hosting(body, ex, n_in, n_out, n_scratch, n_steps):
    if ex is None:
        return body

    def wrapped(*refs):
        a = n_in
        b = a + ex.n_in
        c = b + n_out
        d = c + ex.n_out
        e = d + n_scratch
        ex_refs = (refs[a:b], refs[c:d], refs[e:])
        grid = n_steps if isinstance(n_steps, tuple) else (n_steps,)
        first = last = True
        for ax_, size in enumerate(grid):
            first = first & (pl.program_id(ax_) == 0)
            last = last & (pl.program_id(ax_) == size - 1)

        @pl.when(first)
        def _():
            ex.start(*ex_refs)

        body(*refs[:a], *refs[b:c], *refs[d:e])

        @pl.when(last)
        def _():
            ex.wait(*ex_refs)

    return wrapped


def _hosted_args(ex):
    if ex is None:
        return [], [], [], [], []
    any_spec = pl.BlockSpec(memory_space=pl.ANY)
    return list(ex.arrays), [any_spec] * ex.n_in, [any_spec] * ex.n_out, list(ex.out_shape), list(ex.scratch)


def _scan_fwd(rw, *, name, hosted=None):
    T = rw.shape[0]
    nc = _pick(T // CHUNK, SCAN_CHUNKS)
    rows = CHUNK * nc
    n = T // rows
    h_in, h_in_specs, h_out_specs, h_out_shape, h_scratch = _hosted_args(hosted)

    def body(r_ref, lw_ref, k_ref, v_ref, a_ref, b_ref, y_ref, ck_ref, inv_ref, s_ref):
        @pl.when(pl.program_id(0) == 0)
        def _():
            s_ref[...] = jnp.zeros_like(s_ref)

        S0 = s_ref[...]
        ck_ref[0] = S0
        y, S1, inv = _chunk(S0, *[_heads(ref[...]) for ref in (r_ref, lw_ref, k_ref, v_ref, a_ref, b_ref)])
        y_ref[...] = _unheads(y)
        inv_ref[0] = inv
        s_ref[...] = S1

    col = lambda j: pl.BlockSpec((rows, RW), lambda i: (i, j))
    return pl.pallas_call(
        _hosting(body, hosted, 6, 3, 1, n), name=name, grid=(n,),
        in_specs=[col(j) for j in range(6)] + h_in_specs,
        out_specs=[pl.BlockSpec((rows, RW), lambda i: (i, 0)),
                   pl.BlockSpec((1, NH, HD, HD), lambda i: (i, 0, 0, 0)),
                   pl.BlockSpec((1, NH * nc, CHUNK, CHUNK), lambda i: (i, 0, 0, 0))] + h_out_specs,
        out_shape=[jax.ShapeDtypeStruct((T, RW), F32), jax.ShapeDtypeStruct((n, NH, HD, HD), F32),
                   jax.ShapeDtypeStruct((n, NH * nc, CHUNK, CHUNK), F32)] + h_out_shape,
        scratch_shapes=[pltpu.VMEM((NH, HD, HD), F32)] + h_scratch,
        compiler_params=_cparams(("arbitrary",)),
    )(rw, rw, rw, rw, rw, rw, *h_in)


def _scan_bwd(rw, ck, inv, dy, *, name, hosted=None):
    T = rw.shape[0]
    nc = _pick(T // CHUNK, SCAN_CHUNKS)
    rows = CHUNK * nc
    n = T // rows

    def body(r_ref, lw_ref, k_ref, v_ref, a_ref, b_ref, ck_ref, inv_ref, dy_ref, o_ref, ds_ref):
        @pl.when(pl.program_id(0) == 0)
        def _():
            ds_ref[...] = jnp.zeros_like(ds_ref)

        prim = [_heads(ref[...]) for ref in (r_ref, lw_ref, k_ref, v_ref, a_ref, b_ref)]
        known = inv_ref[0]
        _, vjp = jax.vjp(lambda S0, *p: _chunk(S0, *p, inverse=known)[:2], ck_ref[0], *prim)
        grads = vjp((_heads(dy_ref[...]), ds_ref[...]))
        ds_ref[...] = grads[0]
        o_ref[...] = jnp.concatenate([_unheads(g) for g in grads[1:]], axis=1).astype(o_ref.dtype)

    h_in, h_in_specs, h_out_specs, h_out_shape, h_scratch = _hosted_args(hosted)
    col = lambda j: pl.BlockSpec((rows, RW), lambda i: (n - 1 - i, j))
    return pl.pallas_call(
        _hosting(body, hosted, 9, 1, 1, n), name=name, grid=(n,),
        in_specs=[col(j) for j in range(6)] + [pl.BlockSpec((1, NH, HD, HD), lambda i: (n - 1 - i, 0, 0, 0)),
                                               pl.BlockSpec((1, NH * nc, CHUNK, CHUNK), lambda i: (n - 1 - i, 0, 0, 0)),
                                               pl.BlockSpec((rows, RW), lambda i: (n - 1 - i, 0))] + h_in_specs,
        out_specs=[pl.BlockSpec((rows, 6 * RW), lambda i: (n - 1 - i, 0))] + h_out_specs,
        out_shape=[jax.ShapeDtypeStruct((T, 6 * RW), MXU)] + h_out_shape,
        scratch_shapes=[pltpu.VMEM((NH, HD, HD), F32)] + h_scratch,
        compiler_params=_cparams(("arbitrary",)),
    )(rw, rw, rw, rw, rw, rw, ck, inv, dy, *h_in)


ATTN_BLOCKS = (4, 2, 1)

def _attn_fwd(qkv, tab, qg, kg, sinks, *, name, hosted=None):
    T = qkv.shape[0]
    nb = _pick(T // BLK, ATTN_BLOCKS)
    n = T // (BLK * nb)
    h_in, h_in_specs, h_out_specs, h_out_shape, h_scratch = _hosted_args(hosted)

    def body(c_ref, p_ref, tc_ref, tp_ref, qg_ref, kg_ref, s_ref, o_ref):
        o_ref[...] = _attn_blocks(c_ref[...], p_ref[...], tc_ref[...], tp_ref[...], qg_ref[...], kg_ref[...],
                                  s_ref[...], pl.program_id(0) == 0).astype(o_ref.dtype)

    cur = lambda w: pl.BlockSpec((nb * BLK, w), lambda i: (i, 0))
    prev = lambda w: pl.BlockSpec((BLK, w), lambda i: (jnp.maximum(i * nb - 1, 0), 0))
    return pl.pallas_call(
        _hosting(body, hosted, 7, 1, 0, n), name=name, grid=(n,),
        in_specs=[cur(QKV_W), prev(QKV_W), cur(3 * 128), prev(3 * 128), _full_spec(qg), _full_spec(kg),
                  _full_spec(sinks)] + h_in_specs,
        out_specs=[cur(RW)] + h_out_specs, out_shape=[jax.ShapeDtypeStruct((T, RW), MXU)] + h_out_shape,
        scratch_shapes=h_scratch,
        compiler_params=_cparams(("arbitrary",)),
    )(qkv, qkv, tab, tab, qg, kg, sinks, *h_in)


def _attn_bwd(qkv, tab, qg, kg, sinks, dy, *, name, hosted=None):
    T = qkv.shape[0]
    nb = _pick(T // BLK, ATTN_BLOCKS)
    n = T // (BLK * nb)
    h_in, h_in_specs, h_out_specs, h_out_shape, h_scratch = _hosted_args(hosted)

    def body(c_ref, p_ref, tc_ref, tp_ref, qg_ref, kg_ref, s_ref, dy_ref, dqkv_ref, dqg_ref, dkg_ref, ds_ref, carry_ref):
        i = pl.program_id(0)

        @pl.when(i == 0)
        def _():
            carry_ref[...] = jnp.zeros_like(carry_ref)
            dqg_ref[...] = jnp.zeros_like(dqg_ref)
            dkg_ref[...] = jnp.zeros_like(dkg_ref)
            ds_ref[...] = jnp.zeros_like(ds_ref)

        tc, tp = tc_ref[...], tp_ref[...]
        f = lambda c, p_, qg_, kg_, sk: _attn_blocks(c, p_, tc, tp, qg_, kg_, sk, i == n - 1)
        _, vjp = jax.vjp(f, c_ref[...], p_ref[...], qg_ref[...], kg_ref[...], s_ref[...])
        dc, dp, dqg, dkg, dsk = vjp(dy_ref[...].astype(F32))
        last = slice((nb - 1) * BLK, nb * BLK)
        dqkv_ref[...] = dc.astype(dqkv_ref.dtype)
        dqkv_ref[last, :] = (dc[last] + carry_ref[...]).astype(dqkv_ref.dtype)
        carry_ref[...] = dp
        dqg_ref[...] += dqg
        dkg_ref[...] += dkg
        ds_ref[...] += dsk

    cur = lambda w: pl.BlockSpec((nb * BLK, w), lambda i: (n - 1 - i, 0))
    prev = lambda w: pl.BlockSpec((BLK, w), lambda i: (jnp.maximum((n - 1 - i) * nb - 1, 0), 0))
    return pl.pallas_call(
        _hosting(body, hosted, 8, 4, 1, n), name=name, grid=(n,),
        in_specs=[cur(QKV_W), prev(QKV_W), cur(3 * 128), prev(3 * 128), _full_spec(qg), _full_spec(kg), _full_spec(sinks),
                  cur(RW)] + h_in_specs,
        out_specs=[cur(QKV_W), _full_spec(qg), _full_spec(kg), _full_spec(sinks)] + h_out_specs,
        out_shape=[jax.ShapeDtypeStruct((T, QKV_W), MXU), jax.ShapeDtypeStruct(qg.shape, F32),
                   jax.ShapeDtypeStruct(kg.shape, F32), jax.ShapeDtypeStruct(sinks.shape, F32)] + h_out_shape,
        scratch_shapes=[pltpu.VMEM((BLK, QKV_W), F32)] + h_scratch,
        compiler_params=_cparams(("arbitrary",)),
    )(qkv, qkv, tab, tab, qg, kg, sinks, dy, *h_in)


def _shift_down(cur, prev8, i):
    rolled = pltpu.roll(cur, 1, 0)
    first_row = jnp.where(i > 0, prev8[7:8, :], 0.0)
    row = lax.broadcasted_iota(jnp.int32, cur.shape, 0)
    return jnp.where(row == 0, first_row, rolled)


def _shift_up(cur, next8, i, n):
    tm = cur.shape[0]
    rolled = pltpu.roll(cur, tm - 1, 0)
    last_row = jnp.where(i < n - 1, next8[0:1, :], 0.0)
    row = lax.broadcasted_iota(jnp.int32, cur.shape, 0)
    return jnp.where(row == tm - 1, last_row, rolled)


def _rope_table(positions):
    half = HD // 8
    inv_freq = 500000.0 ** (-jnp.arange(half, dtype=F32) / half)
    lane = jnp.arange(128) % HD
    rotary = lane < 2 * half
    freq = jnp.where(rotary, inv_freq[lane % half], 0.0)
    ang = positions.astype(F32)[:, None] * freq[None, :]
    cos, sin = jnp.cos(ang), jnp.sin(ang)
    return jnp.concatenate([jnp.where(rotary, cos, 1.0), jnp.where(lane < half, -sin, 0.0),
                            jnp.where(rotary & (lane >= half), sin, 0.0)], axis=1)


GATHER_BEHIND = {"f_proj_shift": [('ffn_w3', 512, 768)], "f_proj_gates": [('ffn_w3', 768, 1024)],
                 "f_prep": [('ffn_w2', 352, 704)],
                 "f_scan": [('ffn_w1', 0, 1024), ('w_branch_a', 0, 512), ('w_branch_b', 0, 512)],
                 "f_post": [('ffn_w3', 0, 512)], "f_attn": [('ffn_w2', 0, 352), ('w_out', 0, 256)]}
LATE = ['w_out', 'w_branch_a', 'w_branch_b', 'ffn_w1', 'ffn_w3', 'ffn_w2']
FFN = ('ffn_w1', 'ffn_w3', 'ffn_w2')
BACK_ATTN = ['w_out', 'w_branch_a', 'w_branch_b', 'ffn_w2']
BACK_SCAN = ['ffn_w1', 'ffn_w3']
BACK_LAST = ['w_in', 'decay_up', 'iclr_up', 'gate_up']


def _full_weight(g, ax):
    return g.reshape(-1, g.shape[2]) if ax == 0 else jnp.concatenate([g[j] for j in range(4)], axis=1)


def _local_step(x, target, ada, tab, w, s, shards=None):
    T = x.shape[0]
    tm = _pick(T, (512, 256, 128))
    tm_wide = _pick(T, (256, 128))
    tm_vjp = _pick(T, (256, 128))
    row = lambda n, dt=F32: (n, dt, 'row')
    acc = lambda n, r=1: (n, F32, r)

    def f_norm1(x_, g, ada_):
        return _norm_mod(x_, g, ada_[:, D:2 * D], ada_[:, 0:D])

    landed = {}

    def behind(kernel_name):
        if not shards:
            return None
        pieces = GATHER_BEHIND[kernel_name]
        return _GatherChipsHalved([shards[n] for n, _, _ in pieces], rows=[(lo, hi) for _, lo, hi in pieces])

    def took(kernel_name, got):
        landed.update(zip(GATHER_BEHIND[kernel_name], got))

    def mm_behind(a_, w_, kernel_name, **kw):
        ex = behind(kernel_name)
        res = _mm_nn(a_, w_, name=kernel_name, hosted=ex, **kw)
        if ex:
            took(kernel_name, res[1:])
            return res[0]
        return res

    h1, proj, *got = _mm_then([(None, w['w_in'][:, :SHIFT_W], 'nn')], lambda i, n, y, h, *_: (h, y), [x],
                              [s['norm1_gain'], ada], [row(D, MXU), row(SHIFT_W)], tm=tm, name="f_proj_shift",
                              lhs_fn=f_norm1, hosted=behind("f_proj_shift"))
    took("f_proj_shift", got)
    proj_qkv = _mm_nn(h1, w['w_in'][:, SHIFT_W:SHIFT_W + QKV_W], name="f_proj_qkv")
    proj_g = mm_behind(h1, w['w_in'][:, SHIFT_W + QKV_W:], "f_proj_gates", out_dtype=MXU)
    prep_consts = [s['decay_w0'], s['lora_up'], s['iclr_a0'], s['gate_up'], s['k_k'], s['k_a']]

    def f_prep(i, n, cur, prev8, mu, *params):
        mixed = cur + (_shift_down(cur, prev8, i) - cur) * mu
        return (_prep(mixed, *params),)
    rw, *got = _rowwise(f_prep, [(proj, SHIFT_W)], [s['tshift_mu']] + prep_consts, [row(7 * RW)], tm=tm_wide,
                        name="f_prep", halo=[(proj, SHIFT_W, 'prev')], hosted=behind("f_prep"))
    took("f_prep", got)
    y, ck, inv, *got = _scan_fwd(rw, name="f_scan", hosted=behind("f_scan"))
    took("f_scan", got)
    post_consts = [s['lnx_gain'], s['lnx_bias'], s['r_k']]

    rkvg = [(rw, RW, j) for j in (0, 2, 3, 6)]

    def f_post(i, n, *args):
        return (_post(*args),)
    ya, *got = _rowwise(f_post, [y] + rkvg, post_consts, [row(RW, MXU)], tm=tm_wide, name="f_post",
                        hosted=behind("f_post"))
    took("f_post", got)
    yb, *got = _attn_fwd(proj_qkv, tab, s['q_norm_gain'], s['k_norm_gain'], s['attn_sinks'], name="f_attn",
                         hosted=behind("f_attn"))
    took("f_attn", got)
    w = dict(w)
    if shards:
        ax = dict(SHARDED)
        for n in LATE:
            rows = [landed[key] for key in sorted(k_ for k_ in landed if k_[0] == n)]
            blocks = rows[0] if len(rows) == 1 else jnp.concatenate(rows, axis=1)
            w[n] = blocks if n in FFN else _full_weight(blocks, ax[n])
    else:
        fs = DFF // 4
        w.update({n: w[n].reshape(D, 4, fs).transpose(1, 0, 2) for n in ('ffn_w1', 'ffn_w3')})
        w['ffn_w2'] = w['ffn_w2'].reshape(4, fs, D)
    def f_merge(pg, ya_, yb_, x_, wa, wb, bias, g, ada_):
        ma_ = jnp.dot(ya_, wa, preferred_element_type=F32).astype(MXU)
        mb_ = jnp.dot(yb_, wb, preferred_element_type=F32).astype(MXU)
        return _merge(pg.astype(F32), ma_.astype(F32), mb_.astype(F32), bias), ma_, mb_

    def f_res1(i, n, mo_, merged_, ma_, mb_, pg, ya_, yb_, x_, wa, wb, bias, g, ada_):
        x1_ = x_ + ada_[:, 2 * D:3 * D] * mo_
        return merged_, ma_, mb_, mo_, x1_, _norm_mod(x1_, g, ada_[:, 4 * D:5 * D], ada_[:, 3 * D:4 * D])
    merged, ma, mb, mo, x1, h2 = _mm_then(
        [(None, w['w_out'], 'nn')], f_res1, [proj_g, ya, yb, x],
        [w['w_branch_a'], w['w_branch_b'], s['branch_gate_b'], s['norm2_gain'], ada],
        [row(D, MXU), row(D, MXU), row(D, MXU), row(D), row(D), row(D, MXU)], tm=tm, name="f_out", lhs_fn=f_merge)
    u, v, act = _ffn_in(h2, w['ffn_w1'], w['ffn_w3'], name="f_ffn_in")

    def f_loss(i, n, ff_, x1_, tgt, ada_):
        g2 = ada_[:, 5 * D:6 * D]
        err = x1_ + g2 * ff_ - tgt
        dx2 = err * (1.0 / D)
        loss = 0.5 * jnp.sum(jnp.sum(err * err, axis=1, keepdims=True) * (1.0 / D), axis=0, keepdims=True)
        return dx2, (dx2 * g2), jnp.broadcast_to(loss, (1, 128)), jnp.sum(dx2 * ff_, axis=0, keepdims=True)
    dx2, dff, loss, dgate2 = _mm_then([(act, w['ffn_w2'], 'nn', j) for j in range(4)], f_loss, [x1, target], [ada],
                                      [row(D), row(D, MXU), acc(128), acc(D)], tm=tm, name="f_ffn_out")

    du, dv = _ffn_act_bwd(dff, w['ffn_w2'], u, v, name="b_ffn_out_dx")
    g_w2 = _mm_tn_blocks(act, dff, name="b_ffn_out_dw", out_dtype=MXU)
    g_w1 = _mm_tn_blocks(h2, du, name="b_ffn_w1_dw", out_dtype=MXU)
    g_w3 = _mm_tn_blocks(h2, dv, name="b_ffn_w3_dw", out_dtype=MXU)

    def b_res1(i, n, dh2_, x1_, dx2_, mo_, g, ada_):
        _, vjp = jax.vjp(_norm_mod, x1_, g, ada_[:, 4 * D:5 * D], ada_[:, 3 * D:4 * D])
        dxn, dg, dsc, dsh = vjp(dh2_)
        dx1_ = dxn + dx2_
        g1 = ada_[:, 2 * D:3 * D]
        return dx1_, dx1_ * g1, dg, dsc, dsh, jnp.sum(dx1_ * mo_, axis=0, keepdims=True)
    dx1, dmo, d_gain2, d_scale2, d_shift2, dgate1 = _mm_then(
        [(t, w[n], 'nt', j) for t, n in ((du, 'ffn_w1'), (dv, 'ffn_w3')) for j in range(4)], b_res1, [x1, dx2, mo],
        [s['norm2_gain'], ada],
        [row(D), row(D, MXU), acc(D), acc(D), acc(D), acc(D)], tm=tm_wide, name="b_ffn_in_dx")
    g_wout = _mm_tn(merged, dmo, name="b_out_dw", out_dtype=MXU)

    def b_merge(i, n, dm, pg, ma_, mb_, wa, wb, bias):
        _, vjp = jax.vjp(_merge, pg.astype(F32), ma_.astype(F32), mb_.astype(F32), bias)
        dpg, dma_, dmb_, dbias = vjp(dm)
        dma_, dmb_ = dma_.astype(MXU), dmb_.astype(MXU)
        nt = lambda a_, b_: lax.dot_general(a_, b_, (((1,), (1,)), ((), ())), preferred_element_type=F32)
        return dpg, dma_, dmb_, nt(dma_, wa), nt(dmb_, wb), dbias
    dpg, dma, dmb, dya, dyb, d_bias = _mm_then(
        [(dmo, w['w_out'], 'nt')], b_merge, [proj_g, ma, mb], [w['w_branch_a'], w['w_branch_b'], s['branch_gate_b']],
        [row(GATE_W, MXU), row(D, MXU), row(D, MXU), row(RW), row(RW), acc(GATE_W)], tm=tm, name="b_out_dx")
    g_wa = _mm_tn(ya, dma, name="b_branch_a_dw", out_dtype=MXU, col_shards=4)
    g_wb = _mm_tn(yb, dmb, name="b_branch_b_dw", out_dtype=MXU, col_shards=4)
    gw = dict(w_branch_a=g_wa, w_branch_b=g_wb, w_out=g_wout.reshape(4, D // 4, D), ffn_w1=g_w1, ffn_w3=g_w3,
              ffn_w2=g_w2)
    recv = {}
    dqkv, d_qg, d_kg, d_sinks, *got = _attn_bwd(
        proj_qkv, tab, s['q_norm_gain'], s['k_norm_gain'], s['attn_sinks'], dyb, name="b_attn",
        hosted=shards and _ScatterChips([gw[n] for n in BACK_ATTN]))
    recv.update(zip(BACK_ATTN, got))

    def b_post(i, n, y_, r_, k_, v_, g_, dya_, *params):
        _, vjp = jax.vjp(_post, y_, r_, k_, v_, g_, *params)
        dy_, dr_, dk_, dv_, dg_, *dparams = vjp(dya_)
        return (dy_, jnp.concatenate([dr_, dk_, dv_, dg_], axis=1), *dparams)
    dy, drkvg, d_lnx_gain, d_lnx_bias, d_r_k = _rowwise(
        b_post, [y] + rkvg + [dya], post_consts, [row(RW), row(4 * RW, MXU), acc(RW), acc(RW), acc(RW)], tm=tm_wide,
        name="b_post")
    dscan, *got = _scan_bwd(rw, ck, inv, dy, name="b_scan",
                            hosted=shards and _ScatterChips([gw[n] for n in BACK_SCAN]))
    recv.update(zip(BACK_SCAN, got))

    def b_prep(i, n, cur, drw_, dscan_, prev8, mu, *params):
        shifted = _shift_down(cur, prev8, i)
        mixed = cur + (shifted - cur) * mu
        _, vjp = jax.vjp(_prep, mixed, *params)
        blk = lambda t, j: t[:, j * RW:(j + 1) * RW].astype(F32)
        ct = jnp.concatenate([blk(dscan_, 0) + blk(drw_, 0), blk(dscan_, 1), blk(dscan_, 2) + blk(drw_, 1),
                              blk(dscan_, 3) + blk(drw_, 2), blk(dscan_, 4), blk(dscan_, 5), blk(drw_, 3)], axis=1)
        grads = vjp(ct)
        dmixed = grads[0]
        return (dmixed, jnp.sum(dmixed * (shifted - cur), axis=0, keepdims=True)) + tuple(grads[1:])
    dmixed, d_mu, d_w0, d_lora, d_a0, d_gate_up, d_kk, d_ka = _rowwise(
        b_prep, [(proj, SHIFT_W), drkvg, dscan], [s['tshift_mu']] + prep_consts,
        [row(SHIFT_W), acc(SHIFT_W), acc(RW), acc(2 * RW, 128), acc(RW), acc(RW, 128), acc(RW), acc(RW)],
        tm=tm_vjp, name="b_prep", halo=[(proj, SHIFT_W, 'prev')])

    def b_gather(i, n, dm, dqkv_, dpg_, next8, mu):
        dcur = dm * (1.0 - mu) + _shift_up(dm, next8, i, n) * mu
        return (jnp.concatenate([dcur.astype(MXU), dqkv_, dpg_], axis=1),)
    (dproj,) = _rowwise(b_gather, [dmixed, dqkv, dpg], [s['tshift_mu']], [row(IN_W, MXU)], tm=tm_wide, name="b_gather",
                        halo=[(dmixed, SHIFT_W, 'next')])
    g_win = _mm_tn(h1, dproj, name="b_proj_dw", out_dtype=MXU, col_shards=4)

    def col_blocks(g):
        k, n = g.shape
        return g.reshape(k, 4, n // 4).transpose(1, 0, 2).astype(MXU)
    gw.update(w_in=g_win, decay_up=col_blocks(d_lora[:64, :RW]), iclr_up=col_blocks(d_lora[64:, RW:]),
              gate_up=col_blocks(d_gate_up))
    top, bottom = None, None
    if shards:
        top = _ScatterChips([gw[n] for n in BACK_LAST], rows=[(0, D // 2)] + [None] * (len(BACK_LAST) - 1))
        bottom = _ScatterChips([gw['w_in']], rows=[(D // 2, 3 * D // 4)])
        dh1, *got_top = _mm_nt(dproj, w['w_in'], name="b_proj_dx", hosted=top)
    else:
        dh1 = _mm_nt(dproj, w['w_in'], name="b_proj_dx")

    def b_norm1(i, n, x_, dh1_, dx1_, g, ada_):
        _, vjp = jax.vjp(_norm_mod, x_, g, ada_[:, D:2 * D], ada_[:, 0:D])
        dxn, dg, dsc, dsh = vjp(dh1_)
        return dxn + dx1_, dg, dsc, dsh
    dx, d_gain1, d_scale1, d_shift1, *got_bottom = _rowwise(
        b_norm1, [x, dh1, dx1], [s['norm1_gain'], ada], [row(D), acc(D), acc(D), acc(D)], tm=tm, name="b_norm1",
        hosted=bottom)
    if shards:
        recv.update(zip(BACK_LAST[1:], got_top[1:]))
        recv['w_in'] = [got_top[0], got_bottom[0]]

    d_ada = jnp.concatenate([d_shift1, d_scale1, dgate1, d_shift2, d_scale2, dgate2], axis=1)
    gs = dict(norm1_gain=d_gain1, norm2_gain=d_gain2, tshift_mu=d_mu, decay_w0=d_w0, iclr_a0=d_a0, k_k=d_kk, k_a=d_ka,
              r_k=d_r_k, lnx_gain=d_lnx_gain, lnx_bias=d_lnx_bias, q_norm_gain=d_qg, k_norm_gain=d_kg,
              attn_sinks=d_sinks, branch_gate_b=d_bias)
    return loss, dx, d_ada, gw, gs, recv


ANY = pl.BlockSpec(memory_space=pl.ANY)


def _place():
    x, y, c = lax.axis_index("x"), lax.axis_index("y"), lax.axis_index("c")
    return x, y, c, [(1 - x, y), (x, 1 - y), (1 - x, 1 - y)]


def _all_gather8(x_shard, *, name):
    m_per, n = x_shard.shape

    def body(x_ref, out_ref, send_sems, recv_sems, local_sem):
        x, y, c, chips = _place()
        me, sibling = (x, y, c), (x, y, 1 - c)

        def rows(px, py, pc):
            return out_ref.at[pl.ds((4 * px + 2 * py + pc) * m_per, m_per), :]

        def copy(k, block, to, src=None):
            return pltpu.make_async_remote_copy(
                src_ref=rows(*block) if src is None else src, dst_ref=rows(*block),
                send_sem=send_sems.at[k], recv_sem=recv_sems.at[k], device_id=to, device_id_type=MESH)

        mine = pltpu.make_async_copy(x_ref, rows(*me), local_sem)
        mine.start()
        first = [copy(0, me, sibling, src=x_ref)]
        first += [copy(1 + j, me, (*chip, c), src=x_ref) for j, chip in enumerate(chips)]
        for cp in first:
            cp.start()
        passed = [copy(4 + j, (*chip, c), sibling) for j, chip in enumerate(chips)]
        for j, chip in enumerate(chips):
            copy(1 + j, (*chip, c), me).wait_recv()
            passed[j].start()
        copy(0, sibling, me).wait_recv()
        for j, chip in enumerate(chips):
            copy(4 + j, (*chip, 1 - c), me).wait_recv()
        for cp in first + passed:
            cp.wait_send()
        mine.wait()

    return pl.pallas_call(
        body, name=name, out_shape=jax.ShapeDtypeStruct((8 * m_per, n), x_shard.dtype),
        in_specs=[pl.BlockSpec(memory_space=pltpu.VMEM)], out_specs=pl.BlockSpec(memory_space=pltpu.VMEM),
        scratch_shapes=[pltpu.SemaphoreType.DMA((7,)), pltpu.SemaphoreType.DMA((7,)), pltpu.SemaphoreType.DMA],
    )(x_shard)


class _GatherChips:
    def __init__(self, shards, rows=None):
        n = len(shards)
        self.rows = [r or (0, s.shape[0]) for s, r in zip(shards, rows or [None] * n, strict=True)]
        self.arrays, self.n_in, self.n_out = list(shards), n, n
        self.out_shape = [jax.ShapeDtypeStruct((4, hi - lo) + s.shape[1:], s.dtype)
                          for s, (lo, hi) in zip(shards, self.rows, strict=True)]
        self.scratch = [pltpu.SemaphoreType.DMA((3 * n,)), pltpu.SemaphoreType.DMA((3 * n,)),
                        pltpu.SemaphoreType.DMA((n,))]

    def _copies(self, x_refs, out_refs, sems, receiving):
        send_sems, recv_sems, local_sems = sems
        x, y, c, chips = _place()
        s_me = 2 * x + y
        n = self.n_in
        src = [x_refs[a].at[pl.ds(lo, hi - lo)] for a, (lo, hi) in enumerate(self.rows)]

        def copy(a, k, s):
            return pltpu.make_async_remote_copy(
                src_ref=src[a], dst_ref=out_refs[a].at[s], send_sem=send_sems.at[3 * a + k],
                recv_sem=recv_sems.at[3 * a + k], device_id=(*chips[k], c), device_id_type=MESH)

        mine = [pltpu.make_async_copy(src[a], out_refs[a].at[s_me], local_sems.at[a]) for a in range(n)]
        sends = [copy(a, k, s_me) for a in range(n) for k in range(3)]
        if not receiving:
            return mine, sends
        return mine, sends, [copy(a, k, 2 * px + py) for a in range(n) for k, (px, py) in enumerate(chips)]

    def start(self, x_refs, out_refs, sems):
        mine, sends = self._copies(x_refs, out_refs, sems, False)
        for cp in mine + sends:
            cp.start()

    def wait(self, x_refs, out_refs, sems):
        mine, sends, recvs = self._copies(x_refs, out_refs, sems, True)
        for cp in recvs:
            cp.wait_recv()
        for cp in sends:
            cp.wait_send()
        for cp in mine:
            cp.wait()


class _GatherChipsHalved(_GatherChips):
    def __init__(self, shards, rows=None):
        super().__init__(shards, rows)
        n = self.n_in
        self.scratch = [pltpu.SemaphoreType.DMA((6 * n,)), pltpu.SemaphoreType.DMA((6 * n,)),
                        pltpu.SemaphoreType.DMA((n,))]

    def _copies(self, x_refs, out_refs, sems, receiving):
        send_sems, recv_sems, local_sems = sems
        x, y, c, chips = _place()
        s_me = 2 * x + y
        n = self.n_in

        def half(a, who, first=0):
            lo, hi = self.rows[a]
            return pl.ds(first + who * ((hi - lo) // 2), (hi - lo) // 2)

        def over_chips(a, k, s):
            return pltpu.make_async_remote_copy(
                src_ref=x_refs[a].at[half(a, c, self.rows[a][0])], dst_ref=out_refs[a].at[s, half(a, c)],
                send_sem=send_sems.at[3 * a + k], recv_sem=recv_sems.at[3 * a + k],
                device_id=(*chips[k], c), device_id_type=MESH)

        def to_sibling(a, k, s, who):
            return pltpu.make_async_remote_copy(
                src_ref=out_refs[a].at[s, half(a, who)], dst_ref=out_refs[a].at[s, half(a, who)],
                send_sem=send_sems.at[3 * n + 3 * a + k], recv_sem=recv_sems.at[3 * n + 3 * a + k],
                device_id=(x, y, 1 - c), device_id_type=MESH)

        mine = [pltpu.make_async_copy(x_refs[a].at[pl.ds(lo, hi - lo)], out_refs[a].at[s_me], local_sems.at[a])
                for a, (lo, hi) in enumerate(self.rows)]
        sends = [over_chips(a, k, s_me) for a in range(n) for k in range(3)]
        if not receiving:
            return mine, sends
        pairs = [(a, k, 2 * px + py) for a in range(n) for k, (px, py) in enumerate(chips)]
        landed = [over_chips(a, k, s) for a, k, s in pairs]
        passed_on = [to_sibling(a, k, s, c) for a, k, s in pairs]
        from_sibling = [to_sibling(a, k, s, 1 - c) for a, k, s in pairs]
        return mine, sends, landed, passed_on, from_sibling

    def wait(self, x_refs, out_refs, sems):
        mine, sends, landed, passed_on, from_sibling = self._copies(x_refs, out_refs, sems, True)
        for got, fwd in zip(landed, passed_on, strict=True):
            got.wait_recv()
            fwd.start()
        for cp in from_sibling:
            cp.wait_recv()
        for cp in sends + passed_on:
            cp.wait_send()
        for cp in mine:
            cp.wait()


class _ScatterChips:
    def __init__(self, parts, rows=None):
        n = len(parts)
        self.rows = [r or (0, p.shape[1]) for p, r in zip(parts, rows or [None] * n, strict=True)]
        self.arrays, self.n_in, self.n_out = list(parts), n, n
        self.out_shape = [jax.ShapeDtypeStruct((3, hi - lo) + p.shape[2:], p.dtype)
                          for p, (lo, hi) in zip(parts, self.rows, strict=True)]
        self.scratch = [pltpu.SemaphoreType.DMA((3 * n,)), pltpu.SemaphoreType.DMA((3 * n,))]

    def _copies(self, g_refs, out_refs, sems):
        send_sems, recv_sems = sems
        x, y, c, chips = _place()
        return [pltpu.make_async_remote_copy(
            src_ref=g_refs[a].at[2 * px + py, pl.ds(lo, hi - lo)], dst_ref=out_refs[a].at[k],
            send_sem=send_sems.at[3 * a + k], recv_sem=recv_sems.at[3 * a + k], device_id=(px, py, c),
            device_id_type=MESH)
            for a, (lo, hi) in enumerate(self.rows) for k, (px, py) in enumerate(chips)]

    def start(self, g_refs, out_refs, sems):
        for cp in self._copies(g_refs, out_refs, sems):
            cp.start()

    def wait(self, g_refs, out_refs, sems):
        sends = self._copies(g_refs, out_refs, sems)
        for cp in sends:
            cp.wait_recv()
        for cp in sends:
            cp.wait_send()


def _exchange_call(ex, *, name):
    def body(*refs):
        parts = (refs[:ex.n_in], refs[ex.n_in:ex.n_in + ex.n_out], refs[ex.n_in + ex.n_out:])
        ex.start(*parts)
        ex.wait(*parts)

    return pl.pallas_call(body, name=name, out_shape=ex.out_shape, in_specs=[ANY] * ex.n_in,
                          out_specs=[ANY] * ex.n_out, scratch_shapes=ex.scratch)(*ex.arrays)


class _SwapSibling:
    def __init__(self, vs):
        n = len(vs)
        self.arrays, self.n_in, self.n_out = list(vs), n, n
        self.out_shape = [jax.ShapeDtypeStruct(v.shape, v.dtype) for v in vs]
        self.scratch = [pltpu.SemaphoreType.DMA((n,)), pltpu.SemaphoreType.DMA((n,))]

    def _copies(self, v_refs, out_refs, sems):
        send_sems, recv_sems = sems
        x, y, c, _ = _place()
        return [pltpu.make_async_remote_copy(src_ref=v_refs[a], dst_ref=out_refs[a], send_sem=send_sems.at[a],
                                             recv_sem=recv_sems.at[a], device_id=(x, y, 1 - c), device_id_type=MESH)
                for a in range(self.n_in)]

    def start(self, v_refs, out_refs, sems):
        for cp in self._copies(v_refs, out_refs, sems):
            cp.start()

    def wait(self, v_refs, out_refs, sems):
        for cp in self._copies(v_refs, out_refs, sems):
            cp.wait()


class _Both:
    def __init__(self, first, second):
        self.parts = (first, second)
        self.arrays = first.arrays + second.arrays
        self.n_in, self.n_out = first.n_in + second.n_in, first.n_out + second.n_out
        self.out_shape = first.out_shape + second.out_shape
        self.scratch = first.scratch + second.scratch

    def _split(self, in_refs, out_refs, sems):
        a, b = self.parts
        return ((a, in_refs[:a.n_in], out_refs[:a.n_out], sems[:len(a.scratch)]),
                (b, in_refs[a.n_in:], out_refs[a.n_out:], sems[len(a.scratch):]))

    def start(self, in_refs, out_refs, sems):
        for ex, *refs in self._split(in_refs, out_refs, sems):
            ex.start(*refs)

    def wait(self, in_refs, out_refs, sems):
        for ex, *refs in self._split(in_refs, out_refs, sems):
            ex.wait(*refs)


def _sum_parts(own, others, *, name):
    R, C = own.shape
    tm = _pick(R, (256, 128, 64))

    def body(own_ref, o0_ref, o1_ref, o2_ref, out_ref):
        tot = own_ref[...].astype(F32)
        for ref in (o0_ref, o1_ref, o2_ref):
            tot = tot + ref[...].astype(F32)
        out_ref[...] = tot

    part = lambda k: pl.BlockSpec((None, tm, C), lambda i: (k, i, 0))
    return pl.pallas_call(
        body, name=name, grid=(R // tm,),
        in_specs=[pl.BlockSpec((tm, C), lambda i: (i, 0)), part(0), part(1), part(2)],
        out_specs=pl.BlockSpec((tm, C), lambda i: (i, 0)), out_shape=jax.ShapeDtypeStruct((R, C), F32),
        compiler_params=_cparams(("arbitrary",)),
    )(own, others, others, others)


def _adam_math(w_, m_, v_, g):
    m2 = ADAM_B1 * m_ + (1.0 - ADAM_B1) * g
    v2 = ADAM_B2 * v_ + (1.0 - ADAM_B2) * jnp.square(g)
    m_hat = m2 / (1.0 - ADAM_B1 ** ADAM_STEP)
    v_hat = v2 / (1.0 - ADAM_B2 ** ADAM_STEP)
    delta = -ADAM_LR * (m_hat / (jnp.sqrt(v_hat) + ADAM_EPS) + ADAM_WD * w_)
    return delta, m2, v2


SMALL_SLOTS = 24
SMALL_COLS = 2 * D


def _small_rows(widths):
    firsts, row = [], 0
    for n_i in widths:
        firsts.append(row)
        row += -(-n_i // SMALL_COLS)
    assert row <= SMALL_SLOTS
    return firsts


def _pack_small(grads, *, name):
    n = len(grads)
    firsts = _small_rows([g.shape[1] for g in grads])

    def body(*refs):
        out_ref = refs[n]
        out_ref[...] = jnp.zeros_like(out_ref)
        for first, ref in zip(firsts, refs[:n], strict=True):
            for lo in range(0, ref.shape[1], SMALL_COLS):
                width = min(SMALL_COLS, ref.shape[1] - lo)
                row = first + lo // SMALL_COLS
                out_ref[row:row + 1, 0:width] = ref[:, lo:lo + width]

    return pl.pallas_call(body, name=name, out_shape=jax.ShapeDtypeStruct((SMALL_SLOTS, SMALL_COLS), F32))(*grads)


def _adamw_small(ws, ms, vs, gathered, *, name):
    n = len(ws)
    firsts = _small_rows([w.shape[1] for w in ws] + [128])

    def total(g_ref, first, nc):
        pieces = []
        for lo in range(0, nc, SMALL_COLS):
            width, row = min(SMALL_COLS, nc - lo), first + lo // SMALL_COLS
            g = g_ref[row:row + 1, 0:width]
            for d in range(1, 8):
                g = g + g_ref[d * SMALL_SLOTS + row:d * SMALL_SLOTS + row + 1, 0:width]
            pieces.append(g)
        return pieces[0] if len(pieces) == 1 else jnp.concatenate(pieces, axis=1)

    def body(*refs):
        w_refs, m_refs, v_refs, g_ref = refs[:n], refs[n:2 * n], refs[2 * n:3 * n], refs[3 * n]
        outs = refs[3 * n + 1:]
        for i in range(n):
            g = total(g_ref, firsts[i], w_refs[i].shape[1])
            delta, m2, v2 = _adam_math(w_refs[i][...], m_refs[i][...], v_refs[i][...], g)
            for k, val in enumerate((g, delta, m2, v2)):
                outs[k * n + i][...] = val
        outs[4 * n][...] = total(g_ref, firsts[n], 128)

    shapes = [jax.ShapeDtypeStruct(w.shape, F32) for w in ws]
    res = pl.pallas_call(body, name=name, out_shape=shapes * 4 + [jax.ShapeDtypeStruct((1, 128), F32)],
                         compiler_params=pltpu.CompilerParams(vmem_limit_bytes=VMEM_LIMIT))(*ws, *ms, *vs, gathered)
    return [res[k * n:(k + 1) * n] for k in range(4)], res[4 * n]


def _adamw(w, m, v, gparts, *, tm, name, hosted=None):
    def fn(i, n, w_, m_, v_, *gs):
        g = gs[0]
        for p in gs[1:]:
            g = g + p
        return (g,) + _adam_math(w_, m_, v_, g)
    nc = w.shape[1]
    return _rowwise(fn, [w, m, v] + list(gparts), [], [(nc, F32, 'row')] * 4, tm=tm, name=name, hosted=hosted)


WEIGHTS = ['ada_w', 'ada_b', 'norm1_gain', 'norm2_gain', 'w_in', 'tshift_mu', 'decay_w0', 'decay_up', 'iclr_a0',
           'iclr_up', 'gate_up', 'k_k', 'k_a', 'r_k', 'lnx_gain', 'lnx_bias', 'q_norm_gain', 'k_norm_gain', 'attn_sinks',
           'branch_gate_b', 'w_branch_a', 'w_branch_b', 'w_out', 'ffn_w1', 'ffn_w3', 'ffn_w2']
SHARDED = [('w_in', 1), ('decay_up', 1), ('iclr_up', 1), ('gate_up', 1), ('w_branch_a', 1), ('w_branch_b', 1),
           ('w_out', 0), ('ffn_w1', 1), ('ffn_w3', 1), ('ffn_w2', 0)]
SMALL = ['ada_b', 'norm1_gain', 'norm2_gain', 'tshift_mu', 'decay_w0', 'iclr_a0', 'k_k', 'k_a', 'r_k', 'lnx_gain',
         'lnx_bias', 'q_norm_gain', 'k_norm_gain', 'attn_sinks', 'branch_gate_b']


def kernel(x, c, positions, ada_w, ada_b, norm1_gain, norm2_gain, w_in, tshift_mu, decay_w0, decay_up, iclr_a0, iclr_up, gate_up, k_k, k_a, r_k, lnx_gain, lnx_bias, q_norm_gain, k_norm_gain, attn_sinks, branch_gate_b, w_branch_a, w_branch_b, w_out, ffn_w1, ffn_w3, ffn_w2, loss_target, m_ada_w, m_ada_b, m_norm1_gain, m_norm2_gain, m_w_in, m_tshift_mu, m_decay_w0, m_decay_up, m_iclr_a0, m_iclr_up, m_gate_up, m_k_k, m_k_a, m_r_k, m_lnx_gain, m_lnx_bias, m_q_norm_gain, m_k_norm_gain, m_attn_sinks, m_branch_gate_b, m_w_branch_a, m_w_branch_b, m_w_out, m_ffn_w1, m_ffn_w3, m_ffn_w2, v_ada_w, v_ada_b, v_norm1_gain, v_norm2_gain, v_w_in, v_tshift_mu, v_decay_w0, v_decay_up, v_iclr_a0, v_iclr_up, v_gate_up, v_k_k, v_k_a, v_r_k, v_lnx_gain, v_lnx_bias, v_q_norm_gain, v_k_norm_gain, v_attn_sinks, v_branch_gate_b, v_w_branch_a, v_w_branch_b, v_w_out, v_ffn_w1, v_ffn_w3, v_ffn_w2):
    a = dict(locals())
    W = {n: a[n] for n in WEIGHTS}
    M = {n: a['m_' + n] for n in WEIGHTS}
    V = {n: a['v_' + n] for n in WEIGHTS}
    xi, yi, ci = lax.axis_index("x"), lax.axis_index("y"), lax.axis_index("c")
    me = 4 * xi + 2 * yi + ci
    shard = 2 * xi + yi
    mat = lambda t: t.reshape(t.shape[-2], t.shape[-1])
    sharded = [n for n, _ in SHARDED]

    ax = dict(SHARDED)
    late = LATE
    early = [n for n in sharded if n not in late]
    shards = {n: mat(W[n]).astype(MXU) for n in sharded}
    gathered = _exchange_call(_GatherChipsHalved([shards[n] for n in early]), name="gather_weights")
    full = {n: _full_weight(g, ax[n]) for n, g in zip(early, gathered, strict=True)}

    c_all = _all_gather8(jnp.broadcast_to(c, (8, D)), name="gather_c")[0::8]
    pad_rows = lambda t: jnp.concatenate([t, jnp.zeros((BLK - 8, t.shape[1]), t.dtype)])
    c_all = pad_rows(c_all.astype(MXU))
    ada_cols = _mm_nn(c_all, mat(ada_w).astype(MXU), name="f_ada")[:8]
    ada_all = _all_gather8(ada_cols, name="gather_ada").reshape(2, 2, 2, 8, 6 * D // 4)
    ada_mine = lax.dynamic_index_in_dim(ada_all[:, :, 0], me, axis=2, keepdims=False)
    ada = ada_mine.reshape(1, 6 * D) + mat(ada_b)

    zero = jnp.zeros((64, RW), MXU)
    lora = jnp.concatenate([jnp.concatenate([full['decay_up'], zero], axis=1),
                            jnp.concatenate([zero, full['iclr_up']], axis=1)], axis=0)
    s = {n: W[n].reshape(1, -1) for n in SMALL if n != 'ada_b'}
    s['lora_up'] = lora.astype(F32)
    s['gate_up'] = full['gate_up'].astype(F32)
    tab = _rope_table(positions.reshape(-1))
    loss, dx, d_ada, gw, gs, from_chips = _local_step(x[0], loss_target[0], ada, tab, dict(w_in=full['w_in']), s,
                                                      shards={n: shards[n] for n in late})

    gs['ada_b'] = d_ada
    gsmall = _pack_small([gs[n] for n in SMALL] + [loss], name="pack_small_grads")
    gsmall_all = _all_gather8(gsmall, name="gather_small_grads")
    row = lambda src: [src[n].reshape(1, -1) for n in SMALL]
    sm_out, loss = _adamw_small(row(W), row(M), row(V), gsmall_all, name="adamw_small")
    sm_out = [{n: o.reshape(W[n].shape) for n, o in zip(SMALL, outs_k, strict=True)} for outs_k in sm_out]
    loss = loss[0, 0]

    ada_rows = 6 * D // SMALL_COLS
    d_ada_all = gsmall_all.reshape(8, SMALL_SLOTS, SMALL_COLS)[:, :ada_rows].reshape(8, 6 * D)
    d_ada_cols = lax.dynamic_slice_in_dim(d_ada_all, shard * (6 * D // 4), 6 * D // 4, axis=1)
    g_ada_w = _mm_tn(c_all, pad_rows(d_ada_cols.astype(MXU)), name="b_ada")

    rest = [n for n in sharded if n != 'w_in']
    parts = {n: _sum_parts(lax.dynamic_index_in_dim(gw[n], shard, axis=0, keepdims=False), from_chips[n],
                           name="sum_" + n) for n in rest}
    tail = _Both(_ScatterChips([gw['w_in']], rows=[(3 * D // 4, D)]), _SwapSibling([parts[n] for n in rest]))
    res = _adamw(mat(ada_w), mat(m_ada_w), mat(v_ada_w), [g_ada_w], tm=256, name="adamw_ada", hosted=tail)
    ada_out, last_quarter, others = res[:4], res[4], dict(zip(rest, res[5:], strict=True))
    parts['w_in'] = _sum_parts(lax.dynamic_index_in_dim(gw['w_in'], shard, axis=0, keepdims=False),
                               jnp.concatenate(from_chips['w_in'] + [last_quarter], axis=1), name="sum_w_in")
    others['w_in'] = _exchange_call(_SwapSibling([parts['w_in']]), name="swap_w_in")[0]
    sh_out = {}
    for n in sharded:
        part, other = parts[n], others[n]
        sh_out[n] = _adamw(mat(W[n]), mat(M[n]), mat(V[n]), [part, other], tm=_pick(part.shape[0], (256, 128, 64)),
                           name="adamw_" + n)

    def leaf(k, n):
        if n == 'ada_w':
            return ada_out[k].reshape(W[n].shape)
        if n in sharded:
            return sh_out[n][k].reshape(W[n].shape)
        return sm_out[k][n]
    outs = [leaf(k, n) for k in range(4) for n in WEIGHTS]
    return (loss, dx[None], *outs)
```

```python
import functools
import math

import jax
import jax.numpy as jnp
from jax import lax
from jax.experimental import pallas as pl
from jax.experimental.pallas import tpu as pltpu

F32 = jnp.float32
BF16 = jnp.bfloat16
MXU = BF16
HI = lax.Precision.HIGHEST

D = 1024
HD = 64
NH = 8
RW = NH * HD
SHIFT_W = 3 * RW + 64 + 64 + 128
QKV_W = RW + 2 * 128
GATE_W = 2 * D
IN_W = SHIFT_W + QKV_W + GATE_W
DFF = 2816
BLK = 128
CHUNK = 128
RMS_EPS = 1e-6
GN_EPS = 64e-5
NEG_INF = -1e30
ADAM_LR, ADAM_B1, ADAM_B2, ADAM_EPS, ADAM_WD, ADAM_STEP = 0.001, 0.9, 0.999, 1e-08, 0.01, 10
VMEM_LIMIT = 56 * 1024 * 1024
MESH = pl.DeviceIdType.MESH


def _cparams(sem=None):
    return pltpu.CompilerParams(dimension_semantics=sem, vmem_limit_bytes=VMEM_LIMIT)


def _full_spec(a):
    nd = a.ndim
    return pl.BlockSpec(a.shape, lambda *_: (0,) * nd)


def _rowwise(fn, rows, consts, outs, *, tm, name, halo=(), hosted=None):
    rows = [(a + (0,))[:3] if isinstance(a, tuple) else (a, a.shape[1], 0) for a in rows]
    T = rows[0][0].shape[0]
    assert T % tm == 0 and tm % 8 == 0
    n_tiles = T // tm
    n_in = len(rows) + len(halo) + len(consts)
    in_specs = [pl.BlockSpec((tm, nc), lambda i, j=j: (i, j)) for _, nc, j in rows]
    args = [a for a, _, _ in rows]
    for a, nc, kind in halo:
        if kind == 'prev':
            in_specs.append(pl.BlockSpec((8, nc), lambda i: (jnp.maximum(i * (tm // 8) - 1, 0), 0)))
        else:
            in_specs.append(pl.BlockSpec((8, nc), lambda i: (jnp.minimum((i + 1) * (tm // 8), T // 8 - 1), 0)))
        args.append(a)
    in_specs += [_full_spec(a) for a in consts]
    args += list(consts)
    out_shape, out_specs = [], []
    for ncols, dtype, kind in outs:
        if kind == 'row':
            out_shape.append(jax.ShapeDtypeStruct((T, ncols), dtype))
            out_specs.append(pl.BlockSpec((tm, ncols), lambda i: (i, 0)))
        else:
            out_shape.append(jax.ShapeDtypeStruct((kind, ncols), dtype))
            out_specs.append(pl.BlockSpec((kind, ncols), lambda i: (0, 0)))

    def body(*refs):
        i = pl.program_id(0)
        vals = [r[...] for r in refs[:n_in]]
        res = fn(i, n_tiles, *vals)
        for (ncols, dtype, kind), o_ref, val in zip(outs, refs[n_in:], res, strict=True):
            if kind == 'row':
                o_ref[...] = val.astype(dtype)
            else:
                @pl.when(i == 0)
                def _():
                    o_ref[...] = jnp.zeros_like(o_ref)
                o_ref[...] += val.astype(dtype)

    h_in, h_in_specs, h_out_specs, h_out_shape, h_scratch = _hosted_args(hosted)
    res = pl.pallas_call(
        _hosting(body, hosted, n_in, len(outs), 0, n_tiles), name=name, grid=(n_tiles,),
        in_specs=in_specs + h_in_specs, out_specs=out_specs + h_out_specs, out_shape=out_shape + h_out_shape,
        scratch_shapes=h_scratch, compiler_params=_cparams(("arbitrary",)),
    )(*args, *h_in)
    return res


def _pick(n, cands):
    for c in cands:
        if n % c == 0:
            return c
    return n


MM_ROWS = (1024, 512, 256, 128)
MM_COLS = (1536, 1408, 1024, 896, 768, 512, 256, 128)
MM_WIDE = 3000


def _mm_nn(a, w, *, name, out_dtype=F32, hosted=None):
    T, K = a.shape
    N = w.shape[1]
    tm = _pick(T, MM_ROWS)
    tn = _pick(N, MM_COLS)
    grid = (N // tn, T // tm)
    h_in, h_in_specs, h_out_specs, h_out_shape, h_scratch = _hosted_args(hosted)

    def body(a_ref, w_ref, o_ref):
        o_ref[...] = jnp.dot(a_ref[...], w_ref[...], preferred_element_type=F32).astype(out_dtype)

    res = pl.pallas_call(
        _hosting(body, hosted, 2, 1, 0, grid), name=name, grid=grid,
        in_specs=[pl.BlockSpec((tm, K), lambda j, i: (i, 0)), pl.BlockSpec((K, tn), lambda j, i: (0, j))] + h_in_specs,
        out_specs=[pl.BlockSpec((tm, tn), lambda j, i: (i, j))] + h_out_specs,
        out_shape=[jax.ShapeDtypeStruct((T, N), out_dtype)] + h_out_shape, scratch_shapes=h_scratch,
        compiler_params=_cparams(("arbitrary", "arbitrary")),
    )(a, w, *h_in)
    return res if hosted else res[0]


def _mm_nt(dy, w, *, name, out_dtype=F32, hosted=None):
    T, N = dy.shape
    K = w.shape[0]
    tm = _pick(T, MM_ROWS if N <= MM_WIDE else MM_ROWS[1:])
    tk = _pick(K, MM_COLS[1:])
    grid = (K // tk, T // tm)
    h_in, h_in_specs, h_out_specs, h_out_shape, h_scratch = _hosted_args(hosted)

    def body(dy_ref, w_ref, o_ref):
        o_ref[...] = lax.dot_general(dy_ref[...], w_ref[...], (((1,), (1,)), ((), ())),
                                     preferred_element_type=F32).astype(out_dtype)

    res = pl.pallas_call(
        _hosting(body, hosted, 2, 1, 0, grid), name=name, grid=grid,
        in_specs=[pl.BlockSpec((tm, N), lambda j, i: (i, 0)), pl.BlockSpec((tk, N), lambda j, i: (j, 0))] + h_in_specs,
        out_specs=[pl.BlockSpec((tm, tk), lambda j, i: (i, j))] + h_out_specs,
        out_shape=[jax.ShapeDtypeStruct((T, K), out_dtype)] + h_out_shape, scratch_shapes=h_scratch,
        compiler_params=_cparams(("arbitrary", "arbitrary")),
    )(dy, w, *h_in)
    return res if hosted else res[0]


def _mm_tn(a, dy, *, name, out_dtype=F32, col_shards=None):
    T, K = a.shape
    N = dy.shape[1]
    tm = _pick(T, MM_ROWS)
    tn = N // col_shards if col_shards else _pick(N, MM_COLS[1:])
    n_t = T // tm

    def body(a_ref, dy_ref, o_ref, acc_ref):
        i = pl.program_id(1)

        @pl.when(i == 0)
        def _():
            acc_ref[...] = jnp.zeros_like(acc_ref)

        acc_ref[...] += lax.dot_general(a_ref[...], dy_ref[...], (((0,), (0,)), ((), ())), preferred_element_type=F32)

        @pl.when(i == n_t - 1)
        def _():
            o_ref[...] = acc_ref[...].astype(out_dtype)

    if col_shards:
        out_specs = pl.BlockSpec((None, K, tn), lambda j, i: (j, 0, 0))
        out_shape = jax.ShapeDtypeStruct((col_shards, K, tn), out_dtype)
    else:
        out_specs = pl.BlockSpec((K, tn), lambda j, i: (0, j))
        out_shape = jax.ShapeDtypeStruct((K, N), out_dtype)
    return pl.pallas_call(
        body, name=name, grid=(N // tn, n_t),
        in_specs=[pl.BlockSpec((tm, K), lambda j, i: (i, 0)), pl.BlockSpec((tm, tn), lambda j, i: (i, j))],
        out_specs=out_specs, out_shape=out_shape, scratch_shapes=[pltpu.VMEM((K, tn), F32)],
        compiler_params=_cparams(("arbitrary", "arbitrary")),
    )(a, dy)


def _mm_then(products, fn, rows, consts, outs, *, tm, name, lhs_fn=None, hosted=None):
    products = [(p + (None,))[:4] for p in products]
    T = (rows[0] if lhs_fn else products[0][0]).shape[-2]
    n_tiles = T // tm
    in_specs, args = [], []
    for a, w, _, j in products:
        if a is not None and j is None:
            in_specs.append(pl.BlockSpec((tm, a.shape[1]), lambda i: (i, 0)))
            args.append(a)
        elif a is not None:
            in_specs.append(pl.BlockSpec((None, tm, a.shape[2]), lambda i, j=j: (j, i, 0)))
            args.append(a)
        in_specs.append(_full_spec(w) if j is None else
                        pl.BlockSpec((None,) + w.shape[1:], lambda i, j=j: (j, 0, 0)))
        args.append(w)
    n_w = len(args)
    in_specs += [pl.BlockSpec((tm, a.shape[1]), lambda i: (i, 0)) for a in rows] + [_full_spec(c_) for c_ in consts]
    args += list(rows) + list(consts)
    n_in = len(args)
    out_shape, out_specs = [], []
    for ncols, dtype, kind in outs:
        if kind == 'row':
            out_shape.append(jax.ShapeDtypeStruct((T, ncols), dtype))
            out_specs.append(pl.BlockSpec((tm, ncols), lambda i: (i, 0)))
        else:
            out_shape.append(jax.ShapeDtypeStruct((kind, ncols), dtype))
            out_specs.append(pl.BlockSpec((kind, ncols), lambda i: (0, 0)))

    def body(*refs):
        i = pl.program_id(0)
        tiles = [r[...] for r in refs[n_w:n_in]]
        made = []
        if lhs_fn:
            made = lhs_fn(*tiles)
            made = list(made) if isinstance(made, tuple) else [made]
            made[0] = made[0].astype(MXU)
        y, pos = None, 0
        for a, _, form, _ in products:
            if a is None:
                lhs = made[0]
            else:
                lhs, pos = refs[pos][...], pos + 1
            dims = (((1,), (0,)), ((), ())) if form == 'nn' else (((1,), (1,)), ((), ()))
            t = lax.dot_general(lhs, refs[pos][...], dims, preferred_element_type=F32)
            pos += 1
            y = t if y is None else y + t
        res = fn(i, n_tiles, y, *made, *tiles)
        for (ncols, dtype, kind), o_ref, val in zip(outs, refs[n_in:], res, strict=True):
            if kind == 'row':
                o_ref[...] = val.astype(dtype)
            else:
                @pl.when(i == 0)
                def _():
                    o_ref[...] = jnp.zeros_like(o_ref)
                o_ref[...] += val.astype(dtype)

    h_in, h_in_specs, h_out_specs, h_out_shape, h_scratch = _hosted_args(hosted)
    return pl.pallas_call(
        _hosting(body, hosted, n_in, len(outs), 0, n_tiles), name=name, grid=(n_tiles,),
        in_specs=in_specs + h_in_specs, out_specs=out_specs + h_out_specs, out_shape=out_shape + h_out_shape,
        scratch_shapes=h_scratch, compiler_params=_cparams(("arbitrary",)))(*args, *h_in)


def _seg_ones(n):
    r = lax.broadcasted_iota(jnp.int32, (n, n), 0) // HD
    c = lax.broadcasted_iota(jnp.int32, (n, n), 1) // HD
    return (r == c).astype(F32)


def _segsum_raw(x):
    ones = _seg_ones(x.shape[1])
    if MXU == F32:
        return jnp.dot(x, ones, precision=HI, preferred_element_type=F32)
    hi = x.astype(MXU)
    lo = (x - hi.astype(F32)).astype(MXU)
    ones = ones.astype(MXU)
    return jnp.dot(hi, ones, preferred_element_type=F32) + jnp.dot(lo, ones, preferred_element_type=F32)


@jax.custom_vjp
def _segsum(x):
    return _segsum_raw(x)


def _segsum_fwd(x):
    return _segsum_raw(x), None


def _segsum_bwd(_, g):
    return (_segsum_raw(g),)


_segsum.defvjp(_segsum_fwd, _segsum_bwd)


def _mxu(x):
    return x.astype(MXU)


@jax.custom_vjp
def _bdot(a, b):
    return jnp.dot(_mxu(a), _mxu(b), preferred_element_type=F32)


def _bdot_fwd(a, b):
    return _bdot(a, b), (a, b)


def _bdot_bwd(res, g):
    a, b = res
    da = lax.dot_general(_mxu(g), _mxu(b), (((1,), (1,)), ((), ())), preferred_element_type=F32)
    db = lax.dot_general(_mxu(a), _mxu(g), (((0,), (0,)), ((), ())), preferred_element_type=F32)
    return da.astype(a.dtype), db.astype(b.dtype)


_bdot.defvjp(_bdot_fwd, _bdot_bwd)


def _sigmoid(x):
    return 1.0 / (1.0 + jnp.exp(-x))


def _softplus(x):
    return jnp.maximum(x, 0.0) + jnp.log(1.0 + jnp.exp(jnp.minimum(x, -x)))


def _norm_mod(x, gain, scale, shift):
    inv = lax.rsqrt(jnp.mean(x * x, axis=-1, keepdims=True) + RMS_EPS)
    return (x * inv) * gain * (1.0 + scale) + shift


def _prep(mixed, decay_w0, lora_up, iclr_a0, gate_up, k_k, k_a):
    r = mixed[:, 0:RW]
    k = mixed[:, RW:2 * RW]
    v = mixed[:, 2 * RW:3 * RW]
    z = mixed[:, 3 * RW:3 * RW + 128]
    xg = mixed[:, 3 * RW + 128:]
    lane = lax.broadcasted_iota(jnp.int32, z.shape, 1)
    tz = jnp.where(lane < 64, jnp.tanh(z), z)
    lo = _bdot(tz, lora_up)
    w_log = -_softplus(-(decay_w0 + lo[:, :RW])) - 0.5
    lw = -jnp.exp(w_log)
    a_ic = _sigmoid(iclr_a0 + lo[:, RW:])
    g = _bdot(_sigmoid(xg), gate_up)
    kk = k * k_k
    kk = kk / jnp.maximum(jnp.sqrt(_segsum(kk * kk)), 1e-12)
    k_mod = k * (1.0 + (a_ic - 1.0) * k_a)
    return jnp.concatenate([r, lw, k_mod, v, -kk, kk * a_ic, g], axis=1)


def _post(y, r, k, v, g, lnx_gain, lnx_bias, r_k):
    mu = _segsum(y) * (1.0 / HD)
    yc = y - mu
    var = _segsum(yc * yc) * (1.0 / HD)
    yn = yc * lax.rsqrt(var + GN_EPS) * lnx_gain + lnx_bias
    bonus = _segsum(r * k * r_k) * v
    return (yn + bonus) * g


def _merge(pg, ma, mb, bias):
    gates = _sigmoid(pg + bias)
    return gates[:, :D] * ma + gates[:, D:] * mb


def _swiglu(u, v):
    return u * _sigmoid(u) * v


def _ffn_in(h, w1, w3, *, name):
    T, K = h.shape
    ns, _, Fs = w1.shape
    tm = _pick(T, MM_ROWS)

    def body(h_ref, w1_ref, w3_ref, u_ref, v_ref, a_ref):
        u = jnp.dot(h_ref[...], w1_ref[...], preferred_element_type=F32).astype(MXU)
        v = jnp.dot(h_ref[...], w3_ref[...], preferred_element_type=F32).astype(MXU)
        u_ref[...] = u
        v_ref[...] = v
        a_ref[...] = _swiglu(u.astype(F32), v.astype(F32)).astype(MXU)

    wspec = pl.BlockSpec((None, K, Fs), lambda j, i: (j, 0, 0))
    ospec = pl.BlockSpec((None, tm, Fs), lambda j, i: (j, i, 0))
    return pl.pallas_call(
        body, name=name, grid=(ns, T // tm),
        in_specs=[pl.BlockSpec((tm, K), lambda j, i: (i, 0)), wspec, wspec],
        out_specs=[ospec] * 3, out_shape=[jax.ShapeDtypeStruct((ns, T, Fs), MXU)] * 3,
        compiler_params=_cparams(("arbitrary", "arbitrary")),
    )(h, w1, w3)


def _ffn_act_bwd(dff, w2, u, v, *, name):
    T, N = dff.shape
    ns, Fs, _ = w2.shape
    tm = _pick(T, MM_ROWS)

    def body(dy_ref, w_ref, u_ref, v_ref, du_ref, dv_ref):
        dact = lax.dot_general(dy_ref[...], w_ref[...], (((1,), (1,)), ((), ())), preferred_element_type=F32)
        _, vjp = jax.vjp(_swiglu, u_ref[...].astype(F32), v_ref[...].astype(F32))
        du, dv = vjp(dact)
        du_ref[...] = du.astype(MXU)
        dv_ref[...] = dv.astype(MXU)

    tile = pl.BlockSpec((None, tm, Fs), lambda j, i: (j, i, 0))
    return pl.pallas_call(
        body, name=name, grid=(ns, T // tm),
        in_specs=[pl.BlockSpec((tm, N), lambda j, i: (i, 0)), pl.BlockSpec((None, Fs, N), lambda j, i: (j, 0, 0)),
                  tile, tile],
        out_specs=[tile, tile], out_shape=[jax.ShapeDtypeStruct((ns, T, Fs), MXU)] * 2,
        compiler_params=_cparams(("arbitrary", "arbitrary")),
    )(dff, w2, u, v)


def _mm_tn_blocks(a, dy, *, name, out_dtype):
    a3, d3 = a.ndim == 3, dy.ndim == 3
    ns = a.shape[0] if a3 else dy.shape[0]
    T, K, N = a.shape[-2], a.shape[-1], dy.shape[-1]
    tm = _pick(T, MM_ROWS)
    n_t = T // tm

    def body(a_ref, dy_ref, o_ref, acc_ref):
        i = pl.program_id(1)

        @pl.when(i == 0)
        def _():
            acc_ref[...] = jnp.zeros_like(acc_ref)

        acc_ref[...] += lax.dot_general(a_ref[...], dy_ref[...], (((0,), (0,)), ((), ())), preferred_element_type=F32)

        @pl.when(i == n_t - 1)
        def _():
            o_ref[...] = acc_ref[...].astype(out_dtype)

    spec = lambda is3, n: (pl.BlockSpec((None, tm, n), lambda j, i: (j, i, 0)) if is3
                           else pl.BlockSpec((tm, n), lambda j, i: (i, 0)))
    return pl.pallas_call(
        body, name=name, grid=(ns, n_t), in_specs=[spec(a3, K), spec(d3, N)],
        out_specs=pl.BlockSpec((None, K, N), lambda j, i: (j, 0, 0)),
        out_shape=jax.ShapeDtypeStruct((ns, K, N), out_dtype), scratch_shapes=[pltpu.VMEM((K, N), F32)],
        compiler_params=_cparams(("arbitrary", "arbitrary")),
    )(a, dy)


@functools.partial(jax.custom_vjp, nondiff_argnums=(1,))
def _lane_roll(x, s):
    return pltpu.roll(x, s, 1)


def _lane_roll_fwd(x, s):
    return pltpu.roll(x, s, 1), None


def _lane_roll_bwd(s, _, g):
    n = g.shape[1]
    return (pltpu.roll(g, (n - s) % n, 1),)


_lane_roll.defvjp(_lane_roll_fwd, _lane_roll_bwd)


def _rope(x, cos, sin_lo, sin_hi):
    n = x.shape[1]
    return x * cos + _lane_roll(x, n - 8) * sin_lo + _lane_roll(x, 8) * sin_hi


def _head_rms(x, gain):
    return x * lax.rsqrt(_segsum(x * x) * (1.0 / HD) + RMS_EPS) * gain


def _attn_blocks(qkv_c, qkv_p, tab_c, tab_p, qg, kg, sinks, first):
    nb = qkv_c.shape[0] // BLK
    G = 4

    def tabs(tab, n):
        return [jnp.tile(tab[:, j * 128:(j + 1) * 128], (1, n // 128)) for j in range(3)]

    qg = jnp.concatenate([qg] * NH, axis=1)
    kg = jnp.concatenate([kg] * 2, axis=1)
    q = _rope(_head_rms(qkv_c[:, :RW], qg), *tabs(tab_c, RW))
    k_in = jnp.concatenate([qkv_p[:, RW:RW + 128], qkv_c[:, RW:RW + 128]], axis=0)
    k = _rope(_head_rms(k_in, kg), *tabs(jnp.concatenate([tab_p, tab_c], axis=0), 128))
    v = jnp.concatenate([qkv_p[:, RW + 128:], qkv_c[:, RW + 128:]], axis=0)

    pile = lambda xs: jnp.concatenate([x_[None] for x_ in xs], axis=0)

    def bands(t):
        return pile([t[b * BLK:(b + 2) * BLK, kvh * HD:(kvh + 1) * HD] for kvh in range(2) for b in range(nb)])

    qs = pile([jnp.concatenate([q[b * BLK:(b + 1) * BLK, (G * kvh + g) * HD:(G * kvh + g + 1) * HD]
                                for g in range(G)], axis=0) for kvh in range(2) for b in range(nb)])
    s = _bmm(qs, bands(k), 2, 2, 1) * (HD ** -0.5)
    qi = lax.broadcasted_iota(jnp.int32, (G * BLK, 2 * BLK), 0) % BLK
    kj = lax.broadcasted_iota(jnp.int32, (G * BLK, 2 * BLK), 1)
    dist = qi + BLK - kj
    in_band = (dist >= 0) & (dist < BLK)
    pair = lax.broadcasted_iota(jnp.int32, (2 * nb, 1, 1), 0)
    no_prev = (pair % nb == 0) & first
    valid = in_band[None] & (jnp.logical_not(no_prev) | (kj >= BLK)[None])
    s = jnp.where(valid, s, NEG_INF)
    row_g = lax.broadcasted_iota(jnp.int32, (G * BLK, 1), 0) // BLK
    sink = []
    for kvh in range(2):
        col = jnp.zeros((G * BLK, 1), F32)
        for g in range(G):
            col = jnp.where(row_g == g, sinks[:, G * kvh + g:G * kvh + g + 1], col)
        sink += [col] * nb
    sink = pile(sink)
    m = lax.stop_gradient(jnp.maximum(jnp.max(s, axis=-1, keepdims=True), sink))
    e = jnp.exp(s - m)
    p = e * (1.0 / (jnp.sum(e, axis=-1, keepdims=True) + jnp.exp(sink - m)))
    o = _bmm(p, bands(v), 2, 1, 1)
    return jnp.concatenate([jnp.concatenate([o[kvh * nb + b, g * BLK:(g + 1) * BLK] for kvh in range(2)
                                             for g in range(G)], axis=1) for b in range(nb)], axis=0)


def _heads(x):
    return jnp.stack([x[:, h * HD:(h + 1) * HD] for h in range(NH)], axis=0)


def _unheads(x):
    return jnp.concatenate([x[h] for h in range(NH)], axis=1)


def _split(x, n):
    parts, rest = [], x
    for _ in range(n):
        p = rest.astype(MXU)
        parts.append(p)
        rest = rest - p.astype(F32)
    return parts


def _bdot_batched(a, b, ca, cb):
    return lax.dot_general(a, b, (((ca,), (cb,)), ((0,), (0,))), preferred_element_type=F32)


def _bmm_passes(a, b, ca, cb, passes):
    if MXU == F32:
        return lax.dot_general(a, b, (((ca,), (cb,)), ((0,), (0,))), precision=HI, preferred_element_type=F32)
    if passes == 1:
        return _bdot_batched(a.astype(MXU), b.astype(MXU), ca, cb)
    (a0, a1), (b0, b1) = _split(a, 2), _split(b, 2)
    return _bdot_batched(a0, b0, ca, cb) + (_bdot_batched(a0, b1, ca, cb) + _bdot_batched(a1, b0, ca, cb))


@functools.partial(jax.custom_vjp, nondiff_argnums=(2, 3, 4))
def _bmm(a, b, ca, cb, passes=1):
    return _bmm_passes(a, b, ca, cb, passes)


def _bmm_fwd(a, b, ca, cb, passes):
    return _bmm_passes(a, b, ca, cb, passes), (a, b)


def _bmm_bwd(ca, cb, passes, res, g):
    a, b = res
    if (ca, cb) == (2, 1):
        return _bmm_passes(g, b, 2, 2, passes), _bmm_passes(a, g, 1, 1, passes)
    if (ca, cb) == (2, 2):
        return _bmm_passes(g, b, 2, 1, passes), _bmm_passes(g, a, 1, 1, passes)
    return _bmm_passes(b, g, 2, 2, passes), _bmm_passes(a, g, 2, 1, passes)


_bmm.defvjp(_bmm_fwd, _bmm_bwd)


def _tri_dot(x, transpose):
    C = x.shape[1]
    ri = lax.broadcasted_iota(jnp.int32, (C, C), 0)
    ci = lax.broadcasted_iota(jnp.int32, (C, C), 1)
    tri = jnp.broadcast_to(((ri <= ci) if transpose else (ri >= ci)).astype(MXU), (x.shape[0], C, C))
    if MXU == F32:
        return lax.dot_general(tri, x, (((2,), (1,)), ((0,), (0,))), precision=HI, preferred_element_type=F32)
    p0, p1, p2 = _split(x, 3)
    return _bdot_batched(tri, p0, 2, 1) + (_bdot_batched(tri, p1, 2, 1) + _bdot_batched(tri, p2, 2, 1))


@jax.custom_vjp
def _cumsum_rows(x):
    return _tri_dot(x, False)


def _cumsum_rows_fwd(x):
    return _tri_dot(x, False), None


def _cumsum_rows_bwd(_, g):
    return (_tri_dot(g, True),)


_cumsum_rows.defvjp(_cumsum_rows_fwd, _cumsum_rows_bwd)

P_SCORE = 1
P_SOLVE = 1
P_STATE = 1
SCAN_CHUNKS = (2, 1)


def _neumann(l):
    C = l.shape[1]
    eye = (lax.broadcasted_iota(jnp.int32, (C, C), 0) == lax.broadcasted_iota(jnp.int32, (C, C), 1)).astype(F32)
    x, lp = eye + l, l
    for _ in range(int(math.log2(C)) - 1):
        lp = _bmm(lp, lp, 2, 1, P_SOLVE)
        x = x + _bmm(x, lp, 2, 1, P_SOLVE)
    return x


@jax.custom_vjp
def _unit_lower_inverse(l):
    return _neumann(l)


def _unit_lower_inverse_fwd(l):
    x = _neumann(l)
    return x, x


def _unit_lower_inverse_bwd(x, g):
    return (_bmm(_bmm(x, g, 1, 1, P_SOLVE), x, 2, 2, P_SOLVE),)


_unit_lower_inverse.defvjp(_unit_lower_inverse_fwd, _unit_lower_inverse_bwd)


def _known_inverse(x):
    @jax.custom_vjp
    def f(l):
        return x

    f.defvjp(lambda l: (x, None), lambda _, g: (_bmm(_bmm(x, g, 1, 1, P_SOLVE), x, 2, 2, P_SOLVE),))
    return f


def _chunk(S0, r, lw, k, v, a, b, inverse=None):
    C = CHUNK
    n = r.shape[1] // C
    fold = lambda t: t.reshape(NH * n, C, HD)
    r, lw, k, v, a, b = (fold(t) for t in (r, lw, k, v, a, b))
    ri = lax.broadcasted_iota(jnp.int32, (C, C), 0)
    ci = lax.broadcasted_iota(jnp.int32, (C, C), 1)
    strict = (ri > ci)
    cum = _cumsum_rows(lw)
    p_in = jnp.exp(cum)
    p_ex = jnp.exp(cum - lw)
    p_inv = jnp.exp(-cum)
    at, rt, bt, kt = a * p_ex, r * p_in, b * p_inv, k * p_inv
    lhs = jnp.concatenate([at, rt], axis=1)
    rhs_ = jnp.concatenate([bt, kt], axis=1)
    sc = _bmm(lhs, rhs_, 2, 2, P_SCORE)
    a_ab = jnp.where(strict, sc[:, :C, :C], 0.0)
    a_ak = jnp.where(strict, sc[:, :C, C:], 0.0)
    incl2 = (lax.broadcasted_iota(jnp.int32, (C, 2 * C), 0) >= lax.broadcasted_iota(jnp.int32, (C, 2 * C), 1) % C)
    a_r = jnp.where(incl2, sc[:, C:, :], 0.0)
    av = _bmm(a_ak, v, 2, 1, P_SCORE)
    x = x_all = (_unit_lower_inverse if inverse is None else _known_inverse(inverse))(a_ab)
    p_last = jnp.exp(cum[:, C - 1:C, :])
    per_chunk = lambda t: t.reshape((NH, n) + t.shape[1:])
    lhs, rhs_, a_r, av, x, v, p_last = (per_chunk(t) for t in (lhs, rhs_, a_r, av, x, v, p_last))
    S, ys = S0, []
    for c in range(n):
        s0 = _bmm(lhs[:, c], S, 2, 2, P_STATE)
        u = _bmm(x[:, c], s0[:, :C] + av[:, c], 2, 1, P_SOLVE)
        uv = jnp.concatenate([u, v[:, c]], axis=1)
        ys.append(s0[:, C:] + _bmm(a_r[:, c], uv, 2, 1, P_SCORE))
        S = (S + _bmm(uv, rhs_[:, c], 1, 1, P_STATE)) * p_last[:, c]
    return jnp.concatenate(ys, axis=1), S, x_all


def _hosting(body, ex, n_in, n_out, n_scratch, n_steps):
    if ex is None:
        return body

    def wrapped(*refs):
        a = n_in
        b = a + ex.n_in
        c = b + n_out
        d = c + ex.n_out
        e = d + n_scratch
        ex_refs = (refs[a:b], refs[c:d], refs[e:])
        grid = n_steps if isinstance(n_steps, tuple) else (n_steps,)
        first = last = True
        for ax_, size in enumerate(grid):
            first = first & (pl.program_id(ax_) == 0)
            last = last & (pl.program_id(ax_) == size - 1)

        @pl.when(first)
        def _():
            ex.start(*ex_refs)

        body(*refs[:a], *refs[b:c], *refs[d:e])

        @pl.when(last)
        def _():
            ex.wait(*ex_refs)

    return wrapped


def _hosted_args(ex):
    if ex is None:
        return [], [], [], [], []
    any_spec = pl.BlockSpec(memory_space=pl.ANY)
    return list(ex.arrays), [any_spec] * ex.n_in, [any_spec] * ex.n_out, list(ex.out_shape), list(ex.scratch)


def _scan_fwd(rw, *, name, hosted=None):
    T = rw.shape[0]
    nc = _pick(T // CHUNK, SCAN_CHUNKS)
    rows = CHUNK * nc
    n = T // rows
    h_in, h_in_specs, h_out_specs, h_out_shape, h_scratch = _hosted_args(hosted)

    def body(r_ref, lw_ref, k_ref, v_ref, a_ref, b_ref, y_ref, ck_ref, inv_ref, s_ref):
        @pl.when(pl.program_id(0) == 0)
        def _():
            s_ref[...] = jnp.zeros_like(s_ref)

        S0 = s_ref[...]
        ck_ref[0] = S0
        y, S1, inv = _chunk(S0, *[_heads(ref[...]) for ref in (r_ref, lw_ref, k_ref, v_ref, a_ref, b_ref)])
        y_ref[...] = _unheads(y)
        inv_ref[0] = inv
        s_ref[...] = S1

    col = lambda j: pl.BlockSpec((rows, RW), lambda i: (i, j))
    return pl.pallas_call(
        _hosting(body, hosted, 6, 3, 1, n), name=name, grid=(n,),
        in_specs=[col(j) for j in range(6)] + h_in_specs,
        out_specs=[pl.BlockSpec((rows, RW), lambda i: (i, 0)),
                   pl.BlockSpec((1, NH, HD, HD), lambda i: (i, 0, 0, 0)),
                   pl.BlockSpec((1, NH * nc, CHUNK, CHUNK), lambda i: (i, 0, 0, 0))] + h_out_specs,
        out_shape=[jax.ShapeDtypeStruct((T, RW), F32), jax.ShapeDtypeStruct((n, NH, HD, HD), F32),
                   jax.ShapeDtypeStruct((n, NH * nc, CHUNK, CHUNK), F32)] + h_out_shape,
        scratch_shapes=[pltpu.VMEM((NH, HD, HD), F32)] + h_scratch,
        compiler_params=_cparams(("arbitrary",)),
    )(rw, rw, rw, rw, rw, rw, *h_in)


def _scan_bwd(rw, ck, inv, dy, *, name, hosted=None):
    T = rw.shape[0]
    nc = _pick(T // CHUNK, SCAN_CHUNKS)
    rows = CHUNK * nc
    n = T // rows

    def body(r_ref, lw_ref, k_ref, v_ref, a_ref, b_ref, ck_ref, inv_ref, dy_ref, o_ref, ds_ref):
        @pl.when(pl.program_id(0) == 0)
        def _():
            ds_ref[...] = jnp.zeros_like(ds_ref)

        prim = [_heads(ref[...]) for ref in (r_ref, lw_ref, k_ref, v_ref, a_ref, b_ref)]
        known = inv_ref[0]
        _, vjp = jax.vjp(lambda S0, *p: _chunk(S0, *p, inverse=known)[:2], ck_ref[0], *prim)
        grads = vjp((_heads(dy_ref[...]), ds_ref[...]))
        ds_ref[...] = grads[0]
        o_ref[...] = jnp.concatenate([_unheads(g) for g in grads[1:]], axis=1).astype(o_ref.dtype)

    h_in, h_in_specs, h_out_specs, h_out_shape, h_scratch = _hosted_args(hosted)
    col = lambda j: pl.BlockSpec((rows, RW), lambda i: (n - 1 - i, j))
    return pl.pallas_call(
        _hosting(body, hosted, 9, 1, 1, n), name=name, grid=(n,),
        in_specs=[col(j) for j in range(6)] + [pl.BlockSpec((1, NH, HD, HD), lambda i: (n - 1 - i, 0, 0, 0)),
                                               pl.BlockSpec((1, NH * nc, CHUNK, CHUNK), lambda i: (n - 1 - i, 0, 0, 0)),
                                               pl.BlockSpec((rows, RW), lambda i: (n - 1 - i, 0))] + h_in_specs,
        out_specs=[pl.BlockSpec((rows, 6 * RW), lambda i: (n - 1 - i, 0))] + h_out_specs,
        out_shape=[jax.ShapeDtypeStruct((T, 6 * RW), MXU)] + h_out_shape,
        scratch_shapes=[pltpu.VMEM((NH, HD, HD), F32)] + h_scratch,
        compiler_params=_cparams(("arbitrary",)),
    )(rw, rw, rw, rw, rw, rw, ck, inv, dy, *h_in)


ATTN_BLOCKS = (4, 2, 1)

def _attn_fwd(qkv, tab, qg, kg, sinks, *, name, hosted=None):
    T = qkv.shape[0]
    nb = _pick(T // BLK, ATTN_BLOCKS)
    n = T // (BLK * nb)
    h_in, h_in_specs, h_out_specs, h_out_shape, h_scratch = _hosted_args(hosted)

    def body(c_ref, p_ref, tc_ref, tp_ref, qg_ref, kg_ref, s_ref, o_ref):
        o_ref[...] = _attn_blocks(c_ref[...], p_ref[...], tc_ref[...], tp_ref[...], qg_ref[...], kg_ref[...],
                                  s_ref[...], pl.program_id(0) == 0).astype(o_ref.dtype)

    cur = lambda w: pl.BlockSpec((nb * BLK, w), lambda i: (i, 0))
    prev = lambda w: pl.BlockSpec((BLK, w), lambda i: (jnp.maximum(i * nb - 1, 0), 0))
    return pl.pallas_call(
        _hosting(body, hosted, 7, 1, 0, n), name=name, grid=(n,),
        in_specs=[cur(QKV_W), prev(QKV_W), cur(3 * 128), prev(3 * 128), _full_spec(qg), _full_spec(kg),
                  _full_spec(sinks)] + h_in_specs,
        out_specs=[cur(RW)] + h_out_specs, out_shape=[jax.ShapeDtypeStruct((T, RW), MXU)] + h_out_shape,
        scratch_shapes=h_scratch,
        compiler_params=_cparams(("arbitrary",)),
    )(qkv, qkv, tab, tab, qg, kg, sinks, *h_in)


def _attn_bwd(qkv, tab, qg, kg, sinks, dy, *, name, hosted=None):
    T = qkv.shape[0]
    nb = _pick(T // BLK, ATTN_BLOCKS)
    n = T // (BLK * nb)
    h_in, h_in_specs, h_out_specs, h_out_shape, h_scratch = _hosted_args(hosted)

    def body(c_ref, p_ref, tc_ref, tp_ref, qg_ref, kg_ref, s_ref, dy_ref, dqkv_ref, dqg_ref, dkg_ref, ds_ref, carry_ref):
        i = pl.program_id(0)

        @pl.when(i == 0)
        def _():
            carry_ref[...] = jnp.zeros_like(carry_ref)
            dqg_ref[...] = jnp.zeros_like(dqg_ref)
            dkg_ref[...] = jnp.zeros_like(dkg_ref)
            ds_ref[...] = jnp.zeros_like(ds_ref)

        tc, tp = tc_ref[...], tp_ref[...]
        f = lambda c, p_, qg_, kg_, sk: _attn_blocks(c, p_, tc, tp, qg_, kg_, sk, i == n - 1)
        _, vjp = jax.vjp(f, c_ref[...], p_ref[...], qg_ref[...], kg_ref[...], s_ref[...])
        dc, dp, dqg, dkg, dsk = vjp(dy_ref[...].astype(F32))
        last = slice((nb - 1) * BLK, nb * BLK)
        dqkv_ref[...] = dc.astype(dqkv_ref.dtype)
        dqkv_ref[last, :] = (dc[last] + carry_ref[...]).astype(dqkv_ref.dtype)
        carry_ref[...] = dp
        dqg_ref[...] += dqg
        dkg_ref[...] += dkg
        ds_ref[...] += dsk

    cur = lambda w: pl.BlockSpec((nb * BLK, w), lambda i: (n - 1 - i, 0))
    prev = lambda w: pl.BlockSpec((BLK, w), lambda i: (jnp.maximum((n - 1 - i) * nb - 1, 0), 0))
    return pl.pallas_call(
        _hosting(body, hosted, 8, 4, 1, n), name=name, grid=(n,),
        in_specs=[cur(QKV_W), prev(QKV_W), cur(3 * 128), prev(3 * 128), _full_spec(qg), _full_spec(kg), _full_spec(sinks),
                  cur(RW)] + h_in_specs,
        out_specs=[cur(QKV_W), _full_spec(qg), _full_spec(kg), _full_spec(sinks)] + h_out_specs,
        out_shape=[jax.ShapeDtypeStruct((T, QKV_W), MXU), jax.ShapeDtypeStruct(qg.shape, F32),
                   jax.ShapeDtypeStruct(kg.shape, F32), jax.ShapeDtypeStruct(sinks.shape, F32)] + h_out_shape,
        scratch_shapes=[pltpu.VMEM((BLK, QKV_W), F32)] + h_scratch,
        compiler_params=_cparams(("arbitrary",)),
    )(qkv, qkv, tab, tab, qg, kg, sinks, dy, *h_in)


def _shift_down(cur, prev8, i):
    rolled = pltpu.roll(cur, 1, 0)
    first_row = jnp.where(i > 0, prev8[7:8, :], 0.0)
    row = lax.broadcasted_iota(jnp.int32, cur.shape, 0)
    return jnp.where(row == 0, first_row, rolled)


def _shift_up(cur, next8, i, n):
    tm = cur.shape[0]
    rolled = pltpu.roll(cur, tm - 1, 0)
    last_row = jnp.where(i < n - 1, next8[0:1, :], 0.0)
    row = lax.broadcasted_iota(jnp.int32, cur.shape, 0)
    return jnp.where(row == tm - 1, last_row, rolled)


def _rope_table(positions):
    half = HD // 8
    inv_freq = 500000.0 ** (-jnp.arange(half, dtype=F32) / half)
    lane = jnp.arange(128) % HD
    rotary = lane < 2 * half
    freq = jnp.where(rotary, inv_freq[lane % half], 0.0)
    ang = positions.astype(F32)[:, None] * freq[None, :]
    cos, sin = jnp.cos(ang), jnp.sin(ang)
    return jnp.concatenate([jnp.where(rotary, cos, 1.0), jnp.where(lane < half, -sin, 0.0),
                            jnp.where(rotary & (lane >= half), sin, 0.0)], axis=1)


GATHER_BEHIND = {"f_proj_shift": [('ffn_w3', 512, 768)], "f_proj_gates": [('ffn_w3', 768, 1024)],
                 "f_prep": [('ffn_w2', 352, 704)],
                 "f_scan": [('ffn_w1', 0, 1024), ('w_branch_a', 0, 512), ('w_branch_b', 0, 512)],
                 "f_post": [('ffn_w3', 0, 512)], "f_attn": [('ffn_w2', 0, 352), ('w_out', 0, 256)]}
HALVED_BEHIND = ("f_scan", "f_post", "f_attn")
LATE = ['w_out', 'w_branch_a', 'w_branch_b', 'ffn_w1', 'ffn_w3', 'ffn_w2']
FFN = ('ffn_w1', 'ffn_w3', 'ffn_w2')
BACK_ATTN = ['w_out', 'w_branch_a', 'w_branch_b', 'ffn_w2']
BACK_SCAN = ['ffn_w1', 'ffn_w3']
BACK_LAST = ['w_in', 'decay_up', 'iclr_up', 'gate_up']


def _full_weight(g, ax):
    return g.reshape(-1, g.shape[2]) if ax == 0 else jnp.concatenate([g[j] for j in range(4)], axis=1)


def _local_step(x, target, ada, tab, w, s, shards=None):
    T = x.shape[0]
    tm = _pick(T, (512, 256, 128))
    tm_wide = _pick(T, (256, 128))
    tm_vjp = _pick(T, (256, 128))
    row = lambda n, dt=F32: (n, dt, 'row')
    acc = lambda n, r=1: (n, F32, r)

    def f_norm1(x_, g, ada_):
        return _norm_mod(x_, g, ada_[:, D:2 * D], ada_[:, 0:D])

    landed = {}

    def behind(kernel_name):
        if not shards:
            return None
        pieces = GATHER_BEHIND[kernel_name]
        gather = _GatherChipsHalved if kernel_name in HALVED_BEHIND else _GatherChips
        return gather([shards[n] for n, _, _ in pieces], rows=[(lo, hi) for _, lo, hi in pieces])

    def took(kernel_name, got):
        landed.update(zip(GATHER_BEHIND[kernel_name], got))

    def mm_behind(a_, w_, kernel_name, **kw):
        ex = behind(kernel_name)
        res = _mm_nn(a_, w_, name=kernel_name, hosted=ex, **kw)
        if ex:
            took(kernel_name, res[1:])
            return res[0]
        return res

    h1, proj, *got = _mm_then([(None, w['w_in'][:, :SHIFT_W], 'nn')], lambda i, n, y, h, *_: (h, y), [x],
                              [s['norm1_gain'], ada], [row(D, MXU), row(SHIFT_W)], tm=tm, name="f_proj_shift",
                              lhs_fn=f_norm1, hosted=behind("f_proj_shift"))
    took("f_proj_shift", got)
    proj_qkv = _mm_nn(h1, w['w_in'][:, SHIFT_W:SHIFT_W + QKV_W], name="f_proj_qkv")
    proj_g = mm_behind(h1, w['w_in'][:, SHIFT_W + QKV_W:], "f_proj_gates", out_dtype=MXU)
    prep_consts = [s['decay_w0'], s['lora_up'], s['iclr_a0'], s['gate_up'], s['k_k'], s['k_a']]

    def f_prep(i, n, cur, prev8, mu, *params):
        mixed = cur + (_shift_down(cur, prev8, i) - cur) * mu
        return (_prep(mixed, *params),)
    rw, *got = _rowwise(f_prep, [(proj, SHIFT_W)], [s['tshift_mu']] + prep_consts, [row(7 * RW)], tm=tm_wide,
                        name="f_prep", halo=[(proj, SHIFT_W, 'prev')], hosted=behind("f_prep"))
    took("f_prep", got)
    y, ck, inv, *got = _scan_fwd(rw, name="f_scan", hosted=behind("f_scan"))
    took("f_scan", got)
    post_consts = [s['lnx_gain'], s['lnx_bias'], s['r_k']]

    rkvg = [(rw, RW, j) for j in (0, 2, 3, 6)]

    def f_post(i, n, *args):
        return (_post(*args),)
    ya, *got = _rowwise(f_post, [y] + rkvg, post_consts, [row(RW, MXU)], tm=tm_wide, name="f_post",
                        hosted=behind("f_post"))
    took("f_post", got)
    yb, *got = _attn_fwd(proj_qkv, tab, s['q_norm_gain'], s['k_norm_gain'], s['attn_sinks'], name="f_attn",
                         hosted=behind("f_attn"))
    took("f_attn", got)
    w = dict(w)
    if shards:
        ax = dict(SHARDED)
        for n in LATE:
            rows = [landed[key] for key in sorted(k_ for k_ in landed if k_[0] == n)]
            blocks = rows[0] if len(rows) == 1 else jnp.concatenate(rows, axis=1)
            w[n] = blocks if n in FFN else _full_weight(blocks, ax[n])
    else:
        fs = DFF // 4
        w.update({n: w[n].reshape(D, 4, fs).transpose(1, 0, 2) for n in ('ffn_w1', 'ffn_w3')})
        w['ffn_w2'] = w['ffn_w2'].reshape(4, fs, D)
    def f_merge(pg, ya_, yb_, x_, wa, wb, bias, g, ada_):
        ma_ = jnp.dot(ya_, wa, preferred_element_type=F32).astype(MXU)
        mb_ = jnp.dot(yb_, wb, preferred_element_type=F32).astype(MXU)
        return _merge(pg.astype(F32), ma_.astype(F32), mb_.astype(F32), bias), ma_, mb_

    def f_res1(i, n, mo_, merged_, ma_, mb_, pg, ya_, yb_, x_, wa, wb, bias, g, ada_):
        x1_ = x_ + ada_[:, 2 * D:3 * D] * mo_
        return merged_, ma_, mb_, mo_, x1_, _norm_mod(x1_, g, ada_[:, 4 * D:5 * D], ada_[:, 3 * D:4 * D])
    merged, ma, mb, mo, x1, h2 = _mm_then(
        [(None, w['w_out'], 'nn')], f_res1, [proj_g, ya, yb, x],
        [w['w_branch_a'], w['w_branch_b'], s['branch_gate_b'], s['norm2_gain'], ada],
        [row(D, MXU), row(D, MXU), row(D, MXU), row(D), row(D), row(D, MXU)], tm=tm, name="f_out", lhs_fn=f_merge)
    u, v, act = _ffn_in(h2, w['ffn_w1'], w['ffn_w3'], name="f_ffn_in")

    def f_loss(i, n, ff_, x1_, tgt, ada_):
        g2 = ada_[:, 5 * D:6 * D]
        err = x1_ + g2 * ff_ - tgt
        dx2 = err * (1.0 / D)
        loss = 0.5 * jnp.sum(jnp.sum(err * err, axis=1, keepdims=True) * (1.0 / D), axis=0, keepdims=True)
        return dx2, (dx2 * g2), jnp.broadcast_to(loss, (1, 128)), jnp.sum(dx2 * ff_, axis=0, keepdims=True)
    dx2, dff, loss, dgate2 = _mm_then([(act, w['ffn_w2'], 'nn', j) for j in range(4)], f_loss, [x1, target], [ada],
                                      [row(D), row(D, MXU), acc(128), acc(D)], tm=tm, name="f_ffn_out")

    du, dv = _ffn_act_bwd(dff, w['ffn_w2'], u, v, name="b_ffn_out_dx")
    g_w2 = _mm_tn_blocks(act, dff, name="b_ffn_out_dw", out_dtype=MXU)
    g_w1 = _mm_tn_blocks(h2, du, name="b_ffn_w1_dw", out_dtype=MXU)
    g_w3 = _mm_tn_blocks(h2, dv, name="b_ffn_w3_dw", out_dtype=MXU)

    def b_res1(i, n, dh2_, x1_, dx2_, mo_, g, ada_):
        _, vjp = jax.vjp(_norm_mod, x1_, g, ada_[:, 4 * D:5 * D], ada_[:, 3 * D:4 * D])
        dxn, dg, dsc, dsh = vjp(dh2_)
        dx1_ = dxn + dx2_
        g1 = ada_[:, 2 * D:3 * D]
        return dx1_, dx1_ * g1, dg, dsc, dsh, jnp.sum(dx1_ * mo_, axis=0, keepdims=True)
    dx1, dmo, d_gain2, d_scale2, d_shift2, dgate1 = _mm_then(
        [(t, w[n], 'nt', j) for t, n in ((du, 'ffn_w1'), (dv, 'ffn_w3')) for j in range(4)], b_res1, [x1, dx2, mo],
        [s['norm2_gain'], ada],
        [row(D), row(D, MXU), acc(D), acc(D), acc(D), acc(D)], tm=tm_wide, name="b_ffn_in_dx")
    g_wout = _mm_tn(merged, dmo, name="b_out_dw", out_dtype=MXU)

    def b_merge(i, n, dm, pg, ma_, mb_, wa, wb, bias):
        _, vjp = jax.vjp(_merge, pg.astype(F32), ma_.astype(F32), mb_.astype(F32), bias)
        dpg, dma_, dmb_, dbias = vjp(dm)
        dma_, dmb_ = dma_.astype(MXU), dmb_.astype(MXU)
        nt = lambda a_, b_: lax.dot_general(a_, b_, (((1,), (1,)), ((), ())), preferred_element_type=F32)
        return dpg, dma_, dmb_, nt(dma_, wa), nt(dmb_, wb), dbias
    dpg, dma, dmb, dya, dyb, d_bias = _mm_then(
        [(dmo, w['w_out'], 'nt')], b_merge, [proj_g, ma, mb], [w['w_branch_a'], w['w_branch_b'], s['branch_gate_b']],
        [row(GATE_W, MXU), row(D, MXU), row(D, MXU), row(RW), row(RW), acc(GATE_W)], tm=tm, name="b_out_dx")
    g_wa = _mm_tn(ya, dma, name="b_branch_a_dw", out_dtype=MXU, col_shards=4)
    g_wb = _mm_tn(yb, dmb, name="b_branch_b_dw", out_dtype=MXU, col_shards=4)
    gw = dict(w_branch_a=g_wa, w_branch_b=g_wb, w_out=g_wout.reshape(4, D // 4, D), ffn_w1=g_w1, ffn_w3=g_w3,
              ffn_w2=g_w2)
    recv = {}
    dqkv, d_qg, d_kg, d_sinks, *got = _attn_bwd(
        proj_qkv, tab, s['q_norm_gain'], s['k_norm_gain'], s['attn_sinks'], dyb, name="b_attn",
        hosted=shards and _ScatterChips([gw[n] for n in BACK_ATTN]))
    recv.update(zip(BACK_ATTN, got))

    def b_post(i, n, y_, r_, k_, v_, g_, dya_, *params):
        _, vjp = jax.vjp(_post, y_, r_, k_, v_, g_, *params)
        dy_, dr_, dk_, dv_, dg_, *dparams = vjp(dya_)
        return (dy_, jnp.concatenate([dr_, dk_, dv_, dg_], axis=1), *dparams)
    dy, drkvg, d_lnx_gain, d_lnx_bias, d_r_k = _rowwise(
        b_post, [y] + rkvg + [dya], post_consts, [row(RW), row(4 * RW, MXU), acc(RW), acc(RW), acc(RW)], tm=tm_wide,
        name="b_post")
    dscan, *got = _scan_bwd(rw, ck, inv, dy, name="b_scan",
                            hosted=shards and _ScatterChips([gw[n] for n in BACK_SCAN]))
    recv.update(zip(BACK_SCAN, got))

    def b_prep(i, n, cur, drw_, dscan_, prev8, mu, *params):
        shifted = _shift_down(cur, prev8, i)
        mixed = cur + (shifted - cur) * mu
        _, vjp = jax.vjp(_prep, mixed, *params)
        blk = lambda t, j: t[:, j * RW:(j + 1) * RW].astype(F32)
        ct = jnp.concatenate([blk(dscan_, 0) + blk(drw_, 0), blk(dscan_, 1), blk(dscan_, 2) + blk(drw_, 1),
                              blk(dscan_, 3) + blk(drw_, 2), blk(dscan_, 4), blk(dscan_, 5), blk(drw_, 3)], axis=1)
        grads = vjp(ct)
        dmixed = grads[0]
        return (dmixed, jnp.sum(dmixed * (shifted - cur), axis=0, keepdims=True)) + tuple(grads[1:])
    dmixed, d_mu, d_w0, d_lora, d_a0, d_gate_up, d_kk, d_ka = _rowwise(
        b_prep, [(proj, SHIFT_W), drkvg, dscan], [s['tshift_mu']] + prep_consts,
        [row(SHIFT_W), acc(SHIFT_W), acc(RW), acc(2 * RW, 128), acc(RW), acc(RW, 128), acc(RW), acc(RW)],
        tm=tm_vjp, name="b_prep", halo=[(proj, SHIFT_W, 'prev')])

    def b_gather(i, n, dm, dqkv_, dpg_, next8, mu):
        dcur = dm * (1.0 - mu) + _shift_up(dm, next8, i, n) * mu
        return (jnp.concatenate([dcur.astype(MXU), dqkv_, dpg_], axis=1),)
    (dproj,) = _rowwise(b_gather, [dmixed, dqkv, dpg], [s['tshift_mu']], [row(IN_W, MXU)], tm=tm_wide, name="b_gather",
                        halo=[(dmixed, SHIFT_W, 'next')])
    g_win = _mm_tn(h1, dproj, name="b_proj_dw", out_dtype=MXU, col_shards=4)

    def col_blocks(g):
        k, n = g.shape
        return g.reshape(k, 4, n // 4).transpose(1, 0, 2).astype(MXU)
    gw.update(w_in=g_win, decay_up=col_blocks(d_lora[:64, :RW]), iclr_up=col_blocks(d_lora[64:, RW:]),
              gate_up=col_blocks(d_gate_up))
    top, bottom = None, None
    if shards:
        top = _ScatterChips([gw[n] for n in BACK_LAST], rows=[(0, D // 2)] + [None] * (len(BACK_LAST) - 1))
        bottom = _ScatterChips([gw['w_in']], rows=[(D // 2, 3 * D // 4)])
        dh1, *got_top = _mm_nt(dproj, w['w_in'], name="b_proj_dx", hosted=top)
    else:
        dh1 = _mm_nt(dproj, w['w_in'], name="b_proj_dx")

    def b_norm1(i, n, x_, dh1_, dx1_, g, ada_):
        _, vjp = jax.vjp(_norm_mod, x_, g, ada_[:, D:2 * D], ada_[:, 0:D])
        dxn, dg, dsc, dsh = vjp(dh1_)
        return dxn + dx1_, dg, dsc, dsh
    dx, d_gain1, d_scale1, d_shift1, *got_bottom = _rowwise(
        b_norm1, [x, dh1, dx1], [s['norm1_gain'], ada], [row(D), acc(D), acc(D), acc(D)], tm=tm, name="b_norm1",
        hosted=bottom)
    if shards:
        recv.update(zip(BACK_LAST[1:], got_top[1:]))
        recv['w_in'] = [got_top[0], got_bottom[0]]

    d_ada = jnp.concatenate([d_shift1, d_scale1, dgate1, d_shift2, d_scale2, dgate2], axis=1)
    gs = dict(norm1_gain=d_gain1, norm2_gain=d_gain2, tshift_mu=d_mu, decay_w0=d_w0, iclr_a0=d_a0, k_k=d_kk, k_a=d_ka,
              r_k=d_r_k, lnx_gain=d_lnx_gain, lnx_bias=d_lnx_bias, q_norm_gain=d_qg, k_norm_gain=d_kg,
              attn_sinks=d_sinks, branch_gate_b=d_bias)
    return loss, dx, d_ada, gw, gs, recv


ANY = pl.BlockSpec(memory_space=pl.ANY)


def _place():
    x, y, c = lax.axis_index("x"), lax.axis_index("y"), lax.axis_index("c")
    return x, y, c, [(1 - x, y), (x, 1 - y), (1 - x, 1 - y)]


def _all_gather8(x_shard, *, name):
    m_per, n = x_shard.shape

    def body(x_ref, out_ref, send_sems, recv_sems, local_sem):
        x, y, c, chips = _place()
        me, sibling = (x, y, c), (x, y, 1 - c)

        def rows(px, py, pc):
            return out_ref.at[pl.ds((4 * px + 2 * py + pc) * m_per, m_per), :]

        def copy(k, block, to, src=None):
            return pltpu.make_async_remote_copy(
                src_ref=rows(*block) if src is None else src, dst_ref=rows(*block),
                send_sem=send_sems.at[k], recv_sem=recv_sems.at[k], device_id=to, device_id_type=MESH)

        mine = pltpu.make_async_copy(x_ref, rows(*me), local_sem)
        mine.start()
        first = [copy(0, me, sibling, src=x_ref)]
        first += [copy(1 + j, me, (*chip, c), src=x_ref) for j, chip in enumerate(chips)]
        for cp in first:
            cp.start()
        passed = [copy(4 + j, (*chip, c), sibling) for j, chip in enumerate(chips)]
        for j, chip in enumerate(chips):
            copy(1 + j, (*chip, c), me).wait_recv()
            passed[j].start()
        copy(0, sibling, me).wait_recv()
        for j, chip in enumerate(chips):
            copy(4 + j, (*chip, 1 - c), me).wait_recv()
        for cp in first + passed:
            cp.wait_send()
        mine.wait()

    return pl.pallas_call(
        body, name=name, out_shape=jax.ShapeDtypeStruct((8 * m_per, n), x_shard.dtype),
        in_specs=[pl.BlockSpec(memory_space=pltpu.VMEM)], out_specs=pl.BlockSpec(memory_space=pltpu.VMEM),
        scratch_shapes=[pltpu.SemaphoreType.DMA((7,)), pltpu.SemaphoreType.DMA((7,)), pltpu.SemaphoreType.DMA],
    )(x_shard)


class _GatherChips:
    def __init__(self, shards, rows=None):
        n = len(shards)
        self.rows = [r or (0, s.shape[0]) for s, r in zip(shards, rows or [None] * n, strict=True)]
        self.arrays, self.n_in, self.n_out = list(shards), n, n
        self.out_shape = [jax.ShapeDtypeStruct((4, hi - lo) + s.shape[1:], s.dtype)
                          for s, (lo, hi) in zip(shards, self.rows, strict=True)]
        self.scratch = [pltpu.SemaphoreType.DMA((3 * n,)), pltpu.SemaphoreType.DMA((3 * n,)),
                        pltpu.SemaphoreType.DMA((n,))]

    def _copies(self, x_refs, out_refs, sems, receiving):
        send_sems, recv_sems, local_sems = sems
        x, y, c, chips = _place()
        s_me = 2 * x + y
        n = self.n_in
        src = [x_refs[a].at[pl.ds(lo, hi - lo)] for a, (lo, hi) in enumerate(self.rows)]

        def copy(a, k, s):
            return pltpu.make_async_remote_copy(
                src_ref=src[a], dst_ref=out_refs[a].at[s], send_sem=send_sems.at[3 * a + k],
                recv_sem=recv_sems.at[3 * a + k], device_id=(*chips[k], c), device_id_type=MESH)

        mine = [pltpu.make_async_copy(src[a], out_refs[a].at[s_me], local_sems.at[a]) for a in range(n)]
        sends = [copy(a, k, s_me) for a in range(n) for k in range(3)]
        if not receiving:
            return mine, sends
        return mine, sends, [copy(a, k, 2 * px + py) for a in range(n) for k, (px, py) in enumerate(chips)]

    def start(self, x_refs, out_refs, sems):
        mine, sends = self._copies(x_refs, out_refs, sems, False)
        for cp in mine + sends:
            cp.start()

    def wait(self, x_refs, out_refs, sems):
        mine, sends, recvs = self._copies(x_refs, out_refs, sems, True)
        for cp in recvs:
            cp.wait_recv()
        for cp in sends:
            cp.wait_send()
        for cp in mine:
            cp.wait()


class _GatherChipsHalved(_GatherChips):
    def __init__(self, shards, rows=None):
        super().__init__(shards, rows)
        n = self.n_in
        self.scratch = [pltpu.SemaphoreType.DMA((6 * n,)), pltpu.SemaphoreType.DMA((6 * n,)),
                        pltpu.SemaphoreType.DMA((n,))]

    def _copies(self, x_refs, out_refs, sems, receiving):
        send_sems, recv_sems, local_sems = sems
        x, y, c, chips = _place()
        s_me = 2 * x + y
        n = self.n_in

        def half(a, who, first=0):
            lo, hi = self.rows[a]
            return pl.ds(first + who * ((hi - lo) // 2), (hi - lo) // 2)

        def over_chips(a, k, s):
            return pltpu.make_async_remote_copy(
                src_ref=x_refs[a].at[half(a, c, self.rows[a][0])], dst_ref=out_refs[a].at[s, half(a, c)],
                send_sem=send_sems.at[3 * a + k], recv_sem=recv_sems.at[3 * a + k],
                device_id=(*chips[k], c), device_id_type=MESH)

        def to_sibling(a, k, s, who):
            return pltpu.make_async_remote_copy(
                src_ref=out_refs[a].at[s, half(a, who)], dst_ref=out_refs[a].at[s, half(a, who)],
                send_sem=send_sems.at[3 * n + 3 * a + k], recv_sem=recv_sems.at[3 * n + 3 * a + k],
                device_id=(x, y, 1 - c), device_id_type=MESH)

        mine = [pltpu.make_async_copy(x_refs[a].at[pl.ds(lo, hi - lo)], out_refs[a].at[s_me], local_sems.at[a])
                for a, (lo, hi) in enumerate(self.rows)]
        sends = [over_chips(a, k, s_me) for a in range(n) for k in range(3)]
        if not receiving:
            return mine, sends
        pairs = [(a, k, 2 * px + py) for a in range(n) for k, (px, py) in enumerate(chips)]
        landed = [over_chips(a, k, s) for a, k, s in pairs]
        passed_on = [to_sibling(a, k, s, c) for a, k, s in pairs]
        from_sibling = [to_sibling(a, k, s, 1 - c) for a, k, s in pairs]
        return mine, sends, landed, passed_on, from_sibling

    def wait(self, x_refs, out_refs, sems):
        mine, sends, landed, passed_on, from_sibling = self._copies(x_refs, out_refs, sems, True)
        for got, fwd in zip(landed, passed_on, strict=True):
            got.wait_recv()
            fwd.start()
        for cp in from_sibling:
            cp.wait_recv()
        for cp in sends + passed_on:
            cp.wait_send()
        for cp in mine:
            cp.wait()


class _ScatterChips:
    def __init__(self, parts, rows=None):
        n = len(parts)
        self.rows = [r or (0, p.shape[1]) for p, r in zip(parts, rows or [None] * n, strict=True)]
        self.arrays, self.n_in, self.n_out = list(parts), n, n
        self.out_shape = [jax.ShapeDtypeStruct((3, hi - lo) + p.shape[2:], p.dtype)
                          for p, (lo, hi) in zip(parts, self.rows, strict=True)]
        self.scratch = [pltpu.SemaphoreType.DMA((3 * n,)), pltpu.SemaphoreType.DMA((3 * n,))]

    def _copies(self, g_refs, out_refs, sems):
        send_sems, recv_sems = sems
        x, y, c, chips = _place()
        return [pltpu.make_async_remote_copy(
            src_ref=g_refs[a].at[2 * px + py, pl.ds(lo, hi - lo)], dst_ref=out_refs[a].at[k],
            send_sem=send_sems.at[3 * a + k], recv_sem=recv_sems.at[3 * a + k], device_id=(px, py, c),
            device_id_type=MESH)
            for a, (lo, hi) in enumerate(self.rows) for k, (px, py) in enumerate(chips)]

    def start(self, g_refs, out_refs, sems):
        for cp in self._copies(g_refs, out_refs, sems):
            cp.start()

    def wait(self, g_refs, out_refs, sems):
        sends = self._copies(g_refs, out_refs, sems)
        for cp in sends:
            cp.wait_recv()
        for cp in sends:
            cp.wait_send()


def _exchange_call(ex, *, name):
    def body(*refs):
        parts = (refs[:ex.n_in], refs[ex.n_in:ex.n_in + ex.n_out], refs[ex.n_in + ex.n_out:])
        ex.start(*parts)
        ex.wait(*parts)

    return pl.pallas_call(body, name=name, out_shape=ex.out_shape, in_specs=[ANY] * ex.n_in,
                          out_specs=[ANY] * ex.n_out, scratch_shapes=ex.scratch)(*ex.arrays)


class _SwapSibling:
    def __init__(self, vs):
        n = len(vs)
        self.arrays, self.n_in, self.n_out = list(vs), n, n
        self.out_shape = [jax.ShapeDtypeStruct(v.shape, v.dtype) for v in vs]
        self.scratch = [pltpu.SemaphoreType.DMA((n,)), pltpu.SemaphoreType.DMA((n,))]

    def _copies(self, v_refs, out_refs, sems):
        send_sems, recv_sems = sems
        x, y, c, _ = _place()
        return [pltpu.make_async_remote_copy(src_ref=v_refs[a], dst_ref=out_refs[a], send_sem=send_sems.at[a],
                                             recv_sem=recv_sems.at[a], device_id=(x, y, 1 - c), device_id_type=MESH)
                for a in range(self.n_in)]

    def start(self, v_refs, out_refs, sems):
        for cp in self._copies(v_refs, out_refs, sems):
            cp.start()

    def wait(self, v_refs, out_refs, sems):
        for cp in self._copies(v_refs, out_refs, sems):
            cp.wait()


class _Both:
    def __init__(self, first, second):
        self.parts = (first, second)
        self.arrays = first.arrays + second.arrays
        self.n_in, self.n_out = first.n_in + second.n_in, first.n_out + second.n_out
        self.out_shape = first.out_shape + second.out_shape
        self.scratch = first.scratch + second.scratch

    def _split(self, in_refs, out_refs, sems):
        a, b = self.parts
        return ((a, in_refs[:a.n_in], out_refs[:a.n_out], sems[:len(a.scratch)]),
                (b, in_refs[a.n_in:], out_refs[a.n_out:], sems[len(a.scratch):]))

    def start(self, in_refs, out_refs, sems):
        for ex, *refs in self._split(in_refs, out_refs, sems):
            ex.start(*refs)

    def wait(self, in_refs, out_refs, sems):
        for ex, *refs in self._split(in_refs, out_refs, sems):
            ex.wait(*refs)


def _sum_parts(own, others, *, name):
    R, C = own.shape
    tm = _pick(R, (256, 128, 64))

    def body(own_ref, o0_ref, o1_ref, o2_ref, out_ref):
        tot = own_ref[...].astype(F32)
        for ref in (o0_ref, o1_ref, o2_ref):
            tot = tot + ref[...].astype(F32)
        out_ref[...] = tot

    part = lambda k: pl.BlockSpec((None, tm, C), lambda i: (k, i, 0))
    return pl.pallas_call(
        body, name=name, grid=(R // tm,),
        in_specs=[pl.BlockSpec((tm, C), lambda i: (i, 0)), part(0), part(1), part(2)],
        out_specs=pl.BlockSpec((tm, C), lambda i: (i, 0)), out_shape=jax.ShapeDtypeStruct((R, C), F32),
        compiler_params=_cparams(("arbitrary",)),
    )(own, others, others, others)


def _adam_math(w_, m_, v_, g):
    m2 = ADAM_B1 * m_ + (1.0 - ADAM_B1) * g
    v2 = ADAM_B2 * v_ + (1.0 - ADAM_B2) * jnp.square(g)
    m_hat = m2 / (1.0 - ADAM_B1 ** ADAM_STEP)
    v_hat = v2 / (1.0 - ADAM_B2 ** ADAM_STEP)
    delta = -ADAM_LR * (m_hat / (jnp.sqrt(v_hat) + ADAM_EPS) + ADAM_WD * w_)
    return delta, m2, v2


SMALL_SLOTS = 24
SMALL_COLS = 2 * D


def _small_rows(widths):
    firsts, row = [], 0
    for n_i in widths:
        firsts.append(row)
        row += -(-n_i // SMALL_COLS)
    assert row <= SMALL_SLOTS
    return firsts


def _pack_small(grads, *, name):
    n = len(grads)
    firsts = _small_rows([g.shape[1] for g in grads])

    def body(*refs):
        out_ref = refs[n]
        out_ref[...] = jnp.zeros_like(out_ref)
        for first, ref in zip(firsts, refs[:n], strict=True):
            for lo in range(0, ref.shape[1], SMALL_COLS):
                width = min(SMALL_COLS, ref.shape[1] - lo)
                row = first + lo // SMALL_COLS
                out_ref[row:row + 1, 0:width] = ref[:, lo:lo + width]

    return pl.pallas_call(body, name=name, out_shape=jax.ShapeDtypeStruct((SMALL_SLOTS, SMALL_COLS), F32))(*grads)


def _adamw_small(ws, ms, vs, gathered, *, name):
    n = len(ws)
    firsts = _small_rows([w.shape[1] for w in ws] + [128])

    def total(g_ref, first, nc):
        pieces = []
        for lo in range(0, nc, SMALL_COLS):
            width, row = min(SMALL_COLS, nc - lo), first + lo // SMALL_COLS
            g = g_ref[row:row + 1, 0:width]
            for d in range(1, 8):
                g = g + g_ref[d * SMALL_SLOTS + row:d * SMALL_SLOTS + row + 1, 0:width]
            pieces.append(g)
        return pieces[0] if len(pieces) == 1 else jnp.concatenate(pieces, axis=1)

    def body(*refs):
        w_refs, m_refs, v_refs, g_ref = refs[:n], refs[n:2 * n], refs[2 * n:3 * n], refs[3 * n]
        outs = refs[3 * n + 1:]
        for i in range(n):
            g = total(g_ref, firsts[i], w_refs[i].shape[1])
            delta, m2, v2 = _adam_math(w_refs[i][...], m_refs[i][...], v_refs[i][...], g)
            for k, val in enumerate((g, delta, m2, v2)):
                outs[k * n + i][...] = val
        outs[4 * n][...] = total(g_ref, firsts[n], 128)

    shapes = [jax.ShapeDtypeStruct(w.shape, F32) for w in ws]
    res = pl.pallas_call(body, name=name, out_shape=shapes * 4 + [jax.ShapeDtypeStruct((1, 128), F32)],
                         compiler_params=pltpu.CompilerParams(vmem_limit_bytes=VMEM_LIMIT))(*ws, *ms, *vs, gathered)
    return [res[k * n:(k + 1) * n] for k in range(4)], res[4 * n]


def _adamw(w, m, v, gparts, *, tm, name, hosted=None):
    def fn(i, n, w_, m_, v_, *gs):
        g = gs[0]
        for p in gs[1:]:
            g = g + p
        return (g,) + _adam_math(w_, m_, v_, g)
    nc = w.shape[1]
    return _rowwise(fn, [w, m, v] + list(gparts), [], [(nc, F32, 'row')] * 4, tm=tm, name=name, hosted=hosted)


WEIGHTS = ['ada_w', 'ada_b', 'norm1_gain', 'norm2_gain', 'w_in', 'tshift_mu', 'decay_w0', 'decay_up', 'iclr_a0',
           'iclr_up', 'gate_up', 'k_k', 'k_a', 'r_k', 'lnx_gain', 'lnx_bias', 'q_norm_gain', 'k_norm_gain', 'attn_sinks',
           'branch_gate_b', 'w_branch_a', 'w_branch_b', 'w_out', 'ffn_w1', 'ffn_w3', 'ffn_w2']
SHARDED = [('w_in', 1), ('decay_up', 1), ('iclr_up', 1), ('gate_up', 1), ('w_branch_a', 1), ('w_branch_b', 1),
           ('w_out', 0), ('ffn_w1', 1), ('ffn_w3', 1), ('ffn_w2', 0)]
SMALL = ['ada_b', 'norm1_gain', 'norm2_gain', 'tshift_mu', 'decay_w0', 'iclr_a0', 'k_k', 'k_a', 'r_k', 'lnx_gain',
         'lnx_bias', 'q_norm_gain', 'k_norm_gain', 'attn_sinks', 'branch_gate_b']


def kernel(x, c, positions, ada_w, ada_b, norm1_gain, norm2_gain, w_in, tshift_mu, decay_w0, decay_up, iclr_a0, iclr_up, gate_up, k_k, k_a, r_k, lnx_gain, lnx_bias, q_norm_gain, k_norm_gain, attn_sinks, branch_gate_b, w_branch_a, w_branch_b, w_out, ffn_w1, ffn_w3, ffn_w2, loss_target, m_ada_w, m_ada_b, m_norm1_gain, m_norm2_gain, m_w_in, m_tshift_mu, m_decay_w0, m_decay_up, m_iclr_a0, m_iclr_up, m_gate_up, m_k_k, m_k_a, m_r_k, m_lnx_gain, m_lnx_bias, m_q_norm_gain, m_k_norm_gain, m_attn_sinks, m_branch_gate_b, m_w_branch_a, m_w_branch_b, m_w_out, m_ffn_w1, m_ffn_w3, m_ffn_w2, v_ada_w, v_ada_b, v_norm1_gain, v_norm2_gain, v_w_in, v_tshift_mu, v_decay_w0, v_decay_up, v_iclr_a0, v_iclr_up, v_gate_up, v_k_k, v_k_a, v_r_k, v_lnx_gain, v_lnx_bias, v_q_norm_gain, v_k_norm_gain, v_attn_sinks, v_branch_gate_b, v_w_branch_a, v_w_branch_b, v_w_out, v_ffn_w1, v_ffn_w3, v_ffn_w2):
    a = dict(locals())
    W = {n: a[n] for n in WEIGHTS}
    M = {n: a['m_' + n] for n in WEIGHTS}
    V = {n: a['v_' + n] for n in WEIGHTS}
    xi, yi, ci = lax.axis_index("x"), lax.axis_index("y"), lax.axis_index("c")
    me = 4 * xi + 2 * yi + ci
    shard = 2 * xi + yi
    mat = lambda t: t.reshape(t.shape[-2], t.shape[-1])
    sharded = [n for n, _ in SHARDED]

    ax = dict(SHARDED)
    late = LATE
    early = [n for n in sharded if n not in late]
    shards = {n: mat(W[n]).astype(MXU) for n in sharded}
    gathered = _exchange_call(_GatherChipsHalved([shards[n] for n in early]), name="gather_weights")
    full = {n: _full_weight(g, ax[n]) for n, g in zip(early, gathered, strict=True)}

    c_all = _all_gather8(jnp.broadcast_to(c, (8, D)), name="gather_c")[0::8]
    pad_rows = lambda t: jnp.concatenate([t, jnp.zeros((BLK - 8, t.shape[1]), t.dtype)])
    c_all = pad_rows(c_all.astype(MXU))
    ada_cols = _mm_nn(c_all, mat(ada_w).astype(MXU), name="f_ada")[:8]
    ada_all = _all_gather8(ada_cols, name="gather_ada").reshape(2, 2, 2, 8, 6 * D // 4)
    ada_mine = lax.dynamic_index_in_dim(ada_all[:, :, 0], me, axis=2, keepdims=False)
    ada = ada_mine.reshape(1, 6 * D) + mat(ada_b)

    zero = jnp.zeros((64, RW), MXU)
    lora = jnp.concatenate([jnp.concatenate([full['decay_up'], zero], axis=1),
                            jnp.concatenate([zero, full['iclr_up']], axis=1)], axis=0)
    s = {n: W[n].reshape(1, -1) for n in SMALL if n != 'ada_b'}
    s['lora_up'] = lora.astype(F32)
    s['gate_up'] = full['gate_up'].astype(F32)
    tab = _rope_table(positions.reshape(-1))
    loss, dx, d_ada, gw, gs, from_chips = _local_step(x[0], loss_target[0], ada, tab, dict(w_in=full['w_in']), s,
                                                      shards={n: shards[n] for n in late})

    gs['ada_b'] = d_ada
    gsmall = _pack_small([gs[n] for n in SMALL] + [loss], name="pack_small_grads")
    gsmall_all = _all_gather8(gsmall, name="gather_small_grads")
    row = lambda src: [src[n].reshape(1, -1) for n in SMALL]
    sm_out, loss = _adamw_small(row(W), row(M), row(V), gsmall_all, name="adamw_small")
    sm_out = [{n: o.reshape(W[n].shape) for n, o in zip(SMALL, outs_k, strict=True)} for outs_k in sm_out]
    loss = loss[0, 0]

    ada_rows = 6 * D // SMALL_COLS
    d_ada_all = gsmall_all.reshape(8, SMALL_SLOTS, SMALL_COLS)[:, :ada_rows].reshape(8, 6 * D)
    d_ada_cols = lax.dynamic_slice_in_dim(d_ada_all, shard * (6 * D // 4), 6 * D // 4, axis=1)
    g_ada_w = _mm_tn(c_all, pad_rows(d_ada_cols.astype(MXU)), name="b_ada")

    rest = [n for n in sharded if n != 'w_in']
    parts = {n: _sum_parts(lax.dynamic_index_in_dim(gw[n], shard, axis=0, keepdims=False), from_chips[n],
                           name="sum_" + n) for n in rest}
    tail = _Both(_ScatterChips([gw['w_in']], rows=[(3 * D // 4, D)]), _SwapSibling([parts[n] for n in rest]))
    res = _adamw(mat(ada_w), mat(m_ada_w), mat(v_ada_w), [g_ada_w], tm=256, name="adamw_ada", hosted=tail)
    ada_out, last_quarter, others = res[:4], res[4], dict(zip(rest, res[5:], strict=True))
    parts['w_in'] = _sum_parts(lax.dynamic_index_in_dim(gw['w_in'], shard, axis=0, keepdims=False),
                               jnp.concatenate(from_chips['w_in'] + [last_quarter], axis=1), name="sum_w_in")
    others['w_in'] = _exchange_call(_SwapSibling([parts['w_in']]), name="swap_w_in")[0]
    sh_out = {}
    for n in sharded:
        part, other = parts[n], others[n]
        sh_out[n] = _adamw(mat(W[n]), mat(M[n]), mat(V[n]), [part, other], tm=_pick(part.shape[0], (256, 128, 64)),
                           name="adamw_" + n)

    def leaf(k, n):
        if n == 'ada_w':
            return ada_out[k].reshape(W[n].shape)
        if n in sharded:
            return sh_out[n][k].reshape(W[n].shape)
        return sm_out[k][n]
    outs = [leaf(k, n) for k in range(4) for n in WEIGHTS]
    return (loss, dx[None], *outs)
```
